```python
import jax, jax.numpy as jnp
from jax import lax
import numpy as np

D_MODEL = 1024
BATCH = 8
SEQ = 2048
DEPTH = 1

MIX_WIDTH = D_MODEL
HEAD_DIM = 64
ATTN_WIDTH = MIX_WIDTH // 2
ATTN_HEADS = ATTN_WIDTH // HEAD_DIM
SGU_WIDTH = MIX_WIDTH - ATTN_WIDTH
SGU_GROUP_DIM = 64
SGU_GROUPS = SGU_WIDTH // SGU_GROUP_DIM
SGU_CHUNK = 128
DILATED_PATTERNS = ((128, 1), (512, 4), (2048, 16))
BLOCK_Q = 128
ROPE_THETA = 500000.0
ROT_DIM = HEAD_DIM // 4
ROT_HALF = ROT_DIM // 2
D_FF = ((8 * D_MODEL // 3 + 255) // 256) * 256
IN_PROJ_WIDTH = 3 * ATTN_WIDTH + 2 * SGU_WIDTH
RMS_EPS = 1e-6
LN_EPS = 1e-5

kernel_name = "hymba_dilated_attn_gmlp_sandwich_block"


def rms_norm(x, gain):
    xf = x.astype(jnp.float32)
    y = xf * lax.rsqrt(jnp.mean(xf * xf, axis=-1, keepdims=True) + RMS_EPS)
    return (y * gain.astype(jnp.float32)).astype(x.dtype)


def layer_norm(x, gain, bias):
    xf = x.astype(jnp.float32)
    mu = jnp.mean(xf, axis=-1, keepdims=True)
    xc = xf - mu
    y = xc * lax.rsqrt(jnp.mean(xc * xc, axis=-1, keepdims=True) + LN_EPS)
    return (y * gain.astype(jnp.float32) + bias.astype(jnp.float32)).astype(x.dtype)


def partial_rotary(x, positions):
    inv_freq = ROPE_THETA ** (-jnp.arange(0, ROT_DIM, 2, dtype=jnp.float32) / ROT_DIM)
    ang = positions.astype(jnp.float32)[:, :, None, None] * inv_freq
    cos, sin = jnp.cos(ang), jnp.sin(ang)
    xf = x.astype(jnp.float32)
    x1 = xf[..., :ROT_HALF]
    x2 = xf[..., ROT_HALF:ROT_DIM]
    out = jnp.concatenate([x1 * cos - x2 * sin, x2 * cos + x1 * sin, xf[..., ROT_DIM:]], axis=-1)
    return out.astype(x.dtype)


def dilated_window_attention(q, k, v, window, dilation):
    B, H, S, Dh = q.shape
    L = S // dilation
    W = window // dilation
    nblk = -(-L // BLOCK_Q)
    Lp = nblk * BLOCK_Q

    def strided(t):
        return t.reshape(B, H, L, dilation, Dh).transpose(0, 1, 3, 2, 4)

    qs, ks, vs = strided(q), strided(k), strided(v)
    qs = jnp.pad(qs, ((0, 0), (0, 0), (0, 0), (0, Lp - L), (0, 0)))
    kv_pad = ((0, 0), (0, 0), (0, 0), (W, Lp - L), (0, 0))
    kp = jnp.pad(ks, kv_pad)
    vp = jnp.pad(vs, kv_pad)
    blk = jnp.arange(nblk)[:, None]
    col = jnp.arange(BLOCK_Q + W)[None, :]
    idx = blk * BLOCK_Q + col
    kb = kp[:, :, :, idx]
    vb = vp[:, :, :, idx]
    qb = qs.reshape(B, H, dilation, nblk, BLOCK_Q, Dh)

    scale = 1.0 / np.sqrt(HEAD_DIM)
    s = jnp.einsum('bhrnqd,bhrnkd->bhrnqk', qb, kb,
                   preferred_element_type=jnp.float32) * scale
    row = jnp.arange(BLOCK_Q)[:, None]
    dist = row + W - jnp.arange(BLOCK_Q + W)[None, :]
    key_pos = jnp.arange(nblk)[:, None, None] * BLOCK_Q + jnp.arange(BLOCK_Q + W)[None, None, :] - W
    mask = (dist >= 0)[None] & (dist <= W)[None] & (key_pos >= 0)
    s = jnp.where(mask, s, jnp.finfo(jnp.float32).min)
    m = jnp.max(s, axis=-1, keepdims=True)
    p = jnp.exp(s - m)
    denom = jnp.sum(p, axis=-1, keepdims=True)
    o = jnp.einsum('bhrnqk,bhrnkd->bhrnqd', p, vb.astype(jnp.float32)) / denom
    lse = (m + jnp.log(denom))[..., 0]
    o = o.reshape(B, H, dilation, Lp, Dh)[:, :, :, :L]
    lse = lse.reshape(B, H, dilation, Lp)[:, :, :, :L]
    o = o.transpose(0, 1, 3, 2, 4).reshape(B, H, S, Dh)
    lse = lse.transpose(0, 1, 3, 2).reshape(B, H, S)
    return o, lse


def dilated_mixture_attention(q, k, v):
    outs, lses = [], []
    for window, dilation in DILATED_PATTERNS:
        o, lse = dilated_window_attention(q, k, v, window, dilation)
        outs.append(o)
        lses.append(lse)
    wts = jax.nn.softmax(jnp.stack(lses, axis=0), axis=0)
    o = jnp.sum(wts[..., None] * jnp.stack(outs, axis=0), axis=0)
    return o.astype(q.dtype)


def spatial_gating(u, v, ln_gain, ln_bias, w_spatial, b_spatial):
    B, S, _ = u.shape
    u = jax.nn.gelu(u, approximate=False)
    v = layer_norm(jax.nn.gelu(v, approximate=False), ln_gain, ln_bias)
    vc = v.reshape(B, S // SGU_CHUNK, SGU_CHUNK, SGU_GROUPS, SGU_GROUP_DIM)
    causal = jnp.tril(jnp.ones((SGU_CHUNK, SGU_CHUNK), dtype=bool))
    w = jnp.where(causal[None], w_spatial, 0.0).astype(v.dtype)
    mixed = jnp.einsum('gij,bnjgc->bnigc', w, vc) + b_spatial.T[:, :, None].astype(v.dtype)
    return u * mixed.reshape(B, S, SGU_WIDTH)


def _fwd_setup_inputs(seed: int = 0) -> dict:
    key = jax.random.key(seed)
    ks = jax.random.split(key, 20)
    f32 = jnp.float32

    def nrm(k, shape, scale):
        return jax.random.normal(k, shape, f32) * scale

    def gain(k, width):
        return 1.0 + 0.05 * jax.random.normal(k, (DEPTH, width), f32)

    x = jax.random.normal(ks[0], (BATCH, SEQ, D_MODEL), f32)
    offsets = jax.random.randint(ks[1], (BATCH, 1), 0, 4096, dtype=jnp.int32)
    positions = (jnp.arange(SEQ, dtype=jnp.int32)[None, :] + offsets).astype(jnp.int32)
    return {
        "x": x,
        "positions": positions,
        "pre_mix_norm": gain(ks[2], D_MODEL),
        "w_in": nrm(ks[3], (DEPTH, D_MODEL, IN_PROJ_WIDTH), D_MODEL ** -0.5),
        "sgu_ln_gain": gain(ks[4], SGU_WIDTH),
        "sgu_ln_bias": nrm(ks[5], (DEPTH, SGU_WIDTH), 0.02),
        "sgu_w_spatial": nrm(ks[6], (DEPTH, SGU_GROUPS, SGU_CHUNK, SGU_CHUNK), 0.5 * SGU_CHUNK ** -0.5),
        "sgu_b_spatial": 1.0 + nrm(ks[7], (DEPTH, SGU_GROUPS, SGU_CHUNK), 0.1),
        "attn_out_norm": gain(ks[8], ATTN_WIDTH),
        "sgu_out_norm": gain(ks[9], SGU_WIDTH),
        "w_out": nrm(ks[10], (DEPTH, MIX_WIDTH, D_MODEL), MIX_WIDTH ** -0.5),
        "post_mix_norm": gain(ks[11], D_MODEL),
        "pre_ffn_norm": gain(ks[12], D_MODEL),
        "w_gate": nrm(ks[13], (DEPTH, D_MODEL, D_FF), D_MODEL ** -0.5),
        "w_up": nrm(ks[14], (DEPTH, D_MODEL, D_FF), D_MODEL ** -0.5),
        "w_down": nrm(ks[15], (DEPTH, D_FF, D_MODEL), D_FF ** -0.5),
        "post_ffn_norm": gain(ks[16], D_MODEL),
    }


def _fwd_reference(x, positions, pre_mix_norm, w_in, sgu_ln_gain, sgu_ln_bias, sgu_w_spatial,
              sgu_b_spatial, attn_out_norm, sgu_out_norm, w_out, post_mix_norm,
              pre_ffn_norm, w_gate, w_up, w_down, post_ffn_norm):
    B, S, _ = x.shape
    for l in range(DEPTH):
        h = rms_norm(x, pre_mix_norm[l])
        proj = h @ w_in[l]
        q, k, v_a, u, v_s = jnp.split(
            proj, [ATTN_WIDTH, 2 * ATTN_WIDTH, 3 * ATTN_WIDTH, 3 * ATTN_WIDTH + SGU_WIDTH], axis=-1)
        q = partial_rotary(q.reshape(B, S, ATTN_HEADS, HEAD_DIM), positions).transpose(0, 2, 1, 3)
        k = partial_rotary(k.reshape(B, S, ATTN_HEADS, HEAD_DIM), positions).transpose(0, 2, 1, 3)
        v_a = v_a.reshape(B, S, ATTN_HEADS, HEAD_DIM).transpose(0, 2, 1, 3)
        attn = dilated_mixture_attention(q, k, v_a)
        attn = attn.transpose(0, 2, 1, 3).reshape(B, S, ATTN_WIDTH)
        sgu = spatial_gating(u, v_s, sgu_ln_gain[l], sgu_ln_bias[l],
                             sgu_w_spatial[l], sgu_b_spatial[l])
        mixed = jnp.concatenate([rms_norm(attn, attn_out_norm[l]),
                                 rms_norm(sgu, sgu_out_norm[l])], axis=-1)
        y = mixed @ w_out[l]
        x = x + rms_norm(y, post_mix_norm[l])
        h = rms_norm(x, pre_ffn_norm[l])
        f = (jax.nn.silu(h @ w_gate[l]) * (h @ w_up[l])) @ w_down[l]
        x = x + rms_norm(f, post_ffn_norm[l])
    return x


import jax as _jax
import jax.numpy as _jnp

TWIN_FORMAT = 'train_step'
FWD_PARAMS = ['x', 'positions', 'pre_mix_norm', 'w_in', 'sgu_ln_gain', 'sgu_ln_bias', 'sgu_w_spatial', 'sgu_b_spatial', 'attn_out_norm', 'sgu_out_norm', 'w_out', 'post_mix_norm', 'pre_ffn_norm', 'w_gate', 'w_up', 'w_down', 'post_ffn_norm']
TWIN_WEIGHTS = ['pre_mix_norm', 'w_in', 'sgu_ln_gain', 'sgu_ln_bias', 'sgu_w_spatial', 'sgu_b_spatial', 'attn_out_norm', 'sgu_out_norm', 'w_out', 'post_mix_norm', 'pre_ffn_norm', 'w_gate', 'w_up', 'w_down', 'post_ffn_norm']
TWIN_DIFF_INPUT = 'x'
TWIN_INPUTS = ['x', 'positions', 'pre_mix_norm', 'w_in', 'sgu_ln_gain', 'sgu_ln_bias', 'sgu_w_spatial', 'sgu_b_spatial', 'attn_out_norm', 'sgu_out_norm', 'w_out', 'post_mix_norm', 'pre_ffn_norm', 'w_gate', 'w_up', 'w_down', 'post_ffn_norm', 'loss_target', 'm_pre_mix_norm', 'm_w_in', 'm_sgu_ln_gain', 'm_sgu_ln_bias', 'm_sgu_w_spatial', 'm_sgu_b_spatial', 'm_attn_out_norm', 'm_sgu_out_norm', 'm_w_out', 'm_post_mix_norm', 'm_pre_ffn_norm', 'm_w_gate', 'm_w_up', 'm_w_down', 'm_post_ffn_norm', 'v_pre_mix_norm', 'v_w_in', 'v_sgu_ln_gain', 'v_sgu_ln_bias', 'v_sgu_w_spatial', 'v_sgu_b_spatial', 'v_attn_out_norm', 'v_sgu_out_norm', 'v_w_out', 'v_post_mix_norm', 'v_pre_ffn_norm', 'v_w_gate', 'v_w_up', 'v_w_down', 'v_post_ffn_norm']
TWIN_OUTPUTS = ['loss', 'grad_x', 'grad_pre_mix_norm', 'grad_w_in', 'grad_sgu_ln_gain', 'grad_sgu_ln_bias', 'grad_sgu_w_spatial', 'grad_sgu_b_spatial', 'grad_attn_out_norm', 'grad_sgu_out_norm', 'grad_w_out', 'grad_post_mix_norm', 'grad_pre_ffn_norm', 'grad_w_gate', 'grad_w_up', 'grad_w_down', 'grad_post_ffn_norm', 'delta_pre_mix_norm', 'delta_w_in', 'delta_sgu_ln_gain', 'delta_sgu_ln_bias', 'delta_sgu_w_spatial', 'delta_sgu_b_spatial', 'delta_attn_out_norm', 'delta_sgu_out_norm', 'delta_w_out', 'delta_post_mix_norm', 'delta_pre_ffn_norm', 'delta_w_gate', 'delta_w_up', 'delta_w_down', 'delta_post_ffn_norm', 'new_m_pre_mix_norm', 'new_m_w_in', 'new_m_sgu_ln_gain', 'new_m_sgu_ln_bias', 'new_m_sgu_w_spatial', 'new_m_sgu_b_spatial', 'new_m_attn_out_norm', 'new_m_sgu_out_norm', 'new_m_w_out', 'new_m_post_mix_norm', 'new_m_pre_ffn_norm', 'new_m_w_gate', 'new_m_w_up', 'new_m_w_down', 'new_m_post_ffn_norm', 'new_v_pre_mix_norm', 'new_v_w_in', 'new_v_sgu_ln_gain', 'new_v_sgu_ln_bias', 'new_v_sgu_w_spatial', 'new_v_sgu_b_spatial', 'new_v_attn_out_norm', 'new_v_sgu_out_norm', 'new_v_w_out', 'new_v_post_mix_norm', 'new_v_pre_ffn_norm', 'new_v_w_gate', 'new_v_w_up', 'new_v_w_down', 'new_v_post_ffn_norm']
TWIN_LEAF_KINDS = {'loss': 'loss', 'grad_x': 'grad_x', 'grad_pre_mix_norm': 'grad_w', 'grad_w_in': 'grad_w', 'grad_sgu_ln_gain': 'grad_w', 'grad_sgu_ln_bias': 'grad_w', 'grad_sgu_w_spatial': 'grad_w', 'grad_sgu_b_spatial': 'grad_w', 'grad_attn_out_norm': 'grad_w', 'grad_sgu_out_norm': 'grad_w', 'grad_w_out': 'grad_w', 'grad_post_mix_norm': 'grad_w', 'grad_pre_ffn_norm': 'grad_w', 'grad_w_gate': 'grad_w', 'grad_w_up': 'grad_w', 'grad_w_down': 'grad_w', 'grad_post_ffn_norm': 'grad_w', 'delta_pre_mix_norm': 'delta_w', 'delta_w_in': 'delta_w', 'delta_sgu_ln_gain': 'delta_w', 'delta_sgu_ln_bias': 'delta_w', 'delta_sgu_w_spatial': 'delta_w', 'delta_sgu_b_spatial': 'delta_w', 'delta_attn_out_norm': 'delta_w', 'delta_sgu_out_norm': 'delta_w', 'delta_w_out': 'delta_w', 'delta_post_mix_norm': 'delta_w', 'delta_pre_ffn_norm': 'delta_w', 'delta_w_gate': 'delta_w', 'delta_w_up': 'delta_w', 'delta_w_down': 'delta_w', 'delta_post_ffn_norm': 'delta_w', 'new_m_pre_mix_norm': 'new_m', 'new_m_w_in': 'new_m', 'new_m_sgu_ln_gain': 'new_m', 'new_m_sgu_ln_bias': 'new_m', 'new_m_sgu_w_spatial': 'new_m', 'new_m_sgu_b_spatial': 'new_m', 'new_m_attn_out_norm': 'new_m', 'new_m_sgu_out_norm': 'new_m', 'new_m_w_out': 'new_m', 'new_m_post_mix_norm': 'new_m', 'new_m_pre_ffn_norm': 'new_m', 'new_m_w_gate': 'new_m', 'new_m_w_up': 'new_m', 'new_m_w_down': 'new_m', 'new_m_post_ffn_norm': 'new_m', 'new_v_pre_mix_norm': 'new_v', 'new_v_w_in': 'new_v', 'new_v_sgu_ln_gain': 'new_v', 'new_v_sgu_ln_bias': 'new_v', 'new_v_sgu_w_spatial': 'new_v', 'new_v_sgu_b_spatial': 'new_v', 'new_v_attn_out_norm': 'new_v', 'new_v_sgu_out_norm': 'new_v', 'new_v_w_out': 'new_v', 'new_v_post_mix_norm': 'new_v', 'new_v_pre_ffn_norm': 'new_v', 'new_v_w_gate': 'new_v', 'new_v_w_up': 'new_v', 'new_v_w_down': 'new_v', 'new_v_post_ffn_norm': 'new_v'}


def _forward(args):
    return _fwd_reference(*[args[k] for k in FWD_PARAMS])


def _output_shape():
    out = _jax.eval_shape(lambda: _forward(_fwd_setup_inputs(0)))
    return out.shape, out.dtype

N_MICROBATCH = 1
ADAM_LR = 0.001
ADAM_B1 = 0.9
ADAM_B2 = 0.999
ADAM_EPS = 1e-08
ADAM_WD = 0.01
ADAM_STEP = 10
PER_EXAMPLE_BATCH_AXIS = {'x': 0, 'positions': 0, 'loss_target': 0}
SHARED_INPUTS = []
_WEIGHT_DTYPES = {'pre_mix_norm': _jnp.float32, 'w_in': _jnp.float32, 'sgu_ln_gain': _jnp.float32, 'sgu_ln_bias': _jnp.float32, 'sgu_w_spatial': _jnp.float32, 'sgu_b_spatial': _jnp.float32, 'attn_out_norm': _jnp.float32, 'sgu_out_norm': _jnp.float32, 'w_out': _jnp.float32, 'post_mix_norm': _jnp.float32, 'pre_ffn_norm': _jnp.float32, 'w_gate': _jnp.float32, 'w_up': _jnp.float32, 'w_down': _jnp.float32, 'post_ffn_norm': _jnp.float32}
MOMENT_SCALE = {'pre_mix_norm': 5.254795e-01, 'w_in': 3.369406e-01, 'sgu_ln_gain': 1.048227e-01, 'sgu_ln_bias': 1.074485e-01, 'sgu_w_spatial': 1.436799e-01, 'sgu_b_spatial': 2.032700e-01, 'attn_out_norm': 4.005210e-01, 'sgu_out_norm': 9.142550e-01, 'w_out': 6.381396e-01, 'post_mix_norm': 1.614975e+01, 'pre_ffn_norm': 5.573288e-01, 'w_gate': 1.926196e-01, 'w_up': 2.751596e-01, 'w_down': 4.662861e-01, 'post_ffn_norm': 1.603073e+01}


def _to_microbatches(a, axis):
    t = _jnp.moveaxis(a, axis, 0)
    t = t.reshape((N_MICROBATCH, t.shape[0] // N_MICROBATCH) + t.shape[1:])
    return _jnp.moveaxis(t, 1, axis + 1)


def setup_inputs(seed: int = 0) -> dict:
    inp = _fwd_setup_inputs(seed)
    key = _jax.random.fold_in(_jax.random.key(seed), 7919)
    shape, _ = _output_shape()
    out = dict(inp)
    out["loss_target"] = _jax.random.normal(_jax.random.fold_in(key, 0), shape, _jnp.float32)
    for i, name in enumerate(TWIN_WEIGHTS):
        w = inp[name].astype(_jnp.float32)
        if MOMENT_SCALE is None:
            s = _jnp.sqrt(_jnp.mean(_jnp.square(w)) + 1e-30)
        else:
            s = MOMENT_SCALE[name]
        km, kv = _jax.random.split(_jax.random.fold_in(key, i + 1))
        out[name] = w
        out["m_" + name] = s * _jax.random.normal(km, w.shape, _jnp.float32)
        out["v_" + name] = (s * s) * _jax.random.uniform(kv, w.shape, _jnp.float32, 0.5, 1.5)
    if N_MICROBATCH > 1:
        for name, axis in PER_EXAMPLE_BATCH_AXIS.items():
            out[name] = _to_microbatches(out[name], axis)
    return {'x': out['x'], 'positions': out['positions'], 'pre_mix_norm': out['pre_mix_norm'], 'w_in': out['w_in'], 'sgu_ln_gain': out['sgu_ln_gain'], 'sgu_ln_bias': out['sgu_ln_bias'], 'sgu_w_spatial': out['sgu_w_spatial'], 'sgu_b_spatial': out['sgu_b_spatial'], 'attn_out_norm': out['attn_out_norm'], 'sgu_out_norm': out['sgu_out_norm'], 'w_out': out['w_out'], 'post_mix_norm': out['post_mix_norm'], 'pre_ffn_norm': out['pre_ffn_norm'], 'w_gate': out['w_gate'], 'w_up': out['w_up'], 'w_down': out['w_down'], 'post_ffn_norm': out['post_ffn_norm'], 'loss_target': out['loss_target'], 'm_pre_mix_norm': out['m_pre_mix_norm'], 'm_w_in': out['m_w_in'], 'm_sgu_ln_gain': out['m_sgu_ln_gain'], 'm_sgu_ln_bias': out['m_sgu_ln_bias'], 'm_sgu_w_spatial': out['m_sgu_w_spatial'], 'm_sgu_b_spatial': out['m_sgu_b_spatial'], 'm_attn_out_norm': out['m_attn_out_norm'], 'm_sgu_out_norm': out['m_sgu_out_norm'], 'm_w_out': out['m_w_out'], 'm_post_mix_norm': out['m_post_mix_norm'], 'm_pre_ffn_norm': out['m_pre_ffn_norm'], 'm_w_gate': out['m_w_gate'], 'm_w_up': out['m_w_up'], 'm_w_down': out['m_w_down'], 'm_post_ffn_norm': out['m_post_ffn_norm'], 'v_pre_mix_norm': out['v_pre_mix_norm'], 'v_w_in': out['v_w_in'], 'v_sgu_ln_gain': out['v_sgu_ln_gain'], 'v_sgu_ln_bias': out['v_sgu_ln_bias'], 'v_sgu_w_spatial': out['v_sgu_w_spatial'], 'v_sgu_b_spatial': out['v_sgu_b_spatial'], 'v_attn_out_norm': out['v_attn_out_norm'], 'v_sgu_out_norm': out['v_sgu_out_norm'], 'v_w_out': out['v_w_out'], 'v_post_mix_norm': out['v_post_mix_norm'], 'v_pre_ffn_norm': out['v_pre_ffn_norm'], 'v_w_gate': out['v_w_gate'], 'v_w_up': out['v_w_up'], 'v_w_down': out['v_w_down'], 'v_post_ffn_norm': out['v_post_ffn_norm']}


def _loss(weights, diff, rest, loss_target):
    with _jax.named_scope("forward"):
        args = {**rest, TWIN_DIFF_INPUT: diff, **{k: w.astype(_WEIGHT_DTYPES[k]) for k, w in weights.items()}}
        y = _forward(args)
    with _jax.named_scope("loss_head"):
        err = _jnp.square(y.astype(_jnp.float32) - loss_target)
        return 0.5 * _jnp.sum(_jnp.mean(err, axis=-1)) if err.ndim else 0.5 * err


def _adamw(w, g, m, v):
    m = ADAM_B1 * m + (1.0 - ADAM_B1) * g
    v = ADAM_B2 * v + (1.0 - ADAM_B2) * _jnp.square(g)
    m_hat = m / (1.0 - ADAM_B1 ** ADAM_STEP)
    v_hat = v / (1.0 - ADAM_B2 ** ADAM_STEP)
    delta = -ADAM_LR * (m_hat / (_jnp.sqrt(v_hat) + ADAM_EPS) + ADAM_WD * w)
    return delta, m, v


def reference(x, positions, pre_mix_norm, w_in, sgu_ln_gain, sgu_ln_bias, sgu_w_spatial, sgu_b_spatial, attn_out_norm, sgu_out_norm, w_out, post_mix_norm, pre_ffn_norm, w_gate, w_up, w_down, post_ffn_norm, loss_target, m_pre_mix_norm, m_w_in, m_sgu_ln_gain, m_sgu_ln_bias, m_sgu_w_spatial, m_sgu_b_spatial, m_attn_out_norm, m_sgu_out_norm, m_w_out, m_post_mix_norm, m_pre_ffn_norm, m_w_gate, m_w_up, m_w_down, m_post_ffn_norm, v_pre_mix_norm, v_w_in, v_sgu_ln_gain, v_sgu_ln_bias, v_sgu_w_spatial, v_sgu_b_spatial, v_attn_out_norm, v_sgu_out_norm, v_w_out, v_post_mix_norm, v_pre_ffn_norm, v_w_gate, v_w_up, v_w_down, v_post_ffn_norm):
    given = dict(x=x, positions=positions, pre_mix_norm=pre_mix_norm, w_in=w_in, sgu_ln_gain=sgu_ln_gain, sgu_ln_bias=sgu_ln_bias, sgu_w_spatial=sgu_w_spatial, sgu_b_spatial=sgu_b_spatial, attn_out_norm=attn_out_norm, sgu_out_norm=sgu_out_norm, w_out=w_out, post_mix_norm=post_mix_norm, pre_ffn_norm=pre_ffn_norm, w_gate=w_gate, w_up=w_up, w_down=w_down, post_ffn_norm=post_ffn_norm, loss_target=loss_target, m_pre_mix_norm=m_pre_mix_norm, m_w_in=m_w_in, m_sgu_ln_gain=m_sgu_ln_gain, m_sgu_ln_bias=m_sgu_ln_bias, m_sgu_w_spatial=m_sgu_w_spatial, m_sgu_b_spatial=m_sgu_b_spatial, m_attn_out_norm=m_attn_out_norm, m_sgu_out_norm=m_sgu_out_norm, m_w_out=m_w_out, m_post_mix_norm=m_post_mix_norm, m_pre_ffn_norm=m_pre_ffn_norm, m_w_gate=m_w_gate, m_w_up=m_w_up, m_w_down=m_w_down, m_post_ffn_norm=m_post_ffn_norm, v_pre_mix_norm=v_pre_mix_norm, v_w_in=v_w_in, v_sgu_ln_gain=v_sgu_ln_gain, v_sgu_ln_bias=v_sgu_ln_bias, v_sgu_w_spatial=v_sgu_w_spatial, v_sgu_b_spatial=v_sgu_b_spatial, v_attn_out_norm=v_attn_out_norm, v_sgu_out_norm=v_sgu_out_norm, v_w_out=v_w_out, v_post_mix_norm=v_post_mix_norm, v_pre_ffn_norm=v_pre_ffn_norm, v_w_gate=v_w_gate, v_w_up=v_w_up, v_w_down=v_w_down, v_post_ffn_norm=v_post_ffn_norm)
    weights = {n: given[n] for n in TWIN_WEIGHTS}
    shared = {n: given[n] for n in SHARED_INPUTS}
    per_example = {n: given[n] for n in ['x', 'positions']}
    grad_fn = _jax.value_and_grad(_loss, argnums=(0, 1))

    def one_microbatch(ex, loss_target):
        ex = dict(ex)
        diff = ex.pop(TWIN_DIFF_INPUT)
        return grad_fn(weights, diff, {**shared, **ex}, loss_target)

    if N_MICROBATCH == 1:
        loss, (grad_w, grad_x) = one_microbatch(per_example, given["loss_target"])
    else:
        def body(carry, xs):
            loss_sum, grad_sum = carry
            l_k, (gw_k, gx_k) = one_microbatch(xs[0], xs[1])
            with _jax.named_scope("update"):
                return (loss_sum + l_k, _jax.tree.map(_jnp.add, grad_sum, gw_k)), gx_k

        init = (_jnp.zeros((), _jnp.float32), _jax.tree.map(_jnp.zeros_like, weights))
        (loss, grad_w), grad_x = _jax.lax.scan(body, init, (per_example, given["loss_target"]))
    with _jax.named_scope("update"):
        delta_w, new_m, new_v = {}, {}, {}
        for n in TWIN_WEIGHTS:
            delta_w[n], new_m[n], new_v[n] = _adamw(weights[n], grad_w[n], given["m_" + n], given["v_" + n])
    return (loss, grad_x, *[grad_w[n] for n in TWIN_WEIGHTS], *[delta_w[n] for n in TWIN_WEIGHTS],
            *[new_m[n] for n in TWIN_WEIGHTS], *[new_v[n] for n in TWIN_WEIGHTS])
```

```python
import functools

import numpy as np
import jax
import jax.numpy as jnp
from jax import lax
from jax.experimental import pallas as pl
from jax.experimental.pallas import tpu as pltpu

F32 = jnp.float32
BF16 = jnp.bfloat16

SEQ = 2048
D_MODEL = 1024
ATTN_W = 512
SGU_W = 512
HEAD_DIM = 64
N_GROUPS = 8
CHUNK = 128
D_FF = 2816
IN_W = 3 * ATTN_W + 2 * SGU_W
DILATIONS = (1, 4, 16)
ROPE_THETA = 500000.0
ROT_DIM = 16
ROT_HALF = 8
RMS_EPS = 1e-6
LN_EPS = 1e-5
Q_SCALE = 0.125
NEG = -1e30

N_DEV = 8
MESH_AXES = ("x", "y", "c")
MESH = pl.DeviceIdType.MESH

ADAM_LR = 0.001
ADAM_B1 = 0.9
ADAM_B2 = 0.999
ADAM_EPS = 1e-08
ADAM_WD = 0.01
ADAM_STEP = 10

VMEM_LIMIT = 60 * 1024 * 1024
ANY = pl.BlockSpec(memory_space=pl.ANY)

SMALL = (("pre_mix_norm", 1024), ("sgu_ln_gain", 512), ("sgu_ln_bias", 512), ("sgu_w_spatial", 8 * 128 * 128),
         ("sgu_b_spatial", 1024), ("attn_out_norm", 512), ("sgu_out_norm", 512), ("post_mix_norm", 1024),
         ("pre_ffn_norm", 1024), ("post_ffn_norm", 1024))
SMALL_ROWS = 1152


def _params(sem=("arbitrary",)):
    return pltpu.CompilerParams(dimension_semantics=sem, vmem_limit_bytes=VMEM_LIMIT)


def _dot(a, b):
    return jnp.dot(a, b, preferred_element_type=F32)


def _dot_nt(a, b):
    return lax.dot_general(a, b, (((1,), (1,)), ((), ())), preferred_element_type=F32)


def _dot_tn(a, b):
    return lax.dot_general(a, b, (((0,), (0,)), ((), ())), preferred_element_type=F32)


def _rms(z):
    return lax.rsqrt(jnp.mean(z * z, axis=-1, keepdims=True) + RMS_EPS)


def _rms_bwd(z, gain, d):
    r = _rms(z)
    n = z * r
    dn = d * gain
    dz = r * (dn - n * jnp.mean(dn * n, axis=-1, keepdims=True))
    return dz, jnp.sum(d * n, axis=0, keepdims=True)


def _gelu(z):
    return 0.5 * z * (1.0 + lax.erf(z * np.float32(1.0 / np.sqrt(2.0))))


def _gelu_grad(z):
    cdf = 0.5 * (1.0 + lax.erf(z * np.float32(1.0 / np.sqrt(2.0))))
    return cdf + z * jnp.exp(-0.5 * z * z) * np.float32(1.0 / np.sqrt(2.0 * np.pi))


def _rot_tables(pos_col, invf, ma, mb):
    ang = pos_col.astype(F32) * invf
    s = jnp.sin(ang)
    return jnp.cos(ang), s * ma, s * mb


def _rot(t, c, sa, sb):
    return t * c + pltpu.roll(t, 120, 1) * sa + pltpu.roll(t, 8, 1) * sb


def _rot_t(d, c, sa, sb):
    return d * c + pltpu.roll(d * sa, 8, 1) + pltpu.roll(d * sb, 120, 1)


def _rot_consts():
    lane = np.arange(128) % HEAD_DIM
    inv_freq = (np.float32(ROPE_THETA) ** (-np.arange(0, ROT_DIM, 2, dtype=np.float32) / np.float32(ROT_DIM))).astype(np.float32)
    invf = np.where(lane < ROT_DIM, inv_freq[lane % ROT_HALF], 0.0).astype(np.float32)
    ma = np.where(lane < ROT_HALF, -1.0, 0.0).astype(np.float32)
    mb = np.where((lane >= ROT_HALF) & (lane < ROT_DIM), 1.0, 0.0).astype(np.float32)
    return jnp.asarray(invf[None]), jnp.asarray(ma[None]), jnp.asarray(mb[None])


def _row_spec(tm, w):
    return pl.BlockSpec((tm, w), lambda i: (i, 0))


def _full_spec(shape):
    return pl.BlockSpec(shape, lambda i: (0,) * len(shape))


def in_proj(x, pos_col, g1, w_in_t, rot):
    tm = 512

    def body(x_ref, pos_ref, g_ref, w_ref, invf_ref, ma_ref, mb_ref, h_ref, q_ref, k_ref, v_ref, u_ref, vs_ref):
        xf = x_ref[...]
        h = (xf * _rms(xf) * g_ref[...]).astype(BF16)
        h_ref[...] = h
        proj = _dot_nt(h, w_ref[...])
        c, sa, sb = _rot_tables(pos_ref[...], invf_ref[...], ma_ref[...], mb_ref[...])
        for j in range(ATTN_W // 128):
            q_ref[:, j * 128:(j + 1) * 128] = _rot(proj[:, j * 128:(j + 1) * 128], c, sa, sb) * Q_SCALE
            k_ref[:, j * 128:(j + 1) * 128] = _rot(proj[:, ATTN_W + j * 128:ATTN_W + (j + 1) * 128], c, sa, sb)
        v_ref[...] = proj[:, 2 * ATTN_W:3 * ATTN_W]
        u_ref[...] = proj[:, 3 * ATTN_W:3 * ATTN_W + SGU_W]
        vs_ref[...] = proj[:, 3 * ATTN_W + SGU_W:]

    act = jax.ShapeDtypeStruct((SEQ, 512), F32)
    return pl.pallas_call(
        body, name="in_proj", grid=(SEQ // tm,),
        in_specs=[_row_spec(tm, D_MODEL), _row_spec(tm, 1), _full_spec((1, D_MODEL)), _full_spec((IN_W, D_MODEL)),
                  _full_spec((1, 128)), _full_spec((1, 128)), _full_spec((1, 128))],
        out_specs=[_row_spec(tm, D_MODEL)] + [_row_spec(tm, 512)] * 5,
        out_shape=[jax.ShapeDtypeStruct((SEQ, D_MODEL), BF16)] + [act] * 5,
        compiler_params=_params(),
    )(x, pos_col, g1, w_in_t, *rot)


def _attn_masks():
    row2 = lax.broadcasted_iota(jnp.int32, (128, 256), 0)
    col2 = lax.broadcasted_iota(jnp.int32, (128, 256), 1)
    mask2 = jnp.logical_or(jnp.logical_and(col2 < 128, col2 >= row2), jnp.logical_and(col2 >= 128, (col2 - 128) <= row2))
    row1 = lax.broadcasted_iota(jnp.int32, (128, 128), 0)
    col1 = lax.broadcasted_iota(jnp.int32, (128, 128), 1)
    return col1 < HEAD_DIM, col1 <= row1, mask2


def _for_each_block(fn):
    for p, d in enumerate(DILATIONS):
        n_blk = SEQ // d // 128

        def first(r, carry, p=p, d=d):
            fn(p, d, r, None)
            return carry

        if d == 1:
            fn(p, d, 0, None)
        else:
            lax.fori_loop(0, d, first, 0)
        if n_blk > 1:
            def rest(i, carry, p=p, d=d):
                r = i % d
                n = 1 + i // d
                q0 = n * (128 * d) + r
                prev0 = (n - 1) * (128 * d) + r
                if d == 1:
                    q0, prev0 = pl.multiple_of(q0, 128), pl.multiple_of(prev0, 128)
                fn(p, d, q0, prev0)
                return carry

            lax.fori_loop(0, d * (n_blk - 1), rest, 0)


def attn_fwd(q, k, v):
    def body(q_ref, k_ref, v_ref, o_ref, lse_ref, op_ref, lp_ref):
        head0, mask1, mask2 = _attn_masks()

        def block(p, d, q0, prev0):
            rows = pl.ds(q0, 128, stride=d) if d > 1 else pl.ds(q0, 128)
            qb = q_ref[rows, :]
            if prev0 is None:
                kk = k_ref[rows, :].astype(BF16)
                vv = v_ref[rows, :].astype(BF16)
                mask = mask1
            else:
                prev = pl.ds(prev0, 128, stride=d) if d > 1 else pl.ds(prev0, 128)
                kk = jnp.concatenate([k_ref[prev, :], k_ref[rows, :]], axis=0).astype(BF16)
                vv = jnp.concatenate([v_ref[prev, :], v_ref[rows, :]], axis=0).astype(BF16)
                mask = mask2
            outs, lses = [], []
            for hm in (head0, jnp.logical_not(head0)):
                qm = jnp.where(hm, qb, 0.0).astype(BF16)
                s = jnp.where(mask, _dot_nt(qm, kk), NEG)
                m = jnp.max(s, axis=-1, keepdims=True)
                e = jnp.exp(s - m)
                l = jnp.sum(e, axis=-1, keepdims=True)
                outs.append(_dot(e.astype(BF16), vv) / l)
                lses.append(jnp.broadcast_to(m + jnp.log(l), (128, 128)))
            op_ref[p, rows, :] = jnp.where(head0, outs[0], outs[1])
            lp_ref[p, rows, :] = jnp.where(head0, lses[0], lses[1])

        _for_each_block(block)

        def combine(i, carry):
            rows = pl.ds(pl.multiple_of(i * 256, 256), 256)
            ls = [lp_ref[p, rows, :] for p in range(3)]
            m = jnp.maximum(jnp.maximum(ls[0], ls[1]), ls[2])
            lse = m + jnp.log(jnp.exp(ls[0] - m) + jnp.exp(ls[1] - m) + jnp.exp(ls[2] - m))
            o = jnp.zeros((256, 128), F32)
            for p in range(3):
                o = o + jnp.exp(ls[p] - lse) * op_ref[p, rows, :]
            o_ref[rows, :] = o
            lse_ref[rows, :] = lse
            return carry

        lax.fori_loop(0, SEQ // 256, combine, 0)

    slab = pl.BlockSpec((SEQ, 128), lambda i: (0, i))
    out = jax.ShapeDtypeStruct((SEQ, ATTN_W), F32)
    return pl.pallas_call(
        body, name="attn_fwd", grid=(ATTN_W // 128,),
        in_specs=[slab] * 3, out_specs=[slab] * 2, out_shape=[out, out],
        scratch_shapes=[pltpu.VMEM((3, SEQ, 128), F32), pltpu.VMEM((3, SEQ, 128), F32)],
        compiler_params=_params(),
    )(q, k, v)


def _causal_weights(w_ref):
    row = lax.broadcasted_iota(jnp.int32, (CHUNK, CHUNK), 0)
    col = lax.broadcasted_iota(jnp.int32, (CHUNK, CHUNK), 1)
    return [jnp.where(col <= row, w_ref[g], 0.0).astype(BF16) for g in range(N_GROUPS)], col <= row


def _sgu_chunk_fwd(u, vs, lg, lb, wc, bfull, head0):
    ug = _gelu(u)
    vg = _gelu(vs)
    xc = vg - jnp.mean(vg, axis=-1, keepdims=True)
    rstd = lax.rsqrt(jnp.mean(xc * xc, axis=-1, keepdims=True) + LN_EPS)
    xhat = xc * rstd
    vn = xhat * lg + lb
    mixed = []
    for gp in range(SGU_W // 128):
        vp = vn[:, gp * 128:(gp + 1) * 128].astype(BF16)
        mixed.append(jnp.where(head0, _dot(wc[2 * gp], vp), _dot(wc[2 * gp + 1], vp)))
    ms = jnp.concatenate(mixed, axis=1) + bfull
    return ug, xhat, rstd, vn, ms


def sgu_fwd(u, vs, lg, lb, w_sp, bfull):
    cpb = 4

    def body(u_ref, vs_ref, lg_ref, lb_ref, w_ref, b_ref, o_ref):
        wc, _ = _causal_weights(w_ref)
        head0 = lax.broadcasted_iota(jnp.int32, (CHUNK, 128), 1) < HEAD_DIM
        for ci in range(cpb):
            rows = pl.ds(ci * CHUNK, CHUNK)
            ug, _, _, _, ms = _sgu_chunk_fwd(u_ref[rows, :], vs_ref[rows, :], lg_ref[...], lb_ref[...], wc, b_ref[...], head0)
            o_ref[rows, :] = ug * ms

    tm = cpb * CHUNK
    return pl.pallas_call(
        body, name="sgu_fwd", grid=(SEQ // tm,),
        in_specs=[_row_spec(tm, SGU_W), _row_spec(tm, SGU_W), _full_spec((1, SGU_W)), _full_spec((1, SGU_W)),
                  _full_spec((N_GROUPS, CHUNK, CHUNK)), _full_spec((CHUNK, SGU_W))],
        out_specs=_row_spec(tm, SGU_W), out_shape=jax.ShapeDtypeStruct((SEQ, SGU_W), F32),
        compiler_params=_params(),
    )(u, vs, lg, lb, w_sp, bfull)


def out_proj(attn, sgu, x, ga, gs, w_out, gpm, gpf):
    tm = 512

    def body(a_ref, s_ref, x_ref, ga_ref, gs_ref, w_ref, gpm_ref, gpf_ref, mix_ref, y_ref, x2_ref, h2_ref):
        a = a_ref[...]
        s = s_ref[...]
        an = (a * _rms(a) * ga_ref[...]).astype(BF16)
        sn = (s * _rms(s) * gs_ref[...]).astype(BF16)
        mix_ref[:, :ATTN_W] = an
        mix_ref[:, ATTN_W:] = sn
        y = _dot(an, w_ref[:ATTN_W, :]) + _dot(sn, w_ref[ATTN_W:, :])
        y_ref[...] = y
        x2 = x_ref[...] + y * _rms(y) * gpm_ref[...]
        x2_ref[...] = x2
        h2_ref[...] = (x2 * _rms(x2) * gpf_ref[...]).astype(BF16)

    wide = jax.ShapeDtypeStruct((SEQ, D_MODEL), F32)
    wide16 = jax.ShapeDtypeStruct((SEQ, D_MODEL), BF16)
    return pl.pallas_call(
        body, name="out_proj", grid=(SEQ // tm,),
        in_specs=[_row_spec(tm, ATTN_W), _row_spec(tm, SGU_W), _row_spec(tm, D_MODEL), _full_spec((1, ATTN_W)),
                  _full_spec((1, SGU_W)), _full_spec((D_MODEL, D_MODEL)), _full_spec((1, D_MODEL)), _full_spec((1, D_MODEL))],
        out_specs=[_row_spec(tm, D_MODEL)] * 4, out_shape=[wide16, wide, wide, wide16],
        compiler_params=_params(),
    )(attn, sgu, x, ga, gs, w_out, gpm, gpf)


def ffn_up(h2, w_gate_t, w_up_t):
    tm = 256

    def body(h_ref, wg_ref, wu_ref, g_ref, u_ref, a_ref):
        h = h_ref[...]
        g = _dot_nt(h, wg_ref[...])
        u = _dot_nt(h, wu_ref[...])
        g_ref[...] = g
        u_ref[...] = u
        a_ref[...] = (g * jax.nn.sigmoid(g) * u).astype(BF16)

    ff = jax.ShapeDtypeStruct((SEQ, D_FF), F32)
    return pl.pallas_call(
        body, name="ffn_up", grid=(SEQ // tm,),
        in_specs=[_row_spec(tm, D_MODEL), _full_spec((D_FF, D_MODEL)), _full_spec((D_FF, D_MODEL))],
        out_specs=[_row_spec(tm, D_FF)] * 3, out_shape=[ff, ff, jax.ShapeDtypeStruct((SEQ, D_FF), BF16)],
        compiler_params=_params(),
    )(h2, w_gate_t, w_up_t)


def ffn_down_loss(act, w_down, x2, gpo, target):
    tm = 512

    def body(a_ref, w_ref, x2_ref, g_ref, t_ref, df_ref, dx3_ref, dg_ref, loss_ref):
        f = _dot(a_ref[...], w_ref[...])
        gain = g_ref[...]
        err = x2_ref[...] + f * _rms(f) * gain - t_ref[...]
        dx3 = err * np.float32(1.0 / D_MODEL)
        dx3_ref[...] = dx3
        df, dg = _rms_bwd(f, gain, dx3)
        df_ref[...] = df.astype(BF16)

        @pl.when(pl.program_id(0) == 0)
        def _():
            dg_ref[...] = jnp.zeros_like(dg_ref)
            loss_ref[...] = jnp.zeros_like(loss_ref)

        dg_ref[...] += dg
        loss_ref[...] += jnp.sum(err * err, axis=(0, 1), keepdims=True)

    return pl.pallas_call(
        body, name="ffn_down_loss", grid=(SEQ // tm,),
        in_specs=[_row_spec(tm, D_FF), _full_spec((D_FF, D_MODEL)), _row_spec(tm, D_MODEL), _full_spec((1, D_MODEL)),
                  _row_spec(tm, D_MODEL)],
        out_specs=[_row_spec(tm, D_MODEL), _row_spec(tm, D_MODEL), _full_spec((1, D_MODEL)), _full_spec((1, 1))],
        out_shape=[jax.ShapeDtypeStruct((SEQ, D_MODEL), BF16), jax.ShapeDtypeStruct((SEQ, D_MODEL), F32),
                   jax.ShapeDtypeStruct((1, D_MODEL), F32), jax.ShapeDtypeStruct((1, 1), F32)],
        compiler_params=_params(),
    )(act, w_down, x2, gpo, target)


def ffn_act_bwd(df, w_down, gate, up):
    tm = 256

    def body(df_ref, w_ref, g_ref, u_ref, dg_ref, du_ref):
        dact = _dot_nt(df_ref[...], w_ref[...])
        g = g_ref[...]
        s = jax.nn.sigmoid(g)
        du_ref[...] = (dact * g * s).astype(BF16)
        dg_ref[...] = (dact * u_ref[...] * (s * (1.0 + g * (1.0 - s)))).astype(BF16)

    ff16 = jax.ShapeDtypeStruct((SEQ, D_FF), BF16)
    return pl.pallas_call(
        body, name="ffn_act_bwd", grid=(SEQ // tm,),
        in_specs=[_row_spec(tm, D_MODEL), _full_spec((D_FF, D_MODEL)), _row_spec(tm, D_FF), _row_spec(tm, D_FF)],
        out_specs=[_row_spec(tm, D_FF)] * 2, out_shape=[ff16, ff16],
        compiler_params=_params(),
    )(df, w_down, gate, up)


def ffn_in_bwd(dgate, dup, w_gate_t, w_up_t, x2, gpf, dx3, y, gpm):
    tm = 256

    def body(dg_ref, du_ref, wg_ref, wu_ref, x2_ref, gpf_ref, dx3_ref, y_ref, gpm_ref, dx2_ref, dy_ref, dgpf_ref, dgpm_ref):
        dh2 = _dot(dg_ref[...], wg_ref[...]) + _dot(du_ref[...], wu_ref[...])
        dz, dgpf = _rms_bwd(x2_ref[...], gpf_ref[...], dh2)
        dx2 = dx3_ref[...] + dz
        dx2_ref[...] = dx2
        dy, dgpm = _rms_bwd(y_ref[...], gpm_ref[...], dx2)
        dy_ref[...] = dy.astype(BF16)

        @pl.when(pl.program_id(0) == 0)
        def _():
            dgpf_ref[...] = jnp.zeros_like(dgpf_ref)
            dgpm_ref[...] = jnp.zeros_like(dgpm_ref)

        dgpf_ref[...] += dgpf
        dgpm_ref[...] += dgpm

    vec = jax.ShapeDtypeStruct((1, D_MODEL), F32)
    return pl.pallas_call(
        body, name="ffn_in_bwd", grid=(SEQ // tm,),
        in_specs=[_row_spec(tm, D_FF), _row_spec(tm, D_FF), _full_spec((D_FF, D_MODEL)), _full_spec((D_FF, D_MODEL)),
                  _row_spec(tm, D_MODEL), _full_spec((1, D_MODEL)), _row_spec(tm, D_MODEL), _row_spec(tm, D_MODEL),
                  _full_spec((1, D_MODEL))],
        out_specs=[_row_spec(tm, D_MODEL), _row_spec(tm, D_MODEL), _full_spec((1, D_MODEL)), _full_spec((1, D_MODEL))],
        out_shape=[jax.ShapeDtypeStruct((SEQ, D_MODEL), F32), jax.ShapeDtypeStruct((SEQ, D_MODEL), BF16), vec, vec],
        compiler_params=_params(),
    )(dgate, dup, w_gate_t, w_up_t, x2, gpf, dx3, y, gpm)


def weight_grad(name, a, b):
    m, n = a.shape[1], b.shape[1]
    tr = 256

    def body(a_ref, b_ref, o_ref):
        o_ref[...] = _dot_tn(a_ref[...], b_ref[...]).astype(BF16)

    return pl.pallas_call(
        body, name=name, grid=(m // tr,),
        in_specs=[pl.BlockSpec((SEQ, tr), lambda i: (0, i)), _full_spec((SEQ, n))],
        out_specs=_row_spec(tr, n), out_shape=jax.ShapeDtypeStruct((m, n), BF16),
        compiler_params=_params(),
    )(a, b)


def mix_bwd(dy, w_out, attn, sgu, ga, gs):
    tm = 512

    def body(dy_ref, w_ref, a_ref, s_ref, ga_ref, gs_ref, da_ref, ds_ref, dga_ref, dgs_ref):
        dy = dy_ref[...]
        da, dga = _rms_bwd(a_ref[...], ga_ref[...], _dot_nt(dy, w_ref[:ATTN_W, :]))
        ds, dgs = _rms_bwd(s_ref[...], gs_ref[...], _dot_nt(dy, w_ref[ATTN_W:, :]))
        da_ref[...] = da
        ds_ref[...] = ds

        @pl.when(pl.program_id(0) == 0)
        def _():
            dga_ref[...] = jnp.zeros_like(dga_ref)
            dgs_ref[...] = jnp.zeros_like(dgs_ref)

        dga_ref[...] += dga
        dgs_ref[...] += dgs

    half = jax.ShapeDtypeStruct((SEQ, 512), F32)
    vec = jax.ShapeDtypeStruct((1, 512), F32)
    return pl.pallas_call(
        body, name="mix_bwd", grid=(SEQ // tm,),
        in_specs=[_row_spec(tm, D_MODEL), _full_spec((D_MODEL, D_MODEL)), _row_spec(tm, 512), _row_spec(tm, 512),
                  _full_spec((1, 512)), _full_spec((1, 512))],
        out_specs=[_row_spec(tm, 512), _row_spec(tm, 512), _full_spec((1, 512)), _full_spec((1, 512))],
        out_shape=[half, half, vec, vec],
        compiler_params=_params(),
    )(dy, w_out, attn, sgu, ga, gs)


def sgu_bwd(u, vs, dsgu, lg, lb, w_sp, bfull):
    cpb = 4

    def body(u_ref, vs_ref, d_ref, lg_ref, lb_ref, w_ref, b_ref, du_ref, dvs_ref, dlg_ref, dlb_ref, dw_ref, db_ref):
        wc, causal = _causal_weights(w_ref)
        head0 = lax.broadcasted_iota(jnp.int32, (CHUNK, 128), 1) < HEAD_DIM
        lg = lg_ref[...]

        @pl.when(pl.program_id(0) == 0)
        def _():
            dlg_ref[...] = jnp.zeros_like(dlg_ref)
            dlb_ref[...] = jnp.zeros_like(dlb_ref)
            dw_ref[...] = jnp.zeros_like(dw_ref)
            db_ref[...] = jnp.zeros_like(db_ref)

        for ci in range(cpb):
            rows = pl.ds(ci * CHUNK, CHUNK)
            u = u_ref[rows, :]
            vs = vs_ref[rows, :]
            d = d_ref[rows, :]
            ug, xhat, rstd, vn, ms = _sgu_chunk_fwd(u, vs, lg, lb_ref[...], wc, b_ref[...], head0)
            du_ref[rows, :] = (d * ms * _gelu_grad(u)).astype(BF16)
            dms = d * ug
            db_ref[...] += dms
            dvn = []
            for gp in range(SGU_W // 128):
                dmp = dms[:, gp * 128:(gp + 1) * 128]
                dm0 = jnp.where(head0, dmp, 0.0).astype(BF16)
                dm1 = jnp.where(head0, 0.0, dmp).astype(BF16)
                vp = vn[:, gp * 128:(gp + 1) * 128].astype(BF16)
                dw_ref[2 * gp] += _dot_nt(dm0, vp)
                dw_ref[2 * gp + 1] += _dot_nt(dm1, vp)
                dvn.append(_dot_tn(wc[2 * gp], dm0) + _dot_tn(wc[2 * gp + 1], dm1))
            dvn = jnp.concatenate(dvn, axis=1)
            dlg_ref[...] += jnp.sum(dvn * xhat, axis=0, keepdims=True)
            dlb_ref[...] += jnp.sum(dvn, axis=0, keepdims=True)
            dxh = dvn * lg
            dvg = rstd * (dxh - jnp.mean(dxh, axis=-1, keepdims=True) - xhat * jnp.mean(dxh * xhat, axis=-1, keepdims=True))
            dvs_ref[rows, :] = (dvg * _gelu_grad(vs)).astype(BF16)

        @pl.when(pl.program_id(0) == pl.num_programs(0) - 1)
        def _():
            for g in range(N_GROUPS):
                dw_ref[g] = jnp.where(causal, dw_ref[g], 0.0)

    tm = cpb * CHUNK
    half16 = jax.ShapeDtypeStruct((SEQ, SGU_W), BF16)
    vec = jax.ShapeDtypeStruct((1, SGU_W), F32)
    return pl.pallas_call(
        body, name="sgu_bwd", grid=(SEQ // tm,),
        in_specs=[_row_spec(tm, SGU_W)] * 3 + [_full_spec((1, SGU_W)), _full_spec((1, SGU_W)),
                                                  _full_spec((N_GROUPS, CHUNK, CHUNK)), _full_spec((CHUNK, SGU_W))],
        out_specs=[_row_spec(tm, SGU_W), _row_spec(tm, SGU_W), _full_spec((1, SGU_W)), _full_spec((1, SGU_W)),
                   _full_spec((N_GROUPS, CHUNK, CHUNK)), _full_spec((CHUNK, SGU_W))],
        out_shape=[half16, half16, vec, vec, jax.ShapeDtypeStruct((N_GROUPS, CHUNK, CHUNK), F32),
                   jax.ShapeDtypeStruct((CHUNK, SGU_W), F32)],
        compiler_params=_params(),
    )(u, vs, dsgu, lg, lb, w_sp, bfull)


def attn_bwd(q, k, v, o, lse, do, pos_col, rot):
    def body(q_ref, k_ref, v_ref, o_ref, lse_ref, do_ref, pos_ref, invf_ref, ma_ref, mb_ref,
             dq_ref, dk_ref, dv_ref, dqa_ref, dka_ref, dva_ref, dlt_ref):
        head0, mask1, mask2 = _attn_masks()
        dqa_ref[...] = jnp.zeros_like(dqa_ref)
        dka_ref[...] = jnp.zeros_like(dka_ref)
        dva_ref[...] = jnp.zeros_like(dva_ref)

        def delta(i, carry):
            rows = pl.ds(pl.multiple_of(i * 256, 256), 256)
            prod = do_ref[rows, :] * o_ref[rows, :]
            h0 = lax.broadcasted_iota(jnp.int32, (256, 128), 1) < HEAD_DIM
            d0 = jnp.sum(jnp.where(h0, prod, 0.0), axis=-1, keepdims=True)
            d1 = jnp.sum(jnp.where(h0, 0.0, prod), axis=-1, keepdims=True)
            dlt_ref[rows, :] = jnp.where(h0, d0, d1)
            return carry

        lax.fori_loop(0, SEQ // 256, delta, 0)

        def block(p, d, q0, prev0):
            rows = pl.ds(q0, 128, stride=d) if d > 1 else pl.ds(q0, 128)
            qb = q_ref[rows, :]
            dob = do_ref[rows, :]
            lse_b = lse_ref[rows, :]
            dlt_b = dlt_ref[rows, :]
            if prev0 is None:
                prev = None
                kk = k_ref[rows, :].astype(BF16)
                vv = v_ref[rows, :].astype(BF16)
                mask = mask1
            else:
                prev = pl.ds(prev0, 128, stride=d) if d > 1 else pl.ds(prev0, 128)
                kk = jnp.concatenate([k_ref[prev, :], k_ref[rows, :]], axis=0).astype(BF16)
                vv = jnp.concatenate([v_ref[prev, :], v_ref[rows, :]], axis=0).astype(BF16)
                mask = mask2
            dq_b = jnp.zeros((128, 128), F32)
            dk_b = jnp.zeros(kk.shape, F32)
            dv_b = jnp.zeros(kk.shape, F32)
            for h, hm in enumerate((head0, jnp.logical_not(head0))):
                lane0 = h * HEAD_DIM
                qm = jnp.where(hm, qb, 0.0).astype(BF16)
                dom = jnp.where(hm, dob, 0.0).astype(BF16)
                s = _dot_nt(qm, kk)
                pr = jnp.where(mask, jnp.exp(s - lse_b[:, lane0:lane0 + 1]), 0.0)
                dp = _dot_nt(dom, vv)
                ds = (pr * (dp - dlt_b[:, lane0:lane0 + 1])).astype(BF16)
                dv_b = dv_b + _dot_tn(pr.astype(BF16), dom)
                dk_b = dk_b + _dot_tn(ds, qm)
                dq_b = dq_b + jnp.where(hm, _dot(ds, kk), 0.0)
            dqa_ref[rows, :] += dq_b
            if prev is None:
                dka_ref[rows, :] += dk_b
                dva_ref[rows, :] += dv_b
            else:
                dka_ref[prev, :] += dk_b[:128]
                dva_ref[prev, :] += dv_b[:128]
                dka_ref[rows, :] += dk_b[128:]
                dva_ref[rows, :] += dv_b[128:]

        _for_each_block(block)

        def finish(i, carry):
            rows = pl.ds(pl.multiple_of(i * 256, 256), 256)
            c, sa, sb = _rot_tables(pos_ref[rows, :], invf_ref[...], ma_ref[...], mb_ref[...])
            dq_ref[rows, :] = _rot_t(dqa_ref[rows, :] * Q_SCALE, c, sa, sb).astype(BF16)
            dk_ref[rows, :] = _rot_t(dka_ref[rows, :], c, sa, sb).astype(BF16)
            dv_ref[rows, :] = dva_ref[rows, :].astype(BF16)
            return carry

        lax.fori_loop(0, SEQ // 256, finish, 0)

    slab = pl.BlockSpec((SEQ, 128), lambda i: (0, i))
    out = jax.ShapeDtypeStruct((SEQ, ATTN_W), BF16)
    acc = pltpu.VMEM((SEQ, 128), F32)
    return pl.pallas_call(
        body, name="attn_bwd", grid=(ATTN_W // 128,),
        in_specs=[slab] * 6 + [_full_spec((SEQ, 1)), _full_spec((1, 128)), _full_spec((1, 128)), _full_spec((1, 128))],
        out_specs=[slab] * 3, out_shape=[out, out, out],
        scratch_shapes=[acc, acc, acc, acc],
        compiler_params=_params(),
    )(q, k, v, o, lse, do, pos_col, *rot)


def in_bwd(dproj, w_in_t, x, g1, dx2):
    tm = 512

    def body(dp_ref, w_ref, x_ref, g_ref, dx2_ref, dx_ref, dg_ref):
        dh1 = _dot(dp_ref[...], w_ref[...])
        dz, dg = _rms_bwd(x_ref[...], g_ref[...], dh1)
        dx_ref[...] = dx2_ref[...] + dz

        @pl.when(pl.program_id(0) == 0)
        def _():
            dg_ref[...] = jnp.zeros_like(dg_ref)

        dg_ref[...] += dg

    return pl.pallas_call(
        body, name="in_bwd", grid=(SEQ // tm,),
        in_specs=[_row_spec(tm, IN_W), _full_spec((IN_W, D_MODEL)), _row_spec(tm, D_MODEL), _full_spec((1, D_MODEL)),
                  _row_spec(tm, D_MODEL)],
        out_specs=[_row_spec(tm, D_MODEL), _full_spec((1, D_MODEL))],
        out_shape=[jax.ShapeDtypeStruct((SEQ, D_MODEL), F32), jax.ShapeDtypeStruct((1, D_MODEL), F32)],
        compiler_params=_params(),
    )(dproj, w_in_t, x, g1, dx2)


def _coords():
    return lax.axis_index("x"), lax.axis_index("y"), lax.axis_index("c")


def all_gather_rows(name, shards):
    n = len(shards)
    nrows = [s.shape[0] for s in shards]

    def body(*refs):
        ins, outs = refs[:n], refs[n:2 * n]
        send_sems, recv_sems, local_sems = refs[2 * n:]
        x, y, c = _coords()
        me, sibling = (x, y, c), (x, y, 1 - c)
        chips = [(1 - x, y), (x, 1 - y), (1 - x, 1 - y)]

        def rows(a, px, py, pc):
            return outs[a].at[pl.ds((4 * px + 2 * py + pc) * nrows[a], nrows[a]), :]

        def copy(a, k, block, to, src=None):
            return pltpu.make_async_remote_copy(
                src_ref=rows(a, *block) if src is None else src, dst_ref=rows(a, *block),
                send_sem=send_sems.at[7 * a + k], recv_sem=recv_sems.at[7 * a + k], device_id=to, device_id_type=MESH)

        mine = [pltpu.make_async_copy(ins[a], rows(a, *me), local_sems.at[a]) for a in range(n)]
        for cp in mine:
            cp.start()
        first = []
        for a in range(n):
            first.append(copy(a, 0, me, sibling, src=ins[a]))
            first += [copy(a, 1 + j, me, (*chip, c), src=ins[a]) for j, chip in enumerate(chips)]
        for cp in first:
            cp.start()
        passed = []
        for j, chip in enumerate(chips):
            for a in range(n):
                copy(a, 1 + j, (*chip, c), me).wait_recv()
                fwd = copy(a, 4 + j, (*chip, c), sibling)
                fwd.start()
                passed.append(fwd)
        for a in range(n):
            copy(a, 0, sibling, me).wait_recv()
            for j, chip in enumerate(chips):
                copy(a, 4 + j, (*chip, 1 - c), me).wait_recv()
        for cp in first + passed:
            cp.wait_send()
        for cp in mine:
            cp.wait()

    return pl.pallas_call(
        body, name=name,
        in_specs=[ANY] * n, out_specs=[ANY] * n,
        out_shape=[jax.ShapeDtypeStruct((N_DEV * s.shape[0], s.shape[1]), s.dtype) for s in shards],
        scratch_shapes=[pltpu.SemaphoreType.DMA((7 * n,)), pltpu.SemaphoreType.DMA((7 * n,)), pltpu.SemaphoreType.DMA((n,))],
    )(*shards)


def exchange_cores(name, grads):
    n = len(grads)

    def body(*refs):
        ins, outs = refs[:n], refs[n:2 * n]
        send_sems, recv_sems, local_sems = refs[2 * n:]
        x, y, c = _coords()
        sibling = (x, y, 1 - c)
        copies, locals_ = [], []
        for a in range(n):
            for xy in range(4):
                copies.append(pltpu.make_async_remote_copy(
                    src_ref=ins[a].at[2 * xy + 1 - c], dst_ref=outs[a].at[c, xy],
                    send_sem=send_sems.at[4 * a + xy], recv_sem=recv_sems.at[4 * a + xy],
                    device_id=sibling, device_id_type=MESH))
                locals_.append(pltpu.make_async_copy(ins[a].at[2 * xy + c], outs[a].at[c, xy], local_sems.at[4 * a + xy]))
        for cp in locals_ + copies:
            cp.start()
        for cp in copies:
            cp.wait()
        for cp in locals_:
            cp.wait()

    return pl.pallas_call(
        body, name=name,
        in_specs=[ANY] * n, out_specs=[ANY] * n,
        out_shape=[jax.ShapeDtypeStruct((2, 4) + g.shape[1:], g.dtype) for g in grads],
        scratch_shapes=[pltpu.SemaphoreType.DMA((4 * n,)), pltpu.SemaphoreType.DMA((4 * n,)), pltpu.SemaphoreType.DMA((4 * n,))],
    )(*grads)


def exchange_chips(name, parts):
    n = len(parts)

    def body(*refs):
        ins, outs = refs[:n], refs[n:2 * n]
        send_sems, recv_sems, local_sems = refs[2 * n:]
        x, y, c = _coords()
        mine = 2 * x + y
        chips = [(1 - x, y), (x, 1 - y), (1 - x, 1 - y)]
        copies, locals_ = [], []
        for a in range(n):
            for j, (px, py) in enumerate(chips):
                copies.append(pltpu.make_async_remote_copy(
                    src_ref=ins[a].at[2 * px + py], dst_ref=outs[a].at[mine],
                    send_sem=send_sems.at[3 * a + j], recv_sem=recv_sems.at[3 * a + j],
                    device_id=(px, py, c), device_id_type=MESH))
            locals_.append(pltpu.make_async_copy(ins[a].at[mine], outs[a].at[mine], local_sems.at[a]))
        for cp in locals_ + copies:
            cp.start()
        for cp in copies:
            cp.wait()
        for cp in locals_:
            cp.wait()

    return pl.pallas_call(
        body, name=name,
        in_specs=[ANY] * n, out_specs=[ANY] * n,
        out_shape=[jax.ShapeDtypeStruct(p.shape, p.dtype) for p in parts],
        scratch_shapes=[pltpu.SemaphoreType.DMA((3 * n,)), pltpu.SemaphoreType.DMA((3 * n,)), pltpu.SemaphoreType.DMA((n,))],
    )(*parts)


def sum_slots(name, a, out_dtype):
    s, m, r, w = a.shape

    def body(a_ref, o_ref):
        acc = a_ref[0, 0].astype(F32)
        for i in range(1, s):
            acc = acc + a_ref[i, 0].astype(F32)
        o_ref[0] = acc.astype(out_dtype)

    return pl.pallas_call(
        body, name=name, grid=(m,),
        in_specs=[pl.BlockSpec((s, 1, r, w), lambda i: (0, i, 0, 0))],
        out_specs=pl.BlockSpec((1, r, w), lambda i: (i, 0, 0)),
        out_shape=jax.ShapeDtypeStruct((m, r, w), out_dtype),
        compiler_params=_params(),
    )(a)


def adamw(name, w, g, m, v):
    c1 = np.float32(1.0 - ADAM_B1 ** ADAM_STEP)
    c2 = np.float32(1.0 - ADAM_B2 ** ADAM_STEP)

    def body(w_ref, g_ref, m_ref, v_ref, d_ref, nm_ref, nv_ref):
        g = g_ref[...]
        nm = ADAM_B1 * m_ref[...] + np.float32(1.0 - ADAM_B1) * g
        nv = ADAM_B2 * v_ref[...] + np.float32(1.0 - ADAM_B2) * (g * g)
        nm_ref[...] = nm
        nv_ref[...] = nv
        d_ref[...] = -ADAM_LR * ((nm / c1) / (jnp.sqrt(nv / c2) + ADAM_EPS) + ADAM_WD * w_ref[...])

    shape = jax.ShapeDtypeStruct(w.shape, F32)
    spec = _full_spec(w.shape)
    return pl.pallas_call(
        body, name=name, grid=(1,), in_specs=[spec] * 4, out_specs=[spec] * 3, out_shape=[shape] * 3,
        compiler_params=_params(),
    )(w, g, m, v)


def _pack_small(parts):
    flat = jnp.concatenate([parts[name].reshape(-1) for name, _ in SMALL])
    return jnp.pad(flat, (0, SMALL_ROWS * 128 - flat.shape[0])).reshape(SMALL_ROWS, 128)


def _unpack_small(packed, like):
    flat = packed.reshape(-1)
    out, at = {}, 0
    for name, size in SMALL:
        out[name] = flat[at:at + size].reshape(like[name].shape)
        at += size
    return out


def kernel(x, positions, pre_mix_norm, w_in, sgu_ln_gain, sgu_ln_bias, sgu_w_spatial, sgu_b_spatial, attn_out_norm, sgu_out_norm, w_out, post_mix_norm, pre_ffn_norm, w_gate, w_up, w_down, post_ffn_norm, loss_target, m_pre_mix_norm, m_w_in, m_sgu_ln_gain, m_sgu_ln_bias, m_sgu_w_spatial, m_sgu_b_spatial, m_attn_out_norm, m_sgu_out_norm, m_w_out, m_post_mix_norm, m_pre_ffn_norm, m_w_gate, m_w_up, m_w_down, m_post_ffn_norm, v_pre_mix_norm, v_w_in, v_sgu_ln_gain, v_sgu_ln_bias, v_sgu_w_spatial, v_sgu_b_spatial, v_attn_out_norm, v_sgu_out_norm, v_w_out, v_post_mix_norm, v_pre_ffn_norm, v_w_gate, v_w_up, v_w_down, v_post_ffn_norm):
    small_w = dict(pre_mix_norm=pre_mix_norm, sgu_ln_gain=sgu_ln_gain, sgu_ln_bias=sgu_ln_bias, sgu_w_spatial=sgu_w_spatial,
                   sgu_b_spatial=sgu_b_spatial, attn_out_norm=attn_out_norm, sgu_out_norm=sgu_out_norm,
                   post_mix_norm=post_mix_norm, pre_ffn_norm=pre_ffn_norm, post_ffn_norm=post_ffn_norm)
    small_m = dict(pre_mix_norm=m_pre_mix_norm, sgu_ln_gain=m_sgu_ln_gain, sgu_ln_bias=m_sgu_ln_bias, sgu_w_spatial=m_sgu_w_spatial,
                   sgu_b_spatial=m_sgu_b_spatial, attn_out_norm=m_attn_out_norm, sgu_out_norm=m_sgu_out_norm,
                   post_mix_norm=m_post_mix_norm, pre_ffn_norm=m_pre_ffn_norm, post_ffn_norm=m_post_ffn_norm)
    small_v = dict(pre_mix_norm=v_pre_mix_norm, sgu_ln_gain=v_sgu_ln_gain, sgu_ln_bias=v_sgu_ln_bias, sgu_w_spatial=v_sgu_w_spatial,
                   sgu_b_spatial=v_sgu_b_spatial, attn_out_norm=v_attn_out_norm, sgu_out_norm=v_sgu_out_norm,
                   post_mix_norm=v_post_mix_norm, pre_ffn_norm=v_pre_ffn_norm, post_ffn_norm=v_post_ffn_norm)

    x2d = x[0]
    target = loss_target[0]
    pos_col = positions.reshape(SEQ, 1)
    rot = _rot_consts()
    w_sp = sgu_w_spatial[0]
    bfull = jnp.repeat(sgu_b_spatial[0].T, HEAD_DIM, axis=1)

    shards = [w_in[0].T.astype(BF16), w_gate[0].T.astype(BF16), w_up[0].T.astype(BF16), w_out[0].astype(BF16),
              w_down[0].astype(BF16)]
    w_in_t, w_gate_t, w_up_t, w_out_f, w_down_f = all_gather_rows("gather_weights", shards)

    h1, q, k, v, u, vs = in_proj(x2d, pos_col, pre_mix_norm, w_in_t, rot)
    attn, lse = attn_fwd(q, k, v)
    sgu = sgu_fwd(u, vs, sgu_ln_gain, sgu_ln_bias, w_sp, bfull)
    mix, y, x2, h2 = out_proj(attn, sgu, x2d, attn_out_norm, sgu_out_norm, w_out_f, post_mix_norm, pre_ffn_norm)
    gate, up, act = ffn_up(h2, w_gate_t, w_up_t)
    df, dx3, d_post_ffn, sq_err = ffn_down_loss(act, w_down_f, x2, post_ffn_norm, target)
    loss = lax.psum(sq_err[0, 0] * np.float32(0.5 / D_MODEL), MESH_AXES)

    dgate, dup = ffn_act_bwd(df, w_down_f, gate, up)
    g_w_down = weight_grad("grad_w_down", act, df)
    dx2, dy, d_pre_ffn, d_post_mix = ffn_in_bwd(dgate, dup, w_gate_t, w_up_t, x2, pre_ffn_norm, dx3, y, post_mix_norm)
    g_w_gate_t = weight_grad("grad_w_gate", dgate, h2)
    g_w_up_t = weight_grad("grad_w_up", dup, h2)
    g_w_out = weight_grad("grad_w_out", mix, dy)
    dattn, dsgu, d_attn_out, d_sgu_out = mix_bwd(dy, w_out_f, attn, sgu, attn_out_norm, sgu_out_norm)
    du, dvs, d_ln_gain, d_ln_bias, d_w_sp, d_bfull = sgu_bwd(u, vs, dsgu, sgu_ln_gain, sgu_ln_bias, w_sp, bfull)
    dq, dk, dv = attn_bwd(q, k, v, attn, lse, dattn, pos_col, rot)
    dproj = jnp.concatenate([dq, dk, dv, du, dvs], axis=1)
    grad_x, d_pre_mix = in_bwd(dproj, w_in_t, x2d, pre_mix_norm, dx2)
    g_w_in_t = weight_grad("grad_w_in", dproj, h1)

    d_b_sp = d_bfull.reshape(CHUNK, N_GROUPS, HEAD_DIM).sum(axis=-1).T
    small_g = _pack_small(dict(pre_mix_norm=d_pre_mix, sgu_ln_gain=d_ln_gain, sgu_ln_bias=d_ln_bias, sgu_w_spatial=d_w_sp,
                               sgu_b_spatial=d_b_sp, attn_out_norm=d_attn_out, sgu_out_norm=d_sgu_out,
                               post_mix_norm=d_post_mix, pre_ffn_norm=d_pre_ffn, post_ffn_norm=d_post_ffn))

    grads = [g_w_in_t, g_w_gate_t, g_w_up_t, g_w_out, g_w_down, small_g]
    grads = [g.reshape(N_DEV, g.shape[0] // N_DEV, g.shape[1]) for g in grads]
    by_core = exchange_cores("grads_to_sibling", grads)
    by_chip = [sum_slots(f"sum_cores_{i}", p, p.dtype) for i, p in enumerate(by_core)]
    by_chip = exchange_chips("grads_to_chips", by_chip)
    reduced = [sum_slots(f"sum_chips_{i}", p[:, None], F32)[0] for i, p in enumerate(by_chip)]
    r_w_in_t, r_w_gate_t, r_w_up_t, r_w_out, r_w_down, r_small = reduced
    (all_small,) = all_gather_rows("gather_small_grads", [r_small])

    big = {}
    for name, w, g, m, vv in (("w_in", w_in, r_w_in_t.T, m_w_in, v_w_in), ("w_gate", w_gate, r_w_gate_t.T, m_w_gate, v_w_gate),
                              ("w_up", w_up, r_w_up_t.T, m_w_up, v_w_up), ("w_out", w_out, r_w_out, m_w_out, v_w_out),
                              ("w_down", w_down, r_w_down, m_w_down, v_w_down)):
        d, nm, nv = adamw("adamw_" + name, w[0], g, m[0], vv[0])
        big[name] = (g[None], d[None], nm[None], nv[None])
    sd, snm, snv = adamw("adamw_small", _pack_small(small_w), all_small, _pack_small(small_m), _pack_small(small_v))
    sg, sd, snm, snv = (_unpack_small(t, small_w) for t in (all_small, sd, snm, snv))

    names = ["pre_mix_norm", "w_in", "sgu_ln_gain", "sgu_ln_bias", "sgu_w_spatial", "sgu_b_spatial", "attn_out_norm",
             "sgu_out_norm", "w_out", "post_mix_norm", "pre_ffn_norm", "w_gate", "w_up", "w_down", "post_ffn_norm"]
    outs = [loss, grad_x[None]]
    for i, table in enumerate((sg, sd, snm, snv)):
        for name in names:
            outs.append(big[name][i] if name in big else table[name])
    return tuple(outs)
```

```python
import functools

import numpy as np
import jax
import jax.numpy as jnp
from jax import lax
from jax.experimental import pallas as pl
from jax.experimental.pallas import tpu as pltpu

F32 = jnp.float32
BF16 = jnp.bfloat16

SEQ = 2048
D_MODEL = 1024
ATTN_W = 512
SGU_W = 512
HEAD_DIM = 64
N_GROUPS = 8
CHUNK = 128
D_FF = 2816
IN_W = 3 * ATTN_W + 2 * SGU_W
DILATIONS = (1, 4, 16)
ROPE_THETA = 500000.0
ROT_DIM = 16
ROT_HALF = 8
RMS_EPS = 1e-6
LN_EPS = 1e-5
Q_SCALE = 0.125
NEG = -1e30

N_DEV = 8
MESH_AXES = ("x", "y", "c")
MESH = pl.DeviceIdType.MESH

ADAM_LR = 0.001
ADAM_B1 = 0.9
ADAM_B2 = 0.999
ADAM_EPS = 1e-08
ADAM_WD = 0.01
ADAM_STEP = 10

VMEM_LIMIT = 60 * 1024 * 1024
ANY = pl.BlockSpec(memory_space=pl.ANY)

SMALL = (("pre_mix_norm", 1024), ("sgu_ln_gain", 512), ("sgu_ln_bias", 512), ("sgu_w_spatial", 8 * 128 * 128),
         ("sgu_b_spatial", 1024), ("attn_out_norm", 512), ("sgu_out_norm", 512), ("post_mix_norm", 1024),
         ("pre_ffn_norm", 1024), ("post_ffn_norm", 1024))
SMALL_ROWS = 1152


def _params(sem=("arbitrary",)):
    return pltpu.CompilerParams(dimension_semantics=sem, vmem_limit_bytes=VMEM_LIMIT)


def _dot(a, b):
    return jnp.dot(a, b, preferred_element_type=F32)


def _dot_nt(a, b):
    return lax.dot_general(a, b, (((1,), (1,)), ((), ())), preferred_element_type=F32)


def _dot_tn(a, b):
    return lax.dot_general(a, b, (((0,), (0,)), ((), ())), preferred_element_type=F32)


def _rms(z):
    return lax.rsqrt(jnp.mean(z * z, axis=-1, keepdims=True) + RMS_EPS)


def _rms_bwd(z, gain, d):
    r = _rms(z)
    n = z * r
    dn = d * gain
    dz = r * (dn - n * jnp.mean(dn * n, axis=-1, keepdims=True))
    return dz, jnp.sum(d * n, axis=0, keepdims=True)


def _gelu(z):
    return 0.5 * z * (1.0 + lax.erf(z * np.float32(1.0 / np.sqrt(2.0))))


def _gelu_grad(z):
    cdf = 0.5 * (1.0 + lax.erf(z * np.float32(1.0 / np.sqrt(2.0))))
    return cdf + z * jnp.exp(-0.5 * z * z) * np.float32(1.0 / np.sqrt(2.0 * np.pi))


def _rot_tables(pos_col, invf, ma, mb):
    ang = pos_col.astype(F32) * invf
    s = jnp.sin(ang)
    return jnp.cos(ang), s * ma, s * mb


def _rot(t, c, sa, sb):
    return t * c + pltpu.roll(t, 120, 1) * sa + pltpu.roll(t, 8, 1) * sb


def _rot_t(d, c, sa, sb):
    return d * c + pltpu.roll(d * sa, 8, 1) + pltpu.roll(d * sb, 120, 1)


def _rot_consts():
    lane = np.arange(128) % HEAD_DIM
    inv_freq = (np.float32(ROPE_THETA) ** (-np.arange(0, ROT_DIM, 2, dtype=np.float32) / np.float32(ROT_DIM))).astype(np.float32)
    invf = np.where(lane < ROT_DIM, inv_freq[lane % ROT_HALF], 0.0).astype(np.float32)
    ma = np.where(lane < ROT_HALF, -1.0, 0.0).astype(np.float32)
    mb = np.where((lane >= ROT_HALF) & (lane < ROT_DIM), 1.0, 0.0).astype(np.float32)
    return jnp.asarray(invf[None]), jnp.asarray(ma[None]), jnp.asarray(mb[None])


def _row_spec(tm, w):
    return pl.BlockSpec((tm, w), lambda i: (i, 0))


def _full_spec(shape):
    return pl.BlockSpec(shape, lambda i: (0,) * len(shape))


def in_proj(x, pos_col, g1, w_in_t, rot):
    tm = 512

    def body(x_ref, pos_ref, g_ref, w_ref, invf_ref, ma_ref, mb_ref, h_ref, q_ref, k_ref, v_ref, u_ref, vs_ref):
        xf = x_ref[...]
        h = (xf * _rms(xf) * g_ref[...]).astype(BF16)
        h_ref[...] = h
        proj = _dot_nt(h, w_ref[...])
        c, sa, sb = _rot_tables(pos_ref[...], invf_ref[...], ma_ref[...], mb_ref[...])
        for j in range(ATTN_W // 128):
            q_ref[:, j * 128:(j + 1) * 128] = _rot(proj[:, j * 128:(j + 1) * 128], c, sa, sb) * Q_SCALE
            k_ref[:, j * 128:(j + 1) * 128] = _rot(proj[:, ATTN_W + j * 128:ATTN_W + (j + 1) * 128], c, sa, sb)
        v_ref[...] = proj[:, 2 * ATTN_W:3 * ATTN_W]
        u_ref[...] = proj[:, 3 * ATTN_W:3 * ATTN_W + SGU_W]
        vs_ref[...] = proj[:, 3 * ATTN_W + SGU_W:]

    act = jax.ShapeDtypeStruct((SEQ, 512), F32)
    return pl.pallas_call(
        body, name="in_proj", grid=(SEQ // tm,),
        in_specs=[_row_spec(tm, D_MODEL), _row_spec(tm, 1), _full_spec((1, D_MODEL)), _full_spec((IN_W, D_MODEL)),
                  _full_spec((1, 128)), _full_spec((1, 128)), _full_spec((1, 128))],
        out_specs=[_row_spec(tm, D_MODEL)] + [_row_spec(tm, 512)] * 5,
        out_shape=[jax.ShapeDtypeStruct((SEQ, D_MODEL), BF16)] + [act] * 5,
        compiler_params=_params(),
    )(x, pos_col, g1, w_in_t, *rot)


def _attn_masks():
    row2 = lax.broadcasted_iota(jnp.int32, (128, 256), 0)
    col2 = lax.broadcasted_iota(jnp.int32, (128, 256), 1)
    mask2 = jnp.logical_or(jnp.logical_and(col2 < 128, col2 >= row2), jnp.logical_and(col2 >= 128, (col2 - 128) <= row2))
    row1 = lax.broadcasted_iota(jnp.int32, (128, 128), 0)
    col1 = lax.broadcasted_iota(jnp.int32, (128, 128), 1)
    return col1 < HEAD_DIM, col1 <= row1, mask2


def _for_each_block(fn):
    for p, d in enumerate(DILATIONS):
        n_blk = SEQ // d // 128

        def first(r, carry, p=p, d=d):
            fn(p, d, r, None)
            return carry

        if d == 1:
            fn(p, d, 0, None)
        else:
            lax.fori_loop(0, d, first, 0)
        if n_blk > 1:
            def rest(i, carry, p=p, d=d):
                r = i % d
                n = 1 + i // d
                q0 = n * (128 * d) + r
                prev0 = (n - 1) * (128 * d) + r
                if d == 1:
                    q0, prev0 = pl.multiple_of(q0, 128), pl.multiple_of(prev0, 128)
                fn(p, d, q0, prev0)
                return carry

            lax.fori_loop(0, d * (n_blk - 1), rest, 0)


def attn_fwd(q, k, v):
    def body(q_ref, k_ref, v_ref, o_ref, lse_ref, op_ref, lp_ref):
        head0, mask1, mask2 = _attn_masks()

        def block(p, d, q0, prev0):
            rows = pl.ds(q0, 128, stride=d) if d > 1 else pl.ds(q0, 128)
            qb = q_ref[rows, :]
            if prev0 is None:
                kk = k_ref[rows, :].astype(BF16)
                vv = v_ref[rows, :].astype(BF16)
                mask = mask1
            else:
                prev = pl.ds(prev0, 128, stride=d) if d > 1 else pl.ds(prev0, 128)
                kk = jnp.concatenate([k_ref[prev, :], k_ref[rows, :]], axis=0).astype(BF16)
                vv = jnp.concatenate([v_ref[prev, :], v_ref[rows, :]], axis=0).astype(BF16)
                mask = mask2
            outs, lses = [], []
            for hm in (head0, jnp.logical_not(head0)):
                qm = jnp.where(hm, qb, 0.0).astype(BF16)
                s = jnp.where(mask, _dot_nt(qm, kk), NEG)
                m = jnp.max(s, axis=-1, keepdims=True)
                e = jnp.exp(s - m)
                l = jnp.sum(e, axis=-1, keepdims=True)
                outs.append(_dot(e.astype(BF16), vv) / l)
                lses.append(jnp.broadcast_to(m + jnp.log(l), (128, 128)))
            op_ref[p, rows, :] = jnp.where(head0, outs[0], outs[1])
            lp_ref[p, rows, :] = jnp.where(head0, lses[0], lses[1])

        _for_each_block(block)

        def combine(i, carry):
            rows = pl.ds(pl.multiple_of(i * 256, 256), 256)
            ls = [lp_ref[p, rows, :] for p in range(3)]
            m = jnp.maximum(jnp.maximum(ls[0], ls[1]), ls[2])
            lse = m + jnp.log(jnp.exp(ls[0] - m) + jnp.exp(ls[1] - m) + jnp.exp(ls[2] - m))
            o = jnp.zeros((256, 128), F32)
            for p in range(3):
                o = o + jnp.exp(ls[p] - lse) * op_ref[p, rows, :]
            o_ref[rows, :] = o
            lse_ref[rows, :] = lse
            return carry

        lax.fori_loop(0, SEQ // 256, combine, 0)

    slab = pl.BlockSpec((SEQ, 128), lambda i: (0, i))
    out = jax.ShapeDtypeStruct((SEQ, ATTN_W), F32)
    return pl.pallas_call(
        body, name="attn_fwd", grid=(ATTN_W // 128,),
        in_specs=[slab] * 3, out_specs=[slab] * 2, out_shape=[out, out],
        scratch_shapes=[pltpu.VMEM((3, SEQ, 128), F32), pltpu.VMEM((3, SEQ, 128), F32)],
        compiler_params=_params(),
    )(q, k, v)


def _causal_weights(w_ref):
    row = lax.broadcasted_iota(jnp.int32, (CHUNK, CHUNK), 0)
    col = lax.broadcasted_iota(jnp.int32, (CHUNK, CHUNK), 1)
    return [jnp.where(col <= row, w_ref[g], 0.0).astype(BF16) for g in range(N_GROUPS)], col <= row


def _sgu_chunk_fwd(u, vs, lg, lb, wc, bfull, head0):
    ug = _gelu(u)
    vg = _gelu(vs)
    xc = vg - jnp.mean(vg, axis=-1, keepdims=True)
    rstd = lax.rsqrt(jnp.mean(xc * xc, axis=-1, keepdims=True) + LN_EPS)
    xhat = xc * rstd
    vn = xhat * lg + lb
    mixed = []
    for gp in range(SGU_W // 128):
        vp = vn[:, gp * 128:(gp + 1) * 128].astype(BF16)
        mixed.append(jnp.where(head0, _dot(wc[2 * gp], vp), _dot(wc[2 * gp + 1], vp)))
    ms = jnp.concatenate(mixed, axis=1) + bfull
    return ug, xhat, rstd, vn, ms


def sgu_fwd(u, vs, lg, lb, w_sp, bfull):
    cpb = 4

    def body(u_ref, vs_ref, lg_ref, lb_ref, w_ref, b_ref, o_ref):
        wc, _ = _causal_weights(w_ref)
        head0 = lax.broadcasted_iota(jnp.int32, (CHUNK, 128), 1) < HEAD_DIM
        for ci in range(cpb):
            rows = pl.ds(ci * CHUNK, CHUNK)
            ug, _, _, _, ms = _sgu_chunk_fwd(u_ref[rows, :], vs_ref[rows, :], lg_ref[...], lb_ref[...], wc, b_ref[...], head0)
            o_ref[rows, :] = ug * ms

    tm = cpb * CHUNK
    return pl.pallas_call(
        body, name="sgu_fwd", grid=(SEQ // tm,),
        in_specs=[_row_spec(tm, SGU_W), _row_spec(tm, SGU_W), _full_spec((1, SGU_W)), _full_spec((1, SGU_W)),
                  _full_spec((N_GROUPS, CHUNK, CHUNK)), _full_spec((CHUNK, SGU_W))],
        out_specs=_row_spec(tm, SGU_W), out_shape=jax.ShapeDtypeStruct((SEQ, SGU_W), F32),
        compiler_params=_params(),
    )(u, vs, lg, lb, w_sp, bfull)


def out_proj(attn, sgu, x, ga, gs, w_out, gpm, gpf):
    tm = 512

    def body(a_ref, s_ref, x_ref, ga_ref, gs_ref, w_ref, gpm_ref, gpf_ref, mix_ref, y_ref, x2_ref, h2_ref):
        a = a_ref[...]
        s = s_ref[...]
        an = (a * _rms(a) * ga_ref[...]).astype(BF16)
        sn = (s * _rms(s) * gs_ref[...]).astype(BF16)
        mix_ref[:, :ATTN_W] = an
        mix_ref[:, ATTN_W:] = sn
        y = _dot(an, w_ref[:ATTN_W, :]) + _dot(sn, w_ref[ATTN_W:, :])
        y_ref[...] = y
        x2 = x_ref[...] + y * _rms(y) * gpm_ref[...]
        x2_ref[...] = x2
        h2_ref[...] = (x2 * _rms(x2) * gpf_ref[...]).astype(BF16)

    wide = jax.ShapeDtypeStruct((SEQ, D_MODEL), F32)
    wide16 = jax.ShapeDtypeStruct((SEQ, D_MODEL), BF16)
    return pl.pallas_call(
        body, name="out_proj", grid=(SEQ // tm,),
        in_specs=[_row_spec(tm, ATTN_W), _row_spec(tm, SGU_W), _row_spec(tm, D_MODEL), _full_spec((1, ATTN_W)),
                  _full_spec((1, SGU_W)), _full_spec((D_MODEL, D_MODEL)), _full_spec((1, D_MODEL)), _full_spec((1, D_MODEL))],
        out_specs=[_row_spec(tm, D_MODEL)] * 4, out_shape=[wide16, wide, wide, wide16],
        compiler_params=_params(),
    )(attn, sgu, x, ga, gs, w_out, gpm, gpf)


def ffn_up(h2, w_gate_t, w_up_t):
    tm = 256

    def body(h_ref, wg_ref, wu_ref, g_ref, u_ref, a_ref):
        h = h_ref[...]
        g = _dot_nt(h, wg_ref[...])
        u = _dot_nt(h, wu_ref[...])
        g_ref[...] = g
        u_ref[...] = u
        a_ref[...] = (g * jax.nn.sigmoid(g) * u).astype(BF16)

    ff = jax.ShapeDtypeStruct((SEQ, D_FF), F32)
    return pl.pallas_call(
        body, name="ffn_up", grid=(SEQ // tm,),
        in_specs=[_row_spec(tm, D_MODEL), _full_spec((D_FF, D_MODEL)), _full_spec((D_FF, D_MODEL))],
        out_specs=[_row_spec(tm, D_FF)] * 3, out_shape=[ff, ff, jax.ShapeDtypeStruct((SEQ, D_FF), BF16)],
        compiler_params=_params(),
    )(h2, w_gate_t, w_up_t)


def ffn_down_loss(act, w_down, x2, gpo, target):
    tm = 512

    def body(a_ref, w_ref, x2_ref, g_ref, t_ref, df_ref, dx3_ref, dg_ref, loss_ref):
        f = _dot(a_ref[...], w_ref[...])
        gain = g_ref[...]
        err = x2_ref[...] + f * _rms(f) * gain - t_ref[...]
        dx3 = err * np.float32(1.0 / D_MODEL)
        dx3_ref[...] = dx3
        df, dg = _rms_bwd(f, gain, dx3)
        df_ref[...] = df.astype(BF16)

        @pl.when(pl.program_id(0) == 0)
        def _():
            dg_ref[...] = jnp.zeros_like(dg_ref)
            loss_ref[...] = jnp.zeros_like(loss_ref)

        dg_ref[...] += dg
        loss_ref[...] += jnp.sum(err * err, axis=(0, 1), keepdims=True)

    return pl.pallas_call(
        body, name="ffn_down_loss", grid=(SEQ // tm,),
        in_specs=[_row_spec(tm, D_FF), _full_spec((D_FF, D_MODEL)), _row_spec(tm, D_MODEL), _full_spec((1, D_MODEL)),
                  _row_spec(tm, D_MODEL)],
        out_specs=[_row_spec(tm, D_MODEL), _row_spec(tm, D_MODEL), _full_spec((1, D_MODEL)), _full_spec((1, 1))],
        out_shape=[jax.ShapeDtypeStruct((SEQ, D_MODEL), BF16), jax.ShapeDtypeStruct((SEQ, D_MODEL), F32),
                   jax.ShapeDtypeStruct((1, D_MODEL), F32), jax.ShapeDtypeStruct((1, 1), F32)],
        compiler_params=_params(),
    )(act, w_down, x2, gpo, target)


def ffn_act_bwd(df, w_down, gate, up):
    tm = 256

    def body(df_ref, w_ref, g_ref, u_ref, dg_ref, du_ref):
        dact = _dot_nt(df_ref[...], w_ref[...])
        g = g_ref[...]
        s = jax.nn.sigmoid(g)
        du_ref[...] = (dact * g * s).astype(BF16)
        dg_ref[...] = (dact * u_ref[...] * (s * (1.0 + g * (1.0 - s)))).astype(BF16)

    ff16 = jax.ShapeDtypeStruct((SEQ, D_FF), BF16)
    return pl.pallas_call(
        body, name="ffn_act_bwd", grid=(SEQ // tm,),
        in_specs=[_row_spec(tm, D_MODEL), _full_spec((D_FF, D_MODEL)), _row_spec(tm, D_FF), _row_spec(tm, D_FF)],
        out_specs=[_row_spec(tm, D_FF)] * 2, out_shape=[ff16, ff16],
        compiler_params=_params(),
    )(df, w_down, gate, up)


def ffn_in_bwd(dgate, dup, w_gate_t, w_up_t, x2, gpf, dx3, y, gpm):
    tm = 256

    def body(dg_ref, du_ref, wg_ref, wu_ref, x2_ref, gpf_ref, dx3_ref, y_ref, gpm_ref, dx2_ref, dy_ref, dgpf_ref, dgpm_ref):
        dh2 = _dot(dg_ref[...], wg_ref[...]) + _dot(du_ref[...], wu_ref[...])
        dz, dgpf = _rms_bwd(x2_ref[...], gpf_ref[...], dh2)
        dx2 = dx3_ref[...] + dz
        dx2_ref[...] = dx2
        dy, dgpm = _rms_bwd(y_ref[...], gpm_ref[...], dx2)
        dy_ref[...] = dy.astype(BF16)

        @pl.when(pl.program_id(0) == 0)
        def _():
            dgpf_ref[...] = jnp.zeros_like(dgpf_ref)
            dgpm_ref[...] = jnp.zeros_like(dgpm_ref)

        dgpf_ref[...] += dgpf
        dgpm_ref[...] += dgpm

    vec = jax.ShapeDtypeStruct((1, D_MODEL), F32)
    return pl.pallas_call(
        body, name="ffn_in_bwd", grid=(SEQ // tm,),
        in_specs=[_row_spec(tm, D_FF), _row_spec(tm, D_FF), _full_spec((D_FF, D_MODEL)), _full_spec((D_FF, D_MODEL)),
                  _row_spec(tm, D_MODEL), _full_spec((1, D_MODEL)), _row_spec(tm, D_MODEL), _row_spec(tm, D_MODEL),
                  _full_spec((1, D_MODEL))],
        out_specs=[_row_spec(tm, D_MODEL), _row_spec(tm, D_MODEL), _full_spec((1, D_MODEL)), _full_spec((1, D_MODEL))],
        out_shape=[jax.ShapeDtypeStruct((SEQ, D_MODEL), F32), jax.ShapeDtypeStruct((SEQ, D_MODEL), BF16), vec, vec],
        compiler_params=_params(),
    )(dgate, dup, w_gate_t, w_up_t, x2, gpf, dx3, y, gpm)


def weight_grad(name, a, b):
    m, n = a.shape[1], b.shape[1]
    tr = 256

    def body(a_ref, b_ref, o_ref):
        o_ref[...] = _dot_tn(a_ref[...], b_ref[...]).astype(BF16)

    return pl.pallas_call(
        body, name=name, grid=(m // tr,),
        in_specs=[pl.BlockSpec((SEQ, tr), lambda i: (0, i)), _full_spec((SEQ, n))],
        out_specs=_row_spec(tr, n), out_shape=jax.ShapeDtypeStruct((m, n), BF16),
        compiler_params=_params(),
    )(a, b)


def mix_bwd(dy, w_out, attn, sgu, ga, gs):
    tm = 512

    def body(dy_ref, w_ref, a_ref, s_ref, ga_ref, gs_ref, da_ref, ds_ref, dga_ref, dgs_ref):
        dy = dy_ref[...]
        da, dga = _rms_bwd(a_ref[...], ga_ref[...], _dot_nt(dy, w_ref[:ATTN_W, :]))
        ds, dgs = _rms_bwd(s_ref[...], gs_ref[...], _dot_nt(dy, w_ref[ATTN_W:, :]))
        da_ref[...] = da
        ds_ref[...] = ds

        @pl.when(pl.program_id(0) == 0)
        def _():
            dga_ref[...] = jnp.zeros_like(dga_ref)
            dgs_ref[...] = jnp.zeros_like(dgs_ref)

        dga_ref[...] += dga
        dgs_ref[...] += dgs

    half = jax.ShapeDtypeStruct((SEQ, 512), F32)
    vec = jax.ShapeDtypeStruct((1, 512), F32)
    return pl.pallas_call(
        body, name="mix_bwd", grid=(SEQ // tm,),
        in_specs=[_row_spec(tm, D_MODEL), _full_spec((D_MODEL, D_MODEL)), _row_spec(tm, 512), _row_spec(tm, 512),
                  _full_spec((1, 512)), _full_spec((1, 512))],
        out_specs=[_row_spec(tm, 512), _row_spec(tm, 512), _full_spec((1, 512)), _full_spec((1, 512))],
        out_shape=[half, half, vec, vec],
        compiler_params=_params(),
    )(dy, w_out, attn, sgu, ga, gs)


def sgu_bwd(u, vs, dsgu, lg, lb, w_sp, bfull):
    cpb = 4

    def body(u_ref, vs_ref, d_ref, lg_ref, lb_ref, w_ref, b_ref, du_ref, dvs_ref, dlg_ref, dlb_ref, dw_ref, db_ref):
        wc, causal = _causal_weights(w_ref)
        head0 = lax.broadcasted_iota(jnp.int32, (CHUNK, 128), 1) < HEAD_DIM
        lg = lg_ref[...]

        @pl.when(pl.program_id(0) == 0)
        def _():
            dlg_ref[...] = jnp.zeros_like(dlg_ref)
            dlb_ref[...] = jnp.zeros_like(dlb_ref)
            dw_ref[...] = jnp.zeros_like(dw_ref)
            db_ref[...] = jnp.zeros_like(db_ref)

        for ci in range(cpb):
            rows = pl.ds(ci * CHUNK, CHUNK)
            u = u_ref[rows, :]
            vs = vs_ref[rows, :]
            d = d_ref[rows, :]
            ug, xhat, rstd, vn, ms = _sgu_chunk_fwd(u, vs, lg, lb_ref[...], wc, b_ref[...], head0)
            du_ref[rows, :] = (d * ms * _gelu_grad(u)).astype(BF16)
            dms = d * ug
            db_ref[...] += dms
            dvn = []
            for gp in range(SGU_W // 128):
                dmp = dms[:, gp * 128:(gp + 1) * 128]
                dm0 = jnp.where(head0, dmp, 0.0).astype(BF16)
                dm1 = jnp.where(head0, 0.0, dmp).astype(BF16)
                vp = vn[:, gp * 128:(gp + 1) * 128].astype(BF16)
                dw_ref[2 * gp] += _dot_nt(dm0, vp)
                dw_ref[2 * gp + 1] += _dot_nt(dm1, vp)
                dvn.append(_dot_tn(wc[2 * gp], dm0) + _dot_tn(wc[2 * gp + 1], dm1))
            dvn = jnp.concatenate(dvn, axis=1)
            dlg_ref[...] += jnp.sum(dvn * xhat, axis=0, keepdims=True)
            dlb_ref[...] += jnp.sum(dvn, axis=0, keepdims=True)
            dxh = dvn * lg
            dvg = rstd * (dxh - jnp.mean(dxh, axis=-1, keepdims=True) - xhat * jnp.mean(dxh * xhat, axis=-1, keepdims=True))
            dvs_ref[rows, :] = (dvg * _gelu_grad(vs)).astype(BF16)

        @pl.when(pl.program_id(0) == pl.num_programs(0) - 1)
        def _():
            for g in range(N_GROUPS):
                dw_ref[g] = jnp.where(causal, dw_ref[g], 0.0)

    tm = cpb * CHUNK
    half16 = jax.ShapeDtypeStruct((SEQ, SGU_W), BF16)
    vec = jax.ShapeDtypeStruct((1, SGU_W), F32)
    return pl.pallas_call(
        body, name="sgu_bwd", grid=(SEQ // tm,),
        in_specs=[_row_spec(tm, SGU_W)] * 3 + [_full_spec((1, SGU_W)), _full_spec((1, SGU_W)),
                                                  _full_spec((N_GROUPS, CHUNK, CHUNK)), _full_spec((CHUNK, SGU_W))],
        out_specs=[_row_spec(tm, SGU_W), _row_spec(tm, SGU_W), _full_spec((1, SGU_W)), _full_spec((1, SGU_W)),
                   _full_spec((N_GROUPS, CHUNK, CHUNK)), _full_spec((CHUNK, SGU_W))],
        out_shape=[half16, half16, vec, vec, jax.ShapeDtypeStruct((N_GROUPS, CHUNK, CHUNK), F32),
                   jax.ShapeDtypeStruct((CHUNK, SGU_W), F32)],
        compiler_params=_params(),
    )(u, vs, dsgu, lg, lb, w_sp, bfull)


def attn_bwd(q, k, v, o, lse, do, pos_col, rot):
    def body(q_ref, k_ref, v_ref, o_ref, lse_ref, do_ref, pos_ref, invf_ref, ma_ref, mb_ref,
             dq_ref, dk_ref, dv_ref, dqa_ref, dka_ref, dva_ref, dlt_ref):
        head0, mask1, mask2 = _attn_masks()
        dqa_ref[...] = jnp.zeros_like(dqa_ref)
        dka_ref[...] = jnp.zeros_like(dka_ref)
        dva_ref[...] = jnp.zeros_like(dva_ref)

        def delta(i, carry):
            rows = pl.ds(pl.multiple_of(i * 256, 256), 256)
            prod = do_ref[rows, :] * o_ref[rows, :]
            h0 = lax.broadcasted_iota(jnp.int32, (256, 128), 1) < HEAD_DIM
            d0 = jnp.sum(jnp.where(h0, prod, 0.0), axis=-1, keepdims=True)
            d1 = jnp.sum(jnp.where(h0, 0.0, prod), axis=-1, keepdims=True)
            dlt_ref[rows, :] = jnp.where(h0, d0, d1)
            return carry

        lax.fori_loop(0, SEQ // 256, delta, 0)

        def block(p, d, q0, prev0):
            rows = pl.ds(q0, 128, stride=d) if d > 1 else pl.ds(q0, 128)
            qb = q_ref[rows, :]
            dob = do_ref[rows, :]
            lse_b = lse_ref[rows, :]
            dlt_b = dlt_ref[rows, :]
            if prev0 is None:
                prev = None
                kk = k_ref[rows, :].astype(BF16)
                vv = v_ref[rows, :].astype(BF16)
                mask = mask1
            else:
                prev = pl.ds(prev0, 128, stride=d) if d > 1 else pl.ds(prev0, 128)
                kk = jnp.concatenate([k_ref[prev, :], k_ref[rows, :]], axis=0).astype(BF16)
                vv = jnp.concatenate([v_ref[prev, :], v_ref[rows, :]], axis=0).astype(BF16)
                mask = mask2
            dq_b = jnp.zeros((128, 128), F32)
            dk_b = jnp.zeros(kk.shape, F32)
            dv_b = jnp.zeros(kk.shape, F32)
            for h, hm in enumerate((head0, jnp.logical_not(head0))):
                lane0 = h * HEAD_DIM
                qm = jnp.where(hm, qb, 0.0).astype(BF16)
                dom = jnp.where(hm, dob, 0.0).astype(BF16)
                s = _dot_nt(qm, kk)
                pr = jnp.where(mask, jnp.exp(s - lse_b[:, lane0:lane0 + 1]), 0.0)
                dp = _dot_nt(dom, vv)
                ds = (pr * (dp - dlt_b[:, lane0:lane0 + 1])).astype(BF16)
                dv_b = dv_b + _dot_tn(pr.astype(BF16), dom)
                dk_b = dk_b + _dot_tn(ds, qm)
                dq_b = dq_b + jnp.where(hm, _dot(ds, kk), 0.0)
            dqa_ref[rows, :] += dq_b
            if prev is None:
                dka_ref[rows, :] += dk_b
                dva_ref[rows, :] += dv_b
            else:
                dka_ref[prev, :] += dk_b[:128]
                dva_ref[prev, :] += dv_b[:128]
                dka_ref[rows, :] += dk_b[128:]
                dva_ref[rows, :] += dv_b[128:]

        _for_each_block(block)

        def finish(i, carry):
            rows = pl.ds(pl.multiple_of(i * 256, 256), 256)
            c, sa, sb = _rot_tables(pos_ref[rows, :], invf_ref[...], ma_ref[...], mb_ref[...])
            dq_ref[rows, :] = _rot_t(dqa_ref[rows, :] * Q_SCALE, c, sa, sb).astype(BF16)
            dk_ref[rows, :] = _rot_t(dka_ref[rows, :], c, sa, sb).astype(BF16)
            dv_ref[rows, :] = dva_ref[rows, :].astype(BF16)
            return carry

        lax.fori_loop(0, SEQ // 256, finish, 0)

    slab = pl.BlockSpec((SEQ, 128), lambda i: (0, i))
    out = jax.ShapeDtypeStruct((SEQ, ATTN_W), BF16)
    acc = pltpu.VMEM((SEQ, 128), F32)
    return pl.pallas_call(
        body, name="attn_bwd", grid=(ATTN_W // 128,),
        in_specs=[slab] * 6 + [_full_spec((SEQ, 1)), _full_spec((1, 128)), _full_spec((1, 128)), _full_spec((1, 128))],
        out_specs=[slab] * 3, out_shape=[out, out, out],
        scratch_shapes=[acc, acc, acc, acc],
        compiler_params=_params(),
    )(q, k, v, o, lse, do, pos_col, *rot)


def in_bwd(dproj, w_in_t, x, g1, dx2):
    tm = 512

    def body(dp_ref, w_ref, x_ref, g_ref, dx2_ref, dx_ref, dg_ref):
        dh1 = _dot(dp_ref[...], w_ref[...])
        dz, dg = _rms_bwd(x_ref[...], g_ref[...], dh1)
        dx_ref[...] = dx2_ref[...] + dz

        @pl.when(pl.program_id(0) == 0)
        def _():
            dg_ref[...] = jnp.zeros_like(dg_ref)

        dg_ref[...] += dg

    return pl.pallas_call(
        body, name="in_bwd", grid=(SEQ // tm,),
        in_specs=[_row_spec(tm, IN_W), _full_spec((IN_W, D_MODEL)), _row_spec(tm, D_MODEL), _full_spec((1, D_MODEL)),
                  _row_spec(tm, D_MODEL)],
        out_specs=[_row_spec(tm, D_MODEL), _full_spec((1, D_MODEL))],
        out_shape=[jax.ShapeDtypeStruct((SEQ, D_MODEL), F32), jax.ShapeDtypeStruct((1, D_MODEL), F32)],
        compiler_params=_params(),
    )(dproj, w_in_t, x, g1, dx2)


def _coords():
    return lax.axis_index("x"), lax.axis_index("y"), lax.axis_index("c")


def all_gather_rows(name, shards):
    n = len(shards)
    nrows = [s.shape[0] for s in shards]

    def body(*refs):
        ins, outs = refs[:n], refs[n:2 * n]
        send_sems, recv_sems, local_sems = refs[2 * n:]
        x, y, c = _coords()
        me, sibling = (x, y, c), (x, y, 1 - c)
        chips = [(1 - x, y), (x, 1 - y), (1 - x, 1 - y)]

        def rows(a, px, py, pc):
            return outs[a].at[pl.ds((4 * px + 2 * py + pc) * nrows[a], nrows[a]), :]

        def copy(a, k, block, to, src=None):
            return pltpu.make_async_remote_copy(
                src_ref=rows(a, *block) if src is None else src, dst_ref=rows(a, *block),
                send_sem=send_sems.at[7 * a + k], recv_sem=recv_sems.at[7 * a + k], device_id=to, device_id_type=MESH)

        mine = [pltpu.make_async_copy(ins[a], rows(a, *me), local_sems.at[a]) for a in range(n)]
        for cp in mine:
            cp.start()
        first = []
        for a in range(n):
            first.append(copy(a, 0, me, sibling, src=ins[a]))
            first += [copy(a, 1 + j, me, (*chip, c), src=ins[a]) for j, chip in enumerate(chips)]
        for cp in first:
            cp.start()
        passed = []
        for j, chip in enumerate(chips):
            for a in range(n):
                copy(a, 1 + j, (*chip, c), me).wait_recv()
                fwd = copy(a, 4 + j, (*chip, c), sibling)
                fwd.start()
                passed.append(fwd)
        for a in range(n):
            copy(a, 0, sibling, me).wait_recv()
            for j, chip in enumerate(chips):
                copy(a, 4 + j, (*chip, 1 - c), me).wait_recv()
        for cp in first + passed:
            cp.wait_send()
        for cp in mine:
            cp.wait()

    return pl.pallas_call(
        body, name=name,
        in_specs=[ANY] * n, out_specs=[ANY] * n,
        out_shape=[jax.ShapeDtypeStruct((N_DEV * s.shape[0], s.shape[1]), s.dtype) for s in shards],
        scratch_shapes=[pltpu.SemaphoreType.DMA((7 * n,)), pltpu.SemaphoreType.DMA((7 * n,)), pltpu.SemaphoreType.DMA((n,))],
    )(*shards)


def exchange_cores(name, grads):
    n = len(grads)

    def body(*refs):
        ins, outs = refs[:n], refs[n:2 * n]
        send_sems, recv_sems = refs[2 * n:]
        x, y, c = _coords()
        copies = []
        for a in range(n):
            for xy in range(4):
                copies.append(pltpu.make_async_remote_copy(
                    src_ref=ins[a].at[2 * xy + 1 - c], dst_ref=outs[a].at[xy],
                    send_sem=send_sems.at[4 * a + xy], recv_sem=recv_sems.at[4 * a + xy],
                    device_id=(x, y, 1 - c), device_id_type=MESH))
        for cp in copies:
            cp.start()
        for cp in copies:
            cp.wait()

    return pl.pallas_call(
        body, name=name,
        in_specs=[ANY] * n, out_specs=[ANY] * n,
        out_shape=[jax.ShapeDtypeStruct((4,) + g.shape[1:], g.dtype) for g in grads],
        scratch_shapes=[pltpu.SemaphoreType.DMA((4 * n,)), pltpu.SemaphoreType.DMA((4 * n,))],
    )(*grads)


def sum_cores(name, grad, other, core):
    _, r, w = other.shape

    def body(core_ref, g_ref, o_ref, out_ref):
        out_ref[...] = (g_ref[...].astype(F32) + o_ref[...].astype(F32)).astype(out_ref.dtype)

    return pl.pallas_call(
        body, name=name,
        grid_spec=pltpu.PrefetchScalarGridSpec(
            num_scalar_prefetch=1, grid=(4,),
            in_specs=[pl.BlockSpec((1, r, w), lambda i, core_ref: (2 * i + core_ref[0], 0, 0)),
                      pl.BlockSpec((1, r, w), lambda i, core_ref: (i, 0, 0))],
            out_specs=pl.BlockSpec((1, r, w), lambda i, core_ref: (i, 0, 0))),
        out_shape=jax.ShapeDtypeStruct(other.shape, other.dtype),
        compiler_params=_params(),
    )(core, grad, other)


def exchange_chips(name, parts):
    n = len(parts)

    def body(*refs):
        ins, outs = refs[:n], refs[n:2 * n]
        send_sems, recv_sems = refs[2 * n:]
        x, y, c = _coords()
        mine = 2 * x + y
        chips = [(1 - x, y), (x, 1 - y), (1 - x, 1 - y)]
        copies = []
        for a in range(n):
            for j, (px, py) in enumerate(chips):
                copies.append(pltpu.make_async_remote_copy(
                    src_ref=ins[a].at[2 * px + py], dst_ref=outs[a].at[mine],
                    send_sem=send_sems.at[3 * a + j], recv_sem=recv_sems.at[3 * a + j],
                    device_id=(px, py, c), device_id_type=MESH))
        for cp in copies:
            cp.start()
        for cp in copies:
            cp.wait()

    return pl.pallas_call(
        body, name=name,
        in_specs=[ANY] * n, out_specs=[ANY] * n,
        out_shape=[jax.ShapeDtypeStruct(p.shape, p.dtype) for p in parts],
        scratch_shapes=[pltpu.SemaphoreType.DMA((3 * n,)), pltpu.SemaphoreType.DMA((3 * n,))],
    )(*parts)


def sum_chips(name, part, others, chip_ids):
    _, r, w = part.shape

    def body(ids_ref, p_ref, a_ref, b_ref, c_ref, out_ref):
        acc = p_ref[0].astype(F32) + a_ref[0].astype(F32)
        out_ref[...] = (acc + b_ref[0].astype(F32)) + c_ref[0].astype(F32)

    def pick(k):
        return pl.BlockSpec((1, r, w), lambda i, ids_ref: (ids_ref[k], 0, 0))

    return pl.pallas_call(
        body, name=name,
        grid_spec=pltpu.PrefetchScalarGridSpec(
            num_scalar_prefetch=1, grid=(1,),
            in_specs=[pick(0), pick(1), pick(2), pick(3)],
            out_specs=pl.BlockSpec((r, w), lambda i, ids_ref: (0, 0))),
        out_shape=jax.ShapeDtypeStruct((r, w), F32),
        compiler_params=_params(),
    )(chip_ids, part, others, others, others)


def adamw(name, w, g, m, v):
    c1 = np.float32(1.0 - ADAM_B1 ** ADAM_STEP)
    c2 = np.float32(1.0 - ADAM_B2 ** ADAM_STEP)

    def body(w_ref, g_ref, m_ref, v_ref, d_ref, nm_ref, nv_ref):
        g = g_ref[...]
        nm = ADAM_B1 * m_ref[...] + np.float32(1.0 - ADAM_B1) * g
        nv = ADAM_B2 * v_ref[...] + np.float32(1.0 - ADAM_B2) * (g * g)
        nm_ref[...] = nm
        nv_ref[...] = nv
        d_ref[...] = -ADAM_LR * ((nm / c1) / (jnp.sqrt(nv / c2) + ADAM_EPS) + ADAM_WD * w_ref[...])

    shape = jax.ShapeDtypeStruct(w.shape, F32)
    spec = _full_spec(w.shape)
    return pl.pallas_call(
        body, name=name, grid=(1,), in_specs=[spec] * 4, out_specs=[spec] * 3, out_shape=[shape] * 3,
        compiler_params=_params(),
    )(w, g, m, v)


def _pack_small(parts):
    flat = jnp.concatenate([parts[name].reshape(-1) for name, _ in SMALL])
    return jnp.pad(flat, (0, SMALL_ROWS * 128 - flat.shape[0])).reshape(SMALL_ROWS, 128)


def _unpack_small(packed, like):
    flat = packed.reshape(-1)
    out, at = {}, 0
    for name, size in SMALL:
        out[name] = flat[at:at + size].reshape(like[name].shape)
        at += size
    return out


def kernel(x, positions, pre_mix_norm, w_in, sgu_ln_gain, sgu_ln_bias, sgu_w_spatial, sgu_b_spatial, attn_out_norm, sgu_out_norm, w_out, post_mix_norm, pre_ffn_norm, w_gate, w_up, w_down, post_ffn_norm, loss_target, m_pre_mix_norm, m_w_in, m_sgu_ln_gain, m_sgu_ln_bias, m_sgu_w_spatial, m_sgu_b_spatial, m_attn_out_norm, m_sgu_out_norm, m_w_out, m_post_mix_norm, m_pre_ffn_norm, m_w_gate, m_w_up, m_w_down, m_post_ffn_norm, v_pre_mix_norm, v_w_in, v_sgu_ln_gain, v_sgu_ln_bias, v_sgu_w_spatial, v_sgu_b_spatial, v_attn_out_norm, v_sgu_out_norm, v_w_out, v_post_mix_norm, v_pre_ffn_norm, v_w_gate, v_w_up, v_w_down, v_post_ffn_norm):
    small_w = dict(pre_mix_norm=pre_mix_norm, sgu_ln_gain=sgu_ln_gain, sgu_ln_bias=sgu_ln_bias, sgu_w_spatial=sgu_w_spatial,
                   sgu_b_spatial=sgu_b_spatial, attn_out_norm=attn_out_norm, sgu_out_norm=sgu_out_norm,
                   post_mix_norm=post_mix_norm, pre_ffn_norm=pre_ffn_norm, post_ffn_norm=post_ffn_norm)
    small_m = dict(pre_mix_norm=m_pre_mix_norm, sgu_ln_gain=m_sgu_ln_gain, sgu_ln_bias=m_sgu_ln_bias, sgu_w_spatial=m_sgu_w_spatial,
                   sgu_b_spatial=m_sgu_b_spatial, attn_out_norm=m_attn_out_norm, sgu_out_norm=m_sgu_out_norm,
                   post_mix_norm=m_post_mix_norm, pre_ffn_norm=m_pre_ffn_norm, post_ffn_norm=m_post_ffn_norm)
    small_v = dict(pre_mix_norm=v_pre_mix_norm, sgu_ln_gain=v_sgu_ln_gain, sgu_ln_bias=v_sgu_ln_bias, sgu_w_spatial=v_sgu_w_spatial,
                   sgu_b_spatial=v_sgu_b_spatial, attn_out_norm=v_attn_out_norm, sgu_out_norm=v_sgu_out_norm,
                   post_mix_norm=v_post_mix_norm, pre_ffn_norm=v_pre_ffn_norm, post_ffn_norm=v_post_ffn_norm)

    x2d = x[0]
    target = loss_target[0]
    pos_col = positions.reshape(SEQ, 1)
    rot = _rot_consts()
    w_sp = sgu_w_spatial[0]
    bfull = jnp.repeat(sgu_b_spatial[0].T, HEAD_DIM, axis=1)

    shards = [w_in[0].T.astype(BF16), w_gate[0].T.astype(BF16), w_up[0].T.astype(BF16), w_out[0].astype(BF16),
              w_down[0].astype(BF16)]
    w_in_t, w_gate_t, w_up_t, w_out_f, w_down_f = all_gather_rows("gather_weights", shards)

    h1, q, k, v, u, vs = in_proj(x2d, pos_col, pre_mix_norm, w_in_t, rot)
    attn, lse = attn_fwd(q, k, v)
    sgu = sgu_fwd(u, vs, sgu_ln_gain, sgu_ln_bias, w_sp, bfull)
    mix, y, x2, h2 = out_proj(attn, sgu, x2d, attn_out_norm, sgu_out_norm, w_out_f, post_mix_norm, pre_ffn_norm)
    gate, up, act = ffn_up(h2, w_gate_t, w_up_t)
    df, dx3, d_post_ffn, sq_err = ffn_down_loss(act, w_down_f, x2, post_ffn_norm, target)
    loss = lax.psum(sq_err[0, 0] * np.float32(0.5 / D_MODEL), MESH_AXES)

    dgate, dup = ffn_act_bwd(df, w_down_f, gate, up)
    g_w_down = weight_grad("grad_w_down", act, df)
    dx2, dy, d_pre_ffn, d_post_mix = ffn_in_bwd(dgate, dup, w_gate_t, w_up_t, x2, pre_ffn_norm, dx3, y, post_mix_norm)
    g_w_gate_t = weight_grad("grad_w_gate", dgate, h2)
    g_w_up_t = weight_grad("grad_w_up", dup, h2)
    g_w_out = weight_grad("grad_w_out", mix, dy)
    dattn, dsgu, d_attn_out, d_sgu_out = mix_bwd(dy, w_out_f, attn, sgu, attn_out_norm, sgu_out_norm)
    du, dvs, d_ln_gain, d_ln_bias, d_w_sp, d_bfull = sgu_bwd(u, vs, dsgu, sgu_ln_gain, sgu_ln_bias, w_sp, bfull)
    dq, dk, dv = attn_bwd(q, k, v, attn, lse, dattn, pos_col, rot)
    dproj = jnp.concatenate([dq, dk, dv, du, dvs], axis=1)
    grad_x, d_pre_mix = in_bwd(dproj, w_in_t, x2d, pre_mix_norm, dx2)
    g_w_in_t = weight_grad("grad_w_in", dproj, h1)

    d_b_sp = d_bfull.reshape(CHUNK, N_GROUPS, HEAD_DIM).sum(axis=-1).T
    small_g = _pack_small(dict(pre_mix_norm=d_pre_mix, sgu_ln_gain=d_ln_gain, sgu_ln_bias=d_ln_bias, sgu_w_spatial=d_w_sp,
                               sgu_b_spatial=d_b_sp, attn_out_norm=d_attn_out, sgu_out_norm=d_sgu_out,
                               post_mix_norm=d_post_mix, pre_ffn_norm=d_pre_ffn, post_ffn_norm=d_post_ffn))

    grads = [g_w_in_t, g_w_gate_t, g_w_up_t, g_w_out, g_w_down, small_g]
    grads = [g.reshape(N_DEV, g.shape[0] // N_DEV, g.shape[1]) for g in grads]
    core = lax.axis_index("c").astype(jnp.int32).reshape(1)
    chip = (2 * lax.axis_index("x") + lax.axis_index("y")).astype(jnp.int32)
    chip_ids = jnp.stack([chip, chip ^ 1, chip ^ 2, chip ^ 3])
    from_sibling = exchange_cores("grads_to_sibling", grads)
    by_chip = [sum_cores(f"sum_cores_{i}", g, o, core) for i, (g, o) in enumerate(zip(grads, from_sibling))]
    from_chips = exchange_chips("grads_to_chips", by_chip)
    reduced = [sum_chips(f"sum_chips_{i}", p, o, chip_ids) for i, (p, o) in enumerate(zip(by_chip, from_chips))]
    r_w_in_t, r_w_gate_t, r_w_up_t, r_w_out, r_w_down, r_small = reduced
    (all_small,) = all_gather_rows("gather_small_grads", [r_small])

    big = {}
    for name, w, g, m, vv in (("w_in", w_in, r_w_in_t.T, m_w_in, v_w_in), ("w_gate", w_gate, r_w_gate_t.T, m_w_gate, v_w_gate),
                              ("w_up", w_up, r_w_up_t.T, m_w_up, v_w_up), ("w_out", w_out, r_w_out, m_w_out, v_w_out),
                              ("w_down", w_down, r_w_down, m_w_down, v_w_down)):
        d, nm, nv = adamw("adamw_" + name, w[0], g, m[0], vv[0])
        big[name] = (g[None], d[None], nm[None], nv[None])
    sd, snm, snv = adamw("adamw_small", _pack_small(small_w), all_small, _pack_small(small_m), _pack_small(small_v))
    sg, sd, snm, snv = (_unpack_small(t, small_w) for t in (all_small, sd, snm, snv))

    names = ["pre_mix_norm", "w_in", "sgu_ln_gain", "sgu_ln_bias", "sgu_w_spatial", "sgu_b_spatial", "attn_out_norm",
             "sgu_out_norm", "w_out", "post_mix_norm", "pre_ffn_norm", "w_gate", "w_up", "w_down", "post_ffn_norm"]
    outs = [loss, grad_x[None]]
    for i, table in enumerate((sg, sd, snm, snv)):
        for name in names:
            outs.append(big[name][i] if name in big else table[name])
    return tuple(outs)
```

```python
import functools

import numpy as np
import jax
import jax.numpy as jnp
from jax import lax
from jax.experimental import pallas as pl
from jax.experimental.pallas import tpu as pltpu

F32 = jnp.float32
BF16 = jnp.bfloat16

SEQ = 2048
D_MODEL = 1024
ATTN_W = 512
SGU_W = 512
HEAD_DIM = 64
N_GROUPS = 8
CHUNK = 128
D_FF = 2816
IN_W = 3 * ATTN_W + 2 * SGU_W
DILATIONS = (1, 4, 16)
ROPE_THETA = 500000.0
ROT_DIM = 16
ROT_HALF = 8
RMS_EPS = 1e-6
LN_EPS = 1e-5
Q_SCALE = 0.125
NEG = -1e30

N_DEV = 8
MESH_AXES = ("x", "y", "c")
MESH = pl.DeviceIdType.MESH

ADAM_LR = 0.001
ADAM_B1 = 0.9
ADAM_B2 = 0.999
ADAM_EPS = 1e-08
ADAM_WD = 0.01
ADAM_STEP = 10

VMEM_LIMIT = 60 * 1024 * 1024
ANY = pl.BlockSpec(memory_space=pl.ANY)

SMALL = (("pre_mix_norm", 1024), ("sgu_ln_gain", 512), ("sgu_ln_bias", 512), ("sgu_w_spatial", 8 * 128 * 128),
         ("sgu_b_spatial", 1024), ("attn_out_norm", 512), ("sgu_out_norm", 512), ("post_mix_norm", 1024),
         ("pre_ffn_norm", 1024), ("post_ffn_norm", 1024))
SMALL_ROWS = 1152


def _params(sem=("arbitrary",)):
    return pltpu.CompilerParams(dimension_semantics=sem, vmem_limit_bytes=VMEM_LIMIT)


def _dot(a, b):
    return jnp.dot(a, b, preferred_element_type=F32)


def _dot_nt(a, b):
    return lax.dot_general(a, b, (((1,), (1,)), ((), ())), preferred_element_type=F32)


def _dot_tn(a, b):
    return lax.dot_general(a, b, (((0,), (0,)), ((), ())), preferred_element_type=F32)


def _rms(z):
    return lax.rsqrt(jnp.mean(z * z, axis=-1, keepdims=True) + RMS_EPS)


def _rms_bwd(z, gain, d):
    r = _rms(z)
    n = z * r
    dn = d * gain
    dz = r * (dn - n * jnp.mean(dn * n, axis=-1, keepdims=True))
    return dz, jnp.sum(d * n, axis=0, keepdims=True)


def _gelu(z):
    return 0.5 * z * (1.0 + lax.erf(z * np.float32(1.0 / np.sqrt(2.0))))


def _gelu_grad(z):
    cdf = 0.5 * (1.0 + lax.erf(z * np.float32(1.0 / np.sqrt(2.0))))
    return cdf + z * jnp.exp(-0.5 * z * z) * np.float32(1.0 / np.sqrt(2.0 * np.pi))


def _rot_tables(pos_col, invf, ma, mb):
    ang = pos_col.astype(F32) * invf
    s = jnp.sin(ang)
    return jnp.cos(ang), s * ma, s * mb


def _rot(t, c, sa, sb):
    return t * c + pltpu.roll(t, 120, 1) * sa + pltpu.roll(t, 8, 1) * sb


def _rot_t(d, c, sa, sb):
    return d * c + pltpu.roll(d * sa, 8, 1) + pltpu.roll(d * sb, 120, 1)


def _rot_consts():
    lane = np.arange(128) % HEAD_DIM
    inv_freq = (np.float32(ROPE_THETA) ** (-np.arange(0, ROT_DIM, 2, dtype=np.float32) / np.float32(ROT_DIM))).astype(np.float32)
    invf = np.where(lane < ROT_DIM, inv_freq[lane % ROT_HALF], 0.0).astype(np.float32)
    ma = np.where(lane < ROT_HALF, -1.0, 0.0).astype(np.float32)
    mb = np.where((lane >= ROT_HALF) & (lane < ROT_DIM), 1.0, 0.0).astype(np.float32)
    return jnp.asarray(invf[None]), jnp.asarray(ma[None]), jnp.asarray(mb[None])


def _row_spec(tm, w):
    return pl.BlockSpec((tm, w), lambda i: (i, 0))


def _full_spec(shape):
    return pl.BlockSpec(shape, lambda i: (0,) * len(shape))


def in_proj(x, pos_col, g1, w_in_t, rot, exchanges=()):
    tm = 512

    def body(x_ref, pos_ref, g_ref, w_ref, invf_ref, ma_ref, mb_ref, h_ref, q_ref, k_ref, v_ref, u_ref, vs_ref):
        xf = x_ref[...]
        h = (xf * _rms(xf) * g_ref[...]).astype(BF16)
        h_ref[...] = h
        proj = _dot_nt(h, w_ref[...])
        c, sa, sb = _rot_tables(pos_ref[...], invf_ref[...], ma_ref[...], mb_ref[...])
        for j in range(ATTN_W // 128):
            q_ref[:, j * 128:(j + 1) * 128] = _rot(proj[:, j * 128:(j + 1) * 128], c, sa, sb) * Q_SCALE
            k_ref[:, j * 128:(j + 1) * 128] = _rot(proj[:, ATTN_W + j * 128:ATTN_W + (j + 1) * 128], c, sa, sb)
        v_ref[...] = proj[:, 2 * ATTN_W:3 * ATTN_W]
        u_ref[...] = proj[:, 3 * ATTN_W:3 * ATTN_W + SGU_W]
        vs_ref[...] = proj[:, 3 * ATTN_W + SGU_W:]

    act = jax.ShapeDtypeStruct((SEQ, 512), F32)
    return _hosted(
        "in_proj", body, SEQ // tm,
        [_row_spec(tm, D_MODEL), _row_spec(tm, 1), _full_spec((1, D_MODEL)), _full_spec((IN_W, D_MODEL)),
         _full_spec((1, 128)), _full_spec((1, 128)), _full_spec((1, 128))],
        [_row_spec(tm, D_MODEL)] + [_row_spec(tm, 512)] * 5,
        [jax.ShapeDtypeStruct((SEQ, D_MODEL), BF16)] + [act] * 5,
        (x, pos_col, g1, w_in_t, *rot), exchanges=exchanges)


def _attn_masks():
    row2 = lax.broadcasted_iota(jnp.int32, (128, 256), 0)
    col2 = lax.broadcasted_iota(jnp.int32, (128, 256), 1)
    mask2 = jnp.logical_or(jnp.logical_and(col2 < 128, col2 >= row2), jnp.logical_and(col2 >= 128, (col2 - 128) <= row2))
    row1 = lax.broadcasted_iota(jnp.int32, (128, 128), 0)
    col1 = lax.broadcasted_iota(jnp.int32, (128, 128), 1)
    return col1 < HEAD_DIM, col1 <= row1, mask2


def _for_each_block(fn):
    for p, d in enumerate(DILATIONS):
        n_blk = SEQ // d // 128

        def first(r, carry, p=p, d=d):
            fn(p, d, r, None)
            return carry

        if d == 1:
            fn(p, d, 0, None)
        else:
            lax.fori_loop(0, d, first, 0)
        if n_blk > 1:
            def rest(i, carry, p=p, d=d):
                r = i % d
                n = 1 + i // d
                q0 = n * (128 * d) + r
                prev0 = (n - 1) * (128 * d) + r
                if d == 1:
                    q0, prev0 = pl.multiple_of(q0, 128), pl.multiple_of(prev0, 128)
                fn(p, d, q0, prev0)
                return carry

            lax.fori_loop(0, d * (n_blk - 1), rest, 0)


def attn_fwd(q, k, v, exchanges=()):
    def body(q_ref, k_ref, v_ref, o_ref, lse_ref, op_ref, lp_ref):
        head0, mask1, mask2 = _attn_masks()

        def block(p, d, q0, prev0):
            rows = pl.ds(q0, 128, stride=d) if d > 1 else pl.ds(q0, 128)
            qb = q_ref[rows, :]
            if prev0 is None:
                kk = k_ref[rows, :].astype(BF16)
                vv = v_ref[rows, :].astype(BF16)
                mask = mask1
            else:
                prev = pl.ds(prev0, 128, stride=d) if d > 1 else pl.ds(prev0, 128)
                kk = jnp.concatenate([k_ref[prev, :], k_ref[rows, :]], axis=0).astype(BF16)
                vv = jnp.concatenate([v_ref[prev, :], v_ref[rows, :]], axis=0).astype(BF16)
                mask = mask2
            outs, lses = [], []
            for hm in (head0, jnp.logical_not(head0)):
                qm = jnp.where(hm, qb, 0.0).astype(BF16)
                s = jnp.where(mask, _dot_nt(qm, kk), NEG)
                m = jnp.max(s, axis=-1, keepdims=True)
                e = jnp.exp(s - m)
                l = jnp.sum(e, axis=-1, keepdims=True)
                outs.append(_dot(e.astype(BF16), vv) / l)
                lses.append(jnp.broadcast_to(m + jnp.log(l), (128, 128)))
            op_ref[p, rows, :] = jnp.where(head0, outs[0], outs[1])
            lp_ref[p, rows, :] = jnp.where(head0, lses[0], lses[1])

        _for_each_block(block)

        def combine(i, carry):
            rows = pl.ds(pl.multiple_of(i * 256, 256), 256)
            ls = [lp_ref[p, rows, :] for p in range(3)]
            m = jnp.maximum(jnp.maximum(ls[0], ls[1]), ls[2])
            lse = m + jnp.log(jnp.exp(ls[0] - m) + jnp.exp(ls[1] - m) + jnp.exp(ls[2] - m))
            o = jnp.zeros((256, 128), F32)
            for p in range(3):
                o = o + jnp.exp(ls[p] - lse) * op_ref[p, rows, :]
            o_ref[rows, :] = o
            lse_ref[rows, :] = lse
            return carry

        lax.fori_loop(0, SEQ // 256, combine, 0)

    slab = pl.BlockSpec((SEQ, 128), lambda i: (0, i))
    out = jax.ShapeDtypeStruct((SEQ, ATTN_W), F32)
    return _hosted(
        "attn_fwd", body, ATTN_W // 128, [slab] * 3, [slab] * 2, [out, out], (q, k, v),
        scratch_shapes=[pltpu.VMEM((3, SEQ, 128), F32), pltpu.VMEM((3, SEQ, 128), F32)], exchanges=exchanges)


def _causal_weights(w_ref):
    row = lax.broadcasted_iota(jnp.int32, (CHUNK, CHUNK), 0)
    col = lax.broadcasted_iota(jnp.int32, (CHUNK, CHUNK), 1)
    return [jnp.where(col <= row, w_ref[g], 0.0).astype(BF16) for g in range(N_GROUPS)], col <= row


def _sgu_chunk_fwd(u, vs, lg, lb, wc, bfull, head0):
    ug = _gelu(u)
    vg = _gelu(vs)
    xc = vg - jnp.mean(vg, axis=-1, keepdims=True)
    rstd = lax.rsqrt(jnp.mean(xc * xc, axis=-1, keepdims=True) + LN_EPS)
    xhat = xc * rstd
    vn = xhat * lg + lb
    mixed = []
    for gp in range(SGU_W // 128):
        vp = vn[:, gp * 128:(gp + 1) * 128].astype(BF16)
        mixed.append(jnp.where(head0, _dot(wc[2 * gp], vp), _dot(wc[2 * gp + 1], vp)))
    ms = jnp.concatenate(mixed, axis=1) + bfull
    return ug, xhat, rstd, vn, ms


def sgu_fwd(u, vs, lg, lb, w_sp, bfull, exchanges=()):
    cpb = 4

    def body(u_ref, vs_ref, lg_ref, lb_ref, w_ref, b_ref, o_ref):
        wc, _ = _causal_weights(w_ref)
        head0 = lax.broadcasted_iota(jnp.int32, (CHUNK, 128), 1) < HEAD_DIM
        for ci in range(cpb):
            rows = pl.ds(ci * CHUNK, CHUNK)
            ug, _, _, _, ms = _sgu_chunk_fwd(u_ref[rows, :], vs_ref[rows, :], lg_ref[...], lb_ref[...], wc, b_ref[...], head0)
            o_ref[rows, :] = ug * ms

    tm = cpb * CHUNK
    return _hosted(
        "sgu_fwd", body, SEQ // tm,
        [_row_spec(tm, SGU_W), _row_spec(tm, SGU_W), _full_spec((1, SGU_W)), _full_spec((1, SGU_W)),
         _full_spec((N_GROUPS, CHUNK, CHUNK)), _full_spec((CHUNK, SGU_W))],
        [_row_spec(tm, SGU_W)], [jax.ShapeDtypeStruct((SEQ, SGU_W), F32)],
        (u, vs, lg, lb, w_sp, bfull), exchanges=exchanges)


def out_proj(attn, sgu, x, ga, gs, w_out, gpm, gpf):
    tm = 512

    def body(a_ref, s_ref, x_ref, ga_ref, gs_ref, w_ref, gpm_ref, gpf_ref, mix_ref, y_ref, x2_ref, h2_ref):
        a = a_ref[...]
        s = s_ref[...]
        an = (a * _rms(a) * ga_ref[...]).astype(BF16)
        sn = (s * _rms(s) * gs_ref[...]).astype(BF16)
        mix_ref[:, :ATTN_W] = an
        mix_ref[:, ATTN_W:] = sn
        y = _dot(an, w_ref[:ATTN_W, :]) + _dot(sn, w_ref[ATTN_W:, :])
        y_ref[...] = y
        x2 = x_ref[...] + y * _rms(y) * gpm_ref[...]
        x2_ref[...] = x2
        h2_ref[...] = (x2 * _rms(x2) * gpf_ref[...]).astype(BF16)

    wide = jax.ShapeDtypeStruct((SEQ, D_MODEL), F32)
    wide16 = jax.ShapeDtypeStruct((SEQ, D_MODEL), BF16)
    return pl.pallas_call(
        body, name="out_proj", grid=(SEQ // tm,),
        in_specs=[_row_spec(tm, ATTN_W), _row_spec(tm, SGU_W), _row_spec(tm, D_MODEL), _full_spec((1, ATTN_W)),
                  _full_spec((1, SGU_W)), _full_spec((D_MODEL, D_MODEL)), _full_spec((1, D_MODEL)), _full_spec((1, D_MODEL))],
        out_specs=[_row_spec(tm, D_MODEL)] * 4, out_shape=[wide16, wide, wide, wide16],
        compiler_params=_params(),
    )(attn, sgu, x, ga, gs, w_out, gpm, gpf)


def ffn_up(h2, w_gate_t, w_up_t):
    tm = 256

    def body(h_ref, wg_ref, wu_ref, g_ref, u_ref, a_ref):
        h = h_ref[...]
        g = _dot_nt(h, wg_ref[...])
        u = _dot_nt(h, wu_ref[...])
        g_ref[...] = g
        u_ref[...] = u
        a_ref[...] = (g * jax.nn.sigmoid(g) * u).astype(BF16)

    ff = jax.ShapeDtypeStruct((SEQ, D_FF), F32)
    return pl.pallas_call(
        body, name="ffn_up", grid=(SEQ // tm,),
        in_specs=[_row_spec(tm, D_MODEL), _full_spec((D_FF, D_MODEL)), _full_spec((D_FF, D_MODEL))],
        out_specs=[_row_spec(tm, D_FF)] * 3, out_shape=[ff, ff, jax.ShapeDtypeStruct((SEQ, D_FF), BF16)],
        compiler_params=_params(),
    )(h2, w_gate_t, w_up_t)


def ffn_down_loss(act, w_down, x2, gpo, target):
    tm = 512

    def body(a_ref, w_ref, x2_ref, g_ref, t_ref, df_ref, dx3_ref, dg_ref, loss_ref):
        f = _dot(a_ref[...], w_ref[...])
        gain = g_ref[...]
        err = x2_ref[...] + f * _rms(f) * gain - t_ref[...]
        dx3 = err * np.float32(1.0 / D_MODEL)
        dx3_ref[...] = dx3
        df, dg = _rms_bwd(f, gain, dx3)
        df_ref[...] = df.astype(BF16)

        @pl.when(pl.program_id(0) == 0)
        def _():
            dg_ref[...] = jnp.zeros_like(dg_ref)
            loss_ref[...] = jnp.zeros_like(loss_ref)

        dg_ref[...] += dg
        loss_ref[...] += jnp.sum(err * err, axis=(0, 1), keepdims=True)

    return pl.pallas_call(
        body, name="ffn_down_loss", grid=(SEQ // tm,),
        in_specs=[_row_spec(tm, D_FF), _full_spec((D_FF, D_MODEL)), _row_spec(tm, D_MODEL), _full_spec((1, D_MODEL)),
                  _row_spec(tm, D_MODEL)],
        out_specs=[_row_spec(tm, D_MODEL), _row_spec(tm, D_MODEL), _full_spec((1, D_MODEL)), _full_spec((1, 1))],
        out_shape=[jax.ShapeDtypeStruct((SEQ, D_MODEL), BF16), jax.ShapeDtypeStruct((SEQ, D_MODEL), F32),
                   jax.ShapeDtypeStruct((1, D_MODEL), F32), jax.ShapeDtypeStruct((1, 1), F32)],
        compiler_params=_params(),
    )(act, w_down, x2, gpo, target)


def ffn_act_bwd(df, w_down, gate, up, exchanges=()):
    tm = 256

    def body(df_ref, w_ref, g_ref, u_ref, dg_ref, du_ref):
        dact = _dot_nt(df_ref[...], w_ref[...])
        g = g_ref[...]
        s = jax.nn.sigmoid(g)
        du_ref[...] = (dact * g * s).astype(BF16)
        dg_ref[...] = (dact * u_ref[...] * (s * (1.0 + g * (1.0 - s)))).astype(BF16)

    ff16 = jax.ShapeDtypeStruct((SEQ, D_FF), BF16)
    return _hosted(
        "ffn_act_bwd", body, SEQ // tm,
        [_row_spec(tm, D_MODEL), _full_spec((D_FF, D_MODEL)), _row_spec(tm, D_FF), _row_spec(tm, D_FF)],
        [_row_spec(tm, D_FF)] * 2, [ff16, ff16], (df, w_down, gate, up), exchanges=exchanges)


def ffn_in_bwd(dgate, dup, w_gate_t, w_up_t, x2, gpf, dx3, y, gpm, exchanges=()):
    tm = 256

    def body(dg_ref, du_ref, wg_ref, wu_ref, x2_ref, gpf_ref, dx3_ref, y_ref, gpm_ref, dx2_ref, dy_ref, dgpf_ref, dgpm_ref):
        dh2 = _dot(dg_ref[...], wg_ref[...]) + _dot(du_ref[...], wu_ref[...])
        dz, dgpf = _rms_bwd(x2_ref[...], gpf_ref[...], dh2)
        dx2 = dx3_ref[...] + dz
        dx2_ref[...] = dx2
        dy, dgpm = _rms_bwd(y_ref[...], gpm_ref[...], dx2)
        dy_ref[...] = dy.astype(BF16)

        @pl.when(pl.program_id(0) == 0)
        def _():
            dgpf_ref[...] = jnp.zeros_like(dgpf_ref)
            dgpm_ref[...] = jnp.zeros_like(dgpm_ref)

        dgpf_ref[...] += dgpf
        dgpm_ref[...] += dgpm

    vec = jax.ShapeDtypeStruct((1, D_MODEL), F32)
    return _hosted(
        "ffn_in_bwd", body, SEQ // tm,
        [_row_spec(tm, D_FF), _row_spec(tm, D_FF), _full_spec((D_FF, D_MODEL)), _full_spec((D_FF, D_MODEL)),
         _row_spec(tm, D_MODEL), _full_spec((1, D_MODEL)), _row_spec(tm, D_MODEL), _row_spec(tm, D_MODEL),
         _full_spec((1, D_MODEL))],
        [_row_spec(tm, D_MODEL), _row_spec(tm, D_MODEL), _full_spec((1, D_MODEL)), _full_spec((1, D_MODEL))],
        [jax.ShapeDtypeStruct((SEQ, D_MODEL), F32), jax.ShapeDtypeStruct((SEQ, D_MODEL), BF16), vec, vec],
        (dgate, dup, w_gate_t, w_up_t, x2, gpf, dx3, y, gpm), exchanges=exchanges)


def weight_grad(name, a, b, exchanges=()):
    m, n = a.shape[1], b.shape[1]
    tr = 256

    def body(a_ref, b_ref, o_ref):
        o_ref[...] = _dot_tn(a_ref[...], b_ref[...]).astype(BF16)

    (out,), done = _hosted(
        name, body, m // tr, [pl.BlockSpec((SEQ, tr), lambda i: (0, i)), _full_spec((SEQ, n))],
        [_row_spec(tr, n)], [jax.ShapeDtypeStruct((m, n), BF16)], (a, b), exchanges=exchanges)
    return out.reshape(N_DEV, m // N_DEV, n), done


def mix_bwd(dy, w_out, attn, sgu, ga, gs, exchanges=()):
    tm = 512

    def body(dy_ref, w_ref, a_ref, s_ref, ga_ref, gs_ref, da_ref, ds_ref, dga_ref, dgs_ref):
        dy = dy_ref[...]
        da, dga = _rms_bwd(a_ref[...], ga_ref[...], _dot_nt(dy, w_ref[:ATTN_W, :]))
        ds, dgs = _rms_bwd(s_ref[...], gs_ref[...], _dot_nt(dy, w_ref[ATTN_W:, :]))
        da_ref[...] = da
        ds_ref[...] = ds

        @pl.when(pl.program_id(0) == 0)
        def _():
            dga_ref[...] = jnp.zeros_like(dga_ref)
            dgs_ref[...] = jnp.zeros_like(dgs_ref)

        dga_ref[...] += dga
        dgs_ref[...] += dgs

    half = jax.ShapeDtypeStruct((SEQ, 512), F32)
    vec = jax.ShapeDtypeStruct((1, 512), F32)
    return _hosted(
        "mix_bwd", body, SEQ // tm,
        [_row_spec(tm, D_MODEL), _full_spec((D_MODEL, D_MODEL)), _row_spec(tm, 512), _row_spec(tm, 512),
         _full_spec((1, 512)), _full_spec((1, 512))],
        [_row_spec(tm, 512), _row_spec(tm, 512), _full_spec((1, 512)), _full_spec((1, 512))],
        [half, half, vec, vec], (dy, w_out, attn, sgu, ga, gs), exchanges=exchanges)


def sgu_bwd(u, vs, dsgu, lg, lb, w_sp, bfull):
    cpb = 4

    def body(u_ref, vs_ref, d_ref, lg_ref, lb_ref, w_ref, b_ref, du_ref, dvs_ref, dlg_ref, dlb_ref, dw_ref, db_ref):
        wc, causal = _causal_weights(w_ref)
        head0 = lax.broadcasted_iota(jnp.int32, (CHUNK, 128), 1) < HEAD_DIM
        lg = lg_ref[...]

        @pl.when(pl.program_id(0) == 0)
        def _():
            dlg_ref[...] = jnp.zeros_like(dlg_ref)
            dlb_ref[...] = jnp.zeros_like(dlb_ref)
            dw_ref[...] = jnp.zeros_like(dw_ref)
            db_ref[...] = jnp.zeros_like(db_ref)

        for ci in range(cpb):
            rows = pl.ds(ci * CHUNK, CHUNK)
            u = u_ref[rows, :]
            vs = vs_ref[rows, :]
            d = d_ref[rows, :]
            ug, xhat, rstd, vn, ms = _sgu_chunk_fwd(u, vs, lg, lb_ref[...], wc, b_ref[...], head0)
            du_ref[rows, :] = (d * ms * _gelu_grad(u)).astype(BF16)
            dms = d * ug
            db_ref[...] += dms
            dvn = []
            for gp in range(SGU_W // 128):
                dmp = dms[:, gp * 128:(gp + 1) * 128]
                dm0 = jnp.where(head0, dmp, 0.0).astype(BF16)
                dm1 = jnp.where(head0, 0.0, dmp).astype(BF16)
                vp = vn[:, gp * 128:(gp + 1) * 128].astype(BF16)
                dw_ref[2 * gp] += _dot_nt(dm0, vp)
                dw_ref[2 * gp + 1] += _dot_nt(dm1, vp)
                dvn.append(_dot_tn(wc[2 * gp], dm0) + _dot_tn(wc[2 * gp + 1], dm1))
            dvn = jnp.concatenate(dvn, axis=1)
            dlg_ref[...] += jnp.sum(dvn * xhat, axis=0, keepdims=True)
            dlb_ref[...] += jnp.sum(dvn, axis=0, keepdims=True)
            dxh = dvn * lg
            dvg = rstd * (dxh - jnp.mean(dxh, axis=-1, keepdims=True) - xhat * jnp.mean(dxh * xhat, axis=-1, keepdims=True))
            dvs_ref[rows, :] = (dvg * _gelu_grad(vs)).astype(BF16)

        @pl.when(pl.program_id(0) == pl.num_programs(0) - 1)
        def _():
            for g in range(N_GROUPS):
                dw_ref[g] = jnp.where(causal, dw_ref[g], 0.0)

    tm = cpb * CHUNK
    half16 = jax.ShapeDtypeStruct((SEQ, SGU_W), BF16)
    vec = jax.ShapeDtypeStruct((1, SGU_W), F32)
    return pl.pallas_call(
        body, name="sgu_bwd", grid=(SEQ // tm,),
        in_specs=[_row_spec(tm, SGU_W)] * 3 + [_full_spec((1, SGU_W)), _full_spec((1, SGU_W)),
                                                  _full_spec((N_GROUPS, CHUNK, CHUNK)), _full_spec((CHUNK, SGU_W))],
        out_specs=[_row_spec(tm, SGU_W), _row_spec(tm, SGU_W), _full_spec((1, SGU_W)), _full_spec((1, SGU_W)),
                   _full_spec((N_GROUPS, CHUNK, CHUNK)), _full_spec((CHUNK, SGU_W))],
        out_shape=[half16, half16, vec, vec, jax.ShapeDtypeStruct((N_GROUPS, CHUNK, CHUNK), F32),
                   jax.ShapeDtypeStruct((CHUNK, SGU_W), F32)],
        compiler_params=_params(),
    )(u, vs, dsgu, lg, lb, w_sp, bfull)


def attn_bwd(q, k, v, o, lse, do, pos_col, rot, exchanges=()):
    def body(q_ref, k_ref, v_ref, o_ref, lse_ref, do_ref, pos_ref, invf_ref, ma_ref, mb_ref,
             dq_ref, dk_ref, dv_ref, dqa_ref, dka_ref, dva_ref, dlt_ref):
        head0, mask1, mask2 = _attn_masks()
        dqa_ref[...] = jnp.zeros_like(dqa_ref)
        dka_ref[...] = jnp.zeros_like(dka_ref)
        dva_ref[...] = jnp.zeros_like(dva_ref)

        def delta(i, carry):
            rows = pl.ds(pl.multiple_of(i * 256, 256), 256)
            prod = do_ref[rows, :] * o_ref[rows, :]
            h0 = lax.broadcasted_iota(jnp.int32, (256, 128), 1) < HEAD_DIM
            d0 = jnp.sum(jnp.where(h0, prod, 0.0), axis=-1, keepdims=True)
            d1 = jnp.sum(jnp.where(h0, 0.0, prod), axis=-1, keepdims=True)
            dlt_ref[rows, :] = jnp.where(h0, d0, d1)
            return carry

        lax.fori_loop(0, SEQ // 256, delta, 0)

        def block(p, d, q0, prev0):
            rows = pl.ds(q0, 128, stride=d) if d > 1 else pl.ds(q0, 128)
            qb = q_ref[rows, :]
            dob = do_ref[rows, :]
            lse_b = lse_ref[rows, :]
            dlt_b = dlt_ref[rows, :]
            if prev0 is None:
                prev = None
                kk = k_ref[rows, :].astype(BF16)
                vv = v_ref[rows, :].astype(BF16)
                mask = mask1
            else:
                prev = pl.ds(prev0, 128, stride=d) if d > 1 else pl.ds(prev0, 128)
                kk = jnp.concatenate([k_ref[prev, :], k_ref[rows, :]], axis=0).astype(BF16)
                vv = jnp.concatenate([v_ref[prev, :], v_ref[rows, :]], axis=0).astype(BF16)
                mask = mask2
            dq_b = jnp.zeros((128, 128), F32)
            dk_b = jnp.zeros(kk.shape, F32)
            dv_b = jnp.zeros(kk.shape, F32)
            for h, hm in enumerate((head0, jnp.logical_not(head0))):
                lane0 = h * HEAD_DIM
                qm = jnp.where(hm, qb, 0.0).astype(BF16)
                dom = jnp.where(hm, dob, 0.0).astype(BF16)
                s = _dot_nt(qm, kk)
                pr = jnp.where(mask, jnp.exp(s - lse_b[:, lane0:lane0 + 1]), 0.0)
                dp = _dot_nt(dom, vv)
                ds = (pr * (dp - dlt_b[:, lane0:lane0 + 1])).astype(BF16)
                dv_b = dv_b + _dot_tn(pr.astype(BF16), dom)
                dk_b = dk_b + _dot_tn(ds, qm)
                dq_b = dq_b + jnp.where(hm, _dot(ds, kk), 0.0)
            dqa_ref[rows, :] += dq_b
            if prev is None:
                dka_ref[rows, :] += dk_b
                dva_ref[rows, :] += dv_b
            else:
                dka_ref[prev, :] += dk_b[:128]
                dva_ref[prev, :] += dv_b[:128]
                dka_ref[rows, :] += dk_b[128:]
                dva_ref[rows, :] += dv_b[128:]

        _for_each_block(block)

        def finish(i, carry):
            rows = pl.ds(pl.multiple_of(i * 256, 256), 256)
            c, sa, sb = _rot_tables(pos_ref[rows, :], invf_ref[...], ma_ref[...], mb_ref[...])
            dq_ref[rows, :] = _rot_t(dqa_ref[rows, :] * Q_SCALE, c, sa, sb).astype(BF16)
            dk_ref[rows, :] = _rot_t(dka_ref[rows, :], c, sa, sb).astype(BF16)
            dv_ref[rows, :] = dva_ref[rows, :].astype(BF16)
            return carry

        lax.fori_loop(0, SEQ // 256, finish, 0)

    slab = pl.BlockSpec((SEQ, 128), lambda i: (0, i))
    out = jax.ShapeDtypeStruct((SEQ, ATTN_W), BF16)
    acc = pltpu.VMEM((SEQ, 128), F32)
    return _hosted(
        "attn_bwd", body, ATTN_W // 128,
        [slab] * 6 + [_full_spec((SEQ, 1)), _full_spec((1, 128)), _full_spec((1, 128)), _full_spec((1, 128))],
        [slab] * 3, [out, out, out], (q, k, v, o, lse, do, pos_col, *rot),
        scratch_shapes=[acc, acc, acc, acc], exchanges=exchanges)


def in_bwd(dproj, w_in_t, x, g1, dx2, exchanges=()):
    tm = 512

    def body(dp_ref, w_ref, x_ref, g_ref, dx2_ref, dx_ref, dg_ref):
        dh1 = _dot(dp_ref[...], w_ref[...])
        dz, dg = _rms_bwd(x_ref[...], g_ref[...], dh1)
        dx_ref[...] = dx2_ref[...] + dz

        @pl.when(pl.program_id(0) == 0)
        def _():
            dg_ref[...] = jnp.zeros_like(dg_ref)

        dg_ref[...] += dg

    return _hosted(
        "in_bwd", body, SEQ // tm,
        [_row_spec(tm, IN_W), _full_spec((IN_W, D_MODEL)), _row_spec(tm, D_MODEL), _full_spec((1, D_MODEL)),
         _row_spec(tm, D_MODEL)],
        [_row_spec(tm, D_MODEL), _full_spec((1, D_MODEL))],
        [jax.ShapeDtypeStruct((SEQ, D_MODEL), F32), jax.ShapeDtypeStruct((1, D_MODEL), F32)],
        (dproj, w_in_t, x, g1, dx2), exchanges=exchanges)


def _coords():
    return lax.axis_index("x"), lax.axis_index("y"), lax.axis_index("c")


class Exchange:
    def __init__(self, srcs, bufs, new_shapes, n_sems, make):
        self.srcs, self.bufs, self.new_shapes, self.n_sems, self.make = list(srcs), list(bufs), list(new_shapes), n_sems, make


def _hosted(name, body, n_steps, in_specs, out_specs, out_shape, args, scratch_shapes=(), exchanges=(), prefetch=None):
    out_shape, out_specs = list(out_shape), list(out_specs)
    srcs = [a for ex in exchanges for a in ex.srcs]
    bufs = [a for ex in exchanges for a in ex.bufs]
    news = [s for ex in exchanges for s in ex.new_shapes]
    n_pre = 0 if prefetch is None else 1
    n_in, n_out, n_scr = len(args), len(out_shape), len(scratch_shapes)

    def wrapped(*refs):
        refs = list(refs)
        pre, refs = refs[:n_pre], refs[n_pre:]
        ins = refs[:n_in]
        src_refs = refs[n_in:n_in + len(srcs)]
        at = n_in + len(srcs) + len(bufs)
        outs = refs[at:at + n_out]
        buf_refs = refs[at + n_out:at + n_out + len(bufs)]
        new_refs = refs[at + n_out + len(bufs):at + n_out + len(bufs) + len(news)]
        at += n_out + len(bufs) + len(news)
        scratch, sems = refs[at:at + n_scr], refs[at + n_scr:]

        def copies():
            made, si, bi, ni = [], 0, 0, 0
            for k, ex in enumerate(exchanges):
                made.append(ex.make(src_refs[si:si + len(ex.srcs)], buf_refs[bi:bi + len(ex.bufs)],
                                    new_refs[ni:ni + len(ex.new_shapes)], sems[2 * k], sems[2 * k + 1]))
                si, bi, ni = si + len(ex.srcs), bi + len(ex.bufs), ni + len(ex.new_shapes)
            return made

        if exchanges:
            @pl.when(pl.program_id(0) == 0)
            def _():
                for starts, _, _ in copies():
                    for cp in starts:
                        cp.start()

        body(*pre, *ins, *outs, *scratch)

        if exchanges:
            @pl.when(pl.program_id(0) == n_steps - 1)
            def _():
                for _, sends, recvs in copies():
                    for cp in sends:
                        cp.wait_send()
                    for cp in recvs:
                        cp.wait_recv()

    sem_shapes = []
    for ex in exchanges:
        sem_shapes += [pltpu.SemaphoreType.DMA((ex.n_sems,)), pltpu.SemaphoreType.DMA((ex.n_sems,))]
    all_in = list(in_specs) + [ANY] * (len(srcs) + len(bufs))
    all_out = out_specs + [ANY] * (len(bufs) + len(news))
    all_shape = out_shape + [jax.ShapeDtypeStruct(b.shape, b.dtype) for b in bufs] + news
    all_scratch = list(scratch_shapes) + sem_shapes
    aliases = {n_pre + n_in + len(srcs) + i: n_out + i for i in range(len(bufs))}
    if prefetch is None:
        call = pl.pallas_call(wrapped, name=name, grid=(n_steps,), in_specs=all_in, out_specs=all_out, out_shape=all_shape,
                              scratch_shapes=all_scratch, input_output_aliases=aliases, compiler_params=_params())
        outs = call(*args, *srcs, *bufs)
    else:
        spec = pltpu.PrefetchScalarGridSpec(num_scalar_prefetch=1, grid=(n_steps,), in_specs=all_in, out_specs=all_out,
                                            scratch_shapes=all_scratch)
        call = pl.pallas_call(wrapped, name=name, grid_spec=spec, out_shape=all_shape, input_output_aliases=aliases,
                              compiler_params=_params())
        outs = call(prefetch, *args, *srcs, *bufs)
    results, bi, ni = [], n_out, n_out + len(bufs)
    for ex in exchanges:
        results.append((list(outs[bi:bi + len(ex.bufs)]), list(outs[ni:ni + len(ex.new_shapes)])))
        bi, ni = bi + len(ex.bufs), ni + len(ex.new_shapes)
    return list(outs[:n_out]), results


def _gather_copies(bufs, send_sems, recv_sems):
    x, y, c = _coords()
    me, sibling = (x, y, c), (x, y, 1 - c)
    chips = [(1 - x, y), (x, 1 - y), (1 - x, 1 - y)]

    def copy(a, k, block, to):
        r = bufs[a].shape[0] // N_DEV
        rows = bufs[a].at[pl.ds((4 * block[0] + 2 * block[1] + block[2]) * r, r), :]
        return pltpu.make_async_remote_copy(src_ref=rows, dst_ref=rows, send_sem=send_sems.at[7 * a + k],
                                            recv_sem=recv_sems.at[7 * a + k], device_id=to, device_id_type=MESH)

    first, first_in, passed, passed_in = [], [], [], []
    for a in range(len(bufs)):
        first.append(copy(a, 0, me, sibling))
        first_in.append(copy(a, 0, sibling, me))
        for j, chip in enumerate(chips):
            first.append(copy(a, 1 + j, me, (*chip, c)))
            first_in.append(copy(a, 1 + j, (*chip, c), me))
            passed.append(copy(a, 4 + j, (*chip, c), sibling))
            passed_in.append(copy(a, 4 + j, (*chip, 1 - c), me))
    return first, first_in, passed, passed_in


def gather_send(bufs):
    def make(src_refs, buf_refs, new_refs, send_sems, recv_sems):
        first, first_in, _, _ = _gather_copies(buf_refs, send_sems, recv_sems)
        return first, first, first_in

    return Exchange([], bufs, [], 7 * len(bufs), make)


def gather_pass_on(bufs):
    def make(src_refs, buf_refs, new_refs, send_sems, recv_sems):
        _, _, passed, passed_in = _gather_copies(buf_refs, send_sems, recv_sems)
        return passed, passed, passed_in

    return Exchange([], bufs, [], 7 * len(bufs), make)


def all_gather_in_place(name, bufs):
    n = len(bufs)

    def body(*refs):
        outs, send_sems, recv_sems = refs[n:2 * n], refs[2 * n], refs[2 * n + 1]
        first, first_in, passed, passed_in = _gather_copies(outs, send_sems, recv_sems)
        for cp in first:
            cp.start()
        for cp in first_in:
            cp.wait_recv()
        for cp in passed:
            cp.start()
        for cp in passed_in:
            cp.wait_recv()
        for cp in first + passed:
            cp.wait_send()

    return pl.pallas_call(
        body, name=name, in_specs=[ANY] * n, out_specs=[ANY] * n,
        out_shape=[jax.ShapeDtypeStruct(b.shape, b.dtype) for b in bufs],
        scratch_shapes=[pltpu.SemaphoreType.DMA((7 * n,)), pltpu.SemaphoreType.DMA((7 * n,))],
        input_output_aliases={i: i for i in range(n)},
    )(*bufs)


def place_shards(shards, dev):
    n = len(shards)

    def body(dev_ref, *refs):
        for a in range(n):
            refs[n + a][...] = refs[a][...].astype(BF16)

    spec = pltpu.PrefetchScalarGridSpec(
        num_scalar_prefetch=1, grid=(1,),
        in_specs=[pl.BlockSpec(s.shape, lambda i, dev_ref: (0, 0)) for s in shards],
        out_specs=[pl.BlockSpec(s.shape, lambda i, dev_ref: (dev_ref[0], 0)) for s in shards])
    return pl.pallas_call(
        body, name="place_shards", grid_spec=spec,
        out_shape=[jax.ShapeDtypeStruct((N_DEV * s.shape[0], s.shape[1]), BF16) for s in shards],
        compiler_params=_params(),
    )(dev, *shards)


def to_sibling(grads):
    def make(src_refs, buf_refs, new_refs, send_sems, recv_sems):
        x, y, c = _coords()
        copies = [pltpu.make_async_remote_copy(
            src_ref=src_refs[a].at[2 * xy + 1 - c], dst_ref=new_refs[a].at[xy], send_sem=send_sems.at[4 * a + xy],
            recv_sem=recv_sems.at[4 * a + xy], device_id=(x, y, 1 - c), device_id_type=MESH)
            for a in range(len(src_refs)) for xy in range(4)]
        return copies, copies, copies

    return Exchange(grads, [], [jax.ShapeDtypeStruct((4,) + g.shape[1:], g.dtype) for g in grads], 4 * len(grads), make)


def to_chips(parts):
    def make(src_refs, buf_refs, new_refs, send_sems, recv_sems):
        x, y, c = _coords()
        chips = [(1 - x, y), (x, 1 - y), (1 - x, 1 - y)]
        copies = [pltpu.make_async_remote_copy(
            src_ref=src_refs[a].at[2 * px + py], dst_ref=new_refs[a].at[2 * x + y], send_sem=send_sems.at[3 * a + j],
            recv_sem=recv_sems.at[3 * a + j], device_id=(px, py, c), device_id_type=MESH)
            for a in range(len(src_refs)) for j, (px, py) in enumerate(chips)]
        return copies, copies, copies

    return Exchange(parts, [], [jax.ShapeDtypeStruct(p.shape, p.dtype) for p in parts], 3 * len(parts), make)


def to_owners(grad):
    def make(src_refs, buf_refs, new_refs, send_sems, recv_sems):
        x, y, c = _coords()
        copies = []
        for m in range(1, N_DEV):
            px, py, pc = x ^ (m >> 2), y ^ ((m >> 1) & 1), c ^ (m & 1)
            copies.append(pltpu.make_async_remote_copy(
                src_ref=src_refs[0].at[4 * px + 2 * py + pc], dst_ref=new_refs[0].at[4 * x + 2 * y + c],
                send_sem=send_sems.at[m - 1], recv_sem=recv_sems.at[m - 1], device_id=(px, py, pc), device_id_type=MESH))
        return copies, copies, copies

    return Exchange([grad], [], [jax.ShapeDtypeStruct(grad.shape, grad.dtype)], N_DEV - 1, make)


def exchange_only(name, exchanges):
    def body():
        pass

    return _hosted(name, body, 1, [], [], [], [], exchanges=exchanges)[1]


def sum_cores(name, grad, other, core):
    _, r, w = other.shape

    def body(core_ref, g_ref, o_ref, out_ref):
        out_ref[...] = (g_ref[...].astype(F32) + o_ref[...].astype(F32)).astype(out_ref.dtype)

    return pl.pallas_call(
        body, name=name,
        grid_spec=pltpu.PrefetchScalarGridSpec(
            num_scalar_prefetch=1, grid=(4,),
            in_specs=[pl.BlockSpec((1, r, w), lambda i, core_ref: (2 * i + core_ref[0], 0, 0)),
                      pl.BlockSpec((1, r, w), lambda i, core_ref: (i, 0, 0))],
            out_specs=pl.BlockSpec((1, r, w), lambda i, core_ref: (i, 0, 0))),
        out_shape=jax.ShapeDtypeStruct(other.shape, other.dtype),
        compiler_params=_params(),
    )(core, grad, other)


def sum_owned(name, grad, others, dev_ids):
    _, r, w = grad.shape

    def body(ids_ref, *refs):
        acc = refs[0][0]
        for k in range(1, N_DEV):
            acc = acc + refs[k][0]
        refs[N_DEV][...] = acc

    def pick(k):
        return pl.BlockSpec((1, r, w), lambda i, ids_ref: (ids_ref[k], 0, 0))

    return pl.pallas_call(
        body, name=name,
        grid_spec=pltpu.PrefetchScalarGridSpec(
            num_scalar_prefetch=1, grid=(1,), in_specs=[pick(k) for k in range(N_DEV)],
            out_specs=pl.BlockSpec((r, w), lambda i, ids_ref: (ids_ref[0], 0))),
        out_shape=jax.ShapeDtypeStruct((N_DEV * r, w), F32),
        compiler_params=_params(),
    )(dev_ids, grad, *([others] * (N_DEV - 1)))


def sum_chips(name, part, others, chip_ids):
    _, r, w = part.shape

    def body(ids_ref, p_ref, a_ref, b_ref, c_ref, out_ref):
        acc = p_ref[0].astype(F32) + a_ref[0].astype(F32)
        out_ref[...] = (acc + b_ref[0].astype(F32)) + c_ref[0].astype(F32)

    def pick(k):
        return pl.BlockSpec((1, r, w), lambda i, ids_ref: (ids_ref[k], 0, 0))

    return pl.pallas_call(
        body, name=name,
        grid_spec=pltpu.PrefetchScalarGridSpec(
            num_scalar_prefetch=1, grid=(1,),
            in_specs=[pick(0), pick(1), pick(2), pick(3)],
            out_specs=pl.BlockSpec((r, w), lambda i, ids_ref: (0, 0))),
        out_shape=jax.ShapeDtypeStruct((r, w), F32),
        compiler_params=_params(),
    )(chip_ids, part, others, others, others)


def adamw(name, w, g, m, v):
    c1 = np.float32(1.0 - ADAM_B1 ** ADAM_STEP)
    c2 = np.float32(1.0 - ADAM_B2 ** ADAM_STEP)

    def body(w_ref, g_ref, m_ref, v_ref, d_ref, nm_ref, nv_ref):
        g = g_ref[...]
        nm = ADAM_B1 * m_ref[...] + np.float32(1.0 - ADAM_B1) * g
        nv = ADAM_B2 * v_ref[...] + np.float32(1.0 - ADAM_B2) * (g * g)
        nm_ref[...] = nm
        nv_ref[...] = nv
        d_ref[...] = -ADAM_LR * ((nm / c1) / (jnp.sqrt(nv / c2) + ADAM_EPS) + ADAM_WD * w_ref[...])

    shape = jax.ShapeDtypeStruct(w.shape, F32)
    spec = _full_spec(w.shape)
    return pl.pallas_call(
        body, name=name, grid=(1,), in_specs=[spec] * 4, out_specs=[spec] * 3, out_shape=[shape] * 3,
        compiler_params=_params(),
    )(w, g, m, v)


def _pack_small(parts):
    flat = jnp.concatenate([parts[name].reshape(-1) for name, _ in SMALL])
    return jnp.pad(flat, (0, SMALL_ROWS * 128 - flat.shape[0])).reshape(SMALL_ROWS, 128)


def _unpack_small(packed, like):
    flat = packed.reshape(-1)
    out, at = {}, 0
    for name, size in SMALL:
        out[name] = flat[at:at + size].reshape(like[name].shape)
        at += size
    return out


def kernel(x, positions, pre_mix_norm, w_in, sgu_ln_gain, sgu_ln_bias, sgu_w_spatial, sgu_b_spatial, attn_out_norm, sgu_out_norm, w_out, post_mix_norm, pre_ffn_norm, w_gate, w_up, w_down, post_ffn_norm, loss_target, m_pre_mix_norm, m_w_in, m_sgu_ln_gain, m_sgu_ln_bias, m_sgu_w_spatial, m_sgu_b_spatial, m_attn_out_norm, m_sgu_out_norm, m_w_out, m_post_mix_norm, m_pre_ffn_norm, m_w_gate, m_w_up, m_w_down, m_post_ffn_norm, v_pre_mix_norm, v_w_in, v_sgu_ln_gain, v_sgu_ln_bias, v_sgu_w_spatial, v_sgu_b_spatial, v_attn_out_norm, v_sgu_out_norm, v_w_out, v_post_mix_norm, v_pre_ffn_norm, v_w_gate, v_w_up, v_w_down, v_post_ffn_norm):
    small_w = dict(pre_mix_norm=pre_mix_norm, sgu_ln_gain=sgu_ln_gain, sgu_ln_bias=sgu_ln_bias, sgu_w_spatial=sgu_w_spatial,
                   sgu_b_spatial=sgu_b_spatial, attn_out_norm=attn_out_norm, sgu_out_norm=sgu_out_norm,
                   post_mix_norm=post_mix_norm, pre_ffn_norm=pre_ffn_norm, post_ffn_norm=post_ffn_norm)
    small_m = dict(pre_mix_norm=m_pre_mix_norm, sgu_ln_gain=m_sgu_ln_gain, sgu_ln_bias=m_sgu_ln_bias, sgu_w_spatial=m_sgu_w_spatial,
                   sgu_b_spatial=m_sgu_b_spatial, attn_out_norm=m_attn_out_norm, sgu_out_norm=m_sgu_out_norm,
                   post_mix_norm=m_post_mix_norm, pre_ffn_norm=m_pre_ffn_norm, post_ffn_norm=m_post_ffn_norm)
    small_v = dict(pre_mix_norm=v_pre_mix_norm, sgu_ln_gain=v_sgu_ln_gain, sgu_ln_bias=v_sgu_ln_bias, sgu_w_spatial=v_sgu_w_spatial,
                   sgu_b_spatial=v_sgu_b_spatial, attn_out_norm=v_attn_out_norm, sgu_out_norm=v_sgu_out_norm,
                   post_mix_norm=v_post_mix_norm, pre_ffn_norm=v_pre_ffn_norm, post_ffn_norm=v_post_ffn_norm)

    x2d = x[0]
    target = loss_target[0]
    pos_col = positions.reshape(SEQ, 1)
    rot = _rot_consts()
    w_sp = sgu_w_spatial[0]
    bfull = jnp.repeat(sgu_b_spatial[0].T, HEAD_DIM, axis=1)

    x_i, y_i, c_i = (lax.axis_index(a).astype(jnp.int32) for a in MESH_AXES)
    dev = 4 * x_i + 2 * y_i + c_i
    core = c_i.reshape(1)
    chip = 2 * x_i + y_i
    chip_ids = jnp.stack([chip, chip ^ 1, chip ^ 2, chip ^ 3])
    dev_ids = jnp.stack([dev ^ m for m in range(N_DEV)])

    w_in_t, w_gate_t, w_up_t, w_out_f, w_down_f = place_shards(
        [w_in[0].T, w_gate[0].T, w_up[0].T, w_out[0], w_down[0]], dev.reshape(1))
    (w_in_t,) = all_gather_in_place("gather_w_in", [w_in_t])

    (h1, q, k, v, u, vs), [([w_out_f], _)] = in_proj(
        x2d, pos_col, pre_mix_norm, w_in_t, rot, exchanges=[gather_send([w_out_f])])
    (attn, lse), [([w_out_f], _), ([w_gate_t, w_up_t, w_down_f], _)] = attn_fwd(
        q, k, v, exchanges=[gather_pass_on([w_out_f]), gather_send([w_gate_t, w_up_t, w_down_f])])
    (sgu,), [([w_gate_t, w_up_t, w_down_f], _)] = sgu_fwd(
        u, vs, sgu_ln_gain, sgu_ln_bias, w_sp, bfull, exchanges=[gather_pass_on([w_gate_t, w_up_t, w_down_f])])
    mix, y, x2, h2 = out_proj(attn, sgu, x2d, attn_out_norm, sgu_out_norm, w_out_f, post_mix_norm, pre_ffn_norm)
    gate, up, act = ffn_up(h2, w_gate_t, w_up_t)
    df, dx3, d_post_ffn, sq_err = ffn_down_loss(act, w_down_f, x2, post_ffn_norm, target)
    loss = lax.psum(sq_err[0, 0] * np.float32(0.5 / D_MODEL), MESH_AXES)

    g_w_down, _ = weight_grad("grad_w_down", act, df)
    (dgate, dup), [(_, [s_down])] = ffn_act_bwd(df, w_down_f, gate, up, exchanges=[to_sibling([g_w_down])])
    p_down = sum_cores("sum_cores_down", g_w_down, s_down, core)
    g_w_gate, [(_, [c_down])] = weight_grad("grad_w_gate", dgate, h2, exchanges=[to_chips([p_down])])
    r_w_down = sum_chips("sum_chips_down", p_down, c_down, chip_ids)
    g_w_up, [(_, [s_gate])] = weight_grad("grad_w_up", dup, h2, exchanges=[to_sibling([g_w_gate])])
    p_gate = sum_cores("sum_cores_gate", g_w_gate, s_gate, core)
    (dx2, dy, d_pre_ffn, d_post_mix), [(_, [s_up]), (_, [c_gate])] = ffn_in_bwd(
        dgate, dup, w_gate_t, w_up_t, x2, pre_ffn_norm, dx3, y, post_mix_norm,
        exchanges=[to_sibling([g_w_up]), to_chips([p_gate])])
    p_up = sum_cores("sum_cores_up", g_w_up, s_up, core)
    r_w_gate_t = sum_chips("sum_chips_gate", p_gate, c_gate, chip_ids)
    g_w_out, _ = weight_grad("grad_w_out", mix, dy)
    (dattn, dsgu, d_attn_out, d_sgu_out), [(_, [s_out])] = mix_bwd(
        dy, w_out_f, attn, sgu, attn_out_norm, sgu_out_norm, exchanges=[to_sibling([g_w_out])])
    p_out = sum_cores("sum_cores_out", g_w_out, s_out, core)
    du, dvs, d_ln_gain, d_ln_bias, d_w_sp, d_bfull = sgu_bwd(u, vs, dsgu, sgu_ln_gain, sgu_ln_bias, w_sp, bfull)
    (dq, dk, dv), [(_, [c_up, c_out])] = attn_bwd(
        q, k, v, attn, lse, dattn, pos_col, rot, exchanges=[to_chips([p_up, p_out])])
    r_w_up_t = sum_chips("sum_chips_up", p_up, c_up, chip_ids)
    r_w_out = sum_chips("sum_chips_out", p_out, c_out, chip_ids)
    dproj = jnp.concatenate([dq, dk, dv, du, dvs], axis=1)
    g_w_in, _ = weight_grad("grad_w_in", dproj, h1)
    (grad_x, d_pre_mix), [(_, [s_in])] = in_bwd(
        dproj, w_in_t, x2d, pre_mix_norm, dx2, exchanges=[to_sibling([g_w_in])])
    p_in = sum_cores("sum_cores_in", g_w_in, s_in, core)

    d_b_sp = d_bfull.reshape(CHUNK, N_GROUPS, HEAD_DIM).sum(axis=-1).T
    small_g = _pack_small(dict(pre_mix_norm=d_pre_mix, sgu_ln_gain=d_ln_gain, sgu_ln_bias=d_ln_bias, sgu_w_spatial=d_w_sp,
                               sgu_b_spatial=d_b_sp, attn_out_norm=d_attn_out, sgu_out_norm=d_sgu_out,
                               post_mix_norm=d_post_mix, pre_ffn_norm=d_pre_ffn, post_ffn_norm=d_post_ffn))
    small_g = small_g.reshape(N_DEV, SMALL_ROWS // N_DEV, 128)
    [(_, [c_in]), (_, [o_small])] = exchange_only("grads_tail", [to_chips([p_in]), to_owners(small_g)])
    r_w_in_t = sum_chips("sum_chips_in", p_in, c_in, chip_ids)
    all_small = sum_owned("sum_small", small_g, o_small, dev_ids)
    (all_small,) = all_gather_in_place("gather_small_grads", [all_small])

    big = {}
    for name, w, g, m, vv in (("w_in", w_in, r_w_in_t.T, m_w_in, v_w_in), ("w_gate", w_gate, r_w_gate_t.T, m_w_gate, v_w_gate),
                              ("w_up", w_up, r_w_up_t.T, m_w_up, v_w_up), ("w_out", w_out, r_w_out, m_w_out, v_w_out),
                              ("w_down", w_down, r_w_down, m_w_down, v_w_down)):
        d, nm, nv = adamw("adamw_" + name, w[0], g, m[0], vv[0])
        big[name] = (g[None], d[None], nm[None], nv[None])
    sd, snm, snv = adamw("adamw_small", _pack_small(small_w), all_small, _pack_small(small_m), _pack_small(small_v))
    sg, sd, snm, snv = (_unpack_small(t, small_w) for t in (all_small, sd, snm, snv))

    names = ["pre_mix_norm", "w_in", "sgu_ln_gain", "sgu_ln_bias", "sgu_w_spatial", "sgu_b_spatial", "attn_out_norm",
             "sgu_out_norm", "w_out", "post_mix_norm", "pre_ffn_norm", "w_gate", "w_up", "w_down", "post_ffn_norm"]
    outs = [loss, grad_x[None]]
    for i, table in enumerate((sg, sd, snm, snv)):
        for name in names:
            outs.append(big[name][i] if name in big else table[name])
    return tuple(outs)
```

```python
import functools

import numpy as np
import jax
import jax.numpy as jnp
from jax import lax
from jax.experimental import pallas as pl
from jax.experimental.pallas import tpu as pltpu

F32 = jnp.float32
BF16 = jnp.bfloat16

SEQ = 2048
D_MODEL = 1024
ATTN_W = 512
SGU_W = 512
HEAD_DIM = 64
N_GROUPS = 8
CHUNK = 128
D_FF = 2816
IN_W = 3 * ATTN_W + 2 * SGU_W
DILATIONS = (1, 4, 16)
ROPE_THETA = 500000.0
ROT_DIM = 16
ROT_HALF = 8
RMS_EPS = 1e-6
LN_EPS = 1e-5
Q_SCALE = 0.125
NEG = -1e30

N_DEV = 8
MESH_AXES = ("x", "y", "c")
MESH = pl.DeviceIdType.MESH

ADAM_LR = 0.001
ADAM_B1 = 0.9
ADAM_B2 = 0.999
ADAM_EPS = 1e-08
ADAM_WD = 0.01
ADAM_STEP = 10

VMEM_LIMIT = 60 * 1024 * 1024
ANY = pl.BlockSpec(memory_space=pl.ANY)

SMALL = (("pre_mix_norm", 1024), ("sgu_ln_gain", 512), ("sgu_ln_bias", 512), ("sgu_w_spatial", 8 * 128 * 128),
         ("sgu_b_spatial", 1024), ("attn_out_norm", 512), ("sgu_out_norm", 512), ("post_mix_norm", 1024),
         ("pre_ffn_norm", 1024), ("post_ffn_norm", 1024))
SMALL_ROWS = 1152


def _params(sem=("arbitrary",)):
    return pltpu.CompilerParams(dimension_semantics=sem, vmem_limit_bytes=VMEM_LIMIT)


def _dot(a, b):
    return jnp.dot(a, b, preferred_element_type=F32)


def _dot_nt(a, b):
    return lax.dot_general(a, b, (((1,), (1,)), ((), ())), preferred_element_type=F32)


def _dot_tn(a, b):
    return lax.dot_general(a, b, (((0,), (0,)), ((), ())), preferred_element_type=F32)


def _rms(z):
    return lax.rsqrt(jnp.mean(z * z, axis=-1, keepdims=True) + RMS_EPS)


def _rms_bwd(z, gain, d):
    r = _rms(z)
    n = z * r
    dn = d * gain
    dz = r * (dn - n * jnp.mean(dn * n, axis=-1, keepdims=True))
    return dz, jnp.sum(d * n, axis=0, keepdims=True)


def _gelu(z):
    return 0.5 * z * (1.0 + lax.erf(z * np.float32(1.0 / np.sqrt(2.0))))


def _gelu_grad(z):
    cdf = 0.5 * (1.0 + lax.erf(z * np.float32(1.0 / np.sqrt(2.0))))
    return cdf + z * jnp.exp(-0.5 * z * z) * np.float32(1.0 / np.sqrt(2.0 * np.pi))


def _rot_tables(pos_col, invf, ma, mb):
    ang = pos_col.astype(F32) * invf
    s = jnp.sin(ang)
    return jnp.cos(ang), s * ma, s * mb


def _rot(t, c, sa, sb):
    return t * c + pltpu.roll(t, 120, 1) * sa + pltpu.roll(t, 8, 1) * sb


def _rot_t(d, c, sa, sb):
    return d * c + pltpu.roll(d * sa, 8, 1) + pltpu.roll(d * sb, 120, 1)


def _rot_consts():
    lane = np.arange(128) % HEAD_DIM
    inv_freq = (np.float32(ROPE_THETA) ** (-np.arange(0, ROT_DIM, 2, dtype=np.float32) / np.float32(ROT_DIM))).astype(np.float32)
    invf = np.where(lane < ROT_DIM, inv_freq[lane % ROT_HALF], 0.0).astype(np.float32)
    ma = np.where(lane < ROT_HALF, -1.0, 0.0).astype(np.float32)
    mb = np.where((lane >= ROT_HALF) & (lane < ROT_DIM), 1.0, 0.0).astype(np.float32)
    return jnp.asarray(invf[None]), jnp.asarray(ma[None]), jnp.asarray(mb[None])


def _row_spec(tm, w):
    return pl.BlockSpec((tm, w), lambda i: (i, 0))


def _full_spec(shape):
    return pl.BlockSpec(shape, lambda i: (0,) * len(shape))


def in_proj(x, pos_col, g1, w_in_t, rot, exchanges=()):
    tm = 512

    def body(x_ref, pos_ref, g_ref, w_ref, invf_ref, ma_ref, mb_ref, h_ref, q_ref, k_ref, v_ref, u_ref, vs_ref):
        xf = x_ref[...]
        h = (xf * _rms(xf) * g_ref[...]).astype(BF16)
        h_ref[...] = h
        proj = _dot_nt(h, w_ref[...])
        c, sa, sb = _rot_tables(pos_ref[...], invf_ref[...], ma_ref[...], mb_ref[...])
        for j in range(ATTN_W // 128):
            q_ref[:, j * 128:(j + 1) * 128] = _rot(proj[:, j * 128:(j + 1) * 128], c, sa, sb) * Q_SCALE
            k_ref[:, j * 128:(j + 1) * 128] = _rot(proj[:, ATTN_W + j * 128:ATTN_W + (j + 1) * 128], c, sa, sb)
        v_ref[...] = proj[:, 2 * ATTN_W:3 * ATTN_W]
        u_ref[...] = proj[:, 3 * ATTN_W:3 * ATTN_W + SGU_W]
        vs_ref[...] = proj[:, 3 * ATTN_W + SGU_W:]

    act = jax.ShapeDtypeStruct((SEQ, 512), F32)
    return _hosted(
        "in_proj", body, SEQ // tm,
        [_row_spec(tm, D_MODEL), _row_spec(tm, 1), _full_spec((1, D_MODEL)), _full_spec((IN_W, D_MODEL)),
         _full_spec((1, 128)), _full_spec((1, 128)), _full_spec((1, 128))],
        [_row_spec(tm, D_MODEL)] + [_row_spec(tm, 512)] * 5,
        [jax.ShapeDtypeStruct((SEQ, D_MODEL), BF16)] + [act] * 5,
        (x, pos_col, g1, w_in_t, *rot), exchanges=exchanges)


RES = 16


def _to_residue_order(t):
    return t.reshape(SEQ // RES, RES, -1).transpose(1, 0, 2).reshape(t.shape)


def _from_residue_order(t):
    return t.reshape(RES, SEQ // RES, -1).transpose(1, 0, 2).reshape(t.shape)


def _block_rows(d, r, n):
    if d == 16:
        slices = [(128 * r, 128)]
    elif d == 4:
        slices = [(128 * (4 * b + r) + 32 * n, 32) for b in range(4)]
    else:
        slices = [(128 * b + 8 * n, 8) for b in range(RES)]
    return [(s if isinstance(s, int) else pl.multiple_of(s, z), z) for s, z in slices]


def _block_step(d, i):
    if d == 16:
        return i
    if d == 4:
        return 4 * (i & 31) + (i >> 5)
    return 16 * (i & 7) + (i >> 3)


def _attn_masks(d):
    row2 = _block_step(d, lax.broadcasted_iota(jnp.int32, (128, 256), 0))
    col2 = lax.broadcasted_iota(jnp.int32, (128, 256), 1)
    key2 = _block_step(d, col2 & 127)
    mask2 = jnp.logical_or(jnp.logical_and(col2 < 128, key2 >= row2), jnp.logical_and(col2 >= 128, key2 <= row2))
    row1 = _block_step(d, lax.broadcasted_iota(jnp.int32, (128, 128), 0))
    col1 = lax.broadcasted_iota(jnp.int32, (128, 128), 1)
    return col1 < HEAD_DIM, _block_step(d, col1) <= row1, mask2


def _load_rows(ref, slices):
    parts = [ref[pl.ds(s, z), :] for s, z in slices]
    return parts[0] if len(parts) == 1 else jnp.concatenate(parts, axis=0)


def _for_each_group(fn):
    for p, d in enumerate(DILATIONS):
        masks = _attn_masks(d)
        if d == 16:
            def group(i, carry, p=p, masks=masks):
                fn(p, masks, [(_block_rows(16, 4 * i + g, 0), None) for g in range(4)])
                return carry

            lax.fori_loop(0, 4, group, 0)
        elif d == 4:
            fn(p, masks, [(_block_rows(4, r, 0), None) for r in range(4)])

            def group(i, carry, p=p, masks=masks):
                fn(p, masks, [(_block_rows(4, r, i + 1), _block_rows(4, r, i)) for r in range(4)])
                return carry

            lax.fori_loop(0, 3, group, 0)
        else:
            fn(p, masks, [(_block_rows(1, 0, 0), None)])

            def group(i, carry, p=p, masks=masks):
                fn(p, masks, [(_block_rows(1, 0, 3 * i + g + 1), _block_rows(1, 0, 3 * i + g)) for g in range(3)])
                return carry

            lax.fori_loop(0, 5, group, 0)


def attn_fwd(q, k, v, exchanges=()):
    def body(q_ref, k_ref, v_ref, o_ref, lse_ref, op_ref, lp_ref):
        def group(p, masks, blocks):
            head0, mask1, mask2 = masks
            heads = (head0, jnp.logical_not(head0))
            keys = [rows if prev is None else prev + rows for rows, prev in blocks]
            mask = [mask1 if prev is None else mask2 for _, prev in blocks]
            qb = [_load_rows(q_ref, rows) for rows, _ in blocks]
            kk = [_load_rows(k_ref, ks).astype(BF16) for ks in keys]
            vv = [_load_rows(v_ref, ks).astype(BF16) for ks in keys]
            chains = [(g, hm) for g in range(len(blocks)) for hm in heads]
            s = [jnp.where(mask[g], _dot_nt(jnp.where(hm, qb[g], 0.0).astype(BF16), kk[g]), NEG) for g, hm in chains]
            m = [jnp.max(t, axis=-1, keepdims=True) for t in s]
            e = [jnp.exp(t - mt) for t, mt in zip(s, m)]
            l = [jnp.sum(t, axis=-1, keepdims=True) for t in e]
            pv = [_dot(t.astype(BF16), vv[g]) for t, (g, _) in zip(e, chains)]
            for g, (rows, _) in enumerate(blocks):
                o_blk = jnp.where(head0, pv[2 * g] / l[2 * g], pv[2 * g + 1] / l[2 * g + 1])
                l_blk = jnp.where(head0, jnp.broadcast_to(m[2 * g] + jnp.log(l[2 * g]), (128, 128)),
                                  jnp.broadcast_to(m[2 * g + 1] + jnp.log(l[2 * g + 1]), (128, 128)))
                at = 0
                for start, size in rows:
                    op_ref[p, pl.ds(start, size), :] = o_blk[at:at + size]
                    lp_ref[p, pl.ds(start, size), :] = l_blk[at:at + size]
                    at += size

        _for_each_group(group)

        def combine(i, carry):
            rows = pl.ds(pl.multiple_of(i * 256, 256), 256)
            ls = [lp_ref[p, rows, :] for p in range(3)]
            m = jnp.maximum(jnp.maximum(ls[0], ls[1]), ls[2])
            lse = m + jnp.log(jnp.exp(ls[0] - m) + jnp.exp(ls[1] - m) + jnp.exp(ls[2] - m))
            o = jnp.zeros((256, 128), F32)
            for p in range(3):
                o = o + jnp.exp(ls[p] - lse) * op_ref[p, rows, :]
            o_ref[rows, :] = o
            lse_ref[rows, :] = lse
            return carry

        lax.fori_loop(0, SEQ // 256, combine, 0)

    slab = pl.BlockSpec((SEQ, 128), lambda i: (0, i))
    out = jax.ShapeDtypeStruct((SEQ, ATTN_W), F32)
    return _hosted(
        "attn_fwd", body, ATTN_W // 128, [slab] * 3, [slab] * 2, [out, out], (q, k, v),
        scratch_shapes=[pltpu.VMEM((3, SEQ, 128), F32), pltpu.VMEM((3, SEQ, 128), F32)], exchanges=exchanges)


def _causal_weights(w_ref):
    row = lax.broadcasted_iota(jnp.int32, (CHUNK, CHUNK), 0)
    col = lax.broadcasted_iota(jnp.int32, (CHUNK, CHUNK), 1)
    return [jnp.where(col <= row, w_ref[g], 0.0).astype(BF16) for g in range(N_GROUPS)], col <= row


def _sgu_chunk_fwd(u, vs, lg, lb, wc, bfull, head0):
    ug = _gelu(u)
    vg = _gelu(vs)
    xc = vg - jnp.mean(vg, axis=-1, keepdims=True)
    rstd = lax.rsqrt(jnp.mean(xc * xc, axis=-1, keepdims=True) + LN_EPS)
    xhat = xc * rstd
    vn = xhat * lg + lb
    mixed = []
    for gp in range(SGU_W // 128):
        vp = vn[:, gp * 128:(gp + 1) * 128].astype(BF16)
        mixed.append(jnp.where(head0, _dot(wc[2 * gp], vp), _dot(wc[2 * gp + 1], vp)))
    ms = jnp.concatenate(mixed, axis=1) + bfull
    return ug, xhat, rstd, vn, ms


def sgu_fwd(u, vs, lg, lb, w_sp, bfull, exchanges=()):
    cpb = 4

    def body(u_ref, vs_ref, lg_ref, lb_ref, w_ref, b_ref, o_ref):
        wc, _ = _causal_weights(w_ref)
        head0 = lax.broadcasted_iota(jnp.int32, (CHUNK, 128), 1) < HEAD_DIM
        for ci in range(cpb):
            rows = pl.ds(ci * CHUNK, CHUNK)
            ug, _, _, _, ms = _sgu_chunk_fwd(u_ref[rows, :], vs_ref[rows, :], lg_ref[...], lb_ref[...], wc, b_ref[...], head0)
            o_ref[rows, :] = ug * ms

    tm = cpb * CHUNK
    return _hosted(
        "sgu_fwd", body, SEQ // tm,
        [_row_spec(tm, SGU_W), _row_spec(tm, SGU_W), _full_spec((1, SGU_W)), _full_spec((1, SGU_W)),
         _full_spec((N_GROUPS, CHUNK, CHUNK)), _full_spec((CHUNK, SGU_W))],
        [_row_spec(tm, SGU_W)], [jax.ShapeDtypeStruct((SEQ, SGU_W), F32)],
        (u, vs, lg, lb, w_sp, bfull), exchanges=exchanges)


def out_proj(attn, sgu, x, ga, gs, w_out, gpm, gpf):
    tm = 512

    def body(a_ref, s_ref, x_ref, ga_ref, gs_ref, w_ref, gpm_ref, gpf_ref, mix_ref, y_ref, x2_ref, h2_ref):
        a = a_ref[...]
        s = s_ref[...]
        an = (a * _rms(a) * ga_ref[...]).astype(BF16)
        sn = (s * _rms(s) * gs_ref[...]).astype(BF16)
        mix_ref[:, :ATTN_W] = an
        mix_ref[:, ATTN_W:] = sn
        y = _dot(an, w_ref[:ATTN_W, :]) + _dot(sn, w_ref[ATTN_W:, :])
        y_ref[...] = y
        x2 = x_ref[...] + y * _rms(y) * gpm_ref[...]
        x2_ref[...] = x2
        h2_ref[...] = (x2 * _rms(x2) * gpf_ref[...]).astype(BF16)

    wide = jax.ShapeDtypeStruct((SEQ, D_MODEL), F32)
    wide16 = jax.ShapeDtypeStruct((SEQ, D_MODEL), BF16)
    return pl.pallas_call(
        body, name="out_proj", grid=(SEQ // tm,),
        in_specs=[_row_spec(tm, ATTN_W), _row_spec(tm, SGU_W), _row_spec(tm, D_MODEL), _full_spec((1, ATTN_W)),
                  _full_spec((1, SGU_W)), _full_spec((D_MODEL, D_MODEL)), _full_spec((1, D_MODEL)), _full_spec((1, D_MODEL))],
        out_specs=[_row_spec(tm, D_MODEL)] * 4, out_shape=[wide16, wide, wide, wide16],
        compiler_params=_params(),
    )(attn, sgu, x, ga, gs, w_out, gpm, gpf)


def ffn_up(h2, w_gate_t, w_up_t):
    tm = 256

    def body(h_ref, wg_ref, wu_ref, g_ref, u_ref, a_ref):
        h = h_ref[...]
        g = _dot_nt(h, wg_ref[...])
        u = _dot_nt(h, wu_ref[...])
        g_ref[...] = g
        u_ref[...] = u
        a_ref[...] = (g * jax.nn.sigmoid(g) * u).astype(BF16)

    ff = jax.ShapeDtypeStruct((SEQ, D_FF), F32)
    return pl.pallas_call(
        body, name="ffn_up", grid=(SEQ // tm,),
        in_specs=[_row_spec(tm, D_MODEL), _full_spec((D_FF, D_MODEL)), _full_spec((D_FF, D_MODEL))],
        out_specs=[_row_spec(tm, D_FF)] * 3, out_shape=[ff, ff, jax.ShapeDtypeStruct((SEQ, D_FF), BF16)],
        compiler_params=_params(),
    )(h2, w_gate_t, w_up_t)


def ffn_down_loss(act, w_down, x2, gpo, target):
    tm = 512

    def body(a_ref, w_ref, x2_ref, g_ref, t_ref, df_ref, dx3_ref, dg_ref, loss_ref):
        f = _dot(a_ref[...], w_ref[...])
        gain = g_ref[...]
        err = x2_ref[...] + f * _rms(f) * gain - t_ref[...]
        dx3 = err * np.float32(1.0 / D_MODEL)
        dx3_ref[...] = dx3
        df, dg = _rms_bwd(f, gain, dx3)
        df_ref[...] = df.astype(BF16)

        @pl.when(pl.program_id(0) == 0)
        def _():
            dg_ref[...] = jnp.zeros_like(dg_ref)
            loss_ref[...] = jnp.zeros_like(loss_ref)

        dg_ref[...] += dg
        loss_ref[...] += jnp.sum(err * err, axis=(0, 1), keepdims=True)

    return pl.pallas_call(
        body, name="ffn_down_loss", grid=(SEQ // tm,),
        in_specs=[_row_spec(tm, D_FF), _full_spec((D_FF, D_MODEL)), _row_spec(tm, D_MODEL), _full_spec((1, D_MODEL)),
                  _row_spec(tm, D_MODEL)],
        out_specs=[_row_spec(tm, D_MODEL), _row_spec(tm, D_MODEL), _full_spec((1, D_MODEL)), _full_spec((1, 1))],
        out_shape=[jax.ShapeDtypeStruct((SEQ, D_MODEL), BF16), jax.ShapeDtypeStruct((SEQ, D_MODEL), F32),
                   jax.ShapeDtypeStruct((1, D_MODEL), F32), jax.ShapeDtypeStruct((1, 1), F32)],
        compiler_params=_params(),
    )(act, w_down, x2, gpo, target)


def ffn_act_bwd(df, w_down, gate, up, exchanges=()):
    tm = 256

    def body(df_ref, w_ref, g_ref, u_ref, dg_ref, du_ref):
        dact = _dot_nt(df_ref[...], w_ref[...])
        g = g_ref[...]
        s = jax.nn.sigmoid(g)
        du_ref[...] = (dact * g * s).astype(BF16)
        dg_ref[...] = (dact * u_ref[...] * (s * (1.0 + g * (1.0 - s)))).astype(BF16)

    ff16 = jax.ShapeDtypeStruct((SEQ, D_FF), BF16)
    return _hosted(
        "ffn_act_bwd", body, SEQ // tm,
        [_row_spec(tm, D_MODEL), _full_spec((D_FF, D_MODEL)), _row_spec(tm, D_FF), _row_spec(tm, D_FF)],
        [_row_spec(tm, D_FF)] * 2, [ff16, ff16], (df, w_down, gate, up), exchanges=exchanges)


def ffn_in_bwd(dgate, dup, w_gate_t, w_up_t, x2, gpf, dx3, y, gpm, exchanges=()):
    tm = 256

    def body(dg_ref, du_ref, wg_ref, wu_ref, x2_ref, gpf_ref, dx3_ref, y_ref, gpm_ref, dx2_ref, dy_ref, dgpf_ref, dgpm_ref):
        dh2 = _dot(dg_ref[...], wg_ref[...]) + _dot(du_ref[...], wu_ref[...])
        dz, dgpf = _rms_bwd(x2_ref[...], gpf_ref[...], dh2)
        dx2 = dx3_ref[...] + dz
        dx2_ref[...] = dx2
        dy, dgpm = _rms_bwd(y_ref[...], gpm_ref[...], dx2)
        dy_ref[...] = dy.astype(BF16)

        @pl.when(pl.program_id(0) == 0)
        def _():
            dgpf_ref[...] = jnp.zeros_like(dgpf_ref)
            dgpm_ref[...] = jnp.zeros_like(dgpm_ref)

        dgpf_ref[...] += dgpf
        dgpm_ref[...] += dgpm

    vec = jax.ShapeDtypeStruct((1, D_MODEL), F32)
    return _hosted(
        "ffn_in_bwd", body, SEQ // tm,
        [_row_spec(tm, D_FF), _row_spec(tm, D_FF), _full_spec((D_FF, D_MODEL)), _full_spec((D_FF, D_MODEL)),
         _row_spec(tm, D_MODEL), _full_spec((1, D_MODEL)), _row_spec(tm, D_MODEL), _row_spec(tm, D_MODEL),
         _full_spec((1, D_MODEL))],
        [_row_spec(tm, D_MODEL), _row_spec(tm, D_MODEL), _full_spec((1, D_MODEL)), _full_spec((1, D_MODEL))],
        [jax.ShapeDtypeStruct((SEQ, D_MODEL), F32), jax.ShapeDtypeStruct((SEQ, D_MODEL), BF16), vec, vec],
        (dgate, dup, w_gate_t, w_up_t, x2, gpf, dx3, y, gpm), exchanges=exchanges)


def weight_grad(name, a, b, exchanges=()):
    m, n = a.shape[1], b.shape[1]
    tr = 256

    def body(a_ref, b_ref, o_ref):
        o_ref[...] = _dot_tn(a_ref[...], b_ref[...]).astype(BF16)

    (out,), done = _hosted(
        name, body, m // tr, [pl.BlockSpec((SEQ, tr), lambda i: (0, i)), _full_spec((SEQ, n))],
        [_row_spec(tr, n)], [jax.ShapeDtypeStruct((m, n), BF16)], (a, b), exchanges=exchanges)
    return out.reshape(N_DEV, m // N_DEV, n), done


def mix_bwd(dy, w_out, attn, sgu, ga, gs, exchanges=()):
    tm = 512

    def body(dy_ref, w_ref, a_ref, s_ref, ga_ref, gs_ref, da_ref, ds_ref, dga_ref, dgs_ref):
        dy = dy_ref[...]
        da, dga = _rms_bwd(a_ref[...], ga_ref[...], _dot_nt(dy, w_ref[:ATTN_W, :]))
        ds, dgs = _rms_bwd(s_ref[...], gs_ref[...], _dot_nt(dy, w_ref[ATTN_W:, :]))
        da_ref[...] = da
        ds_ref[...] = ds

        @pl.when(pl.program_id(0) == 0)
        def _():
            dga_ref[...] = jnp.zeros_like(dga_ref)
            dgs_ref[...] = jnp.zeros_like(dgs_ref)

        dga_ref[...] += dga
        dgs_ref[...] += dgs

    half = jax.ShapeDtypeStruct((SEQ, 512), F32)
    vec = jax.ShapeDtypeStruct((1, 512), F32)
    return _hosted(
        "mix_bwd", body, SEQ // tm,
        [_row_spec(tm, D_MODEL), _full_spec((D_MODEL, D_MODEL)), _row_spec(tm, 512), _row_spec(tm, 512),
         _full_spec((1, 512)), _full_spec((1, 512))],
        [_row_spec(tm, 512), _row_spec(tm, 512), _full_spec((1, 512)), _full_spec((1, 512))],
        [half, half, vec, vec], (dy, w_out, attn, sgu, ga, gs), exchanges=exchanges)


def sgu_bwd(u, vs, dsgu, lg, lb, w_sp, bfull):
    cpb = 4

    def body(u_ref, vs_ref, d_ref, lg_ref, lb_ref, w_ref, b_ref, du_ref, dvs_ref, dlg_ref, dlb_ref, dw_ref, db_ref):
        wc, causal = _causal_weights(w_ref)
        head0 = lax.broadcasted_iota(jnp.int32, (CHUNK, 128), 1) < HEAD_DIM
        lg = lg_ref[...]

        @pl.when(pl.program_id(0) == 0)
        def _():
            dlg_ref[...] = jnp.zeros_like(dlg_ref)
            dlb_ref[...] = jnp.zeros_like(dlb_ref)
            dw_ref[...] = jnp.zeros_like(dw_ref)
            db_ref[...] = jnp.zeros_like(db_ref)

        for ci in range(cpb):
            rows = pl.ds(ci * CHUNK, CHUNK)
            u = u_ref[rows, :]
            vs = vs_ref[rows, :]
            d = d_ref[rows, :]
            ug, xhat, rstd, vn, ms = _sgu_chunk_fwd(u, vs, lg, lb_ref[...], wc, b_ref[...], head0)
            du_ref[rows, :] = (d * ms * _gelu_grad(u)).astype(BF16)
            dms = d * ug
            db_ref[...] += dms
            dvn = []
            for gp in range(SGU_W // 128):
                dmp = dms[:, gp * 128:(gp + 1) * 128]
                dm0 = jnp.where(head0, dmp, 0.0).astype(BF16)
                dm1 = jnp.where(head0, 0.0, dmp).astype(BF16)
                vp = vn[:, gp * 128:(gp + 1) * 128].astype(BF16)
                dw_ref[2 * gp] += _dot_nt(dm0, vp)
                dw_ref[2 * gp + 1] += _dot_nt(dm1, vp)
                dvn.append(_dot_tn(wc[2 * gp], dm0) + _dot_tn(wc[2 * gp + 1], dm1))
            dvn = jnp.concatenate(dvn, axis=1)
            dlg_ref[...] += jnp.sum(dvn * xhat, axis=0, keepdims=True)
            dlb_ref[...] += jnp.sum(dvn, axis=0, keepdims=True)
            dxh = dvn * lg
            dvg = rstd * (dxh - jnp.mean(dxh, axis=-1, keepdims=True) - xhat * jnp.mean(dxh * xhat, axis=-1, keepdims=True))
            dvs_ref[rows, :] = (dvg * _gelu_grad(vs)).astype(BF16)

        @pl.when(pl.program_id(0) == pl.num_programs(0) - 1)
        def _():
            for g in range(N_GROUPS):
                dw_ref[g] = jnp.where(causal, dw_ref[g], 0.0)

    tm = cpb * CHUNK
    half16 = jax.ShapeDtypeStruct((SEQ, SGU_W), BF16)
    vec = jax.ShapeDtypeStruct((1, SGU_W), F32)
    return pl.pallas_call(
        body, name="sgu_bwd", grid=(SEQ // tm,),
        in_specs=[_row_spec(tm, SGU_W)] * 3 + [_full_spec((1, SGU_W)), _full_spec((1, SGU_W)),
                                                  _full_spec((N_GROUPS, CHUNK, CHUNK)), _full_spec((CHUNK, SGU_W))],
        out_specs=[_row_spec(tm, SGU_W), _row_spec(tm, SGU_W), _full_spec((1, SGU_W)), _full_spec((1, SGU_W)),
                   _full_spec((N_GROUPS, CHUNK, CHUNK)), _full_spec((CHUNK, SGU_W))],
        out_shape=[half16, half16, vec, vec, jax.ShapeDtypeStruct((N_GROUPS, CHUNK, CHUNK), F32),
                   jax.ShapeDtypeStruct((CHUNK, SGU_W), F32)],
        compiler_params=_params(),
    )(u, vs, dsgu, lg, lb, w_sp, bfull)


def attn_bwd(q, k, v, o, lse, do, pos_col, rot, exchanges=()):
    def body(q_ref, k_ref, v_ref, o_ref, lse_ref, do_ref, pos_ref, invf_ref, ma_ref, mb_ref,
             dq_ref, dk_ref, dv_ref, dqa_ref, dka_ref, dva_ref, dlt_ref):
        dqa_ref[...] = jnp.zeros_like(dqa_ref)
        dka_ref[...] = jnp.zeros_like(dka_ref)
        dva_ref[...] = jnp.zeros_like(dva_ref)

        def delta(i, carry):
            rows = pl.ds(pl.multiple_of(i * 256, 256), 256)
            prod = do_ref[rows, :] * o_ref[rows, :]
            h0 = lax.broadcasted_iota(jnp.int32, (256, 128), 1) < HEAD_DIM
            d0 = jnp.sum(jnp.where(h0, prod, 0.0), axis=-1, keepdims=True)
            d1 = jnp.sum(jnp.where(h0, 0.0, prod), axis=-1, keepdims=True)
            dlt_ref[rows, :] = jnp.where(h0, d0, d1)
            return carry

        lax.fori_loop(0, SEQ // 256, delta, 0)

        def add_rows(ref, slices, val):
            at = 0
            for start, size in slices:
                ref[pl.ds(start, size), :] += val[at:at + size]
                at += size

        def group(p, masks, blocks):
            head0, mask1, mask2 = masks
            heads = (head0, jnp.logical_not(head0))
            keys = [rows if prev is None else prev + rows for rows, prev in blocks]
            mask = [mask1 if prev is None else mask2 for _, prev in blocks]
            kk = [_load_rows(k_ref, ks).astype(BF16) for ks in keys]
            vv = [_load_rows(v_ref, ks).astype(BF16) for ks in keys]
            qb = [_load_rows(q_ref, rows) for rows, _ in blocks]
            dob = [_load_rows(do_ref, rows) for rows, _ in blocks]
            lse_b = [_load_rows(lse_ref, rows) for rows, _ in blocks]
            dlt_b = [_load_rows(dlt_ref, rows) for rows, _ in blocks]
            chains = [(g, h) for g in range(len(blocks)) for h in range(2)]
            qm = [jnp.where(heads[h], qb[g], 0.0).astype(BF16) for g, h in chains]
            dom = [jnp.where(heads[h], dob[g], 0.0).astype(BF16) for g, h in chains]
            s = [_dot_nt(qm[c], kk[g]) for c, (g, h) in enumerate(chains)]
            dp = [_dot_nt(dom[c], vv[g]) for c, (g, h) in enumerate(chains)]
            pr = [jnp.where(mask[g], jnp.exp(s[c] - lse_b[g][:, h * HEAD_DIM:h * HEAD_DIM + 1]), 0.0)
                  for c, (g, h) in enumerate(chains)]
            ds = [(pr[c] * (dp[c] - dlt_b[g][:, h * HEAD_DIM:h * HEAD_DIM + 1])).astype(BF16)
                  for c, (g, h) in enumerate(chains)]
            dv = [_dot_tn(pr[c].astype(BF16), dom[c]) for c in range(len(chains))]
            dk = [_dot_tn(ds[c], qm[c]) for c in range(len(chains))]
            dq = [_dot(ds[c], kk[g]) for c, (g, h) in enumerate(chains)]
            for g, (rows, _) in enumerate(blocks):
                add_rows(dqa_ref, rows, jnp.where(head0, dq[2 * g], dq[2 * g + 1]))
                add_rows(dka_ref, keys[g], dk[2 * g] + dk[2 * g + 1])
                add_rows(dva_ref, keys[g], dv[2 * g] + dv[2 * g + 1])

        _for_each_group(group)

        def finish(i, carry):
            rows = pl.ds(pl.multiple_of(i * 256, 256), 256)
            c, sa, sb = _rot_tables(pos_ref[rows, :], invf_ref[...], ma_ref[...], mb_ref[...])
            dq_ref[rows, :] = _rot_t(dqa_ref[rows, :] * Q_SCALE, c, sa, sb).astype(BF16)
            dk_ref[rows, :] = _rot_t(dka_ref[rows, :], c, sa, sb).astype(BF16)
            dv_ref[rows, :] = dva_ref[rows, :].astype(BF16)
            return carry

        lax.fori_loop(0, SEQ // 256, finish, 0)

    slab = pl.BlockSpec((SEQ, 128), lambda i: (0, i))
    out = jax.ShapeDtypeStruct((SEQ, ATTN_W), BF16)
    acc = pltpu.VMEM((SEQ, 128), F32)
    return _hosted(
        "attn_bwd", body, ATTN_W // 128,
        [slab] * 6 + [_full_spec((SEQ, 1)), _full_spec((1, 128)), _full_spec((1, 128)), _full_spec((1, 128))],
        [slab] * 3, [out, out, out], (q, k, v, o, lse, do, pos_col, *rot),
        scratch_shapes=[acc, acc, acc, acc], exchanges=exchanges)


def in_bwd(dproj, w_in_t, x, g1, dx2, exchanges=()):
    tm = 512

    def body(dp_ref, w_ref, x_ref, g_ref, dx2_ref, dx_ref, dg_ref):
        dh1 = _dot(dp_ref[...], w_ref[...])
        dz, dg = _rms_bwd(x_ref[...], g_ref[...], dh1)
        dx_ref[...] = dx2_ref[...] + dz

        @pl.when(pl.program_id(0) == 0)
        def _():
            dg_ref[...] = jnp.zeros_like(dg_ref)

        dg_ref[...] += dg

    return _hosted(
        "in_bwd", body, SEQ // tm,
        [_row_spec(tm, IN_W), _full_spec((IN_W, D_MODEL)), _row_spec(tm, D_MODEL), _full_spec((1, D_MODEL)),
         _row_spec(tm, D_MODEL)],
        [_row_spec(tm, D_MODEL), _full_spec((1, D_MODEL))],
        [jax.ShapeDtypeStruct((SEQ, D_MODEL), F32), jax.ShapeDtypeStruct((1, D_MODEL), F32)],
        (dproj, w_in_t, x, g1, dx2), exchanges=exchanges)


def _coords():
    return lax.axis_index("x"), lax.axis_index("y"), lax.axis_index("c")


class Exchange:
    def __init__(self, srcs, bufs, new_shapes, n_sems, make):
        self.srcs, self.bufs, self.new_shapes, self.n_sems, self.make = list(srcs), list(bufs), list(new_shapes), n_sems, make


def _hosted(name, body, n_steps, in_specs, out_specs, out_shape, args, scratch_shapes=(), exchanges=(), prefetch=None):
    out_shape, out_specs = list(out_shape), list(out_specs)
    srcs = [a for ex in exchanges for a in ex.srcs]
    bufs = [a for ex in exchanges for a in ex.bufs]
    news = [s for ex in exchanges for s in ex.new_shapes]
    n_pre = 0 if prefetch is None else 1
    n_in, n_out, n_scr = len(args), len(out_shape), len(scratch_shapes)

    def wrapped(*refs):
        refs = list(refs)
        pre, refs = refs[:n_pre], refs[n_pre:]
        ins = refs[:n_in]
        src_refs = refs[n_in:n_in + len(srcs)]
        at = n_in + len(srcs) + len(bufs)
        outs = refs[at:at + n_out]
        buf_refs = refs[at + n_out:at + n_out + len(bufs)]
        new_refs = refs[at + n_out + len(bufs):at + n_out + len(bufs) + len(news)]
        at += n_out + len(bufs) + len(news)
        scratch, sems = refs[at:at + n_scr], refs[at + n_scr:]

        def copies():
            made, si, bi, ni = [], 0, 0, 0
            for k, ex in enumerate(exchanges):
                made.append(ex.make(src_refs[si:si + len(ex.srcs)], buf_refs[bi:bi + len(ex.bufs)],
                                    new_refs[ni:ni + len(ex.new_shapes)], sems[2 * k], sems[2 * k + 1]))
                si, bi, ni = si + len(ex.srcs), bi + len(ex.bufs), ni + len(ex.new_shapes)
            return made

        if exchanges:
            @pl.when(pl.program_id(0) == 0)
            def _():
                for starts, _, _ in copies():
                    for cp in starts:
                        cp.start()

        body(*pre, *ins, *outs, *scratch)

        if exchanges:
            @pl.when(pl.program_id(0) == n_steps - 1)
            def _():
                for _, sends, recvs in copies():
                    for cp in sends:
                        cp.wait_send()
                    for cp in recvs:
                        cp.wait_recv()

    sem_shapes = []
    for ex in exchanges:
        sem_shapes += [pltpu.SemaphoreType.DMA((ex.n_sems,)), pltpu.SemaphoreType.DMA((ex.n_sems,))]
    all_in = list(in_specs) + [ANY] * (len(srcs) + len(bufs))
    all_out = out_specs + [ANY] * (len(bufs) + len(news))
    all_shape = out_shape + [jax.ShapeDtypeStruct(b.shape, b.dtype) for b in bufs] + news
    all_scratch = list(scratch_shapes) + sem_shapes
    aliases = {n_pre + n_in + len(srcs) + i: n_out + i for i in range(len(bufs))}
    if prefetch is None:
        call = pl.pallas_call(wrapped, name=name, grid=(n_steps,), in_specs=all_in, out_specs=all_out, out_shape=all_shape,
                              scratch_shapes=all_scratch, input_output_aliases=aliases, compiler_params=_params())
        outs = call(*args, *srcs, *bufs)
    else:
        spec = pltpu.PrefetchScalarGridSpec(num_scalar_prefetch=1, grid=(n_steps,), in_specs=all_in, out_specs=all_out,
                                            scratch_shapes=all_scratch)
        call = pl.pallas_call(wrapped, name=name, grid_spec=spec, out_shape=all_shape, input_output_aliases=aliases,
                              compiler_params=_params())
        outs = call(prefetch, *args, *srcs, *bufs)
    results, bi, ni = [], n_out, n_out + len(bufs)
    for ex in exchanges:
        results.append((list(outs[bi:bi + len(ex.bufs)]), list(outs[ni:ni + len(ex.new_shapes)])))
        bi, ni = bi + len(ex.bufs), ni + len(ex.new_shapes)
    return list(outs[:n_out]), results


def _gather_copies(bufs, send_sems, recv_sems):
    x, y, c = _coords()
    me, sibling = (x, y, c), (x, y, 1 - c)
    chips = [(1 - x, y), (x, 1 - y), (1 - x, 1 - y)]

    def copy(a, k, block, to):
        r = bufs[a].shape[0] // N_DEV
        rows = bufs[a].at[pl.ds((4 * block[0] + 2 * block[1] + block[2]) * r, r), :]
        return pltpu.make_async_remote_copy(src_ref=rows, dst_ref=rows, send_sem=send_sems.at[7 * a + k],
                                            recv_sem=recv_sems.at[7 * a + k], device_id=to, device_id_type=MESH)

    first, first_in, passed, passed_in = [], [], [], []
    for a in range(len(bufs)):
        first.append(copy(a, 0, me, sibling))
        first_in.append(copy(a, 0, sibling, me))
        for j, chip in enumerate(chips):
            first.append(copy(a, 1 + j, me, (*chip, c)))
            first_in.append(copy(a, 1 + j, (*chip, c), me))
            passed.append(copy(a, 4 + j, (*chip, c), sibling))
            passed_in.append(copy(a, 4 + j, (*chip, 1 - c), me))
    return first, first_in, passed, passed_in


def gather_send(bufs):
    def make(src_refs, buf_refs, new_refs, send_sems, recv_sems):
        first, first_in, _, _ = _gather_copies(buf_refs, send_sems, recv_sems)
        return first, first, first_in

    return Exchange([], bufs, [], 7 * len(bufs), make)


def gather_pass_on(bufs):
    def make(src_refs, buf_refs, new_refs, send_sems, recv_sems):
        _, _, passed, passed_in = _gather_copies(buf_refs, send_sems, recv_sems)
        return passed, passed, passed_in

    return Exchange([], bufs, [], 7 * len(bufs), make)


def all_gather_in_place(name, bufs):
    n = len(bufs)

    def body(*refs):
        outs, send_sems, recv_sems = refs[n:2 * n], refs[2 * n], refs[2 * n + 1]
        first, first_in, passed, passed_in = _gather_copies(outs, send_sems, recv_sems)
        for cp in first:
            cp.start()
        for cp in first_in:
            cp.wait_recv()
        for cp in passed:
            cp.start()
        for cp in passed_in:
            cp.wait_recv()
        for cp in first + passed:
            cp.wait_send()

    return pl.pallas_call(
        body, name=name, in_specs=[ANY] * n, out_specs=[ANY] * n,
        out_shape=[jax.ShapeDtypeStruct(b.shape, b.dtype) for b in bufs],
        scratch_shapes=[pltpu.SemaphoreType.DMA((7 * n,)), pltpu.SemaphoreType.DMA((7 * n,))],
        input_output_aliases={i: i for i in range(n)},
    )(*bufs)


def place_shards(shards, dev):
    n = len(shards)

    def body(dev_ref, *refs):
        for a in range(n):
            refs[n + a][...] = refs[a][...].astype(BF16)

    spec = pltpu.PrefetchScalarGridSpec(
        num_scalar_prefetch=1, grid=(1,),
        in_specs=[pl.BlockSpec(s.shape, lambda i, dev_ref: (0, 0)) for s in shards],
        out_specs=[pl.BlockSpec(s.shape, lambda i, dev_ref: (dev_ref[0], 0)) for s in shards])
    return pl.pallas_call(
        body, name="place_shards", grid_spec=spec,
        out_shape=[jax.ShapeDtypeStruct((N_DEV * s.shape[0], s.shape[1]), BF16) for s in shards],
        compiler_params=_params(),
    )(dev, *shards)


def to_sibling(grads):
    def make(src_refs, buf_refs, new_refs, send_sems, recv_sems):
        x, y, c = _coords()
        copies = [pltpu.make_async_remote_copy(
            src_ref=src_refs[a].at[2 * xy + 1 - c], dst_ref=new_refs[a].at[xy], send_sem=send_sems.at[4 * a + xy],
            recv_sem=recv_sems.at[4 * a + xy], device_id=(x, y, 1 - c), device_id_type=MESH)
            for a in range(len(src_refs)) for xy in range(4)]
        return copies, copies, copies

    return Exchange(grads, [], [jax.ShapeDtypeStruct((4,) + g.shape[1:], g.dtype) for g in grads], 4 * len(grads), make)


def to_chips(parts):
    def make(src_refs, buf_refs, new_refs, send_sems, recv_sems):
        x, y, c = _coords()
        chips = [(1 - x, y), (x, 1 - y), (1 - x, 1 - y)]
        copies = [pltpu.make_async_remote_copy(
            src_ref=src_refs[a].at[2 * px + py], dst_ref=new_refs[a].at[2 * x + y], send_sem=send_sems.at[3 * a + j],
            recv_sem=recv_sems.at[3 * a + j], device_id=(px, py, c), device_id_type=MESH)
            for a in range(len(src_refs)) for j, (px, py) in enumerate(chips)]
        return copies, copies, copies

    return Exchange(parts, [], [jax.ShapeDtypeStruct(p.shape, p.dtype) for p in parts], 3 * len(parts), make)


def to_owners(grad):
    def make(src_refs, buf_refs, new_refs, send_sems, recv_sems):
        x, y, c = _coords()
        copies = []
        for m in range(1, N_DEV):
            px, py, pc = x ^ (m >> 2), y ^ ((m >> 1) & 1), c ^ (m & 1)
            copies.append(pltpu.make_async_remote_copy(
                src_ref=src_refs[0].at[4 * px + 2 * py + pc], dst_ref=new_refs[0].at[4 * x + 2 * y + c],
                send_sem=send_sems.at[m - 1], recv_sem=recv_sems.at[m - 1], device_id=(px, py, pc), device_id_type=MESH))
        return copies, copies, copies

    return Exchange([grad], [], [jax.ShapeDtypeStruct(grad.shape, grad.dtype)], N_DEV - 1, make)


def exchange_only(name, exchanges):
    def body():
        pass

    return _hosted(name, body, 1, [], [], [], [], exchanges=exchanges)[1]


def sum_cores(name, grad, other, core):
    _, r, w = other.shape

    def body(core_ref, g_ref, o_ref, out_ref):
        out_ref[...] = (g_ref[...].astype(F32) + o_ref[...].astype(F32)).astype(out_ref.dtype)

    return pl.pallas_call(
        body, name=name,
        grid_spec=pltpu.PrefetchScalarGridSpec(
            num_scalar_prefetch=1, grid=(4,),
            in_specs=[pl.BlockSpec((1, r, w), lambda i, core_ref: (2 * i + core_ref[0], 0, 0)),
                      pl.BlockSpec((1, r, w), lambda i, core_ref: (i, 0, 0))],
            out_specs=pl.BlockSpec((1, r, w), lambda i, core_ref: (i, 0, 0))),
        out_shape=jax.ShapeDtypeStruct(other.shape, other.dtype),
        compiler_params=_params(),
    )(core, grad, other)


def sum_owned(name, grad, others, dev_ids):
    _, r, w = grad.shape

    def body(ids_ref, *refs):
        acc = refs[0][0]
        for k in range(1, N_DEV):
            acc = acc + refs[k][0]
        refs[N_DEV][...] = acc

    def pick(k):
        return pl.BlockSpec((1, r, w), lambda i, ids_ref: (ids_ref[k], 0, 0))

    return pl.pallas_call(
        body, name=name,
        grid_spec=pltpu.PrefetchScalarGridSpec(
            num_scalar_prefetch=1, grid=(1,), in_specs=[pick(k) for k in range(N_DEV)],
            out_specs=pl.BlockSpec((r, w), lambda i, ids_ref: (ids_ref[0], 0))),
        out_shape=jax.ShapeDtypeStruct((N_DEV * r, w), F32),
        compiler_params=_params(),
    )(dev_ids, grad, *([others] * (N_DEV - 1)))


def sum_chips(name, part, others, chip_ids):
    _, r, w = part.shape

    def body(ids_ref, p_ref, a_ref, b_ref, c_ref, out_ref):
        acc = p_ref[0].astype(F32) + a_ref[0].astype(F32)
        out_ref[...] = (acc + b_ref[0].astype(F32)) + c_ref[0].astype(F32)

    def pick(k):
        return pl.BlockSpec((1, r, w), lambda i, ids_ref: (ids_ref[k], 0, 0))

    return pl.pallas_call(
        body, name=name,
        grid_spec=pltpu.PrefetchScalarGridSpec(
            num_scalar_prefetch=1, grid=(1,),
            in_specs=[pick(0), pick(1), pick(2), pick(3)],
            out_specs=pl.BlockSpec((r, w), lambda i, ids_ref: (0, 0))),
        out_shape=jax.ShapeDtypeStruct((r, w), F32),
        compiler_params=_params(),
    )(chip_ids, part, others, others, others)


def adamw(name, w, g, m, v):
    c1 = np.float32(1.0 - ADAM_B1 ** ADAM_STEP)
    c2 = np.float32(1.0 - ADAM_B2 ** ADAM_STEP)

    def body(w_ref, g_ref, m_ref, v_ref, d_ref, nm_ref, nv_ref):
        g = g_ref[...]
        nm = ADAM_B1 * m_ref[...] + np.float32(1.0 - ADAM_B1) * g
        nv = ADAM_B2 * v_ref[...] + np.float32(1.0 - ADAM_B2) * (g * g)
        nm_ref[...] = nm
        nv_ref[...] = nv
        d_ref[...] = -ADAM_LR * ((nm / c1) / (jnp.sqrt(nv / c2) + ADAM_EPS) + ADAM_WD * w_ref[...])

    shape = jax.ShapeDtypeStruct(w.shape, F32)
    spec = _full_spec(w.shape)
    return pl.pallas_call(
        body, name=name, grid=(1,), in_specs=[spec] * 4, out_specs=[spec] * 3, out_shape=[shape] * 3,
        compiler_params=_params(),
    )(w, g, m, v)


def _pack_small(parts):
    flat = jnp.concatenate([parts[name].reshape(-1) for name, _ in SMALL])
    return jnp.pad(flat, (0, SMALL_ROWS * 128 - flat.shape[0])).reshape(SMALL_ROWS, 128)


def _unpack_small(packed, like):
    flat = packed.reshape(-1)
    out, at = {}, 0
    for name, size in SMALL:
        out[name] = flat[at:at + size].reshape(like[name].shape)
        at += size
    return out


def kernel(x, positions, pre_mix_norm, w_in, sgu_ln_gain, sgu_ln_bias, sgu_w_spatial, sgu_b_spatial, attn_out_norm, sgu_out_norm, w_out, post_mix_norm, pre_ffn_norm, w_gate, w_up, w_down, post_ffn_norm, loss_target, m_pre_mix_norm, m_w_in, m_sgu_ln_gain, m_sgu_ln_bias, m_sgu_w_spatial, m_sgu_b_spatial, m_attn_out_norm, m_sgu_out_norm, m_w_out, m_post_mix_norm, m_pre_ffn_norm, m_w_gate, m_w_up, m_w_down, m_post_ffn_norm, v_pre_mix_norm, v_w_in, v_sgu_ln_gain, v_sgu_ln_bias, v_sgu_w_spatial, v_sgu_b_spatial, v_attn_out_norm, v_sgu_out_norm, v_w_out, v_post_mix_norm, v_pre_ffn_norm, v_w_gate, v_w_up, v_w_down, v_post_ffn_norm):
    small_w = dict(pre_mix_norm=pre_mix_norm, sgu_ln_gain=sgu_ln_gain, sgu_ln_bias=sgu_ln_bias, sgu_w_spatial=sgu_w_spatial,
                   sgu_b_spatial=sgu_b_spatial, attn_out_norm=attn_out_norm, sgu_out_norm=sgu_out_norm,
                   post_mix_norm=post_mix_norm, pre_ffn_norm=pre_ffn_norm, post_ffn_norm=post_ffn_norm)
    small_m = dict(pre_mix_norm=m_pre_mix_norm, sgu_ln_gain=m_sgu_ln_gain, sgu_ln_bias=m_sgu_ln_bias, sgu_w_spatial=m_sgu_w_spatial,
                   sgu_b_spatial=m_sgu_b_spatial, attn_out_norm=m_attn_out_norm, sgu_out_norm=m_sgu_out_norm,
                   post_mix_norm=m_post_mix_norm, pre_ffn_norm=m_pre_ffn_norm, post_ffn_norm=m_post_ffn_norm)
    small_v = dict(pre_mix_norm=v_pre_mix_norm, sgu_ln_gain=v_sgu_ln_gain, sgu_ln_bias=v_sgu_ln_bias, sgu_w_spatial=v_sgu_w_spatial,
                   sgu_b_spatial=v_sgu_b_spatial, attn_out_norm=v_attn_out_norm, sgu_out_norm=v_sgu_out_norm,
                   post_mix_norm=v_post_mix_norm, pre_ffn_norm=v_pre_ffn_norm, post_ffn_norm=v_post_ffn_norm)

    x2d = x[0]
    target = loss_target[0]
    pos_col = positions.reshape(SEQ, 1)
    rot = _rot_consts()
    w_sp = sgu_w_spatial[0]
    bfull = jnp.repeat(sgu_b_spatial[0].T, HEAD_DIM, axis=1)

    x_i, y_i, c_i = (lax.axis_index(a).astype(jnp.int32) for a in MESH_AXES)
    dev = 4 * x_i + 2 * y_i + c_i
    core = c_i.reshape(1)
    chip = 2 * x_i + y_i
    chip_ids = jnp.stack([chip, chip ^ 1, chip ^ 2, chip ^ 3])
    dev_ids = jnp.stack([dev ^ m for m in range(N_DEV)])

    w_in_t, w_gate_t, w_up_t, w_out_f, w_down_f = place_shards(
        [w_in[0].T, w_gate[0].T, w_up[0].T, w_out[0], w_down[0]], dev.reshape(1))
    (w_in_t,) = all_gather_in_place("gather_w_in", [w_in_t])

    (h1, q, k, v, u, vs), [([w_out_f], _)] = in_proj(
        x2d, pos_col, pre_mix_norm, w_in_t, rot, exchanges=[gather_send([w_out_f])])
    q, k, v = (_to_residue_order(t) for t in (q, k, v))
    (attn_r, lse), [([w_out_f], _), ([w_gate_t, w_up_t, w_down_f], _)] = attn_fwd(
        q, k, v, exchanges=[gather_pass_on([w_out_f]), gather_send([w_gate_t, w_up_t, w_down_f])])
    attn = _from_residue_order(attn_r)
    (sgu,), [([w_gate_t, w_up_t, w_down_f], _)] = sgu_fwd(
        u, vs, sgu_ln_gain, sgu_ln_bias, w_sp, bfull, exchanges=[gather_pass_on([w_gate_t, w_up_t, w_down_f])])
    mix, y, x2, h2 = out_proj(attn, sgu, x2d, attn_out_norm, sgu_out_norm, w_out_f, post_mix_norm, pre_ffn_norm)
    gate, up, act = ffn_up(h2, w_gate_t, w_up_t)
    df, dx3, d_post_ffn, sq_err = ffn_down_loss(act, w_down_f, x2, post_ffn_norm, target)
    loss = lax.psum(sq_err[0, 0] * np.float32(0.5 / D_MODEL), MESH_AXES)

    g_w_down, _ = weight_grad("grad_w_down", act, df)
    (dgate, dup), [(_, [s_down])] = ffn_act_bwd(df, w_down_f, gate, up, exchanges=[to_sibling([g_w_down])])
    p_down = sum_cores("sum_cores_down", g_w_down, s_down, core)
    g_w_gate, [(_, [c_down])] = weight_grad("grad_w_gate", dgate, h2, exchanges=[to_chips([p_down])])
    r_w_down = sum_chips("sum_chips_down", p_down, c_down, chip_ids)
    g_w_up, [(_, [s_gate])] = weight_grad("grad_w_up", dup, h2, exchanges=[to_sibling([g_w_gate])])
    p_gate = sum_cores("sum_cores_gate", g_w_gate, s_gate, core)
    (dx2, dy, d_pre_ffn, d_post_mix), [(_, [s_up]), (_, [c_gate])] = ffn_in_bwd(
        dgate, dup, w_gate_t, w_up_t, x2, pre_ffn_norm, dx3, y, post_mix_norm,
        exchanges=[to_sibling([g_w_up]), to_chips([p_gate])])
    p_up = sum_cores("sum_cores_up", g_w_up, s_up, core)
    r_w_gate_t = sum_chips("sum_chips_gate", p_gate, c_gate, chip_ids)
    g_w_out, _ = weight_grad("grad_w_out", mix, dy)
    (dattn, dsgu, d_attn_out, d_sgu_out), [(_, [s_out])] = mix_bwd(
        dy, w_out_f, attn, sgu, attn_out_norm, sgu_out_norm, exchanges=[to_sibling([g_w_out])])
    p_out = sum_cores("sum_cores_out", g_w_out, s_out, core)
    du, dvs, d_ln_gain, d_ln_bias, d_w_sp, d_bfull = sgu_bwd(u, vs, dsgu, sgu_ln_gain, sgu_ln_bias, w_sp, bfull)
    (dq, dk, dv), [(_, [c_up, c_out])] = attn_bwd(
        q, k, v, attn_r, lse, _to_residue_order(dattn), _to_residue_order(pos_col), rot,
        exchanges=[to_chips([p_up, p_out])])
    dq, dk, dv = (_from_residue_order(t) for t in (dq, dk, dv))
    r_w_up_t = sum_chips("sum_chips_up", p_up, c_up, chip_ids)
    r_w_out = sum_chips("sum_chips_out", p_out, c_out, chip_ids)
    dproj = jnp.concatenate([dq, dk, dv, du, dvs], axis=1)
    g_w_in, _ = weight_grad("grad_w_in", dproj, h1)
    (grad_x, d_pre_mix), [(_, [s_in])] = in_bwd(
        dproj, w_in_t, x2d, pre_mix_norm, dx2, exchanges=[to_sibling([g_w_in])])
    p_in = sum_cores("sum_cores_in", g_w_in, s_in, core)

    d_b_sp = d_bfull.reshape(CHUNK, N_GROUPS, HEAD_DIM).sum(axis=-1).T
    small_g = _pack_small(dict(pre_mix_norm=d_pre_mix, sgu_ln_gain=d_ln_gain, sgu_ln_bias=d_ln_bias, sgu_w_spatial=d_w_sp,
                               sgu_b_spatial=d_b_sp, attn_out_norm=d_attn_out, sgu_out_norm=d_sgu_out,
                               post_mix_norm=d_post_mix, pre_ffn_norm=d_pre_ffn, post_ffn_norm=d_post_ffn))
    small_g = small_g.reshape(N_DEV, SMALL_ROWS // N_DEV, 128)
    [(_, [c_in]), (_, [o_small])] = exchange_only("grads_tail", [to_chips([p_in]), to_owners(small_g)])
    r_w_in_t = sum_chips("sum_chips_in", p_in, c_in, chip_ids)
    all_small = sum_owned("sum_small", small_g, o_small, dev_ids)
    (all_small,) = all_gather_in_place("gather_small_grads", [all_small])

    big = {}
    for name, w, g, m, vv in (("w_in", w_in, r_w_in_t.T, m_w_in, v_w_in), ("w_gate", w_gate, r_w_gate_t.T, m_w_gate, v_w_gate),
                              ("w_up", w_up, r_w_up_t.T, m_w_up, v_w_up), ("w_out", w_out, r_w_out, m_w_out, v_w_out),
                              ("w_down", w_down, r_w_down, m_w_down, v_w_down)):
        d, nm, nv = adamw("adamw_" + name, w[0], g, m[0], vv[0])
        big[name] = (g[None], d[None], nm[None], nv[None])
    sd, snm, snv = adamw("adamw_small", _pack_small(small_w), all_small, _pack_small(small_m), _pack_small(small_v))
    sg, sd, snm, snv = (_unpack_small(t, small_w) for t in (all_small, sd, snm, snv))

    names = ["pre_mix_norm", "w_in", "sgu_ln_gain", "sgu_ln_bias", "sgu_w_spatial", "sgu_b_spatial", "attn_out_norm",
             "sgu_out_norm", "w_out", "post_mix_norm", "pre_ffn_norm", "w_gate", "w_up", "w_down", "post_ffn_norm"]
    outs = [loss, grad_x[None]]
    for i, table in enumerate((sg, sd, snm, snv)):
        for name in names:
            outs.append(big[name][i] if name in big else table[name])
    return tuple(outs)
```

```python
import functools

import numpy as np
import jax
import jax.numpy as jnp
from jax import lax
from jax.experimental import pallas as pl
from jax.experimental.pallas import tpu as pltpu

F32 = jnp.float32
BF16 = jnp.bfloat16

SEQ = 2048
D_MODEL = 1024
ATTN_W = 512
SGU_W = 512
HEAD_DIM = 64
N_GROUPS = 8
CHUNK = 128
D_FF = 2816
IN_W = 3 * ATTN_W + 2 * SGU_W
DILATIONS = (1, 4, 16)
ROPE_THETA = 500000.0
ROT_DIM = 16
ROT_HALF = 8
RMS_EPS = 1e-6
LN_EPS = 1e-5
Q_SCALE = 0.125
NEG = -1e30

N_DEV = 8
MESH_AXES = ("x", "y", "c")
MESH = pl.DeviceIdType.MESH

ADAM_LR = 0.001
ADAM_B1 = 0.9
ADAM_B2 = 0.999
ADAM_EPS = 1e-08
ADAM_WD = 0.01
ADAM_STEP = 10

VMEM_LIMIT = 60 * 1024 * 1024
ANY = pl.BlockSpec(memory_space=pl.ANY)

SMALL = (("pre_mix_norm", 1024), ("sgu_ln_gain", 512), ("sgu_ln_bias", 512), ("sgu_w_spatial", 8 * 128 * 128),
         ("sgu_b_spatial", 1024), ("attn_out_norm", 512), ("sgu_out_norm", 512), ("post_mix_norm", 1024),
         ("pre_ffn_norm", 1024), ("post_ffn_norm", 1024), ("loss_sum", 1))
SMALL_ROWS = 1152


def _params(sem=("arbitrary",)):
    return pltpu.CompilerParams(dimension_semantics=sem, vmem_limit_bytes=VMEM_LIMIT)


def _dot(a, b):
    return jnp.dot(a, b, preferred_element_type=F32)


def _dot_nt(a, b):
    return lax.dot_general(a, b, (((1,), (1,)), ((), ())), preferred_element_type=F32)


def _dot_tn(a, b):
    return lax.dot_general(a, b, (((0,), (0,)), ((), ())), preferred_element_type=F32)


def _rms(z):
    return lax.rsqrt(jnp.mean(z * z, axis=-1, keepdims=True) + RMS_EPS)


def _rms_bwd(z, gain, d):
    r = _rms(z)
    n = z * r
    dn = d * gain
    dz = r * (dn - n * jnp.mean(dn * n, axis=-1, keepdims=True))
    return dz, jnp.sum(d * n, axis=0, keepdims=True)


def _gelu(z):
    return 0.5 * z * (1.0 + lax.erf(z * np.float32(1.0 / np.sqrt(2.0))))


def _gelu_grad(z):
    cdf = 0.5 * (1.0 + lax.erf(z * np.float32(1.0 / np.sqrt(2.0))))
    return cdf + z * jnp.exp(-0.5 * z * z) * np.float32(1.0 / np.sqrt(2.0 * np.pi))


def _rot_tables(pos_col, invf, ma, mb):
    ang = pos_col.astype(F32) * invf
    s = jnp.sin(ang)
    return jnp.cos(ang), s * ma, s * mb


def _rot(t, c, sa, sb):
    return t * c + pltpu.roll(t, 120, 1) * sa + pltpu.roll(t, 8, 1) * sb


def _rot_t(d, c, sa, sb):
    return d * c + pltpu.roll(d * sa, 8, 1) + pltpu.roll(d * sb, 120, 1)


def _rot_consts():
    lane = np.arange(128) % HEAD_DIM
    inv_freq = (np.float32(ROPE_THETA) ** (-np.arange(0, ROT_DIM, 2, dtype=np.float32) / np.float32(ROT_DIM))).astype(np.float32)
    invf = np.where(lane < ROT_DIM, inv_freq[lane % ROT_HALF], 0.0).astype(np.float32)
    ma = np.where(lane < ROT_HALF, -1.0, 0.0).astype(np.float32)
    mb = np.where((lane >= ROT_HALF) & (lane < ROT_DIM), 1.0, 0.0).astype(np.float32)
    return jnp.asarray(invf[None]), jnp.asarray(ma[None]), jnp.asarray(mb[None])


def _row_spec(tm, w):
    return pl.BlockSpec((tm, w), lambda i: (i, 0))


def _full_spec(shape):
    return pl.BlockSpec(shape, lambda i: (0,) * len(shape))


def in_proj(x, pos_col, g1, w_in_t, rot, exchanges=()):
    tm = 512

    def body(x_ref, pos_ref, g_ref, w_ref, invf_ref, ma_ref, mb_ref, h_ref, q_ref, k_ref, v_ref, u_ref, vs_ref):
        xf = x_ref[...]
        h = (xf * _rms(xf) * g_ref[...]).astype(BF16)
        h_ref[...] = h
        proj = _dot_nt(h, w_ref[...])
        c, sa, sb = _rot_tables(pos_ref[...], invf_ref[...], ma_ref[...], mb_ref[...])
        for j in range(ATTN_W // 128):
            q_ref[:, j * 128:(j + 1) * 128] = _rot(proj[:, j * 128:(j + 1) * 128], c, sa, sb) * Q_SCALE
            k_ref[:, j * 128:(j + 1) * 128] = _rot(proj[:, ATTN_W + j * 128:ATTN_W + (j + 1) * 128], c, sa, sb)
        v_ref[...] = proj[:, 2 * ATTN_W:3 * ATTN_W]
        u_ref[...] = proj[:, 3 * ATTN_W:3 * ATTN_W + SGU_W]
        vs_ref[...] = proj[:, 3 * ATTN_W + SGU_W:]

    act = jax.ShapeDtypeStruct((SEQ, 512), F32)
    return _hosted(
        "in_proj", body, SEQ // tm,
        [_row_spec(tm, D_MODEL), _row_spec(tm, 1), _full_spec((1, D_MODEL)), _full_spec((IN_W, D_MODEL)),
         _full_spec((1, 128)), _full_spec((1, 128)), _full_spec((1, 128))],
        [_row_spec(tm, D_MODEL)] + [_row_spec(tm, 512)] * 5,
        [jax.ShapeDtypeStruct((SEQ, D_MODEL), BF16)] + [act] * 5,
        (x, pos_col, g1, w_in_t, *rot), exchanges=exchanges)


RES = 16


def _to_residue_order(t):
    return t.reshape(SEQ // RES, RES, -1).transpose(1, 0, 2).reshape(t.shape)


def _from_residue_order(t):
    return t.reshape(RES, SEQ // RES, -1).transpose(1, 0, 2).reshape(t.shape)


def _block_rows(d, r, n):
    if d == 16:
        slices = [(128 * r, 128)]
    elif d == 4:
        slices = [(128 * (4 * b + r) + 32 * n, 32) for b in range(4)]
    else:
        slices = [(128 * b + 8 * n, 8) for b in range(RES)]
    return [(s if isinstance(s, int) else pl.multiple_of(s, z), z) for s, z in slices]


def _block_step(d, i):
    if d == 16:
        return i
    if d == 4:
        return 4 * (i & 31) + (i >> 5)
    return 16 * (i & 7) + (i >> 3)


def _attn_masks(d):
    row2 = _block_step(d, lax.broadcasted_iota(jnp.int32, (128, 256), 0))
    col2 = lax.broadcasted_iota(jnp.int32, (128, 256), 1)
    key2 = _block_step(d, col2 & 127)
    mask2 = jnp.logical_or(jnp.logical_and(col2 < 128, key2 >= row2), jnp.logical_and(col2 >= 128, key2 <= row2))
    row1 = _block_step(d, lax.broadcasted_iota(jnp.int32, (128, 128), 0))
    col1 = lax.broadcasted_iota(jnp.int32, (128, 128), 1)
    return col1 < HEAD_DIM, _block_step(d, col1) <= row1, mask2


def _load_rows(ref, slices):
    parts = [ref[pl.ds(s, z), :] for s, z in slices]
    return parts[0] if len(parts) == 1 else jnp.concatenate(parts, axis=0)


def _for_each_group(fn):
    for p, d in enumerate(DILATIONS):
        masks = _attn_masks(d)
        if d == 16:
            def group(i, carry, p=p, masks=masks):
                fn(p, masks, [(_block_rows(16, 4 * i + g, 0), None) for g in range(4)])
                return carry

            lax.fori_loop(0, 4, group, 0)
        elif d == 4:
            fn(p, masks, [(_block_rows(4, r, 0), None) for r in range(4)])

            def group(i, carry, p=p, masks=masks):
                fn(p, masks, [(_block_rows(4, r, i + 1), _block_rows(4, r, i)) for r in range(4)])
                return carry

            lax.fori_loop(0, 3, group, 0)
        else:
            fn(p, masks, [(_block_rows(1, 0, 0), None)])

            def group(i, carry, p=p, masks=masks):
                fn(p, masks, [(_block_rows(1, 0, 3 * i + g + 1), _block_rows(1, 0, 3 * i + g)) for g in range(3)])
                return carry

            lax.fori_loop(0, 5, group, 0)


def attn_fwd(q, k, v, exchanges=()):
    def body(q_ref, k_ref, v_ref, o_ref, lse_ref, op_ref, lp_ref):
        def group(p, masks, blocks):
            head0, mask1, mask2 = masks
            heads = (head0, jnp.logical_not(head0))
            keys = [rows if prev is None else prev + rows for rows, prev in blocks]
            mask = [mask1 if prev is None else mask2 for _, prev in blocks]
            qb = [_load_rows(q_ref, rows) for rows, _ in blocks]
            kk = [_load_rows(k_ref, ks).astype(BF16) for ks in keys]
            vv = [_load_rows(v_ref, ks).astype(BF16) for ks in keys]
            chains = [(g, hm) for g in range(len(blocks)) for hm in heads]
            s = [jnp.where(mask[g], _dot_nt(jnp.where(hm, qb[g], 0.0).astype(BF16), kk[g]), NEG) for g, hm in chains]
            m = [jnp.max(t, axis=-1, keepdims=True) for t in s]
            e = [jnp.exp(t - mt) for t, mt in zip(s, m)]
            l = [jnp.sum(t, axis=-1, keepdims=True) for t in e]
            pv = [_dot(t.astype(BF16), vv[g]) for t, (g, _) in zip(e, chains)]
            for g, (rows, _) in enumerate(blocks):
                o_blk = jnp.where(head0, pv[2 * g] / l[2 * g], pv[2 * g + 1] / l[2 * g + 1])
                l_blk = jnp.where(head0, jnp.broadcast_to(m[2 * g] + jnp.log(l[2 * g]), (128, 128)),
                                  jnp.broadcast_to(m[2 * g + 1] + jnp.log(l[2 * g + 1]), (128, 128)))
                at = 0
                for start, size in rows:
                    op_ref[p, pl.ds(start, size), :] = o_blk[at:at + size]
                    lp_ref[p, pl.ds(start, size), :] = l_blk[at:at + size]
                    at += size

        _for_each_group(group)

        def combine(i, carry):
            rows = pl.ds(pl.multiple_of(i * 256, 256), 256)
            ls = [lp_ref[p, rows, :] for p in range(3)]
            m = jnp.maximum(jnp.maximum(ls[0], ls[1]), ls[2])
            lse = m + jnp.log(jnp.exp(ls[0] - m) + jnp.exp(ls[1] - m) + jnp.exp(ls[2] - m))
            o = jnp.zeros((256, 128), F32)
            for p in range(3):
                o = o + jnp.exp(ls[p] - lse) * op_ref[p, rows, :]
            o_ref[rows, :] = o
            lse_ref[rows, :] = lse
            return carry

        lax.fori_loop(0, SEQ // 256, combine, 0)

    slab = pl.BlockSpec((SEQ, 128), lambda i: (0, i))
    out = jax.ShapeDtypeStruct((SEQ, ATTN_W), F32)
    return _hosted(
        "attn_fwd", body, ATTN_W // 128, [slab] * 3, [slab] * 2, [out, out], (q, k, v),
        scratch_shapes=[pltpu.VMEM((3, SEQ, 128), F32), pltpu.VMEM((3, SEQ, 128), F32)], exchanges=exchanges)


def _causal_weights(w_ref):
    row = lax.broadcasted_iota(jnp.int32, (CHUNK, CHUNK), 0)
    col = lax.broadcasted_iota(jnp.int32, (CHUNK, CHUNK), 1)
    return [jnp.where(col <= row, w_ref[g], 0.0).astype(BF16) for g in range(N_GROUPS)], col <= row


def _sgu_chunk_fwd(u, vs, lg, lb, wc, bfull, head0):
    ug = _gelu(u)
    vg = _gelu(vs)
    xc = vg - jnp.mean(vg, axis=-1, keepdims=True)
    rstd = lax.rsqrt(jnp.mean(xc * xc, axis=-1, keepdims=True) + LN_EPS)
    xhat = xc * rstd
    vn = xhat * lg + lb
    mixed = []
    for gp in range(SGU_W // 128):
        vp = vn[:, gp * 128:(gp + 1) * 128].astype(BF16)
        mixed.append(jnp.where(head0, _dot(wc[2 * gp], vp), _dot(wc[2 * gp + 1], vp)))
    ms = jnp.concatenate(mixed, axis=1) + bfull
    return ug, xhat, rstd, vn, ms


def sgu_fwd(u, vs, lg, lb, w_sp, bfull, exchanges=()):
    cpb = 4

    def body(u_ref, vs_ref, lg_ref, lb_ref, w_ref, b_ref, o_ref):
        wc, _ = _causal_weights(w_ref)
        head0 = lax.broadcasted_iota(jnp.int32, (CHUNK, 128), 1) < HEAD_DIM
        for ci in range(cpb):
            rows = pl.ds(ci * CHUNK, CHUNK)
            ug, _, _, _, ms = _sgu_chunk_fwd(u_ref[rows, :], vs_ref[rows, :], lg_ref[...], lb_ref[...], wc, b_ref[...], head0)
            o_ref[rows, :] = ug * ms

    tm = cpb * CHUNK
    return _hosted(
        "sgu_fwd", body, SEQ // tm,
        [_row_spec(tm, SGU_W), _row_spec(tm, SGU_W), _full_spec((1, SGU_W)), _full_spec((1, SGU_W)),
         _full_spec((N_GROUPS, CHUNK, CHUNK)), _full_spec((CHUNK, SGU_W))],
        [_row_spec(tm, SGU_W)], [jax.ShapeDtypeStruct((SEQ, SGU_W), F32)],
        (u, vs, lg, lb, w_sp, bfull), exchanges=exchanges)


def out_proj(attn, sgu, x, ga, gs, w_out, gpm, gpf, exchanges=()):
    tm = 512

    def body(a_ref, s_ref, x_ref, ga_ref, gs_ref, w_ref, gpm_ref, gpf_ref, mix_ref, y_ref, x2_ref, h2_ref):
        a = a_ref[...]
        s = s_ref[...]
        an = (a * _rms(a) * ga_ref[...]).astype(BF16)
        sn = (s * _rms(s) * gs_ref[...]).astype(BF16)
        mix_ref[:, :ATTN_W] = an
        mix_ref[:, ATTN_W:] = sn
        y = _dot(an, w_ref[:ATTN_W, :]) + _dot(sn, w_ref[ATTN_W:, :])
        y_ref[...] = y
        x2 = x_ref[...] + y * _rms(y) * gpm_ref[...]
        x2_ref[...] = x2
        h2_ref[...] = (x2 * _rms(x2) * gpf_ref[...]).astype(BF16)

    wide = jax.ShapeDtypeStruct((SEQ, D_MODEL), F32)
    wide16 = jax.ShapeDtypeStruct((SEQ, D_MODEL), BF16)
    return _hosted(
        "out_proj", body, SEQ // tm,
        [_row_spec(tm, ATTN_W), _row_spec(tm, SGU_W), _row_spec(tm, D_MODEL), _full_spec((1, ATTN_W)),
         _full_spec((1, SGU_W)), _full_spec((D_MODEL, D_MODEL)), _full_spec((1, D_MODEL)), _full_spec((1, D_MODEL))],
        [_row_spec(tm, D_MODEL)] * 4, [wide16, wide, wide, wide16],
        (attn, sgu, x, ga, gs, w_out, gpm, gpf), exchanges=exchanges)


def ffn_up(h2, w_gate_t, w_up_t, exchanges=()):
    tm = 256

    def body(h_ref, wg_ref, wu_ref, g_ref, u_ref, a_ref):
        h = h_ref[...]
        g = _dot_nt(h, wg_ref[...])
        u = _dot_nt(h, wu_ref[...])
        g_ref[...] = g
        u_ref[...] = u
        a_ref[...] = (g * jax.nn.sigmoid(g) * u).astype(BF16)

    ff = jax.ShapeDtypeStruct((SEQ, D_FF), F32)
    return _hosted(
        "ffn_up", body, SEQ // tm,
        [_row_spec(tm, D_MODEL), _full_spec((D_FF, D_MODEL)), _full_spec((D_FF, D_MODEL))],
        [_row_spec(tm, D_FF)] * 3, [ff, ff, jax.ShapeDtypeStruct((SEQ, D_FF), BF16)],
        (h2, w_gate_t, w_up_t), exchanges=exchanges)


def ffn_down_loss(act, w_down, x2, gpo, target):
    tm = 512

    def body(a_ref, w_ref, x2_ref, g_ref, t_ref, df_ref, dx3_ref, dg_ref, loss_ref):
        f = _dot(a_ref[...], w_ref[...])
        gain = g_ref[...]
        err = x2_ref[...] + f * _rms(f) * gain - t_ref[...]
        dx3 = err * np.float32(1.0 / D_MODEL)
        dx3_ref[...] = dx3
        df, dg = _rms_bwd(f, gain, dx3)
        df_ref[...] = df.astype(BF16)

        @pl.when(pl.program_id(0) == 0)
        def _():
            dg_ref[...] = jnp.zeros_like(dg_ref)
            loss_ref[...] = jnp.zeros_like(loss_ref)

        dg_ref[...] += dg
        loss_ref[...] += jnp.sum(err * err, axis=(0, 1), keepdims=True)

    return pl.pallas_call(
        body, name="ffn_down_loss", grid=(SEQ // tm,),
        in_specs=[_row_spec(tm, D_FF), _full_spec((D_FF, D_MODEL)), _row_spec(tm, D_MODEL), _full_spec((1, D_MODEL)),
                  _row_spec(tm, D_MODEL)],
        out_specs=[_row_spec(tm, D_MODEL), _row_spec(tm, D_MODEL), _full_spec((1, D_MODEL)), _full_spec((1, 1))],
        out_shape=[jax.ShapeDtypeStruct((SEQ, D_MODEL), BF16), jax.ShapeDtypeStruct((SEQ, D_MODEL), F32),
                   jax.ShapeDtypeStruct((1, D_MODEL), F32), jax.ShapeDtypeStruct((1, 1), F32)],
        compiler_params=_params(),
    )(act, w_down, x2, gpo, target)


def ffn_act_bwd(df, w_down, gate, up, exchanges=()):
    tm = 256

    def body(df_ref, w_ref, g_ref, u_ref, dg_ref, du_ref):
        dact = _dot_nt(df_ref[...], w_ref[...])
        g = g_ref[...]
        s = jax.nn.sigmoid(g)
        du_ref[...] = (dact * g * s).astype(BF16)
        dg_ref[...] = (dact * u_ref[...] * (s * (1.0 + g * (1.0 - s)))).astype(BF16)

    ff16 = jax.ShapeDtypeStruct((SEQ, D_FF), BF16)
    return _hosted(
        "ffn_act_bwd", body, SEQ // tm,
        [_row_spec(tm, D_MODEL), _full_spec((D_FF, D_MODEL)), _row_spec(tm, D_FF), _row_spec(tm, D_FF)],
        [_row_spec(tm, D_FF)] * 2, [ff16, ff16], (df, w_down, gate, up), exchanges=exchanges)


def ffn_in_bwd(dgate, dup, w_gate_t, w_up_t, x2, gpf, dx3, y, gpm, exchanges=()):
    tm = 256

    def body(dg_ref, du_ref, wg_ref, wu_ref, x2_ref, gpf_ref, dx3_ref, y_ref, gpm_ref, dx2_ref, dy_ref, dgpf_ref, dgpm_ref):
        dh2 = _dot(dg_ref[...], wg_ref[...]) + _dot(du_ref[...], wu_ref[...])
        dz, dgpf = _rms_bwd(x2_ref[...], gpf_ref[...], dh2)
        dx2 = dx3_ref[...] + dz
        dx2_ref[...] = dx2
        dy, dgpm = _rms_bwd(y_ref[...], gpm_ref[...], dx2)
        dy_ref[...] = dy.astype(BF16)

        @pl.when(pl.program_id(0) == 0)
        def _():
            dgpf_ref[...] = jnp.zeros_like(dgpf_ref)
            dgpm_ref[...] = jnp.zeros_like(dgpm_ref)

        dgpf_ref[...] += dgpf
        dgpm_ref[...] += dgpm

    vec = jax.ShapeDtypeStruct((1, D_MODEL), F32)
    return _hosted(
        "ffn_in_bwd", body, SEQ // tm,
        [_row_spec(tm, D_FF), _row_spec(tm, D_FF), _full_spec((D_FF, D_MODEL)), _full_spec((D_FF, D_MODEL)),
         _row_spec(tm, D_MODEL), _full_spec((1, D_MODEL)), _row_spec(tm, D_MODEL), _row_spec(tm, D_MODEL),
         _full_spec((1, D_MODEL))],
        [_row_spec(tm, D_MODEL), _row_spec(tm, D_MODEL), _full_spec((1, D_MODEL)), _full_spec((1, D_MODEL))],
        [jax.ShapeDtypeStruct((SEQ, D_MODEL), F32), jax.ShapeDtypeStruct((SEQ, D_MODEL), BF16), vec, vec],
        (dgate, dup, w_gate_t, w_up_t, x2, gpf, dx3, y, gpm), exchanges=exchanges)


def weight_grad(name, a, b, exchanges=()):
    m, n = a.shape[1], b.shape[1]
    tr = 256

    def body(a_ref, b_ref, o_ref):
        o_ref[...] = _dot_tn(a_ref[...], b_ref[...]).astype(BF16)

    (out,), done = _hosted(
        name, body, m // tr, [pl.BlockSpec((SEQ, tr), lambda i: (0, i)), _full_spec((SEQ, n))],
        [_row_spec(tr, n)], [jax.ShapeDtypeStruct((m, n), BF16)], (a, b), exchanges=exchanges)
    return out.reshape(N_DEV, m // N_DEV, n), done


def mix_bwd(dy, w_out, attn, sgu, ga, gs, exchanges=()):
    tm = 512

    def body(dy_ref, w_ref, a_ref, s_ref, ga_ref, gs_ref, da_ref, ds_ref, dga_ref, dgs_ref):
        dy = dy_ref[...]
        da, dga = _rms_bwd(a_ref[...], ga_ref[...], _dot_nt(dy, w_ref[:ATTN_W, :]))
        ds, dgs = _rms_bwd(s_ref[...], gs_ref[...], _dot_nt(dy, w_ref[ATTN_W:, :]))
        da_ref[...] = da
        ds_ref[...] = ds

        @pl.when(pl.program_id(0) == 0)
        def _():
            dga_ref[...] = jnp.zeros_like(dga_ref)
            dgs_ref[...] = jnp.zeros_like(dgs_ref)

        dga_ref[...] += dga
        dgs_ref[...] += dgs

    half = jax.ShapeDtypeStruct((SEQ, 512), F32)
    vec = jax.ShapeDtypeStruct((1, 512), F32)
    return _hosted(
        "mix_bwd", body, SEQ // tm,
        [_row_spec(tm, D_MODEL), _full_spec((D_MODEL, D_MODEL)), _row_spec(tm, 512), _row_spec(tm, 512),
         _full_spec((1, 512)), _full_spec((1, 512))],
        [_row_spec(tm, 512), _row_spec(tm, 512), _full_spec((1, 512)), _full_spec((1, 512))],
        [half, half, vec, vec], (dy, w_out, attn, sgu, ga, gs), exchanges=exchanges)


def sgu_bwd(u, vs, dsgu, lg, lb, w_sp, bfull):
    cpb = 4

    def body(u_ref, vs_ref, d_ref, lg_ref, lb_ref, w_ref, b_ref, du_ref, dvs_ref, dlg_ref, dlb_ref, dw_ref, db_ref):
        wc, causal = _causal_weights(w_ref)
        head0 = lax.broadcasted_iota(jnp.int32, (CHUNK, 128), 1) < HEAD_DIM
        lg = lg_ref[...]

        @pl.when(pl.program_id(0) == 0)
        def _():
            dlg_ref[...] = jnp.zeros_like(dlg_ref)
            dlb_ref[...] = jnp.zeros_like(dlb_ref)
            dw_ref[...] = jnp.zeros_like(dw_ref)
            db_ref[...] = jnp.zeros_like(db_ref)

        for ci in range(cpb):
            rows = pl.ds(ci * CHUNK, CHUNK)
            u = u_ref[rows, :]
            vs = vs_ref[rows, :]
            d = d_ref[rows, :]
            ug, xhat, rstd, vn, ms = _sgu_chunk_fwd(u, vs, lg, lb_ref[...], wc, b_ref[...], head0)
            du_ref[rows, :] = (d * ms * _gelu_grad(u)).astype(BF16)
            dms = d * ug
            db_ref[...] += dms
            dvn = []
            for gp in range(SGU_W // 128):
                dmp = dms[:, gp * 128:(gp + 1) * 128]
                dm0 = jnp.where(head0, dmp, 0.0).astype(BF16)
                dm1 = jnp.where(head0, 0.0, dmp).astype(BF16)
                vp = vn[:, gp * 128:(gp + 1) * 128].astype(BF16)
                dw_ref[2 * gp] += _dot_nt(dm0, vp)
                dw_ref[2 * gp + 1] += _dot_nt(dm1, vp)
                dvn.append(_dot_tn(wc[2 * gp], dm0) + _dot_tn(wc[2 * gp + 1], dm1))
            dvn = jnp.concatenate(dvn, axis=1)
            dlg_ref[...] += jnp.sum(dvn * xhat, axis=0, keepdims=True)
            dlb_ref[...] += jnp.sum(dvn, axis=0, keepdims=True)
            dxh = dvn * lg
            dvg = rstd * (dxh - jnp.mean(dxh, axis=-1, keepdims=True) - xhat * jnp.mean(dxh * xhat, axis=-1, keepdims=True))
            dvs_ref[rows, :] = (dvg * _gelu_grad(vs)).astype(BF16)

        @pl.when(pl.program_id(0) == pl.num_programs(0) - 1)
        def _():
            for g in range(N_GROUPS):
                dw_ref[g] = jnp.where(causal, dw_ref[g], 0.0)

    tm = cpb * CHUNK
    half16 = jax.ShapeDtypeStruct((SEQ, SGU_W), BF16)
    vec = jax.ShapeDtypeStruct((1, SGU_W), F32)
    return pl.pallas_call(
        body, name="sgu_bwd", grid=(SEQ // tm,),
        in_specs=[_row_spec(tm, SGU_W)] * 3 + [_full_spec((1, SGU_W)), _full_spec((1, SGU_W)),
                                                  _full_spec((N_GROUPS, CHUNK, CHUNK)), _full_spec((CHUNK, SGU_W))],
        out_specs=[_row_spec(tm, SGU_W), _row_spec(tm, SGU_W), _full_spec((1, SGU_W)), _full_spec((1, SGU_W)),
                   _full_spec((N_GROUPS, CHUNK, CHUNK)), _full_spec((CHUNK, SGU_W))],
        out_shape=[half16, half16, vec, vec, jax.ShapeDtypeStruct((N_GROUPS, CHUNK, CHUNK), F32),
                   jax.ShapeDtypeStruct((CHUNK, SGU_W), F32)],
        compiler_params=_params(),
    )(u, vs, dsgu, lg, lb, w_sp, bfull)


def attn_bwd(q, k, v, o, lse, do, pos_col, rot, exchanges=()):
    def body(q_ref, k_ref, v_ref, o_ref, lse_ref, do_ref, pos_ref, invf_ref, ma_ref, mb_ref,
             dq_ref, dk_ref, dv_ref, dqa_ref, dka_ref, dva_ref, dlt_ref, rot_ref):
        dqa_ref[...] = jnp.zeros_like(dqa_ref)
        dka_ref[...] = jnp.zeros_like(dka_ref)
        dva_ref[...] = jnp.zeros_like(dva_ref)

        def delta(i, carry):
            rows = pl.ds(pl.multiple_of(i * 256, 256), 256)
            prod = do_ref[rows, :] * o_ref[rows, :]
            h0 = lax.broadcasted_iota(jnp.int32, (256, 128), 1) < HEAD_DIM
            d0 = jnp.sum(jnp.where(h0, prod, 0.0), axis=-1, keepdims=True)
            d1 = jnp.sum(jnp.where(h0, 0.0, prod), axis=-1, keepdims=True)
            dlt_ref[rows, :] = jnp.where(h0, d0, d1)
            return carry

        lax.fori_loop(0, SEQ // 256, delta, 0)

        def add_rows(ref, slices, val):
            at = 0
            for start, size in slices:
                ref[pl.ds(start, size), :] += val[at:at + size]
                at += size

        def group(p, masks, blocks):
            head0, mask1, mask2 = masks
            heads = (head0, jnp.logical_not(head0))
            keys = [rows if prev is None else prev + rows for rows, prev in blocks]
            mask = [mask1 if prev is None else mask2 for _, prev in blocks]
            kk = [_load_rows(k_ref, ks).astype(BF16) for ks in keys]
            vv = [_load_rows(v_ref, ks).astype(BF16) for ks in keys]
            qb = [_load_rows(q_ref, rows) for rows, _ in blocks]
            dob = [_load_rows(do_ref, rows) for rows, _ in blocks]
            lse_b = [_load_rows(lse_ref, rows) for rows, _ in blocks]
            dlt_b = [_load_rows(dlt_ref, rows) for rows, _ in blocks]
            chains = [(g, h) for g in range(len(blocks)) for h in range(2)]
            qm = [jnp.where(heads[h], qb[g], 0.0).astype(BF16) for g, h in chains]
            dom = [jnp.where(heads[h], dob[g], 0.0).astype(BF16) for g, h in chains]
            s = [_dot_nt(qm[c], kk[g]) for c, (g, h) in enumerate(chains)]
            dp = [_dot_nt(dom[c], vv[g]) for c, (g, h) in enumerate(chains)]
            pr = [jnp.where(mask[g], jnp.exp(s[c] - lse_b[g][:, h * HEAD_DIM:h * HEAD_DIM + 1]), 0.0)
                  for c, (g, h) in enumerate(chains)]
            ds = [(pr[c] * (dp[c] - dlt_b[g][:, h * HEAD_DIM:h * HEAD_DIM + 1])).astype(BF16)
                  for c, (g, h) in enumerate(chains)]
            dv = [_dot_tn(pr[c].astype(BF16), dom[c]) for c in range(len(chains))]
            dk = [_dot_tn(ds[c], qm[c]) for c in range(len(chains))]
            dq = [_dot(ds[c], kk[g]) for c, (g, h) in enumerate(chains)]
            for g, (rows, _) in enumerate(blocks):
                add_rows(dqa_ref, rows, jnp.where(head0, dq[2 * g], dq[2 * g + 1]))
                add_rows(dka_ref, keys[g], dk[2 * g] + dk[2 * g + 1])
                add_rows(dva_ref, keys[g], dv[2 * g] + dv[2 * g + 1])

        _for_each_group(group)

        @pl.when(pl.program_id(0) == 0)
        def _():
            def tables(i, carry):
                rows = pl.ds(pl.multiple_of(i * 256, 256), 256)
                c, sa, sb = _rot_tables(pos_ref[rows, :], invf_ref[...], ma_ref[...], mb_ref[...])
                rot_ref[0, rows, :] = c
                rot_ref[1, rows, :] = sa
                rot_ref[2, rows, :] = sb
                return carry

            lax.fori_loop(0, SEQ // 256, tables, 0)

        def finish(i, carry):
            rows = pl.ds(pl.multiple_of(i * 256, 256), 256)
            c, sa, sb = rot_ref[0, rows, :], rot_ref[1, rows, :], rot_ref[2, rows, :]
            dq_ref[rows, :] = _rot_t(dqa_ref[rows, :] * Q_SCALE, c, sa, sb).astype(BF16)
            dk_ref[rows, :] = _rot_t(dka_ref[rows, :], c, sa, sb).astype(BF16)
            dv_ref[rows, :] = dva_ref[rows, :].astype(BF16)
            return carry

        lax.fori_loop(0, SEQ // 256, finish, 0)

    slab = pl.BlockSpec((SEQ, 128), lambda i: (0, i))
    out = jax.ShapeDtypeStruct((SEQ, ATTN_W), BF16)
    acc = pltpu.VMEM((SEQ, 128), F32)
    return _hosted(
        "attn_bwd", body, ATTN_W // 128,
        [slab] * 6 + [_full_spec((SEQ, 1)), _full_spec((1, 128)), _full_spec((1, 128)), _full_spec((1, 128))],
        [slab] * 3, [out, out, out], (q, k, v, o, lse, do, pos_col, *rot),
        scratch_shapes=[acc, acc, acc, acc, pltpu.VMEM((3, SEQ, 128), F32)], exchanges=exchanges)


def in_bwd(dproj, w_in_t, x, g1, dx2, exchanges=()):
    tm = 512

    def body(dp_ref, w_ref, x_ref, g_ref, dx2_ref, dx_ref, dg_ref):
        dh1 = _dot(dp_ref[...], w_ref[...])
        dz, dg = _rms_bwd(x_ref[...], g_ref[...], dh1)
        dx_ref[...] = dx2_ref[...] + dz

        @pl.when(pl.program_id(0) == 0)
        def _():
            dg_ref[...] = jnp.zeros_like(dg_ref)

        dg_ref[...] += dg

    return _hosted(
        "in_bwd", body, SEQ // tm,
        [_row_spec(tm, IN_W), _full_spec((IN_W, D_MODEL)), _row_spec(tm, D_MODEL), _full_spec((1, D_MODEL)),
         _row_spec(tm, D_MODEL)],
        [_row_spec(tm, D_MODEL), _full_spec((1, D_MODEL))],
        [jax.ShapeDtypeStruct((SEQ, D_MODEL), F32), jax.ShapeDtypeStruct((1, D_MODEL), F32)],
        (dproj, w_in_t, x, g1, dx2), exchanges=exchanges)


def _coords():
    return lax.axis_index("x"), lax.axis_index("y"), lax.axis_index("c")


class Exchange:
    def __init__(self, srcs, bufs, new_shapes, n_sems, make, mid_step=None):
        self.srcs, self.bufs, self.new_shapes, self.n_sems, self.make = list(srcs), list(bufs), list(new_shapes), n_sems, make
        self.mid_step = mid_step


def _hosted(name, body, n_steps, in_specs, out_specs, out_shape, args, scratch_shapes=(), exchanges=(), prefetch=None):
    out_shape, out_specs = list(out_shape), list(out_specs)
    srcs = [a for ex in exchanges for a in ex.srcs]
    bufs = [a for ex in exchanges for a in ex.bufs]
    news = [s for ex in exchanges for s in ex.new_shapes]
    n_pre = 0 if prefetch is None else 1
    n_in, n_out, n_scr = len(args), len(out_shape), len(scratch_shapes)

    def wrapped(*refs):
        refs = list(refs)
        pre, refs = refs[:n_pre], refs[n_pre:]
        ins = refs[:n_in]
        src_refs = refs[n_in:n_in + len(srcs)]
        at = n_in + len(srcs) + len(bufs)
        outs = refs[at:at + n_out]
        buf_refs = refs[at + n_out:at + n_out + len(bufs)]
        new_refs = refs[at + n_out + len(bufs):at + n_out + len(bufs) + len(news)]
        at += n_out + len(bufs) + len(news)
        scratch, sems = refs[at:at + n_scr], refs[at + n_scr:]

        def copies(phase, which):
            made, si, bi, ni = [], 0, 0, 0
            for k, ex in enumerate(exchanges):
                if which(ex):
                    made.append(ex.make(phase, src_refs[si:si + len(ex.srcs)], buf_refs[bi:bi + len(ex.bufs)],
                                        new_refs[ni:ni + len(ex.new_shapes)], sems[2 * k], sems[2 * k + 1]))
                si, bi, ni = si + len(ex.srcs), bi + len(ex.bufs), ni + len(ex.new_shapes)
            return made

        if exchanges:
            @pl.when(pl.program_id(0) == 0)
            def _():
                for starts in copies("start", lambda ex: True):
                    for cp in starts:
                        cp.start()

        for mid in sorted({ex.mid_step for ex in exchanges if ex.mid_step is not None}):
            @pl.when(pl.program_id(0) == mid)
            def _(mid=mid):
                for arrivals, starts in copies("middle", lambda ex: ex.mid_step == mid):
                    for cp in arrivals:
                        cp.wait_recv()
                    for cp in starts:
                        cp.start()

        body(*pre, *ins, *outs, *scratch)

        if exchanges:
            @pl.when(pl.program_id(0) == n_steps - 1)
            def _():
                for sends, recvs in copies("end", lambda ex: True):
                    for cp in sends:
                        cp.wait_send()
                    for cp in recvs:
                        cp.wait_recv()

    sem_shapes = []
    for ex in exchanges:
        sem_shapes += [pltpu.SemaphoreType.DMA((ex.n_sems,)), pltpu.SemaphoreType.DMA((ex.n_sems,))]
    all_in = list(in_specs) + [ANY] * (len(srcs) + len(bufs))
    all_out = out_specs + [ANY] * (len(bufs) + len(news))
    all_shape = out_shape + [jax.ShapeDtypeStruct(b.shape, b.dtype) for b in bufs] + news
    all_scratch = list(scratch_shapes) + sem_shapes
    aliases = {n_pre + n_in + len(srcs) + i: n_out + i for i in range(len(bufs))}
    if prefetch is None:
        call = pl.pallas_call(wrapped, name=name, grid=(n_steps,), in_specs=all_in, out_specs=all_out, out_shape=all_shape,
                              scratch_shapes=all_scratch, input_output_aliases=aliases, compiler_params=_params())
        outs = call(*args, *srcs, *bufs)
    else:
        spec = pltpu.PrefetchScalarGridSpec(num_scalar_prefetch=1, grid=(n_steps,), in_specs=all_in, out_specs=all_out,
                                            scratch_shapes=all_scratch)
        call = pl.pallas_call(wrapped, name=name, grid_spec=spec, out_shape=all_shape, input_output_aliases=aliases,
                              compiler_params=_params())
        outs = call(prefetch, *args, *srcs, *bufs)
    results, bi, ni = [], n_out, n_out + len(bufs)
    for ex in exchanges:
        results.append((list(outs[bi:bi + len(ex.bufs)]), list(outs[ni:ni + len(ex.new_shapes)])))
        bi, ni = bi + len(ex.bufs), ni + len(ex.new_shapes)
    return list(outs[:n_out]), results


def _gather_copies(kinds, bufs, ranges, send_sems, recv_sems):
    x, y, c = _coords()
    me, sibling = (x, y, c), (x, y, 1 - c)
    chips = [(1 - x, y), (x, 1 - y), (1 - x, 1 - y)]

    def copy(a, k, block, to):
        lo, hi = ranges[a]
        r = bufs[a].shape[0] // N_DEV
        rows = bufs[a].at[pl.ds((4 * block[0] + 2 * block[1] + block[2]) * r + lo, hi - lo), :]
        return pltpu.make_async_remote_copy(src_ref=rows, dst_ref=rows, send_sem=send_sems.at[7 * a + k],
                                            recv_sem=recv_sems.at[7 * a + k], device_id=to, device_id_type=MESH)

    every = range(len(bufs))
    make = {
        "out": lambda: [copy(a, 0, me, sibling) for a in every]
        + [copy(a, 1 + j, me, (*chip, c)) for a in every for j, chip in enumerate(chips)],
        "from_core": lambda: [copy(a, 0, sibling, me) for a in every],
        "from_chips": lambda: [copy(a, 1 + j, (*chip, c), me) for a in every for j, chip in enumerate(chips)],
        "on": lambda: [copy(a, 4 + j, (*chip, c), sibling) for a in every for j, chip in enumerate(chips)],
        "on_in": lambda: [copy(a, 4 + j, (*chip, 1 - c), me) for a in every for j, chip in enumerate(chips)],
    }
    return [make[kind]() for kind in kinds]


def gather(bufs, mid_step, ranges=None):
    ranges = ranges or [(0, b.shape[0] // N_DEV) for b in bufs]

    def make(phase, src_refs, buf_refs, new_refs, send_sems, recv_sems):
        kinds = {"start": ["out"], "middle": ["from_chips", "on"], "end": ["out", "on", "from_core", "on_in"]}[phase]
        made = _gather_copies(kinds, buf_refs, ranges, send_sems, recv_sems)
        if phase == "start":
            return made[0]
        if phase == "middle":
            return made[0], made[1]
        return made[0] + made[1], made[2] + made[3]

    return Exchange([], bufs, [], 7 * len(bufs), make, mid_step=mid_step)


def all_gather_in_place(name, bufs):
    n = len(bufs)
    ranges = [(0, b.shape[0] // N_DEV) for b in bufs]

    def body(*refs):
        outs, send_sems, recv_sems = refs[n:2 * n], refs[2 * n], refs[2 * n + 1]
        out, from_core, from_chips, on, on_in = _gather_copies(
            ["out", "from_core", "from_chips", "on", "on_in"], outs, ranges, send_sems, recv_sems)
        for cp in out:
            cp.start()
        for cp in from_chips:
            cp.wait_recv()
        for cp in on:
            cp.start()
        for cp in from_core + on_in:
            cp.wait_recv()
        for cp in out + on:
            cp.wait_send()

    return pl.pallas_call(
        body, name=name, in_specs=[ANY] * n, out_specs=[ANY] * n,
        out_shape=[jax.ShapeDtypeStruct(b.shape, b.dtype) for b in bufs],
        scratch_shapes=[pltpu.SemaphoreType.DMA((7 * n,)), pltpu.SemaphoreType.DMA((7 * n,))],
        input_output_aliases={i: i for i in range(n)},
    )(*bufs)


def place_shards(shards, dev):
    n = len(shards)

    def body(dev_ref, *refs):
        for a in range(n):
            refs[n + a][...] = refs[a][...].astype(BF16)

    spec = pltpu.PrefetchScalarGridSpec(
        num_scalar_prefetch=1, grid=(1,),
        in_specs=[pl.BlockSpec(s.shape, lambda i, dev_ref: (0, 0)) for s in shards],
        out_specs=[pl.BlockSpec(s.shape, lambda i, dev_ref: (dev_ref[0], 0)) for s in shards])
    return pl.pallas_call(
        body, name="place_shards", grid_spec=spec,
        out_shape=[jax.ShapeDtypeStruct((N_DEV * s.shape[0], s.shape[1]), BF16) for s in shards],
        compiler_params=_params(),
    )(dev, *shards)


def _swap(copies_of):
    def make(phase, src_refs, buf_refs, new_refs, send_sems, recv_sems):
        copies = copies_of(src_refs, new_refs, send_sems, recv_sems)
        return copies if phase == "start" else (copies, copies)

    return make


def to_sibling(grads):
    def copies_of(src_refs, new_refs, send_sems, recv_sems):
        x, y, c = _coords()
        return [pltpu.make_async_remote_copy(
            src_ref=src_refs[a].at[2 * xy + 1 - c], dst_ref=new_refs[a].at[xy], send_sem=send_sems.at[4 * a + xy],
            recv_sem=recv_sems.at[4 * a + xy], device_id=(x, y, 1 - c), device_id_type=MESH)
            for a in range(len(src_refs)) for xy in range(4)]

    return Exchange(grads, [], [jax.ShapeDtypeStruct((4,) + g.shape[1:], g.dtype) for g in grads], 4 * len(grads),
                    _swap(copies_of))


def to_chips(parts):
    def copies_of(src_refs, new_refs, send_sems, recv_sems):
        x, y, c = _coords()
        chips = [(1 - x, y), (x, 1 - y), (1 - x, 1 - y)]
        return [pltpu.make_async_remote_copy(
            src_ref=src_refs[a].at[2 * px + py], dst_ref=new_refs[a].at[2 * x + y], send_sem=send_sems.at[3 * a + j],
            recv_sem=recv_sems.at[3 * a + j], device_id=(px, py, c), device_id_type=MESH)
            for a in range(len(src_refs)) for j, (px, py) in enumerate(chips)]

    return Exchange(parts, [], [jax.ShapeDtypeStruct(p.shape, p.dtype) for p in parts], 3 * len(parts), _swap(copies_of))


def to_owners(grad):
    def copies_of(src_refs, new_refs, send_sems, recv_sems):
        x, y, c = _coords()
        copies = []
        for m in range(1, N_DEV):
            px, py, pc = x ^ (m >> 2), y ^ ((m >> 1) & 1), c ^ (m & 1)
            copies.append(pltpu.make_async_remote_copy(
                src_ref=src_refs[0].at[4 * px + 2 * py + pc], dst_ref=new_refs[0].at[4 * x + 2 * y + c],
                send_sem=send_sems.at[m - 1], recv_sem=recv_sems.at[m - 1], device_id=(px, py, pc), device_id_type=MESH))
        return copies

    return Exchange([grad], [], [jax.ShapeDtypeStruct(grad.shape, grad.dtype)], N_DEV - 1, _swap(copies_of))


def exchange_only(name, exchanges):
    def body():
        pass

    return _hosted(name, body, 1, [], [], [], [], exchanges=exchanges)[1]


def sum_cores(name, grad, other, core):
    _, r, w = other.shape

    def body(core_ref, g_ref, o_ref, out_ref):
        out_ref[...] = (g_ref[...].astype(F32) + o_ref[...].astype(F32)).astype(out_ref.dtype)

    return pl.pallas_call(
        body, name=name,
        grid_spec=pltpu.PrefetchScalarGridSpec(
            num_scalar_prefetch=1, grid=(4,),
            in_specs=[pl.BlockSpec((1, r, w), lambda i, core_ref: (2 * i + core_ref[0], 0, 0)),
                      pl.BlockSpec((1, r, w), lambda i, core_ref: (i, 0, 0))],
            out_specs=pl.BlockSpec((1, r, w), lambda i, core_ref: (i, 0, 0))),
        out_shape=jax.ShapeDtypeStruct(other.shape, other.dtype),
        compiler_params=_params(),
    )(core, grad, other)


def sum_owned(name, grad, others, dev_ids):
    _, r, w = grad.shape

    def body(ids_ref, *refs):
        acc = refs[0][0]
        for k in range(1, N_DEV):
            acc = acc + refs[k][0]
        refs[N_DEV][...] = acc

    def pick(k):
        return pl.BlockSpec((1, r, w), lambda i, ids_ref: (ids_ref[k], 0, 0))

    return pl.pallas_call(
        body, name=name,
        grid_spec=pltpu.PrefetchScalarGridSpec(
            num_scalar_prefetch=1, grid=(1,), in_specs=[pick(k) for k in range(N_DEV)],
            out_specs=pl.BlockSpec((r, w), lambda i, ids_ref: (ids_ref[0], 0))),
        out_shape=jax.ShapeDtypeStruct((N_DEV * r, w), F32),
        compiler_params=_params(),
    )(dev_ids, grad, *([others] * (N_DEV - 1)))


def sum_chips(name, part, others, chip_ids):
    _, r, w = part.shape

    def body(ids_ref, p_ref, a_ref, b_ref, c_ref, out_ref):
        acc = p_ref[0].astype(F32) + a_ref[0].astype(F32)
        out_ref[...] = (acc + b_ref[0].astype(F32)) + c_ref[0].astype(F32)

    def pick(k):
        return pl.BlockSpec((1, r, w), lambda i, ids_ref: (ids_ref[k], 0, 0))

    return pl.pallas_call(
        body, name=name,
        grid_spec=pltpu.PrefetchScalarGridSpec(
            num_scalar_prefetch=1, grid=(1,),
            in_specs=[pick(0), pick(1), pick(2), pick(3)],
            out_specs=pl.BlockSpec((r, w), lambda i, ids_ref: (0, 0))),
        out_shape=jax.ShapeDtypeStruct((r, w), F32),
        compiler_params=_params(),
    )(chip_ids, part, others, others, others)


def adamw(name, w, g, m, v):
    c1 = np.float32(1.0 - ADAM_B1 ** ADAM_STEP)
    c2 = np.float32(1.0 - ADAM_B2 ** ADAM_STEP)

    def body(w_ref, g_ref, m_ref, v_ref, d_ref, nm_ref, nv_ref):
        g = g_ref[...]
        nm = ADAM_B1 * m_ref[...] + np.float32(1.0 - ADAM_B1) * g
        nv = ADAM_B2 * v_ref[...] + np.float32(1.0 - ADAM_B2) * (g * g)
        nm_ref[...] = nm
        nv_ref[...] = nv
        d_ref[...] = -ADAM_LR * ((nm / c1) / (jnp.sqrt(nv / c2) + ADAM_EPS) + ADAM_WD * w_ref[...])

    shape = jax.ShapeDtypeStruct(w.shape, F32)
    spec = _full_spec(w.shape)
    return pl.pallas_call(
        body, name=name, grid=(1,), in_specs=[spec] * 4, out_specs=[spec] * 3, out_shape=[shape] * 3,
        compiler_params=_params(),
    )(w, g, m, v)


def _pack_small(parts):
    flat = jnp.concatenate([parts[name].reshape(-1) for name, _ in SMALL])
    return jnp.pad(flat, (0, SMALL_ROWS * 128 - flat.shape[0])).reshape(SMALL_ROWS, 128)


def _unpack_small(packed, like):
    flat = packed.reshape(-1)
    out, at = {}, 0
    for name, size in SMALL:
        out[name] = flat[at:at + size].reshape(like[name].shape)
        at += size
    return out


def kernel(x, positions, pre_mix_norm, w_in, sgu_ln_gain, sgu_ln_bias, sgu_w_spatial, sgu_b_spatial, attn_out_norm, sgu_out_norm, w_out, post_mix_norm, pre_ffn_norm, w_gate, w_up, w_down, post_ffn_norm, loss_target, m_pre_mix_norm, m_w_in, m_sgu_ln_gain, m_sgu_ln_bias, m_sgu_w_spatial, m_sgu_b_spatial, m_attn_out_norm, m_sgu_out_norm, m_w_out, m_post_mix_norm, m_pre_ffn_norm, m_w_gate, m_w_up, m_w_down, m_post_ffn_norm, v_pre_mix_norm, v_w_in, v_sgu_ln_gain, v_sgu_ln_bias, v_sgu_w_spatial, v_sgu_b_spatial, v_attn_out_norm, v_sgu_out_norm, v_w_out, v_post_mix_norm, v_pre_ffn_norm, v_w_gate, v_w_up, v_w_down, v_post_ffn_norm):
    small_w = dict(pre_mix_norm=pre_mix_norm, sgu_ln_gain=sgu_ln_gain, sgu_ln_bias=sgu_ln_bias, sgu_w_spatial=sgu_w_spatial,
                   sgu_b_spatial=sgu_b_spatial, attn_out_norm=attn_out_norm, sgu_out_norm=sgu_out_norm,
                   post_mix_norm=post_mix_norm, pre_ffn_norm=pre_ffn_norm, post_ffn_norm=post_ffn_norm)
    small_m = dict(pre_mix_norm=m_pre_mix_norm, sgu_ln_gain=m_sgu_ln_gain, sgu_ln_bias=m_sgu_ln_bias, sgu_w_spatial=m_sgu_w_spatial,
                   sgu_b_spatial=m_sgu_b_spatial, attn_out_norm=m_attn_out_norm, sgu_out_norm=m_sgu_out_norm,
                   post_mix_norm=m_post_mix_norm, pre_ffn_norm=m_pre_ffn_norm, post_ffn_norm=m_post_ffn_norm)
    small_v = dict(pre_mix_norm=v_pre_mix_norm, sgu_ln_gain=v_sgu_ln_gain, sgu_ln_bias=v_sgu_ln_bias, sgu_w_spatial=v_sgu_w_spatial,
                   sgu_b_spatial=v_sgu_b_spatial, attn_out_norm=v_attn_out_norm, sgu_out_norm=v_sgu_out_norm,
                   post_mix_norm=v_post_mix_norm, pre_ffn_norm=v_pre_ffn_norm, post_ffn_norm=v_post_ffn_norm)
    for table in (small_w, small_m, small_v):
        table["loss_sum"] = jnp.zeros((1,), F32)

    x2d = x[0]
    target = loss_target[0]
    pos_col = positions.reshape(SEQ, 1)
    rot = _rot_consts()
    w_sp = sgu_w_spatial[0]
    bfull = jnp.repeat(sgu_b_spatial[0].T, HEAD_DIM, axis=1)

    x_i, y_i, c_i = (lax.axis_index(a).astype(jnp.int32) for a in MESH_AXES)
    dev = 4 * x_i + 2 * y_i + c_i
    core = c_i.reshape(1)
    chip = 2 * x_i + y_i
    chip_ids = jnp.stack([chip, chip ^ 1, chip ^ 2, chip ^ 3])
    dev_ids = jnp.stack([dev ^ m for m in range(N_DEV)])

    w_in_t, w_gate_t, w_up_t, w_out_f, w_down_f = place_shards(
        [w_in[0].T, w_gate[0].T, w_up[0].T, w_out[0], w_down[0]], dev.reshape(1))
    (w_in_t,) = all_gather_in_place("gather_w_in", [w_in_t])

    part, chunk = 112, D_FF // N_DEV
    (h1, q, k, v, u, vs), [([w_out_f, w_gate_t], _)] = in_proj(
        x2d, pos_col, pre_mix_norm, w_in_t, rot,
        exchanges=[gather([w_out_f, w_gate_t], 3, [(0, D_MODEL // N_DEV), (0, part)])])
    q, k, v = (_to_residue_order(t) for t in (q, k, v))
    (attn_r, lse), [([w_gate_t, w_up_t], _)] = attn_fwd(
        q, k, v, exchanges=[gather([w_gate_t, w_up_t], 3, [(part, chunk), (0, part)])])
    attn = _from_residue_order(attn_r)
    (sgu,), _ = sgu_fwd(u, vs, sgu_ln_gain, sgu_ln_bias, w_sp, bfull)
    (mix, y, x2, h2), [([w_up_t], _)] = out_proj(
        attn, sgu, x2d, attn_out_norm, sgu_out_norm, w_out_f, post_mix_norm, pre_ffn_norm,
        exchanges=[gather([w_up_t], 3, [(part, chunk)])])
    (gate, up, act), [([w_down_f], _)] = ffn_up(h2, w_gate_t, w_up_t, exchanges=[gather([w_down_f], 7)])
    df, dx3, d_post_ffn, sq_err = ffn_down_loss(act, w_down_f, x2, post_ffn_norm, target)

    g_w_down, _ = weight_grad("grad_w_down", act, df)
    (dgate, dup), [(_, [s_down])] = ffn_act_bwd(df, w_down_f, gate, up, exchanges=[to_sibling([g_w_down])])
    p_down = sum_cores("sum_cores_down", g_w_down, s_down, core)
    g_w_gate, [(_, [c_down])] = weight_grad("grad_w_gate", dgate, h2, exchanges=[to_chips([p_down])])
    r_w_down = sum_chips("sum_chips_down", p_down, c_down, chip_ids)
    g_w_up, [(_, [s_gate])] = weight_grad("grad_w_up", dup, h2, exchanges=[to_sibling([g_w_gate])])
    p_gate = sum_cores("sum_cores_gate", g_w_gate, s_gate, core)
    (dx2, dy, d_pre_ffn, d_post_mix), [(_, [s_up]), (_, [c_gate])] = ffn_in_bwd(
        dgate, dup, w_gate_t, w_up_t, x2, pre_ffn_norm, dx3, y, post_mix_norm,
        exchanges=[to_sibling([g_w_up]), to_chips([p_gate])])
    p_up = sum_cores("sum_cores_up", g_w_up, s_up, core)
    r_w_gate_t = sum_chips("sum_chips_gate", p_gate, c_gate, chip_ids)
    g_w_out, _ = weight_grad("grad_w_out", mix, dy)
    (dattn, dsgu, d_attn_out, d_sgu_out), [(_, [s_out])] = mix_bwd(
        dy, w_out_f, attn, sgu, attn_out_norm, sgu_out_norm, exchanges=[to_sibling([g_w_out])])
    p_out = sum_cores("sum_cores_out", g_w_out, s_out, core)
    du, dvs, d_ln_gain, d_ln_bias, d_w_sp, d_bfull = sgu_bwd(u, vs, dsgu, sgu_ln_gain, sgu_ln_bias, w_sp, bfull)
    (dq, dk, dv), [(_, [c_up, c_out])] = attn_bwd(
        q, k, v, attn_r, lse, _to_residue_order(dattn), _to_residue_order(pos_col), rot,
        exchanges=[to_chips([p_up, p_out])])
    dq, dk, dv = (_from_residue_order(t) for t in (dq, dk, dv))
    r_w_up_t = sum_chips("sum_chips_up", p_up, c_up, chip_ids)
    r_w_out = sum_chips("sum_chips_out", p_out, c_out, chip_ids)
    dproj = jnp.concatenate([dq, dk, dv, du, dvs], axis=1)
    g_w_in, _ = weight_grad("grad_w_in", dproj, h1)
    (grad_x, d_pre_mix), [(_, [s_in])] = in_bwd(
        dproj, w_in_t, x2d, pre_mix_norm, dx2, exchanges=[to_sibling([g_w_in])])
    p_in = sum_cores("sum_cores_in", g_w_in, s_in, core)

    d_b_sp = d_bfull.reshape(CHUNK, N_GROUPS, HEAD_DIM).sum(axis=-1).T
    small_g = _pack_small(dict(pre_mix_norm=d_pre_mix, sgu_ln_gain=d_ln_gain, sgu_ln_bias=d_ln_bias, sgu_w_spatial=d_w_sp,
                               sgu_b_spatial=d_b_sp, attn_out_norm=d_attn_out, sgu_out_norm=d_sgu_out,
                               post_mix_norm=d_post_mix, pre_ffn_norm=d_pre_ffn, post_ffn_norm=d_post_ffn,
                               loss_sum=sq_err))
    small_g = small_g.reshape(N_DEV, SMALL_ROWS // N_DEV, 128)
    [(_, [c_in]), (_, [o_small])] = exchange_only("grads_tail", [to_chips([p_in]), to_owners(small_g)])
    r_w_in_t = sum_chips("sum_chips_in", p_in, c_in, chip_ids)
    all_small = sum_owned("sum_small", small_g, o_small, dev_ids)
    (all_small,) = all_gather_in_place("gather_small_grads", [all_small])

    big = {}
    for name, w, g, m, vv in (("w_in", w_in, r_w_in_t, m_w_in, v_w_in), ("w_gate", w_gate, r_w_gate_t, m_w_gate, v_w_gate),
                              ("w_up", w_up, r_w_up_t, m_w_up, v_w_up)):
        d, nm, nv = adamw("adamw_" + name, w[0].T, g, m[0].T, vv[0].T)
        big[name] = (g.T[None], d.T[None], nm.T[None], nv.T[None])
    for name, w, g, m, vv in (("w_out", w_out, r_w_out, m_w_out, v_w_out), ("w_down", w_down, r_w_down, m_w_down, v_w_down)):
        d, nm, nv = adamw("adamw_" + name, w[0], g, m[0], vv[0])
        big[name] = (g[None], d[None], nm[None], nv[None])
    sd, snm, snv = adamw("adamw_small", _pack_small(small_w), all_small, _pack_small(small_m), _pack_small(small_v))
    sg, sd, snm, snv = (_unpack_small(t, small_w) for t in (all_small, sd, snm, snv))
    loss = sg["loss_sum"][0] * np.float32(0.5 / D_MODEL)

    names = ["pre_mix_norm", "w_in", "sgu_ln_gain", "sgu_ln_bias", "sgu_w_spatial", "sgu_b_spatial", "attn_out_norm",
             "sgu_out_norm", "w_out", "post_mix_norm", "pre_ffn_norm", "w_gate", "w_up", "w_down", "post_ffn_norm"]
    outs = [loss, grad_x[None]]
    for i, table in enumerate((sg, sd, snm, snv)):
        for name in names:
            outs.append(big[name][i] if name in big else table[name])
    return tuple(outs)
```

```python
import functools

import numpy as np
import jax
import jax.numpy as jnp
from jax import lax
from jax.experimental import pallas as pl
from jax.experimental.pallas import tpu as pltpu

F32 = jnp.float32
BF16 = jnp.bfloat16

SEQ = 2048
D_MODEL = 1024
ATTN_W = 512
SGU_W = 512
HEAD_DIM = 64
N_GROUPS = 8
CHUNK = 128
D_FF = 2816
IN_W = 3 * ATTN_W + 2 * SGU_W
DILATIONS = (1, 4, 16)
ROPE_THETA = 500000.0
ROT_DIM = 16
ROT_HALF = 8
RMS_EPS = 1e-6
LN_EPS = 1e-5
Q_SCALE = 0.125
NEG = -1e30

N_DEV = 8
MESH_AXES = ("x", "y", "c")
MESH = pl.DeviceIdType.MESH

ADAM_LR = 0.001
ADAM_B1 = 0.9
ADAM_B2 = 0.999
ADAM_EPS = 1e-08
ADAM_WD = 0.01
ADAM_STEP = 10

VMEM_LIMIT = 60 * 1024 * 1024
ANY = pl.BlockSpec(memory_space=pl.ANY)

SMALL = (("pre_mix_norm", 1024), ("sgu_ln_gain", 512), ("sgu_ln_bias", 512), ("sgu_w_spatial", 8 * 128 * 128),
         ("sgu_b_spatial", 1024), ("attn_out_norm", 512), ("sgu_out_norm", 512), ("post_mix_norm", 1024),
         ("pre_ffn_norm", 1024), ("post_ffn_norm", 1024), ("loss_sum", 1))
SMALL_ROWS = 1152


def _params(sem=("arbitrary",)):
    return pltpu.CompilerParams(dimension_semantics=sem, vmem_limit_bytes=VMEM_LIMIT)


def _dot(a, b):
    return jnp.dot(a, b, preferred_element_type=F32)


def _dot_nt(a, b):
    return lax.dot_general(a, b, (((1,), (1,)), ((), ())), preferred_element_type=F32)


def _dot_tn(a, b):
    return lax.dot_general(a, b, (((0,), (0,)), ((), ())), preferred_element_type=F32)


def _rms(z):
    return lax.rsqrt(jnp.mean(z * z, axis=-1, keepdims=True) + RMS_EPS)


def _rms_bwd(z, gain, d):
    r = _rms(z)
    n = z * r
    dn = d * gain
    dz = r * (dn - n * jnp.mean(dn * n, axis=-1, keepdims=True))
    return dz, jnp.sum(d * n, axis=0, keepdims=True)


def _gelu(z):
    return 0.5 * z * (1.0 + lax.erf(z * np.float32(1.0 / np.sqrt(2.0))))


def _gelu_grad(z):
    cdf = 0.5 * (1.0 + lax.erf(z * np.float32(1.0 / np.sqrt(2.0))))
    return cdf + z * jnp.exp(-0.5 * z * z) * np.float32(1.0 / np.sqrt(2.0 * np.pi))


def _rot_tables(pos_col, invf, ma, mb):
    ang = pos_col.astype(F32) * invf
    s = jnp.sin(ang)
    return jnp.cos(ang), s * ma, s * mb


def _rot(t, c, sa, sb):
    return t * c + pltpu.roll(t, 120, 1) * sa + pltpu.roll(t, 8, 1) * sb


def _rot_t(d, c, sa, sb):
    return d * c + pltpu.roll(d * sa, 8, 1) + pltpu.roll(d * sb, 120, 1)


def _rot_consts():
    lane = np.arange(128) % HEAD_DIM
    inv_freq = (np.float32(ROPE_THETA) ** (-np.arange(0, ROT_DIM, 2, dtype=np.float32) / np.float32(ROT_DIM))).astype(np.float32)
    invf = np.where(lane < ROT_DIM, inv_freq[lane % ROT_HALF], 0.0).astype(np.float32)
    ma = np.where(lane < ROT_HALF, -1.0, 0.0).astype(np.float32)
    mb = np.where((lane >= ROT_HALF) & (lane < ROT_DIM), 1.0, 0.0).astype(np.float32)
    return jnp.asarray(invf[None]), jnp.asarray(ma[None]), jnp.asarray(mb[None])


def _row_spec(tm, w):
    return pl.BlockSpec((tm, w), lambda i: (i, 0))


def _full_spec(shape):
    return pl.BlockSpec(shape, lambda i: (0,) * len(shape))


def in_proj(x, pos_col, g1, w_in_t, rot, exchanges=()):
    tm = 512

    def body(x_ref, pos_ref, g_ref, w_ref, invf_ref, ma_ref, mb_ref, h_ref, q_ref, k_ref, v_ref, u_ref, vs_ref):
        xf = x_ref[...]
        h = (xf * _rms(xf) * g_ref[...]).astype(BF16)
        h_ref[...] = h
        proj = _dot_nt(h, w_ref[...])
        c, sa, sb = _rot_tables(pos_ref[...], invf_ref[...], ma_ref[...], mb_ref[...])
        for j in range(ATTN_W // 128):
            q_ref[:, j * 128:(j + 1) * 128] = _rot(proj[:, j * 128:(j + 1) * 128], c, sa, sb) * Q_SCALE
            k_ref[:, j * 128:(j + 1) * 128] = _rot(proj[:, ATTN_W + j * 128:ATTN_W + (j + 1) * 128], c, sa, sb)
        v_ref[...] = proj[:, 2 * ATTN_W:3 * ATTN_W]
        u_ref[...] = proj[:, 3 * ATTN_W:3 * ATTN_W + SGU_W]
        vs_ref[...] = proj[:, 3 * ATTN_W + SGU_W:]

    act = jax.ShapeDtypeStruct((SEQ, 512), F32)
    return _hosted(
        "in_proj", body, SEQ // tm,
        [_row_spec(tm, D_MODEL), _row_spec(tm, 1), _full_spec((1, D_MODEL)), _full_spec((IN_W, D_MODEL)),
         _full_spec((1, 128)), _full_spec((1, 128)), _full_spec((1, 128))],
        [_row_spec(tm, D_MODEL)] + [_row_spec(tm, 512)] * 5,
        [jax.ShapeDtypeStruct((SEQ, D_MODEL), BF16)] + [act] * 5,
        (x, pos_col, g1, w_in_t, *rot), exchanges=exchanges)


RES = 16


def _to_residue_order(t):
    return t.reshape(SEQ // RES, RES, -1).transpose(1, 0, 2).reshape(t.shape)


def _from_residue_order(t):
    return t.reshape(RES, SEQ // RES, -1).transpose(1, 0, 2).reshape(t.shape)


def _block_rows(d, r, n):
    if d == 16:
        slices = [(128 * r, 128)]
    elif d == 4:
        slices = [(128 * (4 * b + r) + 32 * n, 32) for b in range(4)]
    else:
        slices = [(128 * b + 8 * n, 8) for b in range(RES)]
    return [(s if isinstance(s, int) else pl.multiple_of(s, z), z) for s, z in slices]


def _block_step(d, i):
    if d == 16:
        return i
    if d == 4:
        return 4 * (i & 31) + (i >> 5)
    return 16 * (i & 7) + (i >> 3)


def _attn_masks(d):
    row2 = _block_step(d, lax.broadcasted_iota(jnp.int32, (128, 256), 0))
    col2 = lax.broadcasted_iota(jnp.int32, (128, 256), 1)
    key2 = _block_step(d, col2 & 127)
    mask2 = jnp.logical_or(jnp.logical_and(col2 < 128, key2 >= row2), jnp.logical_and(col2 >= 128, key2 <= row2))
    row1 = _block_step(d, lax.broadcasted_iota(jnp.int32, (128, 128), 0))
    col1 = lax.broadcasted_iota(jnp.int32, (128, 128), 1)
    return col1 < HEAD_DIM, _block_step(d, col1) <= row1, mask2


def _load_rows(ref, slices):
    parts = [ref[pl.ds(s, z), :] for s, z in slices]
    return parts[0] if len(parts) == 1 else jnp.concatenate(parts, axis=0)


def _for_each_group(fn):
    for p, d in enumerate(DILATIONS):
        masks = _attn_masks(d)
        if d == 16:
            def group(i, carry, p=p, masks=masks):
                fn(p, masks, [(_block_rows(16, 4 * i + g, 0), None) for g in range(4)])
                return carry

            lax.fori_loop(0, 4, group, 0)
        elif d == 4:
            fn(p, masks, [(_block_rows(4, r, 0), None) for r in range(4)])

            def group(i, carry, p=p, masks=masks):
                fn(p, masks, [(_block_rows(4, r, i + 1), _block_rows(4, r, i)) for r in range(4)])
                return carry

            lax.fori_loop(0, 3, group, 0)
        else:
            fn(p, masks, [(_block_rows(1, 0, 0), None)])

            def group(i, carry, p=p, masks=masks):
                fn(p, masks, [(_block_rows(1, 0, 3 * i + g + 1), _block_rows(1, 0, 3 * i + g)) for g in range(3)])
                return carry

            lax.fori_loop(0, 5, group, 0)


def attn_fwd(q, k, v, exchanges=()):
    def body(q_ref, k_ref, v_ref, o_ref, lse_ref, op_ref, lp_ref):
        def group(p, masks, blocks):
            head0, mask1, mask2 = masks
            heads = (head0, jnp.logical_not(head0))
            keys = [rows if prev is None else prev + rows for rows, prev in blocks]
            mask = [mask1 if prev is None else mask2 for _, prev in blocks]
            qb = [_load_rows(q_ref, rows) for rows, _ in blocks]
            kk = [_load_rows(k_ref, ks).astype(BF16) for ks in keys]
            vv = [_load_rows(v_ref, ks).astype(BF16) for ks in keys]
            chains = [(g, hm) for g in range(len(blocks)) for hm in heads]
            s = [jnp.where(mask[g], _dot_nt(jnp.where(hm, qb[g], 0.0).astype(BF16), kk[g]), NEG) for g, hm in chains]
            m = [jnp.max(t, axis=-1, keepdims=True) for t in s]
            e = [jnp.exp(t - mt) for t, mt in zip(s, m)]
            l = [jnp.sum(t, axis=-1, keepdims=True) for t in e]
            pv = [_dot(t.astype(BF16), vv[g]) for t, (g, _) in zip(e, chains)]
            for g, (rows, _) in enumerate(blocks):
                o_blk = jnp.where(head0, pv[2 * g] / l[2 * g], pv[2 * g + 1] / l[2 * g + 1])
                l_blk = jnp.where(head0, jnp.broadcast_to(m[2 * g] + jnp.log(l[2 * g]), (128, 128)),
                                  jnp.broadcast_to(m[2 * g + 1] + jnp.log(l[2 * g + 1]), (128, 128)))
                at = 0
                for start, size in rows:
                    op_ref[p, pl.ds(start, size), :] = o_blk[at:at + size]
                    lp_ref[p, pl.ds(start, size), :] = l_blk[at:at + size]
                    at += size

        _for_each_group(group)

        def combine(i, carry):
            rows = pl.ds(pl.multiple_of(i * 256, 256), 256)
            ls = [lp_ref[p, rows, :] for p in range(3)]
            m = jnp.maximum(jnp.maximum(ls[0], ls[1]), ls[2])
            lse = m + jnp.log(jnp.exp(ls[0] - m) + jnp.exp(ls[1] - m) + jnp.exp(ls[2] - m))
            o = jnp.zeros((256, 128), F32)
            for p in range(3):
                o = o + jnp.exp(ls[p] - lse) * op_ref[p, rows, :]
            o_ref[rows, :] = o
            lse_ref[rows, :] = lse
            return carry

        lax.fori_loop(0, SEQ // 256, combine, 0)

    slab = pl.BlockSpec((SEQ, 128), lambda i: (0, i))
    out = jax.ShapeDtypeStruct((SEQ, ATTN_W), F32)
    return _hosted(
        "attn_fwd", body, ATTN_W // 128, [slab] * 3, [slab] * 2, [out, out], (q, k, v),
        scratch_shapes=[pltpu.VMEM((3, SEQ, 128), F32), pltpu.VMEM((3, SEQ, 128), F32)], exchanges=exchanges)


def _causal_weights(w_ref):
    row = lax.broadcasted_iota(jnp.int32, (CHUNK, CHUNK), 0)
    col = lax.broadcasted_iota(jnp.int32, (CHUNK, CHUNK), 1)
    return [jnp.where(col <= row, w_ref[g], 0.0).astype(BF16) for g in range(N_GROUPS)], col <= row


def _sgu_chunk_fwd(u, vs, lg, lb, wc, bfull, head0):
    ug = _gelu(u)
    vg = _gelu(vs)
    xc = vg - jnp.mean(vg, axis=-1, keepdims=True)
    rstd = lax.rsqrt(jnp.mean(xc * xc, axis=-1, keepdims=True) + LN_EPS)
    xhat = xc * rstd
    vn = xhat * lg + lb
    mixed = []
    for gp in range(SGU_W // 128):
        vp = vn[:, gp * 128:(gp + 1) * 128].astype(BF16)
        mixed.append(jnp.where(head0, _dot(wc[2 * gp], vp), _dot(wc[2 * gp + 1], vp)))
    ms = jnp.concatenate(mixed, axis=1) + bfull
    return ug, xhat, rstd, vn, ms


def sgu_fwd(u, vs, lg, lb, w_sp, bfull, exchanges=()):
    cpb = 4

    def body(u_ref, vs_ref, lg_ref, lb_ref, w_ref, b_ref, o_ref):
        wc, _ = _causal_weights(w_ref)
        head0 = lax.broadcasted_iota(jnp.int32, (CHUNK, 128), 1) < HEAD_DIM
        for ci in range(cpb):
            rows = pl.ds(ci * CHUNK, CHUNK)
            ug, _, _, _, ms = _sgu_chunk_fwd(u_ref[rows, :], vs_ref[rows, :], lg_ref[...], lb_ref[...], wc, b_ref[...], head0)
            o_ref[rows, :] = ug * ms

    tm = cpb * CHUNK
    return _hosted(
        "sgu_fwd", body, SEQ // tm,
        [_row_spec(tm, SGU_W), _row_spec(tm, SGU_W), _full_spec((1, SGU_W)), _full_spec((1, SGU_W)),
         _full_spec((N_GROUPS, CHUNK, CHUNK)), _full_spec((CHUNK, SGU_W))],
        [_row_spec(tm, SGU_W)], [jax.ShapeDtypeStruct((SEQ, SGU_W), F32)],
        (u, vs, lg, lb, w_sp, bfull), exchanges=exchanges)


def out_proj(attn, sgu, x, ga, gs, w_out, gpm, gpf, exchanges=()):
    tm = 512

    def body(a_ref, s_ref, x_ref, ga_ref, gs_ref, w_ref, gpm_ref, gpf_ref, mix_ref, y_ref, x2_ref, h2_ref):
        a = a_ref[...]
        s = s_ref[...]
        an = (a * _rms(a) * ga_ref[...]).astype(BF16)
        sn = (s * _rms(s) * gs_ref[...]).astype(BF16)
        mix_ref[:, :ATTN_W] = an
        mix_ref[:, ATTN_W:] = sn
        y = _dot(an, w_ref[:ATTN_W, :]) + _dot(sn, w_ref[ATTN_W:, :])
        y_ref[...] = y
        x2 = x_ref[...] + y * _rms(y) * gpm_ref[...]
        x2_ref[...] = x2
        h2_ref[...] = (x2 * _rms(x2) * gpf_ref[...]).astype(BF16)

    wide = jax.ShapeDtypeStruct((SEQ, D_MODEL), F32)
    wide16 = jax.ShapeDtypeStruct((SEQ, D_MODEL), BF16)
    return _hosted(
        "out_proj", body, SEQ // tm,
        [_row_spec(tm, ATTN_W), _row_spec(tm, SGU_W), _row_spec(tm, D_MODEL), _full_spec((1, ATTN_W)),
         _full_spec((1, SGU_W)), _full_spec((D_MODEL, D_MODEL)), _full_spec((1, D_MODEL)), _full_spec((1, D_MODEL))],
        [_row_spec(tm, D_MODEL)] * 4, [wide16, wide, wide, wide16],
        (attn, sgu, x, ga, gs, w_out, gpm, gpf), exchanges=exchanges)


def ffn_up(h2, w_gate_t, w_up_t, exchanges=()):
    tm = 256

    def body(h_ref, wg_ref, wu_ref, g_ref, u_ref, a_ref):
        h = h_ref[...]
        g = _dot_nt(h, wg_ref[...])
        u = _dot_nt(h, wu_ref[...])
        g_ref[...] = g
        u_ref[...] = u
        a_ref[...] = (g * jax.nn.sigmoid(g) * u).astype(BF16)

    ff = jax.ShapeDtypeStruct((SEQ, D_FF), F32)
    return _hosted(
        "ffn_up", body, SEQ // tm,
        [_row_spec(tm, D_MODEL), _full_spec((D_FF, D_MODEL)), _full_spec((D_FF, D_MODEL))],
        [_row_spec(tm, D_FF)] * 3, [ff, ff, jax.ShapeDtypeStruct((SEQ, D_FF), BF16)],
        (h2, w_gate_t, w_up_t), exchanges=exchanges)


def ffn_down_loss(act, w_down, x2, gpo, target):
    tm = 512

    def body(a_ref, w_ref, x2_ref, g_ref, t_ref, df_ref, dx3_ref, dg_ref, loss_ref):
        f = _dot(a_ref[...], w_ref[...])
        gain = g_ref[...]
        err = x2_ref[...] + f * _rms(f) * gain - t_ref[...]
        dx3 = err * np.float32(1.0 / D_MODEL)
        dx3_ref[...] = dx3
        df, dg = _rms_bwd(f, gain, dx3)
        df_ref[...] = df.astype(BF16)

        @pl.when(pl.program_id(0) == 0)
        def _():
            dg_ref[...] = jnp.zeros_like(dg_ref)
            loss_ref[...] = jnp.zeros_like(loss_ref)

        dg_ref[...] += dg
        loss_ref[...] += jnp.sum(err * err, axis=(0, 1), keepdims=True)

    return pl.pallas_call(
        body, name="ffn_down_loss", grid=(SEQ // tm,),
        in_specs=[_row_spec(tm, D_FF), _full_spec((D_FF, D_MODEL)), _row_spec(tm, D_MODEL), _full_spec((1, D_MODEL)),
                  _row_spec(tm, D_MODEL)],
        out_specs=[_row_spec(tm, D_MODEL), _row_spec(tm, D_MODEL), _full_spec((1, D_MODEL)), _full_spec((1, 1))],
        out_shape=[jax.ShapeDtypeStruct((SEQ, D_MODEL), BF16), jax.ShapeDtypeStruct((SEQ, D_MODEL), F32),
                   jax.ShapeDtypeStruct((1, D_MODEL), F32), jax.ShapeDtypeStruct((1, 1), F32)],
        compiler_params=_params(),
    )(act, w_down, x2, gpo, target)


def ffn_act_bwd(df, w_down, gate, up, exchanges=()):
    tm = 256

    def body(df_ref, w_ref, g_ref, u_ref, dg_ref, du_ref):
        dact = _dot_nt(df_ref[...], w_ref[...])
        g = g_ref[...]
        s = jax.nn.sigmoid(g)
        du_ref[...] = (dact * g * s).astype(BF16)
        dg_ref[...] = (dact * u_ref[...] * (s * (1.0 + g * (1.0 - s)))).astype(BF16)

    ff16 = jax.ShapeDtypeStruct((SEQ, D_FF), BF16)
    return _hosted(
        "ffn_act_bwd", body, SEQ // tm,
        [_row_spec(tm, D_MODEL), _full_spec((D_FF, D_MODEL)), _row_spec(tm, D_FF), _row_spec(tm, D_FF)],
        [_row_spec(tm, D_FF)] * 2, [ff16, ff16], (df, w_down, gate, up), exchanges=exchanges)


def ffn_in_bwd(dgate, dup, w_gate_t, w_up_t, x2, gpf, dx3, y, gpm, exchanges=()):
    tm = 256

    def body(dg_ref, du_ref, wg_ref, wu_ref, x2_ref, gpf_ref, dx3_ref, y_ref, gpm_ref, dx2_ref, dy_ref, dgpf_ref, dgpm_ref):
        dh2 = _dot(dg_ref[...], wg_ref[...]) + _dot(du_ref[...], wu_ref[...])
        dz, dgpf = _rms_bwd(x2_ref[...], gpf_ref[...], dh2)
        dx2 = dx3_ref[...] + dz
        dx2_ref[...] = dx2
        dy, dgpm = _rms_bwd(y_ref[...], gpm_ref[...], dx2)
        dy_ref[...] = dy.astype(BF16)

        @pl.when(pl.program_id(0) == 0)
        def _():
            dgpf_ref[...] = jnp.zeros_like(dgpf_ref)
            dgpm_ref[...] = jnp.zeros_like(dgpm_ref)

        dgpf_ref[...] += dgpf
        dgpm_ref[...] += dgpm

    vec = jax.ShapeDtypeStruct((1, D_MODEL), F32)
    return _hosted(
        "ffn_in_bwd", body, SEQ // tm,
        [_row_spec(tm, D_FF), _row_spec(tm, D_FF), _full_spec((D_FF, D_MODEL)), _full_spec((D_FF, D_MODEL)),
         _row_spec(tm, D_MODEL), _full_spec((1, D_MODEL)), _row_spec(tm, D_MODEL), _row_spec(tm, D_MODEL),
         _full_spec((1, D_MODEL))],
        [_row_spec(tm, D_MODEL), _row_spec(tm, D_MODEL), _full_spec((1, D_MODEL)), _full_spec((1, D_MODEL))],
        [jax.ShapeDtypeStruct((SEQ, D_MODEL), F32), jax.ShapeDtypeStruct((SEQ, D_MODEL), BF16), vec, vec],
        (dgate, dup, w_gate_t, w_up_t, x2, gpf, dx3, y, gpm), exchanges=exchanges)


def weight_grad(name, a, b, exchanges=()):
    m, n = a.shape[1], b.shape[1]
    tr = 256

    def body(a_ref, b_ref, o_ref):
        o_ref[...] = _dot_tn(a_ref[...], b_ref[...]).astype(BF16)

    (out,), done = _hosted(
        name, body, m // tr, [pl.BlockSpec((SEQ, tr), lambda i: (0, i)), _full_spec((SEQ, n))],
        [_row_spec(tr, n)], [jax.ShapeDtypeStruct((m, n), BF16)], (a, b), exchanges=exchanges)
    return out.reshape(N_DEV, m // N_DEV, n), done


def mix_bwd(dy, w_out, attn, sgu, ga, gs, exchanges=()):
    tm = 512

    def body(dy_ref, w_ref, a_ref, s_ref, ga_ref, gs_ref, da_ref, ds_ref, dga_ref, dgs_ref):
        dy = dy_ref[...]
        da, dga = _rms_bwd(a_ref[...], ga_ref[...], _dot_nt(dy, w_ref[:ATTN_W, :]))
        ds, dgs = _rms_bwd(s_ref[...], gs_ref[...], _dot_nt(dy, w_ref[ATTN_W:, :]))
        da_ref[...] = da
        ds_ref[...] = ds

        @pl.when(pl.program_id(0) == 0)
        def _():
            dga_ref[...] = jnp.zeros_like(dga_ref)
            dgs_ref[...] = jnp.zeros_like(dgs_ref)

        dga_ref[...] += dga
        dgs_ref[...] += dgs

    half = jax.ShapeDtypeStruct((SEQ, 512), F32)
    vec = jax.ShapeDtypeStruct((1, 512), F32)
    return _hosted(
        "mix_bwd", body, SEQ // tm,
        [_row_spec(tm, D_MODEL), _full_spec((D_MODEL, D_MODEL)), _row_spec(tm, 512), _row_spec(tm, 512),
         _full_spec((1, 512)), _full_spec((1, 512))],
        [_row_spec(tm, 512), _row_spec(tm, 512), _full_spec((1, 512)), _full_spec((1, 512))],
        [half, half, vec, vec], (dy, w_out, attn, sgu, ga, gs), exchanges=exchanges)


def sgu_bwd(u, vs, dsgu, lg, lb, w_sp, bfull):
    cpb = 4

    def body(u_ref, vs_ref, d_ref, lg_ref, lb_ref, w_ref, b_ref, du_ref, dvs_ref, dlg_ref, dlb_ref, dw_ref, db_ref):
        wc, causal = _causal_weights(w_ref)
        head0 = lax.broadcasted_iota(jnp.int32, (CHUNK, 128), 1) < HEAD_DIM
        lg = lg_ref[...]

        @pl.when(pl.program_id(0) == 0)
        def _():
            dlg_ref[...] = jnp.zeros_like(dlg_ref)
            dlb_ref[...] = jnp.zeros_like(dlb_ref)
            dw_ref[...] = jnp.zeros_like(dw_ref)
            db_ref[...] = jnp.zeros_like(db_ref)

        for ci in range(cpb):
            rows = pl.ds(ci * CHUNK, CHUNK)
            u = u_ref[rows, :]
            vs = vs_ref[rows, :]
            d = d_ref[rows, :]
            ug, xhat, rstd, vn, ms = _sgu_chunk_fwd(u, vs, lg, lb_ref[...], wc, b_ref[...], head0)
            du_ref[rows, :] = (d * ms * _gelu_grad(u)).astype(BF16)
            dms = d * ug
            db_ref[...] += dms
            dvn = []
            for gp in range(SGU_W // 128):
                dmp = dms[:, gp * 128:(gp + 1) * 128]
                dm0 = jnp.where(head0, dmp, 0.0).astype(BF16)
                dm1 = jnp.where(head0, 0.0, dmp).astype(BF16)
                vp = vn[:, gp * 128:(gp + 1) * 128].astype(BF16)
                dw_ref[2 * gp] += _dot_nt(dm0, vp)
                dw_ref[2 * gp + 1] += _dot_nt(dm1, vp)
                dvn.append(_dot_tn(wc[2 * gp], dm0) + _dot_tn(wc[2 * gp + 1], dm1))
            dvn = jnp.concatenate(dvn, axis=1)
            dlg_ref[...] += jnp.sum(dvn * xhat, axis=0, keepdims=True)
            dlb_ref[...] += jnp.sum(dvn, axis=0, keepdims=True)
            dxh = dvn * lg
            dvg = rstd * (dxh - jnp.mean(dxh, axis=-1, keepdims=True) - xhat * jnp.mean(dxh * xhat, axis=-1, keepdims=True))
            dvs_ref[rows, :] = (dvg * _gelu_grad(vs)).astype(BF16)

        @pl.when(pl.program_id(0) == pl.num_programs(0) - 1)
        def _():
            for g in range(N_GROUPS):
                dw_ref[g] = jnp.where(causal, dw_ref[g], 0.0)

    tm = cpb * CHUNK
    half16 = jax.ShapeDtypeStruct((SEQ, SGU_W), BF16)
    vec = jax.ShapeDtypeStruct((1, SGU_W), F32)
    return pl.pallas_call(
        body, name="sgu_bwd", grid=(SEQ // tm,),
        in_specs=[_row_spec(tm, SGU_W)] * 3 + [_full_spec((1, SGU_W)), _full_spec((1, SGU_W)),
                                                  _full_spec((N_GROUPS, CHUNK, CHUNK)), _full_spec((CHUNK, SGU_W))],
        out_specs=[_row_spec(tm, SGU_W), _row_spec(tm, SGU_W), _full_spec((1, SGU_W)), _full_spec((1, SGU_W)),
                   _full_spec((N_GROUPS, CHUNK, CHUNK)), _full_spec((CHUNK, SGU_W))],
        out_shape=[half16, half16, vec, vec, jax.ShapeDtypeStruct((N_GROUPS, CHUNK, CHUNK), F32),
                   jax.ShapeDtypeStruct((CHUNK, SGU_W), F32)],
        compiler_params=_params(),
    )(u, vs, dsgu, lg, lb, w_sp, bfull)


def attn_bwd(q, k, v, o, lse, do, pos_col, rot, exchanges=()):
    def body(q_ref, k_ref, v_ref, o_ref, lse_ref, do_ref, pos_ref, invf_ref, ma_ref, mb_ref,
             dq_ref, dk_ref, dv_ref, dqa_ref, dka_ref, dva_ref, dlt_ref, rot_ref):
        dqa_ref[...] = jnp.zeros_like(dqa_ref)
        dka_ref[...] = jnp.zeros_like(dka_ref)
        dva_ref[...] = jnp.zeros_like(dva_ref)

        def delta(i, carry):
            rows = pl.ds(pl.multiple_of(i * 256, 256), 256)
            prod = do_ref[rows, :] * o_ref[rows, :]
            h0 = lax.broadcasted_iota(jnp.int32, (256, 128), 1) < HEAD_DIM
            d0 = jnp.sum(jnp.where(h0, prod, 0.0), axis=-1, keepdims=True)
            d1 = jnp.sum(jnp.where(h0, 0.0, prod), axis=-1, keepdims=True)
            dlt_ref[rows, :] = jnp.where(h0, d0, d1)
            return carry

        lax.fori_loop(0, SEQ // 256, delta, 0)

        def add_rows(ref, slices, val):
            at = 0
            for start, size in slices:
                ref[pl.ds(start, size), :] += val[at:at + size]
                at += size

        def group(p, masks, blocks):
            head0, mask1, mask2 = masks
            heads = (head0, jnp.logical_not(head0))
            keys = [rows if prev is None else prev + rows for rows, prev in blocks]
            mask = [mask1 if prev is None else mask2 for _, prev in blocks]
            kk = [_load_rows(k_ref, ks).astype(BF16) for ks in keys]
            vv = [_load_rows(v_ref, ks).astype(BF16) for ks in keys]
            qb = [_load_rows(q_ref, rows) for rows, _ in blocks]
            dob = [_load_rows(do_ref, rows) for rows, _ in blocks]
            lse_b = [_load_rows(lse_ref, rows) for rows, _ in blocks]
            dlt_b = [_load_rows(dlt_ref, rows) for rows, _ in blocks]
            chains = [(g, h) for g in range(len(blocks)) for h in range(2)]
            qm = [jnp.where(heads[h], qb[g], 0.0).astype(BF16) for g, h in chains]
            dom = [jnp.where(heads[h], dob[g], 0.0).astype(BF16) for g, h in chains]
            s = [_dot_nt(qm[c], kk[g]) for c, (g, h) in enumerate(chains)]
            dp = [_dot_nt(dom[c], vv[g]) for c, (g, h) in enumerate(chains)]
            pr = [jnp.where(mask[g], jnp.exp(s[c] - lse_b[g][:, h * HEAD_DIM:h * HEAD_DIM + 1]), 0.0)
                  for c, (g, h) in enumerate(chains)]
            ds = [(pr[c] * (dp[c] - dlt_b[g][:, h * HEAD_DIM:h * HEAD_DIM + 1])).astype(BF16)
                  for c, (g, h) in enumerate(chains)]
            dv = [_dot_tn(pr[c].astype(BF16), dom[c]) for c in range(len(chains))]
            dk = [_dot_tn(ds[c], qm[c]) for c in range(len(chains))]
            dq = [_dot(ds[c], kk[g]) for c, (g, h) in enumerate(chains)]
            for g, (rows, _) in enumerate(blocks):
                add_rows(dqa_ref, rows, jnp.where(head0, dq[2 * g], dq[2 * g + 1]))
                add_rows(dka_ref, keys[g], dk[2 * g] + dk[2 * g + 1])
                add_rows(dva_ref, keys[g], dv[2 * g] + dv[2 * g + 1])

        _for_each_group(group)

        @pl.when(pl.program_id(0) == 0)
        def _():
            def tables(i, carry):
                rows = pl.ds(pl.multiple_of(i * 256, 256), 256)
                c, sa, sb = _rot_tables(pos_ref[rows, :], invf_ref[...], ma_ref[...], mb_ref[...])
                rot_ref[0, rows, :] = c
                rot_ref[1, rows, :] = sa
                rot_ref[2, rows, :] = sb
                return carry

            lax.fori_loop(0, SEQ // 256, tables, 0)

        def finish(i, carry):
            rows = pl.ds(pl.multiple_of(i * 256, 256), 256)
            c, sa, sb = rot_ref[0, rows, :], rot_ref[1, rows, :], rot_ref[2, rows, :]
            dq_ref[rows, :] = _rot_t(dqa_ref[rows, :] * Q_SCALE, c, sa, sb).astype(BF16)
            dk_ref[rows, :] = _rot_t(dka_ref[rows, :], c, sa, sb).astype(BF16)
            dv_ref[rows, :] = dva_ref[rows, :].astype(BF16)
            return carry

        lax.fori_loop(0, SEQ // 256, finish, 0)

    slab = pl.BlockSpec((SEQ, 128), lambda i: (0, i))
    out = jax.ShapeDtypeStruct((SEQ, ATTN_W), BF16)
    acc = pltpu.VMEM((SEQ, 128), F32)
    return _hosted(
        "attn_bwd", body, ATTN_W // 128,
        [slab] * 6 + [_full_spec((SEQ, 1)), _full_spec((1, 128)), _full_spec((1, 128)), _full_spec((1, 128))],
        [slab] * 3, [out, out, out], (q, k, v, o, lse, do, pos_col, *rot),
        scratch_shapes=[acc, acc, acc, acc, pltpu.VMEM((3, SEQ, 128), F32)], exchanges=exchanges)


def in_bwd(dproj, w_in_t, x, g1, dx2, exchanges=()):
    tm = 512

    def body(dp_ref, w_ref, x_ref, g_ref, dx2_ref, dx_ref, dg_ref):
        dh1 = _dot(dp_ref[...], w_ref[...])
        dz, dg = _rms_bwd(x_ref[...], g_ref[...], dh1)
        dx_ref[...] = dx2_ref[...] + dz

        @pl.when(pl.program_id(0) == 0)
        def _():
            dg_ref[...] = jnp.zeros_like(dg_ref)

        dg_ref[...] += dg

    return _hosted(
        "in_bwd", body, SEQ // tm,
        [_row_spec(tm, IN_W), _full_spec((IN_W, D_MODEL)), _row_spec(tm, D_MODEL), _full_spec((1, D_MODEL)),
         _row_spec(tm, D_MODEL)],
        [_row_spec(tm, D_MODEL), _full_spec((1, D_MODEL))],
        [jax.ShapeDtypeStruct((SEQ, D_MODEL), F32), jax.ShapeDtypeStruct((1, D_MODEL), F32)],
        (dproj, w_in_t, x, g1, dx2), exchanges=exchanges)


def _coords():
    return lax.axis_index("x"), lax.axis_index("y"), lax.axis_index("c")


class Exchange:
    def __init__(self, srcs, bufs, new_shapes, n_sems, make, mid_step=None):
        self.srcs, self.bufs, self.new_shapes, self.n_sems, self.make = list(srcs), list(bufs), list(new_shapes), n_sems, make
        self.mid_step = mid_step


def _hosted(name, body, n_steps, in_specs, out_specs, out_shape, args, scratch_shapes=(), exchanges=(), prefetch=None):
    out_shape, out_specs = list(out_shape), list(out_specs)
    srcs = [a for ex in exchanges for a in ex.srcs]
    bufs = [a for ex in exchanges for a in ex.bufs]
    news = [s for ex in exchanges for s in ex.new_shapes]
    n_pre = 0 if prefetch is None else 1
    n_in, n_out, n_scr = len(args), len(out_shape), len(scratch_shapes)

    def wrapped(*refs):
        refs = list(refs)
        pre, refs = refs[:n_pre], refs[n_pre:]
        ins = refs[:n_in]
        src_refs = refs[n_in:n_in + len(srcs)]
        at = n_in + len(srcs) + len(bufs)
        outs = refs[at:at + n_out]
        buf_refs = refs[at + n_out:at + n_out + len(bufs)]
        new_refs = refs[at + n_out + len(bufs):at + n_out + len(bufs) + len(news)]
        at += n_out + len(bufs) + len(news)
        scratch, sems = refs[at:at + n_scr], refs[at + n_scr:]

        def copies(phase, which):
            made, si, bi, ni = [], 0, 0, 0
            for k, ex in enumerate(exchanges):
                if which(ex):
                    made.append(ex.make(phase, src_refs[si:si + len(ex.srcs)], buf_refs[bi:bi + len(ex.bufs)],
                                        new_refs[ni:ni + len(ex.new_shapes)], sems[2 * k], sems[2 * k + 1]))
                si, bi, ni = si + len(ex.srcs), bi + len(ex.bufs), ni + len(ex.new_shapes)
            return made

        if exchanges:
            @pl.when(pl.program_id(0) == 0)
            def _():
                for starts in copies("start", lambda ex: True):
                    for cp in starts:
                        cp.start()

        def pass_on(mid):
            for arrivals, starts in copies("middle", lambda ex: ex.mid_step == mid):
                for cp in arrivals:
                    cp.wait_recv()
                for cp in starts:
                    cp.start()

        for mid in sorted({ex.mid_step for ex in exchanges if isinstance(ex.mid_step, int)}):
            @pl.when(pl.program_id(0) == mid)
            def _(mid=mid):
                pass_on(mid)

        body(*pre, *ins, *outs, *scratch)

        if exchanges:
            @pl.when(pl.program_id(0) == n_steps - 1)
            def _():
                pass_on("end")
                for sends, recvs in copies("end", lambda ex: True):
                    for cp in sends:
                        cp.wait_send()
                    for cp in recvs:
                        cp.wait_recv()

    sem_shapes = []
    for ex in exchanges:
        sem_shapes += [pltpu.SemaphoreType.DMA((ex.n_sems,)), pltpu.SemaphoreType.DMA((ex.n_sems,))]
    all_in = list(in_specs) + [ANY] * (len(srcs) + len(bufs))
    all_out = out_specs + [ANY] * (len(bufs) + len(news))
    all_shape = out_shape + [jax.ShapeDtypeStruct(b.shape, b.dtype) for b in bufs] + news
    all_scratch = list(scratch_shapes) + sem_shapes
    aliases = {n_pre + n_in + len(srcs) + i: n_out + i for i in range(len(bufs))}
    if prefetch is None:
        call = pl.pallas_call(wrapped, name=name, grid=(n_steps,), in_specs=all_in, out_specs=all_out, out_shape=all_shape,
                              scratch_shapes=all_scratch, input_output_aliases=aliases, compiler_params=_params())
        outs = call(*args, *srcs, *bufs)
    else:
        spec = pltpu.PrefetchScalarGridSpec(num_scalar_prefetch=1, grid=(n_steps,), in_specs=all_in, out_specs=all_out,
                                            scratch_shapes=all_scratch)
        call = pl.pallas_call(wrapped, name=name, grid_spec=spec, out_shape=all_shape, input_output_aliases=aliases,
                              compiler_params=_params())
        outs = call(prefetch, *args, *srcs, *bufs)
    results, bi, ni = [], n_out, n_out + len(bufs)
    for ex in exchanges:
        results.append((list(outs[bi:bi + len(ex.bufs)]), list(outs[ni:ni + len(ex.new_shapes)])))
        bi, ni = bi + len(ex.bufs), ni + len(ex.new_shapes)
    return list(outs[:n_out]), results


def _gather_copies(kinds, bufs, ranges, send_sems, recv_sems):
    x, y, c = _coords()
    me, sibling = (x, y, c), (x, y, 1 - c)
    chips = [(1 - x, y), (x, 1 - y), (1 - x, 1 - y)]

    def copy(a, k, block, to):
        lo, hi = ranges[a]
        r = bufs[a].shape[0] // N_DEV
        rows = bufs[a].at[pl.ds((4 * block[0] + 2 * block[1] + block[2]) * r + lo, hi - lo), :]
        return pltpu.make_async_remote_copy(src_ref=rows, dst_ref=rows, send_sem=send_sems.at[7 * a + k],
                                            recv_sem=recv_sems.at[7 * a + k], device_id=to, device_id_type=MESH)

    every = range(len(bufs))
    make = {
        "out": lambda: [copy(a, 0, me, sibling) for a in every]
        + [copy(a, 1 + j, me, (*chip, c)) for a in every for j, chip in enumerate(chips)],
        "from_core": lambda: [copy(a, 0, sibling, me) for a in every],
        "from_chips": lambda: [copy(a, 1 + j, (*chip, c), me) for a in every for j, chip in enumerate(chips)],
        "on": lambda: [copy(a, 4 + j, (*chip, c), sibling) for a in every for j, chip in enumerate(chips)],
        "on_in": lambda: [copy(a, 4 + j, (*chip, 1 - c), me) for a in every for j, chip in enumerate(chips)],
    }
    return [make[kind]() for kind in kinds]


def gather(bufs, mid_step, ranges=None):
    ranges = ranges or [(0, b.shape[0] // N_DEV) for b in bufs]

    def make(phase, src_refs, buf_refs, new_refs, send_sems, recv_sems):
        kinds = {"start": ["out"], "middle": ["from_chips", "on"], "end": ["out", "on", "from_core", "on_in"]}[phase]
        made = _gather_copies(kinds, buf_refs, ranges, send_sems, recv_sems)
        if phase == "start":
            return made[0]
        if phase == "middle":
            return made[0], made[1]
        return made[0] + made[1], made[2] + made[3]

    return Exchange([], bufs, [], 7 * len(bufs), make, mid_step=mid_step)


def all_gather_in_place(name, bufs):
    n = len(bufs)
    ranges = [(0, b.shape[0] // N_DEV) for b in bufs]

    def body(*refs):
        outs, send_sems, recv_sems = refs[n:2 * n], refs[2 * n], refs[2 * n + 1]
        out, from_core, from_chips, on, on_in = _gather_copies(
            ["out", "from_core", "from_chips", "on", "on_in"], outs, ranges, send_sems, recv_sems)
        for cp in out:
            cp.start()
        for cp in from_chips:
            cp.wait_recv()
        for cp in on:
            cp.start()
        for cp in from_core + on_in:
            cp.wait_recv()
        for cp in out + on:
            cp.wait_send()

    return pl.pallas_call(
        body, name=name, in_specs=[ANY] * n, out_specs=[ANY] * n,
        out_shape=[jax.ShapeDtypeStruct(b.shape, b.dtype) for b in bufs],
        scratch_shapes=[pltpu.SemaphoreType.DMA((7 * n,)), pltpu.SemaphoreType.DMA((7 * n,))],
        input_output_aliases={i: i for i in range(n)},
    )(*bufs)


def place_shards(shards, dev):
    n = len(shards)

    def body(dev_ref, *refs):
        for a in range(n):
            refs[n + a][...] = refs[a][...].astype(BF16)

    spec = pltpu.PrefetchScalarGridSpec(
        num_scalar_prefetch=1, grid=(1,),
        in_specs=[pl.BlockSpec(s.shape, lambda i, dev_ref: (0, 0)) for s in shards],
        out_specs=[pl.BlockSpec(s.shape, lambda i, dev_ref: (dev_ref[0], 0)) for s in shards])
    return pl.pallas_call(
        body, name="place_shards", grid_spec=spec,
        out_shape=[jax.ShapeDtypeStruct((N_DEV * s.shape[0], s.shape[1]), BF16) for s in shards],
        compiler_params=_params(),
    )(dev, *shards)


def _swap(copies_of):
    def make(phase, src_refs, buf_refs, new_refs, send_sems, recv_sems):
        copies = copies_of(src_refs, new_refs, send_sems, recv_sems)
        return copies if phase == "start" else (copies, copies)

    return make


def to_sibling(grads):
    def copies_of(src_refs, new_refs, send_sems, recv_sems):
        x, y, c = _coords()
        return [pltpu.make_async_remote_copy(
            src_ref=src_refs[a].at[2 * xy + 1 - c], dst_ref=new_refs[a].at[xy], send_sem=send_sems.at[4 * a + xy],
            recv_sem=recv_sems.at[4 * a + xy], device_id=(x, y, 1 - c), device_id_type=MESH)
            for a in range(len(src_refs)) for xy in range(4)]

    return Exchange(grads, [], [jax.ShapeDtypeStruct((4,) + g.shape[1:], g.dtype) for g in grads], 4 * len(grads),
                    _swap(copies_of))


def to_chips(parts):
    def copies_of(src_refs, new_refs, send_sems, recv_sems):
        x, y, c = _coords()
        chips = [(1 - x, y), (x, 1 - y), (1 - x, 1 - y)]
        return [pltpu.make_async_remote_copy(
            src_ref=src_refs[a].at[2 * px + py], dst_ref=new_refs[a].at[2 * x + y], send_sem=send_sems.at[3 * a + j],
            recv_sem=recv_sems.at[3 * a + j], device_id=(px, py, c), device_id_type=MESH)
            for a in range(len(src_refs)) for j, (px, py) in enumerate(chips)]

    return Exchange(parts, [], [jax.ShapeDtypeStruct(p.shape, p.dtype) for p in parts], 3 * len(parts), _swap(copies_of))


def to_owners(grad):
    def copies_of(src_refs, new_refs, send_sems, recv_sems):
        x, y, c = _coords()
        copies = []
        for m in range(1, N_DEV):
            px, py, pc = x ^ (m >> 2), y ^ ((m >> 1) & 1), c ^ (m & 1)
            copies.append(pltpu.make_async_remote_copy(
                src_ref=src_refs[0].at[4 * px + 2 * py + pc], dst_ref=new_refs[0].at[4 * x + 2 * y + c],
                send_sem=send_sems.at[m - 1], recv_sem=recv_sems.at[m - 1], device_id=(px, py, pc), device_id_type=MESH))
        return copies

    return Exchange([grad], [], [jax.ShapeDtypeStruct(grad.shape, grad.dtype)], N_DEV - 1, _swap(copies_of))


def exchange_only(name, exchanges):
    def body():
        pass

    return _hosted(name, body, 1, [], [], [], [], exchanges=exchanges)[1]


def sum_cores(name, grad, other, core):
    _, r, w = other.shape

    def body(core_ref, g_ref, o_ref, out_ref):
        out_ref[...] = (g_ref[...].astype(F32) + o_ref[...].astype(F32)).astype(out_ref.dtype)

    return pl.pallas_call(
        body, name=name,
        grid_spec=pltpu.PrefetchScalarGridSpec(
            num_scalar_prefetch=1, grid=(4,),
            in_specs=[pl.BlockSpec((1, r, w), lambda i, core_ref: (2 * i + core_ref[0], 0, 0)),
                      pl.BlockSpec((1, r, w), lambda i, core_ref: (i, 0, 0))],
            out_specs=pl.BlockSpec((1, r, w), lambda i, core_ref: (i, 0, 0))),
        out_shape=jax.ShapeDtypeStruct(other.shape, other.dtype),
        compiler_params=_params(),
    )(core, grad, other)


def sum_owned(name, grad, others, dev_ids):
    _, r, w = grad.shape

    def body(ids_ref, *refs):
        acc = refs[0][0]
        for k in range(1, N_DEV):
            acc = acc + refs[k][0]
        refs[N_DEV][...] = acc

    def pick(k):
        return pl.BlockSpec((1, r, w), lambda i, ids_ref: (ids_ref[k], 0, 0))

    return pl.pallas_call(
        body, name=name,
        grid_spec=pltpu.PrefetchScalarGridSpec(
            num_scalar_prefetch=1, grid=(1,), in_specs=[pick(k) for k in range(N_DEV)],
            out_specs=pl.BlockSpec((r, w), lambda i, ids_ref: (ids_ref[0], 0))),
        out_shape=jax.ShapeDtypeStruct((N_DEV * r, w), F32),
        compiler_params=_params(),
    )(dev_ids, grad, *([others] * (N_DEV - 1)))


def sum_chips(name, part, others, chip_ids):
    _, r, w = part.shape

    def body(ids_ref, p_ref, a_ref, b_ref, c_ref, out_ref):
        acc = p_ref[0].astype(F32) + a_ref[0].astype(F32)
        out_ref[...] = (acc + b_ref[0].astype(F32)) + c_ref[0].astype(F32)

    def pick(k):
        return pl.BlockSpec((1, r, w), lambda i, ids_ref: (ids_ref[k], 0, 0))

    return pl.pallas_call(
        body, name=name,
        grid_spec=pltpu.PrefetchScalarGridSpec(
            num_scalar_prefetch=1, grid=(1,),
            in_specs=[pick(0), pick(1), pick(2), pick(3)],
            out_specs=pl.BlockSpec((r, w), lambda i, ids_ref: (0, 0))),
        out_shape=jax.ShapeDtypeStruct((r, w), F32),
        compiler_params=_params(),
    )(chip_ids, part, others, others, others)


def adamw(name, w, g, m, v):
    c1 = np.float32(1.0 - ADAM_B1 ** ADAM_STEP)
    c2 = np.float32(1.0 - ADAM_B2 ** ADAM_STEP)

    def body(w_ref, g_ref, m_ref, v_ref, d_ref, nm_ref, nv_ref):
        g = g_ref[...]
        nm = ADAM_B1 * m_ref[...] + np.float32(1.0 - ADAM_B1) * g
        nv = ADAM_B2 * v_ref[...] + np.float32(1.0 - ADAM_B2) * (g * g)
        nm_ref[...] = nm
        nv_ref[...] = nv
        d_ref[...] = -ADAM_LR * ((nm / c1) / (jnp.sqrt(nv / c2) + ADAM_EPS) + ADAM_WD * w_ref[...])

    shape = jax.ShapeDtypeStruct(w.shape, F32)
    spec = _full_spec(w.shape)
    return pl.pallas_call(
        body, name=name, grid=(1,), in_specs=[spec] * 4, out_specs=[spec] * 3, out_shape=[shape] * 3,
        compiler_params=_params(),
    )(w, g, m, v)


def _pack_small(parts):
    flat = jnp.concatenate([parts[name].reshape(-1) for name, _ in SMALL])
    return jnp.pad(flat, (0, SMALL_ROWS * 128 - flat.shape[0])).reshape(SMALL_ROWS, 128)


def _unpack_small(packed, like):
    flat = packed.reshape(-1)
    out, at = {}, 0
    for name, size in SMALL:
        out[name] = flat[at:at + size].reshape(like[name].shape)
        at += size
    return out


def kernel(x, positions, pre_mix_norm, w_in, sgu_ln_gain, sgu_ln_bias, sgu_w_spatial, sgu_b_spatial, attn_out_norm, sgu_out_norm, w_out, post_mix_norm, pre_ffn_norm, w_gate, w_up, w_down, post_ffn_norm, loss_target, m_pre_mix_norm, m_w_in, m_sgu_ln_gain, m_sgu_ln_bias, m_sgu_w_spatial, m_sgu_b_spatial, m_attn_out_norm, m_sgu_out_norm, m_w_out, m_post_mix_norm, m_pre_ffn_norm, m_w_gate, m_w_up, m_w_down, m_post_ffn_norm, v_pre_mix_norm, v_w_in, v_sgu_ln_gain, v_sgu_ln_bias, v_sgu_w_spatial, v_sgu_b_spatial, v_attn_out_norm, v_sgu_out_norm, v_w_out, v_post_mix_norm, v_pre_ffn_norm, v_w_gate, v_w_up, v_w_down, v_post_ffn_norm):
    small_w = dict(pre_mix_norm=pre_mix_norm, sgu_ln_gain=sgu_ln_gain, sgu_ln_bias=sgu_ln_bias, sgu_w_spatial=sgu_w_spatial,
                   sgu_b_spatial=sgu_b_spatial, attn_out_norm=attn_out_norm, sgu_out_norm=sgu_out_norm,
                   post_mix_norm=post_mix_norm, pre_ffn_norm=pre_ffn_norm, post_ffn_norm=post_ffn_norm)
    small_m = dict(pre_mix_norm=m_pre_mix_norm, sgu_ln_gain=m_sgu_ln_gain, sgu_ln_bias=m_sgu_ln_bias, sgu_w_spatial=m_sgu_w_spatial,
                   sgu_b_spatial=m_sgu_b_spatial, attn_out_norm=m_attn_out_norm, sgu_out_norm=m_sgu_out_norm,
                   post_mix_norm=m_post_mix_norm, pre_ffn_norm=m_pre_ffn_norm, post_ffn_norm=m_post_ffn_norm)
    small_v = dict(pre_mix_norm=v_pre_mix_norm, sgu_ln_gain=v_sgu_ln_gain, sgu_ln_bias=v_sgu_ln_bias, sgu_w_spatial=v_sgu_w_spatial,
                   sgu_b_spatial=v_sgu_b_spatial, attn_out_norm=v_attn_out_norm, sgu_out_norm=v_sgu_out_norm,
                   post_mix_norm=v_post_mix_norm, pre_ffn_norm=v_pre_ffn_norm, post_ffn_norm=v_post_ffn_norm)
    for table in (small_w, small_m, small_v):
        table["loss_sum"] = jnp.zeros((1,), F32)

    x2d = x[0]
    target = loss_target[0]
    pos_col = positions.reshape(SEQ, 1)
    rot = _rot_consts()
    w_sp = sgu_w_spatial[0]
    bfull = jnp.repeat(sgu_b_spatial[0].T, HEAD_DIM, axis=1)

    x_i, y_i, c_i = (lax.axis_index(a).astype(jnp.int32) for a in MESH_AXES)
    dev = 4 * x_i + 2 * y_i + c_i
    core = c_i.reshape(1)
    chip = 2 * x_i + y_i
    chip_ids = jnp.stack([chip, chip ^ 1, chip ^ 2, chip ^ 3])
    dev_ids = jnp.stack([dev ^ m for m in range(N_DEV)])

    w_in_t, w_gate_t, w_up_t, w_out_f, w_down_f = place_shards(
        [w_in[0].T, w_gate[0].T, w_up[0].T, w_out[0], w_down[0]], dev.reshape(1))
    (w_in_t,) = all_gather_in_place("gather_w_in", [w_in_t])

    part, more, chunk = 112, 176, D_FF // N_DEV
    (h1, q, k, v, u, vs), [([w_out_f, w_gate_t], _)] = in_proj(
        x2d, pos_col, pre_mix_norm, w_in_t, rot,
        exchanges=[gather([w_out_f, w_gate_t], "end", [(0, D_MODEL // N_DEV), (0, part)])])
    q, k, v = (_to_residue_order(t) for t in (q, k, v))
    (attn_r, lse), [([w_gate_t, w_up_t], _)] = attn_fwd(
        q, k, v, exchanges=[gather([w_gate_t, w_up_t], "end", [(part, chunk), (0, part)])])
    attn = _from_residue_order(attn_r)
    (sgu,), [([w_up_t], _)] = sgu_fwd(
        u, vs, sgu_ln_gain, sgu_ln_bias, w_sp, bfull, exchanges=[gather([w_up_t], "end", [(part, more)])])
    (mix, y, x2, h2), [([w_up_t], _)] = out_proj(
        attn, sgu, x2d, attn_out_norm, sgu_out_norm, w_out_f, post_mix_norm, pre_ffn_norm,
        exchanges=[gather([w_up_t], "end", [(more, chunk)])])
    (gate, up, act), [([w_down_f], _)] = ffn_up(h2, w_gate_t, w_up_t, exchanges=[gather([w_down_f], "end")])
    df, dx3, d_post_ffn, sq_err = ffn_down_loss(act, w_down_f, x2, post_ffn_norm, target)

    g_w_down, _ = weight_grad("grad_w_down", act, df)
    (dgate, dup), [(_, [s_down])] = ffn_act_bwd(df, w_down_f, gate, up, exchanges=[to_sibling([g_w_down])])
    p_down = sum_cores("sum_cores_down", g_w_down, s_down, core)
    g_w_gate, [(_, [c_down])] = weight_grad("grad_w_gate", dgate, h2, exchanges=[to_chips([p_down])])
    r_w_down = sum_chips("sum_chips_down", p_down, c_down, chip_ids)
    g_w_up, [(_, [s_gate])] = weight_grad("grad_w_up", dup, h2, exchanges=[to_sibling([g_w_gate])])
    p_gate = sum_cores("sum_cores_gate", g_w_gate, s_gate, core)
    (dx2, dy, d_pre_ffn, d_post_mix), [(_, [s_up]), (_, [c_gate])] = ffn_in_bwd(
        dgate, dup, w_gate_t, w_up_t, x2, pre_ffn_norm, dx3, y, post_mix_norm,
        exchanges=[to_sibling([g_w_up]), to_chips([p_gate])])
    p_up = sum_cores("sum_cores_up", g_w_up, s_up, core)
    r_w_gate_t = sum_chips("sum_chips_gate", p_gate, c_gate, chip_ids)
    g_w_out, _ = weight_grad("grad_w_out", mix, dy)
    (dattn, dsgu, d_attn_out, d_sgu_out), [(_, [s_out])] = mix_bwd(
        dy, w_out_f, attn, sgu, attn_out_norm, sgu_out_norm, exchanges=[to_sibling([g_w_out])])
    p_out = sum_cores("sum_cores_out", g_w_out, s_out, core)
    du, dvs, d_ln_gain, d_ln_bias, d_w_sp, d_bfull = sgu_bwd(u, vs, dsgu, sgu_ln_gain, sgu_ln_bias, w_sp, bfull)
    (dq, dk, dv), [(_, [c_up, c_out])] = attn_bwd(
        q, k, v, attn_r, lse, _to_residue_order(dattn), _to_residue_order(pos_col), rot,
        exchanges=[to_chips([p_up, p_out])])
    dq, dk, dv = (_from_residue_order(t) for t in (dq, dk, dv))
    r_w_up_t = sum_chips("sum_chips_up", p_up, c_up, chip_ids)
    r_w_out = sum_chips("sum_chips_out", p_out, c_out, chip_ids)
    dproj = jnp.concatenate([dq, dk, dv, du, dvs], axis=1)
    g_w_in, _ = weight_grad("grad_w_in", dproj, h1)
    (grad_x, d_pre_mix), [(_, [s_in])] = in_bwd(
        dproj, w_in_t, x2d, pre_mix_norm, dx2, exchanges=[to_sibling([g_w_in])])
    p_in = sum_cores("sum_cores_in", g_w_in, s_in, core)

    d_b_sp = d_bfull.reshape(CHUNK, N_GROUPS, HEAD_DIM).sum(axis=-1).T
    small_g = _pack_small(dict(pre_mix_norm=d_pre_mix, sgu_ln_gain=d_ln_gain, sgu_ln_bias=d_ln_bias, sgu_w_spatial=d_w_sp,
                               sgu_b_spatial=d_b_sp, attn_out_norm=d_attn_out, sgu_out_norm=d_sgu_out,
                               post_mix_norm=d_post_mix, pre_ffn_norm=d_pre_ffn, post_ffn_norm=d_post_ffn,
                               loss_sum=sq_err))
    small_g = small_g.reshape(N_DEV, SMALL_ROWS // N_DEV, 128)
    [(_, [c_in]), (_, [o_small])] = exchange_only("grads_tail", [to_chips([p_in]), to_owners(small_g)])
    r_w_in_t = sum_chips("sum_chips_in", p_in, c_in, chip_ids)
    all_small = sum_owned("sum_small", small_g, o_small, dev_ids)
    (all_small,) = all_gather_in_place("gather_small_grads", [all_small])

    big = {}
    for name, w, g, m, vv in (("w_in", w_in, r_w_in_t, m_w_in, v_w_in), ("w_gate", w_gate, r_w_gate_t, m_w_gate, v_w_gate),
                              ("w_up", w_up, r_w_up_t, m_w_up, v_w_up)):
        d, nm, nv = adamw("adamw_" + name, w[0].T, g, m[0].T, vv[0].T)
        big[name] = (g.T[None], d.T[None], nm.T[None], nv.T[None])
    for name, w, g, m, vv in (("w_out", w_out, r_w_out, m_w_out, v_w_out), ("w_down", w_down, r_w_down, m_w_down, v_w_down)):
        d, nm, nv = adamw("adamw_" + name, w[0], g, m[0], vv[0])
        big[name] = (g[None], d[None], nm[None], nv[None])
    sd, snm, snv = adamw("adamw_small", _pack_small(small_w), all_small, _pack_small(small_m), _pack_small(small_v))
    sg, sd, snm, snv = (_unpack_small(t, small_w) for t in (all_small, sd, snm, snv))
    loss = sg["loss_sum"][0] * np.float32(0.5 / D_MODEL)

    names = ["pre_mix_norm", "w_in", "sgu_ln_gain", "sgu_ln_bias", "sgu_w_spatial", "sgu_b_spatial", "attn_out_norm",
             "sgu_out_norm", "w_out", "post_mix_norm", "pre_ffn_norm", "w_gate", "w_up", "w_down", "post_ffn_norm"]
    outs = [loss, grad_x[None]]
    for i, table in enumerate((sg, sd, snm, snv)):
        for name in names:
            outs.append(big[name][i] if name in big else table[name])
    return tuple(outs)
```

```python
import functools

import numpy as np
import jax
import jax.numpy as jnp
from jax import lax
from jax.experimental import pallas as pl
from jax.experimental.pallas import tpu as pltpu

F32 = jnp.float32
BF16 = jnp.bfloat16

SEQ = 2048
D_MODEL = 1024
ATTN_W = 512
SGU_W = 512
HEAD_DIM = 64
N_GROUPS = 8
CHUNK = 128
D_FF = 2816
IN_W = 3 * ATTN_W + 2 * SGU_W
DILATIONS = (1, 4, 16)
ROPE_THETA = 500000.0
ROT_DIM = 16
ROT_HALF = 8
RMS_EPS = 1e-6
LN_EPS = 1e-5
Q_SCALE = 0.125
NEG = -1e30

N_DEV = 8
MESH_AXES = ("x", "y", "c")
MESH = pl.DeviceIdType.MESH

ADAM_LR = 0.001
ADAM_B1 = 0.9
ADAM_B2 = 0.999
ADAM_EPS = 1e-08
ADAM_WD = 0.01
ADAM_STEP = 10

VMEM_LIMIT = 60 * 1024 * 1024
ANY = pl.BlockSpec(memory_space=pl.ANY)

SMALL = (("pre_mix_norm", 1024), ("sgu_ln_gain", 512), ("sgu_ln_bias", 512), ("sgu_w_spatial", 8 * 128 * 128),
         ("sgu_b_spatial", 1024), ("attn_out_norm", 512), ("sgu_out_norm", 512), ("post_mix_norm", 1024),
         ("pre_ffn_norm", 1024), ("post_ffn_norm", 1024), ("loss_sum", 1))
SMALL_ROWS = 1152


def _params(sem=("arbitrary",)):
    return pltpu.CompilerParams(dimension_semantics=sem, vmem_limit_bytes=VMEM_LIMIT)


def _dot(a, b):
    return jnp.dot(a, b, preferred_element_type=F32)


def _dot_nt(a, b):
    return lax.dot_general(a, b, (((1,), (1,)), ((), ())), preferred_element_type=F32)


def _dot_tn(a, b):
    return lax.dot_general(a, b, (((0,), (0,)), ((), ())), preferred_element_type=F32)


def _rms(z):
    return lax.rsqrt(jnp.mean(z * z, axis=-1, keepdims=True) + RMS_EPS)


def _rms_bwd(z, gain, d):
    r = _rms(z)
    n = z * r
    dn = d * gain
    dz = r * (dn - n * jnp.mean(dn * n, axis=-1, keepdims=True))
    return dz, jnp.sum(d * n, axis=0, keepdims=True)


def _gelu(z):
    return 0.5 * z * (1.0 + lax.erf(z * np.float32(1.0 / np.sqrt(2.0))))


def _gelu_grad(z):
    cdf = 0.5 * (1.0 + lax.erf(z * np.float32(1.0 / np.sqrt(2.0))))
    return cdf + z * jnp.exp(-0.5 * z * z) * np.float32(1.0 / np.sqrt(2.0 * np.pi))


def _rot_tables(pos_col, invf, ma, mb):
    ang = pos_col.astype(F32) * invf
    s = jnp.sin(ang)
    return jnp.cos(ang), s * ma, s * mb


def _rot(t, c, sa, sb):
    return t * c + pltpu.roll(t, 120, 1) * sa + pltpu.roll(t, 8, 1) * sb


def _rot_t(d, c, sa, sb):
    return d * c + pltpu.roll(d * sa, 8, 1) + pltpu.roll(d * sb, 120, 1)


def _rot_consts():
    lane = np.arange(128) % HEAD_DIM
    inv_freq = (np.float32(ROPE_THETA) ** (-np.arange(0, ROT_DIM, 2, dtype=np.float32) / np.float32(ROT_DIM))).astype(np.float32)
    invf = np.where(lane < ROT_DIM, inv_freq[lane % ROT_HALF], 0.0).astype(np.float32)
    ma = np.where(lane < ROT_HALF, -1.0, 0.0).astype(np.float32)
    mb = np.where((lane >= ROT_HALF) & (lane < ROT_DIM), 1.0, 0.0).astype(np.float32)
    return jnp.asarray(invf[None]), jnp.asarray(ma[None]), jnp.asarray(mb[None])


def _row_spec(tm, w):
    return pl.BlockSpec((tm, w), lambda i: (i, 0))


def _full_spec(shape):
    return pl.BlockSpec(shape, lambda i: (0,) * len(shape))


def in_proj(x, pos_col, g1, w_in_t, rot, exchanges=()):
    tm = 512

    def body(x_ref, pos_ref, g_ref, w_ref, invf_ref, ma_ref, mb_ref, h_ref, q_ref, k_ref, v_ref, u_ref, vs_ref):
        xf = x_ref[...]
        h = (xf * _rms(xf) * g_ref[...]).astype(BF16)
        h_ref[...] = h
        proj = _dot_nt(h, w_ref[...])
        c, sa, sb = _rot_tables(pos_ref[...], invf_ref[...], ma_ref[...], mb_ref[...])
        for j in range(ATTN_W // 128):
            q_ref[:, j * 128:(j + 1) * 128] = _rot(proj[:, j * 128:(j + 1) * 128], c, sa, sb) * Q_SCALE
            k_ref[:, j * 128:(j + 1) * 128] = _rot(proj[:, ATTN_W + j * 128:ATTN_W + (j + 1) * 128], c, sa, sb)
        v_ref[...] = proj[:, 2 * ATTN_W:3 * ATTN_W]
        u_ref[...] = proj[:, 3 * ATTN_W:3 * ATTN_W + SGU_W]
        vs_ref[...] = proj[:, 3 * ATTN_W + SGU_W:]

    act = jax.ShapeDtypeStruct((SEQ, 512), F32)
    return _hosted(
        "in_proj", body, SEQ // tm,
        [_row_spec(tm, D_MODEL), _row_spec(tm, 1), _full_spec((1, D_MODEL)), _full_spec((IN_W, D_MODEL)),
         _full_spec((1, 128)), _full_spec((1, 128)), _full_spec((1, 128))],
        [_row_spec(tm, D_MODEL)] + [_row_spec(tm, 512)] * 5,
        [jax.ShapeDtypeStruct((SEQ, D_MODEL), BF16)] + [act] * 5,
        (x, pos_col, g1, w_in_t, *rot), exchanges=exchanges)


RES = 16


def _to_residue_order(t):
    return t.reshape(SEQ // RES, RES, -1).transpose(1, 0, 2).reshape(t.shape)


def _from_residue_order(t):
    return t.reshape(RES, SEQ // RES, -1).transpose(1, 0, 2).reshape(t.shape)


def _block_rows(d, r, n):
    if d == 16:
        slices = [(128 * r, 128)]
    elif d == 4:
        slices = [(128 * (4 * b + r) + 32 * n, 32) for b in range(4)]
    else:
        slices = [(128 * b + 8 * n, 8) for b in range(RES)]
    return [(s if isinstance(s, int) else pl.multiple_of(s, z), z) for s, z in slices]


def _block_step(d, i):
    if d == 16:
        return i
    if d == 4:
        return 4 * (i & 31) + (i >> 5)
    return 16 * (i & 7) + (i >> 3)


def _attn_masks(d):
    row2 = _block_step(d, lax.broadcasted_iota(jnp.int32, (128, 256), 0))
    col2 = lax.broadcasted_iota(jnp.int32, (128, 256), 1)
    key2 = _block_step(d, col2 & 127)
    mask2 = jnp.logical_or(jnp.logical_and(col2 < 128, key2 >= row2), jnp.logical_and(col2 >= 128, key2 <= row2))
    row1 = _block_step(d, lax.broadcasted_iota(jnp.int32, (128, 128), 0))
    col1 = lax.broadcasted_iota(jnp.int32, (128, 128), 1)
    return col1 < HEAD_DIM, _block_step(d, col1) <= row1, mask2


def _load_rows(ref, slices):
    parts = [ref[pl.ds(s, z), :] for s, z in slices]
    return parts[0] if len(parts) == 1 else jnp.concatenate(parts, axis=0)


def _for_each_group(fn):
    for p, d in enumerate(DILATIONS):
        masks = _attn_masks(d)
        if d == 16:
            def group(i, carry, p=p, masks=masks):
                fn(p, masks, [(_block_rows(16, 4 * i + g, 0), None) for g in range(4)])
                return carry

            lax.fori_loop(0, 4, group, 0)
        elif d == 4:
            fn(p, masks, [(_block_rows(4, r, 0), None) for r in range(4)])

            def group(i, carry, p=p, masks=masks):
                fn(p, masks, [(_block_rows(4, r, i + 1), _block_rows(4, r, i)) for r in range(4)])
                return carry

            lax.fori_loop(0, 3, group, 0)
        else:
            fn(p, masks, [(_block_rows(1, 0, 0), None)])

            def group(i, carry, p=p, masks=masks):
                fn(p, masks, [(_block_rows(1, 0, 3 * i + g + 1), _block_rows(1, 0, 3 * i + g)) for g in range(3)])
                return carry

            lax.fori_loop(0, 5, group, 0)


def attn_fwd(q, k, v, exchanges=()):
    def body(q_ref, k_ref, v_ref, o_ref, lse_ref, op_ref, lp_ref):
        def group(p, masks, blocks):
            head0, mask1, mask2 = masks
            heads = (head0, jnp.logical_not(head0))
            keys = [rows if prev is None else prev + rows for rows, prev in blocks]
            mask = [mask1 if prev is None else mask2 for _, prev in blocks]
            qb = [_load_rows(q_ref, rows) for rows, _ in blocks]
            kk = [_load_rows(k_ref, ks).astype(BF16) for ks in keys]
            vv = [_load_rows(v_ref, ks).astype(BF16) for ks in keys]
            chains = [(g, hm) for g in range(len(blocks)) for hm in heads]
            s = [jnp.where(mask[g], _dot_nt(jnp.where(hm, qb[g], 0.0).astype(BF16), kk[g]), NEG) for g, hm in chains]
            m = [jnp.max(t, axis=-1, keepdims=True) for t in s]
            e = [jnp.exp(t - mt) for t, mt in zip(s, m)]
            l = [jnp.sum(t, axis=-1, keepdims=True) for t in e]
            pv = [_dot(t.astype(BF16), vv[g]) for t, (g, _) in zip(e, chains)]
            for g, (rows, _) in enumerate(blocks):
                o_blk = jnp.where(head0, pv[2 * g] / l[2 * g], pv[2 * g + 1] / l[2 * g + 1])
                l_blk = jnp.where(head0, jnp.broadcast_to(m[2 * g] + jnp.log(l[2 * g]), (128, 128)),
                                  jnp.broadcast_to(m[2 * g + 1] + jnp.log(l[2 * g + 1]), (128, 128)))
                at = 0
                for start, size in rows:
                    op_ref[p, pl.ds(start, size), :] = o_blk[at:at + size]
                    lp_ref[p, pl.ds(start, size), :] = l_blk[at:at + size]
                    at += size

        _for_each_group(group)

        def combine(i, carry):
            rows = pl.ds(pl.multiple_of(i * 256, 256), 256)
            ls = [lp_ref[p, rows, :] for p in range(3)]
            m = jnp.maximum(jnp.maximum(ls[0], ls[1]), ls[2])
            lse = m + jnp.log(jnp.exp(ls[0] - m) + jnp.exp(ls[1] - m) + jnp.exp(ls[2] - m))
            o = jnp.zeros((256, 128), F32)
            for p in range(3):
                o = o + jnp.exp(ls[p] - lse) * op_ref[p, rows, :]
            o_ref[rows, :] = o
            lse_ref[rows, :] = lse
            return carry

        lax.fori_loop(0, SEQ // 256, combine, 0)

    slab = pl.BlockSpec((SEQ, 128), lambda i: (0, i))
    out = jax.ShapeDtypeStruct((SEQ, ATTN_W), F32)
    return _hosted(
        "attn_fwd", body, ATTN_W // 128, [slab] * 3, [slab] * 2, [out, out], (q, k, v),
        scratch_shapes=[pltpu.VMEM((3, SEQ, 128), F32), pltpu.VMEM((3, SEQ, 128), F32)], exchanges=exchanges)


def _causal_weights(w_ref):
    row = lax.broadcasted_iota(jnp.int32, (CHUNK, CHUNK), 0)
    col = lax.broadcasted_iota(jnp.int32, (CHUNK, CHUNK), 1)
    return [jnp.where(col <= row, w_ref[g], 0.0).astype(BF16) for g in range(N_GROUPS)], col <= row


def _sgu_chunk_fwd(u, vs, lg, lb, wc, bfull, head0):
    ug = _gelu(u)
    vg = _gelu(vs)
    xc = vg - jnp.mean(vg, axis=-1, keepdims=True)
    rstd = lax.rsqrt(jnp.mean(xc * xc, axis=-1, keepdims=True) + LN_EPS)
    xhat = xc * rstd
    vn = xhat * lg + lb
    mixed = []
    for gp in range(SGU_W // 128):
        vp = vn[:, gp * 128:(gp + 1) * 128].astype(BF16)
        mixed.append(jnp.where(head0, _dot(wc[2 * gp], vp), _dot(wc[2 * gp + 1], vp)))
    ms = jnp.concatenate(mixed, axis=1) + bfull
    return ug, xhat, rstd, vn, ms


def sgu_fwd(u, vs, lg, lb, w_sp, bfull, exchanges=()):
    cpb = 4

    def body(u_ref, vs_ref, lg_ref, lb_ref, w_ref, b_ref, o_ref):
        wc, _ = _causal_weights(w_ref)
        head0 = lax.broadcasted_iota(jnp.int32, (CHUNK, 128), 1) < HEAD_DIM
        for ci in range(cpb):
            rows = pl.ds(ci * CHUNK, CHUNK)
            ug, _, _, _, ms = _sgu_chunk_fwd(u_ref[rows, :], vs_ref[rows, :], lg_ref[...], lb_ref[...], wc, b_ref[...], head0)
            o_ref[rows, :] = ug * ms

    tm = cpb * CHUNK
    return _hosted(
        "sgu_fwd", body, SEQ // tm,
        [_row_spec(tm, SGU_W), _row_spec(tm, SGU_W), _full_spec((1, SGU_W)), _full_spec((1, SGU_W)),
         _full_spec((N_GROUPS, CHUNK, CHUNK)), _full_spec((CHUNK, SGU_W))],
        [_row_spec(tm, SGU_W)], [jax.ShapeDtypeStruct((SEQ, SGU_W), F32)],
        (u, vs, lg, lb, w_sp, bfull), exchanges=exchanges)


def out_proj(attn, sgu, x, ga, gs, w_out, gpm, gpf, exchanges=()):
    tm = 512

    def body(a_ref, s_ref, x_ref, ga_ref, gs_ref, w_ref, gpm_ref, gpf_ref, mix_ref, y_ref, x2_ref, h2_ref):
        a = a_ref[...]
        s = s_ref[...]
        an = (a * _rms(a) * ga_ref[...]).astype(BF16)
        sn = (s * _rms(s) * gs_ref[...]).astype(BF16)
        mix_ref[:, :ATTN_W] = an
        mix_ref[:, ATTN_W:] = sn
        y = _dot(an, w_ref[:ATTN_W, :]) + _dot(sn, w_ref[ATTN_W:, :])
        y_ref[...] = y
        x2 = x_ref[...] + y * _rms(y) * gpm_ref[...]
        x2_ref[...] = x2
        h2_ref[...] = (x2 * _rms(x2) * gpf_ref[...]).astype(BF16)

    wide = jax.ShapeDtypeStruct((SEQ, D_MODEL), F32)
    wide16 = jax.ShapeDtypeStruct((SEQ, D_MODEL), BF16)
    return _hosted(
        "out_proj", body, SEQ // tm,
        [_row_spec(tm, ATTN_W), _row_spec(tm, SGU_W), _row_spec(tm, D_MODEL), _full_spec((1, ATTN_W)),
         _full_spec((1, SGU_W)), _full_spec((D_MODEL, D_MODEL)), _full_spec((1, D_MODEL)), _full_spec((1, D_MODEL))],
        [_row_spec(tm, D_MODEL)] * 4, [wide16, wide, wide, wide16],
        (attn, sgu, x, ga, gs, w_out, gpm, gpf), exchanges=exchanges)


def ffn_up(h2, w_gate_t, w_up_t, exchanges=()):
    tm = 256

    def body(h_ref, wg_ref, wu_ref, g_ref, u_ref, a_ref):
        h = h_ref[...]
        g = _dot_nt(h, wg_ref[...])
        u = _dot_nt(h, wu_ref[...])
        g_ref[...] = g.astype(BF16)
        u_ref[...] = u.astype(BF16)
        a_ref[...] = (g * jax.nn.sigmoid(g) * u).astype(BF16)

    ff = jax.ShapeDtypeStruct((SEQ, D_FF), BF16)
    return _hosted(
        "ffn_up", body, SEQ // tm,
        [_row_spec(tm, D_MODEL), _full_spec((D_FF, D_MODEL)), _full_spec((D_FF, D_MODEL))],
        [_row_spec(tm, D_FF)] * 3, [ff, ff, jax.ShapeDtypeStruct((SEQ, D_FF), BF16)],
        (h2, w_gate_t, w_up_t), exchanges=exchanges)


def ffn_down_loss(act, w_down, x2, gpo, target):
    tm = 512

    def body(a_ref, w_ref, x2_ref, g_ref, t_ref, df_ref, dx3_ref, dg_ref, loss_ref):
        f = _dot(a_ref[...], w_ref[...])
        gain = g_ref[...]
        err = x2_ref[...] + f * _rms(f) * gain - t_ref[...]
        dx3 = err * np.float32(1.0 / D_MODEL)
        dx3_ref[...] = dx3
        df, dg = _rms_bwd(f, gain, dx3)
        df_ref[...] = df.astype(BF16)

        @pl.when(pl.program_id(0) == 0)
        def _():
            dg_ref[...] = jnp.zeros_like(dg_ref)
            loss_ref[...] = jnp.zeros_like(loss_ref)

        dg_ref[...] += dg
        loss_ref[...] += jnp.sum(err * err, axis=(0, 1), keepdims=True)

    return pl.pallas_call(
        body, name="ffn_down_loss", grid=(SEQ // tm,),
        in_specs=[_row_spec(tm, D_FF), _full_spec((D_FF, D_MODEL)), _row_spec(tm, D_MODEL), _full_spec((1, D_MODEL)),
                  _row_spec(tm, D_MODEL)],
        out_specs=[_row_spec(tm, D_MODEL), _row_spec(tm, D_MODEL), _full_spec((1, D_MODEL)), _full_spec((1, 1))],
        out_shape=[jax.ShapeDtypeStruct((SEQ, D_MODEL), BF16), jax.ShapeDtypeStruct((SEQ, D_MODEL), F32),
                   jax.ShapeDtypeStruct((1, D_MODEL), F32), jax.ShapeDtypeStruct((1, 1), F32)],
        compiler_params=_params(),
    )(act, w_down, x2, gpo, target)


def ffn_act_bwd(df, w_down, gate, up, exchanges=()):
    tm = 256

    def body(df_ref, w_ref, g_ref, u_ref, dg_ref, du_ref):
        dact = _dot_nt(df_ref[...], w_ref[...])
        g = g_ref[...].astype(F32)
        s = jax.nn.sigmoid(g)
        du_ref[...] = (dact * g * s).astype(BF16)
        dg_ref[...] = (dact * u_ref[...].astype(F32) * (s * (1.0 + g * (1.0 - s)))).astype(BF16)

    ff16 = jax.ShapeDtypeStruct((SEQ, D_FF), BF16)
    return _hosted(
        "ffn_act_bwd", body, SEQ // tm,
        [_row_spec(tm, D_MODEL), _full_spec((D_FF, D_MODEL)), _row_spec(tm, D_FF), _row_spec(tm, D_FF)],
        [_row_spec(tm, D_FF)] * 2, [ff16, ff16], (df, w_down, gate, up), exchanges=exchanges)


def ffn_in_bwd(dgate, dup, w_gate_t, w_up_t, x2, gpf, dx3, y, gpm, exchanges=()):
    tm = 256

    def body(dg_ref, du_ref, wg_ref, wu_ref, x2_ref, gpf_ref, dx3_ref, y_ref, gpm_ref, dx2_ref, dy_ref, dgpf_ref, dgpm_ref):
        dh2 = _dot(dg_ref[...], wg_ref[...]) + _dot(du_ref[...], wu_ref[...])
        dz, dgpf = _rms_bwd(x2_ref[...], gpf_ref[...], dh2)
        dx2 = dx3_ref[...] + dz
        dx2_ref[...] = dx2
        dy, dgpm = _rms_bwd(y_ref[...], gpm_ref[...], dx2)
        dy_ref[...] = dy.astype(BF16)

        @pl.when(pl.program_id(0) == 0)
        def _():
            dgpf_ref[...] = jnp.zeros_like(dgpf_ref)
            dgpm_ref[...] = jnp.zeros_like(dgpm_ref)

        dgpf_ref[...] += dgpf
        dgpm_ref[...] += dgpm

    vec = jax.ShapeDtypeStruct((1, D_MODEL), F32)
    return _hosted(
        "ffn_in_bwd", body, SEQ // tm,
        [_row_spec(tm, D_FF), _row_spec(tm, D_FF), _full_spec((D_FF, D_MODEL)), _full_spec((D_FF, D_MODEL)),
         _row_spec(tm, D_MODEL), _full_spec((1, D_MODEL)), _row_spec(tm, D_MODEL), _row_spec(tm, D_MODEL),
         _full_spec((1, D_MODEL))],
        [_row_spec(tm, D_MODEL), _row_spec(tm, D_MODEL), _full_spec((1, D_MODEL)), _full_spec((1, D_MODEL))],
        [jax.ShapeDtypeStruct((SEQ, D_MODEL), F32), jax.ShapeDtypeStruct((SEQ, D_MODEL), BF16), vec, vec],
        (dgate, dup, w_gate_t, w_up_t, x2, gpf, dx3, y, gpm), exchanges=exchanges)


def weight_grad(name, a, b, exchanges=()):
    m, n = a.shape[1], b.shape[1]
    tr = 256

    def body(a_ref, b_ref, o_ref):
        o_ref[...] = _dot_tn(a_ref[...], b_ref[...]).astype(BF16)

    (out,), done = _hosted(
        name, body, m // tr, [pl.BlockSpec((SEQ, tr), lambda i: (0, i)), _full_spec((SEQ, n))],
        [_row_spec(tr, n)], [jax.ShapeDtypeStruct((m, n), BF16)], (a, b), exchanges=exchanges)
    return out.reshape(N_DEV, m // N_DEV, n), done


def mix_bwd(dy, w_out, attn, sgu, ga, gs, exchanges=()):
    tm = 512

    def body(dy_ref, w_ref, a_ref, s_ref, ga_ref, gs_ref, da_ref, ds_ref, dga_ref, dgs_ref):
        dy = dy_ref[...]
        da, dga = _rms_bwd(a_ref[...], ga_ref[...], _dot_nt(dy, w_ref[:ATTN_W, :]))
        ds, dgs = _rms_bwd(s_ref[...], gs_ref[...], _dot_nt(dy, w_ref[ATTN_W:, :]))
        da_ref[...] = da
        ds_ref[...] = ds

        @pl.when(pl.program_id(0) == 0)
        def _():
            dga_ref[...] = jnp.zeros_like(dga_ref)
            dgs_ref[...] = jnp.zeros_like(dgs_ref)

        dga_ref[...] += dga
        dgs_ref[...] += dgs

    half = jax.ShapeDtypeStruct((SEQ, 512), F32)
    vec = jax.ShapeDtypeStruct((1, 512), F32)
    return _hosted(
        "mix_bwd", body, SEQ // tm,
        [_row_spec(tm, D_MODEL), _full_spec((D_MODEL, D_MODEL)), _row_spec(tm, 512), _row_spec(tm, 512),
         _full_spec((1, 512)), _full_spec((1, 512))],
        [_row_spec(tm, 512), _row_spec(tm, 512), _full_spec((1, 512)), _full_spec((1, 512))],
        [half, half, vec, vec], (dy, w_out, attn, sgu, ga, gs), exchanges=exchanges)


def sgu_bwd(u, vs, dsgu, lg, lb, w_sp, bfull, exchanges=()):
    cpb = 4

    def body(u_ref, vs_ref, d_ref, lg_ref, lb_ref, w_ref, b_ref, du_ref, dvs_ref, dlg_ref, dlb_ref, dw_ref, db_ref):
        wc, causal = _causal_weights(w_ref)
        head0 = lax.broadcasted_iota(jnp.int32, (CHUNK, 128), 1) < HEAD_DIM
        lg = lg_ref[...]

        @pl.when(pl.program_id(0) == 0)
        def _():
            dlg_ref[...] = jnp.zeros_like(dlg_ref)
            dlb_ref[...] = jnp.zeros_like(dlb_ref)
            dw_ref[...] = jnp.zeros_like(dw_ref)
            db_ref[...] = jnp.zeros_like(db_ref)

        for ci in range(cpb):
            rows = pl.ds(ci * CHUNK, CHUNK)
            u = u_ref[rows, :]
            vs = vs_ref[rows, :]
            d = d_ref[rows, :]
            ug, xhat, rstd, vn, ms = _sgu_chunk_fwd(u, vs, lg, lb_ref[...], wc, b_ref[...], head0)
            du_ref[rows, :] = (d * ms * _gelu_grad(u)).astype(BF16)
            dms = d * ug
            db_ref[...] += dms
            dvn = []
            for gp in range(SGU_W // 128):
                dmp = dms[:, gp * 128:(gp + 1) * 128]
                dm0 = jnp.where(head0, dmp, 0.0).astype(BF16)
                dm1 = jnp.where(head0, 0.0, dmp).astype(BF16)
                vp = vn[:, gp * 128:(gp + 1) * 128].astype(BF16)
                dw_ref[2 * gp] += _dot_nt(dm0, vp)
                dw_ref[2 * gp + 1] += _dot_nt(dm1, vp)
                dvn.append(_dot_tn(wc[2 * gp], dm0) + _dot_tn(wc[2 * gp + 1], dm1))
            dvn = jnp.concatenate(dvn, axis=1)
            dlg_ref[...] += jnp.sum(dvn * xhat, axis=0, keepdims=True)
            dlb_ref[...] += jnp.sum(dvn, axis=0, keepdims=True)
            dxh = dvn * lg
            dvg = rstd * (dxh - jnp.mean(dxh, axis=-1, keepdims=True) - xhat * jnp.mean(dxh * xhat, axis=-1, keepdims=True))
            dvs_ref[rows, :] = (dvg * _gelu_grad(vs)).astype(BF16)

        @pl.when(pl.program_id(0) == pl.num_programs(0) - 1)
        def _():
            for g in range(N_GROUPS):
                dw_ref[g] = jnp.where(causal, dw_ref[g], 0.0)

    tm = cpb * CHUNK
    half16 = jax.ShapeDtypeStruct((SEQ, SGU_W), BF16)
    vec = jax.ShapeDtypeStruct((1, SGU_W), F32)
    return _hosted(
        "sgu_bwd", body, SEQ // tm,
        [_row_spec(tm, SGU_W)] * 3 + [_full_spec((1, SGU_W)), _full_spec((1, SGU_W)),
                                      _full_spec((N_GROUPS, CHUNK, CHUNK)), _full_spec((CHUNK, SGU_W))],
        [_row_spec(tm, SGU_W), _row_spec(tm, SGU_W), _full_spec((1, SGU_W)), _full_spec((1, SGU_W)),
         _full_spec((N_GROUPS, CHUNK, CHUNK)), _full_spec((CHUNK, SGU_W))],
        [half16, half16, vec, vec, jax.ShapeDtypeStruct((N_GROUPS, CHUNK, CHUNK), F32),
         jax.ShapeDtypeStruct((CHUNK, SGU_W), F32)],
        (u, vs, dsgu, lg, lb, w_sp, bfull), exchanges=exchanges)


def attn_bwd(q, k, v, o, lse, do, pos_col, rot, exchanges=()):
    def body(q_ref, k_ref, v_ref, o_ref, lse_ref, do_ref, pos_ref, invf_ref, ma_ref, mb_ref,
             dq_ref, dk_ref, dv_ref, dqa_ref, dka_ref, dva_ref, dlt_ref, rot_ref):
        dqa_ref[...] = jnp.zeros_like(dqa_ref)
        dka_ref[...] = jnp.zeros_like(dka_ref)
        dva_ref[...] = jnp.zeros_like(dva_ref)

        def delta(i, carry):
            rows = pl.ds(pl.multiple_of(i * 256, 256), 256)
            prod = do_ref[rows, :] * o_ref[rows, :]
            h0 = lax.broadcasted_iota(jnp.int32, (256, 128), 1) < HEAD_DIM
            d0 = jnp.sum(jnp.where(h0, prod, 0.0), axis=-1, keepdims=True)
            d1 = jnp.sum(jnp.where(h0, 0.0, prod), axis=-1, keepdims=True)
            dlt_ref[rows, :] = jnp.where(h0, d0, d1)
            return carry

        lax.fori_loop(0, SEQ // 256, delta, 0)

        def add_rows(ref, slices, val):
            at = 0
            for start, size in slices:
                ref[pl.ds(start, size), :] += val[at:at + size]
                at += size

        def group(p, masks, blocks):
            head0, mask1, mask2 = masks
            heads = (head0, jnp.logical_not(head0))
            keys = [rows if prev is None else prev + rows for rows, prev in blocks]
            mask = [mask1 if prev is None else mask2 for _, prev in blocks]
            kk = [_load_rows(k_ref, ks).astype(BF16) for ks in keys]
            vv = [_load_rows(v_ref, ks).astype(BF16) for ks in keys]
            qb = [_load_rows(q_ref, rows) for rows, _ in blocks]
            dob = [_load_rows(do_ref, rows) for rows, _ in blocks]
            lse_b = [_load_rows(lse_ref, rows) for rows, _ in blocks]
            dlt_b = [_load_rows(dlt_ref, rows) for rows, _ in blocks]
            chains = [(g, h) for g in range(len(blocks)) for h in range(2)]
            qm = [jnp.where(heads[h], qb[g], 0.0).astype(BF16) for g, h in chains]
            dom = [jnp.where(heads[h], dob[g], 0.0).astype(BF16) for g, h in chains]
            s = [_dot_nt(qm[c], kk[g]) for c, (g, h) in enumerate(chains)]
            dp = [_dot_nt(dom[c], vv[g]) for c, (g, h) in enumerate(chains)]
            pr = [jnp.where(mask[g], jnp.exp(s[c] - lse_b[g][:, h * HEAD_DIM:h * HEAD_DIM + 1]), 0.0)
                  for c, (g, h) in enumerate(chains)]
            ds = [(pr[c] * (dp[c] - dlt_b[g][:, h * HEAD_DIM:h * HEAD_DIM + 1])).astype(BF16)
                  for c, (g, h) in enumerate(chains)]
            dv = [_dot_tn(pr[c].astype(BF16), dom[c]) for c in range(len(chains))]
            dk = [_dot_tn(ds[c], qm[c]) for c in range(len(chains))]
            dq = [_dot(ds[c], kk[g]) for c, (g, h) in enumerate(chains)]
            for g, (rows, _) in enumerate(blocks):
                add_rows(dqa_ref, rows, jnp.where(head0, dq[2 * g], dq[2 * g + 1]))
                add_rows(dka_ref, keys[g], dk[2 * g] + dk[2 * g + 1])
                add_rows(dva_ref, keys[g], dv[2 * g] + dv[2 * g + 1])

        _for_each_group(group)

        @pl.when(pl.program_id(0) == 0)
        def _():
            def tables(i, carry):
                rows = pl.ds(pl.multiple_of(i * 256, 256), 256)
                c, sa, sb = _rot_tables(pos_ref[rows, :], invf_ref[...], ma_ref[...], mb_ref[...])
                rot_ref[0, rows, :] = c
                rot_ref[1, rows, :] = sa
                rot_ref[2, rows, :] = sb
                return carry

            lax.fori_loop(0, SEQ // 256, tables, 0)

        def finish(i, carry):
            rows = pl.ds(pl.multiple_of(i * 256, 256), 256)
            c, sa, sb = rot_ref[0, rows, :], rot_ref[1, rows, :], rot_ref[2, rows, :]
            dq_ref[rows, :] = _rot_t(dqa_ref[rows, :] * Q_SCALE, c, sa, sb).astype(BF16)
            dk_ref[rows, :] = _rot_t(dka_ref[rows, :], c, sa, sb).astype(BF16)
            dv_ref[rows, :] = dva_ref[rows, :].astype(BF16)
            return carry

        lax.fori_loop(0, SEQ // 256, finish, 0)

    slab = pl.BlockSpec((SEQ, 128), lambda i: (0, i))
    out = jax.ShapeDtypeStruct((SEQ, ATTN_W), BF16)
    acc = pltpu.VMEM((SEQ, 128), F32)
    return _hosted(
        "attn_bwd", body, ATTN_W // 128,
        [slab] * 6 + [_full_spec((SEQ, 1)), _full_spec((1, 128)), _full_spec((1, 128)), _full_spec((1, 128))],
        [slab] * 3, [out, out, out], (q, k, v, o, lse, do, pos_col, *rot),
        scratch_shapes=[acc, acc, acc, acc, pltpu.VMEM((3, SEQ, 128), F32)], exchanges=exchanges)


def in_bwd(dproj, w_in_t, x, g1, dx2, exchanges=()):
    tm = 512

    def body(dp_ref, w_ref, x_ref, g_ref, dx2_ref, dx_ref, dg_ref):
        dh1 = _dot(dp_ref[...], w_ref[...])
        dz, dg = _rms_bwd(x_ref[...], g_ref[...], dh1)
        dx_ref[...] = dx2_ref[...] + dz

        @pl.when(pl.program_id(0) == 0)
        def _():
            dg_ref[...] = jnp.zeros_like(dg_ref)

        dg_ref[...] += dg

    return _hosted(
        "in_bwd", body, SEQ // tm,
        [_row_spec(tm, IN_W), _full_spec((IN_W, D_MODEL)), _row_spec(tm, D_MODEL), _full_spec((1, D_MODEL)),
         _row_spec(tm, D_MODEL)],
        [_row_spec(tm, D_MODEL), _full_spec((1, D_MODEL))],
        [jax.ShapeDtypeStruct((SEQ, D_MODEL), F32), jax.ShapeDtypeStruct((1, D_MODEL), F32)],
        (dproj, w_in_t, x, g1, dx2), exchanges=exchanges)


def _coords():
    return lax.axis_index("x"), lax.axis_index("y"), lax.axis_index("c")


class Exchange:
    def __init__(self, srcs, bufs, new_shapes, n_sems, make, mid_step=None):
        self.srcs, self.bufs, self.new_shapes, self.n_sems, self.make = list(srcs), list(bufs), list(new_shapes), n_sems, make
        self.mid_step = mid_step


def _hosted(name, body, n_steps, in_specs, out_specs, out_shape, args, scratch_shapes=(), exchanges=(), prefetch=None):
    out_shape, out_specs = list(out_shape), list(out_specs)
    srcs = [a for ex in exchanges for a in ex.srcs]
    bufs = [a for ex in exchanges for a in ex.bufs]
    news = [s for ex in exchanges for s in ex.new_shapes]
    n_pre = 0 if prefetch is None else 1
    n_in, n_out, n_scr = len(args), len(out_shape), len(scratch_shapes)

    def wrapped(*refs):
        refs = list(refs)
        pre, refs = refs[:n_pre], refs[n_pre:]
        ins = refs[:n_in]
        src_refs = refs[n_in:n_in + len(srcs)]
        at = n_in + len(srcs) + len(bufs)
        outs = refs[at:at + n_out]
        buf_refs = refs[at + n_out:at + n_out + len(bufs)]
        new_refs = refs[at + n_out + len(bufs):at + n_out + len(bufs) + len(news)]
        at += n_out + len(bufs) + len(news)
        scratch, sems = refs[at:at + n_scr], refs[at + n_scr:]

        def copies(phase, which):
            made, si, bi, ni = [], 0, 0, 0
            for k, ex in enumerate(exchanges):
                if which(ex):
                    made.append(ex.make(phase, src_refs[si:si + len(ex.srcs)], buf_refs[bi:bi + len(ex.bufs)],
                                        new_refs[ni:ni + len(ex.new_shapes)], sems[2 * k], sems[2 * k + 1]))
                si, bi, ni = si + len(ex.srcs), bi + len(ex.bufs), ni + len(ex.new_shapes)
            return made

        if exchanges:
            @pl.when(pl.program_id(0) == 0)
            def _():
                for starts in copies("start", lambda ex: True):
                    for cp in starts:
                        cp.start()

        def pass_on(mid):
            for arrivals, starts in copies("middle", lambda ex: ex.mid_step == mid):
                for cp in arrivals:
                    cp.wait_recv()
                for cp in starts:
                    cp.start()

        for mid in sorted({ex.mid_step for ex in exchanges if isinstance(ex.mid_step, int)}):
            @pl.when(pl.program_id(0) == mid)
            def _(mid=mid):
                pass_on(mid)

        body(*pre, *ins, *outs, *scratch)

        if exchanges:
            @pl.when(pl.program_id(0) == n_steps - 1)
            def _():
                pass_on("end")
                for sends, recvs in copies("end", lambda ex: True):
                    for cp in sends:
                        cp.wait_send()
                    for cp in recvs:
                        cp.wait_recv()

    sem_shapes = []
    for ex in exchanges:
        sem_shapes += [pltpu.SemaphoreType.DMA((ex.n_sems,)), pltpu.SemaphoreType.DMA((ex.n_sems,))]
    all_in = list(in_specs) + [ANY] * (len(srcs) + len(bufs))
    all_out = out_specs + [ANY] * (len(bufs) + len(news))
    all_shape = out_shape + [jax.ShapeDtypeStruct(b.shape, b.dtype) for b in bufs] + news
    all_scratch = list(scratch_shapes) + sem_shapes
    aliases = {n_pre + n_in + len(srcs) + i: n_out + i for i in range(len(bufs))}
    if prefetch is None:
        call = pl.pallas_call(wrapped, name=name, grid=(n_steps,), in_specs=all_in, out_specs=all_out, out_shape=all_shape,
                              scratch_shapes=all_scratch, input_output_aliases=aliases, compiler_params=_params())
        outs = call(*args, *srcs, *bufs)
    else:
        spec = pltpu.PrefetchScalarGridSpec(num_scalar_prefetch=1, grid=(n_steps,), in_specs=all_in, out_specs=all_out,
                                            scratch_shapes=all_scratch)
        call = pl.pallas_call(wrapped, name=name, grid_spec=spec, out_shape=all_shape, input_output_aliases=aliases,
                              compiler_params=_params())
        outs = call(prefetch, *args, *srcs, *bufs)
    results, bi, ni = [], n_out, n_out + len(bufs)
    for ex in exchanges:
        results.append((list(outs[bi:bi + len(ex.bufs)]), list(outs[ni:ni + len(ex.new_shapes)])))
        bi, ni = bi + len(ex.bufs), ni + len(ex.new_shapes)
    return list(outs[:n_out]), results


def _gather_copies(kinds, bufs, ranges, send_sems, recv_sems):
    x, y, c = _coords()
    me, sibling = (x, y, c), (x, y, 1 - c)
    chips = [(1 - x, y), (x, 1 - y), (1 - x, 1 - y)]

    def copy(a, k, block, to):
        lo, hi = ranges[a]
        r = bufs[a].shape[0] // N_DEV
        rows = bufs[a].at[pl.ds((4 * block[0] + 2 * block[1] + block[2]) * r + lo, hi - lo), :]
        return pltpu.make_async_remote_copy(src_ref=rows, dst_ref=rows, send_sem=send_sems.at[7 * a + k],
                                            recv_sem=recv_sems.at[7 * a + k], device_id=to, device_id_type=MESH)

    every = range(len(bufs))
    make = {
        "out": lambda: [copy(a, 0, me, sibling) for a in every]
        + [copy(a, 1 + j, me, (*chip, c)) for a in every for j, chip in enumerate(chips)],
        "from_core": lambda: [copy(a, 0, sibling, me) for a in every],
        "from_chips": lambda: [copy(a, 1 + j, (*chip, c), me) for a in every for j, chip in enumerate(chips)],
        "on": lambda: [copy(a, 4 + j, (*chip, c), sibling) for a in every for j, chip in enumerate(chips)],
        "on_in": lambda: [copy(a, 4 + j, (*chip, 1 - c), me) for a in every for j, chip in enumerate(chips)],
    }
    return [make[kind]() for kind in kinds]


def gather(bufs, mid_step, ranges=None):
    ranges = ranges or [(0, b.shape[0] // N_DEV) for b in bufs]

    def make(phase, src_refs, buf_refs, new_refs, send_sems, recv_sems):
        kinds = {"start": ["out"], "middle": ["from_chips", "on"], "end": ["out", "on", "from_core", "on_in"]}[phase]
        made = _gather_copies(kinds, buf_refs, ranges, send_sems, recv_sems)
        if phase == "start":
            return made[0]
        if phase == "middle":
            return made[0], made[1]
        return made[0] + made[1], made[2] + made[3]

    return Exchange([], bufs, [], 7 * len(bufs), make, mid_step=mid_step)


def all_gather_in_place(name, bufs):
    n = len(bufs)
    ranges = [(0, b.shape[0] // N_DEV) for b in bufs]

    def body(*refs):
        outs, send_sems, recv_sems = refs[n:2 * n], refs[2 * n], refs[2 * n + 1]
        out, from_core, from_chips, on, on_in = _gather_copies(
            ["out", "from_core", "from_chips", "on", "on_in"], outs, ranges, send_sems, recv_sems)
        for cp in out:
            cp.start()
        for cp in from_chips:
            cp.wait_recv()
        for cp in on:
            cp.start()
        for cp in from_core + on_in:
            cp.wait_recv()
        for cp in out + on:
            cp.wait_send()

    return pl.pallas_call(
        body, name=name, in_specs=[ANY] * n, out_specs=[ANY] * n,
        out_shape=[jax.ShapeDtypeStruct(b.shape, b.dtype) for b in bufs],
        scratch_shapes=[pltpu.SemaphoreType.DMA((7 * n,)), pltpu.SemaphoreType.DMA((7 * n,))],
        input_output_aliases={i: i for i in range(n)},
    )(*bufs)


def place_shards(shards, dev):
    n = len(shards)

    def body(dev_ref, *refs):
        for a in range(n):
            refs[n + a][...] = refs[a][...].astype(BF16)

    spec = pltpu.PrefetchScalarGridSpec(
        num_scalar_prefetch=1, grid=(1,),
        in_specs=[pl.BlockSpec(s.shape, lambda i, dev_ref: (0, 0)) for s in shards],
        out_specs=[pl.BlockSpec(s.shape, lambda i, dev_ref: (dev_ref[0], 0)) for s in shards])
    return pl.pallas_call(
        body, name="place_shards", grid_spec=spec,
        out_shape=[jax.ShapeDtypeStruct((N_DEV * s.shape[0], s.shape[1]), BF16) for s in shards],
        compiler_params=_params(),
    )(dev, *shards)


def _swap(copies_of):
    def make(phase, src_refs, buf_refs, new_refs, send_sems, recv_sems):
        copies = copies_of(src_refs, new_refs, send_sems, recv_sems)
        return copies if phase == "start" else (copies, copies)

    return make


def to_sibling(grads):
    def copies_of(src_refs, new_refs, send_sems, recv_sems):
        x, y, c = _coords()
        return [pltpu.make_async_remote_copy(
            src_ref=src_refs[a].at[2 * xy + 1 - c], dst_ref=new_refs[a].at[xy], send_sem=send_sems.at[4 * a + xy],
            recv_sem=recv_sems.at[4 * a + xy], device_id=(x, y, 1 - c), device_id_type=MESH)
            for a in range(len(src_refs)) for xy in range(4)]

    return Exchange(grads, [], [jax.ShapeDtypeStruct((4,) + g.shape[1:], g.dtype) for g in grads], 4 * len(grads),
                    _swap(copies_of))


def to_chips(parts):
    def copies_of(src_refs, new_refs, send_sems, recv_sems):
        x, y, c = _coords()
        chips = [(1 - x, y), (x, 1 - y), (1 - x, 1 - y)]
        return [pltpu.make_async_remote_copy(
            src_ref=src_refs[a].at[2 * px + py], dst_ref=new_refs[a].at[2 * x + y], send_sem=send_sems.at[3 * a + j],
            recv_sem=recv_sems.at[3 * a + j], device_id=(px, py, c), device_id_type=MESH)
            for a in range(len(src_refs)) for j, (px, py) in enumerate(chips)]

    return Exchange(parts, [], [jax.ShapeDtypeStruct(p.shape, p.dtype) for p in parts], 3 * len(parts), _swap(copies_of))


def to_owners(grad):
    def copies_of(src_refs, new_refs, send_sems, recv_sems):
        x, y, c = _coords()
        copies = []
        for m in range(1, N_DEV):
            px, py, pc = x ^ (m >> 2), y ^ ((m >> 1) & 1), c ^ (m & 1)
            copies.append(pltpu.make_async_remote_copy(
                src_ref=src_refs[0].at[4 * px + 2 * py + pc], dst_ref=new_refs[0].at[4 * x + 2 * y + c],
                send_sem=send_sems.at[m - 1], recv_sem=recv_sems.at[m - 1], device_id=(px, py, pc), device_id_type=MESH))
        return copies

    return Exchange([grad], [], [jax.ShapeDtypeStruct(grad.shape, grad.dtype)], N_DEV - 1, _swap(copies_of))


def exchange_only(name, exchanges):
    def body():
        pass

    return _hosted(name, body, 1, [], [], [], [], exchanges=exchanges)[1]


def sum_cores(name, grad, other, core):
    _, r, w = other.shape

    def body(core_ref, g_ref, o_ref, out_ref):
        out_ref[...] = (g_ref[...].astype(F32) + o_ref[...].astype(F32)).astype(out_ref.dtype)

    return pl.pallas_call(
        body, name=name,
        grid_spec=pltpu.PrefetchScalarGridSpec(
            num_scalar_prefetch=1, grid=(4,),
            in_specs=[pl.BlockSpec((1, r, w), lambda i, core_ref: (2 * i + core_ref[0], 0, 0)),
                      pl.BlockSpec((1, r, w), lambda i, core_ref: (i, 0, 0))],
            out_specs=pl.BlockSpec((1, r, w), lambda i, core_ref: (i, 0, 0))),
        out_shape=jax.ShapeDtypeStruct(other.shape, other.dtype),
        compiler_params=_params(),
    )(core, grad, other)


def sum_owned(name, grad, others, dev_ids):
    _, r, w = grad.shape

    def body(ids_ref, *refs):
        acc = refs[0][0]
        for k in range(1, N_DEV):
            acc = acc + refs[k][0]
        refs[N_DEV][...] = acc

    def pick(k):
        return pl.BlockSpec((1, r, w), lambda i, ids_ref: (ids_ref[k], 0, 0))

    return pl.pallas_call(
        body, name=name,
        grid_spec=pltpu.PrefetchScalarGridSpec(
            num_scalar_prefetch=1, grid=(1,), in_specs=[pick(k) for k in range(N_DEV)],
            out_specs=pl.BlockSpec((r, w), lambda i, ids_ref: (ids_ref[0], 0))),
        out_shape=jax.ShapeDtypeStruct((N_DEV * r, w), F32),
        compiler_params=_params(),
    )(dev_ids, grad, *([others] * (N_DEV - 1)))


def _adamw_update(w, g, m, v):
    nm = ADAM_B1 * m + np.float32(1.0 - ADAM_B1) * g
    nv = ADAM_B2 * v + np.float32(1.0 - ADAM_B2) * (g * g)
    m_hat = nm / np.float32(1.0 - ADAM_B1 ** ADAM_STEP)
    v_hat = nv / np.float32(1.0 - ADAM_B2 ** ADAM_STEP)
    return -ADAM_LR * (m_hat / (jnp.sqrt(v_hat) + ADAM_EPS) + ADAM_WD * w), nm, nv


def adamw_of_sums(name, part, others, chip_ids, w, m, v):
    _, r, wd = part.shape

    def body(ids_ref, p_ref, a_ref, b_ref, c_ref, w_ref, m_ref, v_ref, g_ref, d_ref, nm_ref, nv_ref):
        g = ((p_ref[0].astype(F32) + a_ref[0].astype(F32)) + b_ref[0].astype(F32)) + c_ref[0].astype(F32)
        g_ref[...] = g
        d_ref[...], nm_ref[...], nv_ref[...] = _adamw_update(w_ref[...], g, m_ref[...], v_ref[...])

    def pick(k):
        return pl.BlockSpec((1, r, wd), lambda i, ids_ref: (ids_ref[k], 0, 0))

    whole = pl.BlockSpec((r, wd), lambda i, ids_ref: (0, 0))
    shape = jax.ShapeDtypeStruct((r, wd), F32)
    return pl.pallas_call(
        body, name=name,
        grid_spec=pltpu.PrefetchScalarGridSpec(
            num_scalar_prefetch=1, grid=(1,), in_specs=[pick(0), pick(1), pick(2), pick(3), whole, whole, whole],
            out_specs=[whole] * 4),
        out_shape=[shape] * 4,
        compiler_params=_params(),
    )(chip_ids, part, others, others, others, w, m, v)


def adamw(name, w, g, m, v):
    def body(w_ref, g_ref, m_ref, v_ref, d_ref, nm_ref, nv_ref):
        d_ref[...], nm_ref[...], nv_ref[...] = _adamw_update(w_ref[...], g_ref[...], m_ref[...], v_ref[...])

    shape = jax.ShapeDtypeStruct(w.shape, F32)
    spec = _full_spec(w.shape)
    return pl.pallas_call(
        body, name=name, grid=(1,), in_specs=[spec] * 4, out_specs=[spec] * 3, out_shape=[shape] * 3,
        compiler_params=_params(),
    )(w, g, m, v)


def _pack_small(parts):
    flat = jnp.concatenate([parts[name].reshape(-1) for name, _ in SMALL])
    return jnp.pad(flat, (0, SMALL_ROWS * 128 - flat.shape[0])).reshape(SMALL_ROWS, 128)


def _unpack_small(packed, like):
    flat = packed.reshape(-1)
    out, at = {}, 0
    for name, size in SMALL:
        out[name] = flat[at:at + size].reshape(like[name].shape)
        at += size
    return out


def kernel(x, positions, pre_mix_norm, w_in, sgu_ln_gain, sgu_ln_bias, sgu_w_spatial, sgu_b_spatial, attn_out_norm, sgu_out_norm, w_out, post_mix_norm, pre_ffn_norm, w_gate, w_up, w_down, post_ffn_norm, loss_target, m_pre_mix_norm, m_w_in, m_sgu_ln_gain, m_sgu_ln_bias, m_sgu_w_spatial, m_sgu_b_spatial, m_attn_out_norm, m_sgu_out_norm, m_w_out, m_post_mix_norm, m_pre_ffn_norm, m_w_gate, m_w_up, m_w_down, m_post_ffn_norm, v_pre_mix_norm, v_w_in, v_sgu_ln_gain, v_sgu_ln_bias, v_sgu_w_spatial, v_sgu_b_spatial, v_attn_out_norm, v_sgu_out_norm, v_w_out, v_post_mix_norm, v_pre_ffn_norm, v_w_gate, v_w_up, v_w_down, v_post_ffn_norm):
    small_w = dict(pre_mix_norm=pre_mix_norm, sgu_ln_gain=sgu_ln_gain, sgu_ln_bias=sgu_ln_bias, sgu_w_spatial=sgu_w_spatial,
                   sgu_b_spatial=sgu_b_spatial, attn_out_norm=attn_out_norm, sgu_out_norm=sgu_out_norm,
                   post_mix_norm=post_mix_norm, pre_ffn_norm=pre_ffn_norm, post_ffn_norm=post_ffn_norm)
    small_m = dict(pre_mix_norm=m_pre_mix_norm, sgu_ln_gain=m_sgu_ln_gain, sgu_ln_bias=m_sgu_ln_bias, sgu_w_spatial=m_sgu_w_spatial,
                   sgu_b_spatial=m_sgu_b_spatial, attn_out_norm=m_attn_out_norm, sgu_out_norm=m_sgu_out_norm,
                   post_mix_norm=m_post_mix_norm, pre_ffn_norm=m_pre_ffn_norm, post_ffn_norm=m_post_ffn_norm)
    small_v = dict(pre_mix_norm=v_pre_mix_norm, sgu_ln_gain=v_sgu_ln_gain, sgu_ln_bias=v_sgu_ln_bias, sgu_w_spatial=v_sgu_w_spatial,
                   sgu_b_spatial=v_sgu_b_spatial, attn_out_norm=v_attn_out_norm, sgu_out_norm=v_sgu_out_norm,
                   post_mix_norm=v_post_mix_norm, pre_ffn_norm=v_pre_ffn_norm, post_ffn_norm=v_post_ffn_norm)
    for table in (small_w, small_m, small_v):
        table["loss_sum"] = jnp.zeros((1,), F32)

    x2d = x[0]
    target = loss_target[0]
    pos_col = positions.reshape(SEQ, 1)
    rot = _rot_consts()
    w_sp = sgu_w_spatial[0]
    bfull = jnp.repeat(sgu_b_spatial[0].T, HEAD_DIM, axis=1)

    x_i, y_i, c_i = (lax.axis_index(a).astype(jnp.int32) for a in MESH_AXES)
    dev = 4 * x_i + 2 * y_i + c_i
    core = c_i.reshape(1)
    chip = 2 * x_i + y_i
    chip_ids = jnp.stack([chip, chip ^ 1, chip ^ 2, chip ^ 3])
    dev_ids = jnp.stack([dev ^ m for m in range(N_DEV)])

    w_in_t, w_gate_t, w_up_t, w_out_f, w_down_f = place_shards(
        [w_in[0].T, w_gate[0].T, w_up[0].T, w_out[0], w_down[0]], dev.reshape(1))
    (w_in_t,) = all_gather_in_place("gather_w_in", [w_in_t])

    part, more, chunk = 112, 176, D_FF // N_DEV
    (h1, q, k, v, u, vs), [([w_out_f, w_gate_t], _)] = in_proj(
        x2d, pos_col, pre_mix_norm, w_in_t, rot,
        exchanges=[gather([w_out_f, w_gate_t], "end", [(0, D_MODEL // N_DEV), (0, part)])])
    q, k, v = (_to_residue_order(t) for t in (q, k, v))
    (attn_r, lse), [([w_gate_t, w_up_t], _)] = attn_fwd(
        q, k, v, exchanges=[gather([w_gate_t, w_up_t], "end", [(part, chunk), (0, part)])])
    attn = _from_residue_order(attn_r)
    (sgu,), [([w_up_t], _)] = sgu_fwd(
        u, vs, sgu_ln_gain, sgu_ln_bias, w_sp, bfull, exchanges=[gather([w_up_t], "end", [(part, more)])])
    (mix, y, x2, h2), [([w_up_t], _)] = out_proj(
        attn, sgu, x2d, attn_out_norm, sgu_out_norm, w_out_f, post_mix_norm, pre_ffn_norm,
        exchanges=[gather([w_up_t], "end", [(more, chunk)])])
    (gate, up, act), [([w_down_f], _)] = ffn_up(h2, w_gate_t, w_up_t, exchanges=[gather([w_down_f], "end")])
    df, dx3, d_post_ffn, sq_err = ffn_down_loss(act, w_down_f, x2, post_ffn_norm, target)

    g_w_down, _ = weight_grad("grad_w_down", act, df)
    (dgate, dup), [(_, [s_down])] = ffn_act_bwd(df, w_down_f, gate, up, exchanges=[to_sibling([g_w_down])])
    p_down = sum_cores("sum_cores_down", g_w_down, s_down, core)
    g_w_gate, _ = weight_grad("grad_w_gate", dgate, h2)
    g_w_up, [(_, [s_gate])] = weight_grad("grad_w_up", dup, h2, exchanges=[to_sibling([g_w_gate])])
    p_gate = sum_cores("sum_cores_gate", g_w_gate, s_gate, core)
    (dx2, dy, d_pre_ffn, d_post_mix), [(_, [s_up]), (_, [c_down])] = ffn_in_bwd(
        dgate, dup, w_gate_t, w_up_t, x2, pre_ffn_norm, dx3, y, post_mix_norm,
        exchanges=[to_sibling([g_w_up]), to_chips([p_down])])
    p_up = sum_cores("sum_cores_up", g_w_up, s_up, core)
    g_w_out, _ = weight_grad("grad_w_out", mix, dy)
    (dattn, dsgu, d_attn_out, d_sgu_out), [(_, [s_out])] = mix_bwd(
        dy, w_out_f, attn, sgu, attn_out_norm, sgu_out_norm, exchanges=[to_sibling([g_w_out])])
    p_out = sum_cores("sum_cores_out", g_w_out, s_out, core)
    (du, dvs, d_ln_gain, d_ln_bias, d_w_sp, d_bfull), [(_, [c_out])] = sgu_bwd(
        u, vs, dsgu, sgu_ln_gain, sgu_ln_bias, w_sp, bfull, exchanges=[to_chips([p_out])])
    (dq, dk, dv), [(_, [c_gate, c_up])] = attn_bwd(
        q, k, v, attn_r, lse, _to_residue_order(dattn), _to_residue_order(pos_col), rot,
        exchanges=[to_chips([p_gate, p_up])])
    dq, dk, dv = (_from_residue_order(t) for t in (dq, dk, dv))
    dproj = jnp.concatenate([dq, dk, dv, du, dvs], axis=1)
    g_w_in, _ = weight_grad("grad_w_in", dproj, h1)
    (grad_x, d_pre_mix), [(_, [s_in])] = in_bwd(
        dproj, w_in_t, x2d, pre_mix_norm, dx2, exchanges=[to_sibling([g_w_in])])
    p_in = sum_cores("sum_cores_in", g_w_in, s_in, core)

    d_b_sp = d_bfull.reshape(CHUNK, N_GROUPS, HEAD_DIM).sum(axis=-1).T
    small_g = _pack_small(dict(pre_mix_norm=d_pre_mix, sgu_ln_gain=d_ln_gain, sgu_ln_bias=d_ln_bias, sgu_w_spatial=d_w_sp,
                               sgu_b_spatial=d_b_sp, attn_out_norm=d_attn_out, sgu_out_norm=d_sgu_out,
                               post_mix_norm=d_post_mix, pre_ffn_norm=d_pre_ffn, post_ffn_norm=d_post_ffn,
                               loss_sum=sq_err))
    small_g = small_g.reshape(N_DEV, SMALL_ROWS // N_DEV, 128)
    [(_, [c_in]), (_, [o_small])] = exchange_only("grads_tail", [to_chips([p_in]), to_owners(small_g)])
    all_small = sum_owned("sum_small", small_g, o_small, dev_ids)
    (all_small,) = all_gather_in_place("gather_small_grads", [all_small])

    big = {}
    for name, w, p, c, m, vv in (("w_in", w_in, p_in, c_in, m_w_in, v_w_in), ("w_gate", w_gate, p_gate, c_gate, m_w_gate, v_w_gate),
                                 ("w_up", w_up, p_up, c_up, m_w_up, v_w_up)):
        big[name] = tuple(t.T[None] for t in adamw_of_sums("adamw_" + name, p, c, chip_ids, w[0].T, m[0].T, vv[0].T))
    for name, w, p, c, m, vv in (("w_out", w_out, p_out, c_out, m_w_out, v_w_out),
                                 ("w_down", w_down, p_down, c_down, m_w_down, v_w_down)):
        big[name] = tuple(t[None] for t in adamw_of_sums("adamw_" + name, p, c, chip_ids, w[0], m[0], vv[0]))
    sd, snm, snv = adamw("adamw_small", _pack_small(small_w), all_small, _pack_small(small_m), _pack_small(small_v))
    sg, sd, snm, snv = (_unpack_small(t, small_w) for t in (all_small, sd, snm, snv))
    loss = sg["loss_sum"][0] * np.float32(0.5 / D_MODEL)

    names = ["pre_mix_norm", "w_in", "sgu_ln_gain", "sgu_ln_bias", "sgu_w_spatial", "sgu_b_spatial", "attn_out_norm",
             "sgu_out_norm", "w_out", "post_mix_norm", "pre_ffn_norm", "w_gate", "w_up", "w_down", "post_ffn_norm"]
    outs = [loss, grad_x[None]]
    for i, table in enumerate((sg, sd, snm, snv)):
        for name in names:
            outs.append(big[name][i] if name in big else table[name])
    return tuple(outs)
```

```python
import functools

import numpy as np
import jax
import jax.numpy as jnp
from jax import lax
from jax.experimental import pallas as pl
from jax.experimental.pallas import tpu as pltpu
from jax.experimental.pallas import tpu_sc as plsc

F32 = jnp.float32
BF16 = jnp.bfloat16

SEQ = 2048
D_MODEL = 1024
ATTN_W = 512
SGU_W = 512
HEAD_DIM = 64
N_GROUPS = 8
CHUNK = 128
D_FF = 2816
IN_W = 3 * ATTN_W + 2 * SGU_W
DILATIONS = (1, 4, 16)
ROPE_THETA = 500000.0
ROT_DIM = 16
ROT_HALF = 8
RMS_EPS = 1e-6
LN_EPS = 1e-5
Q_SCALE = 0.125
NEG = -1e30

N_DEV = 8
MESH_AXES = ("x", "y", "c")
MESH = pl.DeviceIdType.MESH

ADAM_LR = 0.001
ADAM_B1 = 0.9
ADAM_B2 = 0.999
ADAM_EPS = 1e-08
ADAM_WD = 0.01
ADAM_STEP = 10

VMEM_LIMIT = 60 * 1024 * 1024
ANY = pl.BlockSpec(memory_space=pl.ANY)

SMALL = (("pre_mix_norm", 1024), ("sgu_ln_gain", 512), ("sgu_ln_bias", 512), ("sgu_w_spatial", 8 * 128 * 128),
         ("sgu_b_spatial", 1024), ("attn_out_norm", 512), ("sgu_out_norm", 512), ("post_mix_norm", 1024),
         ("pre_ffn_norm", 1024), ("post_ffn_norm", 1024), ("loss_sum", 1))
SMALL_ROWS = 1152


def _params(sem=("arbitrary",)):
    return pltpu.CompilerParams(dimension_semantics=sem, vmem_limit_bytes=VMEM_LIMIT)


def _dot(a, b):
    return jnp.dot(a, b, preferred_element_type=F32)


def _dot_nt(a, b):
    return lax.dot_general(a, b, (((1,), (1,)), ((), ())), preferred_element_type=F32)


def _dot_tn(a, b):
    return lax.dot_general(a, b, (((0,), (0,)), ((), ())), preferred_element_type=F32)


def _rms(z):
    return lax.rsqrt(jnp.mean(z * z, axis=-1, keepdims=True) + RMS_EPS)


def _rms_bwd(z, gain, d):
    r = _rms(z)
    n = z * r
    dn = d * gain
    dz = r * (dn - n * jnp.mean(dn * n, axis=-1, keepdims=True))
    return dz, jnp.sum(d * n, axis=0, keepdims=True)


def _gelu(z):
    return 0.5 * z * (1.0 + lax.erf(z * np.float32(1.0 / np.sqrt(2.0))))


def _gelu_grad(z):
    cdf = 0.5 * (1.0 + lax.erf(z * np.float32(1.0 / np.sqrt(2.0))))
    return cdf + z * jnp.exp(-0.5 * z * z) * np.float32(1.0 / np.sqrt(2.0 * np.pi))


def _rot_tables(pos_col, invf, ma, mb):
    ang = pos_col.astype(F32) * invf
    s = jnp.sin(ang)
    return jnp.cos(ang), s * ma, s * mb


def _rot(t, c, sa, sb):
    return t * c + pltpu.roll(t, 120, 1) * sa + pltpu.roll(t, 8, 1) * sb


def _rot_t(d, c, sa, sb):
    return d * c + pltpu.roll(d * sa, 8, 1) + pltpu.roll(d * sb, 120, 1)


def _rot_consts():
    lane = np.arange(128) % HEAD_DIM
    inv_freq = (np.float32(ROPE_THETA) ** (-np.arange(0, ROT_DIM, 2, dtype=np.float32) / np.float32(ROT_DIM))).astype(np.float32)
    invf = np.where(lane < ROT_DIM, inv_freq[lane % ROT_HALF], 0.0).astype(np.float32)
    ma = np.where(lane < ROT_HALF, -1.0, 0.0).astype(np.float32)
    mb = np.where((lane >= ROT_HALF) & (lane < ROT_DIM), 1.0, 0.0).astype(np.float32)
    return jnp.asarray(invf[None]), jnp.asarray(ma[None]), jnp.asarray(mb[None])


def _row_spec(tm, w):
    return pl.BlockSpec((tm, w), lambda i: (i, 0))


def _full_spec(shape):
    return pl.BlockSpec(shape, lambda i: (0,) * len(shape))


def in_proj(x, pos_col, g1, w_in_t, rot, exchanges=()):
    tm = 512

    def body(x_ref, pos_ref, g_ref, w_ref, invf_ref, ma_ref, mb_ref, h_ref, q_ref, k_ref, v_ref, u_ref, vs_ref):
        xf = x_ref[...]
        h = (xf * _rms(xf) * g_ref[...]).astype(BF16)
        h_ref[...] = h
        proj = _dot_nt(h, w_ref[...])
        c, sa, sb = _rot_tables(pos_ref[...], invf_ref[...], ma_ref[...], mb_ref[...])
        for j in range(ATTN_W // 128):
            q_ref[:, j * 128:(j + 1) * 128] = _rot(proj[:, j * 128:(j + 1) * 128], c, sa, sb) * Q_SCALE
            k_ref[:, j * 128:(j + 1) * 128] = _rot(proj[:, ATTN_W + j * 128:ATTN_W + (j + 1) * 128], c, sa, sb)
        v_ref[...] = proj[:, 2 * ATTN_W:3 * ATTN_W]
        u_ref[...] = proj[:, 3 * ATTN_W:3 * ATTN_W + SGU_W]
        vs_ref[...] = proj[:, 3 * ATTN_W + SGU_W:]

    act = jax.ShapeDtypeStruct((SEQ, 512), F32)
    return _hosted(
        "in_proj", body, SEQ // tm,
        [_row_spec(tm, D_MODEL), _row_spec(tm, 1), _full_spec((1, D_MODEL)), _full_spec((IN_W, D_MODEL)),
         _full_spec((1, 128)), _full_spec((1, 128)), _full_spec((1, 128))],
        [_row_spec(tm, D_MODEL)] + [_row_spec(tm, 512)] * 5,
        [jax.ShapeDtypeStruct((SEQ, D_MODEL), BF16)] + [act] * 5,
        (x, pos_col, g1, w_in_t, *rot), exchanges=exchanges)


RES = 16


def _to_residue_order(t):
    return t.reshape(SEQ // RES, RES, -1).transpose(1, 0, 2).reshape(t.shape)


def _from_residue_order(t):
    return t.reshape(RES, SEQ // RES, -1).transpose(1, 0, 2).reshape(t.shape)


def _block_rows(d, r, n):
    if d == 16:
        slices = [(128 * r, 128)]
    elif d == 4:
        slices = [(128 * (4 * b + r) + 32 * n, 32) for b in range(4)]
    else:
        slices = [(128 * b + 8 * n, 8) for b in range(RES)]
    return [(s if isinstance(s, int) else pl.multiple_of(s, z), z) for s, z in slices]


def _block_step(d, i):
    if d == 16:
        return i
    if d == 4:
        return 4 * (i & 31) + (i >> 5)
    return 16 * (i & 7) + (i >> 3)


def _attn_masks(d):
    row2 = _block_step(d, lax.broadcasted_iota(jnp.int32, (128, 256), 0))
    col2 = lax.broadcasted_iota(jnp.int32, (128, 256), 1)
    key2 = _block_step(d, col2 & 127)
    mask2 = jnp.logical_or(jnp.logical_and(col2 < 128, key2 >= row2), jnp.logical_and(col2 >= 128, key2 <= row2))
    row1 = _block_step(d, lax.broadcasted_iota(jnp.int32, (128, 128), 0))
    col1 = lax.broadcasted_iota(jnp.int32, (128, 128), 1)
    return col1 < HEAD_DIM, _block_step(d, col1) <= row1, mask2


def _load_rows(ref, slices):
    parts = [ref[pl.ds(s, z), :] for s, z in slices]
    return parts[0] if len(parts) == 1 else jnp.concatenate(parts, axis=0)


def _for_each_group(fn):
    for p, d in enumerate(DILATIONS):
        masks = _attn_masks(d)
        if d == 16:
            def group(i, carry, p=p, masks=masks):
                fn(p, masks, [(_block_rows(16, 4 * i + g, 0), None) for g in range(4)])
                return carry

            lax.fori_loop(0, 4, group, 0)
        elif d == 4:
            fn(p, masks, [(_block_rows(4, r, 0), None) for r in range(4)])

            def group(i, carry, p=p, masks=masks):
                fn(p, masks, [(_block_rows(4, r, i + 1), _block_rows(4, r, i)) for r in range(4)])
                return carry

            lax.fori_loop(0, 3, group, 0)
        else:
            fn(p, masks, [(_block_rows(1, 0, 0), None)])

            def group(i, carry, p=p, masks=masks):
                fn(p, masks, [(_block_rows(1, 0, 3 * i + g + 1), _block_rows(1, 0, 3 * i + g)) for g in range(3)])
                return carry

            lax.fori_loop(0, 5, group, 0)


def attn_fwd(q, k, v, exchanges=()):
    def body(q_ref, k_ref, v_ref, o_ref, lse_ref, op_ref, lp_ref):
        def group(p, masks, blocks):
            head0, mask1, mask2 = masks
            heads = (head0, jnp.logical_not(head0))
            keys = [rows if prev is None else prev + rows for rows, prev in blocks]
            mask = [mask1 if prev is None else mask2 for _, prev in blocks]
            qb = [_load_rows(q_ref, rows) for rows, _ in blocks]
            kk = [_load_rows(k_ref, ks).astype(BF16) for ks in keys]
            vv = [_load_rows(v_ref, ks).astype(BF16) for ks in keys]
            chains = [(g, hm) for g in range(len(blocks)) for hm in heads]
            s = [jnp.where(mask[g], _dot_nt(jnp.where(hm, qb[g], 0.0).astype(BF16), kk[g]), NEG) for g, hm in chains]
            m = [jnp.max(t, axis=-1, keepdims=True) for t in s]
            e = [jnp.exp(t - mt) for t, mt in zip(s, m)]
            l = [jnp.sum(t, axis=-1, keepdims=True) for t in e]
            pv = [_dot(t.astype(BF16), vv[g]) for t, (g, _) in zip(e, chains)]
            for g, (rows, _) in enumerate(blocks):
                o_blk = jnp.where(head0, pv[2 * g] / l[2 * g], pv[2 * g + 1] / l[2 * g + 1])
                l_blk = jnp.where(head0, jnp.broadcast_to(m[2 * g] + jnp.log(l[2 * g]), (128, 128)),
                                  jnp.broadcast_to(m[2 * g + 1] + jnp.log(l[2 * g + 1]), (128, 128)))
                at = 0
                for start, size in rows:
                    op_ref[p, pl.ds(start, size), :] = o_blk[at:at + size]
                    lp_ref[p, pl.ds(start, size), :] = l_blk[at:at + size]
                    at += size

        _for_each_group(group)

        def combine(i, carry):
            rows = pl.ds(pl.multiple_of(i * 256, 256), 256)
            ls = [lp_ref[p, rows, :] for p in range(3)]
            m = jnp.maximum(jnp.maximum(ls[0], ls[1]), ls[2])
            lse = m + jnp.log(jnp.exp(ls[0] - m) + jnp.exp(ls[1] - m) + jnp.exp(ls[2] - m))
            o = jnp.zeros((256, 128), F32)
            for p in range(3):
                o = o + jnp.exp(ls[p] - lse) * op_ref[p, rows, :]
            o_ref[rows, :] = o
            lse_ref[rows, :] = lse
            return carry

        lax.fori_loop(0, SEQ // 256, combine, 0)

    slab = pl.BlockSpec((SEQ, 128), lambda i: (0, i))
    out = jax.ShapeDtypeStruct((SEQ, ATTN_W), F32)
    return _hosted(
        "attn_fwd", body, ATTN_W // 128, [slab] * 3, [slab] * 2, [out, out], (q, k, v),
        scratch_shapes=[pltpu.VMEM((3, SEQ, 128), F32), pltpu.VMEM((3, SEQ, 128), F32)], exchanges=exchanges)


def _causal_weights(w_ref):
    row = lax.broadcasted_iota(jnp.int32, (CHUNK, CHUNK), 0)
    col = lax.broadcasted_iota(jnp.int32, (CHUNK, CHUNK), 1)
    return [jnp.where(col <= row, w_ref[g], 0.0).astype(BF16) for g in range(N_GROUPS)], col <= row


def _sgu_chunk_fwd(u, vs, lg, lb, wc, bfull, head0):
    ug = _gelu(u)
    vg = _gelu(vs)
    xc = vg - jnp.mean(vg, axis=-1, keepdims=True)
    rstd = lax.rsqrt(jnp.mean(xc * xc, axis=-1, keepdims=True) + LN_EPS)
    xhat = xc * rstd
    vn = xhat * lg + lb
    mixed = []
    for gp in range(SGU_W // 128):
        vp = vn[:, gp * 128:(gp + 1) * 128].astype(BF16)
        mixed.append(jnp.where(head0, _dot(wc[2 * gp], vp), _dot(wc[2 * gp + 1], vp)))
    ms = jnp.concatenate(mixed, axis=1) + bfull
    return ug, xhat, rstd, vn, ms


def sgu_fwd(u, vs, lg, lb, w_sp, bfull, exchanges=()):
    cpb = 4

    def body(u_ref, vs_ref, lg_ref, lb_ref, w_ref, b_ref, o_ref):
        wc, _ = _causal_weights(w_ref)
        head0 = lax.broadcasted_iota(jnp.int32, (CHUNK, 128), 1) < HEAD_DIM
        for ci in range(cpb):
            rows = pl.ds(ci * CHUNK, CHUNK)
            ug, _, _, _, ms = _sgu_chunk_fwd(u_ref[rows, :], vs_ref[rows, :], lg_ref[...], lb_ref[...], wc, b_ref[...], head0)
            o_ref[rows, :] = ug * ms

    tm = cpb * CHUNK
    return _hosted(
        "sgu_fwd", body, SEQ // tm,
        [_row_spec(tm, SGU_W), _row_spec(tm, SGU_W), _full_spec((1, SGU_W)), _full_spec((1, SGU_W)),
         _full_spec((N_GROUPS, CHUNK, CHUNK)), _full_spec((CHUNK, SGU_W))],
        [_row_spec(tm, SGU_W)], [jax.ShapeDtypeStruct((SEQ, SGU_W), F32)],
        (u, vs, lg, lb, w_sp, bfull), exchanges=exchanges)


def out_proj(attn, sgu, x, ga, gs, w_out, gpm, gpf, exchanges=()):
    tm = 512

    def body(a_ref, s_ref, x_ref, ga_ref, gs_ref, w_ref, gpm_ref, gpf_ref, mix_ref, y_ref, x2_ref, h2_ref):
        a = a_ref[...]
        s = s_ref[...]
        an = (a * _rms(a) * ga_ref[...]).astype(BF16)
        sn = (s * _rms(s) * gs_ref[...]).astype(BF16)
        mix_ref[:, :ATTN_W] = an
        mix_ref[:, ATTN_W:] = sn
        y = _dot(an, w_ref[:ATTN_W, :]) + _dot(sn, w_ref[ATTN_W:, :])
        y_ref[...] = y
        x2 = x_ref[...] + y * _rms(y) * gpm_ref[...]
        x2_ref[...] = x2
        h2_ref[...] = (x2 * _rms(x2) * gpf_ref[...]).astype(BF16)

    wide = jax.ShapeDtypeStruct((SEQ, D_MODEL), F32)
    wide16 = jax.ShapeDtypeStruct((SEQ, D_MODEL), BF16)
    return _hosted(
        "out_proj", body, SEQ // tm,
        [_row_spec(tm, ATTN_W), _row_spec(tm, SGU_W), _row_spec(tm, D_MODEL), _full_spec((1, ATTN_W)),
         _full_spec((1, SGU_W)), _full_spec((D_MODEL, D_MODEL)), _full_spec((1, D_MODEL)), _full_spec((1, D_MODEL))],
        [_row_spec(tm, D_MODEL)] * 4, [wide16, wide, wide, wide16],
        (attn, sgu, x, ga, gs, w_out, gpm, gpf), exchanges=exchanges)


def ffn_up(h2, w_gate_t, w_up_t, exchanges=()):
    tm = 256

    def body(h_ref, wg_ref, wu_ref, g_ref, u_ref, a_ref):
        h = h_ref[...]
        g = _dot_nt(h, wg_ref[...])
        u = _dot_nt(h, wu_ref[...])
        g_ref[...] = g.astype(BF16)
        u_ref[...] = u.astype(BF16)
        a_ref[...] = (g * jax.nn.sigmoid(g) * u).astype(BF16)

    ff = jax.ShapeDtypeStruct((SEQ, D_FF), BF16)
    return _hosted(
        "ffn_up", body, SEQ // tm,
        [_row_spec(tm, D_MODEL), _full_spec((D_FF, D_MODEL)), _full_spec((D_FF, D_MODEL))],
        [_row_spec(tm, D_FF)] * 3, [ff, ff, jax.ShapeDtypeStruct((SEQ, D_FF), BF16)],
        (h2, w_gate_t, w_up_t), exchanges=exchanges)


def ffn_down_loss(act, w_down, x2, gpo, target):
    tm = 512

    def body(a_ref, w_ref, x2_ref, g_ref, t_ref, df_ref, dx3_ref, dg_ref, loss_ref):
        f = _dot(a_ref[...], w_ref[...])
        gain = g_ref[...]
        err = x2_ref[...] + f * _rms(f) * gain - t_ref[...]
        dx3 = err * np.float32(1.0 / D_MODEL)
        dx3_ref[...] = dx3
        df, dg = _rms_bwd(f, gain, dx3)
        df_ref[...] = df.astype(BF16)

        @pl.when(pl.program_id(0) == 0)
        def _():
            dg_ref[...] = jnp.zeros_like(dg_ref)
            loss_ref[...] = jnp.zeros_like(loss_ref)

        dg_ref[...] += dg
        loss_ref[...] += jnp.sum(err * err, axis=(0, 1), keepdims=True)

    return pl.pallas_call(
        body, name="ffn_down_loss", grid=(SEQ // tm,),
        in_specs=[_row_spec(tm, D_FF), _full_spec((D_FF, D_MODEL)), _row_spec(tm, D_MODEL), _full_spec((1, D_MODEL)),
                  _row_spec(tm, D_MODEL)],
        out_specs=[_row_spec(tm, D_MODEL), _row_spec(tm, D_MODEL), _full_spec((1, D_MODEL)), _full_spec((1, 1))],
        out_shape=[jax.ShapeDtypeStruct((SEQ, D_MODEL), BF16), jax.ShapeDtypeStruct((SEQ, D_MODEL), F32),
                   jax.ShapeDtypeStruct((1, D_MODEL), F32), jax.ShapeDtypeStruct((1, 1), F32)],
        compiler_params=_params(),
    )(act, w_down, x2, gpo, target)


def ffn_act_bwd(df, w_down, gate, up, exchanges=()):
    tm = 256

    def body(df_ref, w_ref, g_ref, u_ref, dg_ref, du_ref):
        dact = _dot_nt(df_ref[...], w_ref[...])
        g = g_ref[...].astype(F32)
        s = jax.nn.sigmoid(g)
        du_ref[...] = (dact * g * s).astype(BF16)
        dg_ref[...] = (dact * u_ref[...].astype(F32) * (s * (1.0 + g * (1.0 - s)))).astype(BF16)

    ff16 = jax.ShapeDtypeStruct((SEQ, D_FF), BF16)
    return _hosted(
        "ffn_act_bwd", body, SEQ // tm,
        [_row_spec(tm, D_MODEL), _full_spec((D_FF, D_MODEL)), _row_spec(tm, D_FF), _row_spec(tm, D_FF)],
        [_row_spec(tm, D_FF)] * 2, [ff16, ff16], (df, w_down, gate, up), exchanges=exchanges)


def ffn_in_bwd(dgate, dup, w_gate_t, w_up_t, x2, gpf, dx3, y, gpm, exchanges=()):
    tm = 256

    def body(dg_ref, du_ref, wg_ref, wu_ref, x2_ref, gpf_ref, dx3_ref, y_ref, gpm_ref, dx2_ref, dy_ref, dgpf_ref, dgpm_ref):
        dh2 = _dot(dg_ref[...], wg_ref[...]) + _dot(du_ref[...], wu_ref[...])
        dz, dgpf = _rms_bwd(x2_ref[...], gpf_ref[...], dh2)
        dx2 = dx3_ref[...] + dz
        dx2_ref[...] = dx2
        dy, dgpm = _rms_bwd(y_ref[...], gpm_ref[...], dx2)
        dy_ref[...] = dy.astype(BF16)

        @pl.when(pl.program_id(0) == 0)
        def _():
            dgpf_ref[...] = jnp.zeros_like(dgpf_ref)
            dgpm_ref[...] = jnp.zeros_like(dgpm_ref)

        dgpf_ref[...] += dgpf
        dgpm_ref[...] += dgpm

    vec = jax.ShapeDtypeStruct((1, D_MODEL), F32)
    return _hosted(
        "ffn_in_bwd", body, SEQ // tm,
        [_row_spec(tm, D_FF), _row_spec(tm, D_FF), _full_spec((D_FF, D_MODEL)), _full_spec((D_FF, D_MODEL)),
         _row_spec(tm, D_MODEL), _full_spec((1, D_MODEL)), _row_spec(tm, D_MODEL), _row_spec(tm, D_MODEL),
         _full_spec((1, D_MODEL))],
        [_row_spec(tm, D_MODEL), _row_spec(tm, D_MODEL), _full_spec((1, D_MODEL)), _full_spec((1, D_MODEL))],
        [jax.ShapeDtypeStruct((SEQ, D_MODEL), F32), jax.ShapeDtypeStruct((SEQ, D_MODEL), BF16), vec, vec],
        (dgate, dup, w_gate_t, w_up_t, x2, gpf, dx3, y, gpm), exchanges=exchanges)


def weight_grad(name, a, b, exchanges=()):
    m, n = a.shape[1], b.shape[1]
    tr = 256

    def body(a_ref, b_ref, o_ref):
        o_ref[...] = _dot_tn(a_ref[...], b_ref[...]).astype(BF16)

    (out,), done = _hosted(
        name, body, m // tr, [pl.BlockSpec((SEQ, tr), lambda i: (0, i)), _full_spec((SEQ, n))],
        [_row_spec(tr, n)], [jax.ShapeDtypeStruct((m, n), BF16)], (a, b), exchanges=exchanges)
    return out.reshape(N_DEV, m // N_DEV, n), done


def mix_bwd(dy, w_out, attn, sgu, ga, gs, exchanges=()):
    tm = 512

    def body(dy_ref, w_ref, a_ref, s_ref, ga_ref, gs_ref, da_ref, ds_ref, dga_ref, dgs_ref):
        dy = dy_ref[...]
        da, dga = _rms_bwd(a_ref[...], ga_ref[...], _dot_nt(dy, w_ref[:ATTN_W, :]))
        ds, dgs = _rms_bwd(s_ref[...], gs_ref[...], _dot_nt(dy, w_ref[ATTN_W:, :]))
        da_ref[...] = da
        ds_ref[...] = ds

        @pl.when(pl.program_id(0) == 0)
        def _():
            dga_ref[...] = jnp.zeros_like(dga_ref)
            dgs_ref[...] = jnp.zeros_like(dgs_ref)

        dga_ref[...] += dga
        dgs_ref[...] += dgs

    half = jax.ShapeDtypeStruct((SEQ, 512), F32)
    vec = jax.ShapeDtypeStruct((1, 512), F32)
    return _hosted(
        "mix_bwd", body, SEQ // tm,
        [_row_spec(tm, D_MODEL), _full_spec((D_MODEL, D_MODEL)), _row_spec(tm, 512), _row_spec(tm, 512),
         _full_spec((1, 512)), _full_spec((1, 512))],
        [_row_spec(tm, 512), _row_spec(tm, 512), _full_spec((1, 512)), _full_spec((1, 512))],
        [half, half, vec, vec], (dy, w_out, attn, sgu, ga, gs), exchanges=exchanges)


def sgu_bwd(u, vs, dsgu, lg, lb, w_sp, bfull, exchanges=()):
    cpb = 4

    def body(u_ref, vs_ref, d_ref, lg_ref, lb_ref, w_ref, b_ref, du_ref, dvs_ref, dlg_ref, dlb_ref, dw_ref, db_ref):
        wc, causal = _causal_weights(w_ref)
        head0 = lax.broadcasted_iota(jnp.int32, (CHUNK, 128), 1) < HEAD_DIM
        lg = lg_ref[...]

        @pl.when(pl.program_id(0) == 0)
        def _():
            dlg_ref[...] = jnp.zeros_like(dlg_ref)
            dlb_ref[...] = jnp.zeros_like(dlb_ref)
            dw_ref[...] = jnp.zeros_like(dw_ref)
            db_ref[...] = jnp.zeros_like(db_ref)

        for ci in range(cpb):
            rows = pl.ds(ci * CHUNK, CHUNK)
            u = u_ref[rows, :]
            vs = vs_ref[rows, :]
            d = d_ref[rows, :]
            ug, xhat, rstd, vn, ms = _sgu_chunk_fwd(u, vs, lg, lb_ref[...], wc, b_ref[...], head0)
            du_ref[rows, :] = (d * ms * _gelu_grad(u)).astype(BF16)
            dms = d * ug
            db_ref[...] += dms
            dvn = []
            for gp in range(SGU_W // 128):
                dmp = dms[:, gp * 128:(gp + 1) * 128]
                dm0 = jnp.where(head0, dmp, 0.0).astype(BF16)
                dm1 = jnp.where(head0, 0.0, dmp).astype(BF16)
                vp = vn[:, gp * 128:(gp + 1) * 128].astype(BF16)
                dw_ref[2 * gp] += _dot_nt(dm0, vp)
                dw_ref[2 * gp + 1] += _dot_nt(dm1, vp)
                dvn.append(_dot_tn(wc[2 * gp], dm0) + _dot_tn(wc[2 * gp + 1], dm1))
            dvn = jnp.concatenate(dvn, axis=1)
            dlg_ref[...] += jnp.sum(dvn * xhat, axis=0, keepdims=True)
            dlb_ref[...] += jnp.sum(dvn, axis=0, keepdims=True)
            dxh = dvn * lg
            dvg = rstd * (dxh - jnp.mean(dxh, axis=-1, keepdims=True) - xhat * jnp.mean(dxh * xhat, axis=-1, keepdims=True))
            dvs_ref[rows, :] = (dvg * _gelu_grad(vs)).astype(BF16)

        @pl.when(pl.program_id(0) == pl.num_programs(0) - 1)
        def _():
            for g in range(N_GROUPS):
                dw_ref[g] = jnp.where(causal, dw_ref[g], 0.0)

    tm = cpb * CHUNK
    half16 = jax.ShapeDtypeStruct((SEQ, SGU_W), BF16)
    vec = jax.ShapeDtypeStruct((1, SGU_W), F32)
    return _hosted(
        "sgu_bwd", body, SEQ // tm,
        [_row_spec(tm, SGU_W)] * 3 + [_full_spec((1, SGU_W)), _full_spec((1, SGU_W)),
                                      _full_spec((N_GROUPS, CHUNK, CHUNK)), _full_spec((CHUNK, SGU_W))],
        [_row_spec(tm, SGU_W), _row_spec(tm, SGU_W), _full_spec((1, SGU_W)), _full_spec((1, SGU_W)),
         _full_spec((N_GROUPS, CHUNK, CHUNK)), _full_spec((CHUNK, SGU_W))],
        [half16, half16, vec, vec, jax.ShapeDtypeStruct((N_GROUPS, CHUNK, CHUNK), F32),
         jax.ShapeDtypeStruct((CHUNK, SGU_W), F32)],
        (u, vs, dsgu, lg, lb, w_sp, bfull), exchanges=exchanges)


def attn_bwd(q, k, v, o, lse, do, pos_col, rot, exchanges=()):
    def body(q_ref, k_ref, v_ref, o_ref, lse_ref, do_ref, pos_ref, invf_ref, ma_ref, mb_ref,
             dq_ref, dk_ref, dv_ref, dqa_ref, dka_ref, dva_ref, dlt_ref, rot_ref):
        dqa_ref[...] = jnp.zeros_like(dqa_ref)
        dka_ref[...] = jnp.zeros_like(dka_ref)
        dva_ref[...] = jnp.zeros_like(dva_ref)

        def delta(i, carry):
            rows = pl.ds(pl.multiple_of(i * 256, 256), 256)
            prod = do_ref[rows, :] * o_ref[rows, :]
            h0 = lax.broadcasted_iota(jnp.int32, (256, 128), 1) < HEAD_DIM
            d0 = jnp.sum(jnp.where(h0, prod, 0.0), axis=-1, keepdims=True)
            d1 = jnp.sum(jnp.where(h0, 0.0, prod), axis=-1, keepdims=True)
            dlt_ref[rows, :] = jnp.where(h0, d0, d1)
            return carry

        lax.fori_loop(0, SEQ // 256, delta, 0)

        def add_rows(ref, slices, val):
            at = 0
            for start, size in slices:
                ref[pl.ds(start, size), :] += val[at:at + size]
                at += size

        def group(p, masks, blocks):
            head0, mask1, mask2 = masks
            heads = (head0, jnp.logical_not(head0))
            keys = [rows if prev is None else prev + rows for rows, prev in blocks]
            mask = [mask1 if prev is None else mask2 for _, prev in blocks]
            kk = [_load_rows(k_ref, ks).astype(BF16) for ks in keys]
            vv = [_load_rows(v_ref, ks).astype(BF16) for ks in keys]
            qb = [_load_rows(q_ref, rows) for rows, _ in blocks]
            dob = [_load_rows(do_ref, rows) for rows, _ in blocks]
            lse_b = [_load_rows(lse_ref, rows) for rows, _ in blocks]
            dlt_b = [_load_rows(dlt_ref, rows) for rows, _ in blocks]
            chains = [(g, h) for g in range(len(blocks)) for h in range(2)]
            qm = [jnp.where(heads[h], qb[g], 0.0).astype(BF16) for g, h in chains]
            dom = [jnp.where(heads[h], dob[g], 0.0).astype(BF16) for g, h in chains]
            s = [_dot_nt(qm[c], kk[g]) for c, (g, h) in enumerate(chains)]
            dp = [_dot_nt(dom[c], vv[g]) for c, (g, h) in enumerate(chains)]
            pr = [jnp.where(mask[g], jnp.exp(s[c] - lse_b[g][:, h * HEAD_DIM:h * HEAD_DIM + 1]), 0.0)
                  for c, (g, h) in enumerate(chains)]
            ds = [(pr[c] * (dp[c] - dlt_b[g][:, h * HEAD_DIM:h * HEAD_DIM + 1])).astype(BF16)
                  for c, (g, h) in enumerate(chains)]
            dv = [_dot_tn(pr[c].astype(BF16), dom[c]) for c in range(len(chains))]
            dk = [_dot_tn(ds[c], qm[c]) for c in range(len(chains))]
            dq = [_dot(ds[c], kk[g]) for c, (g, h) in enumerate(chains)]
            for g, (rows, _) in enumerate(blocks):
                add_rows(dqa_ref, rows, jnp.where(head0, dq[2 * g], dq[2 * g + 1]))
                add_rows(dka_ref, keys[g], dk[2 * g] + dk[2 * g + 1])
                add_rows(dva_ref, keys[g], dv[2 * g] + dv[2 * g + 1])

        _for_each_group(group)

        @pl.when(pl.program_id(0) == 0)
        def _():
            def tables(i, carry):
                rows = pl.ds(pl.multiple_of(i * 256, 256), 256)
                c, sa, sb = _rot_tables(pos_ref[rows, :], invf_ref[...], ma_ref[...], mb_ref[...])
                rot_ref[0, rows, :] = c
                rot_ref[1, rows, :] = sa
                rot_ref[2, rows, :] = sb
                return carry

            lax.fori_loop(0, SEQ // 256, tables, 0)

        def finish(i, carry):
            rows = pl.ds(pl.multiple_of(i * 256, 256), 256)
            c, sa, sb = rot_ref[0, rows, :], rot_ref[1, rows, :], rot_ref[2, rows, :]
            dq_ref[rows, :] = _rot_t(dqa_ref[rows, :] * Q_SCALE, c, sa, sb).astype(BF16)
            dk_ref[rows, :] = _rot_t(dka_ref[rows, :], c, sa, sb).astype(BF16)
            dv_ref[rows, :] = dva_ref[rows, :].astype(BF16)
            return carry

        lax.fori_loop(0, SEQ // 256, finish, 0)

    slab = pl.BlockSpec((SEQ, 128), lambda i: (0, i))
    out = jax.ShapeDtypeStruct((SEQ, ATTN_W), BF16)
    acc = pltpu.VMEM((SEQ, 128), F32)
    return _hosted(
        "attn_bwd", body, ATTN_W // 128,
        [slab] * 6 + [_full_spec((SEQ, 1)), _full_spec((1, 128)), _full_spec((1, 128)), _full_spec((1, 128))],
        [slab] * 3, [out, out, out], (q, k, v, o, lse, do, pos_col, *rot),
        scratch_shapes=[acc, acc, acc, acc, pltpu.VMEM((3, SEQ, 128), F32)], exchanges=exchanges)


def in_bwd(dproj, w_in_t, x, g1, dx2, exchanges=()):
    tm = 512

    def body(dp_ref, w_ref, x_ref, g_ref, dx2_ref, dx_ref, dg_ref):
        dh1 = _dot(dp_ref[...], w_ref[...])
        dz, dg = _rms_bwd(x_ref[...], g_ref[...], dh1)
        dx_ref[...] = dx2_ref[...] + dz

        @pl.when(pl.program_id(0) == 0)
        def _():
            dg_ref[...] = jnp.zeros_like(dg_ref)

        dg_ref[...] += dg

    return _hosted(
        "in_bwd", body, SEQ // tm,
        [_row_spec(tm, IN_W), _full_spec((IN_W, D_MODEL)), _row_spec(tm, D_MODEL), _full_spec((1, D_MODEL)),
         _row_spec(tm, D_MODEL)],
        [_row_spec(tm, D_MODEL), _full_spec((1, D_MODEL))],
        [jax.ShapeDtypeStruct((SEQ, D_MODEL), F32), jax.ShapeDtypeStruct((1, D_MODEL), F32)],
        (dproj, w_in_t, x, g1, dx2), exchanges=exchanges)


def _coords():
    return lax.axis_index("x"), lax.axis_index("y"), lax.axis_index("c")


class Exchange:
    def __init__(self, srcs, bufs, new_shapes, n_sems, make, mid_step=None):
        self.srcs, self.bufs, self.new_shapes, self.n_sems, self.make = list(srcs), list(bufs), list(new_shapes), n_sems, make
        self.mid_step = mid_step


def _hosted(name, body, n_steps, in_specs, out_specs, out_shape, args, scratch_shapes=(), exchanges=(), prefetch=None):
    out_shape, out_specs = list(out_shape), list(out_specs)
    srcs = [a for ex in exchanges for a in ex.srcs]
    bufs = [a for ex in exchanges for a in ex.bufs]
    news = [s for ex in exchanges for s in ex.new_shapes]
    n_pre = 0 if prefetch is None else 1
    n_in, n_out, n_scr = len(args), len(out_shape), len(scratch_shapes)

    def wrapped(*refs):
        refs = list(refs)
        pre, refs = refs[:n_pre], refs[n_pre:]
        ins = refs[:n_in]
        src_refs = refs[n_in:n_in + len(srcs)]
        at = n_in + len(srcs) + len(bufs)
        outs = refs[at:at + n_out]
        buf_refs = refs[at + n_out:at + n_out + len(bufs)]
        new_refs = refs[at + n_out + len(bufs):at + n_out + len(bufs) + len(news)]
        at += n_out + len(bufs) + len(news)
        scratch, sems = refs[at:at + n_scr], refs[at + n_scr:]

        def copies(phase, which):
            made, si, bi, ni = [], 0, 0, 0
            for k, ex in enumerate(exchanges):
                if which(ex):
                    made.append(ex.make(phase, src_refs[si:si + len(ex.srcs)], buf_refs[bi:bi + len(ex.bufs)],
                                        new_refs[ni:ni + len(ex.new_shapes)], sems[2 * k], sems[2 * k + 1]))
                si, bi, ni = si + len(ex.srcs), bi + len(ex.bufs), ni + len(ex.new_shapes)
            return made

        if exchanges:
            @pl.when(pl.program_id(0) == 0)
            def _():
                for starts in copies("start", lambda ex: True):
                    for cp in starts:
                        cp.start()

        def pass_on(mid):
            for arrivals, starts in copies("middle", lambda ex: ex.mid_step == mid):
                for cp in arrivals:
                    cp.wait_recv()
                for cp in starts:
                    cp.start()

        for mid in sorted({ex.mid_step for ex in exchanges if isinstance(ex.mid_step, int)}):
            @pl.when(pl.program_id(0) == mid)
            def _(mid=mid):
                pass_on(mid)

        body(*pre, *ins, *outs, *scratch)

        if exchanges:
            @pl.when(pl.program_id(0) == n_steps - 1)
            def _():
                pass_on("end")
                for sends, recvs in copies("end", lambda ex: True):
                    for cp in sends:
                        cp.wait_send()
                    for cp in recvs:
                        cp.wait_recv()

    sem_shapes = []
    for ex in exchanges:
        sem_shapes += [pltpu.SemaphoreType.DMA((ex.n_sems,)), pltpu.SemaphoreType.DMA((ex.n_sems,))]
    all_in = list(in_specs) + [ANY] * (len(srcs) + len(bufs))
    all_out = out_specs + [ANY] * (len(bufs) + len(news))
    all_shape = out_shape + [jax.ShapeDtypeStruct(b.shape, b.dtype) for b in bufs] + news
    all_scratch = list(scratch_shapes) + sem_shapes
    aliases = {n_pre + n_in + len(srcs) + i: n_out + i for i in range(len(bufs))}
    if prefetch is None:
        call = pl.pallas_call(wrapped, name=name, grid=(n_steps,), in_specs=all_in, out_specs=all_out, out_shape=all_shape,
                              scratch_shapes=all_scratch, input_output_aliases=aliases, compiler_params=_params())
        outs = call(*args, *srcs, *bufs)
    else:
        spec = pltpu.PrefetchScalarGridSpec(num_scalar_prefetch=1, grid=(n_steps,), in_specs=all_in, out_specs=all_out,
                                            scratch_shapes=all_scratch)
        call = pl.pallas_call(wrapped, name=name, grid_spec=spec, out_shape=all_shape, input_output_aliases=aliases,
                              compiler_params=_params())
        outs = call(prefetch, *args, *srcs, *bufs)
    results, bi, ni = [], n_out, n_out + len(bufs)
    for ex in exchanges:
        results.append((list(outs[bi:bi + len(ex.bufs)]), list(outs[ni:ni + len(ex.new_shapes)])))
        bi, ni = bi + len(ex.bufs), ni + len(ex.new_shapes)
    return list(outs[:n_out]), results


def _gather_copies(kinds, bufs, ranges, send_sems, recv_sems):
    x, y, c = _coords()
    me, sibling = (x, y, c), (x, y, 1 - c)
    chips = [(1 - x, y), (x, 1 - y), (1 - x, 1 - y)]

    def copy(a, k, block, to):
        lo, hi = ranges[a]
        r = bufs[a].shape[0] // N_DEV
        rows = bufs[a].at[pl.ds((4 * block[0] + 2 * block[1] + block[2]) * r + lo, hi - lo), :]
        return pltpu.make_async_remote_copy(src_ref=rows, dst_ref=rows, send_sem=send_sems.at[7 * a + k],
                                            recv_sem=recv_sems.at[7 * a + k], device_id=to, device_id_type=MESH)

    every = range(len(bufs))
    make = {
        "out": lambda: [copy(a, 0, me, sibling) for a in every]
        + [copy(a, 1 + j, me, (*chip, c)) for a in every for j, chip in enumerate(chips)],
        "from_core": lambda: [copy(a, 0, sibling, me) for a in every],
        "from_chips": lambda: [copy(a, 1 + j, (*chip, c), me) for a in every for j, chip in enumerate(chips)],
        "on": lambda: [copy(a, 4 + j, (*chip, c), sibling) for a in every for j, chip in enumerate(chips)],
        "on_in": lambda: [copy(a, 4 + j, (*chip, 1 - c), me) for a in every for j, chip in enumerate(chips)],
    }
    return [make[kind]() for kind in kinds]


def gather(bufs, mid_step, ranges=None):
    ranges = ranges or [(0, b.shape[0] // N_DEV) for b in bufs]

    def make(phase, src_refs, buf_refs, new_refs, send_sems, recv_sems):
        kinds = {"start": ["out"], "middle": ["from_chips", "on"], "end": ["out", "on", "from_core", "on_in"]}[phase]
        made = _gather_copies(kinds, buf_refs, ranges, send_sems, recv_sems)
        if phase == "start":
            return made[0]
        if phase == "middle":
            return made[0], made[1]
        return made[0] + made[1], made[2] + made[3]

    return Exchange([], bufs, [], 7 * len(bufs), make, mid_step=mid_step)


def all_gather_in_place(name, bufs):
    n = len(bufs)
    ranges = [(0, b.shape[0] // N_DEV) for b in bufs]

    def body(*refs):
        outs, send_sems, recv_sems = refs[n:2 * n], refs[2 * n], refs[2 * n + 1]
        out, from_core, from_chips, on, on_in = _gather_copies(
            ["out", "from_core", "from_chips", "on", "on_in"], outs, ranges, send_sems, recv_sems)
        for cp in out:
            cp.start()
        for cp in from_chips:
            cp.wait_recv()
        for cp in on:
            cp.start()
        for cp in from_core + on_in:
            cp.wait_recv()
        for cp in out + on:
            cp.wait_send()

    return pl.pallas_call(
        body, name=name, in_specs=[ANY] * n, out_specs=[ANY] * n,
        out_shape=[jax.ShapeDtypeStruct(b.shape, b.dtype) for b in bufs],
        scratch_shapes=[pltpu.SemaphoreType.DMA((7 * n,)), pltpu.SemaphoreType.DMA((7 * n,))],
        input_output_aliases={i: i for i in range(n)},
    )(*bufs)


GATHER_PEERS_ID = 1


def all_gather_by_sequencer(name, bufs):
    n = len(bufs)
    ranges = [(0, b.shape[0] // N_DEV) for b in bufs]
    refs = [jax.new_ref(b, memory_space=pltpu.MemorySpace.HBM) for b in bufs]

    @pl.kernel(mesh=plsc.ScalarSubcoreMesh(axis_name="sequencer", num_cores=1), name=name,
               scratch_types=(pltpu.SemaphoreType.DMA((7 * n,)), pltpu.SemaphoreType.DMA((7 * n,))),
               compiler_params=pltpu.CompilerParams(collective_id=GATHER_PEERS_ID))
    def launch(send_sems, recv_sems):
        x, y, c = _coords()
        barrier = pltpu.get_barrier_semaphore()
        for peer in [(x, y, 1 - c), (1 - x, y, c), (x, 1 - y, c), (1 - x, 1 - y, c)]:
            pl.semaphore_signal(barrier, inc=1, device_id=peer, device_id_type=MESH)
        pl.semaphore_wait(barrier, 4)
        out, from_core, from_chips, on, on_in = _gather_copies(
            ["out", "from_core", "from_chips", "on", "on_in"], refs, ranges, send_sems, recv_sems)
        for cp in out:
            cp.start()
        for cp in from_chips:
            cp.wait_recv()
        for cp in on:
            cp.start()
        for cp in from_core + on_in:
            cp.wait_recv()
        for cp in out + on:
            cp.wait_send()

    launch()
    return [ref[...] for ref in refs]


def place_shards(shards, dev):
    n = len(shards)

    def body(dev_ref, *refs):
        for a in range(n):
            refs[n + a][...] = refs[a][...].astype(BF16)

    spec = pltpu.PrefetchScalarGridSpec(
        num_scalar_prefetch=1, grid=(1,),
        in_specs=[pl.BlockSpec(s.shape, lambda i, dev_ref: (0, 0)) for s in shards],
        out_specs=[pl.BlockSpec(s.shape, lambda i, dev_ref: (dev_ref[0], 0)) for s in shards])
    return pl.pallas_call(
        body, name="place_shards", grid_spec=spec,
        out_shape=[jax.ShapeDtypeStruct((N_DEV * s.shape[0], s.shape[1]), BF16) for s in shards],
        compiler_params=_params(),
    )(dev, *shards)


def _swap(copies_of):
    def make(phase, src_refs, buf_refs, new_refs, send_sems, recv_sems):
        copies = copies_of(src_refs, new_refs, send_sems, recv_sems)
        return copies if phase == "start" else (copies, copies)

    return make


def to_sibling(grads):
    def copies_of(src_refs, new_refs, send_sems, recv_sems):
        x, y, c = _coords()
        return [pltpu.make_async_remote_copy(
            src_ref=src_refs[a].at[2 * xy + 1 - c], dst_ref=new_refs[a].at[xy], send_sem=send_sems.at[4 * a + xy],
            recv_sem=recv_sems.at[4 * a + xy], device_id=(x, y, 1 - c), device_id_type=MESH)
            for a in range(len(src_refs)) for xy in range(4)]

    return Exchange(grads, [], [jax.ShapeDtypeStruct((4,) + g.shape[1:], g.dtype) for g in grads], 4 * len(grads),
                    _swap(copies_of))


def to_chips(parts):
    def copies_of(src_refs, new_refs, send_sems, recv_sems):
        x, y, c = _coords()
        chips = [(1 - x, y), (x, 1 - y), (1 - x, 1 - y)]
        return [pltpu.make_async_remote_copy(
            src_ref=src_refs[a].at[2 * px + py], dst_ref=new_refs[a].at[2 * x + y], send_sem=send_sems.at[3 * a + j],
            recv_sem=recv_sems.at[3 * a + j], device_id=(px, py, c), device_id_type=MESH)
            for a in range(len(src_refs)) for j, (px, py) in enumerate(chips)]

    return Exchange(parts, [], [jax.ShapeDtypeStruct(p.shape, p.dtype) for p in parts], 3 * len(parts), _swap(copies_of))


def to_owners(grad):
    def copies_of(src_refs, new_refs, send_sems, recv_sems):
        x, y, c = _coords()
        copies = []
        for m in range(1, N_DEV):
            px, py, pc = x ^ (m >> 2), y ^ ((m >> 1) & 1), c ^ (m & 1)
            copies.append(pltpu.make_async_remote_copy(
                src_ref=src_refs[0].at[4 * px + 2 * py + pc], dst_ref=new_refs[0].at[4 * x + 2 * y + c],
                send_sem=send_sems.at[m - 1], recv_sem=recv_sems.at[m - 1], device_id=(px, py, pc), device_id_type=MESH))
        return copies

    return Exchange([grad], [], [jax.ShapeDtypeStruct(grad.shape, grad.dtype)], N_DEV - 1, _swap(copies_of))


def exchange_only(name, exchanges):
    def body():
        pass

    return _hosted(name, body, 1, [], [], [], [], exchanges=exchanges)[1]


def sum_cores(name, grad, other, core):
    _, r, w = other.shape

    def body(core_ref, g_ref, o_ref, out_ref):
        out_ref[...] = (g_ref[...].astype(F32) + o_ref[...].astype(F32)).astype(out_ref.dtype)

    return pl.pallas_call(
        body, name=name,
        grid_spec=pltpu.PrefetchScalarGridSpec(
            num_scalar_prefetch=1, grid=(4,),
            in_specs=[pl.BlockSpec((1, r, w), lambda i, core_ref: (2 * i + core_ref[0], 0, 0)),
                      pl.BlockSpec((1, r, w), lambda i, core_ref: (i, 0, 0))],
            out_specs=pl.BlockSpec((1, r, w), lambda i, core_ref: (i, 0, 0))),
        out_shape=jax.ShapeDtypeStruct(other.shape, other.dtype),
        compiler_params=_params(),
    )(core, grad, other)


def sum_owned(name, grad, others, dev_ids):
    _, r, w = grad.shape

    def body(ids_ref, *refs):
        acc = refs[0][0]
        for k in range(1, N_DEV):
            acc = acc + refs[k][0]
        refs[N_DEV][...] = acc

    def pick(k):
        return pl.BlockSpec((1, r, w), lambda i, ids_ref: (ids_ref[k], 0, 0))

    return pl.pallas_call(
        body, name=name,
        grid_spec=pltpu.PrefetchScalarGridSpec(
            num_scalar_prefetch=1, grid=(1,), in_specs=[pick(k) for k in range(N_DEV)],
            out_specs=pl.BlockSpec((r, w), lambda i, ids_ref: (ids_ref[0], 0))),
        out_shape=jax.ShapeDtypeStruct((N_DEV * r, w), F32),
        compiler_params=_params(),
    )(dev_ids, grad, *([others] * (N_DEV - 1)))


def _adamw_update(w, g, m, v):
    nm = ADAM_B1 * m + np.float32(1.0 - ADAM_B1) * g
    nv = ADAM_B2 * v + np.float32(1.0 - ADAM_B2) * (g * g)
    m_hat = nm / np.float32(1.0 - ADAM_B1 ** ADAM_STEP)
    v_hat = nv / np.float32(1.0 - ADAM_B2 ** ADAM_STEP)
    return -ADAM_LR * (m_hat / (jnp.sqrt(v_hat) + ADAM_EPS) + ADAM_WD * w), nm, nv


def adamw_of_sums(name, part, others, chip_ids, w, m, v):
    _, r, wd = part.shape

    def body(ids_ref, p_ref, a_ref, b_ref, c_ref, w_ref, m_ref, v_ref, g_ref, d_ref, nm_ref, nv_ref):
        g = ((p_ref[0].astype(F32) + a_ref[0].astype(F32)) + b_ref[0].astype(F32)) + c_ref[0].astype(F32)
        g_ref[...] = g
        d_ref[...], nm_ref[...], nv_ref[...] = _adamw_update(w_ref[...], g, m_ref[...], v_ref[...])

    def pick(k):
        return pl.BlockSpec((1, r, wd), lambda i, ids_ref: (ids_ref[k], 0, 0))

    whole = pl.BlockSpec((r, wd), lambda i, ids_ref: (0, 0))
    shape = jax.ShapeDtypeStruct((r, wd), F32)
    return pl.pallas_call(
        body, name=name,
        grid_spec=pltpu.PrefetchScalarGridSpec(
            num_scalar_prefetch=1, grid=(1,), in_specs=[pick(0), pick(1), pick(2), pick(3), whole, whole, whole],
            out_specs=[whole] * 4),
        out_shape=[shape] * 4,
        compiler_params=_params(),
    )(chip_ids, part, others, others, others, w, m, v)


def adamw(name, w, g, m, v):
    def body(w_ref, g_ref, m_ref, v_ref, d_ref, nm_ref, nv_ref):
        d_ref[...], nm_ref[...], nv_ref[...] = _adamw_update(w_ref[...], g_ref[...], m_ref[...], v_ref[...])

    shape = jax.ShapeDtypeStruct(w.shape, F32)
    spec = _full_spec(w.shape)
    return pl.pallas_call(
        body, name=name, grid=(1,), in_specs=[spec] * 4, out_specs=[spec] * 3, out_shape=[shape] * 3,
        compiler_params=_params(),
    )(w, g, m, v)


def _pack_small(parts):
    flat = jnp.concatenate([parts[name].reshape(-1) for name, _ in SMALL])
    return jnp.pad(flat, (0, SMALL_ROWS * 128 - flat.shape[0])).reshape(SMALL_ROWS, 128)


def _unpack_small(packed, like):
    flat = packed.reshape(-1)
    out, at = {}, 0
    for name, size in SMALL:
        out[name] = flat[at:at + size].reshape(like[name].shape)
        at += size
    return out


def kernel(x, positions, pre_mix_norm, w_in, sgu_ln_gain, sgu_ln_bias, sgu_w_spatial, sgu_b_spatial, attn_out_norm, sgu_out_norm, w_out, post_mix_norm, pre_ffn_norm, w_gate, w_up, w_down, post_ffn_norm, loss_target, m_pre_mix_norm, m_w_in, m_sgu_ln_gain, m_sgu_ln_bias, m_sgu_w_spatial, m_sgu_b_spatial, m_attn_out_norm, m_sgu_out_norm, m_w_out, m_post_mix_norm, m_pre_ffn_norm, m_w_gate, m_w_up, m_w_down, m_post_ffn_norm, v_pre_mix_norm, v_w_in, v_sgu_ln_gain, v_sgu_ln_bias, v_sgu_w_spatial, v_sgu_b_spatial, v_attn_out_norm, v_sgu_out_norm, v_w_out, v_post_mix_norm, v_pre_ffn_norm, v_w_gate, v_w_up, v_w_down, v_post_ffn_norm):
    small_w = dict(pre_mix_norm=pre_mix_norm, sgu_ln_gain=sgu_ln_gain, sgu_ln_bias=sgu_ln_bias, sgu_w_spatial=sgu_w_spatial,
                   sgu_b_spatial=sgu_b_spatial, attn_out_norm=attn_out_norm, sgu_out_norm=sgu_out_norm,
                   post_mix_norm=post_mix_norm, pre_ffn_norm=pre_ffn_norm, post_ffn_norm=post_ffn_norm)
    small_m = dict(pre_mix_norm=m_pre_mix_norm, sgu_ln_gain=m_sgu_ln_gain, sgu_ln_bias=m_sgu_ln_bias, sgu_w_spatial=m_sgu_w_spatial,
                   sgu_b_spatial=m_sgu_b_spatial, attn_out_norm=m_attn_out_norm, sgu_out_norm=m_sgu_out_norm,
                   post_mix_norm=m_post_mix_norm, pre_ffn_norm=m_pre_ffn_norm, post_ffn_norm=m_post_ffn_norm)
    small_v = dict(pre_mix_norm=v_pre_mix_norm, sgu_ln_gain=v_sgu_ln_gain, sgu_ln_bias=v_sgu_ln_bias, sgu_w_spatial=v_sgu_w_spatial,
                   sgu_b_spatial=v_sgu_b_spatial, attn_out_norm=v_attn_out_norm, sgu_out_norm=v_sgu_out_norm,
                   post_mix_norm=v_post_mix_norm, pre_ffn_norm=v_pre_ffn_norm, post_ffn_norm=v_post_ffn_norm)
    for table in (small_w, small_m, small_v):
        table["loss_sum"] = jnp.zeros((1,), F32)

    x2d = x[0]
    target = loss_target[0]
    pos_col = positions.reshape(SEQ, 1)
    rot = _rot_consts()
    w_sp = sgu_w_spatial[0]
    bfull = jnp.repeat(sgu_b_spatial[0].T, HEAD_DIM, axis=1)

    x_i, y_i, c_i = (lax.axis_index(a).astype(jnp.int32) for a in MESH_AXES)
    dev = 4 * x_i + 2 * y_i + c_i
    core = c_i.reshape(1)
    chip = 2 * x_i + y_i
    chip_ids = jnp.stack([chip, chip ^ 1, chip ^ 2, chip ^ 3])
    dev_ids = jnp.stack([dev ^ m for m in range(N_DEV)])

    w_in_t, w_gate_t, w_up_t, w_out_f, w_down_f = place_shards(
        [w_in[0].T, w_gate[0].T, w_up[0].T, w_out[0], w_down[0]], dev.reshape(1))
    (w_in_t,) = all_gather_by_sequencer("gather_w_in", [w_in_t])
    (w_out_f,) = all_gather_by_sequencer("gather_w_out", [w_out_f])
    w_gate_t, w_up_t = all_gather_by_sequencer("gather_w_gate_up", [w_gate_t, w_up_t])
    (w_down_f,) = all_gather_by_sequencer("gather_w_down", [w_down_f])

    (h1, q, k, v, u, vs), _ = in_proj(x2d, pos_col, pre_mix_norm, w_in_t, rot)
    q, k, v = (_to_residue_order(t) for t in (q, k, v))
    (attn_r, lse), _ = attn_fwd(q, k, v)
    attn = _from_residue_order(attn_r)
    (sgu,), _ = sgu_fwd(u, vs, sgu_ln_gain, sgu_ln_bias, w_sp, bfull)
    (mix, y, x2, h2), _ = out_proj(attn, sgu, x2d, attn_out_norm, sgu_out_norm, w_out_f, post_mix_norm, pre_ffn_norm)
    (gate, up, act), _ = ffn_up(h2, w_gate_t, w_up_t)
    df, dx3, d_post_ffn, sq_err = ffn_down_loss(act, w_down_f, x2, post_ffn_norm, target)

    g_w_down, _ = weight_grad("grad_w_down", act, df)
    (dgate, dup), [(_, [s_down])] = ffn_act_bwd(df, w_down_f, gate, up, exchanges=[to_sibling([g_w_down])])
    p_down = sum_cores("sum_cores_down", g_w_down, s_down, core)
    g_w_gate, _ = weight_grad("grad_w_gate", dgate, h2)
    g_w_up, [(_, [s_gate])] = weight_grad("grad_w_up", dup, h2, exchanges=[to_sibling([g_w_gate])])
    p_gate = sum_cores("sum_cores_gate", g_w_gate, s_gate, core)
    (dx2, dy, d_pre_ffn, d_post_mix), [(_, [s_up]), (_, [c_down])] = ffn_in_bwd(
        dgate, dup, w_gate_t, w_up_t, x2, pre_ffn_norm, dx3, y, post_mix_norm,
        exchanges=[to_sibling([g_w_up]), to_chips([p_down])])
    p_up = sum_cores("sum_cores_up", g_w_up, s_up, core)
    g_w_out, _ = weight_grad("grad_w_out", mix, dy)
    (dattn, dsgu, d_attn_out, d_sgu_out), [(_, [s_out])] = mix_bwd(
        dy, w_out_f, attn, sgu, attn_out_norm, sgu_out_norm, exchanges=[to_sibling([g_w_out])])
    p_out = sum_cores("sum_cores_out", g_w_out, s_out, core)
    (du, dvs, d_ln_gain, d_ln_bias, d_w_sp, d_bfull), [(_, [c_out])] = sgu_bwd(
        u, vs, dsgu, sgu_ln_gain, sgu_ln_bias, w_sp, bfull, exchanges=[to_chips([p_out])])
    (dq, dk, dv), [(_, [c_gate, c_up])] = attn_bwd(
        q, k, v, attn_r, lse, _to_residue_order(dattn), _to_residue_order(pos_col), rot,
        exchanges=[to_chips([p_gate, p_up])])
    dq, dk, dv = (_from_residue_order(t) for t in (dq, dk, dv))
    dproj = jnp.concatenate([dq, dk, dv, du, dvs], axis=1)
    g_w_in, _ = weight_grad("grad_w_in", dproj, h1)
    (grad_x, d_pre_mix), [(_, [s_in])] = in_bwd(
        dproj, w_in_t, x2d, pre_mix_norm, dx2, exchanges=[to_sibling([g_w_in])])
    p_in = sum_cores("sum_cores_in", g_w_in, s_in, core)

    d_b_sp = d_bfull.reshape(CHUNK, N_GROUPS, HEAD_DIM).sum(axis=-1).T
    small_g = _pack_small(dict(pre_mix_norm=d_pre_mix, sgu_ln_gain=d_ln_gain, sgu_ln_bias=d_ln_bias, sgu_w_spatial=d_w_sp,
                               sgu_b_spatial=d_b_sp, attn_out_norm=d_attn_out, sgu_out_norm=d_sgu_out,
                               post_mix_norm=d_post_mix, pre_ffn_norm=d_pre_ffn, post_ffn_norm=d_post_ffn,
                               loss_sum=sq_err))
    small_g = small_g.reshape(N_DEV, SMALL_ROWS // N_DEV, 128)
    [(_, [c_in]), (_, [o_small])] = exchange_only("grads_tail", [to_chips([p_in]), to_owners(small_g)])
    all_small = sum_owned("sum_small", small_g, o_small, dev_ids)
    (all_small,) = all_gather_in_place("gather_small_grads", [all_small])

    big = {}
    for name, w, p, c, m, vv in (("w_in", w_in, p_in, c_in, m_w_in, v_w_in), ("w_gate", w_gate, p_gate, c_gate, m_w_gate, v_w_gate),
                                 ("w_up", w_up, p_up, c_up, m_w_up, v_w_up)):
        big[name] = tuple(t.T[None] for t in adamw_of_sums("adamw_" + name, p, c, chip_ids, w[0].T, m[0].T, vv[0].T))
    for name, w, p, c, m, vv in (("w_out", w_out, p_out, c_out, m_w_out, v_w_out),
                                 ("w_down", w_down, p_down, c_down, m_w_down, v_w_down)):
        big[name] = tuple(t[None] for t in adamw_of_sums("adamw_" + name, p, c, chip_ids, w[0], m[0], vv[0]))
    sd, snm, snv = adamw("adamw_small", _pack_small(small_w), all_small, _pack_small(small_m), _pack_small(small_v))
    sg, sd, snm, snv = (_unpack_small(t, small_w) for t in (all_small, sd, snm, snv))
    loss = sg["loss_sum"][0] * np.float32(0.5 / D_MODEL)

    names = ["pre_mix_norm", "w_in", "sgu_ln_gain", "sgu_ln_bias", "sgu_w_spatial", "sgu_b_spatial", "attn_out_norm",
             "sgu_out_norm", "w_out", "post_mix_norm", "pre_ffn_norm", "w_gate", "w_up", "w_down", "post_ffn_norm"]
    outs = [loss, grad_x[None]]
    for i, table in enumerate((sg, sd, snm, snv)):
        for name in names:
            outs.append(big[name][i] if name in big else table[name])
    return tuple(outs)
```

```python
import functools

import numpy as np
import jax
import jax.numpy as jnp
from jax import lax
from jax.experimental import pallas as pl
from jax.experimental.pallas import tpu as pltpu
from jax.experimental.pallas import tpu_sc as plsc

F32 = jnp.float32
BF16 = jnp.bfloat16

SEQ = 2048
D_MODEL = 1024
ATTN_W = 512
SGU_W = 512
HEAD_DIM = 64
N_GROUPS = 8
CHUNK = 128
D_FF = 2816
IN_W = 3 * ATTN_W + 2 * SGU_W
DILATIONS = (1, 4, 16)
ROPE_THETA = 500000.0
ROT_DIM = 16
ROT_HALF = 8
RMS_EPS = 1e-6
LN_EPS = 1e-5
Q_SCALE = 0.125
NEG = -1e30

N_DEV = 8
MESH_AXES = ("x", "y", "c")
MESH = pl.DeviceIdType.MESH

ADAM_LR = 0.001
ADAM_B1 = 0.9
ADAM_B2 = 0.999
ADAM_EPS = 1e-08
ADAM_WD = 0.01
ADAM_STEP = 10

VMEM_LIMIT = 60 * 1024 * 1024
ANY = pl.BlockSpec(memory_space=pl.ANY)

SMALL = (("pre_mix_norm", 1024), ("sgu_ln_gain", 512), ("sgu_ln_bias", 512), ("sgu_w_spatial", 8 * 128 * 128),
         ("sgu_b_spatial", 1024), ("attn_out_norm", 512), ("sgu_out_norm", 512), ("post_mix_norm", 1024),
         ("pre_ffn_norm", 1024), ("post_ffn_norm", 1024), ("loss_sum", 1))
SMALL_ROWS = 1152


def _params(sem=("arbitrary",)):
    return pltpu.CompilerParams(dimension_semantics=sem, vmem_limit_bytes=VMEM_LIMIT)


def _dot(a, b):
    return jnp.dot(a, b, preferred_element_type=F32)


def _dot_nt(a, b):
    return lax.dot_general(a, b, (((1,), (1,)), ((), ())), preferred_element_type=F32)


def _dot_tn(a, b):
    return lax.dot_general(a, b, (((0,), (0,)), ((), ())), preferred_element_type=F32)


def _rms(z):
    return lax.rsqrt(jnp.mean(z * z, axis=-1, keepdims=True) + RMS_EPS)


def _rms_bwd(z, gain, d):
    r = _rms(z)
    n = z * r
    dn = d * gain
    dz = r * (dn - n * jnp.mean(dn * n, axis=-1, keepdims=True))
    return dz, jnp.sum(d * n, axis=0, keepdims=True)


def _gelu(z):
    return 0.5 * z * (1.0 + lax.erf(z * np.float32(1.0 / np.sqrt(2.0))))


def _gelu_grad(z):
    cdf = 0.5 * (1.0 + lax.erf(z * np.float32(1.0 / np.sqrt(2.0))))
    return cdf + z * jnp.exp(-0.5 * z * z) * np.float32(1.0 / np.sqrt(2.0 * np.pi))


def _rot_tables(pos_col, invf, ma, mb):
    ang = pos_col.astype(F32) * invf
    s = jnp.sin(ang)
    return jnp.cos(ang), s * ma, s * mb


def _rot(t, c, sa, sb):
    return t * c + pltpu.roll(t, 120, 1) * sa + pltpu.roll(t, 8, 1) * sb


def _rot_t(d, c, sa, sb):
    return d * c + pltpu.roll(d * sa, 8, 1) + pltpu.roll(d * sb, 120, 1)


def _rot_consts():
    lane = np.arange(128) % HEAD_DIM
    inv_freq = (np.float32(ROPE_THETA) ** (-np.arange(0, ROT_DIM, 2, dtype=np.float32) / np.float32(ROT_DIM))).astype(np.float32)
    invf = np.where(lane < ROT_DIM, inv_freq[lane % ROT_HALF], 0.0).astype(np.float32)
    ma = np.where(lane < ROT_HALF, -1.0, 0.0).astype(np.float32)
    mb = np.where((lane >= ROT_HALF) & (lane < ROT_DIM), 1.0, 0.0).astype(np.float32)
    return jnp.asarray(invf[None]), jnp.asarray(ma[None]), jnp.asarray(mb[None])


def _row_spec(tm, w):
    return pl.BlockSpec((tm, w), lambda i: (i, 0))


def _full_spec(shape):
    return pl.BlockSpec(shape, lambda i: (0,) * len(shape))


def in_proj(x, pos_col, g1, w_in_t, rot, exchanges=()):
    tm = 512

    def body(x_ref, pos_ref, g_ref, w_ref, invf_ref, ma_ref, mb_ref, h_ref, q_ref, k_ref, v_ref, u_ref, vs_ref):
        xf = x_ref[...]
        h = (xf * _rms(xf) * g_ref[...]).astype(BF16)
        h_ref[...] = h
        proj = _dot_nt(h, w_ref[...])
        c, sa, sb = _rot_tables(pos_ref[...], invf_ref[...], ma_ref[...], mb_ref[...])
        for j in range(ATTN_W // 128):
            q_ref[:, j * 128:(j + 1) * 128] = _rot(proj[:, j * 128:(j + 1) * 128], c, sa, sb) * Q_SCALE
            k_ref[:, j * 128:(j + 1) * 128] = _rot(proj[:, ATTN_W + j * 128:ATTN_W + (j + 1) * 128], c, sa, sb)
        v_ref[...] = proj[:, 2 * ATTN_W:3 * ATTN_W]
        u_ref[...] = proj[:, 3 * ATTN_W:3 * ATTN_W + SGU_W]
        vs_ref[...] = proj[:, 3 * ATTN_W + SGU_W:]

    act = jax.ShapeDtypeStruct((SEQ, 512), F32)
    return _hosted(
        "in_proj", body, SEQ // tm,
        [_row_spec(tm, D_MODEL), _row_spec(tm, 1), _full_spec((1, D_MODEL)), _full_spec((IN_W, D_MODEL)),
         _full_spec((1, 128)), _full_spec((1, 128)), _full_spec((1, 128))],
        [_row_spec(tm, D_MODEL)] + [_row_spec(tm, 512)] * 5,
        [jax.ShapeDtypeStruct((SEQ, D_MODEL), BF16)] + [act] * 5,
        (x, pos_col, g1, w_in_t, *rot), exchanges=exchanges)


RES = 16


def _to_residue_order(t):
    return t.reshape(SEQ // RES, RES, -1).transpose(1, 0, 2).reshape(t.shape)


def _from_residue_order(t):
    return t.reshape(RES, SEQ // RES, -1).transpose(1, 0, 2).reshape(t.shape)


def _block_rows(d, r, n):
    if d == 16:
        slices = [(128 * r, 128)]
    elif d == 4:
        slices = [(128 * (4 * b + r) + 32 * n, 32) for b in range(4)]
    else:
        slices = [(128 * b + 8 * n, 8) for b in range(RES)]
    return [(s if isinstance(s, int) else pl.multiple_of(s, z), z) for s, z in slices]


def _block_step(d, i):
    if d == 16:
        return i
    if d == 4:
        return 4 * (i & 31) + (i >> 5)
    return 16 * (i & 7) + (i >> 3)


def _attn_masks(d):
    row2 = _block_step(d, lax.broadcasted_iota(jnp.int32, (128, 256), 0))
    col2 = lax.broadcasted_iota(jnp.int32, (128, 256), 1)
    key2 = _block_step(d, col2 & 127)
    mask2 = jnp.logical_or(jnp.logical_and(col2 < 128, key2 >= row2), jnp.logical_and(col2 >= 128, key2 <= row2))
    row1 = _block_step(d, lax.broadcasted_iota(jnp.int32, (128, 128), 0))
    col1 = lax.broadcasted_iota(jnp.int32, (128, 128), 1)
    return col1 < HEAD_DIM, _block_step(d, col1) <= row1, mask2


def _load_rows(ref, slices):
    parts = [ref[pl.ds(s, z), :] for s, z in slices]
    return parts[0] if len(parts) == 1 else jnp.concatenate(parts, axis=0)


def _for_each_group(fn):
    for p, d in enumerate(DILATIONS):
        masks = _attn_masks(d)
        if d == 16:
            def group(i, carry, p=p, masks=masks):
                fn(p, masks, [(_block_rows(16, 4 * i + g, 0), None) for g in range(4)])
                return carry

            lax.fori_loop(0, 4, group, 0)
        elif d == 4:
            fn(p, masks, [(_block_rows(4, r, 0), None) for r in range(4)])

            def group(i, carry, p=p, masks=masks):
                fn(p, masks, [(_block_rows(4, r, i + 1), _block_rows(4, r, i)) for r in range(4)])
                return carry

            lax.fori_loop(0, 3, group, 0)
        else:
            fn(p, masks, [(_block_rows(1, 0, 0), None)])

            def group(i, carry, p=p, masks=masks):
                fn(p, masks, [(_block_rows(1, 0, 3 * i + g + 1), _block_rows(1, 0, 3 * i + g)) for g in range(3)])
                return carry

            lax.fori_loop(0, 5, group, 0)


def attn_fwd(q, k, v, exchanges=()):
    def body(q_ref, k_ref, v_ref, o_ref, lse_ref, op_ref, lp_ref):
        def group(p, masks, blocks):
            head0, mask1, mask2 = masks
            heads = (head0, jnp.logical_not(head0))
            keys = [rows if prev is None else prev + rows for rows, prev in blocks]
            mask = [mask1 if prev is None else mask2 for _, prev in blocks]
            qb = [_load_rows(q_ref, rows) for rows, _ in blocks]
            kk = [_load_rows(k_ref, ks).astype(BF16) for ks in keys]
            vv = [_load_rows(v_ref, ks).astype(BF16) for ks in keys]
            chains = [(g, hm) for g in range(len(blocks)) for hm in heads]
            s = [jnp.where(mask[g], _dot_nt(jnp.where(hm, qb[g], 0.0).astype(BF16), kk[g]), NEG) for g, hm in chains]
            m = [jnp.max(t, axis=-1, keepdims=True) for t in s]
            e = [jnp.exp(t - mt) for t, mt in zip(s, m)]
            l = [jnp.sum(t, axis=-1, keepdims=True) for t in e]
            pv = [_dot(t.astype(BF16), vv[g]) for t, (g, _) in zip(e, chains)]
            for g, (rows, _) in enumerate(blocks):
                o_blk = jnp.where(head0, pv[2 * g] / l[2 * g], pv[2 * g + 1] / l[2 * g + 1])
                l_blk = jnp.where(head0, jnp.broadcast_to(m[2 * g] + jnp.log(l[2 * g]), (128, 128)),
                                  jnp.broadcast_to(m[2 * g + 1] + jnp.log(l[2 * g + 1]), (128, 128)))
                at = 0
                for start, size in rows:
                    op_ref[p, pl.ds(start, size), :] = o_blk[at:at + size]
                    lp_ref[p, pl.ds(start, size), :] = l_blk[at:at + size]
                    at += size

        _for_each_group(group)

        def combine(i, carry):
            rows = pl.ds(pl.multiple_of(i * 256, 256), 256)
            ls = [lp_ref[p, rows, :] for p in range(3)]
            m = jnp.maximum(jnp.maximum(ls[0], ls[1]), ls[2])
            lse = m + jnp.log(jnp.exp(ls[0] - m) + jnp.exp(ls[1] - m) + jnp.exp(ls[2] - m))
            o = jnp.zeros((256, 128), F32)
            for p in range(3):
                o = o + jnp.exp(ls[p] - lse) * op_ref[p, rows, :]
            o_ref[rows, :] = o
            lse_ref[rows, :] = lse
            return carry

        lax.fori_loop(0, SEQ // 256, combine, 0)

    slab = pl.BlockSpec((SEQ, 128), lambda i: (0, i))
    out = jax.ShapeDtypeStruct((SEQ, ATTN_W), F32)
    return _hosted(
        "attn_fwd", body, ATTN_W // 128, [slab] * 3, [slab] * 2, [out, out], (q, k, v),
        scratch_shapes=[pltpu.VMEM((3, SEQ, 128), F32), pltpu.VMEM((3, SEQ, 128), F32)], exchanges=exchanges)


def _causal_weights(w_ref):
    row = lax.broadcasted_iota(jnp.int32, (CHUNK, CHUNK), 0)
    col = lax.broadcasted_iota(jnp.int32, (CHUNK, CHUNK), 1)
    return [jnp.where(col <= row, w_ref[g], 0.0).astype(BF16) for g in range(N_GROUPS)], col <= row


def _sgu_chunk_fwd(u, vs, lg, lb, wc, bfull, head0):
    ug = _gelu(u)
    vg = _gelu(vs)
    xc = vg - jnp.mean(vg, axis=-1, keepdims=True)
    rstd = lax.rsqrt(jnp.mean(xc * xc, axis=-1, keepdims=True) + LN_EPS)
    xhat = xc * rstd
    vn = xhat * lg + lb
    mixed = []
    for gp in range(SGU_W // 128):
        vp = vn[:, gp * 128:(gp + 1) * 128].astype(BF16)
        mixed.append(jnp.where(head0, _dot(wc[2 * gp], vp), _dot(wc[2 * gp + 1], vp)))
    ms = jnp.concatenate(mixed, axis=1) + bfull
    return ug, xhat, rstd, vn, ms


def sgu_fwd(u, vs, lg, lb, w_sp, bfull, exchanges=()):
    cpb = 4

    def body(u_ref, vs_ref, lg_ref, lb_ref, w_ref, b_ref, o_ref):
        wc, _ = _causal_weights(w_ref)
        head0 = lax.broadcasted_iota(jnp.int32, (CHUNK, 128), 1) < HEAD_DIM
        for ci in range(cpb):
            rows = pl.ds(ci * CHUNK, CHUNK)
            ug, _, _, _, ms = _sgu_chunk_fwd(u_ref[rows, :], vs_ref[rows, :], lg_ref[...], lb_ref[...], wc, b_ref[...], head0)
            o_ref[rows, :] = ug * ms

    tm = cpb * CHUNK
    return _hosted(
        "sgu_fwd", body, SEQ // tm,
        [_row_spec(tm, SGU_W), _row_spec(tm, SGU_W), _full_spec((1, SGU_W)), _full_spec((1, SGU_W)),
         _full_spec((N_GROUPS, CHUNK, CHUNK)), _full_spec((CHUNK, SGU_W))],
        [_row_spec(tm, SGU_W)], [jax.ShapeDtypeStruct((SEQ, SGU_W), F32)],
        (u, vs, lg, lb, w_sp, bfull), exchanges=exchanges)


def out_proj(attn, sgu, x, ga, gs, w_out, gpm, gpf, exchanges=()):
    tm = 512

    def body(a_ref, s_ref, x_ref, ga_ref, gs_ref, w_ref, gpm_ref, gpf_ref, mix_ref, y_ref, x2_ref, h2_ref):
        a = a_ref[...]
        s = s_ref[...]
        an = (a * _rms(a) * ga_ref[...]).astype(BF16)
        sn = (s * _rms(s) * gs_ref[...]).astype(BF16)
        mix_ref[:, :ATTN_W] = an
        mix_ref[:, ATTN_W:] = sn
        y = _dot(an, w_ref[:ATTN_W, :]) + _dot(sn, w_ref[ATTN_W:, :])
        y_ref[...] = y
        x2 = x_ref[...] + y * _rms(y) * gpm_ref[...]
        x2_ref[...] = x2
        h2_ref[...] = (x2 * _rms(x2) * gpf_ref[...]).astype(BF16)

    wide = jax.ShapeDtypeStruct((SEQ, D_MODEL), F32)
    wide16 = jax.ShapeDtypeStruct((SEQ, D_MODEL), BF16)
    return _hosted(
        "out_proj", body, SEQ // tm,
        [_row_spec(tm, ATTN_W), _row_spec(tm, SGU_W), _row_spec(tm, D_MODEL), _full_spec((1, ATTN_W)),
         _full_spec((1, SGU_W)), _full_spec((D_MODEL, D_MODEL)), _full_spec((1, D_MODEL)), _full_spec((1, D_MODEL))],
        [_row_spec(tm, D_MODEL)] * 4, [wide16, wide, wide, wide16],
        (attn, sgu, x, ga, gs, w_out, gpm, gpf), exchanges=exchanges)


def ffn_up(h2, w_gate_t, w_up_t, exchanges=()):
    tm = 256

    def body(h_ref, wg_ref, wu_ref, g_ref, u_ref, a_ref):
        h = h_ref[...]
        g = _dot_nt(h, wg_ref[...])
        u = _dot_nt(h, wu_ref[...])
        g_ref[...] = g.astype(BF16)
        u_ref[...] = u.astype(BF16)
        a_ref[...] = (g * jax.nn.sigmoid(g) * u).astype(BF16)

    ff = jax.ShapeDtypeStruct((SEQ, D_FF), BF16)
    return _hosted(
        "ffn_up", body, SEQ // tm,
        [_row_spec(tm, D_MODEL), _full_spec((D_FF, D_MODEL)), _full_spec((D_FF, D_MODEL))],
        [_row_spec(tm, D_FF)] * 3, [ff, ff, jax.ShapeDtypeStruct((SEQ, D_FF), BF16)],
        (h2, w_gate_t, w_up_t), exchanges=exchanges)


def ffn_down_loss(act, w_down, x2, gpo, target):
    tm = 512

    def body(a_ref, w_ref, x2_ref, g_ref, t_ref, df_ref, dx3_ref, dg_ref, loss_ref):
        f = _dot(a_ref[...], w_ref[...])
        gain = g_ref[...]
        err = x2_ref[...] + f * _rms(f) * gain - t_ref[...]
        dx3 = err * np.float32(1.0 / D_MODEL)
        dx3_ref[...] = dx3
        df, dg = _rms_bwd(f, gain, dx3)
        df_ref[...] = df.astype(BF16)

        @pl.when(pl.program_id(0) == 0)
        def _():
            dg_ref[...] = jnp.zeros_like(dg_ref)
            loss_ref[...] = jnp.zeros_like(loss_ref)

        dg_ref[...] += dg
        loss_ref[...] += jnp.sum(err * err, axis=(0, 1), keepdims=True)

    return pl.pallas_call(
        body, name="ffn_down_loss", grid=(SEQ // tm,),
        in_specs=[_row_spec(tm, D_FF), _full_spec((D_FF, D_MODEL)), _row_spec(tm, D_MODEL), _full_spec((1, D_MODEL)),
                  _row_spec(tm, D_MODEL)],
        out_specs=[_row_spec(tm, D_MODEL), _row_spec(tm, D_MODEL), _full_spec((1, D_MODEL)), _full_spec((1, 1))],
        out_shape=[jax.ShapeDtypeStruct((SEQ, D_MODEL), BF16), jax.ShapeDtypeStruct((SEQ, D_MODEL), F32),
                   jax.ShapeDtypeStruct((1, D_MODEL), F32), jax.ShapeDtypeStruct((1, 1), F32)],
        compiler_params=_params(),
    )(act, w_down, x2, gpo, target)


def ffn_act_bwd(df, w_down, gate, up, exchanges=()):
    tm = 256

    def body(df_ref, w_ref, g_ref, u_ref, dg_ref, du_ref):
        dact = _dot_nt(df_ref[...], w_ref[...])
        g = g_ref[...].astype(F32)
        s = jax.nn.sigmoid(g)
        du_ref[...] = (dact * g * s).astype(BF16)
        dg_ref[...] = (dact * u_ref[...].astype(F32) * (s * (1.0 + g * (1.0 - s)))).astype(BF16)

    ff16 = jax.ShapeDtypeStruct((SEQ, D_FF), BF16)
    return _hosted(
        "ffn_act_bwd", body, SEQ // tm,
        [_row_spec(tm, D_MODEL), _full_spec((D_FF, D_MODEL)), _row_spec(tm, D_FF), _row_spec(tm, D_FF)],
        [_row_spec(tm, D_FF)] * 2, [ff16, ff16], (df, w_down, gate, up), exchanges=exchanges)


def ffn_in_bwd(dgate, dup, w_gate_t, w_up_t, x2, gpf, dx3, y, gpm, exchanges=()):
    tm = 256

    def body(dg_ref, du_ref, wg_ref, wu_ref, x2_ref, gpf_ref, dx3_ref, y_ref, gpm_ref, dx2_ref, dy_ref, dgpf_ref, dgpm_ref):
        dh2 = _dot(dg_ref[...], wg_ref[...]) + _dot(du_ref[...], wu_ref[...])
        dz, dgpf = _rms_bwd(x2_ref[...], gpf_ref[...], dh2)
        dx2 = dx3_ref[...] + dz
        dx2_ref[...] = dx2
        dy, dgpm = _rms_bwd(y_ref[...], gpm_ref[...], dx2)
        dy_ref[...] = dy.astype(BF16)

        @pl.when(pl.program_id(0) == 0)
        def _():
            dgpf_ref[...] = jnp.zeros_like(dgpf_ref)
            dgpm_ref[...] = jnp.zeros_like(dgpm_ref)

        dgpf_ref[...] += dgpf
        dgpm_ref[...] += dgpm

    vec = jax.ShapeDtypeStruct((1, D_MODEL), F32)
    return _hosted(
        "ffn_in_bwd", body, SEQ // tm,
        [_row_spec(tm, D_FF), _row_spec(tm, D_FF), _full_spec((D_FF, D_MODEL)), _full_spec((D_FF, D_MODEL)),
         _row_spec(tm, D_MODEL), _full_spec((1, D_MODEL)), _row_spec(tm, D_MODEL), _row_spec(tm, D_MODEL),
         _full_spec((1, D_MODEL))],
        [_row_spec(tm, D_MODEL), _row_spec(tm, D_MODEL), _full_spec((1, D_MODEL)), _full_spec((1, D_MODEL))],
        [jax.ShapeDtypeStruct((SEQ, D_MODEL), F32), jax.ShapeDtypeStruct((SEQ, D_MODEL), BF16), vec, vec],
        (dgate, dup, w_gate_t, w_up_t, x2, gpf, dx3, y, gpm), exchanges=exchanges)


def weight_grad(name, a, b, exchanges=()):
    m, n = a.shape[1], b.shape[1]
    tr = 256

    def body(a_ref, b_ref, o_ref):
        o_ref[...] = _dot_tn(a_ref[...], b_ref[...]).astype(BF16)

    (out,), done = _hosted(
        name, body, m // tr, [pl.BlockSpec((SEQ, tr), lambda i: (0, i)), _full_spec((SEQ, n))],
        [_row_spec(tr, n)], [jax.ShapeDtypeStruct((m, n), BF16)], (a, b), exchanges=exchanges)
    return out.reshape(N_DEV, m // N_DEV, n), done


def mix_bwd(dy, w_out, attn, sgu, ga, gs, exchanges=()):
    tm = 512

    def body(dy_ref, w_ref, a_ref, s_ref, ga_ref, gs_ref, da_ref, ds_ref, dga_ref, dgs_ref):
        dy = dy_ref[...]
        da, dga = _rms_bwd(a_ref[...], ga_ref[...], _dot_nt(dy, w_ref[:ATTN_W, :]))
        ds, dgs = _rms_bwd(s_ref[...], gs_ref[...], _dot_nt(dy, w_ref[ATTN_W:, :]))
        da_ref[...] = da
        ds_ref[...] = ds

        @pl.when(pl.program_id(0) == 0)
        def _():
            dga_ref[...] = jnp.zeros_like(dga_ref)
            dgs_ref[...] = jnp.zeros_like(dgs_ref)

        dga_ref[...] += dga
        dgs_ref[...] += dgs

    half = jax.ShapeDtypeStruct((SEQ, 512), F32)
    vec = jax.ShapeDtypeStruct((1, 512), F32)
    return _hosted(
        "mix_bwd", body, SEQ // tm,
        [_row_spec(tm, D_MODEL), _full_spec((D_MODEL, D_MODEL)), _row_spec(tm, 512), _row_spec(tm, 512),
         _full_spec((1, 512)), _full_spec((1, 512))],
        [_row_spec(tm, 512), _row_spec(tm, 512), _full_spec((1, 512)), _full_spec((1, 512))],
        [half, half, vec, vec], (dy, w_out, attn, sgu, ga, gs), exchanges=exchanges)


def sgu_bwd(u, vs, dsgu, lg, lb, w_sp, bfull, exchanges=()):
    cpb = 4

    def body(u_ref, vs_ref, d_ref, lg_ref, lb_ref, w_ref, b_ref, du_ref, dvs_ref, dlg_ref, dlb_ref, dw_ref, db_ref):
        wc, causal = _causal_weights(w_ref)
        head0 = lax.broadcasted_iota(jnp.int32, (CHUNK, 128), 1) < HEAD_DIM
        lg = lg_ref[...]

        @pl.when(pl.program_id(0) == 0)
        def _():
            dlg_ref[...] = jnp.zeros_like(dlg_ref)
            dlb_ref[...] = jnp.zeros_like(dlb_ref)
            dw_ref[...] = jnp.zeros_like(dw_ref)
            db_ref[...] = jnp.zeros_like(db_ref)

        for ci in range(cpb):
            rows = pl.ds(ci * CHUNK, CHUNK)
            u = u_ref[rows, :]
            vs = vs_ref[rows, :]
            d = d_ref[rows, :]
            ug, xhat, rstd, vn, ms = _sgu_chunk_fwd(u, vs, lg, lb_ref[...], wc, b_ref[...], head0)
            du_ref[rows, :] = (d * ms * _gelu_grad(u)).astype(BF16)
            dms = d * ug
            db_ref[...] += dms
            dvn = []
            for gp in range(SGU_W // 128):
                dmp = dms[:, gp * 128:(gp + 1) * 128]
                dm0 = jnp.where(head0, dmp, 0.0).astype(BF16)
                dm1 = jnp.where(head0, 0.0, dmp).astype(BF16)
                vp = vn[:, gp * 128:(gp + 1) * 128].astype(BF16)
                dw_ref[2 * gp] += _dot_nt(dm0, vp)
                dw_ref[2 * gp + 1] += _dot_nt(dm1, vp)
                dvn.append(_dot_tn(wc[2 * gp], dm0) + _dot_tn(wc[2 * gp + 1], dm1))
            dvn = jnp.concatenate(dvn, axis=1)
            dlg_ref[...] += jnp.sum(dvn * xhat, axis=0, keepdims=True)
            dlb_ref[...] += jnp.sum(dvn, axis=0, keepdims=True)
            dxh = dvn * lg
            dvg = rstd * (dxh - jnp.mean(dxh, axis=-1, keepdims=True) - xhat * jnp.mean(dxh * xhat, axis=-1, keepdims=True))
            dvs_ref[rows, :] = (dvg * _gelu_grad(vs)).astype(BF16)

        @pl.when(pl.program_id(0) == pl.num_programs(0) - 1)
        def _():
            for g in range(N_GROUPS):
                dw_ref[g] = jnp.where(causal, dw_ref[g], 0.0)

    tm = cpb * CHUNK
    half16 = jax.ShapeDtypeStruct((SEQ, SGU_W), BF16)
    vec = jax.ShapeDtypeStruct((1, SGU_W), F32)
    return _hosted(
        "sgu_bwd", body, SEQ // tm,
        [_row_spec(tm, SGU_W)] * 3 + [_full_spec((1, SGU_W)), _full_spec((1, SGU_W)),
                                      _full_spec((N_GROUPS, CHUNK, CHUNK)), _full_spec((CHUNK, SGU_W))],
        [_row_spec(tm, SGU_W), _row_spec(tm, SGU_W), _full_spec((1, SGU_W)), _full_spec((1, SGU_W)),
         _full_spec((N_GROUPS, CHUNK, CHUNK)), _full_spec((CHUNK, SGU_W))],
        [half16, half16, vec, vec, jax.ShapeDtypeStruct((N_GROUPS, CHUNK, CHUNK), F32),
         jax.ShapeDtypeStruct((CHUNK, SGU_W), F32)],
        (u, vs, dsgu, lg, lb, w_sp, bfull), exchanges=exchanges)


def attn_bwd(q, k, v, o, lse, do, pos_col, rot, exchanges=()):
    def body(q_ref, k_ref, v_ref, o_ref, lse_ref, do_ref, pos_ref, invf_ref, ma_ref, mb_ref,
             dq_ref, dk_ref, dv_ref, dqa_ref, dka_ref, dva_ref, dlt_ref, rot_ref):
        dqa_ref[...] = jnp.zeros_like(dqa_ref)
        dka_ref[...] = jnp.zeros_like(dka_ref)
        dva_ref[...] = jnp.zeros_like(dva_ref)

        def delta(i, carry):
            rows = pl.ds(pl.multiple_of(i * 256, 256), 256)
            prod = do_ref[rows, :] * o_ref[rows, :]
            h0 = lax.broadcasted_iota(jnp.int32, (256, 128), 1) < HEAD_DIM
            d0 = jnp.sum(jnp.where(h0, prod, 0.0), axis=-1, keepdims=True)
            d1 = jnp.sum(jnp.where(h0, 0.0, prod), axis=-1, keepdims=True)
            dlt_ref[rows, :] = jnp.where(h0, d0, d1)
            return carry

        lax.fori_loop(0, SEQ // 256, delta, 0)

        def add_rows(ref, slices, val):
            at = 0
            for start, size in slices:
                ref[pl.ds(start, size), :] += val[at:at + size]
                at += size

        def group(p, masks, blocks):
            head0, mask1, mask2 = masks
            heads = (head0, jnp.logical_not(head0))
            keys = [rows if prev is None else prev + rows for rows, prev in blocks]
            mask = [mask1 if prev is None else mask2 for _, prev in blocks]
            kk = [_load_rows(k_ref, ks).astype(BF16) for ks in keys]
            vv = [_load_rows(v_ref, ks).astype(BF16) for ks in keys]
            qb = [_load_rows(q_ref, rows) for rows, _ in blocks]
            dob = [_load_rows(do_ref, rows) for rows, _ in blocks]
            lse_b = [_load_rows(lse_ref, rows) for rows, _ in blocks]
            dlt_b = [_load_rows(dlt_ref, rows) for rows, _ in blocks]
            chains = [(g, h) for g in range(len(blocks)) for h in range(2)]
            qm = [jnp.where(heads[h], qb[g], 0.0).astype(BF16) for g, h in chains]
            dom = [jnp.where(heads[h], dob[g], 0.0).astype(BF16) for g, h in chains]
            s = [_dot_nt(qm[c], kk[g]) for c, (g, h) in enumerate(chains)]
            dp = [_dot_nt(dom[c], vv[g]) for c, (g, h) in enumerate(chains)]
            pr = [jnp.where(mask[g], jnp.exp(s[c] - lse_b[g][:, h * HEAD_DIM:h * HEAD_DIM + 1]), 0.0)
                  for c, (g, h) in enumerate(chains)]
            ds = [(pr[c] * (dp[c] - dlt_b[g][:, h * HEAD_DIM:h * HEAD_DIM + 1])).astype(BF16)
                  for c, (g, h) in enumerate(chains)]
            dv = [_dot_tn(pr[c].astype(BF16), dom[c]) for c in range(len(chains))]
            dk = [_dot_tn(ds[c], qm[c]) for c in range(len(chains))]
            dq = [_dot(ds[c], kk[g]) for c, (g, h) in enumerate(chains)]
            for g, (rows, _) in enumerate(blocks):
                add_rows(dqa_ref, rows, jnp.where(head0, dq[2 * g], dq[2 * g + 1]))
                add_rows(dka_ref, keys[g], dk[2 * g] + dk[2 * g + 1])
                add_rows(dva_ref, keys[g], dv[2 * g] + dv[2 * g + 1])

        _for_each_group(group)

        @pl.when(pl.program_id(0) == 0)
        def _():
            def tables(i, carry):
                rows = pl.ds(pl.multiple_of(i * 256, 256), 256)
                c, sa, sb = _rot_tables(pos_ref[rows, :], invf_ref[...], ma_ref[...], mb_ref[...])
                rot_ref[0, rows, :] = c
                rot_ref[1, rows, :] = sa
                rot_ref[2, rows, :] = sb
                return carry

            lax.fori_loop(0, SEQ // 256, tables, 0)

        def finish(i, carry):
            rows = pl.ds(pl.multiple_of(i * 256, 256), 256)
            c, sa, sb = rot_ref[0, rows, :], rot_ref[1, rows, :], rot_ref[2, rows, :]
            dq_ref[rows, :] = _rot_t(dqa_ref[rows, :] * Q_SCALE, c, sa, sb).astype(BF16)
            dk_ref[rows, :] = _rot_t(dka_ref[rows, :], c, sa, sb).astype(BF16)
            dv_ref[rows, :] = dva_ref[rows, :].astype(BF16)
            return carry

        lax.fori_loop(0, SEQ // 256, finish, 0)

    slab = pl.BlockSpec((SEQ, 128), lambda i: (0, i))
    out = jax.ShapeDtypeStruct((SEQ, ATTN_W), BF16)
    acc = pltpu.VMEM((SEQ, 128), F32)
    return _hosted(
        "attn_bwd", body, ATTN_W // 128,
        [slab] * 6 + [_full_spec((SEQ, 1)), _full_spec((1, 128)), _full_spec((1, 128)), _full_spec((1, 128))],
        [slab] * 3, [out, out, out], (q, k, v, o, lse, do, pos_col, *rot),
        scratch_shapes=[acc, acc, acc, acc, pltpu.VMEM((3, SEQ, 128), F32)], exchanges=exchanges)


def in_bwd(dproj, w_in_t, x, g1, dx2, exchanges=()):
    tm = 512

    def body(dp_ref, w_ref, x_ref, g_ref, dx2_ref, dx_ref, dg_ref):
        dh1 = _dot(dp_ref[...], w_ref[...])
        dz, dg = _rms_bwd(x_ref[...], g_ref[...], dh1)
        dx_ref[...] = dx2_ref[...] + dz

        @pl.when(pl.program_id(0) == 0)
        def _():
            dg_ref[...] = jnp.zeros_like(dg_ref)

        dg_ref[...] += dg

    return _hosted(
        "in_bwd", body, SEQ // tm,
        [_row_spec(tm, IN_W), _full_spec((IN_W, D_MODEL)), _row_spec(tm, D_MODEL), _full_spec((1, D_MODEL)),
         _row_spec(tm, D_MODEL)],
        [_row_spec(tm, D_MODEL), _full_spec((1, D_MODEL))],
        [jax.ShapeDtypeStruct((SEQ, D_MODEL), F32), jax.ShapeDtypeStruct((1, D_MODEL), F32)],
        (dproj, w_in_t, x, g1, dx2), exchanges=exchanges)


def _coords():
    return lax.axis_index("x"), lax.axis_index("y"), lax.axis_index("c")


class Exchange:
    def __init__(self, srcs, bufs, new_shapes, n_sems, make, mid_step=None):
        self.srcs, self.bufs, self.new_shapes, self.n_sems, self.make = list(srcs), list(bufs), list(new_shapes), n_sems, make
        self.mid_step = mid_step


def _hosted(name, body, n_steps, in_specs, out_specs, out_shape, args, scratch_shapes=(), exchanges=(), prefetch=None):
    out_shape, out_specs = list(out_shape), list(out_specs)
    srcs = [a for ex in exchanges for a in ex.srcs]
    bufs = [a for ex in exchanges for a in ex.bufs]
    news = [s for ex in exchanges for s in ex.new_shapes]
    n_pre = 0 if prefetch is None else 1
    n_in, n_out, n_scr = len(args), len(out_shape), len(scratch_shapes)

    def wrapped(*refs):
        refs = list(refs)
        pre, refs = refs[:n_pre], refs[n_pre:]
        ins = refs[:n_in]
        src_refs = refs[n_in:n_in + len(srcs)]
        at = n_in + len(srcs) + len(bufs)
        outs = refs[at:at + n_out]
        buf_refs = refs[at + n_out:at + n_out + len(bufs)]
        new_refs = refs[at + n_out + len(bufs):at + n_out + len(bufs) + len(news)]
        at += n_out + len(bufs) + len(news)
        scratch, sems = refs[at:at + n_scr], refs[at + n_scr:]

        def copies(phase, which):
            made, si, bi, ni = [], 0, 0, 0
            for k, ex in enumerate(exchanges):
                if which(ex):
                    made.append(ex.make(phase, src_refs[si:si + len(ex.srcs)], buf_refs[bi:bi + len(ex.bufs)],
                                        new_refs[ni:ni + len(ex.new_shapes)], sems[2 * k], sems[2 * k + 1]))
                si, bi, ni = si + len(ex.srcs), bi + len(ex.bufs), ni + len(ex.new_shapes)
            return made

        if exchanges:
            @pl.when(pl.program_id(0) == 0)
            def _():
                for starts in copies("start", lambda ex: True):
                    for cp in starts:
                        cp.start()

        def pass_on(mid):
            for arrivals, starts in copies("middle", lambda ex: ex.mid_step == mid):
                for cp in arrivals:
                    cp.wait_recv()
                for cp in starts:
                    cp.start()

        for mid in sorted({ex.mid_step for ex in exchanges if isinstance(ex.mid_step, int)}):
            @pl.when(pl.program_id(0) == mid)
            def _(mid=mid):
                pass_on(mid)

        body(*pre, *ins, *outs, *scratch)

        if exchanges:
            @pl.when(pl.program_id(0) == n_steps - 1)
            def _():
                pass_on("end")
                for sends, recvs in copies("end", lambda ex: True):
                    for cp in sends:
                        cp.wait_send()
                    for cp in recvs:
                        cp.wait_recv()

    sem_shapes = []
    for ex in exchanges:
        sem_shapes += [pltpu.SemaphoreType.DMA((ex.n_sems,)), pltpu.SemaphoreType.DMA((ex.n_sems,))]
    all_in = list(in_specs) + [ANY] * (len(srcs) + len(bufs))
    all_out = out_specs + [ANY] * (len(bufs) + len(news))
    all_shape = out_shape + [jax.ShapeDtypeStruct(b.shape, b.dtype) for b in bufs] + news
    all_scratch = list(scratch_shapes) + sem_shapes
    aliases = {n_pre + n_in + len(srcs) + i: n_out + i for i in range(len(bufs))}
    if prefetch is None:
        call = pl.pallas_call(wrapped, name=name, grid=(n_steps,), in_specs=all_in, out_specs=all_out, out_shape=all_shape,
                              scratch_shapes=all_scratch, input_output_aliases=aliases, compiler_params=_params())
        outs = call(*args, *srcs, *bufs)
    else:
        spec = pltpu.PrefetchScalarGridSpec(num_scalar_prefetch=1, grid=(n_steps,), in_specs=all_in, out_specs=all_out,
                                            scratch_shapes=all_scratch)
        call = pl.pallas_call(wrapped, name=name, grid_spec=spec, out_shape=all_shape, input_output_aliases=aliases,
                              compiler_params=_params())
        outs = call(prefetch, *args, *srcs, *bufs)
    results, bi, ni = [], n_out, n_out + len(bufs)
    for ex in exchanges:
        results.append((list(outs[bi:bi + len(ex.bufs)]), list(outs[ni:ni + len(ex.new_shapes)])))
        bi, ni = bi + len(ex.bufs), ni + len(ex.new_shapes)
    return list(outs[:n_out]), results


def _gather_copies(kinds, bufs, ranges, send_sems, recv_sems):
    x, y, c = _coords()
    me, sibling = (x, y, c), (x, y, 1 - c)
    chips = [(1 - x, y), (x, 1 - y), (1 - x, 1 - y)]

    def copy(a, k, block, to):
        lo, hi = ranges[a]
        r = bufs[a].shape[0] // N_DEV
        rows = bufs[a].at[pl.ds((4 * block[0] + 2 * block[1] + block[2]) * r + lo, hi - lo), :]
        return pltpu.make_async_remote_copy(src_ref=rows, dst_ref=rows, send_sem=send_sems.at[7 * a + k],
                                            recv_sem=recv_sems.at[7 * a + k], device_id=to, device_id_type=MESH)

    every = range(len(bufs))
    make = {
        "out": lambda: [copy(a, 0, me, sibling) for a in every]
        + [copy(a, 1 + j, me, (*chip, c)) for a in every for j, chip in enumerate(chips)],
        "from_core": lambda: [copy(a, 0, sibling, me) for a in every],
        "from_chips": lambda: [copy(a, 1 + j, (*chip, c), me) for a in every for j, chip in enumerate(chips)],
        "on": lambda: [copy(a, 4 + j, (*chip, c), sibling) for a in every for j, chip in enumerate(chips)],
        "on_in": lambda: [copy(a, 4 + j, (*chip, 1 - c), me) for a in every for j, chip in enumerate(chips)],
    }
    return [make[kind]() for kind in kinds]


def gather(bufs, mid_step, ranges=None):
    ranges = ranges or [(0, b.shape[0] // N_DEV) for b in bufs]

    def make(phase, src_refs, buf_refs, new_refs, send_sems, recv_sems):
        kinds = {"start": ["out"], "middle": ["from_chips", "on"], "end": ["out", "on", "from_core", "on_in"]}[phase]
        made = _gather_copies(kinds, buf_refs, ranges, send_sems, recv_sems)
        if phase == "start":
            return made[0]
        if phase == "middle":
            return made[0], made[1]
        return made[0] + made[1], made[2] + made[3]

    return Exchange([], bufs, [], 7 * len(bufs), make, mid_step=mid_step)


def all_gather_in_place(name, bufs):
    n = len(bufs)
    ranges = [(0, b.shape[0] // N_DEV) for b in bufs]

    def body(*refs):
        outs, send_sems, recv_sems = refs[n:2 * n], refs[2 * n], refs[2 * n + 1]
        out, from_core, from_chips, on, on_in = _gather_copies(
            ["out", "from_core", "from_chips", "on", "on_in"], outs, ranges, send_sems, recv_sems)
        for cp in out:
            cp.start()
        for cp in from_chips:
            cp.wait_recv()
        for cp in on:
            cp.start()
        for cp in from_core + on_in:
            cp.wait_recv()
        for cp in out + on:
            cp.wait_send()

    return pl.pallas_call(
        body, name=name, in_specs=[ANY] * n, out_specs=[ANY] * n,
        out_shape=[jax.ShapeDtypeStruct(b.shape, b.dtype) for b in bufs],
        scratch_shapes=[pltpu.SemaphoreType.DMA((7 * n,)), pltpu.SemaphoreType.DMA((7 * n,))],
        input_output_aliases={i: i for i in range(n)},
    )(*bufs)


TO_GATHER = (1, lambda x, y, c: [(x, y, 1 - c), (1 - x, y, c), (x, 1 - y, c), (1 - x, 1 - y, c)])
TO_SIBLING = (2, lambda x, y, c: [(x, y, 1 - c)])
TO_CHIPS = (3, lambda x, y, c: [(1 - x, y, c), (x, 1 - y, c), (1 - x, 1 - y, c)])
TO_ALL = (4, lambda x, y, c: [(x ^ (m >> 2), y ^ ((m >> 1) & 1), c ^ (m & 1)) for m in range(1, N_DEV)])


def by_sequencer(name, ex, who):
    collective_id, peers_of = who
    hbm = pltpu.MemorySpace.HBM
    src_refs = [jax.new_ref(a, memory_space=hbm) for a in ex.srcs]
    buf_refs = [jax.new_ref(a, memory_space=hbm) for a in ex.bufs]
    new_refs = [jax.empty_ref(s, memory_space=hbm) for s in ex.new_shapes]

    @pl.kernel(mesh=plsc.ScalarSubcoreMesh(axis_name="sequencer", num_cores=1), name=name,
               scratch_types=(pltpu.SemaphoreType.DMA((ex.n_sems,)), pltpu.SemaphoreType.DMA((ex.n_sems,))),
               compiler_params=pltpu.CompilerParams(collective_id=collective_id))
    def launch(send_sems, recv_sems):
        peers = peers_of(*_coords())
        barrier = pltpu.get_barrier_semaphore()
        for peer in peers:
            pl.semaphore_signal(barrier, inc=1, device_id=peer, device_id_type=MESH)
        pl.semaphore_wait(barrier, len(peers))
        refs = (src_refs, buf_refs, new_refs, send_sems, recv_sems)
        for cp in ex.make("start", *refs):
            cp.start()
        if ex.mid_step is not None:
            arrivals, starts = ex.make("middle", *refs)
            for cp in arrivals:
                cp.wait_recv()
            for cp in starts:
                cp.start()
        sends, arrivals = ex.make("end", *refs)
        for cp in arrivals:
            cp.wait_recv()
        for cp in sends:
            cp.wait_send()

    launch()
    return [ref[...] for ref in buf_refs], [ref[...] for ref in new_refs]


def place_shards(shards, dev):
    n = len(shards)

    def body(dev_ref, *refs):
        for a in range(n):
            refs[n + a][...] = refs[a][...].astype(BF16)

    spec = pltpu.PrefetchScalarGridSpec(
        num_scalar_prefetch=1, grid=(1,),
        in_specs=[pl.BlockSpec(s.shape, lambda i, dev_ref: (0, 0)) for s in shards],
        out_specs=[pl.BlockSpec(s.shape, lambda i, dev_ref: (dev_ref[0], 0)) for s in shards])
    return pl.pallas_call(
        body, name="place_shards", grid_spec=spec,
        out_shape=[jax.ShapeDtypeStruct((N_DEV * s.shape[0], s.shape[1]), BF16) for s in shards],
        compiler_params=_params(),
    )(dev, *shards)


def _swap(copies_of):
    def make(phase, src_refs, buf_refs, new_refs, send_sems, recv_sems):
        copies = copies_of(src_refs, new_refs, send_sems, recv_sems)
        return copies if phase == "start" else (copies, copies)

    return make


def to_sibling(grads):
    def copies_of(src_refs, new_refs, send_sems, recv_sems):
        x, y, c = _coords()
        return [pltpu.make_async_remote_copy(
            src_ref=src_refs[a].at[2 * xy + 1 - c], dst_ref=new_refs[a].at[xy], send_sem=send_sems.at[4 * a + xy],
            recv_sem=recv_sems.at[4 * a + xy], device_id=(x, y, 1 - c), device_id_type=MESH)
            for a in range(len(src_refs)) for xy in range(4)]

    return Exchange(grads, [], [jax.ShapeDtypeStruct((4,) + g.shape[1:], g.dtype) for g in grads], 4 * len(grads),
                    _swap(copies_of))


def to_chips(parts):
    def copies_of(src_refs, new_refs, send_sems, recv_sems):
        x, y, c = _coords()
        chips = [(1 - x, y), (x, 1 - y), (1 - x, 1 - y)]
        return [pltpu.make_async_remote_copy(
            src_ref=src_refs[a].at[2 * px + py], dst_ref=new_refs[a].at[2 * x + y], send_sem=send_sems.at[3 * a + j],
            recv_sem=recv_sems.at[3 * a + j], device_id=(px, py, c), device_id_type=MESH)
            for a in range(len(src_refs)) for j, (px, py) in enumerate(chips)]

    return Exchange(parts, [], [jax.ShapeDtypeStruct(p.shape, p.dtype) for p in parts], 3 * len(parts), _swap(copies_of))


def to_owners(grad):
    def copies_of(src_refs, new_refs, send_sems, recv_sems):
        x, y, c = _coords()
        copies = []
        for m in range(1, N_DEV):
            px, py, pc = x ^ (m >> 2), y ^ ((m >> 1) & 1), c ^ (m & 1)
            copies.append(pltpu.make_async_remote_copy(
                src_ref=src_refs[0].at[4 * px + 2 * py + pc], dst_ref=new_refs[0].at[4 * x + 2 * y + c],
                send_sem=send_sems.at[m - 1], recv_sem=recv_sems.at[m - 1], device_id=(px, py, pc), device_id_type=MESH))
        return copies

    return Exchange([grad], [], [jax.ShapeDtypeStruct(grad.shape, grad.dtype)], N_DEV - 1, _swap(copies_of))


def exchange_only(name, exchanges):
    def body():
        pass

    return _hosted(name, body, 1, [], [], [], [], exchanges=exchanges)[1]


def sum_cores(name, grad, other, core):
    _, r, w = other.shape

    def body(core_ref, g_ref, o_ref, out_ref):
        out_ref[...] = (g_ref[...].astype(F32) + o_ref[...].astype(F32)).astype(out_ref.dtype)

    return pl.pallas_call(
        body, name=name,
        grid_spec=pltpu.PrefetchScalarGridSpec(
            num_scalar_prefetch=1, grid=(4,),
            in_specs=[pl.BlockSpec((1, r, w), lambda i, core_ref: (2 * i + core_ref[0], 0, 0)),
                      pl.BlockSpec((1, r, w), lambda i, core_ref: (i, 0, 0))],
            out_specs=pl.BlockSpec((1, r, w), lambda i, core_ref: (i, 0, 0))),
        out_shape=jax.ShapeDtypeStruct(other.shape, other.dtype),
        compiler_params=_params(),
    )(core, grad, other)


def sum_owned(name, grad, others, dev_ids):
    _, r, w = grad.shape

    def body(ids_ref, *refs):
        acc = refs[0][0]
        for k in range(1, N_DEV):
            acc = acc + refs[k][0]
        refs[N_DEV][...] = acc

    def pick(k):
        return pl.BlockSpec((1, r, w), lambda i, ids_ref: (ids_ref[k], 0, 0))

    return pl.pallas_call(
        body, name=name,
        grid_spec=pltpu.PrefetchScalarGridSpec(
            num_scalar_prefetch=1, grid=(1,), in_specs=[pick(k) for k in range(N_DEV)],
            out_specs=pl.BlockSpec((r, w), lambda i, ids_ref: (ids_ref[0], 0))),
        out_shape=jax.ShapeDtypeStruct((N_DEV * r, w), F32),
        compiler_params=_params(),
    )(dev_ids, grad, *([others] * (N_DEV - 1)))


def _adamw_update(w, g, m, v):
    nm = ADAM_B1 * m + np.float32(1.0 - ADAM_B1) * g
    nv = ADAM_B2 * v + np.float32(1.0 - ADAM_B2) * (g * g)
    m_hat = nm / np.float32(1.0 - ADAM_B1 ** ADAM_STEP)
    v_hat = nv / np.float32(1.0 - ADAM_B2 ** ADAM_STEP)
    return -ADAM_LR * (m_hat / (jnp.sqrt(v_hat) + ADAM_EPS) + ADAM_WD * w), nm, nv


def adamw_of_sums(name, part, others, chip_ids, w, m, v):
    _, r, wd = part.shape

    def body(ids_ref, p_ref, a_ref, b_ref, c_ref, w_ref, m_ref, v_ref, g_ref, d_ref, nm_ref, nv_ref):
        g = ((p_ref[0].astype(F32) + a_ref[0].astype(F32)) + b_ref[0].astype(F32)) + c_ref[0].astype(F32)
        g_ref[...] = g
        d_ref[...], nm_ref[...], nv_ref[...] = _adamw_update(w_ref[...], g, m_ref[...], v_ref[...])

    def pick(k):
        return pl.BlockSpec((1, r, wd), lambda i, ids_ref: (ids_ref[k], 0, 0))

    whole = pl.BlockSpec((r, wd), lambda i, ids_ref: (0, 0))
    shape = jax.ShapeDtypeStruct((r, wd), F32)
    return pl.pallas_call(
        body, name=name,
        grid_spec=pltpu.PrefetchScalarGridSpec(
            num_scalar_prefetch=1, grid=(1,), in_specs=[pick(0), pick(1), pick(2), pick(3), whole, whole, whole],
            out_specs=[whole] * 4),
        out_shape=[shape] * 4,
        compiler_params=_params(),
    )(chip_ids, part, others, others, others, w, m, v)


def adamw(name, w, g, m, v):
    def body(w_ref, g_ref, m_ref, v_ref, d_ref, nm_ref, nv_ref):
        d_ref[...], nm_ref[...], nv_ref[...] = _adamw_update(w_ref[...], g_ref[...], m_ref[...], v_ref[...])

    shape = jax.ShapeDtypeStruct(w.shape, F32)
    spec = _full_spec(w.shape)
    return pl.pallas_call(
        body, name=name, grid=(1,), in_specs=[spec] * 4, out_specs=[spec] * 3, out_shape=[shape] * 3,
        compiler_params=_params(),
    )(w, g, m, v)


def _pack_small(parts):
    flat = jnp.concatenate([parts[name].reshape(-1) for name, _ in SMALL])
    return jnp.pad(flat, (0, SMALL_ROWS * 128 - flat.shape[0])).reshape(SMALL_ROWS, 128)


def _unpack_small(packed, like):
    flat = packed.reshape(-1)
    out, at = {}, 0
    for name, size in SMALL:
        out[name] = flat[at:at + size].reshape(like[name].shape)
        at += size
    return out


def kernel(x, positions, pre_mix_norm, w_in, sgu_ln_gain, sgu_ln_bias, sgu_w_spatial, sgu_b_spatial, attn_out_norm, sgu_out_norm, w_out, post_mix_norm, pre_ffn_norm, w_gate, w_up, w_down, post_ffn_norm, loss_target, m_pre_mix_norm, m_w_in, m_sgu_ln_gain, m_sgu_ln_bias, m_sgu_w_spatial, m_sgu_b_spatial, m_attn_out_norm, m_sgu_out_norm, m_w_out, m_post_mix_norm, m_pre_ffn_norm, m_w_gate, m_w_up, m_w_down, m_post_ffn_norm, v_pre_mix_norm, v_w_in, v_sgu_ln_gain, v_sgu_ln_bias, v_sgu_w_spatial, v_sgu_b_spatial, v_attn_out_norm, v_sgu_out_norm, v_w_out, v_post_mix_norm, v_pre_ffn_norm, v_w_gate, v_w_up, v_w_down, v_post_ffn_norm):
    small_w = dict(pre_mix_norm=pre_mix_norm, sgu_ln_gain=sgu_ln_gain, sgu_ln_bias=sgu_ln_bias, sgu_w_spatial=sgu_w_spatial,
                   sgu_b_spatial=sgu_b_spatial, attn_out_norm=attn_out_norm, sgu_out_norm=sgu_out_norm,
                   post_mix_norm=post_mix_norm, pre_ffn_norm=pre_ffn_norm, post_ffn_norm=post_ffn_norm)
    small_m = dict(pre_mix_norm=m_pre_mix_norm, sgu_ln_gain=m_sgu_ln_gain, sgu_ln_bias=m_sgu_ln_bias, sgu_w_spatial=m_sgu_w_spatial,
                   sgu_b_spatial=m_sgu_b_spatial, attn_out_norm=m_attn_out_norm, sgu_out_norm=m_sgu_out_norm,
                   post_mix_norm=m_post_mix_norm, pre_ffn_norm=m_pre_ffn_norm, post_ffn_norm=m_post_ffn_norm)
    small_v = dict(pre_mix_norm=v_pre_mix_norm, sgu_ln_gain=v_sgu_ln_gain, sgu_ln_bias=v_sgu_ln_bias, sgu_w_spatial=v_sgu_w_spatial,
                   sgu_b_spatial=v_sgu_b_spatial, attn_out_norm=v_attn_out_norm, sgu_out_norm=v_sgu_out_norm,
                   post_mix_norm=v_post_mix_norm, pre_ffn_norm=v_pre_ffn_norm, post_ffn_norm=v_post_ffn_norm)
    for table in (small_w, small_m, small_v):
        table["loss_sum"] = jnp.zeros((1,), F32)

    x2d = x[0]
    target = loss_target[0]
    pos_col = positions.reshape(SEQ, 1)
    rot = _rot_consts()
    w_sp = sgu_w_spatial[0]
    bfull = jnp.repeat(sgu_b_spatial[0].T, HEAD_DIM, axis=1)

    x_i, y_i, c_i = (lax.axis_index(a).astype(jnp.int32) for a in MESH_AXES)
    dev = 4 * x_i + 2 * y_i + c_i
    core = c_i.reshape(1)
    chip = 2 * x_i + y_i
    chip_ids = jnp.stack([chip, chip ^ 1, chip ^ 2, chip ^ 3])
    dev_ids = jnp.stack([dev ^ m for m in range(N_DEV)])

    w_in_t, w_gate_t, w_up_t, w_out_f, w_down_f = place_shards(
        [w_in[0].T, w_gate[0].T, w_up[0].T, w_out[0], w_down[0]], dev.reshape(1))
    def gathered(name, bufs):
        return by_sequencer(name, gather(bufs, "end"), TO_GATHER)[0]

    def from_sibling(name, grad):
        return by_sequencer(name, to_sibling([grad]), TO_SIBLING)[1][0]

    def from_chips(name, part):
        return by_sequencer(name, to_chips([part]), TO_CHIPS)[1][0]

    (w_in_t,) = gathered("gather_w_in", [w_in_t])
    (w_out_f,) = gathered("gather_w_out", [w_out_f])
    w_gate_t, w_up_t = gathered("gather_w_gate_up", [w_gate_t, w_up_t])
    (w_down_f,) = gathered("gather_w_down", [w_down_f])

    (h1, q, k, v, u, vs), _ = in_proj(x2d, pos_col, pre_mix_norm, w_in_t, rot)
    q, k, v = (_to_residue_order(t) for t in (q, k, v))
    (attn_r, lse), _ = attn_fwd(q, k, v)
    attn = _from_residue_order(attn_r)
    (sgu,), _ = sgu_fwd(u, vs, sgu_ln_gain, sgu_ln_bias, w_sp, bfull)
    (mix, y, x2, h2), _ = out_proj(attn, sgu, x2d, attn_out_norm, sgu_out_norm, w_out_f, post_mix_norm, pre_ffn_norm)
    (gate, up, act), _ = ffn_up(h2, w_gate_t, w_up_t)
    df, dx3, d_post_ffn, sq_err = ffn_down_loss(act, w_down_f, x2, post_ffn_norm, target)

    g_w_down, _ = weight_grad("grad_w_down", act, df)
    s_down = from_sibling("w_down_to_sibling", g_w_down)
    (dgate, dup), _ = ffn_act_bwd(df, w_down_f, gate, up)
    p_down = sum_cores("sum_cores_down", g_w_down, s_down, core)
    c_down = from_chips("w_down_to_chips", p_down)
    g_w_gate, _ = weight_grad("grad_w_gate", dgate, h2)
    s_gate = from_sibling("w_gate_to_sibling", g_w_gate)
    g_w_up, _ = weight_grad("grad_w_up", dup, h2)
    s_up = from_sibling("w_up_to_sibling", g_w_up)
    p_gate = sum_cores("sum_cores_gate", g_w_gate, s_gate, core)
    c_gate = from_chips("w_gate_to_chips", p_gate)
    (dx2, dy, d_pre_ffn, d_post_mix), _ = ffn_in_bwd(dgate, dup, w_gate_t, w_up_t, x2, pre_ffn_norm, dx3, y, post_mix_norm)
    p_up = sum_cores("sum_cores_up", g_w_up, s_up, core)
    c_up = from_chips("w_up_to_chips", p_up)
    g_w_out, _ = weight_grad("grad_w_out", mix, dy)
    s_out = from_sibling("w_out_to_sibling", g_w_out)
    (dattn, dsgu, d_attn_out, d_sgu_out), _ = mix_bwd(dy, w_out_f, attn, sgu, attn_out_norm, sgu_out_norm)
    p_out = sum_cores("sum_cores_out", g_w_out, s_out, core)
    c_out = from_chips("w_out_to_chips", p_out)
    (du, dvs, d_ln_gain, d_ln_bias, d_w_sp, d_bfull), _ = sgu_bwd(u, vs, dsgu, sgu_ln_gain, sgu_ln_bias, w_sp, bfull)
    (dq, dk, dv), _ = attn_bwd(q, k, v, attn_r, lse, _to_residue_order(dattn), _to_residue_order(pos_col), rot)
    dq, dk, dv = (_from_residue_order(t) for t in (dq, dk, dv))
    dproj = jnp.concatenate([dq, dk, dv, du, dvs], axis=1)
    g_w_in, _ = weight_grad("grad_w_in", dproj, h1)
    s_in = from_sibling("w_in_to_sibling", g_w_in)
    (grad_x, d_pre_mix), _ = in_bwd(dproj, w_in_t, x2d, pre_mix_norm, dx2)
    p_in = sum_cores("sum_cores_in", g_w_in, s_in, core)
    c_in = from_chips("w_in_to_chips", p_in)

    d_b_sp = d_bfull.reshape(CHUNK, N_GROUPS, HEAD_DIM).sum(axis=-1).T
    small_g = _pack_small(dict(pre_mix_norm=d_pre_mix, sgu_ln_gain=d_ln_gain, sgu_ln_bias=d_ln_bias, sgu_w_spatial=d_w_sp,
                               sgu_b_spatial=d_b_sp, attn_out_norm=d_attn_out, sgu_out_norm=d_sgu_out,
                               post_mix_norm=d_post_mix, pre_ffn_norm=d_pre_ffn, post_ffn_norm=d_post_ffn,
                               loss_sum=sq_err))
    small_g = small_g.reshape(N_DEV, SMALL_ROWS // N_DEV, 128)
    o_small = by_sequencer("small_to_owners", to_owners(small_g), TO_ALL)[1][0]
    all_small = sum_owned("sum_small", small_g, o_small, dev_ids)
    (all_small,) = gathered("gather_small_grads", [all_small])

    big = {}
    for name, w, p, c, m, vv in (("w_in", w_in, p_in, c_in, m_w_in, v_w_in), ("w_gate", w_gate, p_gate, c_gate, m_w_gate, v_w_gate),
                                 ("w_up", w_up, p_up, c_up, m_w_up, v_w_up)):
        big[name] = tuple(t.T[None] for t in adamw_of_sums("adamw_" + name, p, c, chip_ids, w[0].T, m[0].T, vv[0].T))
    for name, w, p, c, m, vv in (("w_out", w_out, p_out, c_out, m_w_out, v_w_out),
                                 ("w_down", w_down, p_down, c_down, m_w_down, v_w_down)):
        big[name] = tuple(t[None] for t in adamw_of_sums("adamw_" + name, p, c, chip_ids, w[0], m[0], vv[0]))
    sd, snm, snv = adamw("adamw_small", _pack_small(small_w), all_small, _pack_small(small_m), _pack_small(small_v))
    sg, sd, snm, snv = (_unpack_small(t, small_w) for t in (all_small, sd, snm, snv))
    loss = sg["loss_sum"][0] * np.float32(0.5 / D_MODEL)

    names = ["pre_mix_norm", "w_in", "sgu_ln_gain", "sgu_ln_bias", "sgu_w_spatial", "sgu_b_spatial", "attn_out_norm",
             "sgu_out_norm", "w_out", "post_mix_norm", "pre_ffn_norm", "w_gate", "w_up", "w_down", "post_ffn_norm"]
    outs = [loss, grad_x[None]]
    for i, table in enumerate((sg, sd, snm, snv)):
        for name in names:
            outs.append(big[name][i] if name in big else table[name])
    return tuple(outs)
```

```python
import functools

import numpy as np
import jax
import jax.numpy as jnp
from jax import lax
from jax.experimental import pallas as pl
from jax.experimental.pallas import tpu as pltpu
from jax.experimental.pallas import tpu_sc as plsc

F32 = jnp.float32
BF16 = jnp.bfloat16

SEQ = 2048
D_MODEL = 1024
ATTN_W = 512
SGU_W = 512
HEAD_DIM = 64
N_GROUPS = 8
CHUNK = 128
D_FF = 2816
IN_W = 3 * ATTN_W + 2 * SGU_W
DILATIONS = (1, 4, 16)
ROPE_THETA = 500000.0
ROT_DIM = 16
ROT_HALF = 8
RMS_EPS = 1e-6
LN_EPS = 1e-5
Q_SCALE = 0.125
NEG = -1e30

N_DEV = 8
MESH_AXES = ("x", "y", "c")
MESH = pl.DeviceIdType.MESH

ADAM_LR = 0.001
ADAM_B1 = 0.9
ADAM_B2 = 0.999
ADAM_EPS = 1e-08
ADAM_WD = 0.01
ADAM_STEP = 10

VMEM_LIMIT = 60 * 1024 * 1024
ANY = pl.BlockSpec(memory_space=pl.ANY)

SMALL = (("pre_mix_norm", 1024), ("sgu_ln_gain", 512), ("sgu_ln_bias", 512), ("sgu_w_spatial", 8 * 128 * 128),
         ("sgu_b_spatial", 1024), ("attn_out_norm", 512), ("sgu_out_norm", 512), ("post_mix_norm", 1024),
         ("pre_ffn_norm", 1024), ("post_ffn_norm", 1024), ("loss_sum", 1))
SMALL_ROWS = 1152


def _params(sem=("arbitrary",)):
    return pltpu.CompilerParams(dimension_semantics=sem, vmem_limit_bytes=VMEM_LIMIT)


def _dot(a, b):
    return jnp.dot(a, b, preferred_element_type=F32)


def _dot_nt(a, b):
    return lax.dot_general(a, b, (((1,), (1,)), ((), ())), preferred_element_type=F32)


def _dot_tn(a, b):
    return lax.dot_general(a, b, (((0,), (0,)), ((), ())), preferred_element_type=F32)


def _rms(z):
    return lax.rsqrt(jnp.mean(z * z, axis=-1, keepdims=True) + RMS_EPS)


def _rms_bwd(z, gain, d):
    r = _rms(z)
    n = z * r
    dn = d * gain
    dz = r * (dn - n * jnp.mean(dn * n, axis=-1, keepdims=True))
    return dz, jnp.sum(d * n, axis=0, keepdims=True)


def _gelu(z):
    return 0.5 * z * (1.0 + lax.erf(z * np.float32(1.0 / np.sqrt(2.0))))


def _gelu_grad(z):
    cdf = 0.5 * (1.0 + lax.erf(z * np.float32(1.0 / np.sqrt(2.0))))
    return cdf + z * jnp.exp(-0.5 * z * z) * np.float32(1.0 / np.sqrt(2.0 * np.pi))


def _rot_tables(pos_col, invf, ma, mb):
    ang = pos_col.astype(F32) * invf
    s = jnp.sin(ang)
    return jnp.cos(ang), s * ma, s * mb


def _rot(t, c, sa, sb):
    return t * c + pltpu.roll(t, 120, 1) * sa + pltpu.roll(t, 8, 1) * sb


def _rot_t(d, c, sa, sb):
    return d * c + pltpu.roll(d * sa, 8, 1) + pltpu.roll(d * sb, 120, 1)


def _rot_consts():
    lane = np.arange(128) % HEAD_DIM
    inv_freq = (np.float32(ROPE_THETA) ** (-np.arange(0, ROT_DIM, 2, dtype=np.float32) / np.float32(ROT_DIM))).astype(np.float32)
    invf = np.where(lane < ROT_DIM, inv_freq[lane % ROT_HALF], 0.0).astype(np.float32)
    ma = np.where(lane < ROT_HALF, -1.0, 0.0).astype(np.float32)
    mb = np.where((lane >= ROT_HALF) & (lane < ROT_DIM), 1.0, 0.0).astype(np.float32)
    return jnp.asarray(invf[None]), jnp.asarray(ma[None]), jnp.asarray(mb[None])


def _row_spec(tm, w):
    return pl.BlockSpec((tm, w), lambda i: (i, 0))


def _full_spec(shape):
    return pl.BlockSpec(shape, lambda i: (0,) * len(shape))


def in_proj(x, pos_col, g1, w_in_t, rot, exchanges=()):
    tm = 512

    def body(x_ref, pos_ref, g_ref, w_ref, invf_ref, ma_ref, mb_ref, h_ref, q_ref, k_ref, v_ref, u_ref, vs_ref):
        xf = x_ref[...]
        h = (xf * _rms(xf) * g_ref[...]).astype(BF16)
        h_ref[...] = h
        proj = _dot_nt(h, w_ref[...])
        c, sa, sb = _rot_tables(pos_ref[...], invf_ref[...], ma_ref[...], mb_ref[...])
        for j in range(ATTN_W // 128):
            q_ref[:, j * 128:(j + 1) * 128] = _rot(proj[:, j * 128:(j + 1) * 128], c, sa, sb) * Q_SCALE
            k_ref[:, j * 128:(j + 1) * 128] = _rot(proj[:, ATTN_W + j * 128:ATTN_W + (j + 1) * 128], c, sa, sb)
        v_ref[...] = proj[:, 2 * ATTN_W:3 * ATTN_W]
        u_ref[...] = proj[:, 3 * ATTN_W:3 * ATTN_W + SGU_W]
        vs_ref[...] = proj[:, 3 * ATTN_W + SGU_W:]

    act = jax.ShapeDtypeStruct((SEQ, 512), F32)
    return _hosted(
        "in_proj", body, SEQ // tm,
        [_row_spec(tm, D_MODEL), _row_spec(tm, 1), _full_spec((1, D_MODEL)), _full_spec((IN_W, D_MODEL)),
         _full_spec((1, 128)), _full_spec((1, 128)), _full_spec((1, 128))],
        [_row_spec(tm, D_MODEL)] + [_row_spec(tm, 512)] * 5,
        [jax.ShapeDtypeStruct((SEQ, D_MODEL), BF16)] + [act] * 5,
        (x, pos_col, g1, w_in_t, *rot), exchanges=exchanges)


RES = 16


def _to_residue_order(t):
    return t.reshape(SEQ // RES, RES, -1).transpose(1, 0, 2).reshape(t.shape)


def _from_residue_order(t):
    return t.reshape(RES, SEQ // RES, -1).transpose(1, 0, 2).reshape(t.shape)


def _block_rows(d, r, n):
    if d == 16:
        slices = [(128 * r, 128)]
    elif d == 4:
        slices = [(128 * (4 * b + r) + 32 * n, 32) for b in range(4)]
    else:
        slices = [(128 * b + 8 * n, 8) for b in range(RES)]
    return [(s if isinstance(s, int) else pl.multiple_of(s, z), z) for s, z in slices]


def _block_step(d, i):
    if d == 16:
        return i
    if d == 4:
        return 4 * (i & 31) + (i >> 5)
    return 16 * (i & 7) + (i >> 3)


def _attn_masks(d):
    row2 = _block_step(d, lax.broadcasted_iota(jnp.int32, (128, 256), 0))
    col2 = lax.broadcasted_iota(jnp.int32, (128, 256), 1)
    key2 = _block_step(d, col2 & 127)
    mask2 = jnp.logical_or(jnp.logical_and(col2 < 128, key2 >= row2), jnp.logical_and(col2 >= 128, key2 <= row2))
    row1 = _block_step(d, lax.broadcasted_iota(jnp.int32, (128, 128), 0))
    col1 = lax.broadcasted_iota(jnp.int32, (128, 128), 1)
    return col1 < HEAD_DIM, _block_step(d, col1) <= row1, mask2


def _load_rows(ref, slices):
    parts = [ref[pl.ds(s, z), :] for s, z in slices]
    return parts[0] if len(parts) == 1 else jnp.concatenate(parts, axis=0)


def _for_each_group(fn):
    for p, d in enumerate(DILATIONS):
        masks = _attn_masks(d)
        if d == 16:
            def group(i, carry, p=p, masks=masks):
                fn(p, masks, [(_block_rows(16, 4 * i + g, 0), None) for g in range(4)])
                return carry

            lax.fori_loop(0, 4, group, 0)
        elif d == 4:
            fn(p, masks, [(_block_rows(4, r, 0), None) for r in range(4)])

            def group(i, carry, p=p, masks=masks):
                fn(p, masks, [(_block_rows(4, r, i + 1), _block_rows(4, r, i)) for r in range(4)])
                return carry

            lax.fori_loop(0, 3, group, 0)
        else:
            fn(p, masks, [(_block_rows(1, 0, 0), None)])

            def group(i, carry, p=p, masks=masks):
                fn(p, masks, [(_block_rows(1, 0, 3 * i + g + 1), _block_rows(1, 0, 3 * i + g)) for g in range(3)])
                return carry

            lax.fori_loop(0, 5, group, 0)


def attn_fwd(q, k, v, exchanges=()):
    def body(q_ref, k_ref, v_ref, o_ref, lse_ref, op_ref, lp_ref):
        def group(p, masks, blocks):
            head0, mask1, mask2 = masks
            heads = (head0, jnp.logical_not(head0))
            keys = [rows if prev is None else prev + rows for rows, prev in blocks]
            mask = [mask1 if prev is None else mask2 for _, prev in blocks]
            qb = [_load_rows(q_ref, rows) for rows, _ in blocks]
            kk = [_load_rows(k_ref, ks).astype(BF16) for ks in keys]
            vv = [_load_rows(v_ref, ks).astype(BF16) for ks in keys]
            chains = [(g, hm) for g in range(len(blocks)) for hm in heads]
            s = [jnp.where(mask[g], _dot_nt(jnp.where(hm, qb[g], 0.0).astype(BF16), kk[g]), NEG) for g, hm in chains]
            m = [jnp.max(t, axis=-1, keepdims=True) for t in s]
            e = [jnp.exp(t - mt) for t, mt in zip(s, m)]
            l = [jnp.sum(t, axis=-1, keepdims=True) for t in e]
            pv = [_dot(t.astype(BF16), vv[g]) for t, (g, _) in zip(e, chains)]
            for g, (rows, _) in enumerate(blocks):
                o_blk = jnp.where(head0, pv[2 * g] / l[2 * g], pv[2 * g + 1] / l[2 * g + 1])
                l_blk = jnp.where(head0, jnp.broadcast_to(m[2 * g] + jnp.log(l[2 * g]), (128, 128)),
                                  jnp.broadcast_to(m[2 * g + 1] + jnp.log(l[2 * g + 1]), (128, 128)))
                at = 0
                for start, size in rows:
                    op_ref[p, pl.ds(start, size), :] = o_blk[at:at + size]
                    lp_ref[p, pl.ds(start, size), :] = l_blk[at:at + size]
                    at += size

        _for_each_group(group)

        def combine(i, carry):
            rows = pl.ds(pl.multiple_of(i * 256, 256), 256)
            ls = [lp_ref[p, rows, :] for p in range(3)]
            m = jnp.maximum(jnp.maximum(ls[0], ls[1]), ls[2])
            lse = m + jnp.log(jnp.exp(ls[0] - m) + jnp.exp(ls[1] - m) + jnp.exp(ls[2] - m))
            o = jnp.zeros((256, 128), F32)
            for p in range(3):
                o = o + jnp.exp(ls[p] - lse) * op_ref[p, rows, :]
            o_ref[rows, :] = o
            lse_ref[rows, :] = lse
            return carry

        lax.fori_loop(0, SEQ // 256, combine, 0)

    slab = pl.BlockSpec((SEQ, 128), lambda i: (0, i))
    out = jax.ShapeDtypeStruct((SEQ, ATTN_W), F32)
    return _hosted(
        "attn_fwd", body, ATTN_W // 128, [slab] * 3, [slab] * 2, [out, out], (q, k, v),
        scratch_shapes=[pltpu.VMEM((3, SEQ, 128), F32), pltpu.VMEM((3, SEQ, 128), F32)], exchanges=exchanges)


def _causal_weights(w_ref):
    row = lax.broadcasted_iota(jnp.int32, (CHUNK, CHUNK), 0)
    col = lax.broadcasted_iota(jnp.int32, (CHUNK, CHUNK), 1)
    return [jnp.where(col <= row, w_ref[g], 0.0).astype(BF16) for g in range(N_GROUPS)], col <= row


def _sgu_chunk_fwd(u, vs, lg, lb, wc, bfull, head0):
    ug = _gelu(u)
    vg = _gelu(vs)
    xc = vg - jnp.mean(vg, axis=-1, keepdims=True)
    rstd = lax.rsqrt(jnp.mean(xc * xc, axis=-1, keepdims=True) + LN_EPS)
    xhat = xc * rstd
    vn = xhat * lg + lb
    mixed = []
    for gp in range(SGU_W // 128):
        vp = vn[:, gp * 128:(gp + 1) * 128].astype(BF16)
        mixed.append(jnp.where(head0, _dot(wc[2 * gp], vp), _dot(wc[2 * gp + 1], vp)))
    ms = jnp.concatenate(mixed, axis=1) + bfull
    return ug, xhat, rstd, vn, ms


def sgu_fwd(u, vs, lg, lb, w_sp, bfull, exchanges=()):
    cpb = 4

    def body(u_ref, vs_ref, lg_ref, lb_ref, w_ref, b_ref, o_ref):
        wc, _ = _causal_weights(w_ref)
        head0 = lax.broadcasted_iota(jnp.int32, (CHUNK, 128), 1) < HEAD_DIM
        for ci in range(cpb):
            rows = pl.ds(ci * CHUNK, CHUNK)
            ug, _, _, _, ms = _sgu_chunk_fwd(u_ref[rows, :], vs_ref[rows, :], lg_ref[...], lb_ref[...], wc, b_ref[...], head0)
            o_ref[rows, :] = ug * ms

    tm = cpb * CHUNK
    return _hosted(
        "sgu_fwd", body, SEQ // tm,
        [_row_spec(tm, SGU_W), _row_spec(tm, SGU_W), _full_spec((1, SGU_W)), _full_spec((1, SGU_W)),
         _full_spec((N_GROUPS, CHUNK, CHUNK)), _full_spec((CHUNK, SGU_W))],
        [_row_spec(tm, SGU_W)], [jax.ShapeDtypeStruct((SEQ, SGU_W), F32)],
        (u, vs, lg, lb, w_sp, bfull), exchanges=exchanges)


def out_proj(attn, sgu, x, ga, gs, w_out, gpm, gpf, exchanges=()):
    tm = 512

    def body(a_ref, s_ref, x_ref, ga_ref, gs_ref, w_ref, gpm_ref, gpf_ref, mix_ref, y_ref, x2_ref, h2_ref):
        a = a_ref[...]
        s = s_ref[...]
        an = (a * _rms(a) * ga_ref[...]).astype(BF16)
        sn = (s * _rms(s) * gs_ref[...]).astype(BF16)
        mix_ref[:, :ATTN_W] = an
        mix_ref[:, ATTN_W:] = sn
        y = _dot(an, w_ref[:ATTN_W, :]) + _dot(sn, w_ref[ATTN_W:, :])
        y_ref[...] = y
        x2 = x_ref[...] + y * _rms(y) * gpm_ref[...]
        x2_ref[...] = x2
        h2_ref[...] = (x2 * _rms(x2) * gpf_ref[...]).astype(BF16)

    wide = jax.ShapeDtypeStruct((SEQ, D_MODEL), F32)
    wide16 = jax.ShapeDtypeStruct((SEQ, D_MODEL), BF16)
    return _hosted(
        "out_proj", body, SEQ // tm,
        [_row_spec(tm, ATTN_W), _row_spec(tm, SGU_W), _row_spec(tm, D_MODEL), _full_spec((1, ATTN_W)),
         _full_spec((1, SGU_W)), _full_spec((D_MODEL, D_MODEL)), _full_spec((1, D_MODEL)), _full_spec((1, D_MODEL))],
        [_row_spec(tm, D_MODEL)] * 4, [wide16, wide, wide, wide16],
        (attn, sgu, x, ga, gs, w_out, gpm, gpf), exchanges=exchanges)


def ffn_up(h2, w_gate_t, w_up_t, exchanges=()):
    tm = 256

    def body(h_ref, wg_ref, wu_ref, g_ref, u_ref, a_ref):
        h = h_ref[...]
        g = _dot_nt(h, wg_ref[...])
        u = _dot_nt(h, wu_ref[...])
        g_ref[...] = g.astype(BF16)
        u_ref[...] = u.astype(BF16)
        a_ref[...] = (g * jax.nn.sigmoid(g) * u).astype(BF16)

    ff = jax.ShapeDtypeStruct((SEQ, D_FF), BF16)
    return _hosted(
        "ffn_up", body, SEQ // tm,
        [_row_spec(tm, D_MODEL), _full_spec((D_FF, D_MODEL)), _full_spec((D_FF, D_MODEL))],
        [_row_spec(tm, D_FF)] * 3, [ff, ff, jax.ShapeDtypeStruct((SEQ, D_FF), BF16)],
        (h2, w_gate_t, w_up_t), exchanges=exchanges)


def ffn_down_loss(act, w_down, x2, gpo, target):
    tm = 512

    def body(a_ref, w_ref, x2_ref, g_ref, t_ref, df_ref, dx3_ref, dg_ref, loss_ref):
        f = _dot(a_ref[...], w_ref[...])
        gain = g_ref[...]
        err = x2_ref[...] + f * _rms(f) * gain - t_ref[...]
        dx3 = err * np.float32(1.0 / D_MODEL)
        dx3_ref[...] = dx3
        df, dg = _rms_bwd(f, gain, dx3)
        df_ref[...] = df.astype(BF16)

        @pl.when(pl.program_id(0) == 0)
        def _():
            dg_ref[...] = jnp.zeros_like(dg_ref)
            loss_ref[...] = jnp.zeros_like(loss_ref)

        dg_ref[...] += dg
        loss_ref[...] += jnp.sum(err * err, axis=(0, 1), keepdims=True)

    return pl.pallas_call(
        body, name="ffn_down_loss", grid=(SEQ // tm,),
        in_specs=[_row_spec(tm, D_FF), _full_spec((D_FF, D_MODEL)), _row_spec(tm, D_MODEL), _full_spec((1, D_MODEL)),
                  _row_spec(tm, D_MODEL)],
        out_specs=[_row_spec(tm, D_MODEL), _row_spec(tm, D_MODEL), _full_spec((1, D_MODEL)), _full_spec((1, 1))],
        out_shape=[jax.ShapeDtypeStruct((SEQ, D_MODEL), BF16), jax.ShapeDtypeStruct((SEQ, D_MODEL), F32),
                   jax.ShapeDtypeStruct((1, D_MODEL), F32), jax.ShapeDtypeStruct((1, 1), F32)],
        compiler_params=_params(),
    )(act, w_down, x2, gpo, target)


def ffn_act_bwd(df, w_down, gate, up, exchanges=()):
    tm = 256

    def body(df_ref, w_ref, g_ref, u_ref, dg_ref, du_ref):
        dact = _dot_nt(df_ref[...], w_ref[...])
        g = g_ref[...].astype(F32)
        s = jax.nn.sigmoid(g)
        du_ref[...] = (dact * g * s).astype(BF16)
        dg_ref[...] = (dact * u_ref[...].astype(F32) * (s * (1.0 + g * (1.0 - s)))).astype(BF16)

    ff16 = jax.ShapeDtypeStruct((SEQ, D_FF), BF16)
    return _hosted(
        "ffn_act_bwd", body, SEQ // tm,
        [_row_spec(tm, D_MODEL), _full_spec((D_FF, D_MODEL)), _row_spec(tm, D_FF), _row_spec(tm, D_FF)],
        [_row_spec(tm, D_FF)] * 2, [ff16, ff16], (df, w_down, gate, up), exchanges=exchanges)


def ffn_in_bwd(dgate, dup, w_gate_t, w_up_t, x2, gpf, dx3, y, gpm, exchanges=()):
    tm = 256

    def body(dg_ref, du_ref, wg_ref, wu_ref, x2_ref, gpf_ref, dx3_ref, y_ref, gpm_ref, dx2_ref, dy_ref, dgpf_ref, dgpm_ref):
        dh2 = _dot(dg_ref[...], wg_ref[...]) + _dot(du_ref[...], wu_ref[...])
        dz, dgpf = _rms_bwd(x2_ref[...], gpf_ref[...], dh2)
        dx2 = dx3_ref[...] + dz
        dx2_ref[...] = dx2
        dy, dgpm = _rms_bwd(y_ref[...], gpm_ref[...], dx2)
        dy_ref[...] = dy.astype(BF16)

        @pl.when(pl.program_id(0) == 0)
        def _():
            dgpf_ref[...] = jnp.zeros_like(dgpf_ref)
            dgpm_ref[...] = jnp.zeros_like(dgpm_ref)

        dgpf_ref[...] += dgpf
        dgpm_ref[...] += dgpm

    vec = jax.ShapeDtypeStruct((1, D_MODEL), F32)
    return _hosted(
        "ffn_in_bwd", body, SEQ // tm,
        [_row_spec(tm, D_FF), _row_spec(tm, D_FF), _full_spec((D_FF, D_MODEL)), _full_spec((D_FF, D_MODEL)),
         _row_spec(tm, D_MODEL), _full_spec((1, D_MODEL)), _row_spec(tm, D_MODEL), _row_spec(tm, D_MODEL),
         _full_spec((1, D_MODEL))],
        [_row_spec(tm, D_MODEL), _row_spec(tm, D_MODEL), _full_spec((1, D_MODEL)), _full_spec((1, D_MODEL))],
        [jax.ShapeDtypeStruct((SEQ, D_MODEL), F32), jax.ShapeDtypeStruct((SEQ, D_MODEL), BF16), vec, vec],
        (dgate, dup, w_gate_t, w_up_t, x2, gpf, dx3, y, gpm), exchanges=exchanges)


def weight_grad(name, a, b, exchanges=()):
    m, n = a.shape[1], b.shape[1]
    tr = 256

    def body(a_ref, b_ref, o_ref):
        o_ref[...] = _dot_tn(a_ref[...], b_ref[...]).astype(BF16)

    (out,), done = _hosted(
        name, body, m // tr, [pl.BlockSpec((SEQ, tr), lambda i: (0, i)), _full_spec((SEQ, n))],
        [_row_spec(tr, n)], [jax.ShapeDtypeStruct((m, n), BF16)], (a, b), exchanges=exchanges)
    return out.reshape(N_DEV, m // N_DEV, n), done


def mix_bwd(dy, w_out, attn, sgu, ga, gs, exchanges=()):
    tm = 512

    def body(dy_ref, w_ref, a_ref, s_ref, ga_ref, gs_ref, da_ref, ds_ref, dga_ref, dgs_ref):
        dy = dy_ref[...]
        da, dga = _rms_bwd(a_ref[...], ga_ref[...], _dot_nt(dy, w_ref[:ATTN_W, :]))
        ds, dgs = _rms_bwd(s_ref[...], gs_ref[...], _dot_nt(dy, w_ref[ATTN_W:, :]))
        da_ref[...] = da
        ds_ref[...] = ds

        @pl.when(pl.program_id(0) == 0)
        def _():
            dga_ref[...] = jnp.zeros_like(dga_ref)
            dgs_ref[...] = jnp.zeros_like(dgs_ref)

        dga_ref[...] += dga
        dgs_ref[...] += dgs

    half = jax.ShapeDtypeStruct((SEQ, 512), F32)
    vec = jax.ShapeDtypeStruct((1, 512), F32)
    return _hosted(
        "mix_bwd", body, SEQ // tm,
        [_row_spec(tm, D_MODEL), _full_spec((D_MODEL, D_MODEL)), _row_spec(tm, 512), _row_spec(tm, 512),
         _full_spec((1, 512)), _full_spec((1, 512))],
        [_row_spec(tm, 512), _row_spec(tm, 512), _full_spec((1, 512)), _full_spec((1, 512))],
        [half, half, vec, vec], (dy, w_out, attn, sgu, ga, gs), exchanges=exchanges)


def sgu_bwd(u, vs, dsgu, lg, lb, w_sp, bfull, exchanges=()):
    cpb = 4

    def body(u_ref, vs_ref, d_ref, lg_ref, lb_ref, w_ref, b_ref, du_ref, dvs_ref, dlg_ref, dlb_ref, dw_ref, db_ref):
        wc, causal = _causal_weights(w_ref)
        head0 = lax.broadcasted_iota(jnp.int32, (CHUNK, 128), 1) < HEAD_DIM
        lg = lg_ref[...]

        @pl.when(pl.program_id(0) == 0)
        def _():
            dlg_ref[...] = jnp.zeros_like(dlg_ref)
            dlb_ref[...] = jnp.zeros_like(dlb_ref)
            dw_ref[...] = jnp.zeros_like(dw_ref)
            db_ref[...] = jnp.zeros_like(db_ref)

        for ci in range(cpb):
            rows = pl.ds(ci * CHUNK, CHUNK)
            u = u_ref[rows, :]
            vs = vs_ref[rows, :]
            d = d_ref[rows, :]
            ug, xhat, rstd, vn, ms = _sgu_chunk_fwd(u, vs, lg, lb_ref[...], wc, b_ref[...], head0)
            du_ref[rows, :] = (d * ms * _gelu_grad(u)).astype(BF16)
            dms = d * ug
            db_ref[...] += dms
            dvn = []
            for gp in range(SGU_W // 128):
                dmp = dms[:, gp * 128:(gp + 1) * 128]
                dm0 = jnp.where(head0, dmp, 0.0).astype(BF16)
                dm1 = jnp.where(head0, 0.0, dmp).astype(BF16)
                vp = vn[:, gp * 128:(gp + 1) * 128].astype(BF16)
                dw_ref[2 * gp] += _dot_nt(dm0, vp)
                dw_ref[2 * gp + 1] += _dot_nt(dm1, vp)
                dvn.append(_dot_tn(wc[2 * gp], dm0) + _dot_tn(wc[2 * gp + 1], dm1))
            dvn = jnp.concatenate(dvn, axis=1)
            dlg_ref[...] += jnp.sum(dvn * xhat, axis=0, keepdims=True)
            dlb_ref[...] += jnp.sum(dvn, axis=0, keepdims=True)
            dxh = dvn * lg
            dvg = rstd * (dxh - jnp.mean(dxh, axis=-1, keepdims=True) - xhat * jnp.mean(dxh * xhat, axis=-1, keepdims=True))
            dvs_ref[rows, :] = (dvg * _gelu_grad(vs)).astype(BF16)

        @pl.when(pl.program_id(0) == pl.num_programs(0) - 1)
        def _():
            for g in range(N_GROUPS):
                dw_ref[g] = jnp.where(causal, dw_ref[g], 0.0)

    tm = cpb * CHUNK
    half16 = jax.ShapeDtypeStruct((SEQ, SGU_W), BF16)
    vec = jax.ShapeDtypeStruct((1, SGU_W), F32)
    return _hosted(
        "sgu_bwd", body, SEQ // tm,
        [_row_spec(tm, SGU_W)] * 3 + [_full_spec((1, SGU_W)), _full_spec((1, SGU_W)),
                                      _full_spec((N_GROUPS, CHUNK, CHUNK)), _full_spec((CHUNK, SGU_W))],
        [_row_spec(tm, SGU_W), _row_spec(tm, SGU_W), _full_spec((1, SGU_W)), _full_spec((1, SGU_W)),
         _full_spec((N_GROUPS, CHUNK, CHUNK)), _full_spec((CHUNK, SGU_W))],
        [half16, half16, vec, vec, jax.ShapeDtypeStruct((N_GROUPS, CHUNK, CHUNK), F32),
         jax.ShapeDtypeStruct((CHUNK, SGU_W), F32)],
        (u, vs, dsgu, lg, lb, w_sp, bfull), exchanges=exchanges)


def attn_bwd(q, k, v, o, lse, do, pos_col, rot, exchanges=()):
    def body(q_ref, k_ref, v_ref, o_ref, lse_ref, do_ref, pos_ref, invf_ref, ma_ref, mb_ref,
             dq_ref, dk_ref, dv_ref, dqa_ref, dka_ref, dva_ref, dlt_ref, rot_ref):
        dqa_ref[...] = jnp.zeros_like(dqa_ref)
        dka_ref[...] = jnp.zeros_like(dka_ref)
        dva_ref[...] = jnp.zeros_like(dva_ref)

        def delta(i, carry):
            rows = pl.ds(pl.multiple_of(i * 256, 256), 256)
            prod = do_ref[rows, :] * o_ref[rows, :]
            h0 = lax.broadcasted_iota(jnp.int32, (256, 128), 1) < HEAD_DIM
            d0 = jnp.sum(jnp.where(h0, prod, 0.0), axis=-1, keepdims=True)
            d1 = jnp.sum(jnp.where(h0, 0.0, prod), axis=-1, keepdims=True)
            dlt_ref[rows, :] = jnp.where(h0, d0, d1)
            return carry

        lax.fori_loop(0, SEQ // 256, delta, 0)

        def add_rows(ref, slices, val):
            at = 0
            for start, size in slices:
                ref[pl.ds(start, size), :] += val[at:at + size]
                at += size

        def group(p, masks, blocks):
            head0, mask1, mask2 = masks
            heads = (head0, jnp.logical_not(head0))
            keys = [rows if prev is None else prev + rows for rows, prev in blocks]
            mask = [mask1 if prev is None else mask2 for _, prev in blocks]
            kk = [_load_rows(k_ref, ks).astype(BF16) for ks in keys]
            vv = [_load_rows(v_ref, ks).astype(BF16) for ks in keys]
            qb = [_load_rows(q_ref, rows) for rows, _ in blocks]
            dob = [_load_rows(do_ref, rows) for rows, _ in blocks]
            lse_b = [_load_rows(lse_ref, rows) for rows, _ in blocks]
            dlt_b = [_load_rows(dlt_ref, rows) for rows, _ in blocks]
            chains = [(g, h) for g in range(len(blocks)) for h in range(2)]
            qm = [jnp.where(heads[h], qb[g], 0.0).astype(BF16) for g, h in chains]
            dom = [jnp.where(heads[h], dob[g], 0.0).astype(BF16) for g, h in chains]
            s = [_dot_nt(qm[c], kk[g]) for c, (g, h) in enumerate(chains)]
            dp = [_dot_nt(dom[c], vv[g]) for c, (g, h) in enumerate(chains)]
            pr = [jnp.where(mask[g], jnp.exp(s[c] - lse_b[g][:, h * HEAD_DIM:h * HEAD_DIM + 1]), 0.0)
                  for c, (g, h) in enumerate(chains)]
            ds = [(pr[c] * (dp[c] - dlt_b[g][:, h * HEAD_DIM:h * HEAD_DIM + 1])).astype(BF16)
                  for c, (g, h) in enumerate(chains)]
            dv = [_dot_tn(pr[c].astype(BF16), dom[c]) for c in range(len(chains))]
            dk = [_dot_tn(ds[c], qm[c]) for c in range(len(chains))]
            dq = [_dot(ds[c], kk[g]) for c, (g, h) in enumerate(chains)]
            for g, (rows, _) in enumerate(blocks):
                add_rows(dqa_ref, rows, jnp.where(head0, dq[2 * g], dq[2 * g + 1]))
                add_rows(dka_ref, keys[g], dk[2 * g] + dk[2 * g + 1])
                add_rows(dva_ref, keys[g], dv[2 * g] + dv[2 * g + 1])

        _for_each_group(group)

        @pl.when(pl.program_id(0) == 0)
        def _():
            def tables(i, carry):
                rows = pl.ds(pl.multiple_of(i * 256, 256), 256)
                c, sa, sb = _rot_tables(pos_ref[rows, :], invf_ref[...], ma_ref[...], mb_ref[...])
                rot_ref[0, rows, :] = c
                rot_ref[1, rows, :] = sa
                rot_ref[2, rows, :] = sb
                return carry

            lax.fori_loop(0, SEQ // 256, tables, 0)

        def finish(i, carry):
            rows = pl.ds(pl.multiple_of(i * 256, 256), 256)
            c, sa, sb = rot_ref[0, rows, :], rot_ref[1, rows, :], rot_ref[2, rows, :]
            dq_ref[rows, :] = _rot_t(dqa_ref[rows, :] * Q_SCALE, c, sa, sb).astype(BF16)
            dk_ref[rows, :] = _rot_t(dka_ref[rows, :], c, sa, sb).astype(BF16)
            dv_ref[rows, :] = dva_ref[rows, :].astype(BF16)
            return carry

        lax.fori_loop(0, SEQ // 256, finish, 0)

    slab = pl.BlockSpec((SEQ, 128), lambda i: (0, i))
    out = jax.ShapeDtypeStruct((SEQ, ATTN_W), BF16)
    acc = pltpu.VMEM((SEQ, 128), F32)
    return _hosted(
        "attn_bwd", body, ATTN_W // 128,
        [slab] * 6 + [_full_spec((SEQ, 1)), _full_spec((1, 128)), _full_spec((1, 128)), _full_spec((1, 128))],
        [slab] * 3, [out, out, out], (q, k, v, o, lse, do, pos_col, *rot),
        scratch_shapes=[acc, acc, acc, acc, pltpu.VMEM((3, SEQ, 128), F32)], exchanges=exchanges)


def in_bwd(dproj, w_in_t, x, g1, dx2, exchanges=()):
    tm = 512

    def body(dp_ref, w_ref, x_ref, g_ref, dx2_ref, dx_ref, dg_ref):
        dh1 = _dot(dp_ref[...], w_ref[...])
        dz, dg = _rms_bwd(x_ref[...], g_ref[...], dh1)
        dx_ref[...] = dx2_ref[...] + dz

        @pl.when(pl.program_id(0) == 0)
        def _():
            dg_ref[...] = jnp.zeros_like(dg_ref)

        dg_ref[...] += dg

    return _hosted(
        "in_bwd", body, SEQ // tm,
        [_row_spec(tm, IN_W), _full_spec((IN_W, D_MODEL)), _row_spec(tm, D_MODEL), _full_spec((1, D_MODEL)),
         _row_spec(tm, D_MODEL)],
        [_row_spec(tm, D_MODEL), _full_spec((1, D_MODEL))],
        [jax.ShapeDtypeStruct((SEQ, D_MODEL), F32), jax.ShapeDtypeStruct((1, D_MODEL), F32)],
        (dproj, w_in_t, x, g1, dx2), exchanges=exchanges)


def _coords():
    return lax.axis_index("x"), lax.axis_index("y"), lax.axis_index("c")


class Exchange:
    def __init__(self, srcs, bufs, new_shapes, n_sems, make, mid_step=None):
        self.srcs, self.bufs, self.new_shapes, self.n_sems, self.make = list(srcs), list(bufs), list(new_shapes), n_sems, make
        self.mid_step = mid_step


def _hosted(name, body, n_steps, in_specs, out_specs, out_shape, args, scratch_shapes=(), exchanges=(), prefetch=None):
    out_shape, out_specs = list(out_shape), list(out_specs)
    srcs = [a for ex in exchanges for a in ex.srcs]
    bufs = [a for ex in exchanges for a in ex.bufs]
    news = [s for ex in exchanges for s in ex.new_shapes]
    n_pre = 0 if prefetch is None else 1
    n_in, n_out, n_scr = len(args), len(out_shape), len(scratch_shapes)

    def wrapped(*refs):
        refs = list(refs)
        pre, refs = refs[:n_pre], refs[n_pre:]
        ins = refs[:n_in]
        src_refs = refs[n_in:n_in + len(srcs)]
        at = n_in + len(srcs) + len(bufs)
        outs = refs[at:at + n_out]
        buf_refs = refs[at + n_out:at + n_out + len(bufs)]
        new_refs = refs[at + n_out + len(bufs):at + n_out + len(bufs) + len(news)]
        at += n_out + len(bufs) + len(news)
        scratch, sems = refs[at:at + n_scr], refs[at + n_scr:]

        def copies(phase, which):
            made, si, bi, ni = [], 0, 0, 0
            for k, ex in enumerate(exchanges):
                if which(ex):
                    made.append(ex.make(phase, src_refs[si:si + len(ex.srcs)], buf_refs[bi:bi + len(ex.bufs)],
                                        new_refs[ni:ni + len(ex.new_shapes)], sems[2 * k], sems[2 * k + 1]))
                si, bi, ni = si + len(ex.srcs), bi + len(ex.bufs), ni + len(ex.new_shapes)
            return made

        if exchanges:
            @pl.when(pl.program_id(0) == 0)
            def _():
                for starts in copies("start", lambda ex: True):
                    for cp in starts:
                        cp.start()

        def pass_on(mid):
            for arrivals, starts in copies("middle", lambda ex: ex.mid_step == mid):
                for cp in arrivals:
                    cp.wait_recv()
                for cp in starts:
                    cp.start()

        for mid in sorted({ex.mid_step for ex in exchanges if isinstance(ex.mid_step, int)}):
            @pl.when(pl.program_id(0) == mid)
            def _(mid=mid):
                pass_on(mid)

        body(*pre, *ins, *outs, *scratch)

        if exchanges:
            @pl.when(pl.program_id(0) == n_steps - 1)
            def _():
                pass_on("end")
                for sends, recvs in copies("end", lambda ex: True):
                    for cp in sends:
                        cp.wait_send()
                    for cp in recvs:
                        cp.wait_recv()

    sem_shapes = []
    for ex in exchanges:
        sem_shapes += [pltpu.SemaphoreType.DMA((ex.n_sems,)), pltpu.SemaphoreType.DMA((ex.n_sems,))]
    all_in = list(in_specs) + [ANY] * (len(srcs) + len(bufs))
    all_out = out_specs + [ANY] * (len(bufs) + len(news))
    all_shape = out_shape + [jax.ShapeDtypeStruct(b.shape, b.dtype) for b in bufs] + news
    all_scratch = list(scratch_shapes) + sem_shapes
    aliases = {n_pre + n_in + len(srcs) + i: n_out + i for i in range(len(bufs))}
    if prefetch is None:
        call = pl.pallas_call(wrapped, name=name, grid=(n_steps,), in_specs=all_in, out_specs=all_out, out_shape=all_shape,
                              scratch_shapes=all_scratch, input_output_aliases=aliases, compiler_params=_params())
        outs = call(*args, *srcs, *bufs)
    else:
        spec = pltpu.PrefetchScalarGridSpec(num_scalar_prefetch=1, grid=(n_steps,), in_specs=all_in, out_specs=all_out,
                                            scratch_shapes=all_scratch)
        call = pl.pallas_call(wrapped, name=name, grid_spec=spec, out_shape=all_shape, input_output_aliases=aliases,
                              compiler_params=_params())
        outs = call(prefetch, *args, *srcs, *bufs)
    results, bi, ni = [], n_out, n_out + len(bufs)
    for ex in exchanges:
        results.append((list(outs[bi:bi + len(ex.bufs)]), list(outs[ni:ni + len(ex.new_shapes)])))
        bi, ni = bi + len(ex.bufs), ni + len(ex.new_shapes)
    return list(outs[:n_out]), results


def _gather_copies(kinds, bufs, ranges, send_sems, recv_sems):
    x, y, c = _coords()
    me, sibling = (x, y, c), (x, y, 1 - c)
    chips = [(1 - x, y), (x, 1 - y), (1 - x, 1 - y)]

    def copy(a, k, block, to):
        lo, hi = ranges[a]
        r = bufs[a].shape[0] // N_DEV
        rows = bufs[a].at[pl.ds((4 * block[0] + 2 * block[1] + block[2]) * r + lo, hi - lo), :]
        return pltpu.make_async_remote_copy(src_ref=rows, dst_ref=rows, send_sem=send_sems.at[7 * a + k],
                                            recv_sem=recv_sems.at[7 * a + k], device_id=to, device_id_type=MESH)

    every = range(len(bufs))
    make = {
        "out": lambda: [copy(a, 0, me, sibling) for a in every]
        + [copy(a, 1 + j, me, (*chip, c)) for a in every for j, chip in enumerate(chips)],
        "from_core": lambda: [copy(a, 0, sibling, me) for a in every],
        "from_chips": lambda: [copy(a, 1 + j, (*chip, c), me) for a in every for j, chip in enumerate(chips)],
        "on": lambda: [copy(a, 4 + j, (*chip, c), sibling) for a in every for j, chip in enumerate(chips)],
        "on_in": lambda: [copy(a, 4 + j, (*chip, 1 - c), me) for a in every for j, chip in enumerate(chips)],
    }
    return [make[kind]() for kind in kinds]


def gather(bufs, mid_step, ranges=None):
    ranges = ranges or [(0, b.shape[0] // N_DEV) for b in bufs]

    def make(phase, src_refs, buf_refs, new_refs, send_sems, recv_sems):
        kinds = {"start": ["out"], "middle": ["from_chips", "on"], "end": ["out", "on", "from_core", "on_in"]}[phase]
        made = _gather_copies(kinds, buf_refs, ranges, send_sems, recv_sems)
        if phase == "start":
            return made[0]
        if phase == "middle":
            return made[0], made[1]
        return made[0] + made[1], made[2] + made[3]

    return Exchange([], bufs, [], 7 * len(bufs), make, mid_step=mid_step)


def all_gather_in_place(name, bufs):
    n = len(bufs)
    ranges = [(0, b.shape[0] // N_DEV) for b in bufs]

    def body(*refs):
        outs, send_sems, recv_sems = refs[n:2 * n], refs[2 * n], refs[2 * n + 1]
        out, from_core, from_chips, on, on_in = _gather_copies(
            ["out", "from_core", "from_chips", "on", "on_in"], outs, ranges, send_sems, recv_sems)
        for cp in out:
            cp.start()
        for cp in from_chips:
            cp.wait_recv()
        for cp in on:
            cp.start()
        for cp in from_core + on_in:
            cp.wait_recv()
        for cp in out + on:
            cp.wait_send()

    return pl.pallas_call(
        body, name=name, in_specs=[ANY] * n, out_specs=[ANY] * n,
        out_shape=[jax.ShapeDtypeStruct(b.shape, b.dtype) for b in bufs],
        scratch_shapes=[pltpu.SemaphoreType.DMA((7 * n,)), pltpu.SemaphoreType.DMA((7 * n,))],
        input_output_aliases={i: i for i in range(n)},
    )(*bufs)


TO_GATHER = (1, lambda x, y, c: [(x, y, 1 - c), (1 - x, y, c), (x, 1 - y, c), (1 - x, 1 - y, c)])
TO_SIBLING = (2, lambda x, y, c: [(x, y, 1 - c)])
TO_CHIPS = (3, lambda x, y, c: [(1 - x, y, c), (x, 1 - y, c), (1 - x, 1 - y, c)])
TO_ALL = (4, lambda x, y, c: [(x ^ (m >> 2), y ^ ((m >> 1) & 1), c ^ (m & 1)) for m in range(1, N_DEV)])


def by_sequencer(name, ex, who):
    collective_id, peers_of = who
    hbm = pltpu.MemorySpace.HBM
    src_refs = [jax.new_ref(a, memory_space=hbm) for a in ex.srcs]
    buf_refs = [jax.new_ref(a, memory_space=hbm) for a in ex.bufs]
    new_refs = [jax.empty_ref(s, memory_space=hbm) for s in ex.new_shapes]

    @pl.kernel(mesh=plsc.ScalarSubcoreMesh(axis_name="sequencer", num_cores=1), name=name,
               scratch_types=(pltpu.SemaphoreType.DMA((ex.n_sems,)), pltpu.SemaphoreType.DMA((ex.n_sems,))),
               compiler_params=pltpu.CompilerParams(collective_id=collective_id))
    def launch(send_sems, recv_sems):
        peers = peers_of(*_coords())
        barrier = pltpu.get_barrier_semaphore()
        for peer in peers:
            pl.semaphore_signal(barrier, inc=1, device_id=peer, device_id_type=MESH)
        pl.semaphore_wait(barrier, len(peers))
        refs = (src_refs, buf_refs, new_refs, send_sems, recv_sems)
        for cp in ex.make("start", *refs):
            cp.start()
        if ex.mid_step is not None:
            arrivals, starts = ex.make("middle", *refs)
            for cp in arrivals:
                cp.wait_recv()
            for cp in starts:
                cp.start()
        sends, arrivals = ex.make("end", *refs)
        for cp in arrivals:
            cp.wait_recv()
        for cp in sends:
            cp.wait_send()

    launch()
    return [ref[...] for ref in buf_refs], [ref[...] for ref in new_refs]


def place_shards(shards, dev):
    n = len(shards)

    def body(dev_ref, *refs):
        for a in range(n):
            refs[n + a][...] = refs[a][...].astype(BF16)

    spec = pltpu.PrefetchScalarGridSpec(
        num_scalar_prefetch=1, grid=(1,),
        in_specs=[pl.BlockSpec(s.shape, lambda i, dev_ref: (0, 0)) for s in shards],
        out_specs=[pl.BlockSpec(s.shape, lambda i, dev_ref: (dev_ref[0], 0)) for s in shards])
    return pl.pallas_call(
        body, name="place_shards", grid_spec=spec,
        out_shape=[jax.ShapeDtypeStruct((N_DEV * s.shape[0], s.shape[1]), BF16) for s in shards],
        compiler_params=_params(),
    )(dev, *shards)


def _swap(copies_of):
    def make(phase, src_refs, buf_refs, new_refs, send_sems, recv_sems):
        copies = copies_of(src_refs, new_refs, send_sems, recv_sems)
        return copies if phase == "start" else (copies, copies)

    return make


def to_sibling(grads):
    def copies_of(src_refs, new_refs, send_sems, recv_sems):
        x, y, c = _coords()
        return [pltpu.make_async_remote_copy(
            src_ref=src_refs[a].at[2 * xy + 1 - c], dst_ref=new_refs[a].at[xy], send_sem=send_sems.at[4 * a + xy],
            recv_sem=recv_sems.at[4 * a + xy], device_id=(x, y, 1 - c), device_id_type=MESH)
            for a in range(len(src_refs)) for xy in range(4)]

    return Exchange(grads, [], [jax.ShapeDtypeStruct((4,) + g.shape[1:], g.dtype) for g in grads], 4 * len(grads),
                    _swap(copies_of))


def to_chips(parts):
    def copies_of(src_refs, new_refs, send_sems, recv_sems):
        x, y, c = _coords()
        chips = [(1 - x, y), (x, 1 - y), (1 - x, 1 - y)]
        return [pltpu.make_async_remote_copy(
            src_ref=src_refs[a].at[2 * px + py], dst_ref=new_refs[a].at[2 * x + y], send_sem=send_sems.at[3 * a + j],
            recv_sem=recv_sems.at[3 * a + j], device_id=(px, py, c), device_id_type=MESH)
            for a in range(len(src_refs)) for j, (px, py) in enumerate(chips)]

    return Exchange(parts, [], [jax.ShapeDtypeStruct(p.shape, p.dtype) for p in parts], 3 * len(parts), _swap(copies_of))


def to_owners(grad):
    def copies_of(src_refs, new_refs, send_sems, recv_sems):
        x, y, c = _coords()
        copies = []
        for m in range(1, N_DEV):
            px, py, pc = x ^ (m >> 2), y ^ ((m >> 1) & 1), c ^ (m & 1)
            copies.append(pltpu.make_async_remote_copy(
                src_ref=src_refs[0].at[4 * px + 2 * py + pc], dst_ref=new_refs[0].at[4 * x + 2 * y + c],
                send_sem=send_sems.at[m - 1], recv_sem=recv_sems.at[m - 1], device_id=(px, py, pc), device_id_type=MESH))
        return copies

    return Exchange([grad], [], [jax.ShapeDtypeStruct(grad.shape, grad.dtype)], N_DEV - 1, _swap(copies_of))


def exchange_only(name, exchanges):
    def body():
        pass

    return _hosted(name, body, 1, [], [], [], [], exchanges=exchanges)[1]


def sum_cores(name, grad, other, core):
    _, r, w = other.shape

    def body(core_ref, g_ref, o_ref, out_ref):
        out_ref[...] = (g_ref[...].astype(F32) + o_ref[...].astype(F32)).astype(out_ref.dtype)

    return pl.pallas_call(
        body, name=name,
        grid_spec=pltpu.PrefetchScalarGridSpec(
            num_scalar_prefetch=1, grid=(4,),
            in_specs=[pl.BlockSpec((1, r, w), lambda i, core_ref: (2 * i + core_ref[0], 0, 0)),
                      pl.BlockSpec((1, r, w), lambda i, core_ref: (i, 0, 0))],
            out_specs=pl.BlockSpec((1, r, w), lambda i, core_ref: (i, 0, 0))),
        out_shape=jax.ShapeDtypeStruct(other.shape, other.dtype),
        compiler_params=_params(),
    )(core, grad, other)


def sum_owned(name, grad, others, dev_ids):
    _, r, w = grad.shape

    def body(ids_ref, *refs):
        acc = refs[0][0]
        for k in range(1, N_DEV):
            acc = acc + refs[k][0]
        refs[N_DEV][...] = acc

    def pick(k):
        return pl.BlockSpec((1, r, w), lambda i, ids_ref: (ids_ref[k], 0, 0))

    return pl.pallas_call(
        body, name=name,
        grid_spec=pltpu.PrefetchScalarGridSpec(
            num_scalar_prefetch=1, grid=(1,), in_specs=[pick(k) for k in range(N_DEV)],
            out_specs=pl.BlockSpec((r, w), lambda i, ids_ref: (ids_ref[0], 0))),
        out_shape=jax.ShapeDtypeStruct((N_DEV * r, w), F32),
        compiler_params=_params(),
    )(dev_ids, grad, *([others] * (N_DEV - 1)))


def _adamw_update(w, g, m, v):
    nm = ADAM_B1 * m + np.float32(1.0 - ADAM_B1) * g
    nv = ADAM_B2 * v + np.float32(1.0 - ADAM_B2) * (g * g)
    m_hat = nm / np.float32(1.0 - ADAM_B1 ** ADAM_STEP)
    v_hat = nv / np.float32(1.0 - ADAM_B2 ** ADAM_STEP)
    return -ADAM_LR * (m_hat / (jnp.sqrt(v_hat) + ADAM_EPS) + ADAM_WD * w), nm, nv


def adamw_of_sums(name, part, others, chip_ids, w, m, v, after):
    _, r, wd = part.shape

    def body(ids_ref, p_ref, a_ref, b_ref, c_ref, w_ref, m_ref, v_ref, after_ref, g_ref, d_ref, nm_ref, nv_ref):
        g = ((p_ref[0].astype(F32) + a_ref[0].astype(F32)) + b_ref[0].astype(F32)) + c_ref[0].astype(F32)
        g_ref[...] = g
        d_ref[...], nm_ref[...], nv_ref[...] = _adamw_update(w_ref[...], g, m_ref[...], v_ref[...])

    def pick(k):
        return pl.BlockSpec((1, r, wd), lambda i, ids_ref: (ids_ref[k], 0, 0))

    whole = pl.BlockSpec((r, wd), lambda i, ids_ref: (0, 0))
    shape = jax.ShapeDtypeStruct((r, wd), F32)
    return pl.pallas_call(
        body, name=name,
        grid_spec=pltpu.PrefetchScalarGridSpec(
            num_scalar_prefetch=1, grid=(1,), in_specs=[pick(0), pick(1), pick(2), pick(3), whole, whole, whole, ANY],
            out_specs=[whole] * 4),
        out_shape=[shape] * 4,
        compiler_params=_params(),
    )(chip_ids, part, others, others, others, w, m, v, after)


def adamw(name, w, g, m, v):
    def body(w_ref, g_ref, m_ref, v_ref, d_ref, nm_ref, nv_ref):
        d_ref[...], nm_ref[...], nv_ref[...] = _adamw_update(w_ref[...], g_ref[...], m_ref[...], v_ref[...])

    shape = jax.ShapeDtypeStruct(w.shape, F32)
    spec = _full_spec(w.shape)
    return pl.pallas_call(
        body, name=name, grid=(1,), in_specs=[spec] * 4, out_specs=[spec] * 3, out_shape=[shape] * 3,
        compiler_params=_params(),
    )(w, g, m, v)


def _pack_small(parts):
    flat = jnp.concatenate([parts[name].reshape(-1) for name, _ in SMALL])
    return jnp.pad(flat, (0, SMALL_ROWS * 128 - flat.shape[0])).reshape(SMALL_ROWS, 128)


def _unpack_small(packed, like):
    flat = packed.reshape(-1)
    out, at = {}, 0
    for name, size in SMALL:
        out[name] = flat[at:at + size].reshape(like[name].shape)
        at += size
    return out


def kernel(x, positions, pre_mix_norm, w_in, sgu_ln_gain, sgu_ln_bias, sgu_w_spatial, sgu_b_spatial, attn_out_norm, sgu_out_norm, w_out, post_mix_norm, pre_ffn_norm, w_gate, w_up, w_down, post_ffn_norm, loss_target, m_pre_mix_norm, m_w_in, m_sgu_ln_gain, m_sgu_ln_bias, m_sgu_w_spatial, m_sgu_b_spatial, m_attn_out_norm, m_sgu_out_norm, m_w_out, m_post_mix_norm, m_pre_ffn_norm, m_w_gate, m_w_up, m_w_down, m_post_ffn_norm, v_pre_mix_norm, v_w_in, v_sgu_ln_gain, v_sgu_ln_bias, v_sgu_w_spatial, v_sgu_b_spatial, v_attn_out_norm, v_sgu_out_norm, v_w_out, v_post_mix_norm, v_pre_ffn_norm, v_w_gate, v_w_up, v_w_down, v_post_ffn_norm):
    small_w = dict(pre_mix_norm=pre_mix_norm, sgu_ln_gain=sgu_ln_gain, sgu_ln_bias=sgu_ln_bias, sgu_w_spatial=sgu_w_spatial,
                   sgu_b_spatial=sgu_b_spatial, attn_out_norm=attn_out_norm, sgu_out_norm=sgu_out_norm,
                   post_mix_norm=post_mix_norm, pre_ffn_norm=pre_ffn_norm, post_ffn_norm=post_ffn_norm)
    small_m = dict(pre_mix_norm=m_pre_mix_norm, sgu_ln_gain=m_sgu_ln_gain, sgu_ln_bias=m_sgu_ln_bias, sgu_w_spatial=m_sgu_w_spatial,
                   sgu_b_spatial=m_sgu_b_spatial, attn_out_norm=m_attn_out_norm, sgu_out_norm=m_sgu_out_norm,
                   post_mix_norm=m_post_mix_norm, pre_ffn_norm=m_pre_ffn_norm, post_ffn_norm=m_post_ffn_norm)
    small_v = dict(pre_mix_norm=v_pre_mix_norm, sgu_ln_gain=v_sgu_ln_gain, sgu_ln_bias=v_sgu_ln_bias, sgu_w_spatial=v_sgu_w_spatial,
                   sgu_b_spatial=v_sgu_b_spatial, attn_out_norm=v_attn_out_norm, sgu_out_norm=v_sgu_out_norm,
                   post_mix_norm=v_post_mix_norm, pre_ffn_norm=v_pre_ffn_norm, post_ffn_norm=v_post_ffn_norm)
    for table in (small_w, small_m, small_v):
        table["loss_sum"] = jnp.zeros((1,), F32)

    x2d = x[0]
    target = loss_target[0]
    pos_col = positions.reshape(SEQ, 1)
    rot = _rot_consts()
    w_sp = sgu_w_spatial[0]
    bfull = jnp.repeat(sgu_b_spatial[0].T, HEAD_DIM, axis=1)

    x_i, y_i, c_i = (lax.axis_index(a).astype(jnp.int32) for a in MESH_AXES)
    dev = 4 * x_i + 2 * y_i + c_i
    core = c_i.reshape(1)
    chip = 2 * x_i + y_i
    chip_ids = jnp.stack([chip, chip ^ 1, chip ^ 2, chip ^ 3])
    dev_ids = jnp.stack([dev ^ m for m in range(N_DEV)])

    w_in_t, w_gate_t, w_up_t, w_out_f, w_down_f = place_shards(
        [w_in[0].T, w_gate[0].T, w_up[0].T, w_out[0], w_down[0]], dev.reshape(1))
    def gathered(name, bufs):
        return by_sequencer(name, gather(bufs, "end"), TO_GATHER)[0]

    def from_sibling(name, grad):
        return by_sequencer(name, to_sibling([grad]), TO_SIBLING)[1][0]

    def from_chips(name, part):
        return by_sequencer(name, to_chips([part]), TO_CHIPS)[1][0]

    (w_in_t,) = gathered("gather_w_in", [w_in_t])
    (w_out_f,) = gathered("gather_w_out", [w_out_f])
    w_gate_t, w_up_t = gathered("gather_w_gate_up", [w_gate_t, w_up_t])
    (w_down_f,) = gathered("gather_w_down", [w_down_f])

    (h1, q, k, v, u, vs), _ = in_proj(x2d, pos_col, pre_mix_norm, w_in_t, rot)
    q, k, v = (_to_residue_order(t) for t in (q, k, v))
    (attn_r, lse), _ = attn_fwd(q, k, v)
    attn = _from_residue_order(attn_r)
    (sgu,), _ = sgu_fwd(u, vs, sgu_ln_gain, sgu_ln_bias, w_sp, bfull)
    (mix, y, x2, h2), _ = out_proj(attn, sgu, x2d, attn_out_norm, sgu_out_norm, w_out_f, post_mix_norm, pre_ffn_norm)
    (gate, up, act), _ = ffn_up(h2, w_gate_t, w_up_t)
    df, dx3, d_post_ffn, sq_err = ffn_down_loss(act, w_down_f, x2, post_ffn_norm, target)

    g_w_down, _ = weight_grad("grad_w_down", act, df)
    s_down = from_sibling("w_down_to_sibling", g_w_down)
    (dgate, dup), _ = ffn_act_bwd(df, w_down_f, gate, up)
    p_down = sum_cores("sum_cores_down", g_w_down, s_down, core)
    c_down = from_chips("w_down_to_chips", p_down)
    g_w_gate, _ = weight_grad("grad_w_gate", dgate, h2)
    s_gate = from_sibling("w_gate_to_sibling", g_w_gate)
    g_w_up, _ = weight_grad("grad_w_up", dup, h2)
    s_up = from_sibling("w_up_to_sibling", g_w_up)
    p_gate = sum_cores("sum_cores_gate", g_w_gate, s_gate, core)
    c_gate = from_chips("w_gate_to_chips", p_gate)
    (dx2, dy, d_pre_ffn, d_post_mix), _ = ffn_in_bwd(dgate, dup, w_gate_t, w_up_t, x2, pre_ffn_norm, dx3, y, post_mix_norm)
    p_up = sum_cores("sum_cores_up", g_w_up, s_up, core)
    c_up = from_chips("w_up_to_chips", p_up)
    g_w_out, _ = weight_grad("grad_w_out", mix, dy)
    s_out = from_sibling("w_out_to_sibling", g_w_out)
    (dattn, dsgu, d_attn_out, d_sgu_out), _ = mix_bwd(dy, w_out_f, attn, sgu, attn_out_norm, sgu_out_norm)
    p_out = sum_cores("sum_cores_out", g_w_out, s_out, core)
    c_out = from_chips("w_out_to_chips", p_out)
    (du, dvs, d_ln_gain, d_ln_bias, d_w_sp, d_bfull), _ = sgu_bwd(u, vs, dsgu, sgu_ln_gain, sgu_ln_bias, w_sp, bfull)
    (dq, dk, dv), _ = attn_bwd(q, k, v, attn_r, lse, _to_residue_order(dattn), _to_residue_order(pos_col), rot)
    dq, dk, dv = (_from_residue_order(t) for t in (dq, dk, dv))
    dproj = jnp.concatenate([dq, dk, dv, du, dvs], axis=1)
    g_w_in, _ = weight_grad("grad_w_in", dproj, h1)
    s_in = from_sibling("w_in_to_sibling", g_w_in)
    (grad_x, d_pre_mix), _ = in_bwd(dproj, w_in_t, x2d, pre_mix_norm, dx2)
    p_in = sum_cores("sum_cores_in", g_w_in, s_in, core)
    c_in = from_chips("w_in_to_chips", p_in)

    d_b_sp = d_bfull.reshape(CHUNK, N_GROUPS, HEAD_DIM).sum(axis=-1).T
    small_g = _pack_small(dict(pre_mix_norm=d_pre_mix, sgu_ln_gain=d_ln_gain, sgu_ln_bias=d_ln_bias, sgu_w_spatial=d_w_sp,
                               sgu_b_spatial=d_b_sp, attn_out_norm=d_attn_out, sgu_out_norm=d_sgu_out,
                               post_mix_norm=d_post_mix, pre_ffn_norm=d_pre_ffn, post_ffn_norm=d_post_ffn,
                               loss_sum=sq_err))
    small_g = small_g.reshape(N_DEV, SMALL_ROWS // N_DEV, 128)
    o_small = by_sequencer("small_to_owners", to_owners(small_g), TO_ALL)[1][0]
    all_small = sum_owned("sum_small", small_g, o_small, dev_ids)
    (all_small,) = gathered("gather_small_grads", [all_small])

    big, last = {}, p_in
    for name, w, p, c, m, vv in (("w_down", w_down, p_down, c_down, m_w_down, v_w_down),
                                 ("w_out", w_out, p_out, c_out, m_w_out, v_w_out)):
        outs = adamw_of_sums("adamw_" + name, p, c, chip_ids, w[0], m[0], vv[0], last)
        big[name], last = tuple(t[None] for t in outs), outs[0]
    for name, w, p, c, m, vv in (("w_gate", w_gate, p_gate, c_gate, m_w_gate, v_w_gate), ("w_up", w_up, p_up, c_up, m_w_up, v_w_up),
                                 ("w_in", w_in, p_in, c_in, m_w_in, v_w_in)):
        outs = adamw_of_sums("adamw_" + name, p, c, chip_ids, w[0].T, m[0].T, vv[0].T, last)
        big[name], last = tuple(t.T[None] for t in outs), outs[0]
    sd, snm, snv = adamw("adamw_small", _pack_small(small_w), all_small, _pack_small(small_m), _pack_small(small_v))
    sg, sd, snm, snv = (_unpack_small(t, small_w) for t in (all_small, sd, snm, snv))
    loss = sg["loss_sum"][0] * np.float32(0.5 / D_MODEL)

    names = ["pre_mix_norm", "w_in", "sgu_ln_gain", "sgu_ln_bias", "sgu_w_spatial", "sgu_b_spatial", "attn_out_norm",
             "sgu_out_norm", "w_out", "post_mix_norm", "pre_ffn_norm", "w_gate", "w_up", "w_down", "post_ffn_norm"]
    outs = [loss, grad_x[None]]
    for i, table in enumerate((sg, sd, snm, snv)):
        for name in names:
            outs.append(big[name][i] if name in big else table[name])
    return tuple(outs)
```

```python
import functools

import numpy as np
import jax
import jax.numpy as jnp
from jax import lax
from jax.experimental import pallas as pl
from jax.experimental.pallas import tpu as pltpu
from jax.experimental.pallas import tpu_sc as plsc

F32 = jnp.float32
BF16 = jnp.bfloat16

SEQ = 2048
D_MODEL = 1024
ATTN_W = 512
SGU_W = 512
HEAD_DIM = 64
N_GROUPS = 8
CHUNK = 128
D_FF = 2816
IN_W = 3 * ATTN_W + 2 * SGU_W
DILATIONS = (1, 4, 16)
ROPE_THETA = 500000.0
ROT_DIM = 16
ROT_HALF = 8
RMS_EPS = 1e-6
LN_EPS = 1e-5
Q_SCALE = 0.125
NEG = -1e30

N_DEV = 8
MESH_AXES = ("x", "y", "c")
MESH = pl.DeviceIdType.MESH

ADAM_LR = 0.001
ADAM_B1 = 0.9
ADAM_B2 = 0.999
ADAM_EPS = 1e-08
ADAM_WD = 0.01
ADAM_STEP = 10

VMEM_LIMIT = 60 * 1024 * 1024
ANY = pl.BlockSpec(memory_space=pl.ANY)

SMALL = (("pre_mix_norm", 1024), ("sgu_ln_gain", 512), ("sgu_ln_bias", 512), ("sgu_w_spatial", 8 * 128 * 128),
         ("sgu_b_spatial", 1024), ("attn_out_norm", 512), ("sgu_out_norm", 512), ("post_mix_norm", 1024),
         ("pre_ffn_norm", 1024), ("post_ffn_norm", 1024), ("loss_sum", 1))
SMALL_ROWS = 1152


def _params(sem=("arbitrary",)):
    return pltpu.CompilerParams(dimension_semantics=sem, vmem_limit_bytes=VMEM_LIMIT)


def _dot(a, b):
    return jnp.dot(a, b, preferred_element_type=F32)


def _dot_nt(a, b):
    return lax.dot_general(a, b, (((1,), (1,)), ((), ())), preferred_element_type=F32)


def _dot_tn(a, b):
    return lax.dot_general(a, b, (((0,), (0,)), ((), ())), preferred_element_type=F32)


def _rms(z):
    return lax.rsqrt(jnp.mean(z * z, axis=-1, keepdims=True) + RMS_EPS)


def _rms_bwd(z, gain, d):
    r = _rms(z)
    n = z * r
    dn = d * gain
    dz = r * (dn - n * jnp.mean(dn * n, axis=-1, keepdims=True))
    return dz, jnp.sum(d * n, axis=0, keepdims=True)


def _gelu(z):
    return 0.5 * z * (1.0 + lax.erf(z * np.float32(1.0 / np.sqrt(2.0))))


def _gelu_grad(z):
    cdf = 0.5 * (1.0 + lax.erf(z * np.float32(1.0 / np.sqrt(2.0))))
    return cdf + z * jnp.exp(-0.5 * z * z) * np.float32(1.0 / np.sqrt(2.0 * np.pi))


def _rot_tables(pos_col, invf, ma, mb):
    ang = pos_col.astype(F32) * invf
    s = jnp.sin(ang)
    return jnp.cos(ang), s * ma, s * mb


def _rot(t, c, sa, sb):
    return t * c + pltpu.roll(t, 120, 1) * sa + pltpu.roll(t, 8, 1) * sb


def _rot_t(d, c, sa, sb):
    return d * c + pltpu.roll(d * sa, 8, 1) + pltpu.roll(d * sb, 120, 1)


def _rot_consts():
    lane = np.arange(128) % HEAD_DIM
    inv_freq = (np.float32(ROPE_THETA) ** (-np.arange(0, ROT_DIM, 2, dtype=np.float32) / np.float32(ROT_DIM))).astype(np.float32)
    invf = np.where(lane < ROT_DIM, inv_freq[lane % ROT_HALF], 0.0).astype(np.float32)
    ma = np.where(lane < ROT_HALF, -1.0, 0.0).astype(np.float32)
    mb = np.where((lane >= ROT_HALF) & (lane < ROT_DIM), 1.0, 0.0).astype(np.float32)
    return jnp.asarray(invf[None]), jnp.asarray(ma[None]), jnp.asarray(mb[None])


def _row_spec(tm, w):
    return pl.BlockSpec((tm, w), lambda i: (i, 0))


def _full_spec(shape):
    return pl.BlockSpec(shape, lambda i: (0,) * len(shape))


def in_proj(x, pos_col, g1, w_in_t, rot, exchanges=()):
    tm = 512

    def body(x_ref, pos_ref, g_ref, w_ref, invf_ref, ma_ref, mb_ref, h_ref, q_ref, k_ref, v_ref, u_ref, vs_ref):
        xf = x_ref[...]
        h = (xf * _rms(xf) * g_ref[...]).astype(BF16)
        h_ref[...] = h
        proj = _dot_nt(h, w_ref[...])
        c, sa, sb = _rot_tables(pos_ref[...], invf_ref[...], ma_ref[...], mb_ref[...])
        for j in range(ATTN_W // 128):
            q_ref[:, j * 128:(j + 1) * 128] = _rot(proj[:, j * 128:(j + 1) * 128], c, sa, sb) * Q_SCALE
            k_ref[:, j * 128:(j + 1) * 128] = _rot(proj[:, ATTN_W + j * 128:ATTN_W + (j + 1) * 128], c, sa, sb)
        v_ref[...] = proj[:, 2 * ATTN_W:3 * ATTN_W]
        u_ref[...] = proj[:, 3 * ATTN_W:3 * ATTN_W + SGU_W]
        vs_ref[...] = proj[:, 3 * ATTN_W + SGU_W:]

    act = jax.ShapeDtypeStruct((SEQ, 512), F32)
    return _hosted(
        "in_proj", body, SEQ // tm,
        [_row_spec(tm, D_MODEL), _row_spec(tm, 1), _full_spec((1, D_MODEL)), _full_spec((IN_W, D_MODEL)),
         _full_spec((1, 128)), _full_spec((1, 128)), _full_spec((1, 128))],
        [_row_spec(tm, D_MODEL)] + [_row_spec(tm, 512)] * 5,
        [jax.ShapeDtypeStruct((SEQ, D_MODEL), BF16)] + [act] * 5,
        (x, pos_col, g1, w_in_t, *rot), exchanges=exchanges)


RES = 16


def _to_residue_order(t):
    return t.reshape(SEQ // RES, RES, -1).transpose(1, 0, 2).reshape(t.shape)


def _from_residue_order(t):
    return t.reshape(RES, SEQ // RES, -1).transpose(1, 0, 2).reshape(t.shape)


def _block_rows(d, r, n):
    if d == 16:
        slices = [(128 * r, 128)]
    elif d == 4:
        slices = [(128 * (4 * b + r) + 32 * n, 32) for b in range(4)]
    else:
        slices = [(128 * b + 8 * n, 8) for b in range(RES)]
    return [(s if isinstance(s, int) else pl.multiple_of(s, z), z) for s, z in slices]


def _block_step(d, i):
    if d == 16:
        return i
    if d == 4:
        return 4 * (i & 31) + (i >> 5)
    return 16 * (i & 7) + (i >> 3)


def _attn_masks(d):
    row2 = _block_step(d, lax.broadcasted_iota(jnp.int32, (128, 256), 0))
    col2 = lax.broadcasted_iota(jnp.int32, (128, 256), 1)
    key2 = _block_step(d, col2 & 127)
    mask2 = jnp.logical_or(jnp.logical_and(col2 < 128, key2 >= row2), jnp.logical_and(col2 >= 128, key2 <= row2))
    row1 = _block_step(d, lax.broadcasted_iota(jnp.int32, (128, 128), 0))
    col1 = lax.broadcasted_iota(jnp.int32, (128, 128), 1)
    return col1 < HEAD_DIM, _block_step(d, col1) <= row1, mask2


def _load_rows(ref, slices):
    parts = [ref[pl.ds(s, z), :] for s, z in slices]
    return parts[0] if len(parts) == 1 else jnp.concatenate(parts, axis=0)


def _for_each_group(fn):
    for p, d in enumerate(DILATIONS):
        masks = _attn_masks(d)
        if d == 16:
            def group(i, carry, p=p, masks=masks):
                fn(p, masks, [(_block_rows(16, 4 * i + g, 0), None) for g in range(4)])
                return carry

            lax.fori_loop(0, 4, group, 0)
        elif d == 4:
            fn(p, masks, [(_block_rows(4, r, 0), None) for r in range(4)])

            def group(i, carry, p=p, masks=masks):
                fn(p, masks, [(_block_rows(4, r, i + 1), _block_rows(4, r, i)) for r in range(4)])
                return carry

            lax.fori_loop(0, 3, group, 0)
        else:
            fn(p, masks, [(_block_rows(1, 0, 0), None)])

            def group(i, carry, p=p, masks=masks):
                fn(p, masks, [(_block_rows(1, 0, 3 * i + g + 1), _block_rows(1, 0, 3 * i + g)) for g in range(3)])
                return carry

            lax.fori_loop(0, 5, group, 0)


def attn_fwd(q, k, v, exchanges=()):
    def body(q_ref, k_ref, v_ref, o_ref, lse_ref, op_ref, lp_ref):
        def group(p, masks, blocks):
            head0, mask1, mask2 = masks
            heads = (head0, jnp.logical_not(head0))
            keys = [rows if prev is None else prev + rows for rows, prev in blocks]
            mask = [mask1 if prev is None else mask2 for _, prev in blocks]
            qb = [_load_rows(q_ref, rows) for rows, _ in blocks]
            kk = [_load_rows(k_ref, ks).astype(BF16) for ks in keys]
            vv = [_load_rows(v_ref, ks).astype(BF16) for ks in keys]
            chains = [(g, hm) for g in range(len(blocks)) for hm in heads]
            s = [jnp.where(mask[g], _dot_nt(jnp.where(hm, qb[g], 0.0).astype(BF16), kk[g]), NEG) for g, hm in chains]
            m = [jnp.max(t, axis=-1, keepdims=True) for t in s]
            e = [jnp.exp(t - mt) for t, mt in zip(s, m)]
            l = [jnp.sum(t, axis=-1, keepdims=True) for t in e]
            pv = [_dot(t.astype(BF16), vv[g]) for t, (g, _) in zip(e, chains)]
            for g, (rows, _) in enumerate(blocks):
                o_blk = jnp.where(head0, pv[2 * g] / l[2 * g], pv[2 * g + 1] / l[2 * g + 1])
                l_blk = jnp.where(head0, jnp.broadcast_to(m[2 * g] + jnp.log(l[2 * g]), (128, 128)),
                                  jnp.broadcast_to(m[2 * g + 1] + jnp.log(l[2 * g + 1]), (128, 128)))
                at = 0
                for start, size in rows:
                    op_ref[p, pl.ds(start, size), :] = o_blk[at:at + size]
                    lp_ref[p, pl.ds(start, size), :] = l_blk[at:at + size]
                    at += size

        _for_each_group(group)

        def combine(i, carry):
            rows = pl.ds(pl.multiple_of(i * 256, 256), 256)
            ls = [lp_ref[p, rows, :] for p in range(3)]
            m = jnp.maximum(jnp.maximum(ls[0], ls[1]), ls[2])
            lse = m + jnp.log(jnp.exp(ls[0] - m) + jnp.exp(ls[1] - m) + jnp.exp(ls[2] - m))
            o = jnp.zeros((256, 128), F32)
            for p in range(3):
                o = o + jnp.exp(ls[p] - lse) * op_ref[p, rows, :]
            o_ref[rows, :] = o
            lse_ref[rows, :] = lse
            return carry

        lax.fori_loop(0, SEQ // 256, combine, 0)

    slab = pl.BlockSpec((SEQ, 128), lambda i: (0, i))
    out = jax.ShapeDtypeStruct((SEQ, ATTN_W), F32)
    return _hosted(
        "attn_fwd", body, ATTN_W // 128, [slab] * 3, [slab] * 2, [out, out], (q, k, v),
        scratch_shapes=[pltpu.VMEM((3, SEQ, 128), F32), pltpu.VMEM((3, SEQ, 128), F32)], exchanges=exchanges)


def _causal_weights(w_ref):
    row = lax.broadcasted_iota(jnp.int32, (CHUNK, CHUNK), 0)
    col = lax.broadcasted_iota(jnp.int32, (CHUNK, CHUNK), 1)
    return [jnp.where(col <= row, w_ref[g], 0.0).astype(BF16) for g in range(N_GROUPS)], col <= row


def _sgu_chunk_fwd(u, vs, lg, lb, wc, bfull, head0):
    ug = _gelu(u)
    vg = _gelu(vs)
    xc = vg - jnp.mean(vg, axis=-1, keepdims=True)
    rstd = lax.rsqrt(jnp.mean(xc * xc, axis=-1, keepdims=True) + LN_EPS)
    xhat = xc * rstd
    vn = xhat * lg + lb
    mixed = []
    for gp in range(SGU_W // 128):
        vp = vn[:, gp * 128:(gp + 1) * 128].astype(BF16)
        mixed.append(jnp.where(head0, _dot(wc[2 * gp], vp), _dot(wc[2 * gp + 1], vp)))
    ms = jnp.concatenate(mixed, axis=1) + bfull
    return ug, xhat, rstd, vn, ms


def sgu_fwd(u, vs, lg, lb, w_sp, bfull, exchanges=()):
    cpb = 4

    def body(u_ref, vs_ref, lg_ref, lb_ref, w_ref, b_ref, o_ref):
        wc, _ = _causal_weights(w_ref)
        head0 = lax.broadcasted_iota(jnp.int32, (CHUNK, 128), 1) < HEAD_DIM
        for ci in range(cpb):
            rows = pl.ds(ci * CHUNK, CHUNK)
            ug, _, _, _, ms = _sgu_chunk_fwd(u_ref[rows, :], vs_ref[rows, :], lg_ref[...], lb_ref[...], wc, b_ref[...], head0)
            o_ref[rows, :] = ug * ms

    tm = cpb * CHUNK
    return _hosted(
        "sgu_fwd", body, SEQ // tm,
        [_row_spec(tm, SGU_W), _row_spec(tm, SGU_W), _full_spec((1, SGU_W)), _full_spec((1, SGU_W)),
         _full_spec((N_GROUPS, CHUNK, CHUNK)), _full_spec((CHUNK, SGU_W))],
        [_row_spec(tm, SGU_W)], [jax.ShapeDtypeStruct((SEQ, SGU_W), F32)],
        (u, vs, lg, lb, w_sp, bfull), exchanges=exchanges)


def out_proj(attn, sgu, x, ga, gs, w_out, gpm, gpf, exchanges=()):
    tm = 512

    def body(a_ref, s_ref, x_ref, ga_ref, gs_ref, w_ref, gpm_ref, gpf_ref, mix_ref, y_ref, x2_ref, h2_ref):
        a = a_ref[...]
        s = s_ref[...]
        an = (a * _rms(a) * ga_ref[...]).astype(BF16)
        sn = (s * _rms(s) * gs_ref[...]).astype(BF16)
        mix_ref[:, :ATTN_W] = an
        mix_ref[:, ATTN_W:] = sn
        y = _dot(an, w_ref[:ATTN_W, :]) + _dot(sn, w_ref[ATTN_W:, :])
        y_ref[...] = y
        x2 = x_ref[...] + y * _rms(y) * gpm_ref[...]
        x2_ref[...] = x2
        h2_ref[...] = (x2 * _rms(x2) * gpf_ref[...]).astype(BF16)

    wide = jax.ShapeDtypeStruct((SEQ, D_MODEL), F32)
    wide16 = jax.ShapeDtypeStruct((SEQ, D_MODEL), BF16)
    return _hosted(
        "out_proj", body, SEQ // tm,
        [_row_spec(tm, ATTN_W), _row_spec(tm, SGU_W), _row_spec(tm, D_MODEL), _full_spec((1, ATTN_W)),
         _full_spec((1, SGU_W)), _full_spec((D_MODEL, D_MODEL)), _full_spec((1, D_MODEL)), _full_spec((1, D_MODEL))],
        [_row_spec(tm, D_MODEL)] * 4, [wide16, wide, wide, wide16],
        (attn, sgu, x, ga, gs, w_out, gpm, gpf), exchanges=exchanges)


def ffn_up(h2, w_gate_t, w_up_t, exchanges=()):
    tm = 256

    def body(h_ref, wg_ref, wu_ref, g_ref, u_ref, a_ref):
        h = h_ref[...]
        g = _dot_nt(h, wg_ref[...])
        u = _dot_nt(h, wu_ref[...])
        g_ref[...] = g.astype(BF16)
        u_ref[...] = u.astype(BF16)
        a_ref[...] = (g * jax.nn.sigmoid(g) * u).astype(BF16)

    ff = jax.ShapeDtypeStruct((SEQ, D_FF), BF16)
    return _hosted(
        "ffn_up", body, SEQ // tm,
        [_row_spec(tm, D_MODEL), _full_spec((D_FF, D_MODEL)), _full_spec((D_FF, D_MODEL))],
        [_row_spec(tm, D_FF)] * 3, [ff, ff, jax.ShapeDtypeStruct((SEQ, D_FF), BF16)],
        (h2, w_gate_t, w_up_t), exchanges=exchanges)


def ffn_down_loss(act, w_down, x2, gpo, target):
    tm = 512

    def body(a_ref, w_ref, x2_ref, g_ref, t_ref, df_ref, dx3_ref, dg_ref, loss_ref):
        f = _dot(a_ref[...], w_ref[...])
        gain = g_ref[...]
        err = x2_ref[...] + f * _rms(f) * gain - t_ref[...]
        dx3 = err * np.float32(1.0 / D_MODEL)
        dx3_ref[...] = dx3
        df, dg = _rms_bwd(f, gain, dx3)
        df_ref[...] = df.astype(BF16)

        @pl.when(pl.program_id(0) == 0)
        def _():
            dg_ref[...] = jnp.zeros_like(dg_ref)
            loss_ref[...] = jnp.zeros_like(loss_ref)

        dg_ref[...] += dg
        loss_ref[...] += jnp.sum(err * err, axis=(0, 1), keepdims=True)

    return pl.pallas_call(
        body, name="ffn_down_loss", grid=(SEQ // tm,),
        in_specs=[_row_spec(tm, D_FF), _full_spec((D_FF, D_MODEL)), _row_spec(tm, D_MODEL), _full_spec((1, D_MODEL)),
                  _row_spec(tm, D_MODEL)],
        out_specs=[_row_spec(tm, D_MODEL), _row_spec(tm, D_MODEL), _full_spec((1, D_MODEL)), _full_spec((1, 1))],
        out_shape=[jax.ShapeDtypeStruct((SEQ, D_MODEL), BF16), jax.ShapeDtypeStruct((SEQ, D_MODEL), F32),
                   jax.ShapeDtypeStruct((1, D_MODEL), F32), jax.ShapeDtypeStruct((1, 1), F32)],
        compiler_params=_params(),
    )(act, w_down, x2, gpo, target)


def ffn_act_bwd(df, w_down, gate, up, exchanges=()):
    tm = 256

    def body(df_ref, w_ref, g_ref, u_ref, dg_ref, du_ref):
        dact = _dot_nt(df_ref[...], w_ref[...])
        g = g_ref[...].astype(F32)
        s = jax.nn.sigmoid(g)
        du_ref[...] = (dact * g * s).astype(BF16)
        dg_ref[...] = (dact * u_ref[...].astype(F32) * (s * (1.0 + g * (1.0 - s)))).astype(BF16)

    ff16 = jax.ShapeDtypeStruct((SEQ, D_FF), BF16)
    return _hosted(
        "ffn_act_bwd", body, SEQ // tm,
        [_row_spec(tm, D_MODEL), _full_spec((D_FF, D_MODEL)), _row_spec(tm, D_FF), _row_spec(tm, D_FF)],
        [_row_spec(tm, D_FF)] * 2, [ff16, ff16], (df, w_down, gate, up), exchanges=exchanges)


def ffn_in_bwd(dgate, dup, w_gate_t, w_up_t, x2, gpf, dx3, y, gpm, exchanges=()):
    tm = 256

    def body(dg_ref, du_ref, wg_ref, wu_ref, x2_ref, gpf_ref, dx3_ref, y_ref, gpm_ref, dx2_ref, dy_ref, dgpf_ref, dgpm_ref):
        dh2 = _dot(dg_ref[...], wg_ref[...]) + _dot(du_ref[...], wu_ref[...])
        dz, dgpf = _rms_bwd(x2_ref[...], gpf_ref[...], dh2)
        dx2 = dx3_ref[...] + dz
        dx2_ref[...] = dx2
        dy, dgpm = _rms_bwd(y_ref[...], gpm_ref[...], dx2)
        dy_ref[...] = dy.astype(BF16)

        @pl.when(pl.program_id(0) == 0)
        def _():
            dgpf_ref[...] = jnp.zeros_like(dgpf_ref)
            dgpm_ref[...] = jnp.zeros_like(dgpm_ref)

        dgpf_ref[...] += dgpf
        dgpm_ref[...] += dgpm

    vec = jax.ShapeDtypeStruct((1, D_MODEL), F32)
    return _hosted(
        "ffn_in_bwd", body, SEQ // tm,
        [_row_spec(tm, D_FF), _row_spec(tm, D_FF), _full_spec((D_FF, D_MODEL)), _full_spec((D_FF, D_MODEL)),
         _row_spec(tm, D_MODEL), _full_spec((1, D_MODEL)), _row_spec(tm, D_MODEL), _row_spec(tm, D_MODEL),
         _full_spec((1, D_MODEL))],
        [_row_spec(tm, D_MODEL), _row_spec(tm, D_MODEL), _full_spec((1, D_MODEL)), _full_spec((1, D_MODEL))],
        [jax.ShapeDtypeStruct((SEQ, D_MODEL), F32), jax.ShapeDtypeStruct((SEQ, D_MODEL), BF16), vec, vec],
        (dgate, dup, w_gate_t, w_up_t, x2, gpf, dx3, y, gpm), exchanges=exchanges)


def weight_grad(name, a, b, exchanges=()):
    m, n = a.shape[1], b.shape[1]
    tr = 256

    def body(a_ref, b_ref, o_ref):
        o_ref[...] = _dot_tn(a_ref[...], b_ref[...]).astype(BF16)

    (out,), done = _hosted(
        name, body, m // tr, [pl.BlockSpec((SEQ, tr), lambda i: (0, i)), _full_spec((SEQ, n))],
        [_row_spec(tr, n)], [jax.ShapeDtypeStruct((m, n), BF16)], (a, b), exchanges=exchanges)
    return out.reshape(N_DEV, m // N_DEV, n), done


def mix_bwd(dy, w_out, attn, sgu, ga, gs, exchanges=()):
    tm = 512

    def body(dy_ref, w_ref, a_ref, s_ref, ga_ref, gs_ref, da_ref, ds_ref, dga_ref, dgs_ref):
        dy = dy_ref[...]
        da, dga = _rms_bwd(a_ref[...], ga_ref[...], _dot_nt(dy, w_ref[:ATTN_W, :]))
        ds, dgs = _rms_bwd(s_ref[...], gs_ref[...], _dot_nt(dy, w_ref[ATTN_W:, :]))
        da_ref[...] = da
        ds_ref[...] = ds

        @pl.when(pl.program_id(0) == 0)
        def _():
            dga_ref[...] = jnp.zeros_like(dga_ref)
            dgs_ref[...] = jnp.zeros_like(dgs_ref)

        dga_ref[...] += dga
        dgs_ref[...] += dgs

    half = jax.ShapeDtypeStruct((SEQ, 512), F32)
    vec = jax.ShapeDtypeStruct((1, 512), F32)
    return _hosted(
        "mix_bwd", body, SEQ // tm,
        [_row_spec(tm, D_MODEL), _full_spec((D_MODEL, D_MODEL)), _row_spec(tm, 512), _row_spec(tm, 512),
         _full_spec((1, 512)), _full_spec((1, 512))],
        [_row_spec(tm, 512), _row_spec(tm, 512), _full_spec((1, 512)), _full_spec((1, 512))],
        [half, half, vec, vec], (dy, w_out, attn, sgu, ga, gs), exchanges=exchanges)


def sgu_bwd(u, vs, dsgu, lg, lb, w_sp, bfull, exchanges=()):
    cpb = 4

    def body(u_ref, vs_ref, d_ref, lg_ref, lb_ref, w_ref, b_ref, du_ref, dvs_ref, dlg_ref, dlb_ref, dw_ref, db_ref):
        wc, causal = _causal_weights(w_ref)
        head0 = lax.broadcasted_iota(jnp.int32, (CHUNK, 128), 1) < HEAD_DIM
        lg = lg_ref[...]

        @pl.when(pl.program_id(0) == 0)
        def _():
            dlg_ref[...] = jnp.zeros_like(dlg_ref)
            dlb_ref[...] = jnp.zeros_like(dlb_ref)
            dw_ref[...] = jnp.zeros_like(dw_ref)
            db_ref[...] = jnp.zeros_like(db_ref)

        for ci in range(cpb):
            rows = pl.ds(ci * CHUNK, CHUNK)
            u = u_ref[rows, :]
            vs = vs_ref[rows, :]
            d = d_ref[rows, :]
            ug, xhat, rstd, vn, ms = _sgu_chunk_fwd(u, vs, lg, lb_ref[...], wc, b_ref[...], head0)
            du_ref[rows, :] = (d * ms * _gelu_grad(u)).astype(BF16)
            dms = d * ug
            db_ref[...] += dms
            dvn = []
            for gp in range(SGU_W // 128):
                dmp = dms[:, gp * 128:(gp + 1) * 128]
                dm0 = jnp.where(head0, dmp, 0.0).astype(BF16)
                dm1 = jnp.where(head0, 0.0, dmp).astype(BF16)
                vp = vn[:, gp * 128:(gp + 1) * 128].astype(BF16)
                dw_ref[2 * gp] += _dot_nt(dm0, vp)
                dw_ref[2 * gp + 1] += _dot_nt(dm1, vp)
                dvn.append(_dot_tn(wc[2 * gp], dm0) + _dot_tn(wc[2 * gp + 1], dm1))
            dvn = jnp.concatenate(dvn, axis=1)
            dlg_ref[...] += jnp.sum(dvn * xhat, axis=0, keepdims=True)
            dlb_ref[...] += jnp.sum(dvn, axis=0, keepdims=True)
            dxh = dvn * lg
            dvg = rstd * (dxh - jnp.mean(dxh, axis=-1, keepdims=True) - xhat * jnp.mean(dxh * xhat, axis=-1, keepdims=True))
            dvs_ref[rows, :] = (dvg * _gelu_grad(vs)).astype(BF16)

        @pl.when(pl.program_id(0) == pl.num_programs(0) - 1)
        def _():
            for g in range(N_GROUPS):
                dw_ref[g] = jnp.where(causal, dw_ref[g], 0.0)

    tm = cpb * CHUNK
    half16 = jax.ShapeDtypeStruct((SEQ, SGU_W), BF16)
    vec = jax.ShapeDtypeStruct((1, SGU_W), F32)
    return _hosted(
        "sgu_bwd", body, SEQ // tm,
        [_row_spec(tm, SGU_W)] * 3 + [_full_spec((1, SGU_W)), _full_spec((1, SGU_W)),
                                      _full_spec((N_GROUPS, CHUNK, CHUNK)), _full_spec((CHUNK, SGU_W))],
        [_row_spec(tm, SGU_W), _row_spec(tm, SGU_W), _full_spec((1, SGU_W)), _full_spec((1, SGU_W)),
         _full_spec((N_GROUPS, CHUNK, CHUNK)), _full_spec((CHUNK, SGU_W))],
        [half16, half16, vec, vec, jax.ShapeDtypeStruct((N_GROUPS, CHUNK, CHUNK), F32),
         jax.ShapeDtypeStruct((CHUNK, SGU_W), F32)],
        (u, vs, dsgu, lg, lb, w_sp, bfull), exchanges=exchanges)


def attn_bwd(q, k, v, o, lse, do, pos_col, rot, exchanges=()):
    def body(q_ref, k_ref, v_ref, o_ref, lse_ref, do_ref, pos_ref, invf_ref, ma_ref, mb_ref,
             dq_ref, dk_ref, dv_ref, dqa_ref, dka_ref, dva_ref, dlt_ref, rot_ref):
        dqa_ref[...] = jnp.zeros_like(dqa_ref)
        dka_ref[...] = jnp.zeros_like(dka_ref)
        dva_ref[...] = jnp.zeros_like(dva_ref)

        def delta(i, carry):
            rows = pl.ds(pl.multiple_of(i * 256, 256), 256)
            prod = do_ref[rows, :] * o_ref[rows, :]
            h0 = lax.broadcasted_iota(jnp.int32, (256, 128), 1) < HEAD_DIM
            d0 = jnp.sum(jnp.where(h0, prod, 0.0), axis=-1, keepdims=True)
            d1 = jnp.sum(jnp.where(h0, 0.0, prod), axis=-1, keepdims=True)
            dlt_ref[rows, :] = jnp.where(h0, d0, d1)
            return carry

        lax.fori_loop(0, SEQ // 256, delta, 0)

        def add_rows(ref, slices, val):
            at = 0
            for start, size in slices:
                ref[pl.ds(start, size), :] += val[at:at + size]
                at += size

        def group(p, masks, blocks):
            head0, mask1, mask2 = masks
            heads = (head0, jnp.logical_not(head0))
            keys = [rows if prev is None else prev + rows for rows, prev in blocks]
            mask = [mask1 if prev is None else mask2 for _, prev in blocks]
            kk = [_load_rows(k_ref, ks).astype(BF16) for ks in keys]
            vv = [_load_rows(v_ref, ks).astype(BF16) for ks in keys]
            qb = [_load_rows(q_ref, rows) for rows, _ in blocks]
            dob = [_load_rows(do_ref, rows) for rows, _ in blocks]
            lse_b = [_load_rows(lse_ref, rows) for rows, _ in blocks]
            dlt_b = [_load_rows(dlt_ref, rows) for rows, _ in blocks]
            chains = [(g, h) for g in range(len(blocks)) for h in range(2)]
            qm = [jnp.where(heads[h], qb[g], 0.0).astype(BF16) for g, h in chains]
            dom = [jnp.where(heads[h], dob[g], 0.0).astype(BF16) for g, h in chains]
            s = [_dot_nt(qm[c], kk[g]) for c, (g, h) in enumerate(chains)]
            dp = [_dot_nt(dom[c], vv[g]) for c, (g, h) in enumerate(chains)]
            pr = [jnp.where(mask[g], jnp.exp(s[c] - lse_b[g][:, h * HEAD_DIM:h * HEAD_DIM + 1]), 0.0)
                  for c, (g, h) in enumerate(chains)]
            ds = [(pr[c] * (dp[c] - dlt_b[g][:, h * HEAD_DIM:h * HEAD_DIM + 1])).astype(BF16)
                  for c, (g, h) in enumerate(chains)]
            dv = [_dot_tn(pr[c].astype(BF16), dom[c]) for c in range(len(chains))]
            dk = [_dot_tn(ds[c], qm[c]) for c in range(len(chains))]
            dq = [_dot(ds[c], kk[g]) for c, (g, h) in enumerate(chains)]
            for g, (rows, _) in enumerate(blocks):
                add_rows(dqa_ref, rows, jnp.where(head0, dq[2 * g], dq[2 * g + 1]))
                add_rows(dka_ref, keys[g], dk[2 * g] + dk[2 * g + 1])
                add_rows(dva_ref, keys[g], dv[2 * g] + dv[2 * g + 1])

        _for_each_group(group)

        @pl.when(pl.program_id(0) == 0)
        def _():
            def tables(i, carry):
                rows = pl.ds(pl.multiple_of(i * 256, 256), 256)
                c, sa, sb = _rot_tables(pos_ref[rows, :], invf_ref[...], ma_ref[...], mb_ref[...])
                rot_ref[0, rows, :] = c
                rot_ref[1, rows, :] = sa
                rot_ref[2, rows, :] = sb
                return carry

            lax.fori_loop(0, SEQ // 256, tables, 0)

        def finish(i, carry):
            rows = pl.ds(pl.multiple_of(i * 256, 256), 256)
            c, sa, sb = rot_ref[0, rows, :], rot_ref[1, rows, :], rot_ref[2, rows, :]
            dq_ref[rows, :] = _rot_t(dqa_ref[rows, :] * Q_SCALE, c, sa, sb).astype(BF16)
            dk_ref[rows, :] = _rot_t(dka_ref[rows, :], c, sa, sb).astype(BF16)
            dv_ref[rows, :] = dva_ref[rows, :].astype(BF16)
            return carry

        lax.fori_loop(0, SEQ // 256, finish, 0)

    slab = pl.BlockSpec((SEQ, 128), lambda i: (0, i))
    out = jax.ShapeDtypeStruct((SEQ, ATTN_W), BF16)
    acc = pltpu.VMEM((SEQ, 128), F32)
    return _hosted(
        "attn_bwd", body, ATTN_W // 128,
        [slab] * 6 + [_full_spec((SEQ, 1)), _full_spec((1, 128)), _full_spec((1, 128)), _full_spec((1, 128))],
        [slab] * 3, [out, out, out], (q, k, v, o, lse, do, pos_col, *rot),
        scratch_shapes=[acc, acc, acc, acc, pltpu.VMEM((3, SEQ, 128), F32)], exchanges=exchanges)


def in_bwd(dproj, w_in_t, x, g1, dx2, exchanges=()):
    tm = 512

    def body(dp_ref, w_ref, x_ref, g_ref, dx2_ref, dx_ref, dg_ref):
        dh1 = _dot(dp_ref[...], w_ref[...])
        dz, dg = _rms_bwd(x_ref[...], g_ref[...], dh1)
        dx_ref[...] = dx2_ref[...] + dz

        @pl.when(pl.program_id(0) == 0)
        def _():
            dg_ref[...] = jnp.zeros_like(dg_ref)

        dg_ref[...] += dg

    return _hosted(
        "in_bwd", body, SEQ // tm,
        [_row_spec(tm, IN_W), _full_spec((IN_W, D_MODEL)), _row_spec(tm, D_MODEL), _full_spec((1, D_MODEL)),
         _row_spec(tm, D_MODEL)],
        [_row_spec(tm, D_MODEL), _full_spec((1, D_MODEL))],
        [jax.ShapeDtypeStruct((SEQ, D_MODEL), F32), jax.ShapeDtypeStruct((1, D_MODEL), F32)],
        (dproj, w_in_t, x, g1, dx2), exchanges=exchanges)


def _coords():
    return lax.axis_index("x"), lax.axis_index("y"), lax.axis_index("c")


class Exchange:
    def __init__(self, srcs, bufs, new_shapes, n_sems, make, mid_step=None):
        self.srcs, self.bufs, self.new_shapes, self.n_sems, self.make = list(srcs), list(bufs), list(new_shapes), n_sems, make
        self.mid_step = mid_step


def _hosted(name, body, n_steps, in_specs, out_specs, out_shape, args, scratch_shapes=(), exchanges=(), prefetch=None):
    out_shape, out_specs = list(out_shape), list(out_specs)
    srcs = [a for ex in exchanges for a in ex.srcs]
    bufs = [a for ex in exchanges for a in ex.bufs]
    news = [s for ex in exchanges for s in ex.new_shapes]
    n_pre = 0 if prefetch is None else 1
    n_in, n_out, n_scr = len(args), len(out_shape), len(scratch_shapes)

    def wrapped(*refs):
        refs = list(refs)
        pre, refs = refs[:n_pre], refs[n_pre:]
        ins = refs[:n_in]
        src_refs = refs[n_in:n_in + len(srcs)]
        at = n_in + len(srcs) + len(bufs)
        outs = refs[at:at + n_out]
        buf_refs = refs[at + n_out:at + n_out + len(bufs)]
        new_refs = refs[at + n_out + len(bufs):at + n_out + len(bufs) + len(news)]
        at += n_out + len(bufs) + len(news)
        scratch, sems = refs[at:at + n_scr], refs[at + n_scr:]

        def copies(phase, which):
            made, si, bi, ni = [], 0, 0, 0
            for k, ex in enumerate(exchanges):
                if which(ex):
                    made.append(ex.make(phase, src_refs[si:si + len(ex.srcs)], buf_refs[bi:bi + len(ex.bufs)],
                                        new_refs[ni:ni + len(ex.new_shapes)], sems[2 * k], sems[2 * k + 1]))
                si, bi, ni = si + len(ex.srcs), bi + len(ex.bufs), ni + len(ex.new_shapes)
            return made

        if exchanges:
            @pl.when(pl.program_id(0) == 0)
            def _():
                for starts in copies("start", lambda ex: True):
                    for cp in starts:
                        cp.start()

        def pass_on(mid):
            for arrivals, starts in copies("middle", lambda ex: ex.mid_step == mid):
                for cp in arrivals:
                    cp.wait_recv()
                for cp in starts:
                    cp.start()

        for mid in sorted({ex.mid_step for ex in exchanges if isinstance(ex.mid_step, int)}):
            @pl.when(pl.program_id(0) == mid)
            def _(mid=mid):
                pass_on(mid)

        body(*pre, *ins, *outs, *scratch)

        if exchanges:
            @pl.when(pl.program_id(0) == n_steps - 1)
            def _():
                pass_on("end")
                for sends, recvs in copies("end", lambda ex: True):
                    for cp in sends:
                        cp.wait_send()
                    for cp in recvs:
                        cp.wait_recv()

    sem_shapes = []
    for ex in exchanges:
        sem_shapes += [pltpu.SemaphoreType.DMA((ex.n_sems,)), pltpu.SemaphoreType.DMA((ex.n_sems,))]
    all_in = list(in_specs) + [ANY] * (len(srcs) + len(bufs))
    all_out = out_specs + [ANY] * (len(bufs) + len(news))
    all_shape = out_shape + [jax.ShapeDtypeStruct(b.shape, b.dtype) for b in bufs] + news
    all_scratch = list(scratch_shapes) + sem_shapes
    aliases = {n_pre + n_in + len(srcs) + i: n_out + i for i in range(len(bufs))}
    if prefetch is None:
        call = pl.pallas_call(wrapped, name=name, grid=(n_steps,), in_specs=all_in, out_specs=all_out, out_shape=all_shape,
                              scratch_shapes=all_scratch, input_output_aliases=aliases, compiler_params=_params())
        outs = call(*args, *srcs, *bufs)
    else:
        spec = pltpu.PrefetchScalarGridSpec(num_scalar_prefetch=1, grid=(n_steps,), in_specs=all_in, out_specs=all_out,
                                            scratch_shapes=all_scratch)
        call = pl.pallas_call(wrapped, name=name, grid_spec=spec, out_shape=all_shape, input_output_aliases=aliases,
                              compiler_params=_params())
        outs = call(prefetch, *args, *srcs, *bufs)
    results, bi, ni = [], n_out, n_out + len(bufs)
    for ex in exchanges:
        results.append((list(outs[bi:bi + len(ex.bufs)]), list(outs[ni:ni + len(ex.new_shapes)])))
        bi, ni = bi + len(ex.bufs), ni + len(ex.new_shapes)
    return list(outs[:n_out]), results


def _gather_copies(kinds, bufs, ranges, send_sems, recv_sems):
    x, y, c = _coords()
    me, sibling = (x, y, c), (x, y, 1 - c)
    chips = [(1 - x, y), (x, 1 - y), (1 - x, 1 - y)]

    def copy(a, k, block, to):
        lo, hi = ranges[a]
        r = bufs[a].shape[0] // N_DEV
        rows = bufs[a].at[pl.ds((4 * block[0] + 2 * block[1] + block[2]) * r + lo, hi - lo), :]
        return pltpu.make_async_remote_copy(src_ref=rows, dst_ref=rows, send_sem=send_sems.at[7 * a + k],
                                            recv_sem=recv_sems.at[7 * a + k], device_id=to, device_id_type=MESH)

    every = range(len(bufs))
    make = {
        "out": lambda: [copy(a, 0, me, sibling) for a in every]
        + [copy(a, 1 + j, me, (*chip, c)) for a in every for j, chip in enumerate(chips)],
        "from_core": lambda: [copy(a, 0, sibling, me) for a in every],
        "from_chips": lambda: [copy(a, 1 + j, (*chip, c), me) for a in every for j, chip in enumerate(chips)],
        "on": lambda: [copy(a, 4 + j, (*chip, c), sibling) for a in every for j, chip in enumerate(chips)],
        "on_in": lambda: [copy(a, 4 + j, (*chip, 1 - c), me) for a in every for j, chip in enumerate(chips)],
    }
    return [make[kind]() for kind in kinds]


def gather(bufs, mid_step, ranges=None):
    ranges = ranges or [(0, b.shape[0] // N_DEV) for b in bufs]

    def make(phase, src_refs, buf_refs, new_refs, send_sems, recv_sems):
        kinds = {"start": ["out"], "middle": ["from_chips", "on"], "end": ["out", "on", "from_core", "on_in"]}[phase]
        made = _gather_copies(kinds, buf_refs, ranges, send_sems, recv_sems)
        if phase == "start":
            return made[0]
        if phase == "middle":
            return made[0], made[1]
        return made[0] + made[1], made[2] + made[3]

    return Exchange([], bufs, [], 7 * len(bufs), make, mid_step=mid_step)


def all_gather_in_place(name, bufs):
    n = len(bufs)
    ranges = [(0, b.shape[0] // N_DEV) for b in bufs]

    def body(*refs):
        outs, send_sems, recv_sems = refs[n:2 * n], refs[2 * n], refs[2 * n + 1]
        out, from_core, from_chips, on, on_in = _gather_copies(
            ["out", "from_core", "from_chips", "on", "on_in"], outs, ranges, send_sems, recv_sems)
        for cp in out:
            cp.start()
        for cp in from_chips:
            cp.wait_recv()
        for cp in on:
            cp.start()
        for cp in from_core + on_in:
            cp.wait_recv()
        for cp in out + on:
            cp.wait_send()

    return pl.pallas_call(
        body, name=name, in_specs=[ANY] * n, out_specs=[ANY] * n,
        out_shape=[jax.ShapeDtypeStruct(b.shape, b.dtype) for b in bufs],
        scratch_shapes=[pltpu.SemaphoreType.DMA((7 * n,)), pltpu.SemaphoreType.DMA((7 * n,))],
        input_output_aliases={i: i for i in range(n)},
    )(*bufs)


TO_GATHER = (1, lambda x, y, c: [(x, y, 1 - c), (1 - x, y, c), (x, 1 - y, c), (1 - x, 1 - y, c)])
TO_SIBLING = (2, lambda x, y, c: [(x, y, 1 - c)])
TO_CHIPS = (3, lambda x, y, c: [(1 - x, y, c), (x, 1 - y, c), (1 - x, 1 - y, c)])
TO_ALL = (4, lambda x, y, c: [(x ^ (m >> 2), y ^ ((m >> 1) & 1), c ^ (m & 1)) for m in range(1, N_DEV)])


def by_sequencer(name, exchanges, who):
    collective_id, peers_of = who
    hbm = pltpu.MemorySpace.HBM
    refs = [([jax.new_ref(a, memory_space=hbm) for a in ex.srcs], [jax.new_ref(a, memory_space=hbm) for a in ex.bufs],
             [jax.empty_ref(s, memory_space=hbm) for s in ex.new_shapes]) for ex in exchanges]
    sems = []
    for ex in exchanges:
        sems += [pltpu.SemaphoreType.DMA((ex.n_sems,)), pltpu.SemaphoreType.DMA((ex.n_sems,))]

    @pl.kernel(mesh=plsc.ScalarSubcoreMesh(axis_name="sequencer", num_cores=1), name=name, scratch_types=tuple(sems),
               compiler_params=pltpu.CompilerParams(collective_id=collective_id))
    def launch(*sem_refs):
        peers = peers_of(*_coords())
        barrier = pltpu.get_barrier_semaphore()
        for peer in peers:
            pl.semaphore_signal(barrier, inc=1, device_id=peer, device_id_type=MESH)
        pl.semaphore_wait(barrier, len(peers))

        def make(phase, k):
            return exchanges[k].make(phase, *refs[k], sem_refs[2 * k], sem_refs[2 * k + 1])

        for k in range(len(exchanges)):
            for cp in make("start", k):
                cp.start()
        for k, ex in enumerate(exchanges):
            if ex.mid_step is not None:
                arrivals, starts = make("middle", k)
                for cp in arrivals:
                    cp.wait_recv()
                for cp in starts:
                    cp.start()
        for k in range(len(exchanges)):
            sends, arrivals = make("end", k)
            for cp in arrivals:
                cp.wait_recv()
            for cp in sends:
                cp.wait_send()

    launch()
    return [([ref[...] for ref in bufs], [ref[...] for ref in news]) for _, bufs, news in refs]


def place_shards(shards, dev):
    n = len(shards)

    def body(dev_ref, *refs):
        for a in range(n):
            refs[n + a][...] = refs[a][...].astype(BF16)

    spec = pltpu.PrefetchScalarGridSpec(
        num_scalar_prefetch=1, grid=(1,),
        in_specs=[pl.BlockSpec(s.shape, lambda i, dev_ref: (0, 0)) for s in shards],
        out_specs=[pl.BlockSpec(s.shape, lambda i, dev_ref: (dev_ref[0], 0)) for s in shards])
    return pl.pallas_call(
        body, name="place_shards", grid_spec=spec,
        out_shape=[jax.ShapeDtypeStruct((N_DEV * s.shape[0], s.shape[1]), BF16) for s in shards],
        compiler_params=_params(),
    )(dev, *shards)


def _swap(copies_of):
    def make(phase, src_refs, buf_refs, new_refs, send_sems, recv_sems):
        copies = copies_of(src_refs, new_refs, send_sems, recv_sems)
        return copies if phase == "start" else (copies, copies)

    return make


def to_sibling(grads):
    def copies_of(src_refs, new_refs, send_sems, recv_sems):
        x, y, c = _coords()
        return [pltpu.make_async_remote_copy(
            src_ref=src_refs[a].at[2 * xy + 1 - c], dst_ref=new_refs[a].at[xy], send_sem=send_sems.at[4 * a + xy],
            recv_sem=recv_sems.at[4 * a + xy], device_id=(x, y, 1 - c), device_id_type=MESH)
            for a in range(len(src_refs)) for xy in range(4)]

    return Exchange(grads, [], [jax.ShapeDtypeStruct((4,) + g.shape[1:], g.dtype) for g in grads], 4 * len(grads),
                    _swap(copies_of))


def to_chips(parts):
    def copies_of(src_refs, new_refs, send_sems, recv_sems):
        x, y, c = _coords()
        chips = [(1 - x, y), (x, 1 - y), (1 - x, 1 - y)]
        return [pltpu.make_async_remote_copy(
            src_ref=src_refs[a].at[2 * px + py], dst_ref=new_refs[a].at[2 * x + y], send_sem=send_sems.at[3 * a + j],
            recv_sem=recv_sems.at[3 * a + j], device_id=(px, py, c), device_id_type=MESH)
            for a in range(len(src_refs)) for j, (px, py) in enumerate(chips)]

    return Exchange(parts, [], [jax.ShapeDtypeStruct(p.shape, p.dtype) for p in parts], 3 * len(parts), _swap(copies_of))


def to_owners(grad):
    def copies_of(src_refs, new_refs, send_sems, recv_sems):
        x, y, c = _coords()
        copies = []
        for m in range(1, N_DEV):
            px, py, pc = x ^ (m >> 2), y ^ ((m >> 1) & 1), c ^ (m & 1)
            copies.append(pltpu.make_async_remote_copy(
                src_ref=src_refs[0].at[4 * px + 2 * py + pc], dst_ref=new_refs[0].at[4 * x + 2 * y + c],
                send_sem=send_sems.at[m - 1], recv_sem=recv_sems.at[m - 1], device_id=(px, py, pc), device_id_type=MESH))
        return copies

    return Exchange([grad], [], [jax.ShapeDtypeStruct(grad.shape, grad.dtype)], N_DEV - 1, _swap(copies_of))


def exchange_only(name, exchanges):
    def body():
        pass

    return _hosted(name, body, 1, [], [], [], [], exchanges=exchanges)[1]


def sum_cores(name, grad, other, core):
    _, r, w = other.shape

    def body(core_ref, g_ref, o_ref, out_ref):
        out_ref[...] = (g_ref[...].astype(F32) + o_ref[...].astype(F32)).astype(out_ref.dtype)

    return pl.pallas_call(
        body, name=name,
        grid_spec=pltpu.PrefetchScalarGridSpec(
            num_scalar_prefetch=1, grid=(4,),
            in_specs=[pl.BlockSpec((1, r, w), lambda i, core_ref: (2 * i + core_ref[0], 0, 0)),
                      pl.BlockSpec((1, r, w), lambda i, core_ref: (i, 0, 0))],
            out_specs=pl.BlockSpec((1, r, w), lambda i, core_ref: (i, 0, 0))),
        out_shape=jax.ShapeDtypeStruct(other.shape, other.dtype),
        compiler_params=_params(),
    )(core, grad, other)


def sum_owned(name, grad, others, dev_ids):
    _, r, w = grad.shape

    def body(ids_ref, *refs):
        acc = refs[0][0]
        for k in range(1, N_DEV):
            acc = acc + refs[k][0]
        refs[N_DEV][...] = acc

    def pick(k):
        return pl.BlockSpec((1, r, w), lambda i, ids_ref: (ids_ref[k], 0, 0))

    return pl.pallas_call(
        body, name=name,
        grid_spec=pltpu.PrefetchScalarGridSpec(
            num_scalar_prefetch=1, grid=(1,), in_specs=[pick(k) for k in range(N_DEV)],
            out_specs=pl.BlockSpec((r, w), lambda i, ids_ref: (ids_ref[0], 0))),
        out_shape=jax.ShapeDtypeStruct((N_DEV * r, w), F32),
        compiler_params=_params(),
    )(dev_ids, grad, *([others] * (N_DEV - 1)))


def _adamw_update(w, g, m, v):
    nm = ADAM_B1 * m + np.float32(1.0 - ADAM_B1) * g
    nv = ADAM_B2 * v + np.float32(1.0 - ADAM_B2) * (g * g)
    m_hat = nm / np.float32(1.0 - ADAM_B1 ** ADAM_STEP)
    v_hat = nv / np.float32(1.0 - ADAM_B2 ** ADAM_STEP)
    return -ADAM_LR * (m_hat / (jnp.sqrt(v_hat) + ADAM_EPS) + ADAM_WD * w), nm, nv


def adamw_of_sums(name, part, others, chip_ids, w, m, v, after):
    _, r, wd = part.shape

    def body(ids_ref, p_ref, a_ref, b_ref, c_ref, w_ref, m_ref, v_ref, after_ref, g_ref, d_ref, nm_ref, nv_ref):
        g = ((p_ref[0].astype(F32) + a_ref[0].astype(F32)) + b_ref[0].astype(F32)) + c_ref[0].astype(F32)
        g_ref[...] = g
        d_ref[...], nm_ref[...], nv_ref[...] = _adamw_update(w_ref[...], g, m_ref[...], v_ref[...])

    def pick(k):
        return pl.BlockSpec((1, r, wd), lambda i, ids_ref: (ids_ref[k], 0, 0))

    whole = pl.BlockSpec((r, wd), lambda i, ids_ref: (0, 0))
    shape = jax.ShapeDtypeStruct((r, wd), F32)
    return pl.pallas_call(
        body, name=name,
        grid_spec=pltpu.PrefetchScalarGridSpec(
            num_scalar_prefetch=1, grid=(1,), in_specs=[pick(0), pick(1), pick(2), pick(3), whole, whole, whole, ANY],
            out_specs=[whole] * 4),
        out_shape=[shape] * 4,
        compiler_params=_params(),
    )(chip_ids, part, others, others, others, w, m, v, after)


def adamw(name, w, g, m, v):
    def body(w_ref, g_ref, m_ref, v_ref, d_ref, nm_ref, nv_ref):
        d_ref[...], nm_ref[...], nv_ref[...] = _adamw_update(w_ref[...], g_ref[...], m_ref[...], v_ref[...])

    shape = jax.ShapeDtypeStruct(w.shape, F32)
    spec = _full_spec(w.shape)
    return pl.pallas_call(
        body, name=name, grid=(1,), in_specs=[spec] * 4, out_specs=[spec] * 3, out_shape=[shape] * 3,
        compiler_params=_params(),
    )(w, g, m, v)


def _pack_small(parts):
    flat = jnp.concatenate([parts[name].reshape(-1) for name, _ in SMALL])
    return jnp.pad(flat, (0, SMALL_ROWS * 128 - flat.shape[0])).reshape(SMALL_ROWS, 128)


def _unpack_small(packed, like):
    flat = packed.reshape(-1)
    out, at = {}, 0
    for name, size in SMALL:
        out[name] = flat[at:at + size].reshape(like[name].shape)
        at += size
    return out


def kernel(x, positions, pre_mix_norm, w_in, sgu_ln_gain, sgu_ln_bias, sgu_w_spatial, sgu_b_spatial, attn_out_norm, sgu_out_norm, w_out, post_mix_norm, pre_ffn_norm, w_gate, w_up, w_down, post_ffn_norm, loss_target, m_pre_mix_norm, m_w_in, m_sgu_ln_gain, m_sgu_ln_bias, m_sgu_w_spatial, m_sgu_b_spatial, m_attn_out_norm, m_sgu_out_norm, m_w_out, m_post_mix_norm, m_pre_ffn_norm, m_w_gate, m_w_up, m_w_down, m_post_ffn_norm, v_pre_mix_norm, v_w_in, v_sgu_ln_gain, v_sgu_ln_bias, v_sgu_w_spatial, v_sgu_b_spatial, v_attn_out_norm, v_sgu_out_norm, v_w_out, v_post_mix_norm, v_pre_ffn_norm, v_w_gate, v_w_up, v_w_down, v_post_ffn_norm):
    small_w = dict(pre_mix_norm=pre_mix_norm, sgu_ln_gain=sgu_ln_gain, sgu_ln_bias=sgu_ln_bias, sgu_w_spatial=sgu_w_spatial,
                   sgu_b_spatial=sgu_b_spatial, attn_out_norm=attn_out_norm, sgu_out_norm=sgu_out_norm,
                   post_mix_norm=post_mix_norm, pre_ffn_norm=pre_ffn_norm, post_ffn_norm=post_ffn_norm)
    small_m = dict(pre_mix_norm=m_pre_mix_norm, sgu_ln_gain=m_sgu_ln_gain, sgu_ln_bias=m_sgu_ln_bias, sgu_w_spatial=m_sgu_w_spatial,
                   sgu_b_spatial=m_sgu_b_spatial, attn_out_norm=m_attn_out_norm, sgu_out_norm=m_sgu_out_norm,
                   post_mix_norm=m_post_mix_norm, pre_ffn_norm=m_pre_ffn_norm, post_ffn_norm=m_post_ffn_norm)
    small_v = dict(pre_mix_norm=v_pre_mix_norm, sgu_ln_gain=v_sgu_ln_gain, sgu_ln_bias=v_sgu_ln_bias, sgu_w_spatial=v_sgu_w_spatial,
                   sgu_b_spatial=v_sgu_b_spatial, attn_out_norm=v_attn_out_norm, sgu_out_norm=v_sgu_out_norm,
                   post_mix_norm=v_post_mix_norm, pre_ffn_norm=v_pre_ffn_norm, post_ffn_norm=v_post_ffn_norm)
    for table in (small_w, small_m, small_v):
        table["loss_sum"] = jnp.zeros((1,), F32)

    x2d = x[0]
    target = loss_target[0]
    pos_col = positions.reshape(SEQ, 1)
    rot = _rot_consts()
    w_sp = sgu_w_spatial[0]
    bfull = jnp.repeat(sgu_b_spatial[0].T, HEAD_DIM, axis=1)

    x_i, y_i, c_i = (lax.axis_index(a).astype(jnp.int32) for a in MESH_AXES)
    dev = 4 * x_i + 2 * y_i + c_i
    core = c_i.reshape(1)
    chip = 2 * x_i + y_i
    chip_ids = jnp.stack([chip, chip ^ 1, chip ^ 2, chip ^ 3])
    dev_ids = jnp.stack([dev ^ m for m in range(N_DEV)])

    w_in_t, w_gate_t, w_up_t, w_out_f, w_down_f = place_shards(
        [w_in[0].T, w_gate[0].T, w_up[0].T, w_out[0], w_down[0]], dev.reshape(1))
    def gathered(name, bufs):
        return by_sequencer(name, [gather(bufs, "end")], TO_GATHER)[0][0]

    def from_sibling(name, grads):
        return by_sequencer(name, [to_sibling(grads)], TO_SIBLING)[0][1]

    def from_chips(name, parts):
        return by_sequencer(name, [to_chips(parts)], TO_CHIPS)[0][1]

    (w_in_t,) = gathered("gather_w_in", [w_in_t])
    (w_out_f,) = gathered("gather_w_out", [w_out_f])
    w_gate_t, w_up_t = gathered("gather_w_gate_up", [w_gate_t, w_up_t])
    (w_down_f,) = gathered("gather_w_down", [w_down_f])

    (h1, q, k, v, u, vs), _ = in_proj(x2d, pos_col, pre_mix_norm, w_in_t, rot)
    q, k, v = (_to_residue_order(t) for t in (q, k, v))
    (attn_r, lse), _ = attn_fwd(q, k, v)
    attn = _from_residue_order(attn_r)
    (sgu,), _ = sgu_fwd(u, vs, sgu_ln_gain, sgu_ln_bias, w_sp, bfull)
    (mix, y, x2, h2), _ = out_proj(attn, sgu, x2d, attn_out_norm, sgu_out_norm, w_out_f, post_mix_norm, pre_ffn_norm)
    (gate, up, act), _ = ffn_up(h2, w_gate_t, w_up_t)
    df, dx3, d_post_ffn, sq_err = ffn_down_loss(act, w_down_f, x2, post_ffn_norm, target)

    g_w_down, _ = weight_grad("grad_w_down", act, df)
    (s_down,) = from_sibling("w_down_to_sibling", [g_w_down])
    (dgate, dup), _ = ffn_act_bwd(df, w_down_f, gate, up)
    p_down = sum_cores("sum_cores_down", g_w_down, s_down, core)
    (c_down,) = from_chips("w_down_to_chips", [p_down])
    g_w_gate, _ = weight_grad("grad_w_gate", dgate, h2)
    g_w_up, _ = weight_grad("grad_w_up", dup, h2)
    s_gate, s_up = from_sibling("w_gate_up_to_sibling", [g_w_gate, g_w_up])
    (dx2, dy, d_pre_ffn, d_post_mix), _ = ffn_in_bwd(dgate, dup, w_gate_t, w_up_t, x2, pre_ffn_norm, dx3, y, post_mix_norm)
    p_gate = sum_cores("sum_cores_gate", g_w_gate, s_gate, core)
    p_up = sum_cores("sum_cores_up", g_w_up, s_up, core)
    c_gate, c_up = from_chips("w_gate_up_to_chips", [p_gate, p_up])
    g_w_out, _ = weight_grad("grad_w_out", mix, dy)
    (dattn, dsgu, d_attn_out, d_sgu_out), _ = mix_bwd(dy, w_out_f, attn, sgu, attn_out_norm, sgu_out_norm)
    (du, dvs, d_ln_gain, d_ln_bias, d_w_sp, d_bfull), _ = sgu_bwd(u, vs, dsgu, sgu_ln_gain, sgu_ln_bias, w_sp, bfull)
    (dq, dk, dv), _ = attn_bwd(q, k, v, attn_r, lse, _to_residue_order(dattn), _to_residue_order(pos_col), rot)
    dq, dk, dv = (_from_residue_order(t) for t in (dq, dk, dv))
    dproj = jnp.concatenate([dq, dk, dv, du, dvs], axis=1)
    g_w_in, _ = weight_grad("grad_w_in", dproj, h1)
    s_in, s_out = from_sibling("w_in_out_to_sibling", [g_w_in, g_w_out])
    (grad_x, d_pre_mix), _ = in_bwd(dproj, w_in_t, x2d, pre_mix_norm, dx2)
    p_in = sum_cores("sum_cores_in", g_w_in, s_in, core)
    p_out = sum_cores("sum_cores_out", g_w_out, s_out, core)

    d_b_sp = d_bfull.reshape(CHUNK, N_GROUPS, HEAD_DIM).sum(axis=-1).T
    small_g = _pack_small(dict(pre_mix_norm=d_pre_mix, sgu_ln_gain=d_ln_gain, sgu_ln_bias=d_ln_bias, sgu_w_spatial=d_w_sp,
                               sgu_b_spatial=d_b_sp, attn_out_norm=d_attn_out, sgu_out_norm=d_sgu_out,
                               post_mix_norm=d_post_mix, pre_ffn_norm=d_pre_ffn, post_ffn_norm=d_post_ffn,
                               loss_sum=sq_err))
    small_g = small_g.reshape(N_DEV, SMALL_ROWS // N_DEV, 128)
    (_, (c_in, c_out)), (_, (o_small,)) = by_sequencer(
        "last_sums_to_owners", [to_chips([p_in, p_out]), to_owners(small_g)], TO_ALL)
    all_small = sum_owned("sum_small", small_g, o_small, dev_ids)
    (all_small,) = gathered("gather_small_grads", [all_small])

    big, last = {}, p_in
    for name, w, p, c, m, vv, transposed in (
            ("w_down", w_down, p_down, c_down, m_w_down, v_w_down, False), ("w_gate", w_gate, p_gate, c_gate, m_w_gate, v_w_gate, True),
            ("w_up", w_up, p_up, c_up, m_w_up, v_w_up, True), ("w_out", w_out, p_out, c_out, m_w_out, v_w_out, False),
            ("w_in", w_in, p_in, c_in, m_w_in, v_w_in, True)):
        turn = (lambda t: t.T) if transposed else (lambda t: t)
        outs = adamw_of_sums("adamw_" + name, p, c, chip_ids, turn(w[0]), turn(m[0]), turn(vv[0]), last)
        big[name], last = tuple(turn(t)[None] for t in outs), outs[0]
    sd, snm, snv = adamw("adamw_small", _pack_small(small_w), all_small, _pack_small(small_m), _pack_small(small_v))
    sg, sd, snm, snv = (_unpack_small(t, small_w) for t in (all_small, sd, snm, snv))
    loss = sg["loss_sum"][0] * np.float32(0.5 / D_MODEL)

    names = ["pre_mix_norm", "w_in", "sgu_ln_gain", "sgu_ln_bias", "sgu_w_spatial", "sgu_b_spatial", "attn_out_norm",
             "sgu_out_norm", "w_out", "post_mix_norm", "pre_ffn_norm", "w_gate", "w_up", "w_down", "post_ffn_norm"]
    outs = [loss, grad_x[None]]
    for i, table in enumerate((sg, sd, snm, snv)):
        for name in names:
            outs.append(big[name][i] if name in big else table[name])
    return tuple(outs)
```

```python
import functools

import numpy as np
import jax
import jax.numpy as jnp
from jax import lax
from jax.experimental import pallas as pl
from jax.experimental.pallas import tpu as pltpu
from jax.experimental.pallas import tpu_sc as plsc

F32 = jnp.float32
BF16 = jnp.bfloat16

SEQ = 2048
D_MODEL = 1024
ATTN_W = 512
SGU_W = 512
HEAD_DIM = 64
N_GROUPS = 8
CHUNK = 128
D_FF = 2816
IN_W = 3 * ATTN_W + 2 * SGU_W
DILATIONS = (1, 4, 16)
ROPE_THETA = 500000.0
ROT_DIM = 16
ROT_HALF = 8
RMS_EPS = 1e-6
LN_EPS = 1e-5
Q_SCALE = 0.125
NEG = -1e30

N_DEV = 8
MESH_AXES = ("x", "y", "c")
MESH = pl.DeviceIdType.MESH

ADAM_LR = 0.001
ADAM_B1 = 0.9
ADAM_B2 = 0.999
ADAM_EPS = 1e-08
ADAM_WD = 0.01
ADAM_STEP = 10

VMEM_LIMIT = 60 * 1024 * 1024
ANY = pl.BlockSpec(memory_space=pl.ANY)

SMALL = (("pre_mix_norm", 1024), ("sgu_ln_gain", 512), ("sgu_ln_bias", 512), ("sgu_w_spatial", 8 * 128 * 128),
         ("sgu_b_spatial", 1024), ("attn_out_norm", 512), ("sgu_out_norm", 512), ("post_mix_norm", 1024),
         ("pre_ffn_norm", 1024), ("post_ffn_norm", 1024), ("loss_sum", 1))
SMALL_ROWS = 1152


def _params(sem=("arbitrary",)):
    return pltpu.CompilerParams(dimension_semantics=sem, vmem_limit_bytes=VMEM_LIMIT)


def _dot(a, b):
    return jnp.dot(a, b, preferred_element_type=F32)


def _dot_nt(a, b):
    return lax.dot_general(a, b, (((1,), (1,)), ((), ())), preferred_element_type=F32)


def _dot_tn(a, b):
    return lax.dot_general(a, b, (((0,), (0,)), ((), ())), preferred_element_type=F32)


def _rms(z):
    return lax.rsqrt(jnp.mean(z * z, axis=-1, keepdims=True) + RMS_EPS)


def _rms_bwd(z, gain, d):
    r = _rms(z)
    n = z * r
    dn = d * gain
    dz = r * (dn - n * jnp.mean(dn * n, axis=-1, keepdims=True))
    return dz, jnp.sum(d * n, axis=0, keepdims=True)


def _gelu(z):
    return 0.5 * z * (1.0 + lax.erf(z * np.float32(1.0 / np.sqrt(2.0))))


def _gelu_grad(z):
    cdf = 0.5 * (1.0 + lax.erf(z * np.float32(1.0 / np.sqrt(2.0))))
    return cdf + z * jnp.exp(-0.5 * z * z) * np.float32(1.0 / np.sqrt(2.0 * np.pi))


def _rot_tables(pos_col, invf, ma, mb):
    ang = pos_col.astype(F32) * invf
    s = jnp.sin(ang)
    return jnp.cos(ang), s * ma, s * mb


def _rot(t, c, sa, sb):
    return t * c + pltpu.roll(t, 120, 1) * sa + pltpu.roll(t, 8, 1) * sb


def _rot_t(d, c, sa, sb):
    return d * c + pltpu.roll(d * sa, 8, 1) + pltpu.roll(d * sb, 120, 1)


def _rot_consts():
    lane = np.arange(128) % HEAD_DIM
    inv_freq = (np.float32(ROPE_THETA) ** (-np.arange(0, ROT_DIM, 2, dtype=np.float32) / np.float32(ROT_DIM))).astype(np.float32)
    invf = np.where(lane < ROT_DIM, inv_freq[lane % ROT_HALF], 0.0).astype(np.float32)
    ma = np.where(lane < ROT_HALF, -1.0, 0.0).astype(np.float32)
    mb = np.where((lane >= ROT_HALF) & (lane < ROT_DIM), 1.0, 0.0).astype(np.float32)
    return jnp.asarray(invf[None]), jnp.asarray(ma[None]), jnp.asarray(mb[None])


def _row_spec(tm, w):
    return pl.BlockSpec((tm, w), lambda i: (i, 0))


def _full_spec(shape):
    return pl.BlockSpec(shape, lambda i: (0,) * len(shape))


def in_proj(x, pos_col, g1, w_in_t, rot, exchanges=()):
    tm = 512

    def body(x_ref, pos_ref, g_ref, w_ref, invf_ref, ma_ref, mb_ref, h_ref, q_ref, k_ref, v_ref, u_ref, vs_ref):
        xf = x_ref[...]
        h = (xf * _rms(xf) * g_ref[...]).astype(BF16)
        h_ref[...] = h
        proj = _dot_nt(h, w_ref[...])
        c, sa, sb = _rot_tables(pos_ref[...], invf_ref[...], ma_ref[...], mb_ref[...])
        for j in range(ATTN_W // 128):
            q_ref[:, j * 128:(j + 1) * 128] = _rot(proj[:, j * 128:(j + 1) * 128], c, sa, sb) * Q_SCALE
            k_ref[:, j * 128:(j + 1) * 128] = _rot(proj[:, ATTN_W + j * 128:ATTN_W + (j + 1) * 128], c, sa, sb)
        v_ref[...] = proj[:, 2 * ATTN_W:3 * ATTN_W]
        u_ref[...] = proj[:, 3 * ATTN_W:3 * ATTN_W + SGU_W]
        vs_ref[...] = proj[:, 3 * ATTN_W + SGU_W:]

    act = jax.ShapeDtypeStruct((SEQ, 512), F32)
    return _hosted(
        "in_proj", body, SEQ // tm,
        [_row_spec(tm, D_MODEL), _row_spec(tm, 1), _full_spec((1, D_MODEL)), _full_spec((IN_W, D_MODEL)),
         _full_spec((1, 128)), _full_spec((1, 128)), _full_spec((1, 128))],
        [_row_spec(tm, D_MODEL)] + [_row_spec(tm, 512)] * 5,
        [jax.ShapeDtypeStruct((SEQ, D_MODEL), BF16)] + [act] * 5,
        (x, pos_col, g1, w_in_t, *rot), exchanges=exchanges)


RES = 16


def _to_residue_order(t):
    return t.reshape(SEQ // RES, RES, -1).transpose(1, 0, 2).reshape(t.shape)


def _from_residue_order(t):
    return t.reshape(RES, SEQ // RES, -1).transpose(1, 0, 2).reshape(t.shape)


def _block_rows(d, r, n):
    if d == 16:
        slices = [(128 * r, 128)]
    elif d == 4:
        slices = [(128 * (4 * b + r) + 32 * n, 32) for b in range(4)]
    else:
        slices = [(128 * b + 8 * n, 8) for b in range(RES)]
    return [(s if isinstance(s, int) else pl.multiple_of(s, z), z) for s, z in slices]


def _block_step(d, i):
    if d == 16:
        return i
    if d == 4:
        return 4 * (i & 31) + (i >> 5)
    return 16 * (i & 7) + (i >> 3)


def _attn_masks(d):
    row2 = _block_step(d, lax.broadcasted_iota(jnp.int32, (128, 256), 0))
    col2 = lax.broadcasted_iota(jnp.int32, (128, 256), 1)
    key2 = _block_step(d, col2 & 127)
    mask2 = jnp.logical_or(jnp.logical_and(col2 < 128, key2 >= row2), jnp.logical_and(col2 >= 128, key2 <= row2))
    row1 = _block_step(d, lax.broadcasted_iota(jnp.int32, (128, 128), 0))
    col1 = lax.broadcasted_iota(jnp.int32, (128, 128), 1)
    return col1 < HEAD_DIM, _block_step(d, col1) <= row1, mask2


def _load_rows(ref, slices):
    parts = [ref[pl.ds(s, z), :] for s, z in slices]
    return parts[0] if len(parts) == 1 else jnp.concatenate(parts, axis=0)


def _for_each_group(fn):
    for p, d in enumerate(DILATIONS):
        masks = _attn_masks(d)
        if d == 16:
            def group(i, carry, p=p, masks=masks):
                fn(p, masks, [(_block_rows(16, 4 * i + g, 0), None) for g in range(4)])
                return carry

            lax.fori_loop(0, 4, group, 0)
        elif d == 4:
            fn(p, masks, [(_block_rows(4, r, 0), None) for r in range(4)])

            def group(i, carry, p=p, masks=masks):
                fn(p, masks, [(_block_rows(4, r, i + 1), _block_rows(4, r, i)) for r in range(4)])
                return carry

            lax.fori_loop(0, 3, group, 0)
        else:
            fn(p, masks, [(_block_rows(1, 0, 0), None)])

            def group(i, carry, p=p, masks=masks):
                fn(p, masks, [(_block_rows(1, 0, 3 * i + g + 1), _block_rows(1, 0, 3 * i + g)) for g in range(3)])
                return carry

            lax.fori_loop(0, 5, group, 0)


def attn_fwd(q, k, v, exchanges=()):
    def body(q_ref, k_ref, v_ref, o_ref, lse_ref, op_ref, lp_ref):
        def group(p, masks, blocks):
            head0, mask1, mask2 = masks
            heads = (head0, jnp.logical_not(head0))
            keys = [rows if prev is None else prev + rows for rows, prev in blocks]
            mask = [mask1 if prev is None else mask2 for _, prev in blocks]
            qb = [_load_rows(q_ref, rows) for rows, _ in blocks]
            kk = [_load_rows(k_ref, ks).astype(BF16) for ks in keys]
            vv = [_load_rows(v_ref, ks).astype(BF16) for ks in keys]
            chains = [(g, hm) for g in range(len(blocks)) for hm in heads]
            s = [jnp.where(mask[g], _dot_nt(jnp.where(hm, qb[g], 0.0).astype(BF16), kk[g]), NEG) for g, hm in chains]
            m = [jnp.max(t, axis=-1, keepdims=True) for t in s]
            e = [jnp.exp(t - mt) for t, mt in zip(s, m)]
            l = [jnp.sum(t, axis=-1, keepdims=True) for t in e]
            pv = [_dot(t.astype(BF16), vv[g]) for t, (g, _) in zip(e, chains)]
            for g, (rows, _) in enumerate(blocks):
                o_blk = jnp.where(head0, pv[2 * g] / l[2 * g], pv[2 * g + 1] / l[2 * g + 1])
                l_blk = jnp.where(head0, jnp.broadcast_to(m[2 * g] + jnp.log(l[2 * g]), (128, 128)),
                                  jnp.broadcast_to(m[2 * g + 1] + jnp.log(l[2 * g + 1]), (128, 128)))
                at = 0
                for start, size in rows:
                    op_ref[p, pl.ds(start, size), :] = o_blk[at:at + size]
                    lp_ref[p, pl.ds(start, size), :] = l_blk[at:at + size]
                    at += size

        _for_each_group(group)

        def combine(i, carry):
            rows = pl.ds(pl.multiple_of(i * 256, 256), 256)
            ls = [lp_ref[p, rows, :] for p in range(3)]
            m = jnp.maximum(jnp.maximum(ls[0], ls[1]), ls[2])
            lse = m + jnp.log(jnp.exp(ls[0] - m) + jnp.exp(ls[1] - m) + jnp.exp(ls[2] - m))
            o = jnp.zeros((256, 128), F32)
            for p in range(3):
                o = o + jnp.exp(ls[p] - lse) * op_ref[p, rows, :]
            o_ref[rows, :] = o
            lse_ref[rows, :] = lse
            return carry

        lax.fori_loop(0, SEQ // 256, combine, 0)

    slab = pl.BlockSpec((SEQ, 128), lambda i: (0, i))
    out = jax.ShapeDtypeStruct((SEQ, ATTN_W), F32)
    return _hosted(
        "attn_fwd", body, ATTN_W // 128, [slab] * 3, [slab] * 2, [out, out], (q, k, v),
        scratch_shapes=[pltpu.VMEM((3, SEQ, 128), F32), pltpu.VMEM((3, SEQ, 128), F32)], exchanges=exchanges)


def _causal_weights(w_ref):
    row = lax.broadcasted_iota(jnp.int32, (CHUNK, CHUNK), 0)
    col = lax.broadcasted_iota(jnp.int32, (CHUNK, CHUNK), 1)
    return [jnp.where(col <= row, w_ref[g], 0.0).astype(BF16) for g in range(N_GROUPS)], col <= row


def _sgu_chunk_fwd(u, vs, lg, lb, wc, bfull, head0):
    ug = _gelu(u)
    vg = _gelu(vs)
    xc = vg - jnp.mean(vg, axis=-1, keepdims=True)
    rstd = lax.rsqrt(jnp.mean(xc * xc, axis=-1, keepdims=True) + LN_EPS)
    xhat = xc * rstd
    vn = xhat * lg + lb
    mixed = []
    for gp in range(SGU_W // 128):
        vp = vn[:, gp * 128:(gp + 1) * 128].astype(BF16)
        mixed.append(jnp.where(head0, _dot(wc[2 * gp], vp), _dot(wc[2 * gp + 1], vp)))
    ms = jnp.concatenate(mixed, axis=1) + bfull
    return ug, xhat, rstd, vn, ms


def sgu_fwd(u, vs, lg, lb, w_sp, bfull, exchanges=()):
    cpb = 4

    def body(u_ref, vs_ref, lg_ref, lb_ref, w_ref, b_ref, o_ref):
        wc, _ = _causal_weights(w_ref)
        head0 = lax.broadcasted_iota(jnp.int32, (CHUNK, 128), 1) < HEAD_DIM
        for ci in range(cpb):
            rows = pl.ds(ci * CHUNK, CHUNK)
            ug, _, _, _, ms = _sgu_chunk_fwd(u_ref[rows, :], vs_ref[rows, :], lg_ref[...], lb_ref[...], wc, b_ref[...], head0)
            o_ref[rows, :] = ug * ms

    tm = cpb * CHUNK
    return _hosted(
        "sgu_fwd", body, SEQ // tm,
        [_row_spec(tm, SGU_W), _row_spec(tm, SGU_W), _full_spec((1, SGU_W)), _full_spec((1, SGU_W)),
         _full_spec((N_GROUPS, CHUNK, CHUNK)), _full_spec((CHUNK, SGU_W))],
        [_row_spec(tm, SGU_W)], [jax.ShapeDtypeStruct((SEQ, SGU_W), F32)],
        (u, vs, lg, lb, w_sp, bfull), exchanges=exchanges)


def out_proj(attn, sgu, x, ga, gs, w_out, gpm, gpf, exchanges=()):
    tm = 512

    def body(a_ref, s_ref, x_ref, ga_ref, gs_ref, w_ref, gpm_ref, gpf_ref, mix_ref, y_ref, x2_ref, h2_ref):
        a = a_ref[...]
        s = s_ref[...]
        an = (a * _rms(a) * ga_ref[...]).astype(BF16)
        sn = (s * _rms(s) * gs_ref[...]).astype(BF16)
        mix_ref[:, :ATTN_W] = an
        mix_ref[:, ATTN_W:] = sn
        y = _dot(an, w_ref[:ATTN_W, :]) + _dot(sn, w_ref[ATTN_W:, :])
        y_ref[...] = y
        x2 = x_ref[...] + y * _rms(y) * gpm_ref[...]
        x2_ref[...] = x2
        h2_ref[...] = (x2 * _rms(x2) * gpf_ref[...]).astype(BF16)

    wide = jax.ShapeDtypeStruct((SEQ, D_MODEL), F32)
    wide16 = jax.ShapeDtypeStruct((SEQ, D_MODEL), BF16)
    return _hosted(
        "out_proj", body, SEQ // tm,
        [_row_spec(tm, ATTN_W), _row_spec(tm, SGU_W), _row_spec(tm, D_MODEL), _full_spec((1, ATTN_W)),
         _full_spec((1, SGU_W)), _full_spec((D_MODEL, D_MODEL)), _full_spec((1, D_MODEL)), _full_spec((1, D_MODEL))],
        [_row_spec(tm, D_MODEL)] * 4, [wide16, wide, wide, wide16],
        (attn, sgu, x, ga, gs, w_out, gpm, gpf), exchanges=exchanges)


def ffn_up(h2, w_gate_t, w_up_t, exchanges=()):
    tm = 256

    def body(h_ref, wg_ref, wu_ref, g_ref, u_ref, a_ref):
        h = h_ref[...]
        g = _dot_nt(h, wg_ref[...])
        u = _dot_nt(h, wu_ref[...])
        g_ref[...] = g.astype(BF16)
        u_ref[...] = u.astype(BF16)
        a_ref[...] = (g * jax.nn.sigmoid(g) * u).astype(BF16)

    ff = jax.ShapeDtypeStruct((SEQ, D_FF), BF16)
    return _hosted(
        "ffn_up", body, SEQ // tm,
        [_row_spec(tm, D_MODEL), _full_spec((D_FF, D_MODEL)), _full_spec((D_FF, D_MODEL))],
        [_row_spec(tm, D_FF)] * 3, [ff, ff, jax.ShapeDtypeStruct((SEQ, D_FF), BF16)],
        (h2, w_gate_t, w_up_t), exchanges=exchanges)


def ffn_down_loss(act, w_down, x2, gpo, target):
    tm = 512

    def body(a_ref, w_ref, x2_ref, g_ref, t_ref, df_ref, dx3_ref, dg_ref, loss_ref):
        f = _dot(a_ref[...], w_ref[...])
        gain = g_ref[...]
        err = x2_ref[...] + f * _rms(f) * gain - t_ref[...]
        dx3 = err * np.float32(1.0 / D_MODEL)
        dx3_ref[...] = dx3
        df, dg = _rms_bwd(f, gain, dx3)
        df_ref[...] = df.astype(BF16)

        @pl.when(pl.program_id(0) == 0)
        def _():
            dg_ref[...] = jnp.zeros_like(dg_ref)
            loss_ref[...] = jnp.zeros_like(loss_ref)

        dg_ref[...] += dg
        loss_ref[...] += jnp.sum(err * err, axis=(0, 1), keepdims=True)

    return pl.pallas_call(
        body, name="ffn_down_loss", grid=(SEQ // tm,),
        in_specs=[_row_spec(tm, D_FF), _full_spec((D_FF, D_MODEL)), _row_spec(tm, D_MODEL), _full_spec((1, D_MODEL)),
                  _row_spec(tm, D_MODEL)],
        out_specs=[_row_spec(tm, D_MODEL), _row_spec(tm, D_MODEL), _full_spec((1, D_MODEL)), _full_spec((1, 1))],
        out_shape=[jax.ShapeDtypeStruct((SEQ, D_MODEL), BF16), jax.ShapeDtypeStruct((SEQ, D_MODEL), F32),
                   jax.ShapeDtypeStruct((1, D_MODEL), F32), jax.ShapeDtypeStruct((1, 1), F32)],
        compiler_params=_params(),
    )(act, w_down, x2, gpo, target)


def ffn_act_bwd(df, w_down, gate, up, after=()):
    tm = 256

    def body(df_ref, w_ref, g_ref, u_ref, dg_ref, du_ref):
        dact = _dot_nt(df_ref[...], w_ref[...])
        g = g_ref[...].astype(F32)
        s = jax.nn.sigmoid(g)
        du_ref[...] = (dact * g * s).astype(BF16)
        dg_ref[...] = (dact * u_ref[...].astype(F32) * (s * (1.0 + g * (1.0 - s)))).astype(BF16)

    ff16 = jax.ShapeDtypeStruct((SEQ, D_FF), BF16)
    return _hosted(
        "ffn_act_bwd", body, SEQ // tm,
        [_row_spec(tm, D_MODEL), _full_spec((D_FF, D_MODEL)), _row_spec(tm, D_FF), _row_spec(tm, D_FF)],
        [_row_spec(tm, D_FF)] * 2, [ff16, ff16], (df, w_down, gate, up), after=after)


def ffn_in_bwd(dgate, dup, w_gate_t, w_up_t, x2, gpf, dx3, y, gpm, after=()):
    tm = 256

    def body(dg_ref, du_ref, wg_ref, wu_ref, x2_ref, gpf_ref, dx3_ref, y_ref, gpm_ref, dx2_ref, dy_ref, dgpf_ref, dgpm_ref):
        dh2 = _dot(dg_ref[...], wg_ref[...]) + _dot(du_ref[...], wu_ref[...])
        dz, dgpf = _rms_bwd(x2_ref[...], gpf_ref[...], dh2)
        dx2 = dx3_ref[...] + dz
        dx2_ref[...] = dx2
        dy, dgpm = _rms_bwd(y_ref[...], gpm_ref[...], dx2)
        dy_ref[...] = dy.astype(BF16)

        @pl.when(pl.program_id(0) == 0)
        def _():
            dgpf_ref[...] = jnp.zeros_like(dgpf_ref)
            dgpm_ref[...] = jnp.zeros_like(dgpm_ref)

        dgpf_ref[...] += dgpf
        dgpm_ref[...] += dgpm

    vec = jax.ShapeDtypeStruct((1, D_MODEL), F32)
    return _hosted(
        "ffn_in_bwd", body, SEQ // tm,
        [_row_spec(tm, D_FF), _row_spec(tm, D_FF), _full_spec((D_FF, D_MODEL)), _full_spec((D_FF, D_MODEL)),
         _row_spec(tm, D_MODEL), _full_spec((1, D_MODEL)), _row_spec(tm, D_MODEL), _row_spec(tm, D_MODEL),
         _full_spec((1, D_MODEL))],
        [_row_spec(tm, D_MODEL), _row_spec(tm, D_MODEL), _full_spec((1, D_MODEL)), _full_spec((1, D_MODEL))],
        [jax.ShapeDtypeStruct((SEQ, D_MODEL), F32), jax.ShapeDtypeStruct((SEQ, D_MODEL), BF16), vec, vec],
        (dgate, dup, w_gate_t, w_up_t, x2, gpf, dx3, y, gpm), after=after)


def weight_grad(name, a, b, after=()):
    m, n = a.shape[1], b.shape[1]
    tr = 256

    def body(a_ref, b_ref, o_ref):
        o_ref[...] = _dot_tn(a_ref[...], b_ref[...]).astype(BF16)

    (out,), done = _hosted(
        name, body, m // tr, [pl.BlockSpec((SEQ, tr), lambda i: (0, i)), _full_spec((SEQ, n))],
        [_row_spec(tr, n)], [jax.ShapeDtypeStruct((m, n), BF16)], (a, b), after=after)
    return out.reshape(N_DEV, m // N_DEV, n), done


def mix_bwd(dy, w_out, attn, sgu, ga, gs, after=()):
    tm = 512

    def body(dy_ref, w_ref, a_ref, s_ref, ga_ref, gs_ref, da_ref, ds_ref, dga_ref, dgs_ref):
        dy = dy_ref[...]
        da, dga = _rms_bwd(a_ref[...], ga_ref[...], _dot_nt(dy, w_ref[:ATTN_W, :]))
        ds, dgs = _rms_bwd(s_ref[...], gs_ref[...], _dot_nt(dy, w_ref[ATTN_W:, :]))
        da_ref[...] = da
        ds_ref[...] = ds

        @pl.when(pl.program_id(0) == 0)
        def _():
            dga_ref[...] = jnp.zeros_like(dga_ref)
            dgs_ref[...] = jnp.zeros_like(dgs_ref)

        dga_ref[...] += dga
        dgs_ref[...] += dgs

    half = jax.ShapeDtypeStruct((SEQ, 512), F32)
    vec = jax.ShapeDtypeStruct((1, 512), F32)
    return _hosted(
        "mix_bwd", body, SEQ // tm,
        [_row_spec(tm, D_MODEL), _full_spec((D_MODEL, D_MODEL)), _row_spec(tm, 512), _row_spec(tm, 512),
         _full_spec((1, 512)), _full_spec((1, 512))],
        [_row_spec(tm, 512), _row_spec(tm, 512), _full_spec((1, 512)), _full_spec((1, 512))],
        [half, half, vec, vec], (dy, w_out, attn, sgu, ga, gs), after=after)


def sgu_bwd(u, vs, dsgu, lg, lb, w_sp, bfull, exchanges=()):
    cpb = 4

    def body(u_ref, vs_ref, d_ref, lg_ref, lb_ref, w_ref, b_ref, du_ref, dvs_ref, dlg_ref, dlb_ref, dw_ref, db_ref):
        wc, causal = _causal_weights(w_ref)
        head0 = lax.broadcasted_iota(jnp.int32, (CHUNK, 128), 1) < HEAD_DIM
        lg = lg_ref[...]

        @pl.when(pl.program_id(0) == 0)
        def _():
            dlg_ref[...] = jnp.zeros_like(dlg_ref)
            dlb_ref[...] = jnp.zeros_like(dlb_ref)
            dw_ref[...] = jnp.zeros_like(dw_ref)
            db_ref[...] = jnp.zeros_like(db_ref)

        for ci in range(cpb):
            rows = pl.ds(ci * CHUNK, CHUNK)
            u = u_ref[rows, :]
            vs = vs_ref[rows, :]
            d = d_ref[rows, :]
            ug, xhat, rstd, vn, ms = _sgu_chunk_fwd(u, vs, lg, lb_ref[...], wc, b_ref[...], head0)
            du_ref[rows, :] = (d * ms * _gelu_grad(u)).astype(BF16)
            dms = d * ug
            db_ref[...] += dms
            dvn = []
            for gp in range(SGU_W // 128):
                dmp = dms[:, gp * 128:(gp + 1) * 128]
                dm0 = jnp.where(head0, dmp, 0.0).astype(BF16)
                dm1 = jnp.where(head0, 0.0, dmp).astype(BF16)
                vp = vn[:, gp * 128:(gp + 1) * 128].astype(BF16)
                dw_ref[2 * gp] += _dot_nt(dm0, vp)
                dw_ref[2 * gp + 1] += _dot_nt(dm1, vp)
                dvn.append(_dot_tn(wc[2 * gp], dm0) + _dot_tn(wc[2 * gp + 1], dm1))
            dvn = jnp.concatenate(dvn, axis=1)
            dlg_ref[...] += jnp.sum(dvn * xhat, axis=0, keepdims=True)
            dlb_ref[...] += jnp.sum(dvn, axis=0, keepdims=True)
            dxh = dvn * lg
            dvg = rstd * (dxh - jnp.mean(dxh, axis=-1, keepdims=True) - xhat * jnp.mean(dxh * xhat, axis=-1, keepdims=True))
            dvs_ref[rows, :] = (dvg * _gelu_grad(vs)).astype(BF16)

        @pl.when(pl.program_id(0) == pl.num_programs(0) - 1)
        def _():
            for g in range(N_GROUPS):
                dw_ref[g] = jnp.where(causal, dw_ref[g], 0.0)

    tm = cpb * CHUNK
    half16 = jax.ShapeDtypeStruct((SEQ, SGU_W), BF16)
    vec = jax.ShapeDtypeStruct((1, SGU_W), F32)
    return _hosted(
        "sgu_bwd", body, SEQ // tm,
        [_row_spec(tm, SGU_W)] * 3 + [_full_spec((1, SGU_W)), _full_spec((1, SGU_W)),
                                      _full_spec((N_GROUPS, CHUNK, CHUNK)), _full_spec((CHUNK, SGU_W))],
        [_row_spec(tm, SGU_W), _row_spec(tm, SGU_W), _full_spec((1, SGU_W)), _full_spec((1, SGU_W)),
         _full_spec((N_GROUPS, CHUNK, CHUNK)), _full_spec((CHUNK, SGU_W))],
        [half16, half16, vec, vec, jax.ShapeDtypeStruct((N_GROUPS, CHUNK, CHUNK), F32),
         jax.ShapeDtypeStruct((CHUNK, SGU_W), F32)],
        (u, vs, dsgu, lg, lb, w_sp, bfull), exchanges=exchanges)


def attn_bwd(q, k, v, o, lse, do, pos_col, rot, exchanges=()):
    def body(q_ref, k_ref, v_ref, o_ref, lse_ref, do_ref, pos_ref, invf_ref, ma_ref, mb_ref,
             dq_ref, dk_ref, dv_ref, dqa_ref, dka_ref, dva_ref, dlt_ref, rot_ref):
        dqa_ref[...] = jnp.zeros_like(dqa_ref)
        dka_ref[...] = jnp.zeros_like(dka_ref)
        dva_ref[...] = jnp.zeros_like(dva_ref)

        def delta(i, carry):
            rows = pl.ds(pl.multiple_of(i * 256, 256), 256)
            prod = do_ref[rows, :] * o_ref[rows, :]
            h0 = lax.broadcasted_iota(jnp.int32, (256, 128), 1) < HEAD_DIM
            d0 = jnp.sum(jnp.where(h0, prod, 0.0), axis=-1, keepdims=True)
            d1 = jnp.sum(jnp.where(h0, 0.0, prod), axis=-1, keepdims=True)
            dlt_ref[rows, :] = jnp.where(h0, d0, d1)
            return carry

        lax.fori_loop(0, SEQ // 256, delta, 0)

        def add_rows(ref, slices, val):
            at = 0
            for start, size in slices:
                ref[pl.ds(start, size), :] += val[at:at + size]
                at += size

        def group(p, masks, blocks):
            head0, mask1, mask2 = masks
            heads = (head0, jnp.logical_not(head0))
            keys = [rows if prev is None else prev + rows for rows, prev in blocks]
            mask = [mask1 if prev is None else mask2 for _, prev in blocks]
            kk = [_load_rows(k_ref, ks).astype(BF16) for ks in keys]
            vv = [_load_rows(v_ref, ks).astype(BF16) for ks in keys]
            qb = [_load_rows(q_ref, rows) for rows, _ in blocks]
            dob = [_load_rows(do_ref, rows) for rows, _ in blocks]
            lse_b = [_load_rows(lse_ref, rows) for rows, _ in blocks]
            dlt_b = [_load_rows(dlt_ref, rows) for rows, _ in blocks]
            chains = [(g, h) for g in range(len(blocks)) for h in range(2)]
            qm = [jnp.where(heads[h], qb[g], 0.0).astype(BF16) for g, h in chains]
            dom = [jnp.where(heads[h], dob[g], 0.0).astype(BF16) for g, h in chains]
            s = [_dot_nt(qm[c], kk[g]) for c, (g, h) in enumerate(chains)]
            dp = [_dot_nt(dom[c], vv[g]) for c, (g, h) in enumerate(chains)]
            pr = [jnp.where(mask[g], jnp.exp(s[c] - lse_b[g][:, h * HEAD_DIM:h * HEAD_DIM + 1]), 0.0)
                  for c, (g, h) in enumerate(chains)]
            ds = [(pr[c] * (dp[c] - dlt_b[g][:, h * HEAD_DIM:h * HEAD_DIM + 1])).astype(BF16)
                  for c, (g, h) in enumerate(chains)]
            dv = [_dot_tn(pr[c].astype(BF16), dom[c]) for c in range(len(chains))]
            dk = [_dot_tn(ds[c], qm[c]) for c in range(len(chains))]
            dq = [_dot(ds[c], kk[g]) for c, (g, h) in enumerate(chains)]
            for g, (rows, _) in enumerate(blocks):
                add_rows(dqa_ref, rows, jnp.where(head0, dq[2 * g], dq[2 * g + 1]))
                add_rows(dka_ref, keys[g], dk[2 * g] + dk[2 * g + 1])
                add_rows(dva_ref, keys[g], dv[2 * g] + dv[2 * g + 1])

        _for_each_group(group)

        @pl.when(pl.program_id(0) == 0)
        def _():
            def tables(i, carry):
                rows = pl.ds(pl.multiple_of(i * 256, 256), 256)
                c, sa, sb = _rot_tables(pos_ref[rows, :], invf_ref[...], ma_ref[...], mb_ref[...])
                rot_ref[0, rows, :] = c
                rot_ref[1, rows, :] = sa
                rot_ref[2, rows, :] = sb
                return carry

            lax.fori_loop(0, SEQ // 256, tables, 0)

        def finish(i, carry):
            rows = pl.ds(pl.multiple_of(i * 256, 256), 256)
            c, sa, sb = rot_ref[0, rows, :], rot_ref[1, rows, :], rot_ref[2, rows, :]
            dq_ref[rows, :] = _rot_t(dqa_ref[rows, :] * Q_SCALE, c, sa, sb).astype(BF16)
            dk_ref[rows, :] = _rot_t(dka_ref[rows, :], c, sa, sb).astype(BF16)
            dv_ref[rows, :] = dva_ref[rows, :].astype(BF16)
            return carry

        lax.fori_loop(0, SEQ // 256, finish, 0)

    slab = pl.BlockSpec((SEQ, 128), lambda i: (0, i))
    out = jax.ShapeDtypeStruct((SEQ, ATTN_W), BF16)
    acc = pltpu.VMEM((SEQ, 128), F32)
    return _hosted(
        "attn_bwd", body, ATTN_W // 128,
        [slab] * 6 + [_full_spec((SEQ, 1)), _full_spec((1, 128)), _full_spec((1, 128)), _full_spec((1, 128))],
        [slab] * 3, [out, out, out], (q, k, v, o, lse, do, pos_col, *rot),
        scratch_shapes=[acc, acc, acc, acc, pltpu.VMEM((3, SEQ, 128), F32)], exchanges=exchanges)


def in_bwd(dproj, w_in_t, x, g1, dx2, after=()):
    tm = 512

    def body(dp_ref, w_ref, x_ref, g_ref, dx2_ref, dx_ref, dg_ref):
        dh1 = _dot(dp_ref[...], w_ref[...])
        dz, dg = _rms_bwd(x_ref[...], g_ref[...], dh1)
        dx_ref[...] = dx2_ref[...] + dz

        @pl.when(pl.program_id(0) == 0)
        def _():
            dg_ref[...] = jnp.zeros_like(dg_ref)

        dg_ref[...] += dg

    return _hosted(
        "in_bwd", body, SEQ // tm,
        [_row_spec(tm, IN_W), _full_spec((IN_W, D_MODEL)), _row_spec(tm, D_MODEL), _full_spec((1, D_MODEL)),
         _row_spec(tm, D_MODEL)],
        [_row_spec(tm, D_MODEL), _full_spec((1, D_MODEL))],
        [jax.ShapeDtypeStruct((SEQ, D_MODEL), F32), jax.ShapeDtypeStruct((1, D_MODEL), F32)],
        (dproj, w_in_t, x, g1, dx2), after=after)


def _coords():
    return lax.axis_index("x"), lax.axis_index("y"), lax.axis_index("c")


class Exchange:
    def __init__(self, srcs, bufs, new_shapes, n_sems, make, mid_step=None):
        self.srcs, self.bufs, self.new_shapes, self.n_sems, self.make = list(srcs), list(bufs), list(new_shapes), n_sems, make
        self.mid_step = mid_step


def _hosted(name, body, n_steps, in_specs, out_specs, out_shape, args, scratch_shapes=(), exchanges=(), prefetch=None,
            after=()):
    out_shape, out_specs = list(out_shape), list(out_specs)
    srcs = [a for ex in exchanges for a in ex.srcs] + list(after)
    bufs = [a for ex in exchanges for a in ex.bufs]
    news = [s for ex in exchanges for s in ex.new_shapes]
    n_pre = 0 if prefetch is None else 1
    n_in, n_out, n_scr = len(args), len(out_shape), len(scratch_shapes)

    def wrapped(*refs):
        refs = list(refs)
        pre, refs = refs[:n_pre], refs[n_pre:]
        ins = refs[:n_in]
        src_refs = refs[n_in:n_in + len(srcs)]
        at = n_in + len(srcs) + len(bufs)
        outs = refs[at:at + n_out]
        buf_refs = refs[at + n_out:at + n_out + len(bufs)]
        new_refs = refs[at + n_out + len(bufs):at + n_out + len(bufs) + len(news)]
        at += n_out + len(bufs) + len(news)
        scratch, sems = refs[at:at + n_scr], refs[at + n_scr:]

        def copies(phase, which):
            made, si, bi, ni = [], 0, 0, 0
            for k, ex in enumerate(exchanges):
                if which(ex):
                    made.append(ex.make(phase, src_refs[si:si + len(ex.srcs)], buf_refs[bi:bi + len(ex.bufs)],
                                        new_refs[ni:ni + len(ex.new_shapes)], sems[2 * k], sems[2 * k + 1]))
                si, bi, ni = si + len(ex.srcs), bi + len(ex.bufs), ni + len(ex.new_shapes)
            return made

        if exchanges:
            @pl.when(pl.program_id(0) == 0)
            def _():
                for starts in copies("start", lambda ex: True):
                    for cp in starts:
                        cp.start()

        def pass_on(mid):
            for arrivals, starts in copies("middle", lambda ex: ex.mid_step == mid):
                for cp in arrivals:
                    cp.wait_recv()
                for cp in starts:
                    cp.start()

        for mid in sorted({ex.mid_step for ex in exchanges if isinstance(ex.mid_step, int)}):
            @pl.when(pl.program_id(0) == mid)
            def _(mid=mid):
                pass_on(mid)

        body(*pre, *ins, *outs, *scratch)

        if exchanges:
            @pl.when(pl.program_id(0) == n_steps - 1)
            def _():
                pass_on("end")
                for sends, recvs in copies("end", lambda ex: True):
                    for cp in sends:
                        cp.wait_send()
                    for cp in recvs:
                        cp.wait_recv()

    sem_shapes = []
    for ex in exchanges:
        sem_shapes += [pltpu.SemaphoreType.DMA((ex.n_sems,)), pltpu.SemaphoreType.DMA((ex.n_sems,))]
    all_in = list(in_specs) + [ANY] * (len(srcs) + len(bufs))
    all_out = out_specs + [ANY] * (len(bufs) + len(news))
    all_shape = out_shape + [jax.ShapeDtypeStruct(b.shape, b.dtype) for b in bufs] + news
    all_scratch = list(scratch_shapes) + sem_shapes
    aliases = {n_pre + n_in + len(srcs) + i: n_out + i for i in range(len(bufs))}
    if prefetch is None:
        call = pl.pallas_call(wrapped, name=name, grid=(n_steps,), in_specs=all_in, out_specs=all_out, out_shape=all_shape,
                              scratch_shapes=all_scratch, input_output_aliases=aliases, compiler_params=_params())
        outs = call(*args, *srcs, *bufs)
    else:
        spec = pltpu.PrefetchScalarGridSpec(num_scalar_prefetch=1, grid=(n_steps,), in_specs=all_in, out_specs=all_out,
                                            scratch_shapes=all_scratch)
        call = pl.pallas_call(wrapped, name=name, grid_spec=spec, out_shape=all_shape, input_output_aliases=aliases,
                              compiler_params=_params())
        outs = call(prefetch, *args, *srcs, *bufs)
    results, bi, ni = [], n_out, n_out + len(bufs)
    for ex in exchanges:
        results.append((list(outs[bi:bi + len(ex.bufs)]), list(outs[ni:ni + len(ex.new_shapes)])))
        bi, ni = bi + len(ex.bufs), ni + len(ex.new_shapes)
    return list(outs[:n_out]), results


def _gather_copies(kinds, bufs, ranges, send_sems, recv_sems):
    x, y, c = _coords()
    me, sibling = (x, y, c), (x, y, 1 - c)
    chips = [(1 - x, y), (x, 1 - y), (1 - x, 1 - y)]

    def copy(a, k, block, to):
        lo, hi = ranges[a]
        r = bufs[a].shape[0] // N_DEV
        rows = bufs[a].at[pl.ds((4 * block[0] + 2 * block[1] + block[2]) * r + lo, hi - lo), :]
        return pltpu.make_async_remote_copy(src_ref=rows, dst_ref=rows, send_sem=send_sems.at[7 * a + k],
                                            recv_sem=recv_sems.at[7 * a + k], device_id=to, device_id_type=MESH)

    every = range(len(bufs))
    make = {
        "out": lambda: [copy(a, 0, me, sibling) for a in every]
        + [copy(a, 1 + j, me, (*chip, c)) for a in every for j, chip in enumerate(chips)],
        "from_core": lambda: [copy(a, 0, sibling, me) for a in every],
        "from_chips": lambda: [copy(a, 1 + j, (*chip, c), me) for a in every for j, chip in enumerate(chips)],
        "on": lambda: [copy(a, 4 + j, (*chip, c), sibling) for a in every for j, chip in enumerate(chips)],
        "on_in": lambda: [copy(a, 4 + j, (*chip, 1 - c), me) for a in every for j, chip in enumerate(chips)],
    }
    return [make[kind]() for kind in kinds]


def gather(bufs, mid_step, ranges=None):
    ranges = ranges or [(0, b.shape[0] // N_DEV) for b in bufs]

    def make(phase, src_refs, buf_refs, new_refs, send_sems, recv_sems):
        kinds = {"start": ["out"], "middle": ["from_chips", "on"], "end": ["out", "on", "from_core", "on_in"]}[phase]
        made = _gather_copies(kinds, buf_refs, ranges, send_sems, recv_sems)
        if phase == "start":
            return made[0]
        if phase == "middle":
            return made[0], made[1]
        return made[0] + made[1], made[2] + made[3]

    return Exchange([], bufs, [], 7 * len(bufs), make, mid_step=mid_step)


def all_gather_in_place(name, bufs):
    n = len(bufs)
    ranges = [(0, b.shape[0] // N_DEV) for b in bufs]

    def body(*refs):
        outs, send_sems, recv_sems = refs[n:2 * n], refs[2 * n], refs[2 * n + 1]
        out, from_core, from_chips, on, on_in = _gather_copies(
            ["out", "from_core", "from_chips", "on", "on_in"], outs, ranges, send_sems, recv_sems)
        for cp in out:
            cp.start()
        for cp in from_chips:
            cp.wait_recv()
        for cp in on:
            cp.start()
        for cp in from_core + on_in:
            cp.wait_recv()
        for cp in out + on:
            cp.wait_send()

    return pl.pallas_call(
        body, name=name, in_specs=[ANY] * n, out_specs=[ANY] * n,
        out_shape=[jax.ShapeDtypeStruct(b.shape, b.dtype) for b in bufs],
        scratch_shapes=[pltpu.SemaphoreType.DMA((7 * n,)), pltpu.SemaphoreType.DMA((7 * n,))],
        input_output_aliases={i: i for i in range(n)},
    )(*bufs)


TO_GATHER = (1, lambda x, y, c: [(x, y, 1 - c), (1 - x, y, c), (x, 1 - y, c), (1 - x, 1 - y, c)])
TO_SIBLING = (2, lambda x, y, c: [(x, y, 1 - c)])
TO_CHIPS = (3, lambda x, y, c: [(1 - x, y, c), (x, 1 - y, c), (1 - x, 1 - y, c)])
TO_ALL = (4, lambda x, y, c: [(x ^ (m >> 2), y ^ ((m >> 1) & 1), c ^ (m & 1)) for m in range(1, N_DEV)])


def by_sequencer(name, exchanges, who):
    collective_id, peers_of = who
    hbm = pltpu.MemorySpace.HBM
    refs = [([jax.new_ref(a, memory_space=hbm) for a in ex.srcs], [jax.new_ref(a, memory_space=hbm) for a in ex.bufs],
             [jax.empty_ref(s, memory_space=hbm) for s in ex.new_shapes]) for ex in exchanges]
    sems = []
    for ex in exchanges:
        sems += [pltpu.SemaphoreType.DMA((ex.n_sems,)), pltpu.SemaphoreType.DMA((ex.n_sems,))]

    @pl.kernel(mesh=plsc.ScalarSubcoreMesh(axis_name="sequencer", num_cores=1), name=name, scratch_types=tuple(sems),
               compiler_params=pltpu.CompilerParams(collective_id=collective_id))
    def launch(*sem_refs):
        peers = peers_of(*_coords())
        barrier = pltpu.get_barrier_semaphore()
        for peer in peers:
            pl.semaphore_signal(barrier, inc=1, device_id=peer, device_id_type=MESH)
        pl.semaphore_wait(barrier, len(peers))

        def make(phase, k):
            return exchanges[k].make(phase, *refs[k], sem_refs[2 * k], sem_refs[2 * k + 1])

        for k in range(len(exchanges)):
            for cp in make("start", k):
                cp.start()
        for k, ex in enumerate(exchanges):
            if ex.mid_step is not None:
                arrivals, starts = make("middle", k)
                for cp in arrivals:
                    cp.wait_recv()
                for cp in starts:
                    cp.start()
        for k in range(len(exchanges)):
            sends, arrivals = make("end", k)
            for cp in arrivals:
                cp.wait_recv()
            for cp in sends:
                cp.wait_send()

    launch()
    return [([ref[...] for ref in bufs], [ref[...] for ref in news]) for _, bufs, news in refs]


def place_shards(shards, dev):
    n = len(shards)

    def body(dev_ref, *refs):
        for a in range(n):
            refs[n + a][...] = refs[a][...].astype(BF16)

    spec = pltpu.PrefetchScalarGridSpec(
        num_scalar_prefetch=1, grid=(1,),
        in_specs=[pl.BlockSpec(s.shape, lambda i, dev_ref: (0, 0)) for s in shards],
        out_specs=[pl.BlockSpec(s.shape, lambda i, dev_ref: (dev_ref[0], 0)) for s in shards])
    return pl.pallas_call(
        body, name="place_shards", grid_spec=spec,
        out_shape=[jax.ShapeDtypeStruct((N_DEV * s.shape[0], s.shape[1]), BF16) for s in shards],
        compiler_params=_params(),
    )(dev, *shards)


def _swap(copies_of):
    def make(phase, src_refs, buf_refs, new_refs, send_sems, recv_sems):
        copies = copies_of(src_refs, new_refs, send_sems, recv_sems)
        return copies if phase == "start" else (copies, copies)

    return make


def to_sibling(grads):
    def copies_of(src_refs, new_refs, send_sems, recv_sems):
        x, y, c = _coords()
        return [pltpu.make_async_remote_copy(
            src_ref=src_refs[a].at[2 * xy + 1 - c], dst_ref=new_refs[a].at[xy], send_sem=send_sems.at[4 * a + xy],
            recv_sem=recv_sems.at[4 * a + xy], device_id=(x, y, 1 - c), device_id_type=MESH)
            for a in range(len(src_refs)) for xy in range(4)]

    return Exchange(grads, [], [jax.ShapeDtypeStruct((4,) + g.shape[1:], g.dtype) for g in grads], 4 * len(grads),
                    _swap(copies_of))


def to_chips(parts):
    def copies_of(src_refs, new_refs, send_sems, recv_sems):
        x, y, c = _coords()
        chips = [(1 - x, y), (x, 1 - y), (1 - x, 1 - y)]
        return [pltpu.make_async_remote_copy(
            src_ref=src_refs[a].at[2 * px + py], dst_ref=new_refs[a].at[2 * x + y], send_sem=send_sems.at[3 * a + j],
            recv_sem=recv_sems.at[3 * a + j], device_id=(px, py, c), device_id_type=MESH)
            for a in range(len(src_refs)) for j, (px, py) in enumerate(chips)]

    return Exchange(parts, [], [jax.ShapeDtypeStruct(p.shape, p.dtype) for p in parts], 3 * len(parts), _swap(copies_of))


def to_owners(grad):
    def copies_of(src_refs, new_refs, send_sems, recv_sems):
        x, y, c = _coords()
        copies = []
        for m in range(1, N_DEV):
            px, py, pc = x ^ (m >> 2), y ^ ((m >> 1) & 1), c ^ (m & 1)
            copies.append(pltpu.make_async_remote_copy(
                src_ref=src_refs[0].at[4 * px + 2 * py + pc], dst_ref=new_refs[0].at[4 * x + 2 * y + c],
                send_sem=send_sems.at[m - 1], recv_sem=recv_sems.at[m - 1], device_id=(px, py, pc), device_id_type=MESH))
        return copies

    return Exchange([grad], [], [jax.ShapeDtypeStruct(grad.shape, grad.dtype)], N_DEV - 1, _swap(copies_of))


def exchange_only(name, exchanges):
    def body():
        pass

    return _hosted(name, body, 1, [], [], [], [], exchanges=exchanges)[1]


def sum_cores(name, grad, other, core, after=()):
    _, r, w = other.shape

    def body(core_ref, g_ref, o_ref, *rest):
        rest[-1][...] = (g_ref[...].astype(F32) + o_ref[...].astype(F32)).astype(rest[-1].dtype)

    return pl.pallas_call(
        body, name=name,
        grid_spec=pltpu.PrefetchScalarGridSpec(
            num_scalar_prefetch=1, grid=(4,),
            in_specs=[pl.BlockSpec((1, r, w), lambda i, core_ref: (2 * i + core_ref[0], 0, 0)),
                      pl.BlockSpec((1, r, w), lambda i, core_ref: (i, 0, 0))] + [ANY] * len(after),
            out_specs=pl.BlockSpec((1, r, w), lambda i, core_ref: (i, 0, 0))),
        out_shape=jax.ShapeDtypeStruct(other.shape, other.dtype),
        compiler_params=_params(),
    )(core, grad, other, *after)


def sum_owned(name, grad, others, dev_ids):
    _, r, w = grad.shape

    def body(ids_ref, *refs):
        acc = refs[0][0]
        for k in range(1, N_DEV):
            acc = acc + refs[k][0]
        refs[N_DEV][...] = acc

    def pick(k):
        return pl.BlockSpec((1, r, w), lambda i, ids_ref: (ids_ref[k], 0, 0))

    return pl.pallas_call(
        body, name=name,
        grid_spec=pltpu.PrefetchScalarGridSpec(
            num_scalar_prefetch=1, grid=(1,), in_specs=[pick(k) for k in range(N_DEV)],
            out_specs=pl.BlockSpec((r, w), lambda i, ids_ref: (ids_ref[0], 0))),
        out_shape=jax.ShapeDtypeStruct((N_DEV * r, w), F32),
        compiler_params=_params(),
    )(dev_ids, grad, *([others] * (N_DEV - 1)))


def _adamw_update(w, g, m, v):
    nm = ADAM_B1 * m + np.float32(1.0 - ADAM_B1) * g
    nv = ADAM_B2 * v + np.float32(1.0 - ADAM_B2) * (g * g)
    m_hat = nm / np.float32(1.0 - ADAM_B1 ** ADAM_STEP)
    v_hat = nv / np.float32(1.0 - ADAM_B2 ** ADAM_STEP)
    return -ADAM_LR * (m_hat / (jnp.sqrt(v_hat) + ADAM_EPS) + ADAM_WD * w), nm, nv


def adamw_of_sums(name, part, others, chip_ids, w, m, v, after):
    _, r, wd = part.shape

    def body(ids_ref, p_ref, a_ref, b_ref, c_ref, w_ref, m_ref, v_ref, after_ref, g_ref, d_ref, nm_ref, nv_ref):
        g = ((p_ref[0].astype(F32) + a_ref[0].astype(F32)) + b_ref[0].astype(F32)) + c_ref[0].astype(F32)
        g_ref[...] = g
        d_ref[...], nm_ref[...], nv_ref[...] = _adamw_update(w_ref[...], g, m_ref[...], v_ref[...])

    def pick(k):
        return pl.BlockSpec((1, r, wd), lambda i, ids_ref: (ids_ref[k], 0, 0))

    whole = pl.BlockSpec((r, wd), lambda i, ids_ref: (0, 0))
    shape = jax.ShapeDtypeStruct((r, wd), F32)
    return pl.pallas_call(
        body, name=name,
        grid_spec=pltpu.PrefetchScalarGridSpec(
            num_scalar_prefetch=1, grid=(1,), in_specs=[pick(0), pick(1), pick(2), pick(3), whole, whole, whole, ANY],
            out_specs=[whole] * 4),
        out_shape=[shape] * 4,
        compiler_params=_params(),
    )(chip_ids, part, others, others, others, w, m, v, after)


def adamw(name, w, g, m, v):
    def body(w_ref, g_ref, m_ref, v_ref, d_ref, nm_ref, nv_ref):
        d_ref[...], nm_ref[...], nv_ref[...] = _adamw_update(w_ref[...], g_ref[...], m_ref[...], v_ref[...])

    shape = jax.ShapeDtypeStruct(w.shape, F32)
    spec = _full_spec(w.shape)
    return pl.pallas_call(
        body, name=name, grid=(1,), in_specs=[spec] * 4, out_specs=[spec] * 3, out_shape=[shape] * 3,
        compiler_params=_params(),
    )(w, g, m, v)


def _pack_small(parts):
    flat = jnp.concatenate([parts[name].reshape(-1) for name, _ in SMALL])
    return jnp.pad(flat, (0, SMALL_ROWS * 128 - flat.shape[0])).reshape(SMALL_ROWS, 128)


def _unpack_small(packed, like):
    flat = packed.reshape(-1)
    out, at = {}, 0
    for name, size in SMALL:
        out[name] = flat[at:at + size].reshape(like[name].shape)
        at += size
    return out


def kernel(x, positions, pre_mix_norm, w_in, sgu_ln_gain, sgu_ln_bias, sgu_w_spatial, sgu_b_spatial, attn_out_norm, sgu_out_norm, w_out, post_mix_norm, pre_ffn_norm, w_gate, w_up, w_down, post_ffn_norm, loss_target, m_pre_mix_norm, m_w_in, m_sgu_ln_gain, m_sgu_ln_bias, m_sgu_w_spatial, m_sgu_b_spatial, m_attn_out_norm, m_sgu_out_norm, m_w_out, m_post_mix_norm, m_pre_ffn_norm, m_w_gate, m_w_up, m_w_down, m_post_ffn_norm, v_pre_mix_norm, v_w_in, v_sgu_ln_gain, v_sgu_ln_bias, v_sgu_w_spatial, v_sgu_b_spatial, v_attn_out_norm, v_sgu_out_norm, v_w_out, v_post_mix_norm, v_pre_ffn_norm, v_w_gate, v_w_up, v_w_down, v_post_ffn_norm):
    small_w = dict(pre_mix_norm=pre_mix_norm, sgu_ln_gain=sgu_ln_gain, sgu_ln_bias=sgu_ln_bias, sgu_w_spatial=sgu_w_spatial,
                   sgu_b_spatial=sgu_b_spatial, attn_out_norm=attn_out_norm, sgu_out_norm=sgu_out_norm,
                   post_mix_norm=post_mix_norm, pre_ffn_norm=pre_ffn_norm, post_ffn_norm=post_ffn_norm)
    small_m = dict(pre_mix_norm=m_pre_mix_norm, sgu_ln_gain=m_sgu_ln_gain, sgu_ln_bias=m_sgu_ln_bias, sgu_w_spatial=m_sgu_w_spatial,
                   sgu_b_spatial=m_sgu_b_spatial, attn_out_norm=m_attn_out_norm, sgu_out_norm=m_sgu_out_norm,
                   post_mix_norm=m_post_mix_norm, pre_ffn_norm=m_pre_ffn_norm, post_ffn_norm=m_post_ffn_norm)
    small_v = dict(pre_mix_norm=v_pre_mix_norm, sgu_ln_gain=v_sgu_ln_gain, sgu_ln_bias=v_sgu_ln_bias, sgu_w_spatial=v_sgu_w_spatial,
                   sgu_b_spatial=v_sgu_b_spatial, attn_out_norm=v_attn_out_norm, sgu_out_norm=v_sgu_out_norm,
                   post_mix_norm=v_post_mix_norm, pre_ffn_norm=v_pre_ffn_norm, post_ffn_norm=v_post_ffn_norm)
    for table in (small_w, small_m, small_v):
        table["loss_sum"] = jnp.zeros((1,), F32)

    x2d = x[0]
    target = loss_target[0]
    pos_col = positions.reshape(SEQ, 1)
    rot = _rot_consts()
    w_sp = sgu_w_spatial[0]
    bfull = jnp.repeat(sgu_b_spatial[0].T, HEAD_DIM, axis=1)

    x_i, y_i, c_i = (lax.axis_index(a).astype(jnp.int32) for a in MESH_AXES)
    dev = 4 * x_i + 2 * y_i + c_i
    core = c_i.reshape(1)
    chip = 2 * x_i + y_i
    chip_ids = jnp.stack([chip, chip ^ 1, chip ^ 2, chip ^ 3])
    dev_ids = jnp.stack([dev ^ m for m in range(N_DEV)])

    w_in_t, w_gate_t, w_up_t, w_out_f, w_down_f = place_shards(
        [w_in[0].T, w_gate[0].T, w_up[0].T, w_out[0], w_down[0]], dev.reshape(1))
    def gathered(name, bufs):
        return by_sequencer(name, [gather(bufs, "end")], TO_GATHER)[0][0]

    def from_sibling(name, grads):
        return by_sequencer(name, [to_sibling(grads)], TO_SIBLING)[0][1]

    def from_chips(name, parts):
        return by_sequencer(name, [to_chips(parts)], TO_CHIPS)[0][1]

    (w_in_t,) = gathered("gather_w_in", [w_in_t])
    (w_out_f,) = gathered("gather_w_out", [w_out_f])
    w_gate_t, w_up_t = gathered("gather_w_gate_up", [w_gate_t, w_up_t])
    (w_down_f,) = gathered("gather_w_down", [w_down_f])

    (h1, q, k, v, u, vs), _ = in_proj(x2d, pos_col, pre_mix_norm, w_in_t, rot)
    q, k, v = (_to_residue_order(t) for t in (q, k, v))
    (attn_r, lse), _ = attn_fwd(q, k, v)
    attn = _from_residue_order(attn_r)
    (sgu,), _ = sgu_fwd(u, vs, sgu_ln_gain, sgu_ln_bias, w_sp, bfull)
    (mix, y, x2, h2), _ = out_proj(attn, sgu, x2d, attn_out_norm, sgu_out_norm, w_out_f, post_mix_norm, pre_ffn_norm)
    (gate, up, act), _ = ffn_up(h2, w_gate_t, w_up_t)
    df, dx3, d_post_ffn, sq_err = ffn_down_loss(act, w_down_f, x2, post_ffn_norm, target)

    g_w_down, _ = weight_grad("grad_w_down", act, df)
    (s_down,) = from_sibling("w_down_to_sibling", [g_w_down])
    (dgate, dup), _ = ffn_act_bwd(df, w_down_f, gate, up, after=[g_w_down])
    p_down = sum_cores("sum_cores_down", g_w_down, s_down, core, after=[dgate])
    (c_down,) = from_chips("w_down_to_chips", [p_down])
    g_w_gate, _ = weight_grad("grad_w_gate", dgate, h2, after=[p_down])
    g_w_up, _ = weight_grad("grad_w_up", dup, h2, after=[g_w_gate])
    s_gate, s_up = from_sibling("w_gate_up_to_sibling", [g_w_gate, g_w_up])
    (dx2, dy, d_pre_ffn, d_post_mix), _ = ffn_in_bwd(
        dgate, dup, w_gate_t, w_up_t, x2, pre_ffn_norm, dx3, y, post_mix_norm, after=[g_w_gate, g_w_up])
    p_gate = sum_cores("sum_cores_gate", g_w_gate, s_gate, core, after=[dy])
    p_up = sum_cores("sum_cores_up", g_w_up, s_up, core, after=[dy])
    c_gate, c_up = from_chips("w_gate_up_to_chips", [p_gate, p_up])
    g_w_out, _ = weight_grad("grad_w_out", mix, dy, after=[p_gate, p_up])
    (dattn, dsgu, d_attn_out, d_sgu_out), _ = mix_bwd(dy, w_out_f, attn, sgu, attn_out_norm, sgu_out_norm, after=[g_w_out])
    (du, dvs, d_ln_gain, d_ln_bias, d_w_sp, d_bfull), _ = sgu_bwd(u, vs, dsgu, sgu_ln_gain, sgu_ln_bias, w_sp, bfull)
    (dq, dk, dv), _ = attn_bwd(q, k, v, attn_r, lse, _to_residue_order(dattn), _to_residue_order(pos_col), rot)
    dq, dk, dv = (_from_residue_order(t) for t in (dq, dk, dv))
    dproj = jnp.concatenate([dq, dk, dv, du, dvs], axis=1)
    g_w_in, _ = weight_grad("grad_w_in", dproj, h1)
    s_in, s_out = from_sibling("w_in_out_to_sibling", [g_w_in, g_w_out])
    (grad_x, d_pre_mix), _ = in_bwd(dproj, w_in_t, x2d, pre_mix_norm, dx2, after=[g_w_in])
    p_in = sum_cores("sum_cores_in", g_w_in, s_in, core, after=[d_pre_mix])
    p_out = sum_cores("sum_cores_out", g_w_out, s_out, core, after=[d_pre_mix])

    d_b_sp = d_bfull.reshape(CHUNK, N_GROUPS, HEAD_DIM).sum(axis=-1).T
    small_g = _pack_small(dict(pre_mix_norm=d_pre_mix, sgu_ln_gain=d_ln_gain, sgu_ln_bias=d_ln_bias, sgu_w_spatial=d_w_sp,
                               sgu_b_spatial=d_b_sp, attn_out_norm=d_attn_out, sgu_out_norm=d_sgu_out,
                               post_mix_norm=d_post_mix, pre_ffn_norm=d_pre_ffn, post_ffn_norm=d_post_ffn,
                               loss_sum=sq_err))
    small_g = small_g.reshape(N_DEV, SMALL_ROWS // N_DEV, 128)
    (_, (c_in, c_out)), (_, (o_small,)) = by_sequencer(
        "last_sums_to_owners", [to_chips([p_in, p_out]), to_owners(small_g)], TO_ALL)
    all_small = sum_owned("sum_small", small_g, o_small, dev_ids)
    (all_small,) = gathered("gather_small_grads", [all_small])

    big, last = {}, p_in
    for name, w, p, c, m, vv, transposed in (
            ("w_down", w_down, p_down, c_down, m_w_down, v_w_down, False), ("w_gate", w_gate, p_gate, c_gate, m_w_gate, v_w_gate, True),
            ("w_up", w_up, p_up, c_up, m_w_up, v_w_up, True), ("w_out", w_out, p_out, c_out, m_w_out, v_w_out, False),
            ("w_in", w_in, p_in, c_in, m_w_in, v_w_in, True)):
        turn = (lambda t: t.T) if transposed else (lambda t: t)
        outs = adamw_of_sums("adamw_" + name, p, c, chip_ids, turn(w[0]), turn(m[0]), turn(vv[0]), last)
        big[name], last = tuple(turn(t)[None] for t in outs), outs[0]
    sd, snm, snv = adamw("adamw_small", _pack_small(small_w), all_small, _pack_small(small_m), _pack_small(small_v))
    sg, sd, snm, snv = (_unpack_small(t, small_w) for t in (all_small, sd, snm, snv))
    loss = sg["loss_sum"][0] * np.float32(0.5 / D_MODEL)

    names = ["pre_mix_norm", "w_in", "sgu_ln_gain", "sgu_ln_bias", "sgu_w_spatial", "sgu_b_spatial", "attn_out_norm",
             "sgu_out_norm", "w_out", "post_mix_norm", "pre_ffn_norm", "w_gate", "w_up", "w_down", "post_ffn_norm"]
    outs = [loss, grad_x[None]]
    for i, table in enumerate((sg, sd, snm, snv)):
        for name in names:
            outs.append(big[name][i] if name in big else table[name])
    return tuple(outs)
```

```python
import functools

import numpy as np
import jax
import jax.numpy as jnp
from jax import lax
from jax.experimental import pallas as pl
from jax.experimental.pallas import tpu as pltpu
from jax.experimental.pallas import tpu_sc as plsc

F32 = jnp.float32
BF16 = jnp.bfloat16

SEQ = 2048
D_MODEL = 1024
ATTN_W = 512
SGU_W = 512
HEAD_DIM = 64
N_GROUPS = 8
CHUNK = 128
D_FF = 2816
IN_W = 3 * ATTN_W + 2 * SGU_W
DILATIONS = (1, 4, 16)
ROPE_THETA = 500000.0
ROT_DIM = 16
ROT_HALF = 8
RMS_EPS = 1e-6
LN_EPS = 1e-5
Q_SCALE = 0.125
NEG = -1e30

N_DEV = 8
MESH_AXES = ("x", "y", "c")
MESH = pl.DeviceIdType.MESH

ADAM_LR = 0.001
ADAM_B1 = 0.9
ADAM_B2 = 0.999
ADAM_EPS = 1e-08
ADAM_WD = 0.01
ADAM_STEP = 10

VMEM_LIMIT = 60 * 1024 * 1024
ANY = pl.BlockSpec(memory_space=pl.ANY)

SMALL = (("pre_mix_norm", 1024), ("sgu_ln_gain", 512), ("sgu_ln_bias", 512), ("sgu_w_spatial", 8 * 128 * 128),
         ("sgu_b_spatial", 1024), ("attn_out_norm", 512), ("sgu_out_norm", 512), ("post_mix_norm", 1024),
         ("pre_ffn_norm", 1024), ("post_ffn_norm", 1024), ("loss_sum", 1))
SMALL_ROWS = 1152


def _params(sem=("arbitrary",)):
    return pltpu.CompilerParams(dimension_semantics=sem, vmem_limit_bytes=VMEM_LIMIT)


def _dot(a, b):
    return jnp.dot(a, b, preferred_element_type=F32)


def _dot_nt(a, b):
    return lax.dot_general(a, b, (((1,), (1,)), ((), ())), preferred_element_type=F32)


def _dot_tn(a, b):
    return lax.dot_general(a, b, (((0,), (0,)), ((), ())), preferred_element_type=F32)


def _rms(z):
    return lax.rsqrt(jnp.mean(z * z, axis=-1, keepdims=True) + RMS_EPS)


def _rms_bwd(z, gain, d):
    r = _rms(z)
    n = z * r
    dn = d * gain
    dz = r * (dn - n * jnp.mean(dn * n, axis=-1, keepdims=True))
    return dz, jnp.sum(d * n, axis=0, keepdims=True)


def _gelu(z):
    return 0.5 * z * (1.0 + lax.erf(z * np.float32(1.0 / np.sqrt(2.0))))


def _gelu_grad(z):
    cdf = 0.5 * (1.0 + lax.erf(z * np.float32(1.0 / np.sqrt(2.0))))
    return cdf + z * jnp.exp(-0.5 * z * z) * np.float32(1.0 / np.sqrt(2.0 * np.pi))


def _rot_tables(pos_col, invf, ma, mb):
    ang = pos_col.astype(F32) * invf
    s = jnp.sin(ang)
    return jnp.cos(ang), s * ma, s * mb


def _rot(t, c, sa, sb):
    return t * c + pltpu.roll(t, 120, 1) * sa + pltpu.roll(t, 8, 1) * sb


def _rot_t(d, c, sa, sb):
    return d * c + pltpu.roll(d * sa, 8, 1) + pltpu.roll(d * sb, 120, 1)


def _rot_consts():
    lane = np.arange(128) % HEAD_DIM
    inv_freq = (np.float32(ROPE_THETA) ** (-np.arange(0, ROT_DIM, 2, dtype=np.float32) / np.float32(ROT_DIM))).astype(np.float32)
    invf = np.where(lane < ROT_DIM, inv_freq[lane % ROT_HALF], 0.0).astype(np.float32)
    ma = np.where(lane < ROT_HALF, -1.0, 0.0).astype(np.float32)
    mb = np.where((lane >= ROT_HALF) & (lane < ROT_DIM), 1.0, 0.0).astype(np.float32)
    return jnp.asarray(invf[None]), jnp.asarray(ma[None]), jnp.asarray(mb[None])


def _row_spec(tm, w):
    return pl.BlockSpec((tm, w), lambda i: (i, 0))


def _full_spec(shape):
    return pl.BlockSpec(shape, lambda i: (0,) * len(shape))


def in_proj(x, pos_col, g1, w_in_t, rot, exchanges=()):
    tm = 512

    def body(x_ref, pos_ref, g_ref, w_ref, invf_ref, ma_ref, mb_ref, h_ref, q_ref, k_ref, v_ref, u_ref, vs_ref):
        xf = x_ref[...]
        h = (xf * _rms(xf) * g_ref[...]).astype(BF16)
        h_ref[...] = h
        proj = _dot_nt(h, w_ref[...])
        c, sa, sb = _rot_tables(pos_ref[...], invf_ref[...], ma_ref[...], mb_ref[...])
        for j in range(ATTN_W // 128):
            q_ref[:, j * 128:(j + 1) * 128] = _rot(proj[:, j * 128:(j + 1) * 128], c, sa, sb) * Q_SCALE
            k_ref[:, j * 128:(j + 1) * 128] = _rot(proj[:, ATTN_W + j * 128:ATTN_W + (j + 1) * 128], c, sa, sb)
        v_ref[...] = proj[:, 2 * ATTN_W:3 * ATTN_W]
        u_ref[...] = proj[:, 3 * ATTN_W:3 * ATTN_W + SGU_W]
        vs_ref[...] = proj[:, 3 * ATTN_W + SGU_W:]

    act = jax.ShapeDtypeStruct((SEQ, 512), F32)
    return _hosted(
        "in_proj", body, SEQ // tm,
        [_row_spec(tm, D_MODEL), _row_spec(tm, 1), _full_spec((1, D_MODEL)), _full_spec((IN_W, D_MODEL)),
         _full_spec((1, 128)), _full_spec((1, 128)), _full_spec((1, 128))],
        [_row_spec(tm, D_MODEL)] + [_row_spec(tm, 512)] * 5,
        [jax.ShapeDtypeStruct((SEQ, D_MODEL), BF16)] + [act] * 5,
        (x, pos_col, g1, w_in_t, *rot), exchanges=exchanges)


RES = 16


def _to_residue_order(t):
    return t.reshape(SEQ // RES, RES, -1).transpose(1, 0, 2).reshape(t.shape)


def _from_residue_order(t):
    return t.reshape(RES, SEQ // RES, -1).transpose(1, 0, 2).reshape(t.shape)


def _block_rows(d, r, n):
    if d == 16:
        slices = [(128 * r, 128)]
    elif d == 4:
        slices = [(128 * (4 * b + r) + 32 * n, 32) for b in range(4)]
    else:
        slices = [(128 * b + 8 * n, 8) for b in range(RES)]
    return [(s if isinstance(s, int) else pl.multiple_of(s, z), z) for s, z in slices]


def _block_step(d, i):
    if d == 16:
        return i
    if d == 4:
        return 4 * (i & 31) + (i >> 5)
    return 16 * (i & 7) + (i >> 3)


def _attn_masks(d):
    row2 = _block_step(d, lax.broadcasted_iota(jnp.int32, (128, 256), 0))
    col2 = lax.broadcasted_iota(jnp.int32, (128, 256), 1)
    key2 = _block_step(d, col2 & 127)
    mask2 = jnp.logical_or(jnp.logical_and(col2 < 128, key2 >= row2), jnp.logical_and(col2 >= 128, key2 <= row2))
    row1 = _block_step(d, lax.broadcasted_iota(jnp.int32, (128, 128), 0))
    col1 = lax.broadcasted_iota(jnp.int32, (128, 128), 1)
    return col1 < HEAD_DIM, _block_step(d, col1) <= row1, mask2


def _load_rows(ref, slices):
    parts = [ref[pl.ds(s, z), :] for s, z in slices]
    return parts[0] if len(parts) == 1 else jnp.concatenate(parts, axis=0)


def _for_each_group(fn):
    for p, d in enumerate(DILATIONS):
        masks = _attn_masks(d)
        if d == 16:
            def group(i, carry, p=p, masks=masks):
                fn(p, masks, [(_block_rows(16, 4 * i + g, 0), None) for g in range(4)])
                return carry

            lax.fori_loop(0, 4, group, 0)
        elif d == 4:
            fn(p, masks, [(_block_rows(4, r, 0), None) for r in range(4)])

            def group(i, carry, p=p, masks=masks):
                fn(p, masks, [(_block_rows(4, r, i + 1), _block_rows(4, r, i)) for r in range(4)])
                return carry

            lax.fori_loop(0, 3, group, 0)
        else:
            fn(p, masks, [(_block_rows(1, 0, 0), None)])

            def group(i, carry, p=p, masks=masks):
                fn(p, masks, [(_block_rows(1, 0, 3 * i + g + 1), _block_rows(1, 0, 3 * i + g)) for g in range(3)])
                return carry

            lax.fori_loop(0, 5, group, 0)


def attn_fwd(q, k, v, exchanges=()):
    def body(q_ref, k_ref, v_ref, o_ref, lse_ref, op_ref, lp_ref):
        def group(p, masks, blocks):
            head0, mask1, mask2 = masks
            heads = (head0, jnp.logical_not(head0))
            keys = [rows if prev is None else prev + rows for rows, prev in blocks]
            mask = [mask1 if prev is None else mask2 for _, prev in blocks]
            qb = [_load_rows(q_ref, rows) for rows, _ in blocks]
            kk = [_load_rows(k_ref, ks).astype(BF16) for ks in keys]
            vv = [_load_rows(v_ref, ks).astype(BF16) for ks in keys]
            chains = [(g, hm) for g in range(len(blocks)) for hm in heads]
            s = [jnp.where(mask[g], _dot_nt(jnp.where(hm, qb[g], 0.0).astype(BF16), kk[g]), NEG) for g, hm in chains]
            m = [jnp.max(t, axis=-1, keepdims=True) for t in s]
            e = [jnp.exp(t - mt) for t, mt in zip(s, m)]
            l = [jnp.sum(t, axis=-1, keepdims=True) for t in e]
            pv = [_dot(t.astype(BF16), vv[g]) for t, (g, _) in zip(e, chains)]
            for g, (rows, _) in enumerate(blocks):
                o_blk = jnp.where(head0, pv[2 * g] / l[2 * g], pv[2 * g + 1] / l[2 * g + 1])
                l_blk = jnp.where(head0, jnp.broadcast_to(m[2 * g] + jnp.log(l[2 * g]), (128, 128)),
                                  jnp.broadcast_to(m[2 * g + 1] + jnp.log(l[2 * g + 1]), (128, 128)))
                at = 0
                for start, size in rows:
                    op_ref[p, pl.ds(start, size), :] = o_blk[at:at + size]
                    lp_ref[p, pl.ds(start, size), :] = l_blk[at:at + size]
                    at += size

        _for_each_group(group)

        def combine(i, carry):
            rows = pl.ds(pl.multiple_of(i * 256, 256), 256)
            ls = [lp_ref[p, rows, :] for p in range(3)]
            m = jnp.maximum(jnp.maximum(ls[0], ls[1]), ls[2])
            lse = m + jnp.log(jnp.exp(ls[0] - m) + jnp.exp(ls[1] - m) + jnp.exp(ls[2] - m))
            o = jnp.zeros((256, 128), F32)
            for p in range(3):
                o = o + jnp.exp(ls[p] - lse) * op_ref[p, rows, :]
            o_ref[rows, :] = o
            lse_ref[rows, :] = lse
            return carry

        lax.fori_loop(0, SEQ // 256, combine, 0)

    slab = pl.BlockSpec((SEQ, 128), lambda i: (0, i))
    out = jax.ShapeDtypeStruct((SEQ, ATTN_W), F32)
    return _hosted(
        "attn_fwd", body, ATTN_W // 128, [slab] * 3, [slab] * 2, [out, out], (q, k, v),
        scratch_shapes=[pltpu.VMEM((3, SEQ, 128), F32), pltpu.VMEM((3, SEQ, 128), F32)], exchanges=exchanges)


def _causal_weights(w_ref):
    row = lax.broadcasted_iota(jnp.int32, (CHUNK, CHUNK), 0)
    col = lax.broadcasted_iota(jnp.int32, (CHUNK, CHUNK), 1)
    return [jnp.where(col <= row, w_ref[g], 0.0).astype(BF16) for g in range(N_GROUPS)], col <= row


def _sgu_chunk_fwd(u, vs, lg, lb, wc, bfull, head0):
    ug = _gelu(u)
    vg = _gelu(vs)
    xc = vg - jnp.mean(vg, axis=-1, keepdims=True)
    rstd = lax.rsqrt(jnp.mean(xc * xc, axis=-1, keepdims=True) + LN_EPS)
    xhat = xc * rstd
    vn = xhat * lg + lb
    mixed = []
    for gp in range(SGU_W // 128):
        vp = vn[:, gp * 128:(gp + 1) * 128].astype(BF16)
        mixed.append(jnp.where(head0, _dot(wc[2 * gp], vp), _dot(wc[2 * gp + 1], vp)))
    ms = jnp.concatenate(mixed, axis=1) + bfull
    return ug, xhat, rstd, vn, ms


def sgu_fwd(u, vs, lg, lb, w_sp, bfull, exchanges=()):
    cpb = 4

    def body(u_ref, vs_ref, lg_ref, lb_ref, w_ref, b_ref, o_ref):
        wc, _ = _causal_weights(w_ref)
        head0 = lax.broadcasted_iota(jnp.int32, (CHUNK, 128), 1) < HEAD_DIM
        for ci in range(cpb):
            rows = pl.ds(ci * CHUNK, CHUNK)
            ug, _, _, _, ms = _sgu_chunk_fwd(u_ref[rows, :], vs_ref[rows, :], lg_ref[...], lb_ref[...], wc, b_ref[...], head0)
            o_ref[rows, :] = ug * ms

    tm = cpb * CHUNK
    return _hosted(
        "sgu_fwd", body, SEQ // tm,
        [_row_spec(tm, SGU_W), _row_spec(tm, SGU_W), _full_spec((1, SGU_W)), _full_spec((1, SGU_W)),
         _full_spec((N_GROUPS, CHUNK, CHUNK)), _full_spec((CHUNK, SGU_W))],
        [_row_spec(tm, SGU_W)], [jax.ShapeDtypeStruct((SEQ, SGU_W), F32)],
        (u, vs, lg, lb, w_sp, bfull), exchanges=exchanges)


def out_proj(attn, sgu, x, ga, gs, w_out, gpm, gpf, exchanges=()):
    tm = 512

    def body(a_ref, s_ref, x_ref, ga_ref, gs_ref, w_ref, gpm_ref, gpf_ref, mix_ref, y_ref, x2_ref, h2_ref):
        a = a_ref[...]
        s = s_ref[...]
        an = (a * _rms(a) * ga_ref[...]).astype(BF16)
        sn = (s * _rms(s) * gs_ref[...]).astype(BF16)
        mix_ref[:, :ATTN_W] = an
        mix_ref[:, ATTN_W:] = sn
        y = _dot(an, w_ref[:ATTN_W, :]) + _dot(sn, w_ref[ATTN_W:, :])
        y_ref[...] = y
        x2 = x_ref[...] + y * _rms(y) * gpm_ref[...]
        x2_ref[...] = x2
        h2_ref[...] = (x2 * _rms(x2) * gpf_ref[...]).astype(BF16)

    wide = jax.ShapeDtypeStruct((SEQ, D_MODEL), F32)
    wide16 = jax.ShapeDtypeStruct((SEQ, D_MODEL), BF16)
    return _hosted(
        "out_proj", body, SEQ // tm,
        [_row_spec(tm, ATTN_W), _row_spec(tm, SGU_W), _row_spec(tm, D_MODEL), _full_spec((1, ATTN_W)),
         _full_spec((1, SGU_W)), _full_spec((D_MODEL, D_MODEL)), _full_spec((1, D_MODEL)), _full_spec((1, D_MODEL))],
        [_row_spec(tm, D_MODEL)] * 4, [wide16, wide, wide, wide16],
        (attn, sgu, x, ga, gs, w_out, gpm, gpf), exchanges=exchanges)


def ffn_up(h2, w_gate_t, w_up_t, exchanges=()):
    tm = 256

    def body(h_ref, wg_ref, wu_ref, g_ref, u_ref, a_ref):
        h = h_ref[...]
        g = _dot_nt(h, wg_ref[...])
        u = _dot_nt(h, wu_ref[...])
        g_ref[...] = g.astype(BF16)
        u_ref[...] = u.astype(BF16)
        a_ref[...] = (g * jax.nn.sigmoid(g) * u).astype(BF16)

    ff = jax.ShapeDtypeStruct((SEQ, D_FF), BF16)
    return _hosted(
        "ffn_up", body, SEQ // tm,
        [_row_spec(tm, D_MODEL), _full_spec((D_FF, D_MODEL)), _full_spec((D_FF, D_MODEL))],
        [_row_spec(tm, D_FF)] * 3, [ff, ff, jax.ShapeDtypeStruct((SEQ, D_FF), BF16)],
        (h2, w_gate_t, w_up_t), exchanges=exchanges)


def ffn_down_loss(act, w_down, x2, gpo, target):
    tm = 512

    def body(a_ref, w_ref, x2_ref, g_ref, t_ref, df_ref, dx3_ref, dg_ref, loss_ref):
        f = _dot(a_ref[...], w_ref[...])
        gain = g_ref[...]
        err = x2_ref[...] + f * _rms(f) * gain - t_ref[...]
        dx3 = err * np.float32(1.0 / D_MODEL)
        dx3_ref[...] = dx3
        df, dg = _rms_bwd(f, gain, dx3)
        df_ref[...] = df.astype(BF16)

        @pl.when(pl.program_id(0) == 0)
        def _():
            dg_ref[...] = jnp.zeros_like(dg_ref)
            loss_ref[...] = jnp.zeros_like(loss_ref)

        dg_ref[...] += dg
        loss_ref[...] += jnp.sum(err * err, axis=(0, 1), keepdims=True)

    return pl.pallas_call(
        body, name="ffn_down_loss", grid=(SEQ // tm,),
        in_specs=[_row_spec(tm, D_FF), _full_spec((D_FF, D_MODEL)), _row_spec(tm, D_MODEL), _full_spec((1, D_MODEL)),
                  _row_spec(tm, D_MODEL)],
        out_specs=[_row_spec(tm, D_MODEL), _row_spec(tm, D_MODEL), _full_spec((1, D_MODEL)), _full_spec((1, 1))],
        out_shape=[jax.ShapeDtypeStruct((SEQ, D_MODEL), BF16), jax.ShapeDtypeStruct((SEQ, D_MODEL), F32),
                   jax.ShapeDtypeStruct((1, D_MODEL), F32), jax.ShapeDtypeStruct((1, 1), F32)],
        compiler_params=_params(),
    )(act, w_down, x2, gpo, target)


def ffn_act_bwd(df, w_down, gate, up, after=()):
    tm = 256

    def body(df_ref, w_ref, g_ref, u_ref, dg_ref, du_ref):
        dact = _dot_nt(df_ref[...], w_ref[...])
        g = g_ref[...].astype(F32)
        s = jax.nn.sigmoid(g)
        du_ref[...] = (dact * g * s).astype(BF16)
        dg_ref[...] = (dact * u_ref[...].astype(F32) * (s * (1.0 + g * (1.0 - s)))).astype(BF16)

    ff16 = jax.ShapeDtypeStruct((SEQ, D_FF), BF16)
    return _hosted(
        "ffn_act_bwd", body, SEQ // tm,
        [_row_spec(tm, D_MODEL), _full_spec((D_FF, D_MODEL)), _row_spec(tm, D_FF), _row_spec(tm, D_FF)],
        [_row_spec(tm, D_FF)] * 2, [ff16, ff16], (df, w_down, gate, up), after=after)


def ffn_in_bwd(dgate, dup, w_gate_t, w_up_t, x2, gpf, dx3, y, gpm, after=()):
    tm = 256

    def body(dg_ref, du_ref, wg_ref, wu_ref, x2_ref, gpf_ref, dx3_ref, y_ref, gpm_ref, dx2_ref, dy_ref, dgpf_ref, dgpm_ref):
        dh2 = _dot(dg_ref[...], wg_ref[...]) + _dot(du_ref[...], wu_ref[...])
        dz, dgpf = _rms_bwd(x2_ref[...], gpf_ref[...], dh2)
        dx2 = dx3_ref[...] + dz
        dx2_ref[...] = dx2
        dy, dgpm = _rms_bwd(y_ref[...], gpm_ref[...], dx2)
        dy_ref[...] = dy.astype(BF16)

        @pl.when(pl.program_id(0) == 0)
        def _():
            dgpf_ref[...] = jnp.zeros_like(dgpf_ref)
            dgpm_ref[...] = jnp.zeros_like(dgpm_ref)

        dgpf_ref[...] += dgpf
        dgpm_ref[...] += dgpm

    vec = jax.ShapeDtypeStruct((1, D_MODEL), F32)
    return _hosted(
        "ffn_in_bwd", body, SEQ // tm,
        [_row_spec(tm, D_FF), _row_spec(tm, D_FF), _full_spec((D_FF, D_MODEL)), _full_spec((D_FF, D_MODEL)),
         _row_spec(tm, D_MODEL), _full_spec((1, D_MODEL)), _row_spec(tm, D_MODEL), _row_spec(tm, D_MODEL),
         _full_spec((1, D_MODEL))],
        [_row_spec(tm, D_MODEL), _row_spec(tm, D_MODEL), _full_spec((1, D_MODEL)), _full_spec((1, D_MODEL))],
        [jax.ShapeDtypeStruct((SEQ, D_MODEL), F32), jax.ShapeDtypeStruct((SEQ, D_MODEL), BF16), vec, vec],
        (dgate, dup, w_gate_t, w_up_t, x2, gpf, dx3, y, gpm), after=after)


def weight_grad(name, a, b, after=()):
    m, n = a.shape[1], b.shape[1]
    tr = 256

    def body(a_ref, b_ref, o_ref):
        o_ref[...] = _dot_tn(a_ref[...], b_ref[...]).astype(BF16)

    (out,), done = _hosted(
        name, body, m // tr, [pl.BlockSpec((SEQ, tr), lambda i: (0, i)), _full_spec((SEQ, n))],
        [_row_spec(tr, n)], [jax.ShapeDtypeStruct((m, n), BF16)], (a, b), after=after)
    return out.reshape(N_DEV, m // N_DEV, n), done


def mix_bwd(dy, w_out, attn, sgu, ga, gs, after=()):
    tm = 512

    def body(dy_ref, w_ref, a_ref, s_ref, ga_ref, gs_ref, da_ref, ds_ref, dga_ref, dgs_ref):
        dy = dy_ref[...]
        da, dga = _rms_bwd(a_ref[...], ga_ref[...], _dot_nt(dy, w_ref[:ATTN_W, :]))
        ds, dgs = _rms_bwd(s_ref[...], gs_ref[...], _dot_nt(dy, w_ref[ATTN_W:, :]))
        da_ref[...] = da
        ds_ref[...] = ds

        @pl.when(pl.program_id(0) == 0)
        def _():
            dga_ref[...] = jnp.zeros_like(dga_ref)
            dgs_ref[...] = jnp.zeros_like(dgs_ref)

        dga_ref[...] += dga
        dgs_ref[...] += dgs

    half = jax.ShapeDtypeStruct((SEQ, 512), F32)
    vec = jax.ShapeDtypeStruct((1, 512), F32)
    return _hosted(
        "mix_bwd", body, SEQ // tm,
        [_row_spec(tm, D_MODEL), _full_spec((D_MODEL, D_MODEL)), _row_spec(tm, 512), _row_spec(tm, 512),
         _full_spec((1, 512)), _full_spec((1, 512))],
        [_row_spec(tm, 512), _row_spec(tm, 512), _full_spec((1, 512)), _full_spec((1, 512))],
        [half, half, vec, vec], (dy, w_out, attn, sgu, ga, gs), after=after)


def sgu_bwd(u, vs, dsgu, lg, lb, w_sp, bfull, exchanges=()):
    cpb = 4

    def body(u_ref, vs_ref, d_ref, lg_ref, lb_ref, w_ref, b_ref, du_ref, dvs_ref, dlg_ref, dlb_ref, dw_ref, db_ref):
        wc, causal = _causal_weights(w_ref)
        head0 = lax.broadcasted_iota(jnp.int32, (CHUNK, 128), 1) < HEAD_DIM
        lg = lg_ref[...]

        @pl.when(pl.program_id(0) == 0)
        def _():
            dlg_ref[...] = jnp.zeros_like(dlg_ref)
            dlb_ref[...] = jnp.zeros_like(dlb_ref)
            dw_ref[...] = jnp.zeros_like(dw_ref)
            db_ref[...] = jnp.zeros_like(db_ref)

        for ci in range(cpb):
            rows = pl.ds(ci * CHUNK, CHUNK)
            u = u_ref[rows, :]
            vs = vs_ref[rows, :]
            d = d_ref[rows, :]
            ug, xhat, rstd, vn, ms = _sgu_chunk_fwd(u, vs, lg, lb_ref[...], wc, b_ref[...], head0)
            du_ref[rows, :] = (d * ms * _gelu_grad(u)).astype(BF16)
            dms = d * ug
            db_ref[...] += dms
            dvn = []
            for gp in range(SGU_W // 128):
                dmp = dms[:, gp * 128:(gp + 1) * 128]
                dm0 = jnp.where(head0, dmp, 0.0).astype(BF16)
                dm1 = jnp.where(head0, 0.0, dmp).astype(BF16)
                vp = vn[:, gp * 128:(gp + 1) * 128].astype(BF16)
                dw_ref[2 * gp] += _dot_nt(dm0, vp)
                dw_ref[2 * gp + 1] += _dot_nt(dm1, vp)
                dvn.append(_dot_tn(wc[2 * gp], dm0) + _dot_tn(wc[2 * gp + 1], dm1))
            dvn = jnp.concatenate(dvn, axis=1)
            dlg_ref[...] += jnp.sum(dvn * xhat, axis=0, keepdims=True)
            dlb_ref[...] += jnp.sum(dvn, axis=0, keepdims=True)
            dxh = dvn * lg
            dvg = rstd * (dxh - jnp.mean(dxh, axis=-1, keepdims=True) - xhat * jnp.mean(dxh * xhat, axis=-1, keepdims=True))
            dvs_ref[rows, :] = (dvg * _gelu_grad(vs)).astype(BF16)

        @pl.when(pl.program_id(0) == pl.num_programs(0) - 1)
        def _():
            for g in range(N_GROUPS):
                dw_ref[g] = jnp.where(causal, dw_ref[g], 0.0)

    tm = cpb * CHUNK
    half16 = jax.ShapeDtypeStruct((SEQ, SGU_W), BF16)
    vec = jax.ShapeDtypeStruct((1, SGU_W), F32)
    return _hosted(
        "sgu_bwd", body, SEQ // tm,
        [_row_spec(tm, SGU_W)] * 3 + [_full_spec((1, SGU_W)), _full_spec((1, SGU_W)),
                                      _full_spec((N_GROUPS, CHUNK, CHUNK)), _full_spec((CHUNK, SGU_W))],
        [_row_spec(tm, SGU_W), _row_spec(tm, SGU_W), _full_spec((1, SGU_W)), _full_spec((1, SGU_W)),
         _full_spec((N_GROUPS, CHUNK, CHUNK)), _full_spec((CHUNK, SGU_W))],
        [half16, half16, vec, vec, jax.ShapeDtypeStruct((N_GROUPS, CHUNK, CHUNK), F32),
         jax.ShapeDtypeStruct((CHUNK, SGU_W), F32)],
        (u, vs, dsgu, lg, lb, w_sp, bfull), exchanges=exchanges)


def attn_bwd(q, k, v, o, lse, do, pos_col, rot, exchanges=()):
    def body(q_ref, k_ref, v_ref, o_ref, lse_ref, do_ref, pos_ref, invf_ref, ma_ref, mb_ref,
             dq_ref, dk_ref, dv_ref, dqa_ref, dka_ref, dva_ref, dlt_ref, rot_ref):
        dqa_ref[...] = jnp.zeros_like(dqa_ref)
        dka_ref[...] = jnp.zeros_like(dka_ref)
        dva_ref[...] = jnp.zeros_like(dva_ref)

        def delta(i, carry):
            rows = pl.ds(pl.multiple_of(i * 256, 256), 256)
            prod = do_ref[rows, :] * o_ref[rows, :]
            h0 = lax.broadcasted_iota(jnp.int32, (256, 128), 1) < HEAD_DIM
            d0 = jnp.sum(jnp.where(h0, prod, 0.0), axis=-1, keepdims=True)
            d1 = jnp.sum(jnp.where(h0, 0.0, prod), axis=-1, keepdims=True)
            dlt_ref[rows, :] = jnp.where(h0, d0, d1)
            return carry

        lax.fori_loop(0, SEQ // 256, delta, 0)

        def add_rows(ref, slices, val):
            at = 0
            for start, size in slices:
                ref[pl.ds(start, size), :] += val[at:at + size]
                at += size

        def group(p, masks, blocks):
            head0, mask1, mask2 = masks
            heads = (head0, jnp.logical_not(head0))
            keys = [rows if prev is None else prev + rows for rows, prev in blocks]
            mask = [mask1 if prev is None else mask2 for _, prev in blocks]
            kk = [_load_rows(k_ref, ks).astype(BF16) for ks in keys]
            vv = [_load_rows(v_ref, ks).astype(BF16) for ks in keys]
            qb = [_load_rows(q_ref, rows) for rows, _ in blocks]
            dob = [_load_rows(do_ref, rows) for rows, _ in blocks]
            lse_b = [_load_rows(lse_ref, rows) for rows, _ in blocks]
            dlt_b = [_load_rows(dlt_ref, rows) for rows, _ in blocks]
            chains = [(g, h) for g in range(len(blocks)) for h in range(2)]
            qm = [jnp.where(heads[h], qb[g], 0.0).astype(BF16) for g, h in chains]
            dom = [jnp.where(heads[h], dob[g], 0.0).astype(BF16) for g, h in chains]
            s = [_dot_nt(qm[c], kk[g]) for c, (g, h) in enumerate(chains)]
            dp = [_dot_nt(dom[c], vv[g]) for c, (g, h) in enumerate(chains)]
            pr = [jnp.where(mask[g], jnp.exp(s[c] - lse_b[g][:, h * HEAD_DIM:h * HEAD_DIM + 1]), 0.0)
                  for c, (g, h) in enumerate(chains)]
            ds = [(pr[c] * (dp[c] - dlt_b[g][:, h * HEAD_DIM:h * HEAD_DIM + 1])).astype(BF16)
                  for c, (g, h) in enumerate(chains)]
            dv = [_dot_tn(pr[c].astype(BF16), dom[c]) for c in range(len(chains))]
            dk = [_dot_tn(ds[c], qm[c]) for c in range(len(chains))]
            dq = [_dot(ds[c], kk[g]) for c, (g, h) in enumerate(chains)]
            for g, (rows, _) in enumerate(blocks):
                add_rows(dqa_ref, rows, jnp.where(head0, dq[2 * g], dq[2 * g + 1]))
                add_rows(dka_ref, keys[g], dk[2 * g] + dk[2 * g + 1])
                add_rows(dva_ref, keys[g], dv[2 * g] + dv[2 * g + 1])

        _for_each_group(group)

        @pl.when(pl.program_id(0) == 0)
        def _():
            def tables(i, carry):
                rows = pl.ds(pl.multiple_of(i * 256, 256), 256)
                c, sa, sb = _rot_tables(pos_ref[rows, :], invf_ref[...], ma_ref[...], mb_ref[...])
                rot_ref[0, rows, :] = c
                rot_ref[1, rows, :] = sa
                rot_ref[2, rows, :] = sb
                return carry

            lax.fori_loop(0, SEQ // 256, tables, 0)

        def finish(i, carry):
            rows = pl.ds(pl.multiple_of(i * 256, 256), 256)
            c, sa, sb = rot_ref[0, rows, :], rot_ref[1, rows, :], rot_ref[2, rows, :]
            dq_ref[rows, :] = _rot_t(dqa_ref[rows, :] * Q_SCALE, c, sa, sb).astype(BF16)
            dk_ref[rows, :] = _rot_t(dka_ref[rows, :], c, sa, sb).astype(BF16)
            dv_ref[rows, :] = dva_ref[rows, :].astype(BF16)
            return carry

        lax.fori_loop(0, SEQ // 256, finish, 0)

    slab = pl.BlockSpec((SEQ, 128), lambda i: (0, i))
    out = jax.ShapeDtypeStruct((SEQ, ATTN_W), BF16)
    acc = pltpu.VMEM((SEQ, 128), F32)
    return _hosted(
        "attn_bwd", body, ATTN_W // 128,
        [slab] * 6 + [_full_spec((SEQ, 1)), _full_spec((1, 128)), _full_spec((1, 128)), _full_spec((1, 128))],
        [slab] * 3, [out, out, out], (q, k, v, o, lse, do, pos_col, *rot),
        scratch_shapes=[acc, acc, acc, acc, pltpu.VMEM((3, SEQ, 128), F32)], exchanges=exchanges)


def in_bwd(dproj, w_in_t, x, g1, dx2, after=()):
    tm = 512

    def body(dp_ref, w_ref, x_ref, g_ref, dx2_ref, dx_ref, dg_ref):
        dh1 = _dot(dp_ref[...], w_ref[...])
        dz, dg = _rms_bwd(x_ref[...], g_ref[...], dh1)
        dx_ref[...] = dx2_ref[...] + dz

        @pl.when(pl.program_id(0) == 0)
        def _():
            dg_ref[...] = jnp.zeros_like(dg_ref)

        dg_ref[...] += dg

    return _hosted(
        "in_bwd", body, SEQ // tm,
        [_row_spec(tm, IN_W), _full_spec((IN_W, D_MODEL)), _row_spec(tm, D_MODEL), _full_spec((1, D_MODEL)),
         _row_spec(tm, D_MODEL)],
        [_row_spec(tm, D_MODEL), _full_spec((1, D_MODEL))],
        [jax.ShapeDtypeStruct((SEQ, D_MODEL), F32), jax.ShapeDtypeStruct((1, D_MODEL), F32)],
        (dproj, w_in_t, x, g1, dx2), after=after)


def _coords():
    return lax.axis_index("x"), lax.axis_index("y"), lax.axis_index("c")


class Exchange:
    def __init__(self, srcs, bufs, new_shapes, n_sems, make, mid_step=None):
        self.srcs, self.bufs, self.new_shapes, self.n_sems, self.make = list(srcs), list(bufs), list(new_shapes), n_sems, make
        self.mid_step = mid_step


def _hosted(name, body, n_steps, in_specs, out_specs, out_shape, args, scratch_shapes=(), exchanges=(), prefetch=None,
            after=()):
    out_shape, out_specs = list(out_shape), list(out_specs)
    srcs = [a for ex in exchanges for a in ex.srcs] + list(after)
    bufs = [a for ex in exchanges for a in ex.bufs]
    news = [s for ex in exchanges for s in ex.new_shapes]
    n_pre = 0 if prefetch is None else 1
    n_in, n_out, n_scr = len(args), len(out_shape), len(scratch_shapes)

    def wrapped(*refs):
        refs = list(refs)
        pre, refs = refs[:n_pre], refs[n_pre:]
        ins = refs[:n_in]
        src_refs = refs[n_in:n_in + len(srcs)]
        at = n_in + len(srcs) + len(bufs)
        outs = refs[at:at + n_out]
        buf_refs = refs[at + n_out:at + n_out + len(bufs)]
        new_refs = refs[at + n_out + len(bufs):at + n_out + len(bufs) + len(news)]
        at += n_out + len(bufs) + len(news)
        scratch, sems = refs[at:at + n_scr], refs[at + n_scr:]

        def copies(phase, which):
            made, si, bi, ni = [], 0, 0, 0
            for k, ex in enumerate(exchanges):
                if which(ex):
                    made.append(ex.make(phase, src_refs[si:si + len(ex.srcs)], buf_refs[bi:bi + len(ex.bufs)],
                                        new_refs[ni:ni + len(ex.new_shapes)], sems[2 * k], sems[2 * k + 1]))
                si, bi, ni = si + len(ex.srcs), bi + len(ex.bufs), ni + len(ex.new_shapes)
            return made

        if exchanges:
            @pl.when(pl.program_id(0) == 0)
            def _():
                for starts in copies("start", lambda ex: True):
                    for cp in starts:
                        cp.start()

        def pass_on(mid):
            for arrivals, starts in copies("middle", lambda ex: ex.mid_step == mid):
                for cp in arrivals:
                    cp.wait_recv()
                for cp in starts:
                    cp.start()

        for mid in sorted({ex.mid_step for ex in exchanges if isinstance(ex.mid_step, int)}):
            @pl.when(pl.program_id(0) == mid)
            def _(mid=mid):
                pass_on(mid)

        body(*pre, *ins, *outs, *scratch)

        if exchanges:
            @pl.when(pl.program_id(0) == n_steps - 1)
            def _():
                pass_on("end")
                for sends, recvs in copies("end", lambda ex: True):
                    for cp in sends:
                        cp.wait_send()
                    for cp in recvs:
                        cp.wait_recv()

    sem_shapes = []
    for ex in exchanges:
        sem_shapes += [pltpu.SemaphoreType.DMA((ex.n_sems,)), pltpu.SemaphoreType.DMA((ex.n_sems,))]
    all_in = list(in_specs) + [ANY] * (len(srcs) + len(bufs))
    all_out = out_specs + [ANY] * (len(bufs) + len(news))
    all_shape = out_shape + [jax.ShapeDtypeStruct(b.shape, b.dtype) for b in bufs] + news
    all_scratch = list(scratch_shapes) + sem_shapes
    aliases = {n_pre + n_in + len(srcs) + i: n_out + i for i in range(len(bufs))}
    if prefetch is None:
        call = pl.pallas_call(wrapped, name=name, grid=(n_steps,), in_specs=all_in, out_specs=all_out, out_shape=all_shape,
                              scratch_shapes=all_scratch, input_output_aliases=aliases, compiler_params=_params())
        outs = call(*args, *srcs, *bufs)
    else:
        spec = pltpu.PrefetchScalarGridSpec(num_scalar_prefetch=1, grid=(n_steps,), in_specs=all_in, out_specs=all_out,
                                            scratch_shapes=all_scratch)
        call = pl.pallas_call(wrapped, name=name, grid_spec=spec, out_shape=all_shape, input_output_aliases=aliases,
                              compiler_params=_params())
        outs = call(prefetch, *args, *srcs, *bufs)
    results, bi, ni = [], n_out, n_out + len(bufs)
    for ex in exchanges:
        results.append((list(outs[bi:bi + len(ex.bufs)]), list(outs[ni:ni + len(ex.new_shapes)])))
        bi, ni = bi + len(ex.bufs), ni + len(ex.new_shapes)
    return list(outs[:n_out]), results


def _gather_copies(kinds, bufs, ranges, send_sems, recv_sems):
    x, y, c = _coords()
    me, sibling = (x, y, c), (x, y, 1 - c)
    chips = [(1 - x, y), (x, 1 - y), (1 - x, 1 - y)]

    def copy(a, k, block, to):
        lo, hi = ranges[a]
        r = bufs[a].shape[0] // N_DEV
        rows = bufs[a].at[pl.ds((4 * block[0] + 2 * block[1] + block[2]) * r + lo, hi - lo), :]
        return pltpu.make_async_remote_copy(src_ref=rows, dst_ref=rows, send_sem=send_sems.at[7 * a + k],
                                            recv_sem=recv_sems.at[7 * a + k], device_id=to, device_id_type=MESH)

    every = range(len(bufs))
    make = {
        "out": lambda: [copy(a, 0, me, sibling) for a in every]
        + [copy(a, 1 + j, me, (*chip, c)) for a in every for j, chip in enumerate(chips)],
        "from_core": lambda: [copy(a, 0, sibling, me) for a in every],
        "from_chips": lambda: [copy(a, 1 + j, (*chip, c), me) for a in every for j, chip in enumerate(chips)],
        "on": lambda: [copy(a, 4 + j, (*chip, c), sibling) for a in every for j, chip in enumerate(chips)],
        "on_in": lambda: [copy(a, 4 + j, (*chip, 1 - c), me) for a in every for j, chip in enumerate(chips)],
    }
    return [make[kind]() for kind in kinds]


def gather(bufs, mid_step, ranges=None):
    ranges = ranges or [(0, b.shape[0] // N_DEV) for b in bufs]

    def make(phase, src_refs, buf_refs, new_refs, send_sems, recv_sems):
        kinds = {"start": ["out"], "middle": ["from_chips", "on"], "end": ["out", "on", "from_core", "on_in"]}[phase]
        made = _gather_copies(kinds, buf_refs, ranges, send_sems, recv_sems)
        if phase == "start":
            return made[0]
        if phase == "middle":
            return made[0], made[1]
        return made[0] + made[1], made[2] + made[3]

    return Exchange([], bufs, [], 7 * len(bufs), make, mid_step=mid_step)


def all_gather_in_place(name, bufs):
    n = len(bufs)
    ranges = [(0, b.shape[0] // N_DEV) for b in bufs]

    def body(*refs):
        outs, send_sems, recv_sems = refs[n:2 * n], refs[2 * n], refs[2 * n + 1]
        out, from_core, from_chips, on, on_in = _gather_copies(
            ["out", "from_core", "from_chips", "on", "on_in"], outs, ranges, send_sems, recv_sems)
        for cp in out:
            cp.start()
        for cp in from_chips:
            cp.wait_recv()
        for cp in on:
            cp.start()
        for cp in from_core + on_in:
            cp.wait_recv()
        for cp in out + on:
            cp.wait_send()

    return pl.pallas_call(
        body, name=name, in_specs=[ANY] * n, out_specs=[ANY] * n,
        out_shape=[jax.ShapeDtypeStruct(b.shape, b.dtype) for b in bufs],
        scratch_shapes=[pltpu.SemaphoreType.DMA((7 * n,)), pltpu.SemaphoreType.DMA((7 * n,))],
        input_output_aliases={i: i for i in range(n)},
    )(*bufs)


TO_GATHER = (1, lambda x, y, c: [(x, y, 1 - c), (1 - x, y, c), (x, 1 - y, c), (1 - x, 1 - y, c)])
TO_SIBLING = (2, lambda x, y, c: [(x, y, 1 - c)])
TO_CHIPS = (3, lambda x, y, c: [(1 - x, y, c), (x, 1 - y, c), (1 - x, 1 - y, c)])
TO_ALL = (4, lambda x, y, c: [(x ^ (m >> 2), y ^ ((m >> 1) & 1), c ^ (m & 1)) for m in range(1, N_DEV)])


def by_sequencer(name, exchanges, who):
    collective_id, peers_of = who
    hbm = pltpu.MemorySpace.HBM
    refs = [([jax.new_ref(a, memory_space=hbm) for a in ex.srcs], [jax.new_ref(a, memory_space=hbm) for a in ex.bufs],
             [jax.empty_ref(s, memory_space=hbm) for s in ex.new_shapes]) for ex in exchanges]
    sems = []
    for ex in exchanges:
        sems += [pltpu.SemaphoreType.DMA((ex.n_sems,)), pltpu.SemaphoreType.DMA((ex.n_sems,))]

    @pl.kernel(mesh=plsc.ScalarSubcoreMesh(axis_name="sequencer", num_cores=1), name=name, scratch_types=tuple(sems),
               compiler_params=pltpu.CompilerParams(collective_id=collective_id))
    def launch(*sem_refs):
        peers = peers_of(*_coords())
        barrier = pltpu.get_barrier_semaphore()
        for peer in peers:
            pl.semaphore_signal(barrier, inc=1, device_id=peer, device_id_type=MESH)
        pl.semaphore_wait(barrier, len(peers))

        def make(phase, k):
            return exchanges[k].make(phase, *refs[k], sem_refs[2 * k], sem_refs[2 * k + 1])

        for k in range(len(exchanges)):
            for cp in make("start", k):
                cp.start()
        for k, ex in enumerate(exchanges):
            if ex.mid_step is not None:
                arrivals, starts = make("middle", k)
                for cp in arrivals:
                    cp.wait_recv()
                for cp in starts:
                    cp.start()
        for k in range(len(exchanges)):
            sends, arrivals = make("end", k)
            for cp in arrivals:
                cp.wait_recv()
            for cp in sends:
                cp.wait_send()

    launch()
    return [([ref[...] for ref in bufs], [ref[...] for ref in news]) for _, bufs, news in refs]


def place_shards(shards, dev):
    n = len(shards)

    def body(dev_ref, *refs):
        for a in range(n):
            refs[n + a][...] = refs[a][...].astype(BF16)

    spec = pltpu.PrefetchScalarGridSpec(
        num_scalar_prefetch=1, grid=(1,),
        in_specs=[pl.BlockSpec(s.shape, lambda i, dev_ref: (0, 0)) for s in shards],
        out_specs=[pl.BlockSpec(s.shape, lambda i, dev_ref: (dev_ref[0], 0)) for s in shards])
    return pl.pallas_call(
        body, name="place_shards", grid_spec=spec,
        out_shape=[jax.ShapeDtypeStruct((N_DEV * s.shape[0], s.shape[1]), BF16) for s in shards],
        compiler_params=_params(),
    )(dev, *shards)


def _swap(copies_of):
    def make(phase, src_refs, buf_refs, new_refs, send_sems, recv_sems):
        copies = copies_of(src_refs, new_refs, send_sems, recv_sems)
        return copies if phase == "start" else (copies, copies)

    return make


def to_sibling(grads):
    def copies_of(src_refs, new_refs, send_sems, recv_sems):
        x, y, c = _coords()
        return [pltpu.make_async_remote_copy(
            src_ref=src_refs[a].at[2 * xy + 1 - c], dst_ref=new_refs[a].at[xy], send_sem=send_sems.at[4 * a + xy],
            recv_sem=recv_sems.at[4 * a + xy], device_id=(x, y, 1 - c), device_id_type=MESH)
            for a in range(len(src_refs)) for xy in range(4)]

    return Exchange(grads, [], [jax.ShapeDtypeStruct((4,) + g.shape[1:], g.dtype) for g in grads], 4 * len(grads),
                    _swap(copies_of))


def to_chips(parts):
    def copies_of(src_refs, new_refs, send_sems, recv_sems):
        x, y, c = _coords()
        chips = [(1 - x, y), (x, 1 - y), (1 - x, 1 - y)]
        return [pltpu.make_async_remote_copy(
            src_ref=src_refs[a].at[2 * px + py], dst_ref=new_refs[a].at[2 * x + y], send_sem=send_sems.at[3 * a + j],
            recv_sem=recv_sems.at[3 * a + j], device_id=(px, py, c), device_id_type=MESH)
            for a in range(len(src_refs)) for j, (px, py) in enumerate(chips)]

    return Exchange(parts, [], [jax.ShapeDtypeStruct(p.shape, p.dtype) for p in parts], 3 * len(parts), _swap(copies_of))


def to_owners(grad):
    def copies_of(src_refs, new_refs, send_sems, recv_sems):
        x, y, c = _coords()
        copies = []
        for m in range(1, N_DEV):
            px, py, pc = x ^ (m >> 2), y ^ ((m >> 1) & 1), c ^ (m & 1)
            copies.append(pltpu.make_async_remote_copy(
                src_ref=src_refs[0].at[4 * px + 2 * py + pc], dst_ref=new_refs[0].at[4 * x + 2 * y + c],
                send_sem=send_sems.at[m - 1], recv_sem=recv_sems.at[m - 1], device_id=(px, py, pc), device_id_type=MESH))
        return copies

    return Exchange([grad], [], [jax.ShapeDtypeStruct(grad.shape, grad.dtype)], N_DEV - 1, _swap(copies_of))


def exchange_only(name, exchanges):
    def body():
        pass

    return _hosted(name, body, 1, [], [], [], [], exchanges=exchanges)[1]


def sum_cores(name, grad, other, core, after=()):
    _, r, w = other.shape

    def body(core_ref, g_ref, o_ref, *rest):
        rest[-1][...] = (g_ref[...].astype(F32) + o_ref[...].astype(F32)).astype(rest[-1].dtype)

    return pl.pallas_call(
        body, name=name,
        grid_spec=pltpu.PrefetchScalarGridSpec(
            num_scalar_prefetch=1, grid=(4,),
            in_specs=[pl.BlockSpec((1, r, w), lambda i, core_ref: (2 * i + core_ref[0], 0, 0)),
                      pl.BlockSpec((1, r, w), lambda i, core_ref: (i, 0, 0))] + [ANY] * len(after),
            out_specs=pl.BlockSpec((1, r, w), lambda i, core_ref: (i, 0, 0))),
        out_shape=jax.ShapeDtypeStruct(other.shape, other.dtype),
        compiler_params=_params(),
    )(core, grad, other, *after)


def sum_owned(name, grad, others, dev_ids):
    _, r, w = grad.shape

    def body(ids_ref, *refs):
        acc = refs[0][0]
        for k in range(1, N_DEV):
            acc = acc + refs[k][0]
        refs[N_DEV][...] = acc

    def pick(k):
        return pl.BlockSpec((1, r, w), lambda i, ids_ref: (ids_ref[k], 0, 0))

    return pl.pallas_call(
        body, name=name,
        grid_spec=pltpu.PrefetchScalarGridSpec(
            num_scalar_prefetch=1, grid=(1,), in_specs=[pick(k) for k in range(N_DEV)],
            out_specs=pl.BlockSpec((r, w), lambda i, ids_ref: (ids_ref[0], 0))),
        out_shape=jax.ShapeDtypeStruct((N_DEV * r, w), F32),
        compiler_params=_params(),
    )(dev_ids, grad, *([others] * (N_DEV - 1)))


def _adamw_update(w, g, m, v):
    nm = ADAM_B1 * m + np.float32(1.0 - ADAM_B1) * g
    nv = ADAM_B2 * v + np.float32(1.0 - ADAM_B2) * (g * g)
    m_hat = nm / np.float32(1.0 - ADAM_B1 ** ADAM_STEP)
    v_hat = nv / np.float32(1.0 - ADAM_B2 ** ADAM_STEP)
    return -ADAM_LR * (m_hat / (jnp.sqrt(v_hat) + ADAM_EPS) + ADAM_WD * w), nm, nv


def adamw_of_sums(name, part, others, chip_ids, w, m, v, after):
    _, r, wd = part.shape

    def body(ids_ref, p_ref, a_ref, b_ref, c_ref, w_ref, m_ref, v_ref, after_ref, g_ref, d_ref, nm_ref, nv_ref):
        g = ((p_ref[0].astype(F32) + a_ref[0].astype(F32)) + b_ref[0].astype(F32)) + c_ref[0].astype(F32)
        g_ref[...] = g
        d_ref[...], nm_ref[...], nv_ref[...] = _adamw_update(w_ref[...], g, m_ref[...], v_ref[...])

    def pick(k):
        return pl.BlockSpec((1, r, wd), lambda i, ids_ref: (ids_ref[k], 0, 0))

    whole = pl.BlockSpec((r, wd), lambda i, ids_ref: (0, 0))
    shape = jax.ShapeDtypeStruct((r, wd), F32)
    return pl.pallas_call(
        body, name=name,
        grid_spec=pltpu.PrefetchScalarGridSpec(
            num_scalar_prefetch=1, grid=(1,), in_specs=[pick(0), pick(1), pick(2), pick(3), whole, whole, whole, ANY],
            out_specs=[whole] * 4),
        out_shape=[shape] * 4,
        compiler_params=_params(),
    )(chip_ids, part, others, others, others, w, m, v, after)


def adamw(name, w, g, m, v):
    def body(w_ref, g_ref, m_ref, v_ref, d_ref, nm_ref, nv_ref):
        d_ref[...], nm_ref[...], nv_ref[...] = _adamw_update(w_ref[...], g_ref[...], m_ref[...], v_ref[...])

    shape = jax.ShapeDtypeStruct(w.shape, F32)
    spec = _full_spec(w.shape)
    return pl.pallas_call(
        body, name=name, grid=(1,), in_specs=[spec] * 4, out_specs=[spec] * 3, out_shape=[shape] * 3,
        compiler_params=_params(),
    )(w, g, m, v)


def _pack_small(parts):
    flat = jnp.concatenate([parts[name].reshape(-1) for name, _ in SMALL])
    return jnp.pad(flat, (0, SMALL_ROWS * 128 - flat.shape[0])).reshape(SMALL_ROWS, 128)


def _unpack_small(packed, like):
    flat = packed.reshape(-1)
    out, at = {}, 0
    for name, size in SMALL:
        out[name] = flat[at:at + size].reshape(like[name].shape)
        at += size
    return out


def kernel(x, positions, pre_mix_norm, w_in, sgu_ln_gain, sgu_ln_bias, sgu_w_spatial, sgu_b_spatial, attn_out_norm, sgu_out_norm, w_out, post_mix_norm, pre_ffn_norm, w_gate, w_up, w_down, post_ffn_norm, loss_target, m_pre_mix_norm, m_w_in, m_sgu_ln_gain, m_sgu_ln_bias, m_sgu_w_spatial, m_sgu_b_spatial, m_attn_out_norm, m_sgu_out_norm, m_w_out, m_post_mix_norm, m_pre_ffn_norm, m_w_gate, m_w_up, m_w_down, m_post_ffn_norm, v_pre_mix_norm, v_w_in, v_sgu_ln_gain, v_sgu_ln_bias, v_sgu_w_spatial, v_sgu_b_spatial, v_attn_out_norm, v_sgu_out_norm, v_w_out, v_post_mix_norm, v_pre_ffn_norm, v_w_gate, v_w_up, v_w_down, v_post_ffn_norm):
    small_w = dict(pre_mix_norm=pre_mix_norm, sgu_ln_gain=sgu_ln_gain, sgu_ln_bias=sgu_ln_bias, sgu_w_spatial=sgu_w_spatial,
                   sgu_b_spatial=sgu_b_spatial, attn_out_norm=attn_out_norm, sgu_out_norm=sgu_out_norm,
                   post_mix_norm=post_mix_norm, pre_ffn_norm=pre_ffn_norm, post_ffn_norm=post_ffn_norm)
    small_m = dict(pre_mix_norm=m_pre_mix_norm, sgu_ln_gain=m_sgu_ln_gain, sgu_ln_bias=m_sgu_ln_bias, sgu_w_spatial=m_sgu_w_spatial,
                   sgu_b_spatial=m_sgu_b_spatial, attn_out_norm=m_attn_out_norm, sgu_out_norm=m_sgu_out_norm,
                   post_mix_norm=m_post_mix_norm, pre_ffn_norm=m_pre_ffn_norm, post_ffn_norm=m_post_ffn_norm)
    small_v = dict(pre_mix_norm=v_pre_mix_norm, sgu_ln_gain=v_sgu_ln_gain, sgu_ln_bias=v_sgu_ln_bias, sgu_w_spatial=v_sgu_w_spatial,
                   sgu_b_spatial=v_sgu_b_spatial, attn_out_norm=v_attn_out_norm, sgu_out_norm=v_sgu_out_norm,
                   post_mix_norm=v_post_mix_norm, pre_ffn_norm=v_pre_ffn_norm, post_ffn_norm=v_post_ffn_norm)
    for table in (small_w, small_m, small_v):
        table["loss_sum"] = jnp.zeros((1,), F32)

    x2d = x[0]
    target = loss_target[0]
    pos_col = positions.reshape(SEQ, 1)
    rot = _rot_consts()
    w_sp = sgu_w_spatial[0]
    bfull = jnp.repeat(sgu_b_spatial[0].T, HEAD_DIM, axis=1)

    x_i, y_i, c_i = (lax.axis_index(a).astype(jnp.int32) for a in MESH_AXES)
    dev = 4 * x_i + 2 * y_i + c_i
    core = c_i.reshape(1)
    chip = 2 * x_i + y_i
    chip_ids = jnp.stack([chip, chip ^ 1, chip ^ 2, chip ^ 3])
    dev_ids = jnp.stack([dev ^ m for m in range(N_DEV)])

    w_in_t, w_gate_t, w_up_t, w_out_f, w_down_f = place_shards(
        [w_in[0].T, w_gate[0].T, w_up[0].T, w_out[0], w_down[0]], dev.reshape(1))
    def gathered(name, bufs):
        return by_sequencer(name, [gather(bufs, "end")], TO_GATHER)[0][0]

    def from_sibling(name, grads):
        return by_sequencer(name, [to_sibling(grads)], TO_SIBLING)[0][1]

    def from_chips(name, parts):
        return by_sequencer(name, [to_chips(parts)], TO_CHIPS)[0][1]

    (w_in_t,) = gathered("gather_w_in", [w_in_t])
    (w_out_f,) = gathered("gather_w_out", [w_out_f])
    w_gate_t, w_up_t = gathered("gather_w_gate_up", [w_gate_t, w_up_t])
    (w_down_f,) = gathered("gather_w_down", [w_down_f])

    (h1, q, k, v, u, vs), _ = in_proj(x2d, pos_col, pre_mix_norm, w_in_t, rot)
    q, k, v = (_to_residue_order(t) for t in (q, k, v))
    (attn_r, lse), _ = attn_fwd(q, k, v)
    attn = _from_residue_order(attn_r)
    (sgu,), _ = sgu_fwd(u, vs, sgu_ln_gain, sgu_ln_bias, w_sp, bfull)
    (mix, y, x2, h2), _ = out_proj(attn, sgu, x2d, attn_out_norm, sgu_out_norm, w_out_f, post_mix_norm, pre_ffn_norm)
    (gate, up, act), _ = ffn_up(h2, w_gate_t, w_up_t)
    df, dx3, d_post_ffn, sq_err = ffn_down_loss(act, w_down_f, x2, post_ffn_norm, target)

    g_w_down, _ = weight_grad("grad_w_down", act, df)
    (s_down,) = from_sibling("w_down_to_sibling", [g_w_down])
    (dgate, dup), _ = ffn_act_bwd(df, w_down_f, gate, up, after=[g_w_down])
    p_down = sum_cores("sum_cores_down", g_w_down, s_down, core, after=[dgate])
    (c_down,) = from_chips("w_down_to_chips", [p_down])
    g_w_gate, _ = weight_grad("grad_w_gate", dgate, h2, after=[p_down])
    g_w_up, _ = weight_grad("grad_w_up", dup, h2, after=[g_w_gate])
    s_gate, s_up = from_sibling("w_gate_up_to_sibling", [g_w_gate, g_w_up])
    (dx2, dy, d_pre_ffn, d_post_mix), _ = ffn_in_bwd(
        dgate, dup, w_gate_t, w_up_t, x2, pre_ffn_norm, dx3, y, post_mix_norm, after=[g_w_gate, g_w_up, c_down])
    p_gate = sum_cores("sum_cores_gate", g_w_gate, s_gate, core, after=[dy])
    p_up = sum_cores("sum_cores_up", g_w_up, s_up, core, after=[dy])
    c_gate, c_up = from_chips("w_gate_up_to_chips", [p_gate, p_up])
    g_w_out, _ = weight_grad("grad_w_out", mix, dy, after=[p_gate, p_up])
    (dattn, dsgu, d_attn_out, d_sgu_out), _ = mix_bwd(dy, w_out_f, attn, sgu, attn_out_norm, sgu_out_norm, after=[g_w_out])
    (du, dvs, d_ln_gain, d_ln_bias, d_w_sp, d_bfull), _ = sgu_bwd(u, vs, dsgu, sgu_ln_gain, sgu_ln_bias, w_sp, bfull)
    (dq, dk, dv), _ = attn_bwd(q, k, v, attn_r, lse, _to_residue_order(dattn), _to_residue_order(pos_col), rot)
    dq, dk, dv = (_from_residue_order(t) for t in (dq, dk, dv))
    dproj = jnp.concatenate([dq, dk, dv, du, dvs], axis=1)
    g_w_in, _ = weight_grad("grad_w_in", dproj, h1, after=[c_gate, c_up])
    s_in, s_out = from_sibling("w_in_out_to_sibling", [g_w_in, g_w_out])
    (grad_x, d_pre_mix), _ = in_bwd(dproj, w_in_t, x2d, pre_mix_norm, dx2, after=[g_w_in])
    p_in = sum_cores("sum_cores_in", g_w_in, s_in, core, after=[d_pre_mix])
    p_out = sum_cores("sum_cores_out", g_w_out, s_out, core, after=[d_pre_mix])

    d_b_sp = d_bfull.reshape(CHUNK, N_GROUPS, HEAD_DIM).sum(axis=-1).T
    small_g = _pack_small(dict(pre_mix_norm=d_pre_mix, sgu_ln_gain=d_ln_gain, sgu_ln_bias=d_ln_bias, sgu_w_spatial=d_w_sp,
                               sgu_b_spatial=d_b_sp, attn_out_norm=d_attn_out, sgu_out_norm=d_sgu_out,
                               post_mix_norm=d_post_mix, pre_ffn_norm=d_pre_ffn, post_ffn_norm=d_post_ffn,
                               loss_sum=sq_err))
    small_g = small_g.reshape(N_DEV, SMALL_ROWS // N_DEV, 128)
    (_, (c_in, c_out)), (_, (o_small,)) = by_sequencer(
        "last_sums_to_owners", [to_chips([p_in, p_out]), to_owners(small_g)], TO_ALL)
    all_small = sum_owned("sum_small", small_g, o_small, dev_ids)
    (all_small,) = gathered("gather_small_grads", [all_small])

    big, last = {}, p_in
    for name, w, p, c, m, vv, transposed in (
            ("w_down", w_down, p_down, c_down, m_w_down, v_w_down, False), ("w_gate", w_gate, p_gate, c_gate, m_w_gate, v_w_gate, True),
            ("w_up", w_up, p_up, c_up, m_w_up, v_w_up, True), ("w_out", w_out, p_out, c_out, m_w_out, v_w_out, False),
            ("w_in", w_in, p_in, c_in, m_w_in, v_w_in, True)):
        turn = (lambda t: t.T) if transposed else (lambda t: t)
        outs = adamw_of_sums("adamw_" + name, p, c, chip_ids, turn(w[0]), turn(m[0]), turn(vv[0]), last)
        big[name], last = tuple(turn(t)[None] for t in outs), outs[0]
    sd, snm, snv = adamw("adamw_small", _pack_small(small_w), all_small, _pack_small(small_m), _pack_small(small_v))
    sg, sd, snm, snv = (_unpack_small(t, small_w) for t in (all_small, sd, snm, snv))
    loss = sg["loss_sum"][0] * np.float32(0.5 / D_MODEL)

    names = ["pre_mix_norm", "w_in", "sgu_ln_gain", "sgu_ln_bias", "sgu_w_spatial", "sgu_b_spatial", "attn_out_norm",
             "sgu_out_norm", "w_out", "post_mix_norm", "pre_ffn_norm", "w_gate", "w_up", "w_down", "post_ffn_norm"]
    outs = [loss, grad_x[None]]
    for i, table in enumerate((sg, sd, snm, snv)):
        for name in names:
            outs.append(big[name][i] if name in big else table[name])
    return tuple(outs)
```

```python
import numpy as np
import jax
import jax.numpy as jnp
from jax import lax
from jax.experimental import pallas as pl
from jax.experimental.pallas import tpu as pltpu
from jax.experimental.pallas import tpu_sc as plsc

F32 = jnp.float32
BF16 = jnp.bfloat16

SEQ = 2048
D_MODEL = 1024
ATTN_W = 512
SGU_W = 512
HEAD_DIM = 64
N_GROUPS = 8
CHUNK = 128
D_FF = 2816
IN_W = 3 * ATTN_W + 2 * SGU_W
DILATIONS = (1, 4, 16)
ROPE_THETA = 500000.0
ROT_DIM = 16
ROT_HALF = 8
RMS_EPS = 1e-6
LN_EPS = 1e-5
Q_SCALE = 0.125
NEG = -1e30

N_DEV = 8
MESH_AXES = ("x", "y", "c")
MESH = pl.DeviceIdType.MESH

ADAM_LR = 0.001
ADAM_B1 = 0.9
ADAM_B2 = 0.999
ADAM_EPS = 1e-08
ADAM_WD = 0.01
ADAM_STEP = 10

VMEM_LIMIT = 60 * 1024 * 1024
ANY = pl.BlockSpec(memory_space=pl.ANY)

SMALL = (("pre_mix_norm", 1024), ("sgu_ln_gain", 512), ("sgu_ln_bias", 512), ("sgu_w_spatial", 8 * 128 * 128),
         ("sgu_b_spatial", 1024), ("attn_out_norm", 512), ("sgu_out_norm", 512), ("post_mix_norm", 1024),
         ("pre_ffn_norm", 1024), ("post_ffn_norm", 1024), ("loss_sum", 1))
SMALL_ROWS = 1152


def _params(sem=("arbitrary",)):
    return pltpu.CompilerParams(dimension_semantics=sem, vmem_limit_bytes=VMEM_LIMIT)


def _dot(a, b):
    return jnp.dot(a, b, preferred_element_type=F32)


def _dot_nt(a, b):
    return lax.dot_general(a, b, (((1,), (1,)), ((), ())), preferred_element_type=F32)


def _dot_tn(a, b):
    return lax.dot_general(a, b, (((0,), (0,)), ((), ())), preferred_element_type=F32)


def _rms(z):
    return lax.rsqrt(jnp.mean(z * z, axis=-1, keepdims=True) + RMS_EPS)


def _rms_bwd(z, gain, d):
    r = _rms(z)
    n = z * r
    dn = d * gain
    dz = r * (dn - n * jnp.mean(dn * n, axis=-1, keepdims=True))
    return dz, jnp.sum(d * n, axis=0, keepdims=True)


def _gelu(z):
    return 0.5 * z * (1.0 + lax.erf(z * np.float32(1.0 / np.sqrt(2.0))))


def _gelu_grad(z):
    cdf = 0.5 * (1.0 + lax.erf(z * np.float32(1.0 / np.sqrt(2.0))))
    return cdf + z * jnp.exp(-0.5 * z * z) * np.float32(1.0 / np.sqrt(2.0 * np.pi))


def _rot_tables(pos_col, invf, ma, mb):
    ang = pos_col.astype(F32) * invf
    s = jnp.sin(ang)
    return jnp.cos(ang), s * ma, s * mb


def _rot(t, c, sa, sb):
    return t * c + pltpu.roll(t, 120, 1) * sa + pltpu.roll(t, 8, 1) * sb


def _rot_t(d, c, sa, sb):
    return d * c + pltpu.roll(d * sa, 8, 1) + pltpu.roll(d * sb, 120, 1)


def _rot_consts():
    lane = np.arange(128) % HEAD_DIM
    inv_freq = (np.float32(ROPE_THETA) ** (-np.arange(0, ROT_DIM, 2, dtype=np.float32) / np.float32(ROT_DIM))).astype(np.float32)
    invf = np.where(lane < ROT_DIM, inv_freq[lane % ROT_HALF], 0.0).astype(np.float32)
    ma = np.where(lane < ROT_HALF, -1.0, 0.0).astype(np.float32)
    mb = np.where((lane >= ROT_HALF) & (lane < ROT_DIM), 1.0, 0.0).astype(np.float32)
    return jnp.asarray(invf[None]), jnp.asarray(ma[None]), jnp.asarray(mb[None])


def _row_spec(tm, w):
    return pl.BlockSpec((tm, w), lambda i: (i, 0))


def _full_spec(shape):
    return pl.BlockSpec(shape, lambda i: (0,) * len(shape))


def in_proj(x, pos_col, g1, w_in_t, rot):
    tm = 512

    def body(x_ref, pos_ref, g_ref, w_ref, invf_ref, ma_ref, mb_ref, h_ref, q_ref, k_ref, v_ref, u_ref, vs_ref):
        xf = x_ref[...]
        h = (xf * _rms(xf) * g_ref[...]).astype(BF16)
        h_ref[...] = h
        proj = _dot_nt(h, w_ref[...])
        c, sa, sb = _rot_tables(pos_ref[...], invf_ref[...], ma_ref[...], mb_ref[...])
        for j in range(ATTN_W // 128):
            q_ref[:, j * 128:(j + 1) * 128] = _rot(proj[:, j * 128:(j + 1) * 128], c, sa, sb) * Q_SCALE
            k_ref[:, j * 128:(j + 1) * 128] = _rot(proj[:, ATTN_W + j * 128:ATTN_W + (j + 1) * 128], c, sa, sb)
        v_ref[...] = proj[:, 2 * ATTN_W:3 * ATTN_W]
        u_ref[...] = proj[:, 3 * ATTN_W:3 * ATTN_W + SGU_W]
        vs_ref[...] = proj[:, 3 * ATTN_W + SGU_W:]

    act = jax.ShapeDtypeStruct((SEQ, 512), F32)
    return _call(
        "in_proj", body, SEQ // tm,
        [_row_spec(tm, D_MODEL), _row_spec(tm, 1), _full_spec((1, D_MODEL)), _full_spec((IN_W, D_MODEL)),
         _full_spec((1, 128)), _full_spec((1, 128)), _full_spec((1, 128))],
        [_row_spec(tm, D_MODEL)] + [_row_spec(tm, 512)] * 5,
        [jax.ShapeDtypeStruct((SEQ, D_MODEL), BF16)] + [act] * 5,
        (x, pos_col, g1, w_in_t, *rot))


RES = 16


def _to_residue_order(t):
    return t.reshape(SEQ // RES, RES, -1).transpose(1, 0, 2).reshape(t.shape)


def _from_residue_order(t):
    return t.reshape(RES, SEQ // RES, -1).transpose(1, 0, 2).reshape(t.shape)


def _block_rows(d, r, n):
    if d == 16:
        slices = [(128 * r, 128)]
    elif d == 4:
        slices = [(128 * (4 * b + r) + 32 * n, 32) for b in range(4)]
    else:
        slices = [(128 * b + 8 * n, 8) for b in range(RES)]
    return [(s if isinstance(s, int) else pl.multiple_of(s, z), z) for s, z in slices]


def _block_step(d, i):
    if d == 16:
        return i
    if d == 4:
        return 4 * (i & 31) + (i >> 5)
    return 16 * (i & 7) + (i >> 3)


def _attn_masks(d):
    row2 = _block_step(d, lax.broadcasted_iota(jnp.int32, (128, 256), 0))
    col2 = lax.broadcasted_iota(jnp.int32, (128, 256), 1)
    key2 = _block_step(d, col2 & 127)
    mask2 = jnp.logical_or(jnp.logical_and(col2 < 128, key2 >= row2), jnp.logical_and(col2 >= 128, key2 <= row2))
    row1 = _block_step(d, lax.broadcasted_iota(jnp.int32, (128, 128), 0))
    col1 = lax.broadcasted_iota(jnp.int32, (128, 128), 1)
    return col1 < HEAD_DIM, _block_step(d, col1) <= row1, mask2


def _load_rows(ref, slices):
    parts = [ref[pl.ds(s, z), :] for s, z in slices]
    return parts[0] if len(parts) == 1 else jnp.concatenate(parts, axis=0)


def _for_each_group(fn):
    for p, d in enumerate(DILATIONS):
        masks = _attn_masks(d)
        if d == 16:
            def group(i, carry, p=p, masks=masks):
                fn(p, masks, [(_block_rows(16, 4 * i + g, 0), None) for g in range(4)])
                return carry

            lax.fori_loop(0, 4, group, 0)
        elif d == 4:
            fn(p, masks, [(_block_rows(4, r, 0), None) for r in range(4)])

            def group(i, carry, p=p, masks=masks):
                fn(p, masks, [(_block_rows(4, r, i + 1), _block_rows(4, r, i)) for r in range(4)])
                return carry

            lax.fori_loop(0, 3, group, 0)
        else:
            fn(p, masks, [(_block_rows(1, 0, 0), None)])

            def group(i, carry, p=p, masks=masks):
                fn(p, masks, [(_block_rows(1, 0, 3 * i + g + 1), _block_rows(1, 0, 3 * i + g)) for g in range(3)])
                return carry

            lax.fori_loop(0, 5, group, 0)


def attn_fwd(q, k, v):
    def body(q_ref, k_ref, v_ref, o_ref, lse_ref, op_ref, lp_ref):
        def group(p, masks, blocks):
            head0, mask1, mask2 = masks
            heads = (head0, jnp.logical_not(head0))
            keys = [rows if prev is None else prev + rows for rows, prev in blocks]
            mask = [mask1 if prev is None else mask2 for _, prev in blocks]
            qb = [_load_rows(q_ref, rows) for rows, _ in blocks]
            kk = [_load_rows(k_ref, ks).astype(BF16) for ks in keys]
            vv = [_load_rows(v_ref, ks).astype(BF16) for ks in keys]
            chains = [(g, hm) for g in range(len(blocks)) for hm in heads]
            s = [jnp.where(mask[g], _dot_nt(jnp.where(hm, qb[g], 0.0).astype(BF16), kk[g]), NEG) for g, hm in chains]
            m = [jnp.max(t, axis=-1, keepdims=True) for t in s]
            e = [jnp.exp(t - mt) for t, mt in zip(s, m)]
            l = [jnp.sum(t, axis=-1, keepdims=True) for t in e]
            pv = [_dot(t.astype(BF16), vv[g]) for t, (g, _) in zip(e, chains)]
            for g, (rows, _) in enumerate(blocks):
                o_blk = jnp.where(head0, pv[2 * g] / l[2 * g], pv[2 * g + 1] / l[2 * g + 1])
                l_blk = jnp.where(head0, jnp.broadcast_to(m[2 * g] + jnp.log(l[2 * g]), (128, 128)),
                                  jnp.broadcast_to(m[2 * g + 1] + jnp.log(l[2 * g + 1]), (128, 128)))
                at = 0
                for start, size in rows:
                    op_ref[p, pl.ds(start, size), :] = o_blk[at:at + size]
                    lp_ref[p, pl.ds(start, size), :] = l_blk[at:at + size]
                    at += size

        _for_each_group(group)

        def combine(i, carry):
            rows = pl.ds(pl.multiple_of(i * 256, 256), 256)
            ls = [lp_ref[p, rows, :] for p in range(3)]
            m = jnp.maximum(jnp.maximum(ls[0], ls[1]), ls[2])
            lse = m + jnp.log(jnp.exp(ls[0] - m) + jnp.exp(ls[1] - m) + jnp.exp(ls[2] - m))
            o = jnp.zeros((256, 128), F32)
            for p in range(3):
                o = o + jnp.exp(ls[p] - lse) * op_ref[p, rows, :]
            o_ref[rows, :] = o
            lse_ref[rows, :] = lse
            return carry

        lax.fori_loop(0, SEQ // 256, combine, 0)

    slab = pl.BlockSpec((SEQ, 128), lambda i: (0, i))
    out = jax.ShapeDtypeStruct((SEQ, ATTN_W), F32)
    return _call(
        "attn_fwd", body, ATTN_W // 128, [slab] * 3, [slab] * 2, [out, out], (q, k, v),
        scratch_shapes=[pltpu.VMEM((3, SEQ, 128), F32), pltpu.VMEM((3, SEQ, 128), F32)])


def _causal_weights(w_ref):
    row = lax.broadcasted_iota(jnp.int32, (CHUNK, CHUNK), 0)
    col = lax.broadcasted_iota(jnp.int32, (CHUNK, CHUNK), 1)
    return [jnp.where(col <= row, w_ref[g], 0.0).astype(BF16) for g in range(N_GROUPS)], col <= row


def _sgu_chunk_fwd(u, vs, lg, lb, wc, bfull, head0):
    ug = _gelu(u)
    vg = _gelu(vs)
    xc = vg - jnp.mean(vg, axis=-1, keepdims=True)
    rstd = lax.rsqrt(jnp.mean(xc * xc, axis=-1, keepdims=True) + LN_EPS)
    xhat = xc * rstd
    vn = xhat * lg + lb
    mixed = []
    for gp in range(SGU_W // 128):
        vp = vn[:, gp * 128:(gp + 1) * 128].astype(BF16)
        mixed.append(jnp.where(head0, _dot(wc[2 * gp], vp), _dot(wc[2 * gp + 1], vp)))
    ms = jnp.concatenate(mixed, axis=1) + bfull
    return ug, xhat, rstd, vn, ms


def sgu_fwd(u, vs, lg, lb, w_sp, bfull):
    cpb = 4

    def body(u_ref, vs_ref, lg_ref, lb_ref, w_ref, b_ref, o_ref):
        wc, _ = _causal_weights(w_ref)
        head0 = lax.broadcasted_iota(jnp.int32, (CHUNK, 128), 1) < HEAD_DIM
        for ci in range(cpb):
            rows = pl.ds(ci * CHUNK, CHUNK)
            ug, _, _, _, ms = _sgu_chunk_fwd(u_ref[rows, :], vs_ref[rows, :], lg_ref[...], lb_ref[...], wc, b_ref[...], head0)
            o_ref[rows, :] = ug * ms

    tm = cpb * CHUNK
    return _call(
        "sgu_fwd", body, SEQ // tm,
        [_row_spec(tm, SGU_W), _row_spec(tm, SGU_W), _full_spec((1, SGU_W)), _full_spec((1, SGU_W)),
         _full_spec((N_GROUPS, CHUNK, CHUNK)), _full_spec((CHUNK, SGU_W))],
        [_row_spec(tm, SGU_W)], [jax.ShapeDtypeStruct((SEQ, SGU_W), F32)],
        (u, vs, lg, lb, w_sp, bfull))


def out_proj(attn, sgu, x, ga, gs, w_out, gpm, gpf):
    tm = 512

    def body(a_ref, s_ref, x_ref, ga_ref, gs_ref, w_ref, gpm_ref, gpf_ref, mix_ref, y_ref, x2_ref, h2_ref):
        a = a_ref[...]
        s = s_ref[...]
        an = (a * _rms(a) * ga_ref[...]).astype(BF16)
        sn = (s * _rms(s) * gs_ref[...]).astype(BF16)
        mix_ref[:, :ATTN_W] = an
        mix_ref[:, ATTN_W:] = sn
        y = _dot(an, w_ref[:ATTN_W, :]) + _dot(sn, w_ref[ATTN_W:, :])
        y_ref[...] = y
        x2 = x_ref[...] + y * _rms(y) * gpm_ref[...]
        x2_ref[...] = x2
        h2_ref[...] = (x2 * _rms(x2) * gpf_ref[...]).astype(BF16)

    wide = jax.ShapeDtypeStruct((SEQ, D_MODEL), F32)
    wide16 = jax.ShapeDtypeStruct((SEQ, D_MODEL), BF16)
    return _call(
        "out_proj", body, SEQ // tm,
        [_row_spec(tm, ATTN_W), _row_spec(tm, SGU_W), _row_spec(tm, D_MODEL), _full_spec((1, ATTN_W)),
         _full_spec((1, SGU_W)), _full_spec((D_MODEL, D_MODEL)), _full_spec((1, D_MODEL)), _full_spec((1, D_MODEL))],
        [_row_spec(tm, D_MODEL)] * 4, [wide16, wide, wide, wide16],
        (attn, sgu, x, ga, gs, w_out, gpm, gpf))


def ffn_up(h2, w_gate_t, w_up_t):
    tm = 256

    def body(h_ref, wg_ref, wu_ref, g_ref, u_ref, a_ref):
        h = h_ref[...]
        g = _dot_nt(h, wg_ref[...])
        u = _dot_nt(h, wu_ref[...])
        g_ref[...] = g.astype(BF16)
        u_ref[...] = u.astype(BF16)
        a_ref[...] = (g * jax.nn.sigmoid(g) * u).astype(BF16)

    ff = jax.ShapeDtypeStruct((SEQ, D_FF), BF16)
    return _call(
        "ffn_up", body, SEQ // tm,
        [_row_spec(tm, D_MODEL), _full_spec((D_FF, D_MODEL)), _full_spec((D_FF, D_MODEL))],
        [_row_spec(tm, D_FF)] * 3, [ff, ff, jax.ShapeDtypeStruct((SEQ, D_FF), BF16)],
        (h2, w_gate_t, w_up_t))


def ffn_down_loss(act, w_down, x2, gpo, target):
    tm = 512

    def body(a_ref, w_ref, x2_ref, g_ref, t_ref, df_ref, dx3_ref, dg_ref, loss_ref):
        f = _dot(a_ref[...], w_ref[...])
        gain = g_ref[...]
        err = x2_ref[...] + f * _rms(f) * gain - t_ref[...]
        dx3 = err * np.float32(1.0 / D_MODEL)
        dx3_ref[...] = dx3
        df, dg = _rms_bwd(f, gain, dx3)
        df_ref[...] = df.astype(BF16)

        @pl.when(pl.program_id(0) == 0)
        def _():
            dg_ref[...] = jnp.zeros_like(dg_ref)
            loss_ref[...] = jnp.zeros_like(loss_ref)

        dg_ref[...] += dg
        loss_ref[...] += jnp.sum(err * err, axis=(0, 1), keepdims=True)

    return pl.pallas_call(
        body, name="ffn_down_loss", grid=(SEQ // tm,),
        in_specs=[_row_spec(tm, D_FF), _full_spec((D_FF, D_MODEL)), _row_spec(tm, D_MODEL), _full_spec((1, D_MODEL)),
                  _row_spec(tm, D_MODEL)],
        out_specs=[_row_spec(tm, D_MODEL), _row_spec(tm, D_MODEL), _full_spec((1, D_MODEL)), _full_spec((1, 1))],
        out_shape=[jax.ShapeDtypeStruct((SEQ, D_MODEL), BF16), jax.ShapeDtypeStruct((SEQ, D_MODEL), F32),
                   jax.ShapeDtypeStruct((1, D_MODEL), F32), jax.ShapeDtypeStruct((1, 1), F32)],
        compiler_params=_params(),
    )(act, w_down, x2, gpo, target)


def ffn_act_bwd(df, w_down, gate, up, after=()):
    tm = 256

    def body(df_ref, w_ref, g_ref, u_ref, dg_ref, du_ref):
        dact = _dot_nt(df_ref[...], w_ref[...])
        g = g_ref[...].astype(F32)
        s = jax.nn.sigmoid(g)
        du_ref[...] = (dact * g * s).astype(BF16)
        dg_ref[...] = (dact * u_ref[...].astype(F32) * (s * (1.0 + g * (1.0 - s)))).astype(BF16)

    ff16 = jax.ShapeDtypeStruct((SEQ, D_FF), BF16)
    return _call(
        "ffn_act_bwd", body, SEQ // tm,
        [_row_spec(tm, D_MODEL), _full_spec((D_FF, D_MODEL)), _row_spec(tm, D_FF), _row_spec(tm, D_FF)],
        [_row_spec(tm, D_FF)] * 2, [ff16, ff16], (df, w_down, gate, up), after=after)


def ffn_in_bwd(dgate, dup, w_gate_t, w_up_t, x2, gpf, dx3, y, gpm, after=()):
    tm = 256

    def body(dg_ref, du_ref, wg_ref, wu_ref, x2_ref, gpf_ref, dx3_ref, y_ref, gpm_ref, dx2_ref, dy_ref, dgpf_ref, dgpm_ref):
        dh2 = _dot(dg_ref[...], wg_ref[...]) + _dot(du_ref[...], wu_ref[...])
        dz, dgpf = _rms_bwd(x2_ref[...], gpf_ref[...], dh2)
        dx2 = dx3_ref[...] + dz
        dx2_ref[...] = dx2
        dy, dgpm = _rms_bwd(y_ref[...], gpm_ref[...], dx2)
        dy_ref[...] = dy.astype(BF16)

        @pl.when(pl.program_id(0) == 0)
        def _():
            dgpf_ref[...] = jnp.zeros_like(dgpf_ref)
            dgpm_ref[...] = jnp.zeros_like(dgpm_ref)

        dgpf_ref[...] += dgpf
        dgpm_ref[...] += dgpm

    vec = jax.ShapeDtypeStruct((1, D_MODEL), F32)
    return _call(
        "ffn_in_bwd", body, SEQ // tm,
        [_row_spec(tm, D_FF), _row_spec(tm, D_FF), _full_spec((D_FF, D_MODEL)), _full_spec((D_FF, D_MODEL)),
         _row_spec(tm, D_MODEL), _full_spec((1, D_MODEL)), _row_spec(tm, D_MODEL), _row_spec(tm, D_MODEL),
         _full_spec((1, D_MODEL))],
        [_row_spec(tm, D_MODEL), _row_spec(tm, D_MODEL), _full_spec((1, D_MODEL)), _full_spec((1, D_MODEL))],
        [jax.ShapeDtypeStruct((SEQ, D_MODEL), F32), jax.ShapeDtypeStruct((SEQ, D_MODEL), BF16), vec, vec],
        (dgate, dup, w_gate_t, w_up_t, x2, gpf, dx3, y, gpm), after=after)


def weight_grad(name, a, b, after=()):
    m, n = a.shape[1], b.shape[1]
    tr = 256

    def body(a_ref, b_ref, o_ref):
        o_ref[...] = _dot_tn(a_ref[...], b_ref[...]).astype(BF16)

    (out,) = _call(
        name, body, m // tr, [pl.BlockSpec((SEQ, tr), lambda i: (0, i)), _full_spec((SEQ, n))],
        [_row_spec(tr, n)], [jax.ShapeDtypeStruct((m, n), BF16)], (a, b), after=after)
    return out.reshape(N_DEV, m // N_DEV, n)


def mix_bwd(dy, w_out, attn, sgu, ga, gs, after=()):
    tm = 512

    def body(dy_ref, w_ref, a_ref, s_ref, ga_ref, gs_ref, da_ref, ds_ref, dga_ref, dgs_ref):
        dy = dy_ref[...]
        da, dga = _rms_bwd(a_ref[...], ga_ref[...], _dot_nt(dy, w_ref[:ATTN_W, :]))
        ds, dgs = _rms_bwd(s_ref[...], gs_ref[...], _dot_nt(dy, w_ref[ATTN_W:, :]))
        da_ref[...] = da
        ds_ref[...] = ds

        @pl.when(pl.program_id(0) == 0)
        def _():
            dga_ref[...] = jnp.zeros_like(dga_ref)
            dgs_ref[...] = jnp.zeros_like(dgs_ref)

        dga_ref[...] += dga
        dgs_ref[...] += dgs

    half = jax.ShapeDtypeStruct((SEQ, 512), F32)
    vec = jax.ShapeDtypeStruct((1, 512), F32)
    return _call(
        "mix_bwd", body, SEQ // tm,
        [_row_spec(tm, D_MODEL), _full_spec((D_MODEL, D_MODEL)), _row_spec(tm, 512), _row_spec(tm, 512),
         _full_spec((1, 512)), _full_spec((1, 512))],
        [_row_spec(tm, 512), _row_spec(tm, 512), _full_spec((1, 512)), _full_spec((1, 512))],
        [half, half, vec, vec], (dy, w_out, attn, sgu, ga, gs), after=after)


def sgu_bwd(u, vs, dsgu, lg, lb, w_sp, bfull):
    cpb = 4

    def body(u_ref, vs_ref, d_ref, lg_ref, lb_ref, w_ref, b_ref, du_ref, dvs_ref, dlg_ref, dlb_ref, dw_ref, db_ref):
        wc, causal = _causal_weights(w_ref)
        head0 = lax.broadcasted_iota(jnp.int32, (CHUNK, 128), 1) < HEAD_DIM
        lg = lg_ref[...]

        @pl.when(pl.program_id(0) == 0)
        def _():
            dlg_ref[...] = jnp.zeros_like(dlg_ref)
            dlb_ref[...] = jnp.zeros_like(dlb_ref)
            dw_ref[...] = jnp.zeros_like(dw_ref)
            db_ref[...] = jnp.zeros_like(db_ref)

        for ci in range(cpb):
            rows = pl.ds(ci * CHUNK, CHUNK)
            u = u_ref[rows, :]
            vs = vs_ref[rows, :]
            d = d_ref[rows, :]
            ug, xhat, rstd, vn, ms = _sgu_chunk_fwd(u, vs, lg, lb_ref[...], wc, b_ref[...], head0)
            du_ref[rows, :] = (d * ms * _gelu_grad(u)).astype(BF16)
            dms = d * ug
            db_ref[...] += dms
            dvn = []
            for gp in range(SGU_W // 128):
                dmp = dms[:, gp * 128:(gp + 1) * 128]
                dm0 = jnp.where(head0, dmp, 0.0).astype(BF16)
                dm1 = jnp.where(head0, 0.0, dmp).astype(BF16)
                vp = vn[:, gp * 128:(gp + 1) * 128].astype(BF16)
                dw_ref[2 * gp] += _dot_nt(dm0, vp)
                dw_ref[2 * gp + 1] += _dot_nt(dm1, vp)
                dvn.append(_dot_tn(wc[2 * gp], dm0) + _dot_tn(wc[2 * gp + 1], dm1))
            dvn = jnp.concatenate(dvn, axis=1)
            dlg_ref[...] += jnp.sum(dvn * xhat, axis=0, keepdims=True)
            dlb_ref[...] += jnp.sum(dvn, axis=0, keepdims=True)
            dxh = dvn * lg
            dvg = rstd * (dxh - jnp.mean(dxh, axis=-1, keepdims=True) - xhat * jnp.mean(dxh * xhat, axis=-1, keepdims=True))
            dvs_ref[rows, :] = (dvg * _gelu_grad(vs)).astype(BF16)

        @pl.when(pl.program_id(0) == pl.num_programs(0) - 1)
        def _():
            for g in range(N_GROUPS):
                dw_ref[g] = jnp.where(causal, dw_ref[g], 0.0)

    tm = cpb * CHUNK
    half16 = jax.ShapeDtypeStruct((SEQ, SGU_W), BF16)
    vec = jax.ShapeDtypeStruct((1, SGU_W), F32)
    return _call(
        "sgu_bwd", body, SEQ // tm,
        [_row_spec(tm, SGU_W)] * 3 + [_full_spec((1, SGU_W)), _full_spec((1, SGU_W)),
                                      _full_spec((N_GROUPS, CHUNK, CHUNK)), _full_spec((CHUNK, SGU_W))],
        [_row_spec(tm, SGU_W), _row_spec(tm, SGU_W), _full_spec((1, SGU_W)), _full_spec((1, SGU_W)),
         _full_spec((N_GROUPS, CHUNK, CHUNK)), _full_spec((CHUNK, SGU_W))],
        [half16, half16, vec, vec, jax.ShapeDtypeStruct((N_GROUPS, CHUNK, CHUNK), F32),
         jax.ShapeDtypeStruct((CHUNK, SGU_W), F32)],
        (u, vs, dsgu, lg, lb, w_sp, bfull))


def attn_bwd(q, k, v, o, lse, do, pos_col, rot):
    def body(q_ref, k_ref, v_ref, o_ref, lse_ref, do_ref, pos_ref, invf_ref, ma_ref, mb_ref,
             dq_ref, dk_ref, dv_ref, dqa_ref, dka_ref, dva_ref, dlt_ref, rot_ref):
        dqa_ref[...] = jnp.zeros_like(dqa_ref)
        dka_ref[...] = jnp.zeros_like(dka_ref)
        dva_ref[...] = jnp.zeros_like(dva_ref)

        def delta(i, carry):
            rows = pl.ds(pl.multiple_of(i * 256, 256), 256)
            prod = do_ref[rows, :] * o_ref[rows, :]
            h0 = lax.broadcasted_iota(jnp.int32, (256, 128), 1) < HEAD_DIM
            d0 = jnp.sum(jnp.where(h0, prod, 0.0), axis=-1, keepdims=True)
            d1 = jnp.sum(jnp.where(h0, 0.0, prod), axis=-1, keepdims=True)
            dlt_ref[rows, :] = jnp.where(h0, d0, d1)
            return carry

        lax.fori_loop(0, SEQ // 256, delta, 0)

        def add_rows(ref, slices, val):
            at = 0
            for start, size in slices:
                ref[pl.ds(start, size), :] += val[at:at + size]
                at += size

        def group(p, masks, blocks):
            head0, mask1, mask2 = masks
            heads = (head0, jnp.logical_not(head0))
            keys = [rows if prev is None else prev + rows for rows, prev in blocks]
            mask = [mask1 if prev is None else mask2 for _, prev in blocks]
            kk = [_load_rows(k_ref, ks).astype(BF16) for ks in keys]
            vv = [_load_rows(v_ref, ks).astype(BF16) for ks in keys]
            qb = [_load_rows(q_ref, rows) for rows, _ in blocks]
            dob = [_load_rows(do_ref, rows) for rows, _ in blocks]
            lse_b = [_load_rows(lse_ref, rows) for rows, _ in blocks]
            dlt_b = [_load_rows(dlt_ref, rows) for rows, _ in blocks]
            chains = [(g, h) for g in range(len(blocks)) for h in range(2)]
            qm = [jnp.where(heads[h], qb[g], 0.0).astype(BF16) for g, h in chains]
            dom = [jnp.where(heads[h], dob[g], 0.0).astype(BF16) for g, h in chains]
            s = [_dot_nt(qm[c], kk[g]) for c, (g, h) in enumerate(chains)]
            dp = [_dot_nt(dom[c], vv[g]) for c, (g, h) in enumerate(chains)]
            pr = [jnp.where(mask[g], jnp.exp(s[c] - lse_b[g][:, h * HEAD_DIM:h * HEAD_DIM + 1]), 0.0)
                  for c, (g, h) in enumerate(chains)]
            ds = [(pr[c] * (dp[c] - dlt_b[g][:, h * HEAD_DIM:h * HEAD_DIM + 1])).astype(BF16)
                  for c, (g, h) in enumerate(chains)]
            dv = [_dot_tn(pr[c].astype(BF16), dom[c]) for c in range(len(chains))]
            dk = [_dot_tn(ds[c], qm[c]) for c in range(len(chains))]
            dq = [_dot(ds[c], kk[g]) for c, (g, h) in enumerate(chains)]
            for g, (rows, _) in enumerate(blocks):
                add_rows(dqa_ref, rows, jnp.where(head0, dq[2 * g], dq[2 * g + 1]))
                add_rows(dka_ref, keys[g], dk[2 * g] + dk[2 * g + 1])
                add_rows(dva_ref, keys[g], dv[2 * g] + dv[2 * g + 1])

        _for_each_group(group)

        @pl.when(pl.program_id(0) == 0)
        def _():
            def tables(i, carry):
                rows = pl.ds(pl.multiple_of(i * 256, 256), 256)
                c, sa, sb = _rot_tables(pos_ref[rows, :], invf_ref[...], ma_ref[...], mb_ref[...])
                rot_ref[0, rows, :] = c
                rot_ref[1, rows, :] = sa
                rot_ref[2, rows, :] = sb
                return carry

            lax.fori_loop(0, SEQ // 256, tables, 0)

        def finish(i, carry):
            rows = pl.ds(pl.multiple_of(i * 256, 256), 256)
            c, sa, sb = rot_ref[0, rows, :], rot_ref[1, rows, :], rot_ref[2, rows, :]
            dq_ref[rows, :] = _rot_t(dqa_ref[rows, :] * Q_SCALE, c, sa, sb).astype(BF16)
            dk_ref[rows, :] = _rot_t(dka_ref[rows, :], c, sa, sb).astype(BF16)
            dv_ref[rows, :] = dva_ref[rows, :].astype(BF16)
            return carry

        lax.fori_loop(0, SEQ // 256, finish, 0)

    slab = pl.BlockSpec((SEQ, 128), lambda i: (0, i))
    out = jax.ShapeDtypeStruct((SEQ, ATTN_W), BF16)
    acc = pltpu.VMEM((SEQ, 128), F32)
    return _call(
        "attn_bwd", body, ATTN_W // 128,
        [slab] * 6 + [_full_spec((SEQ, 1)), _full_spec((1, 128)), _full_spec((1, 128)), _full_spec((1, 128))],
        [slab] * 3, [out, out, out], (q, k, v, o, lse, do, pos_col, *rot),
        scratch_shapes=[acc, acc, acc, acc, pltpu.VMEM((3, SEQ, 128), F32)])


def in_bwd(dproj, w_in_t, x, g1, dx2, after=()):
    tm = 512

    def body(dp_ref, w_ref, x_ref, g_ref, dx2_ref, dx_ref, dg_ref):
        dh1 = _dot(dp_ref[...], w_ref[...])
        dz, dg = _rms_bwd(x_ref[...], g_ref[...], dh1)
        dx_ref[...] = dx2_ref[...] + dz

        @pl.when(pl.program_id(0) == 0)
        def _():
            dg_ref[...] = jnp.zeros_like(dg_ref)

        dg_ref[...] += dg

    return _call(
        "in_bwd", body, SEQ // tm,
        [_row_spec(tm, IN_W), _full_spec((IN_W, D_MODEL)), _row_spec(tm, D_MODEL), _full_spec((1, D_MODEL)),
         _row_spec(tm, D_MODEL)],
        [_row_spec(tm, D_MODEL), _full_spec((1, D_MODEL))],
        [jax.ShapeDtypeStruct((SEQ, D_MODEL), F32), jax.ShapeDtypeStruct((1, D_MODEL), F32)],
        (dproj, w_in_t, x, g1, dx2), after=after)


def _coords():
    return lax.axis_index("x"), lax.axis_index("y"), lax.axis_index("c")


class Exchange:
    def __init__(self, srcs, bufs, new_shapes, n_sems, make, has_middle=False):
        self.srcs, self.bufs, self.new_shapes, self.n_sems, self.make = list(srcs), list(bufs), list(new_shapes), n_sems, make
        self.has_middle = has_middle


def _call(name, body, n_steps, in_specs, out_specs, out_shape, args, scratch_shapes=(), after=()):
    n_in = len(args)

    def wrapped(*refs):
        body(*refs[:n_in], *refs[n_in + len(after):])

    return list(pl.pallas_call(
        wrapped, name=name, grid=(n_steps,), in_specs=list(in_specs) + [ANY] * len(after), out_specs=list(out_specs),
        out_shape=list(out_shape), scratch_shapes=list(scratch_shapes), compiler_params=_params(),
    )(*args, *after))


def _gather_copies(kinds, bufs, ranges, send_sems, recv_sems):
    x, y, c = _coords()
    me, sibling = (x, y, c), (x, y, 1 - c)
    chips = [(1 - x, y), (x, 1 - y), (1 - x, 1 - y)]

    def copy(a, k, block, to):
        lo, hi = ranges[a]
        r = bufs[a].shape[0] // N_DEV
        rows = bufs[a].at[pl.ds((4 * block[0] + 2 * block[1] + block[2]) * r + lo, hi - lo), :]
        return pltpu.make_async_remote_copy(src_ref=rows, dst_ref=rows, send_sem=send_sems.at[7 * a + k],
                                            recv_sem=recv_sems.at[7 * a + k], device_id=to, device_id_type=MESH)

    every = range(len(bufs))
    make = {
        "out": lambda: [copy(a, 0, me, sibling) for a in every]
        + [copy(a, 1 + j, me, (*chip, c)) for a in every for j, chip in enumerate(chips)],
        "from_core": lambda: [copy(a, 0, sibling, me) for a in every],
        "from_chips": lambda: [copy(a, 1 + j, (*chip, c), me) for a in every for j, chip in enumerate(chips)],
        "on": lambda: [copy(a, 4 + j, (*chip, c), sibling) for a in every for j, chip in enumerate(chips)],
        "on_in": lambda: [copy(a, 4 + j, (*chip, 1 - c), me) for a in every for j, chip in enumerate(chips)],
    }
    return [make[kind]() for kind in kinds]


def gather(bufs):
    ranges = [(0, b.shape[0] // N_DEV) for b in bufs]

    def make(phase, src_refs, buf_refs, new_refs, send_sems, recv_sems):
        kinds = {"start": ["out"], "middle": ["from_chips", "on"], "end": ["out", "on", "from_core", "on_in"]}[phase]
        made = _gather_copies(kinds, buf_refs, ranges, send_sems, recv_sems)
        if phase == "start":
            return made[0]
        if phase == "middle":
            return made[0], made[1]
        return made[0] + made[1], made[2] + made[3]

    return Exchange([], bufs, [], 7 * len(bufs), make, has_middle=True)


TO_GATHER = (1, lambda x, y, c: [(x, y, 1 - c), (1 - x, y, c), (x, 1 - y, c), (1 - x, 1 - y, c)])
TO_SIBLING = (2, lambda x, y, c: [(x, y, 1 - c)])
TO_CHIPS = (3, lambda x, y, c: [(1 - x, y, c), (x, 1 - y, c), (1 - x, 1 - y, c)])
TO_ALL = (4, lambda x, y, c: [(x ^ (m >> 2), y ^ ((m >> 1) & 1), c ^ (m & 1)) for m in range(1, N_DEV)])


def by_sequencer(name, exchanges, who):
    collective_id, peers_of = who
    hbm = pltpu.MemorySpace.HBM
    refs = [([jax.new_ref(a, memory_space=hbm) for a in ex.srcs], [jax.new_ref(a, memory_space=hbm) for a in ex.bufs],
             [jax.empty_ref(s, memory_space=hbm) for s in ex.new_shapes]) for ex in exchanges]
    sems = []
    for ex in exchanges:
        sems += [pltpu.SemaphoreType.DMA((ex.n_sems,)), pltpu.SemaphoreType.DMA((ex.n_sems,))]

    @pl.kernel(mesh=plsc.ScalarSubcoreMesh(axis_name="sequencer", num_cores=1), name=name, scratch_types=tuple(sems),
               compiler_params=pltpu.CompilerParams(collective_id=collective_id))
    def launch(*sem_refs):
        peers = peers_of(*_coords())
        barrier = pltpu.get_barrier_semaphore()
        for peer in peers:
            pl.semaphore_signal(barrier, inc=1, device_id=peer, device_id_type=MESH)
        pl.semaphore_wait(barrier, len(peers))

        def make(phase, k):
            return exchanges[k].make(phase, *refs[k], sem_refs[2 * k], sem_refs[2 * k + 1])

        for k in range(len(exchanges)):
            for cp in make("start", k):
                cp.start()
        for k, ex in enumerate(exchanges):
            if ex.has_middle:
                arrivals, starts = make("middle", k)
                for cp in arrivals:
                    cp.wait_recv()
                for cp in starts:
                    cp.start()
        for k in range(len(exchanges)):
            sends, arrivals = make("end", k)
            for cp in arrivals:
                cp.wait_recv()
            for cp in sends:
                cp.wait_send()

    launch()
    return [([ref[...] for ref in bufs], [ref[...] for ref in news]) for _, bufs, news in refs]


def place_shards(name, shards, dev):
    n = len(shards)

    def body(dev_ref, *refs):
        for a in range(n):
            refs[n + a][...] = refs[a][...].astype(BF16)

    spec = pltpu.PrefetchScalarGridSpec(
        num_scalar_prefetch=1, grid=(1,),
        in_specs=[pl.BlockSpec(s.shape, lambda i, dev_ref: (0, 0)) for s in shards],
        out_specs=[pl.BlockSpec(s.shape, lambda i, dev_ref: (dev_ref[0], 0)) for s in shards])
    return pl.pallas_call(
        body, name=name, grid_spec=spec,
        out_shape=[jax.ShapeDtypeStruct((N_DEV * s.shape[0], s.shape[1]), BF16) for s in shards],
        compiler_params=_params(),
    )(dev, *shards)


def _swap(copies_of):
    def make(phase, src_refs, buf_refs, new_refs, send_sems, recv_sems):
        copies = copies_of(src_refs, new_refs, send_sems, recv_sems)
        return copies if phase == "start" else (copies, copies)

    return make


def to_sibling(grads):
    def copies_of(src_refs, new_refs, send_sems, recv_sems):
        x, y, c = _coords()
        return [pltpu.make_async_remote_copy(
            src_ref=src_refs[a].at[2 * xy + 1 - c], dst_ref=new_refs[a].at[xy], send_sem=send_sems.at[4 * a + xy],
            recv_sem=recv_sems.at[4 * a + xy], device_id=(x, y, 1 - c), device_id_type=MESH)
            for a in range(len(src_refs)) for xy in range(4)]

    return Exchange(grads, [], [jax.ShapeDtypeStruct((4,) + g.shape[1:], g.dtype) for g in grads], 4 * len(grads),
                    _swap(copies_of))


def to_chips(parts):
    def copies_of(src_refs, new_refs, send_sems, recv_sems):
        x, y, c = _coords()
        chips = [(1 - x, y), (x, 1 - y), (1 - x, 1 - y)]
        return [pltpu.make_async_remote_copy(
            src_ref=src_refs[a].at[2 * px + py], dst_ref=new_refs[a].at[2 * x + y], send_sem=send_sems.at[3 * a + j],
            recv_sem=recv_sems.at[3 * a + j], device_id=(px, py, c), device_id_type=MESH)
            for a in range(len(src_refs)) for j, (px, py) in enumerate(chips)]

    return Exchange(parts, [], [jax.ShapeDtypeStruct(p.shape, p.dtype) for p in parts], 3 * len(parts), _swap(copies_of))


def to_owners(grad):
    def copies_of(src_refs, new_refs, send_sems, recv_sems):
        x, y, c = _coords()
        copies = []
        for m in range(1, N_DEV):
            px, py, pc = x ^ (m >> 2), y ^ ((m >> 1) & 1), c ^ (m & 1)
            copies.append(pltpu.make_async_remote_copy(
                src_ref=src_refs[0].at[4 * px + 2 * py + pc], dst_ref=new_refs[0].at[4 * x + 2 * y + c],
                send_sem=send_sems.at[m - 1], recv_sem=recv_sems.at[m - 1], device_id=(px, py, pc), device_id_type=MESH))
        return copies

    return Exchange([grad], [], [jax.ShapeDtypeStruct(grad.shape, grad.dtype)], N_DEV - 1, _swap(copies_of))


def sum_cores(name, grad, other, core, after=()):
    _, r, w = other.shape

    def body(core_ref, g_ref, o_ref, *rest):
        rest[-1][...] = (g_ref[...].astype(F32) + o_ref[...].astype(F32)).astype(rest[-1].dtype)

    return pl.pallas_call(
        body, name=name,
        grid_spec=pltpu.PrefetchScalarGridSpec(
            num_scalar_prefetch=1, grid=(4,),
            in_specs=[pl.BlockSpec((1, r, w), lambda i, core_ref: (2 * i + core_ref[0], 0, 0)),
                      pl.BlockSpec((1, r, w), lambda i, core_ref: (i, 0, 0))] + [ANY] * len(after),
            out_specs=pl.BlockSpec((1, r, w), lambda i, core_ref: (i, 0, 0))),
        out_shape=jax.ShapeDtypeStruct(other.shape, other.dtype),
        compiler_params=_params(),
    )(core, grad, other, *after)


def sum_owned(name, grad, others, dev_ids, after=()):
    _, r, w = grad.shape

    def body(ids_ref, *refs):
        acc = refs[0][0]
        for k in range(1, N_DEV):
            acc = acc + refs[k][0]
        refs[-1][...] = acc

    def pick(k):
        return pl.BlockSpec((1, r, w), lambda i, ids_ref: (ids_ref[k], 0, 0))

    return pl.pallas_call(
        body, name=name,
        grid_spec=pltpu.PrefetchScalarGridSpec(
            num_scalar_prefetch=1, grid=(1,), in_specs=[pick(k) for k in range(N_DEV)] + [ANY] * len(after),
            out_specs=pl.BlockSpec((r, w), lambda i, ids_ref: (ids_ref[0], 0))),
        out_shape=jax.ShapeDtypeStruct((N_DEV * r, w), F32),
        compiler_params=_params(),
    )(dev_ids, grad, *([others] * (N_DEV - 1)), *after)


def _adamw_update(w, g, m, v):
    nm = ADAM_B1 * m + np.float32(1.0 - ADAM_B1) * g
    nv = ADAM_B2 * v + np.float32(1.0 - ADAM_B2) * (g * g)
    m_hat = nm / np.float32(1.0 - ADAM_B1 ** ADAM_STEP)
    v_hat = nv / np.float32(1.0 - ADAM_B2 ** ADAM_STEP)
    return -ADAM_LR * (m_hat / (jnp.sqrt(v_hat) + ADAM_EPS) + ADAM_WD * w), nm, nv


def adamw_of_sums(name, part, others, chip_ids, w, m, v, after):
    _, r, wd = part.shape

    def body(ids_ref, p_ref, a_ref, b_ref, c_ref, w_ref, m_ref, v_ref, after_ref, g_ref, d_ref, nm_ref, nv_ref):
        g = ((p_ref[0].astype(F32) + a_ref[0].astype(F32)) + b_ref[0].astype(F32)) + c_ref[0].astype(F32)
        g_ref[...] = g
        d_ref[...], nm_ref[...], nv_ref[...] = _adamw_update(w_ref[...], g, m_ref[...], v_ref[...])

    def pick(k):
        return pl.BlockSpec((1, r, wd), lambda i, ids_ref: (ids_ref[k], 0, 0))

    whole = pl.BlockSpec((r, wd), lambda i, ids_ref: (0, 0))
    shape = jax.ShapeDtypeStruct((r, wd), F32)
    return pl.pallas_call(
        body, name=name,
        grid_spec=pltpu.PrefetchScalarGridSpec(
            num_scalar_prefetch=1, grid=(1,), in_specs=[pick(0), pick(1), pick(2), pick(3), whole, whole, whole, ANY],
            out_specs=[whole] * 4),
        out_shape=[shape] * 4,
        compiler_params=_params(),
    )(chip_ids, part, others, others, others, w, m, v, after)


def adamw(name, w, g, m, v):
    def body(w_ref, g_ref, m_ref, v_ref, d_ref, nm_ref, nv_ref):
        d_ref[...], nm_ref[...], nv_ref[...] = _adamw_update(w_ref[...], g_ref[...], m_ref[...], v_ref[...])

    shape = jax.ShapeDtypeStruct(w.shape, F32)
    spec = _full_spec(w.shape)
    return pl.pallas_call(
        body, name=name, grid=(1,), in_specs=[spec] * 4, out_specs=[spec] * 3, out_shape=[shape] * 3,
        compiler_params=_params(),
    )(w, g, m, v)


def _pack_small(parts):
    flat = jnp.concatenate([parts[name].reshape(-1) for name, _ in SMALL])
    return jnp.pad(flat, (0, SMALL_ROWS * 128 - flat.shape[0])).reshape(SMALL_ROWS, 128)


def _unpack_small(packed, like):
    flat = packed.reshape(-1)
    out, at = {}, 0
    for name, size in SMALL:
        out[name] = flat[at:at + size].reshape(like[name].shape)
        at += size
    return out


def kernel(x, positions, pre_mix_norm, w_in, sgu_ln_gain, sgu_ln_bias, sgu_w_spatial, sgu_b_spatial, attn_out_norm, sgu_out_norm, w_out, post_mix_norm, pre_ffn_norm, w_gate, w_up, w_down, post_ffn_norm, loss_target, m_pre_mix_norm, m_w_in, m_sgu_ln_gain, m_sgu_ln_bias, m_sgu_w_spatial, m_sgu_b_spatial, m_attn_out_norm, m_sgu_out_norm, m_w_out, m_post_mix_norm, m_pre_ffn_norm, m_w_gate, m_w_up, m_w_down, m_post_ffn_norm, v_pre_mix_norm, v_w_in, v_sgu_ln_gain, v_sgu_ln_bias, v_sgu_w_spatial, v_sgu_b_spatial, v_attn_out_norm, v_sgu_out_norm, v_w_out, v_post_mix_norm, v_pre_ffn_norm, v_w_gate, v_w_up, v_w_down, v_post_ffn_norm):
    small_w = dict(pre_mix_norm=pre_mix_norm, sgu_ln_gain=sgu_ln_gain, sgu_ln_bias=sgu_ln_bias, sgu_w_spatial=sgu_w_spatial,
                   sgu_b_spatial=sgu_b_spatial, attn_out_norm=attn_out_norm, sgu_out_norm=sgu_out_norm,
                   post_mix_norm=post_mix_norm, pre_ffn_norm=pre_ffn_norm, post_ffn_norm=post_ffn_norm)
    small_m = dict(pre_mix_norm=m_pre_mix_norm, sgu_ln_gain=m_sgu_ln_gain, sgu_ln_bias=m_sgu_ln_bias, sgu_w_spatial=m_sgu_w_spatial,
                   sgu_b_spatial=m_sgu_b_spatial, attn_out_norm=m_attn_out_norm, sgu_out_norm=m_sgu_out_norm,
                   post_mix_norm=m_post_mix_norm, pre_ffn_norm=m_pre_ffn_norm, post_ffn_norm=m_post_ffn_norm)
    small_v = dict(pre_mix_norm=v_pre_mix_norm, sgu_ln_gain=v_sgu_ln_gain, sgu_ln_bias=v_sgu_ln_bias, sgu_w_spatial=v_sgu_w_spatial,
                   sgu_b_spatial=v_sgu_b_spatial, attn_out_norm=v_attn_out_norm, sgu_out_norm=v_sgu_out_norm,
                   post_mix_norm=v_post_mix_norm, pre_ffn_norm=v_pre_ffn_norm, post_ffn_norm=v_post_ffn_norm)
    for table in (small_w, small_m, small_v):
        table["loss_sum"] = jnp.zeros((1,), F32)

    x2d = x[0]
    target = loss_target[0]
    pos_col = positions.reshape(SEQ, 1)
    rot = _rot_consts()
    w_sp = sgu_w_spatial[0]
    bfull = jnp.repeat(sgu_b_spatial[0].T, HEAD_DIM, axis=1)

    x_i, y_i, c_i = (lax.axis_index(a).astype(jnp.int32) for a in MESH_AXES)
    dev = 4 * x_i + 2 * y_i + c_i
    core = c_i.reshape(1)
    chip = 2 * x_i + y_i
    chip_ids = jnp.stack([chip, chip ^ 1, chip ^ 2, chip ^ 3])
    dev_ids = jnp.stack([dev ^ m for m in range(N_DEV)])

    def gathered(name, bufs):
        return by_sequencer(name, [gather(bufs)], TO_GATHER)[0][0]

    def from_sibling(name, grads):
        return by_sequencer(name, [to_sibling(grads)], TO_SIBLING)[0][1]

    def from_chips(name, parts):
        return by_sequencer(name, [to_chips(parts)], TO_CHIPS)[0][1]

    (w_in_t,) = place_shards("place_w_in", [w_in[0].T], dev.reshape(1))
    (w_in_t,) = gathered("gather_w_in", [w_in_t])
    w_gate_t, w_up_t, w_out_f, w_down_f = place_shards(
        "place_weights", [w_gate[0].T, w_up[0].T, w_out[0], w_down[0]], dev.reshape(1))
    (w_out_f,) = gathered("gather_w_out", [w_out_f])
    w_gate_t, w_up_t = gathered("gather_w_gate_up", [w_gate_t, w_up_t])
    (w_down_f,) = gathered("gather_w_down", [w_down_f])

    h1, q, k, v, u, vs = in_proj(x2d, pos_col, pre_mix_norm, w_in_t, rot)
    q, k, v = (_to_residue_order(t) for t in (q, k, v))
    attn_r, lse = attn_fwd(q, k, v)
    attn = _from_residue_order(attn_r)
    (sgu,) = sgu_fwd(u, vs, sgu_ln_gain, sgu_ln_bias, w_sp, bfull)
    mix, y, x2, h2 = out_proj(attn, sgu, x2d, attn_out_norm, sgu_out_norm, w_out_f, post_mix_norm, pre_ffn_norm)
    gate, up, act = ffn_up(h2, w_gate_t, w_up_t)
    df, dx3, d_post_ffn, sq_err = ffn_down_loss(act, w_down_f, x2, post_ffn_norm, target)

    g_w_down = weight_grad("grad_w_down", act, df)
    (s_down,) = from_sibling("w_down_to_sibling", [g_w_down])
    dgate, dup = ffn_act_bwd(df, w_down_f, gate, up, after=[g_w_down])
    p_down = sum_cores("sum_cores_down", g_w_down, s_down, core, after=[dgate])
    (c_down,) = from_chips("w_down_to_chips", [p_down])
    g_w_gate = weight_grad("grad_w_gate", dgate, h2, after=[p_down])
    g_w_up = weight_grad("grad_w_up", dup, h2, after=[g_w_gate])
    s_gate, s_up = from_sibling("w_gate_up_to_sibling", [g_w_gate, g_w_up])
    dx2, dy, d_pre_ffn, d_post_mix = ffn_in_bwd(
        dgate, dup, w_gate_t, w_up_t, x2, pre_ffn_norm, dx3, y, post_mix_norm, after=[g_w_gate, g_w_up, c_down])
    p_gate = sum_cores("sum_cores_gate", g_w_gate, s_gate, core, after=[dy])
    p_up = sum_cores("sum_cores_up", g_w_up, s_up, core, after=[dy])
    c_gate, c_up = from_chips("w_gate_up_to_chips", [p_gate, p_up])
    g_w_out = weight_grad("grad_w_out", mix, dy, after=[p_gate, p_up])
    (s_out,) = from_sibling("w_out_to_sibling", [g_w_out])
    dattn, dsgu, d_attn_out, d_sgu_out = mix_bwd(dy, w_out_f, attn, sgu, attn_out_norm, sgu_out_norm, after=[g_w_out])
    du, dvs, d_ln_gain, d_ln_bias, d_w_sp, d_bfull = sgu_bwd(u, vs, dsgu, sgu_ln_gain, sgu_ln_bias, w_sp, bfull)
    dq, dk, dv = attn_bwd(q, k, v, attn_r, lse, _to_residue_order(dattn), _to_residue_order(pos_col), rot)
    p_out = sum_cores("sum_cores_out", g_w_out, s_out, core, after=[dq])
    (c_out,) = from_chips("w_out_to_chips", [p_out])
    dq, dk, dv = (_from_residue_order(t) for t in (dq, dk, dv))
    dproj = jnp.concatenate([dq, dk, dv, du, dvs], axis=1)
    g_w_in = weight_grad("grad_w_in", dproj, h1, after=[c_gate, c_up])
    (s_in,) = from_sibling("w_in_to_sibling", [g_w_in])
    grad_x, d_pre_mix = in_bwd(dproj, w_in_t, x2d, pre_mix_norm, dx2, after=[g_w_in, c_out])
    p_in = sum_cores("sum_cores_in", g_w_in, s_in, core, after=[d_pre_mix])

    d_b_sp = d_bfull.reshape(CHUNK, N_GROUPS, HEAD_DIM).sum(axis=-1).T
    small_g = _pack_small(dict(pre_mix_norm=d_pre_mix, sgu_ln_gain=d_ln_gain, sgu_ln_bias=d_ln_bias, sgu_w_spatial=d_w_sp,
                               sgu_b_spatial=d_b_sp, attn_out_norm=d_attn_out, sgu_out_norm=d_sgu_out,
                               post_mix_norm=d_post_mix, pre_ffn_norm=d_pre_ffn, post_ffn_norm=d_post_ffn,
                               loss_sum=sq_err))
    small_g = small_g.reshape(N_DEV, SMALL_ROWS // N_DEV, 128)
    (_, (c_in,)), (_, (o_small,)) = by_sequencer(
        "last_sums_to_owners", [to_chips([p_in]), to_owners(small_g)], TO_ALL)

    big, last = {}, p_in
    for name, w, p, c, m, vv, transposed in (
            ("w_down", w_down, p_down, c_down, m_w_down, v_w_down, False), ("w_gate", w_gate, p_gate, c_gate, m_w_gate, v_w_gate, True),
            ("w_up", w_up, p_up, c_up, m_w_up, v_w_up, True), ("w_out", w_out, p_out, c_out, m_w_out, v_w_out, False),
            ("w_in", w_in, p_in, c_in, m_w_in, v_w_in, True)):
        if name == "w_in":
            last = sum_owned("sum_small", small_g, o_small, dev_ids, after=[last])
            (all_small,) = gathered("gather_small_grads", [last])
        turn = (lambda t: t.T) if transposed else (lambda t: t)
        outs = adamw_of_sums("adamw_" + name, p, c, chip_ids, turn(w[0]), turn(m[0]), turn(vv[0]), last)
        big[name], last = tuple(turn(t)[None] for t in outs), outs[0]
    sd, snm, snv = adamw("adamw_small", _pack_small(small_w), all_small, _pack_small(small_m), _pack_small(small_v))
    sg, sd, snm, snv = (_unpack_small(t, small_w) for t in (all_small, sd, snm, snv))
    loss = sg["loss_sum"][0] * np.float32(0.5 / D_MODEL)

    names = ["pre_mix_norm", "w_in", "sgu_ln_gain", "sgu_ln_bias", "sgu_w_spatial", "sgu_b_spatial", "attn_out_norm",
             "sgu_out_norm", "w_out", "post_mix_norm", "pre_ffn_norm", "w_gate", "w_up", "w_down", "post_ffn_norm"]
    outs = [loss, grad_x[None]]
    for i, table in enumerate((sg, sd, snm, snv)):
        for name in names:
            outs.append(big[name][i] if name in big else table[name])
    return tuple(outs)
```

```python
import numpy as np
import jax
import jax.numpy as jnp
from jax import lax
from jax.experimental import pallas as pl
from jax.experimental.pallas import tpu as pltpu
from jax.experimental.pallas import tpu_sc as plsc

F32 = jnp.float32
BF16 = jnp.bfloat16

SEQ = 2048
D_MODEL = 1024
ATTN_W = 512
SGU_W = 512
HEAD_DIM = 64
N_GROUPS = 8
CHUNK = 128
D_FF = 2816
IN_W = 3 * ATTN_W + 2 * SGU_W
DILATIONS = (1, 4, 16)
ROPE_THETA = 500000.0
ROT_DIM = 16
ROT_HALF = 8
RMS_EPS = 1e-6
LN_EPS = 1e-5
Q_SCALE = 0.125
NEG = -1e30

N_DEV = 8
MESH_AXES = ("x", "y", "c")
MESH = pl.DeviceIdType.MESH

ADAM_LR = 0.001
ADAM_B1 = 0.9
ADAM_B2 = 0.999
ADAM_EPS = 1e-08
ADAM_WD = 0.01
ADAM_STEP = 10

VMEM_LIMIT = 60 * 1024 * 1024
ANY = pl.BlockSpec(memory_space=pl.ANY)

SMALL = (("pre_mix_norm", 1024), ("sgu_ln_gain", 512), ("sgu_ln_bias", 512), ("sgu_w_spatial", 8 * 128 * 128),
         ("sgu_b_spatial", 1024), ("attn_out_norm", 512), ("sgu_out_norm", 512), ("post_mix_norm", 1024),
         ("pre_ffn_norm", 1024), ("post_ffn_norm", 1024), ("loss_sum", 1))
SMALL_ROWS = 1152


def _params(sem=("arbitrary",)):
    return pltpu.CompilerParams(dimension_semantics=sem, vmem_limit_bytes=VMEM_LIMIT)


def _dot(a, b):
    return jnp.dot(a, b, preferred_element_type=F32)


def _dot_nt(a, b):
    return lax.dot_general(a, b, (((1,), (1,)), ((), ())), preferred_element_type=F32)


def _dot_tn(a, b):
    return lax.dot_general(a, b, (((0,), (0,)), ((), ())), preferred_element_type=F32)


def _rms(z):
    return lax.rsqrt(jnp.mean(z * z, axis=-1, keepdims=True) + RMS_EPS)


def _rms_bwd(z, gain, d):
    r = _rms(z)
    n = z * r
    dn = d * gain
    dz = r * (dn - n * jnp.mean(dn * n, axis=-1, keepdims=True))
    return dz, jnp.sum(d * n, axis=0, keepdims=True)


def _gelu(z):
    return 0.5 * z * (1.0 + lax.erf(z * np.float32(1.0 / np.sqrt(2.0))))


def _gelu_grad(z):
    cdf = 0.5 * (1.0 + lax.erf(z * np.float32(1.0 / np.sqrt(2.0))))
    return cdf + z * jnp.exp(-0.5 * z * z) * np.float32(1.0 / np.sqrt(2.0 * np.pi))


def _rot_tables(pos_col, invf, ma, mb):
    ang = pos_col.astype(F32) * invf
    s = jnp.sin(ang)
    return jnp.cos(ang), s * ma, s * mb


def _rot(t, c, sa, sb):
    return t * c + pltpu.roll(t, 120, 1) * sa + pltpu.roll(t, 8, 1) * sb


def _rot_t(d, c, sa, sb):
    return d * c + pltpu.roll(d * sa, 8, 1) + pltpu.roll(d * sb, 120, 1)


def _rot_consts():
    lane = np.arange(128) % HEAD_DIM
    inv_freq = (np.float32(ROPE_THETA) ** (-np.arange(0, ROT_DIM, 2, dtype=np.float32) / np.float32(ROT_DIM))).astype(np.float32)
    invf = np.where(lane < ROT_DIM, inv_freq[lane % ROT_HALF], 0.0).astype(np.float32)
    ma = np.where(lane < ROT_HALF, -1.0, 0.0).astype(np.float32)
    mb = np.where((lane >= ROT_HALF) & (lane < ROT_DIM), 1.0, 0.0).astype(np.float32)
    return jnp.asarray(invf[None]), jnp.asarray(ma[None]), jnp.asarray(mb[None])


def _row_spec(tm, w):
    return pl.BlockSpec((tm, w), lambda i: (i, 0))


def _full_spec(shape):
    return pl.BlockSpec(shape, lambda i: (0,) * len(shape))


RES = 16


def _residue_scratch(n_arrays, tm, width):
    return [pltpu.VMEM((2, n_arrays, tm // RES, RES, width), F32), pltpu.SemaphoreType.DMA((2, n_arrays, RES))]


def _to_residue_rows(tiles, outs, scratch, sems, tm, n_steps):
    i = pl.program_id(0)
    slot = i % 2
    per = tm // RES

    def copies(step, s):
        return [pltpu.make_async_copy(scratch.at[s, a, :, b, :],
                                      outs[a].at[pl.ds(pl.multiple_of(b * (SEQ // RES) + per * step, per), per), :],
                                      sems.at[s, a, b]) for a in range(len(outs)) for b in range(RES)]

    @pl.when(i >= 2)
    def _():
        for cp in copies(i - 2, slot):
            cp.wait()

    for a, tile in enumerate(tiles):
        scratch[slot, a] = tile.reshape(per, RES, tile.shape[-1])
    for cp in copies(i, slot):
        cp.start()

    @pl.when(i == n_steps - 1)
    def _():
        for cp in copies(i - 1, 1 - slot) + copies(i, slot):
            cp.wait()


def in_proj(x, pos_col, g1, w_in_t, rot):
    tm = 512
    n_steps = SEQ // tm

    def body(x_ref, pos_ref, g_ref, w_ref, invf_ref, ma_ref, mb_ref, h_ref, u_ref, vs_ref, q_ref, k_ref, v_ref, scratch, sems):
        xf = x_ref[...]
        h = (xf * _rms(xf) * g_ref[...]).astype(BF16)
        h_ref[...] = h
        proj = _dot_nt(h, w_ref[...])
        c, sa, sb = _rot_tables(pos_ref[...], invf_ref[...], ma_ref[...], mb_ref[...])
        slabs = range(ATTN_W // 128)
        q = jnp.concatenate([_rot(proj[:, j * 128:(j + 1) * 128], c, sa, sb) * Q_SCALE for j in slabs], axis=1)
        k = jnp.concatenate([_rot(proj[:, ATTN_W + j * 128:ATTN_W + (j + 1) * 128], c, sa, sb) for j in slabs], axis=1)
        u_ref[...] = proj[:, 3 * ATTN_W:3 * ATTN_W + SGU_W]
        vs_ref[...] = proj[:, 3 * ATTN_W + SGU_W:]
        _to_residue_rows([q, k, proj[:, 2 * ATTN_W:3 * ATTN_W]], [q_ref, k_ref, v_ref], scratch, sems, tm, n_steps)

    act = jax.ShapeDtypeStruct((SEQ, 512), F32)
    return _call(
        "in_proj", body, n_steps,
        [_row_spec(tm, D_MODEL), _row_spec(tm, 1), _full_spec((1, D_MODEL)), _full_spec((IN_W, D_MODEL)),
         _full_spec((1, 128)), _full_spec((1, 128)), _full_spec((1, 128))],
        [_row_spec(tm, D_MODEL)] + [_row_spec(tm, 512)] * 2 + [ANY] * 3,
        [jax.ShapeDtypeStruct((SEQ, D_MODEL), BF16)] + [act] * 5,
        (x, pos_col, g1, w_in_t, *rot), scratch_shapes=_residue_scratch(3, tm, ATTN_W))


def _to_residue_order(t):
    return t.reshape(SEQ // RES, RES, -1).transpose(1, 0, 2).reshape(t.shape)


def _from_residue_order(t):
    return t.reshape(RES, SEQ // RES, -1).transpose(1, 0, 2).reshape(t.shape)


def _block_rows(d, r, n):
    if d == 16:
        slices = [(128 * r, 128)]
    elif d == 4:
        slices = [(128 * (4 * b + r) + 32 * n, 32) for b in range(4)]
    else:
        slices = [(128 * b + 8 * n, 8) for b in range(RES)]
    return [(s if isinstance(s, int) else pl.multiple_of(s, z), z) for s, z in slices]


def _block_step(d, i):
    if d == 16:
        return i
    if d == 4:
        return 4 * (i & 31) + (i >> 5)
    return 16 * (i & 7) + (i >> 3)


def _attn_masks(d):
    row2 = _block_step(d, lax.broadcasted_iota(jnp.int32, (128, 256), 0))
    col2 = lax.broadcasted_iota(jnp.int32, (128, 256), 1)
    key2 = _block_step(d, col2 & 127)
    mask2 = jnp.logical_or(jnp.logical_and(col2 < 128, key2 >= row2), jnp.logical_and(col2 >= 128, key2 <= row2))
    row1 = _block_step(d, lax.broadcasted_iota(jnp.int32, (128, 128), 0))
    col1 = lax.broadcasted_iota(jnp.int32, (128, 128), 1)
    return col1 < HEAD_DIM, _block_step(d, col1) <= row1, mask2


def _load_rows(ref, slices):
    parts = [ref[pl.ds(s, z), :] for s, z in slices]
    return parts[0] if len(parts) == 1 else jnp.concatenate(parts, axis=0)


def _for_each_group(fn):
    for p, d in enumerate(DILATIONS):
        masks = _attn_masks(d)
        if d == 16:
            def group(i, carry, p=p, masks=masks):
                fn(p, masks, [(_block_rows(16, 4 * i + g, 0), None) for g in range(4)])
                return carry

            lax.fori_loop(0, 4, group, 0)
        elif d == 4:
            fn(p, masks, [(_block_rows(4, r, 0), None) for r in range(4)])

            def group(i, carry, p=p, masks=masks):
                fn(p, masks, [(_block_rows(4, r, i + 1), _block_rows(4, r, i)) for r in range(4)])
                return carry

            lax.fori_loop(0, 3, group, 0)
        else:
            fn(p, masks, [(_block_rows(1, 0, 0), None)])

            def group(i, carry, p=p, masks=masks):
                fn(p, masks, [(_block_rows(1, 0, 3 * i + g + 1), _block_rows(1, 0, 3 * i + g)) for g in range(3)])
                return carry

            lax.fori_loop(0, 5, group, 0)


def attn_fwd(q, k, v):
    def body(q_ref, k_ref, v_ref, o_ref, lse_ref, nat_ref, op_ref, lp_ref, sems):
        def group(p, masks, blocks):
            head0, mask1, mask2 = masks
            heads = (head0, jnp.logical_not(head0))
            keys = [rows if prev is None else prev + rows for rows, prev in blocks]
            mask = [mask1 if prev is None else mask2 for _, prev in blocks]
            qb = [_load_rows(q_ref, rows) for rows, _ in blocks]
            kk = [_load_rows(k_ref, ks).astype(BF16) for ks in keys]
            vv = [_load_rows(v_ref, ks).astype(BF16) for ks in keys]
            chains = [(g, hm) for g in range(len(blocks)) for hm in heads]
            s = [jnp.where(mask[g], _dot_nt(jnp.where(hm, qb[g], 0.0).astype(BF16), kk[g]), NEG) for g, hm in chains]
            m = [jnp.max(t, axis=-1, keepdims=True) for t in s]
            e = [jnp.exp(t - mt) for t, mt in zip(s, m)]
            l = [jnp.sum(t, axis=-1, keepdims=True) for t in e]
            pv = [_dot(t.astype(BF16), vv[g]) for t, (g, _) in zip(e, chains)]
            for g, (rows, _) in enumerate(blocks):
                o_blk = jnp.where(head0, pv[2 * g] / l[2 * g], pv[2 * g + 1] / l[2 * g + 1])
                l_blk = jnp.where(head0, jnp.broadcast_to(m[2 * g] + jnp.log(l[2 * g]), (128, 128)),
                                  jnp.broadcast_to(m[2 * g + 1] + jnp.log(l[2 * g + 1]), (128, 128)))
                at = 0
                for start, size in rows:
                    op_ref[p, pl.ds(start, size), :] = o_blk[at:at + size]
                    lp_ref[p, pl.ds(start, size), :] = l_blk[at:at + size]
                    at += size

        _for_each_group(group)

        def combine(i, carry):
            rows = pl.ds(pl.multiple_of(i * 256, 256), 256)
            ls = [lp_ref[p, rows, :] for p in range(3)]
            m = jnp.maximum(jnp.maximum(ls[0], ls[1]), ls[2])
            lse = m + jnp.log(jnp.exp(ls[0] - m) + jnp.exp(ls[1] - m) + jnp.exp(ls[2] - m))
            o = jnp.zeros((256, 128), F32)
            for p in range(3):
                o = o + jnp.exp(ls[p] - lse) * op_ref[p, rows, :]
            o_ref[rows, :] = o
            lse_ref[rows, :] = lse
            return carry

        lax.fori_loop(0, SEQ // 256, combine, 0)

        lanes = pl.ds(pl.multiple_of(pl.program_id(0) * 128, 128), 128)
        back = [pltpu.make_async_copy(o_ref.at[pl.ds(b * (SEQ // RES), SEQ // RES), :], nat_ref.at[:, b, lanes], sems.at[b])
                for b in range(RES)]
        for cp in back:
            cp.start()
        for cp in back:
            cp.wait()

    slab = pl.BlockSpec((SEQ, 128), lambda i: (0, i))
    out = jax.ShapeDtypeStruct((SEQ, ATTN_W), F32)
    attn_r, lse, attn = _call(
        "attn_fwd", body, ATTN_W // 128, [slab] * 3, [slab] * 2 + [ANY],
        [out, out, jax.ShapeDtypeStruct((SEQ // RES, RES, ATTN_W), F32)], (q, k, v),
        scratch_shapes=[pltpu.VMEM((3, SEQ, 128), F32), pltpu.VMEM((3, SEQ, 128), F32), pltpu.SemaphoreType.DMA((RES,))])
    return attn_r, lse, attn.reshape(SEQ, ATTN_W)


def _causal_weights(w_ref):
    row = lax.broadcasted_iota(jnp.int32, (CHUNK, CHUNK), 0)
    col = lax.broadcasted_iota(jnp.int32, (CHUNK, CHUNK), 1)
    return [jnp.where(col <= row, w_ref[g], 0.0).astype(BF16) for g in range(N_GROUPS)], col <= row


def _sgu_chunk_fwd(u, vs, lg, lb, wc, bfull, head0):
    ug = _gelu(u)
    vg = _gelu(vs)
    xc = vg - jnp.mean(vg, axis=-1, keepdims=True)
    rstd = lax.rsqrt(jnp.mean(xc * xc, axis=-1, keepdims=True) + LN_EPS)
    xhat = xc * rstd
    vn = xhat * lg + lb
    mixed = []
    for gp in range(SGU_W // 128):
        vp = vn[:, gp * 128:(gp + 1) * 128].astype(BF16)
        mixed.append(jnp.where(head0, _dot(wc[2 * gp], vp), _dot(wc[2 * gp + 1], vp)))
    ms = jnp.concatenate(mixed, axis=1) + bfull
    return ug, xhat, rstd, vn, ms


def sgu_fwd(u, vs, lg, lb, w_sp, bfull):
    cpb = 4

    def body(u_ref, vs_ref, lg_ref, lb_ref, w_ref, b_ref, o_ref):
        wc, _ = _causal_weights(w_ref)
        head0 = lax.broadcasted_iota(jnp.int32, (CHUNK, 128), 1) < HEAD_DIM
        for ci in range(cpb):
            rows = pl.ds(ci * CHUNK, CHUNK)
            ug, _, _, _, ms = _sgu_chunk_fwd(u_ref[rows, :], vs_ref[rows, :], lg_ref[...], lb_ref[...], wc, b_ref[...], head0)
            o_ref[rows, :] = ug * ms

    tm = cpb * CHUNK
    return _call(
        "sgu_fwd", body, SEQ // tm,
        [_row_spec(tm, SGU_W), _row_spec(tm, SGU_W), _full_spec((1, SGU_W)), _full_spec((1, SGU_W)),
         _full_spec((N_GROUPS, CHUNK, CHUNK)), _full_spec((CHUNK, SGU_W))],
        [_row_spec(tm, SGU_W)], [jax.ShapeDtypeStruct((SEQ, SGU_W), F32)],
        (u, vs, lg, lb, w_sp, bfull))


def out_proj(attn, sgu, x, ga, gs, w_out, gpm, gpf):
    tm = 512

    def body(a_ref, s_ref, x_ref, ga_ref, gs_ref, w_ref, gpm_ref, gpf_ref, mix_ref, y_ref, x2_ref, h2_ref):
        a = a_ref[...]
        s = s_ref[...]
        an = (a * _rms(a) * ga_ref[...]).astype(BF16)
        sn = (s * _rms(s) * gs_ref[...]).astype(BF16)
        mix_ref[:, :ATTN_W] = an
        mix_ref[:, ATTN_W:] = sn
        y = _dot(an, w_ref[:ATTN_W, :]) + _dot(sn, w_ref[ATTN_W:, :])
        y_ref[...] = y
        x2 = x_ref[...] + y * _rms(y) * gpm_ref[...]
        x2_ref[...] = x2
        h2_ref[...] = (x2 * _rms(x2) * gpf_ref[...]).astype(BF16)

    wide = jax.ShapeDtypeStruct((SEQ, D_MODEL), F32)
    wide16 = jax.ShapeDtypeStruct((SEQ, D_MODEL), BF16)
    return _call(
        "out_proj", body, SEQ // tm,
        [_row_spec(tm, ATTN_W), _row_spec(tm, SGU_W), _row_spec(tm, D_MODEL), _full_spec((1, ATTN_W)),
         _full_spec((1, SGU_W)), _full_spec((D_MODEL, D_MODEL)), _full_spec((1, D_MODEL)), _full_spec((1, D_MODEL))],
        [_row_spec(tm, D_MODEL)] * 4, [wide16, wide, wide, wide16],
        (attn, sgu, x, ga, gs, w_out, gpm, gpf))


def ffn_up(h2, w_gate_t, w_up_t):
    tm = 256

    def body(h_ref, wg_ref, wu_ref, g_ref, u_ref, a_ref):
        h = h_ref[...]
        g = _dot_nt(h, wg_ref[...])
        u = _dot_nt(h, wu_ref[...])
        g_ref[...] = g.astype(BF16)
        u_ref[...] = u.astype(BF16)
        a_ref[...] = (g * jax.nn.sigmoid(g) * u).astype(BF16)

    ff = jax.ShapeDtypeStruct((SEQ, D_FF), BF16)
    return _call(
        "ffn_up", body, SEQ // tm,
        [_row_spec(tm, D_MODEL), _full_spec((D_FF, D_MODEL)), _full_spec((D_FF, D_MODEL))],
        [_row_spec(tm, D_FF)] * 3, [ff, ff, jax.ShapeDtypeStruct((SEQ, D_FF), BF16)],
        (h2, w_gate_t, w_up_t))


def ffn_down_loss(act, w_down, x2, gpo, target):
    tm = 512

    def body(a_ref, w_ref, x2_ref, g_ref, t_ref, df_ref, dx3_ref, dg_ref, loss_ref):
        f = _dot(a_ref[...], w_ref[...])
        gain = g_ref[...]
        err = x2_ref[...] + f * _rms(f) * gain - t_ref[...]
        dx3 = err * np.float32(1.0 / D_MODEL)
        dx3_ref[...] = dx3
        df, dg = _rms_bwd(f, gain, dx3)
        df_ref[...] = df.astype(BF16)

        @pl.when(pl.program_id(0) == 0)
        def _():
            dg_ref[...] = jnp.zeros_like(dg_ref)
            loss_ref[...] = jnp.zeros_like(loss_ref)

        dg_ref[...] += dg
        loss_ref[...] += jnp.sum(err * err, axis=(0, 1), keepdims=True)

    return pl.pallas_call(
        body, name="ffn_down_loss", grid=(SEQ // tm,),
        in_specs=[_row_spec(tm, D_FF), _full_spec((D_FF, D_MODEL)), _row_spec(tm, D_MODEL), _full_spec((1, D_MODEL)),
                  _row_spec(tm, D_MODEL)],
        out_specs=[_row_spec(tm, D_MODEL), _row_spec(tm, D_MODEL), _full_spec((1, D_MODEL)), _full_spec((1, 1))],
        out_shape=[jax.ShapeDtypeStruct((SEQ, D_MODEL), BF16), jax.ShapeDtypeStruct((SEQ, D_MODEL), F32),
                   jax.ShapeDtypeStruct((1, D_MODEL), F32), jax.ShapeDtypeStruct((1, 1), F32)],
        compiler_params=_params(),
    )(act, w_down, x2, gpo, target)


def ffn_act_bwd(df, w_down, gate, up, after=()):
    tm = 256

    def body(df_ref, w_ref, g_ref, u_ref, dg_ref, du_ref):
        dact = _dot_nt(df_ref[...], w_ref[...])
        g = g_ref[...].astype(F32)
        s = jax.nn.sigmoid(g)
        du_ref[...] = (dact * g * s).astype(BF16)
        dg_ref[...] = (dact * u_ref[...].astype(F32) * (s * (1.0 + g * (1.0 - s)))).astype(BF16)

    ff16 = jax.ShapeDtypeStruct((SEQ, D_FF), BF16)
    return _call(
        "ffn_act_bwd", body, SEQ // tm,
        [_row_spec(tm, D_MODEL), _full_spec((D_FF, D_MODEL)), _row_spec(tm, D_FF), _row_spec(tm, D_FF)],
        [_row_spec(tm, D_FF)] * 2, [ff16, ff16], (df, w_down, gate, up), after=after)


def ffn_in_bwd(dgate, dup, w_gate_t, w_up_t, x2, gpf, dx3, y, gpm, after=()):
    tm = 256

    def body(dg_ref, du_ref, wg_ref, wu_ref, x2_ref, gpf_ref, dx3_ref, y_ref, gpm_ref, dx2_ref, dy_ref, dgpf_ref, dgpm_ref):
        dh2 = _dot(dg_ref[...], wg_ref[...]) + _dot(du_ref[...], wu_ref[...])
        dz, dgpf = _rms_bwd(x2_ref[...], gpf_ref[...], dh2)
        dx2 = dx3_ref[...] + dz
        dx2_ref[...] = dx2
        dy, dgpm = _rms_bwd(y_ref[...], gpm_ref[...], dx2)
        dy_ref[...] = dy.astype(BF16)

        @pl.when(pl.program_id(0) == 0)
        def _():
            dgpf_ref[...] = jnp.zeros_like(dgpf_ref)
            dgpm_ref[...] = jnp.zeros_like(dgpm_ref)

        dgpf_ref[...] += dgpf
        dgpm_ref[...] += dgpm

    vec = jax.ShapeDtypeStruct((1, D_MODEL), F32)
    return _call(
        "ffn_in_bwd", body, SEQ // tm,
        [_row_spec(tm, D_FF), _row_spec(tm, D_FF), _full_spec((D_FF, D_MODEL)), _full_spec((D_FF, D_MODEL)),
         _row_spec(tm, D_MODEL), _full_spec((1, D_MODEL)), _row_spec(tm, D_MODEL), _row_spec(tm, D_MODEL),
         _full_spec((1, D_MODEL))],
        [_row_spec(tm, D_MODEL), _row_spec(tm, D_MODEL), _full_spec((1, D_MODEL)), _full_spec((1, D_MODEL))],
        [jax.ShapeDtypeStruct((SEQ, D_MODEL), F32), jax.ShapeDtypeStruct((SEQ, D_MODEL), BF16), vec, vec],
        (dgate, dup, w_gate_t, w_up_t, x2, gpf, dx3, y, gpm), after=after)


def weight_grad(name, a, b, after=()):
    m, n = a.shape[1], b.shape[1]
    tr = 256

    def body(a_ref, b_ref, o_ref):
        o_ref[...] = _dot_tn(a_ref[...], b_ref[...]).astype(BF16)

    (out,) = _call(
        name, body, m // tr, [pl.BlockSpec((SEQ, tr), lambda i: (0, i)), _full_spec((SEQ, n))],
        [_row_spec(tr, n)], [jax.ShapeDtypeStruct((m, n), BF16)], (a, b), after=after)
    return out.reshape(N_DEV, m // N_DEV, n)


def mix_bwd(dy, w_out, attn, sgu, ga, gs, after=()):
    tm = 512
    n_steps = SEQ // tm

    def body(dy_ref, w_ref, a_ref, s_ref, ga_ref, gs_ref, ds_ref, dga_ref, dgs_ref, da_ref, scratch, sems):
        dy = dy_ref[...]
        da, dga = _rms_bwd(a_ref[...], ga_ref[...], _dot_nt(dy, w_ref[:ATTN_W, :]))
        ds, dgs = _rms_bwd(s_ref[...], gs_ref[...], _dot_nt(dy, w_ref[ATTN_W:, :]))
        ds_ref[...] = ds
        _to_residue_rows([da], [da_ref], scratch, sems, tm, n_steps)

        @pl.when(pl.program_id(0) == 0)
        def _():
            dga_ref[...] = jnp.zeros_like(dga_ref)
            dgs_ref[...] = jnp.zeros_like(dgs_ref)

        dga_ref[...] += dga
        dgs_ref[...] += dgs

    half = jax.ShapeDtypeStruct((SEQ, 512), F32)
    vec = jax.ShapeDtypeStruct((1, 512), F32)
    return _call(
        "mix_bwd", body, n_steps,
        [_row_spec(tm, D_MODEL), _full_spec((D_MODEL, D_MODEL)), _row_spec(tm, 512), _row_spec(tm, 512),
         _full_spec((1, 512)), _full_spec((1, 512))],
        [_row_spec(tm, 512), _full_spec((1, 512)), _full_spec((1, 512)), ANY],
        [half, vec, vec, half], (dy, w_out, attn, sgu, ga, gs), scratch_shapes=_residue_scratch(1, tm, ATTN_W), after=after)


def sgu_bwd(u, vs, dsgu, lg, lb, w_sp, bfull):
    cpb = 4

    def body(u_ref, vs_ref, d_ref, lg_ref, lb_ref, w_ref, b_ref, du_ref, dvs_ref, dlg_ref, dlb_ref, dw_ref, db_ref):
        wc, causal = _causal_weights(w_ref)
        head0 = lax.broadcasted_iota(jnp.int32, (CHUNK, 128), 1) < HEAD_DIM
        lg = lg_ref[...]

        @pl.when(pl.program_id(0) == 0)
        def _():
            dlg_ref[...] = jnp.zeros_like(dlg_ref)
            dlb_ref[...] = jnp.zeros_like(dlb_ref)
            dw_ref[...] = jnp.zeros_like(dw_ref)
            db_ref[...] = jnp.zeros_like(db_ref)

        for ci in range(cpb):
            rows = pl.ds(ci * CHUNK, CHUNK)
            u = u_ref[rows, :]
            vs = vs_ref[rows, :]
            d = d_ref[rows, :]
            ug, xhat, rstd, vn, ms = _sgu_chunk_fwd(u, vs, lg, lb_ref[...], wc, b_ref[...], head0)
            du_ref[rows, :] = (d * ms * _gelu_grad(u)).astype(BF16)
            dms = d * ug
            db_ref[...] += dms
            dvn = []
            for gp in range(SGU_W // 128):
                dmp = dms[:, gp * 128:(gp + 1) * 128]
                dm0 = jnp.where(head0, dmp, 0.0).astype(BF16)
                dm1 = jnp.where(head0, 0.0, dmp).astype(BF16)
                vp = vn[:, gp * 128:(gp + 1) * 128].astype(BF16)
                dw_ref[2 * gp] += _dot_nt(dm0, vp)
                dw_ref[2 * gp + 1] += _dot_nt(dm1, vp)
                dvn.append(_dot_tn(wc[2 * gp], dm0) + _dot_tn(wc[2 * gp + 1], dm1))
            dvn = jnp.concatenate(dvn, axis=1)
            dlg_ref[...] += jnp.sum(dvn * xhat, axis=0, keepdims=True)
            dlb_ref[...] += jnp.sum(dvn, axis=0, keepdims=True)
            dxh = dvn * lg
            dvg = rstd * (dxh - jnp.mean(dxh, axis=-1, keepdims=True) - xhat * jnp.mean(dxh * xhat, axis=-1, keepdims=True))
            dvs_ref[rows, :] = (dvg * _gelu_grad(vs)).astype(BF16)

        @pl.when(pl.program_id(0) == pl.num_programs(0) - 1)
        def _():
            for g in range(N_GROUPS):
                dw_ref[g] = jnp.where(causal, dw_ref[g], 0.0)

    tm = cpb * CHUNK
    half16 = jax.ShapeDtypeStruct((SEQ, SGU_W), BF16)
    vec = jax.ShapeDtypeStruct((1, SGU_W), F32)
    return _call(
        "sgu_bwd", body, SEQ // tm,
        [_row_spec(tm, SGU_W)] * 3 + [_full_spec((1, SGU_W)), _full_spec((1, SGU_W)),
                                      _full_spec((N_GROUPS, CHUNK, CHUNK)), _full_spec((CHUNK, SGU_W))],
        [_row_spec(tm, SGU_W), _row_spec(tm, SGU_W), _full_spec((1, SGU_W)), _full_spec((1, SGU_W)),
         _full_spec((N_GROUPS, CHUNK, CHUNK)), _full_spec((CHUNK, SGU_W))],
        [half16, half16, vec, vec, jax.ShapeDtypeStruct((N_GROUPS, CHUNK, CHUNK), F32),
         jax.ShapeDtypeStruct((CHUNK, SGU_W), F32)],
        (u, vs, dsgu, lg, lb, w_sp, bfull))


def attn_bwd(q, k, v, o, lse, do, pos_col, rot):
    def body(q_ref, k_ref, v_ref, o_ref, lse_ref, do_ref, pos_ref, invf_ref, ma_ref, mb_ref,
             dq_ref, dk_ref, dv_ref, dqa_ref, dka_ref, dva_ref, dlt_ref, rot_ref):
        dqa_ref[...] = jnp.zeros_like(dqa_ref)
        dka_ref[...] = jnp.zeros_like(dka_ref)
        dva_ref[...] = jnp.zeros_like(dva_ref)

        def delta(i, carry):
            rows = pl.ds(pl.multiple_of(i * 256, 256), 256)
            prod = do_ref[rows, :] * o_ref[rows, :]
            h0 = lax.broadcasted_iota(jnp.int32, (256, 128), 1) < HEAD_DIM
            d0 = jnp.sum(jnp.where(h0, prod, 0.0), axis=-1, keepdims=True)
            d1 = jnp.sum(jnp.where(h0, 0.0, prod), axis=-1, keepdims=True)
            dlt_ref[rows, :] = jnp.where(h0, d0, d1)
            return carry

        lax.fori_loop(0, SEQ // 256, delta, 0)

        def add_rows(ref, slices, val):
            at = 0
            for start, size in slices:
                ref[pl.ds(start, size), :] += val[at:at + size]
                at += size

        def group(p, masks, blocks):
            head0, mask1, mask2 = masks
            heads = (head0, jnp.logical_not(head0))
            keys = [rows if prev is None else prev + rows for rows, prev in blocks]
            mask = [mask1 if prev is None else mask2 for _, prev in blocks]
            kk = [_load_rows(k_ref, ks).astype(BF16) for ks in keys]
            vv = [_load_rows(v_ref, ks).astype(BF16) for ks in keys]
            qb = [_load_rows(q_ref, rows) for rows, _ in blocks]
            dob = [_load_rows(do_ref, rows) for rows, _ in blocks]
            lse_b = [_load_rows(lse_ref, rows) for rows, _ in blocks]
            dlt_b = [_load_rows(dlt_ref, rows) for rows, _ in blocks]
            chains = [(g, h) for g in range(len(blocks)) for h in range(2)]
            qm = [jnp.where(heads[h], qb[g], 0.0).astype(BF16) for g, h in chains]
            dom = [jnp.where(heads[h], dob[g], 0.0).astype(BF16) for g, h in chains]
            s = [_dot_nt(qm[c], kk[g]) for c, (g, h) in enumerate(chains)]
            dp = [_dot_nt(dom[c], vv[g]) for c, (g, h) in enumerate(chains)]
            pr = [jnp.where(mask[g], jnp.exp(s[c] - lse_b[g][:, h * HEAD_DIM:h * HEAD_DIM + 1]), 0.0)
                  for c, (g, h) in enumerate(chains)]
            ds = [(pr[c] * (dp[c] - dlt_b[g][:, h * HEAD_DIM:h * HEAD_DIM + 1])).astype(BF16)
                  for c, (g, h) in enumerate(chains)]
            dv = [_dot_tn(pr[c].astype(BF16), dom[c]) for c in range(len(chains))]
            dk = [_dot_tn(ds[c], qm[c]) for c in range(len(chains))]
            dq = [_dot(ds[c], kk[g]) for c, (g, h) in enumerate(chains)]
            for g, (rows, _) in enumerate(blocks):
                add_rows(dqa_ref, rows, jnp.where(head0, dq[2 * g], dq[2 * g + 1]))
                add_rows(dka_ref, keys[g], dk[2 * g] + dk[2 * g + 1])
                add_rows(dva_ref, keys[g], dv[2 * g] + dv[2 * g + 1])

        _for_each_group(group)

        @pl.when(pl.program_id(0) == 0)
        def _():
            def tables(i, carry):
                rows = pl.ds(pl.multiple_of(i * 256, 256), 256)
                c, sa, sb = _rot_tables(pos_ref[rows, :], invf_ref[...], ma_ref[...], mb_ref[...])
                rot_ref[0, rows, :] = c
                rot_ref[1, rows, :] = sa
                rot_ref[2, rows, :] = sb
                return carry

            lax.fori_loop(0, SEQ // 256, tables, 0)

        def finish(i, carry):
            rows = pl.ds(pl.multiple_of(i * 256, 256), 256)
            c, sa, sb = rot_ref[0, rows, :], rot_ref[1, rows, :], rot_ref[2, rows, :]
            dq_ref[rows, :] = _rot_t(dqa_ref[rows, :] * Q_SCALE, c, sa, sb).astype(BF16)
            dk_ref[rows, :] = _rot_t(dka_ref[rows, :], c, sa, sb).astype(BF16)
            dv_ref[rows, :] = dva_ref[rows, :].astype(BF16)
            return carry

        lax.fori_loop(0, SEQ // 256, finish, 0)

    slab = pl.BlockSpec((SEQ, 128), lambda i: (0, i))
    out = jax.ShapeDtypeStruct((SEQ, ATTN_W), BF16)
    acc = pltpu.VMEM((SEQ, 128), F32)
    return _call(
        "attn_bwd", body, ATTN_W // 128,
        [slab] * 6 + [_full_spec((SEQ, 1)), _full_spec((1, 128)), _full_spec((1, 128)), _full_spec((1, 128))],
        [slab] * 3, [out, out, out], (q, k, v, o, lse, do, pos_col, *rot),
        scratch_shapes=[acc, acc, acc, acc, pltpu.VMEM((3, SEQ, 128), F32)])


def in_bwd(dproj, w_in_t, x, g1, dx2, after=()):
    tm = 512

    def body(dp_ref, w_ref, x_ref, g_ref, dx2_ref, dx_ref, dg_ref):
        dh1 = _dot(dp_ref[...], w_ref[...])
        dz, dg = _rms_bwd(x_ref[...], g_ref[...], dh1)
        dx_ref[...] = dx2_ref[...] + dz

        @pl.when(pl.program_id(0) == 0)
        def _():
            dg_ref[...] = jnp.zeros_like(dg_ref)

        dg_ref[...] += dg

    return _call(
        "in_bwd", body, SEQ // tm,
        [_row_spec(tm, IN_W), _full_spec((IN_W, D_MODEL)), _row_spec(tm, D_MODEL), _full_spec((1, D_MODEL)),
         _row_spec(tm, D_MODEL)],
        [_row_spec(tm, D_MODEL), _full_spec((1, D_MODEL))],
        [jax.ShapeDtypeStruct((SEQ, D_MODEL), F32), jax.ShapeDtypeStruct((1, D_MODEL), F32)],
        (dproj, w_in_t, x, g1, dx2), after=after)


def _coords():
    return lax.axis_index("x"), lax.axis_index("y"), lax.axis_index("c")


class Exchange:
    def __init__(self, srcs, bufs, new_shapes, n_sems, make, has_middle=False):
        self.srcs, self.bufs, self.new_shapes, self.n_sems, self.make = list(srcs), list(bufs), list(new_shapes), n_sems, make
        self.has_middle = has_middle


def _call(name, body, n_steps, in_specs, out_specs, out_shape, args, scratch_shapes=(), after=()):
    n_in = len(args)

    def wrapped(*refs):
        body(*refs[:n_in], *refs[n_in + len(after):])

    return list(pl.pallas_call(
        wrapped, name=name, grid=(n_steps,), in_specs=list(in_specs) + [ANY] * len(after), out_specs=list(out_specs),
        out_shape=list(out_shape), scratch_shapes=list(scratch_shapes), compiler_params=_params(),
    )(*args, *after))


def _gather_copies(kinds, bufs, ranges, send_sems, recv_sems):
    x, y, c = _coords()
    me, sibling = (x, y, c), (x, y, 1 - c)
    chips = [(1 - x, y), (x, 1 - y), (1 - x, 1 - y)]

    def copy(a, k, block, to):
        lo, hi = ranges[a]
        r = bufs[a].shape[0] // N_DEV
        rows = bufs[a].at[pl.ds((4 * block[0] + 2 * block[1] + block[2]) * r + lo, hi - lo), :]
        return pltpu.make_async_remote_copy(src_ref=rows, dst_ref=rows, send_sem=send_sems.at[7 * a + k],
                                            recv_sem=recv_sems.at[7 * a + k], device_id=to, device_id_type=MESH)

    every = range(len(bufs))
    make = {
        "out": lambda: [copy(a, 0, me, sibling) for a in every]
        + [copy(a, 1 + j, me, (*chip, c)) for a in every for j, chip in enumerate(chips)],
        "from_core": lambda: [copy(a, 0, sibling, me) for a in every],
        "from_chips": lambda: [copy(a, 1 + j, (*chip, c), me) for a in every for j, chip in enumerate(chips)],
        "on": lambda: [copy(a, 4 + j, (*chip, c), sibling) for a in every for j, chip in enumerate(chips)],
        "on_in": lambda: [copy(a, 4 + j, (*chip, 1 - c), me) for a in every for j, chip in enumerate(chips)],
    }
    return [make[kind]() for kind in kinds]


def gather(bufs):
    ranges = [(0, b.shape[0] // N_DEV) for b in bufs]

    def make(phase, src_refs, buf_refs, new_refs, send_sems, recv_sems):
        kinds = {"start": ["out"], "middle": ["from_chips", "on"], "end": ["out", "on", "from_core", "on_in"]}[phase]
        made = _gather_copies(kinds, buf_refs, ranges, send_sems, recv_sems)
        if phase == "start":
            return made[0]
        if phase == "middle":
            return made[0], made[1]
        return made[0] + made[1], made[2] + made[3]

    return Exchange([], bufs, [], 7 * len(bufs), make, has_middle=True)


TO_GATHER = (1, lambda x, y, c: [(x, y, 1 - c), (1 - x, y, c), (x, 1 - y, c), (1 - x, 1 - y, c)])
TO_SIBLING = (2, lambda x, y, c: [(x, y, 1 - c)])
TO_CHIPS = (3, lambda x, y, c: [(1 - x, y, c), (x, 1 - y, c), (1 - x, 1 - y, c)])
TO_ALL = (4, lambda x, y, c: [(x ^ (m >> 2), y ^ ((m >> 1) & 1), c ^ (m & 1)) for m in range(1, N_DEV)])


def by_sequencer(name, exchanges, who):
    collective_id, peers_of = who
    hbm = pltpu.MemorySpace.HBM
    refs = [([jax.new_ref(a, memory_space=hbm) for a in ex.srcs], [jax.new_ref(a, memory_space=hbm) for a in ex.bufs],
             [jax.empty_ref(s, memory_space=hbm) for s in ex.new_shapes]) for ex in exchanges]
    sems = []
    for ex in exchanges:
        sems += [pltpu.SemaphoreType.DMA((ex.n_sems,)), pltpu.SemaphoreType.DMA((ex.n_sems,))]

    @pl.kernel(mesh=plsc.ScalarSubcoreMesh(axis_name="sequencer", num_cores=1), name=name, scratch_types=tuple(sems),
               compiler_params=pltpu.CompilerParams(collective_id=collective_id))
    def launch(*sem_refs):
        peers = peers_of(*_coords())
        barrier = pltpu.get_barrier_semaphore()
        for peer in peers:
            pl.semaphore_signal(barrier, inc=1, device_id=peer, device_id_type=MESH)
        pl.semaphore_wait(barrier, len(peers))

        def make(phase, k):
            return exchanges[k].make(phase, *refs[k], sem_refs[2 * k], sem_refs[2 * k + 1])

        for k in range(len(exchanges)):
            for cp in make("start", k):
                cp.start()
        for k, ex in enumerate(exchanges):
            if ex.has_middle:
                arrivals, starts = make("middle", k)
                for cp in arrivals:
                    cp.wait_recv()
                for cp in starts:
                    cp.start()
        for k in range(len(exchanges)):
            sends, arrivals = make("end", k)
            for cp in arrivals:
                cp.wait_recv()
            for cp in sends:
                cp.wait_send()

    launch()
    return [([ref[...] for ref in bufs], [ref[...] for ref in news]) for _, bufs, news in refs]


def place_shards(name, shards, dev):
    n = len(shards)

    def body(dev_ref, *refs):
        for a in range(n):
            refs[n + a][...] = refs[a][...].astype(BF16)

    spec = pltpu.PrefetchScalarGridSpec(
        num_scalar_prefetch=1, grid=(1,),
        in_specs=[pl.BlockSpec(s.shape, lambda i, dev_ref: (0, 0)) for s in shards],
        out_specs=[pl.BlockSpec(s.shape, lambda i, dev_ref: (dev_ref[0], 0)) for s in shards])
    return pl.pallas_call(
        body, name=name, grid_spec=spec,
        out_shape=[jax.ShapeDtypeStruct((N_DEV * s.shape[0], s.shape[1]), BF16) for s in shards],
        compiler_params=_params(),
    )(dev, *shards)


def _swap(copies_of):
    def make(phase, src_refs, buf_refs, new_refs, send_sems, recv_sems):
        copies = copies_of(src_refs, new_refs, send_sems, recv_sems)
        return copies if phase == "start" else (copies, copies)

    return make


def to_sibling(grads):
    def copies_of(src_refs, new_refs, send_sems, recv_sems):
        x, y, c = _coords()
        return [pltpu.make_async_remote_copy(
            src_ref=src_refs[a].at[2 * xy + 1 - c], dst_ref=new_refs[a].at[xy], send_sem=send_sems.at[4 * a + xy],
            recv_sem=recv_sems.at[4 * a + xy], device_id=(x, y, 1 - c), device_id_type=MESH)
            for a in range(len(src_refs)) for xy in range(4)]

    return Exchange(grads, [], [jax.ShapeDtypeStruct((4,) + g.shape[1:], g.dtype) for g in grads], 4 * len(grads),
                    _swap(copies_of))


def to_chips(parts):
    def copies_of(src_refs, new_refs, send_sems, recv_sems):
        x, y, c = _coords()
        chips = [(1 - x, y), (x, 1 - y), (1 - x, 1 - y)]
        return [pltpu.make_async_remote_copy(
            src_ref=src_refs[a].at[2 * px + py], dst_ref=new_refs[a].at[2 * x + y], send_sem=send_sems.at[3 * a + j],
            recv_sem=recv_sems.at[3 * a + j], device_id=(px, py, c), device_id_type=MESH)
            for a in range(len(src_refs)) for j, (px, py) in enumerate(chips)]

    return Exchange(parts, [], [jax.ShapeDtypeStruct(p.shape, p.dtype) for p in parts], 3 * len(parts), _swap(copies_of))


def to_owners(grad):
    def copies_of(src_refs, new_refs, send_sems, recv_sems):
        x, y, c = _coords()
        copies = []
        for m in range(1, N_DEV):
            px, py, pc = x ^ (m >> 2), y ^ ((m >> 1) & 1), c ^ (m & 1)
            copies.append(pltpu.make_async_remote_copy(
                src_ref=src_refs[0].at[4 * px + 2 * py + pc], dst_ref=new_refs[0].at[4 * x + 2 * y + c],
                send_sem=send_sems.at[m - 1], recv_sem=recv_sems.at[m - 1], device_id=(px, py, pc), device_id_type=MESH))
        return copies

    return Exchange([grad], [], [jax.ShapeDtypeStruct(grad.shape, grad.dtype)], N_DEV - 1, _swap(copies_of))


def sum_cores(name, grad, other, core, after=()):
    _, r, w = other.shape

    def body(core_ref, g_ref, o_ref, *rest):
        rest[-1][...] = (g_ref[...].astype(F32) + o_ref[...].astype(F32)).astype(rest[-1].dtype)

    return pl.pallas_call(
        body, name=name,
        grid_spec=pltpu.PrefetchScalarGridSpec(
            num_scalar_prefetch=1, grid=(4,),
            in_specs=[pl.BlockSpec((1, r, w), lambda i, core_ref: (2 * i + core_ref[0], 0, 0)),
                      pl.BlockSpec((1, r, w), lambda i, core_ref: (i, 0, 0))] + [ANY] * len(after),
            out_specs=pl.BlockSpec((1, r, w), lambda i, core_ref: (i, 0, 0))),
        out_shape=jax.ShapeDtypeStruct(other.shape, other.dtype),
        compiler_params=_params(),
    )(core, grad, other, *after)


def sum_owned(name, grad, others, dev_ids, after=()):
    _, r, w = grad.shape

    def body(ids_ref, *refs):
        acc = refs[0][0]
        for k in range(1, N_DEV):
            acc = acc + refs[k][0]
        refs[-1][...] = acc

    def pick(k):
        return pl.BlockSpec((1, r, w), lambda i, ids_ref: (ids_ref[k], 0, 0))

    return pl.pallas_call(
        body, name=name,
        grid_spec=pltpu.PrefetchScalarGridSpec(
            num_scalar_prefetch=1, grid=(1,), in_specs=[pick(k) for k in range(N_DEV)] + [ANY] * len(after),
            out_specs=pl.BlockSpec((r, w), lambda i, ids_ref: (ids_ref[0], 0))),
        out_shape=jax.ShapeDtypeStruct((N_DEV * r, w), F32),
        compiler_params=_params(),
    )(dev_ids, grad, *([others] * (N_DEV - 1)), *after)


def _adamw_update(w, g, m, v):
    nm = ADAM_B1 * m + np.float32(1.0 - ADAM_B1) * g
    nv = ADAM_B2 * v + np.float32(1.0 - ADAM_B2) * (g * g)
    m_hat = nm / np.float32(1.0 - ADAM_B1 ** ADAM_STEP)
    v_hat = nv / np.float32(1.0 - ADAM_B2 ** ADAM_STEP)
    return -ADAM_LR * (m_hat / (jnp.sqrt(v_hat) + ADAM_EPS) + ADAM_WD * w), nm, nv


def adamw_of_sums(name, part, others, chip_ids, w, m, v, after):
    _, r, wd = part.shape

    def body(ids_ref, p_ref, a_ref, b_ref, c_ref, w_ref, m_ref, v_ref, after_ref, g_ref, d_ref, nm_ref, nv_ref):
        g = ((p_ref[0].astype(F32) + a_ref[0].astype(F32)) + b_ref[0].astype(F32)) + c_ref[0].astype(F32)
        g_ref[...] = g
        d_ref[...], nm_ref[...], nv_ref[...] = _adamw_update(w_ref[...], g, m_ref[...], v_ref[...])

    def pick(k):
        return pl.BlockSpec((1, r, wd), lambda i, ids_ref: (ids_ref[k], 0, 0))

    whole = pl.BlockSpec((r, wd), lambda i, ids_ref: (0, 0))
    shape = jax.ShapeDtypeStruct((r, wd), F32)
    return pl.pallas_call(
        body, name=name,
        grid_spec=pltpu.PrefetchScalarGridSpec(
            num_scalar_prefetch=1, grid=(1,), in_specs=[pick(0), pick(1), pick(2), pick(3), whole, whole, whole, ANY],
            out_specs=[whole] * 4),
        out_shape=[shape] * 4,
        compiler_params=_params(),
    )(chip_ids, part, others, others, others, w, m, v, after)


def adamw(name, w, g, m, v):
    def body(w_ref, g_ref, m_ref, v_ref, d_ref, nm_ref, nv_ref):
        d_ref[...], nm_ref[...], nv_ref[...] = _adamw_update(w_ref[...], g_ref[...], m_ref[...], v_ref[...])

    shape = jax.ShapeDtypeStruct(w.shape, F32)
    spec = _full_spec(w.shape)
    return pl.pallas_call(
        body, name=name, grid=(1,), in_specs=[spec] * 4, out_specs=[spec] * 3, out_shape=[shape] * 3,
        compiler_params=_params(),
    )(w, g, m, v)


def _pack_small(parts):
    flat = jnp.concatenate([parts[name].reshape(-1) for name, _ in SMALL])
    return jnp.pad(flat, (0, SMALL_ROWS * 128 - flat.shape[0])).reshape(SMALL_ROWS, 128)


def _unpack_small(packed, like):
    flat = packed.reshape(-1)
    out, at = {}, 0
    for name, size in SMALL:
        out[name] = flat[at:at + size].reshape(like[name].shape)
        at += size
    return out


def kernel(x, positions, pre_mix_norm, w_in, sgu_ln_gain, sgu_ln_bias, sgu_w_spatial, sgu_b_spatial, attn_out_norm, sgu_out_norm, w_out, post_mix_norm, pre_ffn_norm, w_gate, w_up, w_down, post_ffn_norm, loss_target, m_pre_mix_norm, m_w_in, m_sgu_ln_gain, m_sgu_ln_bias, m_sgu_w_spatial, m_sgu_b_spatial, m_attn_out_norm, m_sgu_out_norm, m_w_out, m_post_mix_norm, m_pre_ffn_norm, m_w_gate, m_w_up, m_w_down, m_post_ffn_norm, v_pre_mix_norm, v_w_in, v_sgu_ln_gain, v_sgu_ln_bias, v_sgu_w_spatial, v_sgu_b_spatial, v_attn_out_norm, v_sgu_out_norm, v_w_out, v_post_mix_norm, v_pre_ffn_norm, v_w_gate, v_w_up, v_w_down, v_post_ffn_norm):
    small_w = dict(pre_mix_norm=pre_mix_norm, sgu_ln_gain=sgu_ln_gain, sgu_ln_bias=sgu_ln_bias, sgu_w_spatial=sgu_w_spatial,
                   sgu_b_spatial=sgu_b_spatial, attn_out_norm=attn_out_norm, sgu_out_norm=sgu_out_norm,
                   post_mix_norm=post_mix_norm, pre_ffn_norm=pre_ffn_norm, post_ffn_norm=post_ffn_norm)
    small_m = dict(pre_mix_norm=m_pre_mix_norm, sgu_ln_gain=m_sgu_ln_gain, sgu_ln_bias=m_sgu_ln_bias, sgu_w_spatial=m_sgu_w_spatial,
                   sgu_b_spatial=m_sgu_b_spatial, attn_out_norm=m_attn_out_norm, sgu_out_norm=m_sgu_out_norm,
                   post_mix_norm=m_post_mix_norm, pre_ffn_norm=m_pre_ffn_norm, post_ffn_norm=m_post_ffn_norm)
    small_v = dict(pre_mix_norm=v_pre_mix_norm, sgu_ln_gain=v_sgu_ln_gain, sgu_ln_bias=v_sgu_ln_bias, sgu_w_spatial=v_sgu_w_spatial,
                   sgu_b_spatial=v_sgu_b_spatial, attn_out_norm=v_attn_out_norm, sgu_out_norm=v_sgu_out_norm,
                   post_mix_norm=v_post_mix_norm, pre_ffn_norm=v_pre_ffn_norm, post_ffn_norm=v_post_ffn_norm)
    for table in (small_w, small_m, small_v):
        table["loss_sum"] = jnp.zeros((1,), F32)

    x2d = x[0]
    target = loss_target[0]
    pos_col = positions.reshape(SEQ, 1)
    rot = _rot_consts()
    w_sp = sgu_w_spatial[0]
    bfull = jnp.repeat(sgu_b_spatial[0].T, HEAD_DIM, axis=1)

    x_i, y_i, c_i = (lax.axis_index(a).astype(jnp.int32) for a in MESH_AXES)
    dev = 4 * x_i + 2 * y_i + c_i
    core = c_i.reshape(1)
    chip = 2 * x_i + y_i
    chip_ids = jnp.stack([chip, chip ^ 1, chip ^ 2, chip ^ 3])
    dev_ids = jnp.stack([dev ^ m for m in range(N_DEV)])

    def gathered(name, bufs):
        return by_sequencer(name, [gather(bufs)], TO_GATHER)[0][0]

    def from_sibling(name, grads):
        return by_sequencer(name, [to_sibling(grads)], TO_SIBLING)[0][1]

    def from_chips(name, parts):
        return by_sequencer(name, [to_chips(parts)], TO_CHIPS)[0][1]

    (w_in_t,) = place_shards("place_w_in", [w_in[0].T], dev.reshape(1))
    (w_in_t,) = gathered("gather_w_in", [w_in_t])
    w_gate_t, w_up_t, w_out_f, w_down_f = place_shards(
        "place_weights", [w_gate[0].T, w_up[0].T, w_out[0], w_down[0]], dev.reshape(1))
    (w_out_f,) = gathered("gather_w_out", [w_out_f])
    w_gate_t, w_up_t = gathered("gather_w_gate_up", [w_gate_t, w_up_t])
    (w_down_f,) = gathered("gather_w_down", [w_down_f])

    h1, u, vs, q, k, v = in_proj(x2d, pos_col, pre_mix_norm, w_in_t, rot)
    attn_r, lse, attn = attn_fwd(q, k, v)
    (sgu,) = sgu_fwd(u, vs, sgu_ln_gain, sgu_ln_bias, w_sp, bfull)
    mix, y, x2, h2 = out_proj(attn, sgu, x2d, attn_out_norm, sgu_out_norm, w_out_f, post_mix_norm, pre_ffn_norm)
    gate, up, act = ffn_up(h2, w_gate_t, w_up_t)
    df, dx3, d_post_ffn, sq_err = ffn_down_loss(act, w_down_f, x2, post_ffn_norm, target)

    g_w_down = weight_grad("grad_w_down", act, df)
    (s_down,) = from_sibling("w_down_to_sibling", [g_w_down])
    dgate, dup = ffn_act_bwd(df, w_down_f, gate, up, after=[g_w_down])
    p_down = sum_cores("sum_cores_down", g_w_down, s_down, core, after=[dgate])
    (c_down,) = from_chips("w_down_to_chips", [p_down])
    g_w_gate = weight_grad("grad_w_gate", dgate, h2, after=[p_down])
    g_w_up = weight_grad("grad_w_up", dup, h2, after=[g_w_gate])
    s_gate, s_up = from_sibling("w_gate_up_to_sibling", [g_w_gate, g_w_up])
    dx2, dy, d_pre_ffn, d_post_mix = ffn_in_bwd(
        dgate, dup, w_gate_t, w_up_t, x2, pre_ffn_norm, dx3, y, post_mix_norm, after=[g_w_gate, g_w_up, c_down])
    p_gate = sum_cores("sum_cores_gate", g_w_gate, s_gate, core, after=[dy])
    p_up = sum_cores("sum_cores_up", g_w_up, s_up, core, after=[dy])
    c_gate, c_up = from_chips("w_gate_up_to_chips", [p_gate, p_up])
    g_w_out = weight_grad("grad_w_out", mix, dy, after=[p_gate, p_up])
    (s_out,) = from_sibling("w_out_to_sibling", [g_w_out])
    dsgu, d_attn_out, d_sgu_out, dattn_r = mix_bwd(dy, w_out_f, attn, sgu, attn_out_norm, sgu_out_norm, after=[g_w_out])
    du, dvs, d_ln_gain, d_ln_bias, d_w_sp, d_bfull = sgu_bwd(u, vs, dsgu, sgu_ln_gain, sgu_ln_bias, w_sp, bfull)
    dq, dk, dv = attn_bwd(q, k, v, attn_r, lse, dattn_r, _to_residue_order(pos_col), rot)
    p_out = sum_cores("sum_cores_out", g_w_out, s_out, core, after=[dq])
    (c_out,) = from_chips("w_out_to_chips", [p_out])
    dq, dk, dv = (_from_residue_order(t) for t in (dq, dk, dv))
    dproj = jnp.concatenate([dq, dk, dv, du, dvs], axis=1)
    g_w_in = weight_grad("grad_w_in", dproj, h1, after=[c_gate, c_up])
    (s_in,) = from_sibling("w_in_to_sibling", [g_w_in])
    grad_x, d_pre_mix = in_bwd(dproj, w_in_t, x2d, pre_mix_norm, dx2, after=[g_w_in, c_out])
    p_in = sum_cores("sum_cores_in", g_w_in, s_in, core, after=[d_pre_mix])

    d_b_sp = d_bfull.reshape(CHUNK, N_GROUPS, HEAD_DIM).sum(axis=-1).T
    small_g = _pack_small(dict(pre_mix_norm=d_pre_mix, sgu_ln_gain=d_ln_gain, sgu_ln_bias=d_ln_bias, sgu_w_spatial=d_w_sp,
                               sgu_b_spatial=d_b_sp, attn_out_norm=d_attn_out, sgu_out_norm=d_sgu_out,
                               post_mix_norm=d_post_mix, pre_ffn_norm=d_pre_ffn, post_ffn_norm=d_post_ffn,
                               loss_sum=sq_err))
    small_g = small_g.reshape(N_DEV, SMALL_ROWS // N_DEV, 128)
    (_, (c_in,)), (_, (o_small,)) = by_sequencer(
        "last_sums_to_owners", [to_chips([p_in]), to_owners(small_g)], TO_ALL)

    big, last = {}, p_in
    for name, w, p, c, m, vv, transposed in (
            ("w_down", w_down, p_down, c_down, m_w_down, v_w_down, False), ("w_gate", w_gate, p_gate, c_gate, m_w_gate, v_w_gate, True),
            ("w_up", w_up, p_up, c_up, m_w_up, v_w_up, True), ("w_out", w_out, p_out, c_out, m_w_out, v_w_out, False),
            ("w_in", w_in, p_in, c_in, m_w_in, v_w_in, True)):
        if name == "w_in":
            last = sum_owned("sum_small", small_g, o_small, dev_ids, after=[last])
            (all_small,) = gathered("gather_small_grads", [last])
        turn = (lambda t: t.T) if transposed else (lambda t: t)
        outs = adamw_of_sums("adamw_" + name, p, c, chip_ids, turn(w[0]), turn(m[0]), turn(vv[0]), last)
        big[name], last = tuple(turn(t)[None] for t in outs), outs[0]
    sd, snm, snv = adamw("adamw_small", _pack_small(small_w), all_small, _pack_small(small_m), _pack_small(small_v))
    sg, sd, snm, snv = (_unpack_small(t, small_w) for t in (all_small, sd, snm, snv))
    loss = sg["loss_sum"][0] * np.float32(0.5 / D_MODEL)

    names = ["pre_mix_norm", "w_in", "sgu_ln_gain", "sgu_ln_bias", "sgu_w_spatial", "sgu_b_spatial", "attn_out_norm",
             "sgu_out_norm", "w_out", "post_mix_norm", "pre_ffn_norm", "w_gate", "w_up", "w_down", "post_ffn_norm"]
    outs = [loss, grad_x[None]]
    for i, table in enumerate((sg, sd, snm, snv)):
        for name in names:
            outs.append(big[name][i] if name in big else table[name])
    return tuple(outs)
```

```python
import numpy as np
import jax
import jax.numpy as jnp
from jax import lax
from jax.experimental import pallas as pl
from jax.experimental.pallas import tpu as pltpu
from jax.experimental.pallas import tpu_sc as plsc

F32 = jnp.float32
BF16 = jnp.bfloat16

SEQ = 2048
D_MODEL = 1024
ATTN_W = 512
SGU_W = 512
HEAD_DIM = 64
N_GROUPS = 8
CHUNK = 128
D_FF = 2816
IN_W = 3 * ATTN_W + 2 * SGU_W
DILATIONS = (1, 4, 16)
ROPE_THETA = 500000.0
ROT_DIM = 16
ROT_HALF = 8
RMS_EPS = 1e-6
LN_EPS = 1e-5
Q_SCALE = 0.125
NEG = -1e30

N_DEV = 8
MESH_AXES = ("x", "y", "c")
MESH = pl.DeviceIdType.MESH

ADAM_LR = 0.001
ADAM_B1 = 0.9
ADAM_B2 = 0.999
ADAM_EPS = 1e-08
ADAM_WD = 0.01
ADAM_STEP = 10

VMEM_LIMIT = 60 * 1024 * 1024
ANY = pl.BlockSpec(memory_space=pl.ANY)

SMALL = (("pre_mix_norm", 1024), ("sgu_ln_gain", 512), ("sgu_ln_bias", 512), ("sgu_w_spatial", 8 * 128 * 128),
         ("sgu_b_spatial", 1024), ("attn_out_norm", 512), ("sgu_out_norm", 512), ("post_mix_norm", 1024),
         ("pre_ffn_norm", 1024), ("post_ffn_norm", 1024), ("loss_sum", 1))
SMALL_ROWS = 1152


def _params(sem=("arbitrary",)):
    return pltpu.CompilerParams(dimension_semantics=sem, vmem_limit_bytes=VMEM_LIMIT)


def _dot(a, b):
    return jnp.dot(a, b, preferred_element_type=F32)


def _dot_nt(a, b):
    return lax.dot_general(a, b, (((1,), (1,)), ((), ())), preferred_element_type=F32)


def _dot_tn(a, b):
    return lax.dot_general(a, b, (((0,), (0,)), ((), ())), preferred_element_type=F32)


def _rms(z):
    return lax.rsqrt(jnp.mean(z * z, axis=-1, keepdims=True) + RMS_EPS)


def _rms_bwd(z, gain, d):
    r = _rms(z)
    n = z * r
    dn = d * gain
    dz = r * (dn - n * jnp.mean(dn * n, axis=-1, keepdims=True))
    return dz, jnp.sum(d * n, axis=0, keepdims=True)


def _gelu(z):
    return 0.5 * z * (1.0 + lax.erf(z * np.float32(1.0 / np.sqrt(2.0))))


def _gelu_grad(z):
    cdf = 0.5 * (1.0 + lax.erf(z * np.float32(1.0 / np.sqrt(2.0))))
    return cdf + z * jnp.exp(-0.5 * z * z) * np.float32(1.0 / np.sqrt(2.0 * np.pi))


def _rot_tables(pos_col, invf, ma, mb):
    ang = pos_col.astype(F32) * invf
    s = jnp.sin(ang)
    return jnp.cos(ang), s * ma, s * mb


def _rot(t, c, sa, sb):
    return t * c + pltpu.roll(t, 120, 1) * sa + pltpu.roll(t, 8, 1) * sb


def _rot_t(d, c, sa, sb):
    return d * c + pltpu.roll(d * sa, 8, 1) + pltpu.roll(d * sb, 120, 1)


def _rot_consts():
    lane = np.arange(128) % HEAD_DIM
    inv_freq = (np.float32(ROPE_THETA) ** (-np.arange(0, ROT_DIM, 2, dtype=np.float32) / np.float32(ROT_DIM))).astype(np.float32)
    invf = np.where(lane < ROT_DIM, inv_freq[lane % ROT_HALF], 0.0).astype(np.float32)
    ma = np.where(lane < ROT_HALF, -1.0, 0.0).astype(np.float32)
    mb = np.where((lane >= ROT_HALF) & (lane < ROT_DIM), 1.0, 0.0).astype(np.float32)
    return jnp.asarray(invf[None]), jnp.asarray(ma[None]), jnp.asarray(mb[None])


def _row_spec(tm, w):
    return pl.BlockSpec((tm, w), lambda i: (i, 0))


def _full_spec(shape):
    return pl.BlockSpec(shape, lambda i: (0,) * len(shape))


def _weight_spec(shape):
    return pl.BlockSpec(shape, lambda i: (0,) * len(shape), pipeline_mode=pl.Buffered(1))


RES = 16


def _residue_scratch(n_arrays, tm, width):
    return [pltpu.VMEM((2, n_arrays, tm // RES, RES, width), F32), pltpu.SemaphoreType.DMA((2, n_arrays, RES))]


def _to_residue_rows(tiles, outs, scratch, sems, tm, n_steps):
    i = pl.program_id(0)
    slot = i % 2
    per = tm // RES

    def copies(step, s):
        return [pltpu.make_async_copy(scratch.at[s, a, :, b, :],
                                      outs[a].at[pl.ds(pl.multiple_of(b * (SEQ // RES) + per * step, per), per), :],
                                      sems.at[s, a, b]) for a in range(len(outs)) for b in range(RES)]

    @pl.when(i >= 2)
    def _():
        for cp in copies(i - 2, slot):
            cp.wait()

    for a, tile in enumerate(tiles):
        scratch[slot, a] = tile.reshape(per, RES, tile.shape[-1])
    for cp in copies(i, slot):
        cp.start()

    @pl.when(i == n_steps - 1)
    def _():
        for cp in copies(i - 1, 1 - slot) + copies(i, slot):
            cp.wait()


def in_proj(x, pos_col, g1, w_in_t, rot):
    tm = 512
    n_steps = SEQ // tm

    def body(x_ref, pos_ref, g_ref, w_ref, invf_ref, ma_ref, mb_ref, h_ref, u_ref, vs_ref, q_ref, k_ref, v_ref, scratch, sems):
        xf = x_ref[...]
        h = (xf * _rms(xf) * g_ref[...]).astype(BF16)
        h_ref[...] = h
        proj = _dot_nt(h, w_ref[...])
        c, sa, sb = _rot_tables(pos_ref[...], invf_ref[...], ma_ref[...], mb_ref[...])
        slabs = range(ATTN_W // 128)
        q = jnp.concatenate([_rot(proj[:, j * 128:(j + 1) * 128], c, sa, sb) * Q_SCALE for j in slabs], axis=1)
        k = jnp.concatenate([_rot(proj[:, ATTN_W + j * 128:ATTN_W + (j + 1) * 128], c, sa, sb) for j in slabs], axis=1)
        u_ref[...] = proj[:, 3 * ATTN_W:3 * ATTN_W + SGU_W]
        vs_ref[...] = proj[:, 3 * ATTN_W + SGU_W:]
        _to_residue_rows([q, k, proj[:, 2 * ATTN_W:3 * ATTN_W]], [q_ref, k_ref, v_ref], scratch, sems, tm, n_steps)

    act = jax.ShapeDtypeStruct((SEQ, 512), F32)
    return _call(
        "in_proj", body, n_steps,
        [_row_spec(tm, D_MODEL), _row_spec(tm, 1), _full_spec((1, D_MODEL)), _weight_spec((IN_W, D_MODEL)),
         _full_spec((1, 128)), _full_spec((1, 128)), _full_spec((1, 128))],
        [_row_spec(tm, D_MODEL)] + [_row_spec(tm, 512)] * 2 + [ANY] * 3,
        [jax.ShapeDtypeStruct((SEQ, D_MODEL), BF16)] + [act] * 5,
        (x, pos_col, g1, w_in_t, *rot), scratch_shapes=_residue_scratch(3, tm, ATTN_W))


def _to_residue_order(t):
    return t.reshape(SEQ // RES, RES, -1).transpose(1, 0, 2).reshape(t.shape)


def _from_residue_order(t):
    return t.reshape(RES, SEQ // RES, -1).transpose(1, 0, 2).reshape(t.shape)


def _block_rows(d, r, n):
    if d == 16:
        slices = [(128 * r, 128)]
    elif d == 4:
        slices = [(128 * (4 * b + r) + 32 * n, 32) for b in range(4)]
    else:
        slices = [(128 * b + 8 * n, 8) for b in range(RES)]
    return [(s if isinstance(s, int) else pl.multiple_of(s, z), z) for s, z in slices]


def _block_step(d, i):
    if d == 16:
        return i
    if d == 4:
        return 4 * (i & 31) + (i >> 5)
    return 16 * (i & 7) + (i >> 3)


def _attn_masks(d):
    row2 = _block_step(d, lax.broadcasted_iota(jnp.int32, (128, 256), 0))
    col2 = lax.broadcasted_iota(jnp.int32, (128, 256), 1)
    key2 = _block_step(d, col2 & 127)
    mask2 = jnp.logical_or(jnp.logical_and(col2 < 128, key2 >= row2), jnp.logical_and(col2 >= 128, key2 <= row2))
    row1 = _block_step(d, lax.broadcasted_iota(jnp.int32, (128, 128), 0))
    col1 = lax.broadcasted_iota(jnp.int32, (128, 128), 1)
    return col1 < HEAD_DIM, _block_step(d, col1) <= row1, mask2


def _load_rows(ref, slices):
    parts = [ref[pl.ds(s, z), :] for s, z in slices]
    return parts[0] if len(parts) == 1 else jnp.concatenate(parts, axis=0)


def _for_each_group(fn):
    for p, d in enumerate(DILATIONS):
        masks = _attn_masks(d)
        if d == 16:
            def group(i, carry, p=p, masks=masks):
                fn(p, masks, [(_block_rows(16, 4 * i + g, 0), None) for g in range(4)])
                return carry

            lax.fori_loop(0, 4, group, 0)
        elif d == 4:
            fn(p, masks, [(_block_rows(4, r, 0), None) for r in range(4)])

            def group(i, carry, p=p, masks=masks):
                fn(p, masks, [(_block_rows(4, r, i + 1), _block_rows(4, r, i)) for r in range(4)])
                return carry

            lax.fori_loop(0, 3, group, 0)
        else:
            fn(p, masks, [(_block_rows(1, 0, 0), None)])

            def group(i, carry, p=p, masks=masks):
                fn(p, masks, [(_block_rows(1, 0, 3 * i + g + 1), _block_rows(1, 0, 3 * i + g)) for g in range(3)])
                return carry

            lax.fori_loop(0, 5, group, 0)


def attn_fwd(q, k, v):
    def body(q_ref, k_ref, v_ref, o_ref, lse_ref, nat_ref, op_ref, lp_ref, sems):
        def group(p, masks, blocks):
            head0, mask1, mask2 = masks
            heads = (head0, jnp.logical_not(head0))
            keys = [rows if prev is None else prev + rows for rows, prev in blocks]
            mask = [mask1 if prev is None else mask2 for _, prev in blocks]
            qb = [_load_rows(q_ref, rows) for rows, _ in blocks]
            kk = [_load_rows(k_ref, ks).astype(BF16) for ks in keys]
            vv = [_load_rows(v_ref, ks).astype(BF16) for ks in keys]
            chains = [(g, hm) for g in range(len(blocks)) for hm in heads]
            s = [jnp.where(mask[g], _dot_nt(jnp.where(hm, qb[g], 0.0).astype(BF16), kk[g]), NEG) for g, hm in chains]
            m = [jnp.max(t, axis=-1, keepdims=True) for t in s]
            e = [jnp.exp(t - mt) for t, mt in zip(s, m)]
            l = [jnp.sum(t, axis=-1, keepdims=True) for t in e]
            pv = [_dot(t.astype(BF16), vv[g]) for t, (g, _) in zip(e, chains)]
            for g, (rows, _) in enumerate(blocks):
                o_blk = jnp.where(head0, pv[2 * g] / l[2 * g], pv[2 * g + 1] / l[2 * g + 1])
                l_blk = jnp.where(head0, jnp.broadcast_to(m[2 * g] + jnp.log(l[2 * g]), (128, 128)),
                                  jnp.broadcast_to(m[2 * g + 1] + jnp.log(l[2 * g + 1]), (128, 128)))
                at = 0
                for start, size in rows:
                    op_ref[p, pl.ds(start, size), :] = o_blk[at:at + size]
                    lp_ref[p, pl.ds(start, size), :] = l_blk[at:at + size]
                    at += size

        _for_each_group(group)

        def combine(i, carry):
            rows = pl.ds(pl.multiple_of(i * 256, 256), 256)
            ls = [lp_ref[p, rows, :] for p in range(3)]
            m = jnp.maximum(jnp.maximum(ls[0], ls[1]), ls[2])
            lse = m + jnp.log(jnp.exp(ls[0] - m) + jnp.exp(ls[1] - m) + jnp.exp(ls[2] - m))
            o = jnp.zeros((256, 128), F32)
            for p in range(3):
                o = o + jnp.exp(ls[p] - lse) * op_ref[p, rows, :]
            o_ref[rows, :] = o
            lse_ref[rows, :] = lse
            return carry

        lax.fori_loop(0, SEQ // 256, combine, 0)

        lanes = pl.ds(pl.multiple_of(pl.program_id(0) * 128, 128), 128)
        back = [pltpu.make_async_copy(o_ref.at[pl.ds(b * (SEQ // RES), SEQ // RES), :], nat_ref.at[:, b, lanes], sems.at[b])
                for b in range(RES)]
        for cp in back:
            cp.start()
        for cp in back:
            cp.wait()

    slab = pl.BlockSpec((SEQ, 128), lambda i: (0, i))
    out = jax.ShapeDtypeStruct((SEQ, ATTN_W), F32)
    attn_r, lse, attn = _call(
        "attn_fwd", body, ATTN_W // 128, [slab] * 3, [slab] * 2 + [ANY],
        [out, out, jax.ShapeDtypeStruct((SEQ // RES, RES, ATTN_W), F32)], (q, k, v),
        scratch_shapes=[pltpu.VMEM((3, SEQ, 128), F32), pltpu.VMEM((3, SEQ, 128), F32), pltpu.SemaphoreType.DMA((RES,))])
    return attn_r, lse, attn.reshape(SEQ, ATTN_W)


def _causal_weights(w_ref):
    row = lax.broadcasted_iota(jnp.int32, (CHUNK, CHUNK), 0)
    col = lax.broadcasted_iota(jnp.int32, (CHUNK, CHUNK), 1)
    return [jnp.where(col <= row, w_ref[g], 0.0).astype(BF16) for g in range(N_GROUPS)], col <= row


def _sgu_chunk_fwd(u, vs, lg, lb, wc, bfull, head0):
    ug = _gelu(u)
    vg = _gelu(vs)
    xc = vg - jnp.mean(vg, axis=-1, keepdims=True)
    rstd = lax.rsqrt(jnp.mean(xc * xc, axis=-1, keepdims=True) + LN_EPS)
    xhat = xc * rstd
    vn = xhat * lg + lb
    mixed = []
    for gp in range(SGU_W // 128):
        vp = vn[:, gp * 128:(gp + 1) * 128].astype(BF16)
        mixed.append(jnp.where(head0, _dot(wc[2 * gp], vp), _dot(wc[2 * gp + 1], vp)))
    ms = jnp.concatenate(mixed, axis=1) + bfull
    return ug, xhat, rstd, vn, ms


def sgu_fwd(u, vs, lg, lb, w_sp, bfull):
    cpb = 4

    def body(u_ref, vs_ref, lg_ref, lb_ref, w_ref, b_ref, o_ref):
        wc, _ = _causal_weights(w_ref)
        head0 = lax.broadcasted_iota(jnp.int32, (CHUNK, 128), 1) < HEAD_DIM
        for ci in range(cpb):
            rows = pl.ds(ci * CHUNK, CHUNK)
            ug, _, _, _, ms = _sgu_chunk_fwd(u_ref[rows, :], vs_ref[rows, :], lg_ref[...], lb_ref[...], wc, b_ref[...], head0)
            o_ref[rows, :] = ug * ms

    tm = cpb * CHUNK
    return _call(
        "sgu_fwd", body, SEQ // tm,
        [_row_spec(tm, SGU_W), _row_spec(tm, SGU_W), _full_spec((1, SGU_W)), _full_spec((1, SGU_W)),
         _full_spec((N_GROUPS, CHUNK, CHUNK)), _full_spec((CHUNK, SGU_W))],
        [_row_spec(tm, SGU_W)], [jax.ShapeDtypeStruct((SEQ, SGU_W), F32)],
        (u, vs, lg, lb, w_sp, bfull))


def out_proj(attn, sgu, x, ga, gs, w_out, gpm, gpf):
    tm = 512

    def body(a_ref, s_ref, x_ref, ga_ref, gs_ref, w_ref, gpm_ref, gpf_ref, mix_ref, y_ref, x2_ref, h2_ref):
        a = a_ref[...]
        s = s_ref[...]
        an = (a * _rms(a) * ga_ref[...]).astype(BF16)
        sn = (s * _rms(s) * gs_ref[...]).astype(BF16)
        mix_ref[:, :ATTN_W] = an
        mix_ref[:, ATTN_W:] = sn
        y = _dot(an, w_ref[:ATTN_W, :]) + _dot(sn, w_ref[ATTN_W:, :])
        y_ref[...] = y
        x2 = x_ref[...] + y * _rms(y) * gpm_ref[...]
        x2_ref[...] = x2
        h2_ref[...] = (x2 * _rms(x2) * gpf_ref[...]).astype(BF16)

    wide = jax.ShapeDtypeStruct((SEQ, D_MODEL), F32)
    wide16 = jax.ShapeDtypeStruct((SEQ, D_MODEL), BF16)
    return _call(
        "out_proj", body, SEQ // tm,
        [_row_spec(tm, ATTN_W), _row_spec(tm, SGU_W), _row_spec(tm, D_MODEL), _full_spec((1, ATTN_W)),
         _full_spec((1, SGU_W)), _weight_spec((D_MODEL, D_MODEL)), _full_spec((1, D_MODEL)), _full_spec((1, D_MODEL))],
        [_row_spec(tm, D_MODEL)] * 4, [wide16, wide, wide, wide16],
        (attn, sgu, x, ga, gs, w_out, gpm, gpf))


def ffn_up(h2, w_gate_t, w_up_t):
    tm = 512

    def body(h_ref, wg_ref, wu_ref, g_ref, u_ref, a_ref):
        h = h_ref[...]
        g = _dot_nt(h, wg_ref[...])
        u = _dot_nt(h, wu_ref[...])
        g_ref[...] = g.astype(BF16)
        u_ref[...] = u.astype(BF16)
        a_ref[...] = (g * jax.nn.sigmoid(g) * u).astype(BF16)

    ff = jax.ShapeDtypeStruct((SEQ, D_FF), BF16)
    return _call(
        "ffn_up", body, SEQ // tm,
        [_row_spec(tm, D_MODEL), _weight_spec((D_FF, D_MODEL)), _weight_spec((D_FF, D_MODEL))],
        [_row_spec(tm, D_FF)] * 3, [ff, ff, jax.ShapeDtypeStruct((SEQ, D_FF), BF16)],
        (h2, w_gate_t, w_up_t))


def ffn_down_loss(act, w_down, x2, gpo, target):
    tm = 512

    def body(a_ref, w_ref, x2_ref, g_ref, t_ref, df_ref, dx3_ref, dg_ref, loss_ref):
        f = _dot(a_ref[...], w_ref[...])
        gain = g_ref[...]
        err = x2_ref[...] + f * _rms(f) * gain - t_ref[...]
        dx3 = err * np.float32(1.0 / D_MODEL)
        dx3_ref[...] = dx3
        df, dg = _rms_bwd(f, gain, dx3)
        df_ref[...] = df.astype(BF16)

        @pl.when(pl.program_id(0) == 0)
        def _():
            dg_ref[...] = jnp.zeros_like(dg_ref)
            loss_ref[...] = jnp.zeros_like(loss_ref)

        dg_ref[...] += dg
        loss_ref[...] += jnp.sum(err * err, axis=(0, 1), keepdims=True)

    return pl.pallas_call(
        body, name="ffn_down_loss", grid=(SEQ // tm,),
        in_specs=[_row_spec(tm, D_FF), _weight_spec((D_FF, D_MODEL)), _row_spec(tm, D_MODEL), _full_spec((1, D_MODEL)),
                  _row_spec(tm, D_MODEL)],
        out_specs=[_row_spec(tm, D_MODEL), _row_spec(tm, D_MODEL), _full_spec((1, D_MODEL)), _full_spec((1, 1))],
        out_shape=[jax.ShapeDtypeStruct((SEQ, D_MODEL), BF16), jax.ShapeDtypeStruct((SEQ, D_MODEL), F32),
                   jax.ShapeDtypeStruct((1, D_MODEL), F32), jax.ShapeDtypeStruct((1, 1), F32)],
        compiler_params=_params(),
    )(act, w_down, x2, gpo, target)


def ffn_act_bwd(df, w_down, gate, up, after=()):
    tm = 512

    def body(df_ref, w_ref, g_ref, u_ref, dg_ref, du_ref):
        dact = _dot_nt(df_ref[...], w_ref[...])
        g = g_ref[...].astype(F32)
        s = jax.nn.sigmoid(g)
        du_ref[...] = (dact * g * s).astype(BF16)
        dg_ref[...] = (dact * u_ref[...].astype(F32) * (s * (1.0 + g * (1.0 - s)))).astype(BF16)

    ff16 = jax.ShapeDtypeStruct((SEQ, D_FF), BF16)
    return _call(
        "ffn_act_bwd", body, SEQ // tm,
        [_row_spec(tm, D_MODEL), _weight_spec((D_FF, D_MODEL)), _row_spec(tm, D_FF), _row_spec(tm, D_FF)],
        [_row_spec(tm, D_FF)] * 2, [ff16, ff16], (df, w_down, gate, up), after=after)


def ffn_in_bwd(dgate, dup, w_gate_t, w_up_t, x2, gpf, dx3, y, gpm, after=()):
    tm = 512

    def body(dg_ref, du_ref, wg_ref, wu_ref, x2_ref, gpf_ref, dx3_ref, y_ref, gpm_ref, dx2_ref, dy_ref, dgpf_ref, dgpm_ref):
        dh2 = _dot(dg_ref[...], wg_ref[...]) + _dot(du_ref[...], wu_ref[...])
        dz, dgpf = _rms_bwd(x2_ref[...], gpf_ref[...], dh2)
        dx2 = dx3_ref[...] + dz
        dx2_ref[...] = dx2
        dy, dgpm = _rms_bwd(y_ref[...], gpm_ref[...], dx2)
        dy_ref[...] = dy.astype(BF16)

        @pl.when(pl.program_id(0) == 0)
        def _():
            dgpf_ref[...] = jnp.zeros_like(dgpf_ref)
            dgpm_ref[...] = jnp.zeros_like(dgpm_ref)

        dgpf_ref[...] += dgpf
        dgpm_ref[...] += dgpm

    vec = jax.ShapeDtypeStruct((1, D_MODEL), F32)
    return _call(
        "ffn_in_bwd", body, SEQ // tm,
        [_row_spec(tm, D_FF), _row_spec(tm, D_FF), _weight_spec((D_FF, D_MODEL)), _weight_spec((D_FF, D_MODEL)),
         _row_spec(tm, D_MODEL), _full_spec((1, D_MODEL)), _row_spec(tm, D_MODEL), _row_spec(tm, D_MODEL),
         _full_spec((1, D_MODEL))],
        [_row_spec(tm, D_MODEL), _row_spec(tm, D_MODEL), _full_spec((1, D_MODEL)), _full_spec((1, D_MODEL))],
        [jax.ShapeDtypeStruct((SEQ, D_MODEL), F32), jax.ShapeDtypeStruct((SEQ, D_MODEL), BF16), vec, vec],
        (dgate, dup, w_gate_t, w_up_t, x2, gpf, dx3, y, gpm), after=after)


def weight_grad(name, a, b, after=()):
    m, n = a.shape[1], b.shape[1]
    tr = 256

    def body(a_ref, b_ref, o_ref):
        o_ref[...] = _dot_tn(a_ref[...], b_ref[...]).astype(BF16)

    (out,) = _call(
        name, body, m // tr, [pl.BlockSpec((SEQ, tr), lambda i: (0, i)), _weight_spec((SEQ, n))],
        [_row_spec(tr, n)], [jax.ShapeDtypeStruct((m, n), BF16)], (a, b), after=after)
    return out.reshape(N_DEV, m // N_DEV, n)


def mix_bwd(dy, w_out, attn, sgu, ga, gs, after=()):
    tm = 512
    n_steps = SEQ // tm

    def body(dy_ref, w_ref, a_ref, s_ref, ga_ref, gs_ref, ds_ref, dga_ref, dgs_ref, da_ref, scratch, sems):
        dy = dy_ref[...]
        da, dga = _rms_bwd(a_ref[...], ga_ref[...], _dot_nt(dy, w_ref[:ATTN_W, :]))
        ds, dgs = _rms_bwd(s_ref[...], gs_ref[...], _dot_nt(dy, w_ref[ATTN_W:, :]))
        ds_ref[...] = ds
        _to_residue_rows([da], [da_ref], scratch, sems, tm, n_steps)

        @pl.when(pl.program_id(0) == 0)
        def _():
            dga_ref[...] = jnp.zeros_like(dga_ref)
            dgs_ref[...] = jnp.zeros_like(dgs_ref)

        dga_ref[...] += dga
        dgs_ref[...] += dgs

    half = jax.ShapeDtypeStruct((SEQ, 512), F32)
    vec = jax.ShapeDtypeStruct((1, 512), F32)
    return _call(
        "mix_bwd", body, n_steps,
        [_row_spec(tm, D_MODEL), _weight_spec((D_MODEL, D_MODEL)), _row_spec(tm, 512), _row_spec(tm, 512),
         _full_spec((1, 512)), _full_spec((1, 512))],
        [_row_spec(tm, 512), _full_spec((1, 512)), _full_spec((1, 512)), ANY],
        [half, vec, vec, half], (dy, w_out, attn, sgu, ga, gs), scratch_shapes=_residue_scratch(1, tm, ATTN_W), after=after)


def sgu_bwd(u, vs, dsgu, lg, lb, w_sp, bfull):
    cpb = 4

    def body(u_ref, vs_ref, d_ref, lg_ref, lb_ref, w_ref, b_ref, du_ref, dvs_ref, dlg_ref, dlb_ref, dw_ref, db_ref):
        wc, causal = _causal_weights(w_ref)
        head0 = lax.broadcasted_iota(jnp.int32, (CHUNK, 128), 1) < HEAD_DIM
        lg = lg_ref[...]

        @pl.when(pl.program_id(0) == 0)
        def _():
            dlg_ref[...] = jnp.zeros_like(dlg_ref)
            dlb_ref[...] = jnp.zeros_like(dlb_ref)
            dw_ref[...] = jnp.zeros_like(dw_ref)
            db_ref[...] = jnp.zeros_like(db_ref)

        for ci in range(cpb):
            rows = pl.ds(ci * CHUNK, CHUNK)
            u = u_ref[rows, :]
            vs = vs_ref[rows, :]
            d = d_ref[rows, :]
            ug, xhat, rstd, vn, ms = _sgu_chunk_fwd(u, vs, lg, lb_ref[...], wc, b_ref[...], head0)
            du_ref[rows, :] = (d * ms * _gelu_grad(u)).astype(BF16)
            dms = d * ug
            db_ref[...] += dms
            dvn = []
            for gp in range(SGU_W // 128):
                dmp = dms[:, gp * 128:(gp + 1) * 128]
                dm0 = jnp.where(head0, dmp, 0.0).astype(BF16)
                dm1 = jnp.where(head0, 0.0, dmp).astype(BF16)
                vp = vn[:, gp * 128:(gp + 1) * 128].astype(BF16)
                dw_ref[2 * gp] += _dot_nt(dm0, vp)
                dw_ref[2 * gp + 1] += _dot_nt(dm1, vp)
                dvn.append(_dot_tn(wc[2 * gp], dm0) + _dot_tn(wc[2 * gp + 1], dm1))
            dvn = jnp.concatenate(dvn, axis=1)
            dlg_ref[...] += jnp.sum(dvn * xhat, axis=0, keepdims=True)
            dlb_ref[...] += jnp.sum(dvn, axis=0, keepdims=True)
            dxh = dvn * lg
            dvg = rstd * (dxh - jnp.mean(dxh, axis=-1, keepdims=True) - xhat * jnp.mean(dxh * xhat, axis=-1, keepdims=True))
            dvs_ref[rows, :] = (dvg * _gelu_grad(vs)).astype(BF16)

        @pl.when(pl.program_id(0) == pl.num_programs(0) - 1)
        def _():
            for g in range(N_GROUPS):
                dw_ref[g] = jnp.where(causal, dw_ref[g], 0.0)

    tm = cpb * CHUNK
    half16 = jax.ShapeDtypeStruct((SEQ, SGU_W), BF16)
    vec = jax.ShapeDtypeStruct((1, SGU_W), F32)
    return _call(
        "sgu_bwd", body, SEQ // tm,
        [_row_spec(tm, SGU_W)] * 3 + [_full_spec((1, SGU_W)), _full_spec((1, SGU_W)),
                                      _full_spec((N_GROUPS, CHUNK, CHUNK)), _full_spec((CHUNK, SGU_W))],
        [_row_spec(tm, SGU_W), _row_spec(tm, SGU_W), _full_spec((1, SGU_W)), _full_spec((1, SGU_W)),
         _full_spec((N_GROUPS, CHUNK, CHUNK)), _full_spec((CHUNK, SGU_W))],
        [half16, half16, vec, vec, jax.ShapeDtypeStruct((N_GROUPS, CHUNK, CHUNK), F32),
         jax.ShapeDtypeStruct((CHUNK, SGU_W), F32)],
        (u, vs, dsgu, lg, lb, w_sp, bfull))


def attn_bwd(q, k, v, o, lse, do, pos_col, rot):
    def body(q_ref, k_ref, v_ref, o_ref, lse_ref, do_ref, pos_ref, invf_ref, ma_ref, mb_ref,
             dq_ref, dk_ref, dv_ref, dqa_ref, dka_ref, dva_ref, dlt_ref, rot_ref):
        dqa_ref[...] = jnp.zeros_like(dqa_ref)
        dka_ref[...] = jnp.zeros_like(dka_ref)
        dva_ref[...] = jnp.zeros_like(dva_ref)

        def delta(i, carry):
            rows = pl.ds(pl.multiple_of(i * 256, 256), 256)
            prod = do_ref[rows, :] * o_ref[rows, :]
            h0 = lax.broadcasted_iota(jnp.int32, (256, 128), 1) < HEAD_DIM
            d0 = jnp.sum(jnp.where(h0, prod, 0.0), axis=-1, keepdims=True)
            d1 = jnp.sum(jnp.where(h0, 0.0, prod), axis=-1, keepdims=True)
            dlt_ref[rows, :] = jnp.where(h0, d0, d1)
            return carry

        lax.fori_loop(0, SEQ // 256, delta, 0)

        def add_rows(ref, slices, val):
            at = 0
            for start, size in slices:
                ref[pl.ds(start, size), :] += val[at:at + size]
                at += size

        def group(p, masks, blocks):
            head0, mask1, mask2 = masks
            heads = (head0, jnp.logical_not(head0))
            keys = [rows if prev is None else prev + rows for rows, prev in blocks]
            mask = [mask1 if prev is None else mask2 for _, prev in blocks]
            kk = [_load_rows(k_ref, ks).astype(BF16) for ks in keys]
            vv = [_load_rows(v_ref, ks).astype(BF16) for ks in keys]
            qb = [_load_rows(q_ref, rows) for rows, _ in blocks]
            dob = [_load_rows(do_ref, rows) for rows, _ in blocks]
            lse_b = [_load_rows(lse_ref, rows) for rows, _ in blocks]
            dlt_b = [_load_rows(dlt_ref, rows) for rows, _ in blocks]
            chains = [(g, h) for g in range(len(blocks)) for h in range(2)]
            qm = [jnp.where(heads[h], qb[g], 0.0).astype(BF16) for g, h in chains]
            dom = [jnp.where(heads[h], dob[g], 0.0).astype(BF16) for g, h in chains]
            s = [_dot_nt(qm[c], kk[g]) for c, (g, h) in enumerate(chains)]
            dp = [_dot_nt(dom[c], vv[g]) for c, (g, h) in enumerate(chains)]
            pr = [jnp.where(mask[g], jnp.exp(s[c] - lse_b[g][:, h * HEAD_DIM:h * HEAD_DIM + 1]), 0.0)
                  for c, (g, h) in enumerate(chains)]
            ds = [(pr[c] * (dp[c] - dlt_b[g][:, h * HEAD_DIM:h * HEAD_DIM + 1])).astype(BF16)
                  for c, (g, h) in enumerate(chains)]
            dv = [_dot_tn(pr[c].astype(BF16), dom[c]) for c in range(len(chains))]
            dk = [_dot_tn(ds[c], qm[c]) for c in range(len(chains))]
            dq = [_dot(ds[c], kk[g]) for c, (g, h) in enumerate(chains)]
            for g, (rows, _) in enumerate(blocks):
                add_rows(dqa_ref, rows, jnp.where(head0, dq[2 * g], dq[2 * g + 1]))
                add_rows(dka_ref, keys[g], dk[2 * g] + dk[2 * g + 1])
                add_rows(dva_ref, keys[g], dv[2 * g] + dv[2 * g + 1])

        _for_each_group(group)

        @pl.when(pl.program_id(0) == 0)
        def _():
            def tables(i, carry):
                rows = pl.ds(pl.multiple_of(i * 256, 256), 256)
                c, sa, sb = _rot_tables(pos_ref[rows, :], invf_ref[...], ma_ref[...], mb_ref[...])
                rot_ref[0, rows, :] = c
                rot_ref[1, rows, :] = sa
                rot_ref[2, rows, :] = sb
                return carry

            lax.fori_loop(0, SEQ // 256, tables, 0)

        def finish(i, carry):
            rows = pl.ds(pl.multiple_of(i * 256, 256), 256)
            c, sa, sb = rot_ref[0, rows, :], rot_ref[1, rows, :], rot_ref[2, rows, :]
            dq_ref[rows, :] = _rot_t(dqa_ref[rows, :] * Q_SCALE, c, sa, sb).astype(BF16)
            dk_ref[rows, :] = _rot_t(dka_ref[rows, :], c, sa, sb).astype(BF16)
            dv_ref[rows, :] = dva_ref[rows, :].astype(BF16)
            return carry

        lax.fori_loop(0, SEQ // 256, finish, 0)

    slab = pl.BlockSpec((SEQ, 128), lambda i: (0, i))
    out = jax.ShapeDtypeStruct((SEQ, ATTN_W), BF16)
    acc = pltpu.VMEM((SEQ, 128), F32)
    return _call(
        "attn_bwd", body, ATTN_W // 128,
        [slab] * 6 + [_full_spec((SEQ, 1)), _full_spec((1, 128)), _full_spec((1, 128)), _full_spec((1, 128))],
        [slab] * 3, [out, out, out], (q, k, v, o, lse, do, pos_col, *rot),
        scratch_shapes=[acc, acc, acc, acc, pltpu.VMEM((3, SEQ, 128), F32)])


def in_bwd(dproj, w_in_t, x, g1, dx2, after=()):
    tm = 512

    def body(dp_ref, w_ref, x_ref, g_ref, dx2_ref, dx_ref, dg_ref):
        dh1 = _dot(dp_ref[...], w_ref[...])
        dz, dg = _rms_bwd(x_ref[...], g_ref[...], dh1)
        dx_ref[...] = dx2_ref[...] + dz

        @pl.when(pl.program_id(0) == 0)
        def _():
            dg_ref[...] = jnp.zeros_like(dg_ref)

        dg_ref[...] += dg

    return _call(
        "in_bwd", body, SEQ // tm,
        [_row_spec(tm, IN_W), _weight_spec((IN_W, D_MODEL)), _row_spec(tm, D_MODEL), _full_spec((1, D_MODEL)),
         _row_spec(tm, D_MODEL)],
        [_row_spec(tm, D_MODEL), _full_spec((1, D_MODEL))],
        [jax.ShapeDtypeStruct((SEQ, D_MODEL), F32), jax.ShapeDtypeStruct((1, D_MODEL), F32)],
        (dproj, w_in_t, x, g1, dx2), after=after)


def _coords():
    return lax.axis_index("x"), lax.axis_index("y"), lax.axis_index("c")


class Exchange:
    def __init__(self, srcs, bufs, new_shapes, n_sems, make, has_middle=False):
        self.srcs, self.bufs, self.new_shapes, self.n_sems, self.make = list(srcs), list(bufs), list(new_shapes), n_sems, make
        self.has_middle = has_middle


def _call(name, body, n_steps, in_specs, out_specs, out_shape, args, scratch_shapes=(), after=()):
    n_in = len(args)

    def wrapped(*refs):
        body(*refs[:n_in], *refs[n_in + len(after):])

    return list(pl.pallas_call(
        wrapped, name=name, grid=(n_steps,), in_specs=list(in_specs) + [ANY] * len(after), out_specs=list(out_specs),
        out_shape=list(out_shape), scratch_shapes=list(scratch_shapes), compiler_params=_params(),
    )(*args, *after))


def _gather_copies(kinds, bufs, ranges, send_sems, recv_sems):
    x, y, c = _coords()
    me, sibling = (x, y, c), (x, y, 1 - c)
    chips = [(1 - x, y), (x, 1 - y), (1 - x, 1 - y)]

    def copy(a, k, block, to):
        lo, hi = ranges[a]
        r = bufs[a].shape[0] // N_DEV
        rows = bufs[a].at[pl.ds((4 * block[0] + 2 * block[1] + block[2]) * r + lo, hi - lo), :]
        return pltpu.make_async_remote_copy(src_ref=rows, dst_ref=rows, send_sem=send_sems.at[7 * a + k],
                                            recv_sem=recv_sems.at[7 * a + k], device_id=to, device_id_type=MESH)

    every = range(len(bufs))
    make = {
        "out": lambda: [copy(a, 0, me, sibling) for a in every]
        + [copy(a, 1 + j, me, (*chip, c)) for a in every for j, chip in enumerate(chips)],
        "from_core": lambda: [copy(a, 0, sibling, me) for a in every],
        "from_chips": lambda: [copy(a, 1 + j, (*chip, c), me) for a in every for j, chip in enumerate(chips)],
        "on": lambda: [copy(a, 4 + j, (*chip, c), sibling) for a in every for j, chip in enumerate(chips)],
        "on_in": lambda: [copy(a, 4 + j, (*chip, 1 - c), me) for a in every for j, chip in enumerate(chips)],
    }
    return [make[kind]() for kind in kinds]


def gather(bufs):
    ranges = [(0, b.shape[0] // N_DEV) for b in bufs]

    def make(phase, src_refs, buf_refs, new_refs, send_sems, recv_sems):
        kinds = {"start": ["out"], "middle": ["from_chips", "on"], "end": ["out", "on", "from_core", "on_in"]}[phase]
        made = _gather_copies(kinds, buf_refs, ranges, send_sems, recv_sems)
        if phase == "start":
            return made[0]
        if phase == "middle":
            return made[0], made[1]
        return made[0] + made[1], made[2] + made[3]

    return Exchange([], bufs, [], 7 * len(bufs), make, has_middle=True)


TO_GATHER = (1, lambda x, y, c: [(x, y, 1 - c), (1 - x, y, c), (x, 1 - y, c), (1 - x, 1 - y, c)])
TO_SIBLING = (2, lambda x, y, c: [(x, y, 1 - c)])
TO_CHIPS = (3, lambda x, y, c: [(1 - x, y, c), (x, 1 - y, c), (1 - x, 1 - y, c)])
TO_ALL = (4, lambda x, y, c: [(x ^ (m >> 2), y ^ ((m >> 1) & 1), c ^ (m & 1)) for m in range(1, N_DEV)])


def by_sequencer(name, exchanges, who):
    collective_id, peers_of = who
    hbm = pltpu.MemorySpace.HBM
    refs = [([jax.new_ref(a, memory_space=hbm) for a in ex.srcs], [jax.new_ref(a, memory_space=hbm) for a in ex.bufs],
             [jax.empty_ref(s, memory_space=hbm) for s in ex.new_shapes]) for ex in exchanges]
    sems = []
    for ex in exchanges:
        sems += [pltpu.SemaphoreType.DMA((ex.n_sems,)), pltpu.SemaphoreType.DMA((ex.n_sems,))]

    @pl.kernel(mesh=plsc.ScalarSubcoreMesh(axis_name="sequencer", num_cores=1), name=name, scratch_types=tuple(sems),
               compiler_params=pltpu.CompilerParams(collective_id=collective_id))
    def launch(*sem_refs):
        peers = peers_of(*_coords())
        barrier = pltpu.get_barrier_semaphore()
        for peer in peers:
            pl.semaphore_signal(barrier, inc=1, device_id=peer, device_id_type=MESH)
        pl.semaphore_wait(barrier, len(peers))

        def make(phase, k):
            return exchanges[k].make(phase, *refs[k], sem_refs[2 * k], sem_refs[2 * k + 1])

        for k in range(len(exchanges)):
            for cp in make("start", k):
                cp.start()
        for k, ex in enumerate(exchanges):
            if ex.has_middle:
                arrivals, starts = make("middle", k)
                for cp in arrivals:
                    cp.wait_recv()
                for cp in starts:
                    cp.start()
        for k in range(len(exchanges)):
            sends, arrivals = make("end", k)
            for cp in arrivals:
                cp.wait_recv()
            for cp in sends:
                cp.wait_send()

    launch()
    return [([ref[...] for ref in bufs], [ref[...] for ref in news]) for _, bufs, news in refs]


def place_shards(name, shards, dev):
    n = len(shards)

    def body(dev_ref, *refs):
        for a in range(n):
            refs[n + a][...] = refs[a][...].astype(BF16)

    spec = pltpu.PrefetchScalarGridSpec(
        num_scalar_prefetch=1, grid=(1,),
        in_specs=[pl.BlockSpec(s.shape, lambda i, dev_ref: (0, 0)) for s in shards],
        out_specs=[pl.BlockSpec(s.shape, lambda i, dev_ref: (dev_ref[0], 0)) for s in shards])
    return pl.pallas_call(
        body, name=name, grid_spec=spec,
        out_shape=[jax.ShapeDtypeStruct((N_DEV * s.shape[0], s.shape[1]), BF16) for s in shards],
        compiler_params=_params(),
    )(dev, *shards)


def _swap(copies_of):
    def make(phase, src_refs, buf_refs, new_refs, send_sems, recv_sems):
        copies = copies_of(src_refs, new_refs, send_sems, recv_sems)
        return copies if phase == "start" else (copies, copies)

    return make


def to_sibling(grads):
    def copies_of(src_refs, new_refs, send_sems, recv_sems):
        x, y, c = _coords()
        return [pltpu.make_async_remote_copy(
            src_ref=src_refs[a].at[2 * xy + 1 - c], dst_ref=new_refs[a].at[xy], send_sem=send_sems.at[4 * a + xy],
            recv_sem=recv_sems.at[4 * a + xy], device_id=(x, y, 1 - c), device_id_type=MESH)
            for a in range(len(src_refs)) for xy in range(4)]

    return Exchange(grads, [], [jax.ShapeDtypeStruct((4,) + g.shape[1:], g.dtype) for g in grads], 4 * len(grads),
                    _swap(copies_of))


def to_chips(parts):
    def copies_of(src_refs, new_refs, send_sems, recv_sems):
        x, y, c = _coords()
        chips = [(1 - x, y), (x, 1 - y), (1 - x, 1 - y)]
        return [pltpu.make_async_remote_copy(
            src_ref=src_refs[a].at[2 * px + py], dst_ref=new_refs[a].at[2 * x + y], send_sem=send_sems.at[3 * a + j],
            recv_sem=recv_sems.at[3 * a + j], device_id=(px, py, c), device_id_type=MESH)
            for a in range(len(src_refs)) for j, (px, py) in enumerate(chips)]

    return Exchange(parts, [], [jax.ShapeDtypeStruct(p.shape, p.dtype) for p in parts], 3 * len(parts), _swap(copies_of))


def to_owners(grad):
    def copies_of(src_refs, new_refs, send_sems, recv_sems):
        x, y, c = _coords()
        copies = []
        for m in range(1, N_DEV):
            px, py, pc = x ^ (m >> 2), y ^ ((m >> 1) & 1), c ^ (m & 1)
            copies.append(pltpu.make_async_remote_copy(
                src_ref=src_refs[0].at[4 * px + 2 * py + pc], dst_ref=new_refs[0].at[4 * x + 2 * y + c],
                send_sem=send_sems.at[m - 1], recv_sem=recv_sems.at[m - 1], device_id=(px, py, pc), device_id_type=MESH))
        return copies

    return Exchange([grad], [], [jax.ShapeDtypeStruct(grad.shape, grad.dtype)], N_DEV - 1, _swap(copies_of))


def sum_cores(name, grad, other, core, after=()):
    _, r, w = other.shape

    def body(core_ref, g_ref, o_ref, *rest):
        rest[-1][...] = (g_ref[...].astype(F32) + o_ref[...].astype(F32)).astype(rest[-1].dtype)

    return pl.pallas_call(
        body, name=name,
        grid_spec=pltpu.PrefetchScalarGridSpec(
            num_scalar_prefetch=1, grid=(4,),
            in_specs=[pl.BlockSpec((1, r, w), lambda i, core_ref: (2 * i + core_ref[0], 0, 0)),
                      pl.BlockSpec((1, r, w), lambda i, core_ref: (i, 0, 0))] + [ANY] * len(after),
            out_specs=pl.BlockSpec((1, r, w), lambda i, core_ref: (i, 0, 0))),
        out_shape=jax.ShapeDtypeStruct(other.shape, other.dtype),
        compiler_params=_params(),
    )(core, grad, other, *after)


def sum_owned(name, grad, others, dev_ids, after=()):
    _, r, w = grad.shape

    def body(ids_ref, *refs):
        acc = refs[0][0]
        for k in range(1, N_DEV):
            acc = acc + refs[k][0]
        refs[-1][...] = acc

    def pick(k):
        return pl.BlockSpec((1, r, w), lambda i, ids_ref: (ids_ref[k], 0, 0))

    return pl.pallas_call(
        body, name=name,
        grid_spec=pltpu.PrefetchScalarGridSpec(
            num_scalar_prefetch=1, grid=(1,), in_specs=[pick(k) for k in range(N_DEV)] + [ANY] * len(after),
            out_specs=pl.BlockSpec((r, w), lambda i, ids_ref: (ids_ref[0], 0))),
        out_shape=jax.ShapeDtypeStruct((N_DEV * r, w), F32),
        compiler_params=_params(),
    )(dev_ids, grad, *([others] * (N_DEV - 1)), *after)


def _adamw_update(w, g, m, v):
    nm = ADAM_B1 * m + np.float32(1.0 - ADAM_B1) * g
    nv = ADAM_B2 * v + np.float32(1.0 - ADAM_B2) * (g * g)
    m_hat = nm / np.float32(1.0 - ADAM_B1 ** ADAM_STEP)
    v_hat = nv / np.float32(1.0 - ADAM_B2 ** ADAM_STEP)
    return -ADAM_LR * (m_hat / (jnp.sqrt(v_hat) + ADAM_EPS) + ADAM_WD * w), nm, nv


def adamw_of_sums(name, part, others, chip_ids, w, m, v, after):
    _, r, wd = part.shape

    def body(ids_ref, p_ref, a_ref, b_ref, c_ref, w_ref, m_ref, v_ref, after_ref, g_ref, d_ref, nm_ref, nv_ref):
        g = ((p_ref[0].astype(F32) + a_ref[0].astype(F32)) + b_ref[0].astype(F32)) + c_ref[0].astype(F32)
        g_ref[...] = g
        d_ref[...], nm_ref[...], nv_ref[...] = _adamw_update(w_ref[...], g, m_ref[...], v_ref[...])

    def pick(k):
        return pl.BlockSpec((1, r, wd), lambda i, ids_ref: (ids_ref[k], 0, 0))

    whole = pl.BlockSpec((r, wd), lambda i, ids_ref: (0, 0))
    shape = jax.ShapeDtypeStruct((r, wd), F32)
    return pl.pallas_call(
        body, name=name,
        grid_spec=pltpu.PrefetchScalarGridSpec(
            num_scalar_prefetch=1, grid=(1,), in_specs=[pick(0), pick(1), pick(2), pick(3), whole, whole, whole, ANY],
            out_specs=[whole] * 4),
        out_shape=[shape] * 4,
        compiler_params=_params(),
    )(chip_ids, part, others, others, others, w, m, v, after)


def adamw(name, w, g, m, v):
    def body(w_ref, g_ref, m_ref, v_ref, d_ref, nm_ref, nv_ref):
        d_ref[...], nm_ref[...], nv_ref[...] = _adamw_update(w_ref[...], g_ref[...], m_ref[...], v_ref[...])

    shape = jax.ShapeDtypeStruct(w.shape, F32)
    spec = _full_spec(w.shape)
    return pl.pallas_call(
        body, name=name, grid=(1,), in_specs=[spec] * 4, out_specs=[spec] * 3, out_shape=[shape] * 3,
        compiler_params=_params(),
    )(w, g, m, v)


def _pack_small(parts):
    flat = jnp.concatenate([parts[name].reshape(-1) for name, _ in SMALL])
    return jnp.pad(flat, (0, SMALL_ROWS * 128 - flat.shape[0])).reshape(SMALL_ROWS, 128)


def _unpack_small(packed, like):
    flat = packed.reshape(-1)
    out, at = {}, 0
    for name, size in SMALL:
        out[name] = flat[at:at + size].reshape(like[name].shape)
        at += size
    return out


def kernel(x, positions, pre_mix_norm, w_in, sgu_ln_gain, sgu_ln_bias, sgu_w_spatial, sgu_b_spatial, attn_out_norm, sgu_out_norm, w_out, post_mix_norm, pre_ffn_norm, w_gate, w_up, w_down, post_ffn_norm, loss_target, m_pre_mix_norm, m_w_in, m_sgu_ln_gain, m_sgu_ln_bias, m_sgu_w_spatial, m_sgu_b_spatial, m_attn_out_norm, m_sgu_out_norm, m_w_out, m_post_mix_norm, m_pre_ffn_norm, m_w_gate, m_w_up, m_w_down, m_post_ffn_norm, v_pre_mix_norm, v_w_in, v_sgu_ln_gain, v_sgu_ln_bias, v_sgu_w_spatial, v_sgu_b_spatial, v_attn_out_norm, v_sgu_out_norm, v_w_out, v_post_mix_norm, v_pre_ffn_norm, v_w_gate, v_w_up, v_w_down, v_post_ffn_norm):
    small_w = dict(pre_mix_norm=pre_mix_norm, sgu_ln_gain=sgu_ln_gain, sgu_ln_bias=sgu_ln_bias, sgu_w_spatial=sgu_w_spatial,
                   sgu_b_spatial=sgu_b_spatial, attn_out_norm=attn_out_norm, sgu_out_norm=sgu_out_norm,
                   post_mix_norm=post_mix_norm, pre_ffn_norm=pre_ffn_norm, post_ffn_norm=post_ffn_norm)
    small_m = dict(pre_mix_norm=m_pre_mix_norm, sgu_ln_gain=m_sgu_ln_gain, sgu_ln_bias=m_sgu_ln_bias, sgu_w_spatial=m_sgu_w_spatial,
                   sgu_b_spatial=m_sgu_b_spatial, attn_out_norm=m_attn_out_norm, sgu_out_norm=m_sgu_out_norm,
                   post_mix_norm=m_post_mix_norm, pre_ffn_norm=m_pre_ffn_norm, post_ffn_norm=m_post_ffn_norm)
    small_v = dict(pre_mix_norm=v_pre_mix_norm, sgu_ln_gain=v_sgu_ln_gain, sgu_ln_bias=v_sgu_ln_bias, sgu_w_spatial=v_sgu_w_spatial,
                   sgu_b_spatial=v_sgu_b_spatial, attn_out_norm=v_attn_out_norm, sgu_out_norm=v_sgu_out_norm,
                   post_mix_norm=v_post_mix_norm, pre_ffn_norm=v_pre_ffn_norm, post_ffn_norm=v_post_ffn_norm)
    for table in (small_w, small_m, small_v):
        table["loss_sum"] = jnp.zeros((1,), F32)

    x2d = x[0]
    target = loss_target[0]
    pos_col = positions.reshape(SEQ, 1)
    rot = _rot_consts()
    w_sp = sgu_w_spatial[0]
    bfull = jnp.repeat(sgu_b_spatial[0].T, HEAD_DIM, axis=1)

    x_i, y_i, c_i = (lax.axis_index(a).astype(jnp.int32) for a in MESH_AXES)
    dev = 4 * x_i + 2 * y_i + c_i
    core = c_i.reshape(1)
    chip = 2 * x_i + y_i
    chip_ids = jnp.stack([chip, chip ^ 1, chip ^ 2, chip ^ 3])
    dev_ids = jnp.stack([dev ^ m for m in range(N_DEV)])

    def gathered(name, bufs):
        return by_sequencer(name, [gather(bufs)], TO_GATHER)[0][0]

    def from_sibling(name, grads):
        return by_sequencer(name, [to_sibling(grads)], TO_SIBLING)[0][1]

    def from_chips(name, parts):
        return by_sequencer(name, [to_chips(parts)], TO_CHIPS)[0][1]

    (w_in_t,) = place_shards("place_w_in", [w_in[0].T], dev.reshape(1))
    (w_in_t,) = gathered("gather_w_in", [w_in_t])
    w_gate_t, w_up_t, w_out_f, w_down_f = place_shards(
        "place_weights", [w_gate[0].T, w_up[0].T, w_out[0], w_down[0]], dev.reshape(1))
    (w_out_f,) = gathered("gather_w_out", [w_out_f])
    w_gate_t, w_up_t = gathered("gather_w_gate_up", [w_gate_t, w_up_t])
    (w_down_f,) = gathered("gather_w_down", [w_down_f])

    h1, u, vs, q, k, v = in_proj(x2d, pos_col, pre_mix_norm, w_in_t, rot)
    attn_r, lse, attn = attn_fwd(q, k, v)
    (sgu,) = sgu_fwd(u, vs, sgu_ln_gain, sgu_ln_bias, w_sp, bfull)
    mix, y, x2, h2 = out_proj(attn, sgu, x2d, attn_out_norm, sgu_out_norm, w_out_f, post_mix_norm, pre_ffn_norm)
    gate, up, act = ffn_up(h2, w_gate_t, w_up_t)
    df, dx3, d_post_ffn, sq_err = ffn_down_loss(act, w_down_f, x2, post_ffn_norm, target)

    g_w_down = weight_grad("grad_w_down", act, df)
    (s_down,) = from_sibling("w_down_to_sibling", [g_w_down])
    dgate, dup = ffn_act_bwd(df, w_down_f, gate, up, after=[g_w_down])
    p_down = sum_cores("sum_cores_down", g_w_down, s_down, core, after=[dgate])
    (c_down,) = from_chips("w_down_to_chips", [p_down])
    g_w_gate = weight_grad("grad_w_gate", dgate, h2, after=[p_down])
    g_w_up = weight_grad("grad_w_up", dup, h2, after=[g_w_gate])
    s_gate, s_up = from_sibling("w_gate_up_to_sibling", [g_w_gate, g_w_up])
    dx2, dy, d_pre_ffn, d_post_mix = ffn_in_bwd(
        dgate, dup, w_gate_t, w_up_t, x2, pre_ffn_norm, dx3, y, post_mix_norm, after=[g_w_gate, g_w_up, c_down])
    p_gate = sum_cores("sum_cores_gate", g_w_gate, s_gate, core, after=[dy])
    p_up = sum_cores("sum_cores_up", g_w_up, s_up, core, after=[dy])
    c_gate, c_up = from_chips("w_gate_up_to_chips", [p_gate, p_up])
    g_w_out = weight_grad("grad_w_out", mix, dy, after=[p_gate, p_up])
    (s_out,) = from_sibling("w_out_to_sibling", [g_w_out])
    dsgu, d_attn_out, d_sgu_out, dattn_r = mix_bwd(dy, w_out_f, attn, sgu, attn_out_norm, sgu_out_norm, after=[g_w_out])
    du, dvs, d_ln_gain, d_ln_bias, d_w_sp, d_bfull = sgu_bwd(u, vs, dsgu, sgu_ln_gain, sgu_ln_bias, w_sp, bfull)
    dq, dk, dv = attn_bwd(q, k, v, attn_r, lse, dattn_r, _to_residue_order(pos_col), rot)
    p_out = sum_cores("sum_cores_out", g_w_out, s_out, core, after=[dq])
    (c_out,) = from_chips("w_out_to_chips", [p_out])
    dq, dk, dv = (_from_residue_order(t) for t in (dq, dk, dv))
    dproj = jnp.concatenate([dq, dk, dv, du, dvs], axis=1)
    g_w_in = weight_grad("grad_w_in", dproj, h1, after=[c_gate, c_up])
    (s_in,) = from_sibling("w_in_to_sibling", [g_w_in])
    grad_x, d_pre_mix = in_bwd(dproj, w_in_t, x2d, pre_mix_norm, dx2, after=[g_w_in, c_out])
    p_in = sum_cores("sum_cores_in", g_w_in, s_in, core, after=[d_pre_mix])

    d_b_sp = d_bfull.reshape(CHUNK, N_GROUPS, HEAD_DIM).sum(axis=-1).T
    small_g = _pack_small(dict(pre_mix_norm=d_pre_mix, sgu_ln_gain=d_ln_gain, sgu_ln_bias=d_ln_bias, sgu_w_spatial=d_w_sp,
                               sgu_b_spatial=d_b_sp, attn_out_norm=d_attn_out, sgu_out_norm=d_sgu_out,
                               post_mix_norm=d_post_mix, pre_ffn_norm=d_pre_ffn, post_ffn_norm=d_post_ffn,
                               loss_sum=sq_err))
    small_g = small_g.reshape(N_DEV, SMALL_ROWS // N_DEV, 128)
    (_, (c_in,)), (_, (o_small,)) = by_sequencer(
        "last_sums_to_owners", [to_chips([p_in]), to_owners(small_g)], TO_ALL)

    big, last = {}, p_in
    for name, w, p, c, m, vv, transposed in (
            ("w_down", w_down, p_down, c_down, m_w_down, v_w_down, False), ("w_gate", w_gate, p_gate, c_gate, m_w_gate, v_w_gate, True),
            ("w_up", w_up, p_up, c_up, m_w_up, v_w_up, True), ("w_out", w_out, p_out, c_out, m_w_out, v_w_out, False),
            ("w_in", w_in, p_in, c_in, m_w_in, v_w_in, True)):
        if name == "w_in":
            last = sum_owned("sum_small", small_g, o_small, dev_ids, after=[last])
            (all_small,) = gathered("gather_small_grads", [last])
        turn = (lambda t: t.T) if transposed else (lambda t: t)
        outs = adamw_of_sums("adamw_" + name, p, c, chip_ids, turn(w[0]), turn(m[0]), turn(vv[0]), last)
        big[name], last = tuple(turn(t)[None] for t in outs), outs[0]
    sd, snm, snv = adamw("adamw_small", _pack_small(small_w), all_small, _pack_small(small_m), _pack_small(small_v))
    sg, sd, snm, snv = (_unpack_small(t, small_w) for t in (all_small, sd, snm, snv))
    loss = sg["loss_sum"][0] * np.float32(0.5 / D_MODEL)

    names = ["pre_mix_norm", "w_in", "sgu_ln_gain", "sgu_ln_bias", "sgu_w_spatial", "sgu_b_spatial", "attn_out_norm",
             "sgu_out_norm", "w_out", "post_mix_norm", "pre_ffn_norm", "w_gate", "w_up", "w_down", "post_ffn_norm"]
    outs = [loss, grad_x[None]]
    for i, table in enumerate((sg, sd, snm, snv)):
        for name in names:
            outs.append(big[name][i] if name in big else table[name])
    return tuple(outs)
```

```python
import numpy as np
import jax
import jax.numpy as jnp
from jax import lax
from jax.experimental import pallas as pl
from jax.experimental.pallas import tpu as pltpu
from jax.experimental.pallas import tpu_sc as plsc

F32 = jnp.float32
BF16 = jnp.bfloat16

SEQ = 2048
D_MODEL = 1024
ATTN_W = 512
SGU_W = 512
HEAD_DIM = 64
N_GROUPS = 8
CHUNK = 128
D_FF = 2816
IN_W = 3 * ATTN_W + 2 * SGU_W
DILATIONS = (1, 4, 16)
ROPE_THETA = 500000.0
ROT_DIM = 16
ROT_HALF = 8
RMS_EPS = 1e-6
LN_EPS = 1e-5
Q_SCALE = 0.125
NEG = -1e30

N_DEV = 8
MESH_AXES = ("x", "y", "c")
MESH = pl.DeviceIdType.MESH

ADAM_LR = 0.001
ADAM_B1 = 0.9
ADAM_B2 = 0.999
ADAM_EPS = 1e-08
ADAM_WD = 0.01
ADAM_STEP = 10

VMEM_LIMIT = 60 * 1024 * 1024
ANY = pl.BlockSpec(memory_space=pl.ANY)

SMALL = (("pre_mix_norm", 1024), ("sgu_ln_gain", 512), ("sgu_ln_bias", 512), ("sgu_w_spatial", 8 * 128 * 128),
         ("sgu_b_spatial", 1024), ("attn_out_norm", 512), ("sgu_out_norm", 512), ("post_mix_norm", 1024),
         ("pre_ffn_norm", 1024), ("post_ffn_norm", 1024), ("loss_sum", 1))
SMALL_ROWS = 1152


def _params(sem=("arbitrary",)):
    return pltpu.CompilerParams(dimension_semantics=sem, vmem_limit_bytes=VMEM_LIMIT)


def _dot(a, b):
    return jnp.dot(a, b, preferred_element_type=F32)


def _dot_nt(a, b):
    return lax.dot_general(a, b, (((1,), (1,)), ((), ())), preferred_element_type=F32)


def _dot_tn(a, b):
    return lax.dot_general(a, b, (((0,), (0,)), ((), ())), preferred_element_type=F32)


def _rms(z):
    return lax.rsqrt(jnp.mean(z * z, axis=-1, keepdims=True) + RMS_EPS)


def _rms_bwd(z, gain, d):
    r = _rms(z)
    n = z * r
    dn = d * gain
    dz = r * (dn - n * jnp.mean(dn * n, axis=-1, keepdims=True))
    return dz, jnp.sum(d * n, axis=0, keepdims=True)


def _gelu(z):
    return 0.5 * z * (1.0 + lax.erf(z * np.float32(1.0 / np.sqrt(2.0))))


def _gelu_grad(z):
    cdf = 0.5 * (1.0 + lax.erf(z * np.float32(1.0 / np.sqrt(2.0))))
    return cdf + z * jnp.exp(-0.5 * z * z) * np.float32(1.0 / np.sqrt(2.0 * np.pi))


def _rot_tables(pos_col, invf, ma, mb):
    ang = pos_col.astype(F32) * invf
    s = jnp.sin(ang)
    return jnp.cos(ang), s * ma, s * mb


def _rot(t, c, sa, sb):
    return t * c + pltpu.roll(t, 120, 1) * sa + pltpu.roll(t, 8, 1) * sb


def _rot_t(d, c, sa, sb):
    return d * c + pltpu.roll(d * sa, 8, 1) + pltpu.roll(d * sb, 120, 1)


def _rot_consts():
    lane = np.arange(128) % HEAD_DIM
    inv_freq = (np.float32(ROPE_THETA) ** (-np.arange(0, ROT_DIM, 2, dtype=np.float32) / np.float32(ROT_DIM))).astype(np.float32)
    invf = np.where(lane < ROT_DIM, inv_freq[lane % ROT_HALF], 0.0).astype(np.float32)
    ma = np.where(lane < ROT_HALF, -1.0, 0.0).astype(np.float32)
    mb = np.where((lane >= ROT_HALF) & (lane < ROT_DIM), 1.0, 0.0).astype(np.float32)
    return jnp.asarray(invf[None]), jnp.asarray(ma[None]), jnp.asarray(mb[None])


def _row_spec(tm, w):
    return pl.BlockSpec((tm, w), lambda i: (i, 0))


def _full_spec(shape):
    return pl.BlockSpec(shape, lambda i: (0,) * len(shape))


def _weight_spec(shape):
    return pl.BlockSpec(shape, lambda i: (0,) * len(shape), pipeline_mode=pl.Buffered(1))


RES = 16


def _residue_scratch(n_arrays, tm, width):
    return [pltpu.VMEM((2, n_arrays, tm // RES, RES, width), F32), pltpu.SemaphoreType.DMA((2, n_arrays, RES))]


def _to_residue_rows(tiles, outs, scratch, sems, tm, n_steps):
    i = pl.program_id(0)
    slot = i % 2
    per = tm // RES

    def copies(step, s):
        return [pltpu.make_async_copy(scratch.at[s, a, :, b, :],
                                      outs[a].at[pl.ds(pl.multiple_of(b * (SEQ // RES) + per * step, per), per), :],
                                      sems.at[s, a, b]) for a in range(len(outs)) for b in range(RES)]

    @pl.when(i >= 2)
    def _():
        for cp in copies(i - 2, slot):
            cp.wait()

    for a, tile in enumerate(tiles):
        scratch[slot, a] = tile.reshape(per, RES, tile.shape[-1])
    for cp in copies(i, slot):
        cp.start()

    @pl.when(i == n_steps - 1)
    def _():
        for cp in copies(i - 1, 1 - slot) + copies(i, slot):
            cp.wait()


def in_proj(x, pos_col, g1, w_in_t, rot):
    tm = 512
    n_steps = SEQ // tm

    def body(x_ref, pos_ref, g_ref, w_ref, invf_ref, ma_ref, mb_ref, h_ref, u_ref, vs_ref, q_ref, k_ref, v_ref, scratch, sems):
        xf = x_ref[...]
        h = (xf * _rms(xf) * g_ref[...]).astype(BF16)
        h_ref[...] = h
        proj = _dot_nt(h, w_ref[...])
        c, sa, sb = _rot_tables(pos_ref[...], invf_ref[...], ma_ref[...], mb_ref[...])
        slabs = range(ATTN_W // 128)
        q = jnp.concatenate([_rot(proj[:, j * 128:(j + 1) * 128], c, sa, sb) * Q_SCALE for j in slabs], axis=1)
        k = jnp.concatenate([_rot(proj[:, ATTN_W + j * 128:ATTN_W + (j + 1) * 128], c, sa, sb) for j in slabs], axis=1)
        u_ref[...] = proj[:, 3 * ATTN_W:3 * ATTN_W + SGU_W]
        vs_ref[...] = proj[:, 3 * ATTN_W + SGU_W:]
        _to_residue_rows([q, k, proj[:, 2 * ATTN_W:3 * ATTN_W]], [q_ref, k_ref, v_ref], scratch, sems, tm, n_steps)

    act = jax.ShapeDtypeStruct((SEQ, 512), F32)
    return _call(
        "in_proj", body, n_steps,
        [_row_spec(tm, D_MODEL), _row_spec(tm, 1), _full_spec((1, D_MODEL)), _weight_spec((IN_W, D_MODEL)),
         _full_spec((1, 128)), _full_spec((1, 128)), _full_spec((1, 128))],
        [_row_spec(tm, D_MODEL)] + [_row_spec(tm, 512)] * 2 + [ANY] * 3,
        [jax.ShapeDtypeStruct((SEQ, D_MODEL), BF16)] + [act] * 5,
        (x, pos_col, g1, w_in_t, *rot), scratch_shapes=_residue_scratch(3, tm, ATTN_W))


def _to_residue_order(t):
    return t.reshape(SEQ // RES, RES, -1).transpose(1, 0, 2).reshape(t.shape)


def _from_residue_order(t):
    return t.reshape(RES, SEQ // RES, -1).transpose(1, 0, 2).reshape(t.shape)


def _block_rows(d, r, n):
    if d == 16:
        slices = [(128 * r, 128)]
    elif d == 4:
        slices = [(128 * (4 * b + r) + 32 * n, 32) for b in range(4)]
    else:
        slices = [(128 * b + 8 * n, 8) for b in range(RES)]
    return [(s if isinstance(s, int) else pl.multiple_of(s, z), z) for s, z in slices]


def _block_step(d, i):
    if d == 16:
        return i
    if d == 4:
        return 4 * (i & 31) + (i >> 5)
    return 16 * (i & 7) + (i >> 3)


def _attn_masks(d):
    row2 = _block_step(d, lax.broadcasted_iota(jnp.int32, (128, 256), 0))
    col2 = lax.broadcasted_iota(jnp.int32, (128, 256), 1)
    key2 = _block_step(d, col2 & 127)
    mask2 = jnp.logical_or(jnp.logical_and(col2 < 128, key2 >= row2), jnp.logical_and(col2 >= 128, key2 <= row2))
    row1 = _block_step(d, lax.broadcasted_iota(jnp.int32, (128, 128), 0))
    col1 = lax.broadcasted_iota(jnp.int32, (128, 128), 1)
    return col1 < HEAD_DIM, _block_step(d, col1) <= row1, mask2


def _load_rows(ref, slices):
    parts = [ref[pl.ds(s, z), :] for s, z in slices]
    return parts[0] if len(parts) == 1 else jnp.concatenate(parts, axis=0)


def _for_each_group(fn):
    for p, d in enumerate(DILATIONS):
        masks = _attn_masks(d)
        if d == 16:
            def group(i, carry, p=p, masks=masks):
                fn(p, masks, [(_block_rows(16, 4 * i + g, 0), None) for g in range(4)])
                return carry

            lax.fori_loop(0, 4, group, 0)
        elif d == 4:
            fn(p, masks, [(_block_rows(4, r, 0), None) for r in range(4)])

            def group(i, carry, p=p, masks=masks):
                fn(p, masks, [(_block_rows(4, r, i + 1), _block_rows(4, r, i)) for r in range(4)])
                return carry

            lax.fori_loop(0, 3, group, 0)
        else:
            fn(p, masks, [(_block_rows(1, 0, 0), None)])

            def group(i, carry, p=p, masks=masks):
                fn(p, masks, [(_block_rows(1, 0, 3 * i + g + 1), _block_rows(1, 0, 3 * i + g)) for g in range(3)])
                return carry

            lax.fori_loop(0, 5, group, 0)


def attn_fwd(q, k, v):
    def body(q_ref, k_ref, v_ref, o_ref, lse_ref, nat_ref, op_ref, lp_ref, sems):
        def group(p, masks, blocks):
            head0, mask1, mask2 = masks
            heads = (head0, jnp.logical_not(head0))
            keys = [rows if prev is None else prev + rows for rows, prev in blocks]
            mask = [mask1 if prev is None else mask2 for _, prev in blocks]
            qb = [_load_rows(q_ref, rows) for rows, _ in blocks]
            kk = [_load_rows(k_ref, ks).astype(BF16) for ks in keys]
            vv = [_load_rows(v_ref, ks).astype(BF16) for ks in keys]
            chains = [(g, hm) for g in range(len(blocks)) for hm in heads]
            s = [jnp.where(mask[g], _dot_nt(jnp.where(hm, qb[g], 0.0).astype(BF16), kk[g]), NEG) for g, hm in chains]
            m = [jnp.max(t, axis=-1, keepdims=True) for t in s]
            e = [jnp.exp(t - mt) for t, mt in zip(s, m)]
            l = [jnp.sum(t, axis=-1, keepdims=True) for t in e]
            pv = [_dot(t.astype(BF16), vv[g]) for t, (g, _) in zip(e, chains)]
            for g, (rows, _) in enumerate(blocks):
                o_blk = jnp.where(head0, pv[2 * g] / l[2 * g], pv[2 * g + 1] / l[2 * g + 1])
                l_blk = jnp.where(head0, jnp.broadcast_to(m[2 * g] + jnp.log(l[2 * g]), (128, 128)),
                                  jnp.broadcast_to(m[2 * g + 1] + jnp.log(l[2 * g + 1]), (128, 128)))
                at = 0
                for start, size in rows:
                    op_ref[p, pl.ds(start, size), :] = o_blk[at:at + size]
                    lp_ref[p, pl.ds(start, size), :] = l_blk[at:at + size]
                    at += size

        _for_each_group(group)

        def combine(i, carry):
            rows = pl.ds(pl.multiple_of(i * 256, 256), 256)
            ls = [lp_ref[p, rows, :] for p in range(3)]
            m = jnp.maximum(jnp.maximum(ls[0], ls[1]), ls[2])
            lse = m + jnp.log(jnp.exp(ls[0] - m) + jnp.exp(ls[1] - m) + jnp.exp(ls[2] - m))
            o = jnp.zeros((256, 128), F32)
            for p in range(3):
                o = o + jnp.exp(ls[p] - lse) * op_ref[p, rows, :]
            o_ref[rows, :] = o
            lse_ref[rows, :] = lse
            return carry

        lax.fori_loop(0, SEQ // 256, combine, 0)

        lanes = pl.ds(pl.multiple_of(pl.program_id(0) * 128, 128), 128)
        back = [pltpu.make_async_copy(o_ref.at[pl.ds(b * (SEQ // RES), SEQ // RES), :], nat_ref.at[:, b, lanes], sems.at[b])
                for b in range(RES)]
        for cp in back:
            cp.start()
        for cp in back:
            cp.wait()

    slab = pl.BlockSpec((SEQ, 128), lambda i: (0, i))
    out = jax.ShapeDtypeStruct((SEQ, ATTN_W), F32)
    attn_r, lse, attn = _call(
        "attn_fwd", body, ATTN_W // 128, [slab] * 3, [slab] * 2 + [ANY],
        [out, out, jax.ShapeDtypeStruct((SEQ // RES, RES, ATTN_W), F32)], (q, k, v),
        scratch_shapes=[pltpu.VMEM((3, SEQ, 128), F32), pltpu.VMEM((3, SEQ, 128), F32), pltpu.SemaphoreType.DMA((RES,))])
    return attn_r, lse, attn.reshape(SEQ, ATTN_W)


def _causal_weights(w_ref):
    row = lax.broadcasted_iota(jnp.int32, (CHUNK, CHUNK), 0)
    col = lax.broadcasted_iota(jnp.int32, (CHUNK, CHUNK), 1)
    return [jnp.where(col <= row, w_ref[g], 0.0).astype(BF16) for g in range(N_GROUPS)], col <= row


def _sgu_chunk_fwd(u, vs, lg, lb, wc, bfull, head0):
    ug = _gelu(u)
    vg = _gelu(vs)
    xc = vg - jnp.mean(vg, axis=-1, keepdims=True)
    rstd = lax.rsqrt(jnp.mean(xc * xc, axis=-1, keepdims=True) + LN_EPS)
    xhat = xc * rstd
    vn = xhat * lg + lb
    mixed = []
    for gp in range(SGU_W // 128):
        vp = vn[:, gp * 128:(gp + 1) * 128].astype(BF16)
        mixed.append(jnp.where(head0, _dot(wc[2 * gp], vp), _dot(wc[2 * gp + 1], vp)))
    ms = jnp.concatenate(mixed, axis=1) + bfull
    return ug, xhat, rstd, vn, ms


def sgu_fwd(u, vs, lg, lb, w_sp, bfull):
    cpb = 4

    def body(u_ref, vs_ref, lg_ref, lb_ref, w_ref, b_ref, o_ref):
        wc, _ = _causal_weights(w_ref)
        head0 = lax.broadcasted_iota(jnp.int32, (CHUNK, 128), 1) < HEAD_DIM
        for ci in range(cpb):
            rows = pl.ds(ci * CHUNK, CHUNK)
            ug, _, _, _, ms = _sgu_chunk_fwd(u_ref[rows, :], vs_ref[rows, :], lg_ref[...], lb_ref[...], wc, b_ref[...], head0)
            o_ref[rows, :] = ug * ms

    tm = cpb * CHUNK
    return _call(
        "sgu_fwd", body, SEQ // tm,
        [_row_spec(tm, SGU_W), _row_spec(tm, SGU_W), _full_spec((1, SGU_W)), _full_spec((1, SGU_W)),
         _full_spec((N_GROUPS, CHUNK, CHUNK)), _full_spec((CHUNK, SGU_W))],
        [_row_spec(tm, SGU_W)], [jax.ShapeDtypeStruct((SEQ, SGU_W), F32)],
        (u, vs, lg, lb, w_sp, bfull))


def out_proj(attn, sgu, x, ga, gs, w_out, gpm, gpf):
    tm = 512

    def body(a_ref, s_ref, x_ref, ga_ref, gs_ref, w_ref, gpm_ref, gpf_ref, mix_ref, y_ref, x2_ref, h2_ref):
        a = a_ref[...]
        s = s_ref[...]
        an = (a * _rms(a) * ga_ref[...]).astype(BF16)
        sn = (s * _rms(s) * gs_ref[...]).astype(BF16)
        mix_ref[:, :ATTN_W] = an
        mix_ref[:, ATTN_W:] = sn
        y = _dot(an, w_ref[:ATTN_W, :]) + _dot(sn, w_ref[ATTN_W:, :])
        y_ref[...] = y
        x2 = x_ref[...] + y * _rms(y) * gpm_ref[...]
        x2_ref[...] = x2
        h2_ref[...] = (x2 * _rms(x2) * gpf_ref[...]).astype(BF16)

    wide = jax.ShapeDtypeStruct((SEQ, D_MODEL), F32)
    wide16 = jax.ShapeDtypeStruct((SEQ, D_MODEL), BF16)
    return _call(
        "out_proj", body, SEQ // tm,
        [_row_spec(tm, ATTN_W), _row_spec(tm, SGU_W), _row_spec(tm, D_MODEL), _full_spec((1, ATTN_W)),
         _full_spec((1, SGU_W)), _weight_spec((D_MODEL, D_MODEL)), _full_spec((1, D_MODEL)), _full_spec((1, D_MODEL))],
        [_row_spec(tm, D_MODEL)] * 4, [wide16, wide, wide, wide16],
        (attn, sgu, x, ga, gs, w_out, gpm, gpf))


def ffn_up(h2, w_gate_t, w_up_t):
    tm = 256

    def body(h_ref, wg_ref, wu_ref, g_ref, u_ref, a_ref):
        h = h_ref[...]
        g = _dot_nt(h, wg_ref[...])
        u = _dot_nt(h, wu_ref[...])
        g_ref[...] = g.astype(BF16)
        u_ref[...] = u.astype(BF16)
        a_ref[...] = (g * jax.nn.sigmoid(g) * u).astype(BF16)

    ff = jax.ShapeDtypeStruct((SEQ, D_FF), BF16)
    return _call(
        "ffn_up", body, SEQ // tm,
        [_row_spec(tm, D_MODEL), _weight_spec((D_FF, D_MODEL)), _weight_spec((D_FF, D_MODEL))],
        [_row_spec(tm, D_FF)] * 3, [ff, ff, jax.ShapeDtypeStruct((SEQ, D_FF), BF16)],
        (h2, w_gate_t, w_up_t))


def ffn_down_loss(act, w_down, x2, gpo, target):
    tm = 512

    def body(a_ref, w_ref, x2_ref, g_ref, t_ref, df_ref, dx3_ref, dg_ref, loss_ref):
        f = _dot(a_ref[...], w_ref[...])
        gain = g_ref[...]
        err = x2_ref[...] + f * _rms(f) * gain - t_ref[...]
        dx3 = err * np.float32(1.0 / D_MODEL)
        dx3_ref[...] = dx3
        df, dg = _rms_bwd(f, gain, dx3)
        df_ref[...] = df.astype(BF16)

        @pl.when(pl.program_id(0) == 0)
        def _():
            dg_ref[...] = jnp.zeros_like(dg_ref)
            loss_ref[...] = jnp.zeros_like(loss_ref)

        dg_ref[...] += dg
        loss_ref[...] += jnp.sum(err * err, axis=(0, 1), keepdims=True)

    return pl.pallas_call(
        body, name="ffn_down_loss", grid=(SEQ // tm,),
        in_specs=[_row_spec(tm, D_FF), _weight_spec((D_FF, D_MODEL)), _row_spec(tm, D_MODEL), _full_spec((1, D_MODEL)),
                  _row_spec(tm, D_MODEL)],
        out_specs=[_row_spec(tm, D_MODEL), _row_spec(tm, D_MODEL), _full_spec((1, D_MODEL)), _full_spec((1, 1))],
        out_shape=[jax.ShapeDtypeStruct((SEQ, D_MODEL), BF16), jax.ShapeDtypeStruct((SEQ, D_MODEL), F32),
                   jax.ShapeDtypeStruct((1, D_MODEL), F32), jax.ShapeDtypeStruct((1, 1), F32)],
        compiler_params=_params(),
    )(act, w_down, x2, gpo, target)


def ffn_bwd(df, w_down, gate, up, w_gate_t, w_up_t, x2, gpf, dx3, y, gpm, after=()):
    tm = 256

    def body(df_ref, wd_ref, g_ref, u_ref, wg_ref, wu_ref, x2_ref, gpf_ref, dx3_ref, y_ref, gpm_ref,
             dg_ref, du_ref, dx2_ref, dy_ref, dgpf_ref, dgpm_ref):
        dact = _dot_nt(df_ref[...], wd_ref[...])
        g = g_ref[...].astype(F32)
        s = jax.nn.sigmoid(g)
        dup = (dact * g * s).astype(BF16)
        dgate = (dact * u_ref[...].astype(F32) * (s * (1.0 + g * (1.0 - s)))).astype(BF16)
        du_ref[...] = dup
        dg_ref[...] = dgate
        dh2 = _dot(dgate, wg_ref[...]) + _dot(dup, wu_ref[...])
        dz, dgpf = _rms_bwd(x2_ref[...], gpf_ref[...], dh2)
        dx2 = dx3_ref[...] + dz
        dx2_ref[...] = dx2
        dy, dgpm = _rms_bwd(y_ref[...], gpm_ref[...], dx2)
        dy_ref[...] = dy.astype(BF16)

        @pl.when(pl.program_id(0) == 0)
        def _():
            dgpf_ref[...] = jnp.zeros_like(dgpf_ref)
            dgpm_ref[...] = jnp.zeros_like(dgpm_ref)

        dgpf_ref[...] += dgpf
        dgpm_ref[...] += dgpm

    vec = jax.ShapeDtypeStruct((1, D_MODEL), F32)
    ff16 = jax.ShapeDtypeStruct((SEQ, D_FF), BF16)
    return _call(
        "ffn_bwd", body, SEQ // tm,
        [_row_spec(tm, D_MODEL), _weight_spec((D_FF, D_MODEL)), _row_spec(tm, D_FF), _row_spec(tm, D_FF),
         _weight_spec((D_FF, D_MODEL)), _weight_spec((D_FF, D_MODEL)), _row_spec(tm, D_MODEL), _full_spec((1, D_MODEL)),
         _row_spec(tm, D_MODEL), _row_spec(tm, D_MODEL), _full_spec((1, D_MODEL))],
        [_row_spec(tm, D_FF), _row_spec(tm, D_FF), _row_spec(tm, D_MODEL), _row_spec(tm, D_MODEL),
         _full_spec((1, D_MODEL)), _full_spec((1, D_MODEL))],
        [ff16, ff16, jax.ShapeDtypeStruct((SEQ, D_MODEL), F32), jax.ShapeDtypeStruct((SEQ, D_MODEL), BF16), vec, vec],
        (df, w_down, gate, up, w_gate_t, w_up_t, x2, gpf, dx3, y, gpm), after=after)


def weight_grad(name, a, b, after=()):
    m, n = a.shape[1], b.shape[1]
    tr = 256

    def body(a_ref, b_ref, o_ref):
        o_ref[...] = _dot_tn(a_ref[...], b_ref[...]).astype(BF16)

    (out,) = _call(
        name, body, m // tr, [pl.BlockSpec((SEQ, tr), lambda i: (0, i)), _weight_spec((SEQ, n))],
        [_row_spec(tr, n)], [jax.ShapeDtypeStruct((m, n), BF16)], (a, b), after=after)
    return out.reshape(N_DEV, m // N_DEV, n)


def mix_bwd(dy, w_out, attn, sgu, ga, gs, after=()):
    tm = 512
    n_steps = SEQ // tm

    def body(dy_ref, w_ref, a_ref, s_ref, ga_ref, gs_ref, ds_ref, dga_ref, dgs_ref, da_ref, scratch, sems):
        dy = dy_ref[...]
        da, dga = _rms_bwd(a_ref[...], ga_ref[...], _dot_nt(dy, w_ref[:ATTN_W, :]))
        ds, dgs = _rms_bwd(s_ref[...], gs_ref[...], _dot_nt(dy, w_ref[ATTN_W:, :]))
        ds_ref[...] = ds
        _to_residue_rows([da], [da_ref], scratch, sems, tm, n_steps)

        @pl.when(pl.program_id(0) == 0)
        def _():
            dga_ref[...] = jnp.zeros_like(dga_ref)
            dgs_ref[...] = jnp.zeros_like(dgs_ref)

        dga_ref[...] += dga
        dgs_ref[...] += dgs

    half = jax.ShapeDtypeStruct((SEQ, 512), F32)
    vec = jax.ShapeDtypeStruct((1, 512), F32)
    return _call(
        "mix_bwd", body, n_steps,
        [_row_spec(tm, D_MODEL), _weight_spec((D_MODEL, D_MODEL)), _row_spec(tm, 512), _row_spec(tm, 512),
         _full_spec((1, 512)), _full_spec((1, 512))],
        [_row_spec(tm, 512), _full_spec((1, 512)), _full_spec((1, 512)), ANY],
        [half, vec, vec, half], (dy, w_out, attn, sgu, ga, gs), scratch_shapes=_residue_scratch(1, tm, ATTN_W), after=after)


def sgu_bwd(u, vs, dsgu, lg, lb, w_sp, bfull):
    cpb = 4

    def body(u_ref, vs_ref, d_ref, lg_ref, lb_ref, w_ref, b_ref, du_ref, dvs_ref, dlg_ref, dlb_ref, dw_ref, db_ref):
        wc, causal = _causal_weights(w_ref)
        head0 = lax.broadcasted_iota(jnp.int32, (CHUNK, 128), 1) < HEAD_DIM
        lg = lg_ref[...]

        @pl.when(pl.program_id(0) == 0)
        def _():
            dlg_ref[...] = jnp.zeros_like(dlg_ref)
            dlb_ref[...] = jnp.zeros_like(dlb_ref)
            dw_ref[...] = jnp.zeros_like(dw_ref)
            db_ref[...] = jnp.zeros_like(db_ref)

        for ci in range(cpb):
            rows = pl.ds(ci * CHUNK, CHUNK)
            u = u_ref[rows, :]
            vs = vs_ref[rows, :]
            d = d_ref[rows, :]
            ug, xhat, rstd, vn, ms = _sgu_chunk_fwd(u, vs, lg, lb_ref[...], wc, b_ref[...], head0)
            du_ref[rows, :] = (d * ms * _gelu_grad(u)).astype(BF16)
            dms = d * ug
            db_ref[...] += dms
            dvn = []
            for gp in range(SGU_W // 128):
                dmp = dms[:, gp * 128:(gp + 1) * 128]
                dm0 = jnp.where(head0, dmp, 0.0).astype(BF16)
                dm1 = jnp.where(head0, 0.0, dmp).astype(BF16)
                vp = vn[:, gp * 128:(gp + 1) * 128].astype(BF16)
                dw_ref[2 * gp] += _dot_nt(dm0, vp)
                dw_ref[2 * gp + 1] += _dot_nt(dm1, vp)
                dvn.append(_dot_tn(wc[2 * gp], dm0) + _dot_tn(wc[2 * gp + 1], dm1))
            dvn = jnp.concatenate(dvn, axis=1)
            dlg_ref[...] += jnp.sum(dvn * xhat, axis=0, keepdims=True)
            dlb_ref[...] += jnp.sum(dvn, axis=0, keepdims=True)
            dxh = dvn * lg
            dvg = rstd * (dxh - jnp.mean(dxh, axis=-1, keepdims=True) - xhat * jnp.mean(dxh * xhat, axis=-1, keepdims=True))
            dvs_ref[rows, :] = (dvg * _gelu_grad(vs)).astype(BF16)

        @pl.when(pl.program_id(0) == pl.num_programs(0) - 1)
        def _():
            for g in range(N_GROUPS):
                dw_ref[g] = jnp.where(causal, dw_ref[g], 0.0)

    tm = cpb * CHUNK
    half16 = jax.ShapeDtypeStruct((SEQ, SGU_W), BF16)
    vec = jax.ShapeDtypeStruct((1, SGU_W), F32)
    return _call(
        "sgu_bwd", body, SEQ // tm,
        [_row_spec(tm, SGU_W)] * 3 + [_full_spec((1, SGU_W)), _full_spec((1, SGU_W)),
                                      _full_spec((N_GROUPS, CHUNK, CHUNK)), _full_spec((CHUNK, SGU_W))],
        [_row_spec(tm, SGU_W), _row_spec(tm, SGU_W), _full_spec((1, SGU_W)), _full_spec((1, SGU_W)),
         _full_spec((N_GROUPS, CHUNK, CHUNK)), _full_spec((CHUNK, SGU_W))],
        [half16, half16, vec, vec, jax.ShapeDtypeStruct((N_GROUPS, CHUNK, CHUNK), F32),
         jax.ShapeDtypeStruct((CHUNK, SGU_W), F32)],
        (u, vs, dsgu, lg, lb, w_sp, bfull))


def attn_bwd(q, k, v, o, lse, do, pos_col, rot):
    def body(q_ref, k_ref, v_ref, o_ref, lse_ref, do_ref, pos_ref, invf_ref, ma_ref, mb_ref,
             dq_ref, dk_ref, dv_ref, dqa_ref, dka_ref, dva_ref, dlt_ref, rot_ref):
        dqa_ref[...] = jnp.zeros_like(dqa_ref)
        dka_ref[...] = jnp.zeros_like(dka_ref)
        dva_ref[...] = jnp.zeros_like(dva_ref)

        def delta(i, carry):
            rows = pl.ds(pl.multiple_of(i * 256, 256), 256)
            prod = do_ref[rows, :] * o_ref[rows, :]
            h0 = lax.broadcasted_iota(jnp.int32, (256, 128), 1) < HEAD_DIM
            d0 = jnp.sum(jnp.where(h0, prod, 0.0), axis=-1, keepdims=True)
            d1 = jnp.sum(jnp.where(h0, 0.0, prod), axis=-1, keepdims=True)
            dlt_ref[rows, :] = jnp.where(h0, d0, d1)
            return carry

        lax.fori_loop(0, SEQ // 256, delta, 0)

        def add_rows(ref, slices, val):
            at = 0
            for start, size in slices:
                ref[pl.ds(start, size), :] += val[at:at + size]
                at += size

        def group(p, masks, blocks):
            head0, mask1, mask2 = masks
            heads = (head0, jnp.logical_not(head0))
            keys = [rows if prev is None else prev + rows for rows, prev in blocks]
            mask = [mask1 if prev is None else mask2 for _, prev in blocks]
            kk = [_load_rows(k_ref, ks).astype(BF16) for ks in keys]
            vv = [_load_rows(v_ref, ks).astype(BF16) for ks in keys]
            qb = [_load_rows(q_ref, rows) for rows, _ in blocks]
            dob = [_load_rows(do_ref, rows) for rows, _ in blocks]
            lse_b = [_load_rows(lse_ref, rows) for rows, _ in blocks]
            dlt_b = [_load_rows(dlt_ref, rows) for rows, _ in blocks]
            chains = [(g, h) for g in range(len(blocks)) for h in range(2)]
            qm = [jnp.where(heads[h], qb[g], 0.0).astype(BF16) for g, h in chains]
            dom = [jnp.where(heads[h], dob[g], 0.0).astype(BF16) for g, h in chains]
            s = [_dot_nt(qm[c], kk[g]) for c, (g, h) in enumerate(chains)]
            dp = [_dot_nt(dom[c], vv[g]) for c, (g, h) in enumerate(chains)]
            pr = [jnp.where(mask[g], jnp.exp(s[c] - lse_b[g][:, h * HEAD_DIM:h * HEAD_DIM + 1]), 0.0)
                  for c, (g, h) in enumerate(chains)]
            ds = [(pr[c] * (dp[c] - dlt_b[g][:, h * HEAD_DIM:h * HEAD_DIM + 1])).astype(BF16)
                  for c, (g, h) in enumerate(chains)]
            dv = [_dot_tn(pr[c].astype(BF16), dom[c]) for c in range(len(chains))]
            dk = [_dot_tn(ds[c], qm[c]) for c in range(len(chains))]
            dq = [_dot(ds[c], kk[g]) for c, (g, h) in enumerate(chains)]
            for g, (rows, _) in enumerate(blocks):
                add_rows(dqa_ref, rows, jnp.where(head0, dq[2 * g], dq[2 * g + 1]))
                add_rows(dka_ref, keys[g], dk[2 * g] + dk[2 * g + 1])
                add_rows(dva_ref, keys[g], dv[2 * g] + dv[2 * g + 1])

        _for_each_group(group)

        @pl.when(pl.program_id(0) == 0)
        def _():
            def tables(i, carry):
                rows = pl.ds(pl.multiple_of(i * 256, 256), 256)
                c, sa, sb = _rot_tables(pos_ref[rows, :], invf_ref[...], ma_ref[...], mb_ref[...])
                rot_ref[0, rows, :] = c
                rot_ref[1, rows, :] = sa
                rot_ref[2, rows, :] = sb
                return carry

            lax.fori_loop(0, SEQ // 256, tables, 0)

        def finish(i, carry):
            rows = pl.ds(pl.multiple_of(i * 256, 256), 256)
            c, sa, sb = rot_ref[0, rows, :], rot_ref[1, rows, :], rot_ref[2, rows, :]
            dq_ref[rows, :] = _rot_t(dqa_ref[rows, :] * Q_SCALE, c, sa, sb).astype(BF16)
            dk_ref[rows, :] = _rot_t(dka_ref[rows, :], c, sa, sb).astype(BF16)
            dv_ref[rows, :] = dva_ref[rows, :].astype(BF16)
            return carry

        lax.fori_loop(0, SEQ // 256, finish, 0)

    slab = pl.BlockSpec((SEQ, 128), lambda i: (0, i))
    out = jax.ShapeDtypeStruct((SEQ, ATTN_W), BF16)
    acc = pltpu.VMEM((SEQ, 128), F32)
    return _call(
        "attn_bwd", body, ATTN_W // 128,
        [slab] * 6 + [_full_spec((SEQ, 1)), _full_spec((1, 128)), _full_spec((1, 128)), _full_spec((1, 128))],
        [slab] * 3, [out, out, out], (q, k, v, o, lse, do, pos_col, *rot),
        scratch_shapes=[acc, acc, acc, acc, pltpu.VMEM((3, SEQ, 128), F32)])


def in_bwd(dproj, w_in_t, x, g1, dx2, after=()):
    tm = 512

    def body(dp_ref, w_ref, x_ref, g_ref, dx2_ref, dx_ref, dg_ref):
        dh1 = _dot(dp_ref[...], w_ref[...])
        dz, dg = _rms_bwd(x_ref[...], g_ref[...], dh1)
        dx_ref[...] = dx2_ref[...] + dz

        @pl.when(pl.program_id(0) == 0)
        def _():
            dg_ref[...] = jnp.zeros_like(dg_ref)

        dg_ref[...] += dg

    return _call(
        "in_bwd", body, SEQ // tm,
        [_row_spec(tm, IN_W), _weight_spec((IN_W, D_MODEL)), _row_spec(tm, D_MODEL), _full_spec((1, D_MODEL)),
         _row_spec(tm, D_MODEL)],
        [_row_spec(tm, D_MODEL), _full_spec((1, D_MODEL))],
        [jax.ShapeDtypeStruct((SEQ, D_MODEL), F32), jax.ShapeDtypeStruct((1, D_MODEL), F32)],
        (dproj, w_in_t, x, g1, dx2), after=after)


def _coords():
    return lax.axis_index("x"), lax.axis_index("y"), lax.axis_index("c")


class Exchange:
    def __init__(self, srcs, bufs, new_shapes, n_sems, make, has_middle=False):
        self.srcs, self.bufs, self.new_shapes, self.n_sems, self.make = list(srcs), list(bufs), list(new_shapes), n_sems, make
        self.has_middle = has_middle


def _call(name, body, n_steps, in_specs, out_specs, out_shape, args, scratch_shapes=(), after=()):
    n_in = len(args)

    def wrapped(*refs):
        body(*refs[:n_in], *refs[n_in + len(after):])

    return list(pl.pallas_call(
        wrapped, name=name, grid=(n_steps,), in_specs=list(in_specs) + [ANY] * len(after), out_specs=list(out_specs),
        out_shape=list(out_shape), scratch_shapes=list(scratch_shapes), compiler_params=_params(),
    )(*args, *after))


def _gather_copies(kinds, bufs, ranges, send_sems, recv_sems):
    x, y, c = _coords()
    me, sibling = (x, y, c), (x, y, 1 - c)
    chips = [(1 - x, y), (x, 1 - y), (1 - x, 1 - y)]

    def copy(a, k, block, to):
        lo, hi = ranges[a]
        r = bufs[a].shape[0] // N_DEV
        rows = bufs[a].at[pl.ds((4 * block[0] + 2 * block[1] + block[2]) * r + lo, hi - lo), :]
        return pltpu.make_async_remote_copy(src_ref=rows, dst_ref=rows, send_sem=send_sems.at[7 * a + k],
                                            recv_sem=recv_sems.at[7 * a + k], device_id=to, device_id_type=MESH)

    every = range(len(bufs))
    make = {
        "out": lambda: [copy(a, 0, me, sibling) for a in every]
        + [copy(a, 1 + j, me, (*chip, c)) for a in every for j, chip in enumerate(chips)],
        "from_core": lambda: [copy(a, 0, sibling, me) for a in every],
        "from_chips": lambda: [copy(a, 1 + j, (*chip, c), me) for a in every for j, chip in enumerate(chips)],
        "on": lambda: [copy(a, 4 + j, (*chip, c), sibling) for a in every for j, chip in enumerate(chips)],
        "on_in": lambda: [copy(a, 4 + j, (*chip, 1 - c), me) for a in every for j, chip in enumerate(chips)],
    }
    return [make[kind]() for kind in kinds]


def gather(bufs):
    ranges = [(0, b.shape[0] // N_DEV) for b in bufs]

    def make(phase, src_refs, buf_refs, new_refs, send_sems, recv_sems):
        kinds = {"start": ["out"], "middle": ["from_chips", "on"], "end": ["out", "on", "from_core", "on_in"]}[phase]
        made = _gather_copies(kinds, buf_refs, ranges, send_sems, recv_sems)
        if phase == "start":
            return made[0]
        if phase == "middle":
            return made[0], made[1]
        return made[0] + made[1], made[2] + made[3]

    return Exchange([], bufs, [], 7 * len(bufs), make, has_middle=True)


TO_GATHER = (1, lambda x, y, c: [(x, y, 1 - c), (1 - x, y, c), (x, 1 - y, c), (1 - x, 1 - y, c)])
TO_SIBLING = (2, lambda x, y, c: [(x, y, 1 - c)])
TO_CHIPS = (3, lambda x, y, c: [(1 - x, y, c), (x, 1 - y, c), (1 - x, 1 - y, c)])
TO_ALL = (4, lambda x, y, c: [(x ^ (m >> 2), y ^ ((m >> 1) & 1), c ^ (m & 1)) for m in range(1, N_DEV)])


def by_sequencer(name, exchanges, who):
    collective_id, peers_of = who
    hbm = pltpu.MemorySpace.HBM
    refs = [([jax.new_ref(a, memory_space=hbm) for a in ex.srcs], [jax.new_ref(a, memory_space=hbm) for a in ex.bufs],
             [jax.empty_ref(s, memory_space=hbm) for s in ex.new_shapes]) for ex in exchanges]
    sems = []
    for ex in exchanges:
        sems += [pltpu.SemaphoreType.DMA((ex.n_sems,)), pltpu.SemaphoreType.DMA((ex.n_sems,))]

    @pl.kernel(mesh=plsc.ScalarSubcoreMesh(axis_name="sequencer", num_cores=1), name=name, scratch_types=tuple(sems),
               compiler_params=pltpu.CompilerParams(collective_id=collective_id))
    def launch(*sem_refs):
        peers = peers_of(*_coords())
        barrier = pltpu.get_barrier_semaphore()
        for peer in peers:
            pl.semaphore_signal(barrier, inc=1, device_id=peer, device_id_type=MESH)
        pl.semaphore_wait(barrier, len(peers))

        def make(phase, k):
            return exchanges[k].make(phase, *refs[k], sem_refs[2 * k], sem_refs[2 * k + 1])

        for k in range(len(exchanges)):
            for cp in make("start", k):
                cp.start()
        for k, ex in enumerate(exchanges):
            if ex.has_middle:
                arrivals, starts = make("middle", k)
                for cp in arrivals:
                    cp.wait_recv()
                for cp in starts:
                    cp.start()
        for k in range(len(exchanges)):
            sends, arrivals = make("end", k)
            for cp in arrivals:
                cp.wait_recv()
            for cp in sends:
                cp.wait_send()

    launch()
    return [([ref[...] for ref in bufs], [ref[...] for ref in news]) for _, bufs, news in refs]


def place_shards(name, shards, dev):
    n = len(shards)

    def body(dev_ref, *refs):
        for a in range(n):
            refs[n + a][...] = refs[a][...].astype(BF16)

    spec = pltpu.PrefetchScalarGridSpec(
        num_scalar_prefetch=1, grid=(1,),
        in_specs=[pl.BlockSpec(s.shape, lambda i, dev_ref: (0, 0)) for s in shards],
        out_specs=[pl.BlockSpec(s.shape, lambda i, dev_ref: (dev_ref[0], 0)) for s in shards])
    return pl.pallas_call(
        body, name=name, grid_spec=spec,
        out_shape=[jax.ShapeDtypeStruct((N_DEV * s.shape[0], s.shape[1]), BF16) for s in shards],
        compiler_params=_params(),
    )(dev, *shards)


def _swap(copies_of):
    def make(phase, src_refs, buf_refs, new_refs, send_sems, recv_sems):
        copies = copies_of(src_refs, new_refs, send_sems, recv_sems)
        return copies if phase == "start" else (copies, copies)

    return make


def to_sibling(grads):
    def copies_of(src_refs, new_refs, send_sems, recv_sems):
        x, y, c = _coords()
        return [pltpu.make_async_remote_copy(
            src_ref=src_refs[a].at[2 * xy + 1 - c], dst_ref=new_refs[a].at[xy], send_sem=send_sems.at[4 * a + xy],
            recv_sem=recv_sems.at[4 * a + xy], device_id=(x, y, 1 - c), device_id_type=MESH)
            for a in range(len(src_refs)) for xy in range(4)]

    return Exchange(grads, [], [jax.ShapeDtypeStruct((4,) + g.shape[1:], g.dtype) for g in grads], 4 * len(grads),
                    _swap(copies_of))


def to_chips(parts):
    def copies_of(src_refs, new_refs, send_sems, recv_sems):
        x, y, c = _coords()
        chips = [(1 - x, y), (x, 1 - y), (1 - x, 1 - y)]
        return [pltpu.make_async_remote_copy(
            src_ref=src_refs[a].at[2 * px + py], dst_ref=new_refs[a].at[2 * x + y], send_sem=send_sems.at[3 * a + j],
            recv_sem=recv_sems.at[3 * a + j], device_id=(px, py, c), device_id_type=MESH)
            for a in range(len(src_refs)) for j, (px, py) in enumerate(chips)]

    return Exchange(parts, [], [jax.ShapeDtypeStruct(p.shape, p.dtype) for p in parts], 3 * len(parts), _swap(copies_of))


def to_owners(grad):
    def copies_of(src_refs, new_refs, send_sems, recv_sems):
        x, y, c = _coords()
        copies = []
        for m in range(1, N_DEV):
            px, py, pc = x ^ (m >> 2), y ^ ((m >> 1) & 1), c ^ (m & 1)
            copies.append(pltpu.make_async_remote_copy(
                src_ref=src_refs[0].at[4 * px + 2 * py + pc], dst_ref=new_refs[0].at[4 * x + 2 * y + c],
                send_sem=send_sems.at[m - 1], recv_sem=recv_sems.at[m - 1], device_id=(px, py, pc), device_id_type=MESH))
        return copies

    return Exchange([grad], [], [jax.ShapeDtypeStruct(grad.shape, grad.dtype)], N_DEV - 1, _swap(copies_of))


def sum_cores(name, grad, other, core, after=()):
    _, r, w = other.shape

    def body(core_ref, g_ref, o_ref, *rest):
        rest[-1][...] = (g_ref[...].astype(F32) + o_ref[...].astype(F32)).astype(rest[-1].dtype)

    return pl.pallas_call(
        body, name=name,
        grid_spec=pltpu.PrefetchScalarGridSpec(
            num_scalar_prefetch=1, grid=(4,),
            in_specs=[pl.BlockSpec((1, r, w), lambda i, core_ref: (2 * i + core_ref[0], 0, 0)),
                      pl.BlockSpec((1, r, w), lambda i, core_ref: (i, 0, 0))] + [ANY] * len(after),
            out_specs=pl.BlockSpec((1, r, w), lambda i, core_ref: (i, 0, 0))),
        out_shape=jax.ShapeDtypeStruct(other.shape, other.dtype),
        compiler_params=_params(),
    )(core, grad, other, *after)


def sum_owned(name, grad, others, dev_ids, after=()):
    _, r, w = grad.shape

    def body(ids_ref, *refs):
        acc = refs[0][0]
        for k in range(1, N_DEV):
            acc = acc + refs[k][0]
        refs[-1][...] = acc

    def pick(k):
        return pl.BlockSpec((1, r, w), lambda i, ids_ref: (ids_ref[k], 0, 0))

    return pl.pallas_call(
        body, name=name,
        grid_spec=pltpu.PrefetchScalarGridSpec(
            num_scalar_prefetch=1, grid=(1,), in_specs=[pick(k) for k in range(N_DEV)] + [ANY] * len(after),
            out_specs=pl.BlockSpec((r, w), lambda i, ids_ref: (ids_ref[0], 0))),
        out_shape=jax.ShapeDtypeStruct((N_DEV * r, w), F32),
        compiler_params=_params(),
    )(dev_ids, grad, *([others] * (N_DEV - 1)), *after)


def _adamw_update(w, g, m, v):
    nm = ADAM_B1 * m + np.float32(1.0 - ADAM_B1) * g
    nv = ADAM_B2 * v + np.float32(1.0 - ADAM_B2) * (g * g)
    m_hat = nm / np.float32(1.0 - ADAM_B1 ** ADAM_STEP)
    v_hat = nv / np.float32(1.0 - ADAM_B2 ** ADAM_STEP)
    return -ADAM_LR * (m_hat / (jnp.sqrt(v_hat) + ADAM_EPS) + ADAM_WD * w), nm, nv


def adamw_of_sums(name, part, others, chip_ids, w, m, v, after):
    _, r, wd = part.shape

    def body(ids_ref, p_ref, a_ref, b_ref, c_ref, w_ref, m_ref, v_ref, after_ref, g_ref, d_ref, nm_ref, nv_ref):
        g = ((p_ref[0].astype(F32) + a_ref[0].astype(F32)) + b_ref[0].astype(F32)) + c_ref[0].astype(F32)
        g_ref[...] = g
        d_ref[...], nm_ref[...], nv_ref[...] = _adamw_update(w_ref[...], g, m_ref[...], v_ref[...])

    def pick(k):
        return pl.BlockSpec((1, r, wd), lambda i, ids_ref: (ids_ref[k], 0, 0))

    whole = pl.BlockSpec((r, wd), lambda i, ids_ref: (0, 0))
    shape = jax.ShapeDtypeStruct((r, wd), F32)
    return pl.pallas_call(
        body, name=name,
        grid_spec=pltpu.PrefetchScalarGridSpec(
            num_scalar_prefetch=1, grid=(1,), in_specs=[pick(0), pick(1), pick(2), pick(3), whole, whole, whole, ANY],
            out_specs=[whole] * 4),
        out_shape=[shape] * 4,
        compiler_params=_params(),
    )(chip_ids, part, others, others, others, w, m, v, after)


def adamw(name, w, g, m, v):
    def body(w_ref, g_ref, m_ref, v_ref, d_ref, nm_ref, nv_ref):
        d_ref[...], nm_ref[...], nv_ref[...] = _adamw_update(w_ref[...], g_ref[...], m_ref[...], v_ref[...])

    shape = jax.ShapeDtypeStruct(w.shape, F32)
    spec = _full_spec(w.shape)
    return pl.pallas_call(
        body, name=name, grid=(1,), in_specs=[spec] * 4, out_specs=[spec] * 3, out_shape=[shape] * 3,
        compiler_params=_params(),
    )(w, g, m, v)


def _pack_small(parts):
    flat = jnp.concatenate([parts[name].reshape(-1) for name, _ in SMALL])
    return jnp.pad(flat, (0, SMALL_ROWS * 128 - flat.shape[0])).reshape(SMALL_ROWS, 128)


def _unpack_small(packed, like):
    flat = packed.reshape(-1)
    out, at = {}, 0
    for name, size in SMALL:
        out[name] = flat[at:at + size].reshape(like[name].shape)
        at += size
    return out


def kernel(x, positions, pre_mix_norm, w_in, sgu_ln_gain, sgu_ln_bias, sgu_w_spatial, sgu_b_spatial, attn_out_norm, sgu_out_norm, w_out, post_mix_norm, pre_ffn_norm, w_gate, w_up, w_down, post_ffn_norm, loss_target, m_pre_mix_norm, m_w_in, m_sgu_ln_gain, m_sgu_ln_bias, m_sgu_w_spatial, m_sgu_b_spatial, m_attn_out_norm, m_sgu_out_norm, m_w_out, m_post_mix_norm, m_pre_ffn_norm, m_w_gate, m_w_up, m_w_down, m_post_ffn_norm, v_pre_mix_norm, v_w_in, v_sgu_ln_gain, v_sgu_ln_bias, v_sgu_w_spatial, v_sgu_b_spatial, v_attn_out_norm, v_sgu_out_norm, v_w_out, v_post_mix_norm, v_pre_ffn_norm, v_w_gate, v_w_up, v_w_down, v_post_ffn_norm):
    small_w = dict(pre_mix_norm=pre_mix_norm, sgu_ln_gain=sgu_ln_gain, sgu_ln_bias=sgu_ln_bias, sgu_w_spatial=sgu_w_spatial,
                   sgu_b_spatial=sgu_b_spatial, attn_out_norm=attn_out_norm, sgu_out_norm=sgu_out_norm,
                   post_mix_norm=post_mix_norm, pre_ffn_norm=pre_ffn_norm, post_ffn_norm=post_ffn_norm)
    small_m = dict(pre_mix_norm=m_pre_mix_norm, sgu_ln_gain=m_sgu_ln_gain, sgu_ln_bias=m_sgu_ln_bias, sgu_w_spatial=m_sgu_w_spatial,
                   sgu_b_spatial=m_sgu_b_spatial, attn_out_norm=m_attn_out_norm, sgu_out_norm=m_sgu_out_norm,
                   post_mix_norm=m_post_mix_norm, pre_ffn_norm=m_pre_ffn_norm, post_ffn_norm=m_post_ffn_norm)
    small_v = dict(pre_mix_norm=v_pre_mix_norm, sgu_ln_gain=v_sgu_ln_gain, sgu_ln_bias=v_sgu_ln_bias, sgu_w_spatial=v_sgu_w_spatial,
                   sgu_b_spatial=v_sgu_b_spatial, attn_out_norm=v_attn_out_norm, sgu_out_norm=v_sgu_out_norm,
                   post_mix_norm=v_post_mix_norm, pre_ffn_norm=v_pre_ffn_norm, post_ffn_norm=v_post_ffn_norm)
    for table in (small_w, small_m, small_v):
        table["loss_sum"] = jnp.zeros((1,), F32)

    x2d = x[0]
    target = loss_target[0]
    pos_col = positions.reshape(SEQ, 1)
    rot = _rot_consts()
    w_sp = sgu_w_spatial[0]
    bfull = jnp.repeat(sgu_b_spatial[0].T, HEAD_DIM, axis=1)

    x_i, y_i, c_i = (lax.axis_index(a).astype(jnp.int32) for a in MESH_AXES)
    dev = 4 * x_i + 2 * y_i + c_i
    core = c_i.reshape(1)
    chip = 2 * x_i + y_i
    chip_ids = jnp.stack([chip, chip ^ 1, chip ^ 2, chip ^ 3])
    dev_ids = jnp.stack([dev ^ m for m in range(N_DEV)])

    def gathered(name, bufs):
        return by_sequencer(name, [gather(bufs)], TO_GATHER)[0][0]

    def from_sibling(name, grads):
        return by_sequencer(name, [to_sibling(grads)], TO_SIBLING)[0][1]

    def from_chips(name, parts):
        return by_sequencer(name, [to_chips(parts)], TO_CHIPS)[0][1]

    (w_in_t,) = place_shards("place_w_in", [w_in[0].T], dev.reshape(1))
    (w_in_t,) = gathered("gather_w_in", [w_in_t])
    w_gate_t, w_up_t, w_out_f, w_down_f = place_shards(
        "place_weights", [w_gate[0].T, w_up[0].T, w_out[0], w_down[0]], dev.reshape(1))
    (w_out_f,) = gathered("gather_w_out", [w_out_f])
    w_gate_t, w_up_t = gathered("gather_w_gate_up", [w_gate_t, w_up_t])
    (w_down_f,) = gathered("gather_w_down", [w_down_f])

    h1, u, vs, q, k, v = in_proj(x2d, pos_col, pre_mix_norm, w_in_t, rot)
    attn_r, lse, attn = attn_fwd(q, k, v)
    (sgu,) = sgu_fwd(u, vs, sgu_ln_gain, sgu_ln_bias, w_sp, bfull)
    mix, y, x2, h2 = out_proj(attn, sgu, x2d, attn_out_norm, sgu_out_norm, w_out_f, post_mix_norm, pre_ffn_norm)
    gate, up, act = ffn_up(h2, w_gate_t, w_up_t)
    df, dx3, d_post_ffn, sq_err = ffn_down_loss(act, w_down_f, x2, post_ffn_norm, target)

    g_w_down = weight_grad("grad_w_down", act, df)
    (s_down,) = from_sibling("w_down_to_sibling", [g_w_down])
    dgate, dup, dx2, dy, d_pre_ffn, d_post_mix = ffn_bwd(
        df, w_down_f, gate, up, w_gate_t, w_up_t, x2, pre_ffn_norm, dx3, y, post_mix_norm, after=[g_w_down])
    p_down = sum_cores("sum_cores_down", g_w_down, s_down, core, after=[dy])
    (c_down,) = from_chips("w_down_to_chips", [p_down])
    g_w_gate = weight_grad("grad_w_gate", dgate, h2, after=[p_down])
    g_w_up = weight_grad("grad_w_up", dup, h2, after=[g_w_gate])
    s_gate, s_up = from_sibling("w_gate_up_to_sibling", [g_w_gate, g_w_up])
    g_w_out = weight_grad("grad_w_out", mix, dy, after=[g_w_up, c_down])
    p_gate = sum_cores("sum_cores_gate", g_w_gate, s_gate, core, after=[g_w_out])
    p_up = sum_cores("sum_cores_up", g_w_up, s_up, core, after=[g_w_out])
    c_gate, c_up = from_chips("w_gate_up_to_chips", [p_gate, p_up])
    (s_out,) = from_sibling("w_out_to_sibling", [g_w_out])
    dsgu, d_attn_out, d_sgu_out, dattn_r = mix_bwd(dy, w_out_f, attn, sgu, attn_out_norm, sgu_out_norm, after=[p_gate, p_up])
    du, dvs, d_ln_gain, d_ln_bias, d_w_sp, d_bfull = sgu_bwd(u, vs, dsgu, sgu_ln_gain, sgu_ln_bias, w_sp, bfull)
    dq, dk, dv = attn_bwd(q, k, v, attn_r, lse, dattn_r, _to_residue_order(pos_col), rot)
    p_out = sum_cores("sum_cores_out", g_w_out, s_out, core, after=[dq])
    (c_out,) = from_chips("w_out_to_chips", [p_out])
    dq, dk, dv = (_from_residue_order(t) for t in (dq, dk, dv))
    dproj = jnp.concatenate([dq, dk, dv, du, dvs], axis=1)
    g_w_in = weight_grad("grad_w_in", dproj, h1, after=[c_gate, c_up])
    (s_in,) = from_sibling("w_in_to_sibling", [g_w_in])
    grad_x, d_pre_mix = in_bwd(dproj, w_in_t, x2d, pre_mix_norm, dx2, after=[g_w_in, c_out])
    p_in = sum_cores("sum_cores_in", g_w_in, s_in, core, after=[d_pre_mix])

    d_b_sp = d_bfull.reshape(CHUNK, N_GROUPS, HEAD_DIM).sum(axis=-1).T
    small_g = _pack_small(dict(pre_mix_norm=d_pre_mix, sgu_ln_gain=d_ln_gain, sgu_ln_bias=d_ln_bias, sgu_w_spatial=d_w_sp,
                               sgu_b_spatial=d_b_sp, attn_out_norm=d_attn_out, sgu_out_norm=d_sgu_out,
                               post_mix_norm=d_post_mix, pre_ffn_norm=d_pre_ffn, post_ffn_norm=d_post_ffn,
                               loss_sum=sq_err))
    small_g = small_g.reshape(N_DEV, SMALL_ROWS // N_DEV, 128)
    (_, (c_in,)), (_, (o_small,)) = by_sequencer(
        "last_sums_to_owners", [to_chips([p_in]), to_owners(small_g)], TO_ALL)

    big, last = {}, p_in
    for name, w, p, c, m, vv, transposed in (
            ("w_down", w_down, p_down, c_down, m_w_down, v_w_down, False), ("w_gate", w_gate, p_gate, c_gate, m_w_gate, v_w_gate, True),
            ("w_up", w_up, p_up, c_up, m_w_up, v_w_up, True), ("w_out", w_out, p_out, c_out, m_w_out, v_w_out, False),
            ("w_in", w_in, p_in, c_in, m_w_in, v_w_in, True)):
        if name == "w_in":
            last = sum_owned("sum_small", small_g, o_small, dev_ids, after=[last])
            (all_small,) = gathered("gather_small_grads", [last])
        turn = (lambda t: t.T) if transposed else (lambda t: t)
        outs = adamw_of_sums("adamw_" + name, p, c, chip_ids, turn(w[0]), turn(m[0]), turn(vv[0]), last)
        big[name], last = tuple(turn(t)[None] for t in outs), outs[0]
    sd, snm, snv = adamw("adamw_small", _pack_small(small_w), all_small, _pack_small(small_m), _pack_small(small_v))
    sg, sd, snm, snv = (_unpack_small(t, small_w) for t in (all_small, sd, snm, snv))
    loss = sg["loss_sum"][0] * np.float32(0.5 / D_MODEL)

    names = ["pre_mix_norm", "w_in", "sgu_ln_gain", "sgu_ln_bias", "sgu_w_spatial", "sgu_b_spatial", "attn_out_norm",
             "sgu_out_norm", "w_out", "post_mix_norm", "pre_ffn_norm", "w_gate", "w_up", "w_down", "post_ffn_norm"]
    outs = [loss, grad_x[None]]
    for i, table in enumerate((sg, sd, snm, snv)):
        for name in names:
            outs.append(big[name][i] if name in big else table[name])
    return tuple(outs)
```

```python
import numpy as np
import jax
import jax.numpy as jnp
from jax import lax
from jax.experimental import pallas as pl
from jax.experimental.pallas import tpu as pltpu
from jax.experimental.pallas import tpu_sc as plsc

F32 = jnp.float32
BF16 = jnp.bfloat16

SEQ = 2048
D_MODEL = 1024
ATTN_W = 512
SGU_W = 512
HEAD_DIM = 64
N_GROUPS = 8
CHUNK = 128
D_FF = 2816
IN_W = 3 * ATTN_W + 2 * SGU_W
DILATIONS = (1, 4, 16)
ROPE_THETA = 500000.0
ROT_DIM = 16
ROT_HALF = 8
RMS_EPS = 1e-6
LN_EPS = 1e-5
Q_SCALE = 0.125
NEG = -1e30

N_DEV = 8
MESH_AXES = ("x", "y", "c")
MESH = pl.DeviceIdType.MESH

ADAM_LR = 0.001
ADAM_B1 = 0.9
ADAM_B2 = 0.999
ADAM_EPS = 1e-08
ADAM_WD = 0.01
ADAM_STEP = 10

VMEM_LIMIT = 60 * 1024 * 1024
ANY = pl.BlockSpec(memory_space=pl.ANY)

SMALL = (("pre_mix_norm", 1024), ("sgu_ln_gain", 512), ("sgu_ln_bias", 512), ("sgu_w_spatial", 8 * 128 * 128),
         ("sgu_b_spatial", 1024), ("attn_out_norm", 512), ("sgu_out_norm", 512), ("post_mix_norm", 1024),
         ("pre_ffn_norm", 1024), ("post_ffn_norm", 1024), ("loss_sum", 1))
SMALL_ROWS = 1152


def _params(sem=("arbitrary",)):
    return pltpu.CompilerParams(dimension_semantics=sem, vmem_limit_bytes=VMEM_LIMIT)


def _dot(a, b):
    return jnp.dot(a, b, preferred_element_type=F32)


def _dot_nt(a, b):
    return lax.dot_general(a, b, (((1,), (1,)), ((), ())), preferred_element_type=F32)


def _dot_tn(a, b):
    return lax.dot_general(a, b, (((0,), (0,)), ((), ())), preferred_element_type=F32)


def _rms(z):
    return lax.rsqrt(jnp.mean(z * z, axis=-1, keepdims=True) + RMS_EPS)


def _rms_bwd(z, gain, d):
    r = _rms(z)
    n = z * r
    dn = d * gain
    dz = r * (dn - n * jnp.mean(dn * n, axis=-1, keepdims=True))
    return dz, jnp.sum(d * n, axis=0, keepdims=True)


def _gelu(z):
    return 0.5 * z * (1.0 + lax.erf(z * np.float32(1.0 / np.sqrt(2.0))))


def _gelu_grad(z):
    cdf = 0.5 * (1.0 + lax.erf(z * np.float32(1.0 / np.sqrt(2.0))))
    return cdf + z * jnp.exp(-0.5 * z * z) * np.float32(1.0 / np.sqrt(2.0 * np.pi))


def _rot_tables(pos_col, invf, ma, mb):
    ang = pos_col.astype(F32) * invf
    s = jnp.sin(ang)
    return jnp.cos(ang), s * ma, s * mb


def _rot(t, c, sa, sb):
    return t * c + pltpu.roll(t, 120, 1) * sa + pltpu.roll(t, 8, 1) * sb


def _rot_t(d, c, sa, sb):
    return d * c + pltpu.roll(d * sa, 8, 1) + pltpu.roll(d * sb, 120, 1)


def _rot_consts():
    lane = np.arange(128) % HEAD_DIM
    inv_freq = (np.float32(ROPE_THETA) ** (-np.arange(0, ROT_DIM, 2, dtype=np.float32) / np.float32(ROT_DIM))).astype(np.float32)
    invf = np.where(lane < ROT_DIM, inv_freq[lane % ROT_HALF], 0.0).astype(np.float32)
    ma = np.where(lane < ROT_HALF, -1.0, 0.0).astype(np.float32)
    mb = np.where((lane >= ROT_HALF) & (lane < ROT_DIM), 1.0, 0.0).astype(np.float32)
    return jnp.asarray(invf[None]), jnp.asarray(ma[None]), jnp.asarray(mb[None])


def _row_spec(tm, w):
    return pl.BlockSpec((tm, w), lambda i: (i, 0))


def _full_spec(shape):
    return pl.BlockSpec(shape, lambda i: (0,) * len(shape))


def _weight_spec(shape):
    return pl.BlockSpec(shape, lambda i: (0,) * len(shape), pipeline_mode=pl.Buffered(1))


RES = 16


def _residue_scratch(n_arrays, tm, width):
    return [pltpu.VMEM((2, n_arrays, tm // RES, RES, width), F32), pltpu.SemaphoreType.DMA((2, n_arrays, RES))]


def _to_residue_rows(tiles, outs, scratch, sems, tm, n_steps):
    i = pl.program_id(0)
    slot = i % 2
    per = tm // RES

    def copies(step, s):
        return [pltpu.make_async_copy(scratch.at[s, a, :, b, :],
                                      outs[a].at[pl.ds(pl.multiple_of(b * (SEQ // RES) + per * step, per), per), :],
                                      sems.at[s, a, b]) for a in range(len(outs)) for b in range(RES)]

    @pl.when(i >= 2)
    def _():
        for cp in copies(i - 2, slot):
            cp.wait()

    for a, tile in enumerate(tiles):
        scratch[slot, a] = tile.reshape(per, RES, tile.shape[-1])
    for cp in copies(i, slot):
        cp.start()

    @pl.when(i == n_steps - 1)
    def _():
        for cp in copies(i - 1, 1 - slot) + copies(i, slot):
            cp.wait()


def in_proj(x, pos_col, g1, w_in_t, rot):
    tm = 512
    n_steps = SEQ // tm

    def body(x_ref, pos_ref, g_ref, w_ref, invf_ref, ma_ref, mb_ref, h_ref, u_ref, vs_ref, q_ref, k_ref, v_ref, scratch, sems):
        xf = x_ref[...]
        h = (xf * _rms(xf) * g_ref[...]).astype(BF16)
        h_ref[...] = h
        proj = _dot_nt(h, w_ref[...])
        c, sa, sb = _rot_tables(pos_ref[...], invf_ref[...], ma_ref[...], mb_ref[...])
        slabs = range(ATTN_W // 128)
        q = jnp.concatenate([_rot(proj[:, j * 128:(j + 1) * 128], c, sa, sb) * Q_SCALE for j in slabs], axis=1)
        k = jnp.concatenate([_rot(proj[:, ATTN_W + j * 128:ATTN_W + (j + 1) * 128], c, sa, sb) for j in slabs], axis=1)
        u_ref[...] = proj[:, 3 * ATTN_W:3 * ATTN_W + SGU_W]
        vs_ref[...] = proj[:, 3 * ATTN_W + SGU_W:]
        _to_residue_rows([q, k, proj[:, 2 * ATTN_W:3 * ATTN_W]], [q_ref, k_ref, v_ref], scratch, sems, tm, n_steps)

    act = jax.ShapeDtypeStruct((SEQ, 512), F32)
    return _call(
        "in_proj", body, n_steps,
        [_row_spec(tm, D_MODEL), _row_spec(tm, 1), _full_spec((1, D_MODEL)), _weight_spec((IN_W, D_MODEL)),
         _full_spec((1, 128)), _full_spec((1, 128)), _full_spec((1, 128))],
        [_row_spec(tm, D_MODEL)] + [_row_spec(tm, 512)] * 2 + [ANY] * 3,
        [jax.ShapeDtypeStruct((SEQ, D_MODEL), BF16)] + [act] * 5,
        (x, pos_col, g1, w_in_t, *rot), scratch_shapes=_residue_scratch(3, tm, ATTN_W))


def _to_residue_order(t):
    return t.reshape(SEQ // RES, RES, -1).transpose(1, 0, 2).reshape(t.shape)


def _from_residue_order(t):
    return t.reshape(RES, SEQ // RES, -1).transpose(1, 0, 2).reshape(t.shape)


def _block_rows(d, r, n):
    if d == 16:
        slices = [(128 * r, 128)]
    elif d == 4:
        slices = [(128 * (4 * b + r) + 32 * n, 32) for b in range(4)]
    else:
        slices = [(128 * b + 8 * n, 8) for b in range(RES)]
    return [(s if isinstance(s, int) else pl.multiple_of(s, z), z) for s, z in slices]


def _block_step(d, i):
    if d == 16:
        return i
    if d == 4:
        return 4 * (i & 31) + (i >> 5)
    return 16 * (i & 7) + (i >> 3)


def _attn_masks(d):
    row2 = _block_step(d, lax.broadcasted_iota(jnp.int32, (128, 256), 0))
    col2 = lax.broadcasted_iota(jnp.int32, (128, 256), 1)
    key2 = _block_step(d, col2 & 127)
    mask2 = jnp.logical_or(jnp.logical_and(col2 < 128, key2 >= row2), jnp.logical_and(col2 >= 128, key2 <= row2))
    row1 = _block_step(d, lax.broadcasted_iota(jnp.int32, (128, 128), 0))
    col1 = lax.broadcasted_iota(jnp.int32, (128, 128), 1)
    return col1 < HEAD_DIM, _block_step(d, col1) <= row1, mask2


def _load_rows(ref, slices):
    parts = [ref[pl.ds(s, z), :] for s, z in slices]
    return parts[0] if len(parts) == 1 else jnp.concatenate(parts, axis=0)


def _for_each_group(fn):
    for p, d in enumerate(DILATIONS):
        masks = _attn_masks(d)
        if d == 16:
            def group(i, carry, p=p, masks=masks):
                fn(p, masks, [(_block_rows(16, 4 * i + g, 0), None) for g in range(4)])
                return carry

            lax.fori_loop(0, 4, group, 0)
        elif d == 4:
            fn(p, masks, [(_block_rows(4, r, 0), None) for r in range(4)])

            def group(i, carry, p=p, masks=masks):
                fn(p, masks, [(_block_rows(4, r, i + 1), _block_rows(4, r, i)) for r in range(4)])
                return carry

            lax.fori_loop(0, 3, group, 0)
        else:
            fn(p, masks, [(_block_rows(1, 0, 0), None)])

            def group(i, carry, p=p, masks=masks):
                fn(p, masks, [(_block_rows(1, 0, 3 * i + g + 1), _block_rows(1, 0, 3 * i + g)) for g in range(3)])
                return carry

            lax.fori_loop(0, 5, group, 0)


def attn_fwd(q, k, v):
    def body(q_ref, k_ref, v_ref, o_ref, lse_ref, nat_ref, op_ref, lp_ref, sems):
        def group(p, masks, blocks):
            head0, mask1, mask2 = masks
            heads = (head0, jnp.logical_not(head0))
            keys = [rows if prev is None else prev + rows for rows, prev in blocks]
            mask = [mask1 if prev is None else mask2 for _, prev in blocks]
            qb = [_load_rows(q_ref, rows) for rows, _ in blocks]
            kk = [_load_rows(k_ref, ks).astype(BF16) for ks in keys]
            vv = [_load_rows(v_ref, ks).astype(BF16) for ks in keys]
            chains = [(g, hm) for g in range(len(blocks)) for hm in heads]
            s = [jnp.where(mask[g], _dot_nt(jnp.where(hm, qb[g], 0.0).astype(BF16), kk[g]), NEG) for g, hm in chains]
            m = [jnp.max(t, axis=-1, keepdims=True) for t in s]
            e = [jnp.exp(t - mt) for t, mt in zip(s, m)]
            l = [jnp.sum(t, axis=-1, keepdims=True) for t in e]
            pv = [_dot(t.astype(BF16), vv[g]) for t, (g, _) in zip(e, chains)]
            for g, (rows, _) in enumerate(blocks):
                o_blk = jnp.where(head0, pv[2 * g] / l[2 * g], pv[2 * g + 1] / l[2 * g + 1])
                l_blk = jnp.where(head0, jnp.broadcast_to(m[2 * g] + jnp.log(l[2 * g]), (128, 128)),
                                  jnp.broadcast_to(m[2 * g + 1] + jnp.log(l[2 * g + 1]), (128, 128)))
                at = 0
                for start, size in rows:
                    op_ref[p, pl.ds(start, size), :] = o_blk[at:at + size]
                    lp_ref[p, pl.ds(start, size), :] = l_blk[at:at + size]
                    at += size

        _for_each_group(group)

        def combine(i, carry):
            rows = pl.ds(pl.multiple_of(i * 256, 256), 256)
            ls = [lp_ref[p, rows, :] for p in range(3)]
            m = jnp.maximum(jnp.maximum(ls[0], ls[1]), ls[2])
            lse = m + jnp.log(jnp.exp(ls[0] - m) + jnp.exp(ls[1] - m) + jnp.exp(ls[2] - m))
            o = jnp.zeros((256, 128), F32)
            for p in range(3):
                o = o + jnp.exp(ls[p] - lse) * op_ref[p, rows, :]
            o_ref[rows, :] = o
            lse_ref[rows, :] = lse
            return carry

        lax.fori_loop(0, SEQ // 256, combine, 0)

        lanes = pl.ds(pl.multiple_of(pl.program_id(0) * 128, 128), 128)
        back = [pltpu.make_async_copy(o_ref.at[pl.ds(b * (SEQ // RES), SEQ // RES), :], nat_ref.at[:, b, lanes], sems.at[b])
                for b in range(RES)]
        for cp in back:
            cp.start()
        for cp in back:
            cp.wait()

    slab = pl.BlockSpec((SEQ, 128), lambda i: (0, i))
    out = jax.ShapeDtypeStruct((SEQ, ATTN_W), F32)
    attn_r, lse, attn = _call(
        "attn_fwd", body, ATTN_W // 128, [slab] * 3, [slab] * 2 + [ANY],
        [out, out, jax.ShapeDtypeStruct((SEQ // RES, RES, ATTN_W), F32)], (q, k, v),
        scratch_shapes=[pltpu.VMEM((3, SEQ, 128), F32), pltpu.VMEM((3, SEQ, 128), F32), pltpu.SemaphoreType.DMA((RES,))])
    return attn_r, lse, attn.reshape(SEQ, ATTN_W)


def _causal_weights(w_ref):
    row = lax.broadcasted_iota(jnp.int32, (CHUNK, CHUNK), 0)
    col = lax.broadcasted_iota(jnp.int32, (CHUNK, CHUNK), 1)
    return [jnp.where(col <= row, w_ref[g], 0.0).astype(BF16) for g in range(N_GROUPS)], col <= row


def _sgu_chunk_fwd(u, vs, lg, lb, wc, bfull, head0):
    ug = _gelu(u)
    vg = _gelu(vs)
    xc = vg - jnp.mean(vg, axis=-1, keepdims=True)
    rstd = lax.rsqrt(jnp.mean(xc * xc, axis=-1, keepdims=True) + LN_EPS)
    xhat = xc * rstd
    vn = xhat * lg + lb
    mixed = []
    for gp in range(SGU_W // 128):
        vp = vn[:, gp * 128:(gp + 1) * 128].astype(BF16)
        mixed.append(jnp.where(head0, _dot(wc[2 * gp], vp), _dot(wc[2 * gp + 1], vp)))
    ms = jnp.concatenate(mixed, axis=1) + bfull
    return ug, xhat, rstd, vn, ms


def sgu_fwd(u, vs, lg, lb, w_sp, bfull):
    cpb = 4

    def body(u_ref, vs_ref, lg_ref, lb_ref, w_ref, b_ref, o_ref):
        wc, _ = _causal_weights(w_ref)
        head0 = lax.broadcasted_iota(jnp.int32, (CHUNK, 128), 1) < HEAD_DIM
        for ci in range(cpb):
            rows = pl.ds(ci * CHUNK, CHUNK)
            ug, _, _, _, ms = _sgu_chunk_fwd(u_ref[rows, :], vs_ref[rows, :], lg_ref[...], lb_ref[...], wc, b_ref[...], head0)
            o_ref[rows, :] = ug * ms

    tm = cpb * CHUNK
    return _call(
        "sgu_fwd", body, SEQ // tm,
        [_row_spec(tm, SGU_W), _row_spec(tm, SGU_W), _full_spec((1, SGU_W)), _full_spec((1, SGU_W)),
         _full_spec((N_GROUPS, CHUNK, CHUNK)), _full_spec((CHUNK, SGU_W))],
        [_row_spec(tm, SGU_W)], [jax.ShapeDtypeStruct((SEQ, SGU_W), F32)],
        (u, vs, lg, lb, w_sp, bfull))


def out_proj(attn, sgu, x, ga, gs, w_out, gpm, gpf):
    tm = 512

    def body(a_ref, s_ref, x_ref, ga_ref, gs_ref, w_ref, gpm_ref, gpf_ref, mix_ref, y_ref, x2_ref, h2_ref):
        a = a_ref[...]
        s = s_ref[...]
        an = (a * _rms(a) * ga_ref[...]).astype(BF16)
        sn = (s * _rms(s) * gs_ref[...]).astype(BF16)
        mix_ref[:, :ATTN_W] = an
        mix_ref[:, ATTN_W:] = sn
        y = _dot(an, w_ref[:ATTN_W, :]) + _dot(sn, w_ref[ATTN_W:, :])
        y_ref[...] = y
        x2 = x_ref[...] + y * _rms(y) * gpm_ref[...]
        x2_ref[...] = x2
        h2_ref[...] = (x2 * _rms(x2) * gpf_ref[...]).astype(BF16)

    wide = jax.ShapeDtypeStruct((SEQ, D_MODEL), F32)
    wide16 = jax.ShapeDtypeStruct((SEQ, D_MODEL), BF16)
    return _call(
        "out_proj", body, SEQ // tm,
        [_row_spec(tm, ATTN_W), _row_spec(tm, SGU_W), _row_spec(tm, D_MODEL), _full_spec((1, ATTN_W)),
         _full_spec((1, SGU_W)), _weight_spec((D_MODEL, D_MODEL)), _full_spec((1, D_MODEL)), _full_spec((1, D_MODEL))],
        [_row_spec(tm, D_MODEL)] * 4, [wide16, wide, wide, wide16],
        (attn, sgu, x, ga, gs, w_out, gpm, gpf))


def ffn_up(h2, w_gate_t, w_up_t):
    tm = 256

    def body(h_ref, wg_ref, wu_ref, g_ref, u_ref, a_ref):
        h = h_ref[...]
        g = _dot_nt(h, wg_ref[...])
        u = _dot_nt(h, wu_ref[...])
        g_ref[...] = g.astype(BF16)
        u_ref[...] = u.astype(BF16)
        a_ref[...] = (g * jax.nn.sigmoid(g) * u).astype(BF16)

    ff = jax.ShapeDtypeStruct((SEQ, D_FF), BF16)
    return _call(
        "ffn_up", body, SEQ // tm,
        [_row_spec(tm, D_MODEL), _weight_spec((D_FF, D_MODEL)), _weight_spec((D_FF, D_MODEL))],
        [_row_spec(tm, D_FF)] * 3, [ff, ff, jax.ShapeDtypeStruct((SEQ, D_FF), BF16)],
        (h2, w_gate_t, w_up_t))


def ffn_down_loss(act, w_down, x2, gpo, target):
    tm = 512

    def body(a_ref, w_ref, x2_ref, g_ref, t_ref, df_ref, dx3_ref, dg_ref, loss_ref):
        f = _dot(a_ref[...], w_ref[...])
        gain = g_ref[...]
        err = x2_ref[...] + f * _rms(f) * gain - t_ref[...]
        dx3 = err * np.float32(1.0 / D_MODEL)
        dx3_ref[...] = dx3
        df, dg = _rms_bwd(f, gain, dx3)
        df_ref[...] = df.astype(BF16)

        @pl.when(pl.program_id(0) == 0)
        def _():
            dg_ref[...] = jnp.zeros_like(dg_ref)
            loss_ref[...] = jnp.zeros_like(loss_ref)

        dg_ref[...] += dg
        loss_ref[...] += jnp.sum(err * err, axis=(0, 1), keepdims=True)

    return pl.pallas_call(
        body, name="ffn_down_loss", grid=(SEQ // tm,),
        in_specs=[_row_spec(tm, D_FF), _weight_spec((D_FF, D_MODEL)), _row_spec(tm, D_MODEL), _full_spec((1, D_MODEL)),
                  _row_spec(tm, D_MODEL)],
        out_specs=[_row_spec(tm, D_MODEL), _row_spec(tm, D_MODEL), _full_spec((1, D_MODEL)), _full_spec((1, 1))],
        out_shape=[jax.ShapeDtypeStruct((SEQ, D_MODEL), BF16), jax.ShapeDtypeStruct((SEQ, D_MODEL), F32),
                   jax.ShapeDtypeStruct((1, D_MODEL), F32), jax.ShapeDtypeStruct((1, 1), F32)],
        compiler_params=_params(),
    )(act, w_down, x2, gpo, target)


def ffn_bwd(df, w_down, gate, up, w_gate_t, w_up_t, x2, gpf, dx3, y, gpm, after=()):
    tm = 256

    def body(df_ref, wd_ref, g_ref, u_ref, wg_ref, wu_ref, x2_ref, gpf_ref, dx3_ref, y_ref, gpm_ref,
             dg_ref, du_ref, dx2_ref, dy_ref, dgpf_ref, dgpm_ref):
        dact = _dot_nt(df_ref[...], wd_ref[...])
        g = g_ref[...].astype(F32)
        s = jax.nn.sigmoid(g)
        dup = (dact * g * s).astype(BF16)
        dgate = (dact * u_ref[...].astype(F32) * (s * (1.0 + g * (1.0 - s)))).astype(BF16)
        du_ref[...] = dup
        dg_ref[...] = dgate
        dh2 = _dot(dgate, wg_ref[...]) + _dot(dup, wu_ref[...])
        dz, dgpf = _rms_bwd(x2_ref[...], gpf_ref[...], dh2)
        dx2 = dx3_ref[...] + dz
        dx2_ref[...] = dx2
        dy, dgpm = _rms_bwd(y_ref[...], gpm_ref[...], dx2)
        dy_ref[...] = dy.astype(BF16)

        @pl.when(pl.program_id(0) == 0)
        def _():
            dgpf_ref[...] = jnp.zeros_like(dgpf_ref)
            dgpm_ref[...] = jnp.zeros_like(dgpm_ref)

        dgpf_ref[...] += dgpf
        dgpm_ref[...] += dgpm

    vec = jax.ShapeDtypeStruct((1, D_MODEL), F32)
    ff16 = jax.ShapeDtypeStruct((SEQ, D_FF), BF16)
    return _call(
        "ffn_bwd", body, SEQ // tm,
        [_row_spec(tm, D_MODEL), _weight_spec((D_FF, D_MODEL)), _row_spec(tm, D_FF), _row_spec(tm, D_FF),
         _weight_spec((D_FF, D_MODEL)), _weight_spec((D_FF, D_MODEL)), _row_spec(tm, D_MODEL), _full_spec((1, D_MODEL)),
         _row_spec(tm, D_MODEL), _row_spec(tm, D_MODEL), _full_spec((1, D_MODEL))],
        [_row_spec(tm, D_FF), _row_spec(tm, D_FF), _row_spec(tm, D_MODEL), _row_spec(tm, D_MODEL),
         _full_spec((1, D_MODEL)), _full_spec((1, D_MODEL))],
        [ff16, ff16, jax.ShapeDtypeStruct((SEQ, D_MODEL), F32), jax.ShapeDtypeStruct((SEQ, D_MODEL), BF16), vec, vec],
        (df, w_down, gate, up, w_gate_t, w_up_t, x2, gpf, dx3, y, gpm), after=after)


def weight_grad(name, a, b, after=()):
    m, n = a.shape[1], b.shape[1]
    tr = 256

    def body(a_ref, b_ref, o_ref):
        o_ref[...] = _dot_tn(a_ref[...], b_ref[...]).astype(BF16)

    (out,) = _call(
        name, body, m // tr, [pl.BlockSpec((SEQ, tr), lambda i: (0, i)), _weight_spec((SEQ, n))],
        [_row_spec(tr, n)], [jax.ShapeDtypeStruct((m, n), BF16)], (a, b), after=after)
    return out.reshape(N_DEV, m // N_DEV, n)


def mix_bwd(dy, w_out, attn, sgu, ga, gs, after=()):
    tm = 512
    n_steps = SEQ // tm

    def body(dy_ref, w_ref, a_ref, s_ref, ga_ref, gs_ref, ds_ref, dga_ref, dgs_ref, da_ref, scratch, sems):
        dy = dy_ref[...]
        da, dga = _rms_bwd(a_ref[...], ga_ref[...], _dot_nt(dy, w_ref[:ATTN_W, :]))
        ds, dgs = _rms_bwd(s_ref[...], gs_ref[...], _dot_nt(dy, w_ref[ATTN_W:, :]))
        ds_ref[...] = ds
        _to_residue_rows([da], [da_ref], scratch, sems, tm, n_steps)

        @pl.when(pl.program_id(0) == 0)
        def _():
            dga_ref[...] = jnp.zeros_like(dga_ref)
            dgs_ref[...] = jnp.zeros_like(dgs_ref)

        dga_ref[...] += dga
        dgs_ref[...] += dgs

    half = jax.ShapeDtypeStruct((SEQ, 512), F32)
    vec = jax.ShapeDtypeStruct((1, 512), F32)
    return _call(
        "mix_bwd", body, n_steps,
        [_row_spec(tm, D_MODEL), _weight_spec((D_MODEL, D_MODEL)), _row_spec(tm, 512), _row_spec(tm, 512),
         _full_spec((1, 512)), _full_spec((1, 512))],
        [_row_spec(tm, 512), _full_spec((1, 512)), _full_spec((1, 512)), ANY],
        [half, vec, vec, half], (dy, w_out, attn, sgu, ga, gs), scratch_shapes=_residue_scratch(1, tm, ATTN_W), after=after)


def sgu_bwd(u, vs, dsgu, lg, lb, w_sp, bfull):
    cpb = 4

    def body(u_ref, vs_ref, d_ref, lg_ref, lb_ref, w_ref, b_ref, du_ref, dvs_ref, dlg_ref, dlb_ref, dw_ref, db_ref):
        wc, causal = _causal_weights(w_ref)
        head0 = lax.broadcasted_iota(jnp.int32, (CHUNK, 128), 1) < HEAD_DIM
        lg = lg_ref[...]

        @pl.when(pl.program_id(0) == 0)
        def _():
            dlg_ref[...] = jnp.zeros_like(dlg_ref)
            dlb_ref[...] = jnp.zeros_like(dlb_ref)
            dw_ref[...] = jnp.zeros_like(dw_ref)
            db_ref[...] = jnp.zeros_like(db_ref)

        for ci in range(cpb):
            rows = pl.ds(ci * CHUNK, CHUNK)
            u = u_ref[rows, :]
            vs = vs_ref[rows, :]
            d = d_ref[rows, :]
            ug, xhat, rstd, vn, ms = _sgu_chunk_fwd(u, vs, lg, lb_ref[...], wc, b_ref[...], head0)
            du_ref[rows, :] = (d * ms * _gelu_grad(u)).astype(BF16)
            dms = d * ug
            db_ref[...] += dms
            dvn = []
            for gp in range(SGU_W // 128):
                dmp = dms[:, gp * 128:(gp + 1) * 128]
                dm0 = jnp.where(head0, dmp, 0.0).astype(BF16)
                dm1 = jnp.where(head0, 0.0, dmp).astype(BF16)
                vp = vn[:, gp * 128:(gp + 1) * 128].astype(BF16)
                dw_ref[2 * gp] += _dot_nt(dm0, vp)
                dw_ref[2 * gp + 1] += _dot_nt(dm1, vp)
                dvn.append(_dot_tn(wc[2 * gp], dm0) + _dot_tn(wc[2 * gp + 1], dm1))
            dvn = jnp.concatenate(dvn, axis=1)
            dlg_ref[...] += jnp.sum(dvn * xhat, axis=0, keepdims=True)
            dlb_ref[...] += jnp.sum(dvn, axis=0, keepdims=True)
            dxh = dvn * lg
            dvg = rstd * (dxh - jnp.mean(dxh, axis=-1, keepdims=True) - xhat * jnp.mean(dxh * xhat, axis=-1, keepdims=True))
            dvs_ref[rows, :] = (dvg * _gelu_grad(vs)).astype(BF16)

        @pl.when(pl.program_id(0) == pl.num_programs(0) - 1)
        def _():
            for g in range(N_GROUPS):
                dw_ref[g] = jnp.where(causal, dw_ref[g], 0.0)

    tm = cpb * CHUNK
    half16 = jax.ShapeDtypeStruct((SEQ, SGU_W), BF16)
    vec = jax.ShapeDtypeStruct((1, SGU_W), F32)
    return _call(
        "sgu_bwd", body, SEQ // tm,
        [_row_spec(tm, SGU_W)] * 3 + [_full_spec((1, SGU_W)), _full_spec((1, SGU_W)),
                                      _full_spec((N_GROUPS, CHUNK, CHUNK)), _full_spec((CHUNK, SGU_W))],
        [_row_spec(tm, SGU_W), _row_spec(tm, SGU_W), _full_spec((1, SGU_W)), _full_spec((1, SGU_W)),
         _full_spec((N_GROUPS, CHUNK, CHUNK)), _full_spec((CHUNK, SGU_W))],
        [half16, half16, vec, vec, jax.ShapeDtypeStruct((N_GROUPS, CHUNK, CHUNK), F32),
         jax.ShapeDtypeStruct((CHUNK, SGU_W), F32)],
        (u, vs, dsgu, lg, lb, w_sp, bfull))


def attn_bwd(q, k, v, o, lse, do, pos_col, rot):
    def body(q_ref, k_ref, v_ref, o_ref, lse_ref, do_ref, pos_ref, invf_ref, ma_ref, mb_ref,
             dq_ref, dk_ref, dv_ref, dqa_ref, dka_ref, dva_ref, dlt_ref, rot_ref):
        dqa_ref[...] = jnp.zeros_like(dqa_ref)
        dka_ref[...] = jnp.zeros_like(dka_ref)
        dva_ref[...] = jnp.zeros_like(dva_ref)

        def delta(i, carry):
            rows = pl.ds(pl.multiple_of(i * 256, 256), 256)
            prod = do_ref[rows, :] * o_ref[rows, :]
            h0 = lax.broadcasted_iota(jnp.int32, (256, 128), 1) < HEAD_DIM
            d0 = jnp.sum(jnp.where(h0, prod, 0.0), axis=-1, keepdims=True)
            d1 = jnp.sum(jnp.where(h0, 0.0, prod), axis=-1, keepdims=True)
            dlt_ref[rows, :] = jnp.where(h0, d0, d1)
            return carry

        lax.fori_loop(0, SEQ // 256, delta, 0)

        def add_rows(ref, slices, val):
            at = 0
            for start, size in slices:
                ref[pl.ds(start, size), :] += val[at:at + size]
                at += size

        def group(p, masks, blocks):
            head0, mask1, mask2 = masks
            heads = (head0, jnp.logical_not(head0))
            keys = [rows if prev is None else prev + rows for rows, prev in blocks]
            mask = [mask1 if prev is None else mask2 for _, prev in blocks]
            kk = [_load_rows(k_ref, ks).astype(BF16) for ks in keys]
            vv = [_load_rows(v_ref, ks).astype(BF16) for ks in keys]
            qb = [_load_rows(q_ref, rows) for rows, _ in blocks]
            dob = [_load_rows(do_ref, rows) for rows, _ in blocks]
            lse_b = [_load_rows(lse_ref, rows) for rows, _ in blocks]
            dlt_b = [_load_rows(dlt_ref, rows) for rows, _ in blocks]
            chains = [(g, h) for g in range(len(blocks)) for h in range(2)]
            qm = [jnp.where(heads[h], qb[g], 0.0).astype(BF16) for g, h in chains]
            dom = [jnp.where(heads[h], dob[g], 0.0).astype(BF16) for g, h in chains]
            s = [_dot_nt(qm[c], kk[g]) for c, (g, h) in enumerate(chains)]
            dp = [_dot_nt(dom[c], vv[g]) for c, (g, h) in enumerate(chains)]
            pr = [jnp.where(mask[g], jnp.exp(s[c] - lse_b[g][:, h * HEAD_DIM:h * HEAD_DIM + 1]), 0.0)
                  for c, (g, h) in enumerate(chains)]
            ds = [(pr[c] * (dp[c] - dlt_b[g][:, h * HEAD_DIM:h * HEAD_DIM + 1])).astype(BF16)
                  for c, (g, h) in enumerate(chains)]
            dv = [_dot_tn(pr[c].astype(BF16), dom[c]) for c in range(len(chains))]
            dk = [_dot_tn(ds[c], qm[c]) for c in range(len(chains))]
            dq = [_dot(ds[c], kk[g]) for c, (g, h) in enumerate(chains)]
            for g, (rows, _) in enumerate(blocks):
                add_rows(dqa_ref, rows, jnp.where(head0, dq[2 * g], dq[2 * g + 1]))
                add_rows(dka_ref, keys[g], dk[2 * g] + dk[2 * g + 1])
                add_rows(dva_ref, keys[g], dv[2 * g] + dv[2 * g + 1])

        _for_each_group(group)

        @pl.when(pl.program_id(0) == 0)
        def _():
            def tables(i, carry):
                rows = pl.ds(pl.multiple_of(i * 256, 256), 256)
                c, sa, sb = _rot_tables(pos_ref[rows, :], invf_ref[...], ma_ref[...], mb_ref[...])
                rot_ref[0, rows, :] = c
                rot_ref[1, rows, :] = sa
                rot_ref[2, rows, :] = sb
                return carry

            lax.fori_loop(0, SEQ // 256, tables, 0)

        def finish(i, carry):
            rows = pl.ds(pl.multiple_of(i * 256, 256), 256)
            c, sa, sb = rot_ref[0, rows, :], rot_ref[1, rows, :], rot_ref[2, rows, :]
            dq_ref[rows, :] = _rot_t(dqa_ref[rows, :] * Q_SCALE, c, sa, sb).astype(BF16)
            dk_ref[rows, :] = _rot_t(dka_ref[rows, :], c, sa, sb).astype(BF16)
            dv_ref[rows, :] = dva_ref[rows, :].astype(BF16)
            return carry

        lax.fori_loop(0, SEQ // 256, finish, 0)

    slab = pl.BlockSpec((SEQ, 128), lambda i: (0, i))
    out = jax.ShapeDtypeStruct((SEQ, ATTN_W), BF16)
    acc = pltpu.VMEM((SEQ, 128), F32)
    return _call(
        "attn_bwd", body, ATTN_W // 128,
        [slab] * 6 + [_full_spec((SEQ, 1)), _full_spec((1, 128)), _full_spec((1, 128)), _full_spec((1, 128))],
        [slab] * 3, [out, out, out], (q, k, v, o, lse, do, pos_col, *rot),
        scratch_shapes=[acc, acc, acc, acc, pltpu.VMEM((3, SEQ, 128), F32)])


def in_bwd(dproj, w_in_t, x, g1, dx2, after=()):
    tm = 512

    def body(dp_ref, w_ref, x_ref, g_ref, dx2_ref, dx_ref, dg_ref):
        dh1 = _dot(dp_ref[...], w_ref[...])
        dz, dg = _rms_bwd(x_ref[...], g_ref[...], dh1)
        dx_ref[...] = dx2_ref[...] + dz

        @pl.when(pl.program_id(0) == 0)
        def _():
            dg_ref[...] = jnp.zeros_like(dg_ref)

        dg_ref[...] += dg

    return _call(
        "in_bwd", body, SEQ // tm,
        [_row_spec(tm, IN_W), _weight_spec((IN_W, D_MODEL)), _row_spec(tm, D_MODEL), _full_spec((1, D_MODEL)),
         _row_spec(tm, D_MODEL)],
        [_row_spec(tm, D_MODEL), _full_spec((1, D_MODEL))],
        [jax.ShapeDtypeStruct((SEQ, D_MODEL), F32), jax.ShapeDtypeStruct((1, D_MODEL), F32)],
        (dproj, w_in_t, x, g1, dx2), after=after)


def _coords():
    return lax.axis_index("x"), lax.axis_index("y"), lax.axis_index("c")


class Exchange:
    def __init__(self, srcs, bufs, new_shapes, n_sems, make):
        self.srcs, self.bufs, self.new_shapes, self.n_sems, self.make = list(srcs), list(bufs), list(new_shapes), n_sems, make


def _call(name, body, n_steps, in_specs, out_specs, out_shape, args, scratch_shapes=(), after=()):
    n_in = len(args)

    def wrapped(*refs):
        body(*refs[:n_in], *refs[n_in + len(after):])

    return list(pl.pallas_call(
        wrapped, name=name, grid=(n_steps,), in_specs=list(in_specs) + [ANY] * len(after), out_specs=list(out_specs),
        out_shape=list(out_shape), scratch_shapes=list(scratch_shapes), compiler_params=_params(),
    )(*args, *after))


GATHER_SEMS = 8


def gather(bufs):
    n = len(bufs)

    def make(src_refs, buf_refs, new_refs, send_sems, recv_sems):
        x, y, c = _coords()
        me, sibling = (x, y, c), (x, y, 1 - c)
        over_x, over_y, across = (1 - x, y), (x, 1 - y), (1 - x, 1 - y)

        def copy(a, k, block, to, half=None):
            r = buf_refs[a].shape[0] // N_DEV
            lo, size = (0, r) if half is None else (half * (r // 2), r // 2)
            rows = buf_refs[a].at[pl.ds((4 * block[0] + 2 * block[1] + block[2]) * r + lo, size), :]
            return pltpu.make_async_remote_copy(
                src_ref=rows, dst_ref=rows, send_sem=send_sems.at[GATHER_SEMS * a + k],
                recv_sem=recv_sems.at[GATHER_SEMS * a + k], device_id=to, device_id_type=MESH)

        every = range(n)
        out = ([copy(a, 0, me, sibling) for a in every] + [copy(a, 1, me, (*over_x, c)) for a in every]
               + [copy(a, 2, me, (*over_y, c)) for a in every])
        near_in = [copy(a, 1, (*over_x, c), me) for a in every] + [copy(a, 2, (*over_y, c), me) for a in every]
        relay = ([copy(a, 3, (*over_x, c), (*over_y, c), half=0) for a in every]
                 + [copy(a, 4, (*over_y, c), (*over_x, c), half=1) for a in every])
        near_on = [copy(a, 5, (*over_x, c), sibling) for a in every] + [copy(a, 6, (*over_y, c), sibling) for a in every]
        relay_in = ([copy(a, 3, (*across, c), me, half=0) for a in every]
                    + [copy(a, 4, (*across, c), me, half=1) for a in every])
        far_on = [copy(a, 7, (*across, c), sibling) for a in every]
        from_core = ([copy(a, 0, sibling, me) for a in every] + [copy(a, 5, (*over_x, 1 - c), me) for a in every]
                     + [copy(a, 6, (*over_y, 1 - c), me) for a in every] + [copy(a, 7, (*across, 1 - c), me) for a in every])
        stages = [([], out), (near_in, relay + near_on), (relay_in, far_on)]
        return stages, out + relay + near_on + far_on, from_core

    return Exchange([], bufs, [], GATHER_SEMS * n, make)


TO_GATHER = (1, lambda x, y, c: [(x, y, 1 - c), (1 - x, y, c), (x, 1 - y, c)])
TO_SIBLING = (2, lambda x, y, c: [(x, y, 1 - c)])
TO_CHIPS = (3, lambda x, y, c: [(1 - x, y, c), (x, 1 - y, c), (1 - x, 1 - y, c)])
TO_ALL = (4, lambda x, y, c: [(x ^ (m >> 2), y ^ ((m >> 1) & 1), c ^ (m & 1)) for m in range(1, N_DEV)])


def by_sequencer(name, exchanges, who):
    collective_id, peers_of = who
    hbm = pltpu.MemorySpace.HBM
    refs = [([jax.new_ref(a, memory_space=hbm) for a in ex.srcs], [jax.new_ref(a, memory_space=hbm) for a in ex.bufs],
             [jax.empty_ref(s, memory_space=hbm) for s in ex.new_shapes]) for ex in exchanges]
    sems = []
    for ex in exchanges:
        sems += [pltpu.SemaphoreType.DMA((ex.n_sems,)), pltpu.SemaphoreType.DMA((ex.n_sems,))]

    @pl.kernel(mesh=plsc.ScalarSubcoreMesh(axis_name="sequencer", num_cores=1), name=name, scratch_types=tuple(sems),
               compiler_params=pltpu.CompilerParams(collective_id=collective_id))
    def launch(*sem_refs):
        peers = peers_of(*_coords())
        barrier = pltpu.get_barrier_semaphore()
        for peer in peers:
            pl.semaphore_signal(barrier, inc=1, device_id=peer, device_id_type=MESH)
        pl.semaphore_wait(barrier, len(peers))

        made = [ex.make(*refs[k], sem_refs[2 * k], sem_refs[2 * k + 1]) for k, ex in enumerate(exchanges)]
        for stage in range(max(len(stages) for stages, _, _ in made)):
            for stages, _, _ in made:
                if stage < len(stages):
                    arrivals, starts = stages[stage]
                    for cp in arrivals:
                        cp.wait_recv()
                    for cp in starts:
                        cp.start()
        for _, sends, arrivals in made:
            for cp in arrivals:
                cp.wait_recv()
            for cp in sends:
                cp.wait_send()

    launch()
    return [([ref[...] for ref in bufs], [ref[...] for ref in news]) for _, bufs, news in refs]


def place_shards(name, shards, dev):
    n = len(shards)

    def body(dev_ref, *refs):
        for a in range(n):
            refs[n + a][...] = refs[a][...].astype(BF16)

    spec = pltpu.PrefetchScalarGridSpec(
        num_scalar_prefetch=1, grid=(1,),
        in_specs=[pl.BlockSpec(s.shape, lambda i, dev_ref: (0, 0)) for s in shards],
        out_specs=[pl.BlockSpec(s.shape, lambda i, dev_ref: (dev_ref[0], 0)) for s in shards])
    return pl.pallas_call(
        body, name=name, grid_spec=spec,
        out_shape=[jax.ShapeDtypeStruct((N_DEV * s.shape[0], s.shape[1]), BF16) for s in shards],
        compiler_params=_params(),
    )(dev, *shards)


def _swap(copies_of):
    def make(src_refs, buf_refs, new_refs, send_sems, recv_sems):
        copies = copies_of(src_refs, new_refs, send_sems, recv_sems)
        return [([], copies)], copies, copies

    return make


def to_sibling(grads):
    def copies_of(src_refs, new_refs, send_sems, recv_sems):
        x, y, c = _coords()
        return [pltpu.make_async_remote_copy(
            src_ref=src_refs[a].at[2 * xy + 1 - c], dst_ref=new_refs[a].at[xy], send_sem=send_sems.at[4 * a + xy],
            recv_sem=recv_sems.at[4 * a + xy], device_id=(x, y, 1 - c), device_id_type=MESH)
            for a in range(len(src_refs)) for xy in range(4)]

    return Exchange(grads, [], [jax.ShapeDtypeStruct((4,) + g.shape[1:], g.dtype) for g in grads], 4 * len(grads),
                    _swap(copies_of))


def to_chips(parts):
    def copies_of(src_refs, new_refs, send_sems, recv_sems):
        x, y, c = _coords()
        chips = [(1 - x, y), (x, 1 - y), (1 - x, 1 - y)]
        return [pltpu.make_async_remote_copy(
            src_ref=src_refs[a].at[2 * px + py], dst_ref=new_refs[a].at[2 * x + y], send_sem=send_sems.at[3 * a + j],
            recv_sem=recv_sems.at[3 * a + j], device_id=(px, py, c), device_id_type=MESH)
            for a in range(len(src_refs)) for j, (px, py) in enumerate(chips)]

    return Exchange(parts, [], [jax.ShapeDtypeStruct(p.shape, p.dtype) for p in parts], 3 * len(parts), _swap(copies_of))


def to_owners(grad):
    def copies_of(src_refs, new_refs, send_sems, recv_sems):
        x, y, c = _coords()
        copies = []
        for m in range(1, N_DEV):
            px, py, pc = x ^ (m >> 2), y ^ ((m >> 1) & 1), c ^ (m & 1)
            copies.append(pltpu.make_async_remote_copy(
                src_ref=src_refs[0].at[4 * px + 2 * py + pc], dst_ref=new_refs[0].at[4 * x + 2 * y + c],
                send_sem=send_sems.at[m - 1], recv_sem=recv_sems.at[m - 1], device_id=(px, py, pc), device_id_type=MESH))
        return copies

    return Exchange([grad], [], [jax.ShapeDtypeStruct(grad.shape, grad.dtype)], N_DEV - 1, _swap(copies_of))


def sum_cores(name, grad, other, core, after=()):
    _, r, w = other.shape

    def body(core_ref, g_ref, o_ref, *rest):
        rest[-1][...] = (g_ref[...].astype(F32) + o_ref[...].astype(F32)).astype(rest[-1].dtype)

    return pl.pallas_call(
        body, name=name,
        grid_spec=pltpu.PrefetchScalarGridSpec(
            num_scalar_prefetch=1, grid=(4,),
            in_specs=[pl.BlockSpec((1, r, w), lambda i, core_ref: (2 * i + core_ref[0], 0, 0)),
                      pl.BlockSpec((1, r, w), lambda i, core_ref: (i, 0, 0))] + [ANY] * len(after),
            out_specs=pl.BlockSpec((1, r, w), lambda i, core_ref: (i, 0, 0))),
        out_shape=jax.ShapeDtypeStruct(other.shape, other.dtype),
        compiler_params=_params(),
    )(core, grad, other, *after)


def sum_owned(name, grad, others, dev_ids, after=()):
    _, r, w = grad.shape

    def body(ids_ref, *refs):
        acc = refs[0][0]
        for k in range(1, N_DEV):
            acc = acc + refs[k][0]
        refs[-1][...] = acc

    def pick(k):
        return pl.BlockSpec((1, r, w), lambda i, ids_ref: (ids_ref[k], 0, 0))

    return pl.pallas_call(
        body, name=name,
        grid_spec=pltpu.PrefetchScalarGridSpec(
            num_scalar_prefetch=1, grid=(1,), in_specs=[pick(k) for k in range(N_DEV)] + [ANY] * len(after),
            out_specs=pl.BlockSpec((r, w), lambda i, ids_ref: (ids_ref[0], 0))),
        out_shape=jax.ShapeDtypeStruct((N_DEV * r, w), F32),
        compiler_params=_params(),
    )(dev_ids, grad, *([others] * (N_DEV - 1)), *after)


def _adamw_update(w, g, m, v):
    nm = ADAM_B1 * m + np.float32(1.0 - ADAM_B1) * g
    nv = ADAM_B2 * v + np.float32(1.0 - ADAM_B2) * (g * g)
    m_hat = nm / np.float32(1.0 - ADAM_B1 ** ADAM_STEP)
    v_hat = nv / np.float32(1.0 - ADAM_B2 ** ADAM_STEP)
    return -ADAM_LR * (m_hat / (jnp.sqrt(v_hat) + ADAM_EPS) + ADAM_WD * w), nm, nv


def adamw_of_sums(name, part, others, chip_ids, w, m, v, after):
    _, r, wd = part.shape
    halves = 2
    rows = r // halves

    def body(ids_ref, p_ref, a_ref, b_ref, c_ref, w_ref, m_ref, v_ref, after_ref, g_ref, d_ref, nm_ref, nv_ref):
        g = ((p_ref[0].astype(F32) + a_ref[0].astype(F32)) + b_ref[0].astype(F32)) + c_ref[0].astype(F32)
        g_ref[...] = g
        d_ref[...], nm_ref[...], nv_ref[...] = _adamw_update(w_ref[...], g, m_ref[...], v_ref[...])

    def pick(k):
        return pl.BlockSpec((1, rows, wd), lambda i, ids_ref: (ids_ref[k], i, 0))

    whole = pl.BlockSpec((rows, wd), lambda i, ids_ref: (i, 0))
    shape = jax.ShapeDtypeStruct((r, wd), F32)
    return pl.pallas_call(
        body, name=name,
        grid_spec=pltpu.PrefetchScalarGridSpec(
            num_scalar_prefetch=1, grid=(halves,), in_specs=[pick(0), pick(1), pick(2), pick(3), whole, whole, whole, ANY],
            out_specs=[whole] * 4),
        out_shape=[shape] * 4,
        compiler_params=_params(),
    )(chip_ids, part, others, others, others, w, m, v, after)


def adamw(name, w, g, m, v):
    def body(w_ref, g_ref, m_ref, v_ref, d_ref, nm_ref, nv_ref):
        d_ref[...], nm_ref[...], nv_ref[...] = _adamw_update(w_ref[...], g_ref[...], m_ref[...], v_ref[...])

    shape = jax.ShapeDtypeStruct(w.shape, F32)
    spec = _full_spec(w.shape)
    return pl.pallas_call(
        body, name=name, grid=(1,), in_specs=[spec] * 4, out_specs=[spec] * 3, out_shape=[shape] * 3,
        compiler_params=_params(),
    )(w, g, m, v)


def _pack_small(parts):
    flat = jnp.concatenate([parts[name].reshape(-1) for name, _ in SMALL])
    return jnp.pad(flat, (0, SMALL_ROWS * 128 - flat.shape[0])).reshape(SMALL_ROWS, 128)


def _unpack_small(packed, like):
    flat = packed.reshape(-1)
    out, at = {}, 0
    for name, size in SMALL:
        out[name] = flat[at:at + size].reshape(like[name].shape)
        at += size
    return out


def kernel(x, positions, pre_mix_norm, w_in, sgu_ln_gain, sgu_ln_bias, sgu_w_spatial, sgu_b_spatial, attn_out_norm, sgu_out_norm, w_out, post_mix_norm, pre_ffn_norm, w_gate, w_up, w_down, post_ffn_norm, loss_target, m_pre_mix_norm, m_w_in, m_sgu_ln_gain, m_sgu_ln_bias, m_sgu_w_spatial, m_sgu_b_spatial, m_attn_out_norm, m_sgu_out_norm, m_w_out, m_post_mix_norm, m_pre_ffn_norm, m_w_gate, m_w_up, m_w_down, m_post_ffn_norm, v_pre_mix_norm, v_w_in, v_sgu_ln_gain, v_sgu_ln_bias, v_sgu_w_spatial, v_sgu_b_spatial, v_attn_out_norm, v_sgu_out_norm, v_w_out, v_post_mix_norm, v_pre_ffn_norm, v_w_gate, v_w_up, v_w_down, v_post_ffn_norm):
    small_w = dict(pre_mix_norm=pre_mix_norm, sgu_ln_gain=sgu_ln_gain, sgu_ln_bias=sgu_ln_bias, sgu_w_spatial=sgu_w_spatial,
                   sgu_b_spatial=sgu_b_spatial, attn_out_norm=attn_out_norm, sgu_out_norm=sgu_out_norm,
                   post_mix_norm=post_mix_norm, pre_ffn_norm=pre_ffn_norm, post_ffn_norm=post_ffn_norm)
    small_m = dict(pre_mix_norm=m_pre_mix_norm, sgu_ln_gain=m_sgu_ln_gain, sgu_ln_bias=m_sgu_ln_bias, sgu_w_spatial=m_sgu_w_spatial,
                   sgu_b_spatial=m_sgu_b_spatial, attn_out_norm=m_attn_out_norm, sgu_out_norm=m_sgu_out_norm,
                   post_mix_norm=m_post_mix_norm, pre_ffn_norm=m_pre_ffn_norm, post_ffn_norm=m_post_ffn_norm)
    small_v = dict(pre_mix_norm=v_pre_mix_norm, sgu_ln_gain=v_sgu_ln_gain, sgu_ln_bias=v_sgu_ln_bias, sgu_w_spatial=v_sgu_w_spatial,
                   sgu_b_spatial=v_sgu_b_spatial, attn_out_norm=v_attn_out_norm, sgu_out_norm=v_sgu_out_norm,
                   post_mix_norm=v_post_mix_norm, pre_ffn_norm=v_pre_ffn_norm, post_ffn_norm=v_post_ffn_norm)
    for table in (small_w, small_m, small_v):
        table["loss_sum"] = jnp.zeros((1,), F32)

    x2d = x[0]
    target = loss_target[0]
    pos_col = positions.reshape(SEQ, 1)
    rot = _rot_consts()
    w_sp = sgu_w_spatial[0]
    bfull = jnp.repeat(sgu_b_spatial[0].T, HEAD_DIM, axis=1)

    x_i, y_i, c_i = (lax.axis_index(a).astype(jnp.int32) for a in MESH_AXES)
    dev = 4 * x_i + 2 * y_i + c_i
    core = c_i.reshape(1)
    chip = 2 * x_i + y_i
    chip_ids = jnp.stack([chip, chip ^ 1, chip ^ 2, chip ^ 3])
    dev_ids = jnp.stack([dev ^ m for m in range(N_DEV)])

    def gathered(name, bufs):
        return by_sequencer(name, [gather(bufs)], TO_GATHER)[0][0]

    def from_sibling(name, grads):
        return by_sequencer(name, [to_sibling(grads)], TO_SIBLING)[0][1]

    def from_chips(name, parts):
        return by_sequencer(name, [to_chips(parts)], TO_CHIPS)[0][1]

    (w_in_t,) = place_shards("place_w_in", [w_in[0].T], dev.reshape(1))
    (w_in_t,) = gathered("gather_w_in", [w_in_t])
    w_gate_t, w_up_t, w_out_f, w_down_f = place_shards(
        "place_weights", [w_gate[0].T, w_up[0].T, w_out[0], w_down[0]], dev.reshape(1))
    (w_out_f,) = gathered("gather_w_out", [w_out_f])
    w_gate_t, w_up_t = gathered("gather_w_gate_up", [w_gate_t, w_up_t])
    (w_down_f,) = gathered("gather_w_down", [w_down_f])

    h1, u, vs, q, k, v = in_proj(x2d, pos_col, pre_mix_norm, w_in_t, rot)
    attn_r, lse, attn = attn_fwd(q, k, v)
    (sgu,) = sgu_fwd(u, vs, sgu_ln_gain, sgu_ln_bias, w_sp, bfull)
    mix, y, x2, h2 = out_proj(attn, sgu, x2d, attn_out_norm, sgu_out_norm, w_out_f, post_mix_norm, pre_ffn_norm)
    gate, up, act = ffn_up(h2, w_gate_t, w_up_t)
    df, dx3, d_post_ffn, sq_err = ffn_down_loss(act, w_down_f, x2, post_ffn_norm, target)

    g_w_down = weight_grad("grad_w_down", act, df)
    (s_down,) = from_sibling("w_down_to_sibling", [g_w_down])
    dgate, dup, dx2, dy, d_pre_ffn, d_post_mix = ffn_bwd(
        df, w_down_f, gate, up, w_gate_t, w_up_t, x2, pre_ffn_norm, dx3, y, post_mix_norm, after=[g_w_down])
    p_down = sum_cores("sum_cores_down", g_w_down, s_down, core, after=[dy])
    (c_down,) = from_chips("w_down_to_chips", [p_down])
    g_w_gate = weight_grad("grad_w_gate", dgate, h2, after=[p_down])
    g_w_up = weight_grad("grad_w_up", dup, h2, after=[g_w_gate])
    s_gate, s_up = from_sibling("w_gate_up_to_sibling", [g_w_gate, g_w_up])
    g_w_out = weight_grad("grad_w_out", mix, dy, after=[g_w_up, c_down])
    p_gate = sum_cores("sum_cores_gate", g_w_gate, s_gate, core, after=[g_w_out])
    p_up = sum_cores("sum_cores_up", g_w_up, s_up, core, after=[g_w_out])
    c_gate, c_up = from_chips("w_gate_up_to_chips", [p_gate, p_up])
    (s_out,) = from_sibling("w_out_to_sibling", [g_w_out])
    dsgu, d_attn_out, d_sgu_out, dattn_r = mix_bwd(dy, w_out_f, attn, sgu, attn_out_norm, sgu_out_norm, after=[p_gate, p_up])
    du, dvs, d_ln_gain, d_ln_bias, d_w_sp, d_bfull = sgu_bwd(u, vs, dsgu, sgu_ln_gain, sgu_ln_bias, w_sp, bfull)
    dq, dk, dv = attn_bwd(q, k, v, attn_r, lse, dattn_r, _to_residue_order(pos_col), rot)
    p_out = sum_cores("sum_cores_out", g_w_out, s_out, core, after=[dq])
    (c_out,) = from_chips("w_out_to_chips", [p_out])
    dq, dk, dv = (_from_residue_order(t) for t in (dq, dk, dv))
    dproj = jnp.concatenate([dq, dk, dv, du, dvs], axis=1)
    g_w_in = weight_grad("grad_w_in", dproj, h1, after=[c_gate, c_up])
    (s_in,) = from_sibling("w_in_to_sibling", [g_w_in])
    grad_x, d_pre_mix = in_bwd(dproj, w_in_t, x2d, pre_mix_norm, dx2, after=[g_w_in, c_out])
    p_in = sum_cores("sum_cores_in", g_w_in, s_in, core, after=[d_pre_mix])

    d_b_sp = d_bfull.reshape(CHUNK, N_GROUPS, HEAD_DIM).sum(axis=-1).T
    small_g = _pack_small(dict(pre_mix_norm=d_pre_mix, sgu_ln_gain=d_ln_gain, sgu_ln_bias=d_ln_bias, sgu_w_spatial=d_w_sp,
                               sgu_b_spatial=d_b_sp, attn_out_norm=d_attn_out, sgu_out_norm=d_sgu_out,
                               post_mix_norm=d_post_mix, pre_ffn_norm=d_pre_ffn, post_ffn_norm=d_post_ffn,
                               loss_sum=sq_err))
    small_g = small_g.reshape(N_DEV, SMALL_ROWS // N_DEV, 128)
    (_, (c_in,)), (_, (o_small,)) = by_sequencer(
        "last_sums_to_owners", [to_chips([p_in]), to_owners(small_g)], TO_ALL)

    big, last = {}, p_in
    for name, w, p, c, m, vv, transposed in (
            ("w_down", w_down, p_down, c_down, m_w_down, v_w_down, False), ("w_gate", w_gate, p_gate, c_gate, m_w_gate, v_w_gate, True),
            ("w_up", w_up, p_up, c_up, m_w_up, v_w_up, True), ("w_out", w_out, p_out, c_out, m_w_out, v_w_out, False),
            ("w_in", w_in, p_in, c_in, m_w_in, v_w_in, True)):
        if name == "w_in":
            last = sum_owned("sum_small", small_g, o_small, dev_ids, after=[last])
            (all_small,) = gathered("gather_small_grads", [last])
        turn = (lambda t: t.T) if transposed else (lambda t: t)
        outs = adamw_of_sums("adamw_" + name, p, c, chip_ids, turn(w[0]), turn(m[0]), turn(vv[0]), last)
        big[name], last = tuple(turn(t)[None] for t in outs), outs[0]
    sd, snm, snv = adamw("adamw_small", _pack_small(small_w), all_small, _pack_small(small_m), _pack_small(small_v))
    sg, sd, snm, snv = (_unpack_small(t, small_w) for t in (all_small, sd, snm, snv))
    loss = sg["loss_sum"][0] * np.float32(0.5 / D_MODEL)

    names = ["pre_mix_norm", "w_in", "sgu_ln_gain", "sgu_ln_bias", "sgu_w_spatial", "sgu_b_spatial", "attn_out_norm",
             "sgu_out_norm", "w_out", "post_mix_norm", "pre_ffn_norm", "w_gate", "w_up", "w_down", "post_ffn_norm"]
    outs = [loss, grad_x[None]]
    for i, table in enumerate((sg, sd, snm, snv)):
        for name in names:
            outs.append(big[name][i] if name in big else table[name])
    return tuple(outs)
```

```python
import numpy as np
import jax
import jax.numpy as jnp
from jax import lax
from jax.experimental import pallas as pl
from jax.experimental.pallas import tpu as pltpu
from jax.experimental.pallas import tpu_sc as plsc

F32 = jnp.float32
BF16 = jnp.bfloat16

SEQ = 2048
D_MODEL = 1024
ATTN_W = 512
SGU_W = 512
HEAD_DIM = 64
N_GROUPS = 8
CHUNK = 128
D_FF = 2816
IN_W = 3 * ATTN_W + 2 * SGU_W
DILATIONS = (1, 4, 16)
ROPE_THETA = 500000.0
ROT_DIM = 16
ROT_HALF = 8
RMS_EPS = 1e-6
LN_EPS = 1e-5
Q_SCALE = 0.125
NEG = -1e30

N_DEV = 8
MESH_AXES = ("x", "y", "c")
MESH = pl.DeviceIdType.MESH

ADAM_LR = 0.001
ADAM_B1 = 0.9
ADAM_B2 = 0.999
ADAM_EPS = 1e-08
ADAM_WD = 0.01
ADAM_STEP = 10

VMEM_LIMIT = 60 * 1024 * 1024
ANY = pl.BlockSpec(memory_space=pl.ANY)

SMALL = (("pre_mix_norm", 1024), ("sgu_ln_gain", 512), ("sgu_ln_bias", 512), ("sgu_w_spatial", 8 * 128 * 128),
         ("sgu_b_spatial", 1024), ("attn_out_norm", 512), ("sgu_out_norm", 512), ("post_mix_norm", 1024),
         ("pre_ffn_norm", 1024), ("post_ffn_norm", 1024), ("loss_sum", 1))
SMALL_ROWS = 1152


def _params(sem=("arbitrary",)):
    return pltpu.CompilerParams(dimension_semantics=sem, vmem_limit_bytes=VMEM_LIMIT)


def _dot(a, b):
    return jnp.dot(a, b, preferred_element_type=F32)


def _dot_nt(a, b):
    return lax.dot_general(a, b, (((1,), (1,)), ((), ())), preferred_element_type=F32)


def _dot_tn(a, b):
    return lax.dot_general(a, b, (((0,), (0,)), ((), ())), preferred_element_type=F32)


def _rms(z):
    return lax.rsqrt(jnp.mean(z * z, axis=-1, keepdims=True) + RMS_EPS)


def _rms_bwd(z, gain, d):
    r = _rms(z)
    n = z * r
    dn = d * gain
    dz = r * (dn - n * jnp.mean(dn * n, axis=-1, keepdims=True))
    return dz, jnp.sum(d * n, axis=0, keepdims=True)


def _gelu(z):
    return 0.5 * z * (1.0 + lax.erf(z * np.float32(1.0 / np.sqrt(2.0))))


def _gelu_grad(z):
    cdf = 0.5 * (1.0 + lax.erf(z * np.float32(1.0 / np.sqrt(2.0))))
    return cdf + z * jnp.exp(-0.5 * z * z) * np.float32(1.0 / np.sqrt(2.0 * np.pi))


def _rot_tables(pos_col, invf, ma, mb):
    ang = pos_col.astype(F32) * invf
    s = jnp.sin(ang)
    return jnp.cos(ang), s * ma, s * mb


def _rot(t, c, sa, sb):
    return t * c + pltpu.roll(t, 120, 1) * sa + pltpu.roll(t, 8, 1) * sb


def _rot_t(d, c, sa, sb):
    return d * c + pltpu.roll(d * sa, 8, 1) + pltpu.roll(d * sb, 120, 1)


def _rot_consts():
    lane = np.arange(128) % HEAD_DIM
    inv_freq = (np.float32(ROPE_THETA) ** (-np.arange(0, ROT_DIM, 2, dtype=np.float32) / np.float32(ROT_DIM))).astype(np.float32)
    invf = np.where(lane < ROT_DIM, inv_freq[lane % ROT_HALF], 0.0).astype(np.float32)
    ma = np.where(lane < ROT_HALF, -1.0, 0.0).astype(np.float32)
    mb = np.where((lane >= ROT_HALF) & (lane < ROT_DIM), 1.0, 0.0).astype(np.float32)
    return jnp.asarray(invf[None]), jnp.asarray(ma[None]), jnp.asarray(mb[None])


def _row_spec(tm, w):
    return pl.BlockSpec((tm, w), lambda i: (i, 0))


def _full_spec(shape):
    return pl.BlockSpec(shape, lambda i: (0,) * len(shape))


def _weight_spec(shape):
    return pl.BlockSpec(shape, lambda i: (0,) * len(shape), pipeline_mode=pl.Buffered(1))


RES = 16


def _residue_scratch(n_arrays, tm, width):
    return [pltpu.VMEM((2, n_arrays, tm // RES, RES, width), F32), pltpu.SemaphoreType.DMA((2, n_arrays, RES))]


def _to_residue_rows(tiles, outs, scratch, sems, tm, n_steps):
    i = pl.program_id(0)
    slot = i % 2
    per = tm // RES

    def copies(step, s):
        return [pltpu.make_async_copy(scratch.at[s, a, :, b, :],
                                      outs[a].at[pl.ds(pl.multiple_of(b * (SEQ // RES) + per * step, per), per), :],
                                      sems.at[s, a, b]) for a in range(len(outs)) for b in range(RES)]

    @pl.when(i >= 2)
    def _():
        for cp in copies(i - 2, slot):
            cp.wait()

    for a, tile in enumerate(tiles):
        scratch[slot, a] = tile.reshape(per, RES, tile.shape[-1])
    for cp in copies(i, slot):
        cp.start()

    @pl.when(i == n_steps - 1)
    def _():
        for cp in copies(i - 1, 1 - slot) + copies(i, slot):
            cp.wait()


def in_proj(x, pos_col, g1, w_in_t, rot):
    tm = 512
    n_steps = SEQ // tm

    def body(x_ref, pos_ref, g_ref, w_ref, invf_ref, ma_ref, mb_ref, h_ref, u_ref, vs_ref, q_ref, k_ref, v_ref, scratch, sems):
        xf = x_ref[...]
        h = (xf * _rms(xf) * g_ref[...]).astype(BF16)
        h_ref[...] = h
        proj = _dot_nt(h, w_ref[...])
        c, sa, sb = _rot_tables(pos_ref[...], invf_ref[...], ma_ref[...], mb_ref[...])
        slabs = range(ATTN_W // 128)
        q = jnp.concatenate([_rot(proj[:, j * 128:(j + 1) * 128], c, sa, sb) * Q_SCALE for j in slabs], axis=1)
        k = jnp.concatenate([_rot(proj[:, ATTN_W + j * 128:ATTN_W + (j + 1) * 128], c, sa, sb) for j in slabs], axis=1)
        u_ref[...] = proj[:, 3 * ATTN_W:3 * ATTN_W + SGU_W]
        vs_ref[...] = proj[:, 3 * ATTN_W + SGU_W:]
        _to_residue_rows([q, k, proj[:, 2 * ATTN_W:3 * ATTN_W]], [q_ref, k_ref, v_ref], scratch, sems, tm, n_steps)

    act = jax.ShapeDtypeStruct((SEQ, 512), F32)
    return _call(
        "in_proj", body, n_steps,
        [_row_spec(tm, D_MODEL), _row_spec(tm, 1), _full_spec((1, D_MODEL)), _weight_spec((IN_W, D_MODEL)),
         _full_spec((1, 128)), _full_spec((1, 128)), _full_spec((1, 128))],
        [_row_spec(tm, D_MODEL)] + [_row_spec(tm, 512)] * 2 + [ANY] * 3,
        [jax.ShapeDtypeStruct((SEQ, D_MODEL), BF16)] + [act] * 5,
        (x, pos_col, g1, w_in_t, *rot), scratch_shapes=_residue_scratch(3, tm, ATTN_W))


def _to_residue_order(t):
    return t.reshape(SEQ // RES, RES, -1).transpose(1, 0, 2).reshape(t.shape)


def _from_residue_order(t):
    return t.reshape(RES, SEQ // RES, -1).transpose(1, 0, 2).reshape(t.shape)


def _block_rows(d, r, n):
    if d == 16:
        slices = [(128 * r, 128)]
    elif d == 4:
        slices = [(128 * (4 * b + r) + 32 * n, 32) for b in range(4)]
    else:
        slices = [(128 * b + 8 * n, 8) for b in range(RES)]
    return [(s if isinstance(s, int) else pl.multiple_of(s, z), z) for s, z in slices]


def _block_step(d, i):
    if d == 16:
        return i
    if d == 4:
        return 4 * (i & 31) + (i >> 5)
    return 16 * (i & 7) + (i >> 3)


def _attn_masks(d):
    row2 = _block_step(d, lax.broadcasted_iota(jnp.int32, (128, 256), 0))
    col2 = lax.broadcasted_iota(jnp.int32, (128, 256), 1)
    key2 = _block_step(d, col2 & 127)
    mask2 = jnp.logical_or(jnp.logical_and(col2 < 128, key2 >= row2), jnp.logical_and(col2 >= 128, key2 <= row2))
    row1 = _block_step(d, lax.broadcasted_iota(jnp.int32, (128, 128), 0))
    col1 = lax.broadcasted_iota(jnp.int32, (128, 128), 1)
    return col1 < HEAD_DIM, _block_step(d, col1) <= row1, mask2


def _load_rows(ref, slices):
    parts = [ref[pl.ds(s, z), :] for s, z in slices]
    return parts[0] if len(parts) == 1 else jnp.concatenate(parts, axis=0)


def _for_each_group(fn):
    for p, d in enumerate(DILATIONS):
        masks = _attn_masks(d)
        if d == 16:
            def group(i, carry, p=p, masks=masks):
                fn(p, masks, [(_block_rows(16, 4 * i + g, 0), None) for g in range(4)])
                return carry

            lax.fori_loop(0, 4, group, 0)
        elif d == 4:
            fn(p, masks, [(_block_rows(4, r, 0), None) for r in range(4)])

            def group(i, carry, p=p, masks=masks):
                fn(p, masks, [(_block_rows(4, r, i + 1), _block_rows(4, r, i)) for r in range(4)])
                return carry

            lax.fori_loop(0, 3, group, 0)
        else:
            fn(p, masks, [(_block_rows(1, 0, 0), None)])

            def group(i, carry, p=p, masks=masks):
                fn(p, masks, [(_block_rows(1, 0, 3 * i + g + 1), _block_rows(1, 0, 3 * i + g)) for g in range(3)])
                return carry

            lax.fori_loop(0, 5, group, 0)


def attn_fwd(q, k, v):
    def body(q_ref, k_ref, v_ref, o_ref, lse_ref, nat_ref, op_ref, lp_ref, sems):
        def group(p, masks, blocks):
            head0, mask1, mask2 = masks
            heads = (head0, jnp.logical_not(head0))
            keys = [rows if prev is None else prev + rows for rows, prev in blocks]
            mask = [mask1 if prev is None else mask2 for _, prev in blocks]
            qb = [_load_rows(q_ref, rows) for rows, _ in blocks]
            kk = [_load_rows(k_ref, ks).astype(BF16) for ks in keys]
            vv = [_load_rows(v_ref, ks).astype(BF16) for ks in keys]
            chains = [(g, hm) for g in range(len(blocks)) for hm in heads]
            s = [jnp.where(mask[g], _dot_nt(jnp.where(hm, qb[g], 0.0).astype(BF16), kk[g]), NEG) for g, hm in chains]
            m = [jnp.max(t, axis=-1, keepdims=True) for t in s]
            e = [jnp.exp(t - mt) for t, mt in zip(s, m)]
            l = [jnp.sum(t, axis=-1, keepdims=True) for t in e]
            pv = [_dot(t.astype(BF16), vv[g]) for t, (g, _) in zip(e, chains)]
            for g, (rows, _) in enumerate(blocks):
                o_blk = jnp.where(head0, pv[2 * g] / l[2 * g], pv[2 * g + 1] / l[2 * g + 1])
                l_blk = jnp.where(head0, jnp.broadcast_to(m[2 * g] + jnp.log(l[2 * g]), (128, 128)),
                                  jnp.broadcast_to(m[2 * g + 1] + jnp.log(l[2 * g + 1]), (128, 128)))
                at = 0
                for start, size in rows:
                    op_ref[p, pl.ds(start, size), :] = o_blk[at:at + size]
                    lp_ref[p, pl.ds(start, size), :] = l_blk[at:at + size]
                    at += size

        _for_each_group(group)

        def combine(i, carry):
            rows = pl.ds(pl.multiple_of(i * 256, 256), 256)
            ls = [lp_ref[p, rows, :] for p in range(3)]
            m = jnp.maximum(jnp.maximum(ls[0], ls[1]), ls[2])
            lse = m + jnp.log(jnp.exp(ls[0] - m) + jnp.exp(ls[1] - m) + jnp.exp(ls[2] - m))
            o = jnp.zeros((256, 128), F32)
            for p in range(3):
                o = o + jnp.exp(ls[p] - lse) * op_ref[p, rows, :]
            o_ref[rows, :] = o
            lse_ref[rows, :] = lse
            return carry

        lax.fori_loop(0, SEQ // 256, combine, 0)

        lanes = pl.ds(pl.multiple_of(pl.program_id(0) * 128, 128), 128)
        back = [pltpu.make_async_copy(o_ref.at[pl.ds(b * (SEQ // RES), SEQ // RES), :], nat_ref.at[:, b, lanes], sems.at[b])
                for b in range(RES)]
        for cp in back:
            cp.start()
        for cp in back:
            cp.wait()

    slab = pl.BlockSpec((SEQ, 128), lambda i: (0, i))
    out = jax.ShapeDtypeStruct((SEQ, ATTN_W), F32)
    attn_r, lse, attn = _call(
        "attn_fwd", body, ATTN_W // 128, [slab] * 3, [slab] * 2 + [ANY],
        [out, out, jax.ShapeDtypeStruct((SEQ // RES, RES, ATTN_W), F32)], (q, k, v),
        scratch_shapes=[pltpu.VMEM((3, SEQ, 128), F32), pltpu.VMEM((3, SEQ, 128), F32), pltpu.SemaphoreType.DMA((RES,))])
    return attn_r, lse, attn.reshape(SEQ, ATTN_W)


def _causal_weights(w_ref):
    row = lax.broadcasted_iota(jnp.int32, (CHUNK, CHUNK), 0)
    col = lax.broadcasted_iota(jnp.int32, (CHUNK, CHUNK), 1)
    return [jnp.where(col <= row, w_ref[g], 0.0).astype(BF16) for g in range(N_GROUPS)], col <= row


def _sgu_chunk_fwd(u, vs, lg, lb, wc, bfull, head0):
    ug = _gelu(u)
    vg = _gelu(vs)
    xc = vg - jnp.mean(vg, axis=-1, keepdims=True)
    rstd = lax.rsqrt(jnp.mean(xc * xc, axis=-1, keepdims=True) + LN_EPS)
    xhat = xc * rstd
    vn = xhat * lg + lb
    mixed = []
    for gp in range(SGU_W // 128):
        vp = vn[:, gp * 128:(gp + 1) * 128].astype(BF16)
        mixed.append(jnp.where(head0, _dot(wc[2 * gp], vp), _dot(wc[2 * gp + 1], vp)))
    ms = jnp.concatenate(mixed, axis=1) + bfull
    return ug, xhat, rstd, vn, ms


def sgu_fwd(u, vs, lg, lb, w_sp, bfull):
    cpb = 4

    def body(u_ref, vs_ref, lg_ref, lb_ref, w_ref, b_ref, o_ref):
        wc, _ = _causal_weights(w_ref)
        head0 = lax.broadcasted_iota(jnp.int32, (CHUNK, 128), 1) < HEAD_DIM
        for ci in range(cpb):
            rows = pl.ds(ci * CHUNK, CHUNK)
            ug, _, _, _, ms = _sgu_chunk_fwd(u_ref[rows, :], vs_ref[rows, :], lg_ref[...], lb_ref[...], wc, b_ref[...], head0)
            o_ref[rows, :] = ug * ms

    tm = cpb * CHUNK
    return _call(
        "sgu_fwd", body, SEQ // tm,
        [_row_spec(tm, SGU_W), _row_spec(tm, SGU_W), _full_spec((1, SGU_W)), _full_spec((1, SGU_W)),
         _full_spec((N_GROUPS, CHUNK, CHUNK)), _full_spec((CHUNK, SGU_W))],
        [_row_spec(tm, SGU_W)], [jax.ShapeDtypeStruct((SEQ, SGU_W), F32)],
        (u, vs, lg, lb, w_sp, bfull))


def out_proj(attn, sgu, x, ga, gs, w_out, gpm, gpf):
    tm = 512

    def body(a_ref, s_ref, x_ref, ga_ref, gs_ref, w_ref, gpm_ref, gpf_ref, mix_ref, y_ref, x2_ref, h2_ref):
        a = a_ref[...]
        s = s_ref[...]
        an = (a * _rms(a) * ga_ref[...]).astype(BF16)
        sn = (s * _rms(s) * gs_ref[...]).astype(BF16)
        mix_ref[:, :ATTN_W] = an
        mix_ref[:, ATTN_W:] = sn
        y = _dot(an, w_ref[:ATTN_W, :]) + _dot(sn, w_ref[ATTN_W:, :])
        y_ref[...] = y
        x2 = x_ref[...] + y * _rms(y) * gpm_ref[...]
        x2_ref[...] = x2
        h2_ref[...] = (x2 * _rms(x2) * gpf_ref[...]).astype(BF16)

    wide = jax.ShapeDtypeStruct((SEQ, D_MODEL), F32)
    wide16 = jax.ShapeDtypeStruct((SEQ, D_MODEL), BF16)
    return _call(
        "out_proj", body, SEQ // tm,
        [_row_spec(tm, ATTN_W), _row_spec(tm, SGU_W), _row_spec(tm, D_MODEL), _full_spec((1, ATTN_W)),
         _full_spec((1, SGU_W)), _weight_spec((D_MODEL, D_MODEL)), _full_spec((1, D_MODEL)), _full_spec((1, D_MODEL))],
        [_row_spec(tm, D_MODEL)] * 4, [wide16, wide, wide, wide16],
        (attn, sgu, x, ga, gs, w_out, gpm, gpf))


def ffn_fwd_loss(h2, w_gate_t, w_up_t, w_down, x2, gpo, target):
    tm = 256

    def body(h_ref, wg_ref, wu_ref, w_ref, x2_ref, g_ref, t_ref, gate_ref, up_ref, a_ref, df_ref, dx3_ref, dg_ref, loss_ref):
        h = h_ref[...]
        gate = _dot_nt(h, wg_ref[...])
        up = _dot_nt(h, wu_ref[...])
        gate_ref[...] = gate.astype(BF16)
        up_ref[...] = up.astype(BF16)
        act = (gate * jax.nn.sigmoid(gate) * up).astype(BF16)
        a_ref[...] = act
        f = _dot(act, w_ref[...])
        gain = g_ref[...]
        err = x2_ref[...] + f * _rms(f) * gain - t_ref[...]
        dx3 = err * np.float32(1.0 / D_MODEL)
        dx3_ref[...] = dx3
        df, dg = _rms_bwd(f, gain, dx3)
        df_ref[...] = df.astype(BF16)

        @pl.when(pl.program_id(0) == 0)
        def _():
            dg_ref[...] = jnp.zeros_like(dg_ref)
            loss_ref[...] = jnp.zeros_like(loss_ref)

        dg_ref[...] += dg
        loss_ref[...] += jnp.sum(err * err, axis=(0, 1), keepdims=True)

    ff = jax.ShapeDtypeStruct((SEQ, D_FF), BF16)
    return _call(
        "ffn_fwd_loss", body, SEQ // tm,
        [_row_spec(tm, D_MODEL), _weight_spec((D_FF, D_MODEL)), _weight_spec((D_FF, D_MODEL)), _weight_spec((D_FF, D_MODEL)),
         _row_spec(tm, D_MODEL), _full_spec((1, D_MODEL)), _row_spec(tm, D_MODEL)],
        [_row_spec(tm, D_FF)] * 3 + [_row_spec(tm, D_MODEL), _row_spec(tm, D_MODEL), _full_spec((1, D_MODEL)), _full_spec((1, 1))],
        [ff, ff, ff, jax.ShapeDtypeStruct((SEQ, D_MODEL), BF16), jax.ShapeDtypeStruct((SEQ, D_MODEL), F32),
         jax.ShapeDtypeStruct((1, D_MODEL), F32), jax.ShapeDtypeStruct((1, 1), F32)],
        (h2, w_gate_t, w_up_t, w_down, x2, gpo, target))


def ffn_bwd(df, w_down, gate, up, w_gate_t, w_up_t, x2, gpf, dx3, y, gpm, after=()):
    tm = 256

    def body(df_ref, wd_ref, g_ref, u_ref, wg_ref, wu_ref, x2_ref, gpf_ref, dx3_ref, y_ref, gpm_ref,
             dg_ref, du_ref, dx2_ref, dy_ref, dgpf_ref, dgpm_ref):
        dact = _dot_nt(df_ref[...], wd_ref[...])
        g = g_ref[...].astype(F32)
        s = jax.nn.sigmoid(g)
        dup = (dact * g * s).astype(BF16)
        dgate = (dact * u_ref[...].astype(F32) * (s * (1.0 + g * (1.0 - s)))).astype(BF16)
        du_ref[...] = dup
        dg_ref[...] = dgate
        dh2 = _dot(dgate, wg_ref[...]) + _dot(dup, wu_ref[...])
        dz, dgpf = _rms_bwd(x2_ref[...], gpf_ref[...], dh2)
        dx2 = dx3_ref[...] + dz
        dx2_ref[...] = dx2
        dy, dgpm = _rms_bwd(y_ref[...], gpm_ref[...], dx2)
        dy_ref[...] = dy.astype(BF16)

        @pl.when(pl.program_id(0) == 0)
        def _():
            dgpf_ref[...] = jnp.zeros_like(dgpf_ref)
            dgpm_ref[...] = jnp.zeros_like(dgpm_ref)

        dgpf_ref[...] += dgpf
        dgpm_ref[...] += dgpm

    vec = jax.ShapeDtypeStruct((1, D_MODEL), F32)
    ff16 = jax.ShapeDtypeStruct((SEQ, D_FF), BF16)
    return _call(
        "ffn_bwd", body, SEQ // tm,
        [_row_spec(tm, D_MODEL), _weight_spec((D_FF, D_MODEL)), _row_spec(tm, D_FF), _row_spec(tm, D_FF),
         _weight_spec((D_FF, D_MODEL)), _weight_spec((D_FF, D_MODEL)), _row_spec(tm, D_MODEL), _full_spec((1, D_MODEL)),
         _row_spec(tm, D_MODEL), _row_spec(tm, D_MODEL), _full_spec((1, D_MODEL))],
        [_row_spec(tm, D_FF), _row_spec(tm, D_FF), _row_spec(tm, D_MODEL), _row_spec(tm, D_MODEL),
         _full_spec((1, D_MODEL)), _full_spec((1, D_MODEL))],
        [ff16, ff16, jax.ShapeDtypeStruct((SEQ, D_MODEL), F32), jax.ShapeDtypeStruct((SEQ, D_MODEL), BF16), vec, vec],
        (df, w_down, gate, up, w_gate_t, w_up_t, x2, gpf, dx3, y, gpm), after=after)


def weight_grads(name, lhs, b, after=()):
    m, n, k = lhs[0].shape[1], b.shape[1], len(lhs)
    tr = 256

    def body(*refs):
        for a_ref, o_ref in zip(refs[:k], refs[k + 1:]):
            o_ref[...] = _dot_tn(a_ref[...], refs[k][...]).astype(BF16)

    outs = _call(
        name, body, m // tr, [pl.BlockSpec((SEQ, tr), lambda i: (0, i))] * k + [_weight_spec((SEQ, n))],
        [_row_spec(tr, n)] * k, [jax.ShapeDtypeStruct((m, n), BF16)] * k, (*lhs, b), after=after)
    return [out.reshape(N_DEV, m // N_DEV, n) for out in outs]


def weight_grad(name, a, b, after=()):
    return weight_grads(name, [a], b, after)[0]


def mix_bwd(dy, w_out, attn, sgu, ga, gs, after=()):
    tm = 512
    n_steps = SEQ // tm

    def body(dy_ref, w_ref, a_ref, s_ref, ga_ref, gs_ref, ds_ref, dga_ref, dgs_ref, da_ref, scratch, sems):
        dy = dy_ref[...]
        da, dga = _rms_bwd(a_ref[...], ga_ref[...], _dot_nt(dy, w_ref[:ATTN_W, :]))
        ds, dgs = _rms_bwd(s_ref[...], gs_ref[...], _dot_nt(dy, w_ref[ATTN_W:, :]))
        ds_ref[...] = ds
        _to_residue_rows([da], [da_ref], scratch, sems, tm, n_steps)

        @pl.when(pl.program_id(0) == 0)
        def _():
            dga_ref[...] = jnp.zeros_like(dga_ref)
            dgs_ref[...] = jnp.zeros_like(dgs_ref)

        dga_ref[...] += dga
        dgs_ref[...] += dgs

    half = jax.ShapeDtypeStruct((SEQ, 512), F32)
    vec = jax.ShapeDtypeStruct((1, 512), F32)
    return _call(
        "mix_bwd", body, n_steps,
        [_row_spec(tm, D_MODEL), _weight_spec((D_MODEL, D_MODEL)), _row_spec(tm, 512), _row_spec(tm, 512),
         _full_spec((1, 512)), _full_spec((1, 512))],
        [_row_spec(tm, 512), _full_spec((1, 512)), _full_spec((1, 512)), ANY],
        [half, vec, vec, half], (dy, w_out, attn, sgu, ga, gs), scratch_shapes=_residue_scratch(1, tm, ATTN_W), after=after)


def sgu_bwd(u, vs, dsgu, lg, lb, w_sp, bfull):
    cpb = 4

    def body(u_ref, vs_ref, d_ref, lg_ref, lb_ref, w_ref, b_ref, du_ref, dvs_ref, dlg_ref, dlb_ref, dw_ref, db_ref):
        wc, causal = _causal_weights(w_ref)
        head0 = lax.broadcasted_iota(jnp.int32, (CHUNK, 128), 1) < HEAD_DIM
        lg = lg_ref[...]

        @pl.when(pl.program_id(0) == 0)
        def _():
            dlg_ref[...] = jnp.zeros_like(dlg_ref)
            dlb_ref[...] = jnp.zeros_like(dlb_ref)
            dw_ref[...] = jnp.zeros_like(dw_ref)
            db_ref[...] = jnp.zeros_like(db_ref)

        for ci in range(cpb):
            rows = pl.ds(ci * CHUNK, CHUNK)
            u = u_ref[rows, :]
            vs = vs_ref[rows, :]
            d = d_ref[rows, :]
            ug, xhat, rstd, vn, ms = _sgu_chunk_fwd(u, vs, lg, lb_ref[...], wc, b_ref[...], head0)
            du_ref[rows, :] = (d * ms * _gelu_grad(u)).astype(BF16)
            dms = d * ug
            db_ref[...] += dms
            dvn = []
            for gp in range(SGU_W // 128):
                dmp = dms[:, gp * 128:(gp + 1) * 128]
                dm0 = jnp.where(head0, dmp, 0.0).astype(BF16)
                dm1 = jnp.where(head0, 0.0, dmp).astype(BF16)
                vp = vn[:, gp * 128:(gp + 1) * 128].astype(BF16)
                dw_ref[2 * gp] += _dot_nt(dm0, vp)
                dw_ref[2 * gp + 1] += _dot_nt(dm1, vp)
                dvn.append(_dot_tn(wc[2 * gp], dm0) + _dot_tn(wc[2 * gp + 1], dm1))
            dvn = jnp.concatenate(dvn, axis=1)
            dlg_ref[...] += jnp.sum(dvn * xhat, axis=0, keepdims=True)
            dlb_ref[...] += jnp.sum(dvn, axis=0, keepdims=True)
            dxh = dvn * lg
            dvg = rstd * (dxh - jnp.mean(dxh, axis=-1, keepdims=True) - xhat * jnp.mean(dxh * xhat, axis=-1, keepdims=True))
            dvs_ref[rows, :] = (dvg * _gelu_grad(vs)).astype(BF16)

        @pl.when(pl.program_id(0) == pl.num_programs(0) - 1)
        def _():
            for g in range(N_GROUPS):
                dw_ref[g] = jnp.where(causal, dw_ref[g], 0.0)

    tm = cpb * CHUNK
    half16 = jax.ShapeDtypeStruct((SEQ, SGU_W), BF16)
    vec = jax.ShapeDtypeStruct((1, SGU_W), F32)
    return _call(
        "sgu_bwd", body, SEQ // tm,
        [_row_spec(tm, SGU_W)] * 3 + [_full_spec((1, SGU_W)), _full_spec((1, SGU_W)),
                                      _full_spec((N_GROUPS, CHUNK, CHUNK)), _full_spec((CHUNK, SGU_W))],
        [_row_spec(tm, SGU_W), _row_spec(tm, SGU_W), _full_spec((1, SGU_W)), _full_spec((1, SGU_W)),
         _full_spec((N_GROUPS, CHUNK, CHUNK)), _full_spec((CHUNK, SGU_W))],
        [half16, half16, vec, vec, jax.ShapeDtypeStruct((N_GROUPS, CHUNK, CHUNK), F32),
         jax.ShapeDtypeStruct((CHUNK, SGU_W), F32)],
        (u, vs, dsgu, lg, lb, w_sp, bfull))


def attn_bwd(q, k, v, o, lse, do, pos_col, rot):
    def body(q_ref, k_ref, v_ref, o_ref, lse_ref, do_ref, pos_ref, invf_ref, ma_ref, mb_ref,
             dq_ref, dk_ref, dv_ref, dqa_ref, dka_ref, dva_ref, dlt_ref, rot_ref):
        dqa_ref[...] = jnp.zeros_like(dqa_ref)
        dka_ref[...] = jnp.zeros_like(dka_ref)
        dva_ref[...] = jnp.zeros_like(dva_ref)

        def delta(i, carry):
            rows = pl.ds(pl.multiple_of(i * 256, 256), 256)
            prod = do_ref[rows, :] * o_ref[rows, :]
            h0 = lax.broadcasted_iota(jnp.int32, (256, 128), 1) < HEAD_DIM
            d0 = jnp.sum(jnp.where(h0, prod, 0.0), axis=-1, keepdims=True)
            d1 = jnp.sum(jnp.where(h0, 0.0, prod), axis=-1, keepdims=True)
            dlt_ref[rows, :] = jnp.where(h0, d0, d1)
            return carry

        lax.fori_loop(0, SEQ // 256, delta, 0)

        def add_rows(ref, slices, val):
            at = 0
            for start, size in slices:
                ref[pl.ds(start, size), :] += val[at:at + size]
                at += size

        def group(p, masks, blocks):
            head0, mask1, mask2 = masks
            heads = (head0, jnp.logical_not(head0))
            keys = [rows if prev is None else prev + rows for rows, prev in blocks]
            mask = [mask1 if prev is None else mask2 for _, prev in blocks]
            kk = [_load_rows(k_ref, ks).astype(BF16) for ks in keys]
            vv = [_load_rows(v_ref, ks).astype(BF16) for ks in keys]
            qb = [_load_rows(q_ref, rows) for rows, _ in blocks]
            dob = [_load_rows(do_ref, rows) for rows, _ in blocks]
            lse_b = [_load_rows(lse_ref, rows) for rows, _ in blocks]
            dlt_b = [_load_rows(dlt_ref, rows) for rows, _ in blocks]
            chains = [(g, h) for g in range(len(blocks)) for h in range(2)]
            qm = [jnp.where(heads[h], qb[g], 0.0).astype(BF16) for g, h in chains]
            dom = [jnp.where(heads[h], dob[g], 0.0).astype(BF16) for g, h in chains]
            s = [_dot_nt(qm[c], kk[g]) for c, (g, h) in enumerate(chains)]
            dp = [_dot_nt(dom[c], vv[g]) for c, (g, h) in enumerate(chains)]
            pr = [jnp.where(mask[g], jnp.exp(s[c] - lse_b[g][:, h * HEAD_DIM:h * HEAD_DIM + 1]), 0.0)
                  for c, (g, h) in enumerate(chains)]
            ds = [(pr[c] * (dp[c] - dlt_b[g][:, h * HEAD_DIM:h * HEAD_DIM + 1])).astype(BF16)
                  for c, (g, h) in enumerate(chains)]
            dv = [_dot_tn(pr[c].astype(BF16), dom[c]) for c in range(len(chains))]
            dk = [_dot_tn(ds[c], qm[c]) for c in range(len(chains))]
            dq = [_dot(ds[c], kk[g]) for c, (g, h) in enumerate(chains)]
            for g, (rows, _) in enumerate(blocks):
                add_rows(dqa_ref, rows, jnp.where(head0, dq[2 * g], dq[2 * g + 1]))
                add_rows(dka_ref, keys[g], dk[2 * g] + dk[2 * g + 1])
                add_rows(dva_ref, keys[g], dv[2 * g] + dv[2 * g + 1])

        _for_each_group(group)

        @pl.when(pl.program_id(0) == 0)
        def _():
            def tables(i, carry):
                rows = pl.ds(pl.multiple_of(i * 256, 256), 256)
                c, sa, sb = _rot_tables(pos_ref[rows, :], invf_ref[...], ma_ref[...], mb_ref[...])
                rot_ref[0, rows, :] = c
                rot_ref[1, rows, :] = sa
                rot_ref[2, rows, :] = sb
                return carry

            lax.fori_loop(0, SEQ // 256, tables, 0)

        def finish(i, carry):
            rows = pl.ds(pl.multiple_of(i * 256, 256), 256)
            c, sa, sb = rot_ref[0, rows, :], rot_ref[1, rows, :], rot_ref[2, rows, :]
            dq_ref[rows, :] = _rot_t(dqa_ref[rows, :] * Q_SCALE, c, sa, sb).astype(BF16)
            dk_ref[rows, :] = _rot_t(dka_ref[rows, :], c, sa, sb).astype(BF16)
            dv_ref[rows, :] = dva_ref[rows, :].astype(BF16)
            return carry

        lax.fori_loop(0, SEQ // 256, finish, 0)

    slab = pl.BlockSpec((SEQ, 128), lambda i: (0, i))
    out = jax.ShapeDtypeStruct((SEQ, ATTN_W), BF16)
    acc = pltpu.VMEM((SEQ, 128), F32)
    return _call(
        "attn_bwd", body, ATTN_W // 128,
        [slab] * 6 + [_full_spec((SEQ, 1)), _full_spec((1, 128)), _full_spec((1, 128)), _full_spec((1, 128))],
        [slab] * 3, [out, out, out], (q, k, v, o, lse, do, pos_col, *rot),
        scratch_shapes=[acc, acc, acc, acc, pltpu.VMEM((3, SEQ, 128), F32)])


def in_bwd(dproj, w_in_t, x, g1, dx2, after=()):
    tm = 512

    def body(dp_ref, w_ref, x_ref, g_ref, dx2_ref, dx_ref, dg_ref):
        dh1 = _dot(dp_ref[...], w_ref[...])
        dz, dg = _rms_bwd(x_ref[...], g_ref[...], dh1)
        dx_ref[...] = dx2_ref[...] + dz

        @pl.when(pl.program_id(0) == 0)
        def _():
            dg_ref[...] = jnp.zeros_like(dg_ref)

        dg_ref[...] += dg

    return _call(
        "in_bwd", body, SEQ // tm,
        [_row_spec(tm, IN_W), _weight_spec((IN_W, D_MODEL)), _row_spec(tm, D_MODEL), _full_spec((1, D_MODEL)),
         _row_spec(tm, D_MODEL)],
        [_row_spec(tm, D_MODEL), _full_spec((1, D_MODEL))],
        [jax.ShapeDtypeStruct((SEQ, D_MODEL), F32), jax.ShapeDtypeStruct((1, D_MODEL), F32)],
        (dproj, w_in_t, x, g1, dx2), after=after)


def _coords():
    return lax.axis_index("x"), lax.axis_index("y"), lax.axis_index("c")


class Exchange:
    def __init__(self, srcs, bufs, new_shapes, n_sems, make):
        self.srcs, self.bufs, self.new_shapes, self.n_sems, self.make = list(srcs), list(bufs), list(new_shapes), n_sems, make


def _call(name, body, n_steps, in_specs, out_specs, out_shape, args, scratch_shapes=(), after=()):
    n_in = len(args)

    def wrapped(*refs):
        body(*refs[:n_in], *refs[n_in + len(after):])

    return list(pl.pallas_call(
        wrapped, name=name, grid=(n_steps,), in_specs=list(in_specs) + [ANY] * len(after), out_specs=list(out_specs),
        out_shape=list(out_shape), scratch_shapes=list(scratch_shapes), compiler_params=_params(),
    )(*args, *after))


GATHER_SEMS = 8


def gather(bufs):
    n = len(bufs)

    def make(src_refs, buf_refs, new_refs, send_sems, recv_sems):
        x, y, c = _coords()
        me, sibling = (x, y, c), (x, y, 1 - c)
        over_x, over_y, across = (1 - x, y), (x, 1 - y), (1 - x, 1 - y)

        def copy(a, k, block, to, half=None):
            r = buf_refs[a].shape[0] // N_DEV
            lo, size = (0, r) if half is None else (half * (r // 2), r // 2)
            rows = buf_refs[a].at[pl.ds((4 * block[0] + 2 * block[1] + block[2]) * r + lo, size), :]
            return pltpu.make_async_remote_copy(
                src_ref=rows, dst_ref=rows, send_sem=send_sems.at[GATHER_SEMS * a + k],
                recv_sem=recv_sems.at[GATHER_SEMS * a + k], device_id=to, device_id_type=MESH)

        every = range(n)
        out = ([copy(a, 0, me, sibling) for a in every] + [copy(a, 1, me, (*over_x, c)) for a in every]
               + [copy(a, 2, me, (*over_y, c)) for a in every])
        near_in = [copy(a, 1, (*over_x, c), me) for a in every] + [copy(a, 2, (*over_y, c), me) for a in every]
        relay = ([copy(a, 3, (*over_x, c), (*over_y, c), half=0) for a in every]
                 + [copy(a, 4, (*over_y, c), (*over_x, c), half=1) for a in every])
        near_on = [copy(a, 5, (*over_x, c), sibling) for a in every] + [copy(a, 6, (*over_y, c), sibling) for a in every]
        relay_in = ([copy(a, 3, (*across, c), me, half=0) for a in every]
                    + [copy(a, 4, (*across, c), me, half=1) for a in every])
        far_on = [copy(a, 7, (*across, c), sibling) for a in every]
        from_core = ([copy(a, 0, sibling, me) for a in every] + [copy(a, 5, (*over_x, 1 - c), me) for a in every]
                     + [copy(a, 6, (*over_y, 1 - c), me) for a in every] + [copy(a, 7, (*across, 1 - c), me) for a in every])
        stages = [([], out), (near_in, relay + near_on), (relay_in, far_on)]
        return stages, out + relay + near_on + far_on, from_core

    return Exchange([], bufs, [], GATHER_SEMS * n, make)


TO_GATHER = (1, lambda x, y, c: [(x, y, 1 - c), (1 - x, y, c), (x, 1 - y, c)])
TO_SIBLING = (2, lambda x, y, c: [(x, y, 1 - c)])
TO_CHIPS = (3, lambda x, y, c: [(1 - x, y, c), (x, 1 - y, c), (1 - x, 1 - y, c)])
TO_ALL = (4, lambda x, y, c: [(x ^ (m >> 2), y ^ ((m >> 1) & 1), c ^ (m & 1)) for m in range(1, N_DEV)])


def by_sequencer(name, exchanges, who):
    collective_id, peers_of = who
    hbm = pltpu.MemorySpace.HBM
    refs = [([jax.new_ref(a, memory_space=hbm) for a in ex.srcs], [jax.new_ref(a, memory_space=hbm) for a in ex.bufs],
             [jax.empty_ref(s, memory_space=hbm) for s in ex.new_shapes]) for ex in exchanges]
    sems = []
    for ex in exchanges:
        sems += [pltpu.SemaphoreType.DMA((ex.n_sems,)), pltpu.SemaphoreType.DMA((ex.n_sems,))]

    @pl.kernel(mesh=plsc.ScalarSubcoreMesh(axis_name="sequencer", num_cores=1), name=name, scratch_types=tuple(sems),
               compiler_params=pltpu.CompilerParams(collective_id=collective_id))
    def launch(*sem_refs):
        peers = peers_of(*_coords())
        barrier = pltpu.get_barrier_semaphore()
        for peer in peers:
            pl.semaphore_signal(barrier, inc=1, device_id=peer, device_id_type=MESH)
        pl.semaphore_wait(barrier, len(peers))

        made = [ex.make(*refs[k], sem_refs[2 * k], sem_refs[2 * k + 1]) for k, ex in enumerate(exchanges)]
        for stage in range(max(len(stages) for stages, _, _ in made)):
            for stages, _, _ in made:
                if stage < len(stages):
                    arrivals, starts = stages[stage]
                    for cp in arrivals:
                        cp.wait_recv()
                    for cp in starts:
                        cp.start()
        for _, sends, arrivals in made:
            for cp in arrivals:
                cp.wait_recv()
            for cp in sends:
                cp.wait_send()

    launch()
    return [([ref[...] for ref in bufs], [ref[...] for ref in news]) for _, bufs, news in refs]


def place_shards(name, shards, dev):
    n = len(shards)

    def body(dev_ref, *refs):
        for a in range(n):
            refs[n + a][...] = refs[a][...].astype(BF16)

    spec = pltpu.PrefetchScalarGridSpec(
        num_scalar_prefetch=1, grid=(1,),
        in_specs=[pl.BlockSpec(s.shape, lambda i, dev_ref: (0, 0)) for s in shards],
        out_specs=[pl.BlockSpec(s.shape, lambda i, dev_ref: (dev_ref[0], 0)) for s in shards])
    return pl.pallas_call(
        body, name=name, grid_spec=spec,
        out_shape=[jax.ShapeDtypeStruct((N_DEV * s.shape[0], s.shape[1]), BF16) for s in shards],
        compiler_params=_params(),
    )(dev, *shards)


def _swap(copies_of):
    def make(src_refs, buf_refs, new_refs, send_sems, recv_sems):
        copies = copies_of(src_refs, new_refs, send_sems, recv_sems)
        return [([], copies)], copies, copies

    return make


def to_sibling(grads):
    def copies_of(src_refs, new_refs, send_sems, recv_sems):
        x, y, c = _coords()
        return [pltpu.make_async_remote_copy(
            src_ref=src_refs[a].at[2 * xy + 1 - c], dst_ref=new_refs[a].at[xy], send_sem=send_sems.at[4 * a + xy],
            recv_sem=recv_sems.at[4 * a + xy], device_id=(x, y, 1 - c), device_id_type=MESH)
            for a in range(len(src_refs)) for xy in range(4)]

    return Exchange(grads, [], [jax.ShapeDtypeStruct((4,) + g.shape[1:], g.dtype) for g in grads], 4 * len(grads),
                    _swap(copies_of))


def to_chips(parts):
    def copies_of(src_refs, new_refs, send_sems, recv_sems):
        x, y, c = _coords()
        chips = [(1 - x, y), (x, 1 - y), (1 - x, 1 - y)]
        return [pltpu.make_async_remote_copy(
            src_ref=src_refs[a].at[2 * px + py], dst_ref=new_refs[a].at[2 * x + y], send_sem=send_sems.at[3 * a + j],
            recv_sem=recv_sems.at[3 * a + j], device_id=(px, py, c), device_id_type=MESH)
            for a in range(len(src_refs)) for j, (px, py) in enumerate(chips)]

    return Exchange(parts, [], [jax.ShapeDtypeStruct(p.shape, p.dtype) for p in parts], 3 * len(parts), _swap(copies_of))


def to_owners(grad):
    def copies_of(src_refs, new_refs, send_sems, recv_sems):
        x, y, c = _coords()
        copies = []
        for m in range(1, N_DEV):
            px, py, pc = x ^ (m >> 2), y ^ ((m >> 1) & 1), c ^ (m & 1)
            copies.append(pltpu.make_async_remote_copy(
                src_ref=src_refs[0].at[4 * px + 2 * py + pc], dst_ref=new_refs[0].at[4 * x + 2 * y + c],
                send_sem=send_sems.at[m - 1], recv_sem=recv_sems.at[m - 1], device_id=(px, py, pc), device_id_type=MESH))
        return copies

    return Exchange([grad], [], [jax.ShapeDtypeStruct(grad.shape, grad.dtype)], N_DEV - 1, _swap(copies_of))


def sum_cores(name, grad, other, core, after=()):
    _, r, w = other.shape

    def body(core_ref, g_ref, o_ref, *rest):
        rest[-1][...] = (g_ref[...].astype(F32) + o_ref[...].astype(F32)).astype(rest[-1].dtype)

    return pl.pallas_call(
        body, name=name,
        grid_spec=pltpu.PrefetchScalarGridSpec(
            num_scalar_prefetch=1, grid=(4,),
            in_specs=[pl.BlockSpec((1, r, w), lambda i, core_ref: (2 * i + core_ref[0], 0, 0)),
                      pl.BlockSpec((1, r, w), lambda i, core_ref: (i, 0, 0))] + [ANY] * len(after),
            out_specs=pl.BlockSpec((1, r, w), lambda i, core_ref: (i, 0, 0))),
        out_shape=jax.ShapeDtypeStruct(other.shape, other.dtype),
        compiler_params=_params(),
    )(core, grad, other, *after)


def sum_owned(name, grad, others, dev_ids, after=()):
    _, r, w = grad.shape

    def body(ids_ref, *refs):
        acc = refs[0][0]
        for k in range(1, N_DEV):
            acc = acc + refs[k][0]
        refs[-1][...] = acc

    def pick(k):
        return pl.BlockSpec((1, r, w), lambda i, ids_ref: (ids_ref[k], 0, 0))

    return pl.pallas_call(
        body, name=name,
        grid_spec=pltpu.PrefetchScalarGridSpec(
            num_scalar_prefetch=1, grid=(1,), in_specs=[pick(k) for k in range(N_DEV)] + [ANY] * len(after),
            out_specs=pl.BlockSpec((r, w), lambda i, ids_ref: (ids_ref[0], 0))),
        out_shape=jax.ShapeDtypeStruct((N_DEV * r, w), F32),
        compiler_params=_params(),
    )(dev_ids, grad, *([others] * (N_DEV - 1)), *after)


def _adamw_update(w, g, m, v):
    nm = ADAM_B1 * m + np.float32(1.0 - ADAM_B1) * g
    nv = ADAM_B2 * v + np.float32(1.0 - ADAM_B2) * (g * g)
    m_hat = nm / np.float32(1.0 - ADAM_B1 ** ADAM_STEP)
    v_hat = nv / np.float32(1.0 - ADAM_B2 ** ADAM_STEP)
    return -ADAM_LR * (m_hat / (jnp.sqrt(v_hat) + ADAM_EPS) + ADAM_WD * w), nm, nv


def adamw_of_sums(name, part, others, chip_ids, w, m, v, after):
    _, r, wd = part.shape
    halves = 2
    rows = r // halves

    def body(ids_ref, p_ref, a_ref, b_ref, c_ref, w_ref, m_ref, v_ref, after_ref, g_ref, d_ref, nm_ref, nv_ref):
        g = ((p_ref[0].astype(F32) + a_ref[0].astype(F32)) + b_ref[0].astype(F32)) + c_ref[0].astype(F32)
        g_ref[...] = g
        d_ref[...], nm_ref[...], nv_ref[...] = _adamw_update(w_ref[...], g, m_ref[...], v_ref[...])

    def pick(k):
        return pl.BlockSpec((1, rows, wd), lambda i, ids_ref: (ids_ref[k], i, 0))

    whole = pl.BlockSpec((rows, wd), lambda i, ids_ref: (i, 0))
    shape = jax.ShapeDtypeStruct((r, wd), F32)
    return pl.pallas_call(
        body, name=name,
        grid_spec=pltpu.PrefetchScalarGridSpec(
            num_scalar_prefetch=1, grid=(halves,), in_specs=[pick(0), pick(1), pick(2), pick(3), whole, whole, whole, ANY],
            out_specs=[whole] * 4),
        out_shape=[shape] * 4,
        compiler_params=_params(),
    )(chip_ids, part, others, others, others, w, m, v, after)


def adamw(name, w, g, m, v):
    def body(w_ref, g_ref, m_ref, v_ref, d_ref, nm_ref, nv_ref):
        d_ref[...], nm_ref[...], nv_ref[...] = _adamw_update(w_ref[...], g_ref[...], m_ref[...], v_ref[...])

    shape = jax.ShapeDtypeStruct(w.shape, F32)
    spec = _full_spec(w.shape)
    return pl.pallas_call(
        body, name=name, grid=(1,), in_specs=[spec] * 4, out_specs=[spec] * 3, out_shape=[shape] * 3,
        compiler_params=_params(),
    )(w, g, m, v)


def _pack_small(parts):
    flat = jnp.concatenate([parts[name].reshape(-1) for name, _ in SMALL])
    return jnp.pad(flat, (0, SMALL_ROWS * 128 - flat.shape[0])).reshape(SMALL_ROWS, 128)


def _unpack_small(packed, like):
    flat = packed.reshape(-1)
    out, at = {}, 0
    for name, size in SMALL:
        out[name] = flat[at:at + size].reshape(like[name].shape)
        at += size
    return out


def kernel(x, positions, pre_mix_norm, w_in, sgu_ln_gain, sgu_ln_bias, sgu_w_spatial, sgu_b_spatial, attn_out_norm, sgu_out_norm, w_out, post_mix_norm, pre_ffn_norm, w_gate, w_up, w_down, post_ffn_norm, loss_target, m_pre_mix_norm, m_w_in, m_sgu_ln_gain, m_sgu_ln_bias, m_sgu_w_spatial, m_sgu_b_spatial, m_attn_out_norm, m_sgu_out_norm, m_w_out, m_post_mix_norm, m_pre_ffn_norm, m_w_gate, m_w_up, m_w_down, m_post_ffn_norm, v_pre_mix_norm, v_w_in, v_sgu_ln_gain, v_sgu_ln_bias, v_sgu_w_spatial, v_sgu_b_spatial, v_attn_out_norm, v_sgu_out_norm, v_w_out, v_post_mix_norm, v_pre_ffn_norm, v_w_gate, v_w_up, v_w_down, v_post_ffn_norm):
    small_w = dict(pre_mix_norm=pre_mix_norm, sgu_ln_gain=sgu_ln_gain, sgu_ln_bias=sgu_ln_bias, sgu_w_spatial=sgu_w_spatial,
                   sgu_b_spatial=sgu_b_spatial, attn_out_norm=attn_out_norm, sgu_out_norm=sgu_out_norm,
                   post_mix_norm=post_mix_norm, pre_ffn_norm=pre_ffn_norm, post_ffn_norm=post_ffn_norm)
    small_m = dict(pre_mix_norm=m_pre_mix_norm, sgu_ln_gain=m_sgu_ln_gain, sgu_ln_bias=m_sgu_ln_bias, sgu_w_spatial=m_sgu_w_spatial,
                   sgu_b_spatial=m_sgu_b_spatial, attn_out_norm=m_attn_out_norm, sgu_out_norm=m_sgu_out_norm,
                   post_mix_norm=m_post_mix_norm, pre_ffn_norm=m_pre_ffn_norm, post_ffn_norm=m_post_ffn_norm)
    small_v = dict(pre_mix_norm=v_pre_mix_norm, sgu_ln_gain=v_sgu_ln_gain, sgu_ln_bias=v_sgu_ln_bias, sgu_w_spatial=v_sgu_w_spatial,
                   sgu_b_spatial=v_sgu_b_spatial, attn_out_norm=v_attn_out_norm, sgu_out_norm=v_sgu_out_norm,
                   post_mix_norm=v_post_mix_norm, pre_ffn_norm=v_pre_ffn_norm, post_ffn_norm=v_post_ffn_norm)
    for table in (small_w, small_m, small_v):
        table["loss_sum"] = jnp.zeros((1,), F32)

    x2d = x[0]
    target = loss_target[0]
    pos_col = positions.reshape(SEQ, 1)
    rot = _rot_consts()
    w_sp = sgu_w_spatial[0]
    bfull = jnp.repeat(sgu_b_spatial[0].T, HEAD_DIM, axis=1)

    x_i, y_i, c_i = (lax.axis_index(a).astype(jnp.int32) for a in MESH_AXES)
    dev = 4 * x_i + 2 * y_i + c_i
    core = c_i.reshape(1)
    chip = 2 * x_i + y_i
    chip_ids = jnp.stack([chip, chip ^ 1, chip ^ 2, chip ^ 3])
    dev_ids = jnp.stack([dev ^ m for m in range(N_DEV)])

    def gathered(name, bufs):
        return by_sequencer(name, [gather(bufs)], TO_GATHER)[0][0]

    def from_sibling(name, grads):
        return by_sequencer(name, [to_sibling(grads)], TO_SIBLING)[0][1]

    def from_chips(name, parts):
        return by_sequencer(name, [to_chips(parts)], TO_CHIPS)[0][1]

    (w_in_t,) = place_shards("place_w_in", [w_in[0].T], dev.reshape(1))
    (w_in_t,) = gathered("gather_w_in", [w_in_t])
    w_gate_t, w_up_t, w_out_f, w_down_f = place_shards(
        "place_weights", [w_gate[0].T, w_up[0].T, w_out[0], w_down[0]], dev.reshape(1))
    (w_out_f,) = gathered("gather_w_out", [w_out_f])
    w_gate_t, w_up_t = gathered("gather_w_gate_up", [w_gate_t, w_up_t])
    (w_down_f,) = gathered("gather_w_down", [w_down_f])

    h1, u, vs, q, k, v = in_proj(x2d, pos_col, pre_mix_norm, w_in_t, rot)
    attn_r, lse, attn = attn_fwd(q, k, v)
    (sgu,) = sgu_fwd(u, vs, sgu_ln_gain, sgu_ln_bias, w_sp, bfull)
    mix, y, x2, h2 = out_proj(attn, sgu, x2d, attn_out_norm, sgu_out_norm, w_out_f, post_mix_norm, pre_ffn_norm)
    gate, up, act, df, dx3, d_post_ffn, sq_err = ffn_fwd_loss(h2, w_gate_t, w_up_t, w_down_f, x2, post_ffn_norm, target)

    g_w_down = weight_grad("grad_w_down", act, df)
    (s_down,) = from_sibling("w_down_to_sibling", [g_w_down])
    dgate, dup, dx2, dy, d_pre_ffn, d_post_mix = ffn_bwd(
        df, w_down_f, gate, up, w_gate_t, w_up_t, x2, pre_ffn_norm, dx3, y, post_mix_norm, after=[g_w_down])
    p_down = sum_cores("sum_cores_down", g_w_down, s_down, core, after=[dy])
    (c_down,) = from_chips("w_down_to_chips", [p_down])
    g_w_gate, g_w_up = weight_grads("grad_w_gate_up", [dgate, dup], h2, after=[p_down])
    s_gate, s_up = from_sibling("w_gate_up_to_sibling", [g_w_gate, g_w_up])
    g_w_out = weight_grad("grad_w_out", mix, dy, after=[g_w_up, c_down])
    p_gate = sum_cores("sum_cores_gate", g_w_gate, s_gate, core, after=[g_w_out])
    p_up = sum_cores("sum_cores_up", g_w_up, s_up, core, after=[g_w_out])
    c_gate, c_up = from_chips("w_gate_up_to_chips", [p_gate, p_up])
    (s_out,) = from_sibling("w_out_to_sibling", [g_w_out])
    dsgu, d_attn_out, d_sgu_out, dattn_r = mix_bwd(dy, w_out_f, attn, sgu, attn_out_norm, sgu_out_norm, after=[p_gate, p_up])
    du, dvs, d_ln_gain, d_ln_bias, d_w_sp, d_bfull = sgu_bwd(u, vs, dsgu, sgu_ln_gain, sgu_ln_bias, w_sp, bfull)
    dq, dk, dv = attn_bwd(q, k, v, attn_r, lse, dattn_r, _to_residue_order(pos_col), rot)
    p_out = sum_cores("sum_cores_out", g_w_out, s_out, core, after=[dq])
    (c_out,) = from_chips("w_out_to_chips", [p_out])
    dq, dk, dv = (_from_residue_order(t) for t in (dq, dk, dv))
    dproj = jnp.concatenate([dq, dk, dv, du, dvs], axis=1)
    g_w_in = weight_grad("grad_w_in", dproj, h1, after=[c_gate, c_up])
    (s_in,) = from_sibling("w_in_to_sibling", [g_w_in])
    grad_x, d_pre_mix = in_bwd(dproj, w_in_t, x2d, pre_mix_norm, dx2, after=[g_w_in, c_out])
    p_in = sum_cores("sum_cores_in", g_w_in, s_in, core, after=[d_pre_mix])

    d_b_sp = d_bfull.reshape(CHUNK, N_GROUPS, HEAD_DIM).sum(axis=-1).T
    small_g = _pack_small(dict(pre_mix_norm=d_pre_mix, sgu_ln_gain=d_ln_gain, sgu_ln_bias=d_ln_bias, sgu_w_spatial=d_w_sp,
                               sgu_b_spatial=d_b_sp, attn_out_norm=d_attn_out, sgu_out_norm=d_sgu_out,
                               post_mix_norm=d_post_mix, pre_ffn_norm=d_pre_ffn, post_ffn_norm=d_post_ffn,
                               loss_sum=sq_err))
    small_g = small_g.reshape(N_DEV, SMALL_ROWS // N_DEV, 128)
    (_, (c_in,)), (_, (o_small,)) = by_sequencer(
        "last_sums_to_owners", [to_chips([p_in]), to_owners(small_g)], TO_ALL)

    big, last = {}, p_in
    for name, w, p, c, m, vv, transposed in (
            ("w_down", w_down, p_down, c_down, m_w_down, v_w_down, False), ("w_gate", w_gate, p_gate, c_gate, m_w_gate, v_w_gate, True),
            ("w_up", w_up, p_up, c_up, m_w_up, v_w_up, True), ("w_out", w_out, p_out, c_out, m_w_out, v_w_out, False),
            ("w_in", w_in, p_in, c_in, m_w_in, v_w_in, True)):
        if name == "w_in":
            last = sum_owned("sum_small", small_g, o_small, dev_ids, after=[last])
            (all_small,) = gathered("gather_small_grads", [last])
        turn = (lambda t: t.T) if transposed else (lambda t: t)
        outs = adamw_of_sums("adamw_" + name, p, c, chip_ids, turn(w[0]), turn(m[0]), turn(vv[0]), last)
        big[name], last = tuple(turn(t)[None] for t in outs), outs[0]
    sd, snm, snv = adamw("adamw_small", _pack_small(small_w), all_small, _pack_small(small_m), _pack_small(small_v))
    sg, sd, snm, snv = (_unpack_small(t, small_w) for t in (all_small, sd, snm, snv))
    loss = sg["loss_sum"][0] * np.float32(0.5 / D_MODEL)

    names = ["pre_mix_norm", "w_in", "sgu_ln_gain", "sgu_ln_bias", "sgu_w_spatial", "sgu_b_spatial", "attn_out_norm",
             "sgu_out_norm", "w_out", "post_mix_norm", "pre_ffn_norm", "w_gate", "w_up", "w_down", "post_ffn_norm"]
    outs = [loss, grad_x[None]]
    for i, table in enumerate((sg, sd, snm, snv)):
        for name in names:
            outs.append(big[name][i] if name in big else table[name])
    return tuple(outs)
```

```python
import numpy as np
import jax
import jax.numpy as jnp
from jax import lax
from jax.experimental import pallas as pl
from jax.experimental.pallas import tpu as pltpu
from jax.experimental.pallas import tpu_sc as plsc

F32 = jnp.float32
BF16 = jnp.bfloat16

SEQ = 2048
D_MODEL = 1024
ATTN_W = 512
SGU_W = 512
HEAD_DIM = 64
N_GROUPS = 8
CHUNK = 128
D_FF = 2816
IN_W = 3 * ATTN_W + 2 * SGU_W
DILATIONS = (1, 4, 16)
ROPE_THETA = 500000.0
ROT_DIM = 16
ROT_HALF = 8
RMS_EPS = 1e-6
LN_EPS = 1e-5
Q_SCALE = 0.125
NEG = -1e30

N_DEV = 8
MESH_AXES = ("x", "y", "c")
MESH = pl.DeviceIdType.MESH

ADAM_LR = 0.001
ADAM_B1 = 0.9
ADAM_B2 = 0.999
ADAM_EPS = 1e-08
ADAM_WD = 0.01
ADAM_STEP = 10

VMEM_LIMIT = 60 * 1024 * 1024
ANY = pl.BlockSpec(memory_space=pl.ANY)

SMALL = (("pre_mix_norm", 1024), ("sgu_ln_gain", 512), ("sgu_ln_bias", 512), ("sgu_w_spatial", 8 * 128 * 128),
         ("sgu_b_spatial", 1024), ("attn_out_norm", 512), ("sgu_out_norm", 512), ("post_mix_norm", 1024),
         ("pre_ffn_norm", 1024), ("post_ffn_norm", 1024), ("loss_sum", 1))
SMALL_ROWS = 1152


def _params(sem=("arbitrary",)):
    return pltpu.CompilerParams(dimension_semantics=sem, vmem_limit_bytes=VMEM_LIMIT)


def _dot(a, b):
    return jnp.dot(a, b, preferred_element_type=F32)


def _dot_nt(a, b):
    return lax.dot_general(a, b, (((1,), (1,)), ((), ())), preferred_element_type=F32)


def _dot_tn(a, b):
    return lax.dot_general(a, b, (((0,), (0,)), ((), ())), preferred_element_type=F32)


def _rms(z):
    return lax.rsqrt(jnp.mean(z * z, axis=-1, keepdims=True) + RMS_EPS)


def _rms_bwd(z, gain, d):
    r = _rms(z)
    n = z * r
    dn = d * gain
    dz = r * (dn - n * jnp.mean(dn * n, axis=-1, keepdims=True))
    return dz, jnp.sum(d * n, axis=0, keepdims=True)


def _gelu(z):
    return 0.5 * z * (1.0 + lax.erf(z * np.float32(1.0 / np.sqrt(2.0))))


def _gelu_grad(z):
    cdf = 0.5 * (1.0 + lax.erf(z * np.float32(1.0 / np.sqrt(2.0))))
    return cdf + z * jnp.exp(-0.5 * z * z) * np.float32(1.0 / np.sqrt(2.0 * np.pi))


def _rot_tables(pos_col, invf, ma, mb):
    ang = pos_col.astype(F32) * invf
    s = jnp.sin(ang)
    return jnp.cos(ang), s * ma, s * mb


def _rot(t, c, sa, sb):
    return t * c + pltpu.roll(t, 120, 1) * sa + pltpu.roll(t, 8, 1) * sb


def _rot_t(d, c, sa, sb):
    return d * c + pltpu.roll(d * sa, 8, 1) + pltpu.roll(d * sb, 120, 1)


def _rot_consts():
    lane = np.arange(128) % HEAD_DIM
    inv_freq = (np.float32(ROPE_THETA) ** (-np.arange(0, ROT_DIM, 2, dtype=np.float32) / np.float32(ROT_DIM))).astype(np.float32)
    invf = np.where(lane < ROT_DIM, inv_freq[lane % ROT_HALF], 0.0).astype(np.float32)
    ma = np.where(lane < ROT_HALF, -1.0, 0.0).astype(np.float32)
    mb = np.where((lane >= ROT_HALF) & (lane < ROT_DIM), 1.0, 0.0).astype(np.float32)
    return jnp.asarray(invf[None]), jnp.asarray(ma[None]), jnp.asarray(mb[None])


def _row_spec(tm, w):
    return pl.BlockSpec((tm, w), lambda i: (i, 0))


def _full_spec(shape):
    return pl.BlockSpec(shape, lambda i: (0,) * len(shape))


def _weight_spec(shape):
    return pl.BlockSpec(shape, lambda i: (0,) * len(shape), pipeline_mode=pl.Buffered(1))


RES = 16


def _residue_scratch(n_arrays, tm, width):
    return [pltpu.VMEM((2, n_arrays, tm // RES, RES, width), F32), pltpu.SemaphoreType.DMA((2, n_arrays, RES))]


def _to_residue_rows(tiles, outs, scratch, sems, tm, n_steps):
    i = pl.program_id(0)
    slot = i % 2
    per = tm // RES

    def copies(step, s):
        return [pltpu.make_async_copy(scratch.at[s, a, :, b, :],
                                      outs[a].at[pl.ds(pl.multiple_of(b * (SEQ // RES) + per * step, per), per), :],
                                      sems.at[s, a, b]) for a in range(len(outs)) for b in range(RES)]

    @pl.when(i >= 2)
    def _():
        for cp in copies(i - 2, slot):
            cp.wait()

    for a, tile in enumerate(tiles):
        scratch[slot, a] = tile.reshape(per, RES, tile.shape[-1])
    for cp in copies(i, slot):
        cp.start()

    @pl.when(i == n_steps - 1)
    def _():
        for cp in copies(i - 1, 1 - slot) + copies(i, slot):
            cp.wait()


def in_proj(x, pos_col, g1, w_in_t, rot):
    tm = 512
    n_steps = SEQ // tm

    def body(x_ref, pos_ref, g_ref, w_ref, invf_ref, ma_ref, mb_ref, h_ref, u_ref, vs_ref, q_ref, k_ref, v_ref, scratch, sems):
        xf = x_ref[...]
        h = (xf * _rms(xf) * g_ref[...]).astype(BF16)
        h_ref[...] = h
        proj = _dot_nt(h, w_ref[...])
        c, sa, sb = _rot_tables(pos_ref[...], invf_ref[...], ma_ref[...], mb_ref[...])
        slabs = range(ATTN_W // 128)
        q = jnp.concatenate([_rot(proj[:, j * 128:(j + 1) * 128], c, sa, sb) * Q_SCALE for j in slabs], axis=1)
        k = jnp.concatenate([_rot(proj[:, ATTN_W + j * 128:ATTN_W + (j + 1) * 128], c, sa, sb) for j in slabs], axis=1)
        u_ref[...] = proj[:, 3 * ATTN_W:3 * ATTN_W + SGU_W]
        vs_ref[...] = proj[:, 3 * ATTN_W + SGU_W:]
        _to_residue_rows([q, k, proj[:, 2 * ATTN_W:3 * ATTN_W]], [q_ref, k_ref, v_ref], scratch, sems, tm, n_steps)

    act = jax.ShapeDtypeStruct((SEQ, 512), F32)
    return _call(
        "in_proj", body, n_steps,
        [_row_spec(tm, D_MODEL), _row_spec(tm, 1), _full_spec((1, D_MODEL)), _weight_spec((IN_W, D_MODEL)),
         _full_spec((1, 128)), _full_spec((1, 128)), _full_spec((1, 128))],
        [_row_spec(tm, D_MODEL)] + [_row_spec(tm, 512)] * 2 + [ANY] * 3,
        [jax.ShapeDtypeStruct((SEQ, D_MODEL), BF16)] + [act] * 5,
        (x, pos_col, g1, w_in_t, *rot), scratch_shapes=_residue_scratch(3, tm, ATTN_W))


def _to_residue_order(t):
    return t.reshape(SEQ // RES, RES, -1).transpose(1, 0, 2).reshape(t.shape)


def _from_residue_order(t):
    return t.reshape(RES, SEQ // RES, -1).transpose(1, 0, 2).reshape(t.shape)


def _block_rows(d, r, n):
    if d == 16:
        slices = [(128 * r, 128)]
    elif d == 4:
        slices = [(128 * (4 * b + r) + 32 * n, 32) for b in range(4)]
    else:
        slices = [(128 * b + 8 * n, 8) for b in range(RES)]
    return [(s if isinstance(s, int) else pl.multiple_of(s, z), z) for s, z in slices]


def _block_step(d, i):
    if d == 16:
        return i
    if d == 4:
        return 4 * (i & 31) + (i >> 5)
    return 16 * (i & 7) + (i >> 3)


def _attn_masks(d):
    row2 = _block_step(d, lax.broadcasted_iota(jnp.int32, (128, 256), 0))
    col2 = lax.broadcasted_iota(jnp.int32, (128, 256), 1)
    key2 = _block_step(d, col2 & 127)
    mask2 = jnp.logical_or(jnp.logical_and(col2 < 128, key2 >= row2), jnp.logical_and(col2 >= 128, key2 <= row2))
    row1 = _block_step(d, lax.broadcasted_iota(jnp.int32, (128, 128), 0))
    col1 = lax.broadcasted_iota(jnp.int32, (128, 128), 1)
    return col1 < HEAD_DIM, _block_step(d, col1) <= row1, mask2


def _load_rows(ref, slices):
    parts = [ref[pl.ds(s, z), :] for s, z in slices]
    return parts[0] if len(parts) == 1 else jnp.concatenate(parts, axis=0)


def _for_each_group(fn):
    for p, d in enumerate(DILATIONS):
        masks = _attn_masks(d)
        if d == 16:
            def group(i, carry, p=p, masks=masks):
                fn(p, masks, [(_block_rows(16, 4 * i + g, 0), None) for g in range(4)])
                return carry

            lax.fori_loop(0, 4, group, 0)
        elif d == 4:
            fn(p, masks, [(_block_rows(4, r, 0), None) for r in range(4)])

            def group(i, carry, p=p, masks=masks):
                fn(p, masks, [(_block_rows(4, r, i + 1), _block_rows(4, r, i)) for r in range(4)])
                return carry

            lax.fori_loop(0, 3, group, 0)
        else:
            fn(p, masks, [(_block_rows(1, 0, 0), None)])

            def group(i, carry, p=p, masks=masks):
                fn(p, masks, [(_block_rows(1, 0, 3 * i + g + 1), _block_rows(1, 0, 3 * i + g)) for g in range(3)])
                return carry

            lax.fori_loop(0, 5, group, 0)


def attn_fwd(q, k, v):
    def body(q_ref, k_ref, v_ref, o_ref, lse_ref, nat_ref, op_ref, lp_ref, sems):
        def group(p, masks, blocks):
            head0, mask1, mask2 = masks
            heads = (head0, jnp.logical_not(head0))
            keys = [rows if prev is None else prev + rows for rows, prev in blocks]
            mask = [mask1 if prev is None else mask2 for _, prev in blocks]
            qb = [_load_rows(q_ref, rows) for rows, _ in blocks]
            kk = [_load_rows(k_ref, ks).astype(BF16) for ks in keys]
            vv = [_load_rows(v_ref, ks).astype(BF16) for ks in keys]
            chains = [(g, hm) for g in range(len(blocks)) for hm in heads]
            s = [jnp.where(mask[g], _dot_nt(jnp.where(hm, qb[g], 0.0).astype(BF16), kk[g]), NEG) for g, hm in chains]
            m = [jnp.max(t, axis=-1, keepdims=True) for t in s]
            e = [jnp.exp(t - mt) for t, mt in zip(s, m)]
            l = [jnp.sum(t, axis=-1, keepdims=True) for t in e]
            pv = [_dot(t.astype(BF16), vv[g]) for t, (g, _) in zip(e, chains)]
            for g, (rows, _) in enumerate(blocks):
                o_blk = jnp.where(head0, pv[2 * g] / l[2 * g], pv[2 * g + 1] / l[2 * g + 1])
                l_blk = jnp.where(head0, jnp.broadcast_to(m[2 * g] + jnp.log(l[2 * g]), (128, 128)),
                                  jnp.broadcast_to(m[2 * g + 1] + jnp.log(l[2 * g + 1]), (128, 128)))
                at = 0
                for start, size in rows:
                    op_ref[p, pl.ds(start, size), :] = o_blk[at:at + size]
                    lp_ref[p, pl.ds(start, size), :] = l_blk[at:at + size]
                    at += size

        _for_each_group(group)

        def combine(i, carry):
            rows = pl.ds(pl.multiple_of(i * 256, 256), 256)
            ls = [lp_ref[p, rows, :] for p in range(3)]
            m = jnp.maximum(jnp.maximum(ls[0], ls[1]), ls[2])
            lse = m + jnp.log(jnp.exp(ls[0] - m) + jnp.exp(ls[1] - m) + jnp.exp(ls[2] - m))
            o = jnp.zeros((256, 128), F32)
            for p in range(3):
                o = o + jnp.exp(ls[p] - lse) * op_ref[p, rows, :]
            o_ref[rows, :] = o
            lse_ref[rows, :] = lse
            return carry

        lax.fori_loop(0, SEQ // 256, combine, 0)

        lanes = pl.ds(pl.multiple_of(pl.program_id(0) * 128, 128), 128)
        back = [pltpu.make_async_copy(o_ref.at[pl.ds(b * (SEQ // RES), SEQ // RES), :], nat_ref.at[:, b, lanes], sems.at[b])
                for b in range(RES)]
        for cp in back:
            cp.start()
        for cp in back:
            cp.wait()

    slab = pl.BlockSpec((SEQ, 128), lambda i: (0, i))
    out = jax.ShapeDtypeStruct((SEQ, ATTN_W), F32)
    attn_r, lse, attn = _call(
        "attn_fwd", body, ATTN_W // 128, [slab] * 3, [slab] * 2 + [ANY],
        [out, out, jax.ShapeDtypeStruct((SEQ // RES, RES, ATTN_W), F32)], (q, k, v),
        scratch_shapes=[pltpu.VMEM((3, SEQ, 128), F32), pltpu.VMEM((3, SEQ, 128), F32), pltpu.SemaphoreType.DMA((RES,))])
    return attn_r, lse, attn.reshape(SEQ, ATTN_W)


def _causal_weights(w_ref):
    row = lax.broadcasted_iota(jnp.int32, (CHUNK, CHUNK), 0)
    col = lax.broadcasted_iota(jnp.int32, (CHUNK, CHUNK), 1)
    return [jnp.where(col <= row, w_ref[g], 0.0).astype(BF16) for g in range(N_GROUPS)], col <= row


def _sgu_chunk_fwd(u, vs, lg, lb, wc, bfull, head0):
    ug = _gelu(u)
    vg = _gelu(vs)
    xc = vg - jnp.mean(vg, axis=-1, keepdims=True)
    rstd = lax.rsqrt(jnp.mean(xc * xc, axis=-1, keepdims=True) + LN_EPS)
    xhat = xc * rstd
    vn = xhat * lg + lb
    mixed = []
    for gp in range(SGU_W // 128):
        vp = vn[:, gp * 128:(gp + 1) * 128].astype(BF16)
        mixed.append(jnp.where(head0, _dot(wc[2 * gp], vp), _dot(wc[2 * gp + 1], vp)))
    ms = jnp.concatenate(mixed, axis=1) + bfull
    return ug, xhat, rstd, vn, ms


def sgu_fwd(u, vs, lg, lb, w_sp, bfull):
    cpb = 4

    def body(u_ref, vs_ref, lg_ref, lb_ref, w_ref, b_ref, o_ref):
        wc, _ = _causal_weights(w_ref)
        head0 = lax.broadcasted_iota(jnp.int32, (CHUNK, 128), 1) < HEAD_DIM
        for ci in range(cpb):
            rows = pl.ds(ci * CHUNK, CHUNK)
            ug, _, _, _, ms = _sgu_chunk_fwd(u_ref[rows, :], vs_ref[rows, :], lg_ref[...], lb_ref[...], wc, b_ref[...], head0)
            o_ref[rows, :] = ug * ms

    tm = cpb * CHUNK
    return _call(
        "sgu_fwd", body, SEQ // tm,
        [_row_spec(tm, SGU_W), _row_spec(tm, SGU_W), _full_spec((1, SGU_W)), _full_spec((1, SGU_W)),
         _full_spec((N_GROUPS, CHUNK, CHUNK)), _full_spec((CHUNK, SGU_W))],
        [_row_spec(tm, SGU_W)], [jax.ShapeDtypeStruct((SEQ, SGU_W), F32)],
        (u, vs, lg, lb, w_sp, bfull))


def out_proj(attn, sgu, x, ga, gs, w_out, gpm, gpf):
    tm = 512

    def body(a_ref, s_ref, x_ref, ga_ref, gs_ref, w_ref, gpm_ref, gpf_ref, mix_ref, y_ref, x2_ref, h2_ref):
        a = a_ref[...]
        s = s_ref[...]
        an = (a * _rms(a) * ga_ref[...]).astype(BF16)
        sn = (s * _rms(s) * gs_ref[...]).astype(BF16)
        mix_ref[:, :ATTN_W] = an
        mix_ref[:, ATTN_W:] = sn
        y = _dot(an, w_ref[:ATTN_W, :]) + _dot(sn, w_ref[ATTN_W:, :])
        y_ref[...] = y
        x2 = x_ref[...] + y * _rms(y) * gpm_ref[...]
        x2_ref[...] = x2
        h2_ref[...] = (x2 * _rms(x2) * gpf_ref[...]).astype(BF16)

    wide = jax.ShapeDtypeStruct((SEQ, D_MODEL), F32)
    wide16 = jax.ShapeDtypeStruct((SEQ, D_MODEL), BF16)
    return _call(
        "out_proj", body, SEQ // tm,
        [_row_spec(tm, ATTN_W), _row_spec(tm, SGU_W), _row_spec(tm, D_MODEL), _full_spec((1, ATTN_W)),
         _full_spec((1, SGU_W)), _weight_spec((D_MODEL, D_MODEL)), _full_spec((1, D_MODEL)), _full_spec((1, D_MODEL))],
        [_row_spec(tm, D_MODEL)] * 4, [wide16, wide, wide, wide16],
        (attn, sgu, x, ga, gs, w_out, gpm, gpf))


def ffn_up(h2, w_gate_t, w_up_t):
    tm = 256

    def body(h_ref, wg_ref, wu_ref, g_ref, u_ref, a_ref):
        h = h_ref[...]
        g = _dot_nt(h, wg_ref[...])
        u = _dot_nt(h, wu_ref[...])
        g_ref[...] = g.astype(BF16)
        u_ref[...] = u.astype(BF16)
        a_ref[...] = (g * jax.nn.sigmoid(g) * u).astype(BF16)

    ff = jax.ShapeDtypeStruct((SEQ, D_FF), BF16)
    return _call(
        "ffn_up", body, SEQ // tm,
        [_row_spec(tm, D_MODEL), _weight_spec((D_FF, D_MODEL)), _weight_spec((D_FF, D_MODEL))],
        [_row_spec(tm, D_FF)] * 3, [ff, ff, jax.ShapeDtypeStruct((SEQ, D_FF), BF16)],
        (h2, w_gate_t, w_up_t))


def ffn_down_loss(act, w_down, x2, gpo, target):
    tm = 512

    def body(a_ref, w_ref, x2_ref, g_ref, t_ref, df_ref, dx3_ref, dg_ref, loss_ref):
        f = _dot(a_ref[...], w_ref[...])
        gain = g_ref[...]
        err = x2_ref[...] + f * _rms(f) * gain - t_ref[...]
        dx3 = err * np.float32(1.0 / D_MODEL)
        dx3_ref[...] = dx3
        df, dg = _rms_bwd(f, gain, dx3)
        df_ref[...] = df.astype(BF16)

        @pl.when(pl.program_id(0) == 0)
        def _():
            dg_ref[...] = jnp.zeros_like(dg_ref)
            loss_ref[...] = jnp.zeros_like(loss_ref)

        dg_ref[...] += dg
        loss_ref[...] += jnp.sum(err * err, axis=(0, 1), keepdims=True)

    return pl.pallas_call(
        body, name="ffn_down_loss", grid=(SEQ // tm,),
        in_specs=[_row_spec(tm, D_FF), _weight_spec((D_FF, D_MODEL)), _row_spec(tm, D_MODEL), _full_spec((1, D_MODEL)),
                  _row_spec(tm, D_MODEL)],
        out_specs=[_row_spec(tm, D_MODEL), _row_spec(tm, D_MODEL), _full_spec((1, D_MODEL)), _full_spec((1, 1))],
        out_shape=[jax.ShapeDtypeStruct((SEQ, D_MODEL), BF16), jax.ShapeDtypeStruct((SEQ, D_MODEL), F32),
                   jax.ShapeDtypeStruct((1, D_MODEL), F32), jax.ShapeDtypeStruct((1, 1), F32)],
        compiler_params=_params(),
    )(act, w_down, x2, gpo, target)


def ffn_bwd(df, w_down, gate, up, w_gate_t, w_up_t, x2, gpf, dx3, y, gpm, after=()):
    tm = 256

    def body(df_ref, wd_ref, g_ref, u_ref, wg_ref, wu_ref, x2_ref, gpf_ref, dx3_ref, y_ref, gpm_ref,
             dg_ref, du_ref, dx2_ref, dy_ref, dgpf_ref, dgpm_ref):
        dact = _dot_nt(df_ref[...], wd_ref[...])
        g = g_ref[...].astype(F32)
        s = jax.nn.sigmoid(g)
        dup = (dact * g * s).astype(BF16)
        dgate = (dact * u_ref[...].astype(F32) * (s * (1.0 + g * (1.0 - s)))).astype(BF16)
        du_ref[...] = dup
        dg_ref[...] = dgate
        dh2 = _dot(dgate, wg_ref[...]) + _dot(dup, wu_ref[...])
        dz, dgpf = _rms_bwd(x2_ref[...], gpf_ref[...], dh2)
        dx2 = dx3_ref[...] + dz
        dx2_ref[...] = dx2
        dy, dgpm = _rms_bwd(y_ref[...], gpm_ref[...], dx2)
        dy_ref[...] = dy.astype(BF16)

        @pl.when(pl.program_id(0) == 0)
        def _():
            dgpf_ref[...] = jnp.zeros_like(dgpf_ref)
            dgpm_ref[...] = jnp.zeros_like(dgpm_ref)

        dgpf_ref[...] += dgpf
        dgpm_ref[...] += dgpm

    vec = jax.ShapeDtypeStruct((1, D_MODEL), F32)
    ff16 = jax.ShapeDtypeStruct((SEQ, D_FF), BF16)
    return _call(
        "ffn_bwd", body, SEQ // tm,
        [_row_spec(tm, D_MODEL), _weight_spec((D_FF, D_MODEL)), _row_spec(tm, D_FF), _row_spec(tm, D_FF),
         _weight_spec((D_FF, D_MODEL)), _weight_spec((D_FF, D_MODEL)), _row_spec(tm, D_MODEL), _full_spec((1, D_MODEL)),
         _row_spec(tm, D_MODEL), _row_spec(tm, D_MODEL), _full_spec((1, D_MODEL))],
        [_row_spec(tm, D_FF), _row_spec(tm, D_FF), _row_spec(tm, D_MODEL), _row_spec(tm, D_MODEL),
         _full_spec((1, D_MODEL)), _full_spec((1, D_MODEL))],
        [ff16, ff16, jax.ShapeDtypeStruct((SEQ, D_MODEL), F32), jax.ShapeDtypeStruct((SEQ, D_MODEL), BF16), vec, vec],
        (df, w_down, gate, up, w_gate_t, w_up_t, x2, gpf, dx3, y, gpm), after=after)


def weight_grads(name, lhs, b, after=()):
    m, n, k = lhs[0].shape[1], b.shape[1], len(lhs)
    tr = 256

    def body(*refs):
        for a_ref, o_ref in zip(refs[:k], refs[k + 1:]):
            o_ref[...] = _dot_tn(a_ref[...], refs[k][...]).astype(BF16)

    outs = _call(
        name, body, m // tr, [pl.BlockSpec((SEQ, tr), lambda i: (0, i))] * k + [_weight_spec((SEQ, n))],
        [_row_spec(tr, n)] * k, [jax.ShapeDtypeStruct((m, n), BF16)] * k, (*lhs, b), after=after)
    return [out.reshape(N_DEV, m // N_DEV, n) for out in outs]


def weight_grad(name, a, b, after=()):
    return weight_grads(name, [a], b, after)[0]


def mix_bwd(dy, w_out, attn, sgu, ga, gs, after=()):
    tm = 512
    n_steps = SEQ // tm

    def body(dy_ref, w_ref, a_ref, s_ref, ga_ref, gs_ref, ds_ref, dga_ref, dgs_ref, da_ref, scratch, sems):
        dy = dy_ref[...]
        da, dga = _rms_bwd(a_ref[...], ga_ref[...], _dot_nt(dy, w_ref[:ATTN_W, :]))
        ds, dgs = _rms_bwd(s_ref[...], gs_ref[...], _dot_nt(dy, w_ref[ATTN_W:, :]))
        ds_ref[...] = ds
        _to_residue_rows([da], [da_ref], scratch, sems, tm, n_steps)

        @pl.when(pl.program_id(0) == 0)
        def _():
            dga_ref[...] = jnp.zeros_like(dga_ref)
            dgs_ref[...] = jnp.zeros_like(dgs_ref)

        dga_ref[...] += dga
        dgs_ref[...] += dgs

    half = jax.ShapeDtypeStruct((SEQ, 512), F32)
    vec = jax.ShapeDtypeStruct((1, 512), F32)
    return _call(
        "mix_bwd", body, n_steps,
        [_row_spec(tm, D_MODEL), _weight_spec((D_MODEL, D_MODEL)), _row_spec(tm, 512), _row_spec(tm, 512),
         _full_spec((1, 512)), _full_spec((1, 512))],
        [_row_spec(tm, 512), _full_spec((1, 512)), _full_spec((1, 512)), ANY],
        [half, vec, vec, half], (dy, w_out, attn, sgu, ga, gs), scratch_shapes=_residue_scratch(1, tm, ATTN_W), after=after)


def sgu_bwd(u, vs, dsgu, lg, lb, w_sp, bfull):
    cpb = 4

    def body(u_ref, vs_ref, d_ref, lg_ref, lb_ref, w_ref, b_ref, du_ref, dvs_ref, dlg_ref, dlb_ref, dw_ref, db_ref):
        wc, causal = _causal_weights(w_ref)
        head0 = lax.broadcasted_iota(jnp.int32, (CHUNK, 128), 1) < HEAD_DIM
        lg = lg_ref[...]

        @pl.when(pl.program_id(0) == 0)
        def _():
            dlg_ref[...] = jnp.zeros_like(dlg_ref)
            dlb_ref[...] = jnp.zeros_like(dlb_ref)
            dw_ref[...] = jnp.zeros_like(dw_ref)
            db_ref[...] = jnp.zeros_like(db_ref)

        for ci in range(cpb):
            rows = pl.ds(ci * CHUNK, CHUNK)
            u = u_ref[rows, :]
            vs = vs_ref[rows, :]
            d = d_ref[rows, :]
            ug, xhat, rstd, vn, ms = _sgu_chunk_fwd(u, vs, lg, lb_ref[...], wc, b_ref[...], head0)
            du_ref[rows, :] = (d * ms * _gelu_grad(u)).astype(BF16)
            dms = d * ug
            db_ref[...] += dms
            dvn = []
            for gp in range(SGU_W // 128):
                dmp = dms[:, gp * 128:(gp + 1) * 128]
                dm0 = jnp.where(head0, dmp, 0.0).astype(BF16)
                dm1 = jnp.where(head0, 0.0, dmp).astype(BF16)
                vp = vn[:, gp * 128:(gp + 1) * 128].astype(BF16)
                dw_ref[2 * gp] += _dot_nt(dm0, vp)
                dw_ref[2 * gp + 1] += _dot_nt(dm1, vp)
                dvn.append(_dot_tn(wc[2 * gp], dm0) + _dot_tn(wc[2 * gp + 1], dm1))
            dvn = jnp.concatenate(dvn, axis=1)
            dlg_ref[...] += jnp.sum(dvn * xhat, axis=0, keepdims=True)
            dlb_ref[...] += jnp.sum(dvn, axis=0, keepdims=True)
            dxh = dvn * lg
            dvg = rstd * (dxh - jnp.mean(dxh, axis=-1, keepdims=True) - xhat * jnp.mean(dxh * xhat, axis=-1, keepdims=True))
            dvs_ref[rows, :] = (dvg * _gelu_grad(vs)).astype(BF16)

        @pl.when(pl.program_id(0) == pl.num_programs(0) - 1)
        def _():
            for g in range(N_GROUPS):
                dw_ref[g] = jnp.where(causal, dw_ref[g], 0.0)

    tm = cpb * CHUNK
    half16 = jax.ShapeDtypeStruct((SEQ, SGU_W), BF16)
    vec = jax.ShapeDtypeStruct((1, SGU_W), F32)
    return _call(
        "sgu_bwd", body, SEQ // tm,
        [_row_spec(tm, SGU_W)] * 3 + [_full_spec((1, SGU_W)), _full_spec((1, SGU_W)),
                                      _full_spec((N_GROUPS, CHUNK, CHUNK)), _full_spec((CHUNK, SGU_W))],
        [_row_spec(tm, SGU_W), _row_spec(tm, SGU_W), _full_spec((1, SGU_W)), _full_spec((1, SGU_W)),
         _full_spec((N_GROUPS, CHUNK, CHUNK)), _full_spec((CHUNK, SGU_W))],
        [half16, half16, vec, vec, jax.ShapeDtypeStruct((N_GROUPS, CHUNK, CHUNK), F32),
         jax.ShapeDtypeStruct((CHUNK, SGU_W), F32)],
        (u, vs, dsgu, lg, lb, w_sp, bfull))


def attn_bwd(q, k, v, o, lse, do, pos_col, rot):
    def body(q_ref, k_ref, v_ref, o_ref, lse_ref, do_ref, pos_ref, invf_ref, ma_ref, mb_ref,
             dq_ref, dk_ref, dv_ref, dqa_ref, dka_ref, dva_ref, dlt_ref, rot_ref):
        dqa_ref[...] = jnp.zeros_like(dqa_ref)
        dka_ref[...] = jnp.zeros_like(dka_ref)
        dva_ref[...] = jnp.zeros_like(dva_ref)

        def delta(i, carry):
            rows = pl.ds(pl.multiple_of(i * 256, 256), 256)
            prod = do_ref[rows, :] * o_ref[rows, :]
            h0 = lax.broadcasted_iota(jnp.int32, (256, 128), 1) < HEAD_DIM
            d0 = jnp.sum(jnp.where(h0, prod, 0.0), axis=-1, keepdims=True)
            d1 = jnp.sum(jnp.where(h0, 0.0, prod), axis=-1, keepdims=True)
            dlt_ref[rows, :] = jnp.where(h0, d0, d1)
            return carry

        lax.fori_loop(0, SEQ // 256, delta, 0)

        def add_rows(ref, slices, val):
            at = 0
            for start, size in slices:
                ref[pl.ds(start, size), :] += val[at:at + size]
                at += size

        def group(p, masks, blocks):
            head0, mask1, mask2 = masks
            heads = (head0, jnp.logical_not(head0))
            keys = [rows if prev is None else prev + rows for rows, prev in blocks]
            mask = [mask1 if prev is None else mask2 for _, prev in blocks]
            kk = [_load_rows(k_ref, ks).astype(BF16) for ks in keys]
            vv = [_load_rows(v_ref, ks).astype(BF16) for ks in keys]
            qb = [_load_rows(q_ref, rows) for rows, _ in blocks]
            dob = [_load_rows(do_ref, rows) for rows, _ in blocks]
            lse_b = [_load_rows(lse_ref, rows) for rows, _ in blocks]
            dlt_b = [_load_rows(dlt_ref, rows) for rows, _ in blocks]
            chains = [(g, h) for g in range(len(blocks)) for h in range(2)]
            qm = [jnp.where(heads[h], qb[g], 0.0).astype(BF16) for g, h in chains]
            dom = [jnp.where(heads[h], dob[g], 0.0).astype(BF16) for g, h in chains]
            s = [_dot_nt(qm[c], kk[g]) for c, (g, h) in enumerate(chains)]
            dp = [_dot_nt(dom[c], vv[g]) for c, (g, h) in enumerate(chains)]
            pr = [jnp.where(mask[g], jnp.exp(s[c] - lse_b[g][:, h * HEAD_DIM:h * HEAD_DIM + 1]), 0.0)
                  for c, (g, h) in enumerate(chains)]
            ds = [(pr[c] * (dp[c] - dlt_b[g][:, h * HEAD_DIM:h * HEAD_DIM + 1])).astype(BF16)
                  for c, (g, h) in enumerate(chains)]
            dv = [_dot_tn(pr[c].astype(BF16), dom[c]) for c in range(len(chains))]
            dk = [_dot_tn(ds[c], qm[c]) for c in range(len(chains))]
            dq = [_dot(ds[c], kk[g]) for c, (g, h) in enumerate(chains)]
            for g, (rows, _) in enumerate(blocks):
                add_rows(dqa_ref, rows, jnp.where(head0, dq[2 * g], dq[2 * g + 1]))
                add_rows(dka_ref, keys[g], dk[2 * g] + dk[2 * g + 1])
                add_rows(dva_ref, keys[g], dv[2 * g] + dv[2 * g + 1])

        _for_each_group(group)

        @pl.when(pl.program_id(0) == 0)
        def _():
            def tables(i, carry):
                rows = pl.ds(pl.multiple_of(i * 256, 256), 256)
                c, sa, sb = _rot_tables(pos_ref[rows, :], invf_ref[...], ma_ref[...], mb_ref[...])
                rot_ref[0, rows, :] = c
                rot_ref[1, rows, :] = sa
                rot_ref[2, rows, :] = sb
                return carry

            lax.fori_loop(0, SEQ // 256, tables, 0)

        def finish(i, carry):
            rows = pl.ds(pl.multiple_of(i * 256, 256), 256)
            c, sa, sb = rot_ref[0, rows, :], rot_ref[1, rows, :], rot_ref[2, rows, :]
            dq_ref[rows, :] = _rot_t(dqa_ref[rows, :] * Q_SCALE, c, sa, sb).astype(BF16)
            dk_ref[rows, :] = _rot_t(dka_ref[rows, :], c, sa, sb).astype(BF16)
            dv_ref[rows, :] = dva_ref[rows, :].astype(BF16)
            return carry

        lax.fori_loop(0, SEQ // 256, finish, 0)

    slab = pl.BlockSpec((SEQ, 128), lambda i: (0, i))
    out = jax.ShapeDtypeStruct((SEQ, ATTN_W), BF16)
    acc = pltpu.VMEM((SEQ, 128), F32)
    return _call(
        "attn_bwd", body, ATTN_W // 128,
        [slab] * 6 + [_full_spec((SEQ, 1)), _full_spec((1, 128)), _full_spec((1, 128)), _full_spec((1, 128))],
        [slab] * 3, [out, out, out], (q, k, v, o, lse, do, pos_col, *rot),
        scratch_shapes=[acc, acc, acc, acc, pltpu.VMEM((3, SEQ, 128), F32)])


def in_bwd(dproj, w_in_t, x, g1, dx2, after=()):
    tm = 512

    def body(dp_ref, w_ref, x_ref, g_ref, dx2_ref, dx_ref, dg_ref):
        dh1 = _dot(dp_ref[...], w_ref[...])
        dz, dg = _rms_bwd(x_ref[...], g_ref[...], dh1)
        dx_ref[...] = dx2_ref[...] + dz

        @pl.when(pl.program_id(0) == 0)
        def _():
            dg_ref[...] = jnp.zeros_like(dg_ref)

        dg_ref[...] += dg

    return _call(
        "in_bwd", body, SEQ // tm,
        [_row_spec(tm, IN_W), _weight_spec((IN_W, D_MODEL)), _row_spec(tm, D_MODEL), _full_spec((1, D_MODEL)),
         _row_spec(tm, D_MODEL)],
        [_row_spec(tm, D_MODEL), _full_spec((1, D_MODEL))],
        [jax.ShapeDtypeStruct((SEQ, D_MODEL), F32), jax.ShapeDtypeStruct((1, D_MODEL), F32)],
        (dproj, w_in_t, x, g1, dx2), after=after)


def _coords():
    return lax.axis_index("x"), lax.axis_index("y"), lax.axis_index("c")


class Exchange:
    def __init__(self, srcs, bufs, new_shapes, n_sems, make):
        self.srcs, self.bufs, self.new_shapes, self.n_sems, self.make = list(srcs), list(bufs), list(new_shapes), n_sems, make


def _call(name, body, n_steps, in_specs, out_specs, out_shape, args, scratch_shapes=(), after=()):
    n_in = len(args)

    def wrapped(*refs):
        body(*refs[:n_in], *refs[n_in + len(after):])

    return list(pl.pallas_call(
        wrapped, name=name, grid=(n_steps,), in_specs=list(in_specs) + [ANY] * len(after), out_specs=list(out_specs),
        out_shape=list(out_shape), scratch_shapes=list(scratch_shapes), compiler_params=_params(),
    )(*args, *after))


GATHER_SEMS = 8


def gather(bufs):
    n = len(bufs)

    def make(src_refs, buf_refs, new_refs, send_sems, recv_sems):
        x, y, c = _coords()
        me, sibling = (x, y, c), (x, y, 1 - c)
        over_x, over_y, across = (1 - x, y), (x, 1 - y), (1 - x, 1 - y)

        def copy(a, k, block, to, half=None):
            r = buf_refs[a].shape[0] // N_DEV
            lo, size = (0, r) if half is None else (half * (r // 2), r // 2)
            rows = buf_refs[a].at[pl.ds((4 * block[0] + 2 * block[1] + block[2]) * r + lo, size), :]
            return pltpu.make_async_remote_copy(
                src_ref=rows, dst_ref=rows, send_sem=send_sems.at[GATHER_SEMS * a + k],
                recv_sem=recv_sems.at[GATHER_SEMS * a + k], device_id=to, device_id_type=MESH)

        every = range(n)
        out = ([copy(a, 0, me, sibling) for a in every] + [copy(a, 1, me, (*over_x, c)) for a in every]
               + [copy(a, 2, me, (*over_y, c)) for a in every])
        near_in = [copy(a, 1, (*over_x, c), me) for a in every] + [copy(a, 2, (*over_y, c), me) for a in every]
        relay = ([copy(a, 3, (*over_x, c), (*over_y, c), half=0) for a in every]
                 + [copy(a, 4, (*over_y, c), (*over_x, c), half=1) for a in every])
        near_on = [copy(a, 5, (*over_x, c), sibling) for a in every] + [copy(a, 6, (*over_y, c), sibling) for a in every]
        relay_in = ([copy(a, 3, (*across, c), me, half=0) for a in every]
                    + [copy(a, 4, (*across, c), me, half=1) for a in every])
        far_on = [copy(a, 7, (*across, c), sibling) for a in every]
        from_core = ([copy(a, 0, sibling, me) for a in every] + [copy(a, 5, (*over_x, 1 - c), me) for a in every]
                     + [copy(a, 6, (*over_y, 1 - c), me) for a in every] + [copy(a, 7, (*across, 1 - c), me) for a in every])
        stages = [([], out), (near_in, relay + near_on), (relay_in, far_on)]
        return stages, out + relay + near_on + far_on, from_core

    return Exchange([], bufs, [], GATHER_SEMS * n, make)


TO_GATHER = (1, lambda x, y, c: [(x, y, 1 - c), (1 - x, y, c), (x, 1 - y, c)])
TO_SIBLING = (2, lambda x, y, c: [(x, y, 1 - c)])
TO_CHIPS = (3, lambda x, y, c: [(1 - x, y, c), (x, 1 - y, c), (1 - x, 1 - y, c)])
TO_ALL = (4, lambda x, y, c: [(x ^ (m >> 2), y ^ ((m >> 1) & 1), c ^ (m & 1)) for m in range(1, N_DEV)])


def by_sequencer(name, exchanges, who):
    collective_id, peers_of = who
    hbm = pltpu.MemorySpace.HBM
    refs = [([jax.new_ref(a, memory_space=hbm) for a in ex.srcs], [jax.new_ref(a, memory_space=hbm) for a in ex.bufs],
             [jax.empty_ref(s, memory_space=hbm) for s in ex.new_shapes]) for ex in exchanges]
    sems = []
    for ex in exchanges:
        sems += [pltpu.SemaphoreType.DMA((ex.n_sems,)), pltpu.SemaphoreType.DMA((ex.n_sems,))]

    @pl.kernel(mesh=plsc.ScalarSubcoreMesh(axis_name="sequencer", num_cores=1), name=name, scratch_types=tuple(sems),
               compiler_params=pltpu.CompilerParams(collective_id=collective_id))
    def launch(*sem_refs):
        peers = peers_of(*_coords())
        barrier = pltpu.get_barrier_semaphore()
        for peer in peers:
            pl.semaphore_signal(barrier, inc=1, device_id=peer, device_id_type=MESH)
        pl.semaphore_wait(barrier, len(peers))

        made = [ex.make(*refs[k], sem_refs[2 * k], sem_refs[2 * k + 1]) for k, ex in enumerate(exchanges)]
        for stage in range(max(len(stages) for stages, _, _ in made)):
            for stages, _, _ in made:
                if stage < len(stages):
                    arrivals, starts = stages[stage]
                    for cp in arrivals:
                        cp.wait_recv()
                    for cp in starts:
                        cp.start()
        for _, sends, arrivals in made:
            for cp in arrivals:
                cp.wait_recv()
            for cp in sends:
                cp.wait_send()

    launch()
    return [([ref[...] for ref in bufs], [ref[...] for ref in news]) for _, bufs, news in refs]


def place_shards(name, shards, dev):
    n = len(shards)

    def body(dev_ref, *refs):
        for a in range(n):
            refs[n + a][...] = refs[a][...].astype(BF16)

    spec = pltpu.PrefetchScalarGridSpec(
        num_scalar_prefetch=1, grid=(1,),
        in_specs=[pl.BlockSpec(s.shape, lambda i, dev_ref: (0, 0)) for s in shards],
        out_specs=[pl.BlockSpec(s.shape, lambda i, dev_ref: (dev_ref[0], 0)) for s in shards])
    return pl.pallas_call(
        body, name=name, grid_spec=spec,
        out_shape=[jax.ShapeDtypeStruct((N_DEV * s.shape[0], s.shape[1]), BF16) for s in shards],
        compiler_params=_params(),
    )(dev, *shards)


def _swap(copies_of):
    def make(src_refs, buf_refs, new_refs, send_sems, recv_sems):
        copies = copies_of(src_refs, new_refs, send_sems, recv_sems)
        return [([], copies)], copies, copies

    return make


def to_sibling(grads):
    def copies_of(src_refs, new_refs, send_sems, recv_sems):
        x, y, c = _coords()
        return [pltpu.make_async_remote_copy(
            src_ref=src_refs[a].at[2 * xy + 1 - c], dst_ref=new_refs[a].at[xy], send_sem=send_sems.at[4 * a + xy],
            recv_sem=recv_sems.at[4 * a + xy], device_id=(x, y, 1 - c), device_id_type=MESH)
            for a in range(len(src_refs)) for xy in range(4)]

    return Exchange(grads, [], [jax.ShapeDtypeStruct((4,) + g.shape[1:], g.dtype) for g in grads], 4 * len(grads),
                    _swap(copies_of))


def to_chips(parts):
    def copies_of(src_refs, new_refs, send_sems, recv_sems):
        x, y, c = _coords()
        chips = [(1 - x, y), (x, 1 - y), (1 - x, 1 - y)]
        return [pltpu.make_async_remote_copy(
            src_ref=src_refs[a].at[2 * px + py], dst_ref=new_refs[a].at[2 * x + y], send_sem=send_sems.at[3 * a + j],
            recv_sem=recv_sems.at[3 * a + j], device_id=(px, py, c), device_id_type=MESH)
            for a in range(len(src_refs)) for j, (px, py) in enumerate(chips)]

    return Exchange(parts, [], [jax.ShapeDtypeStruct(p.shape, p.dtype) for p in parts], 3 * len(parts), _swap(copies_of))


def to_owners(grad):
    def copies_of(src_refs, new_refs, send_sems, recv_sems):
        x, y, c = _coords()
        copies = []
        for m in range(1, N_DEV):
            px, py, pc = x ^ (m >> 2), y ^ ((m >> 1) & 1), c ^ (m & 1)
            copies.append(pltpu.make_async_remote_copy(
                src_ref=src_refs[0].at[4 * px + 2 * py + pc], dst_ref=new_refs[0].at[4 * x + 2 * y + c],
                send_sem=send_sems.at[m - 1], recv_sem=recv_sems.at[m - 1], device_id=(px, py, pc), device_id_type=MESH))
        return copies

    return Exchange([grad], [], [jax.ShapeDtypeStruct(grad.shape, grad.dtype)], N_DEV - 1, _swap(copies_of))


def sum_cores(name, grad, other, core, after=()):
    _, r, w = other.shape

    def body(core_ref, g_ref, o_ref, *rest):
        rest[-1][...] = (g_ref[...].astype(F32) + o_ref[...].astype(F32)).astype(rest[-1].dtype)

    return pl.pallas_call(
        body, name=name,
        grid_spec=pltpu.PrefetchScalarGridSpec(
            num_scalar_prefetch=1, grid=(4,),
            in_specs=[pl.BlockSpec((1, r, w), lambda i, core_ref: (2 * i + core_ref[0], 0, 0)),
                      pl.BlockSpec((1, r, w), lambda i, core_ref: (i, 0, 0))] + [ANY] * len(after),
            out_specs=pl.BlockSpec((1, r, w), lambda i, core_ref: (i, 0, 0))),
        out_shape=jax.ShapeDtypeStruct(other.shape, other.dtype),
        compiler_params=_params(),
    )(core, grad, other, *after)


def sum_owned(name, grad, others, dev_ids, after=()):
    _, r, w = grad.shape

    def body(ids_ref, *refs):
        acc = refs[0][0]
        for k in range(1, N_DEV):
            acc = acc + refs[k][0]
        refs[-1][...] = acc

    def pick(k):
        return pl.BlockSpec((1, r, w), lambda i, ids_ref: (ids_ref[k], 0, 0))

    return pl.pallas_call(
        body, name=name,
        grid_spec=pltpu.PrefetchScalarGridSpec(
            num_scalar_prefetch=1, grid=(1,), in_specs=[pick(k) for k in range(N_DEV)] + [ANY] * len(after),
            out_specs=pl.BlockSpec((r, w), lambda i, ids_ref: (ids_ref[0], 0))),
        out_shape=jax.ShapeDtypeStruct((N_DEV * r, w), F32),
        compiler_params=_params(),
    )(dev_ids, grad, *([others] * (N_DEV - 1)), *after)


def _adamw_update(w, g, m, v):
    nm = ADAM_B1 * m + np.float32(1.0 - ADAM_B1) * g
    nv = ADAM_B2 * v + np.float32(1.0 - ADAM_B2) * (g * g)
    m_hat = nm / np.float32(1.0 - ADAM_B1 ** ADAM_STEP)
    v_hat = nv / np.float32(1.0 - ADAM_B2 ** ADAM_STEP)
    return -ADAM_LR * (m_hat / (jnp.sqrt(v_hat) + ADAM_EPS) + ADAM_WD * w), nm, nv


def adamw_of_sums(name, part, others, chip_ids, w, m, v, after):
    _, r, wd = part.shape
    halves = 2
    rows = r // halves

    def body(ids_ref, p_ref, a_ref, b_ref, c_ref, w_ref, m_ref, v_ref, after_ref, g_ref, d_ref, nm_ref, nv_ref):
        g = ((p_ref[0].astype(F32) + a_ref[0].astype(F32)) + b_ref[0].astype(F32)) + c_ref[0].astype(F32)
        g_ref[...] = g
        d_ref[...], nm_ref[...], nv_ref[...] = _adamw_update(w_ref[...], g, m_ref[...], v_ref[...])

    def pick(k):
        return pl.BlockSpec((1, rows, wd), lambda i, ids_ref: (ids_ref[k], i, 0))

    whole = pl.BlockSpec((rows, wd), lambda i, ids_ref: (i, 0))
    shape = jax.ShapeDtypeStruct((r, wd), F32)
    return pl.pallas_call(
        body, name=name,
        grid_spec=pltpu.PrefetchScalarGridSpec(
            num_scalar_prefetch=1, grid=(halves,), in_specs=[pick(0), pick(1), pick(2), pick(3), whole, whole, whole, ANY],
            out_specs=[whole] * 4),
        out_shape=[shape] * 4,
        compiler_params=_params(),
    )(chip_ids, part, others, others, others, w, m, v, after)


def adamw(name, w, g, m, v):
    def body(w_ref, g_ref, m_ref, v_ref, d_ref, nm_ref, nv_ref):
        d_ref[...], nm_ref[...], nv_ref[...] = _adamw_update(w_ref[...], g_ref[...], m_ref[...], v_ref[...])

    shape = jax.ShapeDtypeStruct(w.shape, F32)
    spec = _full_spec(w.shape)
    return pl.pallas_call(
        body, name=name, grid=(1,), in_specs=[spec] * 4, out_specs=[spec] * 3, out_shape=[shape] * 3,
        compiler_params=_params(),
    )(w, g, m, v)


def _pack_small(parts):
    flat = jnp.concatenate([parts[name].reshape(-1) for name, _ in SMALL])
    return jnp.pad(flat, (0, SMALL_ROWS * 128 - flat.shape[0])).reshape(SMALL_ROWS, 128)


def _unpack_small(packed, like):
    flat = packed.reshape(-1)
    out, at = {}, 0
    for name, size in SMALL:
        out[name] = flat[at:at + size].reshape(like[name].shape)
        at += size
    return out


def kernel(x, positions, pre_mix_norm, w_in, sgu_ln_gain, sgu_ln_bias, sgu_w_spatial, sgu_b_spatial, attn_out_norm, sgu_out_norm, w_out, post_mix_norm, pre_ffn_norm, w_gate, w_up, w_down, post_ffn_norm, loss_target, m_pre_mix_norm, m_w_in, m_sgu_ln_gain, m_sgu_ln_bias, m_sgu_w_spatial, m_sgu_b_spatial, m_attn_out_norm, m_sgu_out_norm, m_w_out, m_post_mix_norm, m_pre_ffn_norm, m_w_gate, m_w_up, m_w_down, m_post_ffn_norm, v_pre_mix_norm, v_w_in, v_sgu_ln_gain, v_sgu_ln_bias, v_sgu_w_spatial, v_sgu_b_spatial, v_attn_out_norm, v_sgu_out_norm, v_w_out, v_post_mix_norm, v_pre_ffn_norm, v_w_gate, v_w_up, v_w_down, v_post_ffn_norm):
    small_w = dict(pre_mix_norm=pre_mix_norm, sgu_ln_gain=sgu_ln_gain, sgu_ln_bias=sgu_ln_bias, sgu_w_spatial=sgu_w_spatial,
                   sgu_b_spatial=sgu_b_spatial, attn_out_norm=attn_out_norm, sgu_out_norm=sgu_out_norm,
                   post_mix_norm=post_mix_norm, pre_ffn_norm=pre_ffn_norm, post_ffn_norm=post_ffn_norm)
    small_m = dict(pre_mix_norm=m_pre_mix_norm, sgu_ln_gain=m_sgu_ln_gain, sgu_ln_bias=m_sgu_ln_bias, sgu_w_spatial=m_sgu_w_spatial,
                   sgu_b_spatial=m_sgu_b_spatial, attn_out_norm=m_attn_out_norm, sgu_out_norm=m_sgu_out_norm,
                   post_mix_norm=m_post_mix_norm, pre_ffn_norm=m_pre_ffn_norm, post_ffn_norm=m_post_ffn_norm)
    small_v = dict(pre_mix_norm=v_pre_mix_norm, sgu_ln_gain=v_sgu_ln_gain, sgu_ln_bias=v_sgu_ln_bias, sgu_w_spatial=v_sgu_w_spatial,
                   sgu_b_spatial=v_sgu_b_spatial, attn_out_norm=v_attn_out_norm, sgu_out_norm=v_sgu_out_norm,
                   post_mix_norm=v_post_mix_norm, pre_ffn_norm=v_pre_ffn_norm, post_ffn_norm=v_post_ffn_norm)
    for table in (small_w, small_m, small_v):
        table["loss_sum"] = jnp.zeros((1,), F32)

    x2d = x[0]
    target = loss_target[0]
    pos_col = positions.reshape(SEQ, 1)
    rot = _rot_consts()
    w_sp = sgu_w_spatial[0]
    bfull = jnp.repeat(sgu_b_spatial[0].T, HEAD_DIM, axis=1)

    x_i, y_i, c_i = (lax.axis_index(a).astype(jnp.int32) for a in MESH_AXES)
    dev = 4 * x_i + 2 * y_i + c_i
    core = c_i.reshape(1)
    chip = 2 * x_i + y_i
    chip_ids = jnp.stack([chip, chip ^ 1, chip ^ 2, chip ^ 3])
    dev_ids = jnp.stack([dev ^ m for m in range(N_DEV)])

    def gathered(name, bufs):
        return by_sequencer(name, [gather(bufs)], TO_GATHER)[0][0]

    def from_sibling(name, grads):
        return by_sequencer(name, [to_sibling(grads)], TO_SIBLING)[0][1]

    def from_chips(name, parts):
        return by_sequencer(name, [to_chips(parts)], TO_CHIPS)[0][1]

    (w_in_t,) = place_shards("place_w_in", [w_in[0].T], dev.reshape(1))
    (w_in_t,) = gathered("gather_w_in", [w_in_t])
    w_gate_t, w_up_t, w_out_f, w_down_f = place_shards(
        "place_weights", [w_gate[0].T, w_up[0].T, w_out[0], w_down[0]], dev.reshape(1))
    (w_out_f,) = gathered("gather_w_out", [w_out_f])
    w_gate_t, w_up_t = gathered("gather_w_gate_up", [w_gate_t, w_up_t])
    (w_down_f,) = gathered("gather_w_down", [w_down_f])

    h1, u, vs, q, k, v = in_proj(x2d, pos_col, pre_mix_norm, w_in_t, rot)
    attn_r, lse, attn = attn_fwd(q, k, v)
    (sgu,) = sgu_fwd(u, vs, sgu_ln_gain, sgu_ln_bias, w_sp, bfull)
    mix, y, x2, h2 = out_proj(attn, sgu, x2d, attn_out_norm, sgu_out_norm, w_out_f, post_mix_norm, pre_ffn_norm)
    gate, up, act = ffn_up(h2, w_gate_t, w_up_t)
    df, dx3, d_post_ffn, sq_err = ffn_down_loss(act, w_down_f, x2, post_ffn_norm, target)

    g_w_down = weight_grad("grad_w_down", act, df)
    (s_down,) = from_sibling("w_down_to_sibling", [g_w_down])
    dgate, dup, dx2, dy, d_pre_ffn, d_post_mix = ffn_bwd(
        df, w_down_f, gate, up, w_gate_t, w_up_t, x2, pre_ffn_norm, dx3, y, post_mix_norm, after=[g_w_down])
    p_down = sum_cores("sum_cores_down", g_w_down, s_down, core, after=[dy])
    (c_down,) = from_chips("w_down_to_chips", [p_down])
    g_w_gate, g_w_up = weight_grads("grad_w_gate_up", [dgate, dup], h2, after=[p_down])
    s_gate, s_up = from_sibling("w_gate_up_to_sibling", [g_w_gate, g_w_up])
    g_w_out = weight_grad("grad_w_out", mix, dy, after=[g_w_up, c_down])
    p_gate = sum_cores("sum_cores_gate", g_w_gate, s_gate, core, after=[g_w_out])
    p_up = sum_cores("sum_cores_up", g_w_up, s_up, core, after=[g_w_out])
    c_gate, c_up = from_chips("w_gate_up_to_chips", [p_gate, p_up])
    (s_out,) = from_sibling("w_out_to_sibling", [g_w_out])
    dsgu, d_attn_out, d_sgu_out, dattn_r = mix_bwd(dy, w_out_f, attn, sgu, attn_out_norm, sgu_out_norm, after=[p_gate, p_up])
    du, dvs, d_ln_gain, d_ln_bias, d_w_sp, d_bfull = sgu_bwd(u, vs, dsgu, sgu_ln_gain, sgu_ln_bias, w_sp, bfull)
    dq, dk, dv = attn_bwd(q, k, v, attn_r, lse, dattn_r, _to_residue_order(pos_col), rot)
    p_out = sum_cores("sum_cores_out", g_w_out, s_out, core, after=[dq])
    (c_out,) = from_chips("w_out_to_chips", [p_out])
    dq, dk, dv = (_from_residue_order(t) for t in (dq, dk, dv))
    dproj = jnp.concatenate([dq, dk, dv, du, dvs], axis=1)
    g_w_in = weight_grad("grad_w_in", dproj, h1, after=[c_gate, c_up])
    (s_in,) = from_sibling("w_in_to_sibling", [g_w_in])
    grad_x, d_pre_mix = in_bwd(dproj, w_in_t, x2d, pre_mix_norm, dx2, after=[g_w_in, c_out])
    p_in = sum_cores("sum_cores_in", g_w_in, s_in, core, after=[d_pre_mix])

    d_b_sp = d_bfull.reshape(CHUNK, N_GROUPS, HEAD_DIM).sum(axis=-1).T
    small_g = _pack_small(dict(pre_mix_norm=d_pre_mix, sgu_ln_gain=d_ln_gain, sgu_ln_bias=d_ln_bias, sgu_w_spatial=d_w_sp,
                               sgu_b_spatial=d_b_sp, attn_out_norm=d_attn_out, sgu_out_norm=d_sgu_out,
                               post_mix_norm=d_post_mix, pre_ffn_norm=d_pre_ffn, post_ffn_norm=d_post_ffn,
                               loss_sum=sq_err))
    small_g = small_g.reshape(N_DEV, SMALL_ROWS // N_DEV, 128)
    (_, (c_in,)), (_, (o_small,)) = by_sequencer(
        "last_sums_to_owners", [to_chips([p_in]), to_owners(small_g)], TO_ALL)

    big, last = {}, p_in
    for name, w, p, c, m, vv, transposed in (
            ("w_down", w_down, p_down, c_down, m_w_down, v_w_down, False), ("w_gate", w_gate, p_gate, c_gate, m_w_gate, v_w_gate, True),
            ("w_up", w_up, p_up, c_up, m_w_up, v_w_up, True), ("w_out", w_out, p_out, c_out, m_w_out, v_w_out, False),
            ("w_in", w_in, p_in, c_in, m_w_in, v_w_in, True)):
        if name == "w_in":
            last = sum_owned("sum_small", small_g, o_small, dev_ids, after=[last])
            (all_small,) = gathered("gather_small_grads", [last])
        turn = (lambda t: t.T) if transposed else (lambda t: t)
        outs = adamw_of_sums("adamw_" + name, p, c, chip_ids, turn(w[0]), turn(m[0]), turn(vv[0]), last)
        big[name], last = tuple(turn(t)[None] for t in outs), outs[0]
    sd, snm, snv = adamw("adamw_small", _pack_small(small_w), all_small, _pack_small(small_m), _pack_small(small_v))
    sg, sd, snm, snv = (_unpack_small(t, small_w) for t in (all_small, sd, snm, snv))
    loss = sg["loss_sum"][0] * np.float32(0.5 / D_MODEL)

    names = ["pre_mix_norm", "w_in", "sgu_ln_gain", "sgu_ln_bias", "sgu_w_spatial", "sgu_b_spatial", "attn_out_norm",
             "sgu_out_norm", "w_out", "post_mix_norm", "pre_ffn_norm", "w_gate", "w_up", "w_down", "post_ffn_norm"]
    outs = [loss, grad_x[None]]
    for i, table in enumerate((sg, sd, snm, snv)):
        for name in names:
            outs.append(big[name][i] if name in big else table[name])
    return tuple(outs)
```

```python
import numpy as np
import jax
import jax.numpy as jnp
from jax import lax
from jax.experimental import pallas as pl
from jax.experimental.pallas import tpu as pltpu
from jax.experimental.pallas import tpu_sc as plsc

F32 = jnp.float32
BF16 = jnp.bfloat16

SEQ = 2048
D_MODEL = 1024
ATTN_W = 512
SGU_W = 512
HEAD_DIM = 64
N_GROUPS = 8
CHUNK = 128
D_FF = 2816
IN_W = 3 * ATTN_W + 2 * SGU_W
DILATIONS = (1, 4, 16)
ROPE_THETA = 500000.0
ROT_DIM = 16
ROT_HALF = 8
RMS_EPS = 1e-6
LN_EPS = 1e-5
Q_SCALE = 0.125
NEG = -1e30

N_DEV = 8
MESH_AXES = ("x", "y", "c")
MESH = pl.DeviceIdType.MESH

ADAM_LR = 0.001
ADAM_B1 = 0.9
ADAM_B2 = 0.999
ADAM_EPS = 1e-08
ADAM_WD = 0.01
ADAM_STEP = 10

VMEM_LIMIT = 60 * 1024 * 1024
ANY = pl.BlockSpec(memory_space=pl.ANY)

SMALL = (("pre_mix_norm", 1024), ("sgu_ln_gain", 512), ("sgu_ln_bias", 512), ("sgu_w_spatial", 8 * 128 * 128),
         ("sgu_b_spatial", 1024), ("attn_out_norm", 512), ("sgu_out_norm", 512), ("post_mix_norm", 1024),
         ("pre_ffn_norm", 1024), ("post_ffn_norm", 1024), ("loss_sum", 1))
SMALL_ROWS = 1152


def _params(sem=("arbitrary",)):
    return pltpu.CompilerParams(dimension_semantics=sem, vmem_limit_bytes=VMEM_LIMIT)


def _dot(a, b):
    return jnp.dot(a, b, preferred_element_type=F32)


def _dot_nt(a, b):
    return lax.dot_general(a, b, (((1,), (1,)), ((), ())), preferred_element_type=F32)


def _dot_tn(a, b):
    return lax.dot_general(a, b, (((0,), (0,)), ((), ())), preferred_element_type=F32)


def _rms(z):
    return lax.rsqrt(jnp.mean(z * z, axis=-1, keepdims=True) + RMS_EPS)


def _rms_bwd(z, gain, d):
    r = _rms(z)
    n = z * r
    dn = d * gain
    dz = r * (dn - n * jnp.mean(dn * n, axis=-1, keepdims=True))
    return dz, jnp.sum(d * n, axis=0, keepdims=True)


def _gelu(z):
    return 0.5 * z * (1.0 + lax.erf(z * np.float32(1.0 / np.sqrt(2.0))))


def _gelu_grad(z):
    cdf = 0.5 * (1.0 + lax.erf(z * np.float32(1.0 / np.sqrt(2.0))))
    return cdf + z * jnp.exp(-0.5 * z * z) * np.float32(1.0 / np.sqrt(2.0 * np.pi))


def _rot_tables(pos_col, invf, ma, mb):
    ang = pos_col.astype(F32) * invf
    s = jnp.sin(ang)
    return jnp.cos(ang), s * ma, s * mb


def _rot(t, c, sa, sb):
    return t * c + pltpu.roll(t, 120, 1) * sa + pltpu.roll(t, 8, 1) * sb


def _rot_t(d, c, sa, sb):
    return d * c + pltpu.roll(d * sa, 8, 1) + pltpu.roll(d * sb, 120, 1)


def _rot_consts():
    lane = np.arange(128) % HEAD_DIM
    inv_freq = (np.float32(ROPE_THETA) ** (-np.arange(0, ROT_DIM, 2, dtype=np.float32) / np.float32(ROT_DIM))).astype(np.float32)
    invf = np.where(lane < ROT_DIM, inv_freq[lane % ROT_HALF], 0.0).astype(np.float32)
    ma = np.where(lane < ROT_HALF, -1.0, 0.0).astype(np.float32)
    mb = np.where((lane >= ROT_HALF) & (lane < ROT_DIM), 1.0, 0.0).astype(np.float32)
    return jnp.asarray(invf[None]), jnp.asarray(ma[None]), jnp.asarray(mb[None])


def _row_spec(tm, w):
    return pl.BlockSpec((tm, w), lambda i: (i, 0))


def _full_spec(shape):
    return pl.BlockSpec(shape, lambda i: (0,) * len(shape))


def _weight_spec(shape):
    return pl.BlockSpec(shape, lambda i: (0,) * len(shape), pipeline_mode=pl.Buffered(1))


RES = 16


def _residue_scratch(n_arrays, tm, width):
    return [pltpu.VMEM((2, n_arrays, tm // RES, RES, width), F32), pltpu.SemaphoreType.DMA((2, n_arrays, RES))]


def _to_residue_rows(tiles, outs, scratch, sems, tm, n_steps):
    i = pl.program_id(0)
    slot = i % 2
    per = tm // RES

    def copies(step, s):
        return [pltpu.make_async_copy(scratch.at[s, a, :, b, :],
                                      outs[a].at[pl.ds(pl.multiple_of(b * (SEQ // RES) + per * step, per), per), :],
                                      sems.at[s, a, b]) for a in range(len(outs)) for b in range(RES)]

    @pl.when(i >= 2)
    def _():
        for cp in copies(i - 2, slot):
            cp.wait()

    for a, tile in enumerate(tiles):
        scratch[slot, a] = tile.reshape(per, RES, tile.shape[-1])
    for cp in copies(i, slot):
        cp.start()

    @pl.when(i == n_steps - 1)
    def _():
        for cp in copies(i - 1, 1 - slot) + copies(i, slot):
            cp.wait()


def in_proj(x, pos_col, g1, w_in_t, rot):
    tm = 512
    n_steps = SEQ // tm

    def body(x_ref, pos_ref, g_ref, w_ref, invf_ref, ma_ref, mb_ref, h_ref, u_ref, vs_ref, q_ref, k_ref, v_ref, scratch, sems):
        xf = x_ref[...]
        h = (xf * _rms(xf) * g_ref[...]).astype(BF16)
        h_ref[...] = h
        proj = _dot_nt(h, w_ref[...])
        c, sa, sb = _rot_tables(pos_ref[...], invf_ref[...], ma_ref[...], mb_ref[...])
        slabs = range(ATTN_W // 128)
        q = jnp.concatenate([_rot(proj[:, j * 128:(j + 1) * 128], c, sa, sb) * Q_SCALE for j in slabs], axis=1)
        k = jnp.concatenate([_rot(proj[:, ATTN_W + j * 128:ATTN_W + (j + 1) * 128], c, sa, sb) for j in slabs], axis=1)
        u_ref[...] = proj[:, 3 * ATTN_W:3 * ATTN_W + SGU_W]
        vs_ref[...] = proj[:, 3 * ATTN_W + SGU_W:]
        _to_residue_rows([q, k, proj[:, 2 * ATTN_W:3 * ATTN_W]], [q_ref, k_ref, v_ref], scratch, sems, tm, n_steps)

    act = jax.ShapeDtypeStruct((SEQ, 512), F32)
    return _call(
        "in_proj", body, n_steps,
        [_row_spec(tm, D_MODEL), _row_spec(tm, 1), _full_spec((1, D_MODEL)), _weight_spec((IN_W, D_MODEL)),
         _full_spec((1, 128)), _full_spec((1, 128)), _full_spec((1, 128))],
        [_row_spec(tm, D_MODEL)] + [_row_spec(tm, 512)] * 2 + [ANY] * 3,
        [jax.ShapeDtypeStruct((SEQ, D_MODEL), BF16)] + [act] * 5,
        (x, pos_col, g1, w_in_t, *rot), scratch_shapes=_residue_scratch(3, tm, ATTN_W))


def _to_residue_order(t):
    return t.reshape(SEQ // RES, RES, -1).transpose(1, 0, 2).reshape(t.shape)


def _from_residue_order(t):
    return t.reshape(RES, SEQ // RES, -1).transpose(1, 0, 2).reshape(t.shape)


def _block_rows(d, r, n):
    if d == 16:
        slices = [(128 * r, 128)]
    elif d == 4:
        slices = [(128 * (4 * b + r) + 32 * n, 32) for b in range(4)]
    else:
        slices = [(128 * b + 8 * n, 8) for b in range(RES)]
    return [(s if isinstance(s, int) else pl.multiple_of(s, z), z) for s, z in slices]


def _block_step(d, i):
    if d == 16:
        return i
    if d == 4:
        return 4 * (i & 31) + (i >> 5)
    return 16 * (i & 7) + (i >> 3)


def _attn_masks(d):
    row2 = _block_step(d, lax.broadcasted_iota(jnp.int32, (128, 256), 0))
    col2 = lax.broadcasted_iota(jnp.int32, (128, 256), 1)
    key2 = _block_step(d, col2 & 127)
    mask2 = jnp.logical_or(jnp.logical_and(col2 < 128, key2 >= row2), jnp.logical_and(col2 >= 128, key2 <= row2))
    row1 = _block_step(d, lax.broadcasted_iota(jnp.int32, (128, 128), 0))
    col1 = lax.broadcasted_iota(jnp.int32, (128, 128), 1)
    return col1 < HEAD_DIM, _block_step(d, col1) <= row1, mask2


def _load_rows(ref, slices):
    parts = [ref[pl.ds(s, z), :] for s, z in slices]
    return parts[0] if len(parts) == 1 else jnp.concatenate(parts, axis=0)


def _for_each_group(fn):
    for p, d in enumerate(DILATIONS):
        masks = _attn_masks(d)
        if d == 16:
            def group(i, carry, p=p, masks=masks):
                fn(p, masks, [(_block_rows(16, 4 * i + g, 0), None) for g in range(4)])
                return carry

            lax.fori_loop(0, 4, group, 0)
        elif d == 4:
            fn(p, masks, [(_block_rows(4, r, 0), None) for r in range(4)])

            def group(i, carry, p=p, masks=masks):
                fn(p, masks, [(_block_rows(4, r, i + 1), _block_rows(4, r, i)) for r in range(4)])
                return carry

            lax.fori_loop(0, 3, group, 0)
        else:
            fn(p, masks, [(_block_rows(1, 0, 0), None)])

            def group(i, carry, p=p, masks=masks):
                fn(p, masks, [(_block_rows(1, 0, 3 * i + g + 1), _block_rows(1, 0, 3 * i + g)) for g in range(3)])
                return carry

            lax.fori_loop(0, 5, group, 0)


def attn_fwd(q, k, v):
    def body(q_ref, k_ref, v_ref, o_ref, lse_ref, nat_ref, op_ref, lp_ref, sems):
        def group(p, masks, blocks):
            head0, mask1, mask2 = masks
            heads = (head0, jnp.logical_not(head0))
            keys = [rows if prev is None else prev + rows for rows, prev in blocks]
            mask = [mask1 if prev is None else mask2 for _, prev in blocks]
            qb = [_load_rows(q_ref, rows) for rows, _ in blocks]
            kk = [_load_rows(k_ref, ks).astype(BF16) for ks in keys]
            vv = [_load_rows(v_ref, ks).astype(BF16) for ks in keys]
            chains = [(g, hm) for g in range(len(blocks)) for hm in heads]
            s = [jnp.where(mask[g], _dot_nt(jnp.where(hm, qb[g], 0.0).astype(BF16), kk[g]), NEG) for g, hm in chains]
            m = [jnp.max(t, axis=-1, keepdims=True) for t in s]
            e = [jnp.exp(t - mt) for t, mt in zip(s, m)]
            l = [jnp.sum(t, axis=-1, keepdims=True) for t in e]
            pv = [_dot(t.astype(BF16), vv[g]) for t, (g, _) in zip(e, chains)]
            for g, (rows, _) in enumerate(blocks):
                o_blk = jnp.where(head0, pv[2 * g] / l[2 * g], pv[2 * g + 1] / l[2 * g + 1])
                l_blk = jnp.where(head0, jnp.broadcast_to(m[2 * g] + jnp.log(l[2 * g]), (128, 128)),
                                  jnp.broadcast_to(m[2 * g + 1] + jnp.log(l[2 * g + 1]), (128, 128)))
                at = 0
                for start, size in rows:
                    op_ref[p, pl.ds(start, size), :] = o_blk[at:at + size]
                    lp_ref[p, pl.ds(start, size), :] = l_blk[at:at + size]
                    at += size

        _for_each_group(group)

        def combine(i, carry):
            rows = pl.ds(pl.multiple_of(i * 256, 256), 256)
            ls = [lp_ref[p, rows, :] for p in range(3)]
            m = jnp.maximum(jnp.maximum(ls[0], ls[1]), ls[2])
            lse = m + jnp.log(jnp.exp(ls[0] - m) + jnp.exp(ls[1] - m) + jnp.exp(ls[2] - m))
            o = jnp.zeros((256, 128), F32)
            for p in range(3):
                o = o + jnp.exp(ls[p] - lse) * op_ref[p, rows, :]
            o_ref[rows, :] = o
            lse_ref[rows, :] = lse
            return carry

        lax.fori_loop(0, SEQ // 256, combine, 0)

        lanes = pl.ds(pl.multiple_of(pl.program_id(0) * 128, 128), 128)
        back = [pltpu.make_async_copy(o_ref.at[pl.ds(b * (SEQ // RES), SEQ // RES), :], nat_ref.at[:, b, lanes], sems.at[b])
                for b in range(RES)]
        for cp in back:
            cp.start()
        for cp in back:
            cp.wait()

    slab = pl.BlockSpec((SEQ, 128), lambda i: (0, i))
    out = jax.ShapeDtypeStruct((SEQ, ATTN_W), F32)
    attn_r, lse, attn = _call(
        "attn_fwd", body, ATTN_W // 128, [slab] * 3, [slab] * 2 + [ANY],
        [out, out, jax.ShapeDtypeStruct((SEQ // RES, RES, ATTN_W), F32)], (q, k, v),
        scratch_shapes=[pltpu.VMEM((3, SEQ, 128), F32), pltpu.VMEM((3, SEQ, 128), F32), pltpu.SemaphoreType.DMA((RES,))])
    return attn_r, lse, attn.reshape(SEQ, ATTN_W)


def _causal_weights(w_ref):
    row = lax.broadcasted_iota(jnp.int32, (CHUNK, CHUNK), 0)
    col = lax.broadcasted_iota(jnp.int32, (CHUNK, CHUNK), 1)
    return [jnp.where(col <= row, w_ref[g], 0.0).astype(BF16) for g in range(N_GROUPS)], col <= row


def _sgu_chunk_fwd(u, vs, lg, lb, wc, bfull, head0):
    ug = _gelu(u)
    vg = _gelu(vs)
    xc = vg - jnp.mean(vg, axis=-1, keepdims=True)
    rstd = lax.rsqrt(jnp.mean(xc * xc, axis=-1, keepdims=True) + LN_EPS)
    xhat = xc * rstd
    vn = xhat * lg + lb
    mixed = []
    for gp in range(SGU_W // 128):
        vp = vn[:, gp * 128:(gp + 1) * 128].astype(BF16)
        mixed.append(jnp.where(head0, _dot(wc[2 * gp], vp), _dot(wc[2 * gp + 1], vp)))
    ms = jnp.concatenate(mixed, axis=1) + bfull
    return ug, xhat, rstd, vn, ms


def sgu_fwd(u, vs, lg, lb, w_sp, bfull):
    cpb = 4

    def body(u_ref, vs_ref, lg_ref, lb_ref, w_ref, b_ref, o_ref):
        wc, _ = _causal_weights(w_ref)
        head0 = lax.broadcasted_iota(jnp.int32, (CHUNK, 128), 1) < HEAD_DIM
        for ci in range(cpb):
            rows = pl.ds(ci * CHUNK, CHUNK)
            ug, _, _, _, ms = _sgu_chunk_fwd(u_ref[rows, :], vs_ref[rows, :], lg_ref[...], lb_ref[...], wc, b_ref[...], head0)
            o_ref[rows, :] = ug * ms

    tm = cpb * CHUNK
    return _call(
        "sgu_fwd", body, SEQ // tm,
        [_row_spec(tm, SGU_W), _row_spec(tm, SGU_W), _full_spec((1, SGU_W)), _full_spec((1, SGU_W)),
         _full_spec((N_GROUPS, CHUNK, CHUNK)), _full_spec((CHUNK, SGU_W))],
        [_row_spec(tm, SGU_W)], [jax.ShapeDtypeStruct((SEQ, SGU_W), F32)],
        (u, vs, lg, lb, w_sp, bfull))


def out_proj(attn, sgu, x, ga, gs, w_out, gpm, gpf):
    tm = 512

    def body(a_ref, s_ref, x_ref, ga_ref, gs_ref, w_ref, gpm_ref, gpf_ref, mix_ref, y_ref, x2_ref, h2_ref):
        a = a_ref[...]
        s = s_ref[...]
        an = (a * _rms(a) * ga_ref[...]).astype(BF16)
        sn = (s * _rms(s) * gs_ref[...]).astype(BF16)
        mix_ref[:, :ATTN_W] = an
        mix_ref[:, ATTN_W:] = sn
        y = _dot(an, w_ref[:ATTN_W, :]) + _dot(sn, w_ref[ATTN_W:, :])
        y_ref[...] = y
        x2 = x_ref[...] + y * _rms(y) * gpm_ref[...]
        x2_ref[...] = x2
        h2_ref[...] = (x2 * _rms(x2) * gpf_ref[...]).astype(BF16)

    wide = jax.ShapeDtypeStruct((SEQ, D_MODEL), F32)
    wide16 = jax.ShapeDtypeStruct((SEQ, D_MODEL), BF16)
    return _call(
        "out_proj", body, SEQ // tm,
        [_row_spec(tm, ATTN_W), _row_spec(tm, SGU_W), _row_spec(tm, D_MODEL), _full_spec((1, ATTN_W)),
         _full_spec((1, SGU_W)), _weight_spec((D_MODEL, D_MODEL)), _full_spec((1, D_MODEL)), _full_spec((1, D_MODEL))],
        [_row_spec(tm, D_MODEL)] * 4, [wide16, wide, wide, wide16],
        (attn, sgu, x, ga, gs, w_out, gpm, gpf))


def ffn_up(h2, w_gate_t, w_up_t):
    tm = 256

    def body(h_ref, wg_ref, wu_ref, g_ref, u_ref, a_ref):
        h = h_ref[...]
        g = _dot_nt(h, wg_ref[...])
        u = _dot_nt(h, wu_ref[...])
        g_ref[...] = g.astype(BF16)
        u_ref[...] = u.astype(BF16)
        a_ref[...] = (g * jax.nn.sigmoid(g) * u).astype(BF16)

    ff = jax.ShapeDtypeStruct((SEQ, D_FF), BF16)
    return _call(
        "ffn_up", body, SEQ // tm,
        [_row_spec(tm, D_MODEL), _weight_spec((D_FF, D_MODEL)), _weight_spec((D_FF, D_MODEL))],
        [_row_spec(tm, D_FF)] * 3, [ff, ff, jax.ShapeDtypeStruct((SEQ, D_FF), BF16)],
        (h2, w_gate_t, w_up_t))


def ffn_down_loss(act, w_down, x2, gpo, target):
    tm = 512

    def body(a_ref, w_ref, x2_ref, g_ref, t_ref, df_ref, dx3_ref, dg_ref, loss_ref):
        f = _dot(a_ref[...], w_ref[...])
        gain = g_ref[...]
        err = x2_ref[...] + f * _rms(f) * gain - t_ref[...]
        dx3 = err * np.float32(1.0 / D_MODEL)
        dx3_ref[...] = dx3
        df, dg = _rms_bwd(f, gain, dx3)
        df_ref[...] = df.astype(BF16)

        @pl.when(pl.program_id(0) == 0)
        def _():
            dg_ref[...] = jnp.zeros_like(dg_ref)
            loss_ref[...] = jnp.zeros_like(loss_ref)

        dg_ref[...] += dg
        loss_ref[...] += jnp.sum(err * err, axis=(0, 1), keepdims=True)

    return pl.pallas_call(
        body, name="ffn_down_loss", grid=(SEQ // tm,),
        in_specs=[_row_spec(tm, D_FF), _weight_spec((D_FF, D_MODEL)), _row_spec(tm, D_MODEL), _full_spec((1, D_MODEL)),
                  _row_spec(tm, D_MODEL)],
        out_specs=[_row_spec(tm, D_MODEL), _row_spec(tm, D_MODEL), _full_spec((1, D_MODEL)), _full_spec((1, 1))],
        out_shape=[jax.ShapeDtypeStruct((SEQ, D_MODEL), BF16), jax.ShapeDtypeStruct((SEQ, D_MODEL), F32),
                   jax.ShapeDtypeStruct((1, D_MODEL), F32), jax.ShapeDtypeStruct((1, 1), F32)],
        compiler_params=_params(),
    )(act, w_down, x2, gpo, target)


def ffn_bwd(df, w_down, gate, up, w_gate_t, w_up_t, x2, gpf, dx3, y, gpm, after=()):
    tm = 256

    def body(df_ref, wd_ref, g_ref, u_ref, wg_ref, wu_ref, x2_ref, gpf_ref, dx3_ref, y_ref, gpm_ref,
             dg_ref, du_ref, dx2_ref, dy_ref, dgpf_ref, dgpm_ref):
        dact = _dot_nt(df_ref[...], wd_ref[...])
        g = g_ref[...].astype(F32)
        s = jax.nn.sigmoid(g)
        dup = (dact * g * s).astype(BF16)
        dgate = (dact * u_ref[...].astype(F32) * (s * (1.0 + g * (1.0 - s)))).astype(BF16)
        du_ref[...] = dup
        dg_ref[...] = dgate
        dh2 = _dot(dgate, wg_ref[...]) + _dot(dup, wu_ref[...])
        dz, dgpf = _rms_bwd(x2_ref[...], gpf_ref[...], dh2)
        dx2 = dx3_ref[...] + dz
        dx2_ref[...] = dx2
        dy, dgpm = _rms_bwd(y_ref[...], gpm_ref[...], dx2)
        dy_ref[...] = dy.astype(BF16)

        @pl.when(pl.program_id(0) == 0)
        def _():
            dgpf_ref[...] = jnp.zeros_like(dgpf_ref)
            dgpm_ref[...] = jnp.zeros_like(dgpm_ref)

        dgpf_ref[...] += dgpf
        dgpm_ref[...] += dgpm

    vec = jax.ShapeDtypeStruct((1, D_MODEL), F32)
    ff16 = jax.ShapeDtypeStruct((SEQ, D_FF), BF16)
    return _call(
        "ffn_bwd", body, SEQ // tm,
        [_row_spec(tm, D_MODEL), _weight_spec((D_FF, D_MODEL)), _row_spec(tm, D_FF), _row_spec(tm, D_FF),
         _weight_spec((D_FF, D_MODEL)), _weight_spec((D_FF, D_MODEL)), _row_spec(tm, D_MODEL), _full_spec((1, D_MODEL)),
         _row_spec(tm, D_MODEL), _row_spec(tm, D_MODEL), _full_spec((1, D_MODEL))],
        [_row_spec(tm, D_FF), _row_spec(tm, D_FF), _row_spec(tm, D_MODEL), _row_spec(tm, D_MODEL),
         _full_spec((1, D_MODEL)), _full_spec((1, D_MODEL))],
        [ff16, ff16, jax.ShapeDtypeStruct((SEQ, D_MODEL), F32), jax.ShapeDtypeStruct((SEQ, D_MODEL), BF16), vec, vec],
        (df, w_down, gate, up, w_gate_t, w_up_t, x2, gpf, dx3, y, gpm), after=after)


def weight_grads(name, lhs, b, after=()):
    m, n, k = lhs[0].shape[1], b.shape[1], len(lhs)
    tr = 256

    def body(*refs):
        for a_ref, o_ref in zip(refs[:k], refs[k + 1:]):
            o_ref[...] = _dot_tn(a_ref[...], refs[k][...]).astype(BF16)

    outs = _call(
        name, body, m // tr, [pl.BlockSpec((SEQ, tr), lambda i: (0, i))] * k + [_weight_spec((SEQ, n))],
        [_row_spec(tr, n)] * k, [jax.ShapeDtypeStruct((m, n), BF16)] * k, (*lhs, b), after=after)
    return [out.reshape(N_DEV, m // N_DEV, n) for out in outs]


def weight_grad(name, a, b, after=()):
    return weight_grads(name, [a], b, after)[0]


def mix_bwd(dy, w_out, attn, sgu, ga, gs, after=()):
    tm = 512
    n_steps = SEQ // tm

    def body(dy_ref, w_ref, a_ref, s_ref, ga_ref, gs_ref, ds_ref, dga_ref, dgs_ref, da_ref, scratch, sems):
        dy = dy_ref[...]
        da, dga = _rms_bwd(a_ref[...], ga_ref[...], _dot_nt(dy, w_ref[:ATTN_W, :]))
        ds, dgs = _rms_bwd(s_ref[...], gs_ref[...], _dot_nt(dy, w_ref[ATTN_W:, :]))
        ds_ref[...] = ds
        _to_residue_rows([da], [da_ref], scratch, sems, tm, n_steps)

        @pl.when(pl.program_id(0) == 0)
        def _():
            dga_ref[...] = jnp.zeros_like(dga_ref)
            dgs_ref[...] = jnp.zeros_like(dgs_ref)

        dga_ref[...] += dga
        dgs_ref[...] += dgs

    half = jax.ShapeDtypeStruct((SEQ, 512), F32)
    vec = jax.ShapeDtypeStruct((1, 512), F32)
    return _call(
        "mix_bwd", body, n_steps,
        [_row_spec(tm, D_MODEL), _weight_spec((D_MODEL, D_MODEL)), _row_spec(tm, 512), _row_spec(tm, 512),
         _full_spec((1, 512)), _full_spec((1, 512))],
        [_row_spec(tm, 512), _full_spec((1, 512)), _full_spec((1, 512)), ANY],
        [half, vec, vec, half], (dy, w_out, attn, sgu, ga, gs), scratch_shapes=_residue_scratch(1, tm, ATTN_W), after=after)


def sgu_bwd(u, vs, dsgu, lg, lb, w_sp, bfull):
    cpb = 4

    def body(u_ref, vs_ref, d_ref, lg_ref, lb_ref, w_ref, b_ref, du_ref, dvs_ref, dlg_ref, dlb_ref, dw_ref, db_ref):
        wc, causal = _causal_weights(w_ref)
        head0 = lax.broadcasted_iota(jnp.int32, (CHUNK, 128), 1) < HEAD_DIM
        lg = lg_ref[...]

        @pl.when(pl.program_id(0) == 0)
        def _():
            dlg_ref[...] = jnp.zeros_like(dlg_ref)
            dlb_ref[...] = jnp.zeros_like(dlb_ref)
            dw_ref[...] = jnp.zeros_like(dw_ref)
            db_ref[...] = jnp.zeros_like(db_ref)

        for ci in range(cpb):
            rows = pl.ds(ci * CHUNK, CHUNK)
            u = u_ref[rows, :]
            vs = vs_ref[rows, :]
            d = d_ref[rows, :]
            ug, xhat, rstd, vn, ms = _sgu_chunk_fwd(u, vs, lg, lb_ref[...], wc, b_ref[...], head0)
            du_ref[rows, :] = (d * ms * _gelu_grad(u)).astype(BF16)
            dms = d * ug
            db_ref[...] += dms
            dvn = []
            for gp in range(SGU_W // 128):
                dmp = dms[:, gp * 128:(gp + 1) * 128]
                dm0 = jnp.where(head0, dmp, 0.0).astype(BF16)
                dm1 = jnp.where(head0, 0.0, dmp).astype(BF16)
                vp = vn[:, gp * 128:(gp + 1) * 128].astype(BF16)
                dw_ref[2 * gp] += _dot_nt(dm0, vp)
                dw_ref[2 * gp + 1] += _dot_nt(dm1, vp)
                dvn.append(_dot_tn(wc[2 * gp], dm0) + _dot_tn(wc[2 * gp + 1], dm1))
            dvn = jnp.concatenate(dvn, axis=1)
            dlg_ref[...] += jnp.sum(dvn * xhat, axis=0, keepdims=True)
            dlb_ref[...] += jnp.sum(dvn, axis=0, keepdims=True)
            dxh = dvn * lg
            dvg = rstd * (dxh - jnp.mean(dxh, axis=-1, keepdims=True) - xhat * jnp.mean(dxh * xhat, axis=-1, keepdims=True))
            dvs_ref[rows, :] = (dvg * _gelu_grad(vs)).astype(BF16)

        @pl.when(pl.program_id(0) == pl.num_programs(0) - 1)
        def _():
            for g in range(N_GROUPS):
                dw_ref[g] = jnp.where(causal, dw_ref[g], 0.0)

    tm = cpb * CHUNK
    half16 = jax.ShapeDtypeStruct((SEQ, SGU_W), BF16)
    vec = jax.ShapeDtypeStruct((1, SGU_W), F32)
    return _call(
        "sgu_bwd", body, SEQ // tm,
        [_row_spec(tm, SGU_W)] * 3 + [_full_spec((1, SGU_W)), _full_spec((1, SGU_W)),
                                      _full_spec((N_GROUPS, CHUNK, CHUNK)), _full_spec((CHUNK, SGU_W))],
        [_row_spec(tm, SGU_W), _row_spec(tm, SGU_W), _full_spec((1, SGU_W)), _full_spec((1, SGU_W)),
         _full_spec((N_GROUPS, CHUNK, CHUNK)), _full_spec((CHUNK, SGU_W))],
        [half16, half16, vec, vec, jax.ShapeDtypeStruct((N_GROUPS, CHUNK, CHUNK), F32),
         jax.ShapeDtypeStruct((CHUNK, SGU_W), F32)],
        (u, vs, dsgu, lg, lb, w_sp, bfull))


def attn_bwd(q, k, v, o, lse, do, pos_col, rot):
    def body(q_ref, k_ref, v_ref, o_ref, lse_ref, do_ref, pos_ref, invf_ref, ma_ref, mb_ref,
             dq_ref, dk_ref, dv_ref, dqa_ref, dka_ref, dva_ref, dlt_ref, rot_ref):
        dqa_ref[...] = jnp.zeros_like(dqa_ref)
        dka_ref[...] = jnp.zeros_like(dka_ref)
        dva_ref[...] = jnp.zeros_like(dva_ref)

        def delta(i, carry):
            rows = pl.ds(pl.multiple_of(i * 256, 256), 256)
            prod = do_ref[rows, :] * o_ref[rows, :]
            h0 = lax.broadcasted_iota(jnp.int32, (256, 128), 1) < HEAD_DIM
            d0 = jnp.sum(jnp.where(h0, prod, 0.0), axis=-1, keepdims=True)
            d1 = jnp.sum(jnp.where(h0, 0.0, prod), axis=-1, keepdims=True)
            dlt_ref[rows, :] = jnp.where(h0, d0, d1)
            return carry

        lax.fori_loop(0, SEQ // 256, delta, 0)

        def add_rows(ref, slices, val):
            at = 0
            for start, size in slices:
                ref[pl.ds(start, size), :] += val[at:at + size]
                at += size

        def group(p, masks, blocks):
            head0, mask1, mask2 = masks
            heads = (head0, jnp.logical_not(head0))
            keys = [rows if prev is None else prev + rows for rows, prev in blocks]
            mask = [mask1 if prev is None else mask2 for _, prev in blocks]
            kk = [_load_rows(k_ref, ks).astype(BF16) for ks in keys]
            vv = [_load_rows(v_ref, ks).astype(BF16) for ks in keys]
            qb = [_load_rows(q_ref, rows) for rows, _ in blocks]
            dob = [_load_rows(do_ref, rows) for rows, _ in blocks]
            lse_b = [_load_rows(lse_ref, rows) for rows, _ in blocks]
            dlt_b = [_load_rows(dlt_ref, rows) for rows, _ in blocks]
            chains = [(g, h) for g in range(len(blocks)) for h in range(2)]
            qm = [jnp.where(heads[h], qb[g], 0.0).astype(BF16) for g, h in chains]
            dom = [jnp.where(heads[h], dob[g], 0.0).astype(BF16) for g, h in chains]
            s = [_dot_nt(qm[c], kk[g]) for c, (g, h) in enumerate(chains)]
            dp = [_dot_nt(dom[c], vv[g]) for c, (g, h) in enumerate(chains)]
            pr = [jnp.where(mask[g], jnp.exp(s[c] - lse_b[g][:, h * HEAD_DIM:h * HEAD_DIM + 1]), 0.0)
                  for c, (g, h) in enumerate(chains)]
            ds = [(pr[c] * (dp[c] - dlt_b[g][:, h * HEAD_DIM:h * HEAD_DIM + 1])).astype(BF16)
                  for c, (g, h) in enumerate(chains)]
            dv = [_dot_tn(pr[c].astype(BF16), dom[c]) for c in range(len(chains))]
            dk = [_dot_tn(ds[c], qm[c]) for c in range(len(chains))]
            dq = [_dot(ds[c], kk[g]) for c, (g, h) in enumerate(chains)]
            for g, (rows, _) in enumerate(blocks):
                add_rows(dqa_ref, rows, jnp.where(head0, dq[2 * g], dq[2 * g + 1]))
                add_rows(dka_ref, keys[g], dk[2 * g] + dk[2 * g + 1])
                add_rows(dva_ref, keys[g], dv[2 * g] + dv[2 * g + 1])

        _for_each_group(group)

        @pl.when(pl.program_id(0) == 0)
        def _():
            def tables(i, carry):
                rows = pl.ds(pl.multiple_of(i * 256, 256), 256)
                c, sa, sb = _rot_tables(pos_ref[rows, :], invf_ref[...], ma_ref[...], mb_ref[...])
                rot_ref[0, rows, :] = c
                rot_ref[1, rows, :] = sa
                rot_ref[2, rows, :] = sb
                return carry

            lax.fori_loop(0, SEQ // 256, tables, 0)

        def finish(i, carry):
            rows = pl.ds(pl.multiple_of(i * 256, 256), 256)
            c, sa, sb = rot_ref[0, rows, :], rot_ref[1, rows, :], rot_ref[2, rows, :]
            dq_ref[rows, :] = _rot_t(dqa_ref[rows, :] * Q_SCALE, c, sa, sb).astype(BF16)
            dk_ref[rows, :] = _rot_t(dka_ref[rows, :], c, sa, sb).astype(BF16)
            dv_ref[rows, :] = dva_ref[rows, :].astype(BF16)
            return carry

        lax.fori_loop(0, SEQ // 256, finish, 0)

    slab = pl.BlockSpec((SEQ, 128), lambda i: (0, i))
    out = jax.ShapeDtypeStruct((SEQ, ATTN_W), BF16)
    acc = pltpu.VMEM((SEQ, 128), F32)
    return _call(
        "attn_bwd", body, ATTN_W // 128,
        [slab] * 6 + [_full_spec((SEQ, 1)), _full_spec((1, 128)), _full_spec((1, 128)), _full_spec((1, 128))],
        [slab] * 3, [out, out, out], (q, k, v, o, lse, do, pos_col, *rot),
        scratch_shapes=[acc, acc, acc, acc, pltpu.VMEM((3, SEQ, 128), F32)])


def in_bwd(dproj, w_in_t, x, g1, dx2, after=()):
    tm = 512

    def body(dp_ref, w_ref, x_ref, g_ref, dx2_ref, dx_ref, dg_ref):
        dh1 = _dot(dp_ref[...], w_ref[...])
        dz, dg = _rms_bwd(x_ref[...], g_ref[...], dh1)
        dx_ref[...] = dx2_ref[...] + dz

        @pl.when(pl.program_id(0) == 0)
        def _():
            dg_ref[...] = jnp.zeros_like(dg_ref)

        dg_ref[...] += dg

    return _call(
        "in_bwd", body, SEQ // tm,
        [_row_spec(tm, IN_W), _weight_spec((IN_W, D_MODEL)), _row_spec(tm, D_MODEL), _full_spec((1, D_MODEL)),
         _row_spec(tm, D_MODEL)],
        [_row_spec(tm, D_MODEL), _full_spec((1, D_MODEL))],
        [jax.ShapeDtypeStruct((SEQ, D_MODEL), F32), jax.ShapeDtypeStruct((1, D_MODEL), F32)],
        (dproj, w_in_t, x, g1, dx2), after=after)


def _coords():
    return lax.axis_index("x"), lax.axis_index("y"), lax.axis_index("c")


class Exchange:
    def __init__(self, srcs, bufs, new_shapes, n_sems, make):
        self.srcs, self.bufs, self.new_shapes, self.n_sems, self.make = list(srcs), list(bufs), list(new_shapes), n_sems, make


def _call(name, body, n_steps, in_specs, out_specs, out_shape, args, scratch_shapes=(), after=()):
    n_in = len(args)

    def wrapped(*refs):
        body(*refs[:n_in], *refs[n_in + len(after):])

    return list(pl.pallas_call(
        wrapped, name=name, grid=(n_steps,), in_specs=list(in_specs) + [ANY] * len(after), out_specs=list(out_specs),
        out_shape=list(out_shape), scratch_shapes=list(scratch_shapes), compiler_params=_params(),
    )(*args, *after))


GATHER_SEMS = 8


def gather(bufs):
    n = len(bufs)

    def make(src_refs, buf_refs, new_refs, send_sems, recv_sems):
        x, y, c = _coords()
        me, sibling = (x, y, c), (x, y, 1 - c)
        over_x, over_y, across = (1 - x, y), (x, 1 - y), (1 - x, 1 - y)

        def copy(a, k, block, to, half=None):
            r = buf_refs[a].shape[0] // N_DEV
            lo, size = (0, r) if half is None else (half * (r // 2), r // 2)
            rows = buf_refs[a].at[pl.ds((4 * block[0] + 2 * block[1] + block[2]) * r + lo, size), :]
            return pltpu.make_async_remote_copy(
                src_ref=rows, dst_ref=rows, send_sem=send_sems.at[GATHER_SEMS * a + k],
                recv_sem=recv_sems.at[GATHER_SEMS * a + k], device_id=to, device_id_type=MESH)

        every = range(n)
        out = ([copy(a, 0, me, sibling) for a in every] + [copy(a, 1, me, (*over_x, c)) for a in every]
               + [copy(a, 2, me, (*over_y, c)) for a in every])
        near_in = [copy(a, 1, (*over_x, c), me) for a in every] + [copy(a, 2, (*over_y, c), me) for a in every]
        relay = ([copy(a, 3, (*over_x, c), (*over_y, c), half=0) for a in every]
                 + [copy(a, 4, (*over_y, c), (*over_x, c), half=1) for a in every])
        near_on = [copy(a, 5, (*over_x, c), sibling) for a in every] + [copy(a, 6, (*over_y, c), sibling) for a in every]
        relay_in = ([copy(a, 3, (*across, c), me, half=0) for a in every]
                    + [copy(a, 4, (*across, c), me, half=1) for a in every])
        far_on = [copy(a, 7, (*across, c), sibling) for a in every]
        from_core = ([copy(a, 0, sibling, me) for a in every] + [copy(a, 5, (*over_x, 1 - c), me) for a in every]
                     + [copy(a, 6, (*over_y, 1 - c), me) for a in every] + [copy(a, 7, (*across, 1 - c), me) for a in every])
        stages = [([], out), (near_in, relay + near_on), (relay_in, far_on)]
        return stages, out + relay + near_on + far_on, from_core

    return Exchange([], bufs, [], GATHER_SEMS * n, make)


TO_GATHER = (1, lambda x, y, c: [(x, y, 1 - c), (1 - x, y, c), (x, 1 - y, c)])
TO_SIBLING = (2, lambda x, y, c: [(x, y, 1 - c)])
TO_CHIPS = (3, lambda x, y, c: [(1 - x, y, c), (x, 1 - y, c), (1 - x, 1 - y, c)])
TO_ALL = (4, lambda x, y, c: [(x ^ (m >> 2), y ^ ((m >> 1) & 1), c ^ (m & 1)) for m in range(1, N_DEV)])


def by_sequencer(name, exchanges, who):
    collective_id, peers_of = who
    hbm = pltpu.MemorySpace.HBM
    refs = [([jax.new_ref(a, memory_space=hbm) for a in ex.srcs], [jax.new_ref(a, memory_space=hbm) for a in ex.bufs],
             [jax.empty_ref(s, memory_space=hbm) for s in ex.new_shapes]) for ex in exchanges]
    sems = []
    for ex in exchanges:
        sems += [pltpu.SemaphoreType.DMA((ex.n_sems,)), pltpu.SemaphoreType.DMA((ex.n_sems,))]

    @pl.kernel(mesh=plsc.ScalarSubcoreMesh(axis_name="sequencer", num_cores=1), name=name, scratch_types=tuple(sems),
               compiler_params=pltpu.CompilerParams(collective_id=collective_id))
    def launch(*sem_refs):
        peers = peers_of(*_coords())
        barrier = pltpu.get_barrier_semaphore()
        for peer in peers:
            pl.semaphore_signal(barrier, inc=1, device_id=peer, device_id_type=MESH)
        pl.semaphore_wait(barrier, len(peers))

        made = [ex.make(*refs[k], sem_refs[2 * k], sem_refs[2 * k + 1]) for k, ex in enumerate(exchanges)]
        for stage in range(max(len(stages) for stages, _, _ in made)):
            for stages, _, _ in made:
                if stage < len(stages):
                    arrivals, starts = stages[stage]
                    for cp in arrivals:
                        cp.wait_recv()
                    for cp in starts:
                        cp.start()
        for _, sends, arrivals in made:
            for cp in arrivals:
                cp.wait_recv()
            for cp in sends:
                cp.wait_send()

    launch()
    return [([ref[...] for ref in bufs], [ref[...] for ref in news]) for _, bufs, news in refs]


def place_shards(name, shards, dev):
    n = len(shards)

    def body(dev_ref, *refs):
        for a in range(n):
            refs[n + a][...] = refs[a][...].astype(BF16)

    spec = pltpu.PrefetchScalarGridSpec(
        num_scalar_prefetch=1, grid=(1,),
        in_specs=[pl.BlockSpec(s.shape, lambda i, dev_ref: (0, 0)) for s in shards],
        out_specs=[pl.BlockSpec(s.shape, lambda i, dev_ref: (dev_ref[0], 0)) for s in shards])
    return pl.pallas_call(
        body, name=name, grid_spec=spec,
        out_shape=[jax.ShapeDtypeStruct((N_DEV * s.shape[0], s.shape[1]), BF16) for s in shards],
        compiler_params=_params(),
    )(dev, *shards)


def _swap(copies_of):
    def make(src_refs, buf_refs, new_refs, send_sems, recv_sems):
        copies = copies_of(src_refs, new_refs, send_sems, recv_sems)
        return [([], copies)], copies, copies

    return make


def to_sibling(grads):
    def copies_of(src_refs, new_refs, send_sems, recv_sems):
        x, y, c = _coords()
        return [pltpu.make_async_remote_copy(
            src_ref=src_refs[a].at[2 * xy + 1 - c], dst_ref=new_refs[a].at[xy], send_sem=send_sems.at[4 * a + xy],
            recv_sem=recv_sems.at[4 * a + xy], device_id=(x, y, 1 - c), device_id_type=MESH)
            for a in range(len(src_refs)) for xy in range(4)]

    return Exchange(grads, [], [jax.ShapeDtypeStruct((4,) + g.shape[1:], g.dtype) for g in grads], 4 * len(grads),
                    _swap(copies_of))


def to_chips(parts):
    def copies_of(src_refs, new_refs, send_sems, recv_sems):
        x, y, c = _coords()
        chips = [(1 - x, y), (x, 1 - y), (1 - x, 1 - y)]
        return [pltpu.make_async_remote_copy(
            src_ref=src_refs[a].at[2 * px + py], dst_ref=new_refs[a].at[2 * x + y], send_sem=send_sems.at[3 * a + j],
            recv_sem=recv_sems.at[3 * a + j], device_id=(px, py, c), device_id_type=MESH)
            for a in range(len(src_refs)) for j, (px, py) in enumerate(chips)]

    return Exchange(parts, [], [jax.ShapeDtypeStruct(p.shape, p.dtype) for p in parts], 3 * len(parts), _swap(copies_of))


def to_owners(grad):
    def copies_of(src_refs, new_refs, send_sems, recv_sems):
        x, y, c = _coords()
        copies = []
        for m in range(1, N_DEV):
            px, py, pc = x ^ (m >> 2), y ^ ((m >> 1) & 1), c ^ (m & 1)
            copies.append(pltpu.make_async_remote_copy(
                src_ref=src_refs[0].at[4 * px + 2 * py + pc], dst_ref=new_refs[0].at[4 * x + 2 * y + c],
                send_sem=send_sems.at[m - 1], recv_sem=recv_sems.at[m - 1], device_id=(px, py, pc), device_id_type=MESH))
        return copies

    return Exchange([grad], [], [jax.ShapeDtypeStruct(grad.shape, grad.dtype)], N_DEV - 1, _swap(copies_of))


def to_everyone(vec):
    def copies_of(src_refs, new_refs, send_sems, recv_sems):
        x, y, c = _coords()
        copies = []
        for m in range(1, N_DEV):
            px, py, pc = x ^ (m >> 2), y ^ ((m >> 1) & 1), c ^ (m & 1)
            copies.append(pltpu.make_async_remote_copy(
                src_ref=src_refs[0], dst_ref=new_refs[0].at[4 * x + 2 * y + c],
                send_sem=send_sems.at[m - 1], recv_sem=recv_sems.at[m - 1], device_id=(px, py, pc), device_id_type=MESH))
        return copies

    return Exchange([vec], [], [jax.ShapeDtypeStruct((N_DEV,) + vec.shape, vec.dtype)], N_DEV - 1, _swap(copies_of))


def sum_cores(name, grad, other, core, after=()):
    _, r, w = other.shape

    def body(core_ref, g_ref, o_ref, *rest):
        rest[-1][...] = (g_ref[...].astype(F32) + o_ref[...].astype(F32)).astype(rest[-1].dtype)

    return pl.pallas_call(
        body, name=name,
        grid_spec=pltpu.PrefetchScalarGridSpec(
            num_scalar_prefetch=1, grid=(4,),
            in_specs=[pl.BlockSpec((1, r, w), lambda i, core_ref: (2 * i + core_ref[0], 0, 0)),
                      pl.BlockSpec((1, r, w), lambda i, core_ref: (i, 0, 0))] + [ANY] * len(after),
            out_specs=pl.BlockSpec((1, r, w), lambda i, core_ref: (i, 0, 0))),
        out_shape=jax.ShapeDtypeStruct(other.shape, other.dtype),
        compiler_params=_params(),
    )(core, grad, other, *after)


def sum_owned(name, grad, others, dev_ids, after=()):
    _, r, w = grad.shape

    def body(ids_ref, *refs):
        acc = refs[0][0]
        for k in range(1, N_DEV):
            acc = acc + refs[k][0]
        refs[-1][...] = acc

    def pick(k):
        return pl.BlockSpec((1, r, w), lambda i, ids_ref: (ids_ref[k], 0, 0))

    return pl.pallas_call(
        body, name=name,
        grid_spec=pltpu.PrefetchScalarGridSpec(
            num_scalar_prefetch=1, grid=(1,), in_specs=[pick(k) for k in range(N_DEV)] + [ANY] * len(after),
            out_specs=pl.BlockSpec((r, w), lambda i, ids_ref: (ids_ref[0], 0))),
        out_shape=jax.ShapeDtypeStruct((N_DEV * r, w), F32),
        compiler_params=_params(),
    )(dev_ids, grad, *([others] * (N_DEV - 1)), *after)


def _adamw_update(w, g, m, v):
    nm = ADAM_B1 * m + np.float32(1.0 - ADAM_B1) * g
    nv = ADAM_B2 * v + np.float32(1.0 - ADAM_B2) * (g * g)
    m_hat = nm / np.float32(1.0 - ADAM_B1 ** ADAM_STEP)
    v_hat = nv / np.float32(1.0 - ADAM_B2 ** ADAM_STEP)
    return -ADAM_LR * (m_hat / (jnp.sqrt(v_hat) + ADAM_EPS) + ADAM_WD * w), nm, nv


def adamw_of_sums(name, part, others, chip_ids, w, m, v, after):
    _, r, wd = part.shape
    halves = 2
    rows = r // halves

    def body(ids_ref, p_ref, a_ref, b_ref, c_ref, w_ref, m_ref, v_ref, after_ref, g_ref, d_ref, nm_ref, nv_ref):
        g = ((p_ref[0].astype(F32) + a_ref[0].astype(F32)) + b_ref[0].astype(F32)) + c_ref[0].astype(F32)
        g_ref[...] = g
        d_ref[...], nm_ref[...], nv_ref[...] = _adamw_update(w_ref[...], g, m_ref[...], v_ref[...])

    def pick(k):
        return pl.BlockSpec((1, rows, wd), lambda i, ids_ref: (ids_ref[k], i, 0))

    whole = pl.BlockSpec((rows, wd), lambda i, ids_ref: (i, 0))
    shape = jax.ShapeDtypeStruct((r, wd), F32)
    return pl.pallas_call(
        body, name=name,
        grid_spec=pltpu.PrefetchScalarGridSpec(
            num_scalar_prefetch=1, grid=(halves,), in_specs=[pick(0), pick(1), pick(2), pick(3), whole, whole, whole, ANY],
            out_specs=[whole] * 4),
        out_shape=[shape] * 4,
        compiler_params=_params(),
    )(chip_ids, part, others, others, others, w, m, v, after)


def _pack_small(parts):
    flat = jnp.concatenate([parts[name].reshape(-1) for name, _ in SMALL])
    return jnp.pad(flat, (0, SMALL_ROWS * 128 - flat.shape[0])).reshape(SMALL_ROWS, 128)


LATE = "pre_mix_norm"


def adamw_small(packed_g, late_parts, ws, ms, vs):
    names = [name for name, _ in SMALL if name != "loss_sum"]
    k = len(names)
    shapes = [ws[name].shape[1:] if ws[name].ndim > 2 else ws[name].shape for name in names]
    first_row, at = [], 0
    for name, size in SMALL:
        first_row.append(at // 128)
        at += size

    def body(g_ref, late_ref, *refs):
        w_refs, m_refs, v_refs, outs = refs[:k], refs[k:2 * k], refs[2 * k:3 * k], refs[3 * k:]
        for i, (name, size) in enumerate(SMALL[:k]):
            if name == LATE:
                g = late_ref[0]
                for j in range(1, N_DEV):
                    g = g + late_ref[j]
            else:
                g = g_ref[first_row[i]:first_row[i] + size // 128, :].reshape(shapes[i])
            outs[i][...] = g
            outs[k + i][...], outs[2 * k + i][...], outs[3 * k + i][...] = _adamw_update(
                w_refs[i][...], g, m_refs[i][...], v_refs[i][...])
        outs[4 * k][...] = g_ref[first_row[k]:first_row[k] + 1, 0:1]

    vmem = pl.BlockSpec(memory_space=pltpu.VMEM)
    operands = [t[name].reshape(shape) for t in (ws, ms, vs) for name, shape in zip(names, shapes)]
    outs = pl.pallas_call(
        body, name="adamw_small", in_specs=[vmem] * (2 + 3 * k), out_specs=[vmem] * (4 * k + 1),
        out_shape=[jax.ShapeDtypeStruct(shape, F32) for _ in range(4) for shape in shapes] + [jax.ShapeDtypeStruct((1, 1), F32)],
        compiler_params=_params(()),
    )(packed_g, late_parts, *operands)
    tables = [{name: outs[j * k + i].reshape(ws[name].shape) for i, name in enumerate(names)} for j in range(4)]
    return (*tables, outs[4 * k])


def kernel(x, positions, pre_mix_norm, w_in, sgu_ln_gain, sgu_ln_bias, sgu_w_spatial, sgu_b_spatial, attn_out_norm, sgu_out_norm, w_out, post_mix_norm, pre_ffn_norm, w_gate, w_up, w_down, post_ffn_norm, loss_target, m_pre_mix_norm, m_w_in, m_sgu_ln_gain, m_sgu_ln_bias, m_sgu_w_spatial, m_sgu_b_spatial, m_attn_out_norm, m_sgu_out_norm, m_w_out, m_post_mix_norm, m_pre_ffn_norm, m_w_gate, m_w_up, m_w_down, m_post_ffn_norm, v_pre_mix_norm, v_w_in, v_sgu_ln_gain, v_sgu_ln_bias, v_sgu_w_spatial, v_sgu_b_spatial, v_attn_out_norm, v_sgu_out_norm, v_w_out, v_post_mix_norm, v_pre_ffn_norm, v_w_gate, v_w_up, v_w_down, v_post_ffn_norm):
    small_w = dict(pre_mix_norm=pre_mix_norm, sgu_ln_gain=sgu_ln_gain, sgu_ln_bias=sgu_ln_bias, sgu_w_spatial=sgu_w_spatial,
                   sgu_b_spatial=sgu_b_spatial, attn_out_norm=attn_out_norm, sgu_out_norm=sgu_out_norm,
                   post_mix_norm=post_mix_norm, pre_ffn_norm=pre_ffn_norm, post_ffn_norm=post_ffn_norm)
    small_m = dict(pre_mix_norm=m_pre_mix_norm, sgu_ln_gain=m_sgu_ln_gain, sgu_ln_bias=m_sgu_ln_bias, sgu_w_spatial=m_sgu_w_spatial,
                   sgu_b_spatial=m_sgu_b_spatial, attn_out_norm=m_attn_out_norm, sgu_out_norm=m_sgu_out_norm,
                   post_mix_norm=m_post_mix_norm, pre_ffn_norm=m_pre_ffn_norm, post_ffn_norm=m_post_ffn_norm)
    small_v = dict(pre_mix_norm=v_pre_mix_norm, sgu_ln_gain=v_sgu_ln_gain, sgu_ln_bias=v_sgu_ln_bias, sgu_w_spatial=v_sgu_w_spatial,
                   sgu_b_spatial=v_sgu_b_spatial, attn_out_norm=v_attn_out_norm, sgu_out_norm=v_sgu_out_norm,
                   post_mix_norm=v_post_mix_norm, pre_ffn_norm=v_pre_ffn_norm, post_ffn_norm=v_post_ffn_norm)

    x2d = x[0]
    target = loss_target[0]
    pos_col = positions.reshape(SEQ, 1)
    rot = _rot_consts()
    w_sp = sgu_w_spatial[0]
    bfull = jnp.repeat(sgu_b_spatial[0].T, HEAD_DIM, axis=1)

    x_i, y_i, c_i = (lax.axis_index(a).astype(jnp.int32) for a in MESH_AXES)
    dev = 4 * x_i + 2 * y_i + c_i
    core = c_i.reshape(1)
    chip = 2 * x_i + y_i
    chip_ids = jnp.stack([chip, chip ^ 1, chip ^ 2, chip ^ 3])
    dev_ids = jnp.stack([dev ^ m for m in range(N_DEV)])

    def gathered(name, bufs):
        return by_sequencer(name, [gather(bufs)], TO_GATHER)[0][0]

    def from_sibling(name, grads):
        return by_sequencer(name, [to_sibling(grads)], TO_SIBLING)[0][1]

    def from_chips(name, parts):
        return by_sequencer(name, [to_chips(parts)], TO_CHIPS)[0][1]

    (w_in_t,) = place_shards("place_w_in", [w_in[0].T], dev.reshape(1))
    (w_in_t,) = gathered("gather_w_in", [w_in_t])
    w_gate_t, w_up_t, w_out_f, w_down_f = place_shards(
        "place_weights", [w_gate[0].T, w_up[0].T, w_out[0], w_down[0]], dev.reshape(1))
    (w_out_f,) = gathered("gather_w_out", [w_out_f])
    w_gate_t, w_up_t = gathered("gather_w_gate_up", [w_gate_t, w_up_t])
    (w_down_f,) = gathered("gather_w_down", [w_down_f])

    h1, u, vs, q, k, v = in_proj(x2d, pos_col, pre_mix_norm, w_in_t, rot)
    attn_r, lse, attn = attn_fwd(q, k, v)
    (sgu,) = sgu_fwd(u, vs, sgu_ln_gain, sgu_ln_bias, w_sp, bfull)
    mix, y, x2, h2 = out_proj(attn, sgu, x2d, attn_out_norm, sgu_out_norm, w_out_f, post_mix_norm, pre_ffn_norm)
    gate, up, act = ffn_up(h2, w_gate_t, w_up_t)
    df, dx3, d_post_ffn, sq_err = ffn_down_loss(act, w_down_f, x2, post_ffn_norm, target)

    g_w_down = weight_grad("grad_w_down", act, df)
    (s_down,) = from_sibling("w_down_to_sibling", [g_w_down])
    dgate, dup, dx2, dy, d_pre_ffn, d_post_mix = ffn_bwd(
        df, w_down_f, gate, up, w_gate_t, w_up_t, x2, pre_ffn_norm, dx3, y, post_mix_norm, after=[g_w_down])
    p_down = sum_cores("sum_cores_down", g_w_down, s_down, core, after=[dy])
    (c_down,) = from_chips("w_down_to_chips", [p_down])
    g_w_gate, g_w_up = weight_grads("grad_w_gate_up", [dgate, dup], h2, after=[p_down])
    s_gate, s_up = from_sibling("w_gate_up_to_sibling", [g_w_gate, g_w_up])
    g_w_out = weight_grad("grad_w_out", mix, dy, after=[g_w_up, c_down])
    p_gate = sum_cores("sum_cores_gate", g_w_gate, s_gate, core, after=[g_w_out])
    p_up = sum_cores("sum_cores_up", g_w_up, s_up, core, after=[g_w_out])
    c_gate, c_up = from_chips("w_gate_up_to_chips", [p_gate, p_up])
    (s_out,) = from_sibling("w_out_to_sibling", [g_w_out])
    dsgu, d_attn_out, d_sgu_out, dattn_r = mix_bwd(dy, w_out_f, attn, sgu, attn_out_norm, sgu_out_norm, after=[p_gate, p_up])
    du, dvs, d_ln_gain, d_ln_bias, d_w_sp, d_bfull = sgu_bwd(u, vs, dsgu, sgu_ln_gain, sgu_ln_bias, w_sp, bfull)
    dq, dk, dv = attn_bwd(q, k, v, attn_r, lse, dattn_r, _to_residue_order(pos_col), rot)
    p_out = sum_cores("sum_cores_out", g_w_out, s_out, core, after=[dq])
    (c_out,) = from_chips("w_out_to_chips", [p_out])
    dq, dk, dv = (_from_residue_order(t) for t in (dq, dk, dv))
    dproj = jnp.concatenate([dq, dk, dv, du, dvs], axis=1)
    g_w_in = weight_grad("grad_w_in", dproj, h1, after=[c_gate, c_up])
    (s_in,) = from_sibling("w_in_to_sibling", [g_w_in])

    d_b_sp = d_bfull.reshape(CHUNK, N_GROUPS, HEAD_DIM).sum(axis=-1).T
    small_g = _pack_small(dict(pre_mix_norm=jnp.zeros_like(pre_mix_norm), sgu_ln_gain=d_ln_gain, sgu_ln_bias=d_ln_bias,
                               sgu_w_spatial=d_w_sp, sgu_b_spatial=d_b_sp, attn_out_norm=d_attn_out, sgu_out_norm=d_sgu_out,
                               post_mix_norm=d_post_mix, pre_ffn_norm=d_pre_ffn, post_ffn_norm=d_post_ffn,
                               loss_sum=sq_err))
    small_g = small_g.reshape(N_DEV, SMALL_ROWS // N_DEV, 128)
    ((_, (o_small,)),) = by_sequencer("small_to_owners", [to_owners(small_g)], TO_ALL)
    summed_small = sum_owned("sum_small", small_g, o_small, dev_ids, after=[g_w_in])
    (all_small,) = gathered("gather_small_grads", [summed_small])

    grad_x, d_pre_mix = in_bwd(dproj, w_in_t, x2d, pre_mix_norm, dx2, after=[g_w_in, c_out, summed_small])
    p_in = sum_cores("sum_cores_in", g_w_in, s_in, core, after=[d_pre_mix, all_small])
    (_, (c_in,)), (_, (late_parts,)) = by_sequencer(
        "last_sums_to_owners", [to_chips([p_in]), to_everyone(d_pre_mix)], TO_ALL)
    late_parts = lax.dynamic_update_slice(late_parts, d_pre_mix[None], (dev, 0, 0))

    big, last = {}, p_in
    for name, w, p, c, m, vv, transposed in (
            ("w_down", w_down, p_down, c_down, m_w_down, v_w_down, False), ("w_gate", w_gate, p_gate, c_gate, m_w_gate, v_w_gate, True),
            ("w_up", w_up, p_up, c_up, m_w_up, v_w_up, True), ("w_out", w_out, p_out, c_out, m_w_out, v_w_out, False),
            ("w_in", w_in, p_in, c_in, m_w_in, v_w_in, True)):
        turn = (lambda t: t.T) if transposed else (lambda t: t)
        outs = adamw_of_sums("adamw_" + name, p, c, chip_ids, turn(w[0]), turn(m[0]), turn(vv[0]), last)
        big[name], last = tuple(turn(t)[None] for t in outs), outs[0]
    sg, sd, snm, snv, loss_sum = adamw_small(all_small, late_parts, small_w, small_m, small_v)
    loss = loss_sum[0, 0] * np.float32(0.5 / D_MODEL)

    names = ["pre_mix_norm", "w_in", "sgu_ln_gain", "sgu_ln_bias", "sgu_w_spatial", "sgu_b_spatial", "attn_out_norm",
             "sgu_out_norm", "w_out", "post_mix_norm", "pre_ffn_norm", "w_gate", "w_up", "w_down", "post_ffn_norm"]
    outs = [loss, grad_x[None]]
    for i, table in enumerate((sg, sd, snm, snv)):
        for name in names:
            outs.append(big[name][i] if name in big else table[name])
    return tuple(outs)
```

```python
import numpy as np
import jax
import jax.numpy as jnp
from jax import lax
from jax.experimental import pallas as pl
from jax.experimental.pallas import tpu as pltpu
from jax.experimental.pallas import tpu_sc as plsc

F32 = jnp.float32
BF16 = jnp.bfloat16

SEQ = 2048
D_MODEL = 1024
ATTN_W = 512
SGU_W = 512
HEAD_DIM = 64
N_GROUPS = 8
CHUNK = 128
D_FF = 2816
IN_W = 3 * ATTN_W + 2 * SGU_W
DILATIONS = (1, 4, 16)
ROPE_THETA = 500000.0
ROT_DIM = 16
ROT_HALF = 8
RMS_EPS = 1e-6
LN_EPS = 1e-5
Q_SCALE = 0.125
NEG = -1e30

N_DEV = 8
MESH_AXES = ("x", "y", "c")
MESH = pl.DeviceIdType.MESH

ADAM_LR = 0.001
ADAM_B1 = 0.9
ADAM_B2 = 0.999
ADAM_EPS = 1e-08
ADAM_WD = 0.01
ADAM_STEP = 10

VMEM_LIMIT = 60 * 1024 * 1024
ANY = pl.BlockSpec(memory_space=pl.ANY)

SMALL = (("pre_mix_norm", 1024), ("sgu_ln_gain", 512), ("sgu_ln_bias", 512), ("sgu_w_spatial", 8 * 128 * 128),
         ("sgu_b_spatial", 1024), ("attn_out_norm", 512), ("sgu_out_norm", 512), ("post_mix_norm", 1024),
         ("pre_ffn_norm", 1024), ("post_ffn_norm", 1024), ("loss_sum", 1))
SMALL_ROWS = 1152


def _params(sem=("arbitrary",)):
    return pltpu.CompilerParams(dimension_semantics=sem, vmem_limit_bytes=VMEM_LIMIT)


def _dot(a, b):
    return jnp.dot(a, b, preferred_element_type=F32)


def _dot_nt(a, b):
    return lax.dot_general(a, b, (((1,), (1,)), ((), ())), preferred_element_type=F32)


def _dot_tn(a, b):
    return lax.dot_general(a, b, (((0,), (0,)), ((), ())), preferred_element_type=F32)


def _rms(z):
    return lax.rsqrt(jnp.mean(z * z, axis=-1, keepdims=True) + RMS_EPS)


def _rms_bwd(z, gain, d):
    r = _rms(z)
    n = z * r
    dn = d * gain
    dz = r * (dn - n * jnp.mean(dn * n, axis=-1, keepdims=True))
    return dz, jnp.sum(d * n, axis=0, keepdims=True)


def _gelu(z):
    return 0.5 * z * (1.0 + lax.erf(z * np.float32(1.0 / np.sqrt(2.0))))


def _gelu_grad(z):
    cdf = 0.5 * (1.0 + lax.erf(z * np.float32(1.0 / np.sqrt(2.0))))
    return cdf + z * jnp.exp(-0.5 * z * z) * np.float32(1.0 / np.sqrt(2.0 * np.pi))


def _rot_tables(pos_col, invf, ma, mb):
    ang = pos_col.astype(F32) * invf
    s = jnp.sin(ang)
    return jnp.cos(ang), s * ma, s * mb


def _rot(t, c, sa, sb):
    return t * c + pltpu.roll(t, 120, 1) * sa + pltpu.roll(t, 8, 1) * sb


def _rot_t(d, c, sa, sb):
    return d * c + pltpu.roll(d * sa, 8, 1) + pltpu.roll(d * sb, 120, 1)


def _rot_consts():
    lane = np.arange(128) % HEAD_DIM
    inv_freq = (np.float32(ROPE_THETA) ** (-np.arange(0, ROT_DIM, 2, dtype=np.float32) / np.float32(ROT_DIM))).astype(np.float32)
    invf = np.where(lane < ROT_DIM, inv_freq[lane % ROT_HALF], 0.0).astype(np.float32)
    ma = np.where(lane < ROT_HALF, -1.0, 0.0).astype(np.float32)
    mb = np.where((lane >= ROT_HALF) & (lane < ROT_DIM), 1.0, 0.0).astype(np.float32)
    return jnp.asarray(invf[None]), jnp.asarray(ma[None]), jnp.asarray(mb[None])


def _row_spec(tm, w):
    return pl.BlockSpec((tm, w), lambda i: (i, 0))


def _full_spec(shape):
    return pl.BlockSpec(shape, lambda i: (0,) * len(shape))


def _weight_spec(shape):
    return pl.BlockSpec(shape, lambda i: (0,) * len(shape), pipeline_mode=pl.Buffered(1))


RES = 16


def _residue_scratch(n_arrays, tm, width):
    return [pltpu.VMEM((2, n_arrays, tm // RES, RES, width), F32), pltpu.SemaphoreType.DMA((2, n_arrays, RES))]


def _to_residue_rows(tiles, outs, scratch, sems, tm, n_steps):
    i = pl.program_id(0)
    slot = i % 2
    per = tm // RES

    def copies(step, s):
        return [pltpu.make_async_copy(scratch.at[s, a, :, b, :],
                                      outs[a].at[pl.ds(pl.multiple_of(b * (SEQ // RES) + per * step, per), per), :],
                                      sems.at[s, a, b]) for a in range(len(outs)) for b in range(RES)]

    @pl.when(i >= 2)
    def _():
        for cp in copies(i - 2, slot):
            cp.wait()

    for a, tile in enumerate(tiles):
        scratch[slot, a] = tile.reshape(per, RES, tile.shape[-1])
    for cp in copies(i, slot):
        cp.start()

    @pl.when(i == n_steps - 1)
    def _():
        for cp in copies(i - 1, 1 - slot) + copies(i, slot):
            cp.wait()


def in_proj(x, pos_col, g1, w_in_t, rot):
    tm = 512
    n_steps = SEQ // tm

    def body(x_ref, pos_ref, g_ref, w_ref, invf_ref, ma_ref, mb_ref, h_ref, u_ref, vs_ref, q_ref, k_ref, v_ref, scratch, sems):
        xf = x_ref[...]
        h = (xf * _rms(xf) * g_ref[...]).astype(BF16)
        h_ref[...] = h
        proj = _dot_nt(h, w_ref[...])
        c, sa, sb = _rot_tables(pos_ref[...], invf_ref[...], ma_ref[...], mb_ref[...])
        slabs = range(ATTN_W // 128)
        q = jnp.concatenate([_rot(proj[:, j * 128:(j + 1) * 128], c, sa, sb) * Q_SCALE for j in slabs], axis=1)
        k = jnp.concatenate([_rot(proj[:, ATTN_W + j * 128:ATTN_W + (j + 1) * 128], c, sa, sb) for j in slabs], axis=1)
        u_ref[...] = proj[:, 3 * ATTN_W:3 * ATTN_W + SGU_W]
        vs_ref[...] = proj[:, 3 * ATTN_W + SGU_W:]
        _to_residue_rows([q, k, proj[:, 2 * ATTN_W:3 * ATTN_W]], [q_ref, k_ref, v_ref], scratch, sems, tm, n_steps)

    act = jax.ShapeDtypeStruct((SEQ, 512), F32)
    return _call(
        "in_proj", body, n_steps,
        [_row_spec(tm, D_MODEL), _row_spec(tm, 1), _full_spec((1, D_MODEL)), _weight_spec((IN_W, D_MODEL)),
         _full_spec((1, 128)), _full_spec((1, 128)), _full_spec((1, 128))],
        [_row_spec(tm, D_MODEL)] + [_row_spec(tm, 512)] * 2 + [ANY] * 3,
        [jax.ShapeDtypeStruct((SEQ, D_MODEL), BF16)] + [act] * 5,
        (x, pos_col, g1, w_in_t, *rot), scratch_shapes=_residue_scratch(3, tm, ATTN_W))


def _to_residue_order(t):
    return t.reshape(SEQ // RES, RES, -1).transpose(1, 0, 2).reshape(t.shape)


def _from_residue_order(t):
    return t.reshape(RES, SEQ // RES, -1).transpose(1, 0, 2).reshape(t.shape)


def _block_rows(d, r, n):
    if d == 16:
        slices = [(128 * r, 128)]
    elif d == 4:
        slices = [(128 * (4 * b + r) + 32 * n, 32) for b in range(4)]
    else:
        slices = [(128 * b + 8 * n, 8) for b in range(RES)]
    return [(s if isinstance(s, int) else pl.multiple_of(s, z), z) for s, z in slices]


def _block_step(d, i):
    if d == 16:
        return i
    if d == 4:
        return 4 * (i & 31) + (i >> 5)
    return 16 * (i & 7) + (i >> 3)


def _attn_masks(d):
    row2 = _block_step(d, lax.broadcasted_iota(jnp.int32, (128, 256), 0))
    col2 = lax.broadcasted_iota(jnp.int32, (128, 256), 1)
    key2 = _block_step(d, col2 & 127)
    mask2 = jnp.logical_or(jnp.logical_and(col2 < 128, key2 >= row2), jnp.logical_and(col2 >= 128, key2 <= row2))
    row1 = _block_step(d, lax.broadcasted_iota(jnp.int32, (128, 128), 0))
    col1 = lax.broadcasted_iota(jnp.int32, (128, 128), 1)
    return col1 < HEAD_DIM, _block_step(d, col1) <= row1, mask2


def _load_rows(ref, slices):
    parts = [ref[pl.ds(s, z), :] for s, z in slices]
    return parts[0] if len(parts) == 1 else jnp.concatenate(parts, axis=0)


def _for_each_group(fn):
    for p, d in enumerate(DILATIONS):
        masks = _attn_masks(d)
        if d == 16:
            def group(i, carry, p=p, masks=masks):
                fn(p, masks, [(_block_rows(16, 4 * i + g, 0), None) for g in range(4)])
                return carry

            lax.fori_loop(0, 4, group, 0)
        elif d == 4:
            fn(p, masks, [(_block_rows(4, r, 0), None) for r in range(4)])

            def group(i, carry, p=p, masks=masks):
                fn(p, masks, [(_block_rows(4, r, i + 1), _block_rows(4, r, i)) for r in range(4)])
                return carry

            lax.fori_loop(0, 3, group, 0)
        else:
            fn(p, masks, [(_block_rows(1, 0, 0), None)])

            def group(i, carry, p=p, masks=masks):
                fn(p, masks, [(_block_rows(1, 0, 3 * i + g + 1), _block_rows(1, 0, 3 * i + g)) for g in range(3)])
                return carry

            lax.fori_loop(0, 5, group, 0)


def attn_fwd(q, k, v):
    def body(q_ref, k_ref, v_ref, o_ref, lse_ref, nat_ref, op_ref, lp_ref, sems):
        def group(p, masks, blocks):
            head0, mask1, mask2 = masks
            heads = (head0, jnp.logical_not(head0))
            keys = [rows if prev is None else prev + rows for rows, prev in blocks]
            mask = [mask1 if prev is None else mask2 for _, prev in blocks]
            qb = [_load_rows(q_ref, rows) for rows, _ in blocks]
            kk = [_load_rows(k_ref, ks).astype(BF16) for ks in keys]
            vv = [_load_rows(v_ref, ks).astype(BF16) for ks in keys]
            chains = [(g, hm) for g in range(len(blocks)) for hm in heads]
            s = [jnp.where(mask[g], _dot_nt(jnp.where(hm, qb[g], 0.0).astype(BF16), kk[g]), NEG) for g, hm in chains]
            m = [jnp.max(t, axis=-1, keepdims=True) for t in s]
            e = [jnp.exp(t - mt) for t, mt in zip(s, m)]
            l = [jnp.sum(t, axis=-1, keepdims=True) for t in e]
            pv = [_dot(t.astype(BF16), vv[g]) for t, (g, _) in zip(e, chains)]
            for g, (rows, _) in enumerate(blocks):
                o_blk = jnp.where(head0, pv[2 * g] / l[2 * g], pv[2 * g + 1] / l[2 * g + 1])
                l_blk = jnp.where(head0, jnp.broadcast_to(m[2 * g] + jnp.log(l[2 * g]), (128, 128)),
                                  jnp.broadcast_to(m[2 * g + 1] + jnp.log(l[2 * g + 1]), (128, 128)))
                at = 0
                for start, size in rows:
                    op_ref[p, pl.ds(start, size), :] = o_blk[at:at + size]
                    lp_ref[p, pl.ds(start, size), :] = l_blk[at:at + size]
                    at += size

        _for_each_group(group)

        def combine(i, carry):
            rows = pl.ds(pl.multiple_of(i * 256, 256), 256)
            ls = [lp_ref[p, rows, :] for p in range(3)]
            m = jnp.maximum(jnp.maximum(ls[0], ls[1]), ls[2])
            lse = m + jnp.log(jnp.exp(ls[0] - m) + jnp.exp(ls[1] - m) + jnp.exp(ls[2] - m))
            o = jnp.zeros((256, 128), F32)
            for p in range(3):
                o = o + jnp.exp(ls[p] - lse) * op_ref[p, rows, :]
            o_ref[rows, :] = o
            lse_ref[rows, :] = lse
            return carry

        lax.fori_loop(0, SEQ // 256, combine, 0)

        lanes = pl.ds(pl.multiple_of(pl.program_id(0) * 128, 128), 128)
        back = [pltpu.make_async_copy(o_ref.at[pl.ds(b * (SEQ // RES), SEQ // RES), :], nat_ref.at[:, b, lanes], sems.at[b])
                for b in range(RES)]
        for cp in back:
            cp.start()
        for cp in back:
            cp.wait()

    slab = pl.BlockSpec((SEQ, 128), lambda i: (0, i))
    out = jax.ShapeDtypeStruct((SEQ, ATTN_W), F32)
    attn_r, lse, attn = _call(
        "attn_fwd", body, ATTN_W // 128, [slab] * 3, [slab] * 2 + [ANY],
        [out, out, jax.ShapeDtypeStruct((SEQ // RES, RES, ATTN_W), F32)], (q, k, v),
        scratch_shapes=[pltpu.VMEM((3, SEQ, 128), F32), pltpu.VMEM((3, SEQ, 128), F32), pltpu.SemaphoreType.DMA((RES,))])
    return attn_r, lse, attn.reshape(SEQ, ATTN_W)


def _causal_weights(w_ref):
    row = lax.broadcasted_iota(jnp.int32, (CHUNK, CHUNK), 0)
    col = lax.broadcasted_iota(jnp.int32, (CHUNK, CHUNK), 1)
    return [jnp.where(col <= row, w_ref[g], 0.0).astype(BF16) for g in range(N_GROUPS)], col <= row


def _sgu_chunk_fwd(u, vs, lg, lb, wc, bfull, head0):
    ug = _gelu(u)
    vg = _gelu(vs)
    xc = vg - jnp.mean(vg, axis=-1, keepdims=True)
    rstd = lax.rsqrt(jnp.mean(xc * xc, axis=-1, keepdims=True) + LN_EPS)
    xhat = xc * rstd
    vn = xhat * lg + lb
    mixed = []
    for gp in range(SGU_W // 128):
        vp = vn[:, gp * 128:(gp + 1) * 128].astype(BF16)
        mixed.append(jnp.where(head0, _dot(wc[2 * gp], vp), _dot(wc[2 * gp + 1], vp)))
    ms = jnp.concatenate(mixed, axis=1) + bfull
    return ug, xhat, rstd, vn, ms


def sgu_fwd(u, vs, lg, lb, w_sp, bfull):
    cpb = 4

    def body(u_ref, vs_ref, lg_ref, lb_ref, w_ref, b_ref, o_ref):
        wc, _ = _causal_weights(w_ref)
        head0 = lax.broadcasted_iota(jnp.int32, (CHUNK, 128), 1) < HEAD_DIM
        for ci in range(cpb):
            rows = pl.ds(ci * CHUNK, CHUNK)
            ug, _, _, _, ms = _sgu_chunk_fwd(u_ref[rows, :], vs_ref[rows, :], lg_ref[...], lb_ref[...], wc, b_ref[...], head0)
            o_ref[rows, :] = ug * ms

    tm = cpb * CHUNK
    return _call(
        "sgu_fwd", body, SEQ // tm,
        [_row_spec(tm, SGU_W), _row_spec(tm, SGU_W), _full_spec((1, SGU_W)), _full_spec((1, SGU_W)),
         _full_spec((N_GROUPS, CHUNK, CHUNK)), _full_spec((CHUNK, SGU_W))],
        [_row_spec(tm, SGU_W)], [jax.ShapeDtypeStruct((SEQ, SGU_W), F32)],
        (u, vs, lg, lb, w_sp, bfull))


def out_proj(attn, sgu, x, ga, gs, w_out, gpm, gpf):
    tm = 512

    def body(a_ref, s_ref, x_ref, ga_ref, gs_ref, w_ref, gpm_ref, gpf_ref, mix_ref, y_ref, x2_ref, h2_ref):
        a = a_ref[...]
        s = s_ref[...]
        an = (a * _rms(a) * ga_ref[...]).astype(BF16)
        sn = (s * _rms(s) * gs_ref[...]).astype(BF16)
        mix_ref[:, :ATTN_W] = an
        mix_ref[:, ATTN_W:] = sn
        y = _dot(an, w_ref[:ATTN_W, :]) + _dot(sn, w_ref[ATTN_W:, :])
        y_ref[...] = y
        x2 = x_ref[...] + y * _rms(y) * gpm_ref[...]
        x2_ref[...] = x2
        h2_ref[...] = (x2 * _rms(x2) * gpf_ref[...]).astype(BF16)

    wide = jax.ShapeDtypeStruct((SEQ, D_MODEL), F32)
    wide16 = jax.ShapeDtypeStruct((SEQ, D_MODEL), BF16)
    return _call(
        "out_proj", body, SEQ // tm,
        [_row_spec(tm, ATTN_W), _row_spec(tm, SGU_W), _row_spec(tm, D_MODEL), _full_spec((1, ATTN_W)),
         _full_spec((1, SGU_W)), _weight_spec((D_MODEL, D_MODEL)), _full_spec((1, D_MODEL)), _full_spec((1, D_MODEL))],
        [_row_spec(tm, D_MODEL)] * 4, [wide16, wide, wide, wide16],
        (attn, sgu, x, ga, gs, w_out, gpm, gpf))


def ffn_up(h2, w_gate_t, w_up_t):
    tm = 256

    def body(h_ref, wg_ref, wu_ref, g_ref, u_ref, a_ref):
        h = h_ref[...]
        g = _dot_nt(h, wg_ref[...])
        u = _dot_nt(h, wu_ref[...])
        g_ref[...] = g.astype(BF16)
        u_ref[...] = u.astype(BF16)
        a_ref[...] = (g * jax.nn.sigmoid(g) * u).astype(BF16)

    ff = jax.ShapeDtypeStruct((SEQ, D_FF), BF16)
    return _call(
        "ffn_up", body, SEQ // tm,
        [_row_spec(tm, D_MODEL), _weight_spec((D_FF, D_MODEL)), _weight_spec((D_FF, D_MODEL))],
        [_row_spec(tm, D_FF)] * 3, [ff, ff, jax.ShapeDtypeStruct((SEQ, D_FF), BF16)],
        (h2, w_gate_t, w_up_t))


def ffn_down_loss(act, w_down, x2, gpo, target):
    tm = 512

    def body(a_ref, w_ref, x2_ref, g_ref, t_ref, df_ref, dx3_ref, dg_ref, loss_ref):
        f = _dot(a_ref[...], w_ref[...])
        gain = g_ref[...]
        err = x2_ref[...] + f * _rms(f) * gain - t_ref[...]
        dx3 = err * np.float32(1.0 / D_MODEL)
        dx3_ref[...] = dx3
        df, dg = _rms_bwd(f, gain, dx3)
        df_ref[...] = df.astype(BF16)

        @pl.when(pl.program_id(0) == 0)
        def _():
            dg_ref[...] = jnp.zeros_like(dg_ref)
            loss_ref[...] = jnp.zeros_like(loss_ref)

        dg_ref[...] += dg
        loss_ref[...] += jnp.sum(err * err, axis=(0, 1), keepdims=True)

    return pl.pallas_call(
        body, name="ffn_down_loss", grid=(SEQ // tm,),
        in_specs=[_row_spec(tm, D_FF), _weight_spec((D_FF, D_MODEL)), _row_spec(tm, D_MODEL), _full_spec((1, D_MODEL)),
                  _row_spec(tm, D_MODEL)],
        out_specs=[_row_spec(tm, D_MODEL), _row_spec(tm, D_MODEL), _full_spec((1, D_MODEL)), _full_spec((1, 1))],
        out_shape=[jax.ShapeDtypeStruct((SEQ, D_MODEL), BF16), jax.ShapeDtypeStruct((SEQ, D_MODEL), F32),
                   jax.ShapeDtypeStruct((1, D_MODEL), F32), jax.ShapeDtypeStruct((1, 1), F32)],
        compiler_params=_params(),
    )(act, w_down, x2, gpo, target)


def ffn_bwd(df, w_down, gate, up, w_gate_t, w_up_t, x2, gpf, dx3, y, gpm, after=()):
    tm = 256

    def body(df_ref, wd_ref, g_ref, u_ref, wg_ref, wu_ref, x2_ref, gpf_ref, dx3_ref, y_ref, gpm_ref,
             dg_ref, du_ref, dx2_ref, dy_ref, dgpf_ref, dgpm_ref):
        dact = _dot_nt(df_ref[...], wd_ref[...])
        g = g_ref[...].astype(F32)
        s = jax.nn.sigmoid(g)
        dup = (dact * g * s).astype(BF16)
        dgate = (dact * u_ref[...].astype(F32) * (s * (1.0 + g * (1.0 - s)))).astype(BF16)
        du_ref[...] = dup
        dg_ref[...] = dgate
        dh2 = _dot(dgate, wg_ref[...]) + _dot(dup, wu_ref[...])
        dz, dgpf = _rms_bwd(x2_ref[...], gpf_ref[...], dh2)
        dx2 = dx3_ref[...] + dz
        dx2_ref[...] = dx2
        dy, dgpm = _rms_bwd(y_ref[...], gpm_ref[...], dx2)
        dy_ref[...] = dy.astype(BF16)

        @pl.when(pl.program_id(0) == 0)
        def _():
            dgpf_ref[...] = jnp.zeros_like(dgpf_ref)
            dgpm_ref[...] = jnp.zeros_like(dgpm_ref)

        dgpf_ref[...] += dgpf
        dgpm_ref[...] += dgpm

    vec = jax.ShapeDtypeStruct((1, D_MODEL), F32)
    ff16 = jax.ShapeDtypeStruct((SEQ, D_FF), BF16)
    return _call(
        "ffn_bwd", body, SEQ // tm,
        [_row_spec(tm, D_MODEL), _weight_spec((D_FF, D_MODEL)), _row_spec(tm, D_FF), _row_spec(tm, D_FF),
         _weight_spec((D_FF, D_MODEL)), _weight_spec((D_FF, D_MODEL)), _row_spec(tm, D_MODEL), _full_spec((1, D_MODEL)),
         _row_spec(tm, D_MODEL), _row_spec(tm, D_MODEL), _full_spec((1, D_MODEL))],
        [_row_spec(tm, D_FF), _row_spec(tm, D_FF), _row_spec(tm, D_MODEL), _row_spec(tm, D_MODEL),
         _full_spec((1, D_MODEL)), _full_spec((1, D_MODEL))],
        [ff16, ff16, jax.ShapeDtypeStruct((SEQ, D_MODEL), F32), jax.ShapeDtypeStruct((SEQ, D_MODEL), BF16), vec, vec],
        (df, w_down, gate, up, w_gate_t, w_up_t, x2, gpf, dx3, y, gpm), after=after)


def weight_grads(name, lhs, b, after=()):
    m, n, k = lhs[0].shape[1], b.shape[1], len(lhs)
    tr = 256

    def body(*refs):
        for a_ref, o_ref in zip(refs[:k], refs[k + 1:]):
            o_ref[...] = _dot_tn(a_ref[...], refs[k][...]).astype(BF16)

    outs = _call(
        name, body, m // tr, [pl.BlockSpec((SEQ, tr), lambda i: (0, i))] * k + [_weight_spec((SEQ, n))],
        [_row_spec(tr, n)] * k, [jax.ShapeDtypeStruct((m, n), BF16)] * k, (*lhs, b), after=after)
    return [out.reshape(N_DEV, m // N_DEV, n) for out in outs]


def weight_grad(name, a, b, after=()):
    return weight_grads(name, [a], b, after)[0]


def weight_grad_of_parts(name, parts, b, after=()):
    p, n, k = parts[0].shape[1], b.shape[1], len(parts)
    tr = 256
    per = p // tr

    def body(*refs):
        tile = pl.program_id(0)
        for j in range(k):
            @pl.when(tile // per == j)
            def _(j=j):
                refs[k + 1][...] = _dot_tn(refs[j][...], refs[k][...]).astype(BF16)

    def part_spec(j):
        return pl.BlockSpec((SEQ, tr), lambda i: (0, jnp.clip(i - per * j, 0, per - 1)))

    (out,) = _call(
        name, body, k * per, [part_spec(j) for j in range(k)] + [_weight_spec((SEQ, n))],
        [_row_spec(tr, n)], [jax.ShapeDtypeStruct((k * p, n), BF16)], (*parts, b), after=after)
    return out.reshape(N_DEV, k * p // N_DEV, n)


def mix_bwd(dy, w_out, attn, sgu, ga, gs, after=()):
    tm = 512
    n_steps = SEQ // tm

    def body(dy_ref, w_ref, a_ref, s_ref, ga_ref, gs_ref, ds_ref, dga_ref, dgs_ref, da_ref, scratch, sems):
        dy = dy_ref[...]
        da, dga = _rms_bwd(a_ref[...], ga_ref[...], _dot_nt(dy, w_ref[:ATTN_W, :]))
        ds, dgs = _rms_bwd(s_ref[...], gs_ref[...], _dot_nt(dy, w_ref[ATTN_W:, :]))
        ds_ref[...] = ds
        _to_residue_rows([da], [da_ref], scratch, sems, tm, n_steps)

        @pl.when(pl.program_id(0) == 0)
        def _():
            dga_ref[...] = jnp.zeros_like(dga_ref)
            dgs_ref[...] = jnp.zeros_like(dgs_ref)

        dga_ref[...] += dga
        dgs_ref[...] += dgs

    half = jax.ShapeDtypeStruct((SEQ, 512), F32)
    vec = jax.ShapeDtypeStruct((1, 512), F32)
    return _call(
        "mix_bwd", body, n_steps,
        [_row_spec(tm, D_MODEL), _weight_spec((D_MODEL, D_MODEL)), _row_spec(tm, 512), _row_spec(tm, 512),
         _full_spec((1, 512)), _full_spec((1, 512))],
        [_row_spec(tm, 512), _full_spec((1, 512)), _full_spec((1, 512)), ANY],
        [half, vec, vec, half], (dy, w_out, attn, sgu, ga, gs), scratch_shapes=_residue_scratch(1, tm, ATTN_W), after=after)


def sgu_bwd(u, vs, dsgu, lg, lb, w_sp, bfull):
    cpb = 4

    def body(u_ref, vs_ref, d_ref, lg_ref, lb_ref, w_ref, b_ref, du_ref, dvs_ref, dlg_ref, dlb_ref, dw_ref, db_ref):
        wc, causal = _causal_weights(w_ref)
        head0 = lax.broadcasted_iota(jnp.int32, (CHUNK, 128), 1) < HEAD_DIM
        lg = lg_ref[...]

        @pl.when(pl.program_id(0) == 0)
        def _():
            dlg_ref[...] = jnp.zeros_like(dlg_ref)
            dlb_ref[...] = jnp.zeros_like(dlb_ref)
            dw_ref[...] = jnp.zeros_like(dw_ref)
            db_ref[...] = jnp.zeros_like(db_ref)

        for ci in range(cpb):
            rows = pl.ds(ci * CHUNK, CHUNK)
            u = u_ref[rows, :]
            vs = vs_ref[rows, :]
            d = d_ref[rows, :]
            ug, xhat, rstd, vn, ms = _sgu_chunk_fwd(u, vs, lg, lb_ref[...], wc, b_ref[...], head0)
            du_ref[rows, :] = (d * ms * _gelu_grad(u)).astype(BF16)
            dms = d * ug
            db_ref[...] += dms
            dvn = []
            for gp in range(SGU_W // 128):
                dmp = dms[:, gp * 128:(gp + 1) * 128]
                dm0 = jnp.where(head0, dmp, 0.0).astype(BF16)
                dm1 = jnp.where(head0, 0.0, dmp).astype(BF16)
                vp = vn[:, gp * 128:(gp + 1) * 128].astype(BF16)
                dw_ref[2 * gp] += _dot_nt(dm0, vp)
                dw_ref[2 * gp + 1] += _dot_nt(dm1, vp)
                dvn.append(_dot_tn(wc[2 * gp], dm0) + _dot_tn(wc[2 * gp + 1], dm1))
            dvn = jnp.concatenate(dvn, axis=1)
            dlg_ref[...] += jnp.sum(dvn * xhat, axis=0, keepdims=True)
            dlb_ref[...] += jnp.sum(dvn, axis=0, keepdims=True)
            dxh = dvn * lg
            dvg = rstd * (dxh - jnp.mean(dxh, axis=-1, keepdims=True) - xhat * jnp.mean(dxh * xhat, axis=-1, keepdims=True))
            dvs_ref[rows, :] = (dvg * _gelu_grad(vs)).astype(BF16)

        @pl.when(pl.program_id(0) == pl.num_programs(0) - 1)
        def _():
            for g in range(N_GROUPS):
                dw_ref[g] = jnp.where(causal, dw_ref[g], 0.0)

    tm = cpb * CHUNK
    half16 = jax.ShapeDtypeStruct((SEQ, SGU_W), BF16)
    vec = jax.ShapeDtypeStruct((1, SGU_W), F32)
    return _call(
        "sgu_bwd", body, SEQ // tm,
        [_row_spec(tm, SGU_W)] * 3 + [_full_spec((1, SGU_W)), _full_spec((1, SGU_W)),
                                      _full_spec((N_GROUPS, CHUNK, CHUNK)), _full_spec((CHUNK, SGU_W))],
        [_row_spec(tm, SGU_W), _row_spec(tm, SGU_W), _full_spec((1, SGU_W)), _full_spec((1, SGU_W)),
         _full_spec((N_GROUPS, CHUNK, CHUNK)), _full_spec((CHUNK, SGU_W))],
        [half16, half16, vec, vec, jax.ShapeDtypeStruct((N_GROUPS, CHUNK, CHUNK), F32),
         jax.ShapeDtypeStruct((CHUNK, SGU_W), F32)],
        (u, vs, dsgu, lg, lb, w_sp, bfull))


def attn_bwd(q, k, v, o, lse, do, pos_col, rot):
    def body(q_ref, k_ref, v_ref, o_ref, lse_ref, do_ref, pos_ref, invf_ref, ma_ref, mb_ref,
             dq_ref, dk_ref, dv_ref, dqa_ref, dka_ref, dva_ref, dlt_ref, rot_ref):
        dqa_ref[...] = jnp.zeros_like(dqa_ref)
        dka_ref[...] = jnp.zeros_like(dka_ref)
        dva_ref[...] = jnp.zeros_like(dva_ref)

        def delta(i, carry):
            rows = pl.ds(pl.multiple_of(i * 256, 256), 256)
            prod = do_ref[rows, :] * o_ref[rows, :]
            h0 = lax.broadcasted_iota(jnp.int32, (256, 128), 1) < HEAD_DIM
            d0 = jnp.sum(jnp.where(h0, prod, 0.0), axis=-1, keepdims=True)
            d1 = jnp.sum(jnp.where(h0, 0.0, prod), axis=-1, keepdims=True)
            dlt_ref[rows, :] = jnp.where(h0, d0, d1)
            return carry

        lax.fori_loop(0, SEQ // 256, delta, 0)

        def add_rows(ref, slices, val):
            at = 0
            for start, size in slices:
                ref[pl.ds(start, size), :] += val[at:at + size]
                at += size

        def group(p, masks, blocks):
            head0, mask1, mask2 = masks
            heads = (head0, jnp.logical_not(head0))
            keys = [rows if prev is None else prev + rows for rows, prev in blocks]
            mask = [mask1 if prev is None else mask2 for _, prev in blocks]
            kk = [_load_rows(k_ref, ks).astype(BF16) for ks in keys]
            vv = [_load_rows(v_ref, ks).astype(BF16) for ks in keys]
            qb = [_load_rows(q_ref, rows) for rows, _ in blocks]
            dob = [_load_rows(do_ref, rows) for rows, _ in blocks]
            lse_b = [_load_rows(lse_ref, rows) for rows, _ in blocks]
            dlt_b = [_load_rows(dlt_ref, rows) for rows, _ in blocks]
            chains = [(g, h) for g in range(len(blocks)) for h in range(2)]
            qm = [jnp.where(heads[h], qb[g], 0.0).astype(BF16) for g, h in chains]
            dom = [jnp.where(heads[h], dob[g], 0.0).astype(BF16) for g, h in chains]
            s = [_dot_nt(qm[c], kk[g]) for c, (g, h) in enumerate(chains)]
            dp = [_dot_nt(dom[c], vv[g]) for c, (g, h) in enumerate(chains)]
            pr = [jnp.where(mask[g], jnp.exp(s[c] - lse_b[g][:, h * HEAD_DIM:h * HEAD_DIM + 1]), 0.0)
                  for c, (g, h) in enumerate(chains)]
            ds = [(pr[c] * (dp[c] - dlt_b[g][:, h * HEAD_DIM:h * HEAD_DIM + 1])).astype(BF16)
                  for c, (g, h) in enumerate(chains)]
            dv = [_dot_tn(pr[c].astype(BF16), dom[c]) for c in range(len(chains))]
            dk = [_dot_tn(ds[c], qm[c]) for c in range(len(chains))]
            dq = [_dot(ds[c], kk[g]) for c, (g, h) in enumerate(chains)]
            for g, (rows, _) in enumerate(blocks):
                add_rows(dqa_ref, rows, jnp.where(head0, dq[2 * g], dq[2 * g + 1]))
                add_rows(dka_ref, keys[g], dk[2 * g] + dk[2 * g + 1])
                add_rows(dva_ref, keys[g], dv[2 * g] + dv[2 * g + 1])

        _for_each_group(group)

        @pl.when(pl.program_id(0) == 0)
        def _():
            def tables(i, carry):
                rows = pl.ds(pl.multiple_of(i * 256, 256), 256)
                c, sa, sb = _rot_tables(pos_ref[rows, :], invf_ref[...], ma_ref[...], mb_ref[...])
                rot_ref[0, rows, :] = c
                rot_ref[1, rows, :] = sa
                rot_ref[2, rows, :] = sb
                return carry

            lax.fori_loop(0, SEQ // 256, tables, 0)

        def finish(i, carry):
            rows = pl.ds(pl.multiple_of(i * 256, 256), 256)
            c, sa, sb = rot_ref[0, rows, :], rot_ref[1, rows, :], rot_ref[2, rows, :]
            dq_ref[rows, :] = _rot_t(dqa_ref[rows, :] * Q_SCALE, c, sa, sb).astype(BF16)
            dk_ref[rows, :] = _rot_t(dka_ref[rows, :], c, sa, sb).astype(BF16)
            dv_ref[rows, :] = dva_ref[rows, :].astype(BF16)
            return carry

        lax.fori_loop(0, SEQ // 256, finish, 0)

    slab = pl.BlockSpec((SEQ, 128), lambda i: (0, i))
    out = jax.ShapeDtypeStruct((SEQ, ATTN_W), BF16)
    acc = pltpu.VMEM((SEQ, 128), F32)
    return _call(
        "attn_bwd", body, ATTN_W // 128,
        [slab] * 6 + [_full_spec((SEQ, 1)), _full_spec((1, 128)), _full_spec((1, 128)), _full_spec((1, 128))],
        [slab] * 3, [out, out, out], (q, k, v, o, lse, do, pos_col, *rot),
        scratch_shapes=[acc, acc, acc, acc, pltpu.VMEM((3, SEQ, 128), F32)])


def in_bwd(dproj_parts, w_in_t, x, g1, dx2, after=()):
    tm = 512
    k = len(dproj_parts)

    def body(*refs):
        w_ref, x_ref, g_ref, dx2_ref, dx_ref, dg_ref = refs[k:]
        dh1 = _dot(refs[0][...], w_ref[0:512, :])
        for j in range(1, k):
            dh1 = dh1 + _dot(refs[j][...], w_ref[512 * j:512 * (j + 1), :])
        dz, dg = _rms_bwd(x_ref[...], g_ref[...], dh1)
        dx_ref[...] = dx2_ref[...] + dz

        @pl.when(pl.program_id(0) == 0)
        def _():
            dg_ref[...] = jnp.zeros_like(dg_ref)

        dg_ref[...] += dg

    return _call(
        "in_bwd", body, SEQ // tm,
        [_row_spec(tm, 512)] * k + [_weight_spec((IN_W, D_MODEL)), _row_spec(tm, D_MODEL), _full_spec((1, D_MODEL)),
                                    _row_spec(tm, D_MODEL)],
        [_row_spec(tm, D_MODEL), _full_spec((1, D_MODEL))],
        [jax.ShapeDtypeStruct((SEQ, D_MODEL), F32), jax.ShapeDtypeStruct((1, D_MODEL), F32)],
        (*dproj_parts, w_in_t, x, g1, dx2), after=after)


def _coords():
    return lax.axis_index("x"), lax.axis_index("y"), lax.axis_index("c")


class Exchange:
    def __init__(self, srcs, bufs, new_shapes, n_sems, make):
        self.srcs, self.bufs, self.new_shapes, self.n_sems, self.make = list(srcs), list(bufs), list(new_shapes), n_sems, make


def _call(name, body, n_steps, in_specs, out_specs, out_shape, args, scratch_shapes=(), after=()):
    n_in = len(args)

    def wrapped(*refs):
        body(*refs[:n_in], *refs[n_in + len(after):])

    return list(pl.pallas_call(
        wrapped, name=name, grid=(n_steps,), in_specs=list(in_specs) + [ANY] * len(after), out_specs=list(out_specs),
        out_shape=list(out_shape), scratch_shapes=list(scratch_shapes), compiler_params=_params(),
    )(*args, *after))


GATHER_SEMS = 8


def gather(bufs):
    n = len(bufs)

    def make(src_refs, buf_refs, new_refs, send_sems, recv_sems):
        x, y, c = _coords()
        me, sibling = (x, y, c), (x, y, 1 - c)
        over_x, over_y, across = (1 - x, y), (x, 1 - y), (1 - x, 1 - y)

        def copy(a, k, block, to, half=None):
            r = buf_refs[a].shape[0] // N_DEV
            lo, size = (0, r) if half is None else (half * (r // 2), r // 2)
            rows = buf_refs[a].at[pl.ds((4 * block[0] + 2 * block[1] + block[2]) * r + lo, size), :]
            return pltpu.make_async_remote_copy(
                src_ref=rows, dst_ref=rows, send_sem=send_sems.at[GATHER_SEMS * a + k],
                recv_sem=recv_sems.at[GATHER_SEMS * a + k], device_id=to, device_id_type=MESH)

        every = range(n)
        out = ([copy(a, 0, me, sibling) for a in every] + [copy(a, 1, me, (*over_x, c)) for a in every]
               + [copy(a, 2, me, (*over_y, c)) for a in every])
        near_in = [copy(a, 1, (*over_x, c), me) for a in every] + [copy(a, 2, (*over_y, c), me) for a in every]
        relay = ([copy(a, 3, (*over_x, c), (*over_y, c), half=0) for a in every]
                 + [copy(a, 4, (*over_y, c), (*over_x, c), half=1) for a in every])
        near_on = [copy(a, 5, (*over_x, c), sibling) for a in every] + [copy(a, 6, (*over_y, c), sibling) for a in every]
        relay_in = ([copy(a, 3, (*across, c), me, half=0) for a in every]
                    + [copy(a, 4, (*across, c), me, half=1) for a in every])
        far_on = [copy(a, 7, (*across, c), sibling) for a in every]
        from_core = ([copy(a, 0, sibling, me) for a in every] + [copy(a, 5, (*over_x, 1 - c), me) for a in every]
                     + [copy(a, 6, (*over_y, 1 - c), me) for a in every] + [copy(a, 7, (*across, 1 - c), me) for a in every])
        stages = [([], out), (near_in, relay + near_on), (relay_in, far_on)]
        return stages, out + relay + near_on + far_on, from_core

    return Exchange([], bufs, [], GATHER_SEMS * n, make)


TO_GATHER = (1, lambda x, y, c: [(x, y, 1 - c), (1 - x, y, c), (x, 1 - y, c)])
TO_SIBLING = (2, lambda x, y, c: [(x, y, 1 - c)])
TO_CHIPS = (3, lambda x, y, c: [(1 - x, y, c), (x, 1 - y, c), (1 - x, 1 - y, c)])
TO_ALL = (4, lambda x, y, c: [(x ^ (m >> 2), y ^ ((m >> 1) & 1), c ^ (m & 1)) for m in range(1, N_DEV)])


def by_sequencer(name, exchanges, who):
    collective_id, peers_of = who
    hbm = pltpu.MemorySpace.HBM
    refs = [([jax.new_ref(a, memory_space=hbm) for a in ex.srcs], [jax.new_ref(a, memory_space=hbm) for a in ex.bufs],
             [jax.empty_ref(s, memory_space=hbm) for s in ex.new_shapes]) for ex in exchanges]
    sems = []
    for ex in exchanges:
        sems += [pltpu.SemaphoreType.DMA((ex.n_sems,)), pltpu.SemaphoreType.DMA((ex.n_sems,))]

    @pl.kernel(mesh=plsc.ScalarSubcoreMesh(axis_name="sequencer", num_cores=1), name=name, scratch_types=tuple(sems),
               compiler_params=pltpu.CompilerParams(collective_id=collective_id))
    def launch(*sem_refs):
        peers = peers_of(*_coords())
        barrier = pltpu.get_barrier_semaphore()
        for peer in peers:
            pl.semaphore_signal(barrier, inc=1, device_id=peer, device_id_type=MESH)
        pl.semaphore_wait(barrier, len(peers))

        made = [ex.make(*refs[k], sem_refs[2 * k], sem_refs[2 * k + 1]) for k, ex in enumerate(exchanges)]
        for stage in range(max(len(stages) for stages, _, _ in made)):
            for stages, _, _ in made:
                if stage < len(stages):
                    arrivals, starts = stages[stage]
                    for cp in arrivals:
                        cp.wait_recv()
                    for cp in starts:
                        cp.start()
        for _, sends, arrivals in made:
            for cp in arrivals:
                cp.wait_recv()
            for cp in sends:
                cp.wait_send()

    launch()
    return [([ref[...] for ref in bufs], [ref[...] for ref in news]) for _, bufs, news in refs]


def place_shards(name, shards, dev):
    n = len(shards)

    def body(dev_ref, *refs):
        for a in range(n):
            refs[n + a][...] = refs[a][...].astype(BF16)

    spec = pltpu.PrefetchScalarGridSpec(
        num_scalar_prefetch=1, grid=(1,),
        in_specs=[pl.BlockSpec(s.shape, lambda i, dev_ref: (0, 0)) for s in shards],
        out_specs=[pl.BlockSpec(s.shape, lambda i, dev_ref: (dev_ref[0], 0)) for s in shards])
    return pl.pallas_call(
        body, name=name, grid_spec=spec,
        out_shape=[jax.ShapeDtypeStruct((N_DEV * s.shape[0], s.shape[1]), BF16) for s in shards],
        compiler_params=_params(),
    )(dev, *shards)


def _swap(copies_of):
    def make(src_refs, buf_refs, new_refs, send_sems, recv_sems):
        copies = copies_of(src_refs, new_refs, send_sems, recv_sems)
        return [([], copies)], copies, copies

    return make


def to_sibling(grads):
    def copies_of(src_refs, new_refs, send_sems, recv_sems):
        x, y, c = _coords()
        return [pltpu.make_async_remote_copy(
            src_ref=src_refs[a].at[2 * xy + 1 - c], dst_ref=new_refs[a].at[xy], send_sem=send_sems.at[4 * a + xy],
            recv_sem=recv_sems.at[4 * a + xy], device_id=(x, y, 1 - c), device_id_type=MESH)
            for a in range(len(src_refs)) for xy in range(4)]

    return Exchange(grads, [], [jax.ShapeDtypeStruct((4,) + g.shape[1:], g.dtype) for g in grads], 4 * len(grads),
                    _swap(copies_of))


def to_chips(parts):
    def copies_of(src_refs, new_refs, send_sems, recv_sems):
        x, y, c = _coords()
        chips = [(1 - x, y), (x, 1 - y), (1 - x, 1 - y)]
        return [pltpu.make_async_remote_copy(
            src_ref=src_refs[a].at[2 * px + py], dst_ref=new_refs[a].at[2 * x + y], send_sem=send_sems.at[3 * a + j],
            recv_sem=recv_sems.at[3 * a + j], device_id=(px, py, c), device_id_type=MESH)
            for a in range(len(src_refs)) for j, (px, py) in enumerate(chips)]

    return Exchange(parts, [], [jax.ShapeDtypeStruct(p.shape, p.dtype) for p in parts], 3 * len(parts), _swap(copies_of))


def to_owners(grad):
    def copies_of(src_refs, new_refs, send_sems, recv_sems):
        x, y, c = _coords()
        copies = []
        for m in range(1, N_DEV):
            px, py, pc = x ^ (m >> 2), y ^ ((m >> 1) & 1), c ^ (m & 1)
            copies.append(pltpu.make_async_remote_copy(
                src_ref=src_refs[0].at[4 * px + 2 * py + pc], dst_ref=new_refs[0].at[4 * x + 2 * y + c],
                send_sem=send_sems.at[m - 1], recv_sem=recv_sems.at[m - 1], device_id=(px, py, pc), device_id_type=MESH))
        return copies

    return Exchange([grad], [], [jax.ShapeDtypeStruct(grad.shape, grad.dtype)], N_DEV - 1, _swap(copies_of))


def to_everyone(vec):
    def copies_of(src_refs, new_refs, send_sems, recv_sems):
        x, y, c = _coords()
        copies = []
        for m in range(1, N_DEV):
            px, py, pc = x ^ (m >> 2), y ^ ((m >> 1) & 1), c ^ (m & 1)
            copies.append(pltpu.make_async_remote_copy(
                src_ref=src_refs[0], dst_ref=new_refs[0].at[4 * x + 2 * y + c],
                send_sem=send_sems.at[m - 1], recv_sem=recv_sems.at[m - 1], device_id=(px, py, pc), device_id_type=MESH))
        return copies

    return Exchange([vec], [], [jax.ShapeDtypeStruct((N_DEV,) + vec.shape, vec.dtype)], N_DEV - 1, _swap(copies_of))


def sum_cores(name, grad, other, core, after=()):
    _, r, w = other.shape

    def body(core_ref, g_ref, o_ref, *rest):
        rest[-1][...] = (g_ref[...].astype(F32) + o_ref[...].astype(F32)).astype(rest[-1].dtype)

    return pl.pallas_call(
        body, name=name,
        grid_spec=pltpu.PrefetchScalarGridSpec(
            num_scalar_prefetch=1, grid=(4,),
            in_specs=[pl.BlockSpec((1, r, w), lambda i, core_ref: (2 * i + core_ref[0], 0, 0)),
                      pl.BlockSpec((1, r, w), lambda i, core_ref: (i, 0, 0))] + [ANY] * len(after),
            out_specs=pl.BlockSpec((1, r, w), lambda i, core_ref: (i, 0, 0))),
        out_shape=jax.ShapeDtypeStruct(other.shape, other.dtype),
        compiler_params=_params(),
    )(core, grad, other, *after)


def sum_owned(name, grad, others, dev_ids, after=()):
    _, r, w = grad.shape

    def body(ids_ref, *refs):
        acc = refs[0][0]
        for k in range(1, N_DEV):
            acc = acc + refs[k][0]
        refs[-1][...] = acc

    def pick(k):
        return pl.BlockSpec((1, r, w), lambda i, ids_ref: (ids_ref[k], 0, 0))

    return pl.pallas_call(
        body, name=name,
        grid_spec=pltpu.PrefetchScalarGridSpec(
            num_scalar_prefetch=1, grid=(1,), in_specs=[pick(k) for k in range(N_DEV)] + [ANY] * len(after),
            out_specs=pl.BlockSpec((r, w), lambda i, ids_ref: (ids_ref[0], 0))),
        out_shape=jax.ShapeDtypeStruct((N_DEV * r, w), F32),
        compiler_params=_params(),
    )(dev_ids, grad, *([others] * (N_DEV - 1)), *after)


def _adamw_update(w, g, m, v):
    nm = ADAM_B1 * m + np.float32(1.0 - ADAM_B1) * g
    nv = ADAM_B2 * v + np.float32(1.0 - ADAM_B2) * (g * g)
    m_hat = nm / np.float32(1.0 - ADAM_B1 ** ADAM_STEP)
    v_hat = nv / np.float32(1.0 - ADAM_B2 ** ADAM_STEP)
    return -ADAM_LR * (m_hat / (jnp.sqrt(v_hat) + ADAM_EPS) + ADAM_WD * w), nm, nv


def adamw_of_sums(name, part, others, chip_ids, w, m, v, after):
    _, r, wd = part.shape
    halves = 2
    rows = r // halves

    def body(ids_ref, p_ref, a_ref, b_ref, c_ref, w_ref, m_ref, v_ref, after_ref, g_ref, d_ref, nm_ref, nv_ref):
        g = ((p_ref[0].astype(F32) + a_ref[0].astype(F32)) + b_ref[0].astype(F32)) + c_ref[0].astype(F32)
        g_ref[...] = g
        d_ref[...], nm_ref[...], nv_ref[...] = _adamw_update(w_ref[...], g, m_ref[...], v_ref[...])

    def pick(k):
        return pl.BlockSpec((1, rows, wd), lambda i, ids_ref: (ids_ref[k], i, 0))

    whole = pl.BlockSpec((rows, wd), lambda i, ids_ref: (i, 0))
    shape = jax.ShapeDtypeStruct((r, wd), F32)
    return pl.pallas_call(
        body, name=name,
        grid_spec=pltpu.PrefetchScalarGridSpec(
            num_scalar_prefetch=1, grid=(halves,), in_specs=[pick(0), pick(1), pick(2), pick(3), whole, whole, whole, ANY],
            out_specs=[whole] * 4),
        out_shape=[shape] * 4,
        compiler_params=_params(),
    )(chip_ids, part, others, others, others, w, m, v, after)


def pack_small(parts):
    names = [name for name, _ in SMALL if name in parts]
    operands = [parts[name] for name in names]
    first_row, at = {}, 0
    for name, size in SMALL:
        first_row[name] = at // 128
        at += size
    sizes = dict(SMALL)

    def body(*refs):
        out_ref = refs[-1]
        out_ref[...] = jnp.zeros_like(out_ref)
        for name, ref in zip(names, refs):
            row = first_row[name]
            if name == "loss_sum":
                lane0 = lax.broadcasted_iota(jnp.int32, (1, 128), 1) == 0
                out_ref[row:row + 1, :] = jnp.where(lane0, ref[...], 0.0)
            else:
                rows = sizes[name] // 128
                out_ref[row:row + rows, :] = ref[...].reshape(rows, 128)

    vmem = pl.BlockSpec(memory_space=pltpu.VMEM)
    return pl.pallas_call(
        body, name="pack_small", in_specs=[vmem] * len(names), out_specs=vmem,
        out_shape=jax.ShapeDtypeStruct((SMALL_ROWS, 128), F32), compiler_params=_params(()),
    )(*operands)


LATE = "pre_mix_norm"


def adamw_small(packed_g, late_parts, ws, ms, vs):
    names = [name for name, _ in SMALL if name != "loss_sum"]
    k = len(names)
    shapes = [ws[name].shape[1:] if ws[name].ndim > 2 else ws[name].shape for name in names]
    first_row, at = [], 0
    for name, size in SMALL:
        first_row.append(at // 128)
        at += size

    def body(g_ref, late_ref, *refs):
        w_refs, m_refs, v_refs, outs = refs[:k], refs[k:2 * k], refs[2 * k:3 * k], refs[3 * k:]
        for i, (name, size) in enumerate(SMALL[:k]):
            if name == LATE:
                g = late_ref[0]
                for j in range(1, N_DEV):
                    g = g + late_ref[j]
            else:
                g = g_ref[first_row[i]:first_row[i] + size // 128, :].reshape(shapes[i])
            outs[i][...] = g
            outs[k + i][...], outs[2 * k + i][...], outs[3 * k + i][...] = _adamw_update(
                w_refs[i][...], g, m_refs[i][...], v_refs[i][...])
        outs[4 * k][...] = g_ref[first_row[k]:first_row[k] + 1, 0:1]

    vmem = pl.BlockSpec(memory_space=pltpu.VMEM)
    operands = [t[name].reshape(shape) for t in (ws, ms, vs) for name, shape in zip(names, shapes)]
    outs = pl.pallas_call(
        body, name="adamw_small", in_specs=[vmem] * (2 + 3 * k), out_specs=[vmem] * (4 * k + 1),
        out_shape=[jax.ShapeDtypeStruct(shape, F32) for _ in range(4) for shape in shapes] + [jax.ShapeDtypeStruct((1, 1), F32)],
        compiler_params=_params(()),
    )(packed_g, late_parts, *operands)
    tables = [{name: outs[j * k + i].reshape(ws[name].shape) for i, name in enumerate(names)} for j in range(4)]
    return (*tables, outs[4 * k])


def kernel(x, positions, pre_mix_norm, w_in, sgu_ln_gain, sgu_ln_bias, sgu_w_spatial, sgu_b_spatial, attn_out_norm, sgu_out_norm, w_out, post_mix_norm, pre_ffn_norm, w_gate, w_up, w_down, post_ffn_norm, loss_target, m_pre_mix_norm, m_w_in, m_sgu_ln_gain, m_sgu_ln_bias, m_sgu_w_spatial, m_sgu_b_spatial, m_attn_out_norm, m_sgu_out_norm, m_w_out, m_post_mix_norm, m_pre_ffn_norm, m_w_gate, m_w_up, m_w_down, m_post_ffn_norm, v_pre_mix_norm, v_w_in, v_sgu_ln_gain, v_sgu_ln_bias, v_sgu_w_spatial, v_sgu_b_spatial, v_attn_out_norm, v_sgu_out_norm, v_w_out, v_post_mix_norm, v_pre_ffn_norm, v_w_gate, v_w_up, v_w_down, v_post_ffn_norm):
    small_w = dict(pre_mix_norm=pre_mix_norm, sgu_ln_gain=sgu_ln_gain, sgu_ln_bias=sgu_ln_bias, sgu_w_spatial=sgu_w_spatial,
                   sgu_b_spatial=sgu_b_spatial, attn_out_norm=attn_out_norm, sgu_out_norm=sgu_out_norm,
                   post_mix_norm=post_mix_norm, pre_ffn_norm=pre_ffn_norm, post_ffn_norm=post_ffn_norm)
    small_m = dict(pre_mix_norm=m_pre_mix_norm, sgu_ln_gain=m_sgu_ln_gain, sgu_ln_bias=m_sgu_ln_bias, sgu_w_spatial=m_sgu_w_spatial,
                   sgu_b_spatial=m_sgu_b_spatial, attn_out_norm=m_attn_out_norm, sgu_out_norm=m_sgu_out_norm,
                   post_mix_norm=m_post_mix_norm, pre_ffn_norm=m_pre_ffn_norm, post_ffn_norm=m_post_ffn_norm)
    small_v = dict(pre_mix_norm=v_pre_mix_norm, sgu_ln_gain=v_sgu_ln_gain, sgu_ln_bias=v_sgu_ln_bias, sgu_w_spatial=v_sgu_w_spatial,
                   sgu_b_spatial=v_sgu_b_spatial, attn_out_norm=v_attn_out_norm, sgu_out_norm=v_sgu_out_norm,
                   post_mix_norm=v_post_mix_norm, pre_ffn_norm=v_pre_ffn_norm, post_ffn_norm=v_post_ffn_norm)

    x2d = x[0]
    target = loss_target[0]
    pos_col = positions.reshape(SEQ, 1)
    rot = _rot_consts()
    w_sp = sgu_w_spatial[0]
    bfull = jnp.repeat(sgu_b_spatial[0].T, HEAD_DIM, axis=1)

    x_i, y_i, c_i = (lax.axis_index(a).astype(jnp.int32) for a in MESH_AXES)
    dev = 4 * x_i + 2 * y_i + c_i
    core = c_i.reshape(1)
    chip = 2 * x_i + y_i
    chip_ids = jnp.stack([chip, chip ^ 1, chip ^ 2, chip ^ 3])
    dev_ids = jnp.stack([dev ^ m for m in range(N_DEV)])

    def gathered(name, bufs):
        return by_sequencer(name, [gather(bufs)], TO_GATHER)[0][0]

    def from_sibling(name, grads):
        return by_sequencer(name, [to_sibling(grads)], TO_SIBLING)[0][1]

    def from_chips(name, parts):
        return by_sequencer(name, [to_chips(parts)], TO_CHIPS)[0][1]

    (w_in_t,) = place_shards("place_w_in", [w_in[0].T], dev.reshape(1))
    (w_in_t,) = gathered("gather_w_in", [w_in_t])
    w_gate_t, w_up_t, w_out_f, w_down_f = place_shards(
        "place_weights", [w_gate[0].T, w_up[0].T, w_out[0], w_down[0]], dev.reshape(1))
    (w_out_f,) = gathered("gather_w_out", [w_out_f])
    w_gate_t, w_up_t = gathered("gather_w_gate_up", [w_gate_t, w_up_t])
    (w_down_f,) = gathered("gather_w_down", [w_down_f])

    h1, u, vs, q, k, v = in_proj(x2d, pos_col, pre_mix_norm, w_in_t, rot)
    attn_r, lse, attn = attn_fwd(q, k, v)
    (sgu,) = sgu_fwd(u, vs, sgu_ln_gain, sgu_ln_bias, w_sp, bfull)
    mix, y, x2, h2 = out_proj(attn, sgu, x2d, attn_out_norm, sgu_out_norm, w_out_f, post_mix_norm, pre_ffn_norm)
    gate, up, act = ffn_up(h2, w_gate_t, w_up_t)
    df, dx3, d_post_ffn, sq_err = ffn_down_loss(act, w_down_f, x2, post_ffn_norm, target)

    g_w_down = weight_grad("grad_w_down", act, df)
    (s_down,) = from_sibling("w_down_to_sibling", [g_w_down])
    dgate, dup, dx2, dy, d_pre_ffn, d_post_mix = ffn_bwd(
        df, w_down_f, gate, up, w_gate_t, w_up_t, x2, pre_ffn_norm, dx3, y, post_mix_norm, after=[g_w_down])
    p_down = sum_cores("sum_cores_down", g_w_down, s_down, core, after=[dy])
    (c_down,) = from_chips("w_down_to_chips", [p_down])
    g_w_gate, g_w_up = weight_grads("grad_w_gate_up", [dgate, dup], h2, after=[p_down])
    s_gate, s_up = from_sibling("w_gate_up_to_sibling", [g_w_gate, g_w_up])
    g_w_out = weight_grad("grad_w_out", mix, dy, after=[g_w_up, c_down])
    p_gate = sum_cores("sum_cores_gate", g_w_gate, s_gate, core, after=[g_w_out])
    p_up = sum_cores("sum_cores_up", g_w_up, s_up, core, after=[g_w_out])
    c_gate, c_up = from_chips("w_gate_up_to_chips", [p_gate, p_up])
    (s_out,) = from_sibling("w_out_to_sibling", [g_w_out])
    dsgu, d_attn_out, d_sgu_out, dattn_r = mix_bwd(dy, w_out_f, attn, sgu, attn_out_norm, sgu_out_norm, after=[p_gate, p_up])
    du, dvs, d_ln_gain, d_ln_bias, d_w_sp, d_bfull = sgu_bwd(u, vs, dsgu, sgu_ln_gain, sgu_ln_bias, w_sp, bfull)
    dq, dk, dv = attn_bwd(q, k, v, attn_r, lse, dattn_r, _to_residue_order(pos_col), rot)
    p_out = sum_cores("sum_cores_out", g_w_out, s_out, core, after=[dq])
    (c_out,) = from_chips("w_out_to_chips", [p_out])
    dq, dk, dv = (_from_residue_order(t) for t in (dq, dk, dv))
    dproj = [dq, dk, dv, du, dvs]
    g_w_in = weight_grad_of_parts("grad_w_in", dproj, h1, after=[c_gate, c_up])
    (s_in,) = from_sibling("w_in_to_sibling", [g_w_in])

    d_b_sp = d_bfull.reshape(CHUNK, N_GROUPS, HEAD_DIM).sum(axis=-1).T
    small_g = pack_small(dict(sgu_ln_gain=d_ln_gain, sgu_ln_bias=d_ln_bias, sgu_w_spatial=d_w_sp, sgu_b_spatial=d_b_sp,
                              attn_out_norm=d_attn_out, sgu_out_norm=d_sgu_out, post_mix_norm=d_post_mix,
                              pre_ffn_norm=d_pre_ffn, post_ffn_norm=d_post_ffn, loss_sum=sq_err))
    small_g = small_g.reshape(N_DEV, SMALL_ROWS // N_DEV, 128)
    ((_, (o_small,)),) = by_sequencer("small_to_owners", [to_owners(small_g)], TO_ALL)
    summed_small = sum_owned("sum_small", small_g, o_small, dev_ids, after=[g_w_in])
    (all_small,) = gathered("gather_small_grads", [summed_small])

    grad_x, d_pre_mix = in_bwd(dproj, w_in_t, x2d, pre_mix_norm, dx2, after=[g_w_in, c_out, summed_small])
    p_in = sum_cores("sum_cores_in", g_w_in, s_in, core, after=[d_pre_mix, all_small])
    (_, (c_in,)), (_, (late_parts,)) = by_sequencer(
        "last_sums_to_owners", [to_chips([p_in]), to_everyone(d_pre_mix)], TO_ALL)
    late_parts = lax.dynamic_update_slice(late_parts, d_pre_mix[None], (dev, 0, 0))

    big, last = {}, p_in
    for name, w, p, c, m, vv, transposed in (
            ("w_down", w_down, p_down, c_down, m_w_down, v_w_down, False), ("w_gate", w_gate, p_gate, c_gate, m_w_gate, v_w_gate, True),
            ("w_up", w_up, p_up, c_up, m_w_up, v_w_up, True), ("w_out", w_out, p_out, c_out, m_w_out, v_w_out, False),
            ("w_in", w_in, p_in, c_in, m_w_in, v_w_in, True)):
        turn = (lambda t: t.T) if transposed else (lambda t: t)
        outs = adamw_of_sums("adamw_" + name, p, c, chip_ids, turn(w[0]), turn(m[0]), turn(vv[0]), last)
        big[name], last = tuple(turn(t)[None] for t in outs), outs[0]
    sg, sd, snm, snv, loss_sum = adamw_small(all_small, late_parts, small_w, small_m, small_v)
    loss = loss_sum[0, 0] * np.float32(0.5 / D_MODEL)

    names = ["pre_mix_norm", "w_in", "sgu_ln_gain", "sgu_ln_bias", "sgu_w_spatial", "sgu_b_spatial", "attn_out_norm",
             "sgu_out_norm", "w_out", "post_mix_norm", "pre_ffn_norm", "w_gate", "w_up", "w_down", "post_ffn_norm"]
    outs = [loss, grad_x[None]]
    for i, table in enumerate((sg, sd, snm, snv)):
        for name in names:
            outs.append(big[name][i] if name in big else table[name])
    return tuple(outs)
```

```python
import numpy as np
import jax
import jax.numpy as jnp
from jax import lax
from jax.experimental import pallas as pl
from jax.experimental.pallas import tpu as pltpu
from jax.experimental.pallas import tpu_sc as plsc

F32 = jnp.float32
BF16 = jnp.bfloat16

SEQ = 2048
D_MODEL = 1024
ATTN_W = 512
SGU_W = 512
HEAD_DIM = 64
N_GROUPS = 8
CHUNK = 128
D_FF = 2816
IN_W = 3 * ATTN_W + 2 * SGU_W
DILATIONS = (1, 4, 16)
ROPE_THETA = 500000.0
ROT_DIM = 16
ROT_HALF = 8
RMS_EPS = 1e-6
LN_EPS = 1e-5
Q_SCALE = 0.125
NEG = -1e30

N_DEV = 8
MESH_AXES = ("x", "y", "c")
MESH = pl.DeviceIdType.MESH

ADAM_LR = 0.001
ADAM_B1 = 0.9
ADAM_B2 = 0.999
ADAM_EPS = 1e-08
ADAM_WD = 0.01
ADAM_STEP = 10

VMEM_LIMIT = 60 * 1024 * 1024
ANY = pl.BlockSpec(memory_space=pl.ANY)

SMALL = (("pre_mix_norm", 1024), ("sgu_ln_gain", 512), ("sgu_ln_bias", 512), ("sgu_w_spatial", 8 * 128 * 128),
         ("sgu_b_spatial", 1024), ("attn_out_norm", 512), ("sgu_out_norm", 512), ("post_mix_norm", 1024),
         ("pre_ffn_norm", 1024), ("post_ffn_norm", 1024), ("loss_sum", 1))
SMALL_ROWS = 1152


def _params(sem=("arbitrary",)):
    return pltpu.CompilerParams(dimension_semantics=sem, vmem_limit_bytes=VMEM_LIMIT)


def _dot(a, b):
    return jnp.dot(a, b, preferred_element_type=F32)


def _dot_nt(a, b):
    return lax.dot_general(a, b, (((1,), (1,)), ((), ())), preferred_element_type=F32)


def _dot_tn(a, b):
    return lax.dot_general(a, b, (((0,), (0,)), ((), ())), preferred_element_type=F32)


def _rms(z):
    return lax.rsqrt(jnp.mean(z * z, axis=-1, keepdims=True) + RMS_EPS)


def _rms_bwd(z, gain, d):
    r = _rms(z)
    n = z * r
    dn = d * gain
    dz = r * (dn - n * jnp.mean(dn * n, axis=-1, keepdims=True))
    return dz, jnp.sum(d * n, axis=0, keepdims=True)


def _gelu(z):
    return 0.5 * z * (1.0 + lax.erf(z * np.float32(1.0 / np.sqrt(2.0))))


def _gelu_grad(z):
    cdf = 0.5 * (1.0 + lax.erf(z * np.float32(1.0 / np.sqrt(2.0))))
    return cdf + z * jnp.exp(-0.5 * z * z) * np.float32(1.0 / np.sqrt(2.0 * np.pi))


def _rot_tables(pos_col, invf, ma, mb):
    ang = pos_col.astype(F32) * invf
    s = jnp.sin(ang)
    return jnp.cos(ang), s * ma, s * mb


def _rot(t, c, sa, sb):
    return t * c + pltpu.roll(t, 120, 1) * sa + pltpu.roll(t, 8, 1) * sb


def _rot_t(d, c, sa, sb):
    return d * c + pltpu.roll(d * sa, 8, 1) + pltpu.roll(d * sb, 120, 1)


def _rot_consts():
    lane = np.arange(128) % HEAD_DIM
    inv_freq = (np.float32(ROPE_THETA) ** (-np.arange(0, ROT_DIM, 2, dtype=np.float32) / np.float32(ROT_DIM))).astype(np.float32)
    invf = np.where(lane < ROT_DIM, inv_freq[lane % ROT_HALF], 0.0).astype(np.float32)
    ma = np.where(lane < ROT_HALF, -1.0, 0.0).astype(np.float32)
    mb = np.where((lane >= ROT_HALF) & (lane < ROT_DIM), 1.0, 0.0).astype(np.float32)
    return jnp.asarray(invf[None]), jnp.asarray(ma[None]), jnp.asarray(mb[None])


def _row_spec(tm, w):
    return pl.BlockSpec((tm, w), lambda i: (i, 0))


def _full_spec(shape):
    return pl.BlockSpec(shape, lambda i: (0,) * len(shape))


def _weight_spec(shape):
    return pl.BlockSpec(shape, lambda i: (0,) * len(shape), pipeline_mode=pl.Buffered(1))


RES = 16


def _residue_scratch(n_arrays, tm, width):
    return [pltpu.VMEM((2, n_arrays, tm // RES, RES, width), F32), pltpu.SemaphoreType.DMA((2, n_arrays, RES))]


def _to_residue_rows(tiles, outs, scratch, sems, tm, n_steps):
    i = pl.program_id(0)
    slot = i % 2
    per = tm // RES

    def copies(step, s):
        return [pltpu.make_async_copy(scratch.at[s, a, :, b, :],
                                      outs[a].at[pl.ds(pl.multiple_of(b * (SEQ // RES) + per * step, per), per), :],
                                      sems.at[s, a, b]) for a in range(len(outs)) for b in range(RES)]

    @pl.when(i >= 2)
    def _():
        for cp in copies(i - 2, slot):
            cp.wait()

    for a, tile in enumerate(tiles):
        scratch[slot, a] = tile.reshape(per, RES, tile.shape[-1])
    for cp in copies(i, slot):
        cp.start()

    @pl.when(i == n_steps - 1)
    def _():
        for cp in copies(i - 1, 1 - slot) + copies(i, slot):
            cp.wait()


def in_proj(x, pos_col, g1, w_in_t, rot):
    tm = 512
    n_steps = SEQ // tm

    def body(x_ref, pos_ref, g_ref, w_ref, invf_ref, ma_ref, mb_ref, h_ref, u_ref, vs_ref, q_ref, k_ref, v_ref, scratch, sems):
        xf = x_ref[...]
        h = (xf * _rms(xf) * g_ref[...]).astype(BF16)
        h_ref[...] = h
        proj = _dot_nt(h, w_ref[...])
        c, sa, sb = _rot_tables(pos_ref[...], invf_ref[...], ma_ref[...], mb_ref[...])
        slabs = range(ATTN_W // 128)
        q = jnp.concatenate([_rot(proj[:, j * 128:(j + 1) * 128], c, sa, sb) * Q_SCALE for j in slabs], axis=1)
        k = jnp.concatenate([_rot(proj[:, ATTN_W + j * 128:ATTN_W + (j + 1) * 128], c, sa, sb) for j in slabs], axis=1)
        u_ref[...] = proj[:, 3 * ATTN_W:3 * ATTN_W + SGU_W]
        vs_ref[...] = proj[:, 3 * ATTN_W + SGU_W:]
        _to_residue_rows([q, k, proj[:, 2 * ATTN_W:3 * ATTN_W]], [q_ref, k_ref, v_ref], scratch, sems, tm, n_steps)

    act = jax.ShapeDtypeStruct((SEQ, 512), F32)
    return _call(
        "in_proj", body, n_steps,
        [_row_spec(tm, D_MODEL), _row_spec(tm, 1), _full_spec((1, D_MODEL)), _weight_spec((IN_W, D_MODEL)),
         _full_spec((1, 128)), _full_spec((1, 128)), _full_spec((1, 128))],
        [_row_spec(tm, D_MODEL)] + [_row_spec(tm, 512)] * 2 + [ANY] * 3,
        [jax.ShapeDtypeStruct((SEQ, D_MODEL), BF16)] + [act] * 5,
        (x, pos_col, g1, w_in_t, *rot), scratch_shapes=_residue_scratch(3, tm, ATTN_W))


def _to_residue_order(t):
    return t.reshape(SEQ // RES, RES, -1).transpose(1, 0, 2).reshape(t.shape)


def _from_residue_order(t):
    return t.reshape(RES, SEQ // RES, -1).transpose(1, 0, 2).reshape(t.shape)


def _block_rows(d, r, n):
    if d == 16:
        slices = [(128 * r, 128)]
    elif d == 4:
        slices = [(128 * (4 * b + r) + 32 * n, 32) for b in range(4)]
    else:
        slices = [(128 * b + 8 * n, 8) for b in range(RES)]
    return [(s if isinstance(s, int) else pl.multiple_of(s, z), z) for s, z in slices]


def _block_step(d, i):
    if d == 16:
        return i
    if d == 4:
        return 4 * (i & 31) + (i >> 5)
    return 16 * (i & 7) + (i >> 3)


def _attn_masks(d):
    row2 = _block_step(d, lax.broadcasted_iota(jnp.int32, (128, 256), 0))
    col2 = lax.broadcasted_iota(jnp.int32, (128, 256), 1)
    key2 = _block_step(d, col2 & 127)
    mask2 = jnp.logical_or(jnp.logical_and(col2 < 128, key2 >= row2), jnp.logical_and(col2 >= 128, key2 <= row2))
    row1 = _block_step(d, lax.broadcasted_iota(jnp.int32, (128, 128), 0))
    col1 = lax.broadcasted_iota(jnp.int32, (128, 128), 1)
    return col1 < HEAD_DIM, _block_step(d, col1) <= row1, mask2


def _load_rows(ref, slices):
    parts = [ref[pl.ds(s, z), :] for s, z in slices]
    return parts[0] if len(parts) == 1 else jnp.concatenate(parts, axis=0)


def _for_each_group(fn):
    for p, d in enumerate(DILATIONS):
        masks = _attn_masks(d)
        if d == 16:
            def group(i, carry, p=p, masks=masks):
                fn(p, masks, [(_block_rows(16, 8 * i + g, 0), None) for g in range(8)])
                return carry

            lax.fori_loop(0, 2, group, 0)
        elif d == 4:
            fn(p, masks, [(_block_rows(4, r, 0), None) for r in range(4)])

            def group(i, carry, p=p, masks=masks):
                blocks = [6 * i + g for g in range(6)]
                fn(p, masks, [(_block_rows(4, j % 4, 1 + j // 4), _block_rows(4, j % 4, j // 4)) for j in blocks])
                return carry

            lax.fori_loop(0, 2, group, 0)
        else:
            fn(p, masks, [(_block_rows(1, 0, 0), None)])

            def group(i, carry, p=p, masks=masks):
                fn(p, masks, [(_block_rows(1, 0, 5 * i + g + 1), _block_rows(1, 0, 5 * i + g)) for g in range(5)])
                return carry

            lax.fori_loop(0, 3, group, 0)


def attn_fwd(q, k, v):
    def body(q_ref, k_ref, v_ref, o_ref, lse_ref, nat_ref, op_ref, lp_ref, sems):
        def group(p, masks, blocks):
            head0, mask1, mask2 = masks
            heads = (head0, jnp.logical_not(head0))
            keys = [rows if prev is None else prev + rows for rows, prev in blocks]
            mask = [mask1 if prev is None else mask2 for _, prev in blocks]
            qb = [_load_rows(q_ref, rows) for rows, _ in blocks]
            kk = [_load_rows(k_ref, ks).astype(BF16) for ks in keys]
            vv = [_load_rows(v_ref, ks).astype(BF16) for ks in keys]
            chains = [(g, hm) for g in range(len(blocks)) for hm in heads]
            s = [jnp.where(mask[g], _dot_nt(jnp.where(hm, qb[g], 0.0).astype(BF16), kk[g]), NEG) for g, hm in chains]
            m = [jnp.max(t, axis=-1, keepdims=True) for t in s]
            e = [jnp.exp(t - mt) for t, mt in zip(s, m)]
            l = [jnp.sum(t, axis=-1, keepdims=True) for t in e]
            pv = [_dot(t.astype(BF16), vv[g]) for t, (g, _) in zip(e, chains)]
            for g, (rows, _) in enumerate(blocks):
                o_blk = jnp.where(head0, pv[2 * g] / l[2 * g], pv[2 * g + 1] / l[2 * g + 1])
                l_blk = jnp.where(head0, jnp.broadcast_to(m[2 * g] + jnp.log(l[2 * g]), (128, 128)),
                                  jnp.broadcast_to(m[2 * g + 1] + jnp.log(l[2 * g + 1]), (128, 128)))
                at = 0
                for start, size in rows:
                    op_ref[p, pl.ds(start, size), :] = o_blk[at:at + size]
                    lp_ref[p, pl.ds(start, size), :] = l_blk[at:at + size]
                    at += size

        _for_each_group(group)

        def combine(i, carry):
            rows = pl.ds(pl.multiple_of(i * 256, 256), 256)
            ls = [lp_ref[p, rows, :] for p in range(3)]
            m = jnp.maximum(jnp.maximum(ls[0], ls[1]), ls[2])
            lse = m + jnp.log(jnp.exp(ls[0] - m) + jnp.exp(ls[1] - m) + jnp.exp(ls[2] - m))
            o = jnp.zeros((256, 128), F32)
            for p in range(3):
                o = o + jnp.exp(ls[p] - lse) * op_ref[p, rows, :]
            o_ref[rows, :] = o
            lse_ref[rows, :] = lse
            return carry

        lax.fori_loop(0, SEQ // 256, combine, 0)

        lanes = pl.ds(pl.multiple_of(pl.program_id(0) * 128, 128), 128)
        back = [pltpu.make_async_copy(o_ref.at[pl.ds(b * (SEQ // RES), SEQ // RES), :], nat_ref.at[:, b, lanes], sems.at[b])
                for b in range(RES)]
        for cp in back:
            cp.start()
        for cp in back:
            cp.wait()

    slab = pl.BlockSpec((SEQ, 128), lambda i: (0, i))
    out = jax.ShapeDtypeStruct((SEQ, ATTN_W), F32)
    attn_r, lse, attn = _call(
        "attn_fwd", body, ATTN_W // 128, [slab] * 3, [slab] * 2 + [ANY],
        [out, out, jax.ShapeDtypeStruct((SEQ // RES, RES, ATTN_W), F32)], (q, k, v),
        scratch_shapes=[pltpu.VMEM((3, SEQ, 128), F32), pltpu.VMEM((3, SEQ, 128), F32), pltpu.SemaphoreType.DMA((RES,))])
    return attn_r, lse, attn.reshape(SEQ, ATTN_W)


def _causal_weights(w_ref):
    row = lax.broadcasted_iota(jnp.int32, (CHUNK, CHUNK), 0)
    col = lax.broadcasted_iota(jnp.int32, (CHUNK, CHUNK), 1)
    return [jnp.where(col <= row, w_ref[g], 0.0).astype(BF16) for g in range(N_GROUPS)], col <= row


def _sgu_chunk_fwd(u, vs, lg, lb, wc, bfull, head0):
    ug = _gelu(u)
    vg = _gelu(vs)
    xc = vg - jnp.mean(vg, axis=-1, keepdims=True)
    rstd = lax.rsqrt(jnp.mean(xc * xc, axis=-1, keepdims=True) + LN_EPS)
    xhat = xc * rstd
    vn = xhat * lg + lb
    mixed = []
    for gp in range(SGU_W // 128):
        vp = vn[:, gp * 128:(gp + 1) * 128].astype(BF16)
        mixed.append(jnp.where(head0, _dot(wc[2 * gp], vp), _dot(wc[2 * gp + 1], vp)))
    ms = jnp.concatenate(mixed, axis=1) + bfull
    return ug, xhat, rstd, vn, ms


def sgu_fwd(u, vs, lg, lb, w_sp, bfull):
    cpb = 4

    def body(u_ref, vs_ref, lg_ref, lb_ref, w_ref, b_ref, o_ref):
        wc, _ = _causal_weights(w_ref)
        head0 = lax.broadcasted_iota(jnp.int32, (CHUNK, 128), 1) < HEAD_DIM
        for ci in range(cpb):
            rows = pl.ds(ci * CHUNK, CHUNK)
            ug, _, _, _, ms = _sgu_chunk_fwd(u_ref[rows, :], vs_ref[rows, :], lg_ref[...], lb_ref[...], wc, b_ref[...], head0)
            o_ref[rows, :] = ug * ms

    tm = cpb * CHUNK
    return _call(
        "sgu_fwd", body, SEQ // tm,
        [_row_spec(tm, SGU_W), _row_spec(tm, SGU_W), _full_spec((1, SGU_W)), _full_spec((1, SGU_W)),
         _full_spec((N_GROUPS, CHUNK, CHUNK)), _full_spec((CHUNK, SGU_W))],
        [_row_spec(tm, SGU_W)], [jax.ShapeDtypeStruct((SEQ, SGU_W), F32)],
        (u, vs, lg, lb, w_sp, bfull))


def out_proj(attn, sgu, x, ga, gs, w_out, gpm, gpf):
    tm = 512

    def body(a_ref, s_ref, x_ref, ga_ref, gs_ref, w_ref, gpm_ref, gpf_ref, mix_ref, y_ref, x2_ref, h2_ref):
        a = a_ref[...]
        s = s_ref[...]
        an = (a * _rms(a) * ga_ref[...]).astype(BF16)
        sn = (s * _rms(s) * gs_ref[...]).astype(BF16)
        mix_ref[:, :ATTN_W] = an
        mix_ref[:, ATTN_W:] = sn
        y = _dot(an, w_ref[:ATTN_W, :]) + _dot(sn, w_ref[ATTN_W:, :])
        y_ref[...] = y
        x2 = x_ref[...] + y * _rms(y) * gpm_ref[...]
        x2_ref[...] = x2
        h2_ref[...] = (x2 * _rms(x2) * gpf_ref[...]).astype(BF16)

    wide = jax.ShapeDtypeStruct((SEQ, D_MODEL), F32)
    wide16 = jax.ShapeDtypeStruct((SEQ, D_MODEL), BF16)
    return _call(
        "out_proj", body, SEQ // tm,
        [_row_spec(tm, ATTN_W), _row_spec(tm, SGU_W), _row_spec(tm, D_MODEL), _full_spec((1, ATTN_W)),
         _full_spec((1, SGU_W)), _weight_spec((D_MODEL, D_MODEL)), _full_spec((1, D_MODEL)), _full_spec((1, D_MODEL))],
        [_row_spec(tm, D_MODEL)] * 4, [wide16, wide, wide, wide16],
        (attn, sgu, x, ga, gs, w_out, gpm, gpf))


def ffn_up(h2, w_gate_t, w_up_t):
    tm = 256

    def body(h_ref, wg_ref, wu_ref, g_ref, u_ref, a_ref):
        h = h_ref[...]
        g = _dot_nt(h, wg_ref[...])
        u = _dot_nt(h, wu_ref[...])
        g_ref[...] = g.astype(BF16)
        u_ref[...] = u.astype(BF16)
        a_ref[...] = (g * jax.nn.sigmoid(g) * u).astype(BF16)

    ff = jax.ShapeDtypeStruct((SEQ, D_FF), BF16)
    return _call(
        "ffn_up", body, SEQ // tm,
        [_row_spec(tm, D_MODEL), _weight_spec((D_FF, D_MODEL)), _weight_spec((D_FF, D_MODEL))],
        [_row_spec(tm, D_FF)] * 3, [ff, ff, jax.ShapeDtypeStruct((SEQ, D_FF), BF16)],
        (h2, w_gate_t, w_up_t))


def ffn_down_loss(act, w_down, x2, gpo, target):
    tm = 512

    def body(a_ref, w_ref, x2_ref, g_ref, t_ref, df_ref, dx3_ref, dg_ref, loss_ref):
        f = _dot(a_ref[...], w_ref[...])
        gain = g_ref[...]
        err = x2_ref[...] + f * _rms(f) * gain - t_ref[...]
        dx3 = err * np.float32(1.0 / D_MODEL)
        dx3_ref[...] = dx3
        df, dg = _rms_bwd(f, gain, dx3)
        df_ref[...] = df.astype(BF16)

        @pl.when(pl.program_id(0) == 0)
        def _():
            dg_ref[...] = jnp.zeros_like(dg_ref)
            loss_ref[...] = jnp.zeros_like(loss_ref)

        dg_ref[...] += dg
        loss_ref[...] += jnp.sum(err * err, axis=(0, 1), keepdims=True)

    return pl.pallas_call(
        body, name="ffn_down_loss", grid=(SEQ // tm,),
        in_specs=[_row_spec(tm, D_FF), _weight_spec((D_FF, D_MODEL)), _row_spec(tm, D_MODEL), _full_spec((1, D_MODEL)),
                  _row_spec(tm, D_MODEL)],
        out_specs=[_row_spec(tm, D_MODEL), _row_spec(tm, D_MODEL), _full_spec((1, D_MODEL)), _full_spec((1, 1))],
        out_shape=[jax.ShapeDtypeStruct((SEQ, D_MODEL), BF16), jax.ShapeDtypeStruct((SEQ, D_MODEL), F32),
                   jax.ShapeDtypeStruct((1, D_MODEL), F32), jax.ShapeDtypeStruct((1, 1), F32)],
        compiler_params=_params(),
    )(act, w_down, x2, gpo, target)


def ffn_bwd(df, w_down, gate, up, w_gate_t, w_up_t, x2, gpf, dx3, y, gpm, after=()):
    tm = 256

    def body(df_ref, wd_ref, g_ref, u_ref, wg_ref, wu_ref, x2_ref, gpf_ref, dx3_ref, y_ref, gpm_ref,
             dg_ref, du_ref, dx2_ref, dy_ref, dgpf_ref, dgpm_ref):
        dact = _dot_nt(df_ref[...], wd_ref[...])
        g = g_ref[...].astype(F32)
        s = jax.nn.sigmoid(g)
        dup = (dact * g * s).astype(BF16)
        dgate = (dact * u_ref[...].astype(F32) * (s * (1.0 + g * (1.0 - s)))).astype(BF16)
        du_ref[...] = dup
        dg_ref[...] = dgate
        dh2 = _dot(dgate, wg_ref[...]) + _dot(dup, wu_ref[...])
        dz, dgpf = _rms_bwd(x2_ref[...], gpf_ref[...], dh2)
        dx2 = dx3_ref[...] + dz
        dx2_ref[...] = dx2
        dy, dgpm = _rms_bwd(y_ref[...], gpm_ref[...], dx2)
        dy_ref[...] = dy.astype(BF16)

        @pl.when(pl.program_id(0) == 0)
        def _():
            dgpf_ref[...] = jnp.zeros_like(dgpf_ref)
            dgpm_ref[...] = jnp.zeros_like(dgpm_ref)

        dgpf_ref[...] += dgpf
        dgpm_ref[...] += dgpm

    vec = jax.ShapeDtypeStruct((1, D_MODEL), F32)
    ff16 = jax.ShapeDtypeStruct((SEQ, D_FF), BF16)
    return _call(
        "ffn_bwd", body, SEQ // tm,
        [_row_spec(tm, D_MODEL), _weight_spec((D_FF, D_MODEL)), _row_spec(tm, D_FF), _row_spec(tm, D_FF),
         _weight_spec((D_FF, D_MODEL)), _weight_spec((D_FF, D_MODEL)), _row_spec(tm, D_MODEL), _full_spec((1, D_MODEL)),
         _row_spec(tm, D_MODEL), _row_spec(tm, D_MODEL), _full_spec((1, D_MODEL))],
        [_row_spec(tm, D_FF), _row_spec(tm, D_FF), _row_spec(tm, D_MODEL), _row_spec(tm, D_MODEL),
         _full_spec((1, D_MODEL)), _full_spec((1, D_MODEL))],
        [ff16, ff16, jax.ShapeDtypeStruct((SEQ, D_MODEL), F32), jax.ShapeDtypeStruct((SEQ, D_MODEL), BF16), vec, vec],
        (df, w_down, gate, up, w_gate_t, w_up_t, x2, gpf, dx3, y, gpm), after=after)


def weight_grads(name, lhs, b, after=()):
    m, n, k = lhs[0].shape[1], b.shape[1], len(lhs)
    tr = 256

    def body(*refs):
        for a_ref, o_ref in zip(refs[:k], refs[k + 1:]):
            o_ref[...] = _dot_tn(a_ref[...], refs[k][...]).astype(BF16)

    outs = _call(
        name, body, m // tr, [pl.BlockSpec((SEQ, tr), lambda i: (0, i))] * k + [_weight_spec((SEQ, n))],
        [_row_spec(tr, n)] * k, [jax.ShapeDtypeStruct((m, n), BF16)] * k, (*lhs, b), after=after)
    return [out.reshape(N_DEV, m // N_DEV, n) for out in outs]


def weight_grad(name, a, b, after=()):
    return weight_grads(name, [a], b, after)[0]


def weight_grad_of_parts(name, parts, b, after=()):
    p, n, k = parts[0].shape[1], b.shape[1], len(parts)
    tr = 256
    per = p // tr

    def body(*refs):
        tile = pl.program_id(0)
        for j in range(k):
            @pl.when(tile // per == j)
            def _(j=j):
                refs[k + 1][...] = _dot_tn(refs[j][...], refs[k][...]).astype(BF16)

    def part_spec(j):
        return pl.BlockSpec((SEQ, tr), lambda i: (0, jnp.clip(i - per * j, 0, per - 1)))

    (out,) = _call(
        name, body, k * per, [part_spec(j) for j in range(k)] + [_weight_spec((SEQ, n))],
        [_row_spec(tr, n)], [jax.ShapeDtypeStruct((k * p, n), BF16)], (*parts, b), after=after)
    return out.reshape(N_DEV, k * p // N_DEV, n)


def mix_bwd(dy, w_out, attn, sgu, ga, gs, after=()):
    tm = 512
    n_steps = SEQ // tm

    def body(dy_ref, w_ref, a_ref, s_ref, ga_ref, gs_ref, ds_ref, dga_ref, dgs_ref, da_ref, scratch, sems):
        dy = dy_ref[...]
        da, dga = _rms_bwd(a_ref[...], ga_ref[...], _dot_nt(dy, w_ref[:ATTN_W, :]))
        ds, dgs = _rms_bwd(s_ref[...], gs_ref[...], _dot_nt(dy, w_ref[ATTN_W:, :]))
        ds_ref[...] = ds
        _to_residue_rows([da], [da_ref], scratch, sems, tm, n_steps)

        @pl.when(pl.program_id(0) == 0)
        def _():
            dga_ref[...] = jnp.zeros_like(dga_ref)
            dgs_ref[...] = jnp.zeros_like(dgs_ref)

        dga_ref[...] += dga
        dgs_ref[...] += dgs

    half = jax.ShapeDtypeStruct((SEQ, 512), F32)
    vec = jax.ShapeDtypeStruct((1, 512), F32)
    return _call(
        "mix_bwd", body, n_steps,
        [_row_spec(tm, D_MODEL), _weight_spec((D_MODEL, D_MODEL)), _row_spec(tm, 512), _row_spec(tm, 512),
         _full_spec((1, 512)), _full_spec((1, 512))],
        [_row_spec(tm, 512), _full_spec((1, 512)), _full_spec((1, 512)), ANY],
        [half, vec, vec, half], (dy, w_out, attn, sgu, ga, gs), scratch_shapes=_residue_scratch(1, tm, ATTN_W), after=after)


def sgu_bwd(u, vs, dsgu, lg, lb, w_sp, bfull):
    cpb = 4

    def body(u_ref, vs_ref, d_ref, lg_ref, lb_ref, w_ref, b_ref, du_ref, dvs_ref, dlg_ref, dlb_ref, dw_ref, db_ref):
        wc, causal = _causal_weights(w_ref)
        head0 = lax.broadcasted_iota(jnp.int32, (CHUNK, 128), 1) < HEAD_DIM
        lg = lg_ref[...]

        @pl.when(pl.program_id(0) == 0)
        def _():
            dlg_ref[...] = jnp.zeros_like(dlg_ref)
            dlb_ref[...] = jnp.zeros_like(dlb_ref)
            dw_ref[...] = jnp.zeros_like(dw_ref)
            db_ref[...] = jnp.zeros_like(db_ref)

        for ci in range(cpb):
            rows = pl.ds(ci * CHUNK, CHUNK)
            u = u_ref[rows, :]
            vs = vs_ref[rows, :]
            d = d_ref[rows, :]
            ug, xhat, rstd, vn, ms = _sgu_chunk_fwd(u, vs, lg, lb_ref[...], wc, b_ref[...], head0)
            du_ref[rows, :] = (d * ms * _gelu_grad(u)).astype(BF16)
            dms = d * ug
            db_ref[...] += dms
            dvn = []
            for gp in range(SGU_W // 128):
                dmp = dms[:, gp * 128:(gp + 1) * 128]
                dm0 = jnp.where(head0, dmp, 0.0).astype(BF16)
                dm1 = jnp.where(head0, 0.0, dmp).astype(BF16)
                vp = vn[:, gp * 128:(gp + 1) * 128].astype(BF16)
                dw_ref[2 * gp] += _dot_nt(dm0, vp)
                dw_ref[2 * gp + 1] += _dot_nt(dm1, vp)
                dvn.append(_dot_tn(wc[2 * gp], dm0) + _dot_tn(wc[2 * gp + 1], dm1))
            dvn = jnp.concatenate(dvn, axis=1)
            dlg_ref[...] += jnp.sum(dvn * xhat, axis=0, keepdims=True)
            dlb_ref[...] += jnp.sum(dvn, axis=0, keepdims=True)
            dxh = dvn * lg
            dvg = rstd * (dxh - jnp.mean(dxh, axis=-1, keepdims=True) - xhat * jnp.mean(dxh * xhat, axis=-1, keepdims=True))
            dvs_ref[rows, :] = (dvg * _gelu_grad(vs)).astype(BF16)

        @pl.when(pl.program_id(0) == pl.num_programs(0) - 1)
        def _():
            for g in range(N_GROUPS):
                dw_ref[g] = jnp.where(causal, dw_ref[g], 0.0)

    tm = cpb * CHUNK
    half16 = jax.ShapeDtypeStruct((SEQ, SGU_W), BF16)
    vec = jax.ShapeDtypeStruct((1, SGU_W), F32)
    return _call(
        "sgu_bwd", body, SEQ // tm,
        [_row_spec(tm, SGU_W)] * 3 + [_full_spec((1, SGU_W)), _full_spec((1, SGU_W)),
                                      _full_spec((N_GROUPS, CHUNK, CHUNK)), _full_spec((CHUNK, SGU_W))],
        [_row_spec(tm, SGU_W), _row_spec(tm, SGU_W), _full_spec((1, SGU_W)), _full_spec((1, SGU_W)),
         _full_spec((N_GROUPS, CHUNK, CHUNK)), _full_spec((CHUNK, SGU_W))],
        [half16, half16, vec, vec, jax.ShapeDtypeStruct((N_GROUPS, CHUNK, CHUNK), F32),
         jax.ShapeDtypeStruct((CHUNK, SGU_W), F32)],
        (u, vs, dsgu, lg, lb, w_sp, bfull))


def attn_bwd(q, k, v, o, lse, do, pos_col, rot):
    def body(q_ref, k_ref, v_ref, o_ref, lse_ref, do_ref, pos_ref, invf_ref, ma_ref, mb_ref,
             dq_ref, dk_ref, dv_ref, dqa_ref, dka_ref, dva_ref, dlt_ref, rot_ref):
        dqa_ref[...] = jnp.zeros_like(dqa_ref)
        dka_ref[...] = jnp.zeros_like(dka_ref)
        dva_ref[...] = jnp.zeros_like(dva_ref)

        def delta(i, carry):
            rows = pl.ds(pl.multiple_of(i * 256, 256), 256)
            prod = do_ref[rows, :] * o_ref[rows, :]
            h0 = lax.broadcasted_iota(jnp.int32, (256, 128), 1) < HEAD_DIM
            d0 = jnp.sum(jnp.where(h0, prod, 0.0), axis=-1, keepdims=True)
            d1 = jnp.sum(jnp.where(h0, 0.0, prod), axis=-1, keepdims=True)
            dlt_ref[rows, :] = jnp.where(h0, d0, d1)
            return carry

        lax.fori_loop(0, SEQ // 256, delta, 0)

        def add_rows(ref, slices, val):
            at = 0
            for start, size in slices:
                ref[pl.ds(start, size), :] += val[at:at + size]
                at += size

        def group(p, masks, blocks):
            head0, mask1, mask2 = masks
            heads = (head0, jnp.logical_not(head0))
            keys = [rows if prev is None else prev + rows for rows, prev in blocks]
            mask = [mask1 if prev is None else mask2 for _, prev in blocks]
            kk = [_load_rows(k_ref, ks).astype(BF16) for ks in keys]
            vv = [_load_rows(v_ref, ks).astype(BF16) for ks in keys]
            qb = [_load_rows(q_ref, rows) for rows, _ in blocks]
            dob = [_load_rows(do_ref, rows) for rows, _ in blocks]
            lse_b = [_load_rows(lse_ref, rows) for rows, _ in blocks]
            dlt_b = [_load_rows(dlt_ref, rows) for rows, _ in blocks]
            chains = [(g, h) for g in range(len(blocks)) for h in range(2)]
            qm = [jnp.where(heads[h], qb[g], 0.0).astype(BF16) for g, h in chains]
            dom = [jnp.where(heads[h], dob[g], 0.0).astype(BF16) for g, h in chains]
            s = [_dot_nt(qm[c], kk[g]) for c, (g, h) in enumerate(chains)]
            dp = [_dot_nt(dom[c], vv[g]) for c, (g, h) in enumerate(chains)]
            pr = [jnp.where(mask[g], jnp.exp(s[c] - lse_b[g][:, h * HEAD_DIM:h * HEAD_DIM + 1]), 0.0)
                  for c, (g, h) in enumerate(chains)]
            ds = [(pr[c] * (dp[c] - dlt_b[g][:, h * HEAD_DIM:h * HEAD_DIM + 1])).astype(BF16)
                  for c, (g, h) in enumerate(chains)]
            dv = [_dot_tn(pr[c].astype(BF16), dom[c]) for c in range(len(chains))]
            dk = [_dot_tn(ds[c], qm[c]) for c in range(len(chains))]
            dq = [_dot(ds[c], kk[g]) for c, (g, h) in enumerate(chains)]
            for g, (rows, _) in enumerate(blocks):
                add_rows(dqa_ref, rows, jnp.where(head0, dq[2 * g], dq[2 * g + 1]))
                add_rows(dka_ref, keys[g], dk[2 * g] + dk[2 * g + 1])
                add_rows(dva_ref, keys[g], dv[2 * g] + dv[2 * g + 1])

        _for_each_group(group)

        @pl.when(pl.program_id(0) == 0)
        def _():
            def tables(i, carry):
                rows = pl.ds(pl.multiple_of(i * 256, 256), 256)
                c, sa, sb = _rot_tables(pos_ref[rows, :], invf_ref[...], ma_ref[...], mb_ref[...])
                rot_ref[0, rows, :] = c
                rot_ref[1, rows, :] = sa
                rot_ref[2, rows, :] = sb
                return carry

            lax.fori_loop(0, SEQ // 256, tables, 0)

        def finish(i, carry):
            rows = pl.ds(pl.multiple_of(i * 256, 256), 256)
            c, sa, sb = rot_ref[0, rows, :], rot_ref[1, rows, :], rot_ref[2, rows, :]
            dq_ref[rows, :] = _rot_t(dqa_ref[rows, :] * Q_SCALE, c, sa, sb).astype(BF16)
            dk_ref[rows, :] = _rot_t(dka_ref[rows, :], c, sa, sb).astype(BF16)
            dv_ref[rows, :] = dva_ref[rows, :].astype(BF16)
            return carry

        lax.fori_loop(0, SEQ // 256, finish, 0)

    slab = pl.BlockSpec((SEQ, 128), lambda i: (0, i))
    out = jax.ShapeDtypeStruct((SEQ, ATTN_W), BF16)
    acc = pltpu.VMEM((SEQ, 128), F32)
    return _call(
        "attn_bwd", body, ATTN_W // 128,
        [slab] * 6 + [_full_spec((SEQ, 1)), _full_spec((1, 128)), _full_spec((1, 128)), _full_spec((1, 128))],
        [slab] * 3, [out, out, out], (q, k, v, o, lse, do, pos_col, *rot),
        scratch_shapes=[acc, acc, acc, acc, pltpu.VMEM((3, SEQ, 128), F32)])


def in_bwd(dproj_parts, w_in_t, x, g1, dx2, after=()):
    tm = 512
    k = len(dproj_parts)

    def body(*refs):
        w_ref, x_ref, g_ref, dx2_ref, dx_ref, dg_ref = refs[k:]
        dh1 = _dot(refs[0][...], w_ref[0:512, :])
        for j in range(1, k):
            dh1 = dh1 + _dot(refs[j][...], w_ref[512 * j:512 * (j + 1), :])
        dz, dg = _rms_bwd(x_ref[...], g_ref[...], dh1)
        dx_ref[...] = dx2_ref[...] + dz

        @pl.when(pl.program_id(0) == 0)
        def _():
            dg_ref[...] = jnp.zeros_like(dg_ref)

        dg_ref[...] += dg

    return _call(
        "in_bwd", body, SEQ // tm,
        [_row_spec(tm, 512)] * k + [_weight_spec((IN_W, D_MODEL)), _row_spec(tm, D_MODEL), _full_spec((1, D_MODEL)),
                                    _row_spec(tm, D_MODEL)],
        [_row_spec(tm, D_MODEL), _full_spec((1, D_MODEL))],
        [jax.ShapeDtypeStruct((SEQ, D_MODEL), F32), jax.ShapeDtypeStruct((1, D_MODEL), F32)],
        (*dproj_parts, w_in_t, x, g1, dx2), after=after)


def _coords():
    return lax.axis_index("x"), lax.axis_index("y"), lax.axis_index("c")


class Exchange:
    def __init__(self, srcs, bufs, new_shapes, n_sems, make):
        self.srcs, self.bufs, self.new_shapes, self.n_sems, self.make = list(srcs), list(bufs), list(new_shapes), n_sems, make


def _call(name, body, n_steps, in_specs, out_specs, out_shape, args, scratch_shapes=(), after=()):
    n_in = len(args)

    def wrapped(*refs):
        body(*refs[:n_in], *refs[n_in + len(after):])

    return list(pl.pallas_call(
        wrapped, name=name, grid=(n_steps,), in_specs=list(in_specs) + [ANY] * len(after), out_specs=list(out_specs),
        out_shape=list(out_shape), scratch_shapes=list(scratch_shapes), compiler_params=_params(),
    )(*args, *after))


GATHER_SEMS = 8


def gather(bufs):
    n = len(bufs)

    def make(src_refs, buf_refs, new_refs, send_sems, recv_sems):
        x, y, c = _coords()
        me, sibling = (x, y, c), (x, y, 1 - c)
        over_x, over_y, across = (1 - x, y), (x, 1 - y), (1 - x, 1 - y)

        def copy(a, k, block, to, half=None):
            r = buf_refs[a].shape[0] // N_DEV
            lo, size = (0, r) if half is None else (half * (r // 2), r // 2)
            rows = buf_refs[a].at[pl.ds((4 * block[0] + 2 * block[1] + block[2]) * r + lo, size), :]
            return pltpu.make_async_remote_copy(
                src_ref=rows, dst_ref=rows, send_sem=send_sems.at[GATHER_SEMS * a + k],
                recv_sem=recv_sems.at[GATHER_SEMS * a + k], device_id=to, device_id_type=MESH)

        every = range(n)
        out = ([copy(a, 0, me, sibling) for a in every] + [copy(a, 1, me, (*over_x, c)) for a in every]
               + [copy(a, 2, me, (*over_y, c)) for a in every])
        near_in = [copy(a, 1, (*over_x, c), me) for a in every] + [copy(a, 2, (*over_y, c), me) for a in every]
        relay = ([copy(a, 3, (*over_x, c), (*over_y, c), half=0) for a in every]
                 + [copy(a, 4, (*over_y, c), (*over_x, c), half=1) for a in every])
        near_on = [copy(a, 5, (*over_x, c), sibling) for a in every] + [copy(a, 6, (*over_y, c), sibling) for a in every]
        relay_in = ([copy(a, 3, (*across, c), me, half=0) for a in every]
                    + [copy(a, 4, (*across, c), me, half=1) for a in every])
        far_on = [copy(a, 7, (*across, c), sibling) for a in every]
        from_core = ([copy(a, 0, sibling, me) for a in every] + [copy(a, 5, (*over_x, 1 - c), me) for a in every]
                     + [copy(a, 6, (*over_y, 1 - c), me) for a in every] + [copy(a, 7, (*across, 1 - c), me) for a in every])
        stages = [([], out), (near_in, relay + near_on), (relay_in, far_on)]
        return stages, out + relay + near_on + far_on, from_core

    return Exchange([], bufs, [], GATHER_SEMS * n, make)


TO_GATHER = (1, lambda x, y, c: [(x, y, 1 - c), (1 - x, y, c), (x, 1 - y, c)])
TO_SIBLING = (2, lambda x, y, c: [(x, y, 1 - c)])
TO_CHIPS = (3, lambda x, y, c: [(1 - x, y, c), (x, 1 - y, c), (1 - x, 1 - y, c)])
TO_ALL = (4, lambda x, y, c: [(x ^ (m >> 2), y ^ ((m >> 1) & 1), c ^ (m & 1)) for m in range(1, N_DEV)])


def by_sequencer(name, exchanges, who):
    collective_id, peers_of = who
    hbm = pltpu.MemorySpace.HBM
    refs = [([jax.new_ref(a, memory_space=hbm) for a in ex.srcs], [jax.new_ref(a, memory_space=hbm) for a in ex.bufs],
             [jax.empty_ref(s, memory_space=hbm) for s in ex.new_shapes]) for ex in exchanges]
    sems = []
    for ex in exchanges:
        sems += [pltpu.SemaphoreType.DMA((ex.n_sems,)), pltpu.SemaphoreType.DMA((ex.n_sems,))]

    @pl.kernel(mesh=plsc.ScalarSubcoreMesh(axis_name="sequencer", num_cores=1), name=name, scratch_types=tuple(sems),
               compiler_params=pltpu.CompilerParams(collective_id=collective_id))
    def launch(*sem_refs):
        peers = peers_of(*_coords())
        barrier = pltpu.get_barrier_semaphore()
        for peer in peers:
            pl.semaphore_signal(barrier, inc=1, device_id=peer, device_id_type=MESH)
        pl.semaphore_wait(barrier, len(peers))

        made = [ex.make(*refs[k], sem_refs[2 * k], sem_refs[2 * k + 1]) for k, ex in enumerate(exchanges)]
        for stage in range(max(len(stages) for stages, _, _ in made)):
            for stages, _, _ in made:
                if stage < len(stages):
                    arrivals, starts = stages[stage]
                    for cp in arrivals:
                        cp.wait_recv()
                    for cp in starts:
                        cp.start()
        for _, sends, arrivals in made:
            for cp in arrivals:
                cp.wait_recv()
            for cp in sends:
                cp.wait_send()

    launch()
    return [([ref[...] for ref in bufs], [ref[...] for ref in news]) for _, bufs, news in refs]


def place_shards(name, shards, dev):
    n = len(shards)

    def body(dev_ref, *refs):
        for a in range(n):
            refs[n + a][...] = refs[a][...].astype(BF16)

    spec = pltpu.PrefetchScalarGridSpec(
        num_scalar_prefetch=1, grid=(1,),
        in_specs=[pl.BlockSpec(s.shape, lambda i, dev_ref: (0, 0)) for s in shards],
        out_specs=[pl.BlockSpec(s.shape, lambda i, dev_ref: (dev_ref[0], 0)) for s in shards])
    return pl.pallas_call(
        body, name=name, grid_spec=spec,
        out_shape=[jax.ShapeDtypeStruct((N_DEV * s.shape[0], s.shape[1]), BF16) for s in shards],
        compiler_params=_params(),
    )(dev, *shards)


def _swap(copies_of):
    def make(src_refs, buf_refs, new_refs, send_sems, recv_sems):
        copies = copies_of(src_refs, new_refs, send_sems, recv_sems)
        return [([], copies)], copies, copies

    return make


def to_sibling(grads):
    def copies_of(src_refs, new_refs, send_sems, recv_sems):
        x, y, c = _coords()
        return [pltpu.make_async_remote_copy(
            src_ref=src_refs[a].at[2 * xy + 1 - c], dst_ref=new_refs[a].at[xy], send_sem=send_sems.at[4 * a + xy],
            recv_sem=recv_sems.at[4 * a + xy], device_id=(x, y, 1 - c), device_id_type=MESH)
            for a in range(len(src_refs)) for xy in range(4)]

    return Exchange(grads, [], [jax.ShapeDtypeStruct((4,) + g.shape[1:], g.dtype) for g in grads], 4 * len(grads),
                    _swap(copies_of))


def to_chips(parts):
    def copies_of(src_refs, new_refs, send_sems, recv_sems):
        x, y, c = _coords()
        chips = [(1 - x, y), (x, 1 - y), (1 - x, 1 - y)]
        return [pltpu.make_async_remote_copy(
            src_ref=src_refs[a].at[2 * px + py], dst_ref=new_refs[a].at[2 * x + y], send_sem=send_sems.at[3 * a + j],
            recv_sem=recv_sems.at[3 * a + j], device_id=(px, py, c), device_id_type=MESH)
            for a in range(len(src_refs)) for j, (px, py) in enumerate(chips)]

    return Exchange(parts, [], [jax.ShapeDtypeStruct(p.shape, p.dtype) for p in parts], 3 * len(parts), _swap(copies_of))


def to_owners(grad):
    def copies_of(src_refs, new_refs, send_sems, recv_sems):
        x, y, c = _coords()
        copies = []
        for m in range(1, N_DEV):
            px, py, pc = x ^ (m >> 2), y ^ ((m >> 1) & 1), c ^ (m & 1)
            copies.append(pltpu.make_async_remote_copy(
                src_ref=src_refs[0].at[4 * px + 2 * py + pc], dst_ref=new_refs[0].at[4 * x + 2 * y + c],
                send_sem=send_sems.at[m - 1], recv_sem=recv_sems.at[m - 1], device_id=(px, py, pc), device_id_type=MESH))
        return copies

    return Exchange([grad], [], [jax.ShapeDtypeStruct(grad.shape, grad.dtype)], N_DEV - 1, _swap(copies_of))


def to_everyone(vec):
    def copies_of(src_refs, new_refs, send_sems, recv_sems):
        x, y, c = _coords()
        copies = []
        for m in range(1, N_DEV):
            px, py, pc = x ^ (m >> 2), y ^ ((m >> 1) & 1), c ^ (m & 1)
            copies.append(pltpu.make_async_remote_copy(
                src_ref=src_refs[0], dst_ref=new_refs[0].at[4 * x + 2 * y + c],
                send_sem=send_sems.at[m - 1], recv_sem=recv_sems.at[m - 1], device_id=(px, py, pc), device_id_type=MESH))
        return copies

    return Exchange([vec], [], [jax.ShapeDtypeStruct((N_DEV,) + vec.shape, vec.dtype)], N_DEV - 1, _swap(copies_of))


def sum_cores(name, grad, other, core, after=()):
    _, r, w = other.shape

    def body(core_ref, g_ref, o_ref, *rest):
        rest[-1][...] = (g_ref[...].astype(F32) + o_ref[...].astype(F32)).astype(rest[-1].dtype)

    return pl.pallas_call(
        body, name=name,
        grid_spec=pltpu.PrefetchScalarGridSpec(
            num_scalar_prefetch=1, grid=(4,),
            in_specs=[pl.BlockSpec((1, r, w), lambda i, core_ref: (2 * i + core_ref[0], 0, 0)),
                      pl.BlockSpec((1, r, w), lambda i, core_ref: (i, 0, 0))] + [ANY] * len(after),
            out_specs=pl.BlockSpec((1, r, w), lambda i, core_ref: (i, 0, 0))),
        out_shape=jax.ShapeDtypeStruct(other.shape, other.dtype),
        compiler_params=_params(),
    )(core, grad, other, *after)


def sum_owned(name, grad, others, dev_ids, after=()):
    _, r, w = grad.shape

    def body(ids_ref, *refs):
        acc = refs[0][0]
        for k in range(1, N_DEV):
            acc = acc + refs[k][0]
        refs[-1][...] = acc

    def pick(k):
        return pl.BlockSpec((1, r, w), lambda i, ids_ref: (ids_ref[k], 0, 0))

    return pl.pallas_call(
        body, name=name,
        grid_spec=pltpu.PrefetchScalarGridSpec(
            num_scalar_prefetch=1, grid=(1,), in_specs=[pick(k) for k in range(N_DEV)] + [ANY] * len(after),
            out_specs=pl.BlockSpec((r, w), lambda i, ids_ref: (ids_ref[0], 0))),
        out_shape=jax.ShapeDtypeStruct((N_DEV * r, w), F32),
        compiler_params=_params(),
    )(dev_ids, grad, *([others] * (N_DEV - 1)), *after)


def _adamw_update(w, g, m, v):
    nm = ADAM_B1 * m + np.float32(1.0 - ADAM_B1) * g
    nv = ADAM_B2 * v + np.float32(1.0 - ADAM_B2) * (g * g)
    m_hat = nm / np.float32(1.0 - ADAM_B1 ** ADAM_STEP)
    v_hat = nv / np.float32(1.0 - ADAM_B2 ** ADAM_STEP)
    return -ADAM_LR * (m_hat / (jnp.sqrt(v_hat) + ADAM_EPS) + ADAM_WD * w), nm, nv


def adamw_of_sums(name, part, others, chip_ids, w, m, v, after):
    _, r, wd = part.shape
    halves = 2
    rows = r // halves

    def body(ids_ref, p_ref, a_ref, b_ref, c_ref, w_ref, m_ref, v_ref, after_ref, g_ref, d_ref, nm_ref, nv_ref):
        g = ((p_ref[0].astype(F32) + a_ref[0].astype(F32)) + b_ref[0].astype(F32)) + c_ref[0].astype(F32)
        g_ref[...] = g
        d_ref[...], nm_ref[...], nv_ref[...] = _adamw_update(w_ref[...], g, m_ref[...], v_ref[...])

    def pick(k):
        return pl.BlockSpec((1, rows, wd), lambda i, ids_ref: (ids_ref[k], i, 0))

    whole = pl.BlockSpec((rows, wd), lambda i, ids_ref: (i, 0))
    shape = jax.ShapeDtypeStruct((r, wd), F32)
    return pl.pallas_call(
        body, name=name,
        grid_spec=pltpu.PrefetchScalarGridSpec(
            num_scalar_prefetch=1, grid=(halves,), in_specs=[pick(0), pick(1), pick(2), pick(3), whole, whole, whole, ANY],
            out_specs=[whole] * 4),
        out_shape=[shape] * 4,
        compiler_params=_params(),
    )(chip_ids, part, others, others, others, w, m, v, after)


def pack_small(parts):
    names = [name for name, _ in SMALL if name in parts]
    operands = [parts[name] for name in names]
    first_row, at = {}, 0
    for name, size in SMALL:
        first_row[name] = at // 128
        at += size
    sizes = dict(SMALL)

    def body(*refs):
        out_ref = refs[-1]
        out_ref[...] = jnp.zeros_like(out_ref)
        for name, ref in zip(names, refs):
            row = first_row[name]
            if name == "loss_sum":
                lane0 = lax.broadcasted_iota(jnp.int32, (1, 128), 1) == 0
                out_ref[row:row + 1, :] = jnp.where(lane0, ref[...], 0.0)
            else:
                rows = sizes[name] // 128
                out_ref[row:row + rows, :] = ref[...].reshape(rows, 128)

    vmem = pl.BlockSpec(memory_space=pltpu.VMEM)
    return pl.pallas_call(
        body, name="pack_small", in_specs=[vmem] * len(names), out_specs=vmem,
        out_shape=jax.ShapeDtypeStruct((SMALL_ROWS, 128), F32), compiler_params=_params(()),
    )(*operands)


LATE = "pre_mix_norm"


def adamw_small(packed_g, late_parts, ws, ms, vs):
    names = [name for name, _ in SMALL if name != "loss_sum"]
    k = len(names)
    shapes = [ws[name].shape[1:] if ws[name].ndim > 2 else ws[name].shape for name in names]
    first_row, at = [], 0
    for name, size in SMALL:
        first_row.append(at // 128)
        at += size

    def body(g_ref, late_ref, *refs):
        w_refs, m_refs, v_refs, outs = refs[:k], refs[k:2 * k], refs[2 * k:3 * k], refs[3 * k:]
        for i, (name, size) in enumerate(SMALL[:k]):
            if name == LATE:
                g = late_ref[0]
                for j in range(1, N_DEV):
                    g = g + late_ref[j]
            else:
                g = g_ref[first_row[i]:first_row[i] + size // 128, :].reshape(shapes[i])
            outs[i][...] = g
            outs[k + i][...], outs[2 * k + i][...], outs[3 * k + i][...] = _adamw_update(
                w_refs[i][...], g, m_refs[i][...], v_refs[i][...])
        outs[4 * k][...] = g_ref[first_row[k]:first_row[k] + 1, 0:1]

    vmem = pl.BlockSpec(memory_space=pltpu.VMEM)
    operands = [t[name].reshape(shape) for t in (ws, ms, vs) for name, shape in zip(names, shapes)]
    outs = pl.pallas_call(
        body, name="adamw_small", in_specs=[vmem] * (2 + 3 * k), out_specs=[vmem] * (4 * k + 1),
        out_shape=[jax.ShapeDtypeStruct(shape, F32) for _ in range(4) for shape in shapes] + [jax.ShapeDtypeStruct((1, 1), F32)],
        compiler_params=_params(()),
    )(packed_g, late_parts, *operands)
    tables = [{name: outs[j * k + i].reshape(ws[name].shape) for i, name in enumerate(names)} for j in range(4)]
    return (*tables, outs[4 * k])


def kernel(x, positions, pre_mix_norm, w_in, sgu_ln_gain, sgu_ln_bias, sgu_w_spatial, sgu_b_spatial, attn_out_norm, sgu_out_norm, w_out, post_mix_norm, pre_ffn_norm, w_gate, w_up, w_down, post_ffn_norm, loss_target, m_pre_mix_norm, m_w_in, m_sgu_ln_gain, m_sgu_ln_bias, m_sgu_w_spatial, m_sgu_b_spatial, m_attn_out_norm, m_sgu_out_norm, m_w_out, m_post_mix_norm, m_pre_ffn_norm, m_w_gate, m_w_up, m_w_down, m_post_ffn_norm, v_pre_mix_norm, v_w_in, v_sgu_ln_gain, v_sgu_ln_bias, v_sgu_w_spatial, v_sgu_b_spatial, v_attn_out_norm, v_sgu_out_norm, v_w_out, v_post_mix_norm, v_pre_ffn_norm, v_w_gate, v_w_up, v_w_down, v_post_ffn_norm):
    small_w = dict(pre_mix_norm=pre_mix_norm, sgu_ln_gain=sgu_ln_gain, sgu_ln_bias=sgu_ln_bias, sgu_w_spatial=sgu_w_spatial,
                   sgu_b_spatial=sgu_b_spatial, attn_out_norm=attn_out_norm, sgu_out_norm=sgu_out_norm,
                   post_mix_norm=post_mix_norm, pre_ffn_norm=pre_ffn_norm, post_ffn_norm=post_ffn_norm)
    small_m = dict(pre_mix_norm=m_pre_mix_norm, sgu_ln_gain=m_sgu_ln_gain, sgu_ln_bias=m_sgu_ln_bias, sgu_w_spatial=m_sgu_w_spatial,
                   sgu_b_spatial=m_sgu_b_spatial, attn_out_norm=m_attn_out_norm, sgu_out_norm=m_sgu_out_norm,
                   post_mix_norm=m_post_mix_norm, pre_ffn_norm=m_pre_ffn_norm, post_ffn_norm=m_post_ffn_norm)
    small_v = dict(pre_mix_norm=v_pre_mix_norm, sgu_ln_gain=v_sgu_ln_gain, sgu_ln_bias=v_sgu_ln_bias, sgu_w_spatial=v_sgu_w_spatial,
                   sgu_b_spatial=v_sgu_b_spatial, attn_out_norm=v_attn_out_norm, sgu_out_norm=v_sgu_out_norm,
                   post_mix_norm=v_post_mix_norm, pre_ffn_norm=v_pre_ffn_norm, post_ffn_norm=v_post_ffn_norm)

    x2d = x[0]
    target = loss_target[0]
    pos_col = positions.reshape(SEQ, 1)
    rot = _rot_consts()
    w_sp = sgu_w_spatial[0]
    bfull = jnp.repeat(sgu_b_spatial[0].T, HEAD_DIM, axis=1)

    x_i, y_i, c_i = (lax.axis_index(a).astype(jnp.int32) for a in MESH_AXES)
    dev = 4 * x_i + 2 * y_i + c_i
    core = c_i.reshape(1)
    chip = 2 * x_i + y_i
    chip_ids = jnp.stack([chip, chip ^ 1, chip ^ 2, chip ^ 3])
    dev_ids = jnp.stack([dev ^ m for m in range(N_DEV)])

    def gathered(name, bufs):
        return by_sequencer(name, [gather(bufs)], TO_GATHER)[0][0]

    def from_sibling(name, grads):
        return by_sequencer(name, [to_sibling(grads)], TO_SIBLING)[0][1]

    def from_chips(name, parts):
        return by_sequencer(name, [to_chips(parts)], TO_CHIPS)[0][1]

    (w_in_t,) = place_shards("place_w_in", [w_in[0].T], dev.reshape(1))
    (w_in_t,) = gathered("gather_w_in", [w_in_t])
    w_gate_t, w_up_t, w_out_f, w_down_f = place_shards(
        "place_weights", [w_gate[0].T, w_up[0].T, w_out[0], w_down[0]], dev.reshape(1))
    (w_out_f,) = gathered("gather_w_out", [w_out_f])
    w_gate_t, w_up_t = gathered("gather_w_gate_up", [w_gate_t, w_up_t])
    (w_down_f,) = gathered("gather_w_down", [w_down_f])

    h1, u, vs, q, k, v = in_proj(x2d, pos_col, pre_mix_norm, w_in_t, rot)
    attn_r, lse, attn = attn_fwd(q, k, v)
    (sgu,) = sgu_fwd(u, vs, sgu_ln_gain, sgu_ln_bias, w_sp, bfull)
    mix, y, x2, h2 = out_proj(attn, sgu, x2d, attn_out_norm, sgu_out_norm, w_out_f, post_mix_norm, pre_ffn_norm)
    gate, up, act = ffn_up(h2, w_gate_t, w_up_t)
    df, dx3, d_post_ffn, sq_err = ffn_down_loss(act, w_down_f, x2, post_ffn_norm, target)

    g_w_down = weight_grad("grad_w_down", act, df)
    (s_down,) = from_sibling("w_down_to_sibling", [g_w_down])
    dgate, dup, dx2, dy, d_pre_ffn, d_post_mix = ffn_bwd(
        df, w_down_f, gate, up, w_gate_t, w_up_t, x2, pre_ffn_norm, dx3, y, post_mix_norm, after=[g_w_down])
    p_down = sum_cores("sum_cores_down", g_w_down, s_down, core, after=[dy])
    (c_down,) = from_chips("w_down_to_chips", [p_down])
    g_w_gate, g_w_up = weight_grads("grad_w_gate_up", [dgate, dup], h2, after=[p_down])
    s_gate, s_up = from_sibling("w_gate_up_to_sibling", [g_w_gate, g_w_up])
    g_w_out = weight_grad("grad_w_out", mix, dy, after=[g_w_up, c_down])
    p_gate = sum_cores("sum_cores_gate", g_w_gate, s_gate, core, after=[g_w_out])
    p_up = sum_cores("sum_cores_up", g_w_up, s_up, core, after=[g_w_out])
    c_gate, c_up = from_chips("w_gate_up_to_chips", [p_gate, p_up])
    (s_out,) = from_sibling("w_out_to_sibling", [g_w_out])
    dsgu, d_attn_out, d_sgu_out, dattn_r = mix_bwd(dy, w_out_f, attn, sgu, attn_out_norm, sgu_out_norm, after=[p_gate, p_up])
    du, dvs, d_ln_gain, d_ln_bias, d_w_sp, d_bfull = sgu_bwd(u, vs, dsgu, sgu_ln_gain, sgu_ln_bias, w_sp, bfull)

    d_b_sp = d_bfull.reshape(CHUNK, N_GROUPS, HEAD_DIM).sum(axis=-1).T
    small_g = pack_small(dict(sgu_ln_gain=d_ln_gain, sgu_ln_bias=d_ln_bias, sgu_w_spatial=d_w_sp, sgu_b_spatial=d_b_sp,
                              attn_out_norm=d_attn_out, sgu_out_norm=d_sgu_out, post_mix_norm=d_post_mix,
                              pre_ffn_norm=d_pre_ffn, post_ffn_norm=d_post_ffn, loss_sum=sq_err))
    small_g = small_g.reshape(N_DEV, SMALL_ROWS // N_DEV, 128)
    ((_, (o_small,)),) = by_sequencer("small_to_owners", [to_owners(small_g)], TO_ALL)

    dq, dk, dv = attn_bwd(q, k, v, attn_r, lse, dattn_r, _to_residue_order(pos_col), rot)
    summed_small = sum_owned("sum_small", small_g, o_small, dev_ids, after=[dq])
    (all_small,) = gathered("gather_small_grads", [summed_small])
    p_out = sum_cores("sum_cores_out", g_w_out, s_out, core, after=[dq])
    (c_out,) = from_chips("w_out_to_chips", [p_out])
    dq, dk, dv = (_from_residue_order(t) for t in (dq, dk, dv))
    dproj = [dq, dk, dv, du, dvs]
    g_w_in = weight_grad_of_parts("grad_w_in", dproj, h1, after=[c_gate, c_up, all_small])
    (s_in,) = from_sibling("w_in_to_sibling", [g_w_in])
    grad_x, d_pre_mix = in_bwd(dproj, w_in_t, x2d, pre_mix_norm, dx2, after=[g_w_in, c_out])
    p_in = sum_cores("sum_cores_in", g_w_in, s_in, core, after=[d_pre_mix])
    (_, (c_in,)), (_, (late_parts,)) = by_sequencer(
        "last_sums_to_owners", [to_chips([p_in]), to_everyone(d_pre_mix)], TO_ALL)
    late_parts = lax.dynamic_update_slice(late_parts, d_pre_mix[None], (dev, 0, 0))

    big, last = {}, p_in
    for name, w, p, c, m, vv, transposed in (
            ("w_down", w_down, p_down, c_down, m_w_down, v_w_down, False), ("w_gate", w_gate, p_gate, c_gate, m_w_gate, v_w_gate, True),
            ("w_up", w_up, p_up, c_up, m_w_up, v_w_up, True), ("w_out", w_out, p_out, c_out, m_w_out, v_w_out, False),
            ("w_in", w_in, p_in, c_in, m_w_in, v_w_in, True)):
        turn = (lambda t: t.T) if transposed else (lambda t: t)
        outs = adamw_of_sums("adamw_" + name, p, c, chip_ids, turn(w[0]), turn(m[0]), turn(vv[0]), last)
        big[name], last = tuple(turn(t)[None] for t in outs), outs[0]
    sg, sd, snm, snv, loss_sum = adamw_small(all_small, late_parts, small_w, small_m, small_v)
    loss = loss_sum[0, 0] * np.float32(0.5 / D_MODEL)

    names = ["pre_mix_norm", "w_in", "sgu_ln_gain", "sgu_ln_bias", "sgu_w_spatial", "sgu_b_spatial", "attn_out_norm",
             "sgu_out_norm", "w_out", "post_mix_norm", "pre_ffn_norm", "w_gate", "w_up", "w_down", "post_ffn_norm"]
    outs = [loss, grad_x[None]]
    for i, table in enumerate((sg, sd, snm, snv)):
        for name in names:
            outs.append(big[name][i] if name in big else table[name])
    return tuple(outs)
```

```python
import numpy as np
import jax
import jax.numpy as jnp
from jax import lax
from jax.experimental import pallas as pl
from jax.experimental.pallas import tpu as pltpu
from jax.experimental.pallas import tpu_sc as plsc

F32 = jnp.float32
BF16 = jnp.bfloat16

SEQ = 2048
D_MODEL = 1024
ATTN_W = 512
SGU_W = 512
HEAD_DIM = 64
N_GROUPS = 8
CHUNK = 128
D_FF = 2816
IN_W = 3 * ATTN_W + 2 * SGU_W
DILATIONS = (1, 4, 16)
ROPE_THETA = 500000.0
ROT_DIM = 16
ROT_HALF = 8
RMS_EPS = 1e-6
LN_EPS = 1e-5
Q_SCALE = 0.125
NEG = -1e30

N_DEV = 8
MESH_AXES = ("x", "y", "c")
MESH = pl.DeviceIdType.MESH

ADAM_LR = 0.001
ADAM_B1 = 0.9
ADAM_B2 = 0.999
ADAM_EPS = 1e-08
ADAM_WD = 0.01
ADAM_STEP = 10

VMEM_LIMIT = 60 * 1024 * 1024
ANY = pl.BlockSpec(memory_space=pl.ANY)

SMALL = (("pre_mix_norm", 1024), ("sgu_ln_gain", 512), ("sgu_ln_bias", 512), ("sgu_w_spatial", 8 * 128 * 128),
         ("sgu_b_spatial", 1024), ("attn_out_norm", 512), ("sgu_out_norm", 512), ("post_mix_norm", 1024),
         ("pre_ffn_norm", 1024), ("post_ffn_norm", 1024), ("loss_sum", 1))
SMALL_ROWS = 1152


def _params(sem=("arbitrary",)):
    return pltpu.CompilerParams(dimension_semantics=sem, vmem_limit_bytes=VMEM_LIMIT)


def _dot(a, b):
    return jnp.dot(a, b, preferred_element_type=F32)


def _dot_nt(a, b):
    return lax.dot_general(a, b, (((1,), (1,)), ((), ())), preferred_element_type=F32)


def _dot_tn(a, b):
    return lax.dot_general(a, b, (((0,), (0,)), ((), ())), preferred_element_type=F32)


def _rms(z):
    return lax.rsqrt(jnp.mean(z * z, axis=-1, keepdims=True) + RMS_EPS)


def _rms_bwd(z, gain, d):
    r = _rms(z)
    n = z * r
    dn = d * gain
    dz = r * (dn - n * jnp.mean(dn * n, axis=-1, keepdims=True))
    return dz, jnp.sum(d * n, axis=0, keepdims=True)


def _gelu(z):
    return 0.5 * z * (1.0 + lax.erf(z * np.float32(1.0 / np.sqrt(2.0))))


def _gelu_grad(z):
    cdf = 0.5 * (1.0 + lax.erf(z * np.float32(1.0 / np.sqrt(2.0))))
    return cdf + z * jnp.exp(-0.5 * z * z) * np.float32(1.0 / np.sqrt(2.0 * np.pi))


def _rot_tables(pos_col, invf, ma, mb):
    ang = pos_col.astype(F32) * invf
    s = jnp.sin(ang)
    return jnp.cos(ang), s * ma, s * mb


def _rot(t, c, sa, sb):
    return t * c + pltpu.roll(t, 120, 1) * sa + pltpu.roll(t, 8, 1) * sb


def _rot_t(d, c, sa, sb):
    return d * c + pltpu.roll(d * sa, 8, 1) + pltpu.roll(d * sb, 120, 1)


def _rot_consts():
    lane = np.arange(128) % HEAD_DIM
    inv_freq = (np.float32(ROPE_THETA) ** (-np.arange(0, ROT_DIM, 2, dtype=np.float32) / np.float32(ROT_DIM))).astype(np.float32)
    invf = np.where(lane < ROT_DIM, inv_freq[lane % ROT_HALF], 0.0).astype(np.float32)
    ma = np.where(lane < ROT_HALF, -1.0, 0.0).astype(np.float32)
    mb = np.where((lane >= ROT_HALF) & (lane < ROT_DIM), 1.0, 0.0).astype(np.float32)
    return jnp.asarray(invf[None]), jnp.asarray(ma[None]), jnp.asarray(mb[None])


def _row_spec(tm, w):
    return pl.BlockSpec((tm, w), lambda i: (i, 0))


def _full_spec(shape):
    return pl.BlockSpec(shape, lambda i: (0,) * len(shape))


def _weight_spec(shape):
    return pl.BlockSpec(shape, lambda i: (0,) * len(shape), pipeline_mode=pl.Buffered(1))


RES = 16


def _residue_scratch(n_arrays, tm, width):
    return [pltpu.VMEM((2, n_arrays, tm // RES, RES, width), F32), pltpu.SemaphoreType.DMA((2, n_arrays, RES))]


def _to_residue_rows(tiles, outs, scratch, sems, tm, n_steps):
    i = pl.program_id(0)
    slot = i % 2
    per = tm // RES

    def copies(step, s):
        return [pltpu.make_async_copy(scratch.at[s, a, :, b, :],
                                      outs[a].at[pl.ds(pl.multiple_of(b * (SEQ // RES) + per * step, per), per), :],
                                      sems.at[s, a, b]) for a in range(len(outs)) for b in range(RES)]

    @pl.when(i >= 2)
    def _():
        for cp in copies(i - 2, slot):
            cp.wait()

    for a, tile in enumerate(tiles):
        scratch[slot, a] = tile.reshape(per, RES, tile.shape[-1])
    for cp in copies(i, slot):
        cp.start()

    @pl.when(i == n_steps - 1)
    def _():
        for cp in copies(i - 1, 1 - slot) + copies(i, slot):
            cp.wait()


def in_proj(x, pos_col, g1, w_in_t, rot):
    tm = 512
    n_steps = SEQ // tm

    def body(x_ref, pos_ref, g_ref, w_ref, invf_ref, ma_ref, mb_ref, h_ref, u_ref, vs_ref, q_ref, k_ref, v_ref, scratch, sems):
        xf = x_ref[...]
        h = (xf * _rms(xf) * g_ref[...]).astype(BF16)
        h_ref[...] = h
        proj = _dot_nt(h, w_ref[...])
        c, sa, sb = _rot_tables(pos_ref[...], invf_ref[...], ma_ref[...], mb_ref[...])
        slabs = range(ATTN_W // 128)
        q = jnp.concatenate([_rot(proj[:, j * 128:(j + 1) * 128], c, sa, sb) * Q_SCALE for j in slabs], axis=1)
        k = jnp.concatenate([_rot(proj[:, ATTN_W + j * 128:ATTN_W + (j + 1) * 128], c, sa, sb) for j in slabs], axis=1)
        u_ref[...] = proj[:, 3 * ATTN_W:3 * ATTN_W + SGU_W]
        vs_ref[...] = proj[:, 3 * ATTN_W + SGU_W:]
        _to_residue_rows([q, k, proj[:, 2 * ATTN_W:3 * ATTN_W]], [q_ref, k_ref, v_ref], scratch, sems, tm, n_steps)

    act = jax.ShapeDtypeStruct((SEQ, 512), F32)
    return _call(
        "in_proj", body, n_steps,
        [_row_spec(tm, D_MODEL), _row_spec(tm, 1), _full_spec((1, D_MODEL)), _weight_spec((IN_W, D_MODEL)),
         _full_spec((1, 128)), _full_spec((1, 128)), _full_spec((1, 128))],
        [_row_spec(tm, D_MODEL)] + [_row_spec(tm, 512)] * 2 + [ANY] * 3,
        [jax.ShapeDtypeStruct((SEQ, D_MODEL), BF16)] + [act] * 5,
        (x, pos_col, g1, w_in_t, *rot), scratch_shapes=_residue_scratch(3, tm, ATTN_W))


def _to_residue_order(t):
    return t.reshape(SEQ // RES, RES, -1).transpose(1, 0, 2).reshape(t.shape)


def _from_residue_order(t):
    return t.reshape(RES, SEQ // RES, -1).transpose(1, 0, 2).reshape(t.shape)


def _block_rows(d, r, n):
    if d == 16:
        slices = [(128 * r, 128)]
    elif d == 4:
        slices = [(128 * (4 * b + r) + 32 * n, 32) for b in range(4)]
    else:
        slices = [(128 * b + 8 * n, 8) for b in range(RES)]
    return [(s if isinstance(s, int) else pl.multiple_of(s, z), z) for s, z in slices]


def _block_step(d, i):
    if d == 16:
        return i
    if d == 4:
        return 4 * (i & 31) + (i >> 5)
    return 16 * (i & 7) + (i >> 3)


def _attn_masks(d):
    row2 = _block_step(d, lax.broadcasted_iota(jnp.int32, (128, 256), 0))
    col2 = lax.broadcasted_iota(jnp.int32, (128, 256), 1)
    key2 = _block_step(d, col2 & 127)
    mask2 = jnp.logical_or(jnp.logical_and(col2 < 128, key2 >= row2), jnp.logical_and(col2 >= 128, key2 <= row2))
    row1 = _block_step(d, lax.broadcasted_iota(jnp.int32, (128, 128), 0))
    col1 = lax.broadcasted_iota(jnp.int32, (128, 128), 1)
    return col1 < HEAD_DIM, _block_step(d, col1) <= row1, mask2


def _load_rows(ref, slices):
    parts = [ref[pl.ds(s, z), :] for s, z in slices]
    return parts[0] if len(parts) == 1 else jnp.concatenate(parts, axis=0)


def _for_each_group(fn):
    for p, d in enumerate(DILATIONS):
        masks = _attn_masks(d)
        if d == 16:
            def group(i, carry, p=p, masks=masks):
                fn(p, masks, [(_block_rows(16, 8 * i + g, 0), None) for g in range(8)])
                return carry

            lax.fori_loop(0, 2, group, 0)
        elif d == 4:
            fn(p, masks, [(_block_rows(4, r, 0), None) for r in range(4)])

            def group(i, carry, p=p, masks=masks):
                blocks = [6 * i + g for g in range(6)]
                fn(p, masks, [(_block_rows(4, j % 4, 1 + j // 4), _block_rows(4, j % 4, j // 4)) for j in blocks])
                return carry

            lax.fori_loop(0, 2, group, 0)
        else:
            fn(p, masks, [(_block_rows(1, 0, 0), None)])

            def group(i, carry, p=p, masks=masks):
                fn(p, masks, [(_block_rows(1, 0, 5 * i + g + 1), _block_rows(1, 0, 5 * i + g)) for g in range(5)])
                return carry

            lax.fori_loop(0, 3, group, 0)


def attn_fwd(q, k, v):
    def body(q_ref, k_ref, v_ref, o_ref, lse_ref, nat_ref, op_ref, lp_ref, sems):
        def group(p, masks, blocks):
            head0, mask1, mask2 = masks
            heads = (head0, jnp.logical_not(head0))
            keys = [rows if prev is None else prev + rows for rows, prev in blocks]
            mask = [mask1 if prev is None else mask2 for _, prev in blocks]
            qb = [_load_rows(q_ref, rows) for rows, _ in blocks]
            kk = [_load_rows(k_ref, ks).astype(BF16) for ks in keys]
            vv = [_load_rows(v_ref, ks).astype(BF16) for ks in keys]
            chains = [(g, hm) for g in range(len(blocks)) for hm in heads]
            s = [jnp.where(mask[g], _dot_nt(jnp.where(hm, qb[g], 0.0).astype(BF16), kk[g]), NEG) for g, hm in chains]
            m = [jnp.max(t, axis=-1, keepdims=True) for t in s]
            e = [jnp.exp(t - mt) for t, mt in zip(s, m)]
            l = [jnp.sum(t, axis=-1, keepdims=True) for t in e]
            pv = [_dot(t.astype(BF16), vv[g]) for t, (g, _) in zip(e, chains)]
            for g, (rows, _) in enumerate(blocks):
                o_blk = jnp.where(head0, pv[2 * g] / l[2 * g], pv[2 * g + 1] / l[2 * g + 1])
                l_blk = jnp.where(head0, jnp.broadcast_to(m[2 * g] + jnp.log(l[2 * g]), (128, 128)),
                                  jnp.broadcast_to(m[2 * g + 1] + jnp.log(l[2 * g + 1]), (128, 128)))
                at = 0
                for start, size in rows:
                    op_ref[p, pl.ds(start, size), :] = o_blk[at:at + size]
                    lp_ref[p, pl.ds(start, size), :] = l_blk[at:at + size]
                    at += size

        _for_each_group(group)

        def combine(i, carry):
            rows = pl.ds(pl.multiple_of(i * 256, 256), 256)
            ls = [lp_ref[p, rows, :] for p in range(3)]
            m = jnp.maximum(jnp.maximum(ls[0], ls[1]), ls[2])
            lse = m + jnp.log(jnp.exp(ls[0] - m) + jnp.exp(ls[1] - m) + jnp.exp(ls[2] - m))
            o = jnp.zeros((256, 128), F32)
            for p in range(3):
                o = o + jnp.exp(ls[p] - lse) * op_ref[p, rows, :]
            o_ref[rows, :] = o
            lse_ref[rows, :] = lse
            return carry

        lax.fori_loop(0, SEQ // 256, combine, 0)

        lanes = pl.ds(pl.multiple_of(pl.program_id(0) * 128, 128), 128)
        back = [pltpu.make_async_copy(o_ref.at[pl.ds(b * (SEQ // RES), SEQ // RES), :], nat_ref.at[:, b, lanes], sems.at[b])
                for b in range(RES)]
        for cp in back:
            cp.start()
        for cp in back:
            cp.wait()

    slab = pl.BlockSpec((SEQ, 128), lambda i: (0, i))
    out = jax.ShapeDtypeStruct((SEQ, ATTN_W), F32)
    attn_r, lse, attn = _call(
        "attn_fwd", body, ATTN_W // 128, [slab] * 3, [slab] * 2 + [ANY],
        [out, out, jax.ShapeDtypeStruct((SEQ // RES, RES, ATTN_W), F32)], (q, k, v),
        scratch_shapes=[pltpu.VMEM((3, SEQ, 128), F32), pltpu.VMEM((3, SEQ, 128), F32), pltpu.SemaphoreType.DMA((RES,))])
    return attn_r, lse, attn.reshape(SEQ, ATTN_W)


def _causal_weights(w_ref):
    row = lax.broadcasted_iota(jnp.int32, (CHUNK, CHUNK), 0)
    col = lax.broadcasted_iota(jnp.int32, (CHUNK, CHUNK), 1)
    return [jnp.where(col <= row, w_ref[g], 0.0).astype(BF16) for g in range(N_GROUPS)], col <= row


def _sgu_chunk_fwd(u, vs, lg, lb, wc, bfull, head0):
    ug = _gelu(u)
    vg = _gelu(vs)
    xc = vg - jnp.mean(vg, axis=-1, keepdims=True)
    rstd = lax.rsqrt(jnp.mean(xc * xc, axis=-1, keepdims=True) + LN_EPS)
    xhat = xc * rstd
    vn = xhat * lg + lb
    mixed = []
    for gp in range(SGU_W // 128):
        vp = vn[:, gp * 128:(gp + 1) * 128].astype(BF16)
        mixed.append(jnp.where(head0, _dot(wc[2 * gp], vp), _dot(wc[2 * gp + 1], vp)))
    ms = jnp.concatenate(mixed, axis=1) + bfull
    return ug, xhat, rstd, vn, ms


def sgu_fwd(u, vs, lg, lb, w_sp, bfull):
    cpb = 4

    def body(u_ref, vs_ref, lg_ref, lb_ref, w_ref, b_ref, o_ref):
        wc, _ = _causal_weights(w_ref)
        head0 = lax.broadcasted_iota(jnp.int32, (CHUNK, 128), 1) < HEAD_DIM
        for ci in range(cpb):
            rows = pl.ds(ci * CHUNK, CHUNK)
            ug, _, _, _, ms = _sgu_chunk_fwd(u_ref[rows, :], vs_ref[rows, :], lg_ref[...], lb_ref[...], wc, b_ref[...], head0)
            o_ref[rows, :] = ug * ms

    tm = cpb * CHUNK
    return _call(
        "sgu_fwd", body, SEQ // tm,
        [_row_spec(tm, SGU_W), _row_spec(tm, SGU_W), _full_spec((1, SGU_W)), _full_spec((1, SGU_W)),
         _full_spec((N_GROUPS, CHUNK, CHUNK)), _full_spec((CHUNK, SGU_W))],
        [_row_spec(tm, SGU_W)], [jax.ShapeDtypeStruct((SEQ, SGU_W), F32)],
        (u, vs, lg, lb, w_sp, bfull))


def out_proj(attn, sgu, x, ga, gs, w_out, gpm, gpf):
    tm = 512

    def body(a_ref, s_ref, x_ref, ga_ref, gs_ref, w_ref, gpm_ref, gpf_ref, mix_ref, y_ref, x2_ref, h2_ref):
        a = a_ref[...]
        s = s_ref[...]
        an = (a * _rms(a) * ga_ref[...]).astype(BF16)
        sn = (s * _rms(s) * gs_ref[...]).astype(BF16)
        mix_ref[:, :ATTN_W] = an
        mix_ref[:, ATTN_W:] = sn
        y = _dot(an, w_ref[:ATTN_W, :]) + _dot(sn, w_ref[ATTN_W:, :])
        y_ref[...] = y
        x2 = x_ref[...] + y * _rms(y) * gpm_ref[...]
        x2_ref[...] = x2
        h2_ref[...] = (x2 * _rms(x2) * gpf_ref[...]).astype(BF16)

    wide = jax.ShapeDtypeStruct((SEQ, D_MODEL), F32)
    wide16 = jax.ShapeDtypeStruct((SEQ, D_MODEL), BF16)
    return _call(
        "out_proj", body, SEQ // tm,
        [_row_spec(tm, ATTN_W), _row_spec(tm, SGU_W), _row_spec(tm, D_MODEL), _full_spec((1, ATTN_W)),
         _full_spec((1, SGU_W)), _weight_spec((D_MODEL, D_MODEL)), _full_spec((1, D_MODEL)), _full_spec((1, D_MODEL))],
        [_row_spec(tm, D_MODEL)] * 4, [wide16, wide, wide, wide16],
        (attn, sgu, x, ga, gs, w_out, gpm, gpf))


def ffn_up(h2, w_gate_t, w_up_t):
    tm = 256

    def body(h_ref, wg_ref, wu_ref, g_ref, u_ref, a_ref):
        h = h_ref[...]
        g = _dot_nt(h, wg_ref[...])
        u = _dot_nt(h, wu_ref[...])
        g_ref[...] = g.astype(BF16)
        u_ref[...] = u.astype(BF16)
        a_ref[...] = (g * jax.nn.sigmoid(g) * u).astype(BF16)

    ff = jax.ShapeDtypeStruct((SEQ, D_FF), BF16)
    return _call(
        "ffn_up", body, SEQ // tm,
        [_row_spec(tm, D_MODEL), _weight_spec((D_FF, D_MODEL)), _weight_spec((D_FF, D_MODEL))],
        [_row_spec(tm, D_FF)] * 3, [ff, ff, jax.ShapeDtypeStruct((SEQ, D_FF), BF16)],
        (h2, w_gate_t, w_up_t))


def ffn_down_loss(act, w_down, x2, gpo, target):
    tm = 512

    def body(a_ref, w_ref, x2_ref, g_ref, t_ref, df_ref, dx3_ref, dg_ref, loss_ref):
        f = _dot(a_ref[...], w_ref[...])
        gain = g_ref[...]
        err = x2_ref[...] + f * _rms(f) * gain - t_ref[...]
        dx3 = err * np.float32(1.0 / D_MODEL)
        dx3_ref[...] = dx3
        df, dg = _rms_bwd(f, gain, dx3)
        df_ref[...] = df.astype(BF16)

        @pl.when(pl.program_id(0) == 0)
        def _():
            dg_ref[...] = jnp.zeros_like(dg_ref)
            loss_ref[...] = jnp.zeros_like(loss_ref)

        dg_ref[...] += dg
        loss_ref[...] += jnp.sum(err * err, axis=(0, 1), keepdims=True)

    return pl.pallas_call(
        body, name="ffn_down_loss", grid=(SEQ // tm,),
        in_specs=[_row_spec(tm, D_FF), _weight_spec((D_FF, D_MODEL)), _row_spec(tm, D_MODEL), _full_spec((1, D_MODEL)),
                  _row_spec(tm, D_MODEL)],
        out_specs=[_row_spec(tm, D_MODEL), _row_spec(tm, D_MODEL), _full_spec((1, D_MODEL)), _full_spec((1, 1))],
        out_shape=[jax.ShapeDtypeStruct((SEQ, D_MODEL), BF16), jax.ShapeDtypeStruct((SEQ, D_MODEL), F32),
                   jax.ShapeDtypeStruct((1, D_MODEL), F32), jax.ShapeDtypeStruct((1, 1), F32)],
        compiler_params=_params(),
    )(act, w_down, x2, gpo, target)


def ffn_bwd(df, w_down, gate, up, w_gate_t, w_up_t, x2, gpf, dx3, y, gpm, after=()):
    tm = 256

    def body(df_ref, wd_ref, g_ref, u_ref, wg_ref, wu_ref, x2_ref, gpf_ref, dx3_ref, y_ref, gpm_ref,
             dg_ref, du_ref, dx2_ref, dy_ref, dgpf_ref, dgpm_ref):
        dact = _dot_nt(df_ref[...], wd_ref[...])
        g = g_ref[...].astype(F32)
        s = jax.nn.sigmoid(g)
        dup = (dact * g * s).astype(BF16)
        dgate = (dact * u_ref[...].astype(F32) * (s * (1.0 + g * (1.0 - s)))).astype(BF16)
        du_ref[...] = dup
        dg_ref[...] = dgate
        dh2 = _dot(dgate, wg_ref[...]) + _dot(dup, wu_ref[...])
        dz, dgpf = _rms_bwd(x2_ref[...], gpf_ref[...], dh2)
        dx2 = dx3_ref[...] + dz
        dx2_ref[...] = dx2
        dy, dgpm = _rms_bwd(y_ref[...], gpm_ref[...], dx2)
        dy_ref[...] = dy.astype(BF16)

        @pl.when(pl.program_id(0) == 0)
        def _():
            dgpf_ref[...] = jnp.zeros_like(dgpf_ref)
            dgpm_ref[...] = jnp.zeros_like(dgpm_ref)

        dgpf_ref[...] += dgpf
        dgpm_ref[...] += dgpm

    vec = jax.ShapeDtypeStruct((1, D_MODEL), F32)
    ff16 = jax.ShapeDtypeStruct((SEQ, D_FF), BF16)
    return _call(
        "ffn_bwd", body, SEQ // tm,
        [_row_spec(tm, D_MODEL), _weight_spec((D_FF, D_MODEL)), _row_spec(tm, D_FF), _row_spec(tm, D_FF),
         _weight_spec((D_FF, D_MODEL)), _weight_spec((D_FF, D_MODEL)), _row_spec(tm, D_MODEL), _full_spec((1, D_MODEL)),
         _row_spec(tm, D_MODEL), _row_spec(tm, D_MODEL), _full_spec((1, D_MODEL))],
        [_row_spec(tm, D_FF), _row_spec(tm, D_FF), _row_spec(tm, D_MODEL), _row_spec(tm, D_MODEL),
         _full_spec((1, D_MODEL)), _full_spec((1, D_MODEL))],
        [ff16, ff16, jax.ShapeDtypeStruct((SEQ, D_MODEL), F32), jax.ShapeDtypeStruct((SEQ, D_MODEL), BF16), vec, vec],
        (df, w_down, gate, up, w_gate_t, w_up_t, x2, gpf, dx3, y, gpm), after=after)


def weight_grads(name, lhs, b, after=()):
    m, n, k = lhs[0].shape[1], b.shape[1], len(lhs)
    tr = 256

    def body(*refs):
        for a_ref, o_ref in zip(refs[:k], refs[k + 1:]):
            o_ref[...] = _dot_tn(a_ref[...], refs[k][...]).astype(BF16)

    outs = _call(
        name, body, m // tr, [pl.BlockSpec((SEQ, tr), lambda i: (0, i))] * k + [_weight_spec((SEQ, n))],
        [_row_spec(tr, n)] * k, [jax.ShapeDtypeStruct((m, n), BF16)] * k, (*lhs, b), after=after)
    return [out.reshape(N_DEV, m // N_DEV, n) for out in outs]


def weight_grad(name, a, b, after=()):
    return weight_grads(name, [a], b, after)[0]


def weight_grad_of_parts(name, parts, b, after=()):
    p, n, k = parts[0].shape[1], b.shape[1], len(parts)
    tr = 256
    per = p // tr

    def body(*refs):
        tile = pl.program_id(0)
        for j in range(k):
            @pl.when(tile // per == j)
            def _(j=j):
                refs[k + 1][...] = _dot_tn(refs[j][...], refs[k][...]).astype(BF16)

    def part_spec(j):
        return pl.BlockSpec((SEQ, tr), lambda i: (0, jnp.clip(i - per * j, 0, per - 1)))

    (out,) = _call(
        name, body, k * per, [part_spec(j) for j in range(k)] + [_weight_spec((SEQ, n))],
        [_row_spec(tr, n)], [jax.ShapeDtypeStruct((k * p, n), BF16)], (*parts, b), after=after)
    return out.reshape(N_DEV, k * p // N_DEV, n)


def mix_bwd(dy, w_out, attn, sgu, ga, gs, after=()):
    tm = 512
    n_steps = SEQ // tm

    def body(dy_ref, w_ref, a_ref, s_ref, ga_ref, gs_ref, ds_ref, dga_ref, dgs_ref, da_ref, scratch, sems):
        dy = dy_ref[...]
        da, dga = _rms_bwd(a_ref[...], ga_ref[...], _dot_nt(dy, w_ref[:ATTN_W, :]))
        ds, dgs = _rms_bwd(s_ref[...], gs_ref[...], _dot_nt(dy, w_ref[ATTN_W:, :]))
        ds_ref[...] = ds
        _to_residue_rows([da], [da_ref], scratch, sems, tm, n_steps)

        @pl.when(pl.program_id(0) == 0)
        def _():
            dga_ref[...] = jnp.zeros_like(dga_ref)
            dgs_ref[...] = jnp.zeros_like(dgs_ref)

        dga_ref[...] += dga
        dgs_ref[...] += dgs

    half = jax.ShapeDtypeStruct((SEQ, 512), F32)
    vec = jax.ShapeDtypeStruct((1, 512), F32)
    return _call(
        "mix_bwd", body, n_steps,
        [_row_spec(tm, D_MODEL), _weight_spec((D_MODEL, D_MODEL)), _row_spec(tm, 512), _row_spec(tm, 512),
         _full_spec((1, 512)), _full_spec((1, 512))],
        [_row_spec(tm, 512), _full_spec((1, 512)), _full_spec((1, 512)), ANY],
        [half, vec, vec, half], (dy, w_out, attn, sgu, ga, gs), scratch_shapes=_residue_scratch(1, tm, ATTN_W), after=after)


def sgu_bwd(u, vs, dsgu, lg, lb, w_sp, bfull):
    cpb = 4

    def body(u_ref, vs_ref, d_ref, lg_ref, lb_ref, w_ref, b_ref, du_ref, dvs_ref, dlg_ref, dlb_ref, dw_ref, db_ref):
        wc, causal = _causal_weights(w_ref)
        head0 = lax.broadcasted_iota(jnp.int32, (CHUNK, 128), 1) < HEAD_DIM
        lg = lg_ref[...]

        @pl.when(pl.program_id(0) == 0)
        def _():
            dlg_ref[...] = jnp.zeros_like(dlg_ref)
            dlb_ref[...] = jnp.zeros_like(dlb_ref)
            dw_ref[...] = jnp.zeros_like(dw_ref)
            db_ref[...] = jnp.zeros_like(db_ref)

        for ci in range(cpb):
            rows = pl.ds(ci * CHUNK, CHUNK)
            u = u_ref[rows, :]
            vs = vs_ref[rows, :]
            d = d_ref[rows, :]
            ug, xhat, rstd, vn, ms = _sgu_chunk_fwd(u, vs, lg, lb_ref[...], wc, b_ref[...], head0)
            du_ref[rows, :] = (d * ms * _gelu_grad(u)).astype(BF16)
            dms = d * ug
            db_ref[...] += dms
            dvn = []
            for gp in range(SGU_W // 128):
                dmp = dms[:, gp * 128:(gp + 1) * 128]
                dm0 = jnp.where(head0, dmp, 0.0).astype(BF16)
                dm1 = jnp.where(head0, 0.0, dmp).astype(BF16)
                vp = vn[:, gp * 128:(gp + 1) * 128].astype(BF16)
                dw_ref[2 * gp] += _dot_nt(dm0, vp)
                dw_ref[2 * gp + 1] += _dot_nt(dm1, vp)
                dvn.append(_dot_tn(wc[2 * gp], dm0) + _dot_tn(wc[2 * gp + 1], dm1))
            dvn = jnp.concatenate(dvn, axis=1)
            dlg_ref[...] += jnp.sum(dvn * xhat, axis=0, keepdims=True)
            dlb_ref[...] += jnp.sum(dvn, axis=0, keepdims=True)
            dxh = dvn * lg
            dvg = rstd * (dxh - jnp.mean(dxh, axis=-1, keepdims=True) - xhat * jnp.mean(dxh * xhat, axis=-1, keepdims=True))
            dvs_ref[rows, :] = (dvg * _gelu_grad(vs)).astype(BF16)

        @pl.when(pl.program_id(0) == pl.num_programs(0) - 1)
        def _():
            for g in range(N_GROUPS):
                dw_ref[g] = jnp.where(causal, dw_ref[g], 0.0)

    tm = cpb * CHUNK
    half16 = jax.ShapeDtypeStruct((SEQ, SGU_W), BF16)
    vec = jax.ShapeDtypeStruct((1, SGU_W), F32)
    return _call(
        "sgu_bwd", body, SEQ // tm,
        [_row_spec(tm, SGU_W)] * 3 + [_full_spec((1, SGU_W)), _full_spec((1, SGU_W)),
                                      _full_spec((N_GROUPS, CHUNK, CHUNK)), _full_spec((CHUNK, SGU_W))],
        [_row_spec(tm, SGU_W), _row_spec(tm, SGU_W), _full_spec((1, SGU_W)), _full_spec((1, SGU_W)),
         _full_spec((N_GROUPS, CHUNK, CHUNK)), _full_spec((CHUNK, SGU_W))],
        [half16, half16, vec, vec, jax.ShapeDtypeStruct((N_GROUPS, CHUNK, CHUNK), F32),
         jax.ShapeDtypeStruct((CHUNK, SGU_W), F32)],
        (u, vs, dsgu, lg, lb, w_sp, bfull))


def attn_bwd(q, k, v, o, lse, do, pos_col, rot):
    def body(q_ref, k_ref, v_ref, o_ref, lse_ref, do_ref, pos_ref, invf_ref, ma_ref, mb_ref,
             dq_ref, dk_ref, dv_ref, dqa_ref, dka_ref, dva_ref, dlt_ref, rot_ref):
        dqa_ref[...] = jnp.zeros_like(dqa_ref)
        dka_ref[...] = jnp.zeros_like(dka_ref)
        dva_ref[...] = jnp.zeros_like(dva_ref)

        def delta(i, carry):
            rows = pl.ds(pl.multiple_of(i * 256, 256), 256)
            prod = do_ref[rows, :] * o_ref[rows, :]
            h0 = lax.broadcasted_iota(jnp.int32, (256, 128), 1) < HEAD_DIM
            d0 = jnp.sum(jnp.where(h0, prod, 0.0), axis=-1, keepdims=True)
            d1 = jnp.sum(jnp.where(h0, 0.0, prod), axis=-1, keepdims=True)
            dlt_ref[rows, :] = jnp.where(h0, d0, d1)
            return carry

        lax.fori_loop(0, SEQ // 256, delta, 0)

        def add_rows(ref, slices, val):
            at = 0
            for start, size in slices:
                ref[pl.ds(start, size), :] += val[at:at + size]
                at += size

        def group(p, masks, blocks):
            head0, mask1, mask2 = masks
            heads = (head0, jnp.logical_not(head0))
            keys = [rows if prev is None else prev + rows for rows, prev in blocks]
            mask = [mask1 if prev is None else mask2 for _, prev in blocks]
            kk = [_load_rows(k_ref, ks).astype(BF16) for ks in keys]
            vv = [_load_rows(v_ref, ks).astype(BF16) for ks in keys]
            qb = [_load_rows(q_ref, rows) for rows, _ in blocks]
            dob = [_load_rows(do_ref, rows) for rows, _ in blocks]
            lse_b = [_load_rows(lse_ref, rows) for rows, _ in blocks]
            dlt_b = [_load_rows(dlt_ref, rows) for rows, _ in blocks]
            chains = [(g, h) for g in range(len(blocks)) for h in range(2)]
            qm = [jnp.where(heads[h], qb[g], 0.0).astype(BF16) for g, h in chains]
            dom = [jnp.where(heads[h], dob[g], 0.0).astype(BF16) for g, h in chains]
            s = [_dot_nt(qm[c], kk[g]) for c, (g, h) in enumerate(chains)]
            dp = [_dot_nt(dom[c], vv[g]) for c, (g, h) in enumerate(chains)]
            pr = [jnp.where(mask[g], jnp.exp(s[c] - lse_b[g][:, h * HEAD_DIM:h * HEAD_DIM + 1]), 0.0)
                  for c, (g, h) in enumerate(chains)]
            ds = [(pr[c] * (dp[c] - dlt_b[g][:, h * HEAD_DIM:h * HEAD_DIM + 1])).astype(BF16)
                  for c, (g, h) in enumerate(chains)]
            dv = [_dot_tn(pr[c].astype(BF16), dom[c]) for c in range(len(chains))]
            dk = [_dot_tn(ds[c], qm[c]) for c in range(len(chains))]
            dq = [_dot(ds[c], kk[g]) for c, (g, h) in enumerate(chains)]
            for g, (rows, _) in enumerate(blocks):
                add_rows(dqa_ref, rows, jnp.where(head0, dq[2 * g], dq[2 * g + 1]))
                add_rows(dka_ref, keys[g], dk[2 * g] + dk[2 * g + 1])
                add_rows(dva_ref, keys[g], dv[2 * g] + dv[2 * g + 1])

        _for_each_group(group)

        @pl.when(pl.program_id(0) == 0)
        def _():
            def tables(i, carry):
                rows = pl.ds(pl.multiple_of(i * 256, 256), 256)
                c, sa, sb = _rot_tables(pos_ref[rows, :], invf_ref[...], ma_ref[...], mb_ref[...])
                rot_ref[0, rows, :] = c
                rot_ref[1, rows, :] = sa
                rot_ref[2, rows, :] = sb
                return carry

            lax.fori_loop(0, SEQ // 256, tables, 0)

        def finish(i, carry):
            rows = pl.ds(pl.multiple_of(i * 256, 256), 256)
            c, sa, sb = rot_ref[0, rows, :], rot_ref[1, rows, :], rot_ref[2, rows, :]
            dq_ref[rows, :] = _rot_t(dqa_ref[rows, :] * Q_SCALE, c, sa, sb).astype(BF16)
            dk_ref[rows, :] = _rot_t(dka_ref[rows, :], c, sa, sb).astype(BF16)
            dv_ref[rows, :] = dva_ref[rows, :].astype(BF16)
            return carry

        lax.fori_loop(0, SEQ // 256, finish, 0)

    slab = pl.BlockSpec((SEQ, 128), lambda i: (0, i))
    out = jax.ShapeDtypeStruct((SEQ, ATTN_W), BF16)
    acc = pltpu.VMEM((SEQ, 128), F32)
    return _call(
        "attn_bwd", body, ATTN_W // 128,
        [slab] * 6 + [_full_spec((SEQ, 1)), _full_spec((1, 128)), _full_spec((1, 128)), _full_spec((1, 128))],
        [slab] * 3, [out, out, out], (q, k, v, o, lse, do, pos_col, *rot),
        scratch_shapes=[acc, acc, acc, acc, pltpu.VMEM((3, SEQ, 128), F32)])


def in_bwd(dproj_parts, w_in_t, x, g1, dx2, after=()):
    tm = 512
    k = len(dproj_parts)

    def body(*refs):
        w_ref, x_ref, g_ref, dx2_ref, dx_ref, dg_ref = refs[k:]
        dh1 = _dot(refs[0][...], w_ref[0:512, :])
        for j in range(1, k):
            dh1 = dh1 + _dot(refs[j][...], w_ref[512 * j:512 * (j + 1), :])
        dz, dg = _rms_bwd(x_ref[...], g_ref[...], dh1)
        dx_ref[...] = dx2_ref[...] + dz

        @pl.when(pl.program_id(0) == 0)
        def _():
            dg_ref[...] = jnp.zeros_like(dg_ref)

        dg_ref[...] += dg

    return _call(
        "in_bwd", body, SEQ // tm,
        [_row_spec(tm, 512)] * k + [_weight_spec((IN_W, D_MODEL)), _row_spec(tm, D_MODEL), _full_spec((1, D_MODEL)),
                                    _row_spec(tm, D_MODEL)],
        [_row_spec(tm, D_MODEL), _full_spec((1, D_MODEL))],
        [jax.ShapeDtypeStruct((SEQ, D_MODEL), F32), jax.ShapeDtypeStruct((1, D_MODEL), F32)],
        (*dproj_parts, w_in_t, x, g1, dx2), after=after)


def _coords():
    return lax.axis_index("x"), lax.axis_index("y"), lax.axis_index("c")


class Exchange:
    def __init__(self, srcs, bufs, new_shapes, n_sems, make):
        self.srcs, self.bufs, self.new_shapes, self.n_sems, self.make = list(srcs), list(bufs), list(new_shapes), n_sems, make


def _call(name, body, n_steps, in_specs, out_specs, out_shape, args, scratch_shapes=(), after=()):
    n_in = len(args)

    def wrapped(*refs):
        body(*refs[:n_in], *refs[n_in + len(after):])

    return list(pl.pallas_call(
        wrapped, name=name, grid=(n_steps,), in_specs=list(in_specs) + [ANY] * len(after), out_specs=list(out_specs),
        out_shape=list(out_shape), scratch_shapes=list(scratch_shapes), compiler_params=_params(),
    )(*args, *after))


GATHER_SEMS = 8


def gather(bufs):
    n = len(bufs)

    def make(src_refs, buf_refs, new_refs, send_sems, recv_sems):
        x, y, c = _coords()
        me, sibling = (x, y, c), (x, y, 1 - c)
        over_x, over_y, across = (1 - x, y), (x, 1 - y), (1 - x, 1 - y)

        def copy(a, k, block, to, half=None):
            r = buf_refs[a].shape[0] // N_DEV
            lo, size = (0, r) if half is None else (half * (r // 2), r // 2)
            rows = buf_refs[a].at[pl.ds((4 * block[0] + 2 * block[1] + block[2]) * r + lo, size), :]
            return pltpu.make_async_remote_copy(
                src_ref=rows, dst_ref=rows, send_sem=send_sems.at[GATHER_SEMS * a + k],
                recv_sem=recv_sems.at[GATHER_SEMS * a + k], device_id=to, device_id_type=MESH)

        every = range(n)
        out = ([copy(a, 0, me, sibling) for a in every] + [copy(a, 1, me, (*over_x, c)) for a in every]
               + [copy(a, 2, me, (*over_y, c)) for a in every])
        near_in = [copy(a, 1, (*over_x, c), me) for a in every] + [copy(a, 2, (*over_y, c), me) for a in every]
        relay = ([copy(a, 3, (*over_x, c), (*over_y, c), half=0) for a in every]
                 + [copy(a, 4, (*over_y, c), (*over_x, c), half=1) for a in every])
        near_on = [copy(a, 5, (*over_x, c), sibling) for a in every] + [copy(a, 6, (*over_y, c), sibling) for a in every]
        relay_in = ([copy(a, 3, (*across, c), me, half=0) for a in every]
                    + [copy(a, 4, (*across, c), me, half=1) for a in every])
        far_on = [copy(a, 7, (*across, c), sibling) for a in every]
        from_core = ([copy(a, 0, sibling, me) for a in every] + [copy(a, 5, (*over_x, 1 - c), me) for a in every]
                     + [copy(a, 6, (*over_y, 1 - c), me) for a in every] + [copy(a, 7, (*across, 1 - c), me) for a in every])
        stages = [([], out), (near_in, relay + near_on), (relay_in, far_on)]
        return stages, out + relay + near_on + far_on, from_core

    return Exchange([], bufs, [], GATHER_SEMS * n, make)


TO_GATHER = (1, lambda x, y, c: [(x, y, 1 - c), (1 - x, y, c), (x, 1 - y, c)])
TO_SIBLING = (2, lambda x, y, c: [(x, y, 1 - c)])
TO_CHIPS = (3, lambda x, y, c: [(1 - x, y, c), (x, 1 - y, c), (1 - x, 1 - y, c)])
TO_ALL = (4, lambda x, y, c: [(x ^ (m >> 2), y ^ ((m >> 1) & 1), c ^ (m & 1)) for m in range(1, N_DEV)])


def by_sequencer(name, exchanges, who):
    collective_id, peers_of = who
    hbm = pltpu.MemorySpace.HBM
    refs = [([jax.new_ref(a, memory_space=hbm) for a in ex.srcs], [jax.new_ref(a, memory_space=hbm) for a in ex.bufs],
             [jax.empty_ref(s, memory_space=hbm) for s in ex.new_shapes]) for ex in exchanges]
    sems = []
    for ex in exchanges:
        sems += [pltpu.SemaphoreType.DMA((ex.n_sems,)), pltpu.SemaphoreType.DMA((ex.n_sems,))]

    @pl.kernel(mesh=plsc.ScalarSubcoreMesh(axis_name="sequencer", num_cores=1), name=name, scratch_types=tuple(sems),
               compiler_params=pltpu.CompilerParams(collective_id=collective_id))
    def launch(*sem_refs):
        peers = peers_of(*_coords())
        barrier = pltpu.get_barrier_semaphore()
        for peer in peers:
            pl.semaphore_signal(barrier, inc=1, device_id=peer, device_id_type=MESH)
        pl.semaphore_wait(barrier, len(peers))

        made = [ex.make(*refs[k], sem_refs[2 * k], sem_refs[2 * k + 1]) for k, ex in enumerate(exchanges)]
        for stage in range(max(len(stages) for stages, _, _ in made)):
            for stages, _, _ in made:
                if stage < len(stages):
                    arrivals, starts = stages[stage]
                    for cp in arrivals:
                        cp.wait_recv()
                    for cp in starts:
                        cp.start()
        for _, sends, arrivals in made:
            for cp in arrivals:
                cp.wait_recv()
            for cp in sends:
                cp.wait_send()

    launch()
    return [([ref[...] for ref in bufs], [ref[...] for ref in news]) for _, bufs, news in refs]


def place_shards(name, shards, dev):
    n = len(shards)

    def body(dev_ref, *refs):
        for a in range(n):
            refs[n + a][...] = refs[a][...].astype(BF16)

    spec = pltpu.PrefetchScalarGridSpec(
        num_scalar_prefetch=1, grid=(1,),
        in_specs=[pl.BlockSpec(s.shape, lambda i, dev_ref: (0, 0)) for s in shards],
        out_specs=[pl.BlockSpec(s.shape, lambda i, dev_ref: (dev_ref[0], 0)) for s in shards])
    return pl.pallas_call(
        body, name=name, grid_spec=spec,
        out_shape=[jax.ShapeDtypeStruct((N_DEV * s.shape[0], s.shape[1]), BF16) for s in shards],
        compiler_params=_params(),
    )(dev, *shards)


def _swap(copies_of):
    def make(src_refs, buf_refs, new_refs, send_sems, recv_sems):
        copies = copies_of(src_refs, new_refs, send_sems, recv_sems)
        return [([], copies)], copies, copies

    return make


def to_sibling(grads):
    def copies_of(src_refs, new_refs, send_sems, recv_sems):
        x, y, c = _coords()
        return [pltpu.make_async_remote_copy(
            src_ref=src_refs[a].at[2 * xy + 1 - c], dst_ref=new_refs[a].at[xy], send_sem=send_sems.at[4 * a + xy],
            recv_sem=recv_sems.at[4 * a + xy], device_id=(x, y, 1 - c), device_id_type=MESH)
            for a in range(len(src_refs)) for xy in range(4)]

    return Exchange(grads, [], [jax.ShapeDtypeStruct((4,) + g.shape[1:], g.dtype) for g in grads], 4 * len(grads),
                    _swap(copies_of))


def to_chips(parts):
    def copies_of(src_refs, new_refs, send_sems, recv_sems):
        x, y, c = _coords()
        chips = [(1 - x, y), (x, 1 - y), (1 - x, 1 - y)]
        return [pltpu.make_async_remote_copy(
            src_ref=src_refs[a].at[2 * px + py], dst_ref=new_refs[a].at[2 * x + y], send_sem=send_sems.at[3 * a + j],
            recv_sem=recv_sems.at[3 * a + j], device_id=(px, py, c), device_id_type=MESH)
            for a in range(len(src_refs)) for j, (px, py) in enumerate(chips)]

    return Exchange(parts, [], [jax.ShapeDtypeStruct(p.shape, p.dtype) for p in parts], 3 * len(parts), _swap(copies_of))


def to_owners(grad):
    def copies_of(src_refs, new_refs, send_sems, recv_sems):
        x, y, c = _coords()
        copies = []
        for m in range(1, N_DEV):
            px, py, pc = x ^ (m >> 2), y ^ ((m >> 1) & 1), c ^ (m & 1)
            copies.append(pltpu.make_async_remote_copy(
                src_ref=src_refs[0].at[4 * px + 2 * py + pc], dst_ref=new_refs[0].at[4 * x + 2 * y + c],
                send_sem=send_sems.at[m - 1], recv_sem=recv_sems.at[m - 1], device_id=(px, py, pc), device_id_type=MESH))
        return copies

    return Exchange([grad], [], [jax.ShapeDtypeStruct(grad.shape, grad.dtype)], N_DEV - 1, _swap(copies_of))


def to_everyone(vec):
    def copies_of(src_refs, new_refs, send_sems, recv_sems):
        x, y, c = _coords()
        copies = []
        for m in range(1, N_DEV):
            px, py, pc = x ^ (m >> 2), y ^ ((m >> 1) & 1), c ^ (m & 1)
            copies.append(pltpu.make_async_remote_copy(
                src_ref=src_refs[0], dst_ref=new_refs[0].at[4 * x + 2 * y + c],
                send_sem=send_sems.at[m - 1], recv_sem=recv_sems.at[m - 1], device_id=(px, py, pc), device_id_type=MESH))
        return copies

    return Exchange([vec], [], [jax.ShapeDtypeStruct((N_DEV,) + vec.shape, vec.dtype)], N_DEV - 1, _swap(copies_of))


def sum_cores(name, grad, other, core, after=()):
    _, r, w = other.shape

    def body(core_ref, g_ref, o_ref, *rest):
        rest[-1][...] = (g_ref[...].astype(F32) + o_ref[...].astype(F32)).astype(rest[-1].dtype)

    return pl.pallas_call(
        body, name=name,
        grid_spec=pltpu.PrefetchScalarGridSpec(
            num_scalar_prefetch=1, grid=(4,),
            in_specs=[pl.BlockSpec((1, r, w), lambda i, core_ref: (2 * i + core_ref[0], 0, 0)),
                      pl.BlockSpec((1, r, w), lambda i, core_ref: (i, 0, 0))] + [ANY] * len(after),
            out_specs=pl.BlockSpec((1, r, w), lambda i, core_ref: (i, 0, 0))),
        out_shape=jax.ShapeDtypeStruct(other.shape, other.dtype),
        compiler_params=_params(),
    )(core, grad, other, *after)


def sum_owned(name, grad, others, dev_ids, after=()):
    _, r, w = grad.shape

    def body(ids_ref, *refs):
        acc = refs[0][0]
        for k in range(1, N_DEV):
            acc = acc + refs[k][0]
        refs[-1][...] = acc

    def pick(k):
        return pl.BlockSpec((1, r, w), lambda i, ids_ref: (ids_ref[k], 0, 0))

    return pl.pallas_call(
        body, name=name,
        grid_spec=pltpu.PrefetchScalarGridSpec(
            num_scalar_prefetch=1, grid=(1,), in_specs=[pick(k) for k in range(N_DEV)] + [ANY] * len(after),
            out_specs=pl.BlockSpec((r, w), lambda i, ids_ref: (ids_ref[0], 0))),
        out_shape=jax.ShapeDtypeStruct((N_DEV * r, w), F32),
        compiler_params=_params(),
    )(dev_ids, grad, *([others] * (N_DEV - 1)), *after)


def _adamw_update(w, g, m, v):
    nm = ADAM_B1 * m + np.float32(1.0 - ADAM_B1) * g
    nv = ADAM_B2 * v + np.float32(1.0 - ADAM_B2) * (g * g)
    m_hat = nm / np.float32(1.0 - ADAM_B1 ** ADAM_STEP)
    v_hat = nv / np.float32(1.0 - ADAM_B2 ** ADAM_STEP)
    return -ADAM_LR * (m_hat / (jnp.sqrt(v_hat) + ADAM_EPS) + ADAM_WD * w), nm, nv


def adamw_of_sums(name, part, others, chip_ids, w, m, v, after):
    _, r, wd = part.shape
    halves = 2
    rows = r // halves

    def body(ids_ref, p_ref, a_ref, b_ref, c_ref, w_ref, m_ref, v_ref, after_ref, g_ref, d_ref, nm_ref, nv_ref):
        g = ((p_ref[0].astype(F32) + a_ref[0].astype(F32)) + b_ref[0].astype(F32)) + c_ref[0].astype(F32)
        g_ref[...] = g
        d_ref[...], nm_ref[...], nv_ref[...] = _adamw_update(w_ref[...], g, m_ref[...], v_ref[...])

    def pick(k):
        return pl.BlockSpec((1, rows, wd), lambda i, ids_ref: (ids_ref[k], i, 0))

    whole = pl.BlockSpec((rows, wd), lambda i, ids_ref: (i, 0))
    shape = jax.ShapeDtypeStruct((r, wd), F32)
    return pl.pallas_call(
        body, name=name,
        grid_spec=pltpu.PrefetchScalarGridSpec(
            num_scalar_prefetch=1, grid=(halves,), in_specs=[pick(0), pick(1), pick(2), pick(3), whole, whole, whole, ANY],
            out_specs=[whole] * 4),
        out_shape=[shape] * 4,
        compiler_params=_params(),
    )(chip_ids, part, others, others, others, w, m, v, after)


def pack_small(parts):
    names = [name for name, _ in SMALL if name in parts]
    operands = [parts[name] for name in names]
    first_row, at = {}, 0
    for name, size in SMALL:
        first_row[name] = at // 128
        at += size
    sizes = dict(SMALL)

    def body(*refs):
        out_ref = refs[-1]
        out_ref[...] = jnp.zeros_like(out_ref)
        for name, ref in zip(names, refs):
            row = first_row[name]
            if name == "loss_sum":
                lane0 = lax.broadcasted_iota(jnp.int32, (1, 128), 1) == 0
                out_ref[row:row + 1, :] = jnp.where(lane0, ref[...], 0.0)
            else:
                rows = sizes[name] // 128
                out_ref[row:row + rows, :] = ref[...].reshape(rows, 128)

    vmem = pl.BlockSpec(memory_space=pltpu.VMEM)
    return pl.pallas_call(
        body, name="pack_small", in_specs=[vmem] * len(names), out_specs=vmem,
        out_shape=jax.ShapeDtypeStruct((SMALL_ROWS, 128), F32), compiler_params=_params(()),
    )(*operands)


LATE = "pre_mix_norm"


def adamw_small(packed_g, late_parts, ws, ms, vs):
    names = [name for name, _ in SMALL if name != "loss_sum"]
    k = len(names)
    shapes = [ws[name].shape[1:] if ws[name].ndim > 2 else ws[name].shape for name in names]
    first_row, at = [], 0
    for name, size in SMALL:
        first_row.append(at // 128)
        at += size

    def body(g_ref, late_ref, *refs):
        w_refs, m_refs, v_refs, outs = refs[:k], refs[k:2 * k], refs[2 * k:3 * k], refs[3 * k:]
        for i, (name, size) in enumerate(SMALL[:k]):
            if name == LATE:
                g = late_ref[0]
                for j in range(1, N_DEV):
                    g = g + late_ref[j]
            else:
                g = g_ref[first_row[i]:first_row[i] + size // 128, :].reshape(shapes[i])
            outs[i][...] = g
            outs[k + i][...], outs[2 * k + i][...], outs[3 * k + i][...] = _adamw_update(
                w_refs[i][...], g, m_refs[i][...], v_refs[i][...])
        outs[4 * k][...] = g_ref[first_row[k]:first_row[k] + 1, 0:1]

    vmem = pl.BlockSpec(memory_space=pltpu.VMEM)
    operands = [t[name].reshape(shape) for t in (ws, ms, vs) for name, shape in zip(names, shapes)]
    outs = pl.pallas_call(
        body, name="adamw_small", in_specs=[vmem] * (2 + 3 * k), out_specs=[vmem] * (4 * k + 1),
        out_shape=[jax.ShapeDtypeStruct(shape, F32) for _ in range(4) for shape in shapes] + [jax.ShapeDtypeStruct((1, 1), F32)],
        compiler_params=_params(()),
    )(packed_g, late_parts, *operands)
    tables = [{name: outs[j * k + i].reshape(ws[name].shape) for i, name in enumerate(names)} for j in range(4)]
    return (*tables, outs[4 * k])


def kernel(x, positions, pre_mix_norm, w_in, sgu_ln_gain, sgu_ln_bias, sgu_w_spatial, sgu_b_spatial, attn_out_norm, sgu_out_norm, w_out, post_mix_norm, pre_ffn_norm, w_gate, w_up, w_down, post_ffn_norm, loss_target, m_pre_mix_norm, m_w_in, m_sgu_ln_gain, m_sgu_ln_bias, m_sgu_w_spatial, m_sgu_b_spatial, m_attn_out_norm, m_sgu_out_norm, m_w_out, m_post_mix_norm, m_pre_ffn_norm, m_w_gate, m_w_up, m_w_down, m_post_ffn_norm, v_pre_mix_norm, v_w_in, v_sgu_ln_gain, v_sgu_ln_bias, v_sgu_w_spatial, v_sgu_b_spatial, v_attn_out_norm, v_sgu_out_norm, v_w_out, v_post_mix_norm, v_pre_ffn_norm, v_w_gate, v_w_up, v_w_down, v_post_ffn_norm):
    small_w = dict(pre_mix_norm=pre_mix_norm, sgu_ln_gain=sgu_ln_gain, sgu_ln_bias=sgu_ln_bias, sgu_w_spatial=sgu_w_spatial,
                   sgu_b_spatial=sgu_b_spatial, attn_out_norm=attn_out_norm, sgu_out_norm=sgu_out_norm,
                   post_mix_norm=post_mix_norm, pre_ffn_norm=pre_ffn_norm, post_ffn_norm=post_ffn_norm)
    small_m = dict(pre_mix_norm=m_pre_mix_norm, sgu_ln_gain=m_sgu_ln_gain, sgu_ln_bias=m_sgu_ln_bias, sgu_w_spatial=m_sgu_w_spatial,
                   sgu_b_spatial=m_sgu_b_spatial, attn_out_norm=m_attn_out_norm, sgu_out_norm=m_sgu_out_norm,
                   post_mix_norm=m_post_mix_norm, pre_ffn_norm=m_pre_ffn_norm, post_ffn_norm=m_post_ffn_norm)
    small_v = dict(pre_mix_norm=v_pre_mix_norm, sgu_ln_gain=v_sgu_ln_gain, sgu_ln_bias=v_sgu_ln_bias, sgu_w_spatial=v_sgu_w_spatial,
                   sgu_b_spatial=v_sgu_b_spatial, attn_out_norm=v_attn_out_norm, sgu_out_norm=v_sgu_out_norm,
                   post_mix_norm=v_post_mix_norm, pre_ffn_norm=v_pre_ffn_norm, post_ffn_norm=v_post_ffn_norm)

    x2d = x[0]
    target = loss_target[0]
    pos_col = positions.reshape(SEQ, 1)
    rot = _rot_consts()
    w_sp = sgu_w_spatial[0]
    bfull = jnp.repeat(sgu_b_spatial[0].T, HEAD_DIM, axis=1)

    x_i, y_i, c_i = (lax.axis_index(a).astype(jnp.int32) for a in MESH_AXES)
    dev = 4 * x_i + 2 * y_i + c_i
    core = c_i.reshape(1)
    chip = 2 * x_i + y_i
    chip_ids = jnp.stack([chip, chip ^ 1, chip ^ 2, chip ^ 3])
    dev_ids = jnp.stack([dev ^ m for m in range(N_DEV)])

    def gathered(name, bufs):
        return by_sequencer(name, [gather(bufs)], TO_GATHER)[0][0]

    def from_sibling(name, grads):
        return by_sequencer(name, [to_sibling(grads)], TO_SIBLING)[0][1]

    def from_chips(name, parts):
        return by_sequencer(name, [to_chips(parts)], TO_CHIPS)[0][1]

    (w_in_t,) = place_shards("place_w_in", [w_in[0].T], dev.reshape(1))
    (w_in_t,) = gathered("gather_w_in", [w_in_t])
    w_gate_t, w_up_t, w_out_f, w_down_f = place_shards(
        "place_weights", [w_gate[0].T, w_up[0].T, w_out[0], w_down[0]], dev.reshape(1))
    (w_out_f,) = gathered("gather_w_out", [w_out_f])
    w_gate_t, w_up_t = gathered("gather_w_gate_up", [w_gate_t, w_up_t])
    (w_down_f,) = gathered("gather_w_down", [w_down_f])

    h1, u, vs, q, k, v = in_proj(x2d, pos_col, pre_mix_norm, w_in_t, rot)
    attn_r, lse, attn = attn_fwd(q, k, v)
    (sgu,) = sgu_fwd(u, vs, sgu_ln_gain, sgu_ln_bias, w_sp, bfull)
    mix, y, x2, h2 = out_proj(attn, sgu, x2d, attn_out_norm, sgu_out_norm, w_out_f, post_mix_norm, pre_ffn_norm)
    gate, up, act = ffn_up(h2, w_gate_t, w_up_t)
    df, dx3, d_post_ffn, sq_err = ffn_down_loss(act, w_down_f, x2, post_ffn_norm, target)

    g_w_down = weight_grad("grad_w_down", act, df)
    (s_down,) = from_sibling("w_down_to_sibling", [g_w_down])
    dgate, dup, dx2, dy, d_pre_ffn, d_post_mix = ffn_bwd(
        df, w_down_f, gate, up, w_gate_t, w_up_t, x2, pre_ffn_norm, dx3, y, post_mix_norm, after=[g_w_down])
    p_down = sum_cores("sum_cores_down", g_w_down, s_down, core, after=[dy])
    (c_down,) = from_chips("w_down_to_chips", [p_down])
    g_w_gate, g_w_up = weight_grads("grad_w_gate_up", [dgate, dup], h2, after=[p_down])
    s_gate, s_up = from_sibling("w_gate_up_to_sibling", [g_w_gate, g_w_up])
    g_w_out = weight_grad("grad_w_out", mix, dy, after=[g_w_up, c_down])
    p_gate = sum_cores("sum_cores_gate", g_w_gate, s_gate, core, after=[g_w_out])
    p_up = sum_cores("sum_cores_up", g_w_up, s_up, core, after=[g_w_out])
    c_gate, c_up = from_chips("w_gate_up_to_chips", [p_gate, p_up])
    (s_out,) = from_sibling("w_out_to_sibling", [g_w_out])
    dsgu, d_attn_out, d_sgu_out, dattn_r = mix_bwd(dy, w_out_f, attn, sgu, attn_out_norm, sgu_out_norm, after=[p_gate, p_up])
    du, dvs, d_ln_gain, d_ln_bias, d_w_sp, d_bfull = sgu_bwd(u, vs, dsgu, sgu_ln_gain, sgu_ln_bias, w_sp, bfull)

    d_b_sp = d_bfull.reshape(CHUNK, N_GROUPS, HEAD_DIM).sum(axis=-1).T
    small_g = pack_small(dict(sgu_ln_gain=d_ln_gain, sgu_ln_bias=d_ln_bias, sgu_w_spatial=d_w_sp, sgu_b_spatial=d_b_sp,
                              attn_out_norm=d_attn_out, sgu_out_norm=d_sgu_out, post_mix_norm=d_post_mix,
                              pre_ffn_norm=d_pre_ffn, post_ffn_norm=d_post_ffn, loss_sum=sq_err))
    small_g = small_g.reshape(N_DEV, SMALL_ROWS // N_DEV, 128)
    ((_, (o_small,)),) = by_sequencer("small_to_owners", [to_owners(small_g)], TO_ALL)

    dq, dk, dv = attn_bwd(q, k, v, attn_r, lse, dattn_r, _to_residue_order(pos_col), rot)
    summed_small = sum_owned("sum_small", small_g, o_small, dev_ids, after=[dq])
    (all_small,) = gathered("gather_small_grads", [summed_small])
    p_out = sum_cores("sum_cores_out", g_w_out, s_out, core, after=[dq])
    (c_out,) = from_chips("w_out_to_chips", [p_out])
    dq, dk, dv = (_from_residue_order(t) for t in (dq, dk, dv))
    dproj = [dq, dk, dv, du, dvs]
    g_w_in = weight_grad_of_parts("grad_w_in", dproj, h1, after=[c_gate, c_up])
    (s_in,) = from_sibling("w_in_to_sibling", [g_w_in])
    grad_x, d_pre_mix = in_bwd(dproj, w_in_t, x2d, pre_mix_norm, dx2, after=[g_w_in, all_small])
    p_in = sum_cores("sum_cores_in", g_w_in, s_in, core, after=[d_pre_mix, c_out])
    (_, (c_in,)), (_, (late_parts,)) = by_sequencer(
        "last_sums_to_owners", [to_chips([p_in]), to_everyone(d_pre_mix)], TO_ALL)
    late_parts = lax.dynamic_update_slice(late_parts, d_pre_mix[None], (dev, 0, 0))

    big, last = {}, p_in
    for name, w, p, c, m, vv, transposed in (
            ("w_down", w_down, p_down, c_down, m_w_down, v_w_down, False), ("w_gate", w_gate, p_gate, c_gate, m_w_gate, v_w_gate, True),
            ("w_up", w_up, p_up, c_up, m_w_up, v_w_up, True), ("w_out", w_out, p_out, c_out, m_w_out, v_w_out, False),
            ("w_in", w_in, p_in, c_in, m_w_in, v_w_in, True)):
        turn = (lambda t: t.T) if transposed else (lambda t: t)
        outs = adamw_of_sums("adamw_" + name, p, c, chip_ids, turn(w[0]), turn(m[0]), turn(vv[0]), last)
        big[name], last = tuple(turn(t)[None] for t in outs), outs[0]
    sg, sd, snm, snv, loss_sum = adamw_small(all_small, late_parts, small_w, small_m, small_v)
    loss = loss_sum[0, 0] * np.float32(0.5 / D_MODEL)

    names = ["pre_mix_norm", "w_in", "sgu_ln_gain", "sgu_ln_bias", "sgu_w_spatial", "sgu_b_spatial", "attn_out_norm",
             "sgu_out_norm", "w_out", "post_mix_norm", "pre_ffn_norm", "w_gate", "w_up", "w_down", "post_ffn_norm"]
    outs = [loss, grad_x[None]]
    for i, table in enumerate((sg, sd, snm, snv)):
        for name in names:
            outs.append(big[name][i] if name in big else table[name])
    return tuple(outs)
```

```python
import numpy as np
import jax
import jax.numpy as jnp
from jax import lax
from jax.experimental import pallas as pl
from jax.experimental.pallas import tpu as pltpu
from jax.experimental.pallas import tpu_sc as plsc

F32 = jnp.float32
BF16 = jnp.bfloat16

SEQ = 2048
D_MODEL = 1024
ATTN_W = 512
SGU_W = 512
HEAD_DIM = 64
N_GROUPS = 8
CHUNK = 128
D_FF = 2816
IN_W = 3 * ATTN_W + 2 * SGU_W
DILATIONS = (1, 4, 16)
ROPE_THETA = 500000.0
ROT_DIM = 16
ROT_HALF = 8
RMS_EPS = 1e-6
LN_EPS = 1e-5
Q_SCALE = 0.125
NEG = -1e30

N_DEV = 8
MESH_AXES = ("x", "y", "c")
MESH = pl.DeviceIdType.MESH

ADAM_LR = 0.001
ADAM_B1 = 0.9
ADAM_B2 = 0.999
ADAM_EPS = 1e-08
ADAM_WD = 0.01
ADAM_STEP = 10

VMEM_LIMIT = 60 * 1024 * 1024
ANY = pl.BlockSpec(memory_space=pl.ANY)

SMALL = (("pre_mix_norm", 1024), ("sgu_ln_gain", 512), ("sgu_ln_bias", 512), ("sgu_w_spatial", 8 * 128 * 128),
         ("sgu_b_spatial", 1024), ("attn_out_norm", 512), ("sgu_out_norm", 512), ("post_mix_norm", 1024),
         ("pre_ffn_norm", 1024), ("post_ffn_norm", 1024), ("loss_sum", 1))
SMALL_ROWS = 1152


def _params(sem=("arbitrary",)):
    return pltpu.CompilerParams(dimension_semantics=sem, vmem_limit_bytes=VMEM_LIMIT)


def _dot(a, b):
    return jnp.dot(a, b, preferred_element_type=F32)


def _dot_nt(a, b):
    return lax.dot_general(a, b, (((1,), (1,)), ((), ())), preferred_element_type=F32)


def _dot_tn(a, b):
    return lax.dot_general(a, b, (((0,), (0,)), ((), ())), preferred_element_type=F32)


def _rms(z):
    return lax.rsqrt(jnp.mean(z * z, axis=-1, keepdims=True) + RMS_EPS)


def _rms_bwd(z, gain, d):
    r = _rms(z)
    n = z * r
    dn = d * gain
    dz = r * (dn - n * jnp.mean(dn * n, axis=-1, keepdims=True))
    return dz, jnp.sum(d * n, axis=0, keepdims=True)


def _gelu(z):
    return 0.5 * z * (1.0 + lax.erf(z * np.float32(1.0 / np.sqrt(2.0))))


def _gelu_grad(z):
    cdf = 0.5 * (1.0 + lax.erf(z * np.float32(1.0 / np.sqrt(2.0))))
    return cdf + z * jnp.exp(-0.5 * z * z) * np.float32(1.0 / np.sqrt(2.0 * np.pi))


def _rot_tables(pos_col, invf, ma, mb):
    ang = pos_col.astype(F32) * invf
    s = jnp.sin(ang)
    return jnp.cos(ang), s * ma, s * mb


def _rot(t, c, sa, sb):
    return t * c + pltpu.roll(t, 120, 1) * sa + pltpu.roll(t, 8, 1) * sb


def _rot_t(d, c, sa, sb):
    return d * c + pltpu.roll(d * sa, 8, 1) + pltpu.roll(d * sb, 120, 1)


def _rot_consts():
    lane = np.arange(128) % HEAD_DIM
    inv_freq = (np.float32(ROPE_THETA) ** (-np.arange(0, ROT_DIM, 2, dtype=np.float32) / np.float32(ROT_DIM))).astype(np.float32)
    invf = np.where(lane < ROT_DIM, inv_freq[lane % ROT_HALF], 0.0).astype(np.float32)
    ma = np.where(lane < ROT_HALF, -1.0, 0.0).astype(np.float32)
    mb = np.where((lane >= ROT_HALF) & (lane < ROT_DIM), 1.0, 0.0).astype(np.float32)
    return jnp.asarray(invf[None]), jnp.asarray(ma[None]), jnp.asarray(mb[None])


def _row_spec(tm, w):
    return pl.BlockSpec((tm, w), lambda i: (i, 0))


def _full_spec(shape):
    return pl.BlockSpec(shape, lambda i: (0,) * len(shape))


def _weight_spec(shape):
    return pl.BlockSpec(shape, lambda i: (0,) * len(shape), pipeline_mode=pl.Buffered(1))


RES = 16


def _residue_scratch(n_arrays, tm, width):
    return [pltpu.VMEM((2, n_arrays, tm // RES, RES, width), F32), pltpu.SemaphoreType.DMA((2, n_arrays, RES))]


def _to_residue_rows(tiles, outs, scratch, sems, tm, n_steps):
    i = pl.program_id(0)
    slot = i % 2
    per = tm // RES

    def copies(step, s):
        return [pltpu.make_async_copy(scratch.at[s, a, :, b, :],
                                      outs[a].at[pl.ds(pl.multiple_of(b * (SEQ // RES) + per * step, per), per), :],
                                      sems.at[s, a, b]) for a in range(len(outs)) for b in range(RES)]

    @pl.when(i >= 2)
    def _():
        for cp in copies(i - 2, slot):
            cp.wait()

    for a, tile in enumerate(tiles):
        scratch[slot, a] = tile.reshape(per, RES, tile.shape[-1])
    for cp in copies(i, slot):
        cp.start()

    @pl.when(i == n_steps - 1)
    def _():
        for cp in copies(i - 1, 1 - slot) + copies(i, slot):
            cp.wait()


def in_proj(x, pos_col, g1, w_in_t, rot):
    tm = 512
    n_steps = SEQ // tm

    def body(x_ref, pos_ref, g_ref, w_ref, invf_ref, ma_ref, mb_ref, h_ref, u_ref, vs_ref, q_ref, k_ref, v_ref, scratch, sems):
        xf = x_ref[...]
        h = (xf * _rms(xf) * g_ref[...]).astype(BF16)
        h_ref[...] = h
        proj = _dot_nt(h, w_ref[...])
        c, sa, sb = _rot_tables(pos_ref[...], invf_ref[...], ma_ref[...], mb_ref[...])
        slabs = range(ATTN_W // 128)
        q = jnp.concatenate([_rot(proj[:, j * 128:(j + 1) * 128], c, sa, sb) * Q_SCALE for j in slabs], axis=1)
        k = jnp.concatenate([_rot(proj[:, ATTN_W + j * 128:ATTN_W + (j + 1) * 128], c, sa, sb) for j in slabs], axis=1)
        u_ref[...] = proj[:, 3 * ATTN_W:3 * ATTN_W + SGU_W]
        vs_ref[...] = proj[:, 3 * ATTN_W + SGU_W:]
        _to_residue_rows([q, k, proj[:, 2 * ATTN_W:3 * ATTN_W]], [q_ref, k_ref, v_ref], scratch, sems, tm, n_steps)

    act = jax.ShapeDtypeStruct((SEQ, 512), F32)
    return _call(
        "in_proj", body, n_steps,
        [_row_spec(tm, D_MODEL), _row_spec(tm, 1), _full_spec((1, D_MODEL)), _weight_spec((IN_W, D_MODEL)),
         _full_spec((1, 128)), _full_spec((1, 128)), _full_spec((1, 128))],
        [_row_spec(tm, D_MODEL)] + [_row_spec(tm, 512)] * 2 + [ANY] * 3,
        [jax.ShapeDtypeStruct((SEQ, D_MODEL), BF16)] + [act] * 5,
        (x, pos_col, g1, w_in_t, *rot), scratch_shapes=_residue_scratch(3, tm, ATTN_W))


def _to_residue_order(t):
    return t.reshape(SEQ // RES, RES, -1).transpose(1, 0, 2).reshape(t.shape)


def _from_residue_order(t):
    return t.reshape(RES, SEQ // RES, -1).transpose(1, 0, 2).reshape(t.shape)


def _block_rows(d, r, n):
    if d == 16:
        slices = [(128 * r, 128)]
    elif d == 4:
        slices = [(128 * (4 * b + r) + 32 * n, 32) for b in range(4)]
    else:
        slices = [(128 * b + 8 * n, 8) for b in range(RES)]
    return [(s if isinstance(s, int) else pl.multiple_of(s, z), z) for s, z in slices]


def _block_step(d, i):
    if d == 16:
        return i
    if d == 4:
        return 4 * (i & 31) + (i >> 5)
    return 16 * (i & 7) + (i >> 3)


def _attn_masks(d):
    row2 = _block_step(d, lax.broadcasted_iota(jnp.int32, (128, 256), 0))
    col2 = lax.broadcasted_iota(jnp.int32, (128, 256), 1)
    key2 = _block_step(d, col2 & 127)
    mask2 = jnp.logical_or(jnp.logical_and(col2 < 128, key2 >= row2), jnp.logical_and(col2 >= 128, key2 <= row2))
    row1 = _block_step(d, lax.broadcasted_iota(jnp.int32, (128, 128), 0))
    col1 = lax.broadcasted_iota(jnp.int32, (128, 128), 1)
    return col1 < HEAD_DIM, _block_step(d, col1) <= row1, mask2


def _load_rows(ref, slices):
    parts = [ref[pl.ds(s, z), :] for s, z in slices]
    return parts[0] if len(parts) == 1 else jnp.concatenate(parts, axis=0)


def _for_each_group(fn):
    for p, d in enumerate(DILATIONS):
        masks = _attn_masks(d)
        if d == 16:
            def group(i, carry, p=p, masks=masks):
                fn(p, masks, [(_block_rows(16, 8 * i + g, 0), None) for g in range(8)])
                return carry

            lax.fori_loop(0, 2, group, 0)
        elif d == 4:
            fn(p, masks, [(_block_rows(4, r, 0), None) for r in range(4)])

            def group(i, carry, p=p, masks=masks):
                blocks = [6 * i + g for g in range(6)]
                fn(p, masks, [(_block_rows(4, j % 4, 1 + j // 4), _block_rows(4, j % 4, j // 4)) for j in blocks])
                return carry

            lax.fori_loop(0, 2, group, 0)
        else:
            fn(p, masks, [(_block_rows(1, 0, 0), None)])

            def group(i, carry, p=p, masks=masks):
                fn(p, masks, [(_block_rows(1, 0, 5 * i + g + 1), _block_rows(1, 0, 5 * i + g)) for g in range(5)])
                return carry

            lax.fori_loop(0, 3, group, 0)


def attn_fwd(q, k, v):
    def body(q_ref, k_ref, v_ref, o_ref, lse_ref, nat_ref, op_ref, lp_ref, sems):
        def group(p, masks, blocks):
            head0, mask1, mask2 = masks
            heads = (head0, jnp.logical_not(head0))
            keys = [rows if prev is None else prev + rows for rows, prev in blocks]
            mask = [mask1 if prev is None else mask2 for _, prev in blocks]
            qb = [_load_rows(q_ref, rows) for rows, _ in blocks]
            kk = [_load_rows(k_ref, ks).astype(BF16) for ks in keys]
            vv = [_load_rows(v_ref, ks).astype(BF16) for ks in keys]
            chains = [(g, hm) for g in range(len(blocks)) for hm in heads]
            s = [jnp.where(mask[g], _dot_nt(jnp.where(hm, qb[g], 0.0).astype(BF16), kk[g]), NEG) for g, hm in chains]
            m = [jnp.max(t, axis=-1, keepdims=True) for t in s]
            e = [jnp.exp(t - mt) for t, mt in zip(s, m)]
            l = [jnp.sum(t, axis=-1, keepdims=True) for t in e]
            pv = [_dot(t.astype(BF16), vv[g]) for t, (g, _) in zip(e, chains)]
            for g, (rows, _) in enumerate(blocks):
                o_blk = jnp.where(head0, pv[2 * g] / l[2 * g], pv[2 * g + 1] / l[2 * g + 1])
                l_blk = jnp.where(head0, jnp.broadcast_to(m[2 * g] + jnp.log(l[2 * g]), (128, 128)),
                                  jnp.broadcast_to(m[2 * g + 1] + jnp.log(l[2 * g + 1]), (128, 128)))
                at = 0
                for start, size in rows:
                    op_ref[p, pl.ds(start, size), :] = o_blk[at:at + size]
                    lp_ref[p, pl.ds(start, size), :] = l_blk[at:at + size]
                    at += size

        _for_each_group(group)

        def combine(i, carry):
            rows = pl.ds(pl.multiple_of(i * 256, 256), 256)
            ls = [lp_ref[p, rows, :] for p in range(3)]
            m = jnp.maximum(jnp.maximum(ls[0], ls[1]), ls[2])
            lse = m + jnp.log(jnp.exp(ls[0] - m) + jnp.exp(ls[1] - m) + jnp.exp(ls[2] - m))
            o = jnp.zeros((256, 128), F32)
            for p in range(3):
                o = o + jnp.exp(ls[p] - lse) * op_ref[p, rows, :]
            o_ref[rows, :] = o
            lse_ref[rows, :] = lse
            return carry

        lax.fori_loop(0, SEQ // 256, combine, 0)

        lanes = pl.ds(pl.multiple_of(pl.program_id(0) * 128, 128), 128)
        back = [pltpu.make_async_copy(o_ref.at[pl.ds(b * (SEQ // RES), SEQ // RES), :], nat_ref.at[:, b, lanes], sems.at[b])
                for b in range(RES)]
        for cp in back:
            cp.start()
        for cp in back:
            cp.wait()

    slab = pl.BlockSpec((SEQ, 128), lambda i: (0, i))
    out = jax.ShapeDtypeStruct((SEQ, ATTN_W), F32)
    attn_r, lse, attn = _call(
        "attn_fwd", body, ATTN_W // 128, [slab] * 3, [slab] * 2 + [ANY],
        [out, out, jax.ShapeDtypeStruct((SEQ // RES, RES, ATTN_W), F32)], (q, k, v),
        scratch_shapes=[pltpu.VMEM((3, SEQ, 128), F32), pltpu.VMEM((3, SEQ, 128), F32), pltpu.SemaphoreType.DMA((RES,))])
    return attn_r, lse, attn.reshape(SEQ, ATTN_W)


def _causal_weights(w_ref):
    row = lax.broadcasted_iota(jnp.int32, (CHUNK, CHUNK), 0)
    col = lax.broadcasted_iota(jnp.int32, (CHUNK, CHUNK), 1)
    return [jnp.where(col <= row, w_ref[g], 0.0).astype(BF16) for g in range(N_GROUPS)], col <= row


def _sgu_chunk_fwd(u, vs, lg, lb, wc, bfull, head0):
    ug = _gelu(u)
    vg = _gelu(vs)
    xc = vg - jnp.mean(vg, axis=-1, keepdims=True)
    rstd = lax.rsqrt(jnp.mean(xc * xc, axis=-1, keepdims=True) + LN_EPS)
    xhat = xc * rstd
    vn = xhat * lg + lb
    mixed = []
    for gp in range(SGU_W // 128):
        vp = vn[:, gp * 128:(gp + 1) * 128].astype(BF16)
        mixed.append(jnp.where(head0, _dot(wc[2 * gp], vp), _dot(wc[2 * gp + 1], vp)))
    ms = jnp.concatenate(mixed, axis=1) + bfull
    return ug, xhat, rstd, vn, ms


def sgu_fwd(u, vs, lg, lb, w_sp, bfull):
    cpb = 4

    def body(u_ref, vs_ref, lg_ref, lb_ref, w_ref, b_ref, o_ref):
        wc, _ = _causal_weights(w_ref)
        head0 = lax.broadcasted_iota(jnp.int32, (CHUNK, 128), 1) < HEAD_DIM
        for ci in range(cpb):
            rows = pl.ds(ci * CHUNK, CHUNK)
            ug, _, _, _, ms = _sgu_chunk_fwd(u_ref[rows, :], vs_ref[rows, :], lg_ref[...], lb_ref[...], wc, b_ref[...], head0)
            o_ref[rows, :] = ug * ms

    tm = cpb * CHUNK
    return _call(
        "sgu_fwd", body, SEQ // tm,
        [_row_spec(tm, SGU_W), _row_spec(tm, SGU_W), _full_spec((1, SGU_W)), _full_spec((1, SGU_W)),
         _full_spec((N_GROUPS, CHUNK, CHUNK)), _full_spec((CHUNK, SGU_W))],
        [_row_spec(tm, SGU_W)], [jax.ShapeDtypeStruct((SEQ, SGU_W), F32)],
        (u, vs, lg, lb, w_sp, bfull))


def out_proj(attn, sgu, x, ga, gs, w_out, gpm, gpf):
    tm = 512

    def body(a_ref, s_ref, x_ref, ga_ref, gs_ref, w_ref, gpm_ref, gpf_ref, mix_ref, y_ref, x2_ref, h2_ref):
        a = a_ref[...]
        s = s_ref[...]
        an = (a * _rms(a) * ga_ref[...]).astype(BF16)
        sn = (s * _rms(s) * gs_ref[...]).astype(BF16)
        mix_ref[:, :ATTN_W] = an
        mix_ref[:, ATTN_W:] = sn
        y = _dot(an, w_ref[:ATTN_W, :]) + _dot(sn, w_ref[ATTN_W:, :])
        y_ref[...] = y
        x2 = x_ref[...] + y * _rms(y) * gpm_ref[...]
        x2_ref[...] = x2
        h2_ref[...] = (x2 * _rms(x2) * gpf_ref[...]).astype(BF16)

    wide = jax.ShapeDtypeStruct((SEQ, D_MODEL), F32)
    wide16 = jax.ShapeDtypeStruct((SEQ, D_MODEL), BF16)
    return _call(
        "out_proj", body, SEQ // tm,
        [_row_spec(tm, ATTN_W), _row_spec(tm, SGU_W), _row_spec(tm, D_MODEL), _full_spec((1, ATTN_W)),
         _full_spec((1, SGU_W)), _weight_spec((D_MODEL, D_MODEL)), _full_spec((1, D_MODEL)), _full_spec((1, D_MODEL))],
        [_row_spec(tm, D_MODEL)] * 4, [wide16, wide, wide, wide16],
        (attn, sgu, x, ga, gs, w_out, gpm, gpf))


def ffn_up(h2, w_gate_t, w_up_t):
    tm = 256

    def body(h_ref, wg_ref, wu_ref, g_ref, u_ref, a_ref):
        h = h_ref[...]
        g = _dot_nt(h, wg_ref[...])
        u = _dot_nt(h, wu_ref[...])
        g_ref[...] = g.astype(BF16)
        u_ref[...] = u.astype(BF16)
        a_ref[...] = (g * jax.nn.sigmoid(g) * u).astype(BF16)

    ff = jax.ShapeDtypeStruct((SEQ, D_FF), BF16)
    return _call(
        "ffn_up", body, SEQ // tm,
        [_row_spec(tm, D_MODEL), _weight_spec((D_FF, D_MODEL)), _weight_spec((D_FF, D_MODEL))],
        [_row_spec(tm, D_FF)] * 3, [ff, ff, jax.ShapeDtypeStruct((SEQ, D_FF), BF16)],
        (h2, w_gate_t, w_up_t))


def ffn_down_loss(act, w_down, x2, gpo, target):
    tm = 512

    def body(a_ref, w_ref, x2_ref, g_ref, t_ref, df_ref, dx3_ref, dg_ref, loss_ref):
        f = _dot(a_ref[...], w_ref[...])
        gain = g_ref[...]
        err = x2_ref[...] + f * _rms(f) * gain - t_ref[...]
        dx3 = err * np.float32(1.0 / D_MODEL)
        dx3_ref[...] = dx3
        df, dg = _rms_bwd(f, gain, dx3)
        df_ref[...] = df.astype(BF16)

        @pl.when(pl.program_id(0) == 0)
        def _():
            dg_ref[...] = jnp.zeros_like(dg_ref)
            loss_ref[...] = jnp.zeros_like(loss_ref)

        dg_ref[...] += dg
        loss_ref[...] += jnp.sum(err * err, axis=(0, 1), keepdims=True)

    return pl.pallas_call(
        body, name="ffn_down_loss", grid=(SEQ // tm,),
        in_specs=[_row_spec(tm, D_FF), _weight_spec((D_FF, D_MODEL)), _row_spec(tm, D_MODEL), _full_spec((1, D_MODEL)),
                  _row_spec(tm, D_MODEL)],
        out_specs=[_row_spec(tm, D_MODEL), _row_spec(tm, D_MODEL), _full_spec((1, D_MODEL)), _full_spec((1, 1))],
        out_shape=[jax.ShapeDtypeStruct((SEQ, D_MODEL), BF16), jax.ShapeDtypeStruct((SEQ, D_MODEL), F32),
                   jax.ShapeDtypeStruct((1, D_MODEL), F32), jax.ShapeDtypeStruct((1, 1), F32)],
        compiler_params=_params(),
    )(act, w_down, x2, gpo, target)


def ffn_bwd(df, w_down, gate, up, w_gate_t, w_up_t, x2, gpf, dx3, y, gpm, after=()):
    tm = 256

    def body(df_ref, wd_ref, g_ref, u_ref, wg_ref, wu_ref, x2_ref, gpf_ref, dx3_ref, y_ref, gpm_ref,
             dg_ref, du_ref, dx2_ref, dy_ref, dgpf_ref, dgpm_ref):
        dact = _dot_nt(df_ref[...], wd_ref[...])
        g = g_ref[...].astype(F32)
        s = jax.nn.sigmoid(g)
        dup = (dact * g * s).astype(BF16)
        dgate = (dact * u_ref[...].astype(F32) * (s * (1.0 + g * (1.0 - s)))).astype(BF16)
        du_ref[...] = dup
        dg_ref[...] = dgate
        dh2 = _dot(dgate, wg_ref[...]) + _dot(dup, wu_ref[...])
        dz, dgpf = _rms_bwd(x2_ref[...], gpf_ref[...], dh2)
        dx2 = dx3_ref[...] + dz
        dx2_ref[...] = dx2
        dy, dgpm = _rms_bwd(y_ref[...], gpm_ref[...], dx2)
        dy_ref[...] = dy.astype(BF16)

        @pl.when(pl.program_id(0) == 0)
        def _():
            dgpf_ref[...] = jnp.zeros_like(dgpf_ref)
            dgpm_ref[...] = jnp.zeros_like(dgpm_ref)

        dgpf_ref[...] += dgpf
        dgpm_ref[...] += dgpm

    vec = jax.ShapeDtypeStruct((1, D_MODEL), F32)
    ff16 = jax.ShapeDtypeStruct((SEQ, D_FF), BF16)
    return _call(
        "ffn_bwd", body, SEQ // tm,
        [_row_spec(tm, D_MODEL), _weight_spec((D_FF, D_MODEL)), _row_spec(tm, D_FF), _row_spec(tm, D_FF),
         _weight_spec((D_FF, D_MODEL)), _weight_spec((D_FF, D_MODEL)), _row_spec(tm, D_MODEL), _full_spec((1, D_MODEL)),
         _row_spec(tm, D_MODEL), _row_spec(tm, D_MODEL), _full_spec((1, D_MODEL))],
        [_row_spec(tm, D_FF), _row_spec(tm, D_FF), _row_spec(tm, D_MODEL), _row_spec(tm, D_MODEL),
         _full_spec((1, D_MODEL)), _full_spec((1, D_MODEL))],
        [ff16, ff16, jax.ShapeDtypeStruct((SEQ, D_MODEL), F32), jax.ShapeDtypeStruct((SEQ, D_MODEL), BF16), vec, vec],
        (df, w_down, gate, up, w_gate_t, w_up_t, x2, gpf, dx3, y, gpm), after=after)


def weight_grads(name, lhs, b, after=()):
    m, n, k = lhs[0].shape[1], b.shape[1], len(lhs)
    tr = 256

    def body(*refs):
        for a_ref, o_ref in zip(refs[:k], refs[k + 1:]):
            o_ref[...] = _dot_tn(a_ref[...], refs[k][...]).astype(BF16)

    outs = _call(
        name, body, m // tr, [pl.BlockSpec((SEQ, tr), lambda i: (0, i))] * k + [_weight_spec((SEQ, n))],
        [_row_spec(tr, n)] * k, [jax.ShapeDtypeStruct((m, n), BF16)] * k, (*lhs, b), after=after)
    return [out.reshape(N_DEV, m // N_DEV, n) for out in outs]


def weight_grad(name, a, b, after=()):
    return weight_grads(name, [a], b, after)[0]


def weight_grad_of_parts(name, parts, b, after=()):
    p, n, k = parts[0].shape[1], b.shape[1], len(parts)
    tr = 256
    per = p // tr

    def body(*refs):
        tile = pl.program_id(0)
        for j in range(k):
            @pl.when(tile // per == j)
            def _(j=j):
                refs[k + 1][...] = _dot_tn(refs[j][...], refs[k][...]).astype(BF16)

    def part_spec(j):
        return pl.BlockSpec((SEQ, tr), lambda i: (0, jnp.clip(i - per * j, 0, per - 1)))

    (out,) = _call(
        name, body, k * per, [part_spec(j) for j in range(k)] + [_weight_spec((SEQ, n))],
        [_row_spec(tr, n)], [jax.ShapeDtypeStruct((k * p, n), BF16)], (*parts, b), after=after)
    return out.reshape(N_DEV, k * p // N_DEV, n)


def mix_bwd(dy, w_out, attn, sgu, ga, gs, after=()):
    tm = 512
    n_steps = SEQ // tm

    def body(dy_ref, w_ref, a_ref, s_ref, ga_ref, gs_ref, ds_ref, dga_ref, dgs_ref, da_ref, scratch, sems):
        dy = dy_ref[...]
        da, dga = _rms_bwd(a_ref[...], ga_ref[...], _dot_nt(dy, w_ref[:ATTN_W, :]))
        ds, dgs = _rms_bwd(s_ref[...], gs_ref[...], _dot_nt(dy, w_ref[ATTN_W:, :]))
        ds_ref[...] = ds
        _to_residue_rows([da], [da_ref], scratch, sems, tm, n_steps)

        @pl.when(pl.program_id(0) == 0)
        def _():
            dga_ref[...] = jnp.zeros_like(dga_ref)
            dgs_ref[...] = jnp.zeros_like(dgs_ref)

        dga_ref[...] += dga
        dgs_ref[...] += dgs

    half = jax.ShapeDtypeStruct((SEQ, 512), F32)
    vec = jax.ShapeDtypeStruct((1, 512), F32)
    return _call(
        "mix_bwd", body, n_steps,
        [_row_spec(tm, D_MODEL), _weight_spec((D_MODEL, D_MODEL)), _row_spec(tm, 512), _row_spec(tm, 512),
         _full_spec((1, 512)), _full_spec((1, 512))],
        [_row_spec(tm, 512), _full_spec((1, 512)), _full_spec((1, 512)), ANY],
        [half, vec, vec, half], (dy, w_out, attn, sgu, ga, gs), scratch_shapes=_residue_scratch(1, tm, ATTN_W), after=after)


def sgu_bwd(u, vs, dsgu, lg, lb, w_sp, bfull):
    cpb = 4

    def body(u_ref, vs_ref, d_ref, lg_ref, lb_ref, w_ref, b_ref, du_ref, dvs_ref, dlg_ref, dlb_ref, dw_ref, db_ref):
        wc, causal = _causal_weights(w_ref)
        head0 = lax.broadcasted_iota(jnp.int32, (CHUNK, 128), 1) < HEAD_DIM
        lg = lg_ref[...]

        @pl.when(pl.program_id(0) == 0)
        def _():
            dlg_ref[...] = jnp.zeros_like(dlg_ref)
            dlb_ref[...] = jnp.zeros_like(dlb_ref)
            dw_ref[...] = jnp.zeros_like(dw_ref)
            db_ref[...] = jnp.zeros_like(db_ref)

        for ci in range(cpb):
            rows = pl.ds(ci * CHUNK, CHUNK)
            u = u_ref[rows, :]
            vs = vs_ref[rows, :]
            d = d_ref[rows, :]
            ug, xhat, rstd, vn, ms = _sgu_chunk_fwd(u, vs, lg, lb_ref[...], wc, b_ref[...], head0)
            du_ref[rows, :] = (d * ms * _gelu_grad(u)).astype(BF16)
            dms = d * ug
            db_ref[...] += dms
            dvn = []
            for gp in range(SGU_W // 128):
                dmp = dms[:, gp * 128:(gp + 1) * 128]
                dm0 = jnp.where(head0, dmp, 0.0).astype(BF16)
                dm1 = jnp.where(head0, 0.0, dmp).astype(BF16)
                vp = vn[:, gp * 128:(gp + 1) * 128].astype(BF16)
                dw_ref[2 * gp] += _dot_nt(dm0, vp)
                dw_ref[2 * gp + 1] += _dot_nt(dm1, vp)
                dvn.append(_dot_tn(wc[2 * gp], dm0) + _dot_tn(wc[2 * gp + 1], dm1))
            dvn = jnp.concatenate(dvn, axis=1)
            dlg_ref[...] += jnp.sum(dvn * xhat, axis=0, keepdims=True)
            dlb_ref[...] += jnp.sum(dvn, axis=0, keepdims=True)
            dxh = dvn * lg
            dvg = rstd * (dxh - jnp.mean(dxh, axis=-1, keepdims=True) - xhat * jnp.mean(dxh * xhat, axis=-1, keepdims=True))
            dvs_ref[rows, :] = (dvg * _gelu_grad(vs)).astype(BF16)

        @pl.when(pl.program_id(0) == pl.num_programs(0) - 1)
        def _():
            for g in range(N_GROUPS):
                dw_ref[g] = jnp.where(causal, dw_ref[g], 0.0)

    tm = cpb * CHUNK
    half16 = jax.ShapeDtypeStruct((SEQ, SGU_W), BF16)
    vec = jax.ShapeDtypeStruct((1, SGU_W), F32)
    return _call(
        "sgu_bwd", body, SEQ // tm,
        [_row_spec(tm, SGU_W)] * 3 + [_full_spec((1, SGU_W)), _full_spec((1, SGU_W)),
                                      _full_spec((N_GROUPS, CHUNK, CHUNK)), _full_spec((CHUNK, SGU_W))],
        [_row_spec(tm, SGU_W), _row_spec(tm, SGU_W), _full_spec((1, SGU_W)), _full_spec((1, SGU_W)),
         _full_spec((N_GROUPS, CHUNK, CHUNK)), _full_spec((CHUNK, SGU_W))],
        [half16, half16, vec, vec, jax.ShapeDtypeStruct((N_GROUPS, CHUNK, CHUNK), F32),
         jax.ShapeDtypeStruct((CHUNK, SGU_W), F32)],
        (u, vs, dsgu, lg, lb, w_sp, bfull))


def attn_bwd(q, k, v, o, lse, do, pos_col, rot):
    def body(q_ref, k_ref, v_ref, o_ref, lse_ref, do_ref, pos_ref, invf_ref, ma_ref, mb_ref,
             dq_ref, dk_ref, dv_ref, dqa_ref, dka_ref, dva_ref, dlt_ref, rot_ref):
        dqa_ref[...] = jnp.zeros_like(dqa_ref)
        dka_ref[...] = jnp.zeros_like(dka_ref)
        dva_ref[...] = jnp.zeros_like(dva_ref)

        def delta(i, carry):
            rows = pl.ds(pl.multiple_of(i * 256, 256), 256)
            prod = do_ref[rows, :] * o_ref[rows, :]
            h0 = lax.broadcasted_iota(jnp.int32, (256, 128), 1) < HEAD_DIM
            d0 = jnp.sum(jnp.where(h0, prod, 0.0), axis=-1, keepdims=True)
            d1 = jnp.sum(jnp.where(h0, 0.0, prod), axis=-1, keepdims=True)
            dlt_ref[rows, :] = jnp.where(h0, d0, d1)
            return carry

        lax.fori_loop(0, SEQ // 256, delta, 0)

        def add_rows(ref, slices, val):
            at = 0
            for start, size in slices:
                ref[pl.ds(start, size), :] += val[at:at + size]
                at += size

        def group(p, masks, blocks):
            head0, mask1, mask2 = masks
            heads = (head0, jnp.logical_not(head0))
            keys = [rows if prev is None else prev + rows for rows, prev in blocks]
            mask = [mask1 if prev is None else mask2 for _, prev in blocks]
            kk = [_load_rows(k_ref, ks).astype(BF16) for ks in keys]
            vv = [_load_rows(v_ref, ks).astype(BF16) for ks in keys]
            qb = [_load_rows(q_ref, rows) for rows, _ in blocks]
            dob = [_load_rows(do_ref, rows) for rows, _ in blocks]
            lse_b = [_load_rows(lse_ref, rows) for rows, _ in blocks]
            dlt_b = [_load_rows(dlt_ref, rows) for rows, _ in blocks]
            chains = [(g, h) for g in range(len(blocks)) for h in range(2)]
            qm = [jnp.where(heads[h], qb[g], 0.0).astype(BF16) for g, h in chains]
            dom = [jnp.where(heads[h], dob[g], 0.0).astype(BF16) for g, h in chains]
            s = [_dot_nt(qm[c], kk[g]) for c, (g, h) in enumerate(chains)]
            dp = [_dot_nt(dom[c], vv[g]) for c, (g, h) in enumerate(chains)]
            pr = [jnp.where(mask[g], jnp.exp(s[c] - lse_b[g][:, h * HEAD_DIM:h * HEAD_DIM + 1]), 0.0)
                  for c, (g, h) in enumerate(chains)]
            ds = [(pr[c] * (dp[c] - dlt_b[g][:, h * HEAD_DIM:h * HEAD_DIM + 1])).astype(BF16)
                  for c, (g, h) in enumerate(chains)]
            dv = [_dot_tn(pr[c].astype(BF16), dom[c]) for c in range(len(chains))]
            dk = [_dot_tn(ds[c], qm[c]) for c in range(len(chains))]
            dq = [_dot(ds[c], kk[g]) for c, (g, h) in enumerate(chains)]
            for g, (rows, _) in enumerate(blocks):
                add_rows(dqa_ref, rows, jnp.where(head0, dq[2 * g], dq[2 * g + 1]))
                add_rows(dka_ref, keys[g], dk[2 * g] + dk[2 * g + 1])
                add_rows(dva_ref, keys[g], dv[2 * g] + dv[2 * g + 1])

        _for_each_group(group)

        @pl.when(pl.program_id(0) == 0)
        def _():
            def tables(i, carry):
                rows = pl.ds(pl.multiple_of(i * 256, 256), 256)
                c, sa, sb = _rot_tables(pos_ref[rows, :], invf_ref[...], ma_ref[...], mb_ref[...])
                rot_ref[0, rows, :] = c
                rot_ref[1, rows, :] = sa
                rot_ref[2, rows, :] = sb
                return carry

            lax.fori_loop(0, SEQ // 256, tables, 0)

        def finish(i, carry):
            rows = pl.ds(pl.multiple_of(i * 256, 256), 256)
            c, sa, sb = rot_ref[0, rows, :], rot_ref[1, rows, :], rot_ref[2, rows, :]
            dq_ref[rows, :] = _rot_t(dqa_ref[rows, :] * Q_SCALE, c, sa, sb).astype(BF16)
            dk_ref[rows, :] = _rot_t(dka_ref[rows, :], c, sa, sb).astype(BF16)
            dv_ref[rows, :] = dva_ref[rows, :].astype(BF16)
            return carry

        lax.fori_loop(0, SEQ // 256, finish, 0)

    slab = pl.BlockSpec((SEQ, 128), lambda i: (0, i))
    out = jax.ShapeDtypeStruct((SEQ, ATTN_W), BF16)
    acc = pltpu.VMEM((SEQ, 128), F32)
    return _call(
        "attn_bwd", body, ATTN_W // 128,
        [slab] * 6 + [_full_spec((SEQ, 1)), _full_spec((1, 128)), _full_spec((1, 128)), _full_spec((1, 128))],
        [slab] * 3, [out, out, out], (q, k, v, o, lse, do, pos_col, *rot),
        scratch_shapes=[acc, acc, acc, acc, pltpu.VMEM((3, SEQ, 128), F32)])


def in_bwd(dproj_parts, w_in_t, x, g1, dx2, after=()):
    tm = 512
    k = len(dproj_parts)

    def body(*refs):
        w_ref, x_ref, g_ref, dx2_ref, dx_ref, dg_ref = refs[k:]
        dh1 = _dot(refs[0][...], w_ref[0:512, :])
        for j in range(1, k):
            dh1 = dh1 + _dot(refs[j][...], w_ref[512 * j:512 * (j + 1), :])
        dz, dg = _rms_bwd(x_ref[...], g_ref[...], dh1)
        dx_ref[...] = dx2_ref[...] + dz

        @pl.when(pl.program_id(0) == 0)
        def _():
            dg_ref[...] = jnp.zeros_like(dg_ref)

        dg_ref[...] += dg

    return _call(
        "in_bwd", body, SEQ // tm,
        [_row_spec(tm, 512)] * k + [_weight_spec((IN_W, D_MODEL)), _row_spec(tm, D_MODEL), _full_spec((1, D_MODEL)),
                                    _row_spec(tm, D_MODEL)],
        [_row_spec(tm, D_MODEL), _full_spec((1, D_MODEL))],
        [jax.ShapeDtypeStruct((SEQ, D_MODEL), F32), jax.ShapeDtypeStruct((1, D_MODEL), F32)],
        (*dproj_parts, w_in_t, x, g1, dx2), after=after)


def _coords():
    return lax.axis_index("x"), lax.axis_index("y"), lax.axis_index("c")


class Exchange:
    def __init__(self, srcs, bufs, new_shapes, n_sems, make):
        self.srcs, self.bufs, self.new_shapes, self.n_sems, self.make = list(srcs), list(bufs), list(new_shapes), n_sems, make


def _call(name, body, n_steps, in_specs, out_specs, out_shape, args, scratch_shapes=(), after=()):
    n_in = len(args)

    def wrapped(*refs):
        body(*refs[:n_in], *refs[n_in + len(after):])

    return list(pl.pallas_call(
        wrapped, name=name, grid=(n_steps,), in_specs=list(in_specs) + [ANY] * len(after), out_specs=list(out_specs),
        out_shape=list(out_shape), scratch_shapes=list(scratch_shapes), compiler_params=_params(),
    )(*args, *after))


GATHER_SEMS = 8


def gather(bufs):
    n = len(bufs)

    def make(src_refs, buf_refs, new_refs, send_sems, recv_sems):
        x, y, c = _coords()
        me, sibling = (x, y, c), (x, y, 1 - c)
        over_x, over_y, across = (1 - x, y), (x, 1 - y), (1 - x, 1 - y)

        def copy(a, k, block, to, half=None):
            r = buf_refs[a].shape[0] // N_DEV
            lo, size = (0, r) if half is None else (half * (r // 2), r // 2)
            rows = buf_refs[a].at[pl.ds((4 * block[0] + 2 * block[1] + block[2]) * r + lo, size), :]
            return pltpu.make_async_remote_copy(
                src_ref=rows, dst_ref=rows, send_sem=send_sems.at[GATHER_SEMS * a + k],
                recv_sem=recv_sems.at[GATHER_SEMS * a + k], device_id=to, device_id_type=MESH)

        every = range(n)
        out = ([copy(a, 0, me, sibling) for a in every] + [copy(a, 1, me, (*over_x, c)) for a in every]
               + [copy(a, 2, me, (*over_y, c)) for a in every])
        near_in = [copy(a, 1, (*over_x, c), me) for a in every] + [copy(a, 2, (*over_y, c), me) for a in every]
        relay = ([copy(a, 3, (*over_x, c), (*over_y, c), half=0) for a in every]
                 + [copy(a, 4, (*over_y, c), (*over_x, c), half=1) for a in every])
        near_on = [copy(a, 5, (*over_x, c), sibling) for a in every] + [copy(a, 6, (*over_y, c), sibling) for a in every]
        relay_in = ([copy(a, 3, (*across, c), me, half=0) for a in every]
                    + [copy(a, 4, (*across, c), me, half=1) for a in every])
        far_on = [copy(a, 7, (*across, c), sibling) for a in every]
        from_core = ([copy(a, 0, sibling, me) for a in every] + [copy(a, 5, (*over_x, 1 - c), me) for a in every]
                     + [copy(a, 6, (*over_y, 1 - c), me) for a in every] + [copy(a, 7, (*across, 1 - c), me) for a in every])
        stages = [([], out), (near_in, relay + near_on), (relay_in, far_on)]
        return stages, out + relay + near_on + far_on, from_core

    return Exchange([], bufs, [], GATHER_SEMS * n, make)


TO_GATHER = (1, lambda x, y, c: [(x, y, 1 - c), (1 - x, y, c), (x, 1 - y, c)])
TO_SIBLING = (2, lambda x, y, c: [(x, y, 1 - c)])
TO_CHIPS = (3, lambda x, y, c: [(1 - x, y, c), (x, 1 - y, c), (1 - x, 1 - y, c)])
TO_ALL = (4, lambda x, y, c: [(x ^ (m >> 2), y ^ ((m >> 1) & 1), c ^ (m & 1)) for m in range(1, N_DEV)])


def by_sequencer(name, exchanges, who):
    collective_id, peers_of = who
    hbm = pltpu.MemorySpace.HBM
    refs = [([jax.new_ref(a, memory_space=hbm) for a in ex.srcs], [jax.new_ref(a, memory_space=hbm) for a in ex.bufs],
             [jax.empty_ref(s, memory_space=hbm) for s in ex.new_shapes]) for ex in exchanges]
    sems = []
    for ex in exchanges:
        sems += [pltpu.SemaphoreType.DMA((ex.n_sems,)), pltpu.SemaphoreType.DMA((ex.n_sems,))]

    @pl.kernel(mesh=plsc.ScalarSubcoreMesh(axis_name="sequencer", num_cores=1), name=name, scratch_types=tuple(sems),
               compiler_params=pltpu.CompilerParams(collective_id=collective_id))
    def launch(*sem_refs):
        peers = peers_of(*_coords())
        barrier = pltpu.get_barrier_semaphore()
        for peer in peers:
            pl.semaphore_signal(barrier, inc=1, device_id=peer, device_id_type=MESH)
        pl.semaphore_wait(barrier, len(peers))

        made = [ex.make(*refs[k], sem_refs[2 * k], sem_refs[2 * k + 1]) for k, ex in enumerate(exchanges)]
        for stage in range(max(len(stages) for stages, _, _ in made)):
            for stages, _, _ in made:
                if stage < len(stages):
                    arrivals, starts = stages[stage]
                    for cp in arrivals:
                        cp.wait_recv()
                    for cp in starts:
                        cp.start()
        for _, sends, arrivals in made:
            for cp in arrivals:
                cp.wait_recv()
            for cp in sends:
                cp.wait_send()

    launch()
    return [([ref[...] for ref in bufs], [ref[...] for ref in news]) for _, bufs, news in refs]


def place_shards(name, shards, dev):
    n = len(shards)

    def body(dev_ref, *refs):
        for a in range(n):
            refs[n + a][...] = refs[a][...].astype(BF16)

    spec = pltpu.PrefetchScalarGridSpec(
        num_scalar_prefetch=1, grid=(1,),
        in_specs=[pl.BlockSpec(s.shape, lambda i, dev_ref: (0, 0)) for s in shards],
        out_specs=[pl.BlockSpec(s.shape, lambda i, dev_ref: (dev_ref[0], 0)) for s in shards])
    return pl.pallas_call(
        body, name=name, grid_spec=spec,
        out_shape=[jax.ShapeDtypeStruct((N_DEV * s.shape[0], s.shape[1]), BF16) for s in shards],
        compiler_params=_params(),
    )(dev, *shards)


def _swap(copies_of):
    def make(src_refs, buf_refs, new_refs, send_sems, recv_sems):
        copies = copies_of(src_refs, new_refs, send_sems, recv_sems)
        return [([], copies)], copies, copies

    return make


def to_sibling(grads):
    def copies_of(src_refs, new_refs, send_sems, recv_sems):
        x, y, c = _coords()
        return [pltpu.make_async_remote_copy(
            src_ref=src_refs[a].at[2 * xy + 1 - c], dst_ref=new_refs[a].at[xy], send_sem=send_sems.at[4 * a + xy],
            recv_sem=recv_sems.at[4 * a + xy], device_id=(x, y, 1 - c), device_id_type=MESH)
            for a in range(len(src_refs)) for xy in range(4)]

    return Exchange(grads, [], [jax.ShapeDtypeStruct((4,) + g.shape[1:], g.dtype) for g in grads], 4 * len(grads),
                    _swap(copies_of))


def to_chips(parts):
    def copies_of(src_refs, new_refs, send_sems, recv_sems):
        x, y, c = _coords()
        chips = [(1 - x, y), (x, 1 - y), (1 - x, 1 - y)]
        return [pltpu.make_async_remote_copy(
            src_ref=src_refs[a].at[2 * px + py], dst_ref=new_refs[a].at[2 * x + y], send_sem=send_sems.at[3 * a + j],
            recv_sem=recv_sems.at[3 * a + j], device_id=(px, py, c), device_id_type=MESH)
            for a in range(len(src_refs)) for j, (px, py) in enumerate(chips)]

    return Exchange(parts, [], [jax.ShapeDtypeStruct(p.shape, p.dtype) for p in parts], 3 * len(parts), _swap(copies_of))


def to_owners(grad):
    def copies_of(src_refs, new_refs, send_sems, recv_sems):
        x, y, c = _coords()
        copies = []
        for m in range(1, N_DEV):
            px, py, pc = x ^ (m >> 2), y ^ ((m >> 1) & 1), c ^ (m & 1)
            copies.append(pltpu.make_async_remote_copy(
                src_ref=src_refs[0].at[4 * px + 2 * py + pc], dst_ref=new_refs[0].at[4 * x + 2 * y + c],
                send_sem=send_sems.at[m - 1], recv_sem=recv_sems.at[m - 1], device_id=(px, py, pc), device_id_type=MESH))
        return copies

    return Exchange([grad], [], [jax.ShapeDtypeStruct(grad.shape, grad.dtype)], N_DEV - 1, _swap(copies_of))


def to_everyone(vec):
    def copies_of(src_refs, new_refs, send_sems, recv_sems):
        x, y, c = _coords()
        copies = []
        for m in range(1, N_DEV):
            px, py, pc = x ^ (m >> 2), y ^ ((m >> 1) & 1), c ^ (m & 1)
            copies.append(pltpu.make_async_remote_copy(
                src_ref=src_refs[0], dst_ref=new_refs[0].at[4 * x + 2 * y + c],
                send_sem=send_sems.at[m - 1], recv_sem=recv_sems.at[m - 1], device_id=(px, py, pc), device_id_type=MESH))
        return copies

    return Exchange([vec], [], [jax.ShapeDtypeStruct((N_DEV,) + vec.shape, vec.dtype)], N_DEV - 1, _swap(copies_of))


def sum_cores(name, grads, others, core, after=()):
    k = len(grads)

    def body(core_ref, *refs):
        for j in range(k):
            out_ref = refs[2 * k + len(after) + j]
            out_ref[...] = (refs[j][:, 0].astype(F32) + refs[k + j][...].astype(F32)).astype(out_ref.dtype)

    mine = [pl.BlockSpec((2, 1) + o.shape[1:], lambda i, core_ref: (i, core_ref[0], 0, 0)) for o in others]
    theirs = [pl.BlockSpec((2,) + o.shape[1:], lambda i, core_ref: (i, 0, 0)) for o in others]
    return pl.pallas_call(
        body, name=name,
        grid_spec=pltpu.PrefetchScalarGridSpec(
            num_scalar_prefetch=1, grid=(2,), in_specs=mine + theirs + [ANY] * len(after), out_specs=theirs),
        out_shape=[jax.ShapeDtypeStruct(o.shape, o.dtype) for o in others],
        compiler_params=_params(),
    )(core, *[g.reshape((4, 2) + g.shape[1:]) for g in grads], *others, *after)


def sum_owned(name, grad, others, dev_ids, after=()):
    _, r, w = grad.shape

    def body(ids_ref, *refs):
        acc = refs[0][0]
        for k in range(1, N_DEV):
            acc = acc + refs[k][0]
        refs[-1][...] = acc

    def pick(k):
        return pl.BlockSpec((1, r, w), lambda i, ids_ref: (ids_ref[k], 0, 0))

    return pl.pallas_call(
        body, name=name,
        grid_spec=pltpu.PrefetchScalarGridSpec(
            num_scalar_prefetch=1, grid=(1,), in_specs=[pick(k) for k in range(N_DEV)] + [ANY] * len(after),
            out_specs=pl.BlockSpec((r, w), lambda i, ids_ref: (ids_ref[0], 0))),
        out_shape=jax.ShapeDtypeStruct((N_DEV * r, w), F32),
        compiler_params=_params(),
    )(dev_ids, grad, *([others] * (N_DEV - 1)), *after)


def _adamw_update(w, g, m, v):
    nm = ADAM_B1 * m + np.float32(1.0 - ADAM_B1) * g
    nv = ADAM_B2 * v + np.float32(1.0 - ADAM_B2) * (g * g)
    m_hat = nm / np.float32(1.0 - ADAM_B1 ** ADAM_STEP)
    v_hat = nv / np.float32(1.0 - ADAM_B2 ** ADAM_STEP)
    return -ADAM_LR * (m_hat / (jnp.sqrt(v_hat) + ADAM_EPS) + ADAM_WD * w), nm, nv


def adamw_of_sums(name, parts, others, chip_ids, ws, ms, vs, after):
    n = len(parts)
    halves = 2

    def body(ids_ref, *refs):
        outs = refs[7 * n + 1:]
        for j in range(n):
            p_ref, a_ref, b_ref, c_ref, w_ref, m_ref, v_ref = refs[7 * j:7 * j + 7]
            g = ((p_ref[0].astype(F32) + a_ref[0].astype(F32)) + b_ref[0].astype(F32)) + c_ref[0].astype(F32)
            outs[4 * j][...] = g
            outs[4 * j + 1][...], outs[4 * j + 2][...], outs[4 * j + 3][...] = _adamw_update(w_ref[...], g, m_ref[...], v_ref[...])

    in_specs, out_specs, out_shape, operands = [], [], [], []
    for part, other, w, m, v in zip(parts, others, ws, ms, vs):
        _, r, wd = part.shape
        rows = r // halves
        whole = pl.BlockSpec((rows, wd), lambda i, ids_ref: (i, 0))
        in_specs += [pl.BlockSpec((1, rows, wd), lambda i, ids_ref, k=k: (ids_ref[k], i, 0)) for k in range(4)] + [whole] * 3
        out_specs += [whole] * 4
        out_shape += [jax.ShapeDtypeStruct((r, wd), F32)] * 4
        operands += [part, other, other, other, w, m, v]
    outs = pl.pallas_call(
        body, name=name,
        grid_spec=pltpu.PrefetchScalarGridSpec(
            num_scalar_prefetch=1, grid=(halves,), in_specs=in_specs + [ANY], out_specs=out_specs),
        out_shape=out_shape,
        compiler_params=_params(),
    )(chip_ids, *operands, after)
    return [tuple(outs[4 * j:4 * j + 4]) for j in range(n)]


def pack_small(parts):
    names = [name for name, _ in SMALL if name in parts]
    operands = [parts[name] for name in names]
    first_row, at = {}, 0
    for name, size in SMALL:
        first_row[name] = at // 128
        at += size
    sizes = dict(SMALL)

    def body(*refs):
        out_ref = refs[-1]
        out_ref[...] = jnp.zeros_like(out_ref)
        for name, ref in zip(names, refs):
            row = first_row[name]
            if name == "loss_sum":
                lane0 = lax.broadcasted_iota(jnp.int32, (1, 128), 1) == 0
                out_ref[row:row + 1, :] = jnp.where(lane0, ref[...], 0.0)
            else:
                rows = sizes[name] // 128
                out_ref[row:row + rows, :] = ref[...].reshape(rows, 128)

    vmem = pl.BlockSpec(memory_space=pltpu.VMEM)
    return pl.pallas_call(
        body, name="pack_small", in_specs=[vmem] * len(names), out_specs=vmem,
        out_shape=jax.ShapeDtypeStruct((SMALL_ROWS, 128), F32), compiler_params=_params(()),
    )(*operands)


LATE = "pre_mix_norm"


def adamw_small(packed_g, late_parts, ws, ms, vs):
    names = [name for name, _ in SMALL if name != "loss_sum"]
    k = len(names)
    shapes = [ws[name].shape[1:] if ws[name].ndim > 2 else ws[name].shape for name in names]
    first_row, at = [], 0
    for name, size in SMALL:
        first_row.append(at // 128)
        at += size

    def body(g_ref, late_ref, *refs):
        w_refs, m_refs, v_refs, outs = refs[:k], refs[k:2 * k], refs[2 * k:3 * k], refs[3 * k:]
        for i, (name, size) in enumerate(SMALL[:k]):
            if name == LATE:
                g = late_ref[0]
                for j in range(1, N_DEV):
                    g = g + late_ref[j]
            else:
                g = g_ref[first_row[i]:first_row[i] + size // 128, :].reshape(shapes[i])
            outs[i][...] = g
            outs[k + i][...], outs[2 * k + i][...], outs[3 * k + i][...] = _adamw_update(
                w_refs[i][...], g, m_refs[i][...], v_refs[i][...])
        outs[4 * k][...] = g_ref[first_row[k]:first_row[k] + 1, 0:1]

    vmem = pl.BlockSpec(memory_space=pltpu.VMEM)
    operands = [t[name].reshape(shape) for t in (ws, ms, vs) for name, shape in zip(names, shapes)]
    outs = pl.pallas_call(
        body, name="adamw_small", in_specs=[vmem] * (2 + 3 * k), out_specs=[vmem] * (4 * k + 1),
        out_shape=[jax.ShapeDtypeStruct(shape, F32) for _ in range(4) for shape in shapes] + [jax.ShapeDtypeStruct((1, 1), F32)],
        compiler_params=_params(()),
    )(packed_g, late_parts, *operands)
    tables = [{name: outs[j * k + i].reshape(ws[name].shape) for i, name in enumerate(names)} for j in range(4)]
    return (*tables, outs[4 * k])


def kernel(x, positions, pre_mix_norm, w_in, sgu_ln_gain, sgu_ln_bias, sgu_w_spatial, sgu_b_spatial, attn_out_norm, sgu_out_norm, w_out, post_mix_norm, pre_ffn_norm, w_gate, w_up, w_down, post_ffn_norm, loss_target, m_pre_mix_norm, m_w_in, m_sgu_ln_gain, m_sgu_ln_bias, m_sgu_w_spatial, m_sgu_b_spatial, m_attn_out_norm, m_sgu_out_norm, m_w_out, m_post_mix_norm, m_pre_ffn_norm, m_w_gate, m_w_up, m_w_down, m_post_ffn_norm, v_pre_mix_norm, v_w_in, v_sgu_ln_gain, v_sgu_ln_bias, v_sgu_w_spatial, v_sgu_b_spatial, v_attn_out_norm, v_sgu_out_norm, v_w_out, v_post_mix_norm, v_pre_ffn_norm, v_w_gate, v_w_up, v_w_down, v_post_ffn_norm):
    small_w = dict(pre_mix_norm=pre_mix_norm, sgu_ln_gain=sgu_ln_gain, sgu_ln_bias=sgu_ln_bias, sgu_w_spatial=sgu_w_spatial,
                   sgu_b_spatial=sgu_b_spatial, attn_out_norm=attn_out_norm, sgu_out_norm=sgu_out_norm,
                   post_mix_norm=post_mix_norm, pre_ffn_norm=pre_ffn_norm, post_ffn_norm=post_ffn_norm)
    small_m = dict(pre_mix_norm=m_pre_mix_norm, sgu_ln_gain=m_sgu_ln_gain, sgu_ln_bias=m_sgu_ln_bias, sgu_w_spatial=m_sgu_w_spatial,
                   sgu_b_spatial=m_sgu_b_spatial, attn_out_norm=m_attn_out_norm, sgu_out_norm=m_sgu_out_norm,
                   post_mix_norm=m_post_mix_norm, pre_ffn_norm=m_pre_ffn_norm, post_ffn_norm=m_post_ffn_norm)
    small_v = dict(pre_mix_norm=v_pre_mix_norm, sgu_ln_gain=v_sgu_ln_gain, sgu_ln_bias=v_sgu_ln_bias, sgu_w_spatial=v_sgu_w_spatial,
                   sgu_b_spatial=v_sgu_b_spatial, attn_out_norm=v_attn_out_norm, sgu_out_norm=v_sgu_out_norm,
                   post_mix_norm=v_post_mix_norm, pre_ffn_norm=v_pre_ffn_norm, post_ffn_norm=v_post_ffn_norm)

    x2d = x[0]
    target = loss_target[0]
    pos_col = positions.reshape(SEQ, 1)
    rot = _rot_consts()
    w_sp = sgu_w_spatial[0]
    bfull = jnp.repeat(sgu_b_spatial[0].T, HEAD_DIM, axis=1)

    x_i, y_i, c_i = (lax.axis_index(a).astype(jnp.int32) for a in MESH_AXES)
    dev = 4 * x_i + 2 * y_i + c_i
    core = c_i.reshape(1)
    chip = 2 * x_i + y_i
    chip_ids = jnp.stack([chip, chip ^ 1, chip ^ 2, chip ^ 3])
    dev_ids = jnp.stack([dev ^ m for m in range(N_DEV)])

    def gathered(name, bufs):
        return by_sequencer(name, [gather(bufs)], TO_GATHER)[0][0]

    def from_sibling(name, grads):
        return by_sequencer(name, [to_sibling(grads)], TO_SIBLING)[0][1]

    def from_chips(name, parts):
        return by_sequencer(name, [to_chips(parts)], TO_CHIPS)[0][1]

    (w_in_t,) = place_shards("place_w_in", [w_in[0].T], dev.reshape(1))
    (w_in_t,) = gathered("gather_w_in", [w_in_t])
    w_gate_t, w_up_t, w_out_f, w_down_f = place_shards(
        "place_weights", [w_gate[0].T, w_up[0].T, w_out[0], w_down[0]], dev.reshape(1))
    (w_out_f,) = gathered("gather_w_out", [w_out_f])
    w_gate_t, w_up_t = gathered("gather_w_gate_up", [w_gate_t, w_up_t])
    (w_down_f,) = gathered("gather_w_down", [w_down_f])

    h1, u, vs, q, k, v = in_proj(x2d, pos_col, pre_mix_norm, w_in_t, rot)
    attn_r, lse, attn = attn_fwd(q, k, v)
    (sgu,) = sgu_fwd(u, vs, sgu_ln_gain, sgu_ln_bias, w_sp, bfull)
    mix, y, x2, h2 = out_proj(attn, sgu, x2d, attn_out_norm, sgu_out_norm, w_out_f, post_mix_norm, pre_ffn_norm)
    gate, up, act = ffn_up(h2, w_gate_t, w_up_t)
    df, dx3, d_post_ffn, sq_err = ffn_down_loss(act, w_down_f, x2, post_ffn_norm, target)

    g_w_down = weight_grad("grad_w_down", act, df)
    (s_down,) = from_sibling("w_down_to_sibling", [g_w_down])
    dgate, dup, dx2, dy, d_pre_ffn, d_post_mix = ffn_bwd(
        df, w_down_f, gate, up, w_gate_t, w_up_t, x2, pre_ffn_norm, dx3, y, post_mix_norm, after=[g_w_down])
    (p_down,) = sum_cores("sum_cores_down", [g_w_down], [s_down], core, after=[dy])
    (c_down,) = from_chips("w_down_to_chips", [p_down])
    g_w_gate, g_w_up = weight_grads("grad_w_gate_up", [dgate, dup], h2, after=[p_down])
    s_gate, s_up = from_sibling("w_gate_up_to_sibling", [g_w_gate, g_w_up])
    g_w_out = weight_grad("grad_w_out", mix, dy, after=[g_w_up, c_down])
    p_gate, p_up = sum_cores("sum_cores_gate_up", [g_w_gate, g_w_up], [s_gate, s_up], core, after=[g_w_out])
    c_gate, c_up = from_chips("w_gate_up_to_chips", [p_gate, p_up])
    (s_out,) = from_sibling("w_out_to_sibling", [g_w_out])
    dsgu, d_attn_out, d_sgu_out, dattn_r = mix_bwd(dy, w_out_f, attn, sgu, attn_out_norm, sgu_out_norm, after=[p_gate, p_up])
    du, dvs, d_ln_gain, d_ln_bias, d_w_sp, d_bfull = sgu_bwd(u, vs, dsgu, sgu_ln_gain, sgu_ln_bias, w_sp, bfull)

    d_b_sp = d_bfull.reshape(CHUNK, N_GROUPS, HEAD_DIM).sum(axis=-1).T
    small_g = pack_small(dict(sgu_ln_gain=d_ln_gain, sgu_ln_bias=d_ln_bias, sgu_w_spatial=d_w_sp, sgu_b_spatial=d_b_sp,
                              attn_out_norm=d_attn_out, sgu_out_norm=d_sgu_out, post_mix_norm=d_post_mix,
                              pre_ffn_norm=d_pre_ffn, post_ffn_norm=d_post_ffn, loss_sum=sq_err))
    small_g = small_g.reshape(N_DEV, SMALL_ROWS // N_DEV, 128)
    ((_, (o_small,)),) = by_sequencer("small_to_owners", [to_owners(small_g)], TO_ALL)

    dq, dk, dv = attn_bwd(q, k, v, attn_r, lse, dattn_r, _to_residue_order(pos_col), rot)
    summed_small = sum_owned("sum_small", small_g, o_small, dev_ids, after=[dq])
    (all_small,) = gathered("gather_small_grads", [summed_small])
    (p_out,) = sum_cores("sum_cores_out", [g_w_out], [s_out], core, after=[dq])
    (c_out,) = from_chips("w_out_to_chips", [p_out])
    dq, dk, dv = (_from_residue_order(t) for t in (dq, dk, dv))
    dproj = [dq, dk, dv, du, dvs]
    g_w_in = weight_grad_of_parts("grad_w_in", dproj, h1, after=[c_gate, c_up])
    (s_in,) = from_sibling("w_in_to_sibling", [g_w_in])
    grad_x, d_pre_mix = in_bwd(dproj, w_in_t, x2d, pre_mix_norm, dx2, after=[g_w_in, all_small])
    (p_in,) = sum_cores("sum_cores_in", [g_w_in], [s_in], core, after=[d_pre_mix, c_out])
    (_, (c_in,)), (_, (late_parts,)) = by_sequencer(
        "last_sums_to_owners", [to_chips([p_in]), to_everyone(d_pre_mix)], TO_ALL)
    late_parts = lax.dynamic_update_slice(late_parts, d_pre_mix[None], (dev, 0, 0))

    def same(t):
        return t

    def turned(t):
        return t.T

    big, last = {}, p_in
    for call, weights in (("adamw_ffn", (("w_down", w_down, p_down, c_down, m_w_down, v_w_down, same),
                                         ("w_gate", w_gate, p_gate, c_gate, m_w_gate, v_w_gate, turned),
                                         ("w_up", w_up, p_up, c_up, m_w_up, v_w_up, turned))),
                          ("adamw_w_out", (("w_out", w_out, p_out, c_out, m_w_out, v_w_out, same),)),
                          ("adamw_w_in", (("w_in", w_in, p_in, c_in, m_w_in, v_w_in, turned),))):
        results = adamw_of_sums(call, [p for _, _, p, _, _, _, _ in weights], [c for _, _, _, c, _, _, _ in weights], chip_ids,
                                [turn(w[0]) for _, w, _, _, _, _, turn in weights], [turn(m[0]) for _, _, _, _, m, _, turn in weights],
                                [turn(vv[0]) for _, _, _, _, _, vv, turn in weights], last)
        for (name, _, _, _, _, _, turn), outs in zip(weights, results):
            big[name] = tuple(turn(t)[None] for t in outs)
        last = results[-1][0]
    sg, sd, snm, snv, loss_sum = adamw_small(all_small, late_parts, small_w, small_m, small_v)
    loss = loss_sum[0, 0] * np.float32(0.5 / D_MODEL)

    names = ["pre_mix_norm", "w_in", "sgu_ln_gain", "sgu_ln_bias", "sgu_w_spatial", "sgu_b_spatial", "attn_out_norm",
             "sgu_out_norm", "w_out", "post_mix_norm", "pre_ffn_norm", "w_gate", "w_up", "w_down", "post_ffn_norm"]
    outs = [loss, grad_x[None]]
    for i, table in enumerate((sg, sd, snm, snv)):
        for name in names:
            outs.append(big[name][i] if name in big else table[name])
    return tuple(outs)
```

```python
import numpy as np
import jax
import jax.numpy as jnp
from jax import lax
from jax.experimental import pallas as pl
from jax.experimental.pallas import tpu as pltpu
from jax.experimental.pallas import tpu_sc as plsc

F32 = jnp.float32
BF16 = jnp.bfloat16

SEQ = 2048
D_MODEL = 1024
ATTN_W = 512
SGU_W = 512
HEAD_DIM = 64
N_GROUPS = 8
CHUNK = 128
D_FF = 2816
IN_W = 3 * ATTN_W + 2 * SGU_W
DILATIONS = (1, 4, 16)
ROPE_THETA = 500000.0
ROT_DIM = 16
ROT_HALF = 8
RMS_EPS = 1e-6
LN_EPS = 1e-5
Q_SCALE = 0.125
NEG = -1e30

N_DEV = 8
MESH_AXES = ("x", "y", "c")
MESH = pl.DeviceIdType.MESH

ADAM_LR = 0.001
ADAM_B1 = 0.9
ADAM_B2 = 0.999
ADAM_EPS = 1e-08
ADAM_WD = 0.01
ADAM_STEP = 10

VMEM_LIMIT = 60 * 1024 * 1024
ANY = pl.BlockSpec(memory_space=pl.ANY)

SMALL = (("pre_mix_norm", 1024), ("sgu_ln_gain", 512), ("sgu_ln_bias", 512), ("sgu_w_spatial", 8 * 128 * 128),
         ("sgu_b_spatial", 1024), ("attn_out_norm", 512), ("sgu_out_norm", 512), ("post_mix_norm", 1024),
         ("pre_ffn_norm", 1024), ("post_ffn_norm", 1024), ("loss_sum", 1))
SMALL_ROWS = 1152


def _params(sem=("arbitrary",)):
    return pltpu.CompilerParams(dimension_semantics=sem, vmem_limit_bytes=VMEM_LIMIT)


def _dot(a, b):
    return jnp.dot(a, b, preferred_element_type=F32)


def _dot_nt(a, b):
    return lax.dot_general(a, b, (((1,), (1,)), ((), ())), preferred_element_type=F32)


def _dot_tn(a, b):
    return lax.dot_general(a, b, (((0,), (0,)), ((), ())), preferred_element_type=F32)


def _rms(z):
    return lax.rsqrt(jnp.mean(z * z, axis=-1, keepdims=True) + RMS_EPS)


def _rms_bwd(z, gain, d):
    r = _rms(z)
    n = z * r
    dn = d * gain
    dz = r * (dn - n * jnp.mean(dn * n, axis=-1, keepdims=True))
    return dz, jnp.sum(d * n, axis=0, keepdims=True)


def _gelu(z):
    return 0.5 * z * (1.0 + lax.erf(z * np.float32(1.0 / np.sqrt(2.0))))


def _gelu_grad(z):
    cdf = 0.5 * (1.0 + lax.erf(z * np.float32(1.0 / np.sqrt(2.0))))
    return cdf + z * jnp.exp(-0.5 * z * z) * np.float32(1.0 / np.sqrt(2.0 * np.pi))


def _rot_tables(pos_col, invf, ma, mb):
    ang = pos_col.astype(F32) * invf
    s = jnp.sin(ang)
    return jnp.cos(ang), s * ma, s * mb


def _rot(t, c, sa, sb):
    return t * c + pltpu.roll(t, 120, 1) * sa + pltpu.roll(t, 8, 1) * sb


def _rot_t(d, c, sa, sb):
    return d * c + pltpu.roll(d * sa, 8, 1) + pltpu.roll(d * sb, 120, 1)


def _rot_consts():
    lane = np.arange(128) % HEAD_DIM
    inv_freq = (np.float32(ROPE_THETA) ** (-np.arange(0, ROT_DIM, 2, dtype=np.float32) / np.float32(ROT_DIM))).astype(np.float32)
    invf = np.where(lane < ROT_DIM, inv_freq[lane % ROT_HALF], 0.0).astype(np.float32)
    ma = np.where(lane < ROT_HALF, -1.0, 0.0).astype(np.float32)
    mb = np.where((lane >= ROT_HALF) & (lane < ROT_DIM), 1.0, 0.0).astype(np.float32)
    return jnp.asarray(invf[None]), jnp.asarray(ma[None]), jnp.asarray(mb[None])


def _row_spec(tm, w):
    return pl.BlockSpec((tm, w), lambda i: (i, 0))


def _full_spec(shape):
    return pl.BlockSpec(shape, lambda i: (0,) * len(shape))


def _weight_spec(shape):
    return pl.BlockSpec(shape, lambda i: (0,) * len(shape), pipeline_mode=pl.Buffered(1))


RES = 16


def _residue_scratch(n_arrays, tm, width):
    return [pltpu.VMEM((2, n_arrays, tm // RES, RES, width), F32), pltpu.SemaphoreType.DMA((2, n_arrays, RES))]


def _to_residue_rows(tiles, outs, scratch, sems, tm, n_steps):
    i = pl.program_id(0)
    slot = i % 2
    per = tm // RES

    def copies(step, s):
        return [pltpu.make_async_copy(scratch.at[s, a, :, b, :],
                                      outs[a].at[pl.ds(pl.multiple_of(b * (SEQ // RES) + per * step, per), per), :],
                                      sems.at[s, a, b]) for a in range(len(outs)) for b in range(RES)]

    @pl.when(i >= 2)
    def _():
        for cp in copies(i - 2, slot):
            cp.wait()

    for a, tile in enumerate(tiles):
        scratch[slot, a] = tile.reshape(per, RES, tile.shape[-1])
    for cp in copies(i, slot):
        cp.start()

    @pl.when(i == n_steps - 1)
    def _():
        for cp in copies(i - 1, 1 - slot) + copies(i, slot):
            cp.wait()


def in_proj(x, pos_col, g1, w_in_t, rot):
    tm = 512
    n_steps = SEQ // tm

    def body(x_ref, pos_ref, g_ref, w_ref, invf_ref, ma_ref, mb_ref, h_ref, u_ref, vs_ref, q_ref, k_ref, v_ref, scratch, sems):
        xf = x_ref[...]
        h = (xf * _rms(xf) * g_ref[...]).astype(BF16)
        h_ref[...] = h
        proj = _dot_nt(h, w_ref[...])
        c, sa, sb = _rot_tables(pos_ref[...], invf_ref[...], ma_ref[...], mb_ref[...])
        slabs = range(ATTN_W // 128)
        q = jnp.concatenate([_rot(proj[:, j * 128:(j + 1) * 128], c, sa, sb) * Q_SCALE for j in slabs], axis=1)
        k = jnp.concatenate([_rot(proj[:, ATTN_W + j * 128:ATTN_W + (j + 1) * 128], c, sa, sb) for j in slabs], axis=1)
        u_ref[...] = proj[:, 3 * ATTN_W:3 * ATTN_W + SGU_W]
        vs_ref[...] = proj[:, 3 * ATTN_W + SGU_W:]
        _to_residue_rows([q, k, proj[:, 2 * ATTN_W:3 * ATTN_W]], [q_ref, k_ref, v_ref], scratch, sems, tm, n_steps)

    act = jax.ShapeDtypeStruct((SEQ, 512), F32)
    return _call(
        "in_proj", body, n_steps,
        [_row_spec(tm, D_MODEL), _row_spec(tm, 1), _full_spec((1, D_MODEL)), _weight_spec((IN_W, D_MODEL)),
         _full_spec((1, 128)), _full_spec((1, 128)), _full_spec((1, 128))],
        [_row_spec(tm, D_MODEL)] + [_row_spec(tm, 512)] * 2 + [ANY] * 3,
        [jax.ShapeDtypeStruct((SEQ, D_MODEL), BF16)] + [act] * 5,
        (x, pos_col, g1, w_in_t, *rot), scratch_shapes=_residue_scratch(3, tm, ATTN_W))


def _to_residue_order(t):
    return t.reshape(SEQ // RES, RES, -1).transpose(1, 0, 2).reshape(t.shape)


def _from_residue_order(t):
    return t.reshape(RES, SEQ // RES, -1).transpose(1, 0, 2).reshape(t.shape)


def _block_rows(d, r, n):
    if d == 16:
        slices = [(128 * r, 128)]
    elif d == 4:
        slices = [(128 * (4 * b + r) + 32 * n, 32) for b in range(4)]
    else:
        slices = [(128 * b + 8 * n, 8) for b in range(RES)]
    return [(s if isinstance(s, int) else pl.multiple_of(s, z), z) for s, z in slices]


def _block_step(d, i):
    if d == 16:
        return i
    if d == 4:
        return 4 * (i & 31) + (i >> 5)
    return 16 * (i & 7) + (i >> 3)


def _attn_masks(d):
    row2 = _block_step(d, lax.broadcasted_iota(jnp.int32, (128, 256), 0))
    col2 = lax.broadcasted_iota(jnp.int32, (128, 256), 1)
    key2 = _block_step(d, col2 & 127)
    mask2 = jnp.logical_or(jnp.logical_and(col2 < 128, key2 >= row2), jnp.logical_and(col2 >= 128, key2 <= row2))
    row1 = _block_step(d, lax.broadcasted_iota(jnp.int32, (128, 128), 0))
    col1 = lax.broadcasted_iota(jnp.int32, (128, 128), 1)
    return col1 < HEAD_DIM, _block_step(d, col1) <= row1, mask2


def _load_rows(ref, slices):
    parts = [ref[pl.ds(s, z), :] for s, z in slices]
    return parts[0] if len(parts) == 1 else jnp.concatenate(parts, axis=0)


def _for_each_group(fn):
    for p, d in enumerate(DILATIONS):
        masks = _attn_masks(d)
        if d == 16:
            def group(i, carry, p=p, masks=masks):
                fn(p, masks, [(_block_rows(16, 8 * i + g, 0), None) for g in range(8)])
                return carry

            lax.fori_loop(0, 2, group, 0)
        elif d == 4:
            fn(p, masks, [(_block_rows(4, r, 0), None) for r in range(4)])

            def group(i, carry, p=p, masks=masks):
                blocks = [6 * i + g for g in range(6)]
                fn(p, masks, [(_block_rows(4, j % 4, 1 + j // 4), _block_rows(4, j % 4, j // 4)) for j in blocks])
                return carry

            lax.fori_loop(0, 2, group, 0)
        else:
            fn(p, masks, [(_block_rows(1, 0, 0), None)])

            def group(i, carry, p=p, masks=masks):
                fn(p, masks, [(_block_rows(1, 0, 5 * i + g + 1), _block_rows(1, 0, 5 * i + g)) for g in range(5)])
                return carry

            lax.fori_loop(0, 3, group, 0)


def attn_fwd(q, k, v):
    def body(q_ref, k_ref, v_ref, o_ref, lse_ref, nat_ref, op_ref, lp_ref, sems):
        def group(p, masks, blocks):
            head0, mask1, mask2 = masks
            heads = (head0, jnp.logical_not(head0))
            keys = [rows if prev is None else prev + rows for rows, prev in blocks]
            mask = [mask1 if prev is None else mask2 for _, prev in blocks]
            qb = [_load_rows(q_ref, rows) for rows, _ in blocks]
            kk = [_load_rows(k_ref, ks).astype(BF16) for ks in keys]
            vv = [_load_rows(v_ref, ks).astype(BF16) for ks in keys]
            chains = [(g, hm) for g in range(len(blocks)) for hm in heads]
            s = [jnp.where(mask[g], _dot_nt(jnp.where(hm, qb[g], 0.0).astype(BF16), kk[g]), NEG) for g, hm in chains]
            m = [jnp.max(t, axis=-1, keepdims=True) for t in s]
            e = [jnp.exp(t - mt) for t, mt in zip(s, m)]
            l = [jnp.sum(t, axis=-1, keepdims=True) for t in e]
            pv = [_dot(t.astype(BF16), vv[g]) for t, (g, _) in zip(e, chains)]
            for g, (rows, _) in enumerate(blocks):
                o_blk = jnp.where(head0, pv[2 * g] / l[2 * g], pv[2 * g + 1] / l[2 * g + 1])
                l_blk = jnp.where(head0, jnp.broadcast_to(m[2 * g] + jnp.log(l[2 * g]), (128, 128)),
                                  jnp.broadcast_to(m[2 * g + 1] + jnp.log(l[2 * g + 1]), (128, 128)))
                at = 0
                for start, size in rows:
                    op_ref[p, pl.ds(start, size), :] = o_blk[at:at + size]
                    lp_ref[p, pl.ds(start, size), :] = l_blk[at:at + size]
                    at += size

        _for_each_group(group)

        def combine(i, carry):
            rows = pl.ds(pl.multiple_of(i * 256, 256), 256)
            ls = [lp_ref[p, rows, :] for p in range(3)]
            m = jnp.maximum(jnp.maximum(ls[0], ls[1]), ls[2])
            lse = m + jnp.log(jnp.exp(ls[0] - m) + jnp.exp(ls[1] - m) + jnp.exp(ls[2] - m))
            o = jnp.zeros((256, 128), F32)
            for p in range(3):
                o = o + jnp.exp(ls[p] - lse) * op_ref[p, rows, :]
            o_ref[rows, :] = o
            lse_ref[rows, :] = lse
            return carry

        lax.fori_loop(0, SEQ // 256, combine, 0)

        lanes = pl.ds(pl.multiple_of(pl.program_id(0) * 128, 128), 128)
        back = [pltpu.make_async_copy(o_ref.at[pl.ds(b * (SEQ // RES), SEQ // RES), :], nat_ref.at[:, b, lanes], sems.at[b])
                for b in range(RES)]
        for cp in back:
            cp.start()
        for cp in back:
            cp.wait()

    slab = pl.BlockSpec((SEQ, 128), lambda i: (0, i))
    out = jax.ShapeDtypeStruct((SEQ, ATTN_W), F32)
    attn_r, lse, attn = _call(
        "attn_fwd", body, ATTN_W // 128, [slab] * 3, [slab] * 2 + [ANY],
        [out, out, jax.ShapeDtypeStruct((SEQ // RES, RES, ATTN_W), F32)], (q, k, v),
        scratch_shapes=[pltpu.VMEM((3, SEQ, 128), F32), pltpu.VMEM((3, SEQ, 128), F32), pltpu.SemaphoreType.DMA((RES,))])
    return attn_r, lse, attn.reshape(SEQ, ATTN_W)


def _causal_weights(w_ref):
    row = lax.broadcasted_iota(jnp.int32, (CHUNK, CHUNK), 0)
    col = lax.broadcasted_iota(jnp.int32, (CHUNK, CHUNK), 1)
    return [jnp.where(col <= row, w_ref[g], 0.0).astype(BF16) for g in range(N_GROUPS)], col <= row


def _sgu_chunk_fwd(u, vs, lg, lb, wc, bfull, head0):
    ug = _gelu(u)
    vg = _gelu(vs)
    xc = vg - jnp.mean(vg, axis=-1, keepdims=True)
    rstd = lax.rsqrt(jnp.mean(xc * xc, axis=-1, keepdims=True) + LN_EPS)
    xhat = xc * rstd
    vn = xhat * lg + lb
    mixed = []
    for gp in range(SGU_W // 128):
        vp = vn[:, gp * 128:(gp + 1) * 128].astype(BF16)
        mixed.append(jnp.where(head0, _dot(wc[2 * gp], vp), _dot(wc[2 * gp + 1], vp)))
    ms = jnp.concatenate(mixed, axis=1) + bfull
    return ug, xhat, rstd, vn, ms


def sgu_fwd(u, vs, lg, lb, w_sp, bfull):
    cpb = 4

    def body(u_ref, vs_ref, lg_ref, lb_ref, w_ref, b_ref, o_ref):
        wc, _ = _causal_weights(w_ref)
        head0 = lax.broadcasted_iota(jnp.int32, (CHUNK, 128), 1) < HEAD_DIM
        for ci in range(cpb):
            rows = pl.ds(ci * CHUNK, CHUNK)
            ug, _, _, _, ms = _sgu_chunk_fwd(u_ref[rows, :], vs_ref[rows, :], lg_ref[...], lb_ref[...], wc, b_ref[...], head0)
            o_ref[rows, :] = ug * ms

    tm = cpb * CHUNK
    return _call(
        "sgu_fwd", body, SEQ // tm,
        [_row_spec(tm, SGU_W), _row_spec(tm, SGU_W), _full_spec((1, SGU_W)), _full_spec((1, SGU_W)),
         _full_spec((N_GROUPS, CHUNK, CHUNK)), _full_spec((CHUNK, SGU_W))],
        [_row_spec(tm, SGU_W)], [jax.ShapeDtypeStruct((SEQ, SGU_W), F32)],
        (u, vs, lg, lb, w_sp, bfull))


def out_proj(attn, sgu, x, ga, gs, w_out, gpm, gpf):
    tm = 512

    def body(a_ref, s_ref, x_ref, ga_ref, gs_ref, w_ref, gpm_ref, gpf_ref, mix_ref, y_ref, x2_ref, h2_ref):
        a = a_ref[...]
        s = s_ref[...]
        an = (a * _rms(a) * ga_ref[...]).astype(BF16)
        sn = (s * _rms(s) * gs_ref[...]).astype(BF16)
        mix_ref[:, :ATTN_W] = an
        mix_ref[:, ATTN_W:] = sn
        y = _dot(an, w_ref[:ATTN_W, :]) + _dot(sn, w_ref[ATTN_W:, :])
        y_ref[...] = y
        x2 = x_ref[...] + y * _rms(y) * gpm_ref[...]
        x2_ref[...] = x2
        h2_ref[...] = (x2 * _rms(x2) * gpf_ref[...]).astype(BF16)

    wide = jax.ShapeDtypeStruct((SEQ, D_MODEL), F32)
    wide16 = jax.ShapeDtypeStruct((SEQ, D_MODEL), BF16)
    return _call(
        "out_proj", body, SEQ // tm,
        [_row_spec(tm, ATTN_W), _row_spec(tm, SGU_W), _row_spec(tm, D_MODEL), _full_spec((1, ATTN_W)),
         _full_spec((1, SGU_W)), _weight_spec((D_MODEL, D_MODEL)), _full_spec((1, D_MODEL)), _full_spec((1, D_MODEL))],
        [_row_spec(tm, D_MODEL)] * 4, [wide16, wide, wide, wide16],
        (attn, sgu, x, ga, gs, w_out, gpm, gpf))


def ffn_up(h2, w_gate_t, w_up_t):
    tm = 256

    def body(h_ref, wg_ref, wu_ref, g_ref, u_ref, a_ref):
        h = h_ref[...]
        g = _dot_nt(h, wg_ref[...])
        u = _dot_nt(h, wu_ref[...])
        g_ref[...] = g.astype(BF16)
        u_ref[...] = u.astype(BF16)
        a_ref[...] = (g * jax.nn.sigmoid(g) * u).astype(BF16)

    ff = jax.ShapeDtypeStruct((SEQ, D_FF), BF16)
    return _call(
        "ffn_up", body, SEQ // tm,
        [_row_spec(tm, D_MODEL), _weight_spec((D_FF, D_MODEL)), _weight_spec((D_FF, D_MODEL))],
        [_row_spec(tm, D_FF)] * 3, [ff, ff, jax.ShapeDtypeStruct((SEQ, D_FF), BF16)],
        (h2, w_gate_t, w_up_t))


def ffn_down_loss(act, w_down, x2, gpo, target):
    tm = 512

    def body(a_ref, w_ref, x2_ref, g_ref, t_ref, df_ref, dx3_ref, dg_ref, loss_ref):
        f = _dot(a_ref[...], w_ref[...])
        gain = g_ref[...]
        err = x2_ref[...] + f * _rms(f) * gain - t_ref[...]
        dx3 = err * np.float32(1.0 / D_MODEL)
        dx3_ref[...] = dx3
        df, dg = _rms_bwd(f, gain, dx3)
        df_ref[...] = df.astype(BF16)

        @pl.when(pl.program_id(0) == 0)
        def _():
            dg_ref[...] = jnp.zeros_like(dg_ref)
            loss_ref[...] = jnp.zeros_like(loss_ref)

        dg_ref[...] += dg
        loss_ref[...] += jnp.sum(err * err, axis=(0, 1), keepdims=True)

    return pl.pallas_call(
        body, name="ffn_down_loss", grid=(SEQ // tm,),
        in_specs=[_row_spec(tm, D_FF), _weight_spec((D_FF, D_MODEL)), _row_spec(tm, D_MODEL), _full_spec((1, D_MODEL)),
                  _row_spec(tm, D_MODEL)],
        out_specs=[_row_spec(tm, D_MODEL), _row_spec(tm, D_MODEL), _full_spec((1, D_MODEL)), _full_spec((1, 1))],
        out_shape=[jax.ShapeDtypeStruct((SEQ, D_MODEL), BF16), jax.ShapeDtypeStruct((SEQ, D_MODEL), F32),
                   jax.ShapeDtypeStruct((1, D_MODEL), F32), jax.ShapeDtypeStruct((1, 1), F32)],
        compiler_params=_params(),
    )(act, w_down, x2, gpo, target)


def ffn_bwd(df, w_down, gate, up, w_gate_t, w_up_t, x2, gpf, dx3, y, gpm, after=()):
    tm = 256

    def body(df_ref, wd_ref, g_ref, u_ref, wg_ref, wu_ref, x2_ref, gpf_ref, dx3_ref, y_ref, gpm_ref,
             dg_ref, du_ref, dx2_ref, dy_ref, dgpf_ref, dgpm_ref):
        dact = _dot_nt(df_ref[...], wd_ref[...])
        g = g_ref[...].astype(F32)
        s = jax.nn.sigmoid(g)
        dup = (dact * g * s).astype(BF16)
        dgate = (dact * u_ref[...].astype(F32) * (s * (1.0 + g * (1.0 - s)))).astype(BF16)
        du_ref[...] = dup
        dg_ref[...] = dgate
        dh2 = _dot(dgate, wg_ref[...]) + _dot(dup, wu_ref[...])
        dz, dgpf = _rms_bwd(x2_ref[...], gpf_ref[...], dh2)
        dx2 = dx3_ref[...] + dz
        dx2_ref[...] = dx2
        dy, dgpm = _rms_bwd(y_ref[...], gpm_ref[...], dx2)
        dy_ref[...] = dy.astype(BF16)

        @pl.when(pl.program_id(0) == 0)
        def _():
            dgpf_ref[...] = jnp.zeros_like(dgpf_ref)
            dgpm_ref[...] = jnp.zeros_like(dgpm_ref)

        dgpf_ref[...] += dgpf
        dgpm_ref[...] += dgpm

    vec = jax.ShapeDtypeStruct((1, D_MODEL), F32)
    ff16 = jax.ShapeDtypeStruct((SEQ, D_FF), BF16)
    return _call(
        "ffn_bwd", body, SEQ // tm,
        [_row_spec(tm, D_MODEL), _weight_spec((D_FF, D_MODEL)), _row_spec(tm, D_FF), _row_spec(tm, D_FF),
         _weight_spec((D_FF, D_MODEL)), _weight_spec((D_FF, D_MODEL)), _row_spec(tm, D_MODEL), _full_spec((1, D_MODEL)),
         _row_spec(tm, D_MODEL), _row_spec(tm, D_MODEL), _full_spec((1, D_MODEL))],
        [_row_spec(tm, D_FF), _row_spec(tm, D_FF), _row_spec(tm, D_MODEL), _row_spec(tm, D_MODEL),
         _full_spec((1, D_MODEL)), _full_spec((1, D_MODEL))],
        [ff16, ff16, jax.ShapeDtypeStruct((SEQ, D_MODEL), F32), jax.ShapeDtypeStruct((SEQ, D_MODEL), BF16), vec, vec],
        (df, w_down, gate, up, w_gate_t, w_up_t, x2, gpf, dx3, y, gpm), after=after)


def weight_grads(name, lhs, b, after=()):
    m, n, k = lhs[0].shape[1], b.shape[1], len(lhs)
    tr = 256

    def body(*refs):
        for a_ref, o_ref in zip(refs[:k], refs[k + 1:]):
            o_ref[...] = _dot_tn(a_ref[...], refs[k][...]).astype(BF16)

    outs = _call(
        name, body, m // tr, [pl.BlockSpec((SEQ, tr), lambda i: (0, i))] * k + [_weight_spec((SEQ, n))],
        [_row_spec(tr, n)] * k, [jax.ShapeDtypeStruct((m, n), BF16)] * k, (*lhs, b), after=after)
    return [out.reshape(N_DEV, m // N_DEV, n) for out in outs]


def weight_grad(name, a, b, after=()):
    return weight_grads(name, [a], b, after)[0]


def weight_grad_of_parts(name, parts, b, after=()):
    p, n, k = parts[0].shape[1], b.shape[1], len(parts)
    tr = 512
    per = p // tr

    def body(*refs):
        tile = pl.program_id(0)
        for j in range(k):
            @pl.when(tile // per == j)
            def _(j=j):
                refs[k + 1][...] = _dot_tn(refs[j][...], refs[k][...]).astype(BF16)

    def part_spec(j):
        return pl.BlockSpec((SEQ, tr), lambda i: (0, jnp.clip(i - per * j, 0, per - 1)))

    (out,) = _call(
        name, body, k * per, [part_spec(j) for j in range(k)] + [_weight_spec((SEQ, n))],
        [_row_spec(tr, n)], [jax.ShapeDtypeStruct((k * p, n), BF16)], (*parts, b), after=after)
    return out.reshape(N_DEV, k * p // N_DEV, n)


def mix_bwd(dy, w_out, attn, sgu, ga, gs, after=()):
    tm = 512
    n_steps = SEQ // tm

    def body(dy_ref, w_ref, a_ref, s_ref, ga_ref, gs_ref, ds_ref, dga_ref, dgs_ref, da_ref, scratch, sems):
        dy = dy_ref[...]
        da, dga = _rms_bwd(a_ref[...], ga_ref[...], _dot_nt(dy, w_ref[:ATTN_W, :]))
        ds, dgs = _rms_bwd(s_ref[...], gs_ref[...], _dot_nt(dy, w_ref[ATTN_W:, :]))
        ds_ref[...] = ds
        _to_residue_rows([da], [da_ref], scratch, sems, tm, n_steps)

        @pl.when(pl.program_id(0) == 0)
        def _():
            dga_ref[...] = jnp.zeros_like(dga_ref)
            dgs_ref[...] = jnp.zeros_like(dgs_ref)

        dga_ref[...] += dga
        dgs_ref[...] += dgs

    half = jax.ShapeDtypeStruct((SEQ, 512), F32)
    vec = jax.ShapeDtypeStruct((1, 512), F32)
    return _call(
        "mix_bwd", body, n_steps,
        [_row_spec(tm, D_MODEL), _weight_spec((D_MODEL, D_MODEL)), _row_spec(tm, 512), _row_spec(tm, 512),
         _full_spec((1, 512)), _full_spec((1, 512))],
        [_row_spec(tm, 512), _full_spec((1, 512)), _full_spec((1, 512)), ANY],
        [half, vec, vec, half], (dy, w_out, attn, sgu, ga, gs), scratch_shapes=_residue_scratch(1, tm, ATTN_W), after=after)


def sgu_bwd(u, vs, dsgu, lg, lb, w_sp, bfull):
    cpb = 4

    def body(u_ref, vs_ref, d_ref, lg_ref, lb_ref, w_ref, b_ref, du_ref, dvs_ref, dlg_ref, dlb_ref, dw_ref, db_ref):
        wc, causal = _causal_weights(w_ref)
        head0 = lax.broadcasted_iota(jnp.int32, (CHUNK, 128), 1) < HEAD_DIM
        lg = lg_ref[...]

        @pl.when(pl.program_id(0) == 0)
        def _():
            dlg_ref[...] = jnp.zeros_like(dlg_ref)
            dlb_ref[...] = jnp.zeros_like(dlb_ref)
            dw_ref[...] = jnp.zeros_like(dw_ref)
            db_ref[...] = jnp.zeros_like(db_ref)

        for ci in range(cpb):
            rows = pl.ds(ci * CHUNK, CHUNK)
            u = u_ref[rows, :]
            vs = vs_ref[rows, :]
            d = d_ref[rows, :]
            ug, xhat, rstd, vn, ms = _sgu_chunk_fwd(u, vs, lg, lb_ref[...], wc, b_ref[...], head0)
            du_ref[rows, :] = (d * ms * _gelu_grad(u)).astype(BF16)
            dms = d * ug
            db_ref[...] += dms
            dvn = []
            for gp in range(SGU_W // 128):
                dmp = dms[:, gp * 128:(gp + 1) * 128]
                dm0 = jnp.where(head0, dmp, 0.0).astype(BF16)
                dm1 = jnp.where(head0, 0.0, dmp).astype(BF16)
                vp = vn[:, gp * 128:(gp + 1) * 128].astype(BF16)
                dw_ref[2 * gp] += _dot_nt(dm0, vp)
                dw_ref[2 * gp + 1] += _dot_nt(dm1, vp)
                dvn.append(_dot_tn(wc[2 * gp], dm0) + _dot_tn(wc[2 * gp + 1], dm1))
            dvn = jnp.concatenate(dvn, axis=1)
            dlg_ref[...] += jnp.sum(dvn * xhat, axis=0, keepdims=True)
            dlb_ref[...] += jnp.sum(dvn, axis=0, keepdims=True)
            dxh = dvn * lg
            dvg = rstd * (dxh - jnp.mean(dxh, axis=-1, keepdims=True) - xhat * jnp.mean(dxh * xhat, axis=-1, keepdims=True))
            dvs_ref[rows, :] = (dvg * _gelu_grad(vs)).astype(BF16)

        @pl.when(pl.program_id(0) == pl.num_programs(0) - 1)
        def _():
            for g in range(N_GROUPS):
                dw_ref[g] = jnp.where(causal, dw_ref[g], 0.0)

    tm = cpb * CHUNK
    half16 = jax.ShapeDtypeStruct((SEQ, SGU_W), BF16)
    vec = jax.ShapeDtypeStruct((1, SGU_W), F32)
    return _call(
        "sgu_bwd", body, SEQ // tm,
        [_row_spec(tm, SGU_W)] * 3 + [_full_spec((1, SGU_W)), _full_spec((1, SGU_W)),
                                      _full_spec((N_GROUPS, CHUNK, CHUNK)), _full_spec((CHUNK, SGU_W))],
        [_row_spec(tm, SGU_W), _row_spec(tm, SGU_W), _full_spec((1, SGU_W)), _full_spec((1, SGU_W)),
         _full_spec((N_GROUPS, CHUNK, CHUNK)), _full_spec((CHUNK, SGU_W))],
        [half16, half16, vec, vec, jax.ShapeDtypeStruct((N_GROUPS, CHUNK, CHUNK), F32),
         jax.ShapeDtypeStruct((CHUNK, SGU_W), F32)],
        (u, vs, dsgu, lg, lb, w_sp, bfull))


def attn_bwd(q, k, v, o, lse, do, pos_col, rot):
    def body(q_ref, k_ref, v_ref, o_ref, lse_ref, do_ref, pos_ref, invf_ref, ma_ref, mb_ref,
             dq_ref, dk_ref, dv_ref, dqa_ref, dka_ref, dva_ref, dlt_ref, rot_ref):
        dqa_ref[...] = jnp.zeros_like(dqa_ref)
        dka_ref[...] = jnp.zeros_like(dka_ref)
        dva_ref[...] = jnp.zeros_like(dva_ref)

        def delta(i, carry):
            rows = pl.ds(pl.multiple_of(i * 256, 256), 256)
            prod = do_ref[rows, :] * o_ref[rows, :]
            h0 = lax.broadcasted_iota(jnp.int32, (256, 128), 1) < HEAD_DIM
            d0 = jnp.sum(jnp.where(h0, prod, 0.0), axis=-1, keepdims=True)
            d1 = jnp.sum(jnp.where(h0, 0.0, prod), axis=-1, keepdims=True)
            dlt_ref[rows, :] = jnp.where(h0, d0, d1)
            return carry

        lax.fori_loop(0, SEQ // 256, delta, 0)

        def add_rows(ref, slices, val):
            at = 0
            for start, size in slices:
                ref[pl.ds(start, size), :] += val[at:at + size]
                at += size

        def group(p, masks, blocks):
            head0, mask1, mask2 = masks
            heads = (head0, jnp.logical_not(head0))
            keys = [rows if prev is None else prev + rows for rows, prev in blocks]
            mask = [mask1 if prev is None else mask2 for _, prev in blocks]
            kk = [_load_rows(k_ref, ks).astype(BF16) for ks in keys]
            vv = [_load_rows(v_ref, ks).astype(BF16) for ks in keys]
            qb = [_load_rows(q_ref, rows) for rows, _ in blocks]
            dob = [_load_rows(do_ref, rows) for rows, _ in blocks]
            lse_b = [_load_rows(lse_ref, rows) for rows, _ in blocks]
            dlt_b = [_load_rows(dlt_ref, rows) for rows, _ in blocks]
            chains = [(g, h) for g in range(len(blocks)) for h in range(2)]
            qm = [jnp.where(heads[h], qb[g], 0.0).astype(BF16) for g, h in chains]
            dom = [jnp.where(heads[h], dob[g], 0.0).astype(BF16) for g, h in chains]
            s = [_dot_nt(qm[c], kk[g]) for c, (g, h) in enumerate(chains)]
            dp = [_dot_nt(dom[c], vv[g]) for c, (g, h) in enumerate(chains)]
            pr = [jnp.where(mask[g], jnp.exp(s[c] - lse_b[g][:, h * HEAD_DIM:h * HEAD_DIM + 1]), 0.0)
                  for c, (g, h) in enumerate(chains)]
            ds = [(pr[c] * (dp[c] - dlt_b[g][:, h * HEAD_DIM:h * HEAD_DIM + 1])).astype(BF16)
                  for c, (g, h) in enumerate(chains)]
            dv = [_dot_tn(pr[c].astype(BF16), dom[c]) for c in range(len(chains))]
            dk = [_dot_tn(ds[c], qm[c]) for c in range(len(chains))]
            dq = [_dot(ds[c], kk[g]) for c, (g, h) in enumerate(chains)]
            for g, (rows, _) in enumerate(blocks):
                add_rows(dqa_ref, rows, jnp.where(head0, dq[2 * g], dq[2 * g + 1]))
                add_rows(dka_ref, keys[g], dk[2 * g] + dk[2 * g + 1])
                add_rows(dva_ref, keys[g], dv[2 * g] + dv[2 * g + 1])

        _for_each_group(group)

        @pl.when(pl.program_id(0) == 0)
        def _():
            def tables(i, carry):
                rows = pl.ds(pl.multiple_of(i * 256, 256), 256)
                c, sa, sb = _rot_tables(pos_ref[rows, :], invf_ref[...], ma_ref[...], mb_ref[...])
                rot_ref[0, rows, :] = c
                rot_ref[1, rows, :] = sa
                rot_ref[2, rows, :] = sb
                return carry

            lax.fori_loop(0, SEQ // 256, tables, 0)

        def finish(i, carry):
            rows = pl.ds(pl.multiple_of(i * 256, 256), 256)
            c, sa, sb = rot_ref[0, rows, :], rot_ref[1, rows, :], rot_ref[2, rows, :]
            dq_ref[rows, :] = _rot_t(dqa_ref[rows, :] * Q_SCALE, c, sa, sb).astype(BF16)
            dk_ref[rows, :] = _rot_t(dka_ref[rows, :], c, sa, sb).astype(BF16)
            dv_ref[rows, :] = dva_ref[rows, :].astype(BF16)
            return carry

        lax.fori_loop(0, SEQ // 256, finish, 0)

    slab = pl.BlockSpec((SEQ, 128), lambda i: (0, i))
    out = jax.ShapeDtypeStruct((SEQ, ATTN_W), BF16)
    acc = pltpu.VMEM((SEQ, 128), F32)
    return _call(
        "attn_bwd", body, ATTN_W // 128,
        [slab] * 6 + [_full_spec((SEQ, 1)), _full_spec((1, 128)), _full_spec((1, 128)), _full_spec((1, 128))],
        [slab] * 3, [out, out, out], (q, k, v, o, lse, do, pos_col, *rot),
        scratch_shapes=[acc, acc, acc, acc, pltpu.VMEM((3, SEQ, 128), F32)])


def in_bwd(dproj_parts, w_in_t, x, g1, dx2, after=()):
    tm = 512
    k = len(dproj_parts)

    def body(*refs):
        w_ref, x_ref, g_ref, dx2_ref, dx_ref, dg_ref = refs[k:]
        dh1 = _dot(refs[0][...], w_ref[0:512, :])
        for j in range(1, k):
            dh1 = dh1 + _dot(refs[j][...], w_ref[512 * j:512 * (j + 1), :])
        dz, dg = _rms_bwd(x_ref[...], g_ref[...], dh1)
        dx_ref[...] = dx2_ref[...] + dz

        @pl.when(pl.program_id(0) == 0)
        def _():
            dg_ref[...] = jnp.zeros_like(dg_ref)

        dg_ref[...] += dg

    return _call(
        "in_bwd", body, SEQ // tm,
        [_row_spec(tm, 512)] * k + [_weight_spec((IN_W, D_MODEL)), _row_spec(tm, D_MODEL), _full_spec((1, D_MODEL)),
                                    _row_spec(tm, D_MODEL)],
        [_row_spec(tm, D_MODEL), _full_spec((1, D_MODEL))],
        [jax.ShapeDtypeStruct((SEQ, D_MODEL), F32), jax.ShapeDtypeStruct((1, D_MODEL), F32)],
        (*dproj_parts, w_in_t, x, g1, dx2), after=after)


def _coords():
    return lax.axis_index("x"), lax.axis_index("y"), lax.axis_index("c")


class Exchange:
    def __init__(self, srcs, bufs, new_shapes, n_sems, make):
        self.srcs, self.bufs, self.new_shapes, self.n_sems, self.make = list(srcs), list(bufs), list(new_shapes), n_sems, make


def _call(name, body, n_steps, in_specs, out_specs, out_shape, args, scratch_shapes=(), after=()):
    n_in = len(args)

    def wrapped(*refs):
        body(*refs[:n_in], *refs[n_in + len(after):])

    return list(pl.pallas_call(
        wrapped, name=name, grid=(n_steps,), in_specs=list(in_specs) + [ANY] * len(after), out_specs=list(out_specs),
        out_shape=list(out_shape), scratch_shapes=list(scratch_shapes), compiler_params=_params(),
    )(*args, *after))


GATHER_SEMS = 8


def gather(bufs):
    n = len(bufs)

    def make(src_refs, buf_refs, new_refs, send_sems, recv_sems):
        x, y, c = _coords()
        me, sibling = (x, y, c), (x, y, 1 - c)
        over_x, over_y, across = (1 - x, y), (x, 1 - y), (1 - x, 1 - y)

        def copy(a, k, block, to, half=None):
            r = buf_refs[a].shape[0] // N_DEV
            lo, size = (0, r) if half is None else (half * (r // 2), r // 2)
            rows = buf_refs[a].at[pl.ds((4 * block[0] + 2 * block[1] + block[2]) * r + lo, size), :]
            return pltpu.make_async_remote_copy(
                src_ref=rows, dst_ref=rows, send_sem=send_sems.at[GATHER_SEMS * a + k],
                recv_sem=recv_sems.at[GATHER_SEMS * a + k], device_id=to, device_id_type=MESH)

        every = range(n)
        out = ([copy(a, 0, me, sibling) for a in every] + [copy(a, 1, me, (*over_x, c)) for a in every]
               + [copy(a, 2, me, (*over_y, c)) for a in every])
        near_in = [copy(a, 1, (*over_x, c), me) for a in every] + [copy(a, 2, (*over_y, c), me) for a in every]
        relay = ([copy(a, 3, (*over_x, c), (*over_y, c), half=0) for a in every]
                 + [copy(a, 4, (*over_y, c), (*over_x, c), half=1) for a in every])
        near_on = [copy(a, 5, (*over_x, c), sibling) for a in every] + [copy(a, 6, (*over_y, c), sibling) for a in every]
        relay_in = ([copy(a, 3, (*across, c), me, half=0) for a in every]
                    + [copy(a, 4, (*across, c), me, half=1) for a in every])
        far_on = [copy(a, 7, (*across, c), sibling) for a in every]
        from_core = ([copy(a, 0, sibling, me) for a in every] + [copy(a, 5, (*over_x, 1 - c), me) for a in every]
                     + [copy(a, 6, (*over_y, 1 - c), me) for a in every] + [copy(a, 7, (*across, 1 - c), me) for a in every])
        stages = [([], out), (near_in, relay + near_on), (relay_in, far_on)]
        return stages, out + relay + near_on + far_on, from_core

    return Exchange([], bufs, [], GATHER_SEMS * n, make)


TO_GATHER = (1, lambda x, y, c: [(x, y, 1 - c), (1 - x, y, c), (x, 1 - y, c)])
TO_SIBLING = (2, lambda x, y, c: [(x, y, 1 - c)])
TO_CHIPS = (3, lambda x, y, c: [(1 - x, y, c), (x, 1 - y, c), (1 - x, 1 - y, c)])
TO_ALL = (4, lambda x, y, c: [(x ^ (m >> 2), y ^ ((m >> 1) & 1), c ^ (m & 1)) for m in range(1, N_DEV)])


def by_sequencer(name, exchanges, who):
    collective_id, peers_of = who
    hbm = pltpu.MemorySpace.HBM
    refs = [([jax.new_ref(a, memory_space=hbm) for a in ex.srcs], [jax.new_ref(a, memory_space=hbm) for a in ex.bufs],
             [jax.empty_ref(s, memory_space=hbm) for s in ex.new_shapes]) for ex in exchanges]
    sems = []
    for ex in exchanges:
        sems += [pltpu.SemaphoreType.DMA((ex.n_sems,)), pltpu.SemaphoreType.DMA((ex.n_sems,))]

    @pl.kernel(mesh=plsc.ScalarSubcoreMesh(axis_name="sequencer", num_cores=1), name=name, scratch_types=tuple(sems),
               compiler_params=pltpu.CompilerParams(collective_id=collective_id))
    def launch(*sem_refs):
        peers = peers_of(*_coords())
        barrier = pltpu.get_barrier_semaphore()
        for peer in peers:
            pl.semaphore_signal(barrier, inc=1, device_id=peer, device_id_type=MESH)
        pl.semaphore_wait(barrier, len(peers))

        made = [ex.make(*refs[k], sem_refs[2 * k], sem_refs[2 * k + 1]) for k, ex in enumerate(exchanges)]
        for stage in range(max(len(stages) for stages, _, _ in made)):
            for stages, _, _ in made:
                if stage < len(stages):
                    arrivals, starts = stages[stage]
                    for cp in arrivals:
                        cp.wait_recv()
                    for cp in starts:
                        cp.start()
        for _, sends, arrivals in made:
            for cp in arrivals:
                cp.wait_recv()
            for cp in sends:
                cp.wait_send()

    launch()
    return [([ref[...] for ref in bufs], [ref[...] for ref in news]) for _, bufs, news in refs]


def place_shards(name, shards, dev):
    n = len(shards)

    def body(dev_ref, *refs):
        for a in range(n):
            refs[n + a][...] = refs[a][...].astype(BF16)

    spec = pltpu.PrefetchScalarGridSpec(
        num_scalar_prefetch=1, grid=(1,),
        in_specs=[pl.BlockSpec(s.shape, lambda i, dev_ref: (0, 0)) for s in shards],
        out_specs=[pl.BlockSpec(s.shape, lambda i, dev_ref: (dev_ref[0], 0)) for s in shards])
    return pl.pallas_call(
        body, name=name, grid_spec=spec,
        out_shape=[jax.ShapeDtypeStruct((N_DEV * s.shape[0], s.shape[1]), BF16) for s in shards],
        compiler_params=_params(),
    )(dev, *shards)


def _swap(copies_of):
    def make(src_refs, buf_refs, new_refs, send_sems, recv_sems):
        copies = copies_of(src_refs, new_refs, send_sems, recv_sems)
        return [([], copies)], copies, copies

    return make


def to_sibling(grads):
    def copies_of(src_refs, new_refs, send_sems, recv_sems):
        x, y, c = _coords()
        return [pltpu.make_async_remote_copy(
            src_ref=src_refs[a].at[2 * xy + 1 - c], dst_ref=new_refs[a].at[xy], send_sem=send_sems.at[4 * a + xy],
            recv_sem=recv_sems.at[4 * a + xy], device_id=(x, y, 1 - c), device_id_type=MESH)
            for a in range(len(src_refs)) for xy in range(4)]

    return Exchange(grads, [], [jax.ShapeDtypeStruct((4,) + g.shape[1:], g.dtype) for g in grads], 4 * len(grads),
                    _swap(copies_of))


def to_chips(parts):
    def copies_of(src_refs, new_refs, send_sems, recv_sems):
        x, y, c = _coords()
        chips = [(1 - x, y), (x, 1 - y), (1 - x, 1 - y)]
        return [pltpu.make_async_remote_copy(
            src_ref=src_refs[a].at[2 * px + py], dst_ref=new_refs[a].at[2 * x + y], send_sem=send_sems.at[3 * a + j],
            recv_sem=recv_sems.at[3 * a + j], device_id=(px, py, c), device_id_type=MESH)
            for a in range(len(src_refs)) for j, (px, py) in enumerate(chips)]

    return Exchange(parts, [], [jax.ShapeDtypeStruct(p.shape, p.dtype) for p in parts], 3 * len(parts), _swap(copies_of))


def to_owners(grad):
    def copies_of(src_refs, new_refs, send_sems, recv_sems):
        x, y, c = _coords()
        copies = []
        for m in range(1, N_DEV):
            px, py, pc = x ^ (m >> 2), y ^ ((m >> 1) & 1), c ^ (m & 1)
            copies.append(pltpu.make_async_remote_copy(
                src_ref=src_refs[0].at[4 * px + 2 * py + pc], dst_ref=new_refs[0].at[4 * x + 2 * y + c],
                send_sem=send_sems.at[m - 1], recv_sem=recv_sems.at[m - 1], device_id=(px, py, pc), device_id_type=MESH))
        return copies

    return Exchange([grad], [], [jax.ShapeDtypeStruct(grad.shape, grad.dtype)], N_DEV - 1, _swap(copies_of))


def to_everyone(vec):
    def copies_of(src_refs, new_refs, send_sems, recv_sems):
        x, y, c = _coords()
        copies = []
        for m in range(1, N_DEV):
            px, py, pc = x ^ (m >> 2), y ^ ((m >> 1) & 1), c ^ (m & 1)
            copies.append(pltpu.make_async_remote_copy(
                src_ref=src_refs[0], dst_ref=new_refs[0].at[4 * x + 2 * y + c],
                send_sem=send_sems.at[m - 1], recv_sem=recv_sems.at[m - 1], device_id=(px, py, pc), device_id_type=MESH))
        return copies

    return Exchange([vec], [], [jax.ShapeDtypeStruct((N_DEV,) + vec.shape, vec.dtype)], N_DEV - 1, _swap(copies_of))


def sum_cores(name, grads, others, core, after=()):
    k = len(grads)

    def body(core_ref, *refs):
        for j in range(k):
            out_ref = refs[2 * k + len(after) + j]
            out_ref[...] = (refs[j][:, 0].astype(F32) + refs[k + j][...].astype(F32)).astype(out_ref.dtype)

    mine = [pl.BlockSpec((2, 1) + o.shape[1:], lambda i, core_ref: (i, core_ref[0], 0, 0)) for o in others]
    theirs = [pl.BlockSpec((2,) + o.shape[1:], lambda i, core_ref: (i, 0, 0)) for o in others]
    return pl.pallas_call(
        body, name=name,
        grid_spec=pltpu.PrefetchScalarGridSpec(
            num_scalar_prefetch=1, grid=(2,), in_specs=mine + theirs + [ANY] * len(after), out_specs=theirs),
        out_shape=[jax.ShapeDtypeStruct(o.shape, o.dtype) for o in others],
        compiler_params=_params(),
    )(core, *[g.reshape((4, 2) + g.shape[1:]) for g in grads], *others, *after)


def sum_owned(name, grad, others, dev_ids, after=()):
    _, r, w = grad.shape

    def body(ids_ref, *refs):
        acc = refs[0][0]
        for k in range(1, N_DEV):
            acc = acc + refs[k][0]
        refs[-1][...] = acc

    def pick(k):
        return pl.BlockSpec((1, r, w), lambda i, ids_ref: (ids_ref[k], 0, 0))

    return pl.pallas_call(
        body, name=name,
        grid_spec=pltpu.PrefetchScalarGridSpec(
            num_scalar_prefetch=1, grid=(1,), in_specs=[pick(k) for k in range(N_DEV)] + [ANY] * len(after),
            out_specs=pl.BlockSpec((r, w), lambda i, ids_ref: (ids_ref[0], 0))),
        out_shape=jax.ShapeDtypeStruct((N_DEV * r, w), F32),
        compiler_params=_params(),
    )(dev_ids, grad, *([others] * (N_DEV - 1)), *after)


def _adamw_update(w, g, m, v):
    nm = ADAM_B1 * m + np.float32(1.0 - ADAM_B1) * g
    nv = ADAM_B2 * v + np.float32(1.0 - ADAM_B2) * (g * g)
    m_hat = nm / np.float32(1.0 - ADAM_B1 ** ADAM_STEP)
    v_hat = nv / np.float32(1.0 - ADAM_B2 ** ADAM_STEP)
    return -ADAM_LR * (m_hat / (jnp.sqrt(v_hat) + ADAM_EPS) + ADAM_WD * w), nm, nv


def adamw_of_sums(name, parts, others, chip_ids, ws, ms, vs, after):
    n = len(parts)
    halves = 2

    def body(ids_ref, *refs):
        outs = refs[7 * n + 1:]
        for j in range(n):
            p_ref, a_ref, b_ref, c_ref, w_ref, m_ref, v_ref = refs[7 * j:7 * j + 7]
            g = ((p_ref[0].astype(F32) + a_ref[0].astype(F32)) + b_ref[0].astype(F32)) + c_ref[0].astype(F32)
            outs[4 * j][...] = g
            outs[4 * j + 1][...], outs[4 * j + 2][...], outs[4 * j + 3][...] = _adamw_update(w_ref[...], g, m_ref[...], v_ref[...])

    in_specs, out_specs, out_shape, operands = [], [], [], []
    for part, other, w, m, v in zip(parts, others, ws, ms, vs):
        _, r, wd = part.shape
        rows = r // halves
        whole = pl.BlockSpec((rows, wd), lambda i, ids_ref: (i, 0))
        in_specs += [pl.BlockSpec((1, rows, wd), lambda i, ids_ref, k=k: (ids_ref[k], i, 0)) for k in range(4)] + [whole] * 3
        out_specs += [whole] * 4
        out_shape += [jax.ShapeDtypeStruct((r, wd), F32)] * 4
        operands += [part, other, other, other, w, m, v]
    outs = pl.pallas_call(
        body, name=name,
        grid_spec=pltpu.PrefetchScalarGridSpec(
            num_scalar_prefetch=1, grid=(halves,), in_specs=in_specs + [ANY], out_specs=out_specs),
        out_shape=out_shape,
        compiler_params=_params(),
    )(chip_ids, *operands, after)
    return [tuple(outs[4 * j:4 * j + 4]) for j in range(n)]


def pack_small(parts):
    names = [name for name, _ in SMALL if name in parts]
    operands = [parts[name] for name in names]
    first_row, at = {}, 0
    for name, size in SMALL:
        first_row[name] = at // 128
        at += size
    sizes = dict(SMALL)

    def body(*refs):
        out_ref = refs[-1]
        out_ref[...] = jnp.zeros_like(out_ref)
        for name, ref in zip(names, refs):
            row = first_row[name]
            if name == "loss_sum":
                lane0 = lax.broadcasted_iota(jnp.int32, (1, 128), 1) == 0
                out_ref[row:row + 1, :] = jnp.where(lane0, ref[...], 0.0)
            else:
                rows = sizes[name] // 128
                out_ref[row:row + rows, :] = ref[...].reshape(rows, 128)

    vmem = pl.BlockSpec(memory_space=pltpu.VMEM)
    return pl.pallas_call(
        body, name="pack_small", in_specs=[vmem] * len(names), out_specs=vmem,
        out_shape=jax.ShapeDtypeStruct((SMALL_ROWS, 128), F32), compiler_params=_params(()),
    )(*operands)


LATE = "pre_mix_norm"


def adamw_small(packed_g, late_parts, ws, ms, vs):
    names = [name for name, _ in SMALL if name != "loss_sum"]
    k = len(names)
    shapes = [ws[name].shape[1:] if ws[name].ndim > 2 else ws[name].shape for name in names]
    first_row, at = [], 0
    for name, size in SMALL:
        first_row.append(at // 128)
        at += size

    def body(g_ref, late_ref, *refs):
        w_refs, m_refs, v_refs, outs = refs[:k], refs[k:2 * k], refs[2 * k:3 * k], refs[3 * k:]
        for i, (name, size) in enumerate(SMALL[:k]):
            if name == LATE:
                g = late_ref[0]
                for j in range(1, N_DEV):
                    g = g + late_ref[j]
            else:
                g = g_ref[first_row[i]:first_row[i] + size // 128, :].reshape(shapes[i])
            outs[i][...] = g
            outs[k + i][...], outs[2 * k + i][...], outs[3 * k + i][...] = _adamw_update(
                w_refs[i][...], g, m_refs[i][...], v_refs[i][...])
        outs[4 * k][...] = g_ref[first_row[k]:first_row[k] + 1, 0:1]

    vmem = pl.BlockSpec(memory_space=pltpu.VMEM)
    operands = [t[name].reshape(shape) for t in (ws, ms, vs) for name, shape in zip(names, shapes)]
    outs = pl.pallas_call(
        body, name="adamw_small", in_specs=[vmem] * (2 + 3 * k), out_specs=[vmem] * (4 * k + 1),
        out_shape=[jax.ShapeDtypeStruct(shape, F32) for _ in range(4) for shape in shapes] + [jax.ShapeDtypeStruct((1, 1), F32)],
        compiler_params=_params(()),
    )(packed_g, late_parts, *operands)
    tables = [{name: outs[j * k + i].reshape(ws[name].shape) for i, name in enumerate(names)} for j in range(4)]
    return (*tables, outs[4 * k])


def kernel(x, positions, pre_mix_norm, w_in, sgu_ln_gain, sgu_ln_bias, sgu_w_spatial, sgu_b_spatial, attn_out_norm, sgu_out_norm, w_out, post_mix_norm, pre_ffn_norm, w_gate, w_up, w_down, post_ffn_norm, loss_target, m_pre_mix_norm, m_w_in, m_sgu_ln_gain, m_sgu_ln_bias, m_sgu_w_spatial, m_sgu_b_spatial, m_attn_out_norm, m_sgu_out_norm, m_w_out, m_post_mix_norm, m_pre_ffn_norm, m_w_gate, m_w_up, m_w_down, m_post_ffn_norm, v_pre_mix_norm, v_w_in, v_sgu_ln_gain, v_sgu_ln_bias, v_sgu_w_spatial, v_sgu_b_spatial, v_attn_out_norm, v_sgu_out_norm, v_w_out, v_post_mix_norm, v_pre_ffn_norm, v_w_gate, v_w_up, v_w_down, v_post_ffn_norm):
    small_w = dict(pre_mix_norm=pre_mix_norm, sgu_ln_gain=sgu_ln_gain, sgu_ln_bias=sgu_ln_bias, sgu_w_spatial=sgu_w_spatial,
                   sgu_b_spatial=sgu_b_spatial, attn_out_norm=attn_out_norm, sgu_out_norm=sgu_out_norm,
                   post_mix_norm=post_mix_norm, pre_ffn_norm=pre_ffn_norm, post_ffn_norm=post_ffn_norm)
    small_m = dict(pre_mix_norm=m_pre_mix_norm, sgu_ln_gain=m_sgu_ln_gain, sgu_ln_bias=m_sgu_ln_bias, sgu_w_spatial=m_sgu_w_spatial,
                   sgu_b_spatial=m_sgu_b_spatial, attn_out_norm=m_attn_out_norm, sgu_out_norm=m_sgu_out_norm,
                   post_mix_norm=m_post_mix_norm, pre_ffn_norm=m_pre_ffn_norm, post_ffn_norm=m_post_ffn_norm)
    small_v = dict(pre_mix_norm=v_pre_mix_norm, sgu_ln_gain=v_sgu_ln_gain, sgu_ln_bias=v_sgu_ln_bias, sgu_w_spatial=v_sgu_w_spatial,
                   sgu_b_spatial=v_sgu_b_spatial, attn_out_norm=v_attn_out_norm, sgu_out_norm=v_sgu_out_norm,
                   post_mix_norm=v_post_mix_norm, pre_ffn_norm=v_pre_ffn_norm, post_ffn_norm=v_post_ffn_norm)

    x2d = x[0]
    target = loss_target[0]
    pos_col = positions.reshape(SEQ, 1)
    rot = _rot_consts()
    w_sp = sgu_w_spatial[0]
    bfull = jnp.repeat(sgu_b_spatial[0].T, HEAD_DIM, axis=1)

    x_i, y_i, c_i = (lax.axis_index(a).astype(jnp.int32) for a in MESH_AXES)
    dev = 4 * x_i + 2 * y_i + c_i
    core = c_i.reshape(1)
    chip = 2 * x_i + y_i
    chip_ids = jnp.stack([chip, chip ^ 1, chip ^ 2, chip ^ 3])
    dev_ids = jnp.stack([dev ^ m for m in range(N_DEV)])

    def gathered(name, bufs):
        return by_sequencer(name, [gather(bufs)], TO_GATHER)[0][0]

    def from_sibling(name, grads):
        return by_sequencer(name, [to_sibling(grads)], TO_SIBLING)[0][1]

    def from_chips(name, parts):
        return by_sequencer(name, [to_chips(parts)], TO_CHIPS)[0][1]

    (w_in_t,) = place_shards("place_w_in", [w_in[0].T], dev.reshape(1))
    (w_in_t,) = gathered("gather_w_in", [w_in_t])
    w_gate_t, w_up_t, w_out_f, w_down_f = place_shards(
        "place_weights", [w_gate[0].T, w_up[0].T, w_out[0], w_down[0]], dev.reshape(1))
    (w_out_f,) = gathered("gather_w_out", [w_out_f])
    w_gate_t, w_up_t = gathered("gather_w_gate_up", [w_gate_t, w_up_t])
    (w_down_f,) = gathered("gather_w_down", [w_down_f])

    h1, u, vs, q, k, v = in_proj(x2d, pos_col, pre_mix_norm, w_in_t, rot)
    attn_r, lse, attn = attn_fwd(q, k, v)
    (sgu,) = sgu_fwd(u, vs, sgu_ln_gain, sgu_ln_bias, w_sp, bfull)
    mix, y, x2, h2 = out_proj(attn, sgu, x2d, attn_out_norm, sgu_out_norm, w_out_f, post_mix_norm, pre_ffn_norm)
    gate, up, act = ffn_up(h2, w_gate_t, w_up_t)
    df, dx3, d_post_ffn, sq_err = ffn_down_loss(act, w_down_f, x2, post_ffn_norm, target)

    g_w_down = weight_grad("grad_w_down", act, df)
    (s_down,) = from_sibling("w_down_to_sibling", [g_w_down])
    dgate, dup, dx2, dy, d_pre_ffn, d_post_mix = ffn_bwd(
        df, w_down_f, gate, up, w_gate_t, w_up_t, x2, pre_ffn_norm, dx3, y, post_mix_norm, after=[g_w_down])
    (p_down,) = sum_cores("sum_cores_down", [g_w_down], [s_down], core, after=[dy])
    (c_down,) = from_chips("w_down_to_chips", [p_down])
    g_w_gate, g_w_up = weight_grads("grad_w_gate_up", [dgate, dup], h2, after=[p_down])
    s_gate, s_up = from_sibling("w_gate_up_to_sibling", [g_w_gate, g_w_up])
    g_w_out = weight_grad("grad_w_out", mix, dy, after=[g_w_up, c_down])
    p_gate, p_up = sum_cores("sum_cores_gate_up", [g_w_gate, g_w_up], [s_gate, s_up], core, after=[g_w_out])
    c_gate, c_up = from_chips("w_gate_up_to_chips", [p_gate, p_up])
    (s_out,) = from_sibling("w_out_to_sibling", [g_w_out])
    dsgu, d_attn_out, d_sgu_out, dattn_r = mix_bwd(dy, w_out_f, attn, sgu, attn_out_norm, sgu_out_norm, after=[p_gate, p_up])
    du, dvs, d_ln_gain, d_ln_bias, d_w_sp, d_bfull = sgu_bwd(u, vs, dsgu, sgu_ln_gain, sgu_ln_bias, w_sp, bfull)

    d_b_sp = d_bfull.reshape(CHUNK, N_GROUPS, HEAD_DIM).sum(axis=-1).T
    small_g = pack_small(dict(sgu_ln_gain=d_ln_gain, sgu_ln_bias=d_ln_bias, sgu_w_spatial=d_w_sp, sgu_b_spatial=d_b_sp,
                              attn_out_norm=d_attn_out, sgu_out_norm=d_sgu_out, post_mix_norm=d_post_mix,
                              pre_ffn_norm=d_pre_ffn, post_ffn_norm=d_post_ffn, loss_sum=sq_err))
    small_g = small_g.reshape(N_DEV, SMALL_ROWS // N_DEV, 128)
    ((_, (o_small,)),) = by_sequencer("small_to_owners", [to_owners(small_g)], TO_ALL)

    dq, dk, dv = attn_bwd(q, k, v, attn_r, lse, dattn_r, _to_residue_order(pos_col), rot)
    summed_small = sum_owned("sum_small", small_g, o_small, dev_ids, after=[dq])
    (all_small,) = gathered("gather_small_grads", [summed_small])
    (p_out,) = sum_cores("sum_cores_out", [g_w_out], [s_out], core, after=[dq])
    (c_out,) = from_chips("w_out_to_chips", [p_out])
    dq, dk, dv = (_from_residue_order(t) for t in (dq, dk, dv))
    dproj = [dq, dk, dv, du, dvs]
    g_w_in = weight_grad_of_parts("grad_w_in", dproj, h1, after=[c_gate, c_up])
    (s_in,) = from_sibling("w_in_to_sibling", [g_w_in])
    grad_x, d_pre_mix = in_bwd(dproj, w_in_t, x2d, pre_mix_norm, dx2, after=[g_w_in, all_small])
    (p_in,) = sum_cores("sum_cores_in", [g_w_in], [s_in], core, after=[d_pre_mix, c_out])
    (_, (c_in,)), (_, (late_parts,)) = by_sequencer(
        "last_sums_to_owners", [to_chips([p_in]), to_everyone(d_pre_mix)], TO_ALL)
    late_parts = lax.dynamic_update_slice(late_parts, d_pre_mix[None], (dev, 0, 0))

    def same(t):
        return t

    def turned(t):
        return t.T

    big, last = {}, p_in
    for call, weights in (("adamw_ffn", (("w_down", w_down, p_down, c_down, m_w_down, v_w_down, same),
                                         ("w_gate", w_gate, p_gate, c_gate, m_w_gate, v_w_gate, turned),
                                         ("w_up", w_up, p_up, c_up, m_w_up, v_w_up, turned))),
                          ("adamw_w_out", (("w_out", w_out, p_out, c_out, m_w_out, v_w_out, same),)),
                          ("adamw_w_in", (("w_in", w_in, p_in, c_in, m_w_in, v_w_in, turned),))):
        results = adamw_of_sums(call, [p for _, _, p, _, _, _, _ in weights], [c for _, _, _, c, _, _, _ in weights], chip_ids,
                                [turn(w[0]) for _, w, _, _, _, _, turn in weights], [turn(m[0]) for _, _, _, _, m, _, turn in weights],
                                [turn(vv[0]) for _, _, _, _, _, vv, turn in weights], last)
        for (name, _, _, _, _, _, turn), outs in zip(weights, results):
            big[name] = tuple(turn(t)[None] for t in outs)
        last = results[-1][0]
    sg, sd, snm, snv, loss_sum = adamw_small(all_small, late_parts, small_w, small_m, small_v)
    loss = loss_sum[0, 0] * np.float32(0.5 / D_MODEL)

    names = ["pre_mix_norm", "w_in", "sgu_ln_gain", "sgu_ln_bias", "sgu_w_spatial", "sgu_b_spatial", "attn_out_norm",
             "sgu_out_norm", "w_out", "post_mix_norm", "pre_ffn_norm", "w_gate", "w_up", "w_down", "post_ffn_norm"]
    outs = [loss, grad_x[None]]
    for i, table in enumerate((sg, sd, snm, snv)):
        for name in names:
            outs.append(big[name][i] if name in big else table[name])
    return tuple(outs)
```

```python
import numpy as np
import jax
import jax.numpy as jnp
from jax import lax
from jax.experimental import pallas as pl
from jax.experimental.pallas import tpu as pltpu
from jax.experimental.pallas import tpu_sc as plsc

F32 = jnp.float32
BF16 = jnp.bfloat16

SEQ = 2048
D_MODEL = 1024
ATTN_W = 512
SGU_W = 512
HEAD_DIM = 64
N_GROUPS = 8
CHUNK = 128
D_FF = 2816
IN_W = 3 * ATTN_W + 2 * SGU_W
DILATIONS = (1, 4, 16)
ROPE_THETA = 500000.0
ROT_DIM = 16
ROT_HALF = 8
RMS_EPS = 1e-6
LN_EPS = 1e-5
Q_SCALE = 0.125
NEG = -1e30

N_DEV = 8
MESH_AXES = ("x", "y", "c")
MESH = pl.DeviceIdType.MESH

ADAM_LR = 0.001
ADAM_B1 = 0.9
ADAM_B2 = 0.999
ADAM_EPS = 1e-08
ADAM_WD = 0.01
ADAM_STEP = 10

VMEM_LIMIT = 60 * 1024 * 1024
ANY = pl.BlockSpec(memory_space=pl.ANY)

SMALL = (("pre_mix_norm", 1024), ("sgu_ln_gain", 512), ("sgu_ln_bias", 512), ("sgu_w_spatial", 8 * 128 * 128),
         ("sgu_b_spatial", 1024), ("attn_out_norm", 512), ("sgu_out_norm", 512), ("post_mix_norm", 1024),
         ("pre_ffn_norm", 1024), ("post_ffn_norm", 1024), ("loss_sum", 1))
SMALL_ROWS = 1152


def _params(sem=("arbitrary",)):
    return pltpu.CompilerParams(dimension_semantics=sem, vmem_limit_bytes=VMEM_LIMIT)


def _dot(a, b):
    return jnp.dot(a, b, preferred_element_type=F32)


def _dot_nt(a, b):
    return lax.dot_general(a, b, (((1,), (1,)), ((), ())), preferred_element_type=F32)


def _dot_tn(a, b):
    return lax.dot_general(a, b, (((0,), (0,)), ((), ())), preferred_element_type=F32)


def _rms(z):
    return lax.rsqrt(jnp.mean(z * z, axis=-1, keepdims=True) + RMS_EPS)


def _rms_bwd(z, gain, d):
    r = _rms(z)
    n = z * r
    dn = d * gain
    dz = r * (dn - n * jnp.mean(dn * n, axis=-1, keepdims=True))
    return dz, jnp.sum(d * n, axis=0, keepdims=True)


def _gelu(z):
    return 0.5 * z * (1.0 + lax.erf(z * np.float32(1.0 / np.sqrt(2.0))))


def _gelu_grad(z):
    cdf = 0.5 * (1.0 + lax.erf(z * np.float32(1.0 / np.sqrt(2.0))))
    return cdf + z * jnp.exp(-0.5 * z * z) * np.float32(1.0 / np.sqrt(2.0 * np.pi))


def _rot_tables(pos_col, invf, ma, mb):
    ang = pos_col.astype(F32) * invf
    s = jnp.sin(ang)
    return jnp.cos(ang), s * ma, s * mb


def _rot(t, c, sa, sb):
    return t * c + pltpu.roll(t, 120, 1) * sa + pltpu.roll(t, 8, 1) * sb


def _rot_t(d, c, sa, sb):
    return d * c + pltpu.roll(d * sa, 8, 1) + pltpu.roll(d * sb, 120, 1)


def _rot_consts():
    lane = np.arange(128) % HEAD_DIM
    inv_freq = (np.float32(ROPE_THETA) ** (-np.arange(0, ROT_DIM, 2, dtype=np.float32) / np.float32(ROT_DIM))).astype(np.float32)
    invf = np.where(lane < ROT_DIM, inv_freq[lane % ROT_HALF], 0.0).astype(np.float32)
    ma = np.where(lane < ROT_HALF, -1.0, 0.0).astype(np.float32)
    mb = np.where((lane >= ROT_HALF) & (lane < ROT_DIM), 1.0, 0.0).astype(np.float32)
    return jnp.asarray(invf[None]), jnp.asarray(ma[None]), jnp.asarray(mb[None])


def _row_spec(tm, w):
    return pl.BlockSpec((tm, w), lambda i: (i, 0))


def _full_spec(shape):
    return pl.BlockSpec(shape, lambda i: (0,) * len(shape))


def _weight_spec(shape):
    return pl.BlockSpec(shape, lambda i: (0,) * len(shape), pipeline_mode=pl.Buffered(1))


FF_CHUNKS = (256, 512, 1024, 1024)
FF_SPANS = [(int(o), n) for o, n in zip(np.cumsum((0,) + FF_CHUNKS[:-1]), FF_CHUNKS)]


def _ffn_weight_scratch(n_weights):
    return [pltpu.VMEM((D_FF, D_MODEL), BF16)] * n_weights + [pltpu.SemaphoreType.DMA((n_weights, len(FF_CHUNKS)))]


def _with_ffn_weights(w_hbm, w_vmem, sems, run):
    copies = [[pltpu.make_async_copy(h.at[pl.ds(o, n)], v.at[pl.ds(o, n)], sems.at[j, c])
               for j, (h, v) in enumerate(zip(w_hbm, w_vmem))] for c, (o, n) in enumerate(FF_SPANS)]
    first = pl.program_id(0) == 0

    def wait(c):
        for cp in copies[c]:
            cp.wait()

    @pl.when(first)
    def _():
        for chunk in copies:
            for cp in chunk:
                cp.start()
        run(FF_SPANS, wait)

    @pl.when(jnp.logical_not(first))
    def _():
        run([(0, D_FF)], None)


RES = 16


def _residue_scratch(n_arrays, tm, width):
    return [pltpu.VMEM((2, n_arrays, tm // RES, RES, width), F32), pltpu.SemaphoreType.DMA((2, n_arrays, RES))]


def _to_residue_rows(tiles, outs, scratch, sems, tm, n_steps):
    i = pl.program_id(0)
    slot = i % 2
    per = tm // RES

    def copies(step, s):
        return [pltpu.make_async_copy(scratch.at[s, a, :, b, :],
                                      outs[a].at[pl.ds(pl.multiple_of(b * (SEQ // RES) + per * step, per), per), :],
                                      sems.at[s, a, b]) for a in range(len(outs)) for b in range(RES)]

    @pl.when(i >= 2)
    def _():
        for cp in copies(i - 2, slot):
            cp.wait()

    for a, tile in enumerate(tiles):
        scratch[slot, a] = tile.reshape(per, RES, tile.shape[-1])
    for cp in copies(i, slot):
        cp.start()

    @pl.when(i == n_steps - 1)
    def _():
        for cp in copies(i - 1, 1 - slot) + copies(i, slot):
            cp.wait()


def in_proj(x, pos_col, g1, w_in_t, rot):
    tm = 512
    n_steps = SEQ // tm

    def body(x_ref, pos_ref, g_ref, w_ref, invf_ref, ma_ref, mb_ref, h_ref, u_ref, vs_ref, q_ref, k_ref, v_ref, scratch, sems):
        xf = x_ref[...]
        h = (xf * _rms(xf) * g_ref[...]).astype(BF16)
        h_ref[...] = h
        proj = _dot_nt(h, w_ref[...])
        c, sa, sb = _rot_tables(pos_ref[...], invf_ref[...], ma_ref[...], mb_ref[...])
        slabs = range(ATTN_W // 128)
        q = jnp.concatenate([_rot(proj[:, j * 128:(j + 1) * 128], c, sa, sb) * Q_SCALE for j in slabs], axis=1)
        k = jnp.concatenate([_rot(proj[:, ATTN_W + j * 128:ATTN_W + (j + 1) * 128], c, sa, sb) for j in slabs], axis=1)
        u_ref[...] = proj[:, 3 * ATTN_W:3 * ATTN_W + SGU_W]
        vs_ref[...] = proj[:, 3 * ATTN_W + SGU_W:]
        _to_residue_rows([q, k, proj[:, 2 * ATTN_W:3 * ATTN_W]], [q_ref, k_ref, v_ref], scratch, sems, tm, n_steps)

    act = jax.ShapeDtypeStruct((SEQ, 512), F32)
    return _call(
        "in_proj", body, n_steps,
        [_row_spec(tm, D_MODEL), _row_spec(tm, 1), _full_spec((1, D_MODEL)), _weight_spec((IN_W, D_MODEL)),
         _full_spec((1, 128)), _full_spec((1, 128)), _full_spec((1, 128))],
        [_row_spec(tm, D_MODEL)] + [_row_spec(tm, 512)] * 2 + [ANY] * 3,
        [jax.ShapeDtypeStruct((SEQ, D_MODEL), BF16)] + [act] * 5,
        (x, pos_col, g1, w_in_t, *rot), scratch_shapes=_residue_scratch(3, tm, ATTN_W))


def _to_residue_order(t):
    return t.reshape(SEQ // RES, RES, -1).transpose(1, 0, 2).reshape(t.shape)


def _from_residue_order(t):
    return t.reshape(RES, SEQ // RES, -1).transpose(1, 0, 2).reshape(t.shape)


def _block_rows(d, r, n):
    if d == 16:
        slices = [(128 * r, 128)]
    elif d == 4:
        slices = [(128 * (4 * b + r) + 32 * n, 32) for b in range(4)]
    else:
        slices = [(128 * b + 8 * n, 8) for b in range(RES)]
    return [(s if isinstance(s, int) else pl.multiple_of(s, z), z) for s, z in slices]


def _block_step(d, i):
    if d == 16:
        return i
    if d == 4:
        return 4 * (i & 31) + (i >> 5)
    return 16 * (i & 7) + (i >> 3)


def _attn_masks(d):
    row2 = _block_step(d, lax.broadcasted_iota(jnp.int32, (128, 256), 0))
    col2 = lax.broadcasted_iota(jnp.int32, (128, 256), 1)
    key2 = _block_step(d, col2 & 127)
    mask2 = jnp.logical_or(jnp.logical_and(col2 < 128, key2 >= row2), jnp.logical_and(col2 >= 128, key2 <= row2))
    row1 = _block_step(d, lax.broadcasted_iota(jnp.int32, (128, 128), 0))
    col1 = lax.broadcasted_iota(jnp.int32, (128, 128), 1)
    return col1 < HEAD_DIM, _block_step(d, col1) <= row1, mask2


def _load_rows(ref, slices):
    parts = [ref[pl.ds(s, z), :] for s, z in slices]
    return parts[0] if len(parts) == 1 else jnp.concatenate(parts, axis=0)


def _for_each_group(fn):
    for p, d in enumerate(DILATIONS):
        masks = _attn_masks(d)
        if d == 16:
            def group(i, carry, p=p, masks=masks):
                fn(p, masks, [(_block_rows(16, 8 * i + g, 0), None) for g in range(8)])
                return carry

            lax.fori_loop(0, 2, group, 0)
        elif d == 4:
            fn(p, masks, [(_block_rows(4, r, 0), None) for r in range(4)])

            def group(i, carry, p=p, masks=masks):
                blocks = [6 * i + g for g in range(6)]
                fn(p, masks, [(_block_rows(4, j % 4, 1 + j // 4), _block_rows(4, j % 4, j // 4)) for j in blocks])
                return carry

            lax.fori_loop(0, 2, group, 0)
        else:
            fn(p, masks, [(_block_rows(1, 0, 0), None)])

            def group(i, carry, p=p, masks=masks):
                fn(p, masks, [(_block_rows(1, 0, 5 * i + g + 1), _block_rows(1, 0, 5 * i + g)) for g in range(5)])
                return carry

            lax.fori_loop(0, 3, group, 0)


def attn_fwd(q, k, v):
    def body(q_ref, k_ref, v_ref, o_ref, lse_ref, nat_ref, op_ref, lp_ref, sems):
        def group(p, masks, blocks):
            head0, mask1, mask2 = masks
            heads = (head0, jnp.logical_not(head0))
            keys = [rows if prev is None else prev + rows for rows, prev in blocks]
            mask = [mask1 if prev is None else mask2 for _, prev in blocks]
            qb = [_load_rows(q_ref, rows) for rows, _ in blocks]
            kk = [_load_rows(k_ref, ks).astype(BF16) for ks in keys]
            vv = [_load_rows(v_ref, ks).astype(BF16) for ks in keys]
            chains = [(g, hm) for g in range(len(blocks)) for hm in heads]
            s = [jnp.where(mask[g], _dot_nt(jnp.where(hm, qb[g], 0.0).astype(BF16), kk[g]), NEG) for g, hm in chains]
            m = [jnp.max(t, axis=-1, keepdims=True) for t in s]
            e = [jnp.exp(t - mt) for t, mt in zip(s, m)]
            l = [jnp.sum(t, axis=-1, keepdims=True) for t in e]
            pv = [_dot(t.astype(BF16), vv[g]) for t, (g, _) in zip(e, chains)]
            for g, (rows, _) in enumerate(blocks):
                o_blk = jnp.where(head0, pv[2 * g] / l[2 * g], pv[2 * g + 1] / l[2 * g + 1])
                l_blk = jnp.where(head0, jnp.broadcast_to(m[2 * g] + jnp.log(l[2 * g]), (128, 128)),
                                  jnp.broadcast_to(m[2 * g + 1] + jnp.log(l[2 * g + 1]), (128, 128)))
                at = 0
                for start, size in rows:
                    op_ref[p, pl.ds(start, size), :] = o_blk[at:at + size]
                    lp_ref[p, pl.ds(start, size), :] = l_blk[at:at + size]
                    at += size

        _for_each_group(group)

        def combine(i, carry):
            rows = pl.ds(pl.multiple_of(i * 256, 256), 256)
            ls = [lp_ref[p, rows, :] for p in range(3)]
            m = jnp.maximum(jnp.maximum(ls[0], ls[1]), ls[2])
            lse = m + jnp.log(jnp.exp(ls[0] - m) + jnp.exp(ls[1] - m) + jnp.exp(ls[2] - m))
            o = jnp.zeros((256, 128), F32)
            for p in range(3):
                o = o + jnp.exp(ls[p] - lse) * op_ref[p, rows, :]
            o_ref[rows, :] = o
            lse_ref[rows, :] = lse
            return carry

        lax.fori_loop(0, SEQ // 256, combine, 0)

        lanes = pl.ds(pl.multiple_of(pl.program_id(0) * 128, 128), 128)
        back = [pltpu.make_async_copy(o_ref.at[pl.ds(b * (SEQ // RES), SEQ // RES), :], nat_ref.at[:, b, lanes], sems.at[b])
                for b in range(RES)]
        for cp in back:
            cp.start()
        for cp in back:
            cp.wait()

    slab = pl.BlockSpec((SEQ, 128), lambda i: (0, i))
    out = jax.ShapeDtypeStruct((SEQ, ATTN_W), F32)
    attn_r, lse, attn = _call(
        "attn_fwd", body, ATTN_W // 128, [slab] * 3, [slab] * 2 + [ANY],
        [out, out, jax.ShapeDtypeStruct((SEQ // RES, RES, ATTN_W), F32)], (q, k, v),
        scratch_shapes=[pltpu.VMEM((3, SEQ, 128), F32), pltpu.VMEM((3, SEQ, 128), F32), pltpu.SemaphoreType.DMA((RES,))])
    return attn_r, lse, attn.reshape(SEQ, ATTN_W)


def _causal_weights(w_ref):
    row = lax.broadcasted_iota(jnp.int32, (CHUNK, CHUNK), 0)
    col = lax.broadcasted_iota(jnp.int32, (CHUNK, CHUNK), 1)
    return [jnp.where(col <= row, w_ref[g], 0.0).astype(BF16) for g in range(N_GROUPS)], col <= row


def _sgu_chunk_fwd(u, vs, lg, lb, wc, bfull, head0):
    ug = _gelu(u)
    vg = _gelu(vs)
    xc = vg - jnp.mean(vg, axis=-1, keepdims=True)
    rstd = lax.rsqrt(jnp.mean(xc * xc, axis=-1, keepdims=True) + LN_EPS)
    xhat = xc * rstd
    vn = xhat * lg + lb
    mixed = []
    for gp in range(SGU_W // 128):
        vp = vn[:, gp * 128:(gp + 1) * 128].astype(BF16)
        mixed.append(jnp.where(head0, _dot(wc[2 * gp], vp), _dot(wc[2 * gp + 1], vp)))
    ms = jnp.concatenate(mixed, axis=1) + bfull
    return ug, xhat, rstd, vn, ms


def sgu_fwd(u, vs, lg, lb, w_sp, bfull):
    cpb = 4

    def body(u_ref, vs_ref, lg_ref, lb_ref, w_ref, b_ref, o_ref):
        wc, _ = _causal_weights(w_ref)
        head0 = lax.broadcasted_iota(jnp.int32, (CHUNK, 128), 1) < HEAD_DIM
        for ci in range(cpb):
            rows = pl.ds(ci * CHUNK, CHUNK)
            ug, _, _, _, ms = _sgu_chunk_fwd(u_ref[rows, :], vs_ref[rows, :], lg_ref[...], lb_ref[...], wc, b_ref[...], head0)
            o_ref[rows, :] = ug * ms

    tm = cpb * CHUNK
    return _call(
        "sgu_fwd", body, SEQ // tm,
        [_row_spec(tm, SGU_W), _row_spec(tm, SGU_W), _full_spec((1, SGU_W)), _full_spec((1, SGU_W)),
         _full_spec((N_GROUPS, CHUNK, CHUNK)), _full_spec((CHUNK, SGU_W))],
        [_row_spec(tm, SGU_W)], [jax.ShapeDtypeStruct((SEQ, SGU_W), F32)],
        (u, vs, lg, lb, w_sp, bfull))


def out_proj(attn, sgu, x, ga, gs, w_out, gpm, gpf):
    tm = 512

    def body(a_ref, s_ref, x_ref, ga_ref, gs_ref, w_ref, gpm_ref, gpf_ref, mix_ref, y_ref, x2_ref, h2_ref):
        a = a_ref[...]
        s = s_ref[...]
        an = (a * _rms(a) * ga_ref[...]).astype(BF16)
        sn = (s * _rms(s) * gs_ref[...]).astype(BF16)
        mix_ref[:, :ATTN_W] = an
        mix_ref[:, ATTN_W:] = sn
        y = _dot(an, w_ref[:ATTN_W, :]) + _dot(sn, w_ref[ATTN_W:, :])
        y_ref[...] = y
        x2 = x_ref[...] + y * _rms(y) * gpm_ref[...]
        x2_ref[...] = x2
        h2_ref[...] = (x2 * _rms(x2) * gpf_ref[...]).astype(BF16)

    wide = jax.ShapeDtypeStruct((SEQ, D_MODEL), F32)
    wide16 = jax.ShapeDtypeStruct((SEQ, D_MODEL), BF16)
    return _call(
        "out_proj", body, SEQ // tm,
        [_row_spec(tm, ATTN_W), _row_spec(tm, SGU_W), _row_spec(tm, D_MODEL), _full_spec((1, ATTN_W)),
         _full_spec((1, SGU_W)), _weight_spec((D_MODEL, D_MODEL)), _full_spec((1, D_MODEL)), _full_spec((1, D_MODEL))],
        [_row_spec(tm, D_MODEL)] * 4, [wide16, wide, wide, wide16],
        (attn, sgu, x, ga, gs, w_out, gpm, gpf))


def ffn_up(h2, w_gate_t, w_up_t):
    tm = 256

    def body(h_ref, wg_hbm, wu_hbm, g_ref, u_ref, a_ref, wg_ref, wu_ref, sems):
        def run(spans, wait):
            h = h_ref[...]
            for c, (o, n) in enumerate(spans):
                if wait:
                    wait(c)
                g = _dot_nt(h, wg_ref[o:o + n, :])
                u = _dot_nt(h, wu_ref[o:o + n, :])
                g_ref[:, o:o + n] = g.astype(BF16)
                u_ref[:, o:o + n] = u.astype(BF16)
                a_ref[:, o:o + n] = (g * jax.nn.sigmoid(g) * u).astype(BF16)

        _with_ffn_weights([wg_hbm, wu_hbm], [wg_ref, wu_ref], sems, run)

    ff = jax.ShapeDtypeStruct((SEQ, D_FF), BF16)
    return _call(
        "ffn_up", body, SEQ // tm, [_row_spec(tm, D_MODEL), ANY, ANY],
        [_row_spec(tm, D_FF)] * 3, [ff, ff, jax.ShapeDtypeStruct((SEQ, D_FF), BF16)],
        (h2, w_gate_t, w_up_t), scratch_shapes=_ffn_weight_scratch(2))


def ffn_down_loss(act, w_down, x2, gpo, target):
    tm = 512

    def body(a_ref, w_hbm, x2_ref, g_ref, t_ref, df_ref, dx3_ref, dg_ref, loss_ref, w_ref, sems):
        def run(spans, wait):
            f = None
            for c, (o, n) in enumerate(spans):
                if wait:
                    wait(c)
                part = _dot(a_ref[:, o:o + n], w_ref[o:o + n, :])
                f = part if f is None else f + part
            gain = g_ref[...]
            err = x2_ref[...] + f * _rms(f) * gain - t_ref[...]
            dx3 = err * np.float32(1.0 / D_MODEL)
            dx3_ref[...] = dx3
            df, dg = _rms_bwd(f, gain, dx3)
            df_ref[...] = df.astype(BF16)
            loss = jnp.sum(err * err, axis=(0, 1), keepdims=True)
            if wait:
                dg_ref[...] = dg
                loss_ref[...] = loss
            else:
                dg_ref[...] += dg
                loss_ref[...] += loss

        _with_ffn_weights([w_hbm], [w_ref], sems, run)

    return _call(
        "ffn_down_loss", body, SEQ // tm,
        [_row_spec(tm, D_FF), ANY, _row_spec(tm, D_MODEL), _full_spec((1, D_MODEL)), _row_spec(tm, D_MODEL)],
        [_row_spec(tm, D_MODEL), _row_spec(tm, D_MODEL), _full_spec((1, D_MODEL)), _full_spec((1, 1))],
        [jax.ShapeDtypeStruct((SEQ, D_MODEL), BF16), jax.ShapeDtypeStruct((SEQ, D_MODEL), F32),
         jax.ShapeDtypeStruct((1, D_MODEL), F32), jax.ShapeDtypeStruct((1, 1), F32)],
        (act, w_down, x2, gpo, target), scratch_shapes=_ffn_weight_scratch(1))


def ffn_bwd(df, w_down, gate, up, w_gate_t, w_up_t, x2, gpf, dx3, y, gpm, after=()):
    tm = 256

    def body(df_ref, wd_hbm, g_ref, u_ref, wg_hbm, wu_hbm, x2_ref, gpf_ref, dx3_ref, y_ref, gpm_ref,
             dg_ref, du_ref, dx2_ref, dy_ref, dgpf_ref, dgpm_ref, wd_ref, wg_ref, wu_ref, sems):
        def run(spans, wait):
            df = df_ref[...]
            dh2 = None
            for c, (o, n) in enumerate(spans):
                if wait:
                    wait(c)
                dact = _dot_nt(df, wd_ref[o:o + n, :])
                g = g_ref[:, o:o + n].astype(F32)
                s = jax.nn.sigmoid(g)
                dup = (dact * g * s).astype(BF16)
                dgate = (dact * u_ref[:, o:o + n].astype(F32) * (s * (1.0 + g * (1.0 - s)))).astype(BF16)
                du_ref[:, o:o + n] = dup
                dg_ref[:, o:o + n] = dgate
                part = _dot(dgate, wg_ref[o:o + n, :]) + _dot(dup, wu_ref[o:o + n, :])
                dh2 = part if dh2 is None else dh2 + part
            dz, dgpf = _rms_bwd(x2_ref[...], gpf_ref[...], dh2)
            dx2 = dx3_ref[...] + dz
            dx2_ref[...] = dx2
            dy, dgpm = _rms_bwd(y_ref[...], gpm_ref[...], dx2)
            dy_ref[...] = dy.astype(BF16)
            if wait:
                dgpf_ref[...] = dgpf
                dgpm_ref[...] = dgpm
            else:
                dgpf_ref[...] += dgpf
                dgpm_ref[...] += dgpm

        _with_ffn_weights([wd_hbm, wg_hbm, wu_hbm], [wd_ref, wg_ref, wu_ref], sems, run)

    vec = jax.ShapeDtypeStruct((1, D_MODEL), F32)
    ff16 = jax.ShapeDtypeStruct((SEQ, D_FF), BF16)
    return _call(
        "ffn_bwd", body, SEQ // tm,
        [_row_spec(tm, D_MODEL), ANY, _row_spec(tm, D_FF), _row_spec(tm, D_FF), ANY, ANY, _row_spec(tm, D_MODEL),
         _full_spec((1, D_MODEL)), _row_spec(tm, D_MODEL), _row_spec(tm, D_MODEL), _full_spec((1, D_MODEL))],
        [_row_spec(tm, D_FF), _row_spec(tm, D_FF), _row_spec(tm, D_MODEL), _row_spec(tm, D_MODEL),
         _full_spec((1, D_MODEL)), _full_spec((1, D_MODEL))],
        [ff16, ff16, jax.ShapeDtypeStruct((SEQ, D_MODEL), F32), jax.ShapeDtypeStruct((SEQ, D_MODEL), BF16), vec, vec],
        (df, w_down, gate, up, w_gate_t, w_up_t, x2, gpf, dx3, y, gpm), scratch_shapes=_ffn_weight_scratch(3), after=after)


def weight_grads(name, lhs, b, after=()):
    m, n, k = lhs[0].shape[1], b.shape[1], len(lhs)
    tr = 256

    def body(*refs):
        for a_ref, o_ref in zip(refs[:k], refs[k + 1:]):
            o_ref[...] = _dot_tn(a_ref[...], refs[k][...]).astype(BF16)

    outs = _call(
        name, body, m // tr, [pl.BlockSpec((SEQ, tr), lambda i: (0, i))] * k + [_weight_spec((SEQ, n))],
        [_row_spec(tr, n)] * k, [jax.ShapeDtypeStruct((m, n), BF16)] * k, (*lhs, b), after=after)
    return [out.reshape(N_DEV, m // N_DEV, n) for out in outs]


def weight_grad(name, a, b, after=()):
    return weight_grads(name, [a], b, after)[0]


def weight_grad_of_parts(name, parts, b, after=()):
    p, n, k = parts[0].shape[1], b.shape[1], len(parts)
    tr = 512
    per = p // tr

    def body(*refs):
        tile = pl.program_id(0)
        for j in range(k):
            @pl.when(tile // per == j)
            def _(j=j):
                refs[k + 1][...] = _dot_tn(refs[j][...], refs[k][...]).astype(BF16)

    def part_spec(j):
        return pl.BlockSpec((SEQ, tr), lambda i: (0, jnp.clip(i - per * j, 0, per - 1)))

    (out,) = _call(
        name, body, k * per, [part_spec(j) for j in range(k)] + [_weight_spec((SEQ, n))],
        [_row_spec(tr, n)], [jax.ShapeDtypeStruct((k * p, n), BF16)], (*parts, b), after=after)
    return out.reshape(N_DEV, k * p // N_DEV, n)


def mix_bwd(dy, w_out, attn, sgu, ga, gs, after=()):
    tm = 512
    n_steps = SEQ // tm

    def body(dy_ref, w_ref, a_ref, s_ref, ga_ref, gs_ref, ds_ref, dga_ref, dgs_ref, da_ref, scratch, sems):
        dy = dy_ref[...]
        da, dga = _rms_bwd(a_ref[...], ga_ref[...], _dot_nt(dy, w_ref[:ATTN_W, :]))
        ds, dgs = _rms_bwd(s_ref[...], gs_ref[...], _dot_nt(dy, w_ref[ATTN_W:, :]))
        ds_ref[...] = ds
        _to_residue_rows([da], [da_ref], scratch, sems, tm, n_steps)

        @pl.when(pl.program_id(0) == 0)
        def _():
            dga_ref[...] = jnp.zeros_like(dga_ref)
            dgs_ref[...] = jnp.zeros_like(dgs_ref)

        dga_ref[...] += dga
        dgs_ref[...] += dgs

    half = jax.ShapeDtypeStruct((SEQ, 512), F32)
    vec = jax.ShapeDtypeStruct((1, 512), F32)
    return _call(
        "mix_bwd", body, n_steps,
        [_row_spec(tm, D_MODEL), _weight_spec((D_MODEL, D_MODEL)), _row_spec(tm, 512), _row_spec(tm, 512),
         _full_spec((1, 512)), _full_spec((1, 512))],
        [_row_spec(tm, 512), _full_spec((1, 512)), _full_spec((1, 512)), ANY],
        [half, vec, vec, half], (dy, w_out, attn, sgu, ga, gs), scratch_shapes=_residue_scratch(1, tm, ATTN_W), after=after)


def sgu_bwd(u, vs, dsgu, lg, lb, w_sp, bfull):
    cpb = 4

    def body(u_ref, vs_ref, d_ref, lg_ref, lb_ref, w_ref, b_ref, du_ref, dvs_ref, dlg_ref, dlb_ref, dw_ref, db_ref):
        wc, causal = _causal_weights(w_ref)
        head0 = lax.broadcasted_iota(jnp.int32, (CHUNK, 128), 1) < HEAD_DIM
        lg = lg_ref[...]

        @pl.when(pl.program_id(0) == 0)
        def _():
            dlg_ref[...] = jnp.zeros_like(dlg_ref)
            dlb_ref[...] = jnp.zeros_like(dlb_ref)
            dw_ref[...] = jnp.zeros_like(dw_ref)
            db_ref[...] = jnp.zeros_like(db_ref)

        for ci in range(cpb):
            rows = pl.ds(ci * CHUNK, CHUNK)
            u = u_ref[rows, :]
            vs = vs_ref[rows, :]
            d = d_ref[rows, :]
            ug, xhat, rstd, vn, ms = _sgu_chunk_fwd(u, vs, lg, lb_ref[...], wc, b_ref[...], head0)
            du_ref[rows, :] = (d * ms * _gelu_grad(u)).astype(BF16)
            dms = d * ug
            db_ref[...] += dms
            dvn = []
            for gp in range(SGU_W // 128):
                dmp = dms[:, gp * 128:(gp + 1) * 128]
                dm0 = jnp.where(head0, dmp, 0.0).astype(BF16)
                dm1 = jnp.where(head0, 0.0, dmp).astype(BF16)
                vp = vn[:, gp * 128:(gp + 1) * 128].astype(BF16)
                dw_ref[2 * gp] += _dot_nt(dm0, vp)
                dw_ref[2 * gp + 1] += _dot_nt(dm1, vp)
                dvn.append(_dot_tn(wc[2 * gp], dm0) + _dot_tn(wc[2 * gp + 1], dm1))
            dvn = jnp.concatenate(dvn, axis=1)
            dlg_ref[...] += jnp.sum(dvn * xhat, axis=0, keepdims=True)
            dlb_ref[...] += jnp.sum(dvn, axis=0, keepdims=True)
            dxh = dvn * lg
            dvg = rstd * (dxh - jnp.mean(dxh, axis=-1, keepdims=True) - xhat * jnp.mean(dxh * xhat, axis=-1, keepdims=True))
            dvs_ref[rows, :] = (dvg * _gelu_grad(vs)).astype(BF16)

        @pl.when(pl.program_id(0) == pl.num_programs(0) - 1)
        def _():
            for g in range(N_GROUPS):
                dw_ref[g] = jnp.where(causal, dw_ref[g], 0.0)

    tm = cpb * CHUNK
    half16 = jax.ShapeDtypeStruct((SEQ, SGU_W), BF16)
    vec = jax.ShapeDtypeStruct((1, SGU_W), F32)
    return _call(
        "sgu_bwd", body, SEQ // tm,
        [_row_spec(tm, SGU_W)] * 3 + [_full_spec((1, SGU_W)), _full_spec((1, SGU_W)),
                                      _full_spec((N_GROUPS, CHUNK, CHUNK)), _full_spec((CHUNK, SGU_W))],
        [_row_spec(tm, SGU_W), _row_spec(tm, SGU_W), _full_spec((1, SGU_W)), _full_spec((1, SGU_W)),
         _full_spec((N_GROUPS, CHUNK, CHUNK)), _full_spec((CHUNK, SGU_W))],
        [half16, half16, vec, vec, jax.ShapeDtypeStruct((N_GROUPS, CHUNK, CHUNK), F32),
         jax.ShapeDtypeStruct((CHUNK, SGU_W), F32)],
        (u, vs, dsgu, lg, lb, w_sp, bfull))


def attn_bwd(q, k, v, o, lse, do, pos_col, rot):
    def body(q_ref, k_ref, v_ref, o_ref, lse_ref, do_ref, pos_ref, invf_ref, ma_ref, mb_ref,
             dq_ref, dk_ref, dv_ref, dqa_ref, dka_ref, dva_ref, dlt_ref, rot_ref):
        dqa_ref[...] = jnp.zeros_like(dqa_ref)
        dka_ref[...] = jnp.zeros_like(dka_ref)
        dva_ref[...] = jnp.zeros_like(dva_ref)

        def delta(i, carry):
            rows = pl.ds(pl.multiple_of(i * 256, 256), 256)
            prod = do_ref[rows, :] * o_ref[rows, :]
            h0 = lax.broadcasted_iota(jnp.int32, (256, 128), 1) < HEAD_DIM
            d0 = jnp.sum(jnp.where(h0, prod, 0.0), axis=-1, keepdims=True)
            d1 = jnp.sum(jnp.where(h0, 0.0, prod), axis=-1, keepdims=True)
            dlt_ref[rows, :] = jnp.where(h0, d0, d1)
            return carry

        lax.fori_loop(0, SEQ // 256, delta, 0)

        def add_rows(ref, slices, val):
            at = 0
            for start, size in slices:
                ref[pl.ds(start, size), :] += val[at:at + size]
                at += size

        def group(p, masks, blocks):
            head0, mask1, mask2 = masks
            heads = (head0, jnp.logical_not(head0))
            keys = [rows if prev is None else prev + rows for rows, prev in blocks]
            mask = [mask1 if prev is None else mask2 for _, prev in blocks]
            kk = [_load_rows(k_ref, ks).astype(BF16) for ks in keys]
            vv = [_load_rows(v_ref, ks).astype(BF16) for ks in keys]
            qb = [_load_rows(q_ref, rows) for rows, _ in blocks]
            dob = [_load_rows(do_ref, rows) for rows, _ in blocks]
            lse_b = [_load_rows(lse_ref, rows) for rows, _ in blocks]
            dlt_b = [_load_rows(dlt_ref, rows) for rows, _ in blocks]
            chains = [(g, h) for g in range(len(blocks)) for h in range(2)]
            qm = [jnp.where(heads[h], qb[g], 0.0).astype(BF16) for g, h in chains]
            dom = [jnp.where(heads[h], dob[g], 0.0).astype(BF16) for g, h in chains]
            s = [_dot_nt(qm[c], kk[g]) for c, (g, h) in enumerate(chains)]
            dp = [_dot_nt(dom[c], vv[g]) for c, (g, h) in enumerate(chains)]
            pr = [jnp.where(mask[g], jnp.exp(s[c] - lse_b[g][:, h * HEAD_DIM:h * HEAD_DIM + 1]), 0.0)
                  for c, (g, h) in enumerate(chains)]
            ds = [(pr[c] * (dp[c] - dlt_b[g][:, h * HEAD_DIM:h * HEAD_DIM + 1])).astype(BF16)
                  for c, (g, h) in enumerate(chains)]
            dv = [_dot_tn(pr[c].astype(BF16), dom[c]) for c in range(len(chains))]
            dk = [_dot_tn(ds[c], qm[c]) for c in range(len(chains))]
            dq = [_dot(ds[c], kk[g]) for c, (g, h) in enumerate(chains)]
            for g, (rows, _) in enumerate(blocks):
                add_rows(dqa_ref, rows, jnp.where(head0, dq[2 * g], dq[2 * g + 1]))
                add_rows(dka_ref, keys[g], dk[2 * g] + dk[2 * g + 1])
                add_rows(dva_ref, keys[g], dv[2 * g] + dv[2 * g + 1])

        _for_each_group(group)

        @pl.when(pl.program_id(0) == 0)
        def _():
            def tables(i, carry):
                rows = pl.ds(pl.multiple_of(i * 256, 256), 256)
                c, sa, sb = _rot_tables(pos_ref[rows, :], invf_ref[...], ma_ref[...], mb_ref[...])
                rot_ref[0, rows, :] = c
                rot_ref[1, rows, :] = sa
                rot_ref[2, rows, :] = sb
                return carry

            lax.fori_loop(0, SEQ // 256, tables, 0)

        def finish(i, carry):
            rows = pl.ds(pl.multiple_of(i * 256, 256), 256)
            c, sa, sb = rot_ref[0, rows, :], rot_ref[1, rows, :], rot_ref[2, rows, :]
            dq_ref[rows, :] = _rot_t(dqa_ref[rows, :] * Q_SCALE, c, sa, sb).astype(BF16)
            dk_ref[rows, :] = _rot_t(dka_ref[rows, :], c, sa, sb).astype(BF16)
            dv_ref[rows, :] = dva_ref[rows, :].astype(BF16)
            return carry

        lax.fori_loop(0, SEQ // 256, finish, 0)

    slab = pl.BlockSpec((SEQ, 128), lambda i: (0, i))
    out = jax.ShapeDtypeStruct((SEQ, ATTN_W), BF16)
    acc = pltpu.VMEM((SEQ, 128), F32)
    return _call(
        "attn_bwd", body, ATTN_W // 128,
        [slab] * 6 + [_full_spec((SEQ, 1)), _full_spec((1, 128)), _full_spec((1, 128)), _full_spec((1, 128))],
        [slab] * 3, [out, out, out], (q, k, v, o, lse, do, pos_col, *rot),
        scratch_shapes=[acc, acc, acc, acc, pltpu.VMEM((3, SEQ, 128), F32)])


def in_bwd(dproj_parts, w_in_t, x, g1, dx2, after=()):
    tm = 512
    k = len(dproj_parts)

    def body(*refs):
        w_ref, x_ref, g_ref, dx2_ref, dx_ref, dg_ref = refs[k:]
        dh1 = _dot(refs[0][...], w_ref[0:512, :])
        for j in range(1, k):
            dh1 = dh1 + _dot(refs[j][...], w_ref[512 * j:512 * (j + 1), :])
        dz, dg = _rms_bwd(x_ref[...], g_ref[...], dh1)
        dx_ref[...] = dx2_ref[...] + dz

        @pl.when(pl.program_id(0) == 0)
        def _():
            dg_ref[...] = jnp.zeros_like(dg_ref)

        dg_ref[...] += dg

    return _call(
        "in_bwd", body, SEQ // tm,
        [_row_spec(tm, 512)] * k + [_weight_spec((IN_W, D_MODEL)), _row_spec(tm, D_MODEL), _full_spec((1, D_MODEL)),
                                    _row_spec(tm, D_MODEL)],
        [_row_spec(tm, D_MODEL), _full_spec((1, D_MODEL))],
        [jax.ShapeDtypeStruct((SEQ, D_MODEL), F32), jax.ShapeDtypeStruct((1, D_MODEL), F32)],
        (*dproj_parts, w_in_t, x, g1, dx2), after=after)


def _coords():
    return lax.axis_index("x"), lax.axis_index("y"), lax.axis_index("c")


class Exchange:
    def __init__(self, srcs, bufs, new_shapes, n_sems, make):
        self.srcs, self.bufs, self.new_shapes, self.n_sems, self.make = list(srcs), list(bufs), list(new_shapes), n_sems, make


def _call(name, body, n_steps, in_specs, out_specs, out_shape, args, scratch_shapes=(), after=()):
    n_in = len(args)

    def wrapped(*refs):
        body(*refs[:n_in], *refs[n_in + len(after):])

    return list(pl.pallas_call(
        wrapped, name=name, grid=(n_steps,), in_specs=list(in_specs) + [ANY] * len(after), out_specs=list(out_specs),
        out_shape=list(out_shape), scratch_shapes=list(scratch_shapes), compiler_params=_params(),
    )(*args, *after))


GATHER_SEMS = 8


def gather(bufs):
    n = len(bufs)

    def make(src_refs, buf_refs, new_refs, send_sems, recv_sems):
        x, y, c = _coords()
        me, sibling = (x, y, c), (x, y, 1 - c)
        over_x, over_y, across = (1 - x, y), (x, 1 - y), (1 - x, 1 - y)

        def copy(a, k, block, to, half=None):
            r = buf_refs[a].shape[0] // N_DEV
            lo, size = (0, r) if half is None else (half * (r // 2), r // 2)
            rows = buf_refs[a].at[pl.ds((4 * block[0] + 2 * block[1] + block[2]) * r + lo, size), :]
            return pltpu.make_async_remote_copy(
                src_ref=rows, dst_ref=rows, send_sem=send_sems.at[GATHER_SEMS * a + k],
                recv_sem=recv_sems.at[GATHER_SEMS * a + k], device_id=to, device_id_type=MESH)

        every = range(n)
        out = ([copy(a, 0, me, sibling) for a in every] + [copy(a, 1, me, (*over_x, c)) for a in every]
               + [copy(a, 2, me, (*over_y, c)) for a in every])
        near_in = [copy(a, 1, (*over_x, c), me) for a in every] + [copy(a, 2, (*over_y, c), me) for a in every]
        relay = ([copy(a, 3, (*over_x, c), (*over_y, c), half=0) for a in every]
                 + [copy(a, 4, (*over_y, c), (*over_x, c), half=1) for a in every])
        near_on = [copy(a, 5, (*over_x, c), sibling) for a in every] + [copy(a, 6, (*over_y, c), sibling) for a in every]
        relay_in = ([copy(a, 3, (*across, c), me, half=0) for a in every]
                    + [copy(a, 4, (*across, c), me, half=1) for a in every])
        far_on = [copy(a, 7, (*across, c), sibling) for a in every]
        from_core = ([copy(a, 0, sibling, me) for a in every] + [copy(a, 5, (*over_x, 1 - c), me) for a in every]
                     + [copy(a, 6, (*over_y, 1 - c), me) for a in every] + [copy(a, 7, (*across, 1 - c), me) for a in every])
        stages = [([], out), (near_in, relay + near_on), (relay_in, far_on)]
        return stages, out + relay + near_on + far_on, from_core

    return Exchange([], bufs, [], GATHER_SEMS * n, make)


TO_GATHER = (1, lambda x, y, c: [(x, y, 1 - c), (1 - x, y, c), (x, 1 - y, c)])
TO_SIBLING = (2, lambda x, y, c: [(x, y, 1 - c)])
TO_CHIPS = (3, lambda x, y, c: [(1 - x, y, c), (x, 1 - y, c), (1 - x, 1 - y, c)])
TO_ALL = (4, lambda x, y, c: [(x ^ (m >> 2), y ^ ((m >> 1) & 1), c ^ (m & 1)) for m in range(1, N_DEV)])


def by_sequencer(name, exchanges, who):
    collective_id, peers_of = who
    hbm = pltpu.MemorySpace.HBM
    refs = [([jax.new_ref(a, memory_space=hbm) for a in ex.srcs], [jax.new_ref(a, memory_space=hbm) for a in ex.bufs],
             [jax.empty_ref(s, memory_space=hbm) for s in ex.new_shapes]) for ex in exchanges]
    sems = []
    for ex in exchanges:
        sems += [pltpu.SemaphoreType.DMA((ex.n_sems,)), pltpu.SemaphoreType.DMA((ex.n_sems,))]

    @pl.kernel(mesh=plsc.ScalarSubcoreMesh(axis_name="sequencer", num_cores=1), name=name, scratch_types=tuple(sems),
               compiler_params=pltpu.CompilerParams(collective_id=collective_id))
    def launch(*sem_refs):
        peers = peers_of(*_coords())
        barrier = pltpu.get_barrier_semaphore()
        for peer in peers:
            pl.semaphore_signal(barrier, inc=1, device_id=peer, device_id_type=MESH)
        pl.semaphore_wait(barrier, len(peers))

        made = [ex.make(*refs[k], sem_refs[2 * k], sem_refs[2 * k + 1]) for k, ex in enumerate(exchanges)]
        for stage in range(max(len(stages) for stages, _, _ in made)):
            for stages, _, _ in made:
                if stage < len(stages):
                    arrivals, starts = stages[stage]
                    for cp in arrivals:
                        cp.wait_recv()
                    for cp in starts:
                        cp.start()
        for _, sends, arrivals in made:
            for cp in arrivals:
                cp.wait_recv()
            for cp in sends:
                cp.wait_send()

    launch()
    return [([ref[...] for ref in bufs], [ref[...] for ref in news]) for _, bufs, news in refs]


def place_shards(name, shards, dev):
    n = len(shards)

    def body(dev_ref, *refs):
        for a in range(n):
            refs[n + a][...] = refs[a][...].astype(BF16)

    spec = pltpu.PrefetchScalarGridSpec(
        num_scalar_prefetch=1, grid=(1,),
        in_specs=[pl.BlockSpec(s.shape, lambda i, dev_ref: (0, 0)) for s in shards],
        out_specs=[pl.BlockSpec(s.shape, lambda i, dev_ref: (dev_ref[0], 0)) for s in shards])
    return pl.pallas_call(
        body, name=name, grid_spec=spec,
        out_shape=[jax.ShapeDtypeStruct((N_DEV * s.shape[0], s.shape[1]), BF16) for s in shards],
        compiler_params=_params(),
    )(dev, *shards)


def _swap(copies_of):
    def make(src_refs, buf_refs, new_refs, send_sems, recv_sems):
        copies = copies_of(src_refs, new_refs, send_sems, recv_sems)
        return [([], copies)], copies, copies

    return make


def to_sibling(grads):
    def copies_of(src_refs, new_refs, send_sems, recv_sems):
        x, y, c = _coords()
        return [pltpu.make_async_remote_copy(
            src_ref=src_refs[a].at[2 * xy + 1 - c], dst_ref=new_refs[a].at[xy], send_sem=send_sems.at[4 * a + xy],
            recv_sem=recv_sems.at[4 * a + xy], device_id=(x, y, 1 - c), device_id_type=MESH)
            for a in range(len(src_refs)) for xy in range(4)]

    return Exchange(grads, [], [jax.ShapeDtypeStruct((4,) + g.shape[1:], g.dtype) for g in grads], 4 * len(grads),
                    _swap(copies_of))


def to_chips(parts):
    def copies_of(src_refs, new_refs, send_sems, recv_sems):
        x, y, c = _coords()
        chips = [(1 - x, y), (x, 1 - y), (1 - x, 1 - y)]
        return [pltpu.make_async_remote_copy(
            src_ref=src_refs[a].at[2 * px + py], dst_ref=new_refs[a].at[2 * x + y], send_sem=send_sems.at[3 * a + j],
            recv_sem=recv_sems.at[3 * a + j], device_id=(px, py, c), device_id_type=MESH)
            for a in range(len(src_refs)) for j, (px, py) in enumerate(chips)]

    return Exchange(parts, [], [jax.ShapeDtypeStruct(p.shape, p.dtype) for p in parts], 3 * len(parts), _swap(copies_of))


def to_owners(grad):
    def copies_of(src_refs, new_refs, send_sems, recv_sems):
        x, y, c = _coords()
        copies = []
        for m in range(1, N_DEV):
            px, py, pc = x ^ (m >> 2), y ^ ((m >> 1) & 1), c ^ (m & 1)
            copies.append(pltpu.make_async_remote_copy(
                src_ref=src_refs[0].at[4 * px + 2 * py + pc], dst_ref=new_refs[0].at[4 * x + 2 * y + c],
                send_sem=send_sems.at[m - 1], recv_sem=recv_sems.at[m - 1], device_id=(px, py, pc), device_id_type=MESH))
        return copies

    return Exchange([grad], [], [jax.ShapeDtypeStruct(grad.shape, grad.dtype)], N_DEV - 1, _swap(copies_of))


def to_everyone(vec):
    def copies_of(src_refs, new_refs, send_sems, recv_sems):
        x, y, c = _coords()
        copies = []
        for m in range(1, N_DEV):
            px, py, pc = x ^ (m >> 2), y ^ ((m >> 1) & 1), c ^ (m & 1)
            copies.append(pltpu.make_async_remote_copy(
                src_ref=src_refs[0], dst_ref=new_refs[0].at[4 * x + 2 * y + c],
                send_sem=send_sems.at[m - 1], recv_sem=recv_sems.at[m - 1], device_id=(px, py, pc), device_id_type=MESH))
        return copies

    return Exchange([vec], [], [jax.ShapeDtypeStruct((N_DEV,) + vec.shape, vec.dtype)], N_DEV - 1, _swap(copies_of))


def sum_cores(name, grads, others, core, after=()):
    k = len(grads)

    def body(core_ref, *refs):
        for j in range(k):
            out_ref = refs[2 * k + len(after) + j]
            out_ref[...] = (refs[j][:, 0].astype(F32) + refs[k + j][...].astype(F32)).astype(out_ref.dtype)

    mine = [pl.BlockSpec((2, 1) + o.shape[1:], lambda i, core_ref: (i, core_ref[0], 0, 0)) for o in others]
    theirs = [pl.BlockSpec((2,) + o.shape[1:], lambda i, core_ref: (i, 0, 0)) for o in others]
    return pl.pallas_call(
        body, name=name,
        grid_spec=pltpu.PrefetchScalarGridSpec(
            num_scalar_prefetch=1, grid=(2,), in_specs=mine + theirs + [ANY] * len(after), out_specs=theirs),
        out_shape=[jax.ShapeDtypeStruct(o.shape, o.dtype) for o in others],
        compiler_params=_params(),
    )(core, *[g.reshape((4, 2) + g.shape[1:]) for g in grads], *others, *after)


def sum_owned(name, grad, others, dev_ids, after=()):
    _, r, w = grad.shape

    def body(ids_ref, *refs):
        acc = refs[0][0]
        for k in range(1, N_DEV):
            acc = acc + refs[k][0]
        refs[-1][...] = acc

    def pick(k):
        return pl.BlockSpec((1, r, w), lambda i, ids_ref: (ids_ref[k], 0, 0))

    return pl.pallas_call(
        body, name=name,
        grid_spec=pltpu.PrefetchScalarGridSpec(
            num_scalar_prefetch=1, grid=(1,), in_specs=[pick(k) for k in range(N_DEV)] + [ANY] * len(after),
            out_specs=pl.BlockSpec((r, w), lambda i, ids_ref: (ids_ref[0], 0))),
        out_shape=jax.ShapeDtypeStruct((N_DEV * r, w), F32),
        compiler_params=_params(),
    )(dev_ids, grad, *([others] * (N_DEV - 1)), *after)


def _adamw_update(w, g, m, v):
    nm = ADAM_B1 * m + np.float32(1.0 - ADAM_B1) * g
    nv = ADAM_B2 * v + np.float32(1.0 - ADAM_B2) * (g * g)
    m_hat = nm / np.float32(1.0 - ADAM_B1 ** ADAM_STEP)
    v_hat = nv / np.float32(1.0 - ADAM_B2 ** ADAM_STEP)
    return -ADAM_LR * (m_hat / (jnp.sqrt(v_hat) + ADAM_EPS) + ADAM_WD * w), nm, nv


def adamw_of_sums(name, parts, others, chip_ids, ws, ms, vs, after):
    n = len(parts)
    halves = 2

    def body(ids_ref, *refs):
        outs = refs[7 * n + 1:]
        for j in range(n):
            p_ref, a_ref, b_ref, c_ref, w_ref, m_ref, v_ref = refs[7 * j:7 * j + 7]
            g = ((p_ref[0].astype(F32) + a_ref[0].astype(F32)) + b_ref[0].astype(F32)) + c_ref[0].astype(F32)
            outs[4 * j][...] = g
            outs[4 * j + 1][...], outs[4 * j + 2][...], outs[4 * j + 3][...] = _adamw_update(w_ref[...], g, m_ref[...], v_ref[...])

    in_specs, out_specs, out_shape, operands = [], [], [], []
    for part, other, w, m, v in zip(parts, others, ws, ms, vs):
        _, r, wd = part.shape
        rows = r // halves
        whole = pl.BlockSpec((rows, wd), lambda i, ids_ref: (i, 0))
        in_specs += [pl.BlockSpec((1, rows, wd), lambda i, ids_ref, k=k: (ids_ref[k], i, 0)) for k in range(4)] + [whole] * 3
        out_specs += [whole] * 4
        out_shape += [jax.ShapeDtypeStruct((r, wd), F32)] * 4
        operands += [part, other, other, other, w, m, v]
    outs = pl.pallas_call(
        body, name=name,
        grid_spec=pltpu.PrefetchScalarGridSpec(
            num_scalar_prefetch=1, grid=(halves,), in_specs=in_specs + [ANY], out_specs=out_specs),
        out_shape=out_shape,
        compiler_params=_params(),
    )(chip_ids, *operands, after)
    return [tuple(outs[4 * j:4 * j + 4]) for j in range(n)]


def pack_small(parts):
    names = [name for name, _ in SMALL if name in parts]
    operands = [parts[name] for name in names]
    first_row, at = {}, 0
    for name, size in SMALL:
        first_row[name] = at // 128
        at += size
    sizes = dict(SMALL)

    def body(*refs):
        out_ref = refs[-1]
        out_ref[...] = jnp.zeros_like(out_ref)
        for name, ref in zip(names, refs):
            row = first_row[name]
            if name == "loss_sum":
                lane0 = lax.broadcasted_iota(jnp.int32, (1, 128), 1) == 0
                out_ref[row:row + 1, :] = jnp.where(lane0, ref[...], 0.0)
            else:
                rows = sizes[name] // 128
                out_ref[row:row + rows, :] = ref[...].reshape(rows, 128)

    vmem = pl.BlockSpec(memory_space=pltpu.VMEM)
    return pl.pallas_call(
        body, name="pack_small", in_specs=[vmem] * len(names), out_specs=vmem,
        out_shape=jax.ShapeDtypeStruct((SMALL_ROWS, 128), F32), compiler_params=_params(()),
    )(*operands)


LATE = "pre_mix_norm"


def adamw_small(packed_g, late_parts, ws, ms, vs):
    names = [name for name, _ in SMALL if name != "loss_sum"]
    k = len(names)
    shapes = [ws[name].shape[1:] if ws[name].ndim > 2 else ws[name].shape for name in names]
    first_row, at = [], 0
    for name, size in SMALL:
        first_row.append(at // 128)
        at += size

    def body(g_ref, late_ref, *refs):
        w_refs, m_refs, v_refs, outs = refs[:k], refs[k:2 * k], refs[2 * k:3 * k], refs[3 * k:]
        for i, (name, size) in enumerate(SMALL[:k]):
            if name == LATE:
                g = late_ref[0]
                for j in range(1, N_DEV):
                    g = g + late_ref[j]
            else:
                g = g_ref[first_row[i]:first_row[i] + size // 128, :].reshape(shapes[i])
            outs[i][...] = g
            outs[k + i][...], outs[2 * k + i][...], outs[3 * k + i][...] = _adamw_update(
                w_refs[i][...], g, m_refs[i][...], v_refs[i][...])
        outs[4 * k][...] = g_ref[first_row[k]:first_row[k] + 1, 0:1]

    vmem = pl.BlockSpec(memory_space=pltpu.VMEM)
    operands = [t[name].reshape(shape) for t in (ws, ms, vs) for name, shape in zip(names, shapes)]
    outs = pl.pallas_call(
        body, name="adamw_small", in_specs=[vmem] * (2 + 3 * k), out_specs=[vmem] * (4 * k + 1),
        out_shape=[jax.ShapeDtypeStruct(shape, F32) for _ in range(4) for shape in shapes] + [jax.ShapeDtypeStruct((1, 1), F32)],
        compiler_params=_params(()),
    )(packed_g, late_parts, *operands)
    tables = [{name: outs[j * k + i].reshape(ws[name].shape) for i, name in enumerate(names)} for j in range(4)]
    return (*tables, outs[4 * k])


def kernel(x, positions, pre_mix_norm, w_in, sgu_ln_gain, sgu_ln_bias, sgu_w_spatial, sgu_b_spatial, attn_out_norm, sgu_out_norm, w_out, post_mix_norm, pre_ffn_norm, w_gate, w_up, w_down, post_ffn_norm, loss_target, m_pre_mix_norm, m_w_in, m_sgu_ln_gain, m_sgu_ln_bias, m_sgu_w_spatial, m_sgu_b_spatial, m_attn_out_norm, m_sgu_out_norm, m_w_out, m_post_mix_norm, m_pre_ffn_norm, m_w_gate, m_w_up, m_w_down, m_post_ffn_norm, v_pre_mix_norm, v_w_in, v_sgu_ln_gain, v_sgu_ln_bias, v_sgu_w_spatial, v_sgu_b_spatial, v_attn_out_norm, v_sgu_out_norm, v_w_out, v_post_mix_norm, v_pre_ffn_norm, v_w_gate, v_w_up, v_w_down, v_post_ffn_norm):
    small_w = dict(pre_mix_norm=pre_mix_norm, sgu_ln_gain=sgu_ln_gain, sgu_ln_bias=sgu_ln_bias, sgu_w_spatial=sgu_w_spatial,
                   sgu_b_spatial=sgu_b_spatial, attn_out_norm=attn_out_norm, sgu_out_norm=sgu_out_norm,
                   post_mix_norm=post_mix_norm, pre_ffn_norm=pre_ffn_norm, post_ffn_norm=post_ffn_norm)
    small_m = dict(pre_mix_norm=m_pre_mix_norm, sgu_ln_gain=m_sgu_ln_gain, sgu_ln_bias=m_sgu_ln_bias, sgu_w_spatial=m_sgu_w_spatial,
                   sgu_b_spatial=m_sgu_b_spatial, attn_out_norm=m_attn_out_norm, sgu_out_norm=m_sgu_out_norm,
                   post_mix_norm=m_post_mix_norm, pre_ffn_norm=m_pre_ffn_norm, post_ffn_norm=m_post_ffn_norm)
    small_v = dict(pre_mix_norm=v_pre_mix_norm, sgu_ln_gain=v_sgu_ln_gain, sgu_ln_bias=v_sgu_ln_bias, sgu_w_spatial=v_sgu_w_spatial,
                   sgu_b_spatial=v_sgu_b_spatial, attn_out_norm=v_attn_out_norm, sgu_out_norm=v_sgu_out_norm,
                   post_mix_norm=v_post_mix_norm, pre_ffn_norm=v_pre_ffn_norm, post_ffn_norm=v_post_ffn_norm)

    x2d = x[0]
    target = loss_target[0]
    pos_col = positions.reshape(SEQ, 1)
    rot = _rot_consts()
    w_sp = sgu_w_spatial[0]
    bfull = jnp.repeat(sgu_b_spatial[0].T, HEAD_DIM, axis=1)

    x_i, y_i, c_i = (lax.axis_index(a).astype(jnp.int32) for a in MESH_AXES)
    dev = 4 * x_i + 2 * y_i + c_i
    core = c_i.reshape(1)
    chip = 2 * x_i + y_i
    chip_ids = jnp.stack([chip, chip ^ 1, chip ^ 2, chip ^ 3])
    dev_ids = jnp.stack([dev ^ m for m in range(N_DEV)])

    def gathered(name, bufs):
        return by_sequencer(name, [gather(bufs)], TO_GATHER)[0][0]

    def from_sibling(name, grads):
        return by_sequencer(name, [to_sibling(grads)], TO_SIBLING)[0][1]

    def from_chips(name, parts):
        return by_sequencer(name, [to_chips(parts)], TO_CHIPS)[0][1]

    (w_in_t,) = place_shards("place_w_in", [w_in[0].T], dev.reshape(1))
    (w_in_t,) = gathered("gather_w_in", [w_in_t])
    w_gate_t, w_up_t, w_out_f, w_down_f = place_shards(
        "place_weights", [w_gate[0].T, w_up[0].T, w_out[0], w_down[0]], dev.reshape(1))
    (w_out_f,) = gathered("gather_w_out", [w_out_f])
    w_gate_t, w_up_t = gathered("gather_w_gate_up", [w_gate_t, w_up_t])
    (w_down_f,) = gathered("gather_w_down", [w_down_f])

    h1, u, vs, q, k, v = in_proj(x2d, pos_col, pre_mix_norm, w_in_t, rot)
    attn_r, lse, attn = attn_fwd(q, k, v)
    (sgu,) = sgu_fwd(u, vs, sgu_ln_gain, sgu_ln_bias, w_sp, bfull)
    mix, y, x2, h2 = out_proj(attn, sgu, x2d, attn_out_norm, sgu_out_norm, w_out_f, post_mix_norm, pre_ffn_norm)
    gate, up, act = ffn_up(h2, w_gate_t, w_up_t)
    df, dx3, d_post_ffn, sq_err = ffn_down_loss(act, w_down_f, x2, post_ffn_norm, target)

    g_w_down = weight_grad("grad_w_down", act, df)
    (s_down,) = from_sibling("w_down_to_sibling", [g_w_down])
    dgate, dup, dx2, dy, d_pre_ffn, d_post_mix = ffn_bwd(
        df, w_down_f, gate, up, w_gate_t, w_up_t, x2, pre_ffn_norm, dx3, y, post_mix_norm, after=[g_w_down])
    (p_down,) = sum_cores("sum_cores_down", [g_w_down], [s_down], core, after=[dy])
    (c_down,) = from_chips("w_down_to_chips", [p_down])
    g_w_gate, g_w_up = weight_grads("grad_w_gate_up", [dgate, dup], h2, after=[p_down])
    s_gate, s_up = from_sibling("w_gate_up_to_sibling", [g_w_gate, g_w_up])
    g_w_out = weight_grad("grad_w_out", mix, dy, after=[g_w_up, c_down])
    p_gate, p_up = sum_cores("sum_cores_gate_up", [g_w_gate, g_w_up], [s_gate, s_up], core, after=[g_w_out])
    c_gate, c_up = from_chips("w_gate_up_to_chips", [p_gate, p_up])
    (s_out,) = from_sibling("w_out_to_sibling", [g_w_out])
    dsgu, d_attn_out, d_sgu_out, dattn_r = mix_bwd(dy, w_out_f, attn, sgu, attn_out_norm, sgu_out_norm, after=[p_gate, p_up])
    du, dvs, d_ln_gain, d_ln_bias, d_w_sp, d_bfull = sgu_bwd(u, vs, dsgu, sgu_ln_gain, sgu_ln_bias, w_sp, bfull)

    d_b_sp = d_bfull.reshape(CHUNK, N_GROUPS, HEAD_DIM).sum(axis=-1).T
    small_g = pack_small(dict(sgu_ln_gain=d_ln_gain, sgu_ln_bias=d_ln_bias, sgu_w_spatial=d_w_sp, sgu_b_spatial=d_b_sp,
                              attn_out_norm=d_attn_out, sgu_out_norm=d_sgu_out, post_mix_norm=d_post_mix,
                              pre_ffn_norm=d_pre_ffn, post_ffn_norm=d_post_ffn, loss_sum=sq_err))
    small_g = small_g.reshape(N_DEV, SMALL_ROWS // N_DEV, 128)
    ((_, (o_small,)),) = by_sequencer("small_to_owners", [to_owners(small_g)], TO_ALL)

    dq, dk, dv = attn_bwd(q, k, v, attn_r, lse, dattn_r, _to_residue_order(pos_col), rot)
    summed_small = sum_owned("sum_small", small_g, o_small, dev_ids, after=[dq])
    (all_small,) = gathered("gather_small_grads", [summed_small])
    (p_out,) = sum_cores("sum_cores_out", [g_w_out], [s_out], core, after=[dq])
    (c_out,) = from_chips("w_out_to_chips", [p_out])
    dq, dk, dv = (_from_residue_order(t) for t in (dq, dk, dv))
    dproj = [dq, dk, dv, du, dvs]
    g_w_in = weight_grad_of_parts("grad_w_in", dproj, h1, after=[c_gate, c_up])
    (s_in,) = from_sibling("w_in_to_sibling", [g_w_in])
    grad_x, d_pre_mix = in_bwd(dproj, w_in_t, x2d, pre_mix_norm, dx2, after=[g_w_in, all_small])
    (p_in,) = sum_cores("sum_cores_in", [g_w_in], [s_in], core, after=[d_pre_mix, c_out])
    (_, (c_in,)), (_, (late_parts,)) = by_sequencer(
        "last_sums_to_owners", [to_chips([p_in]), to_everyone(d_pre_mix)], TO_ALL)
    late_parts = lax.dynamic_update_slice(late_parts, d_pre_mix[None], (dev, 0, 0))

    def same(t):
        return t

    def turned(t):
        return t.T

    big, last = {}, p_in
    for call, weights in (("adamw_ffn", (("w_down", w_down, p_down, c_down, m_w_down, v_w_down, same),
                                         ("w_gate", w_gate, p_gate, c_gate, m_w_gate, v_w_gate, turned),
                                         ("w_up", w_up, p_up, c_up, m_w_up, v_w_up, turned))),
                          ("adamw_w_out", (("w_out", w_out, p_out, c_out, m_w_out, v_w_out, same),)),
                          ("adamw_w_in", (("w_in", w_in, p_in, c_in, m_w_in, v_w_in, turned),))):
        results = adamw_of_sums(call, [p for _, _, p, _, _, _, _ in weights], [c for _, _, _, c, _, _, _ in weights], chip_ids,
                                [turn(w[0]) for _, w, _, _, _, _, turn in weights], [turn(m[0]) for _, _, _, _, m, _, turn in weights],
                                [turn(vv[0]) for _, _, _, _, _, vv, turn in weights], last)
        for (name, _, _, _, _, _, turn), outs in zip(weights, results):
            big[name] = tuple(turn(t)[None] for t in outs)
        last = results[-1][0]
    sg, sd, snm, snv, loss_sum = adamw_small(all_small, late_parts, small_w, small_m, small_v)
    loss = loss_sum[0, 0] * np.float32(0.5 / D_MODEL)

    names = ["pre_mix_norm", "w_in", "sgu_ln_gain", "sgu_ln_bias", "sgu_w_spatial", "sgu_b_spatial", "attn_out_norm",
             "sgu_out_norm", "w_out", "post_mix_norm", "pre_ffn_norm", "w_gate", "w_up", "w_down", "post_ffn_norm"]
    outs = [loss, grad_x[None]]
    for i, table in enumerate((sg, sd, snm, snv)):
        for name in names:
            outs.append(big[name][i] if name in big else table[name])
    return tuple(outs)
```

```python
import numpy as np
import jax
import jax.numpy as jnp
from jax import lax
from jax.experimental import pallas as pl
from jax.experimental.pallas import tpu as pltpu
from jax.experimental.pallas import tpu_sc as plsc

F32 = jnp.float32
BF16 = jnp.bfloat16

SEQ = 2048
D_MODEL = 1024
ATTN_W = 512
SGU_W = 512
HEAD_DIM = 64
N_GROUPS = 8
CHUNK = 128
D_FF = 2816
IN_W = 3 * ATTN_W + 2 * SGU_W
DILATIONS = (1, 4, 16)
ROPE_THETA = 500000.0
ROT_DIM = 16
ROT_HALF = 8
RMS_EPS = 1e-6
LN_EPS = 1e-5
Q_SCALE = 0.125
NEG = -1e30

N_DEV = 8
MESH_AXES = ("x", "y", "c")
MESH = pl.DeviceIdType.MESH

ADAM_LR = 0.001
ADAM_B1 = 0.9
ADAM_B2 = 0.999
ADAM_EPS = 1e-08
ADAM_WD = 0.01
ADAM_STEP = 10

VMEM_LIMIT = 60 * 1024 * 1024
ANY = pl.BlockSpec(memory_space=pl.ANY)

SMALL = (("pre_mix_norm", 1024), ("sgu_ln_gain", 512), ("sgu_ln_bias", 512), ("sgu_w_spatial", 8 * 128 * 128),
         ("sgu_b_spatial", 1024), ("attn_out_norm", 512), ("sgu_out_norm", 512), ("post_mix_norm", 1024),
         ("pre_ffn_norm", 1024), ("post_ffn_norm", 1024), ("loss_sum", 1))
SMALL_ROWS = 1152


def _params(sem=("arbitrary",)):
    return pltpu.CompilerParams(dimension_semantics=sem, vmem_limit_bytes=VMEM_LIMIT)


def _dot(a, b):
    return jnp.dot(a, b, preferred_element_type=F32)


def _dot_nt(a, b):
    return lax.dot_general(a, b, (((1,), (1,)), ((), ())), preferred_element_type=F32)


def _dot_tn(a, b):
    return lax.dot_general(a, b, (((0,), (0,)), ((), ())), preferred_element_type=F32)


def _rms(z):
    return lax.rsqrt(jnp.mean(z * z, axis=-1, keepdims=True) + RMS_EPS)


def _rms_bwd(z, gain, d):
    r = _rms(z)
    n = z * r
    dn = d * gain
    dz = r * (dn - n * jnp.mean(dn * n, axis=-1, keepdims=True))
    return dz, jnp.sum(d * n, axis=0, keepdims=True)


def _gelu(z):
    return 0.5 * z * (1.0 + lax.erf(z * np.float32(1.0 / np.sqrt(2.0))))


def _gelu_grad(z):
    cdf = 0.5 * (1.0 + lax.erf(z * np.float32(1.0 / np.sqrt(2.0))))
    return cdf + z * jnp.exp(-0.5 * z * z) * np.float32(1.0 / np.sqrt(2.0 * np.pi))


def _rot_tables(pos_col, invf, ma, mb):
    ang = pos_col.astype(F32) * invf
    s = jnp.sin(ang)
    return jnp.cos(ang), s * ma, s * mb


def _rot(t, c, sa, sb):
    return t * c + pltpu.roll(t, 120, 1) * sa + pltpu.roll(t, 8, 1) * sb


def _rot_t(d, c, sa, sb):
    return d * c + pltpu.roll(d * sa, 8, 1) + pltpu.roll(d * sb, 120, 1)


def _rot_consts():
    lane = np.arange(128) % HEAD_DIM
    inv_freq = (np.float32(ROPE_THETA) ** (-np.arange(0, ROT_DIM, 2, dtype=np.float32) / np.float32(ROT_DIM))).astype(np.float32)
    invf = np.where(lane < ROT_DIM, inv_freq[lane % ROT_HALF], 0.0).astype(np.float32)
    ma = np.where(lane < ROT_HALF, -1.0, 0.0).astype(np.float32)
    mb = np.where((lane >= ROT_HALF) & (lane < ROT_DIM), 1.0, 0.0).astype(np.float32)
    return jnp.asarray(invf[None]), jnp.asarray(ma[None]), jnp.asarray(mb[None])


def _row_spec(tm, w):
    return pl.BlockSpec((tm, w), lambda i: (i, 0))


def _full_spec(shape):
    return pl.BlockSpec(shape, lambda i: (0,) * len(shape))


def _weight_spec(shape):
    return pl.BlockSpec(shape, lambda i: (0,) * len(shape), pipeline_mode=pl.Buffered(1))


FF_CHUNKS = (256, 512, 1024, 1024)
FF_SPANS = [(int(o), n) for o, n in zip(np.cumsum((0,) + FF_CHUNKS[:-1]), FF_CHUNKS)]
FF_WHOLE = [(0, D_FF)]


def _ffn_weight_scratch(n_weights, spans):
    return [pltpu.VMEM((D_FF, D_MODEL), BF16)] * n_weights + [pltpu.SemaphoreType.DMA((n_weights, len(spans)))]


def _with_ffn_weights(w_hbm, w_vmem, sems, spans, run):
    copies = [[pltpu.make_async_copy(h.at[pl.ds(o, n)], v.at[pl.ds(o, n)], sems.at[j, c]) for c, (o, n) in enumerate(spans)]
              for j, (h, v) in enumerate(zip(w_hbm, w_vmem))]
    first = pl.program_id(0) == 0

    @pl.when(first)
    def _():
        for of_weight in copies:
            for cp in of_weight:
                cp.start()
        run(lambda j, c: copies[j][c].wait())

    @pl.when(jnp.logical_not(first))
    def _():
        run(None)


RES = 16


def _residue_scratch(n_arrays, tm, width):
    return [pltpu.VMEM((2, n_arrays, tm // RES, RES, width), F32), pltpu.SemaphoreType.DMA((2, n_arrays, RES))]


def _to_residue_rows(tiles, outs, scratch, sems, tm, n_steps):
    i = pl.program_id(0)
    slot = i % 2
    per = tm // RES

    def copies(step, s):
        return [pltpu.make_async_copy(scratch.at[s, a, :, b, :],
                                      outs[a].at[pl.ds(pl.multiple_of(b * (SEQ // RES) + per * step, per), per), :],
                                      sems.at[s, a, b]) for a in range(len(outs)) for b in range(RES)]

    @pl.when(i >= 2)
    def _():
        for cp in copies(i - 2, slot):
            cp.wait()

    for a, tile in enumerate(tiles):
        scratch[slot, a] = tile.reshape(per, RES, tile.shape[-1])
    for cp in copies(i, slot):
        cp.start()

    @pl.when(i == n_steps - 1)
    def _():
        for cp in copies(i - 1, 1 - slot) + copies(i, slot):
            cp.wait()


def in_proj(x, pos_col, g1, w_in_t, rot):
    tm = 512
    n_steps = SEQ // tm

    def body(x_ref, pos_ref, g_ref, w_ref, invf_ref, ma_ref, mb_ref, h_ref, u_ref, vs_ref, q_ref, k_ref, v_ref, scratch, sems):
        xf = x_ref[...]
        h = (xf * _rms(xf) * g_ref[...]).astype(BF16)
        h_ref[...] = h
        proj = _dot_nt(h, w_ref[...])
        c, sa, sb = _rot_tables(pos_ref[...], invf_ref[...], ma_ref[...], mb_ref[...])
        slabs = range(ATTN_W // 128)
        q = jnp.concatenate([_rot(proj[:, j * 128:(j + 1) * 128], c, sa, sb) * Q_SCALE for j in slabs], axis=1)
        k = jnp.concatenate([_rot(proj[:, ATTN_W + j * 128:ATTN_W + (j + 1) * 128], c, sa, sb) for j in slabs], axis=1)
        u_ref[...] = proj[:, 3 * ATTN_W:3 * ATTN_W + SGU_W]
        vs_ref[...] = proj[:, 3 * ATTN_W + SGU_W:]
        _to_residue_rows([q, k, proj[:, 2 * ATTN_W:3 * ATTN_W]], [q_ref, k_ref, v_ref], scratch, sems, tm, n_steps)

    act = jax.ShapeDtypeStruct((SEQ, 512), F32)
    return _call(
        "in_proj", body, n_steps,
        [_row_spec(tm, D_MODEL), _row_spec(tm, 1), _full_spec((1, D_MODEL)), _weight_spec((IN_W, D_MODEL)),
         _full_spec((1, 128)), _full_spec((1, 128)), _full_spec((1, 128))],
        [_row_spec(tm, D_MODEL)] + [_row_spec(tm, 512)] * 2 + [ANY] * 3,
        [jax.ShapeDtypeStruct((SEQ, D_MODEL), BF16)] + [act] * 5,
        (x, pos_col, g1, w_in_t, *rot), scratch_shapes=_residue_scratch(3, tm, ATTN_W))


def _to_residue_order(t):
    return t.reshape(SEQ // RES, RES, -1).transpose(1, 0, 2).reshape(t.shape)


def _from_residue_order(t):
    return t.reshape(RES, SEQ // RES, -1).transpose(1, 0, 2).reshape(t.shape)


def _block_rows(d, r, n):
    if d == 16:
        slices = [(128 * r, 128)]
    elif d == 4:
        slices = [(128 * (4 * b + r) + 32 * n, 32) for b in range(4)]
    else:
        slices = [(128 * b + 8 * n, 8) for b in range(RES)]
    return [(s if isinstance(s, int) else pl.multiple_of(s, z), z) for s, z in slices]


def _block_step(d, i):
    if d == 16:
        return i
    if d == 4:
        return 4 * (i & 31) + (i >> 5)
    return 16 * (i & 7) + (i >> 3)


def _attn_masks(d):
    row2 = _block_step(d, lax.broadcasted_iota(jnp.int32, (128, 256), 0))
    col2 = lax.broadcasted_iota(jnp.int32, (128, 256), 1)
    key2 = _block_step(d, col2 & 127)
    mask2 = jnp.logical_or(jnp.logical_and(col2 < 128, key2 >= row2), jnp.logical_and(col2 >= 128, key2 <= row2))
    row1 = _block_step(d, lax.broadcasted_iota(jnp.int32, (128, 128), 0))
    col1 = lax.broadcasted_iota(jnp.int32, (128, 128), 1)
    return col1 < HEAD_DIM, _block_step(d, col1) <= row1, mask2


def _load_rows(ref, slices):
    parts = [ref[pl.ds(s, z), :] for s, z in slices]
    return parts[0] if len(parts) == 1 else jnp.concatenate(parts, axis=0)


def _for_each_group(fn):
    for p, d in enumerate(DILATIONS):
        masks = _attn_masks(d)
        if d == 16:
            def group(i, carry, p=p, masks=masks):
                fn(p, masks, [(_block_rows(16, 8 * i + g, 0), None) for g in range(8)])
                return carry

            lax.fori_loop(0, 2, group, 0)
        elif d == 4:
            fn(p, masks, [(_block_rows(4, r, 0), None) for r in range(4)])

            def group(i, carry, p=p, masks=masks):
                blocks = [6 * i + g for g in range(6)]
                fn(p, masks, [(_block_rows(4, j % 4, 1 + j // 4), _block_rows(4, j % 4, j // 4)) for j in blocks])
                return carry

            lax.fori_loop(0, 2, group, 0)
        else:
            fn(p, masks, [(_block_rows(1, 0, 0), None)])

            def group(i, carry, p=p, masks=masks):
                fn(p, masks, [(_block_rows(1, 0, 5 * i + g + 1), _block_rows(1, 0, 5 * i + g)) for g in range(5)])
                return carry

            lax.fori_loop(0, 3, group, 0)


def attn_fwd(q, k, v):
    def body(q_ref, k_ref, v_ref, o_ref, lse_ref, nat_ref, op_ref, lp_ref, sems):
        def group(p, masks, blocks):
            head0, mask1, mask2 = masks
            heads = (head0, jnp.logical_not(head0))
            keys = [rows if prev is None else prev + rows for rows, prev in blocks]
            mask = [mask1 if prev is None else mask2 for _, prev in blocks]
            qb = [_load_rows(q_ref, rows) for rows, _ in blocks]
            kk = [_load_rows(k_ref, ks).astype(BF16) for ks in keys]
            vv = [_load_rows(v_ref, ks).astype(BF16) for ks in keys]
            chains = [(g, hm) for g in range(len(blocks)) for hm in heads]
            s = [jnp.where(mask[g], _dot_nt(jnp.where(hm, qb[g], 0.0).astype(BF16), kk[g]), NEG) for g, hm in chains]
            m = [jnp.max(t, axis=-1, keepdims=True) for t in s]
            e = [jnp.exp(t - mt) for t, mt in zip(s, m)]
            l = [jnp.sum(t, axis=-1, keepdims=True) for t in e]
            pv = [_dot(t.astype(BF16), vv[g]) for t, (g, _) in zip(e, chains)]
            for g, (rows, _) in enumerate(blocks):
                o_blk = jnp.where(head0, pv[2 * g] / l[2 * g], pv[2 * g + 1] / l[2 * g + 1])
                l_blk = jnp.where(head0, jnp.broadcast_to(m[2 * g] + jnp.log(l[2 * g]), (128, 128)),
                                  jnp.broadcast_to(m[2 * g + 1] + jnp.log(l[2 * g + 1]), (128, 128)))
                at = 0
                for start, size in rows:
                    op_ref[p, pl.ds(start, size), :] = o_blk[at:at + size]
                    lp_ref[p, pl.ds(start, size), :] = l_blk[at:at + size]
                    at += size

        _for_each_group(group)

        def combine(i, carry):
            rows = pl.ds(pl.multiple_of(i * 256, 256), 256)
            ls = [lp_ref[p, rows, :] for p in range(3)]
            m = jnp.maximum(jnp.maximum(ls[0], ls[1]), ls[2])
            lse = m + jnp.log(jnp.exp(ls[0] - m) + jnp.exp(ls[1] - m) + jnp.exp(ls[2] - m))
            o = jnp.zeros((256, 128), F32)
            for p in range(3):
                o = o + jnp.exp(ls[p] - lse) * op_ref[p, rows, :]
            o_ref[rows, :] = o
            lse_ref[rows, :] = lse
            return carry

        lax.fori_loop(0, SEQ // 256, combine, 0)

        lanes = pl.ds(pl.multiple_of(pl.program_id(0) * 128, 128), 128)
        back = [pltpu.make_async_copy(o_ref.at[pl.ds(b * (SEQ // RES), SEQ // RES), :], nat_ref.at[:, b, lanes], sems.at[b])
                for b in range(RES)]
        for cp in back:
            cp.start()
        for cp in back:
            cp.wait()

    slab = pl.BlockSpec((SEQ, 128), lambda i: (0, i))
    out = jax.ShapeDtypeStruct((SEQ, ATTN_W), F32)
    attn_r, lse, attn = _call(
        "attn_fwd", body, ATTN_W // 128, [slab] * 3, [slab] * 2 + [ANY],
        [out, out, jax.ShapeDtypeStruct((SEQ // RES, RES, ATTN_W), F32)], (q, k, v),
        scratch_shapes=[pltpu.VMEM((3, SEQ, 128), F32), pltpu.VMEM((3, SEQ, 128), F32), pltpu.SemaphoreType.DMA((RES,))])
    return attn_r, lse, attn.reshape(SEQ, ATTN_W)


def _causal_weights(w_ref):
    row = lax.broadcasted_iota(jnp.int32, (CHUNK, CHUNK), 0)
    col = lax.broadcasted_iota(jnp.int32, (CHUNK, CHUNK), 1)
    return [jnp.where(col <= row, w_ref[g], 0.0).astype(BF16) for g in range(N_GROUPS)], col <= row


def _sgu_chunk_fwd(u, vs, lg, lb, wc, bfull, head0):
    ug = _gelu(u)
    vg = _gelu(vs)
    xc = vg - jnp.mean(vg, axis=-1, keepdims=True)
    rstd = lax.rsqrt(jnp.mean(xc * xc, axis=-1, keepdims=True) + LN_EPS)
    xhat = xc * rstd
    vn = xhat * lg + lb
    mixed = []
    for gp in range(SGU_W // 128):
        vp = vn[:, gp * 128:(gp + 1) * 128].astype(BF16)
        mixed.append(jnp.where(head0, _dot(wc[2 * gp], vp), _dot(wc[2 * gp + 1], vp)))
    ms = jnp.concatenate(mixed, axis=1) + bfull
    return ug, xhat, rstd, vn, ms


def sgu_fwd(u, vs, lg, lb, w_sp, bfull):
    cpb = 4

    def body(u_ref, vs_ref, lg_ref, lb_ref, w_ref, b_ref, o_ref):
        wc, _ = _causal_weights(w_ref)
        head0 = lax.broadcasted_iota(jnp.int32, (CHUNK, 128), 1) < HEAD_DIM
        for ci in range(cpb):
            rows = pl.ds(ci * CHUNK, CHUNK)
            ug, _, _, _, ms = _sgu_chunk_fwd(u_ref[rows, :], vs_ref[rows, :], lg_ref[...], lb_ref[...], wc, b_ref[...], head0)
            o_ref[rows, :] = ug * ms

    tm = cpb * CHUNK
    return _call(
        "sgu_fwd", body, SEQ // tm,
        [_row_spec(tm, SGU_W), _row_spec(tm, SGU_W), _full_spec((1, SGU_W)), _full_spec((1, SGU_W)),
         _full_spec((N_GROUPS, CHUNK, CHUNK)), _full_spec((CHUNK, SGU_W))],
        [_row_spec(tm, SGU_W)], [jax.ShapeDtypeStruct((SEQ, SGU_W), F32)],
        (u, vs, lg, lb, w_sp, bfull))


def out_proj(attn, sgu, x, ga, gs, w_out, gpm, gpf):
    tm = 512

    def body(a_ref, s_ref, x_ref, ga_ref, gs_ref, w_ref, gpm_ref, gpf_ref, mix_ref, y_ref, x2_ref, h2_ref):
        a = a_ref[...]
        s = s_ref[...]
        an = (a * _rms(a) * ga_ref[...]).astype(BF16)
        sn = (s * _rms(s) * gs_ref[...]).astype(BF16)
        mix_ref[:, :ATTN_W] = an
        mix_ref[:, ATTN_W:] = sn
        y = _dot(an, w_ref[:ATTN_W, :]) + _dot(sn, w_ref[ATTN_W:, :])
        y_ref[...] = y
        x2 = x_ref[...] + y * _rms(y) * gpm_ref[...]
        x2_ref[...] = x2
        h2_ref[...] = (x2 * _rms(x2) * gpf_ref[...]).astype(BF16)

    wide = jax.ShapeDtypeStruct((SEQ, D_MODEL), F32)
    wide16 = jax.ShapeDtypeStruct((SEQ, D_MODEL), BF16)
    return _call(
        "out_proj", body, SEQ // tm,
        [_row_spec(tm, ATTN_W), _row_spec(tm, SGU_W), _row_spec(tm, D_MODEL), _full_spec((1, ATTN_W)),
         _full_spec((1, SGU_W)), _weight_spec((D_MODEL, D_MODEL)), _full_spec((1, D_MODEL)), _full_spec((1, D_MODEL))],
        [_row_spec(tm, D_MODEL)] * 4, [wide16, wide, wide, wide16],
        (attn, sgu, x, ga, gs, w_out, gpm, gpf))


def ffn_up(h2, w_gate_t, w_up_t):
    tm = 256

    def body(h_ref, wg_hbm, wu_hbm, g_ref, u_ref, a_ref, wg_ref, wu_ref, sems):
        def run(wait):
            h = h_ref[...]
            if wait:
                wait(0, 0)
            g = _dot_nt(h, wg_ref[...])
            g_ref[...] = g.astype(BF16)
            if wait:
                wait(1, 0)
            u = _dot_nt(h, wu_ref[...])
            u_ref[...] = u.astype(BF16)
            a_ref[...] = (g * jax.nn.sigmoid(g) * u).astype(BF16)

        _with_ffn_weights([wg_hbm, wu_hbm], [wg_ref, wu_ref], sems, FF_WHOLE, run)

    ff = jax.ShapeDtypeStruct((SEQ, D_FF), BF16)
    return _call(
        "ffn_up", body, SEQ // tm, [_row_spec(tm, D_MODEL), ANY, ANY],
        [_row_spec(tm, D_FF)] * 3, [ff, ff, jax.ShapeDtypeStruct((SEQ, D_FF), BF16)],
        (h2, w_gate_t, w_up_t), scratch_shapes=_ffn_weight_scratch(2, FF_WHOLE))


def ffn_down_loss(act, w_down, x2, gpo, target):
    tm = 512

    def body(a_ref, w_hbm, x2_ref, g_ref, t_ref, df_ref, dx3_ref, dg_ref, loss_ref, w_ref, sems):
        def run(wait):
            f = None
            for c, (o, n) in enumerate(FF_SPANS if wait else FF_WHOLE):
                if wait:
                    wait(0, c)
                part = _dot(a_ref[:, o:o + n], w_ref[o:o + n, :])
                f = part if f is None else f + part
            gain = g_ref[...]
            err = x2_ref[...] + f * _rms(f) * gain - t_ref[...]
            dx3 = err * np.float32(1.0 / D_MODEL)
            dx3_ref[...] = dx3
            df, dg = _rms_bwd(f, gain, dx3)
            df_ref[...] = df.astype(BF16)
            loss = jnp.sum(err * err, axis=(0, 1), keepdims=True)
            if wait:
                dg_ref[...] = dg
                loss_ref[...] = loss
            else:
                dg_ref[...] += dg
                loss_ref[...] += loss

        _with_ffn_weights([w_hbm], [w_ref], sems, FF_SPANS, run)

    return _call(
        "ffn_down_loss", body, SEQ // tm,
        [_row_spec(tm, D_FF), ANY, _row_spec(tm, D_MODEL), _full_spec((1, D_MODEL)), _row_spec(tm, D_MODEL)],
        [_row_spec(tm, D_MODEL), _row_spec(tm, D_MODEL), _full_spec((1, D_MODEL)), _full_spec((1, 1))],
        [jax.ShapeDtypeStruct((SEQ, D_MODEL), BF16), jax.ShapeDtypeStruct((SEQ, D_MODEL), F32),
         jax.ShapeDtypeStruct((1, D_MODEL), F32), jax.ShapeDtypeStruct((1, 1), F32)],
        (act, w_down, x2, gpo, target), scratch_shapes=_ffn_weight_scratch(1, FF_SPANS))


def ffn_bwd(df, w_down, gate, up, w_gate_t, w_up_t, x2, gpf, dx3, y, gpm, after=()):
    tm = 256

    def body(df_ref, wd_hbm, g_ref, u_ref, wg_hbm, wu_hbm, x2_ref, gpf_ref, dx3_ref, y_ref, gpm_ref,
             dg_ref, du_ref, dx2_ref, dy_ref, dgpf_ref, dgpm_ref, wd_ref, wg_ref, wu_ref, sems):
        def run(wait):
            if wait:
                wait(0, 0)
            dact = _dot_nt(df_ref[...], wd_ref[...])
            g = g_ref[...].astype(F32)
            s = jax.nn.sigmoid(g)
            dup = (dact * g * s).astype(BF16)
            dgate = (dact * u_ref[...].astype(F32) * (s * (1.0 + g * (1.0 - s)))).astype(BF16)
            du_ref[...] = dup
            dg_ref[...] = dgate
            if wait:
                wait(1, 0)
                wait(2, 0)
            dh2 = _dot(dgate, wg_ref[...]) + _dot(dup, wu_ref[...])
            dz, dgpf = _rms_bwd(x2_ref[...], gpf_ref[...], dh2)
            dx2 = dx3_ref[...] + dz
            dx2_ref[...] = dx2
            dy, dgpm = _rms_bwd(y_ref[...], gpm_ref[...], dx2)
            dy_ref[...] = dy.astype(BF16)
            if wait:
                dgpf_ref[...] = dgpf
                dgpm_ref[...] = dgpm
            else:
                dgpf_ref[...] += dgpf
                dgpm_ref[...] += dgpm

        _with_ffn_weights([wd_hbm, wg_hbm, wu_hbm], [wd_ref, wg_ref, wu_ref], sems, FF_WHOLE, run)

    vec = jax.ShapeDtypeStruct((1, D_MODEL), F32)
    ff16 = jax.ShapeDtypeStruct((SEQ, D_FF), BF16)
    return _call(
        "ffn_bwd", body, SEQ // tm,
        [_row_spec(tm, D_MODEL), ANY, _row_spec(tm, D_FF), _row_spec(tm, D_FF), ANY, ANY, _row_spec(tm, D_MODEL),
         _full_spec((1, D_MODEL)), _row_spec(tm, D_MODEL), _row_spec(tm, D_MODEL), _full_spec((1, D_MODEL))],
        [_row_spec(tm, D_FF), _row_spec(tm, D_FF), _row_spec(tm, D_MODEL), _row_spec(tm, D_MODEL),
         _full_spec((1, D_MODEL)), _full_spec((1, D_MODEL))],
        [ff16, ff16, jax.ShapeDtypeStruct((SEQ, D_MODEL), F32), jax.ShapeDtypeStruct((SEQ, D_MODEL), BF16), vec, vec],
        (df, w_down, gate, up, w_gate_t, w_up_t, x2, gpf, dx3, y, gpm), scratch_shapes=_ffn_weight_scratch(3, FF_WHOLE), after=after)


def weight_grads(name, lhs, b, after=()):
    m, n, k = lhs[0].shape[1], b.shape[1], len(lhs)
    tr = 256

    def body(*refs):
        for a_ref, o_ref in zip(refs[:k], refs[k + 1:]):
            o_ref[...] = _dot_tn(a_ref[...], refs[k][...]).astype(BF16)

    outs = _call(
        name, body, m // tr, [pl.BlockSpec((SEQ, tr), lambda i: (0, i))] * k + [_weight_spec((SEQ, n))],
        [_row_spec(tr, n)] * k, [jax.ShapeDtypeStruct((m, n), BF16)] * k, (*lhs, b), after=after)
    return [out.reshape(N_DEV, m // N_DEV, n) for out in outs]


def weight_grad(name, a, b, after=()):
    return weight_grads(name, [a], b, after)[0]


def weight_grad_of_parts(name, parts, b, after=()):
    p, n, k = parts[0].shape[1], b.shape[1], len(parts)
    tr = 512
    per = p // tr

    def body(*refs):
        tile = pl.program_id(0)
        for j in range(k):
            @pl.when(tile // per == j)
            def _(j=j):
                refs[k + 1][...] = _dot_tn(refs[j][...], refs[k][...]).astype(BF16)

    def part_spec(j):
        return pl.BlockSpec((SEQ, tr), lambda i: (0, jnp.clip(i - per * j, 0, per - 1)))

    (out,) = _call(
        name, body, k * per, [part_spec(j) for j in range(k)] + [_weight_spec((SEQ, n))],
        [_row_spec(tr, n)], [jax.ShapeDtypeStruct((k * p, n), BF16)], (*parts, b), after=after)
    return out.reshape(N_DEV, k * p // N_DEV, n)


def mix_bwd(dy, w_out, attn, sgu, ga, gs, after=()):
    tm = 512
    n_steps = SEQ // tm

    def body(dy_ref, w_ref, a_ref, s_ref, ga_ref, gs_ref, ds_ref, dga_ref, dgs_ref, da_ref, scratch, sems):
        dy = dy_ref[...]
        da, dga = _rms_bwd(a_ref[...], ga_ref[...], _dot_nt(dy, w_ref[:ATTN_W, :]))
        ds, dgs = _rms_bwd(s_ref[...], gs_ref[...], _dot_nt(dy, w_ref[ATTN_W:, :]))
        ds_ref[...] = ds
        _to_residue_rows([da], [da_ref], scratch, sems, tm, n_steps)

        @pl.when(pl.program_id(0) == 0)
        def _():
            dga_ref[...] = jnp.zeros_like(dga_ref)
            dgs_ref[...] = jnp.zeros_like(dgs_ref)

        dga_ref[...] += dga
        dgs_ref[...] += dgs

    half = jax.ShapeDtypeStruct((SEQ, 512), F32)
    vec = jax.ShapeDtypeStruct((1, 512), F32)
    return _call(
        "mix_bwd", body, n_steps,
        [_row_spec(tm, D_MODEL), _weight_spec((D_MODEL, D_MODEL)), _row_spec(tm, 512), _row_spec(tm, 512),
         _full_spec((1, 512)), _full_spec((1, 512))],
        [_row_spec(tm, 512), _full_spec((1, 512)), _full_spec((1, 512)), ANY],
        [half, vec, vec, half], (dy, w_out, attn, sgu, ga, gs), scratch_shapes=_residue_scratch(1, tm, ATTN_W), after=after)


def sgu_bwd(u, vs, dsgu, lg, lb, w_sp, bfull):
    cpb = 4

    def body(u_ref, vs_ref, d_ref, lg_ref, lb_ref, w_ref, b_ref, du_ref, dvs_ref, dlg_ref, dlb_ref, dw_ref, db_ref):
        wc, causal = _causal_weights(w_ref)
        head0 = lax.broadcasted_iota(jnp.int32, (CHUNK, 128), 1) < HEAD_DIM
        lg = lg_ref[...]

        @pl.when(pl.program_id(0) == 0)
        def _():
            dlg_ref[...] = jnp.zeros_like(dlg_ref)
            dlb_ref[...] = jnp.zeros_like(dlb_ref)
            dw_ref[...] = jnp.zeros_like(dw_ref)
            db_ref[...] = jnp.zeros_like(db_ref)

        for ci in range(cpb):
            rows = pl.ds(ci * CHUNK, CHUNK)
            u = u_ref[rows, :]
            vs = vs_ref[rows, :]
            d = d_ref[rows, :]
            ug, xhat, rstd, vn, ms = _sgu_chunk_fwd(u, vs, lg, lb_ref[...], wc, b_ref[...], head0)
            du_ref[rows, :] = (d * ms * _gelu_grad(u)).astype(BF16)
            dms = d * ug
            db_ref[...] += dms
            dvn = []
            for gp in range(SGU_W // 128):
                dmp = dms[:, gp * 128:(gp + 1) * 128]
                dm0 = jnp.where(head0, dmp, 0.0).astype(BF16)
                dm1 = jnp.where(head0, 0.0, dmp).astype(BF16)
                vp = vn[:, gp * 128:(gp + 1) * 128].astype(BF16)
                dw_ref[2 * gp] += _dot_nt(dm0, vp)
                dw_ref[2 * gp + 1] += _dot_nt(dm1, vp)
                dvn.append(_dot_tn(wc[2 * gp], dm0) + _dot_tn(wc[2 * gp + 1], dm1))
            dvn = jnp.concatenate(dvn, axis=1)
            dlg_ref[...] += jnp.sum(dvn * xhat, axis=0, keepdims=True)
            dlb_ref[...] += jnp.sum(dvn, axis=0, keepdims=True)
            dxh = dvn * lg
            dvg = rstd * (dxh - jnp.mean(dxh, axis=-1, keepdims=True) - xhat * jnp.mean(dxh * xhat, axis=-1, keepdims=True))
            dvs_ref[rows, :] = (dvg * _gelu_grad(vs)).astype(BF16)

        @pl.when(pl.program_id(0) == pl.num_programs(0) - 1)
        def _():
            for g in range(N_GROUPS):
                dw_ref[g] = jnp.where(causal, dw_ref[g], 0.0)

    tm = cpb * CHUNK
    half16 = jax.ShapeDtypeStruct((SEQ, SGU_W), BF16)
    vec = jax.ShapeDtypeStruct((1, SGU_W), F32)
    return _call(
        "sgu_bwd", body, SEQ // tm,
        [_row_spec(tm, SGU_W)] * 3 + [_full_spec((1, SGU_W)), _full_spec((1, SGU_W)),
                                      _full_spec((N_GROUPS, CHUNK, CHUNK)), _full_spec((CHUNK, SGU_W))],
        [_row_spec(tm, SGU_W), _row_spec(tm, SGU_W), _full_spec((1, SGU_W)), _full_spec((1, SGU_W)),
         _full_spec((N_GROUPS, CHUNK, CHUNK)), _full_spec((CHUNK, SGU_W))],
        [half16, half16, vec, vec, jax.ShapeDtypeStruct((N_GROUPS, CHUNK, CHUNK), F32),
         jax.ShapeDtypeStruct((CHUNK, SGU_W), F32)],
        (u, vs, dsgu, lg, lb, w_sp, bfull))


def attn_bwd(q, k, v, o, lse, do, pos_col, rot):
    def body(q_ref, k_ref, v_ref, o_ref, lse_ref, do_ref, pos_ref, invf_ref, ma_ref, mb_ref,
             dq_ref, dk_ref, dv_ref, dqa_ref, dka_ref, dva_ref, dlt_ref, rot_ref):
        dqa_ref[...] = jnp.zeros_like(dqa_ref)
        dka_ref[...] = jnp.zeros_like(dka_ref)
        dva_ref[...] = jnp.zeros_like(dva_ref)

        def delta(i, carry):
            rows = pl.ds(pl.multiple_of(i * 256, 256), 256)
            prod = do_ref[rows, :] * o_ref[rows, :]
            h0 = lax.broadcasted_iota(jnp.int32, (256, 128), 1) < HEAD_DIM
            d0 = jnp.sum(jnp.where(h0, prod, 0.0), axis=-1, keepdims=True)
            d1 = jnp.sum(jnp.where(h0, 0.0, prod), axis=-1, keepdims=True)
            dlt_ref[rows, :] = jnp.where(h0, d0, d1)
            return carry

        lax.fori_loop(0, SEQ // 256, delta, 0)

        def add_rows(ref, slices, val):
            at = 0
            for start, size in slices:
                ref[pl.ds(start, size), :] += val[at:at + size]
                at += size

        def group(p, masks, blocks):
            head0, mask1, mask2 = masks
            heads = (head0, jnp.logical_not(head0))
            keys = [rows if prev is None else prev + rows for rows, prev in blocks]
            mask = [mask1 if prev is None else mask2 for _, prev in blocks]
            kk = [_load_rows(k_ref, ks).astype(BF16) for ks in keys]
            vv = [_load_rows(v_ref, ks).astype(BF16) for ks in keys]
            qb = [_load_rows(q_ref, rows) for rows, _ in blocks]
            dob = [_load_rows(do_ref, rows) for rows, _ in blocks]
            lse_b = [_load_rows(lse_ref, rows) for rows, _ in blocks]
            dlt_b = [_load_rows(dlt_ref, rows) for rows, _ in blocks]
            chains = [(g, h) for g in range(len(blocks)) for h in range(2)]
            qm = [jnp.where(heads[h], qb[g], 0.0).astype(BF16) for g, h in chains]
            dom = [jnp.where(heads[h], dob[g], 0.0).astype(BF16) for g, h in chains]
            s = [_dot_nt(qm[c], kk[g]) for c, (g, h) in enumerate(chains)]
            dp = [_dot_nt(dom[c], vv[g]) for c, (g, h) in enumerate(chains)]
            pr = [jnp.where(mask[g], jnp.exp(s[c] - lse_b[g][:, h * HEAD_DIM:h * HEAD_DIM + 1]), 0.0)
                  for c, (g, h) in enumerate(chains)]
            ds = [(pr[c] * (dp[c] - dlt_b[g][:, h * HEAD_DIM:h * HEAD_DIM + 1])).astype(BF16)
                  for c, (g, h) in enumerate(chains)]
            dv = [_dot_tn(pr[c].astype(BF16), dom[c]) for c in range(len(chains))]
            dk = [_dot_tn(ds[c], qm[c]) for c in range(len(chains))]
            dq = [_dot(ds[c], kk[g]) for c, (g, h) in enumerate(chains)]
            for g, (rows, _) in enumerate(blocks):
                add_rows(dqa_ref, rows, jnp.where(head0, dq[2 * g], dq[2 * g + 1]))
                add_rows(dka_ref, keys[g], dk[2 * g] + dk[2 * g + 1])
                add_rows(dva_ref, keys[g], dv[2 * g] + dv[2 * g + 1])

        _for_each_group(group)

        @pl.when(pl.program_id(0) == 0)
        def _():
            def tables(i, carry):
                rows = pl.ds(pl.multiple_of(i * 256, 256), 256)
                c, sa, sb = _rot_tables(pos_ref[rows, :], invf_ref[...], ma_ref[...], mb_ref[...])
                rot_ref[0, rows, :] = c
                rot_ref[1, rows, :] = sa
                rot_ref[2, rows, :] = sb
                return carry

            lax.fori_loop(0, SEQ // 256, tables, 0)

        def finish(i, carry):
            rows = pl.ds(pl.multiple_of(i * 256, 256), 256)
            c, sa, sb = rot_ref[0, rows, :], rot_ref[1, rows, :], rot_ref[2, rows, :]
            dq_ref[rows, :] = _rot_t(dqa_ref[rows, :] * Q_SCALE, c, sa, sb).astype(BF16)
            dk_ref[rows, :] = _rot_t(dka_ref[rows, :], c, sa, sb).astype(BF16)
            dv_ref[rows, :] = dva_ref[rows, :].astype(BF16)
            return carry

        lax.fori_loop(0, SEQ // 256, finish, 0)

    slab = pl.BlockSpec((SEQ, 128), lambda i: (0, i))
    out = jax.ShapeDtypeStruct((SEQ, ATTN_W), BF16)
    acc = pltpu.VMEM((SEQ, 128), F32)
    return _call(
        "attn_bwd", body, ATTN_W // 128,
        [slab] * 6 + [_full_spec((SEQ, 1)), _full_spec((1, 128)), _full_spec((1, 128)), _full_spec((1, 128))],
        [slab] * 3, [out, out, out], (q, k, v, o, lse, do, pos_col, *rot),
        scratch_shapes=[acc, acc, acc, acc, pltpu.VMEM((3, SEQ, 128), F32)])


def in_bwd(dproj_parts, w_in_t, x, g1, dx2, after=()):
    tm = 512
    k = len(dproj_parts)

    def body(*refs):
        w_ref, x_ref, g_ref, dx2_ref, dx_ref, dg_ref = refs[k:]
        dh1 = _dot(refs[0][...], w_ref[0:512, :])
        for j in range(1, k):
            dh1 = dh1 + _dot(refs[j][...], w_ref[512 * j:512 * (j + 1), :])
        dz, dg = _rms_bwd(x_ref[...], g_ref[...], dh1)
        dx_ref[...] = dx2_ref[...] + dz

        @pl.when(pl.program_id(0) == 0)
        def _():
            dg_ref[...] = jnp.zeros_like(dg_ref)

        dg_ref[...] += dg

    return _call(
        "in_bwd", body, SEQ // tm,
        [_row_spec(tm, 512)] * k + [_weight_spec((IN_W, D_MODEL)), _row_spec(tm, D_MODEL), _full_spec((1, D_MODEL)),
                                    _row_spec(tm, D_MODEL)],
        [_row_spec(tm, D_MODEL), _full_spec((1, D_MODEL))],
        [jax.ShapeDtypeStruct((SEQ, D_MODEL), F32), jax.ShapeDtypeStruct((1, D_MODEL), F32)],
        (*dproj_parts, w_in_t, x, g1, dx2), after=after)


def _coords():
    return lax.axis_index("x"), lax.axis_index("y"), lax.axis_index("c")


class Exchange:
    def __init__(self, srcs, bufs, new_shapes, n_sems, make):
        self.srcs, self.bufs, self.new_shapes, self.n_sems, self.make = list(srcs), list(bufs), list(new_shapes), n_sems, make


def _call(name, body, n_steps, in_specs, out_specs, out_shape, args, scratch_shapes=(), after=()):
    n_in = len(args)

    def wrapped(*refs):
        body(*refs[:n_in], *refs[n_in + len(after):])

    return list(pl.pallas_call(
        wrapped, name=name, grid=(n_steps,), in_specs=list(in_specs) + [ANY] * len(after), out_specs=list(out_specs),
        out_shape=list(out_shape), scratch_shapes=list(scratch_shapes), compiler_params=_params(),
    )(*args, *after))


GATHER_SEMS = 8


def gather(bufs):
    n = len(bufs)

    def make(src_refs, buf_refs, new_refs, send_sems, recv_sems):
        x, y, c = _coords()
        me, sibling = (x, y, c), (x, y, 1 - c)
        over_x, over_y, across = (1 - x, y), (x, 1 - y), (1 - x, 1 - y)

        def copy(a, k, block, to, half=None):
            r = buf_refs[a].shape[0] // N_DEV
            lo, size = (0, r) if half is None else (half * (r // 2), r // 2)
            rows = buf_refs[a].at[pl.ds((4 * block[0] + 2 * block[1] + block[2]) * r + lo, size), :]
            return pltpu.make_async_remote_copy(
                src_ref=rows, dst_ref=rows, send_sem=send_sems.at[GATHER_SEMS * a + k],
                recv_sem=recv_sems.at[GATHER_SEMS * a + k], device_id=to, device_id_type=MESH)

        every = range(n)
        out = ([copy(a, 0, me, sibling) for a in every] + [copy(a, 1, me, (*over_x, c)) for a in every]
               + [copy(a, 2, me, (*over_y, c)) for a in every])
        near_in = [copy(a, 1, (*over_x, c), me) for a in every] + [copy(a, 2, (*over_y, c), me) for a in every]
        relay = ([copy(a, 3, (*over_x, c), (*over_y, c), half=0) for a in every]
                 + [copy(a, 4, (*over_y, c), (*over_x, c), half=1) for a in every])
        near_on = [copy(a, 5, (*over_x, c), sibling) for a in every] + [copy(a, 6, (*over_y, c), sibling) for a in every]
        relay_in = ([copy(a, 3, (*across, c), me, half=0) for a in every]
                    + [copy(a, 4, (*across, c), me, half=1) for a in every])
        far_on = [copy(a, 7, (*across, c), sibling) for a in every]
        from_core = ([copy(a, 0, sibling, me) for a in every] + [copy(a, 5, (*over_x, 1 - c), me) for a in every]
                     + [copy(a, 6, (*over_y, 1 - c), me) for a in every] + [copy(a, 7, (*across, 1 - c), me) for a in every])
        stages = [([], out), (near_in, relay + near_on), (relay_in, far_on)]
        return stages, out + relay + near_on + far_on, from_core

    return Exchange([], bufs, [], GATHER_SEMS * n, make)


TO_GATHER = (1, lambda x, y, c: [(x, y, 1 - c), (1 - x, y, c), (x, 1 - y, c)])
TO_SIBLING = (2, lambda x, y, c: [(x, y, 1 - c)])
TO_CHIPS = (3, lambda x, y, c: [(1 - x, y, c), (x, 1 - y, c), (1 - x, 1 - y, c)])
TO_ALL = (4, lambda x, y, c: [(x ^ (m >> 2), y ^ ((m >> 1) & 1), c ^ (m & 1)) for m in range(1, N_DEV)])


def by_sequencer(name, exchanges, who):
    collective_id, peers_of = who
    hbm = pltpu.MemorySpace.HBM
    refs = [([jax.new_ref(a, memory_space=hbm) for a in ex.srcs], [jax.new_ref(a, memory_space=hbm) for a in ex.bufs],
             [jax.empty_ref(s, memory_space=hbm) for s in ex.new_shapes]) for ex in exchanges]
    sems = []
    for ex in exchanges:
        sems += [pltpu.SemaphoreType.DMA((ex.n_sems,)), pltpu.SemaphoreType.DMA((ex.n_sems,))]

    @pl.kernel(mesh=plsc.ScalarSubcoreMesh(axis_name="sequencer", num_cores=1), name=name, scratch_types=tuple(sems),
               compiler_params=pltpu.CompilerParams(collective_id=collective_id))
    def launch(*sem_refs):
        peers = peers_of(*_coords())
        barrier = pltpu.get_barrier_semaphore()
        for peer in peers:
            pl.semaphore_signal(barrier, inc=1, device_id=peer, device_id_type=MESH)
        pl.semaphore_wait(barrier, len(peers))

        made = [ex.make(*refs[k], sem_refs[2 * k], sem_refs[2 * k + 1]) for k, ex in enumerate(exchanges)]
        for stage in range(max(len(stages) for stages, _, _ in made)):
            for stages, _, _ in made:
                if stage < len(stages):
                    arrivals, starts = stages[stage]
                    for cp in arrivals:
                        cp.wait_recv()
                    for cp in starts:
                        cp.start()
        for _, sends, arrivals in made:
            for cp in arrivals:
                cp.wait_recv()
            for cp in sends:
                cp.wait_send()

    launch()
    return [([ref[...] for ref in bufs], [ref[...] for ref in news]) for _, bufs, news in refs]


def place_shards(name, shards, dev):
    n = len(shards)

    def body(dev_ref, *refs):
        for a in range(n):
            refs[n + a][...] = refs[a][...].astype(BF16)

    spec = pltpu.PrefetchScalarGridSpec(
        num_scalar_prefetch=1, grid=(1,),
        in_specs=[pl.BlockSpec(s.shape, lambda i, dev_ref: (0, 0)) for s in shards],
        out_specs=[pl.BlockSpec(s.shape, lambda i, dev_ref: (dev_ref[0], 0)) for s in shards])
    return pl.pallas_call(
        body, name=name, grid_spec=spec,
        out_shape=[jax.ShapeDtypeStruct((N_DEV * s.shape[0], s.shape[1]), BF16) for s in shards],
        compiler_params=_params(),
    )(dev, *shards)


def _swap(copies_of):
    def make(src_refs, buf_refs, new_refs, send_sems, recv_sems):
        copies = copies_of(src_refs, new_refs, send_sems, recv_sems)
        return [([], copies)], copies, copies

    return make


def to_sibling(grads):
    def copies_of(src_refs, new_refs, send_sems, recv_sems):
        x, y, c = _coords()
        return [pltpu.make_async_remote_copy(
            src_ref=src_refs[a].at[2 * xy + 1 - c], dst_ref=new_refs[a].at[xy], send_sem=send_sems.at[4 * a + xy],
            recv_sem=recv_sems.at[4 * a + xy], device_id=(x, y, 1 - c), device_id_type=MESH)
            for a in range(len(src_refs)) for xy in range(4)]

    return Exchange(grads, [], [jax.ShapeDtypeStruct((4,) + g.shape[1:], g.dtype) for g in grads], 4 * len(grads),
                    _swap(copies_of))


def to_chips(parts):
    def copies_of(src_refs, new_refs, send_sems, recv_sems):
        x, y, c = _coords()
        chips = [(1 - x, y), (x, 1 - y), (1 - x, 1 - y)]
        return [pltpu.make_async_remote_copy(
            src_ref=src_refs[a].at[2 * px + py], dst_ref=new_refs[a].at[2 * x + y], send_sem=send_sems.at[3 * a + j],
            recv_sem=recv_sems.at[3 * a + j], device_id=(px, py, c), device_id_type=MESH)
            for a in range(len(src_refs)) for j, (px, py) in enumerate(chips)]

    return Exchange(parts, [], [jax.ShapeDtypeStruct(p.shape, p.dtype) for p in parts], 3 * len(parts), _swap(copies_of))


def to_owners(grad):
    def copies_of(src_refs, new_refs, send_sems, recv_sems):
        x, y, c = _coords()
        copies = []
        for m in range(1, N_DEV):
            px, py, pc = x ^ (m >> 2), y ^ ((m >> 1) & 1), c ^ (m & 1)
            copies.append(pltpu.make_async_remote_copy(
                src_ref=src_refs[0].at[4 * px + 2 * py + pc], dst_ref=new_refs[0].at[4 * x + 2 * y + c],
                send_sem=send_sems.at[m - 1], recv_sem=recv_sems.at[m - 1], device_id=(px, py, pc), device_id_type=MESH))
        return copies

    return Exchange([grad], [], [jax.ShapeDtypeStruct(grad.shape, grad.dtype)], N_DEV - 1, _swap(copies_of))


def to_everyone(vec):
    def copies_of(src_refs, new_refs, send_sems, recv_sems):
        x, y, c = _coords()
        copies = []
        for m in range(1, N_DEV):
            px, py, pc = x ^ (m >> 2), y ^ ((m >> 1) & 1), c ^ (m & 1)
            copies.append(pltpu.make_async_remote_copy(
                src_ref=src_refs[0], dst_ref=new_refs[0].at[4 * x + 2 * y + c],
                send_sem=send_sems.at[m - 1], recv_sem=recv_sems.at[m - 1], device_id=(px, py, pc), device_id_type=MESH))
        return copies

    return Exchange([vec], [], [jax.ShapeDtypeStruct((N_DEV,) + vec.shape, vec.dtype)], N_DEV - 1, _swap(copies_of))


def sum_cores(name, grads, others, core, after=()):
    k = len(grads)

    def body(core_ref, *refs):
        for j in range(k):
            out_ref = refs[2 * k + len(after) + j]
            out_ref[...] = (refs[j][:, 0].astype(F32) + refs[k + j][...].astype(F32)).astype(out_ref.dtype)

    mine = [pl.BlockSpec((2, 1) + o.shape[1:], lambda i, core_ref: (i, core_ref[0], 0, 0)) for o in others]
    theirs = [pl.BlockSpec((2,) + o.shape[1:], lambda i, core_ref: (i, 0, 0)) for o in others]
    return pl.pallas_call(
        body, name=name,
        grid_spec=pltpu.PrefetchScalarGridSpec(
            num_scalar_prefetch=1, grid=(2,), in_specs=mine + theirs + [ANY] * len(after), out_specs=theirs),
        out_shape=[jax.ShapeDtypeStruct(o.shape, o.dtype) for o in others],
        compiler_params=_params(),
    )(core, *[g.reshape((4, 2) + g.shape[1:]) for g in grads], *others, *after)


def sum_owned(name, grad, others, dev_ids, after=()):
    _, r, w = grad.shape

    def body(ids_ref, *refs):
        acc = refs[0][0]
        for k in range(1, N_DEV):
            acc = acc + refs[k][0]
        refs[-1][...] = acc

    def pick(k):
        return pl.BlockSpec((1, r, w), lambda i, ids_ref: (ids_ref[k], 0, 0))

    return pl.pallas_call(
        body, name=name,
        grid_spec=pltpu.PrefetchScalarGridSpec(
            num_scalar_prefetch=1, grid=(1,), in_specs=[pick(k) for k in range(N_DEV)] + [ANY] * len(after),
            out_specs=pl.BlockSpec((r, w), lambda i, ids_ref: (ids_ref[0], 0))),
        out_shape=jax.ShapeDtypeStruct((N_DEV * r, w), F32),
        compiler_params=_params(),
    )(dev_ids, grad, *([others] * (N_DEV - 1)), *after)


def _adamw_update(w, g, m, v):
    nm = ADAM_B1 * m + np.float32(1.0 - ADAM_B1) * g
    nv = ADAM_B2 * v + np.float32(1.0 - ADAM_B2) * (g * g)
    m_hat = nm / np.float32(1.0 - ADAM_B1 ** ADAM_STEP)
    v_hat = nv / np.float32(1.0 - ADAM_B2 ** ADAM_STEP)
    return -ADAM_LR * (m_hat / (jnp.sqrt(v_hat) + ADAM_EPS) + ADAM_WD * w), nm, nv


def adamw_of_sums(name, parts, others, chip_ids, ws, ms, vs, after):
    n = len(parts)
    halves = 2

    def body(ids_ref, *refs):
        outs = refs[7 * n + 1:]
        for j in range(n):
            p_ref, a_ref, b_ref, c_ref, w_ref, m_ref, v_ref = refs[7 * j:7 * j + 7]
            g = ((p_ref[0].astype(F32) + a_ref[0].astype(F32)) + b_ref[0].astype(F32)) + c_ref[0].astype(F32)
            outs[4 * j][...] = g
            outs[4 * j + 1][...], outs[4 * j + 2][...], outs[4 * j + 3][...] = _adamw_update(w_ref[...], g, m_ref[...], v_ref[...])

    in_specs, out_specs, out_shape, operands = [], [], [], []
    for part, other, w, m, v in zip(parts, others, ws, ms, vs):
        _, r, wd = part.shape
        rows = r // halves
        whole = pl.BlockSpec((rows, wd), lambda i, ids_ref: (i, 0))
        in_specs += [pl.BlockSpec((1, rows, wd), lambda i, ids_ref, k=k: (ids_ref[k], i, 0)) for k in range(4)] + [whole] * 3
        out_specs += [whole] * 4
        out_shape += [jax.ShapeDtypeStruct((r, wd), F32)] * 4
        operands += [part, other, other, other, w, m, v]
    outs = pl.pallas_call(
        body, name=name,
        grid_spec=pltpu.PrefetchScalarGridSpec(
            num_scalar_prefetch=1, grid=(halves,), in_specs=in_specs + [ANY], out_specs=out_specs),
        out_shape=out_shape,
        compiler_params=_params(),
    )(chip_ids, *operands, after)
    return [tuple(outs[4 * j:4 * j + 4]) for j in range(n)]


def pack_small(parts):
    names = [name for name, _ in SMALL if name in parts]
    operands = [parts[name] for name in names]
    first_row, at = {}, 0
    for name, size in SMALL:
        first_row[name] = at // 128
        at += size
    sizes = dict(SMALL)

    def body(*refs):
        out_ref = refs[-1]
        out_ref[...] = jnp.zeros_like(out_ref)
        for name, ref in zip(names, refs):
            row = first_row[name]
            if name == "loss_sum":
                lane0 = lax.broadcasted_iota(jnp.int32, (1, 128), 1) == 0
                out_ref[row:row + 1, :] = jnp.where(lane0, ref[...], 0.0)
            else:
                rows = sizes[name] // 128
                out_ref[row:row + rows, :] = ref[...].reshape(rows, 128)

    vmem = pl.BlockSpec(memory_space=pltpu.VMEM)
    return pl.pallas_call(
        body, name="pack_small", in_specs=[vmem] * len(names), out_specs=vmem,
        out_shape=jax.ShapeDtypeStruct((SMALL_ROWS, 128), F32), compiler_params=_params(()),
    )(*operands)


LATE = "pre_mix_norm"


def adamw_small(packed_g, late_parts, ws, ms, vs):
    names = [name for name, _ in SMALL if name != "loss_sum"]
    k = len(names)
    shapes = [ws[name].shape[1:] if ws[name].ndim > 2 else ws[name].shape for name in names]
    first_row, at = [], 0
    for name, size in SMALL:
        first_row.append(at // 128)
        at += size

    def body(g_ref, late_ref, *refs):
        w_refs, m_refs, v_refs, outs = refs[:k], refs[k:2 * k], refs[2 * k:3 * k], refs[3 * k:]
        for i, (name, size) in enumerate(SMALL[:k]):
            if name == LATE:
                g = late_ref[0]
                for j in range(1, N_DEV):
                    g = g + late_ref[j]
            else:
                g = g_ref[first_row[i]:first_row[i] + size // 128, :].reshape(shapes[i])
            outs[i][...] = g
            outs[k + i][...], outs[2 * k + i][...], outs[3 * k + i][...] = _adamw_update(
                w_refs[i][...], g, m_refs[i][...], v_refs[i][...])
        outs[4 * k][...] = g_ref[first_row[k]:first_row[k] + 1, 0:1]

    vmem = pl.BlockSpec(memory_space=pltpu.VMEM)
    operands = [t[name].reshape(shape) for t in (ws, ms, vs) for name, shape in zip(names, shapes)]
    outs = pl.pallas_call(
        body, name="adamw_small", in_specs=[vmem] * (2 + 3 * k), out_specs=[vmem] * (4 * k + 1),
        out_shape=[jax.ShapeDtypeStruct(shape, F32) for _ in range(4) for shape in shapes] + [jax.ShapeDtypeStruct((1, 1), F32)],
        compiler_params=_params(()),
    )(packed_g, late_parts, *operands)
    tables = [{name: outs[j * k + i].reshape(ws[name].shape) for i, name in enumerate(names)} for j in range(4)]
    return (*tables, outs[4 * k])


def kernel(x, positions, pre_mix_norm, w_in, sgu_ln_gain, sgu_ln_bias, sgu_w_spatial, sgu_b_spatial, attn_out_norm, sgu_out_norm, w_out, post_mix_norm, pre_ffn_norm, w_gate, w_up, w_down, post_ffn_norm, loss_target, m_pre_mix_norm, m_w_in, m_sgu_ln_gain, m_sgu_ln_bias, m_sgu_w_spatial, m_sgu_b_spatial, m_attn_out_norm, m_sgu_out_norm, m_w_out, m_post_mix_norm, m_pre_ffn_norm, m_w_gate, m_w_up, m_w_down, m_post_ffn_norm, v_pre_mix_norm, v_w_in, v_sgu_ln_gain, v_sgu_ln_bias, v_sgu_w_spatial, v_sgu_b_spatial, v_attn_out_norm, v_sgu_out_norm, v_w_out, v_post_mix_norm, v_pre_ffn_norm, v_w_gate, v_w_up, v_w_down, v_post_ffn_norm):
    small_w = dict(pre_mix_norm=pre_mix_norm, sgu_ln_gain=sgu_ln_gain, sgu_ln_bias=sgu_ln_bias, sgu_w_spatial=sgu_w_spatial,
                   sgu_b_spatial=sgu_b_spatial, attn_out_norm=attn_out_norm, sgu_out_norm=sgu_out_norm,
                   post_mix_norm=post_mix_norm, pre_ffn_norm=pre_ffn_norm, post_ffn_norm=post_ffn_norm)
    small_m = dict(pre_mix_norm=m_pre_mix_norm, sgu_ln_gain=m_sgu_ln_gain, sgu_ln_bias=m_sgu_ln_bias, sgu_w_spatial=m_sgu_w_spatial,
                   sgu_b_spatial=m_sgu_b_spatial, attn_out_norm=m_attn_out_norm, sgu_out_norm=m_sgu_out_norm,
                   post_mix_norm=m_post_mix_norm, pre_ffn_norm=m_pre_ffn_norm, post_ffn_norm=m_post_ffn_norm)
    small_v = dict(pre_mix_norm=v_pre_mix_norm, sgu_ln_gain=v_sgu_ln_gain, sgu_ln_bias=v_sgu_ln_bias, sgu_w_spatial=v_sgu_w_spatial,
                   sgu_b_spatial=v_sgu_b_spatial, attn_out_norm=v_attn_out_norm, sgu_out_norm=v_sgu_out_norm,
                   post_mix_norm=v_post_mix_norm, pre_ffn_norm=v_pre_ffn_norm, post_ffn_norm=v_post_ffn_norm)

    x2d = x[0]
    target = loss_target[0]
    pos_col = positions.reshape(SEQ, 1)
    rot = _rot_consts()
    w_sp = sgu_w_spatial[0]
    bfull = jnp.repeat(sgu_b_spatial[0].T, HEAD_DIM, axis=1)

    x_i, y_i, c_i = (lax.axis_index(a).astype(jnp.int32) for a in MESH_AXES)
    dev = 4 * x_i + 2 * y_i + c_i
    core = c_i.reshape(1)
    chip = 2 * x_i + y_i
    chip_ids = jnp.stack([chip, chip ^ 1, chip ^ 2, chip ^ 3])
    dev_ids = jnp.stack([dev ^ m for m in range(N_DEV)])

    def gathered(name, bufs):
        return by_sequencer(name, [gather(bufs)], TO_GATHER)[0][0]

    def from_sibling(name, grads):
        return by_sequencer(name, [to_sibling(grads)], TO_SIBLING)[0][1]

    def from_chips(name, parts):
        return by_sequencer(name, [to_chips(parts)], TO_CHIPS)[0][1]

    (w_in_t,) = place_shards("place_w_in", [w_in[0].T], dev.reshape(1))
    (w_in_t,) = gathered("gather_w_in", [w_in_t])
    w_gate_t, w_up_t, w_out_f, w_down_f = place_shards(
        "place_weights", [w_gate[0].T, w_up[0].T, w_out[0], w_down[0]], dev.reshape(1))
    (w_out_f,) = gathered("gather_w_out", [w_out_f])
    w_gate_t, w_up_t = gathered("gather_w_gate_up", [w_gate_t, w_up_t])
    (w_down_f,) = gathered("gather_w_down", [w_down_f])

    h1, u, vs, q, k, v = in_proj(x2d, pos_col, pre_mix_norm, w_in_t, rot)
    attn_r, lse, attn = attn_fwd(q, k, v)
    (sgu,) = sgu_fwd(u, vs, sgu_ln_gain, sgu_ln_bias, w_sp, bfull)
    mix, y, x2, h2 = out_proj(attn, sgu, x2d, attn_out_norm, sgu_out_norm, w_out_f, post_mix_norm, pre_ffn_norm)
    gate, up, act = ffn_up(h2, w_gate_t, w_up_t)
    df, dx3, d_post_ffn, sq_err = ffn_down_loss(act, w_down_f, x2, post_ffn_norm, target)

    g_w_down = weight_grad("grad_w_down", act, df)
    (s_down,) = from_sibling("w_down_to_sibling", [g_w_down])
    dgate, dup, dx2, dy, d_pre_ffn, d_post_mix = ffn_bwd(
        df, w_down_f, gate, up, w_gate_t, w_up_t, x2, pre_ffn_norm, dx3, y, post_mix_norm, after=[g_w_down])
    (p_down,) = sum_cores("sum_cores_down", [g_w_down], [s_down], core, after=[dy])
    (c_down,) = from_chips("w_down_to_chips", [p_down])
    g_w_gate, g_w_up = weight_grads("grad_w_gate_up", [dgate, dup], h2, after=[p_down])
    s_gate, s_up = from_sibling("w_gate_up_to_sibling", [g_w_gate, g_w_up])
    g_w_out = weight_grad("grad_w_out", mix, dy, after=[g_w_up, c_down])
    p_gate, p_up = sum_cores("sum_cores_gate_up", [g_w_gate, g_w_up], [s_gate, s_up], core, after=[g_w_out])
    c_gate, c_up = from_chips("w_gate_up_to_chips", [p_gate, p_up])
    (s_out,) = from_sibling("w_out_to_sibling", [g_w_out])
    dsgu, d_attn_out, d_sgu_out, dattn_r = mix_bwd(dy, w_out_f, attn, sgu, attn_out_norm, sgu_out_norm, after=[p_gate, p_up])
    du, dvs, d_ln_gain, d_ln_bias, d_w_sp, d_bfull = sgu_bwd(u, vs, dsgu, sgu_ln_gain, sgu_ln_bias, w_sp, bfull)

    d_b_sp = d_bfull.reshape(CHUNK, N_GROUPS, HEAD_DIM).sum(axis=-1).T
    small_g = pack_small(dict(sgu_ln_gain=d_ln_gain, sgu_ln_bias=d_ln_bias, sgu_w_spatial=d_w_sp, sgu_b_spatial=d_b_sp,
                              attn_out_norm=d_attn_out, sgu_out_norm=d_sgu_out, post_mix_norm=d_post_mix,
                              pre_ffn_norm=d_pre_ffn, post_ffn_norm=d_post_ffn, loss_sum=sq_err))
    small_g = small_g.reshape(N_DEV, SMALL_ROWS // N_DEV, 128)
    ((_, (o_small,)),) = by_sequencer("small_to_owners", [to_owners(small_g)], TO_ALL)

    dq, dk, dv = attn_bwd(q, k, v, attn_r, lse, dattn_r, _to_residue_order(pos_col), rot)
    summed_small = sum_owned("sum_small", small_g, o_small, dev_ids, after=[dq])
    (all_small,) = gathered("gather_small_grads", [summed_small])
    (p_out,) = sum_cores("sum_cores_out", [g_w_out], [s_out], core, after=[dq])
    (c_out,) = from_chips("w_out_to_chips", [p_out])
    dq, dk, dv = (_from_residue_order(t) for t in (dq, dk, dv))
    dproj = [dq, dk, dv, du, dvs]
    g_w_in = weight_grad_of_parts("grad_w_in", dproj, h1, after=[c_gate, c_up])
    (s_in,) = from_sibling("w_in_to_sibling", [g_w_in])
    grad_x, d_pre_mix = in_bwd(dproj, w_in_t, x2d, pre_mix_norm, dx2, after=[g_w_in, all_small])
    (p_in,) = sum_cores("sum_cores_in", [g_w_in], [s_in], core, after=[d_pre_mix, c_out])
    (_, (c_in,)), (_, (late_parts,)) = by_sequencer(
        "last_sums_to_owners", [to_chips([p_in]), to_everyone(d_pre_mix)], TO_ALL)
    late_parts = lax.dynamic_update_slice(late_parts, d_pre_mix[None], (dev, 0, 0))

    def same(t):
        return t

    def turned(t):
        return t.T

    big, last = {}, p_in
    for call, weights in (("adamw_ffn", (("w_down", w_down, p_down, c_down, m_w_down, v_w_down, same),
                                         ("w_gate", w_gate, p_gate, c_gate, m_w_gate, v_w_gate, turned),
                                         ("w_up", w_up, p_up, c_up, m_w_up, v_w_up, turned))),
                          ("adamw_w_out", (("w_out", w_out, p_out, c_out, m_w_out, v_w_out, same),)),
                          ("adamw_w_in", (("w_in", w_in, p_in, c_in, m_w_in, v_w_in, turned),))):
        results = adamw_of_sums(call, [p for _, _, p, _, _, _, _ in weights], [c for _, _, _, c, _, _, _ in weights], chip_ids,
                                [turn(w[0]) for _, w, _, _, _, _, turn in weights], [turn(m[0]) for _, _, _, _, m, _, turn in weights],
                                [turn(vv[0]) for _, _, _, _, _, vv, turn in weights], last)
        for (name, _, _, _, _, _, turn), outs in zip(weights, results):
            big[name] = tuple(turn(t)[None] for t in outs)
        last = results[-1][0]
    sg, sd, snm, snv, loss_sum = adamw_small(all_small, late_parts, small_w, small_m, small_v)
    loss = loss_sum[0, 0] * np.float32(0.5 / D_MODEL)

    names = ["pre_mix_norm", "w_in", "sgu_ln_gain", "sgu_ln_bias", "sgu_w_spatial", "sgu_b_spatial", "attn_out_norm",
             "sgu_out_norm", "w_out", "post_mix_norm", "pre_ffn_norm", "w_gate", "w_up", "w_down", "post_ffn_norm"]
    outs = [loss, grad_x[None]]
    for i, table in enumerate((sg, sd, snm, snv)):
        for name in names:
            outs.append(big[name][i] if name in big else table[name])
    return tuple(outs)
```

```python
import numpy as np
import jax
import jax.numpy as jnp
from jax import lax
from jax.experimental import pallas as pl
from jax.experimental.pallas import tpu as pltpu
from jax.experimental.pallas import tpu_sc as plsc

F32 = jnp.float32
BF16 = jnp.bfloat16

SEQ = 2048
D_MODEL = 1024
ATTN_W = 512
SGU_W = 512
HEAD_DIM = 64
N_GROUPS = 8
CHUNK = 128
D_FF = 2816
IN_W = 3 * ATTN_W + 2 * SGU_W
DILATIONS = (1, 4, 16)
ROPE_THETA = 500000.0
ROT_DIM = 16
ROT_HALF = 8
RMS_EPS = 1e-6
LN_EPS = 1e-5
Q_SCALE = 0.125
NEG = -1e30

N_DEV = 8
MESH_AXES = ("x", "y", "c")
MESH = pl.DeviceIdType.MESH

ADAM_LR = 0.001
ADAM_B1 = 0.9
ADAM_B2 = 0.999
ADAM_EPS = 1e-08
ADAM_WD = 0.01
ADAM_STEP = 10

VMEM_LIMIT = 60 * 1024 * 1024
ANY = pl.BlockSpec(memory_space=pl.ANY)

SMALL = (("pre_mix_norm", 1024), ("sgu_ln_gain", 512), ("sgu_ln_bias", 512), ("sgu_w_spatial", 8 * 128 * 128),
         ("sgu_b_spatial", 1024), ("attn_out_norm", 512), ("sgu_out_norm", 512), ("post_mix_norm", 1024),
         ("pre_ffn_norm", 1024), ("post_ffn_norm", 1024), ("loss_sum", 1))
SMALL_ROWS = 1152


def _params(sem=("arbitrary",)):
    return pltpu.CompilerParams(dimension_semantics=sem, vmem_limit_bytes=VMEM_LIMIT)


def _dot(a, b):
    return jnp.dot(a, b, preferred_element_type=F32)


def _dot_nt(a, b):
    return lax.dot_general(a, b, (((1,), (1,)), ((), ())), preferred_element_type=F32)


def _dot_tn(a, b):
    return lax.dot_general(a, b, (((0,), (0,)), ((), ())), preferred_element_type=F32)


def _rms(z):
    return lax.rsqrt(jnp.mean(z * z, axis=-1, keepdims=True) + RMS_EPS)


def _rms_bwd(z, gain, d):
    r = _rms(z)
    n = z * r
    dn = d * gain
    dz = r * (dn - n * jnp.mean(dn * n, axis=-1, keepdims=True))
    return dz, jnp.sum(d * n, axis=0, keepdims=True)


def _gelu(z):
    return 0.5 * z * (1.0 + lax.erf(z * np.float32(1.0 / np.sqrt(2.0))))


def _gelu_grad(z):
    cdf = 0.5 * (1.0 + lax.erf(z * np.float32(1.0 / np.sqrt(2.0))))
    return cdf + z * jnp.exp(-0.5 * z * z) * np.float32(1.0 / np.sqrt(2.0 * np.pi))


def _rot_tables(pos_col, invf, ma, mb):
    ang = pos_col.astype(F32) * invf
    s = jnp.sin(ang)
    return jnp.cos(ang), s * ma, s * mb


def _rot(t, c, sa, sb):
    return t * c + pltpu.roll(t, 120, 1) * sa + pltpu.roll(t, 8, 1) * sb


def _rot_t(d, c, sa, sb):
    return d * c + pltpu.roll(d * sa, 8, 1) + pltpu.roll(d * sb, 120, 1)


def _rot_consts():
    lane = np.arange(128) % HEAD_DIM
    inv_freq = (np.float32(ROPE_THETA) ** (-np.arange(0, ROT_DIM, 2, dtype=np.float32) / np.float32(ROT_DIM))).astype(np.float32)
    invf = np.where(lane < ROT_DIM, inv_freq[lane % ROT_HALF], 0.0).astype(np.float32)
    ma = np.where(lane < ROT_HALF, -1.0, 0.0).astype(np.float32)
    mb = np.where((lane >= ROT_HALF) & (lane < ROT_DIM), 1.0, 0.0).astype(np.float32)
    return jnp.asarray(invf[None]), jnp.asarray(ma[None]), jnp.asarray(mb[None])


def _row_spec(tm, w):
    return pl.BlockSpec((tm, w), lambda i: (i, 0))


def _full_spec(shape):
    return pl.BlockSpec(shape, lambda i: (0,) * len(shape))


def _weight_spec(shape):
    return pl.BlockSpec(shape, lambda i: (0,) * len(shape), pipeline_mode=pl.Buffered(1))


FF_CHUNKS = (256, 512, 1024, 1024)
FF_SPANS = [(int(o), n) for o, n in zip(np.cumsum((0,) + FF_CHUNKS[:-1]), FF_CHUNKS)]
FF_WHOLE = [(0, D_FF)]
IN_SPANS = [(512 * j, 512) for j in range(5)]
SEQ_SPANS = [(0, 512), (512, 512), (1024, 1024)]


def _ffn_weight_scratch(n_weights, spans):
    return _weight_scratch((D_FF, D_MODEL), n_weights, spans)


def _weight_scratch(shape, n_weights, spans):
    return [pltpu.VMEM(shape, BF16)] * n_weights + [pltpu.SemaphoreType.DMA((n_weights, len(spans)))]


def _with_weights(w_hbm, w_vmem, sems, spans, run):
    copies = [[pltpu.make_async_copy(h.at[pl.ds(o, n)], v.at[pl.ds(o, n)], sems.at[j, c]) for c, (o, n) in enumerate(spans)]
              for j, (h, v) in enumerate(zip(w_hbm, w_vmem))]
    first = pl.program_id(0) == 0

    @pl.when(first)
    def _():
        for of_weight in copies:
            for cp in of_weight:
                cp.start()
        run(lambda j, c: copies[j][c].wait())

    @pl.when(jnp.logical_not(first))
    def _():
        run(None)


RES = 16


def _residue_scratch(n_arrays, tm, width):
    return [pltpu.VMEM((2, n_arrays, tm // RES, RES, width), F32), pltpu.SemaphoreType.DMA((2, n_arrays, RES))]


def _to_residue_rows(tiles, outs, scratch, sems, tm, n_steps):
    i = pl.program_id(0)
    slot = i % 2
    per = tm // RES

    def copies(step, s):
        return [pltpu.make_async_copy(scratch.at[s, a, :, b, :],
                                      outs[a].at[pl.ds(pl.multiple_of(b * (SEQ // RES) + per * step, per), per), :],
                                      sems.at[s, a, b]) for a in range(len(outs)) for b in range(RES)]

    @pl.when(i >= 2)
    def _():
        for cp in copies(i - 2, slot):
            cp.wait()

    for a, tile in enumerate(tiles):
        scratch[slot, a] = tile.reshape(per, RES, tile.shape[-1])
    for cp in copies(i, slot):
        cp.start()

    @pl.when(i == n_steps - 1)
    def _():
        for cp in copies(i - 1, 1 - slot) + copies(i, slot):
            cp.wait()


def in_proj(x, pos_col, g1, w_in_t, rot):
    tm = 512
    n_steps = SEQ // tm

    def body(x_ref, pos_ref, g_ref, w_hbm, invf_ref, ma_ref, mb_ref, h_ref, u_ref, vs_ref, q_ref, k_ref, v_ref,
             scratch, sems, w_ref, w_sems):
        def run(wait):
            xf = x_ref[...]
            h = (xf * _rms(xf) * g_ref[...]).astype(BF16)
            h_ref[...] = h
            c, sa, sb = _rot_tables(pos_ref[...], invf_ref[...], ma_ref[...], mb_ref[...])
            if wait:
                def part(j):
                    wait(0, j)
                    return _dot_nt(h, w_ref[512 * j:512 * (j + 1), :])
            else:
                proj = _dot_nt(h, w_ref[...])

                def part(j):
                    return proj[:, 512 * j:512 * (j + 1)]
            slabs = range(ATTN_W // 128)
            pq = part(0)
            q = jnp.concatenate([_rot(pq[:, j * 128:(j + 1) * 128], c, sa, sb) * Q_SCALE for j in slabs], axis=1)
            pk = part(1)
            k = jnp.concatenate([_rot(pk[:, j * 128:(j + 1) * 128], c, sa, sb) for j in slabs], axis=1)
            v = part(2)
            u_ref[...] = part(3)
            vs_ref[...] = part(4)
            _to_residue_rows([q, k, v], [q_ref, k_ref, v_ref], scratch, sems, tm, n_steps)

        _with_weights([w_hbm], [w_ref], w_sems, IN_SPANS, run)

    assert ATTN_W == SGU_W == 512 and IN_W == 5 * 512
    act = jax.ShapeDtypeStruct((SEQ, 512), F32)
    return _call(
        "in_proj", body, n_steps,
        [_row_spec(tm, D_MODEL), _row_spec(tm, 1), _full_spec((1, D_MODEL)), ANY,
         _full_spec((1, 128)), _full_spec((1, 128)), _full_spec((1, 128))],
        [_row_spec(tm, D_MODEL)] + [_row_spec(tm, 512)] * 2 + [ANY] * 3,
        [jax.ShapeDtypeStruct((SEQ, D_MODEL), BF16)] + [act] * 5,
        (x, pos_col, g1, w_in_t, *rot),
        scratch_shapes=_residue_scratch(3, tm, ATTN_W) + _weight_scratch((IN_W, D_MODEL), 1, IN_SPANS))


def _to_residue_order(t):
    return t.reshape(SEQ // RES, RES, -1).transpose(1, 0, 2).reshape(t.shape)


def _from_residue_order(t):
    return t.reshape(RES, SEQ // RES, -1).transpose(1, 0, 2).reshape(t.shape)


def _block_rows(d, r, n):
    if d == 16:
        slices = [(128 * r, 128)]
    elif d == 4:
        slices = [(128 * (4 * b + r) + 32 * n, 32) for b in range(4)]
    else:
        slices = [(128 * b + 8 * n, 8) for b in range(RES)]
    return [(s if isinstance(s, int) else pl.multiple_of(s, z), z) for s, z in slices]


def _block_step(d, i):
    if d == 16:
        return i
    if d == 4:
        return 4 * (i & 31) + (i >> 5)
    return 16 * (i & 7) + (i >> 3)


def _attn_masks(d):
    row2 = _block_step(d, lax.broadcasted_iota(jnp.int32, (128, 256), 0))
    col2 = lax.broadcasted_iota(jnp.int32, (128, 256), 1)
    key2 = _block_step(d, col2 & 127)
    mask2 = jnp.logical_or(jnp.logical_and(col2 < 128, key2 >= row2), jnp.logical_and(col2 >= 128, key2 <= row2))
    row1 = _block_step(d, lax.broadcasted_iota(jnp.int32, (128, 128), 0))
    col1 = lax.broadcasted_iota(jnp.int32, (128, 128), 1)
    return col1 < HEAD_DIM, _block_step(d, col1) <= row1, mask2


def _load_rows(ref, slices):
    parts = [ref[pl.ds(s, z), :] for s, z in slices]
    return parts[0] if len(parts) == 1 else jnp.concatenate(parts, axis=0)


def _for_each_group(fn):
    for p, d in enumerate(DILATIONS):
        masks = _attn_masks(d)
        if d == 16:
            def group(i, carry, p=p, masks=masks):
                fn(p, masks, [(_block_rows(16, 8 * i + g, 0), None) for g in range(8)])
                return carry

            lax.fori_loop(0, 2, group, 0)
        elif d == 4:
            fn(p, masks, [(_block_rows(4, r, 0), None) for r in range(4)])

            def group(i, carry, p=p, masks=masks):
                blocks = [6 * i + g for g in range(6)]
                fn(p, masks, [(_block_rows(4, j % 4, 1 + j // 4), _block_rows(4, j % 4, j // 4)) for j in blocks])
                return carry

            lax.fori_loop(0, 2, group, 0)
        else:
            fn(p, masks, [(_block_rows(1, 0, 0), None)])

            def group(i, carry, p=p, masks=masks):
                fn(p, masks, [(_block_rows(1, 0, 5 * i + g + 1), _block_rows(1, 0, 5 * i + g)) for g in range(5)])
                return carry

            lax.fori_loop(0, 3, group, 0)


def attn_fwd(q, k, v):
    def body(q_ref, k_ref, v_ref, o_ref, lse_ref, nat_ref, op_ref, lp_ref, sems):
        def group(p, masks, blocks):
            head0, mask1, mask2 = masks
            heads = (head0, jnp.logical_not(head0))
            keys = [rows if prev is None else prev + rows for rows, prev in blocks]
            mask = [mask1 if prev is None else mask2 for _, prev in blocks]
            qb = [_load_rows(q_ref, rows) for rows, _ in blocks]
            kk = [_load_rows(k_ref, ks).astype(BF16) for ks in keys]
            vv = [_load_rows(v_ref, ks).astype(BF16) for ks in keys]
            chains = [(g, hm) for g in range(len(blocks)) for hm in heads]
            s = [jnp.where(mask[g], _dot_nt(jnp.where(hm, qb[g], 0.0).astype(BF16), kk[g]), NEG) for g, hm in chains]
            m = [jnp.max(t, axis=-1, keepdims=True) for t in s]
            e = [jnp.exp(t - mt) for t, mt in zip(s, m)]
            l = [jnp.sum(t, axis=-1, keepdims=True) for t in e]
            pv = [_dot(t.astype(BF16), vv[g]) for t, (g, _) in zip(e, chains)]
            for g, (rows, _) in enumerate(blocks):
                o_blk = jnp.where(head0, pv[2 * g] / l[2 * g], pv[2 * g + 1] / l[2 * g + 1])
                l_blk = jnp.where(head0, jnp.broadcast_to(m[2 * g] + jnp.log(l[2 * g]), (128, 128)),
                                  jnp.broadcast_to(m[2 * g + 1] + jnp.log(l[2 * g + 1]), (128, 128)))
                at = 0
                for start, size in rows:
                    op_ref[p, pl.ds(start, size), :] = o_blk[at:at + size]
                    lp_ref[p, pl.ds(start, size), :] = l_blk[at:at + size]
                    at += size

        _for_each_group(group)

        def combine(i, carry):
            rows = pl.ds(pl.multiple_of(i * 256, 256), 256)
            ls = [lp_ref[p, rows, :] for p in range(3)]
            m = jnp.maximum(jnp.maximum(ls[0], ls[1]), ls[2])
            lse = m + jnp.log(jnp.exp(ls[0] - m) + jnp.exp(ls[1] - m) + jnp.exp(ls[2] - m))
            o = jnp.zeros((256, 128), F32)
            for p in range(3):
                o = o + jnp.exp(ls[p] - lse) * op_ref[p, rows, :]
            o_ref[rows, :] = o
            lse_ref[rows, :] = lse
            return carry

        lax.fori_loop(0, SEQ // 256, combine, 0)

        lanes = pl.ds(pl.multiple_of(pl.program_id(0) * 128, 128), 128)
        back = [pltpu.make_async_copy(o_ref.at[pl.ds(b * (SEQ // RES), SEQ // RES), :], nat_ref.at[:, b, lanes], sems.at[b])
                for b in range(RES)]
        for cp in back:
            cp.start()
        for cp in back:
            cp.wait()

    slab = pl.BlockSpec((SEQ, 128), lambda i: (0, i))
    out = jax.ShapeDtypeStruct((SEQ, ATTN_W), F32)
    attn_r, lse, attn = _call(
        "attn_fwd", body, ATTN_W // 128, [slab] * 3, [slab] * 2 + [ANY],
        [out, out, jax.ShapeDtypeStruct((SEQ // RES, RES, ATTN_W), F32)], (q, k, v),
        scratch_shapes=[pltpu.VMEM((3, SEQ, 128), F32), pltpu.VMEM((3, SEQ, 128), F32), pltpu.SemaphoreType.DMA((RES,))])
    return attn_r, lse, attn.reshape(SEQ, ATTN_W)


def _causal_weights(w_ref):
    row = lax.broadcasted_iota(jnp.int32, (CHUNK, CHUNK), 0)
    col = lax.broadcasted_iota(jnp.int32, (CHUNK, CHUNK), 1)
    return [jnp.where(col <= row, w_ref[g], 0.0).astype(BF16) for g in range(N_GROUPS)], col <= row


def _sgu_chunk_fwd(u, vs, lg, lb, wc, bfull, head0):
    ug = _gelu(u)
    vg = _gelu(vs)
    xc = vg - jnp.mean(vg, axis=-1, keepdims=True)
    rstd = lax.rsqrt(jnp.mean(xc * xc, axis=-1, keepdims=True) + LN_EPS)
    xhat = xc * rstd
    vn = xhat * lg + lb
    mixed = []
    for gp in range(SGU_W // 128):
        vp = vn[:, gp * 128:(gp + 1) * 128].astype(BF16)
        mixed.append(jnp.where(head0, _dot(wc[2 * gp], vp), _dot(wc[2 * gp + 1], vp)))
    ms = jnp.concatenate(mixed, axis=1) + bfull
    return ug, xhat, rstd, vn, ms


def sgu_fwd(u, vs, lg, lb, w_sp, bfull):
    cpb = 4

    def body(u_ref, vs_ref, lg_ref, lb_ref, w_ref, b_ref, o_ref):
        wc, _ = _causal_weights(w_ref)
        head0 = lax.broadcasted_iota(jnp.int32, (CHUNK, 128), 1) < HEAD_DIM
        for ci in range(cpb):
            rows = pl.ds(ci * CHUNK, CHUNK)
            ug, _, _, _, ms = _sgu_chunk_fwd(u_ref[rows, :], vs_ref[rows, :], lg_ref[...], lb_ref[...], wc, b_ref[...], head0)
            o_ref[rows, :] = ug * ms

    tm = cpb * CHUNK
    return _call(
        "sgu_fwd", body, SEQ // tm,
        [_row_spec(tm, SGU_W), _row_spec(tm, SGU_W), _full_spec((1, SGU_W)), _full_spec((1, SGU_W)),
         _full_spec((N_GROUPS, CHUNK, CHUNK)), _full_spec((CHUNK, SGU_W))],
        [_row_spec(tm, SGU_W)], [jax.ShapeDtypeStruct((SEQ, SGU_W), F32)],
        (u, vs, lg, lb, w_sp, bfull))


def out_proj(attn, sgu, x, ga, gs, w_out, gpm, gpf):
    tm = 512

    def body(a_ref, s_ref, x_ref, ga_ref, gs_ref, w_ref, gpm_ref, gpf_ref, mix_ref, y_ref, x2_ref, h2_ref):
        a = a_ref[...]
        s = s_ref[...]
        an = (a * _rms(a) * ga_ref[...]).astype(BF16)
        sn = (s * _rms(s) * gs_ref[...]).astype(BF16)
        mix_ref[:, :ATTN_W] = an
        mix_ref[:, ATTN_W:] = sn
        y = _dot(an, w_ref[:ATTN_W, :]) + _dot(sn, w_ref[ATTN_W:, :])
        y_ref[...] = y
        x2 = x_ref[...] + y * _rms(y) * gpm_ref[...]
        x2_ref[...] = x2
        h2_ref[...] = (x2 * _rms(x2) * gpf_ref[...]).astype(BF16)

    wide = jax.ShapeDtypeStruct((SEQ, D_MODEL), F32)
    wide16 = jax.ShapeDtypeStruct((SEQ, D_MODEL), BF16)
    return _call(
        "out_proj", body, SEQ // tm,
        [_row_spec(tm, ATTN_W), _row_spec(tm, SGU_W), _row_spec(tm, D_MODEL), _full_spec((1, ATTN_W)),
         _full_spec((1, SGU_W)), _weight_spec((D_MODEL, D_MODEL)), _full_spec((1, D_MODEL)), _full_spec((1, D_MODEL))],
        [_row_spec(tm, D_MODEL)] * 4, [wide16, wide, wide, wide16],
        (attn, sgu, x, ga, gs, w_out, gpm, gpf))


def ffn_up(h2, w_gate_t, w_up_t):
    tm = 256

    def body(h_ref, wg_hbm, wu_hbm, g_ref, u_ref, a_ref, wg_ref, wu_ref, sems):
        def run(wait):
            h = h_ref[...]
            if wait:
                wait(0, 0)
            g = _dot_nt(h, wg_ref[...])
            g_ref[...] = g.astype(BF16)
            if wait:
                wait(1, 0)
            u = _dot_nt(h, wu_ref[...])
            u_ref[...] = u.astype(BF16)
            a_ref[...] = (g * jax.nn.sigmoid(g) * u).astype(BF16)

        _with_weights([wg_hbm, wu_hbm], [wg_ref, wu_ref], sems, FF_WHOLE, run)

    ff = jax.ShapeDtypeStruct((SEQ, D_FF), BF16)
    return _call(
        "ffn_up", body, SEQ // tm, [_row_spec(tm, D_MODEL), ANY, ANY],
        [_row_spec(tm, D_FF)] * 3, [ff, ff, jax.ShapeDtypeStruct((SEQ, D_FF), BF16)],
        (h2, w_gate_t, w_up_t), scratch_shapes=_ffn_weight_scratch(2, FF_WHOLE))


def ffn_down_loss(act, w_down, x2, gpo, target):
    tm = 512

    def body(a_ref, w_hbm, x2_ref, g_ref, t_ref, df_ref, dx3_ref, dg_ref, loss_ref, w_ref, sems):
        def run(wait):
            f = None
            for c, (o, n) in enumerate(FF_SPANS if wait else FF_WHOLE):
                if wait:
                    wait(0, c)
                part = _dot(a_ref[:, o:o + n], w_ref[o:o + n, :])
                f = part if f is None else f + part
            gain = g_ref[...]
            err = x2_ref[...] + f * _rms(f) * gain - t_ref[...]
            dx3 = err * np.float32(1.0 / D_MODEL)
            dx3_ref[...] = dx3
            df, dg = _rms_bwd(f, gain, dx3)
            df_ref[...] = df.astype(BF16)
            loss = jnp.sum(err * err, axis=(0, 1), keepdims=True)
            if wait:
                dg_ref[...] = dg
                loss_ref[...] = loss
            else:
                dg_ref[...] += dg
                loss_ref[...] += loss

        _with_weights([w_hbm], [w_ref], sems, FF_SPANS, run)

    return _call(
        "ffn_down_loss", body, SEQ // tm,
        [_row_spec(tm, D_FF), ANY, _row_spec(tm, D_MODEL), _full_spec((1, D_MODEL)), _row_spec(tm, D_MODEL)],
        [_row_spec(tm, D_MODEL), _row_spec(tm, D_MODEL), _full_spec((1, D_MODEL)), _full_spec((1, 1))],
        [jax.ShapeDtypeStruct((SEQ, D_MODEL), BF16), jax.ShapeDtypeStruct((SEQ, D_MODEL), F32),
         jax.ShapeDtypeStruct((1, D_MODEL), F32), jax.ShapeDtypeStruct((1, 1), F32)],
        (act, w_down, x2, gpo, target), scratch_shapes=_ffn_weight_scratch(1, FF_SPANS))


def ffn_bwd(df, w_down, gate, up, w_gate_t, w_up_t, x2, gpf, dx3, y, gpm, after=()):
    tm = 256

    def body(df_ref, wd_hbm, g_ref, u_ref, wg_hbm, wu_hbm, x2_ref, gpf_ref, dx3_ref, y_ref, gpm_ref,
             dg_ref, du_ref, dx2_ref, dy_ref, dgpf_ref, dgpm_ref, wd_ref, wg_ref, wu_ref, sems):
        def run(wait):
            if wait:
                wait(0, 0)
            dact = _dot_nt(df_ref[...], wd_ref[...])
            g = g_ref[...].astype(F32)
            s = jax.nn.sigmoid(g)
            dup = (dact * g * s).astype(BF16)
            dgate = (dact * u_ref[...].astype(F32) * (s * (1.0 + g * (1.0 - s)))).astype(BF16)
            du_ref[...] = dup
            dg_ref[...] = dgate
            if wait:
                wait(1, 0)
                wait(2, 0)
            dh2 = _dot(dgate, wg_ref[...]) + _dot(dup, wu_ref[...])
            dz, dgpf = _rms_bwd(x2_ref[...], gpf_ref[...], dh2)
            dx2 = dx3_ref[...] + dz
            dx2_ref[...] = dx2
            dy, dgpm = _rms_bwd(y_ref[...], gpm_ref[...], dx2)
            dy_ref[...] = dy.astype(BF16)
            if wait:
                dgpf_ref[...] = dgpf
                dgpm_ref[...] = dgpm
            else:
                dgpf_ref[...] += dgpf
                dgpm_ref[...] += dgpm

        _with_weights([wd_hbm, wg_hbm, wu_hbm], [wd_ref, wg_ref, wu_ref], sems, FF_WHOLE, run)

    vec = jax.ShapeDtypeStruct((1, D_MODEL), F32)
    ff16 = jax.ShapeDtypeStruct((SEQ, D_FF), BF16)
    return _call(
        "ffn_bwd", body, SEQ // tm,
        [_row_spec(tm, D_MODEL), ANY, _row_spec(tm, D_FF), _row_spec(tm, D_FF), ANY, ANY, _row_spec(tm, D_MODEL),
         _full_spec((1, D_MODEL)), _row_spec(tm, D_MODEL), _row_spec(tm, D_MODEL), _full_spec((1, D_MODEL))],
        [_row_spec(tm, D_FF), _row_spec(tm, D_FF), _row_spec(tm, D_MODEL), _row_spec(tm, D_MODEL),
         _full_spec((1, D_MODEL)), _full_spec((1, D_MODEL))],
        [ff16, ff16, jax.ShapeDtypeStruct((SEQ, D_MODEL), F32), jax.ShapeDtypeStruct((SEQ, D_MODEL), BF16), vec, vec],
        (df, w_down, gate, up, w_gate_t, w_up_t, x2, gpf, dx3, y, gpm), scratch_shapes=_ffn_weight_scratch(3, FF_WHOLE), after=after)


def weight_grads(name, lhs, b, after=()):
    m, n, k = lhs[0].shape[1], b.shape[1], len(lhs)
    tr = 256

    def body(*refs):
        b_ref, sems = refs[2 * k + 1:]

        def run(wait):
            _products_over_seq(refs[:k], b_ref, refs[k + 1:2 * k + 1], wait)

        _with_weights([refs[k]], [b_ref], sems, SEQ_SPANS, run)

    outs = _call(
        name, body, m // tr, [pl.BlockSpec((SEQ, tr), lambda i: (0, i))] * k + [ANY],
        [_row_spec(tr, n)] * k, [jax.ShapeDtypeStruct((m, n), BF16)] * k, (*lhs, b),
        scratch_shapes=_weight_scratch(b.shape, 1, SEQ_SPANS), after=after)
    return [out.reshape(N_DEV, m // N_DEV, n) for out in outs]


def _products_over_seq(a_refs, b_ref, o_refs, wait):
    accs = [None] * len(a_refs)
    for c, (o, rows) in enumerate(SEQ_SPANS if wait else [(0, SEQ)]):
        if wait:
            wait(0, c)
        for j, a_ref in enumerate(a_refs):
            part = _dot_tn(a_ref[o:o + rows, :], b_ref[o:o + rows, :])
            accs[j] = part if accs[j] is None else accs[j] + part
    for o_ref, acc in zip(o_refs, accs):
        o_ref[...] = acc.astype(BF16)


def weight_grad(name, a, b, after=()):
    return weight_grads(name, [a], b, after)[0]


def weight_grad_of_parts(name, parts, b, after=()):
    p, n, k = parts[0].shape[1], b.shape[1], len(parts)
    tr = 512
    per = p // tr

    def body(*refs):
        b_hbm, o_ref, b_ref, sems = refs[k:]
        tile = pl.program_id(0)

        def run(wait):
            if wait:
                _products_over_seq(refs[:1], b_ref, [o_ref], wait)
            else:
                for j in range(k):
                    @pl.when(tile // per == j)
                    def _(j=j):
                        _products_over_seq(refs[j:j + 1], b_ref, [o_ref], None)

        _with_weights([b_hbm], [b_ref], sems, SEQ_SPANS, run)

    def part_spec(j):
        return pl.BlockSpec((SEQ, tr), lambda i: (0, jnp.clip(i - per * j, 0, per - 1)))

    (out,) = _call(
        name, body, k * per, [part_spec(j) for j in range(k)] + [ANY],
        [_row_spec(tr, n)], [jax.ShapeDtypeStruct((k * p, n), BF16)], (*parts, b),
        scratch_shapes=_weight_scratch(b.shape, 1, SEQ_SPANS), after=after)
    return out.reshape(N_DEV, k * p // N_DEV, n)


def mix_bwd(dy, w_out, attn, sgu, ga, gs, after=()):
    tm = 512
    n_steps = SEQ // tm

    def body(dy_ref, w_ref, a_ref, s_ref, ga_ref, gs_ref, ds_ref, dga_ref, dgs_ref, da_ref, scratch, sems):
        dy = dy_ref[...]
        da, dga = _rms_bwd(a_ref[...], ga_ref[...], _dot_nt(dy, w_ref[:ATTN_W, :]))
        ds, dgs = _rms_bwd(s_ref[...], gs_ref[...], _dot_nt(dy, w_ref[ATTN_W:, :]))
        ds_ref[...] = ds
        _to_residue_rows([da], [da_ref], scratch, sems, tm, n_steps)

        @pl.when(pl.program_id(0) == 0)
        def _():
            dga_ref[...] = jnp.zeros_like(dga_ref)
            dgs_ref[...] = jnp.zeros_like(dgs_ref)

        dga_ref[...] += dga
        dgs_ref[...] += dgs

    half = jax.ShapeDtypeStruct((SEQ, 512), F32)
    vec = jax.ShapeDtypeStruct((1, 512), F32)
    return _call(
        "mix_bwd", body, n_steps,
        [_row_spec(tm, D_MODEL), _weight_spec((D_MODEL, D_MODEL)), _row_spec(tm, 512), _row_spec(tm, 512),
         _full_spec((1, 512)), _full_spec((1, 512))],
        [_row_spec(tm, 512), _full_spec((1, 512)), _full_spec((1, 512)), ANY],
        [half, vec, vec, half], (dy, w_out, attn, sgu, ga, gs), scratch_shapes=_residue_scratch(1, tm, ATTN_W), after=after)


def sgu_bwd(u, vs, dsgu, lg, lb, w_sp, bfull):
    cpb = 4

    def body(u_ref, vs_ref, d_ref, lg_ref, lb_ref, w_ref, b_ref, du_ref, dvs_ref, dlg_ref, dlb_ref, dw_ref, db_ref):
        wc, causal = _causal_weights(w_ref)
        head0 = lax.broadcasted_iota(jnp.int32, (CHUNK, 128), 1) < HEAD_DIM
        lg = lg_ref[...]

        @pl.when(pl.program_id(0) == 0)
        def _():
            dlg_ref[...] = jnp.zeros_like(dlg_ref)
            dlb_ref[...] = jnp.zeros_like(dlb_ref)
            dw_ref[...] = jnp.zeros_like(dw_ref)
            db_ref[...] = jnp.zeros_like(db_ref)

        for ci in range(cpb):
            rows = pl.ds(ci * CHUNK, CHUNK)
            u = u_ref[rows, :]
            vs = vs_ref[rows, :]
            d = d_ref[rows, :]
            ug, xhat, rstd, vn, ms = _sgu_chunk_fwd(u, vs, lg, lb_ref[...], wc, b_ref[...], head0)
            du_ref[rows, :] = (d * ms * _gelu_grad(u)).astype(BF16)
            dms = d * ug
            db_ref[...] += dms
            dvn = []
            for gp in range(SGU_W // 128):
                dmp = dms[:, gp * 128:(gp + 1) * 128]
                dm0 = jnp.where(head0, dmp, 0.0).astype(BF16)
                dm1 = jnp.where(head0, 0.0, dmp).astype(BF16)
                vp = vn[:, gp * 128:(gp + 1) * 128].astype(BF16)
                dw_ref[2 * gp] += _dot_nt(dm0, vp)
                dw_ref[2 * gp + 1] += _dot_nt(dm1, vp)
                dvn.append(_dot_tn(wc[2 * gp], dm0) + _dot_tn(wc[2 * gp + 1], dm1))
            dvn = jnp.concatenate(dvn, axis=1)
            dlg_ref[...] += jnp.sum(dvn * xhat, axis=0, keepdims=True)
            dlb_ref[...] += jnp.sum(dvn, axis=0, keepdims=True)
            dxh = dvn * lg
            dvg = rstd * (dxh - jnp.mean(dxh, axis=-1, keepdims=True) - xhat * jnp.mean(dxh * xhat, axis=-1, keepdims=True))
            dvs_ref[rows, :] = (dvg * _gelu_grad(vs)).astype(BF16)

        @pl.when(pl.program_id(0) == pl.num_programs(0) - 1)
        def _():
            for g in range(N_GROUPS):
                dw_ref[g] = jnp.where(causal, dw_ref[g], 0.0)

    tm = cpb * CHUNK
    half16 = jax.ShapeDtypeStruct((SEQ, SGU_W), BF16)
    vec = jax.ShapeDtypeStruct((1, SGU_W), F32)
    return _call(
        "sgu_bwd", body, SEQ // tm,
        [_row_spec(tm, SGU_W)] * 3 + [_full_spec((1, SGU_W)), _full_spec((1, SGU_W)),
                                      _full_spec((N_GROUPS, CHUNK, CHUNK)), _full_spec((CHUNK, SGU_W))],
        [_row_spec(tm, SGU_W), _row_spec(tm, SGU_W), _full_spec((1, SGU_W)), _full_spec((1, SGU_W)),
         _full_spec((N_GROUPS, CHUNK, CHUNK)), _full_spec((CHUNK, SGU_W))],
        [half16, half16, vec, vec, jax.ShapeDtypeStruct((N_GROUPS, CHUNK, CHUNK), F32),
         jax.ShapeDtypeStruct((CHUNK, SGU_W), F32)],
        (u, vs, dsgu, lg, lb, w_sp, bfull))


def attn_bwd(q, k, v, o, lse, do, pos_col, rot):
    def body(q_ref, k_ref, v_ref, o_ref, lse_ref, do_ref, pos_ref, invf_ref, ma_ref, mb_ref,
             dq_ref, dk_ref, dv_ref, dqa_ref, dka_ref, dva_ref, dlt_ref, rot_ref):
        dqa_ref[...] = jnp.zeros_like(dqa_ref)
        dka_ref[...] = jnp.zeros_like(dka_ref)
        dva_ref[...] = jnp.zeros_like(dva_ref)

        def delta(i, carry):
            rows = pl.ds(pl.multiple_of(i * 256, 256), 256)
            prod = do_ref[rows, :] * o_ref[rows, :]
            h0 = lax.broadcasted_iota(jnp.int32, (256, 128), 1) < HEAD_DIM
            d0 = jnp.sum(jnp.where(h0, prod, 0.0), axis=-1, keepdims=True)
            d1 = jnp.sum(jnp.where(h0, 0.0, prod), axis=-1, keepdims=True)
            dlt_ref[rows, :] = jnp.where(h0, d0, d1)
            return carry

        lax.fori_loop(0, SEQ // 256, delta, 0)

        def add_rows(ref, slices, val):
            at = 0
            for start, size in slices:
                ref[pl.ds(start, size), :] += val[at:at + size]
                at += size

        def group(p, masks, blocks):
            head0, mask1, mask2 = masks
            heads = (head0, jnp.logical_not(head0))
            keys = [rows if prev is None else prev + rows for rows, prev in blocks]
            mask = [mask1 if prev is None else mask2 for _, prev in blocks]
            kk = [_load_rows(k_ref, ks).astype(BF16) for ks in keys]
            vv = [_load_rows(v_ref, ks).astype(BF16) for ks in keys]
            qb = [_load_rows(q_ref, rows) for rows, _ in blocks]
            dob = [_load_rows(do_ref, rows) for rows, _ in blocks]
            lse_b = [_load_rows(lse_ref, rows) for rows, _ in blocks]
            dlt_b = [_load_rows(dlt_ref, rows) for rows, _ in blocks]
            chains = [(g, h) for g in range(len(blocks)) for h in range(2)]
            qm = [jnp.where(heads[h], qb[g], 0.0).astype(BF16) for g, h in chains]
            dom = [jnp.where(heads[h], dob[g], 0.0).astype(BF16) for g, h in chains]
            s = [_dot_nt(qm[c], kk[g]) for c, (g, h) in enumerate(chains)]
            dp = [_dot_nt(dom[c], vv[g]) for c, (g, h) in enumerate(chains)]
            pr = [jnp.where(mask[g], jnp.exp(s[c] - lse_b[g][:, h * HEAD_DIM:h * HEAD_DIM + 1]), 0.0)
                  for c, (g, h) in enumerate(chains)]
            ds = [(pr[c] * (dp[c] - dlt_b[g][:, h * HEAD_DIM:h * HEAD_DIM + 1])).astype(BF16)
                  for c, (g, h) in enumerate(chains)]
            dv = [_dot_tn(pr[c].astype(BF16), dom[c]) for c in range(len(chains))]
            dk = [_dot_tn(ds[c], qm[c]) for c in range(len(chains))]
            dq = [_dot(ds[c], kk[g]) for c, (g, h) in enumerate(chains)]
            for g, (rows, _) in enumerate(blocks):
                add_rows(dqa_ref, rows, jnp.where(head0, dq[2 * g], dq[2 * g + 1]))
                add_rows(dka_ref, keys[g], dk[2 * g] + dk[2 * g + 1])
                add_rows(dva_ref, keys[g], dv[2 * g] + dv[2 * g + 1])

        _for_each_group(group)

        @pl.when(pl.program_id(0) == 0)
        def _():
            def tables(i, carry):
                rows = pl.ds(pl.multiple_of(i * 256, 256), 256)
                c, sa, sb = _rot_tables(pos_ref[rows, :], invf_ref[...], ma_ref[...], mb_ref[...])
                rot_ref[0, rows, :] = c
                rot_ref[1, rows, :] = sa
                rot_ref[2, rows, :] = sb
                return carry

            lax.fori_loop(0, SEQ // 256, tables, 0)

        def finish(i, carry):
            rows = pl.ds(pl.multiple_of(i * 256, 256), 256)
            c, sa, sb = rot_ref[0, rows, :], rot_ref[1, rows, :], rot_ref[2, rows, :]
            dq_ref[rows, :] = _rot_t(dqa_ref[rows, :] * Q_SCALE, c, sa, sb).astype(BF16)
            dk_ref[rows, :] = _rot_t(dka_ref[rows, :], c, sa, sb).astype(BF16)
            dv_ref[rows, :] = dva_ref[rows, :].astype(BF16)
            return carry

        lax.fori_loop(0, SEQ // 256, finish, 0)

    slab = pl.BlockSpec((SEQ, 128), lambda i: (0, i))
    out = jax.ShapeDtypeStruct((SEQ, ATTN_W), BF16)
    acc = pltpu.VMEM((SEQ, 128), F32)
    return _call(
        "attn_bwd", body, ATTN_W // 128,
        [slab] * 6 + [_full_spec((SEQ, 1)), _full_spec((1, 128)), _full_spec((1, 128)), _full_spec((1, 128))],
        [slab] * 3, [out, out, out], (q, k, v, o, lse, do, pos_col, *rot),
        scratch_shapes=[acc, acc, acc, acc, pltpu.VMEM((3, SEQ, 128), F32)])


def in_bwd(dproj_parts, w_in_t, x, g1, dx2, after=()):
    tm = 512
    k = len(dproj_parts)

    def body(*refs):
        w_hbm, x_ref, g_ref, dx2_ref, dx_ref, dg_ref, w_ref, sems = refs[k:]

        def run(wait):
            dh1 = None
            for j in range(k):
                if wait:
                    wait(0, j)
                part = _dot(refs[j][...], w_ref[512 * j:512 * (j + 1), :])
                dh1 = part if dh1 is None else dh1 + part
            dz, dg = _rms_bwd(x_ref[...], g_ref[...], dh1)
            dx_ref[...] = dx2_ref[...] + dz
            if wait:
                dg_ref[...] = dg
            else:
                dg_ref[...] += dg

        _with_weights([w_hbm], [w_ref], sems, IN_SPANS, run)

    assert k == len(IN_SPANS)
    return _call(
        "in_bwd", body, SEQ // tm,
        [_row_spec(tm, 512)] * k + [ANY, _row_spec(tm, D_MODEL), _full_spec((1, D_MODEL)), _row_spec(tm, D_MODEL)],
        [_row_spec(tm, D_MODEL), _full_spec((1, D_MODEL))],
        [jax.ShapeDtypeStruct((SEQ, D_MODEL), F32), jax.ShapeDtypeStruct((1, D_MODEL), F32)],
        (*dproj_parts, w_in_t, x, g1, dx2), scratch_shapes=_weight_scratch((IN_W, D_MODEL), 1, IN_SPANS), after=after)


def _coords():
    return lax.axis_index("x"), lax.axis_index("y"), lax.axis_index("c")


class Exchange:
    def __init__(self, srcs, bufs, new_shapes, n_sems, make):
        self.srcs, self.bufs, self.new_shapes, self.n_sems, self.make = list(srcs), list(bufs), list(new_shapes), n_sems, make


def _call(name, body, n_steps, in_specs, out_specs, out_shape, args, scratch_shapes=(), after=()):
    n_in = len(args)

    def wrapped(*refs):
        body(*refs[:n_in], *refs[n_in + len(after):])

    return list(pl.pallas_call(
        wrapped, name=name, grid=(n_steps,), in_specs=list(in_specs) + [ANY] * len(after), out_specs=list(out_specs),
        out_shape=list(out_shape), scratch_shapes=list(scratch_shapes), compiler_params=_params(),
    )(*args, *after))


GATHER_SEMS = 8


def gather(bufs):
    n = len(bufs)

    def make(src_refs, buf_refs, new_refs, send_sems, recv_sems):
        x, y, c = _coords()
        me, sibling = (x, y, c), (x, y, 1 - c)
        over_x, over_y, across = (1 - x, y), (x, 1 - y), (1 - x, 1 - y)

        def copy(a, k, block, to, half=None):
            r = buf_refs[a].shape[0] // N_DEV
            lo, size = (0, r) if half is None else (half * (r // 2), r // 2)
            rows = buf_refs[a].at[pl.ds((4 * block[0] + 2 * block[1] + block[2]) * r + lo, size), :]
            return pltpu.make_async_remote_copy(
                src_ref=rows, dst_ref=rows, send_sem=send_sems.at[GATHER_SEMS * a + k],
                recv_sem=recv_sems.at[GATHER_SEMS * a + k], device_id=to, device_id_type=MESH)

        every = range(n)
        out = ([copy(a, 0, me, sibling) for a in every] + [copy(a, 1, me, (*over_x, c)) for a in every]
               + [copy(a, 2, me, (*over_y, c)) for a in every])
        near_in = [copy(a, 1, (*over_x, c), me) for a in every] + [copy(a, 2, (*over_y, c), me) for a in every]
        relay = ([copy(a, 3, (*over_x, c), (*over_y, c), half=0) for a in every]
                 + [copy(a, 4, (*over_y, c), (*over_x, c), half=1) for a in every])
        near_on = [copy(a, 5, (*over_x, c), sibling) for a in every] + [copy(a, 6, (*over_y, c), sibling) for a in every]
        relay_in = ([copy(a, 3, (*across, c), me, half=0) for a in every]
                    + [copy(a, 4, (*across, c), me, half=1) for a in every])
        far_on = [copy(a, 7, (*across, c), sibling) for a in every]
        from_core = ([copy(a, 0, sibling, me) for a in every] + [copy(a, 5, (*over_x, 1 - c), me) for a in every]
                     + [copy(a, 6, (*over_y, 1 - c), me) for a in every] + [copy(a, 7, (*across, 1 - c), me) for a in every])
        stages = [([], out), (near_in, relay + near_on), (relay_in, far_on)]
        return stages, out + relay + near_on + far_on, from_core

    return Exchange([], bufs, [], GATHER_SEMS * n, make)


TO_GATHER = (1, lambda x, y, c: [(x, y, 1 - c), (1 - x, y, c), (x, 1 - y, c)])
TO_SIBLING = (2, lambda x, y, c: [(x, y, 1 - c)])
TO_CHIPS = (3, lambda x, y, c: [(1 - x, y, c), (x, 1 - y, c), (1 - x, 1 - y, c)])
TO_ALL = (4, lambda x, y, c: [(x ^ (m >> 2), y ^ ((m >> 1) & 1), c ^ (m & 1)) for m in range(1, N_DEV)])


def by_sequencer(name, exchanges, who):
    collective_id, peers_of = who
    hbm = pltpu.MemorySpace.HBM
    refs = [([jax.new_ref(a, memory_space=hbm) for a in ex.srcs], [jax.new_ref(a, memory_space=hbm) for a in ex.bufs],
             [jax.empty_ref(s, memory_space=hbm) for s in ex.new_shapes]) for ex in exchanges]
    sems = []
    for ex in exchanges:
        sems += [pltpu.SemaphoreType.DMA((ex.n_sems,)), pltpu.SemaphoreType.DMA((ex.n_sems,))]

    @pl.kernel(mesh=plsc.ScalarSubcoreMesh(axis_name="sequencer", num_cores=1), name=name, scratch_types=tuple(sems),
               compiler_params=pltpu.CompilerParams(collective_id=collective_id))
    def launch(*sem_refs):
        peers = peers_of(*_coords())
        barrier = pltpu.get_barrier_semaphore()
        for peer in peers:
            pl.semaphore_signal(barrier, inc=1, device_id=peer, device_id_type=MESH)
        pl.semaphore_wait(barrier, len(peers))

        made = [ex.make(*refs[k], sem_refs[2 * k], sem_refs[2 * k + 1]) for k, ex in enumerate(exchanges)]
        for stage in range(max(len(stages) for stages, _, _ in made)):
            for stages, _, _ in made:
                if stage < len(stages):
                    arrivals, starts = stages[stage]
                    for cp in arrivals:
                        cp.wait_recv()
                    for cp in starts:
                        cp.start()
        for _, sends, arrivals in made:
            for cp in arrivals:
                cp.wait_recv()
            for cp in sends:
                cp.wait_send()

    launch()
    return [([ref[...] for ref in bufs], [ref[...] for ref in news]) for _, bufs, news in refs]


def place_shards(name, shards, dev):
    n = len(shards)

    def body(dev_ref, *refs):
        for a in range(n):
            refs[n + a][...] = refs[a][...].astype(BF16)

    spec = pltpu.PrefetchScalarGridSpec(
        num_scalar_prefetch=1, grid=(1,),
        in_specs=[pl.BlockSpec(s.shape, lambda i, dev_ref: (0, 0)) for s in shards],
        out_specs=[pl.BlockSpec(s.shape, lambda i, dev_ref: (dev_ref[0], 0)) for s in shards])
    return pl.pallas_call(
        body, name=name, grid_spec=spec,
        out_shape=[jax.ShapeDtypeStruct((N_DEV * s.shape[0], s.shape[1]), BF16) for s in shards],
        compiler_params=_params(),
    )(dev, *shards)


def _swap(copies_of):
    def make(src_refs, buf_refs, new_refs, send_sems, recv_sems):
        copies = copies_of(src_refs, new_refs, send_sems, recv_sems)
        return [([], copies)], copies, copies

    return make


def to_sibling(grads):
    def copies_of(src_refs, new_refs, send_sems, recv_sems):
        x, y, c = _coords()
        return [pltpu.make_async_remote_copy(
            src_ref=src_refs[a].at[2 * xy + 1 - c], dst_ref=new_refs[a].at[xy], send_sem=send_sems.at[4 * a + xy],
            recv_sem=recv_sems.at[4 * a + xy], device_id=(x, y, 1 - c), device_id_type=MESH)
            for a in range(len(src_refs)) for xy in range(4)]

    return Exchange(grads, [], [jax.ShapeDtypeStruct((4,) + g.shape[1:], g.dtype) for g in grads], 4 * len(grads),
                    _swap(copies_of))


def to_chips(parts):
    def copies_of(src_refs, new_refs, send_sems, recv_sems):
        x, y, c = _coords()
        chips = [(1 - x, y), (x, 1 - y), (1 - x, 1 - y)]
        return [pltpu.make_async_remote_copy(
            src_ref=src_refs[a].at[2 * px + py], dst_ref=new_refs[a].at[2 * x + y], send_sem=send_sems.at[3 * a + j],
            recv_sem=recv_sems.at[3 * a + j], device_id=(px, py, c), device_id_type=MESH)
            for a in range(len(src_refs)) for j, (px, py) in enumerate(chips)]

    return Exchange(parts, [], [jax.ShapeDtypeStruct(p.shape, p.dtype) for p in parts], 3 * len(parts), _swap(copies_of))


def to_owners(grad):
    def copies_of(src_refs, new_refs, send_sems, recv_sems):
        x, y, c = _coords()
        copies = []
        for m in range(1, N_DEV):
            px, py, pc = x ^ (m >> 2), y ^ ((m >> 1) & 1), c ^ (m & 1)
            copies.append(pltpu.make_async_remote_copy(
                src_ref=src_refs[0].at[4 * px + 2 * py + pc], dst_ref=new_refs[0].at[4 * x + 2 * y + c],
                send_sem=send_sems.at[m - 1], recv_sem=recv_sems.at[m - 1], device_id=(px, py, pc), device_id_type=MESH))
        return copies

    return Exchange([grad], [], [jax.ShapeDtypeStruct(grad.shape, grad.dtype)], N_DEV - 1, _swap(copies_of))


def to_everyone(vec):
    def copies_of(src_refs, new_refs, send_sems, recv_sems):
        x, y, c = _coords()
        copies = []
        for m in range(1, N_DEV):
            px, py, pc = x ^ (m >> 2), y ^ ((m >> 1) & 1), c ^ (m & 1)
            copies.append(pltpu.make_async_remote_copy(
                src_ref=src_refs[0], dst_ref=new_refs[0].at[4 * x + 2 * y + c],
                send_sem=send_sems.at[m - 1], recv_sem=recv_sems.at[m - 1], device_id=(px, py, pc), device_id_type=MESH))
        return copies

    return Exchange([vec], [], [jax.ShapeDtypeStruct((N_DEV,) + vec.shape, vec.dtype)], N_DEV - 1, _swap(copies_of))


def sum_cores(name, grads, others, core, after=()):
    k = len(grads)

    def body(core_ref, *refs):
        for j in range(k):
            out_ref = refs[2 * k + len(after) + j]
            out_ref[...] = (refs[j][:, 0].astype(F32) + refs[k + j][...].astype(F32)).astype(out_ref.dtype)

    mine = [pl.BlockSpec((2, 1) + o.shape[1:], lambda i, core_ref: (i, core_ref[0], 0, 0)) for o in others]
    theirs = [pl.BlockSpec((2,) + o.shape[1:], lambda i, core_ref: (i, 0, 0)) for o in others]
    return pl.pallas_call(
        body, name=name,
        grid_spec=pltpu.PrefetchScalarGridSpec(
            num_scalar_prefetch=1, grid=(2,), in_specs=mine + theirs + [ANY] * len(after), out_specs=theirs),
        out_shape=[jax.ShapeDtypeStruct(o.shape, o.dtype) for o in others],
        compiler_params=_params(),
    )(core, *[g.reshape((4, 2) + g.shape[1:]) for g in grads], *others, *after)


def sum_owned(name, grad, others, dev_ids, after=()):
    _, r, w = grad.shape

    def body(ids_ref, *refs):
        acc = refs[0][0]
        for k in range(1, N_DEV):
            acc = acc + refs[k][0]
        refs[-1][...] = acc

    def pick(k):
        return pl.BlockSpec((1, r, w), lambda i, ids_ref: (ids_ref[k], 0, 0))

    return pl.pallas_call(
        body, name=name,
        grid_spec=pltpu.PrefetchScalarGridSpec(
            num_scalar_prefetch=1, grid=(1,), in_specs=[pick(k) for k in range(N_DEV)] + [ANY] * len(after),
            out_specs=pl.BlockSpec((r, w), lambda i, ids_ref: (ids_ref[0], 0))),
        out_shape=jax.ShapeDtypeStruct((N_DEV * r, w), F32),
        compiler_params=_params(),
    )(dev_ids, grad, *([others] * (N_DEV - 1)), *after)


def _adamw_update(w, g, m, v):
    nm = ADAM_B1 * m + np.float32(1.0 - ADAM_B1) * g
    nv = ADAM_B2 * v + np.float32(1.0 - ADAM_B2) * (g * g)
    m_hat = nm / np.float32(1.0 - ADAM_B1 ** ADAM_STEP)
    v_hat = nv / np.float32(1.0 - ADAM_B2 ** ADAM_STEP)
    return -ADAM_LR * (m_hat / (jnp.sqrt(v_hat) + ADAM_EPS) + ADAM_WD * w), nm, nv


def adamw_of_sums(name, parts, others, chip_ids, ws, ms, vs, after):
    n = len(parts)
    halves = 2

    def body(ids_ref, *refs):
        outs = refs[7 * n + 1:]
        for j in range(n):
            p_ref, a_ref, b_ref, c_ref, w_ref, m_ref, v_ref = refs[7 * j:7 * j + 7]
            g = ((p_ref[0].astype(F32) + a_ref[0].astype(F32)) + b_ref[0].astype(F32)) + c_ref[0].astype(F32)
            outs[4 * j][...] = g
            outs[4 * j + 1][...], outs[4 * j + 2][...], outs[4 * j + 3][...] = _adamw_update(w_ref[...], g, m_ref[...], v_ref[...])

    in_specs, out_specs, out_shape, operands = [], [], [], []
    for part, other, w, m, v in zip(parts, others, ws, ms, vs):
        _, r, wd = part.shape
        rows = r // halves
        whole = pl.BlockSpec((rows, wd), lambda i, ids_ref: (i, 0))
        in_specs += [pl.BlockSpec((1, rows, wd), lambda i, ids_ref, k=k: (ids_ref[k], i, 0)) for k in range(4)] + [whole] * 3
        out_specs += [whole] * 4
        out_shape += [jax.ShapeDtypeStruct((r, wd), F32)] * 4
        operands += [part, other, other, other, w, m, v]
    outs = pl.pallas_call(
        body, name=name,
        grid_spec=pltpu.PrefetchScalarGridSpec(
            num_scalar_prefetch=1, grid=(halves,), in_specs=in_specs + [ANY], out_specs=out_specs),
        out_shape=out_shape,
        compiler_params=_params(),
    )(chip_ids, *operands, after)
    return [tuple(outs[4 * j:4 * j + 4]) for j in range(n)]


def pack_small(parts):
    names = [name for name, _ in SMALL if name in parts]
    operands = [parts[name] for name in names]
    first_row, at = {}, 0
    for name, size in SMALL:
        first_row[name] = at // 128
        at += size
    sizes = dict(SMALL)

    def body(*refs):
        out_ref = refs[-1]
        out_ref[...] = jnp.zeros_like(out_ref)
        for name, ref in zip(names, refs):
            row = first_row[name]
            if name == "loss_sum":
                lane0 = lax.broadcasted_iota(jnp.int32, (1, 128), 1) == 0
                out_ref[row:row + 1, :] = jnp.where(lane0, ref[...], 0.0)
            else:
                rows = sizes[name] // 128
                out_ref[row:row + rows, :] = ref[...].reshape(rows, 128)

    vmem = pl.BlockSpec(memory_space=pltpu.VMEM)
    return pl.pallas_call(
        body, name="pack_small", in_specs=[vmem] * len(names), out_specs=vmem,
        out_shape=jax.ShapeDtypeStruct((SMALL_ROWS, 128), F32), compiler_params=_params(()),
    )(*operands)


LATE = "pre_mix_norm"


def adamw_small(packed_g, late_parts, ws, ms, vs):
    names = [name for name, _ in SMALL if name != "loss_sum"]
    k = len(names)
    shapes = [ws[name].shape[1:] if ws[name].ndim > 2 else ws[name].shape for name in names]
    first_row, at = [], 0
    for name, size in SMALL:
        first_row.append(at // 128)
        at += size

    def body(g_ref, late_ref, *refs):
        w_refs, m_refs, v_refs, outs = refs[:k], refs[k:2 * k], refs[2 * k:3 * k], refs[3 * k:]
        for i, (name, size) in enumerate(SMALL[:k]):
            if name == LATE:
                g = late_ref[0]
                for j in range(1, N_DEV):
                    g = g + late_ref[j]
            else:
                g = g_ref[first_row[i]:first_row[i] + size // 128, :].reshape(shapes[i])
            outs[i][...] = g
            outs[k + i][...], outs[2 * k + i][...], outs[3 * k + i][...] = _adamw_update(
                w_refs[i][...], g, m_refs[i][...], v_refs[i][...])
        outs[4 * k][...] = g_ref[first_row[k]:first_row[k] + 1, 0:1]

    vmem = pl.BlockSpec(memory_space=pltpu.VMEM)
    operands = [t[name].reshape(shape) for t in (ws, ms, vs) for name, shape in zip(names, shapes)]
    outs = pl.pallas_call(
        body, name="adamw_small", in_specs=[vmem] * (2 + 3 * k), out_specs=[vmem] * (4 * k + 1),
        out_shape=[jax.ShapeDtypeStruct(shape, F32) for _ in range(4) for shape in shapes] + [jax.ShapeDtypeStruct((1, 1), F32)],
        compiler_params=_params(()),
    )(packed_g, late_parts, *operands)
    tables = [{name: outs[j * k + i].reshape(ws[name].shape) for i, name in enumerate(names)} for j in range(4)]
    return (*tables, outs[4 * k])


def kernel(x, positions, pre_mix_norm, w_in, sgu_ln_gain, sgu_ln_bias, sgu_w_spatial, sgu_b_spatial, attn_out_norm, sgu_out_norm, w_out, post_mix_norm, pre_ffn_norm, w_gate, w_up, w_down, post_ffn_norm, loss_target, m_pre_mix_norm, m_w_in, m_sgu_ln_gain, m_sgu_ln_bias, m_sgu_w_spatial, m_sgu_b_spatial, m_attn_out_norm, m_sgu_out_norm, m_w_out, m_post_mix_norm, m_pre_ffn_norm, m_w_gate, m_w_up, m_w_down, m_post_ffn_norm, v_pre_mix_norm, v_w_in, v_sgu_ln_gain, v_sgu_ln_bias, v_sgu_w_spatial, v_sgu_b_spatial, v_attn_out_norm, v_sgu_out_norm, v_w_out, v_post_mix_norm, v_pre_ffn_norm, v_w_gate, v_w_up, v_w_down, v_post_ffn_norm):
    small_w = dict(pre_mix_norm=pre_mix_norm, sgu_ln_gain=sgu_ln_gain, sgu_ln_bias=sgu_ln_bias, sgu_w_spatial=sgu_w_spatial,
                   sgu_b_spatial=sgu_b_spatial, attn_out_norm=attn_out_norm, sgu_out_norm=sgu_out_norm,
                   post_mix_norm=post_mix_norm, pre_ffn_norm=pre_ffn_norm, post_ffn_norm=post_ffn_norm)
    small_m = dict(pre_mix_norm=m_pre_mix_norm, sgu_ln_gain=m_sgu_ln_gain, sgu_ln_bias=m_sgu_ln_bias, sgu_w_spatial=m_sgu_w_spatial,
                   sgu_b_spatial=m_sgu_b_spatial, attn_out_norm=m_attn_out_norm, sgu_out_norm=m_sgu_out_norm,
                   post_mix_norm=m_post_mix_norm, pre_ffn_norm=m_pre_ffn_norm, post_ffn_norm=m_post_ffn_norm)
    small_v = dict(pre_mix_norm=v_pre_mix_norm, sgu_ln_gain=v_sgu_ln_gain, sgu_ln_bias=v_sgu_ln_bias, sgu_w_spatial=v_sgu_w_spatial,
                   sgu_b_spatial=v_sgu_b_spatial, attn_out_norm=v_attn_out_norm, sgu_out_norm=v_sgu_out_norm,
                   post_mix_norm=v_post_mix_norm, pre_ffn_norm=v_pre_ffn_norm, post_ffn_norm=v_post_ffn_norm)

    x2d = x[0]
    target = loss_target[0]
    pos_col = positions.reshape(SEQ, 1)
    rot = _rot_consts()
    w_sp = sgu_w_spatial[0]
    bfull = jnp.repeat(sgu_b_spatial[0].T, HEAD_DIM, axis=1)

    x_i, y_i, c_i = (lax.axis_index(a).astype(jnp.int32) for a in MESH_AXES)
    dev = 4 * x_i + 2 * y_i + c_i
    core = c_i.reshape(1)
    chip = 2 * x_i + y_i
    chip_ids = jnp.stack([chip, chip ^ 1, chip ^ 2, chip ^ 3])
    dev_ids = jnp.stack([dev ^ m for m in range(N_DEV)])

    def gathered(name, bufs):
        return by_sequencer(name, [gather(bufs)], TO_GATHER)[0][0]

    def from_sibling(name, grads):
        return by_sequencer(name, [to_sibling(grads)], TO_SIBLING)[0][1]

    def from_chips(name, parts):
        return by_sequencer(name, [to_chips(parts)], TO_CHIPS)[0][1]

    (w_in_t,) = place_shards("place_w_in", [w_in[0].T], dev.reshape(1))
    (w_in_t,) = gathered("gather_w_in", [w_in_t])
    w_gate_t, w_up_t, w_out_f, w_down_f = place_shards(
        "place_weights", [w_gate[0].T, w_up[0].T, w_out[0], w_down[0]], dev.reshape(1))
    (w_out_f,) = gathered("gather_w_out", [w_out_f])
    w_gate_t, w_up_t = gathered("gather_w_gate_up", [w_gate_t, w_up_t])
    (w_down_f,) = gathered("gather_w_down", [w_down_f])

    h1, u, vs, q, k, v = in_proj(x2d, pos_col, pre_mix_norm, w_in_t, rot)
    attn_r, lse, attn = attn_fwd(q, k, v)
    (sgu,) = sgu_fwd(u, vs, sgu_ln_gain, sgu_ln_bias, w_sp, bfull)
    mix, y, x2, h2 = out_proj(attn, sgu, x2d, attn_out_norm, sgu_out_norm, w_out_f, post_mix_norm, pre_ffn_norm)
    gate, up, act = ffn_up(h2, w_gate_t, w_up_t)
    df, dx3, d_post_ffn, sq_err = ffn_down_loss(act, w_down_f, x2, post_ffn_norm, target)

    g_w_down = weight_grad("grad_w_down", act, df)
    (s_down,) = from_sibling("w_down_to_sibling", [g_w_down])
    dgate, dup, dx2, dy, d_pre_ffn, d_post_mix = ffn_bwd(
        df, w_down_f, gate, up, w_gate_t, w_up_t, x2, pre_ffn_norm, dx3, y, post_mix_norm, after=[g_w_down])
    (p_down,) = sum_cores("sum_cores_down", [g_w_down], [s_down], core, after=[dy])
    (c_down,) = from_chips("w_down_to_chips", [p_down])
    g_w_gate, g_w_up = weight_grads("grad_w_gate_up", [dgate, dup], h2, after=[p_down])
    s_gate, s_up = from_sibling("w_gate_up_to_sibling", [g_w_gate, g_w_up])
    g_w_out = weight_grad("grad_w_out", mix, dy, after=[g_w_up, c_down])
    p_gate, p_up = sum_cores("sum_cores_gate_up", [g_w_gate, g_w_up], [s_gate, s_up], core, after=[g_w_out])
    c_gate, c_up = from_chips("w_gate_up_to_chips", [p_gate, p_up])
    (s_out,) = from_sibling("w_out_to_sibling", [g_w_out])
    dsgu, d_attn_out, d_sgu_out, dattn_r = mix_bwd(dy, w_out_f, attn, sgu, attn_out_norm, sgu_out_norm, after=[p_gate, p_up])
    du, dvs, d_ln_gain, d_ln_bias, d_w_sp, d_bfull = sgu_bwd(u, vs, dsgu, sgu_ln_gain, sgu_ln_bias, w_sp, bfull)

    d_b_sp = d_bfull.reshape(CHUNK, N_GROUPS, HEAD_DIM).sum(axis=-1).T
    small_g = pack_small(dict(sgu_ln_gain=d_ln_gain, sgu_ln_bias=d_ln_bias, sgu_w_spatial=d_w_sp, sgu_b_spatial=d_b_sp,
                              attn_out_norm=d_attn_out, sgu_out_norm=d_sgu_out, post_mix_norm=d_post_mix,
                              pre_ffn_norm=d_pre_ffn, post_ffn_norm=d_post_ffn, loss_sum=sq_err))
    small_g = small_g.reshape(N_DEV, SMALL_ROWS // N_DEV, 128)
    ((_, (o_small,)),) = by_sequencer("small_to_owners", [to_owners(small_g)], TO_ALL)

    dq, dk, dv = attn_bwd(q, k, v, attn_r, lse, dattn_r, _to_residue_order(pos_col), rot)
    summed_small = sum_owned("sum_small", small_g, o_small, dev_ids, after=[dq])
    (all_small,) = gathered("gather_small_grads", [summed_small])
    (p_out,) = sum_cores("sum_cores_out", [g_w_out], [s_out], core, after=[dq])
    (c_out,) = from_chips("w_out_to_chips", [p_out])
    dq, dk, dv = (_from_residue_order(t) for t in (dq, dk, dv))
    dproj = [dq, dk, dv, du, dvs]
    g_w_in = weight_grad_of_parts("grad_w_in", dproj, h1, after=[c_gate, c_up])
    (s_in,) = from_sibling("w_in_to_sibling", [g_w_in])
    grad_x, d_pre_mix = in_bwd(dproj, w_in_t, x2d, pre_mix_norm, dx2, after=[g_w_in, all_small])
    (p_in,) = sum_cores("sum_cores_in", [g_w_in], [s_in], core, after=[d_pre_mix, c_out])
    (_, (c_in,)), (_, (late_parts,)) = by_sequencer(
        "last_sums_to_owners", [to_chips([p_in]), to_everyone(d_pre_mix)], TO_ALL)
    late_parts = lax.dynamic_update_slice(late_parts, d_pre_mix[None], (dev, 0, 0))

    def same(t):
        return t

    def turned(t):
        return t.T

    big, last = {}, p_in
    for call, weights in (("adamw_ffn", (("w_down", w_down, p_down, c_down, m_w_down, v_w_down, same),
                                         ("w_gate", w_gate, p_gate, c_gate, m_w_gate, v_w_gate, turned),
                                         ("w_up", w_up, p_up, c_up, m_w_up, v_w_up, turned))),
                          ("adamw_w_out", (("w_out", w_out, p_out, c_out, m_w_out, v_w_out, same),)),
                          ("adamw_w_in", (("w_in", w_in, p_in, c_in, m_w_in, v_w_in, turned),))):
        results = adamw_of_sums(call, [p for _, _, p, _, _, _, _ in weights], [c for _, _, _, c, _, _, _ in weights], chip_ids,
                                [turn(w[0]) for _, w, _, _, _, _, turn in weights], [turn(m[0]) for _, _, _, _, m, _, turn in weights],
                                [turn(vv[0]) for _, _, _, _, _, vv, turn in weights], last)
        for (name, _, _, _, _, _, turn), outs in zip(weights, results):
            big[name] = tuple(turn(t)[None] for t in outs)
        last = results[-1][0]
    sg, sd, snm, snv, loss_sum = adamw_small(all_small, late_parts, small_w, small_m, small_v)
    loss = loss_sum[0, 0] * np.float32(0.5 / D_MODEL)

    names = ["pre_mix_norm", "w_in", "sgu_ln_gain", "sgu_ln_bias", "sgu_w_spatial", "sgu_b_spatial", "attn_out_norm",
             "sgu_out_norm", "w_out", "post_mix_norm", "pre_ffn_norm", "w_gate", "w_up", "w_down", "post_ffn_norm"]
    outs = [loss, grad_x[None]]
    for i, table in enumerate((sg, sd, snm, snv)):
        for name in names:
            outs.append(big[name][i] if name in big else table[name])
    return tuple(outs)
```

```python
import numpy as np
import jax
import jax.numpy as jnp
from jax import lax
from jax.experimental import pallas as pl
from jax.experimental.pallas import tpu as pltpu
from jax.experimental.pallas import tpu_sc as plsc

F32 = jnp.float32
BF16 = jnp.bfloat16

SEQ = 2048
D_MODEL = 1024
ATTN_W = 512
SGU_W = 512
HEAD_DIM = 64
N_GROUPS = 8
CHUNK = 128
D_FF = 2816
IN_W = 3 * ATTN_W + 2 * SGU_W
DILATIONS = (1, 4, 16)
ROPE_THETA = 500000.0
ROT_DIM = 16
ROT_HALF = 8
RMS_EPS = 1e-6
LN_EPS = 1e-5
Q_SCALE = 0.125
NEG = -1e30

N_DEV = 8
MESH_AXES = ("x", "y", "c")
MESH = pl.DeviceIdType.MESH

ADAM_LR = 0.001
ADAM_B1 = 0.9
ADAM_B2 = 0.999
ADAM_EPS = 1e-08
ADAM_WD = 0.01
ADAM_STEP = 10

VMEM_LIMIT = 60 * 1024 * 1024
ANY = pl.BlockSpec(memory_space=pl.ANY)

SMALL = (("pre_mix_norm", 1024), ("sgu_ln_gain", 512), ("sgu_ln_bias", 512), ("sgu_w_spatial", 8 * 128 * 128),
         ("sgu_b_spatial", 1024), ("attn_out_norm", 512), ("sgu_out_norm", 512), ("post_mix_norm", 1024),
         ("pre_ffn_norm", 1024), ("post_ffn_norm", 1024), ("loss_sum", 1))
SMALL_ROWS = 1152


def _params(sem=("arbitrary",)):
    return pltpu.CompilerParams(dimension_semantics=sem, vmem_limit_bytes=VMEM_LIMIT)


def _dot(a, b):
    return jnp.dot(a, b, preferred_element_type=F32)


def _dot_nt(a, b):
    return lax.dot_general(a, b, (((1,), (1,)), ((), ())), preferred_element_type=F32)


def _dot_tn(a, b):
    return lax.dot_general(a, b, (((0,), (0,)), ((), ())), preferred_element_type=F32)


def _rms(z):
    return lax.rsqrt(jnp.mean(z * z, axis=-1, keepdims=True) + RMS_EPS)


def _rms_bwd(z, gain, d):
    r = _rms(z)
    n = z * r
    dn = d * gain
    dz = r * (dn - n * jnp.mean(dn * n, axis=-1, keepdims=True))
    return dz, jnp.sum(d * n, axis=0, keepdims=True)


def _gelu(z):
    return 0.5 * z * (1.0 + lax.erf(z * np.float32(1.0 / np.sqrt(2.0))))


def _gelu_grad(z):
    cdf = 0.5 * (1.0 + lax.erf(z * np.float32(1.0 / np.sqrt(2.0))))
    return cdf + z * jnp.exp(-0.5 * z * z) * np.float32(1.0 / np.sqrt(2.0 * np.pi))


def _rot_tables(pos_col, invf, ma, mb):
    ang = pos_col.astype(F32) * invf
    s = jnp.sin(ang)
    return jnp.cos(ang), s * ma, s * mb


def _rot(t, c, sa, sb):
    return t * c + pltpu.roll(t, 120, 1) * sa + pltpu.roll(t, 8, 1) * sb


def _rot_t(d, c, sa, sb):
    return d * c + pltpu.roll(d * sa, 8, 1) + pltpu.roll(d * sb, 120, 1)


def _rot_consts():
    lane = np.arange(128) % HEAD_DIM
    inv_freq = (np.float32(ROPE_THETA) ** (-np.arange(0, ROT_DIM, 2, dtype=np.float32) / np.float32(ROT_DIM))).astype(np.float32)
    invf = np.where(lane < ROT_DIM, inv_freq[lane % ROT_HALF], 0.0).astype(np.float32)
    ma = np.where(lane < ROT_HALF, -1.0, 0.0).astype(np.float32)
    mb = np.where((lane >= ROT_HALF) & (lane < ROT_DIM), 1.0, 0.0).astype(np.float32)
    return jnp.asarray(invf[None]), jnp.asarray(ma[None]), jnp.asarray(mb[None])


def _row_spec(tm, w):
    return pl.BlockSpec((tm, w), lambda i: (i, 0))


def _full_spec(shape):
    return pl.BlockSpec(shape, lambda i: (0,) * len(shape))


def _weight_spec(shape):
    return pl.BlockSpec(shape, lambda i: (0,) * len(shape), pipeline_mode=pl.Buffered(1))


FF_CHUNKS = (256, 512, 1024, 1024)
FF_SPANS = [(int(o), n) for o, n in zip(np.cumsum((0,) + FF_CHUNKS[:-1]), FF_CHUNKS)]
FF_WHOLE = [(0, D_FF)]


def _ffn_weight_scratch(n_weights, spans):
    return [pltpu.VMEM((D_FF, D_MODEL), BF16)] * n_weights + [pltpu.SemaphoreType.DMA((n_weights, len(spans)))]


def _with_ffn_weights(w_hbm, w_vmem, sems, spans, run):
    copies = [[pltpu.make_async_copy(h.at[pl.ds(o, n)], v.at[pl.ds(o, n)], sems.at[j, c]) for c, (o, n) in enumerate(spans)]
              for j, (h, v) in enumerate(zip(w_hbm, w_vmem))]
    first = pl.program_id(0) == 0

    @pl.when(first)
    def _():
        for of_weight in copies:
            for cp in of_weight:
                cp.start()
        run(lambda j, c: copies[j][c].wait())

    @pl.when(jnp.logical_not(first))
    def _():
        run(None)


RES = 16


def _residue_scratch(n_arrays, tm, width):
    return [pltpu.VMEM((2, n_arrays, tm // RES, RES, width), F32), pltpu.SemaphoreType.DMA((2, n_arrays, RES))]


def _to_residue_rows(tiles, outs, scratch, sems, tm, n_steps):
    i = pl.program_id(0)
    slot = i % 2
    per = tm // RES

    def copies(step, s):
        return [pltpu.make_async_copy(scratch.at[s, a, :, b, :],
                                      outs[a].at[pl.ds(pl.multiple_of(b * (SEQ // RES) + per * step, per), per), :],
                                      sems.at[s, a, b]) for a in range(len(outs)) for b in range(RES)]

    @pl.when(i >= 2)
    def _():
        for cp in copies(i - 2, slot):
            cp.wait()

    for a, tile in enumerate(tiles):
        scratch[slot, a] = tile.reshape(per, RES, tile.shape[-1])
    for cp in copies(i, slot):
        cp.start()

    @pl.when(i == n_steps - 1)
    def _():
        for cp in copies(i - 1, 1 - slot) + copies(i, slot):
            cp.wait()


def in_proj(x, pos_col, g1, w_in_t, rot):
    tm = 512
    n_steps = SEQ // tm

    def body(x_ref, pos_ref, g_ref, w_ref, invf_ref, ma_ref, mb_ref, h_ref, u_ref, vs_ref, q_ref, k_ref, v_ref, scratch, sems):
        xf = x_ref[...]
        h = (xf * _rms(xf) * g_ref[...]).astype(BF16)
        h_ref[...] = h
        proj = _dot_nt(h, w_ref[...])
        c, sa, sb = _rot_tables(pos_ref[...], invf_ref[...], ma_ref[...], mb_ref[...])
        slabs = range(ATTN_W // 128)
        q = jnp.concatenate([_rot(proj[:, j * 128:(j + 1) * 128], c, sa, sb) * Q_SCALE for j in slabs], axis=1)
        k = jnp.concatenate([_rot(proj[:, ATTN_W + j * 128:ATTN_W + (j + 1) * 128], c, sa, sb) for j in slabs], axis=1)
        u_ref[...] = proj[:, 3 * ATTN_W:3 * ATTN_W + SGU_W]
        vs_ref[...] = proj[:, 3 * ATTN_W + SGU_W:]
        _to_residue_rows([q, k, proj[:, 2 * ATTN_W:3 * ATTN_W]], [q_ref, k_ref, v_ref], scratch, sems, tm, n_steps)

    act = jax.ShapeDtypeStruct((SEQ, 512), F32)
    return _call(
        "in_proj", body, n_steps,
        [_row_spec(tm, D_MODEL), _row_spec(tm, 1), _full_spec((1, D_MODEL)), _weight_spec((IN_W, D_MODEL)),
         _full_spec((1, 128)), _full_spec((1, 128)), _full_spec((1, 128))],
        [_row_spec(tm, D_MODEL)] + [_row_spec(tm, 512)] * 2 + [ANY] * 3,
        [jax.ShapeDtypeStruct((SEQ, D_MODEL), BF16)] + [act] * 5,
        (x, pos_col, g1, w_in_t, *rot), scratch_shapes=_residue_scratch(3, tm, ATTN_W))


def _to_residue_order(t):
    return t.reshape(SEQ // RES, RES, -1).transpose(1, 0, 2).reshape(t.shape)


def _block_rows(d, r, n):
    if d == 16:
        slices = [(128 * r, 128)]
    elif d == 4:
        slices = [(128 * (4 * b + r) + 32 * n, 32) for b in range(4)]
    else:
        slices = [(128 * b + 8 * n, 8) for b in range(RES)]
    return [(s if isinstance(s, int) else pl.multiple_of(s, z), z) for s, z in slices]


def _block_step(d, i):
    if d == 16:
        return i
    if d == 4:
        return 4 * (i & 31) + (i >> 5)
    return 16 * (i & 7) + (i >> 3)


def _attn_masks(d):
    row2 = _block_step(d, lax.broadcasted_iota(jnp.int32, (128, 256), 0))
    col2 = lax.broadcasted_iota(jnp.int32, (128, 256), 1)
    key2 = _block_step(d, col2 & 127)
    mask2 = jnp.logical_or(jnp.logical_and(col2 < 128, key2 >= row2), jnp.logical_and(col2 >= 128, key2 <= row2))
    row1 = _block_step(d, lax.broadcasted_iota(jnp.int32, (128, 128), 0))
    col1 = lax.broadcasted_iota(jnp.int32, (128, 128), 1)
    return col1 < HEAD_DIM, _block_step(d, col1) <= row1, mask2


def _load_rows(ref, slices):
    parts = [ref[pl.ds(s, z), :] for s, z in slices]
    return parts[0] if len(parts) == 1 else jnp.concatenate(parts, axis=0)


def _for_each_group(fn):
    for p, d in enumerate(DILATIONS):
        masks = _attn_masks(d)
        if d == 16:
            def group(i, carry, p=p, masks=masks):
                fn(p, masks, [(_block_rows(16, 8 * i + g, 0), None) for g in range(8)])
                return carry

            lax.fori_loop(0, 2, group, 0)
        elif d == 4:
            fn(p, masks, [(_block_rows(4, r, 0), None) for r in range(4)])

            def group(i, carry, p=p, masks=masks):
                blocks = [6 * i + g for g in range(6)]
                fn(p, masks, [(_block_rows(4, j % 4, 1 + j // 4), _block_rows(4, j % 4, j // 4)) for j in blocks])
                return carry

            lax.fori_loop(0, 2, group, 0)
        else:
            fn(p, masks, [(_block_rows(1, 0, 0), None)])

            def group(i, carry, p=p, masks=masks):
                fn(p, masks, [(_block_rows(1, 0, 5 * i + g + 1), _block_rows(1, 0, 5 * i + g)) for g in range(5)])
                return carry

            lax.fori_loop(0, 3, group, 0)


def attn_fwd(q, k, v):
    def body(q_ref, k_ref, v_ref, o_ref, lse_ref, nat_ref, op_ref, lp_ref, sems):
        def group(p, masks, blocks):
            head0, mask1, mask2 = masks
            heads = (head0, jnp.logical_not(head0))
            keys = [rows if prev is None else prev + rows for rows, prev in blocks]
            mask = [mask1 if prev is None else mask2 for _, prev in blocks]
            qb = [_load_rows(q_ref, rows) for rows, _ in blocks]
            kk = [_load_rows(k_ref, ks).astype(BF16) for ks in keys]
            vv = [_load_rows(v_ref, ks).astype(BF16) for ks in keys]
            chains = [(g, hm) for g in range(len(blocks)) for hm in heads]
            s = [jnp.where(mask[g], _dot_nt(jnp.where(hm, qb[g], 0.0).astype(BF16), kk[g]), NEG) for g, hm in chains]
            m = [jnp.max(t, axis=-1, keepdims=True) for t in s]
            e = [jnp.exp(t - mt) for t, mt in zip(s, m)]
            l = [jnp.sum(t, axis=-1, keepdims=True) for t in e]
            pv = [_dot(t.astype(BF16), vv[g]) for t, (g, _) in zip(e, chains)]
            for g, (rows, _) in enumerate(blocks):
                o_blk = jnp.where(head0, pv[2 * g] / l[2 * g], pv[2 * g + 1] / l[2 * g + 1])
                l_blk = jnp.where(head0, jnp.broadcast_to(m[2 * g] + jnp.log(l[2 * g]), (128, 128)),
                                  jnp.broadcast_to(m[2 * g + 1] + jnp.log(l[2 * g + 1]), (128, 128)))
                at = 0
                for start, size in rows:
                    op_ref[p, pl.ds(start, size), :] = o_blk[at:at + size]
                    lp_ref[p, pl.ds(start, size), :] = l_blk[at:at + size]
                    at += size

        _for_each_group(group)

        def combine(i, carry):
            rows = pl.ds(pl.multiple_of(i * 256, 256), 256)
            ls = [lp_ref[p, rows, :] for p in range(3)]
            m = jnp.maximum(jnp.maximum(ls[0], ls[1]), ls[2])
            lse = m + jnp.log(jnp.exp(ls[0] - m) + jnp.exp(ls[1] - m) + jnp.exp(ls[2] - m))
            o = jnp.zeros((256, 128), F32)
            for p in range(3):
                o = o + jnp.exp(ls[p] - lse) * op_ref[p, rows, :]
            o_ref[rows, :] = o
            lse_ref[rows, :] = lse
            return carry

        lax.fori_loop(0, SEQ // 256, combine, 0)

        lanes = pl.ds(pl.multiple_of(pl.program_id(0) * 128, 128), 128)
        back = [pltpu.make_async_copy(o_ref.at[pl.ds(b * (SEQ // RES), SEQ // RES), :], nat_ref.at[:, b, lanes], sems.at[b])
                for b in range(RES)]
        for cp in back:
            cp.start()
        for cp in back:
            cp.wait()

    slab = pl.BlockSpec((SEQ, 128), lambda i: (0, i))
    out = jax.ShapeDtypeStruct((SEQ, ATTN_W), F32)
    attn_r, lse, attn = _call(
        "attn_fwd", body, ATTN_W // 128, [slab] * 3, [slab] * 2 + [ANY],
        [out, out, jax.ShapeDtypeStruct((SEQ // RES, RES, ATTN_W), F32)], (q, k, v),
        scratch_shapes=[pltpu.VMEM((3, SEQ, 128), F32), pltpu.VMEM((3, SEQ, 128), F32), pltpu.SemaphoreType.DMA((RES,))])
    return attn_r, lse, attn.reshape(SEQ, ATTN_W)


def _causal_weights(w_ref):
    row = lax.broadcasted_iota(jnp.int32, (CHUNK, CHUNK), 0)
    col = lax.broadcasted_iota(jnp.int32, (CHUNK, CHUNK), 1)
    return [jnp.where(col <= row, w_ref[g], 0.0).astype(BF16) for g in range(N_GROUPS)], col <= row


def _sgu_chunk_fwd(u, vs, lg, lb, wc, bfull, head0):
    ug = _gelu(u)
    vg = _gelu(vs)
    xc = vg - jnp.mean(vg, axis=-1, keepdims=True)
    rstd = lax.rsqrt(jnp.mean(xc * xc, axis=-1, keepdims=True) + LN_EPS)
    xhat = xc * rstd
    vn = xhat * lg + lb
    mixed = []
    for gp in range(SGU_W // 128):
        vp = vn[:, gp * 128:(gp + 1) * 128].astype(BF16)
        mixed.append(jnp.where(head0, _dot(wc[2 * gp], vp), _dot(wc[2 * gp + 1], vp)))
    ms = jnp.concatenate(mixed, axis=1) + bfull
    return ug, xhat, rstd, vn, ms


def sgu_fwd(u, vs, lg, lb, w_sp, bfull):
    cpb = 4

    def body(u_ref, vs_ref, lg_ref, lb_ref, w_ref, b_ref, o_ref):
        wc, _ = _causal_weights(w_ref)
        head0 = lax.broadcasted_iota(jnp.int32, (CHUNK, 128), 1) < HEAD_DIM
        for ci in range(cpb):
            rows = pl.ds(ci * CHUNK, CHUNK)
            ug, _, _, _, ms = _sgu_chunk_fwd(u_ref[rows, :], vs_ref[rows, :], lg_ref[...], lb_ref[...], wc, b_ref[...], head0)
            o_ref[rows, :] = ug * ms

    tm = cpb * CHUNK
    return _call(
        "sgu_fwd", body, SEQ // tm,
        [_row_spec(tm, SGU_W), _row_spec(tm, SGU_W), _full_spec((1, SGU_W)), _full_spec((1, SGU_W)),
         _full_spec((N_GROUPS, CHUNK, CHUNK)), _full_spec((CHUNK, SGU_W))],
        [_row_spec(tm, SGU_W)], [jax.ShapeDtypeStruct((SEQ, SGU_W), F32)],
        (u, vs, lg, lb, w_sp, bfull))


def out_proj(attn, sgu, x, ga, gs, w_out, gpm, gpf):
    tm = 512

    def body(a_ref, s_ref, x_ref, ga_ref, gs_ref, w_ref, gpm_ref, gpf_ref, mix_ref, y_ref, x2_ref, h2_ref):
        a = a_ref[...]
        s = s_ref[...]
        an = (a * _rms(a) * ga_ref[...]).astype(BF16)
        sn = (s * _rms(s) * gs_ref[...]).astype(BF16)
        mix_ref[:, :ATTN_W] = an
        mix_ref[:, ATTN_W:] = sn
        y = _dot(an, w_ref[:ATTN_W, :]) + _dot(sn, w_ref[ATTN_W:, :])
        y_ref[...] = y
        x2 = x_ref[...] + y * _rms(y) * gpm_ref[...]
        x2_ref[...] = x2
        h2_ref[...] = (x2 * _rms(x2) * gpf_ref[...]).astype(BF16)

    wide = jax.ShapeDtypeStruct((SEQ, D_MODEL), F32)
    wide16 = jax.ShapeDtypeStruct((SEQ, D_MODEL), BF16)
    return _call(
        "out_proj", body, SEQ // tm,
        [_row_spec(tm, ATTN_W), _row_spec(tm, SGU_W), _row_spec(tm, D_MODEL), _full_spec((1, ATTN_W)),
         _full_spec((1, SGU_W)), _weight_spec((D_MODEL, D_MODEL)), _full_spec((1, D_MODEL)), _full_spec((1, D_MODEL))],
        [_row_spec(tm, D_MODEL)] * 4, [wide16, wide, wide, wide16],
        (attn, sgu, x, ga, gs, w_out, gpm, gpf))


def ffn_up(h2, w_gate_t, w_up_t):
    tm = 256

    def body(h_ref, wg_hbm, wu_hbm, g_ref, u_ref, a_ref, wg_ref, wu_ref, sems):
        def run(wait):
            h = h_ref[...]
            if wait:
                wait(0, 0)
            g = _dot_nt(h, wg_ref[...])
            g_ref[...] = g.astype(BF16)
            if wait:
                wait(1, 0)
            u = _dot_nt(h, wu_ref[...])
            u_ref[...] = u.astype(BF16)
            a_ref[...] = (g * jax.nn.sigmoid(g) * u).astype(BF16)

        _with_ffn_weights([wg_hbm, wu_hbm], [wg_ref, wu_ref], sems, FF_WHOLE, run)

    ff = jax.ShapeDtypeStruct((SEQ, D_FF), BF16)
    return _call(
        "ffn_up", body, SEQ // tm, [_row_spec(tm, D_MODEL), ANY, ANY],
        [_row_spec(tm, D_FF)] * 3, [ff, ff, jax.ShapeDtypeStruct((SEQ, D_FF), BF16)],
        (h2, w_gate_t, w_up_t), scratch_shapes=_ffn_weight_scratch(2, FF_WHOLE))


def ffn_down_loss(act, w_down, x2, gpo, target):
    tm = 512

    def body(a_ref, w_hbm, x2_ref, g_ref, t_ref, df_ref, dx3_ref, dg_ref, loss_ref, w_ref, sems):
        def run(wait):
            f = None
            for c, (o, n) in enumerate(FF_SPANS if wait else FF_WHOLE):
                if wait:
                    wait(0, c)
                part = _dot(a_ref[:, o:o + n], w_ref[o:o + n, :])
                f = part if f is None else f + part
            gain = g_ref[...]
            err = x2_ref[...] + f * _rms(f) * gain - t_ref[...]
            dx3 = err * np.float32(1.0 / D_MODEL)
            dx3_ref[...] = dx3
            df, dg = _rms_bwd(f, gain, dx3)
            df_ref[...] = df.astype(BF16)
            loss = jnp.sum(err * err, axis=(0, 1), keepdims=True)
            if wait:
                dg_ref[...] = dg
                loss_ref[...] = loss
            else:
                dg_ref[...] += dg
                loss_ref[...] += loss

        _with_ffn_weights([w_hbm], [w_ref], sems, FF_SPANS, run)

    return _call(
        "ffn_down_loss", body, SEQ // tm,
        [_row_spec(tm, D_FF), ANY, _row_spec(tm, D_MODEL), _full_spec((1, D_MODEL)), _row_spec(tm, D_MODEL)],
        [_row_spec(tm, D_MODEL), _row_spec(tm, D_MODEL), _full_spec((1, D_MODEL)), _full_spec((1, 1))],
        [jax.ShapeDtypeStruct((SEQ, D_MODEL), BF16), jax.ShapeDtypeStruct((SEQ, D_MODEL), F32),
         jax.ShapeDtypeStruct((1, D_MODEL), F32), jax.ShapeDtypeStruct((1, 1), F32)],
        (act, w_down, x2, gpo, target), scratch_shapes=_ffn_weight_scratch(1, FF_SPANS))


def ffn_bwd(df, w_down, gate, up, w_gate_t, w_up_t, x2, gpf, dx3, y, gpm, after=()):
    tm = 256

    def body(df_ref, wd_hbm, g_ref, u_ref, wg_hbm, wu_hbm, x2_ref, gpf_ref, dx3_ref, y_ref, gpm_ref,
             dg_ref, du_ref, dx2_ref, dy_ref, dgpf_ref, dgpm_ref, wd_ref, wg_ref, wu_ref, sems):
        def run(wait):
            if wait:
                wait(0, 0)
            dact = _dot_nt(df_ref[...], wd_ref[...])
            g = g_ref[...].astype(F32)
            s = jax.nn.sigmoid(g)
            dup = (dact * g * s).astype(BF16)
            dgate = (dact * u_ref[...].astype(F32) * (s * (1.0 + g * (1.0 - s)))).astype(BF16)
            du_ref[...] = dup
            dg_ref[...] = dgate
            if wait:
                wait(1, 0)
                wait(2, 0)
            dh2 = _dot(dgate, wg_ref[...]) + _dot(dup, wu_ref[...])
            dz, dgpf = _rms_bwd(x2_ref[...], gpf_ref[...], dh2)
            dx2 = dx3_ref[...] + dz
            dx2_ref[...] = dx2
            dy, dgpm = _rms_bwd(y_ref[...], gpm_ref[...], dx2)
            dy_ref[...] = dy.astype(BF16)
            if wait:
                dgpf_ref[...] = dgpf
                dgpm_ref[...] = dgpm
            else:
                dgpf_ref[...] += dgpf
                dgpm_ref[...] += dgpm

        _with_ffn_weights([wd_hbm, wg_hbm, wu_hbm], [wd_ref, wg_ref, wu_ref], sems, FF_WHOLE, run)

    vec = jax.ShapeDtypeStruct((1, D_MODEL), F32)
    ff16 = jax.ShapeDtypeStruct((SEQ, D_FF), BF16)
    return _call(
        "ffn_bwd", body, SEQ // tm,
        [_row_spec(tm, D_MODEL), ANY, _row_spec(tm, D_FF), _row_spec(tm, D_FF), ANY, ANY, _row_spec(tm, D_MODEL),
         _full_spec((1, D_MODEL)), _row_spec(tm, D_MODEL), _row_spec(tm, D_MODEL), _full_spec((1, D_MODEL))],
        [_row_spec(tm, D_FF), _row_spec(tm, D_FF), _row_spec(tm, D_MODEL), _row_spec(tm, D_MODEL),
         _full_spec((1, D_MODEL)), _full_spec((1, D_MODEL))],
        [ff16, ff16, jax.ShapeDtypeStruct((SEQ, D_MODEL), F32), jax.ShapeDtypeStruct((SEQ, D_MODEL), BF16), vec, vec],
        (df, w_down, gate, up, w_gate_t, w_up_t, x2, gpf, dx3, y, gpm), scratch_shapes=_ffn_weight_scratch(3, FF_WHOLE), after=after)


def weight_grads(name, lhs, b, after=()):
    m, n, k = lhs[0].shape[1], b.shape[1], len(lhs)
    tr = 256

    def body(*refs):
        for a_ref, o_ref in zip(refs[:k], refs[k + 1:]):
            o_ref[...] = _dot_tn(a_ref[...], refs[k][...]).astype(BF16)

    outs = _call(
        name, body, m // tr, [pl.BlockSpec((SEQ, tr), lambda i: (0, i))] * k + [_weight_spec((SEQ, n))],
        [_row_spec(tr, n)] * k, [jax.ShapeDtypeStruct((m, n), BF16)] * k, (*lhs, b), after=after)
    return [out.reshape(N_DEV, m // N_DEV, n) for out in outs]


def weight_grad(name, a, b, after=()):
    return weight_grads(name, [a], b, after)[0]


def weight_grad_of_parts(name, parts, b, after=()):
    p, n, k = parts[0].shape[1], b.shape[1], len(parts)
    tr = 512
    per = p // tr

    def body(*refs):
        tile = pl.program_id(0)
        for j in range(k):
            @pl.when(tile // per == j)
            def _(j=j):
                refs[k + 1][...] = _dot_tn(refs[j][...].astype(BF16), refs[k][...]).astype(BF16)

    def part_spec(j):
        return pl.BlockSpec((SEQ, tr), lambda i: (0, jnp.clip(i - per * j, 0, per - 1)))

    (out,) = _call(
        name, body, k * per, [part_spec(j) for j in range(k)] + [_weight_spec((SEQ, n))],
        [_row_spec(tr, n)], [jax.ShapeDtypeStruct((k * p, n), BF16)], (*parts, b), after=after)
    return out.reshape(N_DEV, k * p // N_DEV, n)


def mix_bwd(dy, w_out, attn, sgu, ga, gs, after=()):
    tm = 512
    n_steps = SEQ // tm

    def body(dy_ref, w_ref, a_ref, s_ref, ga_ref, gs_ref, ds_ref, dga_ref, dgs_ref, da_ref, scratch, sems):
        dy = dy_ref[...]
        da, dga = _rms_bwd(a_ref[...], ga_ref[...], _dot_nt(dy, w_ref[:ATTN_W, :]))
        ds, dgs = _rms_bwd(s_ref[...], gs_ref[...], _dot_nt(dy, w_ref[ATTN_W:, :]))
        ds_ref[...] = ds
        _to_residue_rows([da], [da_ref], scratch, sems, tm, n_steps)

        @pl.when(pl.program_id(0) == 0)
        def _():
            dga_ref[...] = jnp.zeros_like(dga_ref)
            dgs_ref[...] = jnp.zeros_like(dgs_ref)

        dga_ref[...] += dga
        dgs_ref[...] += dgs

    half = jax.ShapeDtypeStruct((SEQ, 512), F32)
    vec = jax.ShapeDtypeStruct((1, 512), F32)
    return _call(
        "mix_bwd", body, n_steps,
        [_row_spec(tm, D_MODEL), _weight_spec((D_MODEL, D_MODEL)), _row_spec(tm, 512), _row_spec(tm, 512),
         _full_spec((1, 512)), _full_spec((1, 512))],
        [_row_spec(tm, 512), _full_spec((1, 512)), _full_spec((1, 512)), ANY],
        [half, vec, vec, half], (dy, w_out, attn, sgu, ga, gs), scratch_shapes=_residue_scratch(1, tm, ATTN_W), after=after)


def sgu_bwd(u, vs, dsgu, lg, lb, w_sp, bfull):
    cpb = 4

    def body(u_ref, vs_ref, d_ref, lg_ref, lb_ref, w_ref, b_ref, du_ref, dvs_ref, dlg_ref, dlb_ref, dw_ref, db_ref):
        wc, causal = _causal_weights(w_ref)
        head0 = lax.broadcasted_iota(jnp.int32, (CHUNK, 128), 1) < HEAD_DIM
        lg = lg_ref[...]

        @pl.when(pl.program_id(0) == 0)
        def _():
            dlg_ref[...] = jnp.zeros_like(dlg_ref)
            dlb_ref[...] = jnp.zeros_like(dlb_ref)
            dw_ref[...] = jnp.zeros_like(dw_ref)
            db_ref[...] = jnp.zeros_like(db_ref)

        for ci in range(cpb):
            rows = pl.ds(ci * CHUNK, CHUNK)
            u = u_ref[rows, :]
            vs = vs_ref[rows, :]
            d = d_ref[rows, :]
            ug, xhat, rstd, vn, ms = _sgu_chunk_fwd(u, vs, lg, lb_ref[...], wc, b_ref[...], head0)
            du_ref[rows, :] = (d * ms * _gelu_grad(u)).astype(BF16)
            dms = d * ug
            db_ref[...] += dms
            dvn = []
            for gp in range(SGU_W // 128):
                dmp = dms[:, gp * 128:(gp + 1) * 128]
                dm0 = jnp.where(head0, dmp, 0.0).astype(BF16)
                dm1 = jnp.where(head0, 0.0, dmp).astype(BF16)
                vp = vn[:, gp * 128:(gp + 1) * 128].astype(BF16)
                dw_ref[2 * gp] += _dot_nt(dm0, vp)
                dw_ref[2 * gp + 1] += _dot_nt(dm1, vp)
                dvn.append(_dot_tn(wc[2 * gp], dm0) + _dot_tn(wc[2 * gp + 1], dm1))
            dvn = jnp.concatenate(dvn, axis=1)
            dlg_ref[...] += jnp.sum(dvn * xhat, axis=0, keepdims=True)
            dlb_ref[...] += jnp.sum(dvn, axis=0, keepdims=True)
            dxh = dvn * lg
            dvg = rstd * (dxh - jnp.mean(dxh, axis=-1, keepdims=True) - xhat * jnp.mean(dxh * xhat, axis=-1, keepdims=True))
            dvs_ref[rows, :] = (dvg * _gelu_grad(vs)).astype(BF16)

        @pl.when(pl.program_id(0) == pl.num_programs(0) - 1)
        def _():
            for g in range(N_GROUPS):
                dw_ref[g] = jnp.where(causal, dw_ref[g], 0.0)

    tm = cpb * CHUNK
    half16 = jax.ShapeDtypeStruct((SEQ, SGU_W), BF16)
    vec = jax.ShapeDtypeStruct((1, SGU_W), F32)
    return _call(
        "sgu_bwd", body, SEQ // tm,
        [_row_spec(tm, SGU_W)] * 3 + [_full_spec((1, SGU_W)), _full_spec((1, SGU_W)),
                                      _full_spec((N_GROUPS, CHUNK, CHUNK)), _full_spec((CHUNK, SGU_W))],
        [_row_spec(tm, SGU_W), _row_spec(tm, SGU_W), _full_spec((1, SGU_W)), _full_spec((1, SGU_W)),
         _full_spec((N_GROUPS, CHUNK, CHUNK)), _full_spec((CHUNK, SGU_W))],
        [half16, half16, vec, vec, jax.ShapeDtypeStruct((N_GROUPS, CHUNK, CHUNK), F32),
         jax.ShapeDtypeStruct((CHUNK, SGU_W), F32)],
        (u, vs, dsgu, lg, lb, w_sp, bfull))


def attn_bwd(q, k, v, o, lse, do, pos_col, rot):
    n_steps = ATTN_W // 128

    def body(q_ref, k_ref, v_ref, o_ref, lse_ref, do_ref, pos_ref, invf_ref, ma_ref, mb_ref,
             dq_ref, dk_ref, dv_ref, dqa_ref, dka_ref, dva_ref, dlt_ref, rot_ref, out_ref, sems):
        step = pl.program_id(0)
        slot = step % 2

        def back(at_step, s):
            lanes = pl.ds(pl.multiple_of(at_step * 128, 128), 128)
            return [pltpu.make_async_copy(out_ref.at[s, a, pl.ds(b * (SEQ // RES), SEQ // RES), :], nat.at[:, b, lanes],
                                          sems.at[s, a, b]) for a, nat in enumerate((dq_ref, dk_ref, dv_ref)) for b in range(RES)]

        dqa_ref[...] = jnp.zeros_like(dqa_ref)
        dka_ref[...] = jnp.zeros_like(dka_ref)
        dva_ref[...] = jnp.zeros_like(dva_ref)

        def delta(i, carry):
            rows = pl.ds(pl.multiple_of(i * 256, 256), 256)
            prod = do_ref[rows, :] * o_ref[rows, :]
            h0 = lax.broadcasted_iota(jnp.int32, (256, 128), 1) < HEAD_DIM
            d0 = jnp.sum(jnp.where(h0, prod, 0.0), axis=-1, keepdims=True)
            d1 = jnp.sum(jnp.where(h0, 0.0, prod), axis=-1, keepdims=True)
            dlt_ref[rows, :] = jnp.where(h0, d0, d1)
            return carry

        lax.fori_loop(0, SEQ // 256, delta, 0)

        def add_rows(ref, slices, val):
            at = 0
            for start, size in slices:
                ref[pl.ds(start, size), :] += val[at:at + size]
                at += size

        def group(p, masks, blocks):
            head0, mask1, mask2 = masks
            heads = (head0, jnp.logical_not(head0))
            keys = [rows if prev is None else prev + rows for rows, prev in blocks]
            mask = [mask1 if prev is None else mask2 for _, prev in blocks]
            kk = [_load_rows(k_ref, ks).astype(BF16) for ks in keys]
            vv = [_load_rows(v_ref, ks).astype(BF16) for ks in keys]
            qb = [_load_rows(q_ref, rows) for rows, _ in blocks]
            dob = [_load_rows(do_ref, rows) for rows, _ in blocks]
            lse_b = [_load_rows(lse_ref, rows) for rows, _ in blocks]
            dlt_b = [_load_rows(dlt_ref, rows) for rows, _ in blocks]
            chains = [(g, h) for g in range(len(blocks)) for h in range(2)]
            qm = [jnp.where(heads[h], qb[g], 0.0).astype(BF16) for g, h in chains]
            dom = [jnp.where(heads[h], dob[g], 0.0).astype(BF16) for g, h in chains]
            s = [_dot_nt(qm[c], kk[g]) for c, (g, h) in enumerate(chains)]
            dp = [_dot_nt(dom[c], vv[g]) for c, (g, h) in enumerate(chains)]
            pr = [jnp.where(mask[g], jnp.exp(s[c] - lse_b[g][:, h * HEAD_DIM:h * HEAD_DIM + 1]), 0.0)
                  for c, (g, h) in enumerate(chains)]
            ds = [(pr[c] * (dp[c] - dlt_b[g][:, h * HEAD_DIM:h * HEAD_DIM + 1])).astype(BF16)
                  for c, (g, h) in enumerate(chains)]
            dv = [_dot_tn(pr[c].astype(BF16), dom[c]) for c in range(len(chains))]
            dk = [_dot_tn(ds[c], qm[c]) for c in range(len(chains))]
            dq = [_dot(ds[c], kk[g]) for c, (g, h) in enumerate(chains)]
            for g, (rows, _) in enumerate(blocks):
                add_rows(dqa_ref, rows, jnp.where(head0, dq[2 * g], dq[2 * g + 1]))
                add_rows(dka_ref, keys[g], dk[2 * g] + dk[2 * g + 1])
                add_rows(dva_ref, keys[g], dv[2 * g] + dv[2 * g + 1])

        _for_each_group(group)

        @pl.when(pl.program_id(0) == 0)
        def _():
            def tables(i, carry):
                rows = pl.ds(pl.multiple_of(i * 256, 256), 256)
                c, sa, sb = _rot_tables(pos_ref[rows, :], invf_ref[...], ma_ref[...], mb_ref[...])
                rot_ref[0, rows, :] = c
                rot_ref[1, rows, :] = sa
                rot_ref[2, rows, :] = sb
                return carry

            lax.fori_loop(0, SEQ // 256, tables, 0)

        @pl.when(step >= 2)
        def _():
            for cp in back(step - 2, slot):
                cp.wait()

        def finish(i, carry):
            rows = pl.ds(pl.multiple_of(i * 256, 256), 256)
            c, sa, sb = rot_ref[0, rows, :], rot_ref[1, rows, :], rot_ref[2, rows, :]
            out_ref[slot, 0, rows, :] = _rot_t(dqa_ref[rows, :] * Q_SCALE, c, sa, sb)
            out_ref[slot, 1, rows, :] = _rot_t(dka_ref[rows, :], c, sa, sb)
            out_ref[slot, 2, rows, :] = dva_ref[rows, :]
            return carry

        lax.fori_loop(0, SEQ // 256, finish, 0)
        for cp in back(step, slot):
            cp.start()

        @pl.when(step == n_steps - 1)
        def _():
            for cp in back(step - 1, 1 - slot) + back(step, slot):
                cp.wait()

    slab = pl.BlockSpec((SEQ, 128), lambda i: (0, i))
    out = jax.ShapeDtypeStruct((SEQ // RES, RES, ATTN_W), F32)
    acc = pltpu.VMEM((SEQ, 128), F32)
    outs = _call(
        "attn_bwd", body, n_steps,
        [slab] * 6 + [_full_spec((SEQ, 1)), _full_spec((1, 128)), _full_spec((1, 128)), _full_spec((1, 128))],
        [ANY] * 3, [out, out, out], (q, k, v, o, lse, do, pos_col, *rot),
        scratch_shapes=[acc, acc, acc, acc, pltpu.VMEM((3, SEQ, 128), F32), pltpu.VMEM((2, 3, SEQ, 128), F32),
                        pltpu.SemaphoreType.DMA((2, 3, RES))])
    return [t.reshape(SEQ, ATTN_W) for t in outs]


def in_bwd(dproj_parts, w_in_t, x, g1, dx2, after=()):
    tm = 512
    k = len(dproj_parts)

    def body(*refs):
        w_ref, x_ref, g_ref, dx2_ref, dx_ref, dg_ref = refs[k:]
        dh1 = _dot(refs[0][...].astype(BF16), w_ref[0:512, :])
        for j in range(1, k):
            dh1 = dh1 + _dot(refs[j][...].astype(BF16), w_ref[512 * j:512 * (j + 1), :])
        dz, dg = _rms_bwd(x_ref[...], g_ref[...], dh1)
        dx_ref[...] = dx2_ref[...] + dz

        @pl.when(pl.program_id(0) == 0)
        def _():
            dg_ref[...] = jnp.zeros_like(dg_ref)

        dg_ref[...] += dg

    return _call(
        "in_bwd", body, SEQ // tm,
        [_row_spec(tm, 512)] * k + [_weight_spec((IN_W, D_MODEL)), _row_spec(tm, D_MODEL), _full_spec((1, D_MODEL)),
                                    _row_spec(tm, D_MODEL)],
        [_row_spec(tm, D_MODEL), _full_spec((1, D_MODEL))],
        [jax.ShapeDtypeStruct((SEQ, D_MODEL), F32), jax.ShapeDtypeStruct((1, D_MODEL), F32)],
        (*dproj_parts, w_in_t, x, g1, dx2), after=after)


def _coords():
    return lax.axis_index("x"), lax.axis_index("y"), lax.axis_index("c")


class Exchange:
    def __init__(self, srcs, bufs, new_shapes, n_sems, make):
        self.srcs, self.bufs, self.new_shapes, self.n_sems, self.make = list(srcs), list(bufs), list(new_shapes), n_sems, make


def _call(name, body, n_steps, in_specs, out_specs, out_shape, args, scratch_shapes=(), after=()):
    n_in = len(args)

    def wrapped(*refs):
        body(*refs[:n_in], *refs[n_in + len(after):])

    return list(pl.pallas_call(
        wrapped, name=name, grid=(n_steps,), in_specs=list(in_specs) + [ANY] * len(after), out_specs=list(out_specs),
        out_shape=list(out_shape), scratch_shapes=list(scratch_shapes), compiler_params=_params(),
    )(*args, *after))


GATHER_SEMS = 8


def gather(bufs):
    n = len(bufs)

    def make(src_refs, buf_refs, new_refs, send_sems, recv_sems):
        x, y, c = _coords()
        me, sibling = (x, y, c), (x, y, 1 - c)
        over_x, over_y, across = (1 - x, y), (x, 1 - y), (1 - x, 1 - y)

        def copy(a, k, block, to, half=None):
            r = buf_refs[a].shape[0] // N_DEV
            lo, size = (0, r) if half is None else (half * (r // 2), r // 2)
            rows = buf_refs[a].at[pl.ds((4 * block[0] + 2 * block[1] + block[2]) * r + lo, size), :]
            return pltpu.make_async_remote_copy(
                src_ref=rows, dst_ref=rows, send_sem=send_sems.at[GATHER_SEMS * a + k],
                recv_sem=recv_sems.at[GATHER_SEMS * a + k], device_id=to, device_id_type=MESH)

        every = range(n)
        out = ([copy(a, 0, me, sibling) for a in every] + [copy(a, 1, me, (*over_x, c)) for a in every]
               + [copy(a, 2, me, (*over_y, c)) for a in every])
        near_in = [copy(a, 1, (*over_x, c), me) for a in every] + [copy(a, 2, (*over_y, c), me) for a in every]
        relay = ([copy(a, 3, (*over_x, c), (*over_y, c), half=0) for a in every]
                 + [copy(a, 4, (*over_y, c), (*over_x, c), half=1) for a in every])
        near_on = [copy(a, 5, (*over_x, c), sibling) for a in every] + [copy(a, 6, (*over_y, c), sibling) for a in every]
        relay_in = ([copy(a, 3, (*across, c), me, half=0) for a in every]
                    + [copy(a, 4, (*across, c), me, half=1) for a in every])
        far_on = [copy(a, 7, (*across, c), sibling) for a in every]
        from_core = ([copy(a, 0, sibling, me) for a in every] + [copy(a, 5, (*over_x, 1 - c), me) for a in every]
                     + [copy(a, 6, (*over_y, 1 - c), me) for a in every] + [copy(a, 7, (*across, 1 - c), me) for a in every])
        stages = [([], out), (near_in, relay + near_on), (relay_in, far_on)]
        return stages, out + relay + near_on + far_on, from_core

    return Exchange([], bufs, [], GATHER_SEMS * n, make)


TO_GATHER = (1, lambda x, y, c: [(x, y, 1 - c), (1 - x, y, c), (x, 1 - y, c)])
TO_SIBLING = (2, lambda x, y, c: [(x, y, 1 - c)])
TO_CHIPS = (3, lambda x, y, c: [(1 - x, y, c), (x, 1 - y, c), (1 - x, 1 - y, c)])
TO_ALL = (4, lambda x, y, c: [(x ^ (m >> 2), y ^ ((m >> 1) & 1), c ^ (m & 1)) for m in range(1, N_DEV)])


def by_sequencer(name, exchanges, who):
    collective_id, peers_of = who
    hbm = pltpu.MemorySpace.HBM
    refs = [([jax.new_ref(a, memory_space=hbm) for a in ex.srcs], [jax.new_ref(a, memory_space=hbm) for a in ex.bufs],
             [jax.empty_ref(s, memory_space=hbm) for s in ex.new_shapes]) for ex in exchanges]
    sems = []
    for ex in exchanges:
        sems += [pltpu.SemaphoreType.DMA((ex.n_sems,)), pltpu.SemaphoreType.DMA((ex.n_sems,))]

    @pl.kernel(mesh=plsc.ScalarSubcoreMesh(axis_name="sequencer", num_cores=1), name=name, scratch_types=tuple(sems),
               compiler_params=pltpu.CompilerParams(collective_id=collective_id))
    def launch(*sem_refs):
        peers = peers_of(*_coords())
        barrier = pltpu.get_barrier_semaphore()
        for peer in peers:
            pl.semaphore_signal(barrier, inc=1, device_id=peer, device_id_type=MESH)
        pl.semaphore_wait(barrier, len(peers))

        made = [ex.make(*refs[k], sem_refs[2 * k], sem_refs[2 * k + 1]) for k, ex in enumerate(exchanges)]
        for stage in range(max(len(stages) for stages, _, _ in made)):
            for stages, _, _ in made:
                if stage < len(stages):
                    arrivals, starts = stages[stage]
                    for cp in arrivals:
                        cp.wait_recv()
                    for cp in starts:
                        cp.start()
        for _, sends, arrivals in made:
            for cp in arrivals:
                cp.wait_recv()
            for cp in sends:
                cp.wait_send()

    launch()
    return [([ref[...] for ref in bufs], [ref[...] for ref in news]) for _, bufs, news in refs]


def place_shards(name, shards, dev):
    n = len(shards)

    def body(dev_ref, *refs):
        for a in range(n):
            refs[n + a][...] = refs[a][...].astype(BF16)

    spec = pltpu.PrefetchScalarGridSpec(
        num_scalar_prefetch=1, grid=(1,),
        in_specs=[pl.BlockSpec(s.shape, lambda i, dev_ref: (0, 0)) for s in shards],
        out_specs=[pl.BlockSpec(s.shape, lambda i, dev_ref: (dev_ref[0], 0)) for s in shards])
    return pl.pallas_call(
        body, name=name, grid_spec=spec,
        out_shape=[jax.ShapeDtypeStruct((N_DEV * s.shape[0], s.shape[1]), BF16) for s in shards],
        compiler_params=_params(),
    )(dev, *shards)


def _swap(copies_of):
    def make(src_refs, buf_refs, new_refs, send_sems, recv_sems):
        copies = copies_of(src_refs, new_refs, send_sems, recv_sems)
        return [([], copies)], copies, copies

    return make


def to_sibling(grads):
    def copies_of(src_refs, new_refs, send_sems, recv_sems):
        x, y, c = _coords()
        return [pltpu.make_async_remote_copy(
            src_ref=src_refs[a].at[2 * xy + 1 - c], dst_ref=new_refs[a].at[xy], send_sem=send_sems.at[4 * a + xy],
            recv_sem=recv_sems.at[4 * a + xy], device_id=(x, y, 1 - c), device_id_type=MESH)
            for a in range(len(src_refs)) for xy in range(4)]

    return Exchange(grads, [], [jax.ShapeDtypeStruct((4,) + g.shape[1:], g.dtype) for g in grads], 4 * len(grads),
                    _swap(copies_of))


def to_chips(parts):
    def copies_of(src_refs, new_refs, send_sems, recv_sems):
        x, y, c = _coords()
        chips = [(1 - x, y), (x, 1 - y), (1 - x, 1 - y)]
        return [pltpu.make_async_remote_copy(
            src_ref=src_refs[a].at[2 * px + py], dst_ref=new_refs[a].at[2 * x + y], send_sem=send_sems.at[3 * a + j],
            recv_sem=recv_sems.at[3 * a + j], device_id=(px, py, c), device_id_type=MESH)
            for a in range(len(src_refs)) for j, (px, py) in enumerate(chips)]

    return Exchange(parts, [], [jax.ShapeDtypeStruct(p.shape, p.dtype) for p in parts], 3 * len(parts), _swap(copies_of))


def to_owners(grad):
    def copies_of(src_refs, new_refs, send_sems, recv_sems):
        x, y, c = _coords()
        copies = []
        for m in range(1, N_DEV):
            px, py, pc = x ^ (m >> 2), y ^ ((m >> 1) & 1), c ^ (m & 1)
            copies.append(pltpu.make_async_remote_copy(
                src_ref=src_refs[0].at[4 * px + 2 * py + pc], dst_ref=new_refs[0].at[4 * x + 2 * y + c],
                send_sem=send_sems.at[m - 1], recv_sem=recv_sems.at[m - 1], device_id=(px, py, pc), device_id_type=MESH))
        return copies

    return Exchange([grad], [], [jax.ShapeDtypeStruct(grad.shape, grad.dtype)], N_DEV - 1, _swap(copies_of))


def to_everyone(vec):
    def copies_of(src_refs, new_refs, send_sems, recv_sems):
        x, y, c = _coords()
        copies = []
        for m in range(1, N_DEV):
            px, py, pc = x ^ (m >> 2), y ^ ((m >> 1) & 1), c ^ (m & 1)
            copies.append(pltpu.make_async_remote_copy(
                src_ref=src_refs[0], dst_ref=new_refs[0].at[4 * x + 2 * y + c],
                send_sem=send_sems.at[m - 1], recv_sem=recv_sems.at[m - 1], device_id=(px, py, pc), device_id_type=MESH))
        return copies

    return Exchange([vec], [], [jax.ShapeDtypeStruct((N_DEV,) + vec.shape, vec.dtype)], N_DEV - 1, _swap(copies_of))


def sum_cores(name, grads, others, core, after=()):
    k = len(grads)

    def body(core_ref, *refs):
        for j in range(k):
            out_ref = refs[2 * k + len(after) + j]
            out_ref[...] = (refs[j][:, 0].astype(F32) + refs[k + j][...].astype(F32)).astype(out_ref.dtype)

    mine = [pl.BlockSpec((2, 1) + o.shape[1:], lambda i, core_ref: (i, core_ref[0], 0, 0)) for o in others]
    theirs = [pl.BlockSpec((2,) + o.shape[1:], lambda i, core_ref: (i, 0, 0)) for o in others]
    return pl.pallas_call(
        body, name=name,
        grid_spec=pltpu.PrefetchScalarGridSpec(
            num_scalar_prefetch=1, grid=(2,), in_specs=mine + theirs + [ANY] * len(after), out_specs=theirs),
        out_shape=[jax.ShapeDtypeStruct(o.shape, o.dtype) for o in others],
        compiler_params=_params(),
    )(core, *[g.reshape((4, 2) + g.shape[1:]) for g in grads], *others, *after)


def sum_owned(name, grad, others, dev_ids, after=()):
    _, r, w = grad.shape

    def body(ids_ref, *refs):
        acc = refs[0][0]
        for k in range(1, N_DEV):
            acc = acc + refs[k][0]
        refs[-1][...] = acc

    def pick(k):
        return pl.BlockSpec((1, r, w), lambda i, ids_ref: (ids_ref[k], 0, 0))

    return pl.pallas_call(
        body, name=name,
        grid_spec=pltpu.PrefetchScalarGridSpec(
            num_scalar_prefetch=1, grid=(1,), in_specs=[pick(k) for k in range(N_DEV)] + [ANY] * len(after),
            out_specs=pl.BlockSpec((r, w), lambda i, ids_ref: (ids_ref[0], 0))),
        out_shape=jax.ShapeDtypeStruct((N_DEV * r, w), F32),
        compiler_params=_params(),
    )(dev_ids, grad, *([others] * (N_DEV - 1)), *after)


def _adamw_update(w, g, m, v):
    nm = ADAM_B1 * m + np.float32(1.0 - ADAM_B1) * g
    nv = ADAM_B2 * v + np.float32(1.0 - ADAM_B2) * (g * g)
    m_hat = nm / np.float32(1.0 - ADAM_B1 ** ADAM_STEP)
    v_hat = nv / np.float32(1.0 - ADAM_B2 ** ADAM_STEP)
    return -ADAM_LR * (m_hat / (jnp.sqrt(v_hat) + ADAM_EPS) + ADAM_WD * w), nm, nv


def adamw_of_sums(name, parts, others, chip_ids, ws, ms, vs, after):
    n = len(parts)
    halves = 2

    def body(ids_ref, *refs):
        outs = refs[7 * n + 1:]
        for j in range(n):
            p_ref, a_ref, b_ref, c_ref, w_ref, m_ref, v_ref = refs[7 * j:7 * j + 7]
            g = ((p_ref[0].astype(F32) + a_ref[0].astype(F32)) + b_ref[0].astype(F32)) + c_ref[0].astype(F32)
            outs[4 * j][...] = g
            outs[4 * j + 1][...], outs[4 * j + 2][...], outs[4 * j + 3][...] = _adamw_update(w_ref[...], g, m_ref[...], v_ref[...])

    in_specs, out_specs, out_shape, operands = [], [], [], []
    for part, other, w, m, v in zip(parts, others, ws, ms, vs):
        _, r, wd = part.shape
        rows = r // halves
        whole = pl.BlockSpec((rows, wd), lambda i, ids_ref: (i, 0))
        in_specs += [pl.BlockSpec((1, rows, wd), lambda i, ids_ref, k=k: (ids_ref[k], i, 0)) for k in range(4)] + [whole] * 3
        out_specs += [whole] * 4
        out_shape += [jax.ShapeDtypeStruct((r, wd), F32)] * 4
        operands += [part, other, other, other, w, m, v]
    outs = pl.pallas_call(
        body, name=name,
        grid_spec=pltpu.PrefetchScalarGridSpec(
            num_scalar_prefetch=1, grid=(halves,), in_specs=in_specs + [ANY], out_specs=out_specs),
        out_shape=out_shape,
        compiler_params=_params(),
    )(chip_ids, *operands, after)
    return [tuple(outs[4 * j:4 * j + 4]) for j in range(n)]


def pack_small(parts):
    names = [name for name, _ in SMALL if name in parts]
    operands = [parts[name] for name in names]
    first_row, at = {}, 0
    for name, size in SMALL:
        first_row[name] = at // 128
        at += size
    sizes = dict(SMALL)

    def body(*refs):
        out_ref = refs[-1]
        out_ref[...] = jnp.zeros_like(out_ref)
        for name, ref in zip(names, refs):
            row = first_row[name]
            if name == "loss_sum":
                lane0 = lax.broadcasted_iota(jnp.int32, (1, 128), 1) == 0
                out_ref[row:row + 1, :] = jnp.where(lane0, ref[...], 0.0)
            else:
                rows = sizes[name] // 128
                out_ref[row:row + rows, :] = ref[...].reshape(rows, 128)

    vmem = pl.BlockSpec(memory_space=pltpu.VMEM)
    return pl.pallas_call(
        body, name="pack_small", in_specs=[vmem] * len(names), out_specs=vmem,
        out_shape=jax.ShapeDtypeStruct((SMALL_ROWS, 128), F32), compiler_params=_params(()),
    )(*operands)


LATE = "pre_mix_norm"


def adamw_small(packed_g, late_parts, ws, ms, vs):
    names = [name for name, _ in SMALL if name != "loss_sum"]
    k = len(names)
    shapes = [ws[name].shape[1:] if ws[name].ndim > 2 else ws[name].shape for name in names]
    first_row, at = [], 0
    for name, size in SMALL:
        first_row.append(at // 128)
        at += size

    def body(g_ref, late_ref, *refs):
        w_refs, m_refs, v_refs, outs = refs[:k], refs[k:2 * k], refs[2 * k:3 * k], refs[3 * k:]
        for i, (name, size) in enumerate(SMALL[:k]):
            if name == LATE:
                g = late_ref[0]
                for j in range(1, N_DEV):
                    g = g + late_ref[j]
            else:
                g = g_ref[first_row[i]:first_row[i] + size // 128, :].reshape(shapes[i])
            outs[i][...] = g
            outs[k + i][...], outs[2 * k + i][...], outs[3 * k + i][...] = _adamw_update(
                w_refs[i][...], g, m_refs[i][...], v_refs[i][...])
        outs[4 * k][...] = g_ref[first_row[k]:first_row[k] + 1, 0:1]

    vmem = pl.BlockSpec(memory_space=pltpu.VMEM)
    operands = [t[name].reshape(shape) for t in (ws, ms, vs) for name, shape in zip(names, shapes)]
    outs = pl.pallas_call(
        body, name="adamw_small", in_specs=[vmem] * (2 + 3 * k), out_specs=[vmem] * (4 * k + 1),
        out_shape=[jax.ShapeDtypeStruct(shape, F32) for _ in range(4) for shape in shapes] + [jax.ShapeDtypeStruct((1, 1), F32)],
        compiler_params=_params(()),
    )(packed_g, late_parts, *operands)
    tables = [{name: outs[j * k + i].reshape(ws[name].shape) for i, name in enumerate(names)} for j in range(4)]
    return (*tables, outs[4 * k])


def kernel(x, positions, pre_mix_norm, w_in, sgu_ln_gain, sgu_ln_bias, sgu_w_spatial, sgu_b_spatial, attn_out_norm, sgu_out_norm, w_out, post_mix_norm, pre_ffn_norm, w_gate, w_up, w_down, post_ffn_norm, loss_target, m_pre_mix_norm, m_w_in, m_sgu_ln_gain, m_sgu_ln_bias, m_sgu_w_spatial, m_sgu_b_spatial, m_attn_out_norm, m_sgu_out_norm, m_w_out, m_post_mix_norm, m_pre_ffn_norm, m_w_gate, m_w_up, m_w_down, m_post_ffn_norm, v_pre_mix_norm, v_w_in, v_sgu_ln_gain, v_sgu_ln_bias, v_sgu_w_spatial, v_sgu_b_spatial, v_attn_out_norm, v_sgu_out_norm, v_w_out, v_post_mix_norm, v_pre_ffn_norm, v_w_gate, v_w_up, v_w_down, v_post_ffn_norm):
    small_w = dict(pre_mix_norm=pre_mix_norm, sgu_ln_gain=sgu_ln_gain, sgu_ln_bias=sgu_ln_bias, sgu_w_spatial=sgu_w_spatial,
                   sgu_b_spatial=sgu_b_spatial, attn_out_norm=attn_out_norm, sgu_out_norm=sgu_out_norm,
                   post_mix_norm=post_mix_norm, pre_ffn_norm=pre_ffn_norm, post_ffn_norm=post_ffn_norm)
    small_m = dict(pre_mix_norm=m_pre_mix_norm, sgu_ln_gain=m_sgu_ln_gain, sgu_ln_bias=m_sgu_ln_bias, sgu_w_spatial=m_sgu_w_spatial,
                   sgu_b_spatial=m_sgu_b_spatial, attn_out_norm=m_attn_out_norm, sgu_out_norm=m_sgu_out_norm,
                   post_mix_norm=m_post_mix_norm, pre_ffn_norm=m_pre_ffn_norm, post_ffn_norm=m_post_ffn_norm)
    small_v = dict(pre_mix_norm=v_pre_mix_norm, sgu_ln_gain=v_sgu_ln_gain, sgu_ln_bias=v_sgu_ln_bias, sgu_w_spatial=v_sgu_w_spatial,
                   sgu_b_spatial=v_sgu_b_spatial, attn_out_norm=v_attn_out_norm, sgu_out_norm=v_sgu_out_norm,
                   post_mix_norm=v_post_mix_norm, pre_ffn_norm=v_pre_ffn_norm, post_ffn_norm=v_post_ffn_norm)

    x2d = x[0]
    target = loss_target[0]
    pos_col = positions.reshape(SEQ, 1)
    rot = _rot_consts()
    w_sp = sgu_w_spatial[0]
    bfull = jnp.repeat(sgu_b_spatial[0].T, HEAD_DIM, axis=1)

    x_i, y_i, c_i = (lax.axis_index(a).astype(jnp.int32) for a in MESH_AXES)
    dev = 4 * x_i + 2 * y_i + c_i
    core = c_i.reshape(1)
    chip = 2 * x_i + y_i
    chip_ids = jnp.stack([chip, chip ^ 1, chip ^ 2, chip ^ 3])
    dev_ids = jnp.stack([dev ^ m for m in range(N_DEV)])

    def gathered(name, bufs):
        return by_sequencer(name, [gather(bufs)], TO_GATHER)[0][0]

    def from_sibling(name, grads):
        return by_sequencer(name, [to_sibling(grads)], TO_SIBLING)[0][1]

    def from_chips(name, parts):
        return by_sequencer(name, [to_chips(parts)], TO_CHIPS)[0][1]

    (w_in_t,) = place_shards("place_w_in", [w_in[0].T], dev.reshape(1))
    (w_in_t,) = gathered("gather_w_in", [w_in_t])
    w_gate_t, w_up_t, w_out_f, w_down_f = place_shards(
        "place_weights", [w_gate[0].T, w_up[0].T, w_out[0], w_down[0]], dev.reshape(1))
    (w_out_f,) = gathered("gather_w_out", [w_out_f])
    w_gate_t, w_up_t = gathered("gather_w_gate_up", [w_gate_t, w_up_t])
    (w_down_f,) = gathered("gather_w_down", [w_down_f])

    h1, u, vs, q, k, v = in_proj(x2d, pos_col, pre_mix_norm, w_in_t, rot)
    attn_r, lse, attn = attn_fwd(q, k, v)
    (sgu,) = sgu_fwd(u, vs, sgu_ln_gain, sgu_ln_bias, w_sp, bfull)
    mix, y, x2, h2 = out_proj(attn, sgu, x2d, attn_out_norm, sgu_out_norm, w_out_f, post_mix_norm, pre_ffn_norm)
    gate, up, act = ffn_up(h2, w_gate_t, w_up_t)
    df, dx3, d_post_ffn, sq_err = ffn_down_loss(act, w_down_f, x2, post_ffn_norm, target)

    g_w_down = weight_grad("grad_w_down", act, df)
    (s_down,) = from_sibling("w_down_to_sibling", [g_w_down])
    dgate, dup, dx2, dy, d_pre_ffn, d_post_mix = ffn_bwd(
        df, w_down_f, gate, up, w_gate_t, w_up_t, x2, pre_ffn_norm, dx3, y, post_mix_norm, after=[g_w_down])
    (p_down,) = sum_cores("sum_cores_down", [g_w_down], [s_down], core, after=[dy])
    (c_down,) = from_chips("w_down_to_chips", [p_down])
    g_w_gate, g_w_up = weight_grads("grad_w_gate_up", [dgate, dup], h2, after=[p_down])
    s_gate, s_up = from_sibling("w_gate_up_to_sibling", [g_w_gate, g_w_up])
    g_w_out = weight_grad("grad_w_out", mix, dy, after=[g_w_up, c_down])
    p_gate, p_up = sum_cores("sum_cores_gate_up", [g_w_gate, g_w_up], [s_gate, s_up], core, after=[g_w_out])
    c_gate, c_up = from_chips("w_gate_up_to_chips", [p_gate, p_up])
    (s_out,) = from_sibling("w_out_to_sibling", [g_w_out])
    dsgu, d_attn_out, d_sgu_out, dattn_r = mix_bwd(dy, w_out_f, attn, sgu, attn_out_norm, sgu_out_norm, after=[p_gate, p_up])
    du, dvs, d_ln_gain, d_ln_bias, d_w_sp, d_bfull = sgu_bwd(u, vs, dsgu, sgu_ln_gain, sgu_ln_bias, w_sp, bfull)

    d_b_sp = d_bfull.reshape(CHUNK, N_GROUPS, HEAD_DIM).sum(axis=-1).T
    small_g = pack_small(dict(sgu_ln_gain=d_ln_gain, sgu_ln_bias=d_ln_bias, sgu_w_spatial=d_w_sp, sgu_b_spatial=d_b_sp,
                              attn_out_norm=d_attn_out, sgu_out_norm=d_sgu_out, post_mix_norm=d_post_mix,
                              pre_ffn_norm=d_pre_ffn, post_ffn_norm=d_post_ffn, loss_sum=sq_err))
    small_g = small_g.reshape(N_DEV, SMALL_ROWS // N_DEV, 128)
    ((_, (o_small,)),) = by_sequencer("small_to_owners", [to_owners(small_g)], TO_ALL)

    dq, dk, dv = attn_bwd(q, k, v, attn_r, lse, dattn_r, _to_residue_order(pos_col), rot)
    summed_small = sum_owned("sum_small", small_g, o_small, dev_ids, after=[dq])
    (all_small,) = gathered("gather_small_grads", [summed_small])
    (p_out,) = sum_cores("sum_cores_out", [g_w_out], [s_out], core, after=[dq])
    (c_out,) = from_chips("w_out_to_chips", [p_out])
    dproj = [dq, dk, dv, du, dvs]
    g_w_in = weight_grad_of_parts("grad_w_in", dproj, h1, after=[c_gate, c_up])
    (s_in,) = from_sibling("w_in_to_sibling", [g_w_in])
    grad_x, d_pre_mix = in_bwd(dproj, w_in_t, x2d, pre_mix_norm, dx2, after=[g_w_in, all_small])
    (p_in,) = sum_cores("sum_cores_in", [g_w_in], [s_in], core, after=[d_pre_mix, c_out])
    (_, (c_in,)), (_, (late_parts,)) = by_sequencer(
        "last_sums_to_owners", [to_chips([p_in]), to_everyone(d_pre_mix)], TO_ALL)
    late_parts = lax.dynamic_update_slice(late_parts, d_pre_mix[None], (dev, 0, 0))

    def same(t):
        return t

    def turned(t):
        return t.T

    big, last = {}, p_in
    for call, weights in (("adamw_ffn", (("w_down", w_down, p_down, c_down, m_w_down, v_w_down, same),
                                         ("w_gate", w_gate, p_gate, c_gate, m_w_gate, v_w_gate, turned),
                                         ("w_up", w_up, p_up, c_up, m_w_up, v_w_up, turned))),
                          ("adamw_w_out", (("w_out", w_out, p_out, c_out, m_w_out, v_w_out, same),)),
                          ("adamw_w_in", (("w_in", w_in, p_in, c_in, m_w_in, v_w_in, turned),))):
        results = adamw_of_sums(call, [p for _, _, p, _, _, _, _ in weights], [c for _, _, _, c, _, _, _ in weights], chip_ids,
                                [turn(w[0]) for _, w, _, _, _, _, turn in weights], [turn(m[0]) for _, _, _, _, m, _, turn in weights],
                                [turn(vv[0]) for _, _, _, _, _, vv, turn in weights], last)
        for (name, _, _, _, _, _, turn), outs in zip(weights, results):
            big[name] = tuple(turn(t)[None] for t in outs)
        last = results[-1][0]
    sg, sd, snm, snv, loss_sum = adamw_small(all_small, late_parts, small_w, small_m, small_v)
    loss = loss_sum[0, 0] * np.float32(0.5 / D_MODEL)

    names = ["pre_mix_norm", "w_in", "sgu_ln_gain", "sgu_ln_bias", "sgu_w_spatial", "sgu_b_spatial", "attn_out_norm",
             "sgu_out_norm", "w_out", "post_mix_norm", "pre_ffn_norm", "w_gate", "w_up", "w_down", "post_ffn_norm"]
    outs = [loss, grad_x[None]]
    for i, table in enumerate((sg, sd, snm, snv)):
        for name in names:
            outs.append(big[name][i] if name in big else table[name])
    return tuple(outs)
```

```python
import numpy as np
import jax
import jax.numpy as jnp
from jax import lax
from jax.experimental import pallas as pl
from jax.experimental.pallas import tpu as pltpu
from jax.experimental.pallas import tpu_sc as plsc

F32 = jnp.float32
BF16 = jnp.bfloat16

SEQ = 2048
D_MODEL = 1024
ATTN_W = 512
SGU_W = 512
HEAD_DIM = 64
N_GROUPS = 8
CHUNK = 128
D_FF = 2816
IN_W = 3 * ATTN_W + 2 * SGU_W
DILATIONS = (1, 4, 16)
ROPE_THETA = 500000.0
ROT_DIM = 16
ROT_HALF = 8
RMS_EPS = 1e-6
LN_EPS = 1e-5
Q_SCALE = 0.125
NEG = -1e30

N_DEV = 8
MESH_AXES = ("x", "y", "c")
MESH = pl.DeviceIdType.MESH

ADAM_LR = 0.001
ADAM_B1 = 0.9
ADAM_B2 = 0.999
ADAM_EPS = 1e-08
ADAM_WD = 0.01
ADAM_STEP = 10

VMEM_LIMIT = 60 * 1024 * 1024
ANY = pl.BlockSpec(memory_space=pl.ANY)

SMALL = (("pre_mix_norm", 1024), ("sgu_ln_gain", 512), ("sgu_ln_bias", 512), ("sgu_w_spatial", 8 * 128 * 128),
         ("sgu_b_spatial", 1024), ("attn_out_norm", 512), ("sgu_out_norm", 512), ("post_mix_norm", 1024),
         ("pre_ffn_norm", 1024), ("post_ffn_norm", 1024), ("loss_sum", 1))
SMALL_ROWS = 1152


def _params(sem=("arbitrary",)):
    return pltpu.CompilerParams(dimension_semantics=sem, vmem_limit_bytes=VMEM_LIMIT)


def _dot(a, b):
    return jnp.dot(a, b, preferred_element_type=F32)


def _dot_nt(a, b):
    return lax.dot_general(a, b, (((1,), (1,)), ((), ())), preferred_element_type=F32)


def _dot_tn(a, b):
    return lax.dot_general(a, b, (((0,), (0,)), ((), ())), preferred_element_type=F32)


def _rms(z):
    return lax.rsqrt(jnp.mean(z * z, axis=-1, keepdims=True) + RMS_EPS)


def _rms_bwd(z, gain, d):
    r = _rms(z)
    n = z * r
    dn = d * gain
    dz = r * (dn - n * jnp.mean(dn * n, axis=-1, keepdims=True))
    return dz, jnp.sum(d * n, axis=0, keepdims=True)


def _gelu(z):
    return 0.5 * z * (1.0 + lax.erf(z * np.float32(1.0 / np.sqrt(2.0))))


def _gelu_grad(z):
    cdf = 0.5 * (1.0 + lax.erf(z * np.float32(1.0 / np.sqrt(2.0))))
    return cdf + z * jnp.exp(-0.5 * z * z) * np.float32(1.0 / np.sqrt(2.0 * np.pi))


def _rot_tables(pos_col, invf, ma, mb):
    ang = pos_col.astype(F32) * invf
    s = jnp.sin(ang)
    return jnp.cos(ang), s * ma, s * mb


def _rot(t, c, sa, sb):
    return t * c + pltpu.roll(t, 120, 1) * sa + pltpu.roll(t, 8, 1) * sb


def _rot_t(d, c, sa, sb):
    return d * c + pltpu.roll(d * sa, 8, 1) + pltpu.roll(d * sb, 120, 1)


def _rot_consts():
    lane = np.arange(128) % HEAD_DIM
    inv_freq = (np.float32(ROPE_THETA) ** (-np.arange(0, ROT_DIM, 2, dtype=np.float32) / np.float32(ROT_DIM))).astype(np.float32)
    invf = np.where(lane < ROT_DIM, inv_freq[lane % ROT_HALF], 0.0).astype(np.float32)
    ma = np.where(lane < ROT_HALF, -1.0, 0.0).astype(np.float32)
    mb = np.where((lane >= ROT_HALF) & (lane < ROT_DIM), 1.0, 0.0).astype(np.float32)
    return jnp.asarray(invf[None]), jnp.asarray(ma[None]), jnp.asarray(mb[None])


def _row_spec(tm, w):
    return pl.BlockSpec((tm, w), lambda i: (i, 0))


def _full_spec(shape):
    return pl.BlockSpec(shape, lambda i: (0,) * len(shape))


def _weight_spec(shape):
    return pl.BlockSpec(shape, lambda i: (0,) * len(shape), pipeline_mode=pl.Buffered(1))


FF_CHUNKS = (256, 512, 1024, 1024)
FF_SPANS = [(int(o), n) for o, n in zip(np.cumsum((0,) + FF_CHUNKS[:-1]), FF_CHUNKS)]
FF_WHOLE = [(0, D_FF)]


def _ffn_weight_scratch(n_weights, spans):
    return [pltpu.VMEM((D_FF, D_MODEL), BF16)] * n_weights + [pltpu.SemaphoreType.DMA((n_weights, len(spans)))]


def _with_ffn_weights(w_hbm, w_vmem, sems, spans, run):
    copies = [[pltpu.make_async_copy(h.at[pl.ds(o, n)], v.at[pl.ds(o, n)], sems.at[j, c]) for c, (o, n) in enumerate(spans)]
              for j, (h, v) in enumerate(zip(w_hbm, w_vmem))]
    first = pl.program_id(0) == 0

    @pl.when(first)
    def _():
        for of_weight in copies:
            for cp in of_weight:
                cp.start()
        run(lambda j, c: copies[j][c].wait())

    @pl.when(jnp.logical_not(first))
    def _():
        run(None)


RES = 16


def _residue_scratch(n_arrays, tm, width):
    return [pltpu.VMEM((2, n_arrays, tm // RES, RES, width), F32), pltpu.SemaphoreType.DMA((2, n_arrays, RES))]


def _to_residue_rows(tiles, outs, scratch, sems, tm, n_steps):
    i = pl.program_id(0)
    slot = i % 2
    per = tm // RES

    def copies(step, s):
        return [pltpu.make_async_copy(scratch.at[s, a, :, b, :],
                                      outs[a].at[pl.ds(pl.multiple_of(b * (SEQ // RES) + per * step, per), per), :],
                                      sems.at[s, a, b]) for a in range(len(outs)) for b in range(RES)]

    @pl.when(i >= 2)
    def _():
        for cp in copies(i - 2, slot):
            cp.wait()

    for a, tile in enumerate(tiles):
        scratch[slot, a] = tile.reshape(per, RES, tile.shape[-1])
    for cp in copies(i, slot):
        cp.start()

    @pl.when(i == n_steps - 1)
    def _():
        for cp in copies(i - 1, 1 - slot) + copies(i, slot):
            cp.wait()


def in_proj(x, pos_col, g1, w_in_t, rot):
    tm = 512
    n_steps = SEQ // tm

    def body(x_ref, pos_ref, g_ref, w_ref, invf_ref, ma_ref, mb_ref, h_ref, u_ref, vs_ref, q_ref, k_ref, v_ref, scratch, sems):
        xf = x_ref[...]
        h = (xf * _rms(xf) * g_ref[...]).astype(BF16)
        h_ref[...] = h
        proj = _dot_nt(h, w_ref[...])
        c, sa, sb = _rot_tables(pos_ref[...], invf_ref[...], ma_ref[...], mb_ref[...])
        slabs = range(ATTN_W // 128)
        q = jnp.concatenate([_rot(proj[:, j * 128:(j + 1) * 128], c, sa, sb) * Q_SCALE for j in slabs], axis=1)
        k = jnp.concatenate([_rot(proj[:, ATTN_W + j * 128:ATTN_W + (j + 1) * 128], c, sa, sb) for j in slabs], axis=1)
        u_ref[...] = proj[:, 3 * ATTN_W:3 * ATTN_W + SGU_W]
        vs_ref[...] = proj[:, 3 * ATTN_W + SGU_W:]
        _to_residue_rows([q, k, proj[:, 2 * ATTN_W:3 * ATTN_W]], [q_ref, k_ref, v_ref], scratch, sems, tm, n_steps)

    act = jax.ShapeDtypeStruct((SEQ, 512), F32)
    return _call(
        "in_proj", body, n_steps,
        [_row_spec(tm, D_MODEL), _row_spec(tm, 1), _full_spec((1, D_MODEL)), _weight_spec((IN_W, D_MODEL)),
         _full_spec((1, 128)), _full_spec((1, 128)), _full_spec((1, 128))],
        [_row_spec(tm, D_MODEL)] + [_row_spec(tm, 512)] * 2 + [ANY] * 3,
        [jax.ShapeDtypeStruct((SEQ, D_MODEL), BF16)] + [act] * 5,
        (x, pos_col, g1, w_in_t, *rot), scratch_shapes=_residue_scratch(3, tm, ATTN_W))


def _to_residue_order(t):
    return t.reshape(SEQ // RES, RES, -1).transpose(1, 0, 2).reshape(t.shape)


def _block_rows(d, r, n):
    if d == 16:
        slices = [(128 * r, 128)]
    elif d == 4:
        slices = [(128 * (4 * b + r) + 32 * n, 32) for b in range(4)]
    else:
        slices = [(128 * b + 8 * n, 8) for b in range(RES)]
    return [(s if isinstance(s, int) else pl.multiple_of(s, z), z) for s, z in slices]


def _block_step(d, i):
    if d == 16:
        return i
    if d == 4:
        return 4 * (i & 31) + (i >> 5)
    return 16 * (i & 7) + (i >> 3)


def _attn_masks(d):
    row2 = _block_step(d, lax.broadcasted_iota(jnp.int32, (128, 256), 0))
    col2 = lax.broadcasted_iota(jnp.int32, (128, 256), 1)
    key2 = _block_step(d, col2 & 127)
    mask2 = jnp.logical_or(jnp.logical_and(col2 < 128, key2 >= row2), jnp.logical_and(col2 >= 128, key2 <= row2))
    row1 = _block_step(d, lax.broadcasted_iota(jnp.int32, (128, 128), 0))
    col1 = lax.broadcasted_iota(jnp.int32, (128, 128), 1)
    return col1 < HEAD_DIM, _block_step(d, col1) <= row1, mask2


def _load_rows(ref, slices):
    parts = [ref[pl.ds(s, z), :] for s, z in slices]
    return parts[0] if len(parts) == 1 else jnp.concatenate(parts, axis=0)


def _for_each_group(fn):
    for p, d in enumerate(DILATIONS):
        masks = _attn_masks(d)
        if d == 16:
            def group(i, carry, p=p, masks=masks):
                fn(p, masks, [(_block_rows(16, 8 * i + g, 0), None) for g in range(8)])
                return carry

            lax.fori_loop(0, 2, group, 0)
        elif d == 4:
            fn(p, masks, [(_block_rows(4, r, 0), None) for r in range(4)])

            def group(i, carry, p=p, masks=masks):
                blocks = [6 * i + g for g in range(6)]
                fn(p, masks, [(_block_rows(4, j % 4, 1 + j // 4), _block_rows(4, j % 4, j // 4)) for j in blocks])
                return carry

            lax.fori_loop(0, 2, group, 0)
        else:
            fn(p, masks, [(_block_rows(1, 0, 0), None)])

            def group(i, carry, p=p, masks=masks):
                fn(p, masks, [(_block_rows(1, 0, 5 * i + g + 1), _block_rows(1, 0, 5 * i + g)) for g in range(5)])
                return carry

            lax.fori_loop(0, 3, group, 0)


def attn_fwd(q, k, v):
    def body(q_ref, k_ref, v_ref, o_ref, lse_ref, nat_ref, op_ref, lp_ref, sems):
        def group(p, masks, blocks):
            head0, mask1, mask2 = masks
            heads = (head0, jnp.logical_not(head0))
            keys = [rows if prev is None else prev + rows for rows, prev in blocks]
            mask = [mask1 if prev is None else mask2 for _, prev in blocks]
            qb = [_load_rows(q_ref, rows) for rows, _ in blocks]
            kk = [_load_rows(k_ref, ks).astype(BF16) for ks in keys]
            vv = [_load_rows(v_ref, ks).astype(BF16) for ks in keys]
            chains = [(g, hm) for g in range(len(blocks)) for hm in heads]
            s = [jnp.where(mask[g], _dot_nt(jnp.where(hm, qb[g], 0.0).astype(BF16), kk[g]), NEG) for g, hm in chains]
            m = [jnp.max(t, axis=-1, keepdims=True) for t in s]
            e = [jnp.exp(t - mt) for t, mt in zip(s, m)]
            l = [jnp.sum(t, axis=-1, keepdims=True) for t in e]
            pv = [_dot(t.astype(BF16), vv[g]) for t, (g, _) in zip(e, chains)]
            for g, (rows, _) in enumerate(blocks):
                o_blk = jnp.where(head0, pv[2 * g] / l[2 * g], pv[2 * g + 1] / l[2 * g + 1])
                l_blk = jnp.where(head0, jnp.broadcast_to(m[2 * g] + jnp.log(l[2 * g]), (128, 128)),
                                  jnp.broadcast_to(m[2 * g + 1] + jnp.log(l[2 * g + 1]), (128, 128)))
                at = 0
                for start, size in rows:
                    op_ref[p, pl.ds(start, size), :] = o_blk[at:at + size]
                    lp_ref[p, pl.ds(start, size), :] = l_blk[at:at + size]
                    at += size

        _for_each_group(group)

        def combine(i, carry):
            rows = pl.ds(pl.multiple_of(i * 256, 256), 256)
            ls = [lp_ref[p, rows, :] for p in range(3)]
            m = jnp.maximum(jnp.maximum(ls[0], ls[1]), ls[2])
            lse = m + jnp.log(jnp.exp(ls[0] - m) + jnp.exp(ls[1] - m) + jnp.exp(ls[2] - m))
            o = jnp.zeros((256, 128), F32)
            for p in range(3):
                o = o + jnp.exp(ls[p] - lse) * op_ref[p, rows, :]
            o_ref[rows, :] = o
            lse_ref[rows, :] = lse
            return carry

        lax.fori_loop(0, SEQ // 256, combine, 0)

        lanes = pl.ds(pl.multiple_of(pl.program_id(0) * 128, 128), 128)
        back = [pltpu.make_async_copy(o_ref.at[pl.ds(b * (SEQ // RES), SEQ // RES), :], nat_ref.at[:, b, lanes], sems.at[b])
                for b in range(RES)]
        for cp in back:
            cp.start()
        for cp in back:
            cp.wait()

    slab = pl.BlockSpec((SEQ, 128), lambda i: (0, i))
    out = jax.ShapeDtypeStruct((SEQ, ATTN_W), F32)
    attn_r, lse, attn = _call(
        "attn_fwd", body, ATTN_W // 128, [slab] * 3, [slab] * 2 + [ANY],
        [out, out, jax.ShapeDtypeStruct((SEQ // RES, RES, ATTN_W), F32)], (q, k, v),
        scratch_shapes=[pltpu.VMEM((3, SEQ, 128), F32), pltpu.VMEM((3, SEQ, 128), F32), pltpu.SemaphoreType.DMA((RES,))])
    return attn_r, lse, attn.reshape(SEQ, ATTN_W)


def _causal_weights(w_ref):
    row = lax.broadcasted_iota(jnp.int32, (CHUNK, CHUNK), 0)
    col = lax.broadcasted_iota(jnp.int32, (CHUNK, CHUNK), 1)
    return [jnp.where(col <= row, w_ref[g], 0.0).astype(BF16) for g in range(N_GROUPS)], col <= row


def _sgu_chunk_fwd(u, vs, lg, lb, wc, bfull, head0):
    ug = _gelu(u)
    vg = _gelu(vs)
    xc = vg - jnp.mean(vg, axis=-1, keepdims=True)
    rstd = lax.rsqrt(jnp.mean(xc * xc, axis=-1, keepdims=True) + LN_EPS)
    xhat = xc * rstd
    vn = xhat * lg + lb
    mixed = []
    for gp in range(SGU_W // 128):
        vp = vn[:, gp * 128:(gp + 1) * 128].astype(BF16)
        mixed.append(jnp.where(head0, _dot(wc[2 * gp], vp), _dot(wc[2 * gp + 1], vp)))
    ms = jnp.concatenate(mixed, axis=1) + bfull
    return ug, xhat, rstd, vn, ms


def sgu_fwd(u, vs, lg, lb, w_sp, bfull):
    cpb = 4

    def body(u_ref, vs_ref, lg_ref, lb_ref, w_ref, b_ref, o_ref):
        wc, _ = _causal_weights(w_ref)
        head0 = lax.broadcasted_iota(jnp.int32, (CHUNK, 128), 1) < HEAD_DIM
        for ci in range(cpb):
            rows = pl.ds(ci * CHUNK, CHUNK)
            ug, _, _, _, ms = _sgu_chunk_fwd(u_ref[rows, :], vs_ref[rows, :], lg_ref[...], lb_ref[...], wc, b_ref[...], head0)
            o_ref[rows, :] = ug * ms

    tm = cpb * CHUNK
    return _call(
        "sgu_fwd", body, SEQ // tm,
        [_row_spec(tm, SGU_W), _row_spec(tm, SGU_W), _full_spec((1, SGU_W)), _full_spec((1, SGU_W)),
         _full_spec((N_GROUPS, CHUNK, CHUNK)), _full_spec((CHUNK, SGU_W))],
        [_row_spec(tm, SGU_W)], [jax.ShapeDtypeStruct((SEQ, SGU_W), F32)],
        (u, vs, lg, lb, w_sp, bfull))


def out_proj(attn, sgu, x, ga, gs, w_out, gpm, gpf):
    tm = 512

    def body(a_ref, s_ref, x_ref, ga_ref, gs_ref, w_ref, gpm_ref, gpf_ref, mix_ref, y_ref, x2_ref, h2_ref):
        a = a_ref[...]
        s = s_ref[...]
        an = (a * _rms(a) * ga_ref[...]).astype(BF16)
        sn = (s * _rms(s) * gs_ref[...]).astype(BF16)
        mix_ref[:, :ATTN_W] = an
        mix_ref[:, ATTN_W:] = sn
        y = _dot(an, w_ref[:ATTN_W, :]) + _dot(sn, w_ref[ATTN_W:, :])
        y_ref[...] = y
        x2 = x_ref[...] + y * _rms(y) * gpm_ref[...]
        x2_ref[...] = x2
        h2_ref[...] = (x2 * _rms(x2) * gpf_ref[...]).astype(BF16)

    wide = jax.ShapeDtypeStruct((SEQ, D_MODEL), F32)
    wide16 = jax.ShapeDtypeStruct((SEQ, D_MODEL), BF16)
    return _call(
        "out_proj", body, SEQ // tm,
        [_row_spec(tm, ATTN_W), _row_spec(tm, SGU_W), _row_spec(tm, D_MODEL), _full_spec((1, ATTN_W)),
         _full_spec((1, SGU_W)), _weight_spec((D_MODEL, D_MODEL)), _full_spec((1, D_MODEL)), _full_spec((1, D_MODEL))],
        [_row_spec(tm, D_MODEL)] * 4, [wide16, wide, wide, wide16],
        (attn, sgu, x, ga, gs, w_out, gpm, gpf))


def ffn_up(h2, w_gate_t, w_up_t):
    tm = 256

    def body(h_ref, wg_hbm, wu_hbm, g_ref, u_ref, a_ref, wg_ref, wu_ref, sems):
        def run(wait):
            h = h_ref[...]
            if wait:
                wait(0, 0)
            g = _dot_nt(h, wg_ref[...])
            g_ref[...] = g.astype(BF16)
            if wait:
                wait(1, 0)
            u = _dot_nt(h, wu_ref[...])
            u_ref[...] = u.astype(BF16)
            a_ref[...] = (g * jax.nn.sigmoid(g) * u).astype(BF16)

        _with_ffn_weights([wg_hbm, wu_hbm], [wg_ref, wu_ref], sems, FF_WHOLE, run)

    ff = jax.ShapeDtypeStruct((SEQ, D_FF), BF16)
    return _call(
        "ffn_up", body, SEQ // tm, [_row_spec(tm, D_MODEL), ANY, ANY],
        [_row_spec(tm, D_FF)] * 3, [ff, ff, jax.ShapeDtypeStruct((SEQ, D_FF), BF16)],
        (h2, w_gate_t, w_up_t), scratch_shapes=_ffn_weight_scratch(2, FF_WHOLE))


def ffn_down_loss(act, w_down, x2, gpo, target):
    tm = 512

    def body(a_ref, w_hbm, x2_ref, g_ref, t_ref, df_ref, dx3_ref, dg_ref, loss_ref, w_ref, sems):
        def run(wait):
            f = None
            for c, (o, n) in enumerate(FF_SPANS if wait else FF_WHOLE):
                if wait:
                    wait(0, c)
                part = _dot(a_ref[:, o:o + n], w_ref[o:o + n, :])
                f = part if f is None else f + part
            gain = g_ref[...]
            err = x2_ref[...] + f * _rms(f) * gain - t_ref[...]
            dx3 = err * np.float32(1.0 / D_MODEL)
            dx3_ref[...] = dx3
            df, dg = _rms_bwd(f, gain, dx3)
            df_ref[...] = df.astype(BF16)
            loss = jnp.sum(err * err, axis=(0, 1), keepdims=True)
            if wait:
                dg_ref[...] = dg
                loss_ref[...] = loss
            else:
                dg_ref[...] += dg
                loss_ref[...] += loss

        _with_ffn_weights([w_hbm], [w_ref], sems, FF_SPANS, run)

    return _call(
        "ffn_down_loss", body, SEQ // tm,
        [_row_spec(tm, D_FF), ANY, _row_spec(tm, D_MODEL), _full_spec((1, D_MODEL)), _row_spec(tm, D_MODEL)],
        [_row_spec(tm, D_MODEL), _row_spec(tm, D_MODEL), _full_spec((1, D_MODEL)), _full_spec((1, 1))],
        [jax.ShapeDtypeStruct((SEQ, D_MODEL), BF16), jax.ShapeDtypeStruct((SEQ, D_MODEL), F32),
         jax.ShapeDtypeStruct((1, D_MODEL), F32), jax.ShapeDtypeStruct((1, 1), F32)],
        (act, w_down, x2, gpo, target), scratch_shapes=_ffn_weight_scratch(1, FF_SPANS))


def ffn_bwd(df, w_down, gate, up, w_gate_t, w_up_t, x2, gpf, dx3, y, gpm, after=()):
    tm = 256

    def body(df_ref, wd_hbm, g_ref, u_ref, wg_hbm, wu_hbm, x2_ref, gpf_ref, dx3_ref, y_ref, gpm_ref,
             dg_ref, du_ref, dx2_ref, dy_ref, dgpf_ref, dgpm_ref, wd_ref, wg_ref, wu_ref, sems):
        def run(wait):
            if wait:
                wait(0, 0)
            dact = _dot_nt(df_ref[...], wd_ref[...])
            g = g_ref[...].astype(F32)
            s = jax.nn.sigmoid(g)
            dup = (dact * g * s).astype(BF16)
            dgate = (dact * u_ref[...].astype(F32) * (s * (1.0 + g * (1.0 - s)))).astype(BF16)
            du_ref[...] = dup
            dg_ref[...] = dgate
            if wait:
                wait(1, 0)
                wait(2, 0)
            dh2 = _dot(dgate, wg_ref[...]) + _dot(dup, wu_ref[...])
            dz, dgpf = _rms_bwd(x2_ref[...], gpf_ref[...], dh2)
            dx2 = dx3_ref[...] + dz
            dx2_ref[...] = dx2
            dy, dgpm = _rms_bwd(y_ref[...], gpm_ref[...], dx2)
            dy_ref[...] = dy.astype(BF16)
            if wait:
                dgpf_ref[...] = dgpf
                dgpm_ref[...] = dgpm
            else:
                dgpf_ref[...] += dgpf
                dgpm_ref[...] += dgpm

        _with_ffn_weights([wd_hbm, wg_hbm, wu_hbm], [wd_ref, wg_ref, wu_ref], sems, FF_WHOLE, run)

    vec = jax.ShapeDtypeStruct((1, D_MODEL), F32)
    ff16 = jax.ShapeDtypeStruct((SEQ, D_FF), BF16)
    return _call(
        "ffn_bwd", body, SEQ // tm,
        [_row_spec(tm, D_MODEL), ANY, _row_spec(tm, D_FF), _row_spec(tm, D_FF), ANY, ANY, _row_spec(tm, D_MODEL),
         _full_spec((1, D_MODEL)), _row_spec(tm, D_MODEL), _row_spec(tm, D_MODEL), _full_spec((1, D_MODEL))],
        [_row_spec(tm, D_FF), _row_spec(tm, D_FF), _row_spec(tm, D_MODEL), _row_spec(tm, D_MODEL),
         _full_spec((1, D_MODEL)), _full_spec((1, D_MODEL))],
        [ff16, ff16, jax.ShapeDtypeStruct((SEQ, D_MODEL), F32), jax.ShapeDtypeStruct((SEQ, D_MODEL), BF16), vec, vec],
        (df, w_down, gate, up, w_gate_t, w_up_t, x2, gpf, dx3, y, gpm), scratch_shapes=_ffn_weight_scratch(3, FF_WHOLE), after=after)


def weight_grads(name, lhs, b, after=()):
    m, n, k = lhs[0].shape[1], b.shape[1], len(lhs)
    tr = 256

    def body(*refs):
        for a_ref, o_ref in zip(refs[:k], refs[k + 1:]):
            o_ref[...] = _dot_tn(a_ref[...], refs[k][...]).astype(BF16)

    outs = _call(
        name, body, m // tr, [pl.BlockSpec((SEQ, tr), lambda i: (0, i))] * k + [_weight_spec((SEQ, n))],
        [_row_spec(tr, n)] * k, [jax.ShapeDtypeStruct((m, n), BF16)] * k, (*lhs, b), after=after)
    return [out.reshape(N_DEV, m // N_DEV, n) for out in outs]


def weight_grad(name, a, b, after=()):
    return weight_grads(name, [a], b, after)[0]


def weight_grad_of_parts(name, parts, b, after=()):
    p, n, k = parts[0].shape[1], b.shape[1], len(parts)
    tr = 512
    per = p // tr

    def body(*refs):
        tile = pl.program_id(0)
        for j in range(k):
            @pl.when(tile // per == j)
            def _(j=j):
                refs[k + 1][...] = _dot_tn(refs[j][...].astype(BF16), refs[k][...]).astype(BF16)

    def part_spec(j):
        return pl.BlockSpec((SEQ, tr), lambda i: (0, jnp.clip(i - per * j, 0, per - 1)))

    (out,) = _call(
        name, body, k * per, [part_spec(j) for j in range(k)] + [_weight_spec((SEQ, n))],
        [_row_spec(tr, n)], [jax.ShapeDtypeStruct((k * p, n), BF16)], (*parts, b), after=after)
    return out.reshape(N_DEV, k * p // N_DEV, n)


def mix_bwd(dy, w_out, attn, sgu, ga, gs, after=()):
    tm = 512
    n_steps = SEQ // tm

    def body(dy_ref, w_ref, a_ref, s_ref, ga_ref, gs_ref, ds_ref, dga_ref, dgs_ref, da_ref, scratch, sems):
        dy = dy_ref[...]
        da, dga = _rms_bwd(a_ref[...], ga_ref[...], _dot_nt(dy, w_ref[:ATTN_W, :]))
        ds, dgs = _rms_bwd(s_ref[...], gs_ref[...], _dot_nt(dy, w_ref[ATTN_W:, :]))
        ds_ref[...] = ds
        _to_residue_rows([da], [da_ref], scratch, sems, tm, n_steps)

        @pl.when(pl.program_id(0) == 0)
        def _():
            dga_ref[...] = jnp.zeros_like(dga_ref)
            dgs_ref[...] = jnp.zeros_like(dgs_ref)

        dga_ref[...] += dga
        dgs_ref[...] += dgs

    half = jax.ShapeDtypeStruct((SEQ, 512), F32)
    vec = jax.ShapeDtypeStruct((1, 512), F32)
    return _call(
        "mix_bwd", body, n_steps,
        [_row_spec(tm, D_MODEL), _weight_spec((D_MODEL, D_MODEL)), _row_spec(tm, 512), _row_spec(tm, 512),
         _full_spec((1, 512)), _full_spec((1, 512))],
        [_row_spec(tm, 512), _full_spec((1, 512)), _full_spec((1, 512)), ANY],
        [half, vec, vec, half], (dy, w_out, attn, sgu, ga, gs), scratch_shapes=_residue_scratch(1, tm, ATTN_W), after=after)


def sgu_bwd(u, vs, dsgu, lg, lb, w_sp, bfull):
    cpb = 4

    def body(u_ref, vs_ref, d_ref, lg_ref, lb_ref, w_ref, b_ref, du_ref, dvs_ref, dlg_ref, dlb_ref, dw_ref, db_ref):
        wc, causal = _causal_weights(w_ref)
        head0 = lax.broadcasted_iota(jnp.int32, (CHUNK, 128), 1) < HEAD_DIM
        lg = lg_ref[...]

        @pl.when(pl.program_id(0) == 0)
        def _():
            dlg_ref[...] = jnp.zeros_like(dlg_ref)
            dlb_ref[...] = jnp.zeros_like(dlb_ref)
            dw_ref[...] = jnp.zeros_like(dw_ref)
            db_ref[...] = jnp.zeros_like(db_ref)

        for ci in range(cpb):
            rows = pl.ds(ci * CHUNK, CHUNK)
            u = u_ref[rows, :]
            vs = vs_ref[rows, :]
            d = d_ref[rows, :]
            ug, xhat, rstd, vn, ms = _sgu_chunk_fwd(u, vs, lg, lb_ref[...], wc, b_ref[...], head0)
            du_ref[rows, :] = (d * ms * _gelu_grad(u)).astype(BF16)
            dms = d * ug
            db_ref[...] += dms
            dvn = []
            for gp in range(SGU_W // 128):
                dmp = dms[:, gp * 128:(gp + 1) * 128]
                dm0 = jnp.where(head0, dmp, 0.0).astype(BF16)
                dm1 = jnp.where(head0, 0.0, dmp).astype(BF16)
                vp = vn[:, gp * 128:(gp + 1) * 128].astype(BF16)
                dw_ref[2 * gp] += _dot_nt(dm0, vp)
                dw_ref[2 * gp + 1] += _dot_nt(dm1, vp)
                dvn.append(_dot_tn(wc[2 * gp], dm0) + _dot_tn(wc[2 * gp + 1], dm1))
            dvn = jnp.concatenate(dvn, axis=1)
            dlg_ref[...] += jnp.sum(dvn * xhat, axis=0, keepdims=True)
            dlb_ref[...] += jnp.sum(dvn, axis=0, keepdims=True)
            dxh = dvn * lg
            dvg = rstd * (dxh - jnp.mean(dxh, axis=-1, keepdims=True) - xhat * jnp.mean(dxh * xhat, axis=-1, keepdims=True))
            dvs_ref[rows, :] = (dvg * _gelu_grad(vs)).astype(BF16)

        @pl.when(pl.program_id(0) == pl.num_programs(0) - 1)
        def _():
            for g in range(N_GROUPS):
                dw_ref[g] = jnp.where(causal, dw_ref[g], 0.0)

    tm = cpb * CHUNK
    half16 = jax.ShapeDtypeStruct((SEQ, SGU_W), BF16)
    vec = jax.ShapeDtypeStruct((1, SGU_W), F32)
    return _call(
        "sgu_bwd", body, SEQ // tm,
        [_row_spec(tm, SGU_W)] * 3 + [_full_spec((1, SGU_W)), _full_spec((1, SGU_W)),
                                      _full_spec((N_GROUPS, CHUNK, CHUNK)), _full_spec((CHUNK, SGU_W))],
        [_row_spec(tm, SGU_W), _row_spec(tm, SGU_W), _full_spec((1, SGU_W)), _full_spec((1, SGU_W)),
         _full_spec((N_GROUPS, CHUNK, CHUNK)), _full_spec((CHUNK, SGU_W))],
        [half16, half16, vec, vec, jax.ShapeDtypeStruct((N_GROUPS, CHUNK, CHUNK), F32),
         jax.ShapeDtypeStruct((CHUNK, SGU_W), F32)],
        (u, vs, dsgu, lg, lb, w_sp, bfull))


def attn_bwd(q, k, v, o, lse, do, pos_col, rot):
    n_steps = ATTN_W // 128

    def body(q_ref, k_ref, v_ref, o_ref, lse_ref, do_ref, pos_ref, invf_ref, ma_ref, mb_ref,
             dq_ref, dk_ref, dv_ref, dqa_ref, dka_ref, dva_ref, dlt_ref, rot_ref, out_ref, sems):
        step = pl.program_id(0)
        slot = step % 2

        def back(at_step, s):
            lanes = pl.ds(pl.multiple_of(at_step * 128, 128), 128)
            return [pltpu.make_async_copy(out_ref.at[s, a, pl.ds(b * (SEQ // RES), SEQ // RES), :], nat.at[:, b, lanes],
                                          sems.at[s, a, b]) for a, nat in enumerate((dq_ref, dk_ref, dv_ref)) for b in range(RES)]

        dqa_ref[...] = jnp.zeros_like(dqa_ref)
        dka_ref[...] = jnp.zeros_like(dka_ref)
        dva_ref[...] = jnp.zeros_like(dva_ref)

        def delta(i, carry):
            rows = pl.ds(pl.multiple_of(i * 256, 256), 256)
            prod = do_ref[rows, :] * o_ref[rows, :]
            h0 = lax.broadcasted_iota(jnp.int32, (256, 128), 1) < HEAD_DIM
            d0 = jnp.sum(jnp.where(h0, prod, 0.0), axis=-1, keepdims=True)
            d1 = jnp.sum(jnp.where(h0, 0.0, prod), axis=-1, keepdims=True)
            dlt_ref[rows, :] = jnp.where(h0, d0, d1)
            return carry

        lax.fori_loop(0, SEQ // 256, delta, 0)

        def add_rows(ref, slices, val):
            at = 0
            for start, size in slices:
                ref[pl.ds(start, size), :] += val[at:at + size]
                at += size

        def group(p, masks, blocks):
            head0, mask1, mask2 = masks
            heads = (head0, jnp.logical_not(head0))
            keys = [rows if prev is None else prev + rows for rows, prev in blocks]
            mask = [mask1 if prev is None else mask2 for _, prev in blocks]
            kk = [_load_rows(k_ref, ks).astype(BF16) for ks in keys]
            vv = [_load_rows(v_ref, ks).astype(BF16) for ks in keys]
            qb = [_load_rows(q_ref, rows) for rows, _ in blocks]
            dob = [_load_rows(do_ref, rows) for rows, _ in blocks]
            lse_b = [_load_rows(lse_ref, rows) for rows, _ in blocks]
            dlt_b = [_load_rows(dlt_ref, rows) for rows, _ in blocks]
            chains = [(g, h) for g in range(len(blocks)) for h in range(2)]
            qm = [jnp.where(heads[h], qb[g], 0.0).astype(BF16) for g, h in chains]
            dom = [jnp.where(heads[h], dob[g], 0.0).astype(BF16) for g, h in chains]
            s = [_dot_nt(qm[c], kk[g]) for c, (g, h) in enumerate(chains)]
            dp = [_dot_nt(dom[c], vv[g]) for c, (g, h) in enumerate(chains)]
            pr = [jnp.where(mask[g], jnp.exp(s[c] - lse_b[g][:, h * HEAD_DIM:h * HEAD_DIM + 1]), 0.0)
                  for c, (g, h) in enumerate(chains)]
            ds = [(pr[c] * (dp[c] - dlt_b[g][:, h * HEAD_DIM:h * HEAD_DIM + 1])).astype(BF16)
                  for c, (g, h) in enumerate(chains)]
            dv = [_dot_tn(pr[c].astype(BF16), dom[c]) for c in range(len(chains))]
            dk = [_dot_tn(ds[c], qm[c]) for c in range(len(chains))]
            dq = [_dot(ds[c], kk[g]) for c, (g, h) in enumerate(chains)]
            for g, (rows, _) in enumerate(blocks):
                add_rows(dqa_ref, rows, jnp.where(head0, dq[2 * g], dq[2 * g + 1]))
                add_rows(dka_ref, keys[g], dk[2 * g] + dk[2 * g + 1])
                add_rows(dva_ref, keys[g], dv[2 * g] + dv[2 * g + 1])

        _for_each_group(group)

        @pl.when(pl.program_id(0) == 0)
        def _():
            def tables(i, carry):
                rows = pl.ds(pl.multiple_of(i * 256, 256), 256)
                c, sa, sb = _rot_tables(pos_ref[rows, :], invf_ref[...], ma_ref[...], mb_ref[...])
                rot_ref[0, rows, :] = c
                rot_ref[1, rows, :] = sa
                rot_ref[2, rows, :] = sb
                return carry

            lax.fori_loop(0, SEQ // 256, tables, 0)

        @pl.when(step >= 2)
        def _():
            for cp in back(step - 2, slot):
                cp.wait()

        def finish(i, carry):
            rows = pl.ds(pl.multiple_of(i * 256, 256), 256)
            c, sa, sb = rot_ref[0, rows, :], rot_ref[1, rows, :], rot_ref[2, rows, :]
            out_ref[slot, 0, rows, :] = _rot_t(dqa_ref[rows, :] * Q_SCALE, c, sa, sb)
            out_ref[slot, 1, rows, :] = _rot_t(dka_ref[rows, :], c, sa, sb)
            out_ref[slot, 2, rows, :] = dva_ref[rows, :]
            return carry

        lax.fori_loop(0, SEQ // 256, finish, 0)
        for cp in back(step, slot):
            cp.start()

        @pl.when(step == n_steps - 1)
        def _():
            for cp in back(step - 1, 1 - slot) + back(step, slot):
                cp.wait()

    slab = pl.BlockSpec((SEQ, 128), lambda i: (0, i))
    out = jax.ShapeDtypeStruct((SEQ // RES, RES, ATTN_W), F32)
    acc = pltpu.VMEM((SEQ, 128), F32)
    outs = _call(
        "attn_bwd", body, n_steps,
        [slab] * 6 + [_full_spec((SEQ, 1)), _full_spec((1, 128)), _full_spec((1, 128)), _full_spec((1, 128))],
        [ANY] * 3, [out, out, out], (q, k, v, o, lse, do, pos_col, *rot),
        scratch_shapes=[acc, acc, acc, acc, pltpu.VMEM((3, SEQ, 128), F32), pltpu.VMEM((2, 3, SEQ, 128), F32),
                        pltpu.SemaphoreType.DMA((2, 3, RES))])
    return [t.reshape(SEQ, ATTN_W) for t in outs]


def in_bwd(dproj_parts, w_in_t, x, g1, dx2, after=()):
    tm = 512
    k = len(dproj_parts)

    def body(*refs):
        w_ref, x_ref, g_ref, dx2_ref, dx_ref, dg_ref = refs[k:]
        dh1 = _dot(refs[0][...].astype(BF16), w_ref[0:512, :])
        for j in range(1, k):
            dh1 = dh1 + _dot(refs[j][...].astype(BF16), w_ref[512 * j:512 * (j + 1), :])
        dz, dg = _rms_bwd(x_ref[...], g_ref[...], dh1)
        dx_ref[...] = dx2_ref[...] + dz

        @pl.when(pl.program_id(0) == 0)
        def _():
            dg_ref[...] = jnp.zeros_like(dg_ref)

        dg_ref[...] += dg

    return _call(
        "in_bwd", body, SEQ // tm,
        [_row_spec(tm, 512)] * k + [_weight_spec((IN_W, D_MODEL)), _row_spec(tm, D_MODEL), _full_spec((1, D_MODEL)),
                                    _row_spec(tm, D_MODEL)],
        [_row_spec(tm, D_MODEL), _full_spec((1, D_MODEL))],
        [jax.ShapeDtypeStruct((SEQ, D_MODEL), F32), jax.ShapeDtypeStruct((1, D_MODEL), F32)],
        (*dproj_parts, w_in_t, x, g1, dx2), after=after)


def _coords():
    return lax.axis_index("x"), lax.axis_index("y"), lax.axis_index("c")


class Exchange:
    def __init__(self, srcs, bufs, new_shapes, n_sems, make):
        self.srcs, self.bufs, self.new_shapes, self.n_sems, self.make = list(srcs), list(bufs), list(new_shapes), n_sems, make


def _call(name, body, n_steps, in_specs, out_specs, out_shape, args, scratch_shapes=(), after=()):
    n_in = len(args)

    def wrapped(*refs):
        body(*refs[:n_in], *refs[n_in + len(after):])

    return list(pl.pallas_call(
        wrapped, name=name, grid=(n_steps,), in_specs=list(in_specs) + [ANY] * len(after), out_specs=list(out_specs),
        out_shape=list(out_shape), scratch_shapes=list(scratch_shapes), compiler_params=_params(),
    )(*args, *after))


GATHER_SEMS = 8


def gather(bufs):
    n = len(bufs)

    def make(src_refs, buf_refs, new_refs, send_sems, recv_sems):
        x, y, c = _coords()
        me, sibling = (x, y, c), (x, y, 1 - c)
        over_x, over_y, across = (1 - x, y), (x, 1 - y), (1 - x, 1 - y)

        def copy(a, k, block, to, half=None):
            r = buf_refs[a].shape[0] // N_DEV
            lo, size = (0, r) if half is None else (half * (r // 2), r // 2)
            rows = buf_refs[a].at[pl.ds((4 * block[0] + 2 * block[1] + block[2]) * r + lo, size), :]
            return pltpu.make_async_remote_copy(
                src_ref=rows, dst_ref=rows, send_sem=send_sems.at[GATHER_SEMS * a + k],
                recv_sem=recv_sems.at[GATHER_SEMS * a + k], device_id=to, device_id_type=MESH)

        every = range(n)
        out = ([copy(a, 0, me, sibling) for a in every] + [copy(a, 1, me, (*over_x, c)) for a in every]
               + [copy(a, 2, me, (*over_y, c)) for a in every])
        near_in = [copy(a, 1, (*over_x, c), me) for a in every] + [copy(a, 2, (*over_y, c), me) for a in every]
        relay = ([copy(a, 3, (*over_x, c), (*over_y, c), half=0) for a in every]
                 + [copy(a, 4, (*over_y, c), (*over_x, c), half=1) for a in every])
        near_on = [copy(a, 5, (*over_x, c), sibling) for a in every] + [copy(a, 6, (*over_y, c), sibling) for a in every]
        relay_in = ([copy(a, 3, (*across, c), me, half=0) for a in every]
                    + [copy(a, 4, (*across, c), me, half=1) for a in every])
        far_on = [copy(a, 7, (*across, c), sibling) for a in every]
        from_core = ([copy(a, 0, sibling, me) for a in every] + [copy(a, 5, (*over_x, 1 - c), me) for a in every]
                     + [copy(a, 6, (*over_y, 1 - c), me) for a in every] + [copy(a, 7, (*across, 1 - c), me) for a in every])
        stages = [([], out), (near_in, relay + near_on), (relay_in, far_on)]
        return stages, out + relay + near_on + far_on, from_core

    return Exchange([], bufs, [], GATHER_SEMS * n, make)


TO_GATHER = (1, lambda x, y, c: [(x, y, 1 - c), (1 - x, y, c), (x, 1 - y, c)])
TO_SIBLING = (2, lambda x, y, c: [(x, y, 1 - c)])
TO_CHIPS = (3, lambda x, y, c: [(1 - x, y, c), (x, 1 - y, c), (1 - x, 1 - y, c)])
TO_ALL = (4, lambda x, y, c: [(x ^ (m >> 2), y ^ ((m >> 1) & 1), c ^ (m & 1)) for m in range(1, N_DEV)])


def by_sequencer(name, exchanges, who):
    collective_id, peers_of = who
    hbm = pltpu.MemorySpace.HBM
    refs = [([jax.new_ref(a, memory_space=hbm) for a in ex.srcs], [jax.new_ref(a, memory_space=hbm) for a in ex.bufs],
             [jax.empty_ref(s, memory_space=hbm) for s in ex.new_shapes]) for ex in exchanges]
    sems = []
    for ex in exchanges:
        sems += [pltpu.SemaphoreType.DMA((ex.n_sems,)), pltpu.SemaphoreType.DMA((ex.n_sems,))]

    @pl.kernel(mesh=plsc.ScalarSubcoreMesh(axis_name="sequencer", num_cores=1), name=name, scratch_types=tuple(sems),
               compiler_params=pltpu.CompilerParams(collective_id=collective_id))
    def launch(*sem_refs):
        peers = peers_of(*_coords())
        barrier = pltpu.get_barrier_semaphore()
        for peer in peers:
            pl.semaphore_signal(barrier, inc=1, device_id=peer, device_id_type=MESH)
        pl.semaphore_wait(barrier, len(peers))

        made = [ex.make(*refs[k], sem_refs[2 * k], sem_refs[2 * k + 1]) for k, ex in enumerate(exchanges)]
        for stage in range(max(len(stages) for stages, _, _ in made)):
            for stages, _, _ in made:
                if stage < len(stages):
                    arrivals, starts = stages[stage]
                    for cp in arrivals:
                        cp.wait_recv()
                    for cp in starts:
                        cp.start()
        for _, sends, arrivals in made:
            for cp in arrivals:
                cp.wait_recv()
            for cp in sends:
                cp.wait_send()

    launch()
    return [([ref[...] for ref in bufs], [ref[...] for ref in news]) for _, bufs, news in refs]


def place_shards(name, shards, dev):
    n = len(shards)

    def body(dev_ref, *refs):
        for a in range(n):
            refs[n + a][...] = refs[a][...].astype(BF16)

    spec = pltpu.PrefetchScalarGridSpec(
        num_scalar_prefetch=1, grid=(1,),
        in_specs=[pl.BlockSpec(s.shape, lambda i, dev_ref: (0, 0)) for s in shards],
        out_specs=[pl.BlockSpec(s.shape, lambda i, dev_ref: (dev_ref[0], 0)) for s in shards])
    return pl.pallas_call(
        body, name=name, grid_spec=spec,
        out_shape=[jax.ShapeDtypeStruct((N_DEV * s.shape[0], s.shape[1]), BF16) for s in shards],
        compiler_params=_params(),
    )(dev, *shards)


def _swap(copies_of):
    def make(src_refs, buf_refs, new_refs, send_sems, recv_sems):
        copies = copies_of(src_refs, new_refs, send_sems, recv_sems)
        return [([], copies)], copies, copies

    return make


def to_sibling(grads):
    def copies_of(src_refs, new_refs, send_sems, recv_sems):
        x, y, c = _coords()
        return [pltpu.make_async_remote_copy(
            src_ref=src_refs[a].at[2 * xy + 1 - c], dst_ref=new_refs[a].at[xy], send_sem=send_sems.at[4 * a + xy],
            recv_sem=recv_sems.at[4 * a + xy], device_id=(x, y, 1 - c), device_id_type=MESH)
            for a in range(len(src_refs)) for xy in range(4)]

    return Exchange(grads, [], [jax.ShapeDtypeStruct((4,) + g.shape[1:], g.dtype) for g in grads], 4 * len(grads),
                    _swap(copies_of))


def to_chips(parts):
    def copies_of(src_refs, new_refs, send_sems, recv_sems):
        x, y, c = _coords()
        chips = [(1 - x, y), (x, 1 - y), (1 - x, 1 - y)]
        return [pltpu.make_async_remote_copy(
            src_ref=src_refs[a].at[2 * px + py], dst_ref=new_refs[a].at[2 * x + y], send_sem=send_sems.at[3 * a + j],
            recv_sem=recv_sems.at[3 * a + j], device_id=(px, py, c), device_id_type=MESH)
            for a in range(len(src_refs)) for j, (px, py) in enumerate(chips)]

    return Exchange(parts, [], [jax.ShapeDtypeStruct(p.shape, p.dtype) for p in parts], 3 * len(parts), _swap(copies_of))


def to_owners(grad):
    def copies_of(src_refs, new_refs, send_sems, recv_sems):
        x, y, c = _coords()
        copies = []
        for m in range(1, N_DEV):
            px, py, pc = x ^ (m >> 2), y ^ ((m >> 1) & 1), c ^ (m & 1)
            copies.append(pltpu.make_async_remote_copy(
                src_ref=src_refs[0].at[4 * px + 2 * py + pc], dst_ref=new_refs[0].at[4 * x + 2 * y + c],
                send_sem=send_sems.at[m - 1], recv_sem=recv_sems.at[m - 1], device_id=(px, py, pc), device_id_type=MESH))
        return copies

    return Exchange([grad], [], [jax.ShapeDtypeStruct(grad.shape, grad.dtype)], N_DEV - 1, _swap(copies_of))


def to_everyone(vec):
    def copies_of(src_refs, new_refs, send_sems, recv_sems):
        x, y, c = _coords()
        copies = []
        for m in range(1, N_DEV):
            px, py, pc = x ^ (m >> 2), y ^ ((m >> 1) & 1), c ^ (m & 1)
            copies.append(pltpu.make_async_remote_copy(
                src_ref=src_refs[0], dst_ref=new_refs[0].at[4 * x + 2 * y + c],
                send_sem=send_sems.at[m - 1], recv_sem=recv_sems.at[m - 1], device_id=(px, py, pc), device_id_type=MESH))
        return copies

    return Exchange([vec], [], [jax.ShapeDtypeStruct((N_DEV,) + vec.shape, vec.dtype)], N_DEV - 1, _swap(copies_of))


def sum_cores(name, grads, others, core, after=()):
    k = len(grads)

    def body(core_ref, *refs):
        for j in range(k):
            out_ref = refs[2 * k + len(after) + j]
            out_ref[...] = (refs[j][:, 0].astype(F32) + refs[k + j][...].astype(F32)).astype(out_ref.dtype)

    mine = [pl.BlockSpec((2, 1) + o.shape[1:], lambda i, core_ref: (i, core_ref[0], 0, 0)) for o in others]
    theirs = [pl.BlockSpec((2,) + o.shape[1:], lambda i, core_ref: (i, 0, 0)) for o in others]
    return pl.pallas_call(
        body, name=name,
        grid_spec=pltpu.PrefetchScalarGridSpec(
            num_scalar_prefetch=1, grid=(2,), in_specs=mine + theirs + [ANY] * len(after), out_specs=theirs),
        out_shape=[jax.ShapeDtypeStruct(o.shape, o.dtype) for o in others],
        compiler_params=_params(),
    )(core, *[g.reshape((4, 2) + g.shape[1:]) for g in grads], *others, *after)


def sum_owned(name, grad, others, dev_ids, after=()):
    _, r, w = grad.shape

    def body(ids_ref, *refs):
        acc = refs[0][0]
        for k in range(1, N_DEV):
            acc = acc + refs[k][0]
        refs[-1][...] = acc

    def pick(k):
        return pl.BlockSpec((1, r, w), lambda i, ids_ref: (ids_ref[k], 0, 0))

    return pl.pallas_call(
        body, name=name,
        grid_spec=pltpu.PrefetchScalarGridSpec(
            num_scalar_prefetch=1, grid=(1,), in_specs=[pick(k) for k in range(N_DEV)] + [ANY] * len(after),
            out_specs=pl.BlockSpec((r, w), lambda i, ids_ref: (ids_ref[0], 0))),
        out_shape=jax.ShapeDtypeStruct((N_DEV * r, w), F32),
        compiler_params=_params(),
    )(dev_ids, grad, *([others] * (N_DEV - 1)), *after)


def _adamw_update(w, g, m, v):
    nm = ADAM_B1 * m + np.float32(1.0 - ADAM_B1) * g
    nv = ADAM_B2 * v + np.float32(1.0 - ADAM_B2) * (g * g)
    m_hat = nm / np.float32(1.0 - ADAM_B1 ** ADAM_STEP)
    v_hat = nv / np.float32(1.0 - ADAM_B2 ** ADAM_STEP)
    return -ADAM_LR * (m_hat / (jnp.sqrt(v_hat) + ADAM_EPS) + ADAM_WD * w), nm, nv


def adamw_of_sums(name, parts, others, chip_ids, ws, ms, vs, after):
    n = len(parts)
    halves = 2

    def body(ids_ref, *refs):
        outs = refs[7 * n + 1:]
        for j in range(n):
            p_ref, a_ref, b_ref, c_ref, w_ref, m_ref, v_ref = refs[7 * j:7 * j + 7]
            g = ((p_ref[0].astype(F32) + a_ref[0].astype(F32)) + b_ref[0].astype(F32)) + c_ref[0].astype(F32)
            outs[4 * j][...] = g
            outs[4 * j + 1][...], outs[4 * j + 2][...], outs[4 * j + 3][...] = _adamw_update(w_ref[...], g, m_ref[...], v_ref[...])

    in_specs, out_specs, out_shape, operands = [], [], [], []
    for part, other, w, m, v in zip(parts, others, ws, ms, vs):
        _, r, wd = part.shape
        rows = r // halves
        whole = pl.BlockSpec((rows, wd), lambda i, ids_ref: (i, 0))
        in_specs += [pl.BlockSpec((1, rows, wd), lambda i, ids_ref, k=k: (ids_ref[k], i, 0)) for k in range(4)] + [whole] * 3
        out_specs += [whole] * 4
        out_shape += [jax.ShapeDtypeStruct((r, wd), F32)] * 4
        operands += [part, other, other, other, w, m, v]
    outs = pl.pallas_call(
        body, name=name,
        grid_spec=pltpu.PrefetchScalarGridSpec(
            num_scalar_prefetch=1, grid=(halves,), in_specs=in_specs + [ANY], out_specs=out_specs),
        out_shape=out_shape,
        compiler_params=_params(),
    )(chip_ids, *operands, after)
    return [tuple(outs[4 * j:4 * j + 4]) for j in range(n)]


def pack_small(parts):
    names = [name for name, _ in SMALL if name in parts]
    operands = [parts[name] for name in names]
    first_row, at = {}, 0
    for name, size in SMALL:
        first_row[name] = at // 128
        at += size
    sizes = dict(SMALL)

    def body(*refs):
        out_ref = refs[-1]
        out_ref[...] = jnp.zeros_like(out_ref)
        for name, ref in zip(names, refs):
            row = first_row[name]
            if name == "loss_sum":
                lane0 = lax.broadcasted_iota(jnp.int32, (1, 128), 1) == 0
                out_ref[row:row + 1, :] = jnp.where(lane0, ref[...], 0.0)
            else:
                rows = sizes[name] // 128
                out_ref[row:row + rows, :] = ref[...].reshape(rows, 128)

    vmem = pl.BlockSpec(memory_space=pltpu.VMEM)
    return pl.pallas_call(
        body, name="pack_small", in_specs=[vmem] * len(names), out_specs=vmem,
        out_shape=jax.ShapeDtypeStruct((SMALL_ROWS, 128), F32), compiler_params=_params(()),
    )(*operands)


LATE = "pre_mix_norm"


def adamw_small(packed_g, late_parts, ws, ms, vs):
    names = [name for name, _ in SMALL if name != "loss_sum"]
    k = len(names)
    shapes = [ws[name].shape[1:] if ws[name].ndim > 2 else ws[name].shape for name in names]
    first_row, at = [], 0
    for name, size in SMALL:
        first_row.append(at // 128)
        at += size

    def body(g_ref, late_ref, *refs):
        w_refs, m_refs, v_refs, outs = refs[:k], refs[k:2 * k], refs[2 * k:3 * k], refs[3 * k:]
        for i, (name, size) in enumerate(SMALL[:k]):
            if name == LATE:
                g = late_ref[0]
                for j in range(1, N_DEV):
                    g = g + late_ref[j]
            else:
                g = g_ref[first_row[i]:first_row[i] + size // 128, :].reshape(shapes[i])
            outs[i][...] = g
            outs[k + i][...], outs[2 * k + i][...], outs[3 * k + i][...] = _adamw_update(
                w_refs[i][...], g, m_refs[i][...], v_refs[i][...])
        outs[4 * k][...] = g_ref[first_row[k]:first_row[k] + 1, 0:1]

    vmem = pl.BlockSpec(memory_space=pltpu.VMEM)
    operands = [t[name].reshape(shape) for t in (ws, ms, vs) for name, shape in zip(names, shapes)]
    outs = pl.pallas_call(
        body, name="adamw_small", in_specs=[vmem] * (2 + 3 * k), out_specs=[vmem] * (4 * k + 1),
        out_shape=[jax.ShapeDtypeStruct(shape, F32) for _ in range(4) for shape in shapes] + [jax.ShapeDtypeStruct((1, 1), F32)],
        compiler_params=_params(()),
    )(packed_g, late_parts, *operands)
    tables = [{name: outs[j * k + i].reshape(ws[name].shape) for i, name in enumerate(names)} for j in range(4)]
    return (*tables, outs[4 * k])


def kernel(x, positions, pre_mix_norm, w_in, sgu_ln_gain, sgu_ln_bias, sgu_w_spatial, sgu_b_spatial, attn_out_norm, sgu_out_norm, w_out, post_mix_norm, pre_ffn_norm, w_gate, w_up, w_down, post_ffn_norm, loss_target, m_pre_mix_norm, m_w_in, m_sgu_ln_gain, m_sgu_ln_bias, m_sgu_w_spatial, m_sgu_b_spatial, m_attn_out_norm, m_sgu_out_norm, m_w_out, m_post_mix_norm, m_pre_ffn_norm, m_w_gate, m_w_up, m_w_down, m_post_ffn_norm, v_pre_mix_norm, v_w_in, v_sgu_ln_gain, v_sgu_ln_bias, v_sgu_w_spatial, v_sgu_b_spatial, v_attn_out_norm, v_sgu_out_norm, v_w_out, v_post_mix_norm, v_pre_ffn_norm, v_w_gate, v_w_up, v_w_down, v_post_ffn_norm):
    small_w = dict(pre_mix_norm=pre_mix_norm, sgu_ln_gain=sgu_ln_gain, sgu_ln_bias=sgu_ln_bias, sgu_w_spatial=sgu_w_spatial,
                   sgu_b_spatial=sgu_b_spatial, attn_out_norm=attn_out_norm, sgu_out_norm=sgu_out_norm,
                   post_mix_norm=post_mix_norm, pre_ffn_norm=pre_ffn_norm, post_ffn_norm=post_ffn_norm)
    small_m = dict(pre_mix_norm=m_pre_mix_norm, sgu_ln_gain=m_sgu_ln_gain, sgu_ln_bias=m_sgu_ln_bias, sgu_w_spatial=m_sgu_w_spatial,
                   sgu_b_spatial=m_sgu_b_spatial, attn_out_norm=m_attn_out_norm, sgu_out_norm=m_sgu_out_norm,
                   post_mix_norm=m_post_mix_norm, pre_ffn_norm=m_pre_ffn_norm, post_ffn_norm=m_post_ffn_norm)
    small_v = dict(pre_mix_norm=v_pre_mix_norm, sgu_ln_gain=v_sgu_ln_gain, sgu_ln_bias=v_sgu_ln_bias, sgu_w_spatial=v_sgu_w_spatial,
                   sgu_b_spatial=v_sgu_b_spatial, attn_out_norm=v_attn_out_norm, sgu_out_norm=v_sgu_out_norm,
                   post_mix_norm=v_post_mix_norm, pre_ffn_norm=v_pre_ffn_norm, post_ffn_norm=v_post_ffn_norm)

    x2d = x[0]
    target = loss_target[0]
    pos_col = positions.reshape(SEQ, 1)
    rot = _rot_consts()
    w_sp = sgu_w_spatial[0]
    bfull = jnp.repeat(sgu_b_spatial[0].T, HEAD_DIM, axis=1)

    x_i, y_i, c_i = (lax.axis_index(a).astype(jnp.int32) for a in MESH_AXES)
    dev = 4 * x_i + 2 * y_i + c_i
    core = c_i.reshape(1)
    chip = 2 * x_i + y_i
    chip_ids = jnp.stack([chip, chip ^ 1, chip ^ 2, chip ^ 3])
    dev_ids = jnp.stack([dev ^ m for m in range(N_DEV)])

    def gathered(name, bufs):
        return by_sequencer(name, [gather(bufs)], TO_GATHER)[0][0]

    def from_sibling(name, grads):
        return by_sequencer(name, [to_sibling(grads)], TO_SIBLING)[0][1]

    def from_chips(name, parts):
        return by_sequencer(name, [to_chips(parts)], TO_CHIPS)[0][1]

    (w_in_t,) = place_shards("place_w_in", [w_in[0].T], dev.reshape(1))
    (w_in_t,) = gathered("gather_w_in", [w_in_t])
    w_gate_t, w_up_t, w_out_f, w_down_f = place_shards(
        "place_weights", [w_gate[0].T, w_up[0].T, w_out[0], w_down[0]], dev.reshape(1))
    (w_out_f,) = gathered("gather_w_out", [w_out_f])
    w_gate_t, w_up_t = gathered("gather_w_gate_up", [w_gate_t, w_up_t])
    (w_down_f,) = gathered("gather_w_down", [w_down_f])

    h1, u, vs, q, k, v = in_proj(x2d, pos_col, pre_mix_norm, w_in_t, rot)
    attn_r, lse, attn = attn_fwd(q, k, v)
    (sgu,) = sgu_fwd(u, vs, sgu_ln_gain, sgu_ln_bias, w_sp, bfull)
    mix, y, x2, h2 = out_proj(attn, sgu, x2d, attn_out_norm, sgu_out_norm, w_out_f, post_mix_norm, pre_ffn_norm)
    gate, up, act = ffn_up(h2, w_gate_t, w_up_t)
    df, dx3, d_post_ffn, sq_err = ffn_down_loss(act, w_down_f, x2, post_ffn_norm, target)

    g_w_down = weight_grad("grad_w_down", act, df)
    (s_down,) = from_sibling("w_down_to_sibling", [g_w_down])
    dgate, dup, dx2, dy, d_pre_ffn, d_post_mix = ffn_bwd(
        df, w_down_f, gate, up, w_gate_t, w_up_t, x2, pre_ffn_norm, dx3, y, post_mix_norm, after=[g_w_down])
    (p_down,) = sum_cores("sum_cores_down", [g_w_down], [s_down], core, after=[dy])
    (c_down,) = from_chips("w_down_to_chips", [p_down])
    g_w_gate, g_w_up = weight_grads("grad_w_gate_up", [dgate, dup], h2, after=[p_down])
    s_gate, s_up = from_sibling("w_gate_up_to_sibling", [g_w_gate, g_w_up])
    g_w_out = weight_grad("grad_w_out", mix, dy, after=[g_w_up, c_down])
    p_gate, p_up = sum_cores("sum_cores_gate_up", [g_w_gate, g_w_up], [s_gate, s_up], core, after=[g_w_out])
    c_gate, c_up = from_chips("w_gate_up_to_chips", [p_gate, p_up])
    (s_out,) = from_sibling("w_out_to_sibling", [g_w_out])
    dsgu, d_attn_out, d_sgu_out, dattn_r = mix_bwd(dy, w_out_f, attn, sgu, attn_out_norm, sgu_out_norm, after=[p_gate, p_up])
    du, dvs, d_ln_gain, d_ln_bias, d_w_sp, d_bfull = sgu_bwd(u, vs, dsgu, sgu_ln_gain, sgu_ln_bias, w_sp, bfull)

    d_b_sp = d_bfull.reshape(CHUNK, N_GROUPS, HEAD_DIM).sum(axis=-1).T
    small_g = pack_small(dict(sgu_ln_gain=d_ln_gain, sgu_ln_bias=d_ln_bias, sgu_w_spatial=d_w_sp, sgu_b_spatial=d_b_sp,
                              attn_out_norm=d_attn_out, sgu_out_norm=d_sgu_out, post_mix_norm=d_post_mix,
                              pre_ffn_norm=d_pre_ffn, post_ffn_norm=d_post_ffn, loss_sum=sq_err))
    small_g = small_g.reshape(N_DEV, SMALL_ROWS // N_DEV, 128)
    ((_, (o_small,)),) = by_sequencer("small_to_owners", [to_owners(small_g)], TO_ALL)

    dq, dk, dv = attn_bwd(q, k, v, attn_r, lse, dattn_r, _to_residue_order(pos_col), rot)
    summed_small = sum_owned("sum_small", small_g, o_small, dev_ids, after=[dq, c_gate, c_up])
    (all_small,) = gathered("gather_small_grads", [summed_small])
    (p_out,) = sum_cores("sum_cores_out", [g_w_out], [s_out], core, after=[dq])
    (c_out,) = from_chips("w_out_to_chips", [p_out])
    dproj = [dq, dk, dv, du, dvs]
    g_w_in = weight_grad_of_parts("grad_w_in", dproj, h1, after=[c_gate, c_up])
    (s_in,) = from_sibling("w_in_to_sibling", [g_w_in])
    grad_x, d_pre_mix = in_bwd(dproj, w_in_t, x2d, pre_mix_norm, dx2, after=[g_w_in, all_small])
    (p_in,) = sum_cores("sum_cores_in", [g_w_in], [s_in], core, after=[d_pre_mix, c_out])
    (_, (c_in,)), (_, (late_parts,)) = by_sequencer(
        "last_sums_to_owners", [to_chips([p_in]), to_everyone(d_pre_mix)], TO_ALL)
    late_parts = lax.dynamic_update_slice(late_parts, d_pre_mix[None], (dev, 0, 0))

    def same(t):
        return t

    def turned(t):
        return t.T

    big, last = {}, p_in
    for call, weights in (("adamw_ffn", (("w_down", w_down, p_down, c_down, m_w_down, v_w_down, same),
                                         ("w_gate", w_gate, p_gate, c_gate, m_w_gate, v_w_gate, turned),
                                         ("w_up", w_up, p_up, c_up, m_w_up, v_w_up, turned))),
                          ("adamw_w_out", (("w_out", w_out, p_out, c_out, m_w_out, v_w_out, same),)),
                          ("adamw_w_in", (("w_in", w_in, p_in, c_in, m_w_in, v_w_in, turned),))):
        results = adamw_of_sums(call, [p for _, _, p, _, _, _, _ in weights], [c for _, _, _, c, _, _, _ in weights], chip_ids,
                                [turn(w[0]) for _, w, _, _, _, _, turn in weights], [turn(m[0]) for _, _, _, _, m, _, turn in weights],
                                [turn(vv[0]) for _, _, _, _, _, vv, turn in weights], last)
        for (name, _, _, _, _, _, turn), outs in zip(weights, results):
            big[name] = tuple(turn(t)[None] for t in outs)
        last = results[-1][0]
    sg, sd, snm, snv, loss_sum = adamw_small(all_small, late_parts, small_w, small_m, small_v)
    loss = loss_sum[0, 0] * np.float32(0.5 / D_MODEL)

    names = ["pre_mix_norm", "w_in", "sgu_ln_gain", "sgu_ln_bias", "sgu_w_spatial", "sgu_b_spatial", "attn_out_norm",
             "sgu_out_norm", "w_out", "post_mix_norm", "pre_ffn_norm", "w_gate", "w_up", "w_down", "post_ffn_norm"]
    outs = [loss, grad_x[None]]
    for i, table in enumerate((sg, sd, snm, snv)):
        for name in names:
            outs.append(big[name][i] if name in big else table[name])
    return tuple(outs)
```

```python
import numpy as np
import jax
import jax.numpy as jnp
from jax import lax
from jax.experimental import pallas as pl
from jax.experimental.pallas import tpu as pltpu
from jax.experimental.pallas import tpu_sc as plsc

F32 = jnp.float32
BF16 = jnp.bfloat16

SEQ = 2048
D_MODEL = 1024
ATTN_W = 512
SGU_W = 512
HEAD_DIM = 64
N_GROUPS = 8
CHUNK = 128
D_FF = 2816
IN_W = 3 * ATTN_W + 2 * SGU_W
DILATIONS = (1, 4, 16)
ROPE_THETA = 500000.0
ROT_DIM = 16
ROT_HALF = 8
RMS_EPS = 1e-6
LN_EPS = 1e-5
Q_SCALE = 0.125
NEG = -1e30

N_DEV = 8
MESH_AXES = ("x", "y", "c")
MESH = pl.DeviceIdType.MESH

ADAM_LR = 0.001
ADAM_B1 = 0.9
ADAM_B2 = 0.999
ADAM_EPS = 1e-08
ADAM_WD = 0.01
ADAM_STEP = 10

VMEM_LIMIT = 60 * 1024 * 1024
ANY = pl.BlockSpec(memory_space=pl.ANY)

SMALL = (("pre_mix_norm", 1024), ("sgu_ln_gain", 512), ("sgu_ln_bias", 512), ("sgu_w_spatial", 8 * 128 * 128),
         ("sgu_b_spatial", 1024), ("attn_out_norm", 512), ("sgu_out_norm", 512), ("post_mix_norm", 1024),
         ("pre_ffn_norm", 1024), ("post_ffn_norm", 1024), ("loss_sum", 1))
SMALL_ROWS = 1152


def _params(sem=("arbitrary",)):
    return pltpu.CompilerParams(dimension_semantics=sem, vmem_limit_bytes=VMEM_LIMIT)


def _dot(a, b):
    return jnp.dot(a, b, preferred_element_type=F32)


def _dot_nt(a, b):
    return lax.dot_general(a, b, (((1,), (1,)), ((), ())), preferred_element_type=F32)


def _dot_tn(a, b):
    return lax.dot_general(a, b, (((0,), (0,)), ((), ())), preferred_element_type=F32)


def _rms(z):
    return lax.rsqrt(jnp.mean(z * z, axis=-1, keepdims=True) + RMS_EPS)


def _rms_bwd(z, gain, d):
    r = _rms(z)
    n = z * r
    dn = d * gain
    dz = r * (dn - n * jnp.mean(dn * n, axis=-1, keepdims=True))
    return dz, jnp.sum(d * n, axis=0, keepdims=True)


def _gelu(z):
    return 0.5 * z * (1.0 + lax.erf(z * np.float32(1.0 / np.sqrt(2.0))))


def _gelu_grad(z):
    cdf = 0.5 * (1.0 + lax.erf(z * np.float32(1.0 / np.sqrt(2.0))))
    return cdf + z * jnp.exp(-0.5 * z * z) * np.float32(1.0 / np.sqrt(2.0 * np.pi))


def _rot_tables(pos_col, invf, ma, mb):
    ang = pos_col.astype(F32) * invf
    s = jnp.sin(ang)
    return jnp.cos(ang), s * ma, s * mb


def _rot(t, c, sa, sb):
    return t * c + pltpu.roll(t, 120, 1) * sa + pltpu.roll(t, 8, 1) * sb


def _rot_t(d, c, sa, sb):
    return d * c + pltpu.roll(d * sa, 8, 1) + pltpu.roll(d * sb, 120, 1)


def _rot_consts():
    lane = np.arange(128) % HEAD_DIM
    inv_freq = (np.float32(ROPE_THETA) ** (-np.arange(0, ROT_DIM, 2, dtype=np.float32) / np.float32(ROT_DIM))).astype(np.float32)
    invf = np.where(lane < ROT_DIM, inv_freq[lane % ROT_HALF], 0.0).astype(np.float32)
    ma = np.where(lane < ROT_HALF, -1.0, 0.0).astype(np.float32)
    mb = np.where((lane >= ROT_HALF) & (lane < ROT_DIM), 1.0, 0.0).astype(np.float32)
    return jnp.asarray(invf[None]), jnp.asarray(ma[None]), jnp.asarray(mb[None])


def _row_spec(tm, w):
    return pl.BlockSpec((tm, w), lambda i: (i, 0))


def _full_spec(shape):
    return pl.BlockSpec(shape, lambda i: (0,) * len(shape))


def _weight_spec(shape):
    return pl.BlockSpec(shape, lambda i: (0,) * len(shape), pipeline_mode=pl.Buffered(1))


FF_CHUNKS = (256, 512, 1024, 1024)
FF_SPANS = [(int(o), n) for o, n in zip(np.cumsum((0,) + FF_CHUNKS[:-1]), FF_CHUNKS)]
FF_WHOLE = [(0, D_FF)]


def _ffn_weight_scratch(n_weights, spans):
    return [pltpu.VMEM((D_FF, D_MODEL), BF16)] * n_weights + [pltpu.SemaphoreType.DMA((n_weights, len(spans)))]


def _with_ffn_weights(w_hbm, w_vmem, sems, spans, run):
    copies = [[pltpu.make_async_copy(h.at[pl.ds(o, n)], v.at[pl.ds(o, n)], sems.at[j, c]) for c, (o, n) in enumerate(spans)]
              for j, (h, v) in enumerate(zip(w_hbm, w_vmem))]
    first = pl.program_id(0) == 0

    @pl.when(first)
    def _():
        for of_weight in copies:
            for cp in of_weight:
                cp.start()
        run(lambda j, c: copies[j][c].wait())

    @pl.when(jnp.logical_not(first))
    def _():
        run(None)


RES = 16


def _residue_scratch(n_arrays, tm, width):
    return [pltpu.VMEM((2, n_arrays, tm // RES, RES, width), F32), pltpu.SemaphoreType.DMA((2, n_arrays, RES))]


def _to_residue_rows(tiles, outs, scratch, sems, tm, n_steps):
    i = pl.program_id(0)
    slot = i % 2
    per = tm // RES

    def copies(step, s):
        return [pltpu.make_async_copy(scratch.at[s, a, :, b, :],
                                      outs[a].at[pl.ds(pl.multiple_of(b * (SEQ // RES) + per * step, per), per), :],
                                      sems.at[s, a, b]) for a in range(len(outs)) for b in range(RES)]

    @pl.when(i >= 2)
    def _():
        for cp in copies(i - 2, slot):
            cp.wait()

    for a, tile in enumerate(tiles):
        scratch[slot, a] = tile.reshape(per, RES, tile.shape[-1])
    for cp in copies(i, slot):
        cp.start()

    @pl.when(i == n_steps - 1)
    def _():
        for cp in copies(i - 1, 1 - slot) + copies(i, slot):
            cp.wait()


def in_proj(x, pos_col, g1, w_in_t, rot):
    tm = 512
    n_steps = SEQ // tm

    def body(x_ref, pos_ref, g_ref, w_ref, invf_ref, ma_ref, mb_ref, h_ref, u_ref, vs_ref, q_ref, k_ref, v_ref, scratch, sems):
        xf = x_ref[...]
        h = (xf * _rms(xf) * g_ref[...]).astype(BF16)
        h_ref[...] = h
        proj = _dot_nt(h, w_ref[...])
        c, sa, sb = _rot_tables(pos_ref[...], invf_ref[...], ma_ref[...], mb_ref[...])
        slabs = range(ATTN_W // 128)
        q = jnp.concatenate([_rot(proj[:, j * 128:(j + 1) * 128], c, sa, sb) * Q_SCALE for j in slabs], axis=1)
        k = jnp.concatenate([_rot(proj[:, ATTN_W + j * 128:ATTN_W + (j + 1) * 128], c, sa, sb) for j in slabs], axis=1)
        u_ref[...] = proj[:, 3 * ATTN_W:3 * ATTN_W + SGU_W]
        vs_ref[...] = proj[:, 3 * ATTN_W + SGU_W:]
        _to_residue_rows([q, k, proj[:, 2 * ATTN_W:3 * ATTN_W]], [q_ref, k_ref, v_ref], scratch, sems, tm, n_steps)

    act = jax.ShapeDtypeStruct((SEQ, 512), F32)
    return _call(
        "in_proj", body, n_steps,
        [_row_spec(tm, D_MODEL), _row_spec(tm, 1), _full_spec((1, D_MODEL)), _weight_spec((IN_W, D_MODEL)),
         _full_spec((1, 128)), _full_spec((1, 128)), _full_spec((1, 128))],
        [_row_spec(tm, D_MODEL)] + [_row_spec(tm, 512)] * 2 + [ANY] * 3,
        [jax.ShapeDtypeStruct((SEQ, D_MODEL), BF16)] + [act] * 5,
        (x, pos_col, g1, w_in_t, *rot), scratch_shapes=_residue_scratch(3, tm, ATTN_W))


def _to_residue_order(t):
    return t.reshape(SEQ // RES, RES, -1).transpose(1, 0, 2).reshape(t.shape)


def _block_rows(d, r, n):
    if d == 16:
        slices = [(128 * r, 128)]
    elif d == 4:
        slices = [(128 * (4 * b + r) + 32 * n, 32) for b in range(4)]
    else:
        slices = [(128 * b + 8 * n, 8) for b in range(RES)]
    return [(s if isinstance(s, int) else pl.multiple_of(s, z), z) for s, z in slices]


def _block_step(d, i):
    if d == 16:
        return i
    if d == 4:
        return 4 * (i & 31) + (i >> 5)
    return 16 * (i & 7) + (i >> 3)


def _attn_masks(d):
    row2 = _block_step(d, lax.broadcasted_iota(jnp.int32, (128, 256), 0))
    col2 = lax.broadcasted_iota(jnp.int32, (128, 256), 1)
    key2 = _block_step(d, col2 & 127)
    mask2 = jnp.logical_or(jnp.logical_and(col2 < 128, key2 >= row2), jnp.logical_and(col2 >= 128, key2 <= row2))
    row1 = _block_step(d, lax.broadcasted_iota(jnp.int32, (128, 128), 0))
    col1 = lax.broadcasted_iota(jnp.int32, (128, 128), 1)
    return col1 < HEAD_DIM, _block_step(d, col1) <= row1, mask2


def _load_rows(ref, slices):
    parts = [ref[pl.ds(s, z), :] for s, z in slices]
    return parts[0] if len(parts) == 1 else jnp.concatenate(parts, axis=0)


def _for_each_group(fn):
    for p, d in enumerate(DILATIONS):
        masks = _attn_masks(d)
        if d == 16:
            def group(i, carry, p=p, masks=masks):
                fn(p, masks, [(_block_rows(16, 8 * i + g, 0), None) for g in range(8)])
                return carry

            lax.fori_loop(0, 2, group, 0)
        elif d == 4:
            fn(p, masks, [(_block_rows(4, r, 0), None) for r in range(4)])

            def group(i, carry, p=p, masks=masks):
                blocks = [6 * i + g for g in range(6)]
                fn(p, masks, [(_block_rows(4, j % 4, 1 + j // 4), _block_rows(4, j % 4, j // 4)) for j in blocks])
                return carry

            lax.fori_loop(0, 2, group, 0)
        else:
            fn(p, masks, [(_block_rows(1, 0, 0), None)])

            def group(i, carry, p=p, masks=masks):
                fn(p, masks, [(_block_rows(1, 0, 5 * i + g + 1), _block_rows(1, 0, 5 * i + g)) for g in range(5)])
                return carry

            lax.fori_loop(0, 3, group, 0)


def attn_fwd(q, k, v):
    def body(q_ref, k_ref, v_ref, o_ref, lse_ref, nat_ref, op_ref, lp_ref, sems):
        def group(p, masks, blocks):
            head0, mask1, mask2 = masks
            heads = (head0, jnp.logical_not(head0))
            keys = [rows if prev is None else prev + rows for rows, prev in blocks]
            mask = [mask1 if prev is None else mask2 for _, prev in blocks]
            qb = [_load_rows(q_ref, rows) for rows, _ in blocks]
            kk = [_load_rows(k_ref, ks).astype(BF16) for ks in keys]
            vv = [_load_rows(v_ref, ks).astype(BF16) for ks in keys]
            chains = [(g, hm) for g in range(len(blocks)) for hm in heads]
            s = [jnp.where(mask[g], _dot_nt(jnp.where(hm, qb[g], 0.0).astype(BF16), kk[g]), NEG) for g, hm in chains]
            m = [jnp.max(t, axis=-1, keepdims=True) for t in s]
            e = [jnp.exp(t - mt) for t, mt in zip(s, m)]
            l = [jnp.sum(t, axis=-1, keepdims=True) for t in e]
            pv = [_dot(t.astype(BF16), vv[g]) for t, (g, _) in zip(e, chains)]
            for g, (rows, _) in enumerate(blocks):
                o_blk = jnp.where(head0, pv[2 * g] / l[2 * g], pv[2 * g + 1] / l[2 * g + 1])
                l_blk = jnp.where(head0, jnp.broadcast_to(m[2 * g] + jnp.log(l[2 * g]), (128, 128)),
                                  jnp.broadcast_to(m[2 * g + 1] + jnp.log(l[2 * g + 1]), (128, 128)))
                at = 0
                for start, size in rows:
                    op_ref[p, pl.ds(start, size), :] = o_blk[at:at + size]
                    lp_ref[p, pl.ds(start, size), :] = l_blk[at:at + size]
                    at += size

        _for_each_group(group)

        def combine(i, carry):
            rows = pl.ds(pl.multiple_of(i * 256, 256), 256)
            ls = [lp_ref[p, rows, :] for p in range(3)]
            m = jnp.maximum(jnp.maximum(ls[0], ls[1]), ls[2])
            lse = m + jnp.log(jnp.exp(ls[0] - m) + jnp.exp(ls[1] - m) + jnp.exp(ls[2] - m))
            o = jnp.zeros((256, 128), F32)
            for p in range(3):
                o = o + jnp.exp(ls[p] - lse) * op_ref[p, rows, :]
            o_ref[rows, :] = o
            lse_ref[rows, :] = lse
            return carry

        lax.fori_loop(0, SEQ // 256, combine, 0)

        lanes = pl.ds(pl.multiple_of(pl.program_id(0) * 128, 128), 128)
        back = [pltpu.make_async_copy(o_ref.at[pl.ds(b * (SEQ // RES), SEQ // RES), :], nat_ref.at[:, b, lanes], sems.at[b])
                for b in range(RES)]
        for cp in back:
            cp.start()
        for cp in back:
            cp.wait()

    slab = pl.BlockSpec((SEQ, 128), lambda i: (0, i))
    out = jax.ShapeDtypeStruct((SEQ, ATTN_W), F32)
    attn_r, lse, attn = _call(
        "attn_fwd", body, ATTN_W // 128, [slab] * 3, [slab] * 2 + [ANY],
        [out, out, jax.ShapeDtypeStruct((SEQ // RES, RES, ATTN_W), F32)], (q, k, v),
        scratch_shapes=[pltpu.VMEM((3, SEQ, 128), F32), pltpu.VMEM((3, SEQ, 128), F32), pltpu.SemaphoreType.DMA((RES,))])
    return attn_r, lse, attn.reshape(SEQ, ATTN_W)


def _causal_weights(w_ref):
    row = lax.broadcasted_iota(jnp.int32, (CHUNK, CHUNK), 0)
    col = lax.broadcasted_iota(jnp.int32, (CHUNK, CHUNK), 1)
    return [jnp.where(col <= row, w_ref[g], 0.0).astype(BF16) for g in range(N_GROUPS)], col <= row


def _sgu_chunk_fwd(u, vs, lg, lb, wc, bfull, head0):
    ug = _gelu(u)
    vg = _gelu(vs)
    xc = vg - jnp.mean(vg, axis=-1, keepdims=True)
    rstd = lax.rsqrt(jnp.mean(xc * xc, axis=-1, keepdims=True) + LN_EPS)
    xhat = xc * rstd
    vn = xhat * lg + lb
    mixed = []
    for gp in range(SGU_W // 128):
        vp = vn[:, gp * 128:(gp + 1) * 128].astype(BF16)
        mixed.append(jnp.where(head0, _dot(wc[2 * gp], vp), _dot(wc[2 * gp + 1], vp)))
    ms = jnp.concatenate(mixed, axis=1) + bfull
    return ug, xhat, rstd, vn, ms


def sgu_fwd(u, vs, lg, lb, w_sp, bfull):
    cpb = 4

    def body(u_ref, vs_ref, lg_ref, lb_ref, w_ref, b_ref, o_ref):
        wc, _ = _causal_weights(w_ref)
        head0 = lax.broadcasted_iota(jnp.int32, (CHUNK, 128), 1) < HEAD_DIM
        for ci in range(cpb):
            rows = pl.ds(ci * CHUNK, CHUNK)
            ug, _, _, _, ms = _sgu_chunk_fwd(u_ref[rows, :], vs_ref[rows, :], lg_ref[...], lb_ref[...], wc, b_ref[...], head0)
            o_ref[rows, :] = ug * ms

    tm = cpb * CHUNK
    return _call(
        "sgu_fwd", body, SEQ // tm,
        [_row_spec(tm, SGU_W), _row_spec(tm, SGU_W), _full_spec((1, SGU_W)), _full_spec((1, SGU_W)),
         _full_spec((N_GROUPS, CHUNK, CHUNK)), _full_spec((CHUNK, SGU_W))],
        [_row_spec(tm, SGU_W)], [jax.ShapeDtypeStruct((SEQ, SGU_W), F32)],
        (u, vs, lg, lb, w_sp, bfull))


def out_proj(attn, sgu, x, ga, gs, w_out, gpm, gpf):
    tm = 512

    def body(a_ref, s_ref, x_ref, ga_ref, gs_ref, w_ref, gpm_ref, gpf_ref, mix_ref, y_ref, x2_ref, h2_ref):
        a = a_ref[...]
        s = s_ref[...]
        an = (a * _rms(a) * ga_ref[...]).astype(BF16)
        sn = (s * _rms(s) * gs_ref[...]).astype(BF16)
        mix_ref[:, :ATTN_W] = an
        mix_ref[:, ATTN_W:] = sn
        y = _dot(an, w_ref[:ATTN_W, :]) + _dot(sn, w_ref[ATTN_W:, :])
        y_ref[...] = y
        x2 = x_ref[...] + y * _rms(y) * gpm_ref[...]
        x2_ref[...] = x2
        h2_ref[...] = (x2 * _rms(x2) * gpf_ref[...]).astype(BF16)

    wide = jax.ShapeDtypeStruct((SEQ, D_MODEL), F32)
    wide16 = jax.ShapeDtypeStruct((SEQ, D_MODEL), BF16)
    return _call(
        "out_proj", body, SEQ // tm,
        [_row_spec(tm, ATTN_W), _row_spec(tm, SGU_W), _row_spec(tm, D_MODEL), _full_spec((1, ATTN_W)),
         _full_spec((1, SGU_W)), _weight_spec((D_MODEL, D_MODEL)), _full_spec((1, D_MODEL)), _full_spec((1, D_MODEL))],
        [_row_spec(tm, D_MODEL)] * 4, [wide16, wide, wide, wide16],
        (attn, sgu, x, ga, gs, w_out, gpm, gpf))


def ffn_up(h2, w_gate_t, w_up_t):
    tm = 256

    def body(h_ref, wg_hbm, wu_hbm, g_ref, u_ref, a_ref, wg_ref, wu_ref, sems):
        def run(wait):
            h = h_ref[...]
            if wait:
                wait(0, 0)
            g = _dot_nt(h, wg_ref[...])
            g_ref[...] = g.astype(BF16)
            if wait:
                wait(1, 0)
            u = _dot_nt(h, wu_ref[...])
            u_ref[...] = u.astype(BF16)
            a_ref[...] = (g * jax.nn.sigmoid(g) * u).astype(BF16)

        _with_ffn_weights([wg_hbm, wu_hbm], [wg_ref, wu_ref], sems, FF_WHOLE, run)

    ff = jax.ShapeDtypeStruct((SEQ, D_FF), BF16)
    return _call(
        "ffn_up", body, SEQ // tm, [_row_spec(tm, D_MODEL), ANY, ANY],
        [_row_spec(tm, D_FF)] * 3, [ff, ff, jax.ShapeDtypeStruct((SEQ, D_FF), BF16)],
        (h2, w_gate_t, w_up_t), scratch_shapes=_ffn_weight_scratch(2, FF_WHOLE))


def ffn_down_loss(act, w_down, x2, gpo, target):
    tm = 512

    def body(a_ref, w_hbm, x2_ref, g_ref, t_ref, df_ref, dx3_ref, dg_ref, loss_ref, w_ref, sems):
        def run(wait):
            f = None
            for c, (o, n) in enumerate(FF_SPANS if wait else FF_WHOLE):
                if wait:
                    wait(0, c)
                part = _dot(a_ref[:, o:o + n], w_ref[o:o + n, :])
                f = part if f is None else f + part
            gain = g_ref[...]
            err = x2_ref[...] + f * _rms(f) * gain - t_ref[...]
            dx3 = err * np.float32(1.0 / D_MODEL)
            dx3_ref[...] = dx3
            df, dg = _rms_bwd(f, gain, dx3)
            df_ref[...] = df.astype(BF16)
            loss = jnp.sum(err * err, axis=(0, 1), keepdims=True)
            if wait:
                dg_ref[...] = dg
                loss_ref[...] = loss
            else:
                dg_ref[...] += dg
                loss_ref[...] += loss

        _with_ffn_weights([w_hbm], [w_ref], sems, FF_SPANS, run)

    return _call(
        "ffn_down_loss", body, SEQ // tm,
        [_row_spec(tm, D_FF), ANY, _row_spec(tm, D_MODEL), _full_spec((1, D_MODEL)), _row_spec(tm, D_MODEL)],
        [_row_spec(tm, D_MODEL), _row_spec(tm, D_MODEL), _full_spec((1, D_MODEL)), _full_spec((1, 1))],
        [jax.ShapeDtypeStruct((SEQ, D_MODEL), BF16), jax.ShapeDtypeStruct((SEQ, D_MODEL), F32),
         jax.ShapeDtypeStruct((1, D_MODEL), F32), jax.ShapeDtypeStruct((1, 1), F32)],
        (act, w_down, x2, gpo, target), scratch_shapes=_ffn_weight_scratch(1, FF_SPANS))


def ffn_bwd(df, w_down, gate, up, w_gate_t, w_up_t, x2, gpf, dx3, y, gpm, after=()):
    tm = 256

    def body(df_ref, wd_hbm, g_ref, u_ref, wg_hbm, wu_hbm, x2_ref, gpf_ref, dx3_ref, y_ref, gpm_ref,
             dg_ref, du_ref, dx2_ref, dy_ref, dgpf_ref, dgpm_ref, wd_ref, wg_ref, wu_ref, sems):
        def run(wait):
            if wait:
                wait(0, 0)
            dact = _dot_nt(df_ref[...], wd_ref[...])
            g = g_ref[...].astype(F32)
            s = jax.nn.sigmoid(g)
            dup = (dact * g * s).astype(BF16)
            dgate = (dact * u_ref[...].astype(F32) * (s * (1.0 + g * (1.0 - s)))).astype(BF16)
            du_ref[...] = dup
            dg_ref[...] = dgate
            if wait:
                wait(1, 0)
                wait(2, 0)
            dh2 = _dot(dgate, wg_ref[...]) + _dot(dup, wu_ref[...])
            dz, dgpf = _rms_bwd(x2_ref[...], gpf_ref[...], dh2)
            dx2 = dx3_ref[...] + dz
            dx2_ref[...] = dx2
            dy, dgpm = _rms_bwd(y_ref[...], gpm_ref[...], dx2)
            dy_ref[...] = dy.astype(BF16)
            if wait:
                dgpf_ref[...] = dgpf
                dgpm_ref[...] = dgpm
            else:
                dgpf_ref[...] += dgpf
                dgpm_ref[...] += dgpm

        _with_ffn_weights([wd_hbm, wg_hbm, wu_hbm], [wd_ref, wg_ref, wu_ref], sems, FF_WHOLE, run)

    vec = jax.ShapeDtypeStruct((1, D_MODEL), F32)
    ff16 = jax.ShapeDtypeStruct((SEQ, D_FF), BF16)
    return _call(
        "ffn_bwd", body, SEQ // tm,
        [_row_spec(tm, D_MODEL), ANY, _row_spec(tm, D_FF), _row_spec(tm, D_FF), ANY, ANY, _row_spec(tm, D_MODEL),
         _full_spec((1, D_MODEL)), _row_spec(tm, D_MODEL), _row_spec(tm, D_MODEL), _full_spec((1, D_MODEL))],
        [_row_spec(tm, D_FF), _row_spec(tm, D_FF), _row_spec(tm, D_MODEL), _row_spec(tm, D_MODEL),
         _full_spec((1, D_MODEL)), _full_spec((1, D_MODEL))],
        [ff16, ff16, jax.ShapeDtypeStruct((SEQ, D_MODEL), F32), jax.ShapeDtypeStruct((SEQ, D_MODEL), BF16), vec, vec],
        (df, w_down, gate, up, w_gate_t, w_up_t, x2, gpf, dx3, y, gpm), scratch_shapes=_ffn_weight_scratch(3, FF_WHOLE), after=after)


def weight_grads(name, lhs, b, after=()):
    m, n, k = lhs[0].shape[1], b.shape[1], len(lhs)
    tr = 256

    def body(*refs):
        for a_ref, o_ref in zip(refs[:k], refs[k + 1:]):
            o_ref[...] = _dot_tn(a_ref[...], refs[k][...]).astype(BF16)

    outs = _call(
        name, body, m // tr, [pl.BlockSpec((SEQ, tr), lambda i: (0, i))] * k + [_weight_spec((SEQ, n))],
        [_row_spec(tr, n)] * k, [jax.ShapeDtypeStruct((m, n), BF16)] * k, (*lhs, b), after=after)
    return [out.reshape(N_DEV, m // N_DEV, n) for out in outs]


def weight_grad(name, a, b, after=()):
    return weight_grads(name, [a], b, after)[0]


def weight_grad_of_parts(name, parts, b, after=()):
    p, n, k = parts[0].shape[1], b.shape[1], len(parts)
    tr = 512
    per = p // tr

    def body(*refs):
        tile = pl.program_id(0)
        for j in range(k):
            @pl.when(tile // per == j)
            def _(j=j):
                refs[k + 1][...] = _dot_tn(refs[j][...].astype(BF16), refs[k][...]).astype(BF16)

    def part_spec(j):
        return pl.BlockSpec((SEQ, tr), lambda i: (0, jnp.clip(i - per * j, 0, per - 1)))

    (out,) = _call(
        name, body, k * per, [part_spec(j) for j in range(k)] + [_weight_spec((SEQ, n))],
        [_row_spec(tr, n)], [jax.ShapeDtypeStruct((k * p, n), BF16)], (*parts, b), after=after)
    return out.reshape(N_DEV, k * p // N_DEV, n)


def mix_bwd(dy, w_out, attn, sgu, ga, gs, after=()):
    tm = 512
    n_steps = SEQ // tm

    def body(dy_ref, w_ref, a_ref, s_ref, ga_ref, gs_ref, ds_ref, dga_ref, dgs_ref, da_ref, scratch, sems):
        dy = dy_ref[...]
        da, dga = _rms_bwd(a_ref[...], ga_ref[...], _dot_nt(dy, w_ref[:ATTN_W, :]))
        ds, dgs = _rms_bwd(s_ref[...], gs_ref[...], _dot_nt(dy, w_ref[ATTN_W:, :]))
        ds_ref[...] = ds
        _to_residue_rows([da], [da_ref], scratch, sems, tm, n_steps)

        @pl.when(pl.program_id(0) == 0)
        def _():
            dga_ref[...] = jnp.zeros_like(dga_ref)
            dgs_ref[...] = jnp.zeros_like(dgs_ref)

        dga_ref[...] += dga
        dgs_ref[...] += dgs

    half = jax.ShapeDtypeStruct((SEQ, 512), F32)
    vec = jax.ShapeDtypeStruct((1, 512), F32)
    return _call(
        "mix_bwd", body, n_steps,
        [_row_spec(tm, D_MODEL), _weight_spec((D_MODEL, D_MODEL)), _row_spec(tm, 512), _row_spec(tm, 512),
         _full_spec((1, 512)), _full_spec((1, 512))],
        [_row_spec(tm, 512), _full_spec((1, 512)), _full_spec((1, 512)), ANY],
        [half, vec, vec, half], (dy, w_out, attn, sgu, ga, gs), scratch_shapes=_residue_scratch(1, tm, ATTN_W), after=after)


def sgu_bwd(u, vs, dsgu, lg, lb, w_sp, bfull, after=()):
    cpb = 4

    def body(u_ref, vs_ref, d_ref, lg_ref, lb_ref, w_ref, b_ref, du_ref, dvs_ref, dlg_ref, dlb_ref, dw_ref, db_ref):
        wc, causal = _causal_weights(w_ref)
        head0 = lax.broadcasted_iota(jnp.int32, (CHUNK, 128), 1) < HEAD_DIM
        lg = lg_ref[...]

        @pl.when(pl.program_id(0) == 0)
        def _():
            dlg_ref[...] = jnp.zeros_like(dlg_ref)
            dlb_ref[...] = jnp.zeros_like(dlb_ref)
            dw_ref[...] = jnp.zeros_like(dw_ref)
            db_ref[...] = jnp.zeros_like(db_ref)

        for ci in range(cpb):
            rows = pl.ds(ci * CHUNK, CHUNK)
            u = u_ref[rows, :]
            vs = vs_ref[rows, :]
            d = d_ref[rows, :]
            ug, xhat, rstd, vn, ms = _sgu_chunk_fwd(u, vs, lg, lb_ref[...], wc, b_ref[...], head0)
            du_ref[rows, :] = (d * ms * _gelu_grad(u)).astype(BF16)
            dms = d * ug
            db_ref[...] += dms
            dvn = []
            for gp in range(SGU_W // 128):
                dmp = dms[:, gp * 128:(gp + 1) * 128]
                dm0 = jnp.where(head0, dmp, 0.0).astype(BF16)
                dm1 = jnp.where(head0, 0.0, dmp).astype(BF16)
                vp = vn[:, gp * 128:(gp + 1) * 128].astype(BF16)
                dw_ref[2 * gp] += _dot_nt(dm0, vp)
                dw_ref[2 * gp + 1] += _dot_nt(dm1, vp)
                dvn.append(_dot_tn(wc[2 * gp], dm0) + _dot_tn(wc[2 * gp + 1], dm1))
            dvn = jnp.concatenate(dvn, axis=1)
            dlg_ref[...] += jnp.sum(dvn * xhat, axis=0, keepdims=True)
            dlb_ref[...] += jnp.sum(dvn, axis=0, keepdims=True)
            dxh = dvn * lg
            dvg = rstd * (dxh - jnp.mean(dxh, axis=-1, keepdims=True) - xhat * jnp.mean(dxh * xhat, axis=-1, keepdims=True))
            dvs_ref[rows, :] = (dvg * _gelu_grad(vs)).astype(BF16)

        @pl.when(pl.program_id(0) == pl.num_programs(0) - 1)
        def _():
            for g in range(N_GROUPS):
                dw_ref[g] = jnp.where(causal, dw_ref[g], 0.0)

    tm = cpb * CHUNK
    half16 = jax.ShapeDtypeStruct((SEQ, SGU_W), BF16)
    vec = jax.ShapeDtypeStruct((1, SGU_W), F32)
    return _call(
        "sgu_bwd", body, SEQ // tm,
        [_row_spec(tm, SGU_W)] * 3 + [_full_spec((1, SGU_W)), _full_spec((1, SGU_W)),
                                      _full_spec((N_GROUPS, CHUNK, CHUNK)), _full_spec((CHUNK, SGU_W))],
        [_row_spec(tm, SGU_W), _row_spec(tm, SGU_W), _full_spec((1, SGU_W)), _full_spec((1, SGU_W)),
         _full_spec((N_GROUPS, CHUNK, CHUNK)), _full_spec((CHUNK, SGU_W))],
        [half16, half16, vec, vec, jax.ShapeDtypeStruct((N_GROUPS, CHUNK, CHUNK), F32),
         jax.ShapeDtypeStruct((CHUNK, SGU_W), F32)],
        (u, vs, dsgu, lg, lb, w_sp, bfull), after=after)


def attn_bwd(q, k, v, o, lse, do, pos_col, rot):
    n_steps = ATTN_W // 128

    def body(q_ref, k_ref, v_ref, o_ref, lse_ref, do_ref, pos_ref, invf_ref, ma_ref, mb_ref,
             dq_ref, dk_ref, dv_ref, dqa_ref, dka_ref, dva_ref, dlt_ref, rot_ref, out_ref, sems):
        step = pl.program_id(0)
        slot = step % 2

        def back(at_step, s):
            lanes = pl.ds(pl.multiple_of(at_step * 128, 128), 128)
            return [pltpu.make_async_copy(out_ref.at[s, a, pl.ds(b * (SEQ // RES), SEQ // RES), :], nat.at[:, b, lanes],
                                          sems.at[s, a, b]) for a, nat in enumerate((dq_ref, dk_ref, dv_ref)) for b in range(RES)]

        dqa_ref[...] = jnp.zeros_like(dqa_ref)
        dka_ref[...] = jnp.zeros_like(dka_ref)
        dva_ref[...] = jnp.zeros_like(dva_ref)

        def delta(i, carry):
            rows = pl.ds(pl.multiple_of(i * 256, 256), 256)
            prod = do_ref[rows, :] * o_ref[rows, :]
            h0 = lax.broadcasted_iota(jnp.int32, (256, 128), 1) < HEAD_DIM
            d0 = jnp.sum(jnp.where(h0, prod, 0.0), axis=-1, keepdims=True)
            d1 = jnp.sum(jnp.where(h0, 0.0, prod), axis=-1, keepdims=True)
            dlt_ref[rows, :] = jnp.where(h0, d0, d1)
            return carry

        lax.fori_loop(0, SEQ // 256, delta, 0)

        def add_rows(ref, slices, val):
            at = 0
            for start, size in slices:
                ref[pl.ds(start, size), :] += val[at:at + size]
                at += size

        def group(p, masks, blocks):
            head0, mask1, mask2 = masks
            heads = (head0, jnp.logical_not(head0))
            keys = [rows if prev is None else prev + rows for rows, prev in blocks]
            mask = [mask1 if prev is None else mask2 for _, prev in blocks]
            kk = [_load_rows(k_ref, ks).astype(BF16) for ks in keys]
            vv = [_load_rows(v_ref, ks).astype(BF16) for ks in keys]
            qb = [_load_rows(q_ref, rows) for rows, _ in blocks]
            dob = [_load_rows(do_ref, rows) for rows, _ in blocks]
            lse_b = [_load_rows(lse_ref, rows) for rows, _ in blocks]
            dlt_b = [_load_rows(dlt_ref, rows) for rows, _ in blocks]
            chains = [(g, h) for g in range(len(blocks)) for h in range(2)]
            qm = [jnp.where(heads[h], qb[g], 0.0).astype(BF16) for g, h in chains]
            dom = [jnp.where(heads[h], dob[g], 0.0).astype(BF16) for g, h in chains]
            s = [_dot_nt(qm[c], kk[g]) for c, (g, h) in enumerate(chains)]
            dp = [_dot_nt(dom[c], vv[g]) for c, (g, h) in enumerate(chains)]
            pr = [jnp.where(mask[g], jnp.exp(s[c] - lse_b[g][:, h * HEAD_DIM:h * HEAD_DIM + 1]), 0.0)
                  for c, (g, h) in enumerate(chains)]
            ds = [(pr[c] * (dp[c] - dlt_b[g][:, h * HEAD_DIM:h * HEAD_DIM + 1])).astype(BF16)
                  for c, (g, h) in enumerate(chains)]
            dv = [_dot_tn(pr[c].astype(BF16), dom[c]) for c in range(len(chains))]
            dk = [_dot_tn(ds[c], qm[c]) for c in range(len(chains))]
            dq = [_dot(ds[c], kk[g]) for c, (g, h) in enumerate(chains)]
            for g, (rows, _) in enumerate(blocks):
                add_rows(dqa_ref, rows, jnp.where(head0, dq[2 * g], dq[2 * g + 1]))
                add_rows(dka_ref, keys[g], dk[2 * g] + dk[2 * g + 1])
                add_rows(dva_ref, keys[g], dv[2 * g] + dv[2 * g + 1])

        _for_each_group(group)

        @pl.when(pl.program_id(0) == 0)
        def _():
            def tables(i, carry):
                rows = pl.ds(pl.multiple_of(i * 256, 256), 256)
                c, sa, sb = _rot_tables(pos_ref[rows, :], invf_ref[...], ma_ref[...], mb_ref[...])
                rot_ref[0, rows, :] = c
                rot_ref[1, rows, :] = sa
                rot_ref[2, rows, :] = sb
                return carry

            lax.fori_loop(0, SEQ // 256, tables, 0)

        @pl.when(step >= 2)
        def _():
            for cp in back(step - 2, slot):
                cp.wait()

        def finish(i, carry):
            rows = pl.ds(pl.multiple_of(i * 256, 256), 256)
            c, sa, sb = rot_ref[0, rows, :], rot_ref[1, rows, :], rot_ref[2, rows, :]
            out_ref[slot, 0, rows, :] = _rot_t(dqa_ref[rows, :] * Q_SCALE, c, sa, sb)
            out_ref[slot, 1, rows, :] = _rot_t(dka_ref[rows, :], c, sa, sb)
            out_ref[slot, 2, rows, :] = dva_ref[rows, :]
            return carry

        lax.fori_loop(0, SEQ // 256, finish, 0)
        for cp in back(step, slot):
            cp.start()

        @pl.when(step == n_steps - 1)
        def _():
            for cp in back(step - 1, 1 - slot) + back(step, slot):
                cp.wait()

    slab = pl.BlockSpec((SEQ, 128), lambda i: (0, i))
    out = jax.ShapeDtypeStruct((SEQ // RES, RES, ATTN_W), F32)
    acc = pltpu.VMEM((SEQ, 128), F32)
    outs = _call(
        "attn_bwd", body, n_steps,
        [slab] * 6 + [_full_spec((SEQ, 1)), _full_spec((1, 128)), _full_spec((1, 128)), _full_spec((1, 128))],
        [ANY] * 3, [out, out, out], (q, k, v, o, lse, do, pos_col, *rot),
        scratch_shapes=[acc, acc, acc, acc, pltpu.VMEM((3, SEQ, 128), F32), pltpu.VMEM((2, 3, SEQ, 128), F32),
                        pltpu.SemaphoreType.DMA((2, 3, RES))])
    return [t.reshape(SEQ, ATTN_W) for t in outs]


def in_bwd(dproj_parts, w_in_t, x, g1, dx2, after=()):
    tm = 512
    k = len(dproj_parts)

    def body(*refs):
        w_ref, x_ref, g_ref, dx2_ref, dx_ref, dg_ref = refs[k:]
        dh1 = _dot(refs[0][...].astype(BF16), w_ref[0:512, :])
        for j in range(1, k):
            dh1 = dh1 + _dot(refs[j][...].astype(BF16), w_ref[512 * j:512 * (j + 1), :])
        dz, dg = _rms_bwd(x_ref[...], g_ref[...], dh1)
        dx_ref[...] = dx2_ref[...] + dz

        @pl.when(pl.program_id(0) == 0)
        def _():
            dg_ref[...] = jnp.zeros_like(dg_ref)

        dg_ref[...] += dg

    return _call(
        "in_bwd", body, SEQ // tm,
        [_row_spec(tm, 512)] * k + [_weight_spec((IN_W, D_MODEL)), _row_spec(tm, D_MODEL), _full_spec((1, D_MODEL)),
                                    _row_spec(tm, D_MODEL)],
        [_row_spec(tm, D_MODEL), _full_spec((1, D_MODEL))],
        [jax.ShapeDtypeStruct((SEQ, D_MODEL), F32), jax.ShapeDtypeStruct((1, D_MODEL), F32)],
        (*dproj_parts, w_in_t, x, g1, dx2), after=after)


def _coords():
    return lax.axis_index("x"), lax.axis_index("y"), lax.axis_index("c")


class Exchange:
    def __init__(self, srcs, bufs, new_shapes, n_sems, make):
        self.srcs, self.bufs, self.new_shapes, self.n_sems, self.make = list(srcs), list(bufs), list(new_shapes), n_sems, make


def _call(name, body, n_steps, in_specs, out_specs, out_shape, args, scratch_shapes=(), after=()):
    n_in = len(args)

    def wrapped(*refs):
        body(*refs[:n_in], *refs[n_in + len(after):])

    return list(pl.pallas_call(
        wrapped, name=name, grid=(n_steps,), in_specs=list(in_specs) + [ANY] * len(after), out_specs=list(out_specs),
        out_shape=list(out_shape), scratch_shapes=list(scratch_shapes), compiler_params=_params(),
    )(*args, *after))


GATHER_SEMS = 8


def gather(bufs):
    n = len(bufs)

    def make(src_refs, buf_refs, new_refs, send_sems, recv_sems):
        x, y, c = _coords()
        me, sibling = (x, y, c), (x, y, 1 - c)
        over_x, over_y, across = (1 - x, y), (x, 1 - y), (1 - x, 1 - y)

        def copy(a, k, block, to, half=None):
            r = buf_refs[a].shape[0] // N_DEV
            lo, size = (0, r) if half is None else (half * (r // 2), r // 2)
            rows = buf_refs[a].at[pl.ds((4 * block[0] + 2 * block[1] + block[2]) * r + lo, size), :]
            return pltpu.make_async_remote_copy(
                src_ref=rows, dst_ref=rows, send_sem=send_sems.at[GATHER_SEMS * a + k],
                recv_sem=recv_sems.at[GATHER_SEMS * a + k], device_id=to, device_id_type=MESH)

        every = range(n)
        out = ([copy(a, 0, me, sibling) for a in every] + [copy(a, 1, me, (*over_x, c)) for a in every]
               + [copy(a, 2, me, (*over_y, c)) for a in every])
        near_in = [copy(a, 1, (*over_x, c), me) for a in every] + [copy(a, 2, (*over_y, c), me) for a in every]
        relay = ([copy(a, 3, (*over_x, c), (*over_y, c), half=0) for a in every]
                 + [copy(a, 4, (*over_y, c), (*over_x, c), half=1) for a in every])
        near_on = [copy(a, 5, (*over_x, c), sibling) for a in every] + [copy(a, 6, (*over_y, c), sibling) for a in every]
        relay_in = ([copy(a, 3, (*across, c), me, half=0) for a in every]
                    + [copy(a, 4, (*across, c), me, half=1) for a in every])
        far_on = [copy(a, 7, (*across, c), sibling) for a in every]
        from_core = ([copy(a, 0, sibling, me) for a in every] + [copy(a, 5, (*over_x, 1 - c), me) for a in every]
                     + [copy(a, 6, (*over_y, 1 - c), me) for a in every] + [copy(a, 7, (*across, 1 - c), me) for a in every])
        stages = [([], out), (near_in, relay + near_on), (relay_in, far_on)]
        return stages, out + relay + near_on + far_on, from_core

    return Exchange([], bufs, [], GATHER_SEMS * n, make)


TO_GATHER = (1, lambda x, y, c: [(x, y, 1 - c), (1 - x, y, c), (x, 1 - y, c)])
TO_SIBLING = (2, lambda x, y, c: [(x, y, 1 - c)])
TO_CHIPS = (3, lambda x, y, c: [(1 - x, y, c), (x, 1 - y, c), (1 - x, 1 - y, c)])
TO_ALL = (4, lambda x, y, c: [(x ^ (m >> 2), y ^ ((m >> 1) & 1), c ^ (m & 1)) for m in range(1, N_DEV)])


def by_sequencer(name, exchanges, who):
    collective_id, peers_of = who
    hbm = pltpu.MemorySpace.HBM
    refs = [([jax.new_ref(a, memory_space=hbm) for a in ex.srcs], [jax.new_ref(a, memory_space=hbm) for a in ex.bufs],
             [jax.empty_ref(s, memory_space=hbm) for s in ex.new_shapes]) for ex in exchanges]
    sems = []
    for ex in exchanges:
        sems += [pltpu.SemaphoreType.DMA((ex.n_sems,)), pltpu.SemaphoreType.DMA((ex.n_sems,))]

    @pl.kernel(mesh=plsc.ScalarSubcoreMesh(axis_name="sequencer", num_cores=1), name=name, scratch_types=tuple(sems),
               compiler_params=pltpu.CompilerParams(collective_id=collective_id))
    def launch(*sem_refs):
        peers = peers_of(*_coords())
        barrier = pltpu.get_barrier_semaphore()
        for peer in peers:
            pl.semaphore_signal(barrier, inc=1, device_id=peer, device_id_type=MESH)
        pl.semaphore_wait(barrier, len(peers))

        made = [ex.make(*refs[k], sem_refs[2 * k], sem_refs[2 * k + 1]) for k, ex in enumerate(exchanges)]
        for stage in range(max(len(stages) for stages, _, _ in made)):
            for stages, _, _ in made:
                if stage < len(stages):
                    arrivals, starts = stages[stage]
                    for cp in arrivals:
                        cp.wait_recv()
                    for cp in starts:
                        cp.start()
        for _, sends, arrivals in made:
            for cp in arrivals:
                cp.wait_recv()
            for cp in sends:
                cp.wait_send()

    launch()
    return [([ref[...] for ref in bufs], [ref[...] for ref in news]) for _, bufs, news in refs]


def place_shards(name, shards, dev):
    n = len(shards)

    def body(dev_ref, *refs):
        for a in range(n):
            refs[n + a][...] = refs[a][...].astype(BF16)

    spec = pltpu.PrefetchScalarGridSpec(
        num_scalar_prefetch=1, grid=(1,),
        in_specs=[pl.BlockSpec(s.shape, lambda i, dev_ref: (0, 0)) for s in shards],
        out_specs=[pl.BlockSpec(s.shape, lambda i, dev_ref: (dev_ref[0], 0)) for s in shards])
    return pl.pallas_call(
        body, name=name, grid_spec=spec,
        out_shape=[jax.ShapeDtypeStruct((N_DEV * s.shape[0], s.shape[1]), BF16) for s in shards],
        compiler_params=_params(),
    )(dev, *shards)


def _swap(copies_of):
    def make(src_refs, buf_refs, new_refs, send_sems, recv_sems):
        copies = copies_of(src_refs, new_refs, send_sems, recv_sems)
        return [([], copies)], copies, copies

    return make


def to_sibling(grads):
    def copies_of(src_refs, new_refs, send_sems, recv_sems):
        x, y, c = _coords()
        return [pltpu.make_async_remote_copy(
            src_ref=src_refs[a].at[2 * xy + 1 - c], dst_ref=new_refs[a].at[xy], send_sem=send_sems.at[4 * a + xy],
            recv_sem=recv_sems.at[4 * a + xy], device_id=(x, y, 1 - c), device_id_type=MESH)
            for a in range(len(src_refs)) for xy in range(4)]

    return Exchange(grads, [], [jax.ShapeDtypeStruct((4,) + g.shape[1:], g.dtype) for g in grads], 4 * len(grads),
                    _swap(copies_of))


def to_chips(parts):
    def copies_of(src_refs, new_refs, send_sems, recv_sems):
        x, y, c = _coords()
        chips = [(1 - x, y), (x, 1 - y), (1 - x, 1 - y)]
        return [pltpu.make_async_remote_copy(
            src_ref=src_refs[a].at[2 * px + py], dst_ref=new_refs[a].at[2 * x + y], send_sem=send_sems.at[3 * a + j],
            recv_sem=recv_sems.at[3 * a + j], device_id=(px, py, c), device_id_type=MESH)
            for a in range(len(src_refs)) for j, (px, py) in enumerate(chips)]

    return Exchange(parts, [], [jax.ShapeDtypeStruct(p.shape, p.dtype) for p in parts], 3 * len(parts), _swap(copies_of))


def to_owners(grad):
    def copies_of(src_refs, new_refs, send_sems, recv_sems):
        x, y, c = _coords()
        copies = []
        for m in range(1, N_DEV):
            px, py, pc = x ^ (m >> 2), y ^ ((m >> 1) & 1), c ^ (m & 1)
            copies.append(pltpu.make_async_remote_copy(
                src_ref=src_refs[0].at[4 * px + 2 * py + pc], dst_ref=new_refs[0].at[4 * x + 2 * y + c],
                send_sem=send_sems.at[m - 1], recv_sem=recv_sems.at[m - 1], device_id=(px, py, pc), device_id_type=MESH))
        return copies

    return Exchange([grad], [], [jax.ShapeDtypeStruct(grad.shape, grad.dtype)], N_DEV - 1, _swap(copies_of))


def to_everyone(vec):
    def copies_of(src_refs, new_refs, send_sems, recv_sems):
        x, y, c = _coords()
        copies = []
        for m in range(1, N_DEV):
            px, py, pc = x ^ (m >> 2), y ^ ((m >> 1) & 1), c ^ (m & 1)
            copies.append(pltpu.make_async_remote_copy(
                src_ref=src_refs[0], dst_ref=new_refs[0].at[4 * x + 2 * y + c],
                send_sem=send_sems.at[m - 1], recv_sem=recv_sems.at[m - 1], device_id=(px, py, pc), device_id_type=MESH))
        return copies

    return Exchange([vec], [], [jax.ShapeDtypeStruct((N_DEV,) + vec.shape, vec.dtype)], N_DEV - 1, _swap(copies_of))


def sum_cores(name, grads, others, core, after=()):
    k = len(grads)

    def body(core_ref, *refs):
        for j in range(k):
            out_ref = refs[2 * k + len(after) + j]
            out_ref[...] = (refs[j][:, 0].astype(F32) + refs[k + j][...].astype(F32)).astype(out_ref.dtype)

    mine = [pl.BlockSpec((2, 1) + o.shape[1:], lambda i, core_ref: (i, core_ref[0], 0, 0)) for o in others]
    theirs = [pl.BlockSpec((2,) + o.shape[1:], lambda i, core_ref: (i, 0, 0)) for o in others]
    return pl.pallas_call(
        body, name=name,
        grid_spec=pltpu.PrefetchScalarGridSpec(
            num_scalar_prefetch=1, grid=(2,), in_specs=mine + theirs + [ANY] * len(after), out_specs=theirs),
        out_shape=[jax.ShapeDtypeStruct(o.shape, o.dtype) for o in others],
        compiler_params=_params(),
    )(core, *[g.reshape((4, 2) + g.shape[1:]) for g in grads], *others, *after)


def sum_owned(name, grad, others, dev_ids, after=()):
    _, r, w = grad.shape

    def body(ids_ref, *refs):
        acc = refs[0][0]
        for k in range(1, N_DEV):
            acc = acc + refs[k][0]
        refs[-1][...] = acc

    def pick(k):
        return pl.BlockSpec((1, r, w), lambda i, ids_ref: (ids_ref[k], 0, 0))

    return pl.pallas_call(
        body, name=name,
        grid_spec=pltpu.PrefetchScalarGridSpec(
            num_scalar_prefetch=1, grid=(1,), in_specs=[pick(k) for k in range(N_DEV)] + [ANY] * len(after),
            out_specs=pl.BlockSpec((r, w), lambda i, ids_ref: (ids_ref[0], 0))),
        out_shape=jax.ShapeDtypeStruct((N_DEV * r, w), F32),
        compiler_params=_params(),
    )(dev_ids, grad, *([others] * (N_DEV - 1)), *after)


def _adamw_update(w, g, m, v):
    nm = ADAM_B1 * m + np.float32(1.0 - ADAM_B1) * g
    nv = ADAM_B2 * v + np.float32(1.0 - ADAM_B2) * (g * g)
    m_hat = nm / np.float32(1.0 - ADAM_B1 ** ADAM_STEP)
    v_hat = nv / np.float32(1.0 - ADAM_B2 ** ADAM_STEP)
    return -ADAM_LR * (m_hat / (jnp.sqrt(v_hat) + ADAM_EPS) + ADAM_WD * w), nm, nv


def adamw_of_sums(name, parts, others, chip_ids, ws, ms, vs, after):
    n = len(parts)
    halves = 2

    def body(ids_ref, *refs):
        outs = refs[7 * n + 1:]
        for j in range(n):
            p_ref, a_ref, b_ref, c_ref, w_ref, m_ref, v_ref = refs[7 * j:7 * j + 7]
            g = ((p_ref[0].astype(F32) + a_ref[0].astype(F32)) + b_ref[0].astype(F32)) + c_ref[0].astype(F32)
            outs[4 * j][...] = g
            outs[4 * j + 1][...], outs[4 * j + 2][...], outs[4 * j + 3][...] = _adamw_update(w_ref[...], g, m_ref[...], v_ref[...])

    in_specs, out_specs, out_shape, operands = [], [], [], []
    for part, other, w, m, v in zip(parts, others, ws, ms, vs):
        _, r, wd = part.shape
        rows = r // halves
        whole = pl.BlockSpec((rows, wd), lambda i, ids_ref: (i, 0))
        in_specs += [pl.BlockSpec((1, rows, wd), lambda i, ids_ref, k=k: (ids_ref[k], i, 0)) for k in range(4)] + [whole] * 3
        out_specs += [whole] * 4
        out_shape += [jax.ShapeDtypeStruct((r, wd), F32)] * 4
        operands += [part, other, other, other, w, m, v]
    outs = pl.pallas_call(
        body, name=name,
        grid_spec=pltpu.PrefetchScalarGridSpec(
            num_scalar_prefetch=1, grid=(halves,), in_specs=in_specs + [ANY], out_specs=out_specs),
        out_shape=out_shape,
        compiler_params=_params(),
    )(chip_ids, *operands, after)
    return [tuple(outs[4 * j:4 * j + 4]) for j in range(n)]


def pack_small(parts):
    names = [name for name, _ in SMALL if name in parts]
    operands = [parts[name] for name in names]
    first_row, at = {}, 0
    for name, size in SMALL:
        first_row[name] = at // 128
        at += size
    sizes = dict(SMALL)

    def body(*refs):
        out_ref = refs[-1]
        out_ref[...] = jnp.zeros_like(out_ref)
        for name, ref in zip(names, refs):
            row = first_row[name]
            if name == "loss_sum":
                lane0 = lax.broadcasted_iota(jnp.int32, (1, 128), 1) == 0
                out_ref[row:row + 1, :] = jnp.where(lane0, ref[...], 0.0)
            else:
                rows = sizes[name] // 128
                out_ref[row:row + rows, :] = ref[...].reshape(rows, 128)

    vmem = pl.BlockSpec(memory_space=pltpu.VMEM)
    return pl.pallas_call(
        body, name="pack_small", in_specs=[vmem] * len(names), out_specs=vmem,
        out_shape=jax.ShapeDtypeStruct((SMALL_ROWS, 128), F32), compiler_params=_params(()),
    )(*operands)


LATE = "pre_mix_norm"


def adamw_small(packed_g, late_parts, ws, ms, vs):
    names = [name for name, _ in SMALL if name != "loss_sum"]
    k = len(names)
    shapes = [ws[name].shape[1:] if ws[name].ndim > 2 else ws[name].shape for name in names]
    first_row, at = [], 0
    for name, size in SMALL:
        first_row.append(at // 128)
        at += size

    def body(g_ref, late_ref, *refs):
        w_refs, m_refs, v_refs, outs = refs[:k], refs[k:2 * k], refs[2 * k:3 * k], refs[3 * k:]
        for i, (name, size) in enumerate(SMALL[:k]):
            if name == LATE:
                g = late_ref[0]
                for j in range(1, N_DEV):
                    g = g + late_ref[j]
            else:
                g = g_ref[first_row[i]:first_row[i] + size // 128, :].reshape(shapes[i])
            outs[i][...] = g
            outs[k + i][...], outs[2 * k + i][...], outs[3 * k + i][...] = _adamw_update(
                w_refs[i][...], g, m_refs[i][...], v_refs[i][...])
        outs[4 * k][...] = g_ref[first_row[k]:first_row[k] + 1, 0:1]

    vmem = pl.BlockSpec(memory_space=pltpu.VMEM)
    operands = [t[name].reshape(shape) for t in (ws, ms, vs) for name, shape in zip(names, shapes)]
    outs = pl.pallas_call(
        body, name="adamw_small", in_specs=[vmem] * (2 + 3 * k), out_specs=[vmem] * (4 * k + 1),
        out_shape=[jax.ShapeDtypeStruct(shape, F32) for _ in range(4) for shape in shapes] + [jax.ShapeDtypeStruct((1, 1), F32)],
        compiler_params=_params(()),
    )(packed_g, late_parts, *operands)
    tables = [{name: outs[j * k + i].reshape(ws[name].shape) for i, name in enumerate(names)} for j in range(4)]
    return (*tables, outs[4 * k])


def kernel(x, positions, pre_mix_norm, w_in, sgu_ln_gain, sgu_ln_bias, sgu_w_spatial, sgu_b_spatial, attn_out_norm, sgu_out_norm, w_out, post_mix_norm, pre_ffn_norm, w_gate, w_up, w_down, post_ffn_norm, loss_target, m_pre_mix_norm, m_w_in, m_sgu_ln_gain, m_sgu_ln_bias, m_sgu_w_spatial, m_sgu_b_spatial, m_attn_out_norm, m_sgu_out_norm, m_w_out, m_post_mix_norm, m_pre_ffn_norm, m_w_gate, m_w_up, m_w_down, m_post_ffn_norm, v_pre_mix_norm, v_w_in, v_sgu_ln_gain, v_sgu_ln_bias, v_sgu_w_spatial, v_sgu_b_spatial, v_attn_out_norm, v_sgu_out_norm, v_w_out, v_post_mix_norm, v_pre_ffn_norm, v_w_gate, v_w_up, v_w_down, v_post_ffn_norm):
    small_w = dict(pre_mix_norm=pre_mix_norm, sgu_ln_gain=sgu_ln_gain, sgu_ln_bias=sgu_ln_bias, sgu_w_spatial=sgu_w_spatial,
                   sgu_b_spatial=sgu_b_spatial, attn_out_norm=attn_out_norm, sgu_out_norm=sgu_out_norm,
                   post_mix_norm=post_mix_norm, pre_ffn_norm=pre_ffn_norm, post_ffn_norm=post_ffn_norm)
    small_m = dict(pre_mix_norm=m_pre_mix_norm, sgu_ln_gain=m_sgu_ln_gain, sgu_ln_bias=m_sgu_ln_bias, sgu_w_spatial=m_sgu_w_spatial,
                   sgu_b_spatial=m_sgu_b_spatial, attn_out_norm=m_attn_out_norm, sgu_out_norm=m_sgu_out_norm,
                   post_mix_norm=m_post_mix_norm, pre_ffn_norm=m_pre_ffn_norm, post_ffn_norm=m_post_ffn_norm)
    small_v = dict(pre_mix_norm=v_pre_mix_norm, sgu_ln_gain=v_sgu_ln_gain, sgu_ln_bias=v_sgu_ln_bias, sgu_w_spatial=v_sgu_w_spatial,
                   sgu_b_spatial=v_sgu_b_spatial, attn_out_norm=v_attn_out_norm, sgu_out_norm=v_sgu_out_norm,
                   post_mix_norm=v_post_mix_norm, pre_ffn_norm=v_pre_ffn_norm, post_ffn_norm=v_post_ffn_norm)

    x2d = x[0]
    target = loss_target[0]
    pos_col = positions.reshape(SEQ, 1)
    rot = _rot_consts()
    w_sp = sgu_w_spatial[0]
    bfull = jnp.repeat(sgu_b_spatial[0].T, HEAD_DIM, axis=1)

    x_i, y_i, c_i = (lax.axis_index(a).astype(jnp.int32) for a in MESH_AXES)
    dev = 4 * x_i + 2 * y_i + c_i
    core = c_i.reshape(1)
    chip = 2 * x_i + y_i
    chip_ids = jnp.stack([chip, chip ^ 1, chip ^ 2, chip ^ 3])
    dev_ids = jnp.stack([dev ^ m for m in range(N_DEV)])

    def gathered(name, bufs):
        return by_sequencer(name, [gather(bufs)], TO_GATHER)[0][0]

    def from_sibling(name, grads):
        return by_sequencer(name, [to_sibling(grads)], TO_SIBLING)[0][1]

    def from_chips(name, parts):
        return by_sequencer(name, [to_chips(parts)], TO_CHIPS)[0][1]

    (w_in_t,) = place_shards("place_w_in", [w_in[0].T], dev.reshape(1))
    (w_in_t,) = gathered("gather_w_in", [w_in_t])
    w_gate_t, w_up_t, w_out_f, w_down_f = place_shards(
        "place_weights", [w_gate[0].T, w_up[0].T, w_out[0], w_down[0]], dev.reshape(1))
    (w_out_f,) = gathered("gather_w_out", [w_out_f])
    w_gate_t, w_up_t = gathered("gather_w_gate_up", [w_gate_t, w_up_t])
    (w_down_f,) = gathered("gather_w_down", [w_down_f])

    h1, u, vs, q, k, v = in_proj(x2d, pos_col, pre_mix_norm, w_in_t, rot)
    attn_r, lse, attn = attn_fwd(q, k, v)
    (sgu,) = sgu_fwd(u, vs, sgu_ln_gain, sgu_ln_bias, w_sp, bfull)
    mix, y, x2, h2 = out_proj(attn, sgu, x2d, attn_out_norm, sgu_out_norm, w_out_f, post_mix_norm, pre_ffn_norm)
    gate, up, act = ffn_up(h2, w_gate_t, w_up_t)
    df, dx3, d_post_ffn, sq_err = ffn_down_loss(act, w_down_f, x2, post_ffn_norm, target)

    g_w_down = weight_grad("grad_w_down", act, df)
    (s_down,) = from_sibling("w_down_to_sibling", [g_w_down])
    dgate, dup, dx2, dy, d_pre_ffn, d_post_mix = ffn_bwd(
        df, w_down_f, gate, up, w_gate_t, w_up_t, x2, pre_ffn_norm, dx3, y, post_mix_norm, after=[g_w_down])
    (p_down,) = sum_cores("sum_cores_down", [g_w_down], [s_down], core, after=[dy])
    (c_down,) = from_chips("w_down_to_chips", [p_down])
    g_w_gate, g_w_up = weight_grads("grad_w_gate_up", [dgate, dup], h2, after=[p_down])
    s_gate, s_up = from_sibling("w_gate_up_to_sibling", [g_w_gate, g_w_up])
    g_w_out = weight_grad("grad_w_out", mix, dy, after=[g_w_up])
    (s_out,) = from_sibling("w_out_to_sibling", [g_w_out])
    dsgu, d_attn_out, d_sgu_out, dattn_r = mix_bwd(dy, w_out_f, attn, sgu, attn_out_norm, sgu_out_norm, after=[g_w_out, c_down])
    p_gate, p_up = sum_cores("sum_cores_gate_up", [g_w_gate, g_w_up], [s_gate, s_up], core, after=[dsgu])
    c_gate, c_up = from_chips("w_gate_up_to_chips", [p_gate, p_up])
    du, dvs, d_ln_gain, d_ln_bias, d_w_sp, d_bfull = sgu_bwd(
        u, vs, dsgu, sgu_ln_gain, sgu_ln_bias, w_sp, bfull, after=[p_gate, p_up])

    d_b_sp = d_bfull.reshape(CHUNK, N_GROUPS, HEAD_DIM).sum(axis=-1).T
    small_g = pack_small(dict(sgu_ln_gain=d_ln_gain, sgu_ln_bias=d_ln_bias, sgu_w_spatial=d_w_sp, sgu_b_spatial=d_b_sp,
                              attn_out_norm=d_attn_out, sgu_out_norm=d_sgu_out, post_mix_norm=d_post_mix,
                              pre_ffn_norm=d_pre_ffn, post_ffn_norm=d_post_ffn, loss_sum=sq_err))
    small_g = small_g.reshape(N_DEV, SMALL_ROWS // N_DEV, 128)
    ((_, (o_small,)),) = by_sequencer("small_to_owners", [to_owners(small_g)], TO_ALL)

    dq, dk, dv = attn_bwd(q, k, v, attn_r, lse, dattn_r, _to_residue_order(pos_col), rot)
    summed_small = sum_owned("sum_small", small_g, o_small, dev_ids, after=[dq, c_gate, c_up])
    (all_small,) = gathered("gather_small_grads", [summed_small])
    (p_out,) = sum_cores("sum_cores_out", [g_w_out], [s_out], core, after=[dq])
    (c_out,) = from_chips("w_out_to_chips", [p_out])
    dproj = [dq, dk, dv, du, dvs]
    g_w_in = weight_grad_of_parts("grad_w_in", dproj, h1, after=[c_gate, c_up])
    (s_in,) = from_sibling("w_in_to_sibling", [g_w_in])
    grad_x, d_pre_mix = in_bwd(dproj, w_in_t, x2d, pre_mix_norm, dx2, after=[g_w_in, all_small])
    (p_in,) = sum_cores("sum_cores_in", [g_w_in], [s_in], core, after=[d_pre_mix, c_out])
    (_, (c_in,)), (_, (late_parts,)) = by_sequencer(
        "last_sums_to_owners", [to_chips([p_in]), to_everyone(d_pre_mix)], TO_ALL)
    late_parts = lax.dynamic_update_slice(late_parts, d_pre_mix[None], (dev, 0, 0))

    def same(t):
        return t

    def turned(t):
        return t.T

    big, last = {}, p_in
    for call, weights in (("adamw_ffn", (("w_down", w_down, p_down, c_down, m_w_down, v_w_down, same),
                                         ("w_gate", w_gate, p_gate, c_gate, m_w_gate, v_w_gate, turned),
                                         ("w_up", w_up, p_up, c_up, m_w_up, v_w_up, turned))),
                          ("adamw_w_out", (("w_out", w_out, p_out, c_out, m_w_out, v_w_out, same),)),
                          ("adamw_w_in", (("w_in", w_in, p_in, c_in, m_w_in, v_w_in, turned),))):
        results = adamw_of_sums(call, [p for _, _, p, _, _, _, _ in weights], [c for _, _, _, c, _, _, _ in weights], chip_ids,
                                [turn(w[0]) for _, w, _, _, _, _, turn in weights], [turn(m[0]) for _, _, _, _, m, _, turn in weights],
                                [turn(vv[0]) for _, _, _, _, _, vv, turn in weights], last)
        for (name, _, _, _, _, _, turn), outs in zip(weights, results):
            big[name] = tuple(turn(t)[None] for t in outs)
        last = results[-1][0]
    sg, sd, snm, snv, loss_sum = adamw_small(all_small, late_parts, small_w, small_m, small_v)
    loss = loss_sum[0, 0] * np.float32(0.5 / D_MODEL)

    names = ["pre_mix_norm", "w_in", "sgu_ln_gain", "sgu_ln_bias", "sgu_w_spatial", "sgu_b_spatial", "attn_out_norm",
             "sgu_out_norm", "w_out", "post_mix_norm", "pre_ffn_norm", "w_gate", "w_up", "w_down", "post_ffn_norm"]
    outs = [loss, grad_x[None]]
    for i, table in enumerate((sg, sd, snm, snv)):
        for name in names:
            outs.append(big[name][i] if name in big else table[name])
    return tuple(outs)
```

```python
import numpy as np
import jax
import jax.numpy as jnp
from jax import lax
from jax.experimental import pallas as pl
from jax.experimental.pallas import tpu as pltpu
from jax.experimental.pallas import tpu_sc as plsc

F32 = jnp.float32
BF16 = jnp.bfloat16

SEQ = 2048
D_MODEL = 1024
ATTN_W = 512
SGU_W = 512
HEAD_DIM = 64
N_GROUPS = 8
CHUNK = 128
D_FF = 2816
IN_W = 3 * ATTN_W + 2 * SGU_W
DILATIONS = (1, 4, 16)
ROPE_THETA = 500000.0
ROT_DIM = 16
ROT_HALF = 8
RMS_EPS = 1e-6
LN_EPS = 1e-5
Q_SCALE = 0.125
NEG = -1e30

N_DEV = 8
MESH_AXES = ("x", "y", "c")
MESH = pl.DeviceIdType.MESH

ADAM_LR = 0.001
ADAM_B1 = 0.9
ADAM_B2 = 0.999
ADAM_EPS = 1e-08
ADAM_WD = 0.01
ADAM_STEP = 10

VMEM_LIMIT = 60 * 1024 * 1024
ANY = pl.BlockSpec(memory_space=pl.ANY)

SMALL = (("pre_mix_norm", 1024), ("sgu_ln_gain", 512), ("sgu_ln_bias", 512), ("sgu_w_spatial", 8 * 128 * 128),
         ("sgu_b_spatial", 1024), ("attn_out_norm", 512), ("sgu_out_norm", 512), ("post_mix_norm", 1024),
         ("pre_ffn_norm", 1024), ("post_ffn_norm", 1024), ("loss_sum", 1))
SMALL_ROWS = 1152


def _params(sem=("arbitrary",)):
    return pltpu.CompilerParams(dimension_semantics=sem, vmem_limit_bytes=VMEM_LIMIT)


def _dot(a, b):
    return jnp.dot(a, b, preferred_element_type=F32)


def _dot_nt(a, b):
    return lax.dot_general(a, b, (((1,), (1,)), ((), ())), preferred_element_type=F32)


def _dot_tn(a, b):
    return lax.dot_general(a, b, (((0,), (0,)), ((), ())), preferred_element_type=F32)


def _rms(z):
    return lax.rsqrt(jnp.mean(z * z, axis=-1, keepdims=True) + RMS_EPS)


def _rms_bwd(z, gain, d):
    r = _rms(z)
    n = z * r
    dn = d * gain
    dz = r * (dn - n * jnp.mean(dn * n, axis=-1, keepdims=True))
    return dz, jnp.sum(d * n, axis=0, keepdims=True)


def _gelu(z):
    return 0.5 * z * (1.0 + lax.erf(z * np.float32(1.0 / np.sqrt(2.0))))


def _gelu_grad(z):
    cdf = 0.5 * (1.0 + lax.erf(z * np.float32(1.0 / np.sqrt(2.0))))
    return cdf + z * jnp.exp(-0.5 * z * z) * np.float32(1.0 / np.sqrt(2.0 * np.pi))


def _rot_tables(pos_col, invf, ma, mb):
    ang = pos_col.astype(F32) * invf
    s = jnp.sin(ang)
    return jnp.cos(ang), s * ma, s * mb


def _rot(t, c, sa, sb):
    return t * c + pltpu.roll(t, 120, 1) * sa + pltpu.roll(t, 8, 1) * sb


def _rot_t(d, c, sa, sb):
    return d * c + pltpu.roll(d * sa, 8, 1) + pltpu.roll(d * sb, 120, 1)


def _rot_consts():
    lane = np.arange(128) % HEAD_DIM
    inv_freq = (np.float32(ROPE_THETA) ** (-np.arange(0, ROT_DIM, 2, dtype=np.float32) / np.float32(ROT_DIM))).astype(np.float32)
    invf = np.where(lane < ROT_DIM, inv_freq[lane % ROT_HALF], 0.0).astype(np.float32)
    ma = np.where(lane < ROT_HALF, -1.0, 0.0).astype(np.float32)
    mb = np.where((lane >= ROT_HALF) & (lane < ROT_DIM), 1.0, 0.0).astype(np.float32)
    return jnp.asarray(invf[None]), jnp.asarray(ma[None]), jnp.asarray(mb[None])


def _row_spec(tm, w):
    return pl.BlockSpec((tm, w), lambda i: (i, 0))


def _full_spec(shape):
    return pl.BlockSpec(shape, lambda i: (0,) * len(shape))


def _weight_spec(shape):
    return pl.BlockSpec(shape, lambda i: (0,) * len(shape), pipeline_mode=pl.Buffered(1))


FF_CHUNKS = (256, 512, 1024, 1024)
FF_SPANS = [(int(o), n) for o, n in zip(np.cumsum((0,) + FF_CHUNKS[:-1]), FF_CHUNKS)]
FF_WHOLE = [(0, D_FF)]


def _ffn_weight_scratch(n_weights, spans):
    return [pltpu.VMEM((D_FF, D_MODEL), BF16)] * n_weights + [pltpu.SemaphoreType.DMA((n_weights, len(spans)))]


def _with_ffn_weights(w_hbm, w_vmem, sems, spans, run):
    copies = [[pltpu.make_async_copy(h.at[pl.ds(o, n)], v.at[pl.ds(o, n)], sems.at[j, c]) for c, (o, n) in enumerate(spans)]
              for j, (h, v) in enumerate(zip(w_hbm, w_vmem))]
    first = pl.program_id(0) == 0

    @pl.when(first)
    def _():
        for of_weight in copies:
            for cp in of_weight:
                cp.start()
        run(lambda j, c: copies[j][c].wait())

    @pl.when(jnp.logical_not(first))
    def _():
        run(None)


RES = 16


def _residue_scratch(n_arrays, tm, width):
    return [pltpu.VMEM((2, n_arrays, tm // RES, RES, width), F32), pltpu.SemaphoreType.DMA((2, n_arrays, RES))]


def _to_residue_rows(tiles, outs, scratch, sems, tm, n_steps):
    i = pl.program_id(0)
    slot = i % 2
    per = tm // RES

    def copies(step, s):
        return [pltpu.make_async_copy(scratch.at[s, a, :, b, :],
                                      outs[a].at[pl.ds(pl.multiple_of(b * (SEQ // RES) + per * step, per), per), :],
                                      sems.at[s, a, b]) for a in range(len(outs)) for b in range(RES)]

    @pl.when(i >= 2)
    def _():
        for cp in copies(i - 2, slot):
            cp.wait()

    for a, tile in enumerate(tiles):
        scratch[slot, a] = tile.reshape(per, RES, tile.shape[-1])
    for cp in copies(i, slot):
        cp.start()

    @pl.when(i == n_steps - 1)
    def _():
        for cp in copies(i - 1, 1 - slot) + copies(i, slot):
            cp.wait()


def in_proj(x, pos_col, g1, w_in_t, rot):
    tm = 512
    n_steps = SEQ // tm

    def body(x_ref, pos_ref, g_ref, w_ref, invf_ref, ma_ref, mb_ref, h_ref, u_ref, vs_ref, q_ref, k_ref, v_ref, scratch, sems):
        xf = x_ref[...]
        h = (xf * _rms(xf) * g_ref[...]).astype(BF16)
        h_ref[...] = h
        proj = _dot_nt(h, w_ref[...])
        c, sa, sb = _rot_tables(pos_ref[...], invf_ref[...], ma_ref[...], mb_ref[...])
        slabs = range(ATTN_W // 128)
        q = jnp.concatenate([_rot(proj[:, j * 128:(j + 1) * 128], c, sa, sb) * Q_SCALE for j in slabs], axis=1)
        k = jnp.concatenate([_rot(proj[:, ATTN_W + j * 128:ATTN_W + (j + 1) * 128], c, sa, sb) for j in slabs], axis=1)
        u_ref[...] = proj[:, 3 * ATTN_W:3 * ATTN_W + SGU_W]
        vs_ref[...] = proj[:, 3 * ATTN_W + SGU_W:]
        _to_residue_rows([q, k, proj[:, 2 * ATTN_W:3 * ATTN_W]], [q_ref, k_ref, v_ref], scratch, sems, tm, n_steps)

    act = jax.ShapeDtypeStruct((SEQ, 512), F32)
    return _call(
        "in_proj", body, n_steps,
        [_row_spec(tm, D_MODEL), _row_spec(tm, 1), _full_spec((1, D_MODEL)), _weight_spec((IN_W, D_MODEL)),
         _full_spec((1, 128)), _full_spec((1, 128)), _full_spec((1, 128))],
        [_row_spec(tm, D_MODEL)] + [_row_spec(tm, 512)] * 2 + [ANY] * 3,
        [jax.ShapeDtypeStruct((SEQ, D_MODEL), BF16)] + [act] * 5,
        (x, pos_col, g1, w_in_t, *rot), scratch_shapes=_residue_scratch(3, tm, ATTN_W))


def _to_residue_order(t):
    return t.reshape(SEQ // RES, RES, -1).transpose(1, 0, 2).reshape(t.shape)


def _block_rows(d, r, n):
    if d == 16:
        slices = [(128 * r, 128)]
    elif d == 4:
        slices = [(128 * (4 * b + r) + 32 * n, 32) for b in range(4)]
    else:
        slices = [(128 * b + 8 * n, 8) for b in range(RES)]
    return [(s if isinstance(s, int) else pl.multiple_of(s, z), z) for s, z in slices]


def _block_step(d, i):
    if d == 16:
        return i
    if d == 4:
        return 4 * (i & 31) + (i >> 5)
    return 16 * (i & 7) + (i >> 3)


def _attn_masks(d):
    row2 = _block_step(d, lax.broadcasted_iota(jnp.int32, (128, 256), 0))
    col2 = lax.broadcasted_iota(jnp.int32, (128, 256), 1)
    key2 = _block_step(d, col2 & 127)
    mask2 = jnp.logical_or(jnp.logical_and(col2 < 128, key2 >= row2), jnp.logical_and(col2 >= 128, key2 <= row2))
    row1 = _block_step(d, lax.broadcasted_iota(jnp.int32, (128, 128), 0))
    col1 = lax.broadcasted_iota(jnp.int32, (128, 128), 1)
    return col1 < HEAD_DIM, _block_step(d, col1) <= row1, mask2


def _load_rows(ref, slices):
    parts = [ref[pl.ds(s, z), :] for s, z in slices]
    return parts[0] if len(parts) == 1 else jnp.concatenate(parts, axis=0)


def _for_each_group(fn):
    for p, d in enumerate(DILATIONS):
        masks = _attn_masks(d)
        if d == 16:
            def group(i, carry, p=p, masks=masks):
                fn(p, masks, [(_block_rows(16, 8 * i + g, 0), None) for g in range(8)])
                return carry

            lax.fori_loop(0, 2, group, 0)
        elif d == 4:
            fn(p, masks, [(_block_rows(4, r, 0), None) for r in range(4)])

            def group(i, carry, p=p, masks=masks):
                blocks = [6 * i + g for g in range(6)]
                fn(p, masks, [(_block_rows(4, j % 4, 1 + j // 4), _block_rows(4, j % 4, j // 4)) for j in blocks])
                return carry

            lax.fori_loop(0, 2, group, 0)
        else:
            fn(p, masks, [(_block_rows(1, 0, 0), None)])

            def group(i, carry, p=p, masks=masks):
                fn(p, masks, [(_block_rows(1, 0, 5 * i + g + 1), _block_rows(1, 0, 5 * i + g)) for g in range(5)])
                return carry

            lax.fori_loop(0, 3, group, 0)


def attn_fwd(q, k, v):
    def body(q_ref, k_ref, v_ref, o_ref, lse_ref, nat_ref, op_ref, lp_ref, sems):
        def group(p, masks, blocks):
            head0, mask1, mask2 = masks
            heads = (head0, jnp.logical_not(head0))
            keys = [rows if prev is None else prev + rows for rows, prev in blocks]
            mask = [mask1 if prev is None else mask2 for _, prev in blocks]
            qb = [_load_rows(q_ref, rows) for rows, _ in blocks]
            kk = [_load_rows(k_ref, ks).astype(BF16) for ks in keys]
            vv = [_load_rows(v_ref, ks).astype(BF16) for ks in keys]
            chains = [(g, hm) for g in range(len(blocks)) for hm in heads]
            s = [jnp.where(mask[g], _dot_nt(jnp.where(hm, qb[g], 0.0).astype(BF16), kk[g]), NEG) for g, hm in chains]
            m = [jnp.max(t, axis=-1, keepdims=True) for t in s]
            e = [jnp.exp(t - mt) for t, mt in zip(s, m)]
            l = [jnp.sum(t, axis=-1, keepdims=True) for t in e]
            pv = [_dot(t.astype(BF16), vv[g]) for t, (g, _) in zip(e, chains)]
            for g, (rows, _) in enumerate(blocks):
                o_blk = jnp.where(head0, pv[2 * g] / l[2 * g], pv[2 * g + 1] / l[2 * g + 1])
                l_blk = jnp.where(head0, jnp.broadcast_to(m[2 * g] + jnp.log(l[2 * g]), (128, 128)),
                                  jnp.broadcast_to(m[2 * g + 1] + jnp.log(l[2 * g + 1]), (128, 128)))
                at = 0
                for start, size in rows:
                    op_ref[p, pl.ds(start, size), :] = o_blk[at:at + size]
                    lp_ref[p, pl.ds(start, size), :] = l_blk[at:at + size]
                    at += size

        _for_each_group(group)

        def combine(i, carry):
            rows = pl.ds(pl.multiple_of(i * 256, 256), 256)
            ls = [lp_ref[p, rows, :] for p in range(3)]
            m = jnp.maximum(jnp.maximum(ls[0], ls[1]), ls[2])
            lse = m + jnp.log(jnp.exp(ls[0] - m) + jnp.exp(ls[1] - m) + jnp.exp(ls[2] - m))
            o = jnp.zeros((256, 128), F32)
            for p in range(3):
                o = o + jnp.exp(ls[p] - lse) * op_ref[p, rows, :]
            o_ref[rows, :] = o
            lse_ref[rows, :] = lse
            return carry

        lax.fori_loop(0, SEQ // 256, combine, 0)

        lanes = pl.ds(pl.multiple_of(pl.program_id(0) * 128, 128), 128)
        back = [pltpu.make_async_copy(o_ref.at[pl.ds(b * (SEQ // RES), SEQ // RES), :], nat_ref.at[:, b, lanes], sems.at[b])
                for b in range(RES)]
        for cp in back:
            cp.start()
        for cp in back:
            cp.wait()

    slab = pl.BlockSpec((SEQ, 128), lambda i: (0, i))
    out = jax.ShapeDtypeStruct((SEQ, ATTN_W), F32)
    attn_r, lse, attn = _call(
        "attn_fwd", body, ATTN_W // 128, [slab] * 3, [slab] * 2 + [ANY],
        [out, out, jax.ShapeDtypeStruct((SEQ // RES, RES, ATTN_W), F32)], (q, k, v),
        scratch_shapes=[pltpu.VMEM((3, SEQ, 128), F32), pltpu.VMEM((3, SEQ, 128), F32), pltpu.SemaphoreType.DMA((RES,))])
    return attn_r, lse, attn.reshape(SEQ, ATTN_W)


def _causal_weights(w_ref):
    row = lax.broadcasted_iota(jnp.int32, (CHUNK, CHUNK), 0)
    col = lax.broadcasted_iota(jnp.int32, (CHUNK, CHUNK), 1)
    return [jnp.where(col <= row, w_ref[g], 0.0).astype(BF16) for g in range(N_GROUPS)], col <= row


def _sgu_chunk_fwd(u, vs, lg, lb, wc, bfull, head0):
    ug = _gelu(u)
    vg = _gelu(vs)
    xc = vg - jnp.mean(vg, axis=-1, keepdims=True)
    rstd = lax.rsqrt(jnp.mean(xc * xc, axis=-1, keepdims=True) + LN_EPS)
    xhat = xc * rstd
    vn = xhat * lg + lb
    mixed = []
    for gp in range(SGU_W // 128):
        vp = vn[:, gp * 128:(gp + 1) * 128].astype(BF16)
        mixed.append(jnp.where(head0, _dot(wc[2 * gp], vp), _dot(wc[2 * gp + 1], vp)))
    ms = jnp.concatenate(mixed, axis=1) + bfull
    return ug, xhat, rstd, vn, ms


def sgu_fwd(u, vs, lg, lb, w_sp, bfull):
    cpb = 4

    def body(u_ref, vs_ref, lg_ref, lb_ref, w_ref, b_ref, o_ref):
        wc, _ = _causal_weights(w_ref)
        head0 = lax.broadcasted_iota(jnp.int32, (CHUNK, 128), 1) < HEAD_DIM
        for ci in range(cpb):
            rows = pl.ds(ci * CHUNK, CHUNK)
            ug, _, _, _, ms = _sgu_chunk_fwd(u_ref[rows, :], vs_ref[rows, :], lg_ref[...], lb_ref[...], wc, b_ref[...], head0)
            o_ref[rows, :] = ug * ms

    tm = cpb * CHUNK
    return _call(
        "sgu_fwd", body, SEQ // tm,
        [_row_spec(tm, SGU_W), _row_spec(tm, SGU_W), _full_spec((1, SGU_W)), _full_spec((1, SGU_W)),
         _full_spec((N_GROUPS, CHUNK, CHUNK)), _full_spec((CHUNK, SGU_W))],
        [_row_spec(tm, SGU_W)], [jax.ShapeDtypeStruct((SEQ, SGU_W), F32)],
        (u, vs, lg, lb, w_sp, bfull))


def out_proj(attn, sgu, x, ga, gs, w_out, gpm, gpf):
    tm = 512

    def body(a_ref, s_ref, x_ref, ga_ref, gs_ref, w_ref, gpm_ref, gpf_ref, mix_ref, y_ref, x2_ref, h2_ref):
        a = a_ref[...]
        s = s_ref[...]
        an = (a * _rms(a) * ga_ref[...]).astype(BF16)
        sn = (s * _rms(s) * gs_ref[...]).astype(BF16)
        mix_ref[:, :ATTN_W] = an
        mix_ref[:, ATTN_W:] = sn
        y = _dot(an, w_ref[:ATTN_W, :]) + _dot(sn, w_ref[ATTN_W:, :])
        y_ref[...] = y
        x2 = x_ref[...] + y * _rms(y) * gpm_ref[...]
        x2_ref[...] = x2
        h2_ref[...] = (x2 * _rms(x2) * gpf_ref[...]).astype(BF16)

    wide = jax.ShapeDtypeStruct((SEQ, D_MODEL), F32)
    wide16 = jax.ShapeDtypeStruct((SEQ, D_MODEL), BF16)
    return _call(
        "out_proj", body, SEQ // tm,
        [_row_spec(tm, ATTN_W), _row_spec(tm, SGU_W), _row_spec(tm, D_MODEL), _full_spec((1, ATTN_W)),
         _full_spec((1, SGU_W)), _weight_spec((D_MODEL, D_MODEL)), _full_spec((1, D_MODEL)), _full_spec((1, D_MODEL))],
        [_row_spec(tm, D_MODEL)] * 4, [wide16, wide, wide, wide16],
        (attn, sgu, x, ga, gs, w_out, gpm, gpf))


def ffn_up(h2, w_gate_t, w_up_t):
    tm = 256

    def body(h_ref, wg_hbm, wu_hbm, g_ref, u_ref, a_ref, wg_ref, wu_ref, sems):
        def run(wait):
            h = h_ref[...]
            if wait:
                wait(0, 0)
            g = _dot_nt(h, wg_ref[...])
            g_ref[...] = g.astype(BF16)
            if wait:
                wait(1, 0)
            u = _dot_nt(h, wu_ref[...])
            u_ref[...] = u.astype(BF16)
            a_ref[...] = (g * jax.nn.sigmoid(g) * u).astype(BF16)

        _with_ffn_weights([wg_hbm, wu_hbm], [wg_ref, wu_ref], sems, FF_WHOLE, run)

    ff = jax.ShapeDtypeStruct((SEQ, D_FF), BF16)
    return _call(
        "ffn_up", body, SEQ // tm, [_row_spec(tm, D_MODEL), ANY, ANY],
        [_row_spec(tm, D_FF)] * 3, [ff, ff, jax.ShapeDtypeStruct((SEQ, D_FF), BF16)],
        (h2, w_gate_t, w_up_t), scratch_shapes=_ffn_weight_scratch(2, FF_WHOLE))


def ffn_down_loss(act, w_down, x2, gpo, target):
    tm = 512

    def body(a_ref, w_hbm, x2_ref, g_ref, t_ref, df_ref, dx3_ref, dg_ref, loss_ref, w_ref, sems):
        def run(wait):
            f = None
            for c, (o, n) in enumerate(FF_SPANS if wait else FF_WHOLE):
                if wait:
                    wait(0, c)
                part = _dot(a_ref[:, o:o + n], w_ref[o:o + n, :])
                f = part if f is None else f + part
            gain = g_ref[...]
            err = x2_ref[...] + f * _rms(f) * gain - t_ref[...]
            dx3 = err * np.float32(1.0 / D_MODEL)
            dx3_ref[...] = dx3
            df, dg = _rms_bwd(f, gain, dx3)
            df_ref[...] = df.astype(BF16)
            loss = jnp.sum(err * err, axis=(0, 1), keepdims=True)
            if wait:
                dg_ref[...] = dg
                loss_ref[...] = loss
            else:
                dg_ref[...] += dg
                loss_ref[...] += loss

        _with_ffn_weights([w_hbm], [w_ref], sems, FF_SPANS, run)

    return _call(
        "ffn_down_loss", body, SEQ // tm,
        [_row_spec(tm, D_FF), ANY, _row_spec(tm, D_MODEL), _full_spec((1, D_MODEL)), _row_spec(tm, D_MODEL)],
        [_row_spec(tm, D_MODEL), _row_spec(tm, D_MODEL), _full_spec((1, D_MODEL)), _full_spec((1, 1))],
        [jax.ShapeDtypeStruct((SEQ, D_MODEL), BF16), jax.ShapeDtypeStruct((SEQ, D_MODEL), F32),
         jax.ShapeDtypeStruct((1, D_MODEL), F32), jax.ShapeDtypeStruct((1, 1), F32)],
        (act, w_down, x2, gpo, target), scratch_shapes=_ffn_weight_scratch(1, FF_SPANS))


def ffn_bwd(df, w_down, gate, up, w_gate_t, w_up_t, x2, gpf, dx3, y, gpm, after=()):
    tm = 256

    def body(df_ref, wd_hbm, g_ref, u_ref, wg_hbm, wu_hbm, x2_ref, gpf_ref, dx3_ref, y_ref, gpm_ref,
             dg_ref, du_ref, dx2_ref, dy_ref, dgpf_ref, dgpm_ref, wd_ref, wg_ref, wu_ref, sems):
        def run(wait):
            if wait:
                wait(0, 0)
            dact = _dot_nt(df_ref[...], wd_ref[...])
            g = g_ref[...].astype(F32)
            s = jax.nn.sigmoid(g)
            dup = (dact * g * s).astype(BF16)
            dgate = (dact * u_ref[...].astype(F32) * (s * (1.0 + g * (1.0 - s)))).astype(BF16)
            du_ref[...] = dup
            dg_ref[...] = dgate
            if wait:
                wait(1, 0)
                wait(2, 0)
            dh2 = _dot(dgate, wg_ref[...]) + _dot(dup, wu_ref[...])
            dz, dgpf = _rms_bwd(x2_ref[...], gpf_ref[...], dh2)
            dx2 = dx3_ref[...] + dz
            dx2_ref[...] = dx2
            dy, dgpm = _rms_bwd(y_ref[...], gpm_ref[...], dx2)
            dy_ref[...] = dy.astype(BF16)
            if wait:
                dgpf_ref[...] = dgpf
                dgpm_ref[...] = dgpm
            else:
                dgpf_ref[...] += dgpf
                dgpm_ref[...] += dgpm

        _with_ffn_weights([wd_hbm, wg_hbm, wu_hbm], [wd_ref, wg_ref, wu_ref], sems, FF_WHOLE, run)

    vec = jax.ShapeDtypeStruct((1, D_MODEL), F32)
    ff16 = jax.ShapeDtypeStruct((SEQ, D_FF), BF16)
    return _call(
        "ffn_bwd", body, SEQ // tm,
        [_row_spec(tm, D_MODEL), ANY, _row_spec(tm, D_FF), _row_spec(tm, D_FF), ANY, ANY, _row_spec(tm, D_MODEL),
         _full_spec((1, D_MODEL)), _row_spec(tm, D_MODEL), _row_spec(tm, D_MODEL), _full_spec((1, D_MODEL))],
        [_row_spec(tm, D_FF), _row_spec(tm, D_FF), _row_spec(tm, D_MODEL), _row_spec(tm, D_MODEL),
         _full_spec((1, D_MODEL)), _full_spec((1, D_MODEL))],
        [ff16, ff16, jax.ShapeDtypeStruct((SEQ, D_MODEL), F32), jax.ShapeDtypeStruct((SEQ, D_MODEL), BF16), vec, vec],
        (df, w_down, gate, up, w_gate_t, w_up_t, x2, gpf, dx3, y, gpm), scratch_shapes=_ffn_weight_scratch(3, FF_WHOLE), after=after)


def weight_grads(name, lhs, b, after=()):
    m, n, k = lhs[0].shape[1], b.shape[1], len(lhs)
    tr = 256

    def body(*refs):
        for a_ref, o_ref in zip(refs[:k], refs[k + 1:]):
            o_ref[...] = _dot_tn(a_ref[...], refs[k][...]).astype(BF16)

    outs = _call(
        name, body, m // tr, [pl.BlockSpec((SEQ, tr), lambda i: (0, i))] * k + [_weight_spec((SEQ, n))],
        [_row_spec(tr, n)] * k, [jax.ShapeDtypeStruct((m, n), BF16)] * k, (*lhs, b), after=after)
    return [out.reshape(N_DEV, m // N_DEV, n) for out in outs]


def weight_grad(name, a, b, after=()):
    return weight_grads(name, [a], b, after)[0]


def weight_grad_of_parts(name, parts, b, after=()):
    p, n, k = parts[0].shape[1], b.shape[1], len(parts)
    tr = 512
    per = p // tr

    def body(*refs):
        tile = pl.program_id(0)
        for j in range(k):
            @pl.when(tile // per == j)
            def _(j=j):
                refs[k + 1][...] = _dot_tn(refs[j][...].astype(BF16), refs[k][...]).astype(BF16)

    def part_spec(j):
        return pl.BlockSpec((SEQ, tr), lambda i: (0, jnp.clip(i - per * j, 0, per - 1)))

    (out,) = _call(
        name, body, k * per, [part_spec(j) for j in range(k)] + [_weight_spec((SEQ, n))],
        [_row_spec(tr, n)], [jax.ShapeDtypeStruct((k * p, n), BF16)], (*parts, b), after=after)
    return out.reshape(N_DEV, k * p // N_DEV, n)


def mix_bwd(dy, w_out, attn, sgu, ga, gs, after=()):
    tm = 512
    n_steps = SEQ // tm

    def body(dy_ref, w_ref, a_ref, s_ref, ga_ref, gs_ref, ds_ref, dga_ref, dgs_ref, da_ref, scratch, sems):
        dy = dy_ref[...]
        da, dga = _rms_bwd(a_ref[...], ga_ref[...], _dot_nt(dy, w_ref[:ATTN_W, :]))
        ds, dgs = _rms_bwd(s_ref[...], gs_ref[...], _dot_nt(dy, w_ref[ATTN_W:, :]))
        ds_ref[...] = ds
        _to_residue_rows([da], [da_ref], scratch, sems, tm, n_steps)

        @pl.when(pl.program_id(0) == 0)
        def _():
            dga_ref[...] = jnp.zeros_like(dga_ref)
            dgs_ref[...] = jnp.zeros_like(dgs_ref)

        dga_ref[...] += dga
        dgs_ref[...] += dgs

    half = jax.ShapeDtypeStruct((SEQ, 512), F32)
    vec = jax.ShapeDtypeStruct((1, 512), F32)
    return _call(
        "mix_bwd", body, n_steps,
        [_row_spec(tm, D_MODEL), _weight_spec((D_MODEL, D_MODEL)), _row_spec(tm, 512), _row_spec(tm, 512),
         _full_spec((1, 512)), _full_spec((1, 512))],
        [_row_spec(tm, 512), _full_spec((1, 512)), _full_spec((1, 512)), ANY],
        [half, vec, vec, half], (dy, w_out, attn, sgu, ga, gs), scratch_shapes=_residue_scratch(1, tm, ATTN_W), after=after)


def sgu_bwd(u, vs, dsgu, lg, lb, w_sp, bfull, after=()):
    cpb = 4

    def body(u_ref, vs_ref, d_ref, lg_ref, lb_ref, w_ref, b_ref, du_ref, dvs_ref, dlg_ref, dlb_ref, dw_ref, db_ref):
        wc, causal = _causal_weights(w_ref)
        head0 = lax.broadcasted_iota(jnp.int32, (CHUNK, 128), 1) < HEAD_DIM
        lg = lg_ref[...]

        @pl.when(pl.program_id(0) == 0)
        def _():
            dlg_ref[...] = jnp.zeros_like(dlg_ref)
            dlb_ref[...] = jnp.zeros_like(dlb_ref)
            dw_ref[...] = jnp.zeros_like(dw_ref)
            db_ref[...] = jnp.zeros_like(db_ref)

        for ci in range(cpb):
            rows = pl.ds(ci * CHUNK, CHUNK)
            u = u_ref[rows, :]
            vs = vs_ref[rows, :]
            d = d_ref[rows, :]
            ug, xhat, rstd, vn, ms = _sgu_chunk_fwd(u, vs, lg, lb_ref[...], wc, b_ref[...], head0)
            du_ref[rows, :] = (d * ms * _gelu_grad(u)).astype(BF16)
            dms = d * ug
            db_ref[...] += dms
            dvn = []
            for gp in range(SGU_W // 128):
                dmp = dms[:, gp * 128:(gp + 1) * 128]
                dm0 = jnp.where(head0, dmp, 0.0).astype(BF16)
                dm1 = jnp.where(head0, 0.0, dmp).astype(BF16)
                vp = vn[:, gp * 128:(gp + 1) * 128].astype(BF16)
                dw_ref[2 * gp] += _dot_nt(dm0, vp)
                dw_ref[2 * gp + 1] += _dot_nt(dm1, vp)
                dvn.append(_dot_tn(wc[2 * gp], dm0) + _dot_tn(wc[2 * gp + 1], dm1))
            dvn = jnp.concatenate(dvn, axis=1)
            dlg_ref[...] += jnp.sum(dvn * xhat, axis=0, keepdims=True)
            dlb_ref[...] += jnp.sum(dvn, axis=0, keepdims=True)
            dxh = dvn * lg
            dvg = rstd * (dxh - jnp.mean(dxh, axis=-1, keepdims=True) - xhat * jnp.mean(dxh * xhat, axis=-1, keepdims=True))
            dvs_ref[rows, :] = (dvg * _gelu_grad(vs)).astype(BF16)

        @pl.when(pl.program_id(0) == pl.num_programs(0) - 1)
        def _():
            for g in range(N_GROUPS):
                dw_ref[g] = jnp.where(causal, dw_ref[g], 0.0)

    tm = cpb * CHUNK
    half16 = jax.ShapeDtypeStruct((SEQ, SGU_W), BF16)
    vec = jax.ShapeDtypeStruct((1, SGU_W), F32)
    return _call(
        "sgu_bwd", body, SEQ // tm,
        [_row_spec(tm, SGU_W)] * 3 + [_full_spec((1, SGU_W)), _full_spec((1, SGU_W)),
                                      _full_spec((N_GROUPS, CHUNK, CHUNK)), _full_spec((CHUNK, SGU_W))],
        [_row_spec(tm, SGU_W), _row_spec(tm, SGU_W), _full_spec((1, SGU_W)), _full_spec((1, SGU_W)),
         _full_spec((N_GROUPS, CHUNK, CHUNK)), _full_spec((CHUNK, SGU_W))],
        [half16, half16, vec, vec, jax.ShapeDtypeStruct((N_GROUPS, CHUNK, CHUNK), F32),
         jax.ShapeDtypeStruct((CHUNK, SGU_W), F32)],
        (u, vs, dsgu, lg, lb, w_sp, bfull), after=after)


def attn_bwd(q, k, v, o, lse, do, pos_col, rot):
    n_steps = ATTN_W // 128

    def body(q_ref, k_ref, v_ref, o_ref, lse_ref, do_ref, pos_ref, invf_ref, ma_ref, mb_ref,
             dq_ref, dk_ref, dv_ref, dqa_ref, dka_ref, dva_ref, dlt_ref, rot_ref, out_ref, sems):
        step = pl.program_id(0)
        slot = step % 2

        def back(at_step, s):
            lanes = pl.ds(pl.multiple_of(at_step * 128, 128), 128)
            return [pltpu.make_async_copy(out_ref.at[s, a, pl.ds(b * (SEQ // RES), SEQ // RES), :], nat.at[:, b, lanes],
                                          sems.at[s, a, b]) for a, nat in enumerate((dq_ref, dk_ref, dv_ref)) for b in range(RES)]

        dqa_ref[...] = jnp.zeros_like(dqa_ref)
        dka_ref[...] = jnp.zeros_like(dka_ref)
        dva_ref[...] = jnp.zeros_like(dva_ref)

        def delta(i, carry):
            rows = pl.ds(pl.multiple_of(i * 256, 256), 256)
            prod = do_ref[rows, :] * o_ref[rows, :]
            h0 = lax.broadcasted_iota(jnp.int32, (256, 128), 1) < HEAD_DIM
            d0 = jnp.sum(jnp.where(h0, prod, 0.0), axis=-1, keepdims=True)
            d1 = jnp.sum(jnp.where(h0, 0.0, prod), axis=-1, keepdims=True)
            dlt_ref[rows, :] = jnp.where(h0, d0, d1)
            return carry

        lax.fori_loop(0, SEQ // 256, delta, 0)

        def add_rows(ref, slices, val):
            at = 0
            for start, size in slices:
                ref[pl.ds(start, size), :] += val[at:at + size]
                at += size

        def group(p, masks, blocks):
            head0, mask1, mask2 = masks
            heads = (head0, jnp.logical_not(head0))
            keys = [rows if prev is None else prev + rows for rows, prev in blocks]
            mask = [mask1 if prev is None else mask2 for _, prev in blocks]
            kk = [_load_rows(k_ref, ks).astype(BF16) for ks in keys]
            vv = [_load_rows(v_ref, ks).astype(BF16) for ks in keys]
            qb = [_load_rows(q_ref, rows) for rows, _ in blocks]
            dob = [_load_rows(do_ref, rows) for rows, _ in blocks]
            lse_b = [_load_rows(lse_ref, rows) for rows, _ in blocks]
            dlt_b = [_load_rows(dlt_ref, rows) for rows, _ in blocks]
            chains = [(g, h) for g in range(len(blocks)) for h in range(2)]
            qm = [jnp.where(heads[h], qb[g], 0.0).astype(BF16) for g, h in chains]
            dom = [jnp.where(heads[h], dob[g], 0.0).astype(BF16) for g, h in chains]
            s = [_dot_nt(qm[c], kk[g]) for c, (g, h) in enumerate(chains)]
            dp = [_dot_nt(dom[c], vv[g]) for c, (g, h) in enumerate(chains)]
            pr = [jnp.where(mask[g], jnp.exp(s[c] - lse_b[g][:, h * HEAD_DIM:h * HEAD_DIM + 1]), 0.0)
                  for c, (g, h) in enumerate(chains)]
            ds = [(pr[c] * (dp[c] - dlt_b[g][:, h * HEAD_DIM:h * HEAD_DIM + 1])).astype(BF16)
                  for c, (g, h) in enumerate(chains)]
            dv = [_dot_tn(pr[c].astype(BF16), dom[c]) for c in range(len(chains))]
            dk = [_dot_tn(ds[c], qm[c]) for c in range(len(chains))]
            dq = [_dot(ds[c], kk[g]) for c, (g, h) in enumerate(chains)]
            for g, (rows, _) in enumerate(blocks):
                add_rows(dqa_ref, rows, jnp.where(head0, dq[2 * g], dq[2 * g + 1]))
                add_rows(dka_ref, keys[g], dk[2 * g] + dk[2 * g + 1])
                add_rows(dva_ref, keys[g], dv[2 * g] + dv[2 * g + 1])

        _for_each_group(group)

        @pl.when(pl.program_id(0) == 0)
        def _():
            def tables(i, carry):
                rows = pl.ds(pl.multiple_of(i * 256, 256), 256)
                c, sa, sb = _rot_tables(pos_ref[rows, :], invf_ref[...], ma_ref[...], mb_ref[...])
                rot_ref[0, rows, :] = c
                rot_ref[1, rows, :] = sa
                rot_ref[2, rows, :] = sb
                return carry

            lax.fori_loop(0, SEQ // 256, tables, 0)

        @pl.when(step >= 2)
        def _():
            for cp in back(step - 2, slot):
                cp.wait()

        def finish(i, carry):
            rows = pl.ds(pl.multiple_of(i * 256, 256), 256)
            c, sa, sb = rot_ref[0, rows, :], rot_ref[1, rows, :], rot_ref[2, rows, :]
            out_ref[slot, 0, rows, :] = _rot_t(dqa_ref[rows, :] * Q_SCALE, c, sa, sb)
            out_ref[slot, 1, rows, :] = _rot_t(dka_ref[rows, :], c, sa, sb)
            out_ref[slot, 2, rows, :] = dva_ref[rows, :]
            return carry

        lax.fori_loop(0, SEQ // 256, finish, 0)
        for cp in back(step, slot):
            cp.start()

        @pl.when(step == n_steps - 1)
        def _():
            for cp in back(step - 1, 1 - slot) + back(step, slot):
                cp.wait()

    slab = pl.BlockSpec((SEQ, 128), lambda i: (0, i))
    out = jax.ShapeDtypeStruct((SEQ // RES, RES, ATTN_W), F32)
    acc = pltpu.VMEM((SEQ, 128), F32)
    outs = _call(
        "attn_bwd", body, n_steps,
        [slab] * 6 + [_full_spec((SEQ, 1)), _full_spec((1, 128)), _full_spec((1, 128)), _full_spec((1, 128))],
        [ANY] * 3, [out, out, out], (q, k, v, o, lse, do, pos_col, *rot),
        scratch_shapes=[acc, acc, acc, acc, pltpu.VMEM((3, SEQ, 128), F32), pltpu.VMEM((2, 3, SEQ, 128), F32),
                        pltpu.SemaphoreType.DMA((2, 3, RES))])
    return [t.reshape(SEQ, ATTN_W) for t in outs]


def in_bwd(dproj_parts, w_in_t, x, g1, dx2, after=()):
    tm = 512
    k = len(dproj_parts)

    def body(*refs):
        w_ref, x_ref, g_ref, dx2_ref, dx_ref, dg_ref = refs[k:]
        dh1 = _dot(refs[0][...].astype(BF16), w_ref[0:512, :])
        for j in range(1, k):
            dh1 = dh1 + _dot(refs[j][...].astype(BF16), w_ref[512 * j:512 * (j + 1), :])
        dz, dg = _rms_bwd(x_ref[...], g_ref[...], dh1)
        dx_ref[...] = dx2_ref[...] + dz

        @pl.when(pl.program_id(0) == 0)
        def _():
            dg_ref[...] = jnp.zeros_like(dg_ref)

        dg_ref[...] += dg

    return _call(
        "in_bwd", body, SEQ // tm,
        [_row_spec(tm, 512)] * k + [_weight_spec((IN_W, D_MODEL)), _row_spec(tm, D_MODEL), _full_spec((1, D_MODEL)),
                                    _row_spec(tm, D_MODEL)],
        [_row_spec(tm, D_MODEL), _full_spec((1, D_MODEL))],
        [jax.ShapeDtypeStruct((SEQ, D_MODEL), F32), jax.ShapeDtypeStruct((1, D_MODEL), F32)],
        (*dproj_parts, w_in_t, x, g1, dx2), after=after)


def _coords():
    return lax.axis_index("x"), lax.axis_index("y"), lax.axis_index("c")


class Exchange:
    def __init__(self, srcs, bufs, new_shapes, n_sems, make):
        self.srcs, self.bufs, self.new_shapes, self.n_sems, self.make = list(srcs), list(bufs), list(new_shapes), n_sems, make


def _call(name, body, n_steps, in_specs, out_specs, out_shape, args, scratch_shapes=(), after=()):
    n_in = len(args)

    def wrapped(*refs):
        body(*refs[:n_in], *refs[n_in + len(after):])

    return list(pl.pallas_call(
        wrapped, name=name, grid=(n_steps,), in_specs=list(in_specs) + [ANY] * len(after), out_specs=list(out_specs),
        out_shape=list(out_shape), scratch_shapes=list(scratch_shapes), compiler_params=_params(),
    )(*args, *after))


GATHER_SEMS = 8


def gather(bufs):
    n = len(bufs)

    def make(src_refs, buf_refs, new_refs, send_sems, recv_sems):
        x, y, c = _coords()
        me, sibling = (x, y, c), (x, y, 1 - c)
        over_x, over_y, across = (1 - x, y), (x, 1 - y), (1 - x, 1 - y)

        def copy(a, k, block, to, half=None):
            r = buf_refs[a].shape[0] // N_DEV
            lo, size = (0, r) if half is None else (half * (r // 2), r // 2)
            rows = buf_refs[a].at[pl.ds((4 * block[0] + 2 * block[1] + block[2]) * r + lo, size), :]
            return pltpu.make_async_remote_copy(
                src_ref=rows, dst_ref=rows, send_sem=send_sems.at[GATHER_SEMS * a + k],
                recv_sem=recv_sems.at[GATHER_SEMS * a + k], device_id=to, device_id_type=MESH)

        every = range(n)
        out = ([copy(a, 0, me, sibling) for a in every] + [copy(a, 1, me, (*over_x, c)) for a in every]
               + [copy(a, 2, me, (*over_y, c)) for a in every])
        near_in = [copy(a, 1, (*over_x, c), me) for a in every] + [copy(a, 2, (*over_y, c), me) for a in every]
        relay = ([copy(a, 3, (*over_x, c), (*over_y, c), half=0) for a in every]
                 + [copy(a, 4, (*over_y, c), (*over_x, c), half=1) for a in every])
        near_on = [copy(a, 5, (*over_x, c), sibling) for a in every] + [copy(a, 6, (*over_y, c), sibling) for a in every]
        relay_in = ([copy(a, 3, (*across, c), me, half=0) for a in every]
                    + [copy(a, 4, (*across, c), me, half=1) for a in every])
        far_on = [copy(a, 7, (*across, c), sibling) for a in every]
        from_core = ([copy(a, 0, sibling, me) for a in every] + [copy(a, 5, (*over_x, 1 - c), me) for a in every]
                     + [copy(a, 6, (*over_y, 1 - c), me) for a in every] + [copy(a, 7, (*across, 1 - c), me) for a in every])
        stages = [([], out), (near_in, relay + near_on), (relay_in, far_on)]
        return stages, out + relay + near_on + far_on, from_core

    return Exchange([], bufs, [], GATHER_SEMS * n, make)


TO_GATHER = (1, lambda x, y, c: [(x, y, 1 - c), (1 - x, y, c), (x, 1 - y, c)])
TO_SIBLING = (2, lambda x, y, c: [(x, y, 1 - c)])
TO_CHIPS = (3, lambda x, y, c: [(1 - x, y, c), (x, 1 - y, c), (1 - x, 1 - y, c)])
TO_ALL = (4, lambda x, y, c: [(x ^ (m >> 2), y ^ ((m >> 1) & 1), c ^ (m & 1)) for m in range(1, N_DEV)])


def by_sequencer(name, exchanges, who):
    collective_id, peers_of = who
    hbm = pltpu.MemorySpace.HBM
    refs = [([jax.new_ref(a, memory_space=hbm) for a in ex.srcs], [jax.new_ref(a, memory_space=hbm) for a in ex.bufs],
             [jax.empty_ref(s, memory_space=hbm) for s in ex.new_shapes]) for ex in exchanges]
    sems = []
    for ex in exchanges:
        sems += [pltpu.SemaphoreType.DMA((ex.n_sems,)), pltpu.SemaphoreType.DMA((ex.n_sems,))]

    @pl.kernel(mesh=plsc.ScalarSubcoreMesh(axis_name="sequencer", num_cores=1), name=name, scratch_types=tuple(sems),
               compiler_params=pltpu.CompilerParams(collective_id=collective_id))
    def launch(*sem_refs):
        peers = peers_of(*_coords())
        barrier = pltpu.get_barrier_semaphore()
        for peer in peers:
            pl.semaphore_signal(barrier, inc=1, device_id=peer, device_id_type=MESH)
        pl.semaphore_wait(barrier, len(peers))

        made = [ex.make(*refs[k], sem_refs[2 * k], sem_refs[2 * k + 1]) for k, ex in enumerate(exchanges)]
        for stage in range(max(len(stages) for stages, _, _ in made)):
            for stages, _, _ in made:
                if stage < len(stages):
                    arrivals, starts = stages[stage]
                    for cp in arrivals:
                        cp.wait_recv()
                    for cp in starts:
                        cp.start()
        for _, sends, arrivals in made:
            for cp in arrivals:
                cp.wait_recv()
            for cp in sends:
                cp.wait_send()

    launch()
    return [([ref[...] for ref in bufs], [ref[...] for ref in news]) for _, bufs, news in refs]


def place_shards(name, shards, dev):
    n = len(shards)

    def body(dev_ref, *refs):
        for a in range(n):
            refs[n + a][...] = refs[a][...].astype(BF16)

    spec = pltpu.PrefetchScalarGridSpec(
        num_scalar_prefetch=1, grid=(1,),
        in_specs=[pl.BlockSpec(s.shape, lambda i, dev_ref: (0, 0)) for s in shards],
        out_specs=[pl.BlockSpec(s.shape, lambda i, dev_ref: (dev_ref[0], 0)) for s in shards])
    return pl.pallas_call(
        body, name=name, grid_spec=spec,
        out_shape=[jax.ShapeDtypeStruct((N_DEV * s.shape[0], s.shape[1]), BF16) for s in shards],
        compiler_params=_params(),
    )(dev, *shards)


def _swap(copies_of):
    def make(src_refs, buf_refs, new_refs, send_sems, recv_sems):
        copies = copies_of(src_refs, new_refs, send_sems, recv_sems)
        return [([], copies)], copies, copies

    return make


def to_sibling(grads):
    def copies_of(src_refs, new_refs, send_sems, recv_sems):
        x, y, c = _coords()
        return [pltpu.make_async_remote_copy(
            src_ref=src_refs[a].at[2 * xy + 1 - c], dst_ref=new_refs[a].at[xy], send_sem=send_sems.at[4 * a + xy],
            recv_sem=recv_sems.at[4 * a + xy], device_id=(x, y, 1 - c), device_id_type=MESH)
            for a in range(len(src_refs)) for xy in range(4)]

    return Exchange(grads, [], [jax.ShapeDtypeStruct((4,) + g.shape[1:], g.dtype) for g in grads], 4 * len(grads),
                    _swap(copies_of))


def to_chips(parts):
    def copies_of(src_refs, new_refs, send_sems, recv_sems):
        x, y, c = _coords()
        chips = [(1 - x, y), (x, 1 - y), (1 - x, 1 - y)]
        return [pltpu.make_async_remote_copy(
            src_ref=src_refs[a].at[2 * px + py], dst_ref=new_refs[a].at[2 * x + y], send_sem=send_sems.at[3 * a + j],
            recv_sem=recv_sems.at[3 * a + j], device_id=(px, py, c), device_id_type=MESH)
            for a in range(len(src_refs)) for j, (px, py) in enumerate(chips)]

    return Exchange(parts, [], [jax.ShapeDtypeStruct(p.shape, p.dtype) for p in parts], 3 * len(parts), _swap(copies_of))


def to_owners(grad):
    def copies_of(src_refs, new_refs, send_sems, recv_sems):
        x, y, c = _coords()
        copies = []
        for m in range(1, N_DEV):
            px, py, pc = x ^ (m >> 2), y ^ ((m >> 1) & 1), c ^ (m & 1)
            copies.append(pltpu.make_async_remote_copy(
                src_ref=src_refs[0].at[4 * px + 2 * py + pc], dst_ref=new_refs[0].at[4 * x + 2 * y + c],
                send_sem=send_sems.at[m - 1], recv_sem=recv_sems.at[m - 1], device_id=(px, py, pc), device_id_type=MESH))
        return copies

    return Exchange([grad], [], [jax.ShapeDtypeStruct(grad.shape, grad.dtype)], N_DEV - 1, _swap(copies_of))


def to_everyone(vec):
    def copies_of(src_refs, new_refs, send_sems, recv_sems):
        x, y, c = _coords()
        copies = []
        for m in range(1, N_DEV):
            px, py, pc = x ^ (m >> 2), y ^ ((m >> 1) & 1), c ^ (m & 1)
            copies.append(pltpu.make_async_remote_copy(
                src_ref=src_refs[0], dst_ref=new_refs[0].at[4 * x + 2 * y + c],
                send_sem=send_sems.at[m - 1], recv_sem=recv_sems.at[m - 1], device_id=(px, py, pc), device_id_type=MESH))
        return copies

    return Exchange([vec], [], [jax.ShapeDtypeStruct((N_DEV,) + vec.shape, vec.dtype)], N_DEV - 1, _swap(copies_of))


def sum_cores(name, grads, others, core, after=()):
    k = len(grads)

    def body(core_ref, *refs):
        for j in range(k):
            out_ref = refs[2 * k + len(after) + j]
            out_ref[...] = (refs[j][:, 0].astype(F32) + refs[k + j][...].astype(F32)).astype(out_ref.dtype)

    mine = [pl.BlockSpec((2, 1) + o.shape[1:], lambda i, core_ref: (i, core_ref[0], 0, 0)) for o in others]
    theirs = [pl.BlockSpec((2,) + o.shape[1:], lambda i, core_ref: (i, 0, 0)) for o in others]
    return pl.pallas_call(
        body, name=name,
        grid_spec=pltpu.PrefetchScalarGridSpec(
            num_scalar_prefetch=1, grid=(2,), in_specs=mine + theirs + [ANY] * len(after), out_specs=theirs),
        out_shape=[jax.ShapeDtypeStruct(o.shape, o.dtype) for o in others],
        compiler_params=_params(),
    )(core, *[g.reshape((4, 2) + g.shape[1:]) for g in grads], *others, *after)


def sum_owned(name, grad, others, dev_ids, after=()):
    _, r, w = grad.shape

    def body(ids_ref, *refs):
        acc = refs[0][0]
        for k in range(1, N_DEV):
            acc = acc + refs[k][0]
        refs[-1][...] = acc

    def pick(k):
        return pl.BlockSpec((1, r, w), lambda i, ids_ref: (ids_ref[k], 0, 0))

    return pl.pallas_call(
        body, name=name,
        grid_spec=pltpu.PrefetchScalarGridSpec(
            num_scalar_prefetch=1, grid=(1,), in_specs=[pick(k) for k in range(N_DEV)] + [ANY] * len(after),
            out_specs=pl.BlockSpec((r, w), lambda i, ids_ref: (ids_ref[0], 0))),
        out_shape=jax.ShapeDtypeStruct((N_DEV * r, w), F32),
        compiler_params=_params(),
    )(dev_ids, grad, *([others] * (N_DEV - 1)), *after)


def _adamw_update(w, g, m, v):
    nm = ADAM_B1 * m + np.float32(1.0 - ADAM_B1) * g
    nv = ADAM_B2 * v + np.float32(1.0 - ADAM_B2) * (g * g)
    m_hat = nm / np.float32(1.0 - ADAM_B1 ** ADAM_STEP)
    v_hat = nv / np.float32(1.0 - ADAM_B2 ** ADAM_STEP)
    return -ADAM_LR * (m_hat / (jnp.sqrt(v_hat) + ADAM_EPS) + ADAM_WD * w), nm, nv


def adamw_of_sums(name, parts, others, chip_ids, ws, ms, vs, after):
    n = len(parts)
    halves = 2

    def body(ids_ref, *refs):
        outs = refs[7 * n + len(after):]
        for j in range(n):
            p_ref, a_ref, b_ref, c_ref, w_ref, m_ref, v_ref = refs[7 * j:7 * j + 7]
            g = ((p_ref[0].astype(F32) + a_ref[0].astype(F32)) + b_ref[0].astype(F32)) + c_ref[0].astype(F32)
            outs[4 * j][...] = g
            outs[4 * j + 1][...], outs[4 * j + 2][...], outs[4 * j + 3][...] = _adamw_update(w_ref[...], g, m_ref[...], v_ref[...])

    in_specs, out_specs, out_shape, operands = [], [], [], []
    for part, other, w, m, v in zip(parts, others, ws, ms, vs):
        _, r, wd = part.shape
        rows = r // halves
        whole = pl.BlockSpec((rows, wd), lambda i, ids_ref: (i, 0))
        in_specs += [pl.BlockSpec((1, rows, wd), lambda i, ids_ref, k=k: (ids_ref[k], i, 0)) for k in range(4)] + [whole] * 3
        out_specs += [whole] * 4
        out_shape += [jax.ShapeDtypeStruct((r, wd), F32)] * 4
        operands += [part, other, other, other, w, m, v]
    outs = pl.pallas_call(
        body, name=name,
        grid_spec=pltpu.PrefetchScalarGridSpec(
            num_scalar_prefetch=1, grid=(halves,), in_specs=in_specs + [ANY] * len(after), out_specs=out_specs),
        out_shape=out_shape,
        compiler_params=_params(),
    )(chip_ids, *operands, *after)
    return [tuple(outs[4 * j:4 * j + 4]) for j in range(n)]


def pack_small(parts):
    names = [name for name, _ in SMALL if name in parts]
    operands = [parts[name] for name in names]
    first_row, at = {}, 0
    for name, size in SMALL:
        first_row[name] = at // 128
        at += size
    sizes = dict(SMALL)

    def body(*refs):
        out_ref = refs[-1]
        out_ref[...] = jnp.zeros_like(out_ref)
        for name, ref in zip(names, refs):
            row = first_row[name]
            if name == "loss_sum":
                lane0 = lax.broadcasted_iota(jnp.int32, (1, 128), 1) == 0
                out_ref[row:row + 1, :] = jnp.where(lane0, ref[...], 0.0)
            else:
                rows = sizes[name] // 128
                out_ref[row:row + rows, :] = ref[...].reshape(rows, 128)

    vmem = pl.BlockSpec(memory_space=pltpu.VMEM)
    return pl.pallas_call(
        body, name="pack_small", in_specs=[vmem] * len(names), out_specs=vmem,
        out_shape=jax.ShapeDtypeStruct((SMALL_ROWS, 128), F32), compiler_params=_params(()),
    )(*operands)


LATE = "pre_mix_norm"


def adamw_small(packed_g, late_parts, ws, ms, vs, after=()):
    names = [name for name, _ in SMALL if name != "loss_sum"]
    k = len(names)
    shapes = [ws[name].shape[1:] if ws[name].ndim > 2 else ws[name].shape for name in names]
    first_row, at = [], 0
    for name, size in SMALL:
        first_row.append(at // 128)
        at += size

    def body(g_ref, late_ref, *refs):
        w_refs, m_refs, v_refs, outs = refs[:k], refs[k:2 * k], refs[2 * k:3 * k], refs[3 * k + len(after):]
        for i, (name, size) in enumerate(SMALL[:k]):
            if name == LATE:
                g = late_ref[0]
                for j in range(1, N_DEV):
                    g = g + late_ref[j]
            else:
                g = g_ref[first_row[i]:first_row[i] + size // 128, :].reshape(shapes[i])
            outs[i][...] = g
            outs[k + i][...], outs[2 * k + i][...], outs[3 * k + i][...] = _adamw_update(
                w_refs[i][...], g, m_refs[i][...], v_refs[i][...])
        outs[4 * k][...] = g_ref[first_row[k]:first_row[k] + 1, 0:1]

    vmem = pl.BlockSpec(memory_space=pltpu.VMEM)
    operands = [t[name].reshape(shape) for t in (ws, ms, vs) for name, shape in zip(names, shapes)]
    outs = pl.pallas_call(
        body, name="adamw_small", in_specs=[vmem] * (2 + 3 * k) + [ANY] * len(after), out_specs=[vmem] * (4 * k + 1),
        out_shape=[jax.ShapeDtypeStruct(shape, F32) for _ in range(4) for shape in shapes] + [jax.ShapeDtypeStruct((1, 1), F32)],
        compiler_params=_params(()),
    )(packed_g, late_parts, *operands, *after)
    tables = [{name: outs[j * k + i].reshape(ws[name].shape) for i, name in enumerate(names)} for j in range(4)]
    return (*tables, outs[4 * k])


def kernel(x, positions, pre_mix_norm, w_in, sgu_ln_gain, sgu_ln_bias, sgu_w_spatial, sgu_b_spatial, attn_out_norm, sgu_out_norm, w_out, post_mix_norm, pre_ffn_norm, w_gate, w_up, w_down, post_ffn_norm, loss_target, m_pre_mix_norm, m_w_in, m_sgu_ln_gain, m_sgu_ln_bias, m_sgu_w_spatial, m_sgu_b_spatial, m_attn_out_norm, m_sgu_out_norm, m_w_out, m_post_mix_norm, m_pre_ffn_norm, m_w_gate, m_w_up, m_w_down, m_post_ffn_norm, v_pre_mix_norm, v_w_in, v_sgu_ln_gain, v_sgu_ln_bias, v_sgu_w_spatial, v_sgu_b_spatial, v_attn_out_norm, v_sgu_out_norm, v_w_out, v_post_mix_norm, v_pre_ffn_norm, v_w_gate, v_w_up, v_w_down, v_post_ffn_norm):
    small_w = dict(pre_mix_norm=pre_mix_norm, sgu_ln_gain=sgu_ln_gain, sgu_ln_bias=sgu_ln_bias, sgu_w_spatial=sgu_w_spatial,
                   sgu_b_spatial=sgu_b_spatial, attn_out_norm=attn_out_norm, sgu_out_norm=sgu_out_norm,
                   post_mix_norm=post_mix_norm, pre_ffn_norm=pre_ffn_norm, post_ffn_norm=post_ffn_norm)
    small_m = dict(pre_mix_norm=m_pre_mix_norm, sgu_ln_gain=m_sgu_ln_gain, sgu_ln_bias=m_sgu_ln_bias, sgu_w_spatial=m_sgu_w_spatial,
                   sgu_b_spatial=m_sgu_b_spatial, attn_out_norm=m_attn_out_norm, sgu_out_norm=m_sgu_out_norm,
                   post_mix_norm=m_post_mix_norm, pre_ffn_norm=m_pre_ffn_norm, post_ffn_norm=m_post_ffn_norm)
    small_v = dict(pre_mix_norm=v_pre_mix_norm, sgu_ln_gain=v_sgu_ln_gain, sgu_ln_bias=v_sgu_ln_bias, sgu_w_spatial=v_sgu_w_spatial,
                   sgu_b_spatial=v_sgu_b_spatial, attn_out_norm=v_attn_out_norm, sgu_out_norm=v_sgu_out_norm,
                   post_mix_norm=v_post_mix_norm, pre_ffn_norm=v_pre_ffn_norm, post_ffn_norm=v_post_ffn_norm)

    x2d = x[0]
    target = loss_target[0]
    pos_col = positions.reshape(SEQ, 1)
    rot = _rot_consts()
    w_sp = sgu_w_spatial[0]
    bfull = jnp.repeat(sgu_b_spatial[0].T, HEAD_DIM, axis=1)

    x_i, y_i, c_i = (lax.axis_index(a).astype(jnp.int32) for a in MESH_AXES)
    dev = 4 * x_i + 2 * y_i + c_i
    core = c_i.reshape(1)
    chip = 2 * x_i + y_i
    chip_ids = jnp.stack([chip, chip ^ 1, chip ^ 2, chip ^ 3])
    dev_ids = jnp.stack([dev ^ m for m in range(N_DEV)])

    def gathered(name, bufs):
        return by_sequencer(name, [gather(bufs)], TO_GATHER)[0][0]

    def from_sibling(name, grads):
        return by_sequencer(name, [to_sibling(grads)], TO_SIBLING)[0][1]

    def from_chips(name, parts):
        return by_sequencer(name, [to_chips(parts)], TO_CHIPS)[0][1]

    (w_in_t,) = place_shards("place_w_in", [w_in[0].T], dev.reshape(1))
    (w_in_t,) = gathered("gather_w_in", [w_in_t])
    w_gate_t, w_up_t, w_out_f, w_down_f = place_shards(
        "place_weights", [w_gate[0].T, w_up[0].T, w_out[0], w_down[0]], dev.reshape(1))
    (w_out_f,) = gathered("gather_w_out", [w_out_f])
    w_gate_t, w_up_t = gathered("gather_w_gate_up", [w_gate_t, w_up_t])
    (w_down_f,) = gathered("gather_w_down", [w_down_f])

    h1, u, vs, q, k, v = in_proj(x2d, pos_col, pre_mix_norm, w_in_t, rot)
    attn_r, lse, attn = attn_fwd(q, k, v)
    (sgu,) = sgu_fwd(u, vs, sgu_ln_gain, sgu_ln_bias, w_sp, bfull)
    mix, y, x2, h2 = out_proj(attn, sgu, x2d, attn_out_norm, sgu_out_norm, w_out_f, post_mix_norm, pre_ffn_norm)
    gate, up, act = ffn_up(h2, w_gate_t, w_up_t)
    df, dx3, d_post_ffn, sq_err = ffn_down_loss(act, w_down_f, x2, post_ffn_norm, target)

    g_w_down = weight_grad("grad_w_down", act, df)
    (s_down,) = from_sibling("w_down_to_sibling", [g_w_down])
    dgate, dup, dx2, dy, d_pre_ffn, d_post_mix = ffn_bwd(
        df, w_down_f, gate, up, w_gate_t, w_up_t, x2, pre_ffn_norm, dx3, y, post_mix_norm, after=[g_w_down])
    (p_down,) = sum_cores("sum_cores_down", [g_w_down], [s_down], core, after=[dy])
    (c_down,) = from_chips("w_down_to_chips", [p_down])
    g_w_gate, g_w_up = weight_grads("grad_w_gate_up", [dgate, dup], h2, after=[p_down])
    s_gate, s_up = from_sibling("w_gate_up_to_sibling", [g_w_gate, g_w_up])
    g_w_out = weight_grad("grad_w_out", mix, dy, after=[g_w_up])
    (s_out,) = from_sibling("w_out_to_sibling", [g_w_out])
    dsgu, d_attn_out, d_sgu_out, dattn_r = mix_bwd(dy, w_out_f, attn, sgu, attn_out_norm, sgu_out_norm, after=[g_w_out, c_down])
    p_gate, p_up = sum_cores("sum_cores_gate_up", [g_w_gate, g_w_up], [s_gate, s_up], core, after=[dsgu])
    c_gate, c_up = from_chips("w_gate_up_to_chips", [p_gate, p_up])
    du, dvs, d_ln_gain, d_ln_bias, d_w_sp, d_bfull = sgu_bwd(
        u, vs, dsgu, sgu_ln_gain, sgu_ln_bias, w_sp, bfull, after=[p_gate, p_up])

    d_b_sp = d_bfull.reshape(CHUNK, N_GROUPS, HEAD_DIM).sum(axis=-1).T
    small_g = pack_small(dict(sgu_ln_gain=d_ln_gain, sgu_ln_bias=d_ln_bias, sgu_w_spatial=d_w_sp, sgu_b_spatial=d_b_sp,
                              attn_out_norm=d_attn_out, sgu_out_norm=d_sgu_out, post_mix_norm=d_post_mix,
                              pre_ffn_norm=d_pre_ffn, post_ffn_norm=d_post_ffn, loss_sum=sq_err))
    small_g = small_g.reshape(N_DEV, SMALL_ROWS // N_DEV, 128)
    ((_, (o_small,)),) = by_sequencer("small_to_owners", [to_owners(small_g)], TO_ALL)

    dq, dk, dv = attn_bwd(q, k, v, attn_r, lse, dattn_r, _to_residue_order(pos_col), rot)
    summed_small = sum_owned("sum_small", small_g, o_small, dev_ids, after=[dq, c_gate, c_up])
    (all_small,) = gathered("gather_small_grads", [summed_small])
    (p_out,) = sum_cores("sum_cores_out", [g_w_out], [s_out], core, after=[dq])
    (c_out,) = from_chips("w_out_to_chips", [p_out])
    dproj = [dq, dk, dv, du, dvs]
    g_w_in = weight_grad_of_parts("grad_w_in", dproj, h1, after=[c_gate, c_up])
    (s_in,) = from_sibling("w_in_to_sibling", [g_w_in])

    def same(t):
        return t

    def turned(t):
        return t.T

    big = {}

    def adamw(call, weights, after):
        results = adamw_of_sums(call, [p for _, _, p, _, _, _, _ in weights], [c for _, _, _, c, _, _, _ in weights], chip_ids,
                                [turn(w[0]) for _, w, _, _, _, _, turn in weights], [turn(m[0]) for _, _, _, _, m, _, turn in weights],
                                [turn(vv[0]) for _, _, _, _, _, vv, turn in weights], after)
        for (name, _, _, _, _, _, turn), outs in zip(weights, results):
            big[name] = tuple(turn(t)[None] for t in outs)
        return results[-1][0]

    done_ffn = adamw("adamw_ffn", (("w_down", w_down, p_down, c_down, m_w_down, v_w_down, same),
                                   ("w_gate", w_gate, p_gate, c_gate, m_w_gate, v_w_gate, turned),
                                   ("w_up", w_up, p_up, c_up, m_w_up, v_w_up, turned)), [g_w_in, all_small])
    (p_in,) = sum_cores("sum_cores_in", [g_w_in], [s_in], core, after=[done_ffn, c_out])
    (c_in,) = from_chips("w_in_to_chips", [p_in])
    grad_x, d_pre_mix = in_bwd(dproj, w_in_t, x2d, pre_mix_norm, dx2, after=[p_in])
    ((_, (late_parts,)),) = by_sequencer("pre_mix_to_everyone", [to_everyone(d_pre_mix)], TO_ALL)
    late_parts = lax.dynamic_update_slice(late_parts, d_pre_mix[None], (dev, 0, 0))
    done_out = adamw("adamw_w_out", (("w_out", w_out, p_out, c_out, m_w_out, v_w_out, same),), [d_pre_mix])
    done_in = adamw("adamw_w_in", (("w_in", w_in, p_in, c_in, m_w_in, v_w_in, turned),), [done_out])
    sg, sd, snm, snv, loss_sum = adamw_small(all_small, late_parts, small_w, small_m, small_v, after=[done_in])
    loss = loss_sum[0, 0] * np.float32(0.5 / D_MODEL)

    names = ["pre_mix_norm", "w_in", "sgu_ln_gain", "sgu_ln_bias", "sgu_w_spatial", "sgu_b_spatial", "attn_out_norm",
             "sgu_out_norm", "w_out", "post_mix_norm", "pre_ffn_norm", "w_gate", "w_up", "w_down", "post_ffn_norm"]
    outs = [loss, grad_x[None]]
    for i, table in enumerate((sg, sd, snm, snv)):
        for name in names:
            outs.append(big[name][i] if name in big else table[name])
    return tuple(outs)
```

```python
import numpy as np
import jax
import jax.numpy as jnp
from jax import lax
from jax.experimental import pallas as pl
from jax.experimental.pallas import tpu as pltpu
from jax.experimental.pallas import tpu_sc as plsc

F32 = jnp.float32
BF16 = jnp.bfloat16

SEQ = 2048
D_MODEL = 1024
ATTN_W = 512
SGU_W = 512
HEAD_DIM = 64
N_GROUPS = 8
CHUNK = 128
D_FF = 2816
IN_W = 3 * ATTN_W + 2 * SGU_W
DILATIONS = (1, 4, 16)
ROPE_THETA = 500000.0
ROT_DIM = 16
ROT_HALF = 8
RMS_EPS = 1e-6
LN_EPS = 1e-5
Q_SCALE = 0.125
NEG = -1e30

N_DEV = 8
MESH_AXES = ("x", "y", "c")
MESH = pl.DeviceIdType.MESH

ADAM_LR = 0.001
ADAM_B1 = 0.9
ADAM_B2 = 0.999
ADAM_EPS = 1e-08
ADAM_WD = 0.01
ADAM_STEP = 10

VMEM_LIMIT = 60 * 1024 * 1024
ANY = pl.BlockSpec(memory_space=pl.ANY)

SMALL = (("pre_mix_norm", 1024), ("sgu_ln_gain", 512), ("sgu_ln_bias", 512), ("sgu_w_spatial", 8 * 128 * 128),
         ("sgu_b_spatial", 1024), ("attn_out_norm", 512), ("sgu_out_norm", 512), ("post_mix_norm", 1024),
         ("pre_ffn_norm", 1024), ("post_ffn_norm", 1024), ("loss_sum", 1))
SMALL_ROWS = 1152


def _params(sem=("arbitrary",)):
    return pltpu.CompilerParams(dimension_semantics=sem, vmem_limit_bytes=VMEM_LIMIT)


def _dot(a, b):
    return jnp.dot(a, b, preferred_element_type=F32)


def _dot_nt(a, b):
    return lax.dot_general(a, b, (((1,), (1,)), ((), ())), preferred_element_type=F32)


def _dot_tn(a, b):
    return lax.dot_general(a, b, (((0,), (0,)), ((), ())), preferred_element_type=F32)


def _rms(z):
    return lax.rsqrt(jnp.mean(z * z, axis=-1, keepdims=True) + RMS_EPS)


def _rms_bwd(z, gain, d):
    r = _rms(z)
    n = z * r
    dn = d * gain
    dz = r * (dn - n * jnp.mean(dn * n, axis=-1, keepdims=True))
    return dz, jnp.sum(d * n, axis=0, keepdims=True)


def _gelu(z):
    return 0.5 * z * (1.0 + lax.erf(z * np.float32(1.0 / np.sqrt(2.0))))


def _gelu_grad(z):
    cdf = 0.5 * (1.0 + lax.erf(z * np.float32(1.0 / np.sqrt(2.0))))
    return cdf + z * jnp.exp(-0.5 * z * z) * np.float32(1.0 / np.sqrt(2.0 * np.pi))


def _rot_tables(pos_col, invf, ma, mb):
    ang = pos_col.astype(F32) * invf
    s = jnp.sin(ang)
    return jnp.cos(ang), s * ma, s * mb


def _rot(t, c, sa, sb):
    return t * c + pltpu.roll(t, 120, 1) * sa + pltpu.roll(t, 8, 1) * sb


def _rot_t(d, c, sa, sb):
    return d * c + pltpu.roll(d * sa, 8, 1) + pltpu.roll(d * sb, 120, 1)


def _rot_consts():
    lane = np.arange(128) % HEAD_DIM
    inv_freq = (np.float32(ROPE_THETA) ** (-np.arange(0, ROT_DIM, 2, dtype=np.float32) / np.float32(ROT_DIM))).astype(np.float32)
    invf = np.where(lane < ROT_DIM, inv_freq[lane % ROT_HALF], 0.0).astype(np.float32)
    ma = np.where(lane < ROT_HALF, -1.0, 0.0).astype(np.float32)
    mb = np.where((lane >= ROT_HALF) & (lane < ROT_DIM), 1.0, 0.0).astype(np.float32)
    return jnp.asarray(invf[None]), jnp.asarray(ma[None]), jnp.asarray(mb[None])


def _row_spec(tm, w):
    return pl.BlockSpec((tm, w), lambda i: (i, 0))


def _full_spec(shape):
    return pl.BlockSpec(shape, lambda i: (0,) * len(shape))


def _weight_spec(shape):
    return pl.BlockSpec(shape, lambda i: (0,) * len(shape), pipeline_mode=pl.Buffered(1))


FF_CHUNKS = (256, 512, 1024, 1024)
FF_SPANS = [(int(o), n) for o, n in zip(np.cumsum((0,) + FF_CHUNKS[:-1]), FF_CHUNKS)]
FF_WHOLE = [(0, D_FF)]


def _ffn_weight_scratch(n_weights, spans):
    return [pltpu.VMEM((D_FF, D_MODEL), BF16)] * n_weights + [pltpu.SemaphoreType.DMA((n_weights, len(spans)))]


def _with_ffn_weights(w_hbm, w_vmem, sems, spans, run):
    copies = [[pltpu.make_async_copy(h.at[pl.ds(o, n)], v.at[pl.ds(o, n)], sems.at[j, c]) for c, (o, n) in enumerate(spans)]
              for j, (h, v) in enumerate(zip(w_hbm, w_vmem))]
    first = pl.program_id(0) == 0

    @pl.when(first)
    def _():
        for of_weight in copies:
            for cp in of_weight:
                cp.start()
        run(lambda j, c: copies[j][c].wait())

    @pl.when(jnp.logical_not(first))
    def _():
        run(None)


RES = 16


def _residue_scratch(n_arrays, tm, width):
    return [pltpu.VMEM((2, n_arrays, tm // RES, RES, width), F32), pltpu.SemaphoreType.DMA((2, n_arrays, RES))]


def _to_residue_rows(tiles, outs, scratch, sems, tm, n_steps):
    i = pl.program_id(0)
    slot = i % 2
    per = tm // RES

    def copies(step, s):
        return [pltpu.make_async_copy(scratch.at[s, a, :, b, :],
                                      outs[a].at[pl.ds(pl.multiple_of(b * (SEQ // RES) + per * step, per), per), :],
                                      sems.at[s, a, b]) for a in range(len(outs)) for b in range(RES)]

    @pl.when(i >= 2)
    def _():
        for cp in copies(i - 2, slot):
            cp.wait()

    for a, tile in enumerate(tiles):
        scratch[slot, a] = tile.reshape(per, RES, tile.shape[-1])
    for cp in copies(i, slot):
        cp.start()

    @pl.when(i == n_steps - 1)
    def _():
        for cp in copies(i - 1, 1 - slot) + copies(i, slot):
            cp.wait()


def in_proj(x, pos_col, g1, w_in_t, rot):
    tm = 512
    n_steps = SEQ // tm

    def body(x_ref, pos_ref, g_ref, w_ref, invf_ref, ma_ref, mb_ref, h_ref, u_ref, vs_ref, q_ref, k_ref, v_ref, scratch, sems):
        xf = x_ref[...]
        h = (xf * _rms(xf) * g_ref[...]).astype(BF16)
        h_ref[...] = h
        proj = _dot_nt(h, w_ref[...])
        c, sa, sb = _rot_tables(pos_ref[...], invf_ref[...], ma_ref[...], mb_ref[...])
        slabs = range(ATTN_W // 128)
        q = jnp.concatenate([_rot(proj[:, j * 128:(j + 1) * 128], c, sa, sb) * Q_SCALE for j in slabs], axis=1)
        k = jnp.concatenate([_rot(proj[:, ATTN_W + j * 128:ATTN_W + (j + 1) * 128], c, sa, sb) for j in slabs], axis=1)
        u_ref[...] = proj[:, 3 * ATTN_W:3 * ATTN_W + SGU_W]
        vs_ref[...] = proj[:, 3 * ATTN_W + SGU_W:]
        _to_residue_rows([q, k, proj[:, 2 * ATTN_W:3 * ATTN_W]], [q_ref, k_ref, v_ref], scratch, sems, tm, n_steps)

    act = jax.ShapeDtypeStruct((SEQ, 512), F32)
    return _call(
        "in_proj", body, n_steps,
        [_row_spec(tm, D_MODEL), _row_spec(tm, 1), _full_spec((1, D_MODEL)), _weight_spec((IN_W, D_MODEL)),
         _full_spec((1, 128)), _full_spec((1, 128)), _full_spec((1, 128))],
        [_row_spec(tm, D_MODEL)] + [_row_spec(tm, 512)] * 2 + [ANY] * 3,
        [jax.ShapeDtypeStruct((SEQ, D_MODEL), BF16)] + [act] * 5,
        (x, pos_col, g1, w_in_t, *rot), scratch_shapes=_residue_scratch(3, tm, ATTN_W))


def _to_residue_order(t):
    return t.reshape(SEQ // RES, RES, -1).transpose(1, 0, 2).reshape(t.shape)


def _block_rows(d, r, n):
    if d == 16:
        slices = [(128 * r, 128)]
    elif d == 4:
        slices = [(128 * (4 * b + r) + 32 * n, 32) for b in range(4)]
    else:
        slices = [(128 * b + 8 * n, 8) for b in range(RES)]
    return [(s if isinstance(s, int) else pl.multiple_of(s, z), z) for s, z in slices]


def _block_step(d, i):
    if d == 16:
        return i
    if d == 4:
        return 4 * (i & 31) + (i >> 5)
    return 16 * (i & 7) + (i >> 3)


def _attn_masks(d):
    row2 = _block_step(d, lax.broadcasted_iota(jnp.int32, (128, 256), 0))
    col2 = lax.broadcasted_iota(jnp.int32, (128, 256), 1)
    key2 = _block_step(d, col2 & 127)
    mask2 = jnp.logical_or(jnp.logical_and(col2 < 128, key2 >= row2), jnp.logical_and(col2 >= 128, key2 <= row2))
    row1 = _block_step(d, lax.broadcasted_iota(jnp.int32, (128, 128), 0))
    col1 = lax.broadcasted_iota(jnp.int32, (128, 128), 1)
    return col1 < HEAD_DIM, _block_step(d, col1) <= row1, mask2


def _load_rows(ref, slices):
    parts = [ref[pl.ds(s, z), :] for s, z in slices]
    return parts[0] if len(parts) == 1 else jnp.concatenate(parts, axis=0)


def _for_each_group(fn):
    for p, d in enumerate(DILATIONS):
        masks = _attn_masks(d)
        if d == 16:
            def group(i, carry, p=p, masks=masks):
                fn(p, masks, [(_block_rows(16, 8 * i + g, 0), None) for g in range(8)])
                return carry

            lax.fori_loop(0, 2, group, 0)
        elif d == 4:
            fn(p, masks, [(_block_rows(4, r, 0), None) for r in range(4)])

            def group(i, carry, p=p, masks=masks):
                blocks = [6 * i + g for g in range(6)]
                fn(p, masks, [(_block_rows(4, j % 4, 1 + j // 4), _block_rows(4, j % 4, j // 4)) for j in blocks])
                return carry

            lax.fori_loop(0, 2, group, 0)
        else:
            fn(p, masks, [(_block_rows(1, 0, 0), None)])

            def group(i, carry, p=p, masks=masks):
                fn(p, masks, [(_block_rows(1, 0, 5 * i + g + 1), _block_rows(1, 0, 5 * i + g)) for g in range(5)])
                return carry

            lax.fori_loop(0, 3, group, 0)


def attn_fwd(q, k, v):
    def body(q_ref, k_ref, v_ref, o_ref, lse_ref, nat_ref, op_ref, lp_ref, sems):
        def group(p, masks, blocks):
            head0, mask1, mask2 = masks
            heads = (head0, jnp.logical_not(head0))
            keys = [rows if prev is None else prev + rows for rows, prev in blocks]
            mask = [mask1 if prev is None else mask2 for _, prev in blocks]
            qb = [_load_rows(q_ref, rows) for rows, _ in blocks]
            kk = [_load_rows(k_ref, ks).astype(BF16) for ks in keys]
            vv = [_load_rows(v_ref, ks).astype(BF16) for ks in keys]
            chains = [(g, hm) for g in range(len(blocks)) for hm in heads]
            s = [jnp.where(mask[g], _dot_nt(jnp.where(hm, qb[g], 0.0).astype(BF16), kk[g]), NEG) for g, hm in chains]
            m = [jnp.max(t, axis=-1, keepdims=True) for t in s]
            e = [jnp.exp(t - mt) for t, mt in zip(s, m)]
            l = [jnp.sum(t, axis=-1, keepdims=True) for t in e]
            pv = [_dot(t.astype(BF16), vv[g]) for t, (g, _) in zip(e, chains)]
            for g, (rows, _) in enumerate(blocks):
                o_blk = jnp.where(head0, pv[2 * g] / l[2 * g], pv[2 * g + 1] / l[2 * g + 1])
                l_blk = jnp.where(head0, jnp.broadcast_to(m[2 * g] + jnp.log(l[2 * g]), (128, 128)),
                                  jnp.broadcast_to(m[2 * g + 1] + jnp.log(l[2 * g + 1]), (128, 128)))
                at = 0
                for start, size in rows:
                    op_ref[p, pl.ds(start, size), :] = o_blk[at:at + size]
                    lp_ref[p, pl.ds(start, size), :] = l_blk[at:at + size]
                    at += size

        _for_each_group(group)

        def combine(i, carry):
            rows = pl.ds(pl.multiple_of(i * 256, 256), 256)
            ls = [lp_ref[p, rows, :] for p in range(3)]
            m = jnp.maximum(jnp.maximum(ls[0], ls[1]), ls[2])
            lse = m + jnp.log(jnp.exp(ls[0] - m) + jnp.exp(ls[1] - m) + jnp.exp(ls[2] - m))
            o = jnp.zeros((256, 128), F32)
            for p in range(3):
                o = o + jnp.exp(ls[p] - lse) * op_ref[p, rows, :]
            o_ref[rows, :] = o
            lse_ref[rows, :] = lse
            return carry

        lax.fori_loop(0, SEQ // 256, combine, 0)

        lanes = pl.ds(pl.multiple_of(pl.program_id(0) * 128, 128), 128)
        back = [pltpu.make_async_copy(o_ref.at[pl.ds(b * (SEQ // RES), SEQ // RES), :], nat_ref.at[:, b, lanes], sems.at[b])
                for b in range(RES)]
        for cp in back:
            cp.start()
        for cp in back:
            cp.wait()

    slab = pl.BlockSpec((SEQ, 128), lambda i: (0, i))
    out = jax.ShapeDtypeStruct((SEQ, ATTN_W), F32)
    attn_r, lse, attn = _call(
        "attn_fwd", body, ATTN_W // 128, [slab] * 3, [slab] * 2 + [ANY],
        [out, out, jax.ShapeDtypeStruct((SEQ // RES, RES, ATTN_W), F32)], (q, k, v),
        scratch_shapes=[pltpu.VMEM((3, SEQ, 128), F32), pltpu.VMEM((3, SEQ, 128), F32), pltpu.SemaphoreType.DMA((RES,))])
    return attn_r, lse, attn.reshape(SEQ, ATTN_W)


def _causal_weights(w_ref):
    row = lax.broadcasted_iota(jnp.int32, (CHUNK, CHUNK), 0)
    col = lax.broadcasted_iota(jnp.int32, (CHUNK, CHUNK), 1)
    return [jnp.where(col <= row, w_ref[g], 0.0).astype(BF16) for g in range(N_GROUPS)], col <= row


def _sgu_chunk_fwd(u, vs, lg, lb, wc, bfull, head0):
    ug = _gelu(u)
    vg = _gelu(vs)
    xc = vg - jnp.mean(vg, axis=-1, keepdims=True)
    rstd = lax.rsqrt(jnp.mean(xc * xc, axis=-1, keepdims=True) + LN_EPS)
    xhat = xc * rstd
    vn = xhat * lg + lb
    mixed = []
    for gp in range(SGU_W // 128):
        vp = vn[:, gp * 128:(gp + 1) * 128].astype(BF16)
        mixed.append(jnp.where(head0, _dot(wc[2 * gp], vp), _dot(wc[2 * gp + 1], vp)))
    ms = jnp.concatenate(mixed, axis=1) + bfull
    return ug, xhat, rstd, vn, ms


def sgu_fwd(u, vs, lg, lb, w_sp, bfull):
    cpb = 4

    def body(u_ref, vs_ref, lg_ref, lb_ref, w_ref, b_ref, o_ref):
        wc, _ = _causal_weights(w_ref)
        head0 = lax.broadcasted_iota(jnp.int32, (CHUNK, 128), 1) < HEAD_DIM
        for ci in range(cpb):
            rows = pl.ds(ci * CHUNK, CHUNK)
            ug, _, _, _, ms = _sgu_chunk_fwd(u_ref[rows, :], vs_ref[rows, :], lg_ref[...], lb_ref[...], wc, b_ref[...], head0)
            o_ref[rows, :] = ug * ms

    tm = cpb * CHUNK
    return _call(
        "sgu_fwd", body, SEQ // tm,
        [_row_spec(tm, SGU_W), _row_spec(tm, SGU_W), _full_spec((1, SGU_W)), _full_spec((1, SGU_W)),
         _full_spec((N_GROUPS, CHUNK, CHUNK)), _full_spec((CHUNK, SGU_W))],
        [_row_spec(tm, SGU_W)], [jax.ShapeDtypeStruct((SEQ, SGU_W), F32)],
        (u, vs, lg, lb, w_sp, bfull))


def out_proj(attn, sgu, x, ga, gs, w_out, gpm, gpf):
    tm = 512

    def body(a_ref, s_ref, x_ref, ga_ref, gs_ref, w_ref, gpm_ref, gpf_ref, mix_ref, y_ref, x2_ref, h2_ref):
        a = a_ref[...]
        s = s_ref[...]
        an = (a * _rms(a) * ga_ref[...]).astype(BF16)
        sn = (s * _rms(s) * gs_ref[...]).astype(BF16)
        mix_ref[:, :ATTN_W] = an
        mix_ref[:, ATTN_W:] = sn
        y = _dot(an, w_ref[:ATTN_W, :]) + _dot(sn, w_ref[ATTN_W:, :])
        y_ref[...] = y
        x2 = x_ref[...] + y * _rms(y) * gpm_ref[...]
        x2_ref[...] = x2
        h2_ref[...] = (x2 * _rms(x2) * gpf_ref[...]).astype(BF16)

    wide = jax.ShapeDtypeStruct((SEQ, D_MODEL), F32)
    wide16 = jax.ShapeDtypeStruct((SEQ, D_MODEL), BF16)
    return _call(
        "out_proj", body, SEQ // tm,
        [_row_spec(tm, ATTN_W), _row_spec(tm, SGU_W), _row_spec(tm, D_MODEL), _full_spec((1, ATTN_W)),
         _full_spec((1, SGU_W)), _weight_spec((D_MODEL, D_MODEL)), _full_spec((1, D_MODEL)), _full_spec((1, D_MODEL))],
        [_row_spec(tm, D_MODEL)] * 4, [wide16, wide, wide, wide16],
        (attn, sgu, x, ga, gs, w_out, gpm, gpf))


def ffn_up(h2, w_gate_t, w_up_t):
    tm = 256

    def body(h_ref, wg_hbm, wu_hbm, g_ref, u_ref, a_ref, wg_ref, wu_ref, sems):
        def run(wait):
            h = h_ref[...]
            if wait:
                wait(0, 0)
            g = _dot_nt(h, wg_ref[...])
            g_ref[...] = g.astype(BF16)
            if wait:
                wait(1, 0)
            u = _dot_nt(h, wu_ref[...])
            u_ref[...] = u.astype(BF16)
            a_ref[...] = (g * jax.nn.sigmoid(g) * u).astype(BF16)

        _with_ffn_weights([wg_hbm, wu_hbm], [wg_ref, wu_ref], sems, FF_WHOLE, run)

    ff = jax.ShapeDtypeStruct((SEQ, D_FF), BF16)
    return _call(
        "ffn_up", body, SEQ // tm, [_row_spec(tm, D_MODEL), ANY, ANY],
        [_row_spec(tm, D_FF)] * 3, [ff, ff, jax.ShapeDtypeStruct((SEQ, D_FF), BF16)],
        (h2, w_gate_t, w_up_t), scratch_shapes=_ffn_weight_scratch(2, FF_WHOLE))


def ffn_down_loss(act, w_down, x2, gpo, target):
    tm = 512

    def body(a_ref, w_hbm, x2_ref, g_ref, t_ref, df_ref, dx3_ref, dg_ref, loss_ref, w_ref, sems):
        def run(wait):
            f = None
            for c, (o, n) in enumerate(FF_SPANS if wait else FF_WHOLE):
                if wait:
                    wait(0, c)
                part = _dot(a_ref[:, o:o + n], w_ref[o:o + n, :])
                f = part if f is None else f + part
            gain = g_ref[...]
            err = x2_ref[...] + f * _rms(f) * gain - t_ref[...]
            dx3 = err * np.float32(1.0 / D_MODEL)
            dx3_ref[...] = dx3
            df, dg = _rms_bwd(f, gain, dx3)
            df_ref[...] = df.astype(BF16)
            loss = jnp.sum(err * err, axis=(0, 1), keepdims=True)
            if wait:
                dg_ref[...] = dg
                loss_ref[...] = loss
            else:
                dg_ref[...] += dg
                loss_ref[...] += loss

        _with_ffn_weights([w_hbm], [w_ref], sems, FF_SPANS, run)

    return _call(
        "ffn_down_loss", body, SEQ // tm,
        [_row_spec(tm, D_FF), ANY, _row_spec(tm, D_MODEL), _full_spec((1, D_MODEL)), _row_spec(tm, D_MODEL)],
        [_row_spec(tm, D_MODEL), _row_spec(tm, D_MODEL), _full_spec((1, D_MODEL)), _full_spec((1, 1))],
        [jax.ShapeDtypeStruct((SEQ, D_MODEL), BF16), jax.ShapeDtypeStruct((SEQ, D_MODEL), F32),
         jax.ShapeDtypeStruct((1, D_MODEL), F32), jax.ShapeDtypeStruct((1, 1), F32)],
        (act, w_down, x2, gpo, target), scratch_shapes=_ffn_weight_scratch(1, FF_SPANS))


def ffn_bwd(df, w_down, gate, up, w_gate_t, w_up_t, x2, gpf, dx3, y, gpm, after=()):
    tm = 256

    def body(df_ref, wd_hbm, g_ref, u_ref, wg_hbm, wu_hbm, x2_ref, gpf_ref, dx3_ref, y_ref, gpm_ref,
             dg_ref, du_ref, dx2_ref, dy_ref, dgpf_ref, dgpm_ref, wd_ref, wg_ref, wu_ref, sems):
        def run(wait):
            if wait:
                wait(0, 0)
            dact = _dot_nt(df_ref[...], wd_ref[...])
            g = g_ref[...].astype(F32)
            s = jax.nn.sigmoid(g)
            dup = (dact * g * s).astype(BF16)
            dgate = (dact * u_ref[...].astype(F32) * (s * (1.0 + g * (1.0 - s)))).astype(BF16)
            du_ref[...] = dup
            dg_ref[...] = dgate
            if wait:
                wait(1, 0)
                wait(2, 0)
            dh2 = _dot(dgate, wg_ref[...]) + _dot(dup, wu_ref[...])
            dz, dgpf = _rms_bwd(x2_ref[...], gpf_ref[...], dh2)
            dx2 = dx3_ref[...] + dz
            dx2_ref[...] = dx2
            dy, dgpm = _rms_bwd(y_ref[...], gpm_ref[...], dx2)
            dy_ref[...] = dy.astype(BF16)
            if wait:
                dgpf_ref[...] = dgpf
                dgpm_ref[...] = dgpm
            else:
                dgpf_ref[...] += dgpf
                dgpm_ref[...] += dgpm

        _with_ffn_weights([wd_hbm, wg_hbm, wu_hbm], [wd_ref, wg_ref, wu_ref], sems, FF_WHOLE, run)

    vec = jax.ShapeDtypeStruct((1, D_MODEL), F32)
    ff16 = jax.ShapeDtypeStruct((SEQ, D_FF), BF16)
    return _call(
        "ffn_bwd", body, SEQ // tm,
        [_row_spec(tm, D_MODEL), ANY, _row_spec(tm, D_FF), _row_spec(tm, D_FF), ANY, ANY, _row_spec(tm, D_MODEL),
         _full_spec((1, D_MODEL)), _row_spec(tm, D_MODEL), _row_spec(tm, D_MODEL), _full_spec((1, D_MODEL))],
        [_row_spec(tm, D_FF), _row_spec(tm, D_FF), _row_spec(tm, D_MODEL), _row_spec(tm, D_MODEL),
         _full_spec((1, D_MODEL)), _full_spec((1, D_MODEL))],
        [ff16, ff16, jax.ShapeDtypeStruct((SEQ, D_MODEL), F32), jax.ShapeDtypeStruct((SEQ, D_MODEL), BF16), vec, vec],
        (df, w_down, gate, up, w_gate_t, w_up_t, x2, gpf, dx3, y, gpm), scratch_shapes=_ffn_weight_scratch(3, FF_WHOLE), after=after)


def weight_grads(name, lhs, b, after=()):
    m, n, k = lhs[0].shape[1], b.shape[1], len(lhs)
    tr = 256

    def body(*refs):
        for a_ref, o_ref in zip(refs[:k], refs[k + 1:]):
            o_ref[...] = _dot_tn(a_ref[...], refs[k][...]).astype(BF16)

    outs = _call(
        name, body, m // tr, [pl.BlockSpec((SEQ, tr), lambda i: (0, i))] * k + [_weight_spec((SEQ, n))],
        [_row_spec(tr, n)] * k, [jax.ShapeDtypeStruct((m, n), BF16)] * k, (*lhs, b), after=after)
    return [out.reshape(N_DEV, m // N_DEV, n) for out in outs]


def weight_grad(name, a, b, after=()):
    return weight_grads(name, [a], b, after)[0]


def weight_grad_of_parts(name, parts, b, after=()):
    p, n, k = parts[0].shape[1], b.shape[1], len(parts)
    tr = 512
    per = p // tr

    def body(*refs):
        tile = pl.program_id(0)
        for j in range(k):
            @pl.when(tile // per == j)
            def _(j=j):
                refs[k + 1][...] = _dot_tn(refs[j][...].astype(BF16), refs[k][...]).astype(BF16)

    def part_spec(j):
        return pl.BlockSpec((SEQ, tr), lambda i: (0, jnp.clip(i - per * j, 0, per - 1)))

    (out,) = _call(
        name, body, k * per, [part_spec(j) for j in range(k)] + [_weight_spec((SEQ, n))],
        [_row_spec(tr, n)], [jax.ShapeDtypeStruct((k * p, n), BF16)], (*parts, b), after=after)
    return out.reshape(N_DEV, k * p // N_DEV, n)


def mix_bwd(dy, w_out, attn, sgu, ga, gs, after=()):
    tm = 512
    n_steps = SEQ // tm

    def body(dy_ref, w_ref, a_ref, s_ref, ga_ref, gs_ref, ds_ref, dga_ref, dgs_ref, da_ref, scratch, sems):
        dy = dy_ref[...]
        da, dga = _rms_bwd(a_ref[...], ga_ref[...], _dot_nt(dy, w_ref[:ATTN_W, :]))
        ds, dgs = _rms_bwd(s_ref[...], gs_ref[...], _dot_nt(dy, w_ref[ATTN_W:, :]))
        ds_ref[...] = ds
        _to_residue_rows([da], [da_ref], scratch, sems, tm, n_steps)

        @pl.when(pl.program_id(0) == 0)
        def _():
            dga_ref[...] = jnp.zeros_like(dga_ref)
            dgs_ref[...] = jnp.zeros_like(dgs_ref)

        dga_ref[...] += dga
        dgs_ref[...] += dgs

    half = jax.ShapeDtypeStruct((SEQ, 512), F32)
    vec = jax.ShapeDtypeStruct((1, 512), F32)
    return _call(
        "mix_bwd", body, n_steps,
        [_row_spec(tm, D_MODEL), _weight_spec((D_MODEL, D_MODEL)), _row_spec(tm, 512), _row_spec(tm, 512),
         _full_spec((1, 512)), _full_spec((1, 512))],
        [_row_spec(tm, 512), _full_spec((1, 512)), _full_spec((1, 512)), ANY],
        [half, vec, vec, half], (dy, w_out, attn, sgu, ga, gs), scratch_shapes=_residue_scratch(1, tm, ATTN_W), after=after)


def sgu_bwd(u, vs, dsgu, lg, lb, w_sp, bfull, after=()):
    cpb = 4

    def body(u_ref, vs_ref, d_ref, lg_ref, lb_ref, w_ref, b_ref, du_ref, dvs_ref, dlg_ref, dlb_ref, dw_ref, db_ref):
        wc, causal = _causal_weights(w_ref)
        head0 = lax.broadcasted_iota(jnp.int32, (CHUNK, 128), 1) < HEAD_DIM
        lg = lg_ref[...]

        @pl.when(pl.program_id(0) == 0)
        def _():
            dlg_ref[...] = jnp.zeros_like(dlg_ref)
            dlb_ref[...] = jnp.zeros_like(dlb_ref)
            dw_ref[...] = jnp.zeros_like(dw_ref)
            db_ref[...] = jnp.zeros_like(db_ref)

        for ci in range(cpb):
            rows = pl.ds(ci * CHUNK, CHUNK)
            u = u_ref[rows, :]
            vs = vs_ref[rows, :]
            d = d_ref[rows, :]
            ug, xhat, rstd, vn, ms = _sgu_chunk_fwd(u, vs, lg, lb_ref[...], wc, b_ref[...], head0)
            du_ref[rows, :] = (d * ms * _gelu_grad(u)).astype(BF16)
            dms = d * ug
            db_ref[...] += dms
            dvn = []
            for gp in range(SGU_W // 128):
                dmp = dms[:, gp * 128:(gp + 1) * 128]
                dm0 = jnp.where(head0, dmp, 0.0).astype(BF16)
                dm1 = jnp.where(head0, 0.0, dmp).astype(BF16)
                vp = vn[:, gp * 128:(gp + 1) * 128].astype(BF16)
                dw_ref[2 * gp] += _dot_nt(dm0, vp)
                dw_ref[2 * gp + 1] += _dot_nt(dm1, vp)
                dvn.append(_dot_tn(wc[2 * gp], dm0) + _dot_tn(wc[2 * gp + 1], dm1))
            dvn = jnp.concatenate(dvn, axis=1)
            dlg_ref[...] += jnp.sum(dvn * xhat, axis=0, keepdims=True)
            dlb_ref[...] += jnp.sum(dvn, axis=0, keepdims=True)
            dxh = dvn * lg
            dvg = rstd * (dxh - jnp.mean(dxh, axis=-1, keepdims=True) - xhat * jnp.mean(dxh * xhat, axis=-1, keepdims=True))
            dvs_ref[rows, :] = (dvg * _gelu_grad(vs)).astype(BF16)

        @pl.when(pl.program_id(0) == pl.num_programs(0) - 1)
        def _():
            for g in range(N_GROUPS):
                dw_ref[g] = jnp.where(causal, dw_ref[g], 0.0)

    tm = cpb * CHUNK
    half16 = jax.ShapeDtypeStruct((SEQ, SGU_W), BF16)
    vec = jax.ShapeDtypeStruct((1, SGU_W), F32)
    return _call(
        "sgu_bwd", body, SEQ // tm,
        [_row_spec(tm, SGU_W)] * 3 + [_full_spec((1, SGU_W)), _full_spec((1, SGU_W)),
                                      _full_spec((N_GROUPS, CHUNK, CHUNK)), _full_spec((CHUNK, SGU_W))],
        [_row_spec(tm, SGU_W), _row_spec(tm, SGU_W), _full_spec((1, SGU_W)), _full_spec((1, SGU_W)),
         _full_spec((N_GROUPS, CHUNK, CHUNK)), _full_spec((CHUNK, SGU_W))],
        [half16, half16, vec, vec, jax.ShapeDtypeStruct((N_GROUPS, CHUNK, CHUNK), F32),
         jax.ShapeDtypeStruct((CHUNK, SGU_W), F32)],
        (u, vs, dsgu, lg, lb, w_sp, bfull), after=after)


def attn_bwd(q, k, v, o, lse, do, pos_col, rot):
    n_steps = ATTN_W // 128

    def body(q_ref, k_ref, v_ref, o_ref, lse_ref, do_ref, pos_ref, invf_ref, ma_ref, mb_ref,
             dq_ref, dk_ref, dv_ref, dqa_ref, dka_ref, dva_ref, dlt_ref, rot_ref, out_ref, sems):
        step = pl.program_id(0)
        slot = step % 2

        def back(at_step, s):
            lanes = pl.ds(pl.multiple_of(at_step * 128, 128), 128)
            return [pltpu.make_async_copy(out_ref.at[s, a, pl.ds(b * (SEQ // RES), SEQ // RES), :], nat.at[:, b, lanes],
                                          sems.at[s, a, b]) for a, nat in enumerate((dq_ref, dk_ref, dv_ref)) for b in range(RES)]

        dqa_ref[...] = jnp.zeros_like(dqa_ref)
        dka_ref[...] = jnp.zeros_like(dka_ref)
        dva_ref[...] = jnp.zeros_like(dva_ref)

        def delta(i, carry):
            rows = pl.ds(pl.multiple_of(i * 256, 256), 256)
            prod = do_ref[rows, :] * o_ref[rows, :]
            h0 = lax.broadcasted_iota(jnp.int32, (256, 128), 1) < HEAD_DIM
            d0 = jnp.sum(jnp.where(h0, prod, 0.0), axis=-1, keepdims=True)
            d1 = jnp.sum(jnp.where(h0, 0.0, prod), axis=-1, keepdims=True)
            dlt_ref[rows, :] = jnp.where(h0, d0, d1)
            return carry

        lax.fori_loop(0, SEQ // 256, delta, 0)

        def add_rows(ref, slices, val):
            at = 0
            for start, size in slices:
                ref[pl.ds(start, size), :] += val[at:at + size]
                at += size

        def group(p, masks, blocks):
            head0, mask1, mask2 = masks
            heads = (head0, jnp.logical_not(head0))
            keys = [rows if prev is None else prev + rows for rows, prev in blocks]
            mask = [mask1 if prev is None else mask2 for _, prev in blocks]
            kk = [_load_rows(k_ref, ks).astype(BF16) for ks in keys]
            vv = [_load_rows(v_ref, ks).astype(BF16) for ks in keys]
            qb = [_load_rows(q_ref, rows) for rows, _ in blocks]
            dob = [_load_rows(do_ref, rows) for rows, _ in blocks]
            lse_b = [_load_rows(lse_ref, rows) for rows, _ in blocks]
            dlt_b = [_load_rows(dlt_ref, rows) for rows, _ in blocks]
            chains = [(g, h) for g in range(len(blocks)) for h in range(2)]
            qm = [jnp.where(heads[h], qb[g], 0.0).astype(BF16) for g, h in chains]
            dom = [jnp.where(heads[h], dob[g], 0.0).astype(BF16) for g, h in chains]
            s = [_dot_nt(qm[c], kk[g]) for c, (g, h) in enumerate(chains)]
            dp = [_dot_nt(dom[c], vv[g]) for c, (g, h) in enumerate(chains)]
            pr = [jnp.where(mask[g], jnp.exp(s[c] - lse_b[g][:, h * HEAD_DIM:h * HEAD_DIM + 1]), 0.0)
                  for c, (g, h) in enumerate(chains)]
            ds = [(pr[c] * (dp[c] - dlt_b[g][:, h * HEAD_DIM:h * HEAD_DIM + 1])).astype(BF16)
                  for c, (g, h) in enumerate(chains)]
            dv = [_dot_tn(pr[c].astype(BF16), dom[c]) for c in range(len(chains))]
            dk = [_dot_tn(ds[c], qm[c]) for c in range(len(chains))]
            dq = [_dot(ds[c], kk[g]) for c, (g, h) in enumerate(chains)]
            for g, (rows, _) in enumerate(blocks):
                add_rows(dqa_ref, rows, jnp.where(head0, dq[2 * g], dq[2 * g + 1]))
                add_rows(dka_ref, keys[g], dk[2 * g] + dk[2 * g + 1])
                add_rows(dva_ref, keys[g], dv[2 * g] + dv[2 * g + 1])

        _for_each_group(group)

        @pl.when(pl.program_id(0) == 0)
        def _():
            def tables(i, carry):
                rows = pl.ds(pl.multiple_of(i * 256, 256), 256)
                c, sa, sb = _rot_tables(pos_ref[rows, :], invf_ref[...], ma_ref[...], mb_ref[...])
                rot_ref[0, rows, :] = c
                rot_ref[1, rows, :] = sa
                rot_ref[2, rows, :] = sb
                return carry

            lax.fori_loop(0, SEQ // 256, tables, 0)

        @pl.when(step >= 2)
        def _():
            for cp in back(step - 2, slot):
                cp.wait()

        def finish(i, carry):
            rows = pl.ds(pl.multiple_of(i * 256, 256), 256)
            c, sa, sb = rot_ref[0, rows, :], rot_ref[1, rows, :], rot_ref[2, rows, :]
            out_ref[slot, 0, rows, :] = _rot_t(dqa_ref[rows, :] * Q_SCALE, c, sa, sb)
            out_ref[slot, 1, rows, :] = _rot_t(dka_ref[rows, :], c, sa, sb)
            out_ref[slot, 2, rows, :] = dva_ref[rows, :]
            return carry

        lax.fori_loop(0, SEQ // 256, finish, 0)
        for cp in back(step, slot):
            cp.start()

        @pl.when(step == n_steps - 1)
        def _():
            for cp in back(step - 1, 1 - slot) + back(step, slot):
                cp.wait()

    slab = pl.BlockSpec((SEQ, 128), lambda i: (0, i))
    out = jax.ShapeDtypeStruct((SEQ // RES, RES, ATTN_W), F32)
    acc = pltpu.VMEM((SEQ, 128), F32)
    outs = _call(
        "attn_bwd", body, n_steps,
        [slab] * 6 + [_full_spec((SEQ, 1)), _full_spec((1, 128)), _full_spec((1, 128)), _full_spec((1, 128))],
        [ANY] * 3, [out, out, out], (q, k, v, o, lse, do, pos_col, *rot),
        scratch_shapes=[acc, acc, acc, acc, pltpu.VMEM((3, SEQ, 128), F32), pltpu.VMEM((2, 3, SEQ, 128), F32),
                        pltpu.SemaphoreType.DMA((2, 3, RES))])
    return [t.reshape(SEQ, ATTN_W) for t in outs]


def in_bwd(dproj_parts, w_in_t, x, g1, dx2, after=()):
    tm = 512
    k = len(dproj_parts)

    def body(*refs):
        w_ref, x_ref, g_ref, dx2_ref, dx_ref, dg_ref = refs[k:]
        dh1 = _dot(refs[0][...].astype(BF16), w_ref[0:512, :])
        for j in range(1, k):
            dh1 = dh1 + _dot(refs[j][...].astype(BF16), w_ref[512 * j:512 * (j + 1), :])
        dz, dg = _rms_bwd(x_ref[...], g_ref[...], dh1)
        dx_ref[...] = dx2_ref[...] + dz

        @pl.when(pl.program_id(0) == 0)
        def _():
            dg_ref[...] = jnp.zeros_like(dg_ref)

        dg_ref[...] += dg

    return _call(
        "in_bwd", body, SEQ // tm,
        [_row_spec(tm, 512)] * k + [_weight_spec((IN_W, D_MODEL)), _row_spec(tm, D_MODEL), _full_spec((1, D_MODEL)),
                                    _row_spec(tm, D_MODEL)],
        [_row_spec(tm, D_MODEL), _full_spec((1, D_MODEL))],
        [jax.ShapeDtypeStruct((SEQ, D_MODEL), F32), jax.ShapeDtypeStruct((1, D_MODEL), F32)],
        (*dproj_parts, w_in_t, x, g1, dx2), after=after)


def _coords():
    return lax.axis_index("x"), lax.axis_index("y"), lax.axis_index("c")


class Exchange:
    def __init__(self, srcs, bufs, new_shapes, n_sems, make):
        self.srcs, self.bufs, self.new_shapes, self.n_sems, self.make = list(srcs), list(bufs), list(new_shapes), n_sems, make


def _call(name, body, n_steps, in_specs, out_specs, out_shape, args, scratch_shapes=(), after=()):
    n_in = len(args)

    def wrapped(*refs):
        body(*refs[:n_in], *refs[n_in + len(after):])

    return list(pl.pallas_call(
        wrapped, name=name, grid=(n_steps,), in_specs=list(in_specs) + [ANY] * len(after), out_specs=list(out_specs),
        out_shape=list(out_shape), scratch_shapes=list(scratch_shapes), compiler_params=_params(),
    )(*args, *after))


GATHER_SEMS = 8


def gather(bufs):
    n = len(bufs)

    def make(src_refs, buf_refs, new_refs, send_sems, recv_sems):
        x, y, c = _coords()
        me, sibling = (x, y, c), (x, y, 1 - c)
        over_x, over_y, across = (1 - x, y), (x, 1 - y), (1 - x, 1 - y)

        def copy(a, k, block, to, half=None):
            r = buf_refs[a].shape[0] // N_DEV
            lo, size = (0, r) if half is None else (half * (r // 2), r // 2)
            rows = buf_refs[a].at[pl.ds((4 * block[0] + 2 * block[1] + block[2]) * r + lo, size), :]
            return pltpu.make_async_remote_copy(
                src_ref=rows, dst_ref=rows, send_sem=send_sems.at[GATHER_SEMS * a + k],
                recv_sem=recv_sems.at[GATHER_SEMS * a + k], device_id=to, device_id_type=MESH)

        every = range(n)
        out = ([copy(a, 0, me, sibling) for a in every] + [copy(a, 1, me, (*over_x, c)) for a in every]
               + [copy(a, 2, me, (*over_y, c)) for a in every])
        near_in = [copy(a, 1, (*over_x, c), me) for a in every] + [copy(a, 2, (*over_y, c), me) for a in every]
        relay = ([copy(a, 3, (*over_x, c), (*over_y, c), half=0) for a in every]
                 + [copy(a, 4, (*over_y, c), (*over_x, c), half=1) for a in every])
        near_on = [copy(a, 5, (*over_x, c), sibling) for a in every] + [copy(a, 6, (*over_y, c), sibling) for a in every]
        relay_in = ([copy(a, 3, (*across, c), me, half=0) for a in every]
                    + [copy(a, 4, (*across, c), me, half=1) for a in every])
        far_on = [copy(a, 7, (*across, c), sibling) for a in every]
        from_core = ([copy(a, 0, sibling, me) for a in every] + [copy(a, 5, (*over_x, 1 - c), me) for a in every]
                     + [copy(a, 6, (*over_y, 1 - c), me) for a in every] + [copy(a, 7, (*across, 1 - c), me) for a in every])
        stages = [([], out), (near_in, relay + near_on), (relay_in, far_on)]
        return stages, out + relay + near_on + far_on, from_core

    return Exchange([], bufs, [], GATHER_SEMS * n, make)


TO_GATHER = (1, lambda x, y, c: [(x, y, 1 - c), (1 - x, y, c), (x, 1 - y, c)])
TO_SIBLING = (2, lambda x, y, c: [(x, y, 1 - c)])
TO_CHIPS = (3, lambda x, y, c: [(1 - x, y, c), (x, 1 - y, c), (1 - x, 1 - y, c)])
TO_ALL = (4, lambda x, y, c: [(x ^ (m >> 2), y ^ ((m >> 1) & 1), c ^ (m & 1)) for m in range(1, N_DEV)])


def by_sequencer(name, exchanges, who):
    collective_id, peers_of = who
    hbm = pltpu.MemorySpace.HBM
    refs = [([jax.new_ref(a, memory_space=hbm) for a in ex.srcs], [jax.new_ref(a, memory_space=hbm) for a in ex.bufs],
             [jax.empty_ref(s, memory_space=hbm) for s in ex.new_shapes]) for ex in exchanges]
    sems = []
    for ex in exchanges:
        sems += [pltpu.SemaphoreType.DMA((ex.n_sems,)), pltpu.SemaphoreType.DMA((ex.n_sems,))]

    @pl.kernel(mesh=plsc.ScalarSubcoreMesh(axis_name="sequencer", num_cores=1), name=name, scratch_types=tuple(sems),
               compiler_params=pltpu.CompilerParams(collective_id=collective_id))
    def launch(*sem_refs):
        peers = peers_of(*_coords())
        barrier = pltpu.get_barrier_semaphore()
        for peer in peers:
            pl.semaphore_signal(barrier, inc=1, device_id=peer, device_id_type=MESH)
        pl.semaphore_wait(barrier, len(peers))

        made = [ex.make(*refs[k], sem_refs[2 * k], sem_refs[2 * k + 1]) for k, ex in enumerate(exchanges)]
        for stage in range(max(len(stages) for stages, _, _ in made)):
            for stages, _, _ in made:
                if stage < len(stages):
                    arrivals, starts = stages[stage]
                    for cp in arrivals:
                        cp.wait_recv()
                    for cp in starts:
                        cp.start()
        for _, sends, arrivals in made:
            for cp in arrivals:
                cp.wait_recv()
            for cp in sends:
                cp.wait_send()

    launch()
    return [([ref[...] for ref in bufs], [ref[...] for ref in news]) for _, bufs, news in refs]


def place_shards(name, shards, dev):
    n = len(shards)

    def body(dev_ref, *refs):
        for a in range(n):
            refs[n + a][...] = refs[a][...].astype(BF16)

    spec = pltpu.PrefetchScalarGridSpec(
        num_scalar_prefetch=1, grid=(1,),
        in_specs=[pl.BlockSpec(s.shape, lambda i, dev_ref: (0, 0)) for s in shards],
        out_specs=[pl.BlockSpec(s.shape, lambda i, dev_ref: (dev_ref[0], 0)) for s in shards])
    return pl.pallas_call(
        body, name=name, grid_spec=spec,
        out_shape=[jax.ShapeDtypeStruct((N_DEV * s.shape[0], s.shape[1]), BF16) for s in shards],
        compiler_params=_params(),
    )(dev, *shards)


def _swap(copies_of):
    def make(src_refs, buf_refs, new_refs, send_sems, recv_sems):
        copies = copies_of(src_refs, new_refs, send_sems, recv_sems)
        return [([], copies)], copies, copies

    return make


def to_sibling(grads):
    def copies_of(src_refs, new_refs, send_sems, recv_sems):
        x, y, c = _coords()
        return [pltpu.make_async_remote_copy(
            src_ref=src_refs[a].at[2 * xy + 1 - c], dst_ref=new_refs[a].at[xy], send_sem=send_sems.at[4 * a + xy],
            recv_sem=recv_sems.at[4 * a + xy], device_id=(x, y, 1 - c), device_id_type=MESH)
            for a in range(len(src_refs)) for xy in range(4)]

    return Exchange(grads, [], [jax.ShapeDtypeStruct((4,) + g.shape[1:], g.dtype) for g in grads], 4 * len(grads),
                    _swap(copies_of))


def to_chips(parts):
    def copies_of(src_refs, new_refs, send_sems, recv_sems):
        x, y, c = _coords()
        chips = [(1 - x, y), (x, 1 - y), (1 - x, 1 - y)]
        return [pltpu.make_async_remote_copy(
            src_ref=src_refs[a].at[2 * px + py], dst_ref=new_refs[a].at[2 * x + y], send_sem=send_sems.at[3 * a + j],
            recv_sem=recv_sems.at[3 * a + j], device_id=(px, py, c), device_id_type=MESH)
            for a in range(len(src_refs)) for j, (px, py) in enumerate(chips)]

    return Exchange(parts, [], [jax.ShapeDtypeStruct(p.shape, p.dtype) for p in parts], 3 * len(parts), _swap(copies_of))


def to_owners(grad):
    def copies_of(src_refs, new_refs, send_sems, recv_sems):
        x, y, c = _coords()
        copies = []
        for m in range(1, N_DEV):
            px, py, pc = x ^ (m >> 2), y ^ ((m >> 1) & 1), c ^ (m & 1)
            copies.append(pltpu.make_async_remote_copy(
                src_ref=src_refs[0].at[4 * px + 2 * py + pc], dst_ref=new_refs[0].at[4 * x + 2 * y + c],
                send_sem=send_sems.at[m - 1], recv_sem=recv_sems.at[m - 1], device_id=(px, py, pc), device_id_type=MESH))
        return copies

    return Exchange([grad], [], [jax.ShapeDtypeStruct(grad.shape, grad.dtype)], N_DEV - 1, _swap(copies_of))


def to_everyone(vec):
    def copies_of(src_refs, new_refs, send_sems, recv_sems):
        x, y, c = _coords()
        copies = []
        for m in range(1, N_DEV):
            px, py, pc = x ^ (m >> 2), y ^ ((m >> 1) & 1), c ^ (m & 1)
            copies.append(pltpu.make_async_remote_copy(
                src_ref=src_refs[0], dst_ref=new_refs[0].at[4 * x + 2 * y + c],
                send_sem=send_sems.at[m - 1], recv_sem=recv_sems.at[m - 1], device_id=(px, py, pc), device_id_type=MESH))
        return copies

    return Exchange([vec], [], [jax.ShapeDtypeStruct((N_DEV,) + vec.shape, vec.dtype)], N_DEV - 1, _swap(copies_of))


def sum_cores(name, grads, others, core, after=()):
    k = len(grads)

    def body(core_ref, *refs):
        for j in range(k):
            out_ref = refs[2 * k + len(after) + j]
            out_ref[...] = (refs[j][:, 0].astype(F32) + refs[k + j][...].astype(F32)).astype(out_ref.dtype)

    mine = [pl.BlockSpec((2, 1) + o.shape[1:], lambda i, core_ref: (i, core_ref[0], 0, 0)) for o in others]
    theirs = [pl.BlockSpec((2,) + o.shape[1:], lambda i, core_ref: (i, 0, 0)) for o in others]
    return pl.pallas_call(
        body, name=name,
        grid_spec=pltpu.PrefetchScalarGridSpec(
            num_scalar_prefetch=1, grid=(2,), in_specs=mine + theirs + [ANY] * len(after), out_specs=theirs),
        out_shape=[jax.ShapeDtypeStruct(o.shape, o.dtype) for o in others],
        compiler_params=_params(),
    )(core, *[g.reshape((4, 2) + g.shape[1:]) for g in grads], *others, *after)


def sum_owned(name, grad, others, dev_ids, after=()):
    _, r, w = grad.shape

    def body(ids_ref, *refs):
        acc = refs[0][0]
        for k in range(1, N_DEV):
            acc = acc + refs[k][0]
        refs[-1][...] = acc

    def pick(k):
        return pl.BlockSpec((1, r, w), lambda i, ids_ref: (ids_ref[k], 0, 0))

    return pl.pallas_call(
        body, name=name,
        grid_spec=pltpu.PrefetchScalarGridSpec(
            num_scalar_prefetch=1, grid=(1,), in_specs=[pick(k) for k in range(N_DEV)] + [ANY] * len(after),
            out_specs=pl.BlockSpec((r, w), lambda i, ids_ref: (ids_ref[0], 0))),
        out_shape=jax.ShapeDtypeStruct((N_DEV * r, w), F32),
        compiler_params=_params(),
    )(dev_ids, grad, *([others] * (N_DEV - 1)), *after)


def _adamw_update(w, g, m, v):
    nm = ADAM_B1 * m + np.float32(1.0 - ADAM_B1) * g
    nv = ADAM_B2 * v + np.float32(1.0 - ADAM_B2) * (g * g)
    m_hat = nm / np.float32(1.0 - ADAM_B1 ** ADAM_STEP)
    v_hat = nv / np.float32(1.0 - ADAM_B2 ** ADAM_STEP)
    return -ADAM_LR * (m_hat / (jnp.sqrt(v_hat) + ADAM_EPS) + ADAM_WD * w), nm, nv


def adamw_of_sums(name, parts, others, chip_ids, ws, ms, vs, after):
    n = len(parts)
    halves = 2

    def body(ids_ref, *refs):
        outs = refs[7 * n + len(after):]
        for j in range(n):
            p_ref, a_ref, b_ref, c_ref, w_ref, m_ref, v_ref = refs[7 * j:7 * j + 7]
            g = ((p_ref[0].astype(F32) + a_ref[0].astype(F32)) + b_ref[0].astype(F32)) + c_ref[0].astype(F32)
            outs[4 * j][...] = g
            outs[4 * j + 1][...], outs[4 * j + 2][...], outs[4 * j + 3][...] = _adamw_update(w_ref[...], g, m_ref[...], v_ref[...])

    in_specs, out_specs, out_shape, operands = [], [], [], []
    for part, other, w, m, v in zip(parts, others, ws, ms, vs):
        _, r, wd = part.shape
        rows = r // halves
        whole = pl.BlockSpec((rows, wd), lambda i, ids_ref: (i, 0))
        in_specs += [pl.BlockSpec((1, rows, wd), lambda i, ids_ref, k=k: (ids_ref[k], i, 0)) for k in range(4)] + [whole] * 3
        out_specs += [whole] * 4
        out_shape += [jax.ShapeDtypeStruct((r, wd), F32)] * 4
        operands += [part, other, other, other, w, m, v]
    outs = pl.pallas_call(
        body, name=name,
        grid_spec=pltpu.PrefetchScalarGridSpec(
            num_scalar_prefetch=1, grid=(halves,), in_specs=in_specs + [ANY] * len(after), out_specs=out_specs),
        out_shape=out_shape,
        compiler_params=_params(),
    )(chip_ids, *operands, *after)
    return [tuple(outs[4 * j:4 * j + 4]) for j in range(n)]


def pack_small(parts):
    names = [name for name, _ in SMALL if name in parts]
    operands = [parts[name] for name in names]
    first_row, at = {}, 0
    for name, size in SMALL:
        first_row[name] = at // 128
        at += size
    sizes = dict(SMALL)

    def body(*refs):
        out_ref = refs[-1]
        out_ref[...] = jnp.zeros_like(out_ref)
        for name, ref in zip(names, refs):
            row = first_row[name]
            if name == "loss_sum":
                lane0 = lax.broadcasted_iota(jnp.int32, (1, 128), 1) == 0
                out_ref[row:row + 1, :] = jnp.where(lane0, ref[...], 0.0)
            else:
                rows = sizes[name] // 128
                out_ref[row:row + rows, :] = ref[...].reshape(rows, 128)

    vmem = pl.BlockSpec(memory_space=pltpu.VMEM)
    return pl.pallas_call(
        body, name="pack_small", in_specs=[vmem] * len(names), out_specs=vmem,
        out_shape=jax.ShapeDtypeStruct((SMALL_ROWS, 128), F32), compiler_params=_params(()),
    )(*operands)


LATE = "pre_mix_norm"


def adamw_small(packed_g, late_parts, late_own, dev, ws, ms, vs, after=()):
    names = [name for name, _ in SMALL if name != "loss_sum"]
    k = len(names)
    shapes = [ws[name].shape[1:] if ws[name].ndim > 2 else ws[name].shape for name in names]
    first_row, at = [], 0
    for name, size in SMALL:
        first_row.append(at // 128)
        at += size

    def body(g_ref, late_ref, own_ref, dev_ref, *refs):
        w_refs, m_refs, v_refs, outs = refs[:k], refs[k:2 * k], refs[2 * k:3 * k], refs[3 * k + len(after):]
        for i, (name, size) in enumerate(SMALL[:k]):
            if name == LATE:
                parts = [jnp.where(dev_ref[0] == j, own_ref[...], late_ref[j]) for j in range(N_DEV)]
                g = parts[0]
                for j in range(1, N_DEV):
                    g = g + parts[j]
            else:
                g = g_ref[first_row[i]:first_row[i] + size // 128, :].reshape(shapes[i])
            outs[i][...] = g
            outs[k + i][...], outs[2 * k + i][...], outs[3 * k + i][...] = _adamw_update(
                w_refs[i][...], g, m_refs[i][...], v_refs[i][...])
        outs[4 * k][...] = g_ref[first_row[k]:first_row[k] + 1, 0:1]

    vmem = pl.BlockSpec(memory_space=pltpu.VMEM)
    operands = [t[name].reshape(shape) for t in (ws, ms, vs) for name, shape in zip(names, shapes)]
    outs = pl.pallas_call(
        body, name="adamw_small",
        in_specs=[vmem] * 3 + [pl.BlockSpec(memory_space=pltpu.SMEM)] + [vmem] * (3 * k) + [ANY] * len(after),
        out_specs=[vmem] * (4 * k + 1),
        out_shape=[jax.ShapeDtypeStruct(shape, F32) for _ in range(4) for shape in shapes] + [jax.ShapeDtypeStruct((1, 1), F32)],
        compiler_params=_params(()),
    )(packed_g, late_parts, late_own, dev, *operands, *after)
    tables = [{name: outs[j * k + i].reshape(ws[name].shape) for i, name in enumerate(names)} for j in range(4)]
    return (*tables, outs[4 * k])


def kernel(x, positions, pre_mix_norm, w_in, sgu_ln_gain, sgu_ln_bias, sgu_w_spatial, sgu_b_spatial, attn_out_norm, sgu_out_norm, w_out, post_mix_norm, pre_ffn_norm, w_gate, w_up, w_down, post_ffn_norm, loss_target, m_pre_mix_norm, m_w_in, m_sgu_ln_gain, m_sgu_ln_bias, m_sgu_w_spatial, m_sgu_b_spatial, m_attn_out_norm, m_sgu_out_norm, m_w_out, m_post_mix_norm, m_pre_ffn_norm, m_w_gate, m_w_up, m_w_down, m_post_ffn_norm, v_pre_mix_norm, v_w_in, v_sgu_ln_gain, v_sgu_ln_bias, v_sgu_w_spatial, v_sgu_b_spatial, v_attn_out_norm, v_sgu_out_norm, v_w_out, v_post_mix_norm, v_pre_ffn_norm, v_w_gate, v_w_up, v_w_down, v_post_ffn_norm):
    small_w = dict(pre_mix_norm=pre_mix_norm, sgu_ln_gain=sgu_ln_gain, sgu_ln_bias=sgu_ln_bias, sgu_w_spatial=sgu_w_spatial,
                   sgu_b_spatial=sgu_b_spatial, attn_out_norm=attn_out_norm, sgu_out_norm=sgu_out_norm,
                   post_mix_norm=post_mix_norm, pre_ffn_norm=pre_ffn_norm, post_ffn_norm=post_ffn_norm)
    small_m = dict(pre_mix_norm=m_pre_mix_norm, sgu_ln_gain=m_sgu_ln_gain, sgu_ln_bias=m_sgu_ln_bias, sgu_w_spatial=m_sgu_w_spatial,
                   sgu_b_spatial=m_sgu_b_spatial, attn_out_norm=m_attn_out_norm, sgu_out_norm=m_sgu_out_norm,
                   post_mix_norm=m_post_mix_norm, pre_ffn_norm=m_pre_ffn_norm, post_ffn_norm=m_post_ffn_norm)
    small_v = dict(pre_mix_norm=v_pre_mix_norm, sgu_ln_gain=v_sgu_ln_gain, sgu_ln_bias=v_sgu_ln_bias, sgu_w_spatial=v_sgu_w_spatial,
                   sgu_b_spatial=v_sgu_b_spatial, attn_out_norm=v_attn_out_norm, sgu_out_norm=v_sgu_out_norm,
                   post_mix_norm=v_post_mix_norm, pre_ffn_norm=v_pre_ffn_norm, post_ffn_norm=v_post_ffn_norm)

    x2d = x[0]
    target = loss_target[0]
    pos_col = positions.reshape(SEQ, 1)
    rot = _rot_consts()
    w_sp = sgu_w_spatial[0]
    bfull = jnp.repeat(sgu_b_spatial[0].T, HEAD_DIM, axis=1)

    x_i, y_i, c_i = (lax.axis_index(a).astype(jnp.int32) for a in MESH_AXES)
    dev = 4 * x_i + 2 * y_i + c_i
    core = c_i.reshape(1)
    chip = 2 * x_i + y_i
    chip_ids = jnp.stack([chip, chip ^ 1, chip ^ 2, chip ^ 3])
    dev_ids = jnp.stack([dev ^ m for m in range(N_DEV)])

    def gathered(name, bufs):
        return by_sequencer(name, [gather(bufs)], TO_GATHER)[0][0]

    def from_sibling(name, grads):
        return by_sequencer(name, [to_sibling(grads)], TO_SIBLING)[0][1]

    def from_chips(name, parts):
        return by_sequencer(name, [to_chips(parts)], TO_CHIPS)[0][1]

    (w_in_t,) = place_shards("place_w_in", [w_in[0].T], dev.reshape(1))
    (w_in_t,) = gathered("gather_w_in", [w_in_t])
    w_gate_t, w_up_t, w_out_f, w_down_f = place_shards(
        "place_weights", [w_gate[0].T, w_up[0].T, w_out[0], w_down[0]], dev.reshape(1))
    (w_out_f,) = gathered("gather_w_out", [w_out_f])
    w_gate_t, w_up_t = gathered("gather_w_gate_up", [w_gate_t, w_up_t])
    (w_down_f,) = gathered("gather_w_down", [w_down_f])

    h1, u, vs, q, k, v = in_proj(x2d, pos_col, pre_mix_norm, w_in_t, rot)
    attn_r, lse, attn = attn_fwd(q, k, v)
    (sgu,) = sgu_fwd(u, vs, sgu_ln_gain, sgu_ln_bias, w_sp, bfull)
    mix, y, x2, h2 = out_proj(attn, sgu, x2d, attn_out_norm, sgu_out_norm, w_out_f, post_mix_norm, pre_ffn_norm)
    gate, up, act = ffn_up(h2, w_gate_t, w_up_t)
    df, dx3, d_post_ffn, sq_err = ffn_down_loss(act, w_down_f, x2, post_ffn_norm, target)

    g_w_down = weight_grad("grad_w_down", act, df)
    (s_down,) = from_sibling("w_down_to_sibling", [g_w_down])
    dgate, dup, dx2, dy, d_pre_ffn, d_post_mix = ffn_bwd(
        df, w_down_f, gate, up, w_gate_t, w_up_t, x2, pre_ffn_norm, dx3, y, post_mix_norm, after=[g_w_down])
    (p_down,) = sum_cores("sum_cores_down", [g_w_down], [s_down], core, after=[dy])
    (c_down,) = from_chips("w_down_to_chips", [p_down])
    g_w_gate, g_w_up = weight_grads("grad_w_gate_up", [dgate, dup], h2, after=[p_down])
    s_gate, s_up = from_sibling("w_gate_up_to_sibling", [g_w_gate, g_w_up])
    g_w_out = weight_grad("grad_w_out", mix, dy, after=[g_w_up])
    (s_out,) = from_sibling("w_out_to_sibling", [g_w_out])
    dsgu, d_attn_out, d_sgu_out, dattn_r = mix_bwd(dy, w_out_f, attn, sgu, attn_out_norm, sgu_out_norm, after=[g_w_out, c_down])
    p_gate, p_up = sum_cores("sum_cores_gate_up", [g_w_gate, g_w_up], [s_gate, s_up], core, after=[dsgu])
    c_gate, c_up = from_chips("w_gate_up_to_chips", [p_gate, p_up])
    du, dvs, d_ln_gain, d_ln_bias, d_w_sp, d_bfull = sgu_bwd(
        u, vs, dsgu, sgu_ln_gain, sgu_ln_bias, w_sp, bfull, after=[p_gate, p_up])

    d_b_sp = d_bfull.reshape(CHUNK, N_GROUPS, HEAD_DIM).sum(axis=-1).T
    small_g = pack_small(dict(sgu_ln_gain=d_ln_gain, sgu_ln_bias=d_ln_bias, sgu_w_spatial=d_w_sp, sgu_b_spatial=d_b_sp,
                              attn_out_norm=d_attn_out, sgu_out_norm=d_sgu_out, post_mix_norm=d_post_mix,
                              pre_ffn_norm=d_pre_ffn, post_ffn_norm=d_post_ffn, loss_sum=sq_err))
    small_g = small_g.reshape(N_DEV, SMALL_ROWS // N_DEV, 128)
    ((_, (o_small,)),) = by_sequencer("small_to_owners", [to_owners(small_g)], TO_ALL)

    dq, dk, dv = attn_bwd(q, k, v, attn_r, lse, dattn_r, _to_residue_order(pos_col), rot)
    summed_small = sum_owned("sum_small", small_g, o_small, dev_ids, after=[dq, c_gate, c_up])
    (all_small,) = gathered("gather_small_grads", [summed_small])
    (p_out,) = sum_cores("sum_cores_out", [g_w_out], [s_out], core, after=[dq])
    (c_out,) = from_chips("w_out_to_chips", [p_out])
    dproj = [dq, dk, dv, du, dvs]
    g_w_in = weight_grad_of_parts("grad_w_in", dproj, h1, after=[c_gate, c_up])
    (s_in,) = from_sibling("w_in_to_sibling", [g_w_in])

    def same(t):
        return t

    def turned(t):
        return t.T

    big = {}

    def adamw(call, weights, after):
        results = adamw_of_sums(call, [p for _, _, p, _, _, _, _ in weights], [c for _, _, _, c, _, _, _ in weights], chip_ids,
                                [turn(w[0]) for _, w, _, _, _, _, turn in weights], [turn(m[0]) for _, _, _, _, m, _, turn in weights],
                                [turn(vv[0]) for _, _, _, _, _, vv, turn in weights], after)
        for (name, _, _, _, _, _, turn), outs in zip(weights, results):
            big[name] = tuple(turn(t)[None] for t in outs)
        return results[-1][0]

    done_ffn = adamw("adamw_ffn", (("w_down", w_down, p_down, c_down, m_w_down, v_w_down, same),
                                   ("w_gate", w_gate, p_gate, c_gate, m_w_gate, v_w_gate, turned),
                                   ("w_up", w_up, p_up, c_up, m_w_up, v_w_up, turned)), [g_w_in, all_small])
    (p_in,) = sum_cores("sum_cores_in", [g_w_in], [s_in], core, after=[done_ffn, c_out])
    (c_in,) = from_chips("w_in_to_chips", [p_in])
    grad_x, d_pre_mix = in_bwd(dproj, w_in_t, x2d, pre_mix_norm, dx2, after=[p_in])
    ((_, (late_parts,)),) = by_sequencer("pre_mix_to_everyone", [to_everyone(d_pre_mix)], TO_ALL)
    done_out = adamw("adamw_w_out", (("w_out", w_out, p_out, c_out, m_w_out, v_w_out, same),), [d_pre_mix])
    done_in = adamw("adamw_w_in", (("w_in", w_in, p_in, c_in, m_w_in, v_w_in, turned),), [done_out])
    sg, sd, snm, snv, loss_sum = adamw_small(
        all_small, late_parts, d_pre_mix, dev.reshape(1), small_w, small_m, small_v, after=[done_in])
    loss = loss_sum[0, 0] * np.float32(0.5 / D_MODEL)

    names = ["pre_mix_norm", "w_in", "sgu_ln_gain", "sgu_ln_bias", "sgu_w_spatial", "sgu_b_spatial", "attn_out_norm",
             "sgu_out_norm", "w_out", "post_mix_norm", "pre_ffn_norm", "w_gate", "w_up", "w_down", "post_ffn_norm"]
    outs = [loss, grad_x[None]]
    for i, table in enumerate((sg, sd, snm, snv)):
        for name in names:
            outs.append(big[name][i] if name in big else table[name])
    return tuple(outs)
```

```python
import numpy as np
import jax
import jax.numpy as jnp
from jax import lax
from jax.experimental import pallas as pl
from jax.experimental.pallas import tpu as pltpu
from jax.experimental.pallas import tpu_sc as plsc

F32 = jnp.float32
BF16 = jnp.bfloat16

SEQ = 2048
D_MODEL = 1024
ATTN_W = 512
SGU_W = 512
HEAD_DIM = 64
N_GROUPS = 8
CHUNK = 128
D_FF = 2816
IN_W = 3 * ATTN_W + 2 * SGU_W
DILATIONS = (1, 4, 16)
ROPE_THETA = 500000.0
ROT_DIM = 16
ROT_HALF = 8
RMS_EPS = 1e-6
LN_EPS = 1e-5
Q_SCALE = 0.125
NEG = -1e30

N_DEV = 8
MESH_AXES = ("x", "y", "c")
MESH = pl.DeviceIdType.MESH

ADAM_LR = 0.001
ADAM_B1 = 0.9
ADAM_B2 = 0.999
ADAM_EPS = 1e-08
ADAM_WD = 0.01
ADAM_STEP = 10

VMEM_LIMIT = 60 * 1024 * 1024
ANY = pl.BlockSpec(memory_space=pl.ANY)

SMALL = (("pre_mix_norm", 1024), ("sgu_ln_gain", 512), ("sgu_ln_bias", 512), ("sgu_w_spatial", 8 * 128 * 128),
         ("sgu_b_spatial", 1024), ("attn_out_norm", 512), ("sgu_out_norm", 512), ("post_mix_norm", 1024),
         ("pre_ffn_norm", 1024), ("post_ffn_norm", 1024), ("loss_sum", 1))
SMALL_ROWS = 1152


def _params(sem=("arbitrary",)):
    return pltpu.CompilerParams(dimension_semantics=sem, vmem_limit_bytes=VMEM_LIMIT)


def _dot(a, b):
    return jnp.dot(a, b, preferred_element_type=F32)


def _dot_nt(a, b):
    return lax.dot_general(a, b, (((1,), (1,)), ((), ())), preferred_element_type=F32)


def _dot_tn(a, b):
    return lax.dot_general(a, b, (((0,), (0,)), ((), ())), preferred_element_type=F32)


def _rms(z):
    return lax.rsqrt(jnp.mean(z * z, axis=-1, keepdims=True) + RMS_EPS)


def _rms_bwd(z, gain, d):
    r = _rms(z)
    n = z * r
    dn = d * gain
    dz = r * (dn - n * jnp.mean(dn * n, axis=-1, keepdims=True))
    return dz, jnp.sum(d * n, axis=0, keepdims=True)


def _gelu(z):
    return 0.5 * z * (1.0 + lax.erf(z * np.float32(1.0 / np.sqrt(2.0))))


def _gelu_grad(z):
    cdf = 0.5 * (1.0 + lax.erf(z * np.float32(1.0 / np.sqrt(2.0))))
    return cdf + z * jnp.exp(-0.5 * z * z) * np.float32(1.0 / np.sqrt(2.0 * np.pi))


def _rot_tables(pos_col, invf, ma, mb):
    ang = pos_col.astype(F32) * invf
    s = jnp.sin(ang)
    return jnp.cos(ang), s * ma, s * mb


def _rot(t, c, sa, sb):
    return t * c + pltpu.roll(t, 120, 1) * sa + pltpu.roll(t, 8, 1) * sb


def _rot_t(d, c, sa, sb):
    return d * c + pltpu.roll(d * sa, 8, 1) + pltpu.roll(d * sb, 120, 1)


def _rot_consts():
    lane = np.arange(128) % HEAD_DIM
    inv_freq = (np.float32(ROPE_THETA) ** (-np.arange(0, ROT_DIM, 2, dtype=np.float32) / np.float32(ROT_DIM))).astype(np.float32)
    invf = np.where(lane < ROT_DIM, inv_freq[lane % ROT_HALF], 0.0).astype(np.float32)
    ma = np.where(lane < ROT_HALF, -1.0, 0.0).astype(np.float32)
    mb = np.where((lane >= ROT_HALF) & (lane < ROT_DIM), 1.0, 0.0).astype(np.float32)
    return jnp.asarray(invf[None]), jnp.asarray(ma[None]), jnp.asarray(mb[None])


def _row_spec(tm, w):
    return pl.BlockSpec((tm, w), lambda i: (i, 0))


def _full_spec(shape):
    return pl.BlockSpec(shape, lambda i: (0,) * len(shape))


def _weight_spec(shape):
    return pl.BlockSpec(shape, lambda i: (0,) * len(shape), pipeline_mode=pl.Buffered(1))


FF_CHUNKS = (256, 512, 1024, 1024)
FF_SPANS = [(int(o), n) for o, n in zip(np.cumsum((0,) + FF_CHUNKS[:-1]), FF_CHUNKS)]
FF_WHOLE = [(0, D_FF)]


def _ffn_weight_scratch(n_weights, spans):
    return [pltpu.VMEM((D_FF, D_MODEL), BF16)] * n_weights + [pltpu.SemaphoreType.DMA((n_weights, len(spans)))]


def _with_ffn_weights(w_hbm, w_vmem, sems, spans, run):
    copies = [[pltpu.make_async_copy(h.at[pl.ds(o, n)], v.at[pl.ds(o, n)], sems.at[j, c]) for c, (o, n) in enumerate(spans)]
              for j, (h, v) in enumerate(zip(w_hbm, w_vmem))]
    first = pl.program_id(0) == 0

    @pl.when(first)
    def _():
        for of_weight in copies:
            for cp in of_weight:
                cp.start()
        run(lambda j, c: copies[j][c].wait())

    @pl.when(jnp.logical_not(first))
    def _():
        run(None)


RES = 16


def _residue_scratch(n_arrays, tm, width):
    return [pltpu.VMEM((2, n_arrays, tm // RES, RES, width), F32), pltpu.SemaphoreType.DMA((2, n_arrays, RES))]


def _to_residue_rows(tiles, outs, scratch, sems, tm, n_steps):
    i = pl.program_id(0)
    slot = i % 2
    per = tm // RES

    def copies(step, s):
        return [pltpu.make_async_copy(scratch.at[s, a, :, b, :],
                                      outs[a].at[pl.ds(pl.multiple_of(b * (SEQ // RES) + per * step, per), per), :],
                                      sems.at[s, a, b]) for a in range(len(outs)) for b in range(RES)]

    @pl.when(i >= 2)
    def _():
        for cp in copies(i - 2, slot):
            cp.wait()

    for a, tile in enumerate(tiles):
        scratch[slot, a] = tile.reshape(per, RES, tile.shape[-1])
    for cp in copies(i, slot):
        cp.start()

    @pl.when(i == n_steps - 1)
    def _():
        for cp in copies(i - 1, 1 - slot) + copies(i, slot):
            cp.wait()


HALF = D_MODEL // 2


def in_proj_first(x, g1, w_in_ta):
    tm = 512

    def body(x_ref, g_ref, w_ref, h_ref, part_ref):
        xf = x_ref[...]
        h = (xf * _rms(xf) * g_ref[...]).astype(BF16)
        h_ref[...] = h
        part_ref[...] = _dot_nt(h[:, :HALF], w_ref[...])

    return _call(
        "in_proj_first", body, SEQ // tm,
        [_row_spec(tm, D_MODEL), _full_spec((1, D_MODEL)), _weight_spec((IN_W, HALF))],
        [_row_spec(tm, D_MODEL), _row_spec(tm, IN_W)],
        [jax.ShapeDtypeStruct((SEQ, D_MODEL), BF16), jax.ShapeDtypeStruct((SEQ, IN_W), F32)], (x, g1, w_in_ta))


def in_proj(h1, part, pos_col, w_in_tb, rot):
    tm = 512
    n_steps = SEQ // tm

    def body(h_ref, part_ref, pos_ref, w_ref, invf_ref, ma_ref, mb_ref, u_ref, vs_ref, q_ref, k_ref, v_ref, scratch, sems):
        proj = part_ref[...] + _dot_nt(h_ref[...], w_ref[...])
        c, sa, sb = _rot_tables(pos_ref[...], invf_ref[...], ma_ref[...], mb_ref[...])
        slabs = range(ATTN_W // 128)
        q = jnp.concatenate([_rot(proj[:, j * 128:(j + 1) * 128], c, sa, sb) * Q_SCALE for j in slabs], axis=1)
        k = jnp.concatenate([_rot(proj[:, ATTN_W + j * 128:ATTN_W + (j + 1) * 128], c, sa, sb) for j in slabs], axis=1)
        u_ref[...] = proj[:, 3 * ATTN_W:3 * ATTN_W + SGU_W]
        vs_ref[...] = proj[:, 3 * ATTN_W + SGU_W:]
        _to_residue_rows([q, k, proj[:, 2 * ATTN_W:3 * ATTN_W]], [q_ref, k_ref, v_ref], scratch, sems, tm, n_steps)

    act = jax.ShapeDtypeStruct((SEQ, 512), F32)
    return _call(
        "in_proj", body, n_steps,
        [pl.BlockSpec((tm, HALF), lambda i: (i, 1)), _row_spec(tm, IN_W), _row_spec(tm, 1), _weight_spec((IN_W, HALF)),
         _full_spec((1, 128)), _full_spec((1, 128)), _full_spec((1, 128))],
        [_row_spec(tm, 512)] * 2 + [ANY] * 3, [act] * 5,
        (h1, part, pos_col, w_in_tb, *rot), scratch_shapes=_residue_scratch(3, tm, ATTN_W))


def _to_residue_order(t):
    return t.reshape(SEQ // RES, RES, -1).transpose(1, 0, 2).reshape(t.shape)


def _block_rows(d, r, n):
    if d == 16:
        slices = [(128 * r, 128)]
    elif d == 4:
        slices = [(128 * (4 * b + r) + 32 * n, 32) for b in range(4)]
    else:
        slices = [(128 * b + 8 * n, 8) for b in range(RES)]
    return [(s if isinstance(s, int) else pl.multiple_of(s, z), z) for s, z in slices]


def _block_step(d, i):
    if d == 16:
        return i
    if d == 4:
        return 4 * (i & 31) + (i >> 5)
    return 16 * (i & 7) + (i >> 3)


def _attn_masks(d):
    row2 = _block_step(d, lax.broadcasted_iota(jnp.int32, (128, 256), 0))
    col2 = lax.broadcasted_iota(jnp.int32, (128, 256), 1)
    key2 = _block_step(d, col2 & 127)
    mask2 = jnp.logical_or(jnp.logical_and(col2 < 128, key2 >= row2), jnp.logical_and(col2 >= 128, key2 <= row2))
    row1 = _block_step(d, lax.broadcasted_iota(jnp.int32, (128, 128), 0))
    col1 = lax.broadcasted_iota(jnp.int32, (128, 128), 1)
    return col1 < HEAD_DIM, _block_step(d, col1) <= row1, mask2


def _load_rows(ref, slices):
    parts = [ref[pl.ds(s, z), :] for s, z in slices]
    return parts[0] if len(parts) == 1 else jnp.concatenate(parts, axis=0)


def _for_each_group(fn):
    for p, d in enumerate(DILATIONS):
        masks = _attn_masks(d)
        if d == 16:
            def group(i, carry, p=p, masks=masks):
                fn(p, masks, [(_block_rows(16, 8 * i + g, 0), None) for g in range(8)])
                return carry

            lax.fori_loop(0, 2, group, 0)
        elif d == 4:
            fn(p, masks, [(_block_rows(4, r, 0), None) for r in range(4)])

            def group(i, carry, p=p, masks=masks):
                blocks = [6 * i + g for g in range(6)]
                fn(p, masks, [(_block_rows(4, j % 4, 1 + j // 4), _block_rows(4, j % 4, j // 4)) for j in blocks])
                return carry

            lax.fori_loop(0, 2, group, 0)
        else:
            fn(p, masks, [(_block_rows(1, 0, 0), None)])

            def group(i, carry, p=p, masks=masks):
                fn(p, masks, [(_block_rows(1, 0, 5 * i + g + 1), _block_rows(1, 0, 5 * i + g)) for g in range(5)])
                return carry

            lax.fori_loop(0, 3, group, 0)


def attn_fwd(q, k, v):
    def body(q_ref, k_ref, v_ref, o_ref, lse_ref, nat_ref, op_ref, lp_ref, sems):
        def group(p, masks, blocks):
            head0, mask1, mask2 = masks
            heads = (head0, jnp.logical_not(head0))
            keys = [rows if prev is None else prev + rows for rows, prev in blocks]
            mask = [mask1 if prev is None else mask2 for _, prev in blocks]
            qb = [_load_rows(q_ref, rows) for rows, _ in blocks]
            kk = [_load_rows(k_ref, ks).astype(BF16) for ks in keys]
            vv = [_load_rows(v_ref, ks).astype(BF16) for ks in keys]
            chains = [(g, hm) for g in range(len(blocks)) for hm in heads]
            s = [jnp.where(mask[g], _dot_nt(jnp.where(hm, qb[g], 0.0).astype(BF16), kk[g]), NEG) for g, hm in chains]
            m = [jnp.max(t, axis=-1, keepdims=True) for t in s]
            e = [jnp.exp(t - mt) for t, mt in zip(s, m)]
            l = [jnp.sum(t, axis=-1, keepdims=True) for t in e]
            pv = [_dot(t.astype(BF16), vv[g]) for t, (g, _) in zip(e, chains)]
            for g, (rows, _) in enumerate(blocks):
                o_blk = jnp.where(head0, pv[2 * g] / l[2 * g], pv[2 * g + 1] / l[2 * g + 1])
                l_blk = jnp.where(head0, jnp.broadcast_to(m[2 * g] + jnp.log(l[2 * g]), (128, 128)),
                                  jnp.broadcast_to(m[2 * g + 1] + jnp.log(l[2 * g + 1]), (128, 128)))
                at = 0
                for start, size in rows:
                    op_ref[p, pl.ds(start, size), :] = o_blk[at:at + size]
                    lp_ref[p, pl.ds(start, size), :] = l_blk[at:at + size]
                    at += size

        _for_each_group(group)

        def combine(i, carry):
            rows = pl.ds(pl.multiple_of(i * 256, 256), 256)
            ls = [lp_ref[p, rows, :] for p in range(3)]
            m = jnp.maximum(jnp.maximum(ls[0], ls[1]), ls[2])
            lse = m + jnp.log(jnp.exp(ls[0] - m) + jnp.exp(ls[1] - m) + jnp.exp(ls[2] - m))
            o = jnp.zeros((256, 128), F32)
            for p in range(3):
                o = o + jnp.exp(ls[p] - lse) * op_ref[p, rows, :]
            o_ref[rows, :] = o
            lse_ref[rows, :] = lse
            return carry

        lax.fori_loop(0, SEQ // 256, combine, 0)

        lanes = pl.ds(pl.multiple_of(pl.program_id(0) * 128, 128), 128)
        back = [pltpu.make_async_copy(o_ref.at[pl.ds(b * (SEQ // RES), SEQ // RES), :], nat_ref.at[:, b, lanes], sems.at[b])
                for b in range(RES)]
        for cp in back:
            cp.start()
        for cp in back:
            cp.wait()

    slab = pl.BlockSpec((SEQ, 128), lambda i: (0, i))
    out = jax.ShapeDtypeStruct((SEQ, ATTN_W), F32)
    attn_r, lse, attn = _call(
        "attn_fwd", body, ATTN_W // 128, [slab] * 3, [slab] * 2 + [ANY],
        [out, out, jax.ShapeDtypeStruct((SEQ // RES, RES, ATTN_W), F32)], (q, k, v),
        scratch_shapes=[pltpu.VMEM((3, SEQ, 128), F32), pltpu.VMEM((3, SEQ, 128), F32), pltpu.SemaphoreType.DMA((RES,))])
    return attn_r, lse, attn.reshape(SEQ, ATTN_W)


def _causal_weights(w_ref):
    row = lax.broadcasted_iota(jnp.int32, (CHUNK, CHUNK), 0)
    col = lax.broadcasted_iota(jnp.int32, (CHUNK, CHUNK), 1)
    return [jnp.where(col <= row, w_ref[g], 0.0).astype(BF16) for g in range(N_GROUPS)], col <= row


def _sgu_chunk_fwd(u, vs, lg, lb, wc, bfull, head0):
    ug = _gelu(u)
    vg = _gelu(vs)
    xc = vg - jnp.mean(vg, axis=-1, keepdims=True)
    rstd = lax.rsqrt(jnp.mean(xc * xc, axis=-1, keepdims=True) + LN_EPS)
    xhat = xc * rstd
    vn = xhat * lg + lb
    mixed = []
    for gp in range(SGU_W // 128):
        vp = vn[:, gp * 128:(gp + 1) * 128].astype(BF16)
        mixed.append(jnp.where(head0, _dot(wc[2 * gp], vp), _dot(wc[2 * gp + 1], vp)))
    ms = jnp.concatenate(mixed, axis=1) + bfull
    return ug, xhat, rstd, vn, ms


def sgu_fwd(u, vs, lg, lb, w_sp, bfull):
    cpb = 4

    def body(u_ref, vs_ref, lg_ref, lb_ref, w_ref, b_ref, o_ref):
        wc, _ = _causal_weights(w_ref)
        head0 = lax.broadcasted_iota(jnp.int32, (CHUNK, 128), 1) < HEAD_DIM
        for ci in range(cpb):
            rows = pl.ds(ci * CHUNK, CHUNK)
            ug, _, _, _, ms = _sgu_chunk_fwd(u_ref[rows, :], vs_ref[rows, :], lg_ref[...], lb_ref[...], wc, b_ref[...], head0)
            o_ref[rows, :] = ug * ms

    tm = cpb * CHUNK
    return _call(
        "sgu_fwd", body, SEQ // tm,
        [_row_spec(tm, SGU_W), _row_spec(tm, SGU_W), _full_spec((1, SGU_W)), _full_spec((1, SGU_W)),
         _full_spec((N_GROUPS, CHUNK, CHUNK)), _full_spec((CHUNK, SGU_W))],
        [_row_spec(tm, SGU_W)], [jax.ShapeDtypeStruct((SEQ, SGU_W), F32)],
        (u, vs, lg, lb, w_sp, bfull))


def out_proj(attn, sgu, x, ga, gs, w_out, gpm, gpf):
    tm = 512

    def body(a_ref, s_ref, x_ref, ga_ref, gs_ref, w_ref, gpm_ref, gpf_ref, mix_ref, y_ref, x2_ref, h2_ref):
        a = a_ref[...]
        s = s_ref[...]
        an = (a * _rms(a) * ga_ref[...]).astype(BF16)
        sn = (s * _rms(s) * gs_ref[...]).astype(BF16)
        mix_ref[:, :ATTN_W] = an
        mix_ref[:, ATTN_W:] = sn
        y = _dot(an, w_ref[:ATTN_W, :]) + _dot(sn, w_ref[ATTN_W:, :])
        y_ref[...] = y
        x2 = x_ref[...] + y * _rms(y) * gpm_ref[...]
        x2_ref[...] = x2
        h2_ref[...] = (x2 * _rms(x2) * gpf_ref[...]).astype(BF16)

    wide = jax.ShapeDtypeStruct((SEQ, D_MODEL), F32)
    wide16 = jax.ShapeDtypeStruct((SEQ, D_MODEL), BF16)
    return _call(
        "out_proj", body, SEQ // tm,
        [_row_spec(tm, ATTN_W), _row_spec(tm, SGU_W), _row_spec(tm, D_MODEL), _full_spec((1, ATTN_W)),
         _full_spec((1, SGU_W)), _weight_spec((D_MODEL, D_MODEL)), _full_spec((1, D_MODEL)), _full_spec((1, D_MODEL))],
        [_row_spec(tm, D_MODEL)] * 4, [wide16, wide, wide, wide16],
        (attn, sgu, x, ga, gs, w_out, gpm, gpf))


def ffn_up(h2, w_gate_t, w_up_t):
    tm = 256

    def body(h_ref, wg_hbm, wu_hbm, g_ref, u_ref, a_ref, wg_ref, wu_ref, sems):
        def run(wait):
            h = h_ref[...]
            if wait:
                wait(0, 0)
            g = _dot_nt(h, wg_ref[...])
            g_ref[...] = g.astype(BF16)
            if wait:
                wait(1, 0)
            u = _dot_nt(h, wu_ref[...])
            u_ref[...] = u.astype(BF16)
            a_ref[...] = (g * jax.nn.sigmoid(g) * u).astype(BF16)

        _with_ffn_weights([wg_hbm, wu_hbm], [wg_ref, wu_ref], sems, FF_WHOLE, run)

    ff = jax.ShapeDtypeStruct((SEQ, D_FF), BF16)
    return _call(
        "ffn_up", body, SEQ // tm, [_row_spec(tm, D_MODEL), ANY, ANY],
        [_row_spec(tm, D_FF)] * 3, [ff, ff, jax.ShapeDtypeStruct((SEQ, D_FF), BF16)],
        (h2, w_gate_t, w_up_t), scratch_shapes=_ffn_weight_scratch(2, FF_WHOLE))


def ffn_down_loss(act, w_down, x2, gpo, target):
    tm = 512

    def body(a_ref, w_hbm, x2_ref, g_ref, t_ref, df_ref, dx3_ref, dg_ref, loss_ref, w_ref, sems):
        def run(wait):
            f = None
            for c, (o, n) in enumerate(FF_SPANS if wait else FF_WHOLE):
                if wait:
                    wait(0, c)
                part = _dot(a_ref[:, o:o + n], w_ref[o:o + n, :])
                f = part if f is None else f + part
            gain = g_ref[...]
            err = x2_ref[...] + f * _rms(f) * gain - t_ref[...]
            dx3 = err * np.float32(1.0 / D_MODEL)
            dx3_ref[...] = dx3
            df, dg = _rms_bwd(f, gain, dx3)
            df_ref[...] = df.astype(BF16)
            loss = jnp.sum(err * err, axis=(0, 1), keepdims=True)
            if wait:
                dg_ref[...] = dg
                loss_ref[...] = loss
            else:
                dg_ref[...] += dg
                loss_ref[...] += loss

        _with_ffn_weights([w_hbm], [w_ref], sems, FF_SPANS, run)

    return _call(
        "ffn_down_loss", body, SEQ // tm,
        [_row_spec(tm, D_FF), ANY, _row_spec(tm, D_MODEL), _full_spec((1, D_MODEL)), _row_spec(tm, D_MODEL)],
        [_row_spec(tm, D_MODEL), _row_spec(tm, D_MODEL), _full_spec((1, D_MODEL)), _full_spec((1, 1))],
        [jax.ShapeDtypeStruct((SEQ, D_MODEL), BF16), jax.ShapeDtypeStruct((SEQ, D_MODEL), F32),
         jax.ShapeDtypeStruct((1, D_MODEL), F32), jax.ShapeDtypeStruct((1, 1), F32)],
        (act, w_down, x2, gpo, target), scratch_shapes=_ffn_weight_scratch(1, FF_SPANS))


def ffn_bwd(df, w_down, gate, up, w_gate_t, w_up_t, x2, gpf, dx3, y, gpm, after=()):
    tm = 256

    def body(df_ref, wd_hbm, g_ref, u_ref, wg_hbm, wu_hbm, x2_ref, gpf_ref, dx3_ref, y_ref, gpm_ref,
             dg_ref, du_ref, dx2_ref, dy_ref, dgpf_ref, dgpm_ref, wd_ref, wg_ref, wu_ref, sems):
        def run(wait):
            if wait:
                wait(0, 0)
            dact = _dot_nt(df_ref[...], wd_ref[...])
            g = g_ref[...].astype(F32)
            s = jax.nn.sigmoid(g)
            dup = (dact * g * s).astype(BF16)
            dgate = (dact * u_ref[...].astype(F32) * (s * (1.0 + g * (1.0 - s)))).astype(BF16)
            du_ref[...] = dup
            dg_ref[...] = dgate
            if wait:
                wait(1, 0)
                wait(2, 0)
            dh2 = _dot(dgate, wg_ref[...]) + _dot(dup, wu_ref[...])
            dz, dgpf = _rms_bwd(x2_ref[...], gpf_ref[...], dh2)
            dx2 = dx3_ref[...] + dz
            dx2_ref[...] = dx2
            dy, dgpm = _rms_bwd(y_ref[...], gpm_ref[...], dx2)
            dy_ref[...] = dy.astype(BF16)
            if wait:
                dgpf_ref[...] = dgpf
                dgpm_ref[...] = dgpm
            else:
                dgpf_ref[...] += dgpf
                dgpm_ref[...] += dgpm

        _with_ffn_weights([wd_hbm, wg_hbm, wu_hbm], [wd_ref, wg_ref, wu_ref], sems, FF_WHOLE, run)

    vec = jax.ShapeDtypeStruct((1, D_MODEL), F32)
    ff16 = jax.ShapeDtypeStruct((SEQ, D_FF), BF16)
    return _call(
        "ffn_bwd", body, SEQ // tm,
        [_row_spec(tm, D_MODEL), ANY, _row_spec(tm, D_FF), _row_spec(tm, D_FF), ANY, ANY, _row_spec(tm, D_MODEL),
         _full_spec((1, D_MODEL)), _row_spec(tm, D_MODEL), _row_spec(tm, D_MODEL), _full_spec((1, D_MODEL))],
        [_row_spec(tm, D_FF), _row_spec(tm, D_FF), _row_spec(tm, D_MODEL), _row_spec(tm, D_MODEL),
         _full_spec((1, D_MODEL)), _full_spec((1, D_MODEL))],
        [ff16, ff16, jax.ShapeDtypeStruct((SEQ, D_MODEL), F32), jax.ShapeDtypeStruct((SEQ, D_MODEL), BF16), vec, vec],
        (df, w_down, gate, up, w_gate_t, w_up_t, x2, gpf, dx3, y, gpm), scratch_shapes=_ffn_weight_scratch(3, FF_WHOLE), after=after)


def weight_grads(name, lhs, b, after=()):
    m, n, k = lhs[0].shape[1], b.shape[1], len(lhs)
    tr = 256

    def body(*refs):
        for a_ref, o_ref in zip(refs[:k], refs[k + 1:]):
            o_ref[...] = _dot_tn(a_ref[...], refs[k][...]).astype(BF16)

    outs = _call(
        name, body, m // tr, [pl.BlockSpec((SEQ, tr), lambda i: (0, i))] * k + [_weight_spec((SEQ, n))],
        [_row_spec(tr, n)] * k, [jax.ShapeDtypeStruct((m, n), BF16)] * k, (*lhs, b), after=after)
    return [out.reshape(N_DEV, m // N_DEV, n) for out in outs]


def weight_grad(name, a, b, after=()):
    return weight_grads(name, [a], b, after)[0]


def weight_grad_of_parts(name, parts, b, after=()):
    p, n, k = parts[0].shape[1], b.shape[1], len(parts)
    tr = 512
    per = p // tr

    def body(*refs):
        tile = pl.program_id(0)
        for j in range(k):
            @pl.when(tile // per == j)
            def _(j=j):
                refs[k + 1][...] = _dot_tn(refs[j][...].astype(BF16), refs[k][...]).astype(BF16)

    def part_spec(j):
        return pl.BlockSpec((SEQ, tr), lambda i: (0, jnp.clip(i - per * j, 0, per - 1)))

    (out,) = _call(
        name, body, k * per, [part_spec(j) for j in range(k)] + [_weight_spec((SEQ, n))],
        [_row_spec(tr, n)], [jax.ShapeDtypeStruct((k * p, n), BF16)], (*parts, b), after=after)
    return out.reshape(N_DEV, k * p // N_DEV, n)


def mix_bwd(dy, w_out, attn, sgu, ga, gs, after=()):
    tm = 512
    n_steps = SEQ // tm

    def body(dy_ref, w_ref, a_ref, s_ref, ga_ref, gs_ref, ds_ref, dga_ref, dgs_ref, da_ref, scratch, sems):
        dy = dy_ref[...]
        da, dga = _rms_bwd(a_ref[...], ga_ref[...], _dot_nt(dy, w_ref[:ATTN_W, :]))
        ds, dgs = _rms_bwd(s_ref[...], gs_ref[...], _dot_nt(dy, w_ref[ATTN_W:, :]))
        ds_ref[...] = ds
        _to_residue_rows([da], [da_ref], scratch, sems, tm, n_steps)

        @pl.when(pl.program_id(0) == 0)
        def _():
            dga_ref[...] = jnp.zeros_like(dga_ref)
            dgs_ref[...] = jnp.zeros_like(dgs_ref)

        dga_ref[...] += dga
        dgs_ref[...] += dgs

    half = jax.ShapeDtypeStruct((SEQ, 512), F32)
    vec = jax.ShapeDtypeStruct((1, 512), F32)
    return _call(
        "mix_bwd", body, n_steps,
        [_row_spec(tm, D_MODEL), _weight_spec((D_MODEL, D_MODEL)), _row_spec(tm, 512), _row_spec(tm, 512),
         _full_spec((1, 512)), _full_spec((1, 512))],
        [_row_spec(tm, 512), _full_spec((1, 512)), _full_spec((1, 512)), ANY],
        [half, vec, vec, half], (dy, w_out, attn, sgu, ga, gs), scratch_shapes=_residue_scratch(1, tm, ATTN_W), after=after)


def sgu_bwd(u, vs, dsgu, lg, lb, w_sp, bfull, after=()):
    cpb = 4

    def body(u_ref, vs_ref, d_ref, lg_ref, lb_ref, w_ref, b_ref, du_ref, dvs_ref, dlg_ref, dlb_ref, dw_ref, db_ref):
        wc, causal = _causal_weights(w_ref)
        head0 = lax.broadcasted_iota(jnp.int32, (CHUNK, 128), 1) < HEAD_DIM
        lg = lg_ref[...]

        @pl.when(pl.program_id(0) == 0)
        def _():
            dlg_ref[...] = jnp.zeros_like(dlg_ref)
            dlb_ref[...] = jnp.zeros_like(dlb_ref)
            dw_ref[...] = jnp.zeros_like(dw_ref)
            db_ref[...] = jnp.zeros_like(db_ref)

        for ci in range(cpb):
            rows = pl.ds(ci * CHUNK, CHUNK)
            u = u_ref[rows, :]
            vs = vs_ref[rows, :]
            d = d_ref[rows, :]
            ug, xhat, rstd, vn, ms = _sgu_chunk_fwd(u, vs, lg, lb_ref[...], wc, b_ref[...], head0)
            du_ref[rows, :] = (d * ms * _gelu_grad(u)).astype(BF16)
            dms = d * ug
            db_ref[...] += dms
            dvn = []
            for gp in range(SGU_W // 128):
                dmp = dms[:, gp * 128:(gp + 1) * 128]
                dm0 = jnp.where(head0, dmp, 0.0).astype(BF16)
                dm1 = jnp.where(head0, 0.0, dmp).astype(BF16)
                vp = vn[:, gp * 128:(gp + 1) * 128].astype(BF16)
                dw_ref[2 * gp] += _dot_nt(dm0, vp)
                dw_ref[2 * gp + 1] += _dot_nt(dm1, vp)
                dvn.append(_dot_tn(wc[2 * gp], dm0) + _dot_tn(wc[2 * gp + 1], dm1))
            dvn = jnp.concatenate(dvn, axis=1)
            dlg_ref[...] += jnp.sum(dvn * xhat, axis=0, keepdims=True)
            dlb_ref[...] += jnp.sum(dvn, axis=0, keepdims=True)
            dxh = dvn * lg
            dvg = rstd * (dxh - jnp.mean(dxh, axis=-1, keepdims=True) - xhat * jnp.mean(dxh * xhat, axis=-1, keepdims=True))
            dvs_ref[rows, :] = (dvg * _gelu_grad(vs)).astype(BF16)

        @pl.when(pl.program_id(0) == pl.num_programs(0) - 1)
        def _():
            for g in range(N_GROUPS):
                dw_ref[g] = jnp.where(causal, dw_ref[g], 0.0)

    tm = cpb * CHUNK
    half16 = jax.ShapeDtypeStruct((SEQ, SGU_W), BF16)
    vec = jax.ShapeDtypeStruct((1, SGU_W), F32)
    return _call(
        "sgu_bwd", body, SEQ // tm,
        [_row_spec(tm, SGU_W)] * 3 + [_full_spec((1, SGU_W)), _full_spec((1, SGU_W)),
                                      _full_spec((N_GROUPS, CHUNK, CHUNK)), _full_spec((CHUNK, SGU_W))],
        [_row_spec(tm, SGU_W), _row_spec(tm, SGU_W), _full_spec((1, SGU_W)), _full_spec((1, SGU_W)),
         _full_spec((N_GROUPS, CHUNK, CHUNK)), _full_spec((CHUNK, SGU_W))],
        [half16, half16, vec, vec, jax.ShapeDtypeStruct((N_GROUPS, CHUNK, CHUNK), F32),
         jax.ShapeDtypeStruct((CHUNK, SGU_W), F32)],
        (u, vs, dsgu, lg, lb, w_sp, bfull), after=after)


def attn_bwd(q, k, v, o, lse, do, pos_col, rot):
    n_steps = ATTN_W // 128

    def body(q_ref, k_ref, v_ref, o_ref, lse_ref, do_ref, pos_ref, invf_ref, ma_ref, mb_ref,
             dq_ref, dk_ref, dv_ref, dqa_ref, dka_ref, dva_ref, dlt_ref, rot_ref, out_ref, sems):
        step = pl.program_id(0)
        slot = step % 2

        def back(at_step, s):
            lanes = pl.ds(pl.multiple_of(at_step * 128, 128), 128)
            return [pltpu.make_async_copy(out_ref.at[s, a, pl.ds(b * (SEQ // RES), SEQ // RES), :], nat.at[:, b, lanes],
                                          sems.at[s, a, b]) for a, nat in enumerate((dq_ref, dk_ref, dv_ref)) for b in range(RES)]

        dqa_ref[...] = jnp.zeros_like(dqa_ref)
        dka_ref[...] = jnp.zeros_like(dka_ref)
        dva_ref[...] = jnp.zeros_like(dva_ref)

        def delta(i, carry):
            rows = pl.ds(pl.multiple_of(i * 256, 256), 256)
            prod = do_ref[rows, :] * o_ref[rows, :]
            h0 = lax.broadcasted_iota(jnp.int32, (256, 128), 1) < HEAD_DIM
            d0 = jnp.sum(jnp.where(h0, prod, 0.0), axis=-1, keepdims=True)
            d1 = jnp.sum(jnp.where(h0, 0.0, prod), axis=-1, keepdims=True)
            dlt_ref[rows, :] = jnp.where(h0, d0, d1)
            return carry

        lax.fori_loop(0, SEQ // 256, delta, 0)

        def add_rows(ref, slices, val):
            at = 0
            for start, size in slices:
                ref[pl.ds(start, size), :] += val[at:at + size]
                at += size

        def group(p, masks, blocks):
            head0, mask1, mask2 = masks
            heads = (head0, jnp.logical_not(head0))
            keys = [rows if prev is None else prev + rows for rows, prev in blocks]
            mask = [mask1 if prev is None else mask2 for _, prev in blocks]
            kk = [_load_rows(k_ref, ks).astype(BF16) for ks in keys]
            vv = [_load_rows(v_ref, ks).astype(BF16) for ks in keys]
            qb = [_load_rows(q_ref, rows) for rows, _ in blocks]
            dob = [_load_rows(do_ref, rows) for rows, _ in blocks]
            lse_b = [_load_rows(lse_ref, rows) for rows, _ in blocks]
            dlt_b = [_load_rows(dlt_ref, rows) for rows, _ in blocks]
            chains = [(g, h) for g in range(len(blocks)) for h in range(2)]
            qm = [jnp.where(heads[h], qb[g], 0.0).astype(BF16) for g, h in chains]
            dom = [jnp.where(heads[h], dob[g], 0.0).astype(BF16) for g, h in chains]
            s = [_dot_nt(qm[c], kk[g]) for c, (g, h) in enumerate(chains)]
            dp = [_dot_nt(dom[c], vv[g]) for c, (g, h) in enumerate(chains)]
            pr = [jnp.where(mask[g], jnp.exp(s[c] - lse_b[g][:, h * HEAD_DIM:h * HEAD_DIM + 1]), 0.0)
                  for c, (g, h) in enumerate(chains)]
            ds = [(pr[c] * (dp[c] - dlt_b[g][:, h * HEAD_DIM:h * HEAD_DIM + 1])).astype(BF16)
                  for c, (g, h) in enumerate(chains)]
            dv = [_dot_tn(pr[c].astype(BF16), dom[c]) for c in range(len(chains))]
            dk = [_dot_tn(ds[c], qm[c]) for c in range(len(chains))]
            dq = [_dot(ds[c], kk[g]) for c, (g, h) in enumerate(chains)]
            for g, (rows, _) in enumerate(blocks):
                add_rows(dqa_ref, rows, jnp.where(head0, dq[2 * g], dq[2 * g + 1]))
                add_rows(dka_ref, keys[g], dk[2 * g] + dk[2 * g + 1])
                add_rows(dva_ref, keys[g], dv[2 * g] + dv[2 * g + 1])

        _for_each_group(group)

        @pl.when(pl.program_id(0) == 0)
        def _():
            def tables(i, carry):
                rows = pl.ds(pl.multiple_of(i * 256, 256), 256)
                c, sa, sb = _rot_tables(pos_ref[rows, :], invf_ref[...], ma_ref[...], mb_ref[...])
                rot_ref[0, rows, :] = c
                rot_ref[1, rows, :] = sa
                rot_ref[2, rows, :] = sb
                return carry

            lax.fori_loop(0, SEQ // 256, tables, 0)

        @pl.when(step >= 2)
        def _():
            for cp in back(step - 2, slot):
                cp.wait()

        def finish(i, carry):
            rows = pl.ds(pl.multiple_of(i * 256, 256), 256)
            c, sa, sb = rot_ref[0, rows, :], rot_ref[1, rows, :], rot_ref[2, rows, :]
            out_ref[slot, 0, rows, :] = _rot_t(dqa_ref[rows, :] * Q_SCALE, c, sa, sb)
            out_ref[slot, 1, rows, :] = _rot_t(dka_ref[rows, :], c, sa, sb)
            out_ref[slot, 2, rows, :] = dva_ref[rows, :]
            return carry

        lax.fori_loop(0, SEQ // 256, finish, 0)
        for cp in back(step, slot):
            cp.start()

        @pl.when(step == n_steps - 1)
        def _():
            for cp in back(step - 1, 1 - slot) + back(step, slot):
                cp.wait()

    slab = pl.BlockSpec((SEQ, 128), lambda i: (0, i))
    out = jax.ShapeDtypeStruct((SEQ // RES, RES, ATTN_W), F32)
    acc = pltpu.VMEM((SEQ, 128), F32)
    outs = _call(
        "attn_bwd", body, n_steps,
        [slab] * 6 + [_full_spec((SEQ, 1)), _full_spec((1, 128)), _full_spec((1, 128)), _full_spec((1, 128))],
        [ANY] * 3, [out, out, out], (q, k, v, o, lse, do, pos_col, *rot),
        scratch_shapes=[acc, acc, acc, acc, pltpu.VMEM((3, SEQ, 128), F32), pltpu.VMEM((2, 3, SEQ, 128), F32),
                        pltpu.SemaphoreType.DMA((2, 3, RES))])
    return [t.reshape(SEQ, ATTN_W) for t in outs]


def in_bwd(dproj_parts, w_in_t_halves, x, g1, dx2, after=()):
    tm = 512
    k = len(dproj_parts)

    def body(*refs):
        wa_ref, wb_ref, x_ref, g_ref, dx2_ref, dx_ref, dg_ref = refs[k:]
        parts = [refs[j][...].astype(BF16) for j in range(k)]
        halves = []
        for w_ref in (wa_ref, wb_ref):
            half = _dot(parts[0], w_ref[0:512, :])
            for j in range(1, k):
                half = half + _dot(parts[j], w_ref[512 * j:512 * (j + 1), :])
            halves.append(half)
        dh1 = jnp.concatenate(halves, axis=1)
        dz, dg = _rms_bwd(x_ref[...], g_ref[...], dh1)
        dx_ref[...] = dx2_ref[...] + dz

        @pl.when(pl.program_id(0) == 0)
        def _():
            dg_ref[...] = jnp.zeros_like(dg_ref)

        dg_ref[...] += dg

    return _call(
        "in_bwd", body, SEQ // tm,
        [_row_spec(tm, 512)] * k + [_weight_spec((IN_W, HALF)), _weight_spec((IN_W, HALF)), _row_spec(tm, D_MODEL),
                                    _full_spec((1, D_MODEL)), _row_spec(tm, D_MODEL)],
        [_row_spec(tm, D_MODEL), _full_spec((1, D_MODEL))],
        [jax.ShapeDtypeStruct((SEQ, D_MODEL), F32), jax.ShapeDtypeStruct((1, D_MODEL), F32)],
        (*dproj_parts, *w_in_t_halves, x, g1, dx2), after=after)


def _coords():
    return lax.axis_index("x"), lax.axis_index("y"), lax.axis_index("c")


class Exchange:
    def __init__(self, srcs, bufs, new_shapes, n_sems, make):
        self.srcs, self.bufs, self.new_shapes, self.n_sems, self.make = list(srcs), list(bufs), list(new_shapes), n_sems, make


def _call(name, body, n_steps, in_specs, out_specs, out_shape, args, scratch_shapes=(), after=()):
    n_in = len(args)

    def wrapped(*refs):
        body(*refs[:n_in], *refs[n_in + len(after):])

    return list(pl.pallas_call(
        wrapped, name=name, grid=(n_steps,), in_specs=list(in_specs) + [ANY] * len(after), out_specs=list(out_specs),
        out_shape=list(out_shape), scratch_shapes=list(scratch_shapes), compiler_params=_params(),
    )(*args, *after))


GATHER_SEMS = 8


def gather(bufs):
    n = len(bufs)

    def make(src_refs, buf_refs, new_refs, send_sems, recv_sems):
        x, y, c = _coords()
        me, sibling = (x, y, c), (x, y, 1 - c)
        over_x, over_y, across = (1 - x, y), (x, 1 - y), (1 - x, 1 - y)

        def copy(a, k, block, to, half=None):
            r = buf_refs[a].shape[0] // N_DEV
            lo, size = (0, r) if half is None else (half * (r // 2), r // 2)
            rows = buf_refs[a].at[pl.ds((4 * block[0] + 2 * block[1] + block[2]) * r + lo, size), :]
            return pltpu.make_async_remote_copy(
                src_ref=rows, dst_ref=rows, send_sem=send_sems.at[GATHER_SEMS * a + k],
                recv_sem=recv_sems.at[GATHER_SEMS * a + k], device_id=to, device_id_type=MESH)

        every = range(n)
        out = ([copy(a, 0, me, sibling) for a in every] + [copy(a, 1, me, (*over_x, c)) for a in every]
               + [copy(a, 2, me, (*over_y, c)) for a in every])
        near_in = [copy(a, 1, (*over_x, c), me) for a in every] + [copy(a, 2, (*over_y, c), me) for a in every]
        relay = ([copy(a, 3, (*over_x, c), (*over_y, c), half=0) for a in every]
                 + [copy(a, 4, (*over_y, c), (*over_x, c), half=1) for a in every])
        near_on = [copy(a, 5, (*over_x, c), sibling) for a in every] + [copy(a, 6, (*over_y, c), sibling) for a in every]
        relay_in = ([copy(a, 3, (*across, c), me, half=0) for a in every]
                    + [copy(a, 4, (*across, c), me, half=1) for a in every])
        far_on = [copy(a, 7, (*across, c), sibling) for a in every]
        from_core = ([copy(a, 0, sibling, me) for a in every] + [copy(a, 5, (*over_x, 1 - c), me) for a in every]
                     + [copy(a, 6, (*over_y, 1 - c), me) for a in every] + [copy(a, 7, (*across, 1 - c), me) for a in every])
        stages = [([], out), (near_in, relay + near_on), (relay_in, far_on)]
        return stages, out + relay + near_on + far_on, from_core

    return Exchange([], bufs, [], GATHER_SEMS * n, make)


TO_GATHER = (1, lambda x, y, c: [(x, y, 1 - c), (1 - x, y, c), (x, 1 - y, c)])
TO_SIBLING = (2, lambda x, y, c: [(x, y, 1 - c)])
TO_CHIPS = (3, lambda x, y, c: [(1 - x, y, c), (x, 1 - y, c), (1 - x, 1 - y, c)])
TO_ALL = (4, lambda x, y, c: [(x ^ (m >> 2), y ^ ((m >> 1) & 1), c ^ (m & 1)) for m in range(1, N_DEV)])


def by_sequencer(name, exchanges, who):
    collective_id, peers_of = who
    hbm = pltpu.MemorySpace.HBM
    refs = [([jax.new_ref(a, memory_space=hbm) for a in ex.srcs], [jax.new_ref(a, memory_space=hbm) for a in ex.bufs],
             [jax.empty_ref(s, memory_space=hbm) for s in ex.new_shapes]) for ex in exchanges]
    sems = []
    for ex in exchanges:
        sems += [pltpu.SemaphoreType.DMA((ex.n_sems,)), pltpu.SemaphoreType.DMA((ex.n_sems,))]

    @pl.kernel(mesh=plsc.ScalarSubcoreMesh(axis_name="sequencer", num_cores=1), name=name, scratch_types=tuple(sems),
               compiler_params=pltpu.CompilerParams(collective_id=collective_id))
    def launch(*sem_refs):
        peers = peers_of(*_coords())
        barrier = pltpu.get_barrier_semaphore()
        for peer in peers:
            pl.semaphore_signal(barrier, inc=1, device_id=peer, device_id_type=MESH)
        pl.semaphore_wait(barrier, len(peers))

        made = [ex.make(*refs[k], sem_refs[2 * k], sem_refs[2 * k + 1]) for k, ex in enumerate(exchanges)]
        for stage in range(max(len(stages) for stages, _, _ in made)):
            for stages, _, _ in made:
                if stage < len(stages):
                    arrivals, starts = stages[stage]
                    for cp in arrivals:
                        cp.wait_recv()
                    for cp in starts:
                        cp.start()
        for _, sends, arrivals in made:
            for cp in arrivals:
                cp.wait_recv()
            for cp in sends:
                cp.wait_send()

    launch()
    return [([ref[...] for ref in bufs], [ref[...] for ref in news]) for _, bufs, news in refs]


def place_shards(name, shards, dev, col_parts=1):
    blocks = [(s, (s.shape[0], s.shape[1] // col_parts), c) for s in shards for c in range(col_parts)]
    n = len(blocks)

    def body(dev_ref, *refs):
        for a in range(n):
            refs[n + a][...] = refs[a][...].astype(BF16)

    spec = pltpu.PrefetchScalarGridSpec(
        num_scalar_prefetch=1, grid=(1,),
        in_specs=[pl.BlockSpec(shape, lambda i, dev_ref, c=c: (0, c)) for _, shape, c in blocks],
        out_specs=[pl.BlockSpec(shape, lambda i, dev_ref: (dev_ref[0], 0)) for _, shape, _ in blocks])
    return pl.pallas_call(
        body, name=name, grid_spec=spec,
        out_shape=[jax.ShapeDtypeStruct((N_DEV * shape[0], shape[1]), BF16) for _, shape, _ in blocks],
        compiler_params=_params(),
    )(dev, *[s for s, _, _ in blocks])


def _swap(copies_of):
    def make(src_refs, buf_refs, new_refs, send_sems, recv_sems):
        copies = copies_of(src_refs, new_refs, send_sems, recv_sems)
        return [([], copies)], copies, copies

    return make


def to_sibling(grads):
    def copies_of(src_refs, new_refs, send_sems, recv_sems):
        x, y, c = _coords()
        return [pltpu.make_async_remote_copy(
            src_ref=src_refs[a].at[2 * xy + 1 - c], dst_ref=new_refs[a].at[xy], send_sem=send_sems.at[4 * a + xy],
            recv_sem=recv_sems.at[4 * a + xy], device_id=(x, y, 1 - c), device_id_type=MESH)
            for a in range(len(src_refs)) for xy in range(4)]

    return Exchange(grads, [], [jax.ShapeDtypeStruct((4,) + g.shape[1:], g.dtype) for g in grads], 4 * len(grads),
                    _swap(copies_of))


def to_chips(parts):
    def copies_of(src_refs, new_refs, send_sems, recv_sems):
        x, y, c = _coords()
        chips = [(1 - x, y), (x, 1 - y), (1 - x, 1 - y)]
        return [pltpu.make_async_remote_copy(
            src_ref=src_refs[a].at[2 * px + py], dst_ref=new_refs[a].at[2 * x + y], send_sem=send_sems.at[3 * a + j],
            recv_sem=recv_sems.at[3 * a + j], device_id=(px, py, c), device_id_type=MESH)
            for a in range(len(src_refs)) for j, (px, py) in enumerate(chips)]

    return Exchange(parts, [], [jax.ShapeDtypeStruct(p.shape, p.dtype) for p in parts], 3 * len(parts), _swap(copies_of))


def to_owners(grad):
    def copies_of(src_refs, new_refs, send_sems, recv_sems):
        x, y, c = _coords()
        copies = []
        for m in range(1, N_DEV):
            px, py, pc = x ^ (m >> 2), y ^ ((m >> 1) & 1), c ^ (m & 1)
            copies.append(pltpu.make_async_remote_copy(
                src_ref=src_refs[0].at[4 * px + 2 * py + pc], dst_ref=new_refs[0].at[4 * x + 2 * y + c],
                send_sem=send_sems.at[m - 1], recv_sem=recv_sems.at[m - 1], device_id=(px, py, pc), device_id_type=MESH))
        return copies

    return Exchange([grad], [], [jax.ShapeDtypeStruct(grad.shape, grad.dtype)], N_DEV - 1, _swap(copies_of))


def to_everyone(vec):
    def copies_of(src_refs, new_refs, send_sems, recv_sems):
        x, y, c = _coords()
        copies = []
        for m in range(1, N_DEV):
            px, py, pc = x ^ (m >> 2), y ^ ((m >> 1) & 1), c ^ (m & 1)
            copies.append(pltpu.make_async_remote_copy(
                src_ref=src_refs[0], dst_ref=new_refs[0].at[4 * x + 2 * y + c],
                send_sem=send_sems.at[m - 1], recv_sem=recv_sems.at[m - 1], device_id=(px, py, pc), device_id_type=MESH))
        return copies

    return Exchange([vec], [], [jax.ShapeDtypeStruct((N_DEV,) + vec.shape, vec.dtype)], N_DEV - 1, _swap(copies_of))


def sum_cores(name, grads, others, core, after=()):
    k = len(grads)

    def body(core_ref, *refs):
        for j in range(k):
            out_ref = refs[2 * k + len(after) + j]
            out_ref[...] = (refs[j][:, 0].astype(F32) + refs[k + j][...].astype(F32)).astype(out_ref.dtype)

    mine = [pl.BlockSpec((2, 1) + o.shape[1:], lambda i, core_ref: (i, core_ref[0], 0, 0)) for o in others]
    theirs = [pl.BlockSpec((2,) + o.shape[1:], lambda i, core_ref: (i, 0, 0)) for o in others]
    return pl.pallas_call(
        body, name=name,
        grid_spec=pltpu.PrefetchScalarGridSpec(
            num_scalar_prefetch=1, grid=(2,), in_specs=mine + theirs + [ANY] * len(after), out_specs=theirs),
        out_shape=[jax.ShapeDtypeStruct(o.shape, o.dtype) for o in others],
        compiler_params=_params(),
    )(core, *[g.reshape((4, 2) + g.shape[1:]) for g in grads], *others, *after)


def sum_owned(name, grad, others, dev_ids, after=()):
    _, r, w = grad.shape

    def body(ids_ref, *refs):
        acc = refs[0][0]
        for k in range(1, N_DEV):
            acc = acc + refs[k][0]
        refs[-1][...] = acc

    def pick(k):
        return pl.BlockSpec((1, r, w), lambda i, ids_ref: (ids_ref[k], 0, 0))

    return pl.pallas_call(
        body, name=name,
        grid_spec=pltpu.PrefetchScalarGridSpec(
            num_scalar_prefetch=1, grid=(1,), in_specs=[pick(k) for k in range(N_DEV)] + [ANY] * len(after),
            out_specs=pl.BlockSpec((r, w), lambda i, ids_ref: (ids_ref[0], 0))),
        out_shape=jax.ShapeDtypeStruct((N_DEV * r, w), F32),
        compiler_params=_params(),
    )(dev_ids, grad, *([others] * (N_DEV - 1)), *after)


def _adamw_update(w, g, m, v):
    nm = ADAM_B1 * m + np.float32(1.0 - ADAM_B1) * g
    nv = ADAM_B2 * v + np.float32(1.0 - ADAM_B2) * (g * g)
    m_hat = nm / np.float32(1.0 - ADAM_B1 ** ADAM_STEP)
    v_hat = nv / np.float32(1.0 - ADAM_B2 ** ADAM_STEP)
    return -ADAM_LR * (m_hat / (jnp.sqrt(v_hat) + ADAM_EPS) + ADAM_WD * w), nm, nv


def adamw_of_sums(name, parts, others, chip_ids, ws, ms, vs, after):
    n = len(parts)
    halves = 2

    def body(ids_ref, *refs):
        outs = refs[7 * n + len(after):]
        for j in range(n):
            p_ref, a_ref, b_ref, c_ref, w_ref, m_ref, v_ref = refs[7 * j:7 * j + 7]
            g = ((p_ref[0].astype(F32) + a_ref[0].astype(F32)) + b_ref[0].astype(F32)) + c_ref[0].astype(F32)
            outs[4 * j][...] = g
            outs[4 * j + 1][...], outs[4 * j + 2][...], outs[4 * j + 3][...] = _adamw_update(w_ref[...], g, m_ref[...], v_ref[...])

    in_specs, out_specs, out_shape, operands = [], [], [], []
    for part, other, w, m, v in zip(parts, others, ws, ms, vs):
        _, r, wd = part.shape
        rows = r // halves
        whole = pl.BlockSpec((rows, wd), lambda i, ids_ref: (i, 0))
        in_specs += [pl.BlockSpec((1, rows, wd), lambda i, ids_ref, k=k: (ids_ref[k], i, 0)) for k in range(4)] + [whole] * 3
        out_specs += [whole] * 4
        out_shape += [jax.ShapeDtypeStruct((r, wd), F32)] * 4
        operands += [part, other, other, other, w, m, v]
    outs = pl.pallas_call(
        body, name=name,
        grid_spec=pltpu.PrefetchScalarGridSpec(
            num_scalar_prefetch=1, grid=(halves,), in_specs=in_specs + [ANY] * len(after), out_specs=out_specs),
        out_shape=out_shape,
        compiler_params=_params(),
    )(chip_ids, *operands, *after)
    return [tuple(outs[4 * j:4 * j + 4]) for j in range(n)]


def pack_small(parts):
    names = [name for name, _ in SMALL if name in parts]
    operands = [parts[name] for name in names]
    first_row, at = {}, 0
    for name, size in SMALL:
        first_row[name] = at // 128
        at += size
    sizes = dict(SMALL)

    def body(*refs):
        out_ref = refs[-1]
        out_ref[...] = jnp.zeros_like(out_ref)
        for name, ref in zip(names, refs):
            row = first_row[name]
            if name == "loss_sum":
                lane0 = lax.broadcasted_iota(jnp.int32, (1, 128), 1) == 0
                out_ref[row:row + 1, :] = jnp.where(lane0, ref[...], 0.0)
            else:
                rows = sizes[name] // 128
                out_ref[row:row + rows, :] = ref[...].reshape(rows, 128)

    vmem = pl.BlockSpec(memory_space=pltpu.VMEM)
    return pl.pallas_call(
        body, name="pack_small", in_specs=[vmem] * len(names), out_specs=vmem,
        out_shape=jax.ShapeDtypeStruct((SMALL_ROWS, 128), F32), compiler_params=_params(()),
    )(*operands)


LATE = "pre_mix_norm"


def adamw_small(packed_g, late_parts, late_own, dev, ws, ms, vs, after=()):
    names = [name for name, _ in SMALL if name != "loss_sum"]
    k = len(names)
    shapes = [ws[name].shape[1:] if ws[name].ndim > 2 else ws[name].shape for name in names]
    first_row, at = [], 0
    for name, size in SMALL:
        first_row.append(at // 128)
        at += size

    def body(g_ref, late_ref, own_ref, dev_ref, *refs):
        w_refs, m_refs, v_refs, outs = refs[:k], refs[k:2 * k], refs[2 * k:3 * k], refs[3 * k + len(after):]
        for i, (name, size) in enumerate(SMALL[:k]):
            if name == LATE:
                parts = [jnp.where(dev_ref[0] == j, own_ref[...], late_ref[j]) for j in range(N_DEV)]
                g = parts[0]
                for j in range(1, N_DEV):
                    g = g + parts[j]
            else:
                g = g_ref[first_row[i]:first_row[i] + size // 128, :].reshape(shapes[i])
            outs[i][...] = g
            outs[k + i][...], outs[2 * k + i][...], outs[3 * k + i][...] = _adamw_update(
                w_refs[i][...], g, m_refs[i][...], v_refs[i][...])
        outs[4 * k][...] = g_ref[first_row[k]:first_row[k] + 1, 0:1]

    vmem = pl.BlockSpec(memory_space=pltpu.VMEM)
    operands = [t[name].reshape(shape) for t in (ws, ms, vs) for name, shape in zip(names, shapes)]
    outs = pl.pallas_call(
        body, name="adamw_small",
        in_specs=[vmem] * 3 + [pl.BlockSpec(memory_space=pltpu.SMEM)] + [vmem] * (3 * k) + [ANY] * len(after),
        out_specs=[vmem] * (4 * k + 1),
        out_shape=[jax.ShapeDtypeStruct(shape, F32) for _ in range(4) for shape in shapes] + [jax.ShapeDtypeStruct((1, 1), F32)],
        compiler_params=_params(()),
    )(packed_g, late_parts, late_own, dev, *operands, *after)
    tables = [{name: outs[j * k + i].reshape(ws[name].shape) for i, name in enumerate(names)} for j in range(4)]
    return (*tables, outs[4 * k])


def kernel(x, positions, pre_mix_norm, w_in, sgu_ln_gain, sgu_ln_bias, sgu_w_spatial, sgu_b_spatial, attn_out_norm, sgu_out_norm, w_out, post_mix_norm, pre_ffn_norm, w_gate, w_up, w_down, post_ffn_norm, loss_target, m_pre_mix_norm, m_w_in, m_sgu_ln_gain, m_sgu_ln_bias, m_sgu_w_spatial, m_sgu_b_spatial, m_attn_out_norm, m_sgu_out_norm, m_w_out, m_post_mix_norm, m_pre_ffn_norm, m_w_gate, m_w_up, m_w_down, m_post_ffn_norm, v_pre_mix_norm, v_w_in, v_sgu_ln_gain, v_sgu_ln_bias, v_sgu_w_spatial, v_sgu_b_spatial, v_attn_out_norm, v_sgu_out_norm, v_w_out, v_post_mix_norm, v_pre_ffn_norm, v_w_gate, v_w_up, v_w_down, v_post_ffn_norm):
    small_w = dict(pre_mix_norm=pre_mix_norm, sgu_ln_gain=sgu_ln_gain, sgu_ln_bias=sgu_ln_bias, sgu_w_spatial=sgu_w_spatial,
                   sgu_b_spatial=sgu_b_spatial, attn_out_norm=attn_out_norm, sgu_out_norm=sgu_out_norm,
                   post_mix_norm=post_mix_norm, pre_ffn_norm=pre_ffn_norm, post_ffn_norm=post_ffn_norm)
    small_m = dict(pre_mix_norm=m_pre_mix_norm, sgu_ln_gain=m_sgu_ln_gain, sgu_ln_bias=m_sgu_ln_bias, sgu_w_spatial=m_sgu_w_spatial,
                   sgu_b_spatial=m_sgu_b_spatial, attn_out_norm=m_attn_out_norm, sgu_out_norm=m_sgu_out_norm,
                   post_mix_norm=m_post_mix_norm, pre_ffn_norm=m_pre_ffn_norm, post_ffn_norm=m_post_ffn_norm)
    small_v = dict(pre_mix_norm=v_pre_mix_norm, sgu_ln_gain=v_sgu_ln_gain, sgu_ln_bias=v_sgu_ln_bias, sgu_w_spatial=v_sgu_w_spatial,
                   sgu_b_spatial=v_sgu_b_spatial, attn_out_norm=v_attn_out_norm, sgu_out_norm=v_sgu_out_norm,
                   post_mix_norm=v_post_mix_norm, pre_ffn_norm=v_pre_ffn_norm, post_ffn_norm=v_post_ffn_norm)

    x2d = x[0]
    target = loss_target[0]
    pos_col = positions.reshape(SEQ, 1)
    rot = _rot_consts()
    w_sp = sgu_w_spatial[0]
    bfull = jnp.repeat(sgu_b_spatial[0].T, HEAD_DIM, axis=1)

    x_i, y_i, c_i = (lax.axis_index(a).astype(jnp.int32) for a in MESH_AXES)
    dev = 4 * x_i + 2 * y_i + c_i
    core = c_i.reshape(1)
    chip = 2 * x_i + y_i
    chip_ids = jnp.stack([chip, chip ^ 1, chip ^ 2, chip ^ 3])
    dev_ids = jnp.stack([dev ^ m for m in range(N_DEV)])

    def gathered(name, bufs):
        return by_sequencer(name, [gather(bufs)], TO_GATHER)[0][0]

    def from_sibling(name, grads):
        return by_sequencer(name, [to_sibling(grads)], TO_SIBLING)[0][1]

    def from_chips(name, parts):
        return by_sequencer(name, [to_chips(parts)], TO_CHIPS)[0][1]

    w_in_ta, w_in_tb = place_shards("place_w_in", [w_in[0].T], dev.reshape(1), col_parts=2)
    (w_in_ta,) = gathered("gather_w_in_first", [w_in_ta])
    (w_in_tb,) = gathered("gather_w_in_second", [w_in_tb])
    w_gate_t, w_up_t, w_out_f, w_down_f = place_shards(
        "place_weights", [w_gate[0].T, w_up[0].T, w_out[0], w_down[0]], dev.reshape(1))
    (w_out_f,) = gathered("gather_w_out", [w_out_f])
    w_gate_t, w_up_t = gathered("gather_w_gate_up", [w_gate_t, w_up_t])
    (w_down_f,) = gathered("gather_w_down", [w_down_f])

    h1, proj_part = in_proj_first(x2d, pre_mix_norm, w_in_ta)
    u, vs, q, k, v = in_proj(h1, proj_part, pos_col, w_in_tb, rot)
    attn_r, lse, attn = attn_fwd(q, k, v)
    (sgu,) = sgu_fwd(u, vs, sgu_ln_gain, sgu_ln_bias, w_sp, bfull)
    mix, y, x2, h2 = out_proj(attn, sgu, x2d, attn_out_norm, sgu_out_norm, w_out_f, post_mix_norm, pre_ffn_norm)
    gate, up, act = ffn_up(h2, w_gate_t, w_up_t)
    df, dx3, d_post_ffn, sq_err = ffn_down_loss(act, w_down_f, x2, post_ffn_norm, target)

    g_w_down = weight_grad("grad_w_down", act, df)
    (s_down,) = from_sibling("w_down_to_sibling", [g_w_down])
    dgate, dup, dx2, dy, d_pre_ffn, d_post_mix = ffn_bwd(
        df, w_down_f, gate, up, w_gate_t, w_up_t, x2, pre_ffn_norm, dx3, y, post_mix_norm, after=[g_w_down])
    (p_down,) = sum_cores("sum_cores_down", [g_w_down], [s_down], core, after=[dy])
    (c_down,) = from_chips("w_down_to_chips", [p_down])
    g_w_gate, g_w_up = weight_grads("grad_w_gate_up", [dgate, dup], h2, after=[p_down])
    s_gate, s_up = from_sibling("w_gate_up_to_sibling", [g_w_gate, g_w_up])
    g_w_out = weight_grad("grad_w_out", mix, dy, after=[g_w_up])
    (s_out,) = from_sibling("w_out_to_sibling", [g_w_out])
    dsgu, d_attn_out, d_sgu_out, dattn_r = mix_bwd(dy, w_out_f, attn, sgu, attn_out_norm, sgu_out_norm, after=[g_w_out, c_down])
    p_gate, p_up = sum_cores("sum_cores_gate_up", [g_w_gate, g_w_up], [s_gate, s_up], core, after=[dsgu])
    c_gate, c_up = from_chips("w_gate_up_to_chips", [p_gate, p_up])
    du, dvs, d_ln_gain, d_ln_bias, d_w_sp, d_bfull = sgu_bwd(
        u, vs, dsgu, sgu_ln_gain, sgu_ln_bias, w_sp, bfull, after=[p_gate, p_up])

    d_b_sp = d_bfull.reshape(CHUNK, N_GROUPS, HEAD_DIM).sum(axis=-1).T
    small_g = pack_small(dict(sgu_ln_gain=d_ln_gain, sgu_ln_bias=d_ln_bias, sgu_w_spatial=d_w_sp, sgu_b_spatial=d_b_sp,
                              attn_out_norm=d_attn_out, sgu_out_norm=d_sgu_out, post_mix_norm=d_post_mix,
                              pre_ffn_norm=d_pre_ffn, post_ffn_norm=d_post_ffn, loss_sum=sq_err))
    small_g = small_g.reshape(N_DEV, SMALL_ROWS // N_DEV, 128)
    ((_, (o_small,)),) = by_sequencer("small_to_owners", [to_owners(small_g)], TO_ALL)

    dq, dk, dv = attn_bwd(q, k, v, attn_r, lse, dattn_r, _to_residue_order(pos_col), rot)
    summed_small = sum_owned("sum_small", small_g, o_small, dev_ids, after=[dq, c_gate, c_up])
    (all_small,) = gathered("gather_small_grads", [summed_small])
    (p_out,) = sum_cores("sum_cores_out", [g_w_out], [s_out], core, after=[dq])
    (c_out,) = from_chips("w_out_to_chips", [p_out])
    dproj = [dq, dk, dv, du, dvs]
    g_w_in = weight_grad_of_parts("grad_w_in", dproj, h1, after=[c_gate, c_up])
    (s_in,) = from_sibling("w_in_to_sibling", [g_w_in])

    def same(t):
        return t

    def turned(t):
        return t.T

    big = {}

    def adamw(call, weights, after):
        results = adamw_of_sums(call, [p for _, _, p, _, _, _, _ in weights], [c for _, _, _, c, _, _, _ in weights], chip_ids,
                                [turn(w[0]) for _, w, _, _, _, _, turn in weights], [turn(m[0]) for _, _, _, _, m, _, turn in weights],
                                [turn(vv[0]) for _, _, _, _, _, vv, turn in weights], after)
        for (name, _, _, _, _, _, turn), outs in zip(weights, results):
            big[name] = tuple(turn(t)[None] for t in outs)
        return results[-1][0]

    done_ffn = adamw("adamw_ffn", (("w_down", w_down, p_down, c_down, m_w_down, v_w_down, same),
                                   ("w_gate", w_gate, p_gate, c_gate, m_w_gate, v_w_gate, turned),
                                   ("w_up", w_up, p_up, c_up, m_w_up, v_w_up, turned)), [g_w_in, all_small])
    (p_in,) = sum_cores("sum_cores_in", [g_w_in], [s_in], core, after=[done_ffn, c_out])
    (c_in,) = from_chips("w_in_to_chips", [p_in])
    grad_x, d_pre_mix = in_bwd(dproj, [w_in_ta, w_in_tb], x2d, pre_mix_norm, dx2, after=[p_in])
    ((_, (late_parts,)),) = by_sequencer("pre_mix_to_everyone", [to_everyone(d_pre_mix)], TO_ALL)
    done_out = adamw("adamw_w_out", (("w_out", w_out, p_out, c_out, m_w_out, v_w_out, same),), [d_pre_mix])
    done_in = adamw("adamw_w_in", (("w_in", w_in, p_in, c_in, m_w_in, v_w_in, turned),), [done_out])
    sg, sd, snm, snv, loss_sum = adamw_small(
        all_small, late_parts, d_pre_mix, dev.reshape(1), small_w, small_m, small_v, after=[done_in])
    loss = loss_sum[0, 0] * np.float32(0.5 / D_MODEL)

    names = ["pre_mix_norm", "w_in", "sgu_ln_gain", "sgu_ln_bias", "sgu_w_spatial", "sgu_b_spatial", "attn_out_norm",
             "sgu_out_norm", "w_out", "post_mix_norm", "pre_ffn_norm", "w_gate", "w_up", "w_down", "post_ffn_norm"]
    outs = [loss, grad_x[None]]
    for i, table in enumerate((sg, sd, snm, snv)):
        for name in names:
            outs.append(big[name][i] if name in big else table[name])
    return tuple(outs)
```

```python
import numpy as np
import jax
import jax.numpy as jnp
from jax import lax
from jax.experimental import pallas as pl
from jax.experimental.pallas import tpu as pltpu
from jax.experimental.pallas import tpu_sc as plsc

F32 = jnp.float32
BF16 = jnp.bfloat16

SEQ = 2048
D_MODEL = 1024
ATTN_W = 512
SGU_W = 512
HEAD_DIM = 64
N_GROUPS = 8
CHUNK = 128
D_FF = 2816
IN_W = 3 * ATTN_W + 2 * SGU_W
DILATIONS = (1, 4, 16)
ROPE_THETA = 500000.0
ROT_DIM = 16
ROT_HALF = 8
RMS_EPS = 1e-6
LN_EPS = 1e-5
Q_SCALE = 0.125
NEG = -1e30

N_DEV = 8
MESH_AXES = ("x", "y", "c")
MESH = pl.DeviceIdType.MESH

ADAM_LR = 0.001
ADAM_B1 = 0.9
ADAM_B2 = 0.999
ADAM_EPS = 1e-08
ADAM_WD = 0.01
ADAM_STEP = 10

VMEM_LIMIT = 60 * 1024 * 1024
ANY = pl.BlockSpec(memory_space=pl.ANY)

SMALL = (("pre_mix_norm", 1024), ("sgu_ln_gain", 512), ("sgu_ln_bias", 512), ("sgu_w_spatial", 8 * 128 * 128),
         ("sgu_b_spatial", 1024), ("attn_out_norm", 512), ("sgu_out_norm", 512), ("post_mix_norm", 1024),
         ("pre_ffn_norm", 1024), ("post_ffn_norm", 1024), ("loss_sum", 1))
SMALL_ROWS = 1152


def _params(sem=("arbitrary",)):
    return pltpu.CompilerParams(dimension_semantics=sem, vmem_limit_bytes=VMEM_LIMIT)


def _dot(a, b):
    return jnp.dot(a, b, preferred_element_type=F32)


def _dot_nt(a, b):
    return lax.dot_general(a, b, (((1,), (1,)), ((), ())), preferred_element_type=F32)


def _dot_tn(a, b):
    return lax.dot_general(a, b, (((0,), (0,)), ((), ())), preferred_element_type=F32)


def _rms(z):
    return lax.rsqrt(jnp.mean(z * z, axis=-1, keepdims=True) + RMS_EPS)


def _rms_bwd(z, gain, d):
    r = _rms(z)
    n = z * r
    dn = d * gain
    dz = r * (dn - n * jnp.mean(dn * n, axis=-1, keepdims=True))
    return dz, jnp.sum(d * n, axis=0, keepdims=True)


def _gelu(z):
    return 0.5 * z * (1.0 + lax.erf(z * np.float32(1.0 / np.sqrt(2.0))))


def _gelu_grad(z):
    cdf = 0.5 * (1.0 + lax.erf(z * np.float32(1.0 / np.sqrt(2.0))))
    return cdf + z * jnp.exp(-0.5 * z * z) * np.float32(1.0 / np.sqrt(2.0 * np.pi))


def _rot_tables(pos_col, invf, ma, mb):
    ang = pos_col.astype(F32) * invf
    s = jnp.sin(ang)
    return jnp.cos(ang), s * ma, s * mb


def _rot(t, c, sa, sb):
    return t * c + pltpu.roll(t, 120, 1) * sa + pltpu.roll(t, 8, 1) * sb


def _rot_t(d, c, sa, sb):
    return d * c + pltpu.roll(d * sa, 8, 1) + pltpu.roll(d * sb, 120, 1)


def _rot_consts():
    lane = np.arange(128) % HEAD_DIM
    inv_freq = (np.float32(ROPE_THETA) ** (-np.arange(0, ROT_DIM, 2, dtype=np.float32) / np.float32(ROT_DIM))).astype(np.float32)
    invf = np.where(lane < ROT_DIM, inv_freq[lane % ROT_HALF], 0.0).astype(np.float32)
    ma = np.where(lane < ROT_HALF, -1.0, 0.0).astype(np.float32)
    mb = np.where((lane >= ROT_HALF) & (lane < ROT_DIM), 1.0, 0.0).astype(np.float32)
    return jnp.asarray(invf[None]), jnp.asarray(ma[None]), jnp.asarray(mb[None])


def _row_spec(tm, w):
    return pl.BlockSpec((tm, w), lambda i: (i, 0))


def _full_spec(shape):
    return pl.BlockSpec(shape, lambda i: (0,) * len(shape))


def _weight_spec(shape):
    return pl.BlockSpec(shape, lambda i: (0,) * len(shape), pipeline_mode=pl.Buffered(1))


FF_CHUNKS = (256, 512, 1024, 1024)
FF_SPANS = [(int(o), n) for o, n in zip(np.cumsum((0,) + FF_CHUNKS[:-1]), FF_CHUNKS)]
FF_WHOLE = [(0, D_FF)]


def _ffn_weight_scratch(n_weights, spans):
    return [pltpu.VMEM((D_FF, D_MODEL), BF16)] * n_weights + [pltpu.SemaphoreType.DMA((n_weights, len(spans)))]


def _with_ffn_weights(w_hbm, w_vmem, sems, spans, run):
    copies = [[pltpu.make_async_copy(h.at[pl.ds(o, n)], v.at[pl.ds(o, n)], sems.at[j, c]) for c, (o, n) in enumerate(spans)]
              for j, (h, v) in enumerate(zip(w_hbm, w_vmem))]
    first = pl.program_id(0) == 0

    @pl.when(first)
    def _():
        for of_weight in copies:
            for cp in of_weight:
                cp.start()
        run(lambda j, c: copies[j][c].wait())

    @pl.when(jnp.logical_not(first))
    def _():
        run(None)


RES = 16


def _residue_scratch(n_arrays, tm, width):
    return [pltpu.VMEM((2, n_arrays, tm // RES, RES, width), F32), pltpu.SemaphoreType.DMA((2, n_arrays, RES))]


def _to_residue_rows(tiles, outs, scratch, sems, tm, n_steps):
    i = pl.program_id(0)
    slot = i % 2
    per = tm // RES

    def copies(step, s):
        return [pltpu.make_async_copy(scratch.at[s, a, :, b, :],
                                      outs[a].at[pl.ds(pl.multiple_of(b * (SEQ // RES) + per * step, per), per), :],
                                      sems.at[s, a, b]) for a in range(len(outs)) for b in range(RES)]

    @pl.when(i >= 2)
    def _():
        for cp in copies(i - 2, slot):
            cp.wait()

    for a, tile in enumerate(tiles):
        scratch[slot, a] = tile.reshape(per, RES, tile.shape[-1])
    for cp in copies(i, slot):
        cp.start()

    @pl.when(i == n_steps - 1)
    def _():
        for cp in copies(i - 1, 1 - slot) + copies(i, slot):
            cp.wait()


def in_proj(x, pos_col, g1, w_in_t, rot):
    tm = 512
    n_steps = SEQ // tm

    def body(x_ref, pos_ref, g_ref, w_ref, invf_ref, ma_ref, mb_ref, h_ref, u_ref, vs_ref, q_ref, k_ref, v_ref, scratch, sems):
        xf = x_ref[...]
        h = (xf * _rms(xf) * g_ref[...]).astype(BF16)
        h_ref[...] = h
        proj = _dot_nt(h, w_ref[...])
        c, sa, sb = _rot_tables(pos_ref[...], invf_ref[...], ma_ref[...], mb_ref[...])
        slabs = range(ATTN_W // 128)
        q = jnp.concatenate([_rot(proj[:, j * 128:(j + 1) * 128], c, sa, sb) * Q_SCALE for j in slabs], axis=1)
        k = jnp.concatenate([_rot(proj[:, ATTN_W + j * 128:ATTN_W + (j + 1) * 128], c, sa, sb) for j in slabs], axis=1)
        u_ref[...] = proj[:, 3 * ATTN_W:3 * ATTN_W + SGU_W]
        vs_ref[...] = proj[:, 3 * ATTN_W + SGU_W:]
        _to_residue_rows([q, k, proj[:, 2 * ATTN_W:3 * ATTN_W]], [q_ref, k_ref, v_ref], scratch, sems, tm, n_steps)

    act = jax.ShapeDtypeStruct((SEQ, 512), F32)
    return _call(
        "in_proj", body, n_steps,
        [_row_spec(tm, D_MODEL), _row_spec(tm, 1), _full_spec((1, D_MODEL)), _weight_spec((IN_W, D_MODEL)),
         _full_spec((1, 128)), _full_spec((1, 128)), _full_spec((1, 128))],
        [_row_spec(tm, D_MODEL)] + [_row_spec(tm, 512)] * 2 + [ANY] * 3,
        [jax.ShapeDtypeStruct((SEQ, D_MODEL), BF16)] + [act] * 5,
        (x, pos_col, g1, w_in_t, *rot), scratch_shapes=_residue_scratch(3, tm, ATTN_W))


def _to_residue_order(t):
    return t.reshape(SEQ // RES, RES, -1).transpose(1, 0, 2).reshape(t.shape)


def _block_rows(d, r, n):
    if d == 16:
        slices = [(128 * r, 128)]
    elif d == 4:
        slices = [(128 * (4 * b + r) + 32 * n, 32) for b in range(4)]
    else:
        slices = [(128 * b + 8 * n, 8) for b in range(RES)]
    return [(s if isinstance(s, int) else pl.multiple_of(s, z), z) for s, z in slices]


def _block_step(d, i):
    if d == 16:
        return i
    if d == 4:
        return 4 * (i & 31) + (i >> 5)
    return 16 * (i & 7) + (i >> 3)


def _attn_masks(d):
    row2 = _block_step(d, lax.broadcasted_iota(jnp.int32, (128, 256), 0))
    col2 = lax.broadcasted_iota(jnp.int32, (128, 256), 1)
    key2 = _block_step(d, col2 & 127)
    mask2 = jnp.logical_or(jnp.logical_and(col2 < 128, key2 >= row2), jnp.logical_and(col2 >= 128, key2 <= row2))
    row1 = _block_step(d, lax.broadcasted_iota(jnp.int32, (128, 128), 0))
    col1 = lax.broadcasted_iota(jnp.int32, (128, 128), 1)
    return col1 < HEAD_DIM, _block_step(d, col1) <= row1, mask2


def _load_rows(ref, slices):
    parts = [ref[pl.ds(s, z), :] for s, z in slices]
    return parts[0] if len(parts) == 1 else jnp.concatenate(parts, axis=0)


def _for_each_group(fn):
    for p, d in enumerate(DILATIONS):
        masks = _attn_masks(d)
        if d == 16:
            def group(i, carry, p=p, masks=masks):
                fn(p, masks, [(_block_rows(16, 8 * i + g, 0), None) for g in range(8)])
                return carry

            lax.fori_loop(0, 2, group, 0)
        elif d == 4:
            fn(p, masks, [(_block_rows(4, r, 0), None) for r in range(4)])

            def group(i, carry, p=p, masks=masks):
                blocks = [6 * i + g for g in range(6)]
                fn(p, masks, [(_block_rows(4, j % 4, 1 + j // 4), _block_rows(4, j % 4, j // 4)) for j in blocks])
                return carry

            lax.fori_loop(0, 2, group, 0)
        else:
            fn(p, masks, [(_block_rows(1, 0, 0), None)])

            def group(i, carry, p=p, masks=masks):
                fn(p, masks, [(_block_rows(1, 0, 5 * i + g + 1), _block_rows(1, 0, 5 * i + g)) for g in range(5)])
                return carry

            lax.fori_loop(0, 3, group, 0)


def attn_fwd(q, k, v):
    def body(q_ref, k_ref, v_ref, o_ref, lse_ref, nat_ref, op_ref, lp_ref, sems):
        def group(p, masks, blocks):
            head0, mask1, mask2 = masks
            heads = (head0, jnp.logical_not(head0))
            keys = [rows if prev is None else prev + rows for rows, prev in blocks]
            mask = [mask1 if prev is None else mask2 for _, prev in blocks]
            qb = [_load_rows(q_ref, rows) for rows, _ in blocks]
            kk = [_load_rows(k_ref, ks).astype(BF16) for ks in keys]
            vv = [_load_rows(v_ref, ks).astype(BF16) for ks in keys]
            chains = [(g, hm) for g in range(len(blocks)) for hm in heads]
            s = [jnp.where(mask[g], _dot_nt(jnp.where(hm, qb[g], 0.0).astype(BF16), kk[g]), NEG) for g, hm in chains]
            m = [jnp.max(t, axis=-1, keepdims=True) for t in s]
            e = [jnp.exp(t - mt) for t, mt in zip(s, m)]
            l = [jnp.sum(t, axis=-1, keepdims=True) for t in e]
            pv = [_dot(t.astype(BF16), vv[g]) for t, (g, _) in zip(e, chains)]
            for g, (rows, _) in enumerate(blocks):
                o_blk = jnp.where(head0, pv[2 * g] / l[2 * g], pv[2 * g + 1] / l[2 * g + 1])
                l_blk = jnp.where(head0, jnp.broadcast_to(m[2 * g] + jnp.log(l[2 * g]), (128, 128)),
                                  jnp.broadcast_to(m[2 * g + 1] + jnp.log(l[2 * g + 1]), (128, 128)))
                at = 0
                for start, size in rows:
                    op_ref[p, pl.ds(start, size), :] = o_blk[at:at + size]
                    lp_ref[p, pl.ds(start, size), :] = l_blk[at:at + size]
                    at += size

        _for_each_group(group)

        def combine(i, carry):
            rows = pl.ds(pl.multiple_of(i * 256, 256), 256)
            ls = [lp_ref[p, rows, :] for p in range(3)]
            m = jnp.maximum(jnp.maximum(ls[0], ls[1]), ls[2])
            lse = m + jnp.log(jnp.exp(ls[0] - m) + jnp.exp(ls[1] - m) + jnp.exp(ls[2] - m))
            o = jnp.zeros((256, 128), F32)
            for p in range(3):
                o = o + jnp.exp(ls[p] - lse) * op_ref[p, rows, :]
            o_ref[rows, :] = o
            lse_ref[rows, :] = lse
            return carry

        lax.fori_loop(0, SEQ // 256, combine, 0)

        lanes = pl.ds(pl.multiple_of(pl.program_id(0) * 128, 128), 128)
        back = [pltpu.make_async_copy(o_ref.at[pl.ds(b * (SEQ // RES), SEQ // RES), :], nat_ref.at[:, b, lanes], sems.at[b])
                for b in range(RES)]
        for cp in back:
            cp.start()
        for cp in back:
            cp.wait()

    slab = pl.BlockSpec((SEQ, 128), lambda i: (0, i))
    out = jax.ShapeDtypeStruct((SEQ, ATTN_W), F32)
    attn_r, lse, attn = _call(
        "attn_fwd", body, ATTN_W // 128, [slab] * 3, [slab] * 2 + [ANY],
        [out, out, jax.ShapeDtypeStruct((SEQ // RES, RES, ATTN_W), F32)], (q, k, v),
        scratch_shapes=[pltpu.VMEM((3, SEQ, 128), F32), pltpu.VMEM((3, SEQ, 128), F32), pltpu.SemaphoreType.DMA((RES,))])
    return attn_r, lse, attn.reshape(SEQ, ATTN_W)


def _causal_weights(w_ref):
    row = lax.broadcasted_iota(jnp.int32, (CHUNK, CHUNK), 0)
    col = lax.broadcasted_iota(jnp.int32, (CHUNK, CHUNK), 1)
    return [jnp.where(col <= row, w_ref[g], 0.0).astype(BF16) for g in range(N_GROUPS)], col <= row


def _sgu_chunk_fwd(u, vs, lg, lb, wc, bfull, head0):
    ug = _gelu(u)
    vg = _gelu(vs)
    xc = vg - jnp.mean(vg, axis=-1, keepdims=True)
    rstd = lax.rsqrt(jnp.mean(xc * xc, axis=-1, keepdims=True) + LN_EPS)
    xhat = xc * rstd
    vn = xhat * lg + lb
    mixed = []
    for gp in range(SGU_W // 128):
        vp = vn[:, gp * 128:(gp + 1) * 128].astype(BF16)
        mixed.append(jnp.where(head0, _dot(wc[2 * gp], vp), _dot(wc[2 * gp + 1], vp)))
    ms = jnp.concatenate(mixed, axis=1) + bfull
    return ug, xhat, rstd, vn, ms


def sgu_fwd(u, vs, lg, lb, w_sp, bfull):
    cpb = 4

    def body(u_ref, vs_ref, lg_ref, lb_ref, w_ref, b_ref, o_ref):
        wc, _ = _causal_weights(w_ref)
        head0 = lax.broadcasted_iota(jnp.int32, (CHUNK, 128), 1) < HEAD_DIM
        for ci in range(cpb):
            rows = pl.ds(ci * CHUNK, CHUNK)
            ug, _, _, _, ms = _sgu_chunk_fwd(u_ref[rows, :], vs_ref[rows, :], lg_ref[...], lb_ref[...], wc, b_ref[...], head0)
            o_ref[rows, :] = ug * ms

    tm = cpb * CHUNK
    return _call(
        "sgu_fwd", body, SEQ // tm,
        [_row_spec(tm, SGU_W), _row_spec(tm, SGU_W), _full_spec((1, SGU_W)), _full_spec((1, SGU_W)),
         _full_spec((N_GROUPS, CHUNK, CHUNK)), _full_spec((CHUNK, SGU_W))],
        [_row_spec(tm, SGU_W)], [jax.ShapeDtypeStruct((SEQ, SGU_W), F32)],
        (u, vs, lg, lb, w_sp, bfull))


def out_proj(attn, sgu, x, ga, gs, w_out, gpm, gpf):
    tm = 512

    def body(a_ref, s_ref, x_ref, ga_ref, gs_ref, w_ref, gpm_ref, gpf_ref, mix_ref, y_ref, x2_ref, h2_ref):
        a = a_ref[...]
        s = s_ref[...]
        an = (a * _rms(a) * ga_ref[...]).astype(BF16)
        sn = (s * _rms(s) * gs_ref[...]).astype(BF16)
        mix_ref[:, :ATTN_W] = an
        mix_ref[:, ATTN_W:] = sn
        y = _dot(an, w_ref[:ATTN_W, :]) + _dot(sn, w_ref[ATTN_W:, :])
        y_ref[...] = y
        x2 = x_ref[...] + y * _rms(y) * gpm_ref[...]
        x2_ref[...] = x2
        h2_ref[...] = (x2 * _rms(x2) * gpf_ref[...]).astype(BF16)

    wide = jax.ShapeDtypeStruct((SEQ, D_MODEL), F32)
    wide16 = jax.ShapeDtypeStruct((SEQ, D_MODEL), BF16)
    return _call(
        "out_proj", body, SEQ // tm,
        [_row_spec(tm, ATTN_W), _row_spec(tm, SGU_W), _row_spec(tm, D_MODEL), _full_spec((1, ATTN_W)),
         _full_spec((1, SGU_W)), _weight_spec((D_MODEL, D_MODEL)), _full_spec((1, D_MODEL)), _full_spec((1, D_MODEL))],
        [_row_spec(tm, D_MODEL)] * 4, [wide16, wide, wide, wide16],
        (attn, sgu, x, ga, gs, w_out, gpm, gpf))


def ffn_up(h2, w_gate_t, w_up_t):
    tm = 256

    def body(h_ref, wg_hbm, wu_hbm, g_ref, u_ref, a_ref, wg_ref, wu_ref, sems):
        def run(wait):
            h = h_ref[...]
            if wait:
                wait(0, 0)
            g = _dot_nt(h, wg_ref[...])
            g_ref[...] = g.astype(BF16)
            if wait:
                wait(1, 0)
            u = _dot_nt(h, wu_ref[...])
            u_ref[...] = u.astype(BF16)
            a_ref[...] = (g * jax.nn.sigmoid(g) * u).astype(BF16)

        _with_ffn_weights([wg_hbm, wu_hbm], [wg_ref, wu_ref], sems, FF_WHOLE, run)

    ff = jax.ShapeDtypeStruct((SEQ, D_FF), BF16)
    return _call(
        "ffn_up", body, SEQ // tm, [_row_spec(tm, D_MODEL), ANY, ANY],
        [_row_spec(tm, D_FF)] * 3, [ff, ff, jax.ShapeDtypeStruct((SEQ, D_FF), BF16)],
        (h2, w_gate_t, w_up_t), scratch_shapes=_ffn_weight_scratch(2, FF_WHOLE))


def ffn_down_loss(act, w_down, x2, gpo, target):
    tm = 512

    def body(a_ref, w_hbm, x2_ref, g_ref, t_ref, df_ref, dx3_ref, dg_ref, loss_ref, w_ref, sems):
        def run(wait):
            f = None
            for c, (o, n) in enumerate(FF_SPANS if wait else FF_WHOLE):
                if wait:
                    wait(0, c)
                part = _dot(a_ref[:, o:o + n], w_ref[o:o + n, :])
                f = part if f is None else f + part
            gain = g_ref[...]
            err = x2_ref[...] + f * _rms(f) * gain - t_ref[...]
            dx3 = err * np.float32(1.0 / D_MODEL)
            dx3_ref[...] = dx3
            df, dg = _rms_bwd(f, gain, dx3)
            df_ref[...] = df.astype(BF16)
            loss = jnp.sum(err * err, axis=(0, 1), keepdims=True)
            if wait:
                dg_ref[...] = dg
                loss_ref[...] = loss
            else:
                dg_ref[...] += dg
                loss_ref[...] += loss

        _with_ffn_weights([w_hbm], [w_ref], sems, FF_SPANS, run)

    return _call(
        "ffn_down_loss", body, SEQ // tm,
        [_row_spec(tm, D_FF), ANY, _row_spec(tm, D_MODEL), _full_spec((1, D_MODEL)), _row_spec(tm, D_MODEL)],
        [_row_spec(tm, D_MODEL), _row_spec(tm, D_MODEL), _full_spec((1, D_MODEL)), _full_spec((1, 1))],
        [jax.ShapeDtypeStruct((SEQ, D_MODEL), BF16), jax.ShapeDtypeStruct((SEQ, D_MODEL), F32),
         jax.ShapeDtypeStruct((1, D_MODEL), F32), jax.ShapeDtypeStruct((1, 1), F32)],
        (act, w_down, x2, gpo, target), scratch_shapes=_ffn_weight_scratch(1, FF_SPANS))


def ffn_bwd(df, w_down, gate, up, w_gate_t, w_up_t, x2, gpf, dx3, y, gpm, after=()):
    tm = 256

    def body(df_ref, wd_hbm, g_ref, u_ref, wg_hbm, wu_hbm, x2_ref, gpf_ref, dx3_ref, y_ref, gpm_ref,
             dg_ref, du_ref, dx2_ref, dy_ref, dgpf_ref, dgpm_ref, wd_ref, wg_ref, wu_ref, sems):
        def run(wait):
            if wait:
                wait(0, 0)
            dact = _dot_nt(df_ref[...], wd_ref[...])
            g = g_ref[...].astype(F32)
            s = jax.nn.sigmoid(g)
            dup = (dact * g * s).astype(BF16)
            dgate = (dact * u_ref[...].astype(F32) * (s * (1.0 + g * (1.0 - s)))).astype(BF16)
            du_ref[...] = dup
            dg_ref[...] = dgate
            if wait:
                wait(1, 0)
                wait(2, 0)
            dh2 = _dot(dgate, wg_ref[...]) + _dot(dup, wu_ref[...])
            dz, dgpf = _rms_bwd(x2_ref[...], gpf_ref[...], dh2)
            dx2 = dx3_ref[...] + dz
            dx2_ref[...] = dx2
            dy, dgpm = _rms_bwd(y_ref[...], gpm_ref[...], dx2)
            dy_ref[...] = dy.astype(BF16)
            if wait:
                dgpf_ref[...] = dgpf
                dgpm_ref[...] = dgpm
            else:
                dgpf_ref[...] += dgpf
                dgpm_ref[...] += dgpm

        _with_ffn_weights([wd_hbm, wg_hbm, wu_hbm], [wd_ref, wg_ref, wu_ref], sems, FF_WHOLE, run)

    vec = jax.ShapeDtypeStruct((1, D_MODEL), F32)
    ff16 = jax.ShapeDtypeStruct((SEQ, D_FF), BF16)
    return _call(
        "ffn_bwd", body, SEQ // tm,
        [_row_spec(tm, D_MODEL), ANY, _row_spec(tm, D_FF), _row_spec(tm, D_FF), ANY, ANY, _row_spec(tm, D_MODEL),
         _full_spec((1, D_MODEL)), _row_spec(tm, D_MODEL), _row_spec(tm, D_MODEL), _full_spec((1, D_MODEL))],
        [_row_spec(tm, D_FF), _row_spec(tm, D_FF), _row_spec(tm, D_MODEL), _row_spec(tm, D_MODEL),
         _full_spec((1, D_MODEL)), _full_spec((1, D_MODEL))],
        [ff16, ff16, jax.ShapeDtypeStruct((SEQ, D_MODEL), F32), jax.ShapeDtypeStruct((SEQ, D_MODEL), BF16), vec, vec],
        (df, w_down, gate, up, w_gate_t, w_up_t, x2, gpf, dx3, y, gpm), scratch_shapes=_ffn_weight_scratch(3, FF_WHOLE), after=after)


def weight_grads(name, lhs, b, after=()):
    m, n, k = lhs[0].shape[1], b.shape[1], len(lhs)
    tr = 256

    def body(*refs):
        for a_ref, o_ref in zip(refs[:k], refs[k + 1:]):
            o_ref[...] = _dot_tn(a_ref[...], refs[k][...]).astype(BF16)

    outs = _call(
        name, body, m // tr, [pl.BlockSpec((SEQ, tr), lambda i: (0, i))] * k + [_weight_spec((SEQ, n))],
        [_row_spec(tr, n)] * k, [jax.ShapeDtypeStruct((m, n), BF16)] * k, (*lhs, b), after=after)
    return [out.reshape(N_DEV, m // N_DEV, n) for out in outs]


def weight_grad(name, a, b, after=()):
    return weight_grads(name, [a], b, after)[0]


def weight_grad_of_parts(name, parts, b, after=()):
    p, n, k = parts[0].shape[1], b.shape[1], len(parts)
    tr = 512
    per = p // tr

    def body(*refs):
        tile = pl.program_id(0)
        for j in range(k):
            @pl.when(tile // per == j)
            def _(j=j):
                refs[k + 1][...] = _dot_tn(refs[j][...].astype(BF16), refs[k][...]).astype(BF16)

    def part_spec(j):
        return pl.BlockSpec((SEQ, tr), lambda i: (0, jnp.clip(i - per * j, 0, per - 1)))

    (out,) = _call(
        name, body, k * per, [part_spec(j) for j in range(k)] + [_weight_spec((SEQ, n))],
        [_row_spec(tr, n)], [jax.ShapeDtypeStruct((k * p, n), BF16)], (*parts, b), after=after)
    return out.reshape(N_DEV, k * p // N_DEV, n)


def mix_bwd(dy, w_out, attn, sgu, ga, gs, after=()):
    tm = 512
    n_steps = SEQ // tm

    def body(dy_ref, w_ref, a_ref, s_ref, ga_ref, gs_ref, ds_ref, dga_ref, dgs_ref, da_ref, scratch, sems):
        dy = dy_ref[...]
        da, dga = _rms_bwd(a_ref[...], ga_ref[...], _dot_nt(dy, w_ref[:ATTN_W, :]))
        ds, dgs = _rms_bwd(s_ref[...], gs_ref[...], _dot_nt(dy, w_ref[ATTN_W:, :]))
        ds_ref[...] = ds
        _to_residue_rows([da], [da_ref], scratch, sems, tm, n_steps)

        @pl.when(pl.program_id(0) == 0)
        def _():
            dga_ref[...] = jnp.zeros_like(dga_ref)
            dgs_ref[...] = jnp.zeros_like(dgs_ref)

        dga_ref[...] += dga
        dgs_ref[...] += dgs

    half = jax.ShapeDtypeStruct((SEQ, 512), F32)
    vec = jax.ShapeDtypeStruct((1, 512), F32)
    return _call(
        "mix_bwd", body, n_steps,
        [_row_spec(tm, D_MODEL), _weight_spec((D_MODEL, D_MODEL)), _row_spec(tm, 512), _row_spec(tm, 512),
         _full_spec((1, 512)), _full_spec((1, 512))],
        [_row_spec(tm, 512), _full_spec((1, 512)), _full_spec((1, 512)), ANY],
        [half, vec, vec, half], (dy, w_out, attn, sgu, ga, gs), scratch_shapes=_residue_scratch(1, tm, ATTN_W), after=after)


def sgu_bwd(u, vs, dsgu, lg, lb, w_sp, bfull, after=()):
    cpb = 4

    def body(u_ref, vs_ref, d_ref, lg_ref, lb_ref, w_ref, b_ref, du_ref, dvs_ref, dlg_ref, dlb_ref, dw_ref, db_ref):
        wc, causal = _causal_weights(w_ref)
        head0 = lax.broadcasted_iota(jnp.int32, (CHUNK, 128), 1) < HEAD_DIM
        lg = lg_ref[...]

        @pl.when(pl.program_id(0) == 0)
        def _():
            dlg_ref[...] = jnp.zeros_like(dlg_ref)
            dlb_ref[...] = jnp.zeros_like(dlb_ref)
            dw_ref[...] = jnp.zeros_like(dw_ref)
            db_ref[...] = jnp.zeros_like(db_ref)

        for ci in range(cpb):
            rows = pl.ds(ci * CHUNK, CHUNK)
            u = u_ref[rows, :]
            vs = vs_ref[rows, :]
            d = d_ref[rows, :]
            ug, xhat, rstd, vn, ms = _sgu_chunk_fwd(u, vs, lg, lb_ref[...], wc, b_ref[...], head0)
            du_ref[rows, :] = (d * ms * _gelu_grad(u)).astype(BF16)
            dms = d * ug
            db_ref[...] += dms
            dvn = []
            for gp in range(SGU_W // 128):
                dmp = dms[:, gp * 128:(gp + 1) * 128]
                dm0 = jnp.where(head0, dmp, 0.0).astype(BF16)
                dm1 = jnp.where(head0, 0.0, dmp).astype(BF16)
                vp = vn[:, gp * 128:(gp + 1) * 128].astype(BF16)
                dw_ref[2 * gp] += _dot_nt(dm0, vp)
                dw_ref[2 * gp + 1] += _dot_nt(dm1, vp)
                dvn.append(_dot_tn(wc[2 * gp], dm0) + _dot_tn(wc[2 * gp + 1], dm1))
            dvn = jnp.concatenate(dvn, axis=1)
            dlg_ref[...] += jnp.sum(dvn * xhat, axis=0, keepdims=True)
            dlb_ref[...] += jnp.sum(dvn, axis=0, keepdims=True)
            dxh = dvn * lg
            dvg = rstd * (dxh - jnp.mean(dxh, axis=-1, keepdims=True) - xhat * jnp.mean(dxh * xhat, axis=-1, keepdims=True))
            dvs_ref[rows, :] = (dvg * _gelu_grad(vs)).astype(BF16)

        @pl.when(pl.program_id(0) == pl.num_programs(0) - 1)
        def _():
            for g in range(N_GROUPS):
                dw_ref[g] = jnp.where(causal, dw_ref[g], 0.0)

    tm = cpb * CHUNK
    half16 = jax.ShapeDtypeStruct((SEQ, SGU_W), BF16)
    vec = jax.ShapeDtypeStruct((1, SGU_W), F32)
    return _call(
        "sgu_bwd", body, SEQ // tm,
        [_row_spec(tm, SGU_W)] * 3 + [_full_spec((1, SGU_W)), _full_spec((1, SGU_W)),
                                      _full_spec((N_GROUPS, CHUNK, CHUNK)), _full_spec((CHUNK, SGU_W))],
        [_row_spec(tm, SGU_W), _row_spec(tm, SGU_W), _full_spec((1, SGU_W)), _full_spec((1, SGU_W)),
         _full_spec((N_GROUPS, CHUNK, CHUNK)), _full_spec((CHUNK, SGU_W))],
        [half16, half16, vec, vec, jax.ShapeDtypeStruct((N_GROUPS, CHUNK, CHUNK), F32),
         jax.ShapeDtypeStruct((CHUNK, SGU_W), F32)],
        (u, vs, dsgu, lg, lb, w_sp, bfull), after=after)


def attn_bwd(q, k, v, o, lse, do, pos_col, rot):
    n_steps = ATTN_W // 128

    def body(q_ref, k_ref, v_ref, o_ref, lse_ref, do_ref, pos_ref, invf_ref, ma_ref, mb_ref,
             dq_ref, dk_ref, dv_ref, dqa_ref, dka_ref, dva_ref, dlt_ref, rot_ref, out_ref, sems):
        step = pl.program_id(0)
        slot = step % 2

        def back(at_step, s):
            lanes = pl.ds(pl.multiple_of(at_step * 128, 128), 128)
            return [pltpu.make_async_copy(out_ref.at[s, a, pl.ds(b * (SEQ // RES), SEQ // RES), :], nat.at[:, b, lanes],
                                          sems.at[s, a, b]) for a, nat in enumerate((dq_ref, dk_ref, dv_ref)) for b in range(RES)]

        dqa_ref[...] = jnp.zeros_like(dqa_ref)
        dka_ref[...] = jnp.zeros_like(dka_ref)
        dva_ref[...] = jnp.zeros_like(dva_ref)

        def delta(i, carry):
            rows = pl.ds(pl.multiple_of(i * 256, 256), 256)
            prod = do_ref[rows, :] * o_ref[rows, :]
            h0 = lax.broadcasted_iota(jnp.int32, (256, 128), 1) < HEAD_DIM
            d0 = jnp.sum(jnp.where(h0, prod, 0.0), axis=-1, keepdims=True)
            d1 = jnp.sum(jnp.where(h0, 0.0, prod), axis=-1, keepdims=True)
            dlt_ref[rows, :] = jnp.where(h0, d0, d1)
            return carry

        lax.fori_loop(0, SEQ // 256, delta, 0)

        def add_rows(ref, slices, val):
            at = 0
            for start, size in slices:
                ref[pl.ds(start, size), :] += val[at:at + size]
                at += size

        def group(p, masks, blocks):
            head0, mask1, mask2 = masks
            heads = (head0, jnp.logical_not(head0))
            keys = [rows if prev is None else prev + rows for rows, prev in blocks]
            mask = [mask1 if prev is None else mask2 for _, prev in blocks]
            kk = [_load_rows(k_ref, ks).astype(BF16) for ks in keys]
            vv = [_load_rows(v_ref, ks).astype(BF16) for ks in keys]
            qb = [_load_rows(q_ref, rows) for rows, _ in blocks]
            dob = [_load_rows(do_ref, rows) for rows, _ in blocks]
            lse_b = [_load_rows(lse_ref, rows) for rows, _ in blocks]
            dlt_b = [_load_rows(dlt_ref, rows) for rows, _ in blocks]
            chains = [(g, h) for g in range(len(blocks)) for h in range(2)]
            qm = [jnp.where(heads[h], qb[g], 0.0).astype(BF16) for g, h in chains]
            dom = [jnp.where(heads[h], dob[g], 0.0).astype(BF16) for g, h in chains]
            s = [_dot_nt(qm[c], kk[g]) for c, (g, h) in enumerate(chains)]
            dp = [_dot_nt(dom[c], vv[g]) for c, (g, h) in enumerate(chains)]
            pr = [jnp.where(mask[g], jnp.exp(s[c] - lse_b[g][:, h * HEAD_DIM:h * HEAD_DIM + 1]), 0.0)
                  for c, (g, h) in enumerate(chains)]
            ds = [(pr[c] * (dp[c] - dlt_b[g][:, h * HEAD_DIM:h * HEAD_DIM + 1])).astype(BF16)
                  for c, (g, h) in enumerate(chains)]
            dv = [_dot_tn(pr[c].astype(BF16), dom[c]) for c in range(len(chains))]
            dk = [_dot_tn(ds[c], qm[c]) for c in range(len(chains))]
            dq = [_dot(ds[c], kk[g]) for c, (g, h) in enumerate(chains)]
            for g, (rows, _) in enumerate(blocks):
                add_rows(dqa_ref, rows, jnp.where(head0, dq[2 * g], dq[2 * g + 1]))
                add_rows(dka_ref, keys[g], dk[2 * g] + dk[2 * g + 1])
                add_rows(dva_ref, keys[g], dv[2 * g] + dv[2 * g + 1])

        _for_each_group(group)

        @pl.when(pl.program_id(0) == 0)
        def _():
            def tables(i, carry):
                rows = pl.ds(pl.multiple_of(i * 256, 256), 256)
                c, sa, sb = _rot_tables(pos_ref[rows, :], invf_ref[...], ma_ref[...], mb_ref[...])
                rot_ref[0, rows, :] = c
                rot_ref[1, rows, :] = sa
                rot_ref[2, rows, :] = sb
                return carry

            lax.fori_loop(0, SEQ // 256, tables, 0)

        @pl.when(step >= 2)
        def _():
            for cp in back(step - 2, slot):
                cp.wait()

        def finish(i, carry):
            rows = pl.ds(pl.multiple_of(i * 256, 256), 256)
            c, sa, sb = rot_ref[0, rows, :], rot_ref[1, rows, :], rot_ref[2, rows, :]
            out_ref[slot, 0, rows, :] = _rot_t(dqa_ref[rows, :] * Q_SCALE, c, sa, sb)
            out_ref[slot, 1, rows, :] = _rot_t(dka_ref[rows, :], c, sa, sb)
            out_ref[slot, 2, rows, :] = dva_ref[rows, :]
            return carry

        lax.fori_loop(0, SEQ // 256, finish, 0)
        for cp in back(step, slot):
            cp.start()

        @pl.when(step == n_steps - 1)
        def _():
            for cp in back(step - 1, 1 - slot) + back(step, slot):
                cp.wait()

    slab = pl.BlockSpec((SEQ, 128), lambda i: (0, i))
    out = jax.ShapeDtypeStruct((SEQ // RES, RES, ATTN_W), F32)
    acc = pltpu.VMEM((SEQ, 128), F32)
    outs = _call(
        "attn_bwd", body, n_steps,
        [slab] * 6 + [_full_spec((SEQ, 1)), _full_spec((1, 128)), _full_spec((1, 128)), _full_spec((1, 128))],
        [ANY] * 3, [out, out, out], (q, k, v, o, lse, do, pos_col, *rot),
        scratch_shapes=[acc, acc, acc, acc, pltpu.VMEM((3, SEQ, 128), F32), pltpu.VMEM((2, 3, SEQ, 128), F32),
                        pltpu.SemaphoreType.DMA((2, 3, RES))])
    return [t.reshape(SEQ, ATTN_W) for t in outs]


def in_bwd(dproj_parts, w_in_t, x, g1, dx2, after=()):
    tm = 512
    k = len(dproj_parts)

    def body(*refs):
        w_ref, x_ref, g_ref, dx2_ref, dx_ref, dg_ref = refs[k:]
        dh1 = _dot(refs[0][...].astype(BF16), w_ref[0:512, :])
        for j in range(1, k):
            dh1 = dh1 + _dot(refs[j][...].astype(BF16), w_ref[512 * j:512 * (j + 1), :])
        dz, dg = _rms_bwd(x_ref[...], g_ref[...], dh1)
        dx_ref[...] = dx2_ref[...] + dz

        @pl.when(pl.program_id(0) == 0)
        def _():
            dg_ref[...] = jnp.zeros_like(dg_ref)

        dg_ref[...] += dg

    return _call(
        "in_bwd", body, SEQ // tm,
        [_row_spec(tm, 512)] * k + [_weight_spec((IN_W, D_MODEL)), _row_spec(tm, D_MODEL), _full_spec((1, D_MODEL)),
                                    _row_spec(tm, D_MODEL)],
        [_row_spec(tm, D_MODEL), _full_spec((1, D_MODEL))],
        [jax.ShapeDtypeStruct((SEQ, D_MODEL), F32), jax.ShapeDtypeStruct((1, D_MODEL), F32)],
        (*dproj_parts, w_in_t, x, g1, dx2), after=after)


def _coords():
    return lax.axis_index("x"), lax.axis_index("y"), lax.axis_index("c")


class Exchange:
    def __init__(self, srcs, bufs, new_shapes, n_sems, make):
        self.srcs, self.bufs, self.new_shapes, self.n_sems, self.make = list(srcs), list(bufs), list(new_shapes), n_sems, make


def _call(name, body, n_steps, in_specs, out_specs, out_shape, args, scratch_shapes=(), after=()):
    n_in = len(args)

    def wrapped(*refs):
        body(*refs[:n_in], *refs[n_in + len(after):])

    return list(pl.pallas_call(
        wrapped, name=name, grid=(n_steps,), in_specs=list(in_specs) + [ANY] * len(after), out_specs=list(out_specs),
        out_shape=list(out_shape), scratch_shapes=list(scratch_shapes), compiler_params=_params(),
    )(*args, *after))


GATHER_SEMS = 8


def gather(bufs):
    n = len(bufs)

    def make(src_refs, buf_refs, new_refs, send_sems, recv_sems):
        x, y, c = _coords()
        me, sibling = (x, y, c), (x, y, 1 - c)
        over_x, over_y, across = (1 - x, y), (x, 1 - y), (1 - x, 1 - y)

        def copy(a, k, block, to, half=None):
            r = buf_refs[a].shape[0] // N_DEV
            lo, size = (0, r) if half is None else (half * (r // 2), r // 2)
            rows = buf_refs[a].at[pl.ds((4 * block[0] + 2 * block[1] + block[2]) * r + lo, size), :]
            return pltpu.make_async_remote_copy(
                src_ref=rows, dst_ref=rows, send_sem=send_sems.at[GATHER_SEMS * a + k],
                recv_sem=recv_sems.at[GATHER_SEMS * a + k], device_id=to, device_id_type=MESH)

        every = range(n)
        out = ([copy(a, 0, me, sibling) for a in every] + [copy(a, 1, me, (*over_x, c)) for a in every]
               + [copy(a, 2, me, (*over_y, c)) for a in every])
        near_in = [copy(a, 1, (*over_x, c), me) for a in every] + [copy(a, 2, (*over_y, c), me) for a in every]
        relay = ([copy(a, 3, (*over_x, c), (*over_y, c), half=0) for a in every]
                 + [copy(a, 4, (*over_y, c), (*over_x, c), half=1) for a in every])
        near_on = [copy(a, 5, (*over_x, c), sibling) for a in every] + [copy(a, 6, (*over_y, c), sibling) for a in every]
        relay_in = ([copy(a, 3, (*across, c), me, half=0) for a in every]
                    + [copy(a, 4, (*across, c), me, half=1) for a in every])
        far_on = [copy(a, 7, (*across, c), sibling) for a in every]
        from_core = ([copy(a, 0, sibling, me) for a in every] + [copy(a, 5, (*over_x, 1 - c), me) for a in every]
                     + [copy(a, 6, (*over_y, 1 - c), me) for a in every] + [copy(a, 7, (*across, 1 - c), me) for a in every])
        stages = [([], out), (near_in, relay + near_on), (relay_in, far_on)]
        return stages, out + relay + near_on + far_on, from_core

    return Exchange([], bufs, [], GATHER_SEMS * n, make)


TO_GATHER = (1, lambda x, y, c: [(x, y, 1 - c), (1 - x, y, c), (x, 1 - y, c)])
TO_SIBLING = (2, lambda x, y, c: [(x, y, 1 - c)])
TO_CHIPS = (3, lambda x, y, c: [(1 - x, y, c), (x, 1 - y, c), (1 - x, 1 - y, c)])
TO_ALL = (4, lambda x, y, c: [(x ^ (m >> 2), y ^ ((m >> 1) & 1), c ^ (m & 1)) for m in range(1, N_DEV)])


def by_sequencer(name, exchanges, who):
    collective_id, peers_of = who
    hbm = pltpu.MemorySpace.HBM
    refs = [([jax.new_ref(a, memory_space=hbm) for a in ex.srcs], [jax.new_ref(a, memory_space=hbm) for a in ex.bufs],
             [jax.empty_ref(s, memory_space=hbm) for s in ex.new_shapes]) for ex in exchanges]
    sems = []
    for ex in exchanges:
        sems += [pltpu.SemaphoreType.DMA((ex.n_sems,)), pltpu.SemaphoreType.DMA((ex.n_sems,))]

    @pl.kernel(mesh=plsc.ScalarSubcoreMesh(axis_name="sequencer", num_cores=1), name=name, scratch_types=tuple(sems),
               compiler_params=pltpu.CompilerParams(collective_id=collective_id))
    def launch(*sem_refs):
        peers = peers_of(*_coords())
        barrier = pltpu.get_barrier_semaphore()
        for peer in peers:
            pl.semaphore_signal(barrier, inc=1, device_id=peer, device_id_type=MESH)
        pl.semaphore_wait(barrier, len(peers))

        made = [ex.make(*refs[k], sem_refs[2 * k], sem_refs[2 * k + 1]) for k, ex in enumerate(exchanges)]
        for stage in range(max(len(stages) for stages, _, _ in made)):
            for stages, _, _ in made:
                if stage < len(stages):
                    arrivals, starts = stages[stage]
                    for cp in arrivals:
                        cp.wait_recv()
                    for cp in starts:
                        cp.start()
        for _, sends, arrivals in made:
            for cp in arrivals:
                cp.wait_recv()
            for cp in sends:
                cp.wait_send()

    launch()
    return [([ref[...] for ref in bufs], [ref[...] for ref in news]) for _, bufs, news in refs]


def place_shards(name, shards, dev):
    n = len(shards)

    def body(dev_ref, *refs):
        for a in range(n):
            refs[n + a][...] = refs[a][...].astype(BF16)

    spec = pltpu.PrefetchScalarGridSpec(
        num_scalar_prefetch=1, grid=(1,),
        in_specs=[pl.BlockSpec(s.shape, lambda i, dev_ref: (0, 0)) for s in shards],
        out_specs=[pl.BlockSpec(s.shape, lambda i, dev_ref: (dev_ref[0], 0)) for s in shards])
    return pl.pallas_call(
        body, name=name, grid_spec=spec,
        out_shape=[jax.ShapeDtypeStruct((N_DEV * s.shape[0], s.shape[1]), BF16) for s in shards],
        compiler_params=_params(),
    )(dev, *shards)


def _swap(copies_of):
    def make(src_refs, buf_refs, new_refs, send_sems, recv_sems):
        copies = copies_of(src_refs, new_refs, send_sems, recv_sems)
        return [([], copies)], copies, copies

    return make


def to_sibling(grads):
    def copies_of(src_refs, new_refs, send_sems, recv_sems):
        x, y, c = _coords()
        return [pltpu.make_async_remote_copy(
            src_ref=src_refs[a].at[2 * xy + 1 - c], dst_ref=new_refs[a].at[xy], send_sem=send_sems.at[4 * a + xy],
            recv_sem=recv_sems.at[4 * a + xy], device_id=(x, y, 1 - c), device_id_type=MESH)
            for a in range(len(src_refs)) for xy in range(4)]

    return Exchange(grads, [], [jax.ShapeDtypeStruct((4,) + g.shape[1:], g.dtype) for g in grads], 4 * len(grads),
                    _swap(copies_of))


def to_chips(parts):
    def copies_of(src_refs, new_refs, send_sems, recv_sems):
        x, y, c = _coords()
        chips = [(1 - x, y), (x, 1 - y), (1 - x, 1 - y)]
        return [pltpu.make_async_remote_copy(
            src_ref=src_refs[a].at[2 * px + py], dst_ref=new_refs[a].at[2 * x + y], send_sem=send_sems.at[3 * a + j],
            recv_sem=recv_sems.at[3 * a + j], device_id=(px, py, c), device_id_type=MESH)
            for a in range(len(src_refs)) for j, (px, py) in enumerate(chips)]

    return Exchange(parts, [], [jax.ShapeDtypeStruct(p.shape, p.dtype) for p in parts], 3 * len(parts), _swap(copies_of))


def to_owners(grad):
    def copies_of(src_refs, new_refs, send_sems, recv_sems):
        x, y, c = _coords()
        copies = []
        for m in range(1, N_DEV):
            px, py, pc = x ^ (m >> 2), y ^ ((m >> 1) & 1), c ^ (m & 1)
            copies.append(pltpu.make_async_remote_copy(
                src_ref=src_refs[0].at[4 * px + 2 * py + pc], dst_ref=new_refs[0].at[4 * x + 2 * y + c],
                send_sem=send_sems.at[m - 1], recv_sem=recv_sems.at[m - 1], device_id=(px, py, pc), device_id_type=MESH))
        return copies

    return Exchange([grad], [], [jax.ShapeDtypeStruct(grad.shape, grad.dtype)], N_DEV - 1, _swap(copies_of))


def to_everyone(vec):
    def copies_of(src_refs, new_refs, send_sems, recv_sems):
        x, y, c = _coords()
        copies = []
        for m in range(1, N_DEV):
            px, py, pc = x ^ (m >> 2), y ^ ((m >> 1) & 1), c ^ (m & 1)
            copies.append(pltpu.make_async_remote_copy(
                src_ref=src_refs[0], dst_ref=new_refs[0].at[4 * x + 2 * y + c],
                send_sem=send_sems.at[m - 1], recv_sem=recv_sems.at[m - 1], device_id=(px, py, pc), device_id_type=MESH))
        return copies

    return Exchange([vec], [], [jax.ShapeDtypeStruct((N_DEV,) + vec.shape, vec.dtype)], N_DEV - 1, _swap(copies_of))


def sum_cores(name, grads, others, core, after=()):
    k = len(grads)

    def body(core_ref, *refs):
        for j in range(k):
            out_ref = refs[2 * k + len(after) + j]
            out_ref[...] = (refs[j][:, 0].astype(F32) + refs[k + j][...].astype(F32)).astype(out_ref.dtype)

    mine = [pl.BlockSpec((2, 1) + o.shape[1:], lambda i, core_ref: (i, core_ref[0], 0, 0)) for o in others]
    theirs = [pl.BlockSpec((2,) + o.shape[1:], lambda i, core_ref: (i, 0, 0)) for o in others]
    return pl.pallas_call(
        body, name=name,
        grid_spec=pltpu.PrefetchScalarGridSpec(
            num_scalar_prefetch=1, grid=(2,), in_specs=mine + theirs + [ANY] * len(after), out_specs=theirs),
        out_shape=[jax.ShapeDtypeStruct(o.shape, o.dtype) for o in others],
        compiler_params=_params(),
    )(core, *[g.reshape((4, 2) + g.shape[1:]) for g in grads], *others, *after)


def sum_owned(name, grad, others, dev_ids, after=()):
    _, r, w = grad.shape

    def body(ids_ref, *refs):
        acc = refs[0][0]
        for k in range(1, N_DEV):
            acc = acc + refs[k][0]
        refs[-1][...] = acc

    def pick(k):
        return pl.BlockSpec((1, r, w), lambda i, ids_ref: (ids_ref[k], 0, 0))

    return pl.pallas_call(
        body, name=name,
        grid_spec=pltpu.PrefetchScalarGridSpec(
            num_scalar_prefetch=1, grid=(1,), in_specs=[pick(k) for k in range(N_DEV)] + [ANY] * len(after),
            out_specs=pl.BlockSpec((r, w), lambda i, ids_ref: (ids_ref[0], 0))),
        out_shape=jax.ShapeDtypeStruct((N_DEV * r, w), F32),
        compiler_params=_params(),
    )(dev_ids, grad, *([others] * (N_DEV - 1)), *after)


def _adamw_update(w, g, m, v):
    nm = ADAM_B1 * m + np.float32(1.0 - ADAM_B1) * g
    nv = ADAM_B2 * v + np.float32(1.0 - ADAM_B2) * (g * g)
    m_hat = nm / np.float32(1.0 - ADAM_B1 ** ADAM_STEP)
    v_hat = nv / np.float32(1.0 - ADAM_B2 ** ADAM_STEP)
    return -ADAM_LR * (m_hat / (jnp.sqrt(v_hat) + ADAM_EPS) + ADAM_WD * w), nm, nv


def adamw_of_sums(name, parts, others, chip_ids, ws, ms, vs, after):
    n = len(parts)
    halves = 2

    def body(ids_ref, *refs):
        outs = refs[7 * n + len(after):]
        for j in range(n):
            p_ref, a_ref, b_ref, c_ref, w_ref, m_ref, v_ref = refs[7 * j:7 * j + 7]
            g = ((p_ref[0].astype(F32) + a_ref[0].astype(F32)) + b_ref[0].astype(F32)) + c_ref[0].astype(F32)
            outs[4 * j][...] = g
            outs[4 * j + 1][...], outs[4 * j + 2][...], outs[4 * j + 3][...] = _adamw_update(w_ref[...], g, m_ref[...], v_ref[...])

    in_specs, out_specs, out_shape, operands = [], [], [], []
    for part, other, w, m, v in zip(parts, others, ws, ms, vs):
        _, r, wd = part.shape
        rows = r // halves
        whole = pl.BlockSpec((rows, wd), lambda i, ids_ref: (i, 0))
        in_specs += [pl.BlockSpec((1, rows, wd), lambda i, ids_ref, k=k: (ids_ref[k], i, 0)) for k in range(4)] + [whole] * 3
        out_specs += [whole] * 4
        out_shape += [jax.ShapeDtypeStruct((r, wd), F32)] * 4
        operands += [part, other, other, other, w, m, v]
    outs = pl.pallas_call(
        body, name=name,
        grid_spec=pltpu.PrefetchScalarGridSpec(
            num_scalar_prefetch=1, grid=(halves,), in_specs=in_specs + [ANY] * len(after), out_specs=out_specs),
        out_shape=out_shape,
        compiler_params=_params(),
    )(chip_ids, *operands, *after)
    return [tuple(outs[4 * j:4 * j + 4]) for j in range(n)]


def pack_small(parts):
    names = [name for name, _ in SMALL if name in parts]
    operands = [parts[name] for name in names]
    first_row, at = {}, 0
    for name, size in SMALL:
        first_row[name] = at // 128
        at += size
    sizes = dict(SMALL)

    def body(*refs):
        out_ref = refs[-1]
        out_ref[...] = jnp.zeros_like(out_ref)
        for name, ref in zip(names, refs):
            row = first_row[name]
            if name == "loss_sum":
                lane0 = lax.broadcasted_iota(jnp.int32, (1, 128), 1) == 0
                out_ref[row:row + 1, :] = jnp.where(lane0, ref[...], 0.0)
            else:
                rows = sizes[name] // 128
                out_ref[row:row + rows, :] = ref[...].reshape(rows, 128)

    vmem = pl.BlockSpec(memory_space=pltpu.VMEM)
    return pl.pallas_call(
        body, name="pack_small", in_specs=[vmem] * len(names), out_specs=vmem,
        out_shape=jax.ShapeDtypeStruct((SMALL_ROWS, 128), F32), compiler_params=_params(()),
    )(*operands)


LATE = "pre_mix_norm"


def adamw_small(packed_g, late, ws, ms, vs, after=()):
    names = [LATE] if late else [name for name, _ in SMALL if name not in ("loss_sum", LATE)]
    k = len(names)
    shapes = [ws[name].shape[1:] if ws[name].ndim > 2 else ws[name].shape for name in names]
    first_row, at = {}, 0
    for name, size in SMALL:
        first_row[name] = at // 128
        at += size
    sizes = dict(SMALL)
    n_in = 3 if late else 1

    def body(*refs):
        w_refs, m_refs, v_refs = (refs[n_in + j * k:n_in + (j + 1) * k] for j in range(3))
        outs = refs[n_in + 3 * k + len(after):]
        for i, name in enumerate(names):
            if late:
                late_ref, own_ref, dev_ref = refs[:n_in]
                parts = [jnp.where(dev_ref[0] == j, own_ref[...], late_ref[j]) for j in range(N_DEV)]
                g = parts[0]
                for j in range(1, N_DEV):
                    g = g + parts[j]
            else:
                g = refs[0][first_row[name]:first_row[name] + sizes[name] // 128, :].reshape(shapes[i])
            outs[i][...] = g
            outs[k + i][...], outs[2 * k + i][...], outs[3 * k + i][...] = _adamw_update(
                w_refs[i][...], g, m_refs[i][...], v_refs[i][...])
        if not late:
            outs[4 * k][...] = refs[0][first_row["loss_sum"]:first_row["loss_sum"] + 1, 0:1]

    vmem = pl.BlockSpec(memory_space=pltpu.VMEM)
    operands = [t[name].reshape(shape) for t in (ws, ms, vs) for name, shape in zip(names, shapes)]
    outs = pl.pallas_call(
        body, name="adamw_late" if late else "adamw_small",
        in_specs=([vmem, vmem, pl.BlockSpec(memory_space=pltpu.SMEM)] if late else [vmem]) + [vmem] * (3 * k) + [ANY] * len(after),
        out_specs=[vmem] * (4 * k + (0 if late else 1)),
        out_shape=[jax.ShapeDtypeStruct(shape, F32) for _ in range(4) for shape in shapes]
        + ([] if late else [jax.ShapeDtypeStruct((1, 1), F32)]),
        compiler_params=_params(()),
    )(*(late or (packed_g,)), *operands, *after)
    tables = [{name: outs[j * k + i].reshape(ws[name].shape) for i, name in enumerate(names)} for j in range(4)]
    return (*tables, None if late else outs[4 * k])


def kernel(x, positions, pre_mix_norm, w_in, sgu_ln_gain, sgu_ln_bias, sgu_w_spatial, sgu_b_spatial, attn_out_norm, sgu_out_norm, w_out, post_mix_norm, pre_ffn_norm, w_gate, w_up, w_down, post_ffn_norm, loss_target, m_pre_mix_norm, m_w_in, m_sgu_ln_gain, m_sgu_ln_bias, m_sgu_w_spatial, m_sgu_b_spatial, m_attn_out_norm, m_sgu_out_norm, m_w_out, m_post_mix_norm, m_pre_ffn_norm, m_w_gate, m_w_up, m_w_down, m_post_ffn_norm, v_pre_mix_norm, v_w_in, v_sgu_ln_gain, v_sgu_ln_bias, v_sgu_w_spatial, v_sgu_b_spatial, v_attn_out_norm, v_sgu_out_norm, v_w_out, v_post_mix_norm, v_pre_ffn_norm, v_w_gate, v_w_up, v_w_down, v_post_ffn_norm):
    small_w = dict(pre_mix_norm=pre_mix_norm, sgu_ln_gain=sgu_ln_gain, sgu_ln_bias=sgu_ln_bias, sgu_w_spatial=sgu_w_spatial,
                   sgu_b_spatial=sgu_b_spatial, attn_out_norm=attn_out_norm, sgu_out_norm=sgu_out_norm,
                   post_mix_norm=post_mix_norm, pre_ffn_norm=pre_ffn_norm, post_ffn_norm=post_ffn_norm)
    small_m = dict(pre_mix_norm=m_pre_mix_norm, sgu_ln_gain=m_sgu_ln_gain, sgu_ln_bias=m_sgu_ln_bias, sgu_w_spatial=m_sgu_w_spatial,
                   sgu_b_spatial=m_sgu_b_spatial, attn_out_norm=m_attn_out_norm, sgu_out_norm=m_sgu_out_norm,
                   post_mix_norm=m_post_mix_norm, pre_ffn_norm=m_pre_ffn_norm, post_ffn_norm=m_post_ffn_norm)
    small_v = dict(pre_mix_norm=v_pre_mix_norm, sgu_ln_gain=v_sgu_ln_gain, sgu_ln_bias=v_sgu_ln_bias, sgu_w_spatial=v_sgu_w_spatial,
                   sgu_b_spatial=v_sgu_b_spatial, attn_out_norm=v_attn_out_norm, sgu_out_norm=v_sgu_out_norm,
                   post_mix_norm=v_post_mix_norm, pre_ffn_norm=v_pre_ffn_norm, post_ffn_norm=v_post_ffn_norm)

    x2d = x[0]
    target = loss_target[0]
    pos_col = positions.reshape(SEQ, 1)
    rot = _rot_consts()
    w_sp = sgu_w_spatial[0]
    bfull = jnp.repeat(sgu_b_spatial[0].T, HEAD_DIM, axis=1)

    x_i, y_i, c_i = (lax.axis_index(a).astype(jnp.int32) for a in MESH_AXES)
    dev = 4 * x_i + 2 * y_i + c_i
    core = c_i.reshape(1)
    chip = 2 * x_i + y_i
    chip_ids = jnp.stack([chip, chip ^ 1, chip ^ 2, chip ^ 3])
    dev_ids = jnp.stack([dev ^ m for m in range(N_DEV)])

    def gathered(name, bufs):
        return by_sequencer(name, [gather(bufs)], TO_GATHER)[0][0]

    def from_sibling(name, grads):
        return by_sequencer(name, [to_sibling(grads)], TO_SIBLING)[0][1]

    def from_chips(name, parts):
        return by_sequencer(name, [to_chips(parts)], TO_CHIPS)[0][1]

    (w_in_t,) = place_shards("place_w_in", [w_in[0].T], dev.reshape(1))
    (w_in_t,) = gathered("gather_w_in", [w_in_t])
    w_gate_t, w_up_t, w_out_f, w_down_f = place_shards(
        "place_weights", [w_gate[0].T, w_up[0].T, w_out[0], w_down[0]], dev.reshape(1))
    (w_out_f,) = gathered("gather_w_out", [w_out_f])
    w_gate_t, w_up_t = gathered("gather_w_gate_up", [w_gate_t, w_up_t])
    (w_down_f,) = gathered("gather_w_down", [w_down_f])

    h1, u, vs, q, k, v = in_proj(x2d, pos_col, pre_mix_norm, w_in_t, rot)
    attn_r, lse, attn = attn_fwd(q, k, v)
    (sgu,) = sgu_fwd(u, vs, sgu_ln_gain, sgu_ln_bias, w_sp, bfull)
    mix, y, x2, h2 = out_proj(attn, sgu, x2d, attn_out_norm, sgu_out_norm, w_out_f, post_mix_norm, pre_ffn_norm)
    gate, up, act = ffn_up(h2, w_gate_t, w_up_t)
    df, dx3, d_post_ffn, sq_err = ffn_down_loss(act, w_down_f, x2, post_ffn_norm, target)

    g_w_down = weight_grad("grad_w_down", act, df)
    (s_down,) = from_sibling("w_down_to_sibling", [g_w_down])
    dgate, dup, dx2, dy, d_pre_ffn, d_post_mix = ffn_bwd(
        df, w_down_f, gate, up, w_gate_t, w_up_t, x2, pre_ffn_norm, dx3, y, post_mix_norm, after=[g_w_down])
    (p_down,) = sum_cores("sum_cores_down", [g_w_down], [s_down], core, after=[dy])
    (c_down,) = from_chips("w_down_to_chips", [p_down])
    g_w_gate, g_w_up = weight_grads("grad_w_gate_up", [dgate, dup], h2, after=[p_down])
    s_gate, s_up = from_sibling("w_gate_up_to_sibling", [g_w_gate, g_w_up])
    g_w_out = weight_grad("grad_w_out", mix, dy, after=[g_w_up])
    (s_out,) = from_sibling("w_out_to_sibling", [g_w_out])
    dsgu, d_attn_out, d_sgu_out, dattn_r = mix_bwd(dy, w_out_f, attn, sgu, attn_out_norm, sgu_out_norm, after=[g_w_out, c_down])
    p_gate, p_up = sum_cores("sum_cores_gate_up", [g_w_gate, g_w_up], [s_gate, s_up], core, after=[dsgu])
    c_gate, c_up = from_chips("w_gate_up_to_chips", [p_gate, p_up])
    du, dvs, d_ln_gain, d_ln_bias, d_w_sp, d_bfull = sgu_bwd(
        u, vs, dsgu, sgu_ln_gain, sgu_ln_bias, w_sp, bfull, after=[p_gate, p_up])

    d_b_sp = d_bfull.reshape(CHUNK, N_GROUPS, HEAD_DIM).sum(axis=-1).T
    small_g = pack_small(dict(sgu_ln_gain=d_ln_gain, sgu_ln_bias=d_ln_bias, sgu_w_spatial=d_w_sp, sgu_b_spatial=d_b_sp,
                              attn_out_norm=d_attn_out, sgu_out_norm=d_sgu_out, post_mix_norm=d_post_mix,
                              pre_ffn_norm=d_pre_ffn, post_ffn_norm=d_post_ffn, loss_sum=sq_err))
    small_g = small_g.reshape(N_DEV, SMALL_ROWS // N_DEV, 128)
    ((_, (o_small,)),) = by_sequencer("small_to_owners", [to_owners(small_g)], TO_ALL)

    dq, dk, dv = attn_bwd(q, k, v, attn_r, lse, dattn_r, _to_residue_order(pos_col), rot)
    summed_small = sum_owned("sum_small", small_g, o_small, dev_ids, after=[dq, c_gate, c_up])
    (all_small,) = gathered("gather_small_grads", [summed_small])
    (p_out,) = sum_cores("sum_cores_out", [g_w_out], [s_out], core, after=[dq])
    (c_out,) = from_chips("w_out_to_chips", [p_out])
    dproj = [dq, dk, dv, du, dvs]
    g_w_in = weight_grad_of_parts("grad_w_in", dproj, h1, after=[c_gate, c_up])
    (s_in,) = from_sibling("w_in_to_sibling", [g_w_in])

    def same(t):
        return t

    def turned(t):
        return t.T

    big = {}

    def adamw(call, weights, after):
        results = adamw_of_sums(call, [p for _, _, p, _, _, _, _ in weights], [c for _, _, _, c, _, _, _ in weights], chip_ids,
                                [turn(w[0]) for _, w, _, _, _, _, turn in weights], [turn(m[0]) for _, _, _, _, m, _, turn in weights],
                                [turn(vv[0]) for _, _, _, _, _, vv, turn in weights], after)
        for (name, _, _, _, _, _, turn), outs in zip(weights, results):
            big[name] = tuple(turn(t)[None] for t in outs)
        return results[-1][0]

    done_ffn = adamw("adamw_ffn", (("w_down", w_down, p_down, c_down, m_w_down, v_w_down, same),
                                   ("w_gate", w_gate, p_gate, c_gate, m_w_gate, v_w_gate, turned),
                                   ("w_up", w_up, p_up, c_up, m_w_up, v_w_up, turned)), [g_w_in, all_small])
    (p_in,) = sum_cores("sum_cores_in", [g_w_in], [s_in], core, after=[done_ffn, c_out])
    (c_in,) = from_chips("w_in_to_chips", [p_in])
    grad_x, d_pre_mix = in_bwd(dproj, w_in_t, x2d, pre_mix_norm, dx2, after=[p_in])
    ((_, (late_parts,)),) = by_sequencer("pre_mix_to_everyone", [to_everyone(d_pre_mix)], TO_ALL)
    done_out = adamw("adamw_w_out", (("w_out", w_out, p_out, c_out, m_w_out, v_w_out, same),), [d_pre_mix])
    *early, loss_sum = adamw_small(all_small, None, small_w, small_m, small_v, after=[done_out])
    done_in = adamw("adamw_w_in", (("w_in", w_in, p_in, c_in, m_w_in, v_w_in, turned),), [loss_sum])
    *late, _ = adamw_small(None, (late_parts, d_pre_mix, dev.reshape(1)), small_w, small_m, small_v, after=[done_in])
    sg, sd, snm, snv = ({**a, **b} for a, b in zip(early, late))
    loss = loss_sum[0, 0] * np.float32(0.5 / D_MODEL)

    names = ["pre_mix_norm", "w_in", "sgu_ln_gain", "sgu_ln_bias", "sgu_w_spatial", "sgu_b_spatial", "attn_out_norm",
             "sgu_out_norm", "w_out", "post_mix_norm", "pre_ffn_norm", "w_gate", "w_up", "w_down", "post_ffn_norm"]
    outs = [loss, grad_x[None]]
    for i, table in enumerate((sg, sd, snm, snv)):
        for name in names:
            outs.append(big[name][i] if name in big else table[name])
    return tuple(outs)
```

```python
import numpy as np
import jax
import jax.numpy as jnp
from jax import lax
from jax.experimental import pallas as pl
from jax.experimental.pallas import tpu as pltpu
from jax.experimental.pallas import tpu_sc as plsc

F32 = jnp.float32
BF16 = jnp.bfloat16

SEQ = 2048
D_MODEL = 1024
ATTN_W = 512
SGU_W = 512
HEAD_DIM = 64
N_GROUPS = 8
CHUNK = 128
D_FF = 2816
IN_W = 3 * ATTN_W + 2 * SGU_W
DILATIONS = (1, 4, 16)
ROPE_THETA = 500000.0
ROT_DIM = 16
ROT_HALF = 8
RMS_EPS = 1e-6
LN_EPS = 1e-5
Q_SCALE = 0.125
NEG = -1e30

N_DEV = 8
MESH_AXES = ("x", "y", "c")
MESH = pl.DeviceIdType.MESH

ADAM_LR = 0.001
ADAM_B1 = 0.9
ADAM_B2 = 0.999
ADAM_EPS = 1e-08
ADAM_WD = 0.01
ADAM_STEP = 10

VMEM_LIMIT = 60 * 1024 * 1024
ANY = pl.BlockSpec(memory_space=pl.ANY)

SMALL = (("pre_mix_norm", 1024), ("sgu_ln_gain", 512), ("sgu_ln_bias", 512), ("sgu_w_spatial", 8 * 128 * 128),
         ("sgu_b_spatial", 1024), ("attn_out_norm", 512), ("sgu_out_norm", 512), ("post_mix_norm", 1024),
         ("pre_ffn_norm", 1024), ("post_ffn_norm", 1024), ("loss_sum", 1))
SMALL_ROWS = 1152


def _params(sem=("arbitrary",)):
    return pltpu.CompilerParams(dimension_semantics=sem, vmem_limit_bytes=VMEM_LIMIT)


def _dot(a, b):
    return jnp.dot(a, b, preferred_element_type=F32)


def _dot_nt(a, b):
    return lax.dot_general(a, b, (((1,), (1,)), ((), ())), preferred_element_type=F32)


def _dot_tn(a, b):
    return lax.dot_general(a, b, (((0,), (0,)), ((), ())), preferred_element_type=F32)


def _rms(z):
    return lax.rsqrt(jnp.mean(z * z, axis=-1, keepdims=True) + RMS_EPS)


def _rms_bwd(z, gain, d):
    r = _rms(z)
    n = z * r
    dn = d * gain
    dz = r * (dn - n * jnp.mean(dn * n, axis=-1, keepdims=True))
    return dz, jnp.sum(d * n, axis=0, keepdims=True)


def _gelu(z):
    return 0.5 * z * (1.0 + lax.erf(z * np.float32(1.0 / np.sqrt(2.0))))


def _gelu_grad(z):
    cdf = 0.5 * (1.0 + lax.erf(z * np.float32(1.0 / np.sqrt(2.0))))
    return cdf + z * jnp.exp(-0.5 * z * z) * np.float32(1.0 / np.sqrt(2.0 * np.pi))


def _rot_tables(pos_col, invf, ma, mb):
    ang = pos_col.astype(F32) * invf
    s = jnp.sin(ang)
    return jnp.cos(ang), s * ma, s * mb


def _rot(t, c, sa, sb):
    return t * c + pltpu.roll(t, 120, 1) * sa + pltpu.roll(t, 8, 1) * sb


def _rot_t(d, c, sa, sb):
    return d * c + pltpu.roll(d * sa, 8, 1) + pltpu.roll(d * sb, 120, 1)


def _rot_consts():
    lane = np.arange(128) % HEAD_DIM
    inv_freq = (np.float32(ROPE_THETA) ** (-np.arange(0, ROT_DIM, 2, dtype=np.float32) / np.float32(ROT_DIM))).astype(np.float32)
    invf = np.where(lane < ROT_DIM, inv_freq[lane % ROT_HALF], 0.0).astype(np.float32)
    ma = np.where(lane < ROT_HALF, -1.0, 0.0).astype(np.float32)
    mb = np.where((lane >= ROT_HALF) & (lane < ROT_DIM), 1.0, 0.0).astype(np.float32)
    return jnp.asarray(invf[None]), jnp.asarray(ma[None]), jnp.asarray(mb[None])


def _row_spec(tm, w):
    return pl.BlockSpec((tm, w), lambda i: (i, 0))


def _full_spec(shape):
    return pl.BlockSpec(shape, lambda i: (0,) * len(shape))


def _weight_spec(shape):
    return pl.BlockSpec(shape, lambda i: (0,) * len(shape), pipeline_mode=pl.Buffered(1))


FF_CHUNKS = (256, 512, 1024, 1024)
FF_SPANS = [(int(o), n) for o, n in zip(np.cumsum((0,) + FF_CHUNKS[:-1]), FF_CHUNKS)]
FF_WHOLE = [(0, D_FF)]


def _ffn_weight_scratch(n_weights, spans):
    return [pltpu.VMEM((D_FF, D_MODEL), BF16)] * n_weights + [pltpu.SemaphoreType.DMA((n_weights, len(spans)))]


def _with_ffn_weights(w_hbm, w_vmem, sems, spans, run):
    copies = [[pltpu.make_async_copy(h.at[pl.ds(o, n)], v.at[pl.ds(o, n)], sems.at[j, c]) for c, (o, n) in enumerate(spans)]
              for j, (h, v) in enumerate(zip(w_hbm, w_vmem))]
    first = pl.program_id(0) == 0

    @pl.when(first)
    def _():
        for of_weight in copies:
            for cp in of_weight:
                cp.start()
        run(lambda j, c: copies[j][c].wait())

    @pl.when(jnp.logical_not(first))
    def _():
        run(None)


RES = 16


def _residue_scratch(n_arrays, tm, width):
    return [pltpu.VMEM((2, n_arrays, tm // RES, RES, width), F32), pltpu.SemaphoreType.DMA((2, n_arrays, RES))]


def _to_residue_rows(tiles, outs, scratch, sems, tm, n_steps):
    i = pl.program_id(0)
    slot = i % 2
    per = tm // RES

    def copies(step, s):
        return [pltpu.make_async_copy(scratch.at[s, a, :, b, :],
                                      outs[a].at[pl.ds(pl.multiple_of(b * (SEQ // RES) + per * step, per), per), :],
                                      sems.at[s, a, b]) for a in range(len(outs)) for b in range(RES)]

    @pl.when(i >= 2)
    def _():
        for cp in copies(i - 2, slot):
            cp.wait()

    for a, tile in enumerate(tiles):
        scratch[slot, a] = tile.reshape(per, RES, tile.shape[-1])
    for cp in copies(i, slot):
        cp.start()

    @pl.when(i == n_steps - 1)
    def _():
        for cp in copies(i - 1, 1 - slot) + copies(i, slot):
            cp.wait()


def pre_norm(x, g1):
    tm = 512

    def body(x_ref, g_ref, h_ref):
        xf = x_ref[...]
        h_ref[...] = (xf * _rms(xf) * g_ref[...]).astype(BF16)

    return _call("pre_norm", body, SEQ // tm, [_row_spec(tm, D_MODEL), _full_spec((1, D_MODEL))], [_row_spec(tm, D_MODEL)],
                 [jax.ShapeDtypeStruct((SEQ, D_MODEL), BF16)], (x, g1))[0]


def in_proj(h1, pos_col, w_in_t, rot):
    tm = 512
    n_steps = SEQ // tm

    def body(h_ref, pos_ref, w_ref, invf_ref, ma_ref, mb_ref, u_ref, vs_ref, q_ref, k_ref, v_ref, scratch, sems):
        proj = _dot_nt(h_ref[...], w_ref[...])
        c, sa, sb = _rot_tables(pos_ref[...], invf_ref[...], ma_ref[...], mb_ref[...])
        slabs = range(ATTN_W // 128)
        q = jnp.concatenate([_rot(proj[:, j * 128:(j + 1) * 128], c, sa, sb) * Q_SCALE for j in slabs], axis=1)
        k = jnp.concatenate([_rot(proj[:, ATTN_W + j * 128:ATTN_W + (j + 1) * 128], c, sa, sb) for j in slabs], axis=1)
        u_ref[...] = proj[:, 3 * ATTN_W:3 * ATTN_W + SGU_W]
        vs_ref[...] = proj[:, 3 * ATTN_W + SGU_W:]
        _to_residue_rows([q, k, proj[:, 2 * ATTN_W:3 * ATTN_W]], [q_ref, k_ref, v_ref], scratch, sems, tm, n_steps)

    act = jax.ShapeDtypeStruct((SEQ, 512), F32)
    return _call(
        "in_proj", body, n_steps,
        [_row_spec(tm, D_MODEL), _row_spec(tm, 1), _weight_spec((IN_W, D_MODEL)),
         _full_spec((1, 128)), _full_spec((1, 128)), _full_spec((1, 128))],
        [_row_spec(tm, 512)] * 2 + [ANY] * 3, [act] * 5,
        (h1, pos_col, w_in_t, *rot), scratch_shapes=_residue_scratch(3, tm, ATTN_W))


def _to_residue_order(t):
    return t.reshape(SEQ // RES, RES, -1).transpose(1, 0, 2).reshape(t.shape)


def _block_rows(d, r, n):
    if d == 16:
        slices = [(128 * r, 128)]
    elif d == 4:
        slices = [(128 * (4 * b + r) + 32 * n, 32) for b in range(4)]
    else:
        slices = [(128 * b + 8 * n, 8) for b in range(RES)]
    return [(s if isinstance(s, int) else pl.multiple_of(s, z), z) for s, z in slices]


def _block_step(d, i):
    if d == 16:
        return i
    if d == 4:
        return 4 * (i & 31) + (i >> 5)
    return 16 * (i & 7) + (i >> 3)


def _attn_masks(d):
    row2 = _block_step(d, lax.broadcasted_iota(jnp.int32, (128, 256), 0))
    col2 = lax.broadcasted_iota(jnp.int32, (128, 256), 1)
    key2 = _block_step(d, col2 & 127)
    mask2 = jnp.logical_or(jnp.logical_and(col2 < 128, key2 >= row2), jnp.logical_and(col2 >= 128, key2 <= row2))
    row1 = _block_step(d, lax.broadcasted_iota(jnp.int32, (128, 128), 0))
    col1 = lax.broadcasted_iota(jnp.int32, (128, 128), 1)
    return col1 < HEAD_DIM, _block_step(d, col1) <= row1, mask2


def _load_rows(ref, slices):
    parts = [ref[pl.ds(s, z), :] for s, z in slices]
    return parts[0] if len(parts) == 1 else jnp.concatenate(parts, axis=0)


def _for_each_group(fn):
    for p, d in enumerate(DILATIONS):
        masks = _attn_masks(d)
        if d == 16:
            def group(i, carry, p=p, masks=masks):
                fn(p, masks, [(_block_rows(16, 8 * i + g, 0), None) for g in range(8)])
                return carry

            lax.fori_loop(0, 2, group, 0)
        elif d == 4:
            fn(p, masks, [(_block_rows(4, r, 0), None) for r in range(4)])

            def group(i, carry, p=p, masks=masks):
                blocks = [6 * i + g for g in range(6)]
                fn(p, masks, [(_block_rows(4, j % 4, 1 + j // 4), _block_rows(4, j % 4, j // 4)) for j in blocks])
                return carry

            lax.fori_loop(0, 2, group, 0)
        else:
            fn(p, masks, [(_block_rows(1, 0, 0), None)])

            def group(i, carry, p=p, masks=masks):
                fn(p, masks, [(_block_rows(1, 0, 5 * i + g + 1), _block_rows(1, 0, 5 * i + g)) for g in range(5)])
                return carry

            lax.fori_loop(0, 3, group, 0)


def attn_fwd(q, k, v):
    def body(q_ref, k_ref, v_ref, o_ref, lse_ref, nat_ref, op_ref, lp_ref, sems):
        def group(p, masks, blocks):
            head0, mask1, mask2 = masks
            heads = (head0, jnp.logical_not(head0))
            keys = [rows if prev is None else prev + rows for rows, prev in blocks]
            mask = [mask1 if prev is None else mask2 for _, prev in blocks]
            qb = [_load_rows(q_ref, rows) for rows, _ in blocks]
            kk = [_load_rows(k_ref, ks).astype(BF16) for ks in keys]
            vv = [_load_rows(v_ref, ks).astype(BF16) for ks in keys]
            chains = [(g, hm) for g in range(len(blocks)) for hm in heads]
            s = [jnp.where(mask[g], _dot_nt(jnp.where(hm, qb[g], 0.0).astype(BF16), kk[g]), NEG) for g, hm in chains]
            m = [jnp.max(t, axis=-1, keepdims=True) for t in s]
            e = [jnp.exp(t - mt) for t, mt in zip(s, m)]
            l = [jnp.sum(t, axis=-1, keepdims=True) for t in e]
            pv = [_dot(t.astype(BF16), vv[g]) for t, (g, _) in zip(e, chains)]
            for g, (rows, _) in enumerate(blocks):
                o_blk = jnp.where(head0, pv[2 * g] / l[2 * g], pv[2 * g + 1] / l[2 * g + 1])
                l_blk = jnp.where(head0, jnp.broadcast_to(m[2 * g] + jnp.log(l[2 * g]), (128, 128)),
                                  jnp.broadcast_to(m[2 * g + 1] + jnp.log(l[2 * g + 1]), (128, 128)))
                at = 0
                for start, size in rows:
                    op_ref[p, pl.ds(start, size), :] = o_blk[at:at + size]
                    lp_ref[p, pl.ds(start, size), :] = l_blk[at:at + size]
                    at += size

        _for_each_group(group)

        def combine(i, carry):
            rows = pl.ds(pl.multiple_of(i * 256, 256), 256)
            ls = [lp_ref[p, rows, :] for p in range(3)]
            m = jnp.maximum(jnp.maximum(ls[0], ls[1]), ls[2])
            lse = m + jnp.log(jnp.exp(ls[0] - m) + jnp.exp(ls[1] - m) + jnp.exp(ls[2] - m))
            o = jnp.zeros((256, 128), F32)
            for p in range(3):
                o = o + jnp.exp(ls[p] - lse) * op_ref[p, rows, :]
            o_ref[rows, :] = o
            lse_ref[rows, :] = lse
            return carry

        lax.fori_loop(0, SEQ // 256, combine, 0)

        lanes = pl.ds(pl.multiple_of(pl.program_id(0) * 128, 128), 128)
        back = [pltpu.make_async_copy(o_ref.at[pl.ds(b * (SEQ // RES), SEQ // RES), :], nat_ref.at[:, b, lanes], sems.at[b])
                for b in range(RES)]
        for cp in back:
            cp.start()
        for cp in back:
            cp.wait()

    slab = pl.BlockSpec((SEQ, 128), lambda i: (0, i))
    out = jax.ShapeDtypeStruct((SEQ, ATTN_W), F32)
    attn_r, lse, attn = _call(
        "attn_fwd", body, ATTN_W // 128, [slab] * 3, [slab] * 2 + [ANY],
        [out, out, jax.ShapeDtypeStruct((SEQ // RES, RES, ATTN_W), F32)], (q, k, v),
        scratch_shapes=[pltpu.VMEM((3, SEQ, 128), F32), pltpu.VMEM((3, SEQ, 128), F32), pltpu.SemaphoreType.DMA((RES,))])
    return attn_r, lse, attn.reshape(SEQ, ATTN_W)


def _causal_weights(w_ref):
    row = lax.broadcasted_iota(jnp.int32, (CHUNK, CHUNK), 0)
    col = lax.broadcasted_iota(jnp.int32, (CHUNK, CHUNK), 1)
    return [jnp.where(col <= row, w_ref[g], 0.0).astype(BF16) for g in range(N_GROUPS)], col <= row


def _sgu_chunk_fwd(u, vs, lg, lb, wc, bfull, head0):
    ug = _gelu(u)
    vg = _gelu(vs)
    xc = vg - jnp.mean(vg, axis=-1, keepdims=True)
    rstd = lax.rsqrt(jnp.mean(xc * xc, axis=-1, keepdims=True) + LN_EPS)
    xhat = xc * rstd
    vn = xhat * lg + lb
    mixed = []
    for gp in range(SGU_W // 128):
        vp = vn[:, gp * 128:(gp + 1) * 128].astype(BF16)
        mixed.append(jnp.where(head0, _dot(wc[2 * gp], vp), _dot(wc[2 * gp + 1], vp)))
    ms = jnp.concatenate(mixed, axis=1) + bfull
    return ug, xhat, rstd, vn, ms


def sgu_fwd(u, vs, lg, lb, w_sp, bfull):
    cpb = 4

    def body(u_ref, vs_ref, lg_ref, lb_ref, w_ref, b_ref, o_ref):
        wc, _ = _causal_weights(w_ref)
        head0 = lax.broadcasted_iota(jnp.int32, (CHUNK, 128), 1) < HEAD_DIM
        for ci in range(cpb):
            rows = pl.ds(ci * CHUNK, CHUNK)
            ug, _, _, _, ms = _sgu_chunk_fwd(u_ref[rows, :], vs_ref[rows, :], lg_ref[...], lb_ref[...], wc, b_ref[...], head0)
            o_ref[rows, :] = ug * ms

    tm = cpb * CHUNK
    return _call(
        "sgu_fwd", body, SEQ // tm,
        [_row_spec(tm, SGU_W), _row_spec(tm, SGU_W), _full_spec((1, SGU_W)), _full_spec((1, SGU_W)),
         _full_spec((N_GROUPS, CHUNK, CHUNK)), _full_spec((CHUNK, SGU_W))],
        [_row_spec(tm, SGU_W)], [jax.ShapeDtypeStruct((SEQ, SGU_W), F32)],
        (u, vs, lg, lb, w_sp, bfull))


def out_proj(attn, sgu, x, ga, gs, w_out, gpm, gpf):
    tm = 512

    def body(a_ref, s_ref, x_ref, ga_ref, gs_ref, w_ref, gpm_ref, gpf_ref, mix_ref, y_ref, x2_ref, h2_ref):
        a = a_ref[...]
        s = s_ref[...]
        an = (a * _rms(a) * ga_ref[...]).astype(BF16)
        sn = (s * _rms(s) * gs_ref[...]).astype(BF16)
        mix_ref[:, :ATTN_W] = an
        mix_ref[:, ATTN_W:] = sn
        y = _dot(an, w_ref[:ATTN_W, :]) + _dot(sn, w_ref[ATTN_W:, :])
        y_ref[...] = y
        x2 = x_ref[...] + y * _rms(y) * gpm_ref[...]
        x2_ref[...] = x2
        h2_ref[...] = (x2 * _rms(x2) * gpf_ref[...]).astype(BF16)

    wide = jax.ShapeDtypeStruct((SEQ, D_MODEL), F32)
    wide16 = jax.ShapeDtypeStruct((SEQ, D_MODEL), BF16)
    return _call(
        "out_proj", body, SEQ // tm,
        [_row_spec(tm, ATTN_W), _row_spec(tm, SGU_W), _row_spec(tm, D_MODEL), _full_spec((1, ATTN_W)),
         _full_spec((1, SGU_W)), _weight_spec((D_MODEL, D_MODEL)), _full_spec((1, D_MODEL)), _full_spec((1, D_MODEL))],
        [_row_spec(tm, D_MODEL)] * 4, [wide16, wide, wide, wide16],
        (attn, sgu, x, ga, gs, w_out, gpm, gpf))


def ffn_up(h2, w_gate_t, w_up_t):
    tm = 256

    def body(h_ref, wg_hbm, wu_hbm, g_ref, u_ref, a_ref, wg_ref, wu_ref, sems):
        def run(wait):
            h = h_ref[...]
            if wait:
                wait(0, 0)
            g = _dot_nt(h, wg_ref[...])
            g_ref[...] = g.astype(BF16)
            if wait:
                wait(1, 0)
            u = _dot_nt(h, wu_ref[...])
            u_ref[...] = u.astype(BF16)
            a_ref[...] = (g * jax.nn.sigmoid(g) * u).astype(BF16)

        _with_ffn_weights([wg_hbm, wu_hbm], [wg_ref, wu_ref], sems, FF_WHOLE, run)

    ff = jax.ShapeDtypeStruct((SEQ, D_FF), BF16)
    return _call(
        "ffn_up", body, SEQ // tm, [_row_spec(tm, D_MODEL), ANY, ANY],
        [_row_spec(tm, D_FF)] * 3, [ff, ff, jax.ShapeDtypeStruct((SEQ, D_FF), BF16)],
        (h2, w_gate_t, w_up_t), scratch_shapes=_ffn_weight_scratch(2, FF_WHOLE))


def ffn_down_loss(act, w_down, x2, gpo, target):
    tm = 512

    def body(a_ref, w_hbm, x2_ref, g_ref, t_ref, df_ref, dx3_ref, dg_ref, loss_ref, w_ref, sems):
        def run(wait):
            f = None
            for c, (o, n) in enumerate(FF_SPANS if wait else FF_WHOLE):
                if wait:
                    wait(0, c)
                part = _dot(a_ref[:, o:o + n], w_ref[o:o + n, :])
                f = part if f is None else f + part
            gain = g_ref[...]
            err = x2_ref[...] + f * _rms(f) * gain - t_ref[...]
            dx3 = err * np.float32(1.0 / D_MODEL)
            dx3_ref[...] = dx3
            df, dg = _rms_bwd(f, gain, dx3)
            df_ref[...] = df.astype(BF16)
            loss = jnp.sum(err * err, axis=(0, 1), keepdims=True)
            if wait:
                dg_ref[...] = dg
                loss_ref[...] = loss
            else:
                dg_ref[...] += dg
                loss_ref[...] += loss

        _with_ffn_weights([w_hbm], [w_ref], sems, FF_SPANS, run)

    return _call(
        "ffn_down_loss", body, SEQ // tm,
        [_row_spec(tm, D_FF), ANY, _row_spec(tm, D_MODEL), _full_spec((1, D_MODEL)), _row_spec(tm, D_MODEL)],
        [_row_spec(tm, D_MODEL), _row_spec(tm, D_MODEL), _full_spec((1, D_MODEL)), _full_spec((1, 1))],
        [jax.ShapeDtypeStruct((SEQ, D_MODEL), BF16), jax.ShapeDtypeStruct((SEQ, D_MODEL), F32),
         jax.ShapeDtypeStruct((1, D_MODEL), F32), jax.ShapeDtypeStruct((1, 1), F32)],
        (act, w_down, x2, gpo, target), scratch_shapes=_ffn_weight_scratch(1, FF_SPANS))


def ffn_bwd(df, w_down, gate, up, w_gate_t, w_up_t, x2, gpf, dx3, y, gpm, after=()):
    tm = 256

    def body(df_ref, wd_hbm, g_ref, u_ref, wg_hbm, wu_hbm, x2_ref, gpf_ref, dx3_ref, y_ref, gpm_ref,
             dg_ref, du_ref, dx2_ref, dy_ref, dgpf_ref, dgpm_ref, wd_ref, wg_ref, wu_ref, sems):
        def run(wait):
            if wait:
                wait(0, 0)
            dact = _dot_nt(df_ref[...], wd_ref[...])
            g = g_ref[...].astype(F32)
            s = jax.nn.sigmoid(g)
            dup = (dact * g * s).astype(BF16)
            dgate = (dact * u_ref[...].astype(F32) * (s * (1.0 + g * (1.0 - s)))).astype(BF16)
            du_ref[...] = dup
            dg_ref[...] = dgate
            if wait:
                wait(1, 0)
                wait(2, 0)
            dh2 = _dot(dgate, wg_ref[...]) + _dot(dup, wu_ref[...])
            dz, dgpf = _rms_bwd(x2_ref[...], gpf_ref[...], dh2)
            dx2 = dx3_ref[...] + dz
            dx2_ref[...] = dx2
            dy, dgpm = _rms_bwd(y_ref[...], gpm_ref[...], dx2)
            dy_ref[...] = dy.astype(BF16)
            if wait:
                dgpf_ref[...] = dgpf
                dgpm_ref[...] = dgpm
            else:
                dgpf_ref[...] += dgpf
                dgpm_ref[...] += dgpm

        _with_ffn_weights([wd_hbm, wg_hbm, wu_hbm], [wd_ref, wg_ref, wu_ref], sems, FF_WHOLE, run)

    vec = jax.ShapeDtypeStruct((1, D_MODEL), F32)
    ff16 = jax.ShapeDtypeStruct((SEQ, D_FF), BF16)
    return _call(
        "ffn_bwd", body, SEQ // tm,
        [_row_spec(tm, D_MODEL), ANY, _row_spec(tm, D_FF), _row_spec(tm, D_FF), ANY, ANY, _row_spec(tm, D_MODEL),
         _full_spec((1, D_MODEL)), _row_spec(tm, D_MODEL), _row_spec(tm, D_MODEL), _full_spec((1, D_MODEL))],
        [_row_spec(tm, D_FF), _row_spec(tm, D_FF), _row_spec(tm, D_MODEL), _row_spec(tm, D_MODEL),
         _full_spec((1, D_MODEL)), _full_spec((1, D_MODEL))],
        [ff16, ff16, jax.ShapeDtypeStruct((SEQ, D_MODEL), F32), jax.ShapeDtypeStruct((SEQ, D_MODEL), BF16), vec, vec],
        (df, w_down, gate, up, w_gate_t, w_up_t, x2, gpf, dx3, y, gpm), scratch_shapes=_ffn_weight_scratch(3, FF_WHOLE), after=after)


def weight_grads(name, lhs, b, after=()):
    m, n, k = lhs[0].shape[1], b.shape[1], len(lhs)
    tr = 256

    def body(*refs):
        for a_ref, o_ref in zip(refs[:k], refs[k + 1:]):
            o_ref[...] = _dot_tn(a_ref[...], refs[k][...]).astype(BF16)

    outs = _call(
        name, body, m // tr, [pl.BlockSpec((SEQ, tr), lambda i: (0, i))] * k + [_weight_spec((SEQ, n))],
        [_row_spec(tr, n)] * k, [jax.ShapeDtypeStruct((m, n), BF16)] * k, (*lhs, b), after=after)
    return [out.reshape(N_DEV, m // N_DEV, n) for out in outs]


def weight_grad(name, a, b, after=()):
    return weight_grads(name, [a], b, after)[0]


def weight_grad_of_parts(name, parts, b, after=()):
    p, n, k = parts[0].shape[1], b.shape[1], len(parts)
    tr = 512
    per = p // tr

    def body(*refs):
        tile = pl.program_id(0)
        for j in range(k):
            @pl.when(tile // per == j)
            def _(j=j):
                refs[k + 1][...] = _dot_tn(refs[j][...].astype(BF16), refs[k][...]).astype(BF16)

    def part_spec(j):
        return pl.BlockSpec((SEQ, tr), lambda i: (0, jnp.clip(i - per * j, 0, per - 1)))

    (out,) = _call(
        name, body, k * per, [part_spec(j) for j in range(k)] + [_weight_spec((SEQ, n))],
        [_row_spec(tr, n)], [jax.ShapeDtypeStruct((k * p, n), BF16)], (*parts, b), after=after)
    return out.reshape(N_DEV, k * p // N_DEV, n)


def mix_bwd(dy, w_out, attn, sgu, ga, gs, after=()):
    tm = 512
    n_steps = SEQ // tm

    def body(dy_ref, w_ref, a_ref, s_ref, ga_ref, gs_ref, ds_ref, dga_ref, dgs_ref, da_ref, scratch, sems):
        dy = dy_ref[...]
        da, dga = _rms_bwd(a_ref[...], ga_ref[...], _dot_nt(dy, w_ref[:ATTN_W, :]))
        ds, dgs = _rms_bwd(s_ref[...], gs_ref[...], _dot_nt(dy, w_ref[ATTN_W:, :]))
        ds_ref[...] = ds
        _to_residue_rows([da], [da_ref], scratch, sems, tm, n_steps)

        @pl.when(pl.program_id(0) == 0)
        def _():
            dga_ref[...] = jnp.zeros_like(dga_ref)
            dgs_ref[...] = jnp.zeros_like(dgs_ref)

        dga_ref[...] += dga
        dgs_ref[...] += dgs

    half = jax.ShapeDtypeStruct((SEQ, 512), F32)
    vec = jax.ShapeDtypeStruct((1, 512), F32)
    return _call(
        "mix_bwd", body, n_steps,
        [_row_spec(tm, D_MODEL), _weight_spec((D_MODEL, D_MODEL)), _row_spec(tm, 512), _row_spec(tm, 512),
         _full_spec((1, 512)), _full_spec((1, 512))],
        [_row_spec(tm, 512), _full_spec((1, 512)), _full_spec((1, 512)), ANY],
        [half, vec, vec, half], (dy, w_out, attn, sgu, ga, gs), scratch_shapes=_residue_scratch(1, tm, ATTN_W), after=after)


def sgu_bwd(u, vs, dsgu, lg, lb, w_sp, bfull, after=()):
    cpb = 4

    def body(u_ref, vs_ref, d_ref, lg_ref, lb_ref, w_ref, b_ref, du_ref, dvs_ref, dlg_ref, dlb_ref, dw_ref, db_ref):
        wc, causal = _causal_weights(w_ref)
        head0 = lax.broadcasted_iota(jnp.int32, (CHUNK, 128), 1) < HEAD_DIM
        lg = lg_ref[...]

        @pl.when(pl.program_id(0) == 0)
        def _():
            dlg_ref[...] = jnp.zeros_like(dlg_ref)
            dlb_ref[...] = jnp.zeros_like(dlb_ref)
            dw_ref[...] = jnp.zeros_like(dw_ref)
            db_ref[...] = jnp.zeros_like(db_ref)

        for ci in range(cpb):
            rows = pl.ds(ci * CHUNK, CHUNK)
            u = u_ref[rows, :]
            vs = vs_ref[rows, :]
            d = d_ref[rows, :]
            ug, xhat, rstd, vn, ms = _sgu_chunk_fwd(u, vs, lg, lb_ref[...], wc, b_ref[...], head0)
            du_ref[rows, :] = (d * ms * _gelu_grad(u)).astype(BF16)
            dms = d * ug
            db_ref[...] += dms
            dvn = []
            for gp in range(SGU_W // 128):
                dmp = dms[:, gp * 128:(gp + 1) * 128]
                dm0 = jnp.where(head0, dmp, 0.0).astype(BF16)
                dm1 = jnp.where(head0, 0.0, dmp).astype(BF16)
                vp = vn[:, gp * 128:(gp + 1) * 128].astype(BF16)
                dw_ref[2 * gp] += _dot_nt(dm0, vp)
                dw_ref[2 * gp + 1] += _dot_nt(dm1, vp)
                dvn.append(_dot_tn(wc[2 * gp], dm0) + _dot_tn(wc[2 * gp + 1], dm1))
            dvn = jnp.concatenate(dvn, axis=1)
            dlg_ref[...] += jnp.sum(dvn * xhat, axis=0, keepdims=True)
            dlb_ref[...] += jnp.sum(dvn, axis=0, keepdims=True)
            dxh = dvn * lg
            dvg = rstd * (dxh - jnp.mean(dxh, axis=-1, keepdims=True) - xhat * jnp.mean(dxh * xhat, axis=-1, keepdims=True))
            dvs_ref[rows, :] = (dvg * _gelu_grad(vs)).astype(BF16)

        @pl.when(pl.program_id(0) == pl.num_programs(0) - 1)
        def _():
            for g in range(N_GROUPS):
                dw_ref[g] = jnp.where(causal, dw_ref[g], 0.0)

    tm = cpb * CHUNK
    half16 = jax.ShapeDtypeStruct((SEQ, SGU_W), BF16)
    vec = jax.ShapeDtypeStruct((1, SGU_W), F32)
    return _call(
        "sgu_bwd", body, SEQ // tm,
        [_row_spec(tm, SGU_W)] * 3 + [_full_spec((1, SGU_W)), _full_spec((1, SGU_W)),
                                      _full_spec((N_GROUPS, CHUNK, CHUNK)), _full_spec((CHUNK, SGU_W))],
        [_row_spec(tm, SGU_W), _row_spec(tm, SGU_W), _full_spec((1, SGU_W)), _full_spec((1, SGU_W)),
         _full_spec((N_GROUPS, CHUNK, CHUNK)), _full_spec((CHUNK, SGU_W))],
        [half16, half16, vec, vec, jax.ShapeDtypeStruct((N_GROUPS, CHUNK, CHUNK), F32),
         jax.ShapeDtypeStruct((CHUNK, SGU_W), F32)],
        (u, vs, dsgu, lg, lb, w_sp, bfull), after=after)


def attn_bwd(q, k, v, o, lse, do, pos_col, rot):
    n_steps = ATTN_W // 128

    def body(q_ref, k_ref, v_ref, o_ref, lse_ref, do_ref, pos_ref, invf_ref, ma_ref, mb_ref,
             dq_ref, dk_ref, dv_ref, dqa_ref, dka_ref, dva_ref, dlt_ref, rot_ref, out_ref, sems):
        step = pl.program_id(0)
        slot = step % 2

        def back(at_step, s):
            lanes = pl.ds(pl.multiple_of(at_step * 128, 128), 128)
            return [pltpu.make_async_copy(out_ref.at[s, a, pl.ds(b * (SEQ // RES), SEQ // RES), :], nat.at[:, b, lanes],
                                          sems.at[s, a, b]) for a, nat in enumerate((dq_ref, dk_ref, dv_ref)) for b in range(RES)]

        dqa_ref[...] = jnp.zeros_like(dqa_ref)
        dka_ref[...] = jnp.zeros_like(dka_ref)
        dva_ref[...] = jnp.zeros_like(dva_ref)

        def delta(i, carry):
            rows = pl.ds(pl.multiple_of(i * 256, 256), 256)
            prod = do_ref[rows, :] * o_ref[rows, :]
            h0 = lax.broadcasted_iota(jnp.int32, (256, 128), 1) < HEAD_DIM
            d0 = jnp.sum(jnp.where(h0, prod, 0.0), axis=-1, keepdims=True)
            d1 = jnp.sum(jnp.where(h0, 0.0, prod), axis=-1, keepdims=True)
            dlt_ref[rows, :] = jnp.where(h0, d0, d1)
            return carry

        lax.fori_loop(0, SEQ // 256, delta, 0)

        def add_rows(ref, slices, val):
            at = 0
            for start, size in slices:
                ref[pl.ds(start, size), :] += val[at:at + size]
                at += size

        def group(p, masks, blocks):
            head0, mask1, mask2 = masks
            heads = (head0, jnp.logical_not(head0))
            keys = [rows if prev is None else prev + rows for rows, prev in blocks]
            mask = [mask1 if prev is None else mask2 for _, prev in blocks]
            kk = [_load_rows(k_ref, ks).astype(BF16) for ks in keys]
            vv = [_load_rows(v_ref, ks).astype(BF16) for ks in keys]
            qb = [_load_rows(q_ref, rows) for rows, _ in blocks]
            dob = [_load_rows(do_ref, rows) for rows, _ in blocks]
            lse_b = [_load_rows(lse_ref, rows) for rows, _ in blocks]
            dlt_b = [_load_rows(dlt_ref, rows) for rows, _ in blocks]
            chains = [(g, h) for g in range(len(blocks)) for h in range(2)]
            qm = [jnp.where(heads[h], qb[g], 0.0).astype(BF16) for g, h in chains]
            dom = [jnp.where(heads[h], dob[g], 0.0).astype(BF16) for g, h in chains]
            s = [_dot_nt(qm[c], kk[g]) for c, (g, h) in enumerate(chains)]
            dp = [_dot_nt(dom[c], vv[g]) for c, (g, h) in enumerate(chains)]
            pr = [jnp.where(mask[g], jnp.exp(s[c] - lse_b[g][:, h * HEAD_DIM:h * HEAD_DIM + 1]), 0.0)
                  for c, (g, h) in enumerate(chains)]
            ds = [(pr[c] * (dp[c] - dlt_b[g][:, h * HEAD_DIM:h * HEAD_DIM + 1])).astype(BF16)
                  for c, (g, h) in enumerate(chains)]
            dv = [_dot_tn(pr[c].astype(BF16), dom[c]) for c in range(len(chains))]
            dk = [_dot_tn(ds[c], qm[c]) for c in range(len(chains))]
            dq = [_dot(ds[c], kk[g]) for c, (g, h) in enumerate(chains)]
            for g, (rows, _) in enumerate(blocks):
                add_rows(dqa_ref, rows, jnp.where(head0, dq[2 * g], dq[2 * g + 1]))
                add_rows(dka_ref, keys[g], dk[2 * g] + dk[2 * g + 1])
                add_rows(dva_ref, keys[g], dv[2 * g] + dv[2 * g + 1])

        _for_each_group(group)

        @pl.when(pl.program_id(0) == 0)
        def _():
            def tables(i, carry):
                rows = pl.ds(pl.multiple_of(i * 256, 256), 256)
                c, sa, sb = _rot_tables(pos_ref[rows, :], invf_ref[...], ma_ref[...], mb_ref[...])
                rot_ref[0, rows, :] = c
                rot_ref[1, rows, :] = sa
                rot_ref[2, rows, :] = sb
                return carry

            lax.fori_loop(0, SEQ // 256, tables, 0)

        @pl.when(step >= 2)
        def _():
            for cp in back(step - 2, slot):
                cp.wait()

        def finish(i, carry):
            rows = pl.ds(pl.multiple_of(i * 256, 256), 256)
            c, sa, sb = rot_ref[0, rows, :], rot_ref[1, rows, :], rot_ref[2, rows, :]
            out_ref[slot, 0, rows, :] = _rot_t(dqa_ref[rows, :] * Q_SCALE, c, sa, sb)
            out_ref[slot, 1, rows, :] = _rot_t(dka_ref[rows, :], c, sa, sb)
            out_ref[slot, 2, rows, :] = dva_ref[rows, :]
            return carry

        lax.fori_loop(0, SEQ // 256, finish, 0)
        for cp in back(step, slot):
            cp.start()

        @pl.when(step == n_steps - 1)
        def _():
            for cp in back(step - 1, 1 - slot) + back(step, slot):
                cp.wait()

    slab = pl.BlockSpec((SEQ, 128), lambda i: (0, i))
    out = jax.ShapeDtypeStruct((SEQ // RES, RES, ATTN_W), F32)
    acc = pltpu.VMEM((SEQ, 128), F32)
    outs = _call(
        "attn_bwd", body, n_steps,
        [slab] * 6 + [_full_spec((SEQ, 1)), _full_spec((1, 128)), _full_spec((1, 128)), _full_spec((1, 128))],
        [ANY] * 3, [out, out, out], (q, k, v, o, lse, do, pos_col, *rot),
        scratch_shapes=[acc, acc, acc, acc, pltpu.VMEM((3, SEQ, 128), F32), pltpu.VMEM((2, 3, SEQ, 128), F32),
                        pltpu.SemaphoreType.DMA((2, 3, RES))])
    return [t.reshape(SEQ, ATTN_W) for t in outs]


def in_bwd(dproj_parts, w_in_t, x, g1, dx2, after=()):
    tm = 512
    k = len(dproj_parts)

    def body(*refs):
        w_ref, x_ref, g_ref, dx2_ref, dx_ref, dg_ref = refs[k:]
        dh1 = _dot(refs[0][...].astype(BF16), w_ref[0:512, :])
        for j in range(1, k):
            dh1 = dh1 + _dot(refs[j][...].astype(BF16), w_ref[512 * j:512 * (j + 1), :])
        dz, dg = _rms_bwd(x_ref[...], g_ref[...], dh1)
        dx_ref[...] = dx2_ref[...] + dz

        @pl.when(pl.program_id(0) == 0)
        def _():
            dg_ref[...] = jnp.zeros_like(dg_ref)

        dg_ref[...] += dg

    return _call(
        "in_bwd", body, SEQ // tm,
        [_row_spec(tm, 512)] * k + [_weight_spec((IN_W, D_MODEL)), _row_spec(tm, D_MODEL), _full_spec((1, D_MODEL)),
                                    _row_spec(tm, D_MODEL)],
        [_row_spec(tm, D_MODEL), _full_spec((1, D_MODEL))],
        [jax.ShapeDtypeStruct((SEQ, D_MODEL), F32), jax.ShapeDtypeStruct((1, D_MODEL), F32)],
        (*dproj_parts, w_in_t, x, g1, dx2), after=after)


def _coords():
    return lax.axis_index("x"), lax.axis_index("y"), lax.axis_index("c")


class Exchange:
    def __init__(self, srcs, bufs, new_shapes, n_sems, make):
        self.srcs, self.bufs, self.new_shapes, self.n_sems, self.make = list(srcs), list(bufs), list(new_shapes), n_sems, make


def _call(name, body, n_steps, in_specs, out_specs, out_shape, args, scratch_shapes=(), after=()):
    n_in = len(args)

    def wrapped(*refs):
        body(*refs[:n_in], *refs[n_in + len(after):])

    return list(pl.pallas_call(
        wrapped, name=name, grid=(n_steps,), in_specs=list(in_specs) + [ANY] * len(after), out_specs=list(out_specs),
        out_shape=list(out_shape), scratch_shapes=list(scratch_shapes), compiler_params=_params(),
    )(*args, *after))


GATHER_SEMS = 8


def gather(bufs):
    n = len(bufs)

    def make(src_refs, buf_refs, new_refs, send_sems, recv_sems):
        x, y, c = _coords()
        me, sibling = (x, y, c), (x, y, 1 - c)
        over_x, over_y, across = (1 - x, y), (x, 1 - y), (1 - x, 1 - y)

        def copy(a, k, block, to, half=None):
            r = buf_refs[a].shape[0] // N_DEV
            lo, size = (0, r) if half is None else (half * (r // 2), r // 2)
            rows = buf_refs[a].at[pl.ds((4 * block[0] + 2 * block[1] + block[2]) * r + lo, size), :]
            return pltpu.make_async_remote_copy(
                src_ref=rows, dst_ref=rows, send_sem=send_sems.at[GATHER_SEMS * a + k],
                recv_sem=recv_sems.at[GATHER_SEMS * a + k], device_id=to, device_id_type=MESH)

        every = range(n)
        out = ([copy(a, 0, me, sibling) for a in every] + [copy(a, 1, me, (*over_x, c)) for a in every]
               + [copy(a, 2, me, (*over_y, c)) for a in every])
        near_in = [copy(a, 1, (*over_x, c), me) for a in every] + [copy(a, 2, (*over_y, c), me) for a in every]
        relay = ([copy(a, 3, (*over_x, c), (*over_y, c), half=0) for a in every]
                 + [copy(a, 4, (*over_y, c), (*over_x, c), half=1) for a in every])
        near_on = [copy(a, 5, (*over_x, c), sibling) for a in every] + [copy(a, 6, (*over_y, c), sibling) for a in every]
        relay_in = ([copy(a, 3, (*across, c), me, half=0) for a in every]
                    + [copy(a, 4, (*across, c), me, half=1) for a in every])
        far_on = [copy(a, 7, (*across, c), sibling) for a in every]
        from_core = ([copy(a, 0, sibling, me) for a in every] + [copy(a, 5, (*over_x, 1 - c), me) for a in every]
                     + [copy(a, 6, (*over_y, 1 - c), me) for a in every] + [copy(a, 7, (*across, 1 - c), me) for a in every])
        stages = [([], out), (near_in, relay + near_on), (relay_in, far_on)]
        return stages, out + relay + near_on + far_on, from_core

    return Exchange([], bufs, [], GATHER_SEMS * n, make)


TO_GATHER = (1, lambda x, y, c: [(x, y, 1 - c), (1 - x, y, c), (x, 1 - y, c)])
TO_SIBLING = (2, lambda x, y, c: [(x, y, 1 - c)])
TO_CHIPS = (3, lambda x, y, c: [(1 - x, y, c), (x, 1 - y, c), (1 - x, 1 - y, c)])
TO_ALL = (4, lambda x, y, c: [(x ^ (m >> 2), y ^ ((m >> 1) & 1), c ^ (m & 1)) for m in range(1, N_DEV)])


def by_sequencer(name, exchanges, who):
    collective_id, peers_of = who
    hbm = pltpu.MemorySpace.HBM
    refs = [([jax.new_ref(a, memory_space=hbm) for a in ex.srcs], [jax.new_ref(a, memory_space=hbm) for a in ex.bufs],
             [jax.empty_ref(s, memory_space=hbm) for s in ex.new_shapes]) for ex in exchanges]
    sems = []
    for ex in exchanges:
        sems += [pltpu.SemaphoreType.DMA((ex.n_sems,)), pltpu.SemaphoreType.DMA((ex.n_sems,))]

    @pl.kernel(mesh=plsc.ScalarSubcoreMesh(axis_name="sequencer", num_cores=1), name=name, scratch_types=tuple(sems),
               compiler_params=pltpu.CompilerParams(collective_id=collective_id))
    def launch(*sem_refs):
        peers = peers_of(*_coords())
        barrier = pltpu.get_barrier_semaphore()
        for peer in peers:
            pl.semaphore_signal(barrier, inc=1, device_id=peer, device_id_type=MESH)
        pl.semaphore_wait(barrier, len(peers))

        made = [ex.make(*refs[k], sem_refs[2 * k], sem_refs[2 * k + 1]) for k, ex in enumerate(exchanges)]
        for stage in range(max(len(stages) for stages, _, _ in made)):
            for stages, _, _ in made:
                if stage < len(stages):
                    arrivals, starts = stages[stage]
                    for cp in arrivals:
                        cp.wait_recv()
                    for cp in starts:
                        cp.start()
        for _, sends, arrivals in made:
            for cp in arrivals:
                cp.wait_recv()
            for cp in sends:
                cp.wait_send()

    launch()
    return [([ref[...] for ref in bufs], [ref[...] for ref in news]) for _, bufs, news in refs]


def place_shards(name, shards, dev):
    n = len(shards)

    def body(dev_ref, *refs):
        for a in range(n):
            refs[n + a][...] = refs[a][...].astype(BF16)

    spec = pltpu.PrefetchScalarGridSpec(
        num_scalar_prefetch=1, grid=(1,),
        in_specs=[pl.BlockSpec(s.shape, lambda i, dev_ref: (0, 0)) for s in shards],
        out_specs=[pl.BlockSpec(s.shape, lambda i, dev_ref: (dev_ref[0], 0)) for s in shards])
    return pl.pallas_call(
        body, name=name, grid_spec=spec,
        out_shape=[jax.ShapeDtypeStruct((N_DEV * s.shape[0], s.shape[1]), BF16) for s in shards],
        compiler_params=_params(),
    )(dev, *shards)


def _swap(copies_of):
    def make(src_refs, buf_refs, new_refs, send_sems, recv_sems):
        copies = copies_of(src_refs, new_refs, send_sems, recv_sems)
        return [([], copies)], copies, copies

    return make


def to_sibling(grads):
    def copies_of(src_refs, new_refs, send_sems, recv_sems):
        x, y, c = _coords()
        return [pltpu.make_async_remote_copy(
            src_ref=src_refs[a].at[2 * xy + 1 - c], dst_ref=new_refs[a].at[xy], send_sem=send_sems.at[4 * a + xy],
            recv_sem=recv_sems.at[4 * a + xy], device_id=(x, y, 1 - c), device_id_type=MESH)
            for a in range(len(src_refs)) for xy in range(4)]

    return Exchange(grads, [], [jax.ShapeDtypeStruct((4,) + g.shape[1:], g.dtype) for g in grads], 4 * len(grads),
                    _swap(copies_of))


def to_chips(parts):
    def copies_of(src_refs, new_refs, send_sems, recv_sems):
        x, y, c = _coords()
        chips = [(1 - x, y), (x, 1 - y), (1 - x, 1 - y)]
        return [pltpu.make_async_remote_copy(
            src_ref=src_refs[a].at[2 * px + py], dst_ref=new_refs[a].at[2 * x + y], send_sem=send_sems.at[3 * a + j],
            recv_sem=recv_sems.at[3 * a + j], device_id=(px, py, c), device_id_type=MESH)
            for a in range(len(src_refs)) for j, (px, py) in enumerate(chips)]

    return Exchange(parts, [], [jax.ShapeDtypeStruct(p.shape, p.dtype) for p in parts], 3 * len(parts), _swap(copies_of))


def to_owners(grad):
    def copies_of(src_refs, new_refs, send_sems, recv_sems):
        x, y, c = _coords()
        copies = []
        for m in range(1, N_DEV):
            px, py, pc = x ^ (m >> 2), y ^ ((m >> 1) & 1), c ^ (m & 1)
            copies.append(pltpu.make_async_remote_copy(
                src_ref=src_refs[0].at[4 * px + 2 * py + pc], dst_ref=new_refs[0].at[4 * x + 2 * y + c],
                send_sem=send_sems.at[m - 1], recv_sem=recv_sems.at[m - 1], device_id=(px, py, pc), device_id_type=MESH))
        return copies

    return Exchange([grad], [], [jax.ShapeDtypeStruct(grad.shape, grad.dtype)], N_DEV - 1, _swap(copies_of))


def to_everyone(vec):
    def copies_of(src_refs, new_refs, send_sems, recv_sems):
        x, y, c = _coords()
        copies = []
        for m in range(1, N_DEV):
            px, py, pc = x ^ (m >> 2), y ^ ((m >> 1) & 1), c ^ (m & 1)
            copies.append(pltpu.make_async_remote_copy(
                src_ref=src_refs[0], dst_ref=new_refs[0].at[4 * x + 2 * y + c],
                send_sem=send_sems.at[m - 1], recv_sem=recv_sems.at[m - 1], device_id=(px, py, pc), device_id_type=MESH))
        return copies

    return Exchange([vec], [], [jax.ShapeDtypeStruct((N_DEV,) + vec.shape, vec.dtype)], N_DEV - 1, _swap(copies_of))


def sum_cores(name, grads, others, core, after=()):
    k = len(grads)

    def body(core_ref, *refs):
        for j in range(k):
            out_ref = refs[2 * k + len(after) + j]
            out_ref[...] = (refs[j][:, 0].astype(F32) + refs[k + j][...].astype(F32)).astype(out_ref.dtype)

    mine = [pl.BlockSpec((2, 1) + o.shape[1:], lambda i, core_ref: (i, core_ref[0], 0, 0)) for o in others]
    theirs = [pl.BlockSpec((2,) + o.shape[1:], lambda i, core_ref: (i, 0, 0)) for o in others]
    return pl.pallas_call(
        body, name=name,
        grid_spec=pltpu.PrefetchScalarGridSpec(
            num_scalar_prefetch=1, grid=(2,), in_specs=mine + theirs + [ANY] * len(after), out_specs=theirs),
        out_shape=[jax.ShapeDtypeStruct(o.shape, o.dtype) for o in others],
        compiler_params=_params(),
    )(core, *[g.reshape((4, 2) + g.shape[1:]) for g in grads], *others, *after)


def sum_owned(name, grad, others, dev_ids, after=()):
    _, r, w = grad.shape

    def body(ids_ref, *refs):
        acc = refs[0][0]
        for k in range(1, N_DEV):
            acc = acc + refs[k][0]
        refs[-1][...] = acc

    def pick(k):
        return pl.BlockSpec((1, r, w), lambda i, ids_ref: (ids_ref[k], 0, 0))

    return pl.pallas_call(
        body, name=name,
        grid_spec=pltpu.PrefetchScalarGridSpec(
            num_scalar_prefetch=1, grid=(1,), in_specs=[pick(k) for k in range(N_DEV)] + [ANY] * len(after),
            out_specs=pl.BlockSpec((r, w), lambda i, ids_ref: (ids_ref[0], 0))),
        out_shape=jax.ShapeDtypeStruct((N_DEV * r, w), F32),
        compiler_params=_params(),
    )(dev_ids, grad, *([others] * (N_DEV - 1)), *after)


def _adamw_update(w, g, m, v):
    nm = ADAM_B1 * m + np.float32(1.0 - ADAM_B1) * g
    nv = ADAM_B2 * v + np.float32(1.0 - ADAM_B2) * (g * g)
    m_hat = nm / np.float32(1.0 - ADAM_B1 ** ADAM_STEP)
    v_hat = nv / np.float32(1.0 - ADAM_B2 ** ADAM_STEP)
    return -ADAM_LR * (m_hat / (jnp.sqrt(v_hat) + ADAM_EPS) + ADAM_WD * w), nm, nv


def adamw_of_sums(name, parts, others, chip_ids, ws, ms, vs, after):
    n = len(parts)
    halves = 2

    def body(ids_ref, *refs):
        outs = refs[7 * n + len(after):]
        for j in range(n):
            p_ref, a_ref, b_ref, c_ref, w_ref, m_ref, v_ref = refs[7 * j:7 * j + 7]
            g = ((p_ref[0].astype(F32) + a_ref[0].astype(F32)) + b_ref[0].astype(F32)) + c_ref[0].astype(F32)
            outs[4 * j][...] = g
            outs[4 * j + 1][...], outs[4 * j + 2][...], outs[4 * j + 3][...] = _adamw_update(w_ref[...], g, m_ref[...], v_ref[...])

    in_specs, out_specs, out_shape, operands = [], [], [], []
    for part, other, w, m, v in zip(parts, others, ws, ms, vs):
        _, r, wd = part.shape
        rows = r // halves
        whole = pl.BlockSpec((rows, wd), lambda i, ids_ref: (i, 0))
        in_specs += [pl.BlockSpec((1, rows, wd), lambda i, ids_ref, k=k: (ids_ref[k], i, 0)) for k in range(4)] + [whole] * 3
        out_specs += [whole] * 4
        out_shape += [jax.ShapeDtypeStruct((r, wd), F32)] * 4
        operands += [part, other, other, other, w, m, v]
    outs = pl.pallas_call(
        body, name=name,
        grid_spec=pltpu.PrefetchScalarGridSpec(
            num_scalar_prefetch=1, grid=(halves,), in_specs=in_specs + [ANY] * len(after), out_specs=out_specs),
        out_shape=out_shape,
        compiler_params=_params(),
    )(chip_ids, *operands, *after)
    return [tuple(outs[4 * j:4 * j + 4]) for j in range(n)]


def pack_small(parts):
    names = [name for name, _ in SMALL if name in parts]
    operands = [parts[name] for name in names]
    first_row, at = {}, 0
    for name, size in SMALL:
        first_row[name] = at // 128
        at += size
    sizes = dict(SMALL)

    def body(*refs):
        out_ref = refs[-1]
        out_ref[...] = jnp.zeros_like(out_ref)
        for name, ref in zip(names, refs):
            row = first_row[name]
            if name == "loss_sum":
                lane0 = lax.broadcasted_iota(jnp.int32, (1, 128), 1) == 0
                out_ref[row:row + 1, :] = jnp.where(lane0, ref[...], 0.0)
            else:
                rows = sizes[name] // 128
                out_ref[row:row + rows, :] = ref[...].reshape(rows, 128)

    vmem = pl.BlockSpec(memory_space=pltpu.VMEM)
    return pl.pallas_call(
        body, name="pack_small", in_specs=[vmem] * len(names), out_specs=vmem,
        out_shape=jax.ShapeDtypeStruct((SMALL_ROWS, 128), F32), compiler_params=_params(()),
    )(*operands)


LATE = "pre_mix_norm"


def adamw_small(packed_g, late, ws, ms, vs, after=()):
    names = [LATE] if late else [name for name, _ in SMALL if name not in ("loss_sum", LATE)]
    k = len(names)
    shapes = [ws[name].shape[1:] if ws[name].ndim > 2 else ws[name].shape for name in names]
    first_row, at = {}, 0
    for name, size in SMALL:
        first_row[name] = at // 128
        at += size
    sizes = dict(SMALL)
    n_in = 3 if late else 1

    def body(*refs):
        w_refs, m_refs, v_refs = (refs[n_in + j * k:n_in + (j + 1) * k] for j in range(3))
        outs = refs[n_in + 3 * k + len(after):]
        for i, name in enumerate(names):
            if late:
                late_ref, own_ref, dev_ref = refs[:n_in]
                parts = [jnp.where(dev_ref[0] == j, own_ref[...], late_ref[j]) for j in range(N_DEV)]
                g = parts[0]
                for j in range(1, N_DEV):
                    g = g + parts[j]
            else:
                g = refs[0][first_row[name]:first_row[name] + sizes[name] // 128, :].reshape(shapes[i])
            outs[i][...] = g
            outs[k + i][...], outs[2 * k + i][...], outs[3 * k + i][...] = _adamw_update(
                w_refs[i][...], g, m_refs[i][...], v_refs[i][...])
        if not late:
            outs[4 * k][...] = refs[0][first_row["loss_sum"]:first_row["loss_sum"] + 1, 0:1]

    vmem = pl.BlockSpec(memory_space=pltpu.VMEM)
    operands = [t[name].reshape(shape) for t in (ws, ms, vs) for name, shape in zip(names, shapes)]
    outs = pl.pallas_call(
        body, name="adamw_late" if late else "adamw_small",
        in_specs=([vmem, vmem, pl.BlockSpec(memory_space=pltpu.SMEM)] if late else [vmem]) + [vmem] * (3 * k) + [ANY] * len(after),
        out_specs=[vmem] * (4 * k + (0 if late else 1)),
        out_shape=[jax.ShapeDtypeStruct(shape, F32) for _ in range(4) for shape in shapes]
        + ([] if late else [jax.ShapeDtypeStruct((1, 1), F32)]),
        compiler_params=_params(()),
    )(*(late or (packed_g,)), *operands, *after)
    tables = [{name: outs[j * k + i].reshape(ws[name].shape) for i, name in enumerate(names)} for j in range(4)]
    return (*tables, None if late else outs[4 * k])


def kernel(x, positions, pre_mix_norm, w_in, sgu_ln_gain, sgu_ln_bias, sgu_w_spatial, sgu_b_spatial, attn_out_norm, sgu_out_norm, w_out, post_mix_norm, pre_ffn_norm, w_gate, w_up, w_down, post_ffn_norm, loss_target, m_pre_mix_norm, m_w_in, m_sgu_ln_gain, m_sgu_ln_bias, m_sgu_w_spatial, m_sgu_b_spatial, m_attn_out_norm, m_sgu_out_norm, m_w_out, m_post_mix_norm, m_pre_ffn_norm, m_w_gate, m_w_up, m_w_down, m_post_ffn_norm, v_pre_mix_norm, v_w_in, v_sgu_ln_gain, v_sgu_ln_bias, v_sgu_w_spatial, v_sgu_b_spatial, v_attn_out_norm, v_sgu_out_norm, v_w_out, v_post_mix_norm, v_pre_ffn_norm, v_w_gate, v_w_up, v_w_down, v_post_ffn_norm):
    small_w = dict(pre_mix_norm=pre_mix_norm, sgu_ln_gain=sgu_ln_gain, sgu_ln_bias=sgu_ln_bias, sgu_w_spatial=sgu_w_spatial,
                   sgu_b_spatial=sgu_b_spatial, attn_out_norm=attn_out_norm, sgu_out_norm=sgu_out_norm,
                   post_mix_norm=post_mix_norm, pre_ffn_norm=pre_ffn_norm, post_ffn_norm=post_ffn_norm)
    small_m = dict(pre_mix_norm=m_pre_mix_norm, sgu_ln_gain=m_sgu_ln_gain, sgu_ln_bias=m_sgu_ln_bias, sgu_w_spatial=m_sgu_w_spatial,
                   sgu_b_spatial=m_sgu_b_spatial, attn_out_norm=m_attn_out_norm, sgu_out_norm=m_sgu_out_norm,
                   post_mix_norm=m_post_mix_norm, pre_ffn_norm=m_pre_ffn_norm, post_ffn_norm=m_post_ffn_norm)
    small_v = dict(pre_mix_norm=v_pre_mix_norm, sgu_ln_gain=v_sgu_ln_gain, sgu_ln_bias=v_sgu_ln_bias, sgu_w_spatial=v_sgu_w_spatial,
                   sgu_b_spatial=v_sgu_b_spatial, attn_out_norm=v_attn_out_norm, sgu_out_norm=v_sgu_out_norm,
                   post_mix_norm=v_post_mix_norm, pre_ffn_norm=v_pre_ffn_norm, post_ffn_norm=v_post_ffn_norm)

    x2d = x[0]
    target = loss_target[0]
    pos_col = positions.reshape(SEQ, 1)
    rot = _rot_consts()
    w_sp = sgu_w_spatial[0]
    bfull = jnp.repeat(sgu_b_spatial[0].T, HEAD_DIM, axis=1)

    x_i, y_i, c_i = (lax.axis_index(a).astype(jnp.int32) for a in MESH_AXES)
    dev = 4 * x_i + 2 * y_i + c_i
    core = c_i.reshape(1)
    chip = 2 * x_i + y_i
    chip_ids = jnp.stack([chip, chip ^ 1, chip ^ 2, chip ^ 3])
    dev_ids = jnp.stack([dev ^ m for m in range(N_DEV)])

    def gathered(name, bufs):
        return by_sequencer(name, [gather(bufs)], TO_GATHER)[0][0]

    def from_sibling(name, grads):
        return by_sequencer(name, [to_sibling(grads)], TO_SIBLING)[0][1]

    def from_chips(name, parts):
        return by_sequencer(name, [to_chips(parts)], TO_CHIPS)[0][1]

    (w_in_t,) = place_shards("place_w_in", [w_in[0].T], dev.reshape(1))
    (w_in_t,) = gathered("gather_w_in", [w_in_t])
    w_gate_t, w_up_t, w_out_f, w_down_f = place_shards(
        "place_weights", [w_gate[0].T, w_up[0].T, w_out[0], w_down[0]], dev.reshape(1))
    (w_out_f,) = gathered("gather_w_out", [w_out_f])
    w_gate_t, w_up_t = gathered("gather_w_gate_up", [w_gate_t, w_up_t])
    (w_down_f,) = gathered("gather_w_down", [w_down_f])

    h1 = pre_norm(x2d, pre_mix_norm)
    u, vs, q, k, v = in_proj(h1, pos_col, w_in_t, rot)
    attn_r, lse, attn = attn_fwd(q, k, v)
    (sgu,) = sgu_fwd(u, vs, sgu_ln_gain, sgu_ln_bias, w_sp, bfull)
    mix, y, x2, h2 = out_proj(attn, sgu, x2d, attn_out_norm, sgu_out_norm, w_out_f, post_mix_norm, pre_ffn_norm)
    gate, up, act = ffn_up(h2, w_gate_t, w_up_t)
    df, dx3, d_post_ffn, sq_err = ffn_down_loss(act, w_down_f, x2, post_ffn_norm, target)

    g_w_down = weight_grad("grad_w_down", act, df)
    (s_down,) = from_sibling("w_down_to_sibling", [g_w_down])
    dgate, dup, dx2, dy, d_pre_ffn, d_post_mix = ffn_bwd(
        df, w_down_f, gate, up, w_gate_t, w_up_t, x2, pre_ffn_norm, dx3, y, post_mix_norm, after=[g_w_down])
    (p_down,) = sum_cores("sum_cores_down", [g_w_down], [s_down], core, after=[dy])
    (c_down,) = from_chips("w_down_to_chips", [p_down])
    g_w_gate, g_w_up = weight_grads("grad_w_gate_up", [dgate, dup], h2, after=[p_down])
    s_gate, s_up = from_sibling("w_gate_up_to_sibling", [g_w_gate, g_w_up])
    g_w_out = weight_grad("grad_w_out", mix, dy, after=[g_w_up])
    (s_out,) = from_sibling("w_out_to_sibling", [g_w_out])
    dsgu, d_attn_out, d_sgu_out, dattn_r = mix_bwd(dy, w_out_f, attn, sgu, attn_out_norm, sgu_out_norm, after=[g_w_out, c_down])
    p_gate, p_up = sum_cores("sum_cores_gate_up", [g_w_gate, g_w_up], [s_gate, s_up], core, after=[dsgu])
    c_gate, c_up = from_chips("w_gate_up_to_chips", [p_gate, p_up])
    du, dvs, d_ln_gain, d_ln_bias, d_w_sp, d_bfull = sgu_bwd(
        u, vs, dsgu, sgu_ln_gain, sgu_ln_bias, w_sp, bfull, after=[p_gate, p_up])

    d_b_sp = d_bfull.reshape(CHUNK, N_GROUPS, HEAD_DIM).sum(axis=-1).T
    small_g = pack_small(dict(sgu_ln_gain=d_ln_gain, sgu_ln_bias=d_ln_bias, sgu_w_spatial=d_w_sp, sgu_b_spatial=d_b_sp,
                              attn_out_norm=d_attn_out, sgu_out_norm=d_sgu_out, post_mix_norm=d_post_mix,
                              pre_ffn_norm=d_pre_ffn, post_ffn_norm=d_post_ffn, loss_sum=sq_err))
    small_g = small_g.reshape(N_DEV, SMALL_ROWS // N_DEV, 128)
    ((_, (o_small,)),) = by_sequencer("small_to_owners", [to_owners(small_g)], TO_ALL)

    dq, dk, dv = attn_bwd(q, k, v, attn_r, lse, dattn_r, _to_residue_order(pos_col), rot)
    summed_small = sum_owned("sum_small", small_g, o_small, dev_ids, after=[dq, c_gate, c_up])
    (all_small,) = gathered("gather_small_grads", [summed_small])
    (p_out,) = sum_cores("sum_cores_out", [g_w_out], [s_out], core, after=[dq])
    (c_out,) = from_chips("w_out_to_chips", [p_out])
    dproj = [dq, dk, dv, du, dvs]
    g_w_in = weight_grad_of_parts("grad_w_in", dproj, h1, after=[c_gate, c_up])
    (s_in,) = from_sibling("w_in_to_sibling", [g_w_in])

    def same(t):
        return t

    def turned(t):
        return t.T

    big = {}

    def adamw(call, weights, after):
        results = adamw_of_sums(call, [p for _, _, p, _, _, _, _ in weights], [c for _, _, _, c, _, _, _ in weights], chip_ids,
                                [turn(w[0]) for _, w, _, _, _, _, turn in weights], [turn(m[0]) for _, _, _, _, m, _, turn in weights],
                                [turn(vv[0]) for _, _, _, _, _, vv, turn in weights], after)
        for (name, _, _, _, _, _, turn), outs in zip(weights, results):
            big[name] = tuple(turn(t)[None] for t in outs)
        return results[-1][0]

    done_ffn = adamw("adamw_ffn", (("w_down", w_down, p_down, c_down, m_w_down, v_w_down, same),
                                   ("w_gate", w_gate, p_gate, c_gate, m_w_gate, v_w_gate, turned),
                                   ("w_up", w_up, p_up, c_up, m_w_up, v_w_up, turned)), [g_w_in, all_small])
    (p_in,) = sum_cores("sum_cores_in", [g_w_in], [s_in], core, after=[done_ffn, c_out])
    (c_in,) = from_chips("w_in_to_chips", [p_in])
    grad_x, d_pre_mix = in_bwd(dproj, w_in_t, x2d, pre_mix_norm, dx2, after=[p_in])
    ((_, (late_parts,)),) = by_sequencer("pre_mix_to_everyone", [to_everyone(d_pre_mix)], TO_ALL)
    done_out = adamw("adamw_w_out", (("w_out", w_out, p_out, c_out, m_w_out, v_w_out, same),), [d_pre_mix])
    *early, loss_sum = adamw_small(all_small, None, small_w, small_m, small_v, after=[done_out])
    done_in = adamw("adamw_w_in", (("w_in", w_in, p_in, c_in, m_w_in, v_w_in, turned),), [loss_sum])
    *late, _ = adamw_small(None, (late_parts, d_pre_mix, dev.reshape(1)), small_w, small_m, small_v, after=[done_in])
    sg, sd, snm, snv = ({**a, **b} for a, b in zip(early, late))
    loss = loss_sum[0, 0] * np.float32(0.5 / D_MODEL)

    names = ["pre_mix_norm", "w_in", "sgu_ln_gain", "sgu_ln_bias", "sgu_w_spatial", "sgu_b_spatial", "attn_out_norm",
             "sgu_out_norm", "w_out", "post_mix_norm", "pre_ffn_norm", "w_gate", "w_up", "w_down", "post_ffn_norm"]
    outs = [loss, grad_x[None]]
    for i, table in enumerate((sg, sd, snm, snv)):
        for name in names:
            outs.append(big[name][i] if name in big else table[name])
    return tuple(outs)
```

```python
import numpy as np
import jax
import jax.numpy as jnp
from jax import lax
from jax.experimental import pallas as pl
from jax.experimental.pallas import tpu as pltpu
from jax.experimental.pallas import tpu_sc as plsc

F32 = jnp.float32
BF16 = jnp.bfloat16

SEQ = 2048
D_MODEL = 1024
ATTN_W = 512
SGU_W = 512
HEAD_DIM = 64
N_GROUPS = 8
CHUNK = 128
D_FF = 2816
IN_W = 3 * ATTN_W + 2 * SGU_W
DILATIONS = (1, 4, 16)
ROPE_THETA = 500000.0
ROT_DIM = 16
ROT_HALF = 8
RMS_EPS = 1e-6
LN_EPS = 1e-5
Q_SCALE = 0.125
NEG = -1e30

N_DEV = 8
MESH_AXES = ("x", "y", "c")
MESH = pl.DeviceIdType.MESH

ADAM_LR = 0.001
ADAM_B1 = 0.9
ADAM_B2 = 0.999
ADAM_EPS = 1e-08
ADAM_WD = 0.01
ADAM_STEP = 10

VMEM_LIMIT = 60 * 1024 * 1024
ANY = pl.BlockSpec(memory_space=pl.ANY)

SMALL = (("pre_mix_norm", 1024), ("sgu_ln_gain", 512), ("sgu_ln_bias", 512), ("sgu_w_spatial", 8 * 128 * 128),
         ("sgu_b_spatial", 1024), ("attn_out_norm", 512), ("sgu_out_norm", 512), ("post_mix_norm", 1024),
         ("pre_ffn_norm", 1024), ("post_ffn_norm", 1024), ("loss_sum", 1))
SMALL_ROWS = 1152


def _params(sem=("arbitrary",)):
    return pltpu.CompilerParams(dimension_semantics=sem, vmem_limit_bytes=VMEM_LIMIT)


def _dot(a, b):
    return jnp.dot(a, b, preferred_element_type=F32)


def _dot_nt(a, b):
    return lax.dot_general(a, b, (((1,), (1,)), ((), ())), preferred_element_type=F32)


def _dot_tn(a, b):
    return lax.dot_general(a, b, (((0,), (0,)), ((), ())), preferred_element_type=F32)


def _rms(z):
    return lax.rsqrt(jnp.mean(z * z, axis=-1, keepdims=True) + RMS_EPS)


def _rms_bwd(z, gain, d):
    r = _rms(z)
    n = z * r
    dn = d * gain
    dz = r * (dn - n * jnp.mean(dn * n, axis=-1, keepdims=True))
    return dz, jnp.sum(d * n, axis=0, keepdims=True)


def _gelu(z):
    return 0.5 * z * (1.0 + lax.erf(z * np.float32(1.0 / np.sqrt(2.0))))


def _gelu_grad(z):
    cdf = 0.5 * (1.0 + lax.erf(z * np.float32(1.0 / np.sqrt(2.0))))
    return cdf + z * jnp.exp(-0.5 * z * z) * np.float32(1.0 / np.sqrt(2.0 * np.pi))


def _rot_tables(pos_col, invf, ma, mb):
    ang = pos_col.astype(F32) * invf
    s = jnp.sin(ang)
    return jnp.cos(ang), s * ma, s * mb


def _rot(t, c, sa, sb):
    return t * c + pltpu.roll(t, 120, 1) * sa + pltpu.roll(t, 8, 1) * sb


def _rot_t(d, c, sa, sb):
    return d * c + pltpu.roll(d * sa, 8, 1) + pltpu.roll(d * sb, 120, 1)


def _rot_consts():
    lane = np.arange(128) % HEAD_DIM
    inv_freq = (np.float32(ROPE_THETA) ** (-np.arange(0, ROT_DIM, 2, dtype=np.float32) / np.float32(ROT_DIM))).astype(np.float32)
    invf = np.where(lane < ROT_DIM, inv_freq[lane % ROT_HALF], 0.0).astype(np.float32)
    ma = np.where(lane < ROT_HALF, -1.0, 0.0).astype(np.float32)
    mb = np.where((lane >= ROT_HALF) & (lane < ROT_DIM), 1.0, 0.0).astype(np.float32)
    return jnp.asarray(invf[None]), jnp.asarray(ma[None]), jnp.asarray(mb[None])


def _row_spec(tm, w):
    return pl.BlockSpec((tm, w), lambda i: (i, 0))


def _full_spec(shape):
    return pl.BlockSpec(shape, lambda i: (0,) * len(shape))


def _weight_spec(shape):
    return pl.BlockSpec(shape, lambda i: (0,) * len(shape), pipeline_mode=pl.Buffered(1))


FF_CHUNKS = (256, 512, 1024, 1024)
FF_SPANS = [(int(o), n) for o, n in zip(np.cumsum((0,) + FF_CHUNKS[:-1]), FF_CHUNKS)]
FF_WHOLE = [(0, D_FF)]


def _ffn_weight_scratch(n_weights, spans):
    return [pltpu.VMEM((D_FF, D_MODEL), BF16)] * n_weights + [pltpu.SemaphoreType.DMA((n_weights, len(spans)))]


def _with_ffn_weights(w_hbm, w_vmem, sems, spans, run):
    copies = [[pltpu.make_async_copy(h.at[pl.ds(o, n)], v.at[pl.ds(o, n)], sems.at[j, c]) for c, (o, n) in enumerate(spans)]
              for j, (h, v) in enumerate(zip(w_hbm, w_vmem))]
    first = pl.program_id(0) == 0

    @pl.when(first)
    def _():
        for of_weight in copies:
            for cp in of_weight:
                cp.start()
        run(lambda j, c: copies[j][c].wait())

    @pl.when(jnp.logical_not(first))
    def _():
        run(None)


RES = 16


def _residue_scratch(n_arrays, tm, width):
    return [pltpu.VMEM((2, n_arrays, tm // RES, RES, width), F32), pltpu.SemaphoreType.DMA((2, n_arrays, RES))]


def _to_residue_rows(tiles, outs, scratch, sems, tm, n_steps):
    i = pl.program_id(0)
    slot = i % 2
    per = tm // RES

    def copies(step, s):
        return [pltpu.make_async_copy(scratch.at[s, a, :, b, :],
                                      outs[a].at[pl.ds(pl.multiple_of(b * (SEQ // RES) + per * step, per), per), :],
                                      sems.at[s, a, b]) for a in range(len(outs)) for b in range(RES)]

    @pl.when(i >= 2)
    def _():
        for cp in copies(i - 2, slot):
            cp.wait()

    for a, tile in enumerate(tiles):
        scratch[slot, a] = tile.reshape(per, RES, tile.shape[-1])
    for cp in copies(i, slot):
        cp.start()

    @pl.when(i == n_steps - 1)
    def _():
        for cp in copies(i - 1, 1 - slot) + copies(i, slot):
            cp.wait()


def pre_norm(x, g1, after=()):
    tm = 512

    def body(x_ref, g_ref, h_ref):
        xf = x_ref[...]
        h_ref[...] = (xf * _rms(xf) * g_ref[...]).astype(BF16)

    return _call("pre_norm", body, SEQ // tm, [_row_spec(tm, D_MODEL), _full_spec((1, D_MODEL))], [_row_spec(tm, D_MODEL)],
                 [jax.ShapeDtypeStruct((SEQ, D_MODEL), BF16)], (x, g1), after=after)[0]


def in_proj(h1, pos_col, w_in_t, rot):
    tm = 512
    n_steps = SEQ // tm

    def body(h_ref, pos_ref, w_ref, invf_ref, ma_ref, mb_ref, u_ref, vs_ref, q_ref, k_ref, v_ref, scratch, sems):
        proj = _dot_nt(h_ref[...], w_ref[...])
        c, sa, sb = _rot_tables(pos_ref[...], invf_ref[...], ma_ref[...], mb_ref[...])
        slabs = range(ATTN_W // 128)
        q = jnp.concatenate([_rot(proj[:, j * 128:(j + 1) * 128], c, sa, sb) * Q_SCALE for j in slabs], axis=1)
        k = jnp.concatenate([_rot(proj[:, ATTN_W + j * 128:ATTN_W + (j + 1) * 128], c, sa, sb) for j in slabs], axis=1)
        u_ref[...] = proj[:, 3 * ATTN_W:3 * ATTN_W + SGU_W]
        vs_ref[...] = proj[:, 3 * ATTN_W + SGU_W:]
        _to_residue_rows([q, k, proj[:, 2 * ATTN_W:3 * ATTN_W]], [q_ref, k_ref, v_ref], scratch, sems, tm, n_steps)

    act = jax.ShapeDtypeStruct((SEQ, 512), F32)
    return _call(
        "in_proj", body, n_steps,
        [_row_spec(tm, D_MODEL), _row_spec(tm, 1), _weight_spec((IN_W, D_MODEL)),
         _full_spec((1, 128)), _full_spec((1, 128)), _full_spec((1, 128))],
        [_row_spec(tm, 512)] * 2 + [ANY] * 3, [act] * 5,
        (h1, pos_col, w_in_t, *rot), scratch_shapes=_residue_scratch(3, tm, ATTN_W))


def _to_residue_order(t):
    return t.reshape(SEQ // RES, RES, -1).transpose(1, 0, 2).reshape(t.shape)


def _block_rows(d, r, n):
    if d == 16:
        slices = [(128 * r, 128)]
    elif d == 4:
        slices = [(128 * (4 * b + r) + 32 * n, 32) for b in range(4)]
    else:
        slices = [(128 * b + 8 * n, 8) for b in range(RES)]
    return [(s if isinstance(s, int) else pl.multiple_of(s, z), z) for s, z in slices]


def _block_step(d, i):
    if d == 16:
        return i
    if d == 4:
        return 4 * (i & 31) + (i >> 5)
    return 16 * (i & 7) + (i >> 3)


def _attn_masks(d):
    row2 = _block_step(d, lax.broadcasted_iota(jnp.int32, (128, 256), 0))
    col2 = lax.broadcasted_iota(jnp.int32, (128, 256), 1)
    key2 = _block_step(d, col2 & 127)
    mask2 = jnp.logical_or(jnp.logical_and(col2 < 128, key2 >= row2), jnp.logical_and(col2 >= 128, key2 <= row2))
    row1 = _block_step(d, lax.broadcasted_iota(jnp.int32, (128, 128), 0))
    col1 = lax.broadcasted_iota(jnp.int32, (128, 128), 1)
    return col1 < HEAD_DIM, _block_step(d, col1) <= row1, mask2


def _load_rows(ref, slices):
    parts = [ref[pl.ds(s, z), :] for s, z in slices]
    return parts[0] if len(parts) == 1 else jnp.concatenate(parts, axis=0)


def _for_each_group(fn):
    for p, d in enumerate(DILATIONS):
        masks = _attn_masks(d)
        if d == 16:
            def group(i, carry, p=p, masks=masks):
                fn(p, masks, [(_block_rows(16, 8 * i + g, 0), None) for g in range(8)])
                return carry

            lax.fori_loop(0, 2, group, 0)
        elif d == 4:
            fn(p, masks, [(_block_rows(4, r, 0), None) for r in range(4)])

            def group(i, carry, p=p, masks=masks):
                blocks = [6 * i + g for g in range(6)]
                fn(p, masks, [(_block_rows(4, j % 4, 1 + j // 4), _block_rows(4, j % 4, j // 4)) for j in blocks])
                return carry

            lax.fori_loop(0, 2, group, 0)
        else:
            fn(p, masks, [(_block_rows(1, 0, 0), None)])

            def group(i, carry, p=p, masks=masks):
                fn(p, masks, [(_block_rows(1, 0, 5 * i + g + 1), _block_rows(1, 0, 5 * i + g)) for g in range(5)])
                return carry

            lax.fori_loop(0, 3, group, 0)


def attn_fwd(q, k, v):
    def body(q_ref, k_ref, v_ref, o_ref, lse_ref, nat_ref, op_ref, lp_ref, sems):
        def group(p, masks, blocks):
            head0, mask1, mask2 = masks
            heads = (head0, jnp.logical_not(head0))
            keys = [rows if prev is None else prev + rows for rows, prev in blocks]
            mask = [mask1 if prev is None else mask2 for _, prev in blocks]
            qb = [_load_rows(q_ref, rows) for rows, _ in blocks]
            kk = [_load_rows(k_ref, ks).astype(BF16) for ks in keys]
            vv = [_load_rows(v_ref, ks).astype(BF16) for ks in keys]
            chains = [(g, hm) for g in range(len(blocks)) for hm in heads]
            s = [jnp.where(mask[g], _dot_nt(jnp.where(hm, qb[g], 0.0).astype(BF16), kk[g]), NEG) for g, hm in chains]
            m = [jnp.max(t, axis=-1, keepdims=True) for t in s]
            e = [jnp.exp(t - mt) for t, mt in zip(s, m)]
            l = [jnp.sum(t, axis=-1, keepdims=True) for t in e]
            pv = [_dot(t.astype(BF16), vv[g]) for t, (g, _) in zip(e, chains)]
            for g, (rows, _) in enumerate(blocks):
                o_blk = jnp.where(head0, pv[2 * g] / l[2 * g], pv[2 * g + 1] / l[2 * g + 1])
                l_blk = jnp.where(head0, jnp.broadcast_to(m[2 * g] + jnp.log(l[2 * g]), (128, 128)),
                                  jnp.broadcast_to(m[2 * g + 1] + jnp.log(l[2 * g + 1]), (128, 128)))
                at = 0
                for start, size in rows:
                    op_ref[p, pl.ds(start, size), :] = o_blk[at:at + size]
                    lp_ref[p, pl.ds(start, size), :] = l_blk[at:at + size]
                    at += size

        _for_each_group(group)

        def combine(i, carry):
            rows = pl.ds(pl.multiple_of(i * 256, 256), 256)
            ls = [lp_ref[p, rows, :] for p in range(3)]
            m = jnp.maximum(jnp.maximum(ls[0], ls[1]), ls[2])
            lse = m + jnp.log(jnp.exp(ls[0] - m) + jnp.exp(ls[1] - m) + jnp.exp(ls[2] - m))
            o = jnp.zeros((256, 128), F32)
            for p in range(3):
                o = o + jnp.exp(ls[p] - lse) * op_ref[p, rows, :]
            o_ref[rows, :] = o
            lse_ref[rows, :] = lse
            return carry

        lax.fori_loop(0, SEQ // 256, combine, 0)

        lanes = pl.ds(pl.multiple_of(pl.program_id(0) * 128, 128), 128)
        back = [pltpu.make_async_copy(o_ref.at[pl.ds(b * (SEQ // RES), SEQ // RES), :], nat_ref.at[:, b, lanes], sems.at[b])
                for b in range(RES)]
        for cp in back:
            cp.start()
        for cp in back:
            cp.wait()

    slab = pl.BlockSpec((SEQ, 128), lambda i: (0, i))
    out = jax.ShapeDtypeStruct((SEQ, ATTN_W), F32)
    attn_r, lse, attn = _call(
        "attn_fwd", body, ATTN_W // 128, [slab] * 3, [slab] * 2 + [ANY],
        [out, out, jax.ShapeDtypeStruct((SEQ // RES, RES, ATTN_W), F32)], (q, k, v),
        scratch_shapes=[pltpu.VMEM((3, SEQ, 128), F32), pltpu.VMEM((3, SEQ, 128), F32), pltpu.SemaphoreType.DMA((RES,))])
    return attn_r, lse, attn.reshape(SEQ, ATTN_W)


def _causal_weights(w_ref):
    row = lax.broadcasted_iota(jnp.int32, (CHUNK, CHUNK), 0)
    col = lax.broadcasted_iota(jnp.int32, (CHUNK, CHUNK), 1)
    return [jnp.where(col <= row, w_ref[g], 0.0).astype(BF16) for g in range(N_GROUPS)], col <= row


def _sgu_chunk_fwd(u, vs, lg, lb, wc, bfull, head0):
    ug = _gelu(u)
    vg = _gelu(vs)
    xc = vg - jnp.mean(vg, axis=-1, keepdims=True)
    rstd = lax.rsqrt(jnp.mean(xc * xc, axis=-1, keepdims=True) + LN_EPS)
    xhat = xc * rstd
    vn = xhat * lg + lb
    mixed = []
    for gp in range(SGU_W // 128):
        vp = vn[:, gp * 128:(gp + 1) * 128].astype(BF16)
        mixed.append(jnp.where(head0, _dot(wc[2 * gp], vp), _dot(wc[2 * gp + 1], vp)))
    ms = jnp.concatenate(mixed, axis=1) + bfull
    return ug, xhat, rstd, vn, ms


def sgu_fwd(u, vs, lg, lb, w_sp, bfull):
    cpb = 4

    def body(u_ref, vs_ref, lg_ref, lb_ref, w_ref, b_ref, o_ref):
        wc, _ = _causal_weights(w_ref)
        head0 = lax.broadcasted_iota(jnp.int32, (CHUNK, 128), 1) < HEAD_DIM
        for ci in range(cpb):
            rows = pl.ds(ci * CHUNK, CHUNK)
            ug, _, _, _, ms = _sgu_chunk_fwd(u_ref[rows, :], vs_ref[rows, :], lg_ref[...], lb_ref[...], wc, b_ref[...], head0)
            o_ref[rows, :] = ug * ms

    tm = cpb * CHUNK
    return _call(
        "sgu_fwd", body, SEQ // tm,
        [_row_spec(tm, SGU_W), _row_spec(tm, SGU_W), _full_spec((1, SGU_W)), _full_spec((1, SGU_W)),
         _full_spec((N_GROUPS, CHUNK, CHUNK)), _full_spec((CHUNK, SGU_W))],
        [_row_spec(tm, SGU_W)], [jax.ShapeDtypeStruct((SEQ, SGU_W), F32)],
        (u, vs, lg, lb, w_sp, bfull))


def out_proj(attn, sgu, x, ga, gs, w_out, gpm, gpf):
    tm = 512

    def body(a_ref, s_ref, x_ref, ga_ref, gs_ref, w_ref, gpm_ref, gpf_ref, mix_ref, y_ref, x2_ref, h2_ref):
        a = a_ref[...]
        s = s_ref[...]
        an = (a * _rms(a) * ga_ref[...]).astype(BF16)
        sn = (s * _rms(s) * gs_ref[...]).astype(BF16)
        mix_ref[:, :ATTN_W] = an
        mix_ref[:, ATTN_W:] = sn
        y = _dot(an, w_ref[:ATTN_W, :]) + _dot(sn, w_ref[ATTN_W:, :])
        y_ref[...] = y
        x2 = x_ref[...] + y * _rms(y) * gpm_ref[...]
        x2_ref[...] = x2
        h2_ref[...] = (x2 * _rms(x2) * gpf_ref[...]).astype(BF16)

    wide = jax.ShapeDtypeStruct((SEQ, D_MODEL), F32)
    wide16 = jax.ShapeDtypeStruct((SEQ, D_MODEL), BF16)
    return _call(
        "out_proj", body, SEQ // tm,
        [_row_spec(tm, ATTN_W), _row_spec(tm, SGU_W), _row_spec(tm, D_MODEL), _full_spec((1, ATTN_W)),
         _full_spec((1, SGU_W)), _weight_spec((D_MODEL, D_MODEL)), _full_spec((1, D_MODEL)), _full_spec((1, D_MODEL))],
        [_row_spec(tm, D_MODEL)] * 4, [wide16, wide, wide, wide16],
        (attn, sgu, x, ga, gs, w_out, gpm, gpf))


def ffn_up(h2, w_gate_t, w_up_t):
    tm = 256

    def body(h_ref, wg_hbm, wu_hbm, g_ref, u_ref, a_ref, wg_ref, wu_ref, sems):
        def run(wait):
            h = h_ref[...]
            if wait:
                wait(0, 0)
            g = _dot_nt(h, wg_ref[...])
            g_ref[...] = g.astype(BF16)
            if wait:
                wait(1, 0)
            u = _dot_nt(h, wu_ref[...])
            u_ref[...] = u.astype(BF16)
            a_ref[...] = (g * jax.nn.sigmoid(g) * u).astype(BF16)

        _with_ffn_weights([wg_hbm, wu_hbm], [wg_ref, wu_ref], sems, FF_WHOLE, run)

    ff = jax.ShapeDtypeStruct((SEQ, D_FF), BF16)
    return _call(
        "ffn_up", body, SEQ // tm, [_row_spec(tm, D_MODEL), ANY, ANY],
        [_row_spec(tm, D_FF)] * 3, [ff, ff, jax.ShapeDtypeStruct((SEQ, D_FF), BF16)],
        (h2, w_gate_t, w_up_t), scratch_shapes=_ffn_weight_scratch(2, FF_WHOLE))


def ffn_down_loss(act, w_down, x2, gpo, target):
    tm = 512

    def body(a_ref, w_hbm, x2_ref, g_ref, t_ref, df_ref, dx3_ref, dg_ref, loss_ref, w_ref, sems):
        def run(wait):
            f = None
            for c, (o, n) in enumerate(FF_SPANS if wait else FF_WHOLE):
                if wait:
                    wait(0, c)
                part = _dot(a_ref[:, o:o + n], w_ref[o:o + n, :])
                f = part if f is None else f + part
            gain = g_ref[...]
            err = x2_ref[...] + f * _rms(f) * gain - t_ref[...]
            dx3 = err * np.float32(1.0 / D_MODEL)
            dx3_ref[...] = dx3
            df, dg = _rms_bwd(f, gain, dx3)
            df_ref[...] = df.astype(BF16)
            loss = jnp.sum(err * err, axis=(0, 1), keepdims=True)
            if wait:
                dg_ref[...] = dg
                loss_ref[...] = loss
            else:
                dg_ref[...] += dg
                loss_ref[...] += loss

        _with_ffn_weights([w_hbm], [w_ref], sems, FF_SPANS, run)

    return _call(
        "ffn_down_loss", body, SEQ // tm,
        [_row_spec(tm, D_FF), ANY, _row_spec(tm, D_MODEL), _full_spec((1, D_MODEL)), _row_spec(tm, D_MODEL)],
        [_row_spec(tm, D_MODEL), _row_spec(tm, D_MODEL), _full_spec((1, D_MODEL)), _full_spec((1, 1))],
        [jax.ShapeDtypeStruct((SEQ, D_MODEL), BF16), jax.ShapeDtypeStruct((SEQ, D_MODEL), F32),
         jax.ShapeDtypeStruct((1, D_MODEL), F32), jax.ShapeDtypeStruct((1, 1), F32)],
        (act, w_down, x2, gpo, target), scratch_shapes=_ffn_weight_scratch(1, FF_SPANS))


def ffn_bwd(df, w_down, gate, up, w_gate_t, w_up_t, x2, gpf, dx3, y, gpm, after=()):
    tm = 256

    def body(df_ref, wd_hbm, g_ref, u_ref, wg_hbm, wu_hbm, x2_ref, gpf_ref, dx3_ref, y_ref, gpm_ref,
             dg_ref, du_ref, dx2_ref, dy_ref, dgpf_ref, dgpm_ref, wd_ref, wg_ref, wu_ref, sems):
        def run(wait):
            if wait:
                wait(0, 0)
            dact = _dot_nt(df_ref[...], wd_ref[...])
            g = g_ref[...].astype(F32)
            s = jax.nn.sigmoid(g)
            dup = (dact * g * s).astype(BF16)
            dgate = (dact * u_ref[...].astype(F32) * (s * (1.0 + g * (1.0 - s)))).astype(BF16)
            du_ref[...] = dup
            dg_ref[...] = dgate
            if wait:
                wait(1, 0)
                wait(2, 0)
            dh2 = _dot(dgate, wg_ref[...]) + _dot(dup, wu_ref[...])
            dz, dgpf = _rms_bwd(x2_ref[...], gpf_ref[...], dh2)
            dx2 = dx3_ref[...] + dz
            dx2_ref[...] = dx2
            dy, dgpm = _rms_bwd(y_ref[...], gpm_ref[...], dx2)
            dy_ref[...] = dy.astype(BF16)
            if wait:
                dgpf_ref[...] = dgpf
                dgpm_ref[...] = dgpm
            else:
                dgpf_ref[...] += dgpf
                dgpm_ref[...] += dgpm

        _with_ffn_weights([wd_hbm, wg_hbm, wu_hbm], [wd_ref, wg_ref, wu_ref], sems, FF_WHOLE, run)

    vec = jax.ShapeDtypeStruct((1, D_MODEL), F32)
    ff16 = jax.ShapeDtypeStruct((SEQ, D_FF), BF16)
    return _call(
        "ffn_bwd", body, SEQ // tm,
        [_row_spec(tm, D_MODEL), ANY, _row_spec(tm, D_FF), _row_spec(tm, D_FF), ANY, ANY, _row_spec(tm, D_MODEL),
         _full_spec((1, D_MODEL)), _row_spec(tm, D_MODEL), _row_spec(tm, D_MODEL), _full_spec((1, D_MODEL))],
        [_row_spec(tm, D_FF), _row_spec(tm, D_FF), _row_spec(tm, D_MODEL), _row_spec(tm, D_MODEL),
         _full_spec((1, D_MODEL)), _full_spec((1, D_MODEL))],
        [ff16, ff16, jax.ShapeDtypeStruct((SEQ, D_MODEL), F32), jax.ShapeDtypeStruct((SEQ, D_MODEL), BF16), vec, vec],
        (df, w_down, gate, up, w_gate_t, w_up_t, x2, gpf, dx3, y, gpm), scratch_shapes=_ffn_weight_scratch(3, FF_WHOLE), after=after)


def weight_grads(name, lhs, b, after=()):
    m, n, k = lhs[0].shape[1], b.shape[1], len(lhs)
    tr = 256

    def body(*refs):
        for a_ref, o_ref in zip(refs[:k], refs[k + 1:]):
            o_ref[...] = _dot_tn(a_ref[...], refs[k][...]).astype(BF16)

    outs = _call(
        name, body, m // tr, [pl.BlockSpec((SEQ, tr), lambda i: (0, i))] * k + [_weight_spec((SEQ, n))],
        [_row_spec(tr, n)] * k, [jax.ShapeDtypeStruct((m, n), BF16)] * k, (*lhs, b), after=after)
    return [out.reshape(N_DEV, m // N_DEV, n) for out in outs]


def weight_grad(name, a, b, after=()):
    return weight_grads(name, [a], b, after)[0]


def weight_grad_of_parts(name, parts, b, after=()):
    p, n, k = parts[0].shape[1], b.shape[1], len(parts)
    tr = 512
    per = p // tr

    def body(*refs):
        tile = pl.program_id(0)
        for j in range(k):
            @pl.when(tile // per == j)
            def _(j=j):
                refs[k + 1][...] = _dot_tn(refs[j][...].astype(BF16), refs[k][...]).astype(BF16)

    def part_spec(j):
        return pl.BlockSpec((SEQ, tr), lambda i: (0, jnp.clip(i - per * j, 0, per - 1)))

    (out,) = _call(
        name, body, k * per, [part_spec(j) for j in range(k)] + [_weight_spec((SEQ, n))],
        [_row_spec(tr, n)], [jax.ShapeDtypeStruct((k * p, n), BF16)], (*parts, b), after=after)
    return out.reshape(N_DEV, k * p // N_DEV, n)


def mix_bwd(dy, w_out, attn, sgu, ga, gs, after=()):
    tm = 512
    n_steps = SEQ // tm

    def body(dy_ref, w_ref, a_ref, s_ref, ga_ref, gs_ref, ds_ref, dga_ref, dgs_ref, da_ref, scratch, sems):
        dy = dy_ref[...]
        da, dga = _rms_bwd(a_ref[...], ga_ref[...], _dot_nt(dy, w_ref[:ATTN_W, :]))
        ds, dgs = _rms_bwd(s_ref[...], gs_ref[...], _dot_nt(dy, w_ref[ATTN_W:, :]))
        ds_ref[...] = ds
        _to_residue_rows([da], [da_ref], scratch, sems, tm, n_steps)

        @pl.when(pl.program_id(0) == 0)
        def _():
            dga_ref[...] = jnp.zeros_like(dga_ref)
            dgs_ref[...] = jnp.zeros_like(dgs_ref)

        dga_ref[...] += dga
        dgs_ref[...] += dgs

    half = jax.ShapeDtypeStruct((SEQ, 512), F32)
    vec = jax.ShapeDtypeStruct((1, 512), F32)
    return _call(
        "mix_bwd", body, n_steps,
        [_row_spec(tm, D_MODEL), _weight_spec((D_MODEL, D_MODEL)), _row_spec(tm, 512), _row_spec(tm, 512),
         _full_spec((1, 512)), _full_spec((1, 512))],
        [_row_spec(tm, 512), _full_spec((1, 512)), _full_spec((1, 512)), ANY],
        [half, vec, vec, half], (dy, w_out, attn, sgu, ga, gs), scratch_shapes=_residue_scratch(1, tm, ATTN_W), after=after)


def sgu_bwd(u, vs, dsgu, lg, lb, w_sp, bfull, after=()):
    cpb = 4

    def body(u_ref, vs_ref, d_ref, lg_ref, lb_ref, w_ref, b_ref, du_ref, dvs_ref, dlg_ref, dlb_ref, dw_ref, db_ref):
        wc, causal = _causal_weights(w_ref)
        head0 = lax.broadcasted_iota(jnp.int32, (CHUNK, 128), 1) < HEAD_DIM
        lg = lg_ref[...]

        @pl.when(pl.program_id(0) == 0)
        def _():
            dlg_ref[...] = jnp.zeros_like(dlg_ref)
            dlb_ref[...] = jnp.zeros_like(dlb_ref)
            dw_ref[...] = jnp.zeros_like(dw_ref)
            db_ref[...] = jnp.zeros_like(db_ref)

        for ci in range(cpb):
            rows = pl.ds(ci * CHUNK, CHUNK)
            u = u_ref[rows, :]
            vs = vs_ref[rows, :]
            d = d_ref[rows, :]
            ug, xhat, rstd, vn, ms = _sgu_chunk_fwd(u, vs, lg, lb_ref[...], wc, b_ref[...], head0)
            du_ref[rows, :] = (d * ms * _gelu_grad(u)).astype(BF16)
            dms = d * ug
            db_ref[...] += dms
            dvn = []
            for gp in range(SGU_W // 128):
                dmp = dms[:, gp * 128:(gp + 1) * 128]
                dm0 = jnp.where(head0, dmp, 0.0).astype(BF16)
                dm1 = jnp.where(head0, 0.0, dmp).astype(BF16)
                vp = vn[:, gp * 128:(gp + 1) * 128].astype(BF16)
                dw_ref[2 * gp] += _dot_nt(dm0, vp)
                dw_ref[2 * gp + 1] += _dot_nt(dm1, vp)
                dvn.append(_dot_tn(wc[2 * gp], dm0) + _dot_tn(wc[2 * gp + 1], dm1))
            dvn = jnp.concatenate(dvn, axis=1)
            dlg_ref[...] += jnp.sum(dvn * xhat, axis=0, keepdims=True)
            dlb_ref[...] += jnp.sum(dvn, axis=0, keepdims=True)
            dxh = dvn * lg
            dvg = rstd * (dxh - jnp.mean(dxh, axis=-1, keepdims=True) - xhat * jnp.mean(dxh * xhat, axis=-1, keepdims=True))
            dvs_ref[rows, :] = (dvg * _gelu_grad(vs)).astype(BF16)

        @pl.when(pl.program_id(0) == pl.num_programs(0) - 1)
        def _():
            for g in range(N_GROUPS):
                dw_ref[g] = jnp.where(causal, dw_ref[g], 0.0)

    tm = cpb * CHUNK
    half16 = jax.ShapeDtypeStruct((SEQ, SGU_W), BF16)
    vec = jax.ShapeDtypeStruct((1, SGU_W), F32)
    return _call(
        "sgu_bwd", body, SEQ // tm,
        [_row_spec(tm, SGU_W)] * 3 + [_full_spec((1, SGU_W)), _full_spec((1, SGU_W)),
                                      _full_spec((N_GROUPS, CHUNK, CHUNK)), _full_spec((CHUNK, SGU_W))],
        [_row_spec(tm, SGU_W), _row_spec(tm, SGU_W), _full_spec((1, SGU_W)), _full_spec((1, SGU_W)),
         _full_spec((N_GROUPS, CHUNK, CHUNK)), _full_spec((CHUNK, SGU_W))],
        [half16, half16, vec, vec, jax.ShapeDtypeStruct((N_GROUPS, CHUNK, CHUNK), F32),
         jax.ShapeDtypeStruct((CHUNK, SGU_W), F32)],
        (u, vs, dsgu, lg, lb, w_sp, bfull), after=after)


def attn_bwd(q, k, v, o, lse, do, pos_col, rot):
    n_steps = ATTN_W // 128

    def body(q_ref, k_ref, v_ref, o_ref, lse_ref, do_ref, pos_ref, invf_ref, ma_ref, mb_ref,
             dq_ref, dk_ref, dv_ref, dqa_ref, dka_ref, dva_ref, dlt_ref, rot_ref, out_ref, sems):
        step = pl.program_id(0)
        slot = step % 2

        def back(at_step, s):
            lanes = pl.ds(pl.multiple_of(at_step * 128, 128), 128)
            return [pltpu.make_async_copy(out_ref.at[s, a, pl.ds(b * (SEQ // RES), SEQ // RES), :], nat.at[:, b, lanes],
                                          sems.at[s, a, b]) for a, nat in enumerate((dq_ref, dk_ref, dv_ref)) for b in range(RES)]

        dqa_ref[...] = jnp.zeros_like(dqa_ref)
        dka_ref[...] = jnp.zeros_like(dka_ref)
        dva_ref[...] = jnp.zeros_like(dva_ref)

        def delta(i, carry):
            rows = pl.ds(pl.multiple_of(i * 256, 256), 256)
            prod = do_ref[rows, :] * o_ref[rows, :]
            h0 = lax.broadcasted_iota(jnp.int32, (256, 128), 1) < HEAD_DIM
            d0 = jnp.sum(jnp.where(h0, prod, 0.0), axis=-1, keepdims=True)
            d1 = jnp.sum(jnp.where(h0, 0.0, prod), axis=-1, keepdims=True)
            dlt_ref[rows, :] = jnp.where(h0, d0, d1)
            return carry

        lax.fori_loop(0, SEQ // 256, delta, 0)

        def add_rows(ref, slices, val):
            at = 0
            for start, size in slices:
                ref[pl.ds(start, size), :] += val[at:at + size]
                at += size

        def group(p, masks, blocks):
            head0, mask1, mask2 = masks
            heads = (head0, jnp.logical_not(head0))
            keys = [rows if prev is None else prev + rows for rows, prev in blocks]
            mask = [mask1 if prev is None else mask2 for _, prev in blocks]
            kk = [_load_rows(k_ref, ks).astype(BF16) for ks in keys]
            vv = [_load_rows(v_ref, ks).astype(BF16) for ks in keys]
            qb = [_load_rows(q_ref, rows) for rows, _ in blocks]
            dob = [_load_rows(do_ref, rows) for rows, _ in blocks]
            lse_b = [_load_rows(lse_ref, rows) for rows, _ in blocks]
            dlt_b = [_load_rows(dlt_ref, rows) for rows, _ in blocks]
            chains = [(g, h) for g in range(len(blocks)) for h in range(2)]
            qm = [jnp.where(heads[h], qb[g], 0.0).astype(BF16) for g, h in chains]
            dom = [jnp.where(heads[h], dob[g], 0.0).astype(BF16) for g, h in chains]
            s = [_dot_nt(qm[c], kk[g]) for c, (g, h) in enumerate(chains)]
            dp = [_dot_nt(dom[c], vv[g]) for c, (g, h) in enumerate(chains)]
            pr = [jnp.where(mask[g], jnp.exp(s[c] - lse_b[g][:, h * HEAD_DIM:h * HEAD_DIM + 1]), 0.0)
                  for c, (g, h) in enumerate(chains)]
            ds = [(pr[c] * (dp[c] - dlt_b[g][:, h * HEAD_DIM:h * HEAD_DIM + 1])).astype(BF16)
                  for c, (g, h) in enumerate(chains)]
            dv = [_dot_tn(pr[c].astype(BF16), dom[c]) for c in range(len(chains))]
            dk = [_dot_tn(ds[c], qm[c]) for c in range(len(chains))]
            dq = [_dot(ds[c], kk[g]) for c, (g, h) in enumerate(chains)]
            for g, (rows, _) in enumerate(blocks):
                add_rows(dqa_ref, rows, jnp.where(head0, dq[2 * g], dq[2 * g + 1]))
                add_rows(dka_ref, keys[g], dk[2 * g] + dk[2 * g + 1])
                add_rows(dva_ref, keys[g], dv[2 * g] + dv[2 * g + 1])

        _for_each_group(group)

        @pl.when(pl.program_id(0) == 0)
        def _():
            def tables(i, carry):
                rows = pl.ds(pl.multiple_of(i * 256, 256), 256)
                c, sa, sb = _rot_tables(pos_ref[rows, :], invf_ref[...], ma_ref[...], mb_ref[...])
                rot_ref[0, rows, :] = c
                rot_ref[1, rows, :] = sa
                rot_ref[2, rows, :] = sb
                return carry

            lax.fori_loop(0, SEQ // 256, tables, 0)

        @pl.when(step >= 2)
        def _():
            for cp in back(step - 2, slot):
                cp.wait()

        def finish(i, carry):
            rows = pl.ds(pl.multiple_of(i * 256, 256), 256)
            c, sa, sb = rot_ref[0, rows, :], rot_ref[1, rows, :], rot_ref[2, rows, :]
            out_ref[slot, 0, rows, :] = _rot_t(dqa_ref[rows, :] * Q_SCALE, c, sa, sb)
            out_ref[slot, 1, rows, :] = _rot_t(dka_ref[rows, :], c, sa, sb)
            out_ref[slot, 2, rows, :] = dva_ref[rows, :]
            return carry

        lax.fori_loop(0, SEQ // 256, finish, 0)
        for cp in back(step, slot):
            cp.start()

        @pl.when(step == n_steps - 1)
        def _():
            for cp in back(step - 1, 1 - slot) + back(step, slot):
                cp.wait()

    slab = pl.BlockSpec((SEQ, 128), lambda i: (0, i))
    out = jax.ShapeDtypeStruct((SEQ // RES, RES, ATTN_W), F32)
    acc = pltpu.VMEM((SEQ, 128), F32)
    outs = _call(
        "attn_bwd", body, n_steps,
        [slab] * 6 + [_full_spec((SEQ, 1)), _full_spec((1, 128)), _full_spec((1, 128)), _full_spec((1, 128))],
        [ANY] * 3, [out, out, out], (q, k, v, o, lse, do, pos_col, *rot),
        scratch_shapes=[acc, acc, acc, acc, pltpu.VMEM((3, SEQ, 128), F32), pltpu.VMEM((2, 3, SEQ, 128), F32),
                        pltpu.SemaphoreType.DMA((2, 3, RES))])
    return [t.reshape(SEQ, ATTN_W) for t in outs]


def in_bwd(dproj_parts, w_in_t, x, g1, dx2, after=()):
    tm = 512
    k = len(dproj_parts)

    def body(*refs):
        w_ref, x_ref, g_ref, dx2_ref, dx_ref, dg_ref = refs[k:]
        dh1 = _dot(refs[0][...].astype(BF16), w_ref[0:512, :])
        for j in range(1, k):
            dh1 = dh1 + _dot(refs[j][...].astype(BF16), w_ref[512 * j:512 * (j + 1), :])
        dz, dg = _rms_bwd(x_ref[...], g_ref[...], dh1)
        dx_ref[...] = dx2_ref[...] + dz

        @pl.when(pl.program_id(0) == 0)
        def _():
            dg_ref[...] = jnp.zeros_like(dg_ref)

        dg_ref[...] += dg

    return _call(
        "in_bwd", body, SEQ // tm,
        [_row_spec(tm, 512)] * k + [_weight_spec((IN_W, D_MODEL)), _row_spec(tm, D_MODEL), _full_spec((1, D_MODEL)),
                                    _row_spec(tm, D_MODEL)],
        [_row_spec(tm, D_MODEL), _full_spec((1, D_MODEL))],
        [jax.ShapeDtypeStruct((SEQ, D_MODEL), F32), jax.ShapeDtypeStruct((1, D_MODEL), F32)],
        (*dproj_parts, w_in_t, x, g1, dx2), after=after)


def _coords():
    return lax.axis_index("x"), lax.axis_index("y"), lax.axis_index("c")


class Exchange:
    def __init__(self, srcs, bufs, new_shapes, n_sems, make):
        self.srcs, self.bufs, self.new_shapes, self.n_sems, self.make = list(srcs), list(bufs), list(new_shapes), n_sems, make


def _call(name, body, n_steps, in_specs, out_specs, out_shape, args, scratch_shapes=(), after=()):
    n_in = len(args)

    def wrapped(*refs):
        body(*refs[:n_in], *refs[n_in + len(after):])

    return list(pl.pallas_call(
        wrapped, name=name, grid=(n_steps,), in_specs=list(in_specs) + [ANY] * len(after), out_specs=list(out_specs),
        out_shape=list(out_shape), scratch_shapes=list(scratch_shapes), compiler_params=_params(),
    )(*args, *after))


GATHER_SEMS = 8


def gather(bufs):
    n = len(bufs)

    def make(src_refs, buf_refs, new_refs, send_sems, recv_sems):
        x, y, c = _coords()
        me, sibling = (x, y, c), (x, y, 1 - c)
        over_x, over_y, across = (1 - x, y), (x, 1 - y), (1 - x, 1 - y)

        def copy(a, k, block, to, half=None):
            r = buf_refs[a].shape[0] // N_DEV
            lo, size = (0, r) if half is None else (half * (r // 2), r // 2)
            rows = buf_refs[a].at[pl.ds((4 * block[0] + 2 * block[1] + block[2]) * r + lo, size), :]
            return pltpu.make_async_remote_copy(
                src_ref=rows, dst_ref=rows, send_sem=send_sems.at[GATHER_SEMS * a + k],
                recv_sem=recv_sems.at[GATHER_SEMS * a + k], device_id=to, device_id_type=MESH)

        every = range(n)
        out = ([copy(a, 0, me, sibling) for a in every] + [copy(a, 1, me, (*over_x, c)) for a in every]
               + [copy(a, 2, me, (*over_y, c)) for a in every])
        near_in = [copy(a, 1, (*over_x, c), me) for a in every] + [copy(a, 2, (*over_y, c), me) for a in every]
        relay = ([copy(a, 3, (*over_x, c), (*over_y, c), half=0) for a in every]
                 + [copy(a, 4, (*over_y, c), (*over_x, c), half=1) for a in every])
        near_on = [copy(a, 5, (*over_x, c), sibling) for a in every] + [copy(a, 6, (*over_y, c), sibling) for a in every]
        relay_in = ([copy(a, 3, (*across, c), me, half=0) for a in every]
                    + [copy(a, 4, (*across, c), me, half=1) for a in every])
        far_on = [copy(a, 7, (*across, c), sibling) for a in every]
        from_core = ([copy(a, 0, sibling, me) for a in every] + [copy(a, 5, (*over_x, 1 - c), me) for a in every]
                     + [copy(a, 6, (*over_y, 1 - c), me) for a in every] + [copy(a, 7, (*across, 1 - c), me) for a in every])
        stages = [([], out), (near_in, relay + near_on), (relay_in, far_on)]
        return stages, out + relay + near_on + far_on, from_core

    return Exchange([], bufs, [], GATHER_SEMS * n, make)


TO_GATHER = (1, lambda x, y, c: [(x, y, 1 - c), (1 - x, y, c), (x, 1 - y, c)])
TO_SIBLING = (2, lambda x, y, c: [(x, y, 1 - c)])
TO_CHIPS = (3, lambda x, y, c: [(1 - x, y, c), (x, 1 - y, c), (1 - x, 1 - y, c)])
TO_ALL = (4, lambda x, y, c: [(x ^ (m >> 2), y ^ ((m >> 1) & 1), c ^ (m & 1)) for m in range(1, N_DEV)])


def by_sequencer(name, exchanges, who):
    collective_id, peers_of = who
    hbm = pltpu.MemorySpace.HBM
    refs = [([jax.new_ref(a, memory_space=hbm) for a in ex.srcs], [jax.new_ref(a, memory_space=hbm) for a in ex.bufs],
             [jax.empty_ref(s, memory_space=hbm) for s in ex.new_shapes]) for ex in exchanges]
    sems = []
    for ex in exchanges:
        sems += [pltpu.SemaphoreType.DMA((ex.n_sems,)), pltpu.SemaphoreType.DMA((ex.n_sems,))]

    @pl.kernel(mesh=plsc.ScalarSubcoreMesh(axis_name="sequencer", num_cores=1), name=name, scratch_types=tuple(sems),
               compiler_params=pltpu.CompilerParams(collective_id=collective_id))
    def launch(*sem_refs):
        peers = peers_of(*_coords())
        barrier = pltpu.get_barrier_semaphore()
        for peer in peers:
            pl.semaphore_signal(barrier, inc=1, device_id=peer, device_id_type=MESH)
        pl.semaphore_wait(barrier, len(peers))

        made = [ex.make(*refs[k], sem_refs[2 * k], sem_refs[2 * k + 1]) for k, ex in enumerate(exchanges)]
        for stage in range(max(len(stages) for stages, _, _ in made)):
            for stages, _, _ in made:
                if stage < len(stages):
                    arrivals, starts = stages[stage]
                    for cp in arrivals:
                        cp.wait_recv()
                    for cp in starts:
                        cp.start()
        for _, sends, arrivals in made:
            for cp in arrivals:
                cp.wait_recv()
            for cp in sends:
                cp.wait_send()

    launch()
    return [([ref[...] for ref in bufs], [ref[...] for ref in news]) for _, bufs, news in refs]


def place_shards(name, shards, dev):
    n = len(shards)

    def body(dev_ref, *refs):
        for a in range(n):
            refs[n + a][...] = refs[a][...].astype(BF16)

    spec = pltpu.PrefetchScalarGridSpec(
        num_scalar_prefetch=1, grid=(1,),
        in_specs=[pl.BlockSpec(s.shape, lambda i, dev_ref: (0, 0)) for s in shards],
        out_specs=[pl.BlockSpec(s.shape, lambda i, dev_ref: (dev_ref[0], 0)) for s in shards])
    return pl.pallas_call(
        body, name=name, grid_spec=spec,
        out_shape=[jax.ShapeDtypeStruct((N_DEV * s.shape[0], s.shape[1]), BF16) for s in shards],
        compiler_params=_params(),
    )(dev, *shards)


def _swap(copies_of):
    def make(src_refs, buf_refs, new_refs, send_sems, recv_sems):
        copies = copies_of(src_refs, new_refs, send_sems, recv_sems)
        return [([], copies)], copies, copies

    return make


def to_sibling(grads):
    def copies_of(src_refs, new_refs, send_sems, recv_sems):
        x, y, c = _coords()
        return [pltpu.make_async_remote_copy(
            src_ref=src_refs[a].at[2 * xy + 1 - c], dst_ref=new_refs[a].at[xy], send_sem=send_sems.at[4 * a + xy],
            recv_sem=recv_sems.at[4 * a + xy], device_id=(x, y, 1 - c), device_id_type=MESH)
            for a in range(len(src_refs)) for xy in range(4)]

    return Exchange(grads, [], [jax.ShapeDtypeStruct((4,) + g.shape[1:], g.dtype) for g in grads], 4 * len(grads),
                    _swap(copies_of))


def to_chips(parts):
    def copies_of(src_refs, new_refs, send_sems, recv_sems):
        x, y, c = _coords()
        chips = [(1 - x, y), (x, 1 - y), (1 - x, 1 - y)]
        return [pltpu.make_async_remote_copy(
            src_ref=src_refs[a].at[2 * px + py], dst_ref=new_refs[a].at[2 * x + y], send_sem=send_sems.at[3 * a + j],
            recv_sem=recv_sems.at[3 * a + j], device_id=(px, py, c), device_id_type=MESH)
            for a in range(len(src_refs)) for j, (px, py) in enumerate(chips)]

    return Exchange(parts, [], [jax.ShapeDtypeStruct(p.shape, p.dtype) for p in parts], 3 * len(parts), _swap(copies_of))


def to_owners(grad):
    def copies_of(src_refs, new_refs, send_sems, recv_sems):
        x, y, c = _coords()
        copies = []
        for m in range(1, N_DEV):
            px, py, pc = x ^ (m >> 2), y ^ ((m >> 1) & 1), c ^ (m & 1)
            copies.append(pltpu.make_async_remote_copy(
                src_ref=src_refs[0].at[4 * px + 2 * py + pc], dst_ref=new_refs[0].at[4 * x + 2 * y + c],
                send_sem=send_sems.at[m - 1], recv_sem=recv_sems.at[m - 1], device_id=(px, py, pc), device_id_type=MESH))
        return copies

    return Exchange([grad], [], [jax.ShapeDtypeStruct(grad.shape, grad.dtype)], N_DEV - 1, _swap(copies_of))


def to_everyone(vec):
    def copies_of(src_refs, new_refs, send_sems, recv_sems):
        x, y, c = _coords()
        copies = []
        for m in range(1, N_DEV):
            px, py, pc = x ^ (m >> 2), y ^ ((m >> 1) & 1), c ^ (m & 1)
            copies.append(pltpu.make_async_remote_copy(
                src_ref=src_refs[0], dst_ref=new_refs[0].at[4 * x + 2 * y + c],
                send_sem=send_sems.at[m - 1], recv_sem=recv_sems.at[m - 1], device_id=(px, py, pc), device_id_type=MESH))
        return copies

    return Exchange([vec], [], [jax.ShapeDtypeStruct((N_DEV,) + vec.shape, vec.dtype)], N_DEV - 1, _swap(copies_of))


def sum_cores(name, grads, others, core, after=()):
    k = len(grads)

    def body(core_ref, *refs):
        for j in range(k):
            out_ref = refs[2 * k + len(after) + j]
            out_ref[...] = (refs[j][:, 0].astype(F32) + refs[k + j][...].astype(F32)).astype(out_ref.dtype)

    mine = [pl.BlockSpec((2, 1) + o.shape[1:], lambda i, core_ref: (i, core_ref[0], 0, 0)) for o in others]
    theirs = [pl.BlockSpec((2,) + o.shape[1:], lambda i, core_ref: (i, 0, 0)) for o in others]
    return pl.pallas_call(
        body, name=name,
        grid_spec=pltpu.PrefetchScalarGridSpec(
            num_scalar_prefetch=1, grid=(2,), in_specs=mine + theirs + [ANY] * len(after), out_specs=theirs),
        out_shape=[jax.ShapeDtypeStruct(o.shape, o.dtype) for o in others],
        compiler_params=_params(),
    )(core, *[g.reshape((4, 2) + g.shape[1:]) for g in grads], *others, *after)


def sum_owned(name, grad, others, dev_ids, after=()):
    _, r, w = grad.shape

    def body(ids_ref, *refs):
        acc = refs[0][0]
        for k in range(1, N_DEV):
            acc = acc + refs[k][0]
        refs[-1][...] = acc

    def pick(k):
        return pl.BlockSpec((1, r, w), lambda i, ids_ref: (ids_ref[k], 0, 0))

    return pl.pallas_call(
        body, name=name,
        grid_spec=pltpu.PrefetchScalarGridSpec(
            num_scalar_prefetch=1, grid=(1,), in_specs=[pick(k) for k in range(N_DEV)] + [ANY] * len(after),
            out_specs=pl.BlockSpec((r, w), lambda i, ids_ref: (ids_ref[0], 0))),
        out_shape=jax.ShapeDtypeStruct((N_DEV * r, w), F32),
        compiler_params=_params(),
    )(dev_ids, grad, *([others] * (N_DEV - 1)), *after)


def _adamw_update(w, g, m, v):
    nm = ADAM_B1 * m + np.float32(1.0 - ADAM_B1) * g
    nv = ADAM_B2 * v + np.float32(1.0 - ADAM_B2) * (g * g)
    m_hat = nm / np.float32(1.0 - ADAM_B1 ** ADAM_STEP)
    v_hat = nv / np.float32(1.0 - ADAM_B2 ** ADAM_STEP)
    return -ADAM_LR * (m_hat / (jnp.sqrt(v_hat) + ADAM_EPS) + ADAM_WD * w), nm, nv


def adamw_of_sums(name, parts, others, chip_ids, ws, ms, vs, after):
    n = len(parts)
    halves = 2

    def body(ids_ref, *refs):
        outs = refs[7 * n + len(after):]
        for j in range(n):
            p_ref, a_ref, b_ref, c_ref, w_ref, m_ref, v_ref = refs[7 * j:7 * j + 7]
            g = ((p_ref[0].astype(F32) + a_ref[0].astype(F32)) + b_ref[0].astype(F32)) + c_ref[0].astype(F32)
            outs[4 * j][...] = g
            outs[4 * j + 1][...], outs[4 * j + 2][...], outs[4 * j + 3][...] = _adamw_update(w_ref[...], g, m_ref[...], v_ref[...])

    in_specs, out_specs, out_shape, operands = [], [], [], []
    for part, other, w, m, v in zip(parts, others, ws, ms, vs):
        _, r, wd = part.shape
        rows = r // halves
        whole = pl.BlockSpec((rows, wd), lambda i, ids_ref: (i, 0))
        in_specs += [pl.BlockSpec((1, rows, wd), lambda i, ids_ref, k=k: (ids_ref[k], i, 0)) for k in range(4)] + [whole] * 3
        out_specs += [whole] * 4
        out_shape += [jax.ShapeDtypeStruct((r, wd), F32)] * 4
        operands += [part, other, other, other, w, m, v]
    outs = pl.pallas_call(
        body, name=name,
        grid_spec=pltpu.PrefetchScalarGridSpec(
            num_scalar_prefetch=1, grid=(halves,), in_specs=in_specs + [ANY] * len(after), out_specs=out_specs),
        out_shape=out_shape,
        compiler_params=_params(),
    )(chip_ids, *operands, *after)
    return [tuple(outs[4 * j:4 * j + 4]) for j in range(n)]


def pack_small(parts):
    names = [name for name, _ in SMALL if name in parts]
    operands = [parts[name] for name in names]
    first_row, at = {}, 0
    for name, size in SMALL:
        first_row[name] = at // 128
        at += size
    sizes = dict(SMALL)

    def body(*refs):
        out_ref = refs[-1]
        out_ref[...] = jnp.zeros_like(out_ref)
        for name, ref in zip(names, refs):
            row = first_row[name]
            if name == "loss_sum":
                lane0 = lax.broadcasted_iota(jnp.int32, (1, 128), 1) == 0
                out_ref[row:row + 1, :] = jnp.where(lane0, ref[...], 0.0)
            else:
                rows = sizes[name] // 128
                out_ref[row:row + rows, :] = ref[...].reshape(rows, 128)

    vmem = pl.BlockSpec(memory_space=pltpu.VMEM)
    return pl.pallas_call(
        body, name="pack_small", in_specs=[vmem] * len(names), out_specs=vmem,
        out_shape=jax.ShapeDtypeStruct((SMALL_ROWS, 128), F32), compiler_params=_params(()),
    )(*operands)


LATE = "pre_mix_norm"


def adamw_small(packed_g, late, ws, ms, vs, after=()):
    names = [LATE] if late else [name for name, _ in SMALL if name not in ("loss_sum", LATE)]
    k = len(names)
    shapes = [ws[name].shape[1:] if ws[name].ndim > 2 else ws[name].shape for name in names]
    first_row, at = {}, 0
    for name, size in SMALL:
        first_row[name] = at // 128
        at += size
    sizes = dict(SMALL)
    n_in = 3 if late else 1

    def body(*refs):
        w_refs, m_refs, v_refs = (refs[n_in + j * k:n_in + (j + 1) * k] for j in range(3))
        outs = refs[n_in + 3 * k + len(after):]
        for i, name in enumerate(names):
            if late:
                late_ref, own_ref, dev_ref = refs[:n_in]
                parts = [jnp.where(dev_ref[0] == j, own_ref[...], late_ref[j]) for j in range(N_DEV)]
                g = parts[0]
                for j in range(1, N_DEV):
                    g = g + parts[j]
            else:
                g = refs[0][first_row[name]:first_row[name] + sizes[name] // 128, :].reshape(shapes[i])
            outs[i][...] = g
            outs[k + i][...], outs[2 * k + i][...], outs[3 * k + i][...] = _adamw_update(
                w_refs[i][...], g, m_refs[i][...], v_refs[i][...])
        if not late:
            outs[4 * k][...] = refs[0][first_row["loss_sum"]:first_row["loss_sum"] + 1, 0:1]

    vmem = pl.BlockSpec(memory_space=pltpu.VMEM)
    operands = [t[name].reshape(shape) for t in (ws, ms, vs) for name, shape in zip(names, shapes)]
    outs = pl.pallas_call(
        body, name="adamw_late" if late else "adamw_small",
        in_specs=([vmem, vmem, pl.BlockSpec(memory_space=pltpu.SMEM)] if late else [vmem]) + [vmem] * (3 * k) + [ANY] * len(after),
        out_specs=[vmem] * (4 * k + (0 if late else 1)),
        out_shape=[jax.ShapeDtypeStruct(shape, F32) for _ in range(4) for shape in shapes]
        + ([] if late else [jax.ShapeDtypeStruct((1, 1), F32)]),
        compiler_params=_params(()),
    )(*(late or (packed_g,)), *operands, *after)
    tables = [{name: outs[j * k + i].reshape(ws[name].shape) for i, name in enumerate(names)} for j in range(4)]
    return (*tables, None if late else outs[4 * k])


def kernel(x, positions, pre_mix_norm, w_in, sgu_ln_gain, sgu_ln_bias, sgu_w_spatial, sgu_b_spatial, attn_out_norm, sgu_out_norm, w_out, post_mix_norm, pre_ffn_norm, w_gate, w_up, w_down, post_ffn_norm, loss_target, m_pre_mix_norm, m_w_in, m_sgu_ln_gain, m_sgu_ln_bias, m_sgu_w_spatial, m_sgu_b_spatial, m_attn_out_norm, m_sgu_out_norm, m_w_out, m_post_mix_norm, m_pre_ffn_norm, m_w_gate, m_w_up, m_w_down, m_post_ffn_norm, v_pre_mix_norm, v_w_in, v_sgu_ln_gain, v_sgu_ln_bias, v_sgu_w_spatial, v_sgu_b_spatial, v_attn_out_norm, v_sgu_out_norm, v_w_out, v_post_mix_norm, v_pre_ffn_norm, v_w_gate, v_w_up, v_w_down, v_post_ffn_norm):
    small_w = dict(pre_mix_norm=pre_mix_norm, sgu_ln_gain=sgu_ln_gain, sgu_ln_bias=sgu_ln_bias, sgu_w_spatial=sgu_w_spatial,
                   sgu_b_spatial=sgu_b_spatial, attn_out_norm=attn_out_norm, sgu_out_norm=sgu_out_norm,
                   post_mix_norm=post_mix_norm, pre_ffn_norm=pre_ffn_norm, post_ffn_norm=post_ffn_norm)
    small_m = dict(pre_mix_norm=m_pre_mix_norm, sgu_ln_gain=m_sgu_ln_gain, sgu_ln_bias=m_sgu_ln_bias, sgu_w_spatial=m_sgu_w_spatial,
                   sgu_b_spatial=m_sgu_b_spatial, attn_out_norm=m_attn_out_norm, sgu_out_norm=m_sgu_out_norm,
                   post_mix_norm=m_post_mix_norm, pre_ffn_norm=m_pre_ffn_norm, post_ffn_norm=m_post_ffn_norm)
    small_v = dict(pre_mix_norm=v_pre_mix_norm, sgu_ln_gain=v_sgu_ln_gain, sgu_ln_bias=v_sgu_ln_bias, sgu_w_spatial=v_sgu_w_spatial,
                   sgu_b_spatial=v_sgu_b_spatial, attn_out_norm=v_attn_out_norm, sgu_out_norm=v_sgu_out_norm,
                   post_mix_norm=v_post_mix_norm, pre_ffn_norm=v_pre_ffn_norm, post_ffn_norm=v_post_ffn_norm)

    x2d = x[0]
    target = loss_target[0]
    pos_col = positions.reshape(SEQ, 1)
    rot = _rot_consts()
    w_sp = sgu_w_spatial[0]
    bfull = jnp.repeat(sgu_b_spatial[0].T, HEAD_DIM, axis=1)

    x_i, y_i, c_i = (lax.axis_index(a).astype(jnp.int32) for a in MESH_AXES)
    dev = 4 * x_i + 2 * y_i + c_i
    core = c_i.reshape(1)
    chip = 2 * x_i + y_i
    chip_ids = jnp.stack([chip, chip ^ 1, chip ^ 2, chip ^ 3])
    dev_ids = jnp.stack([dev ^ m for m in range(N_DEV)])

    def gathered(name, bufs):
        return by_sequencer(name, [gather(bufs)], TO_GATHER)[0][0]

    def from_sibling(name, grads):
        return by_sequencer(name, [to_sibling(grads)], TO_SIBLING)[0][1]

    def from_chips(name, parts):
        return by_sequencer(name, [to_chips(parts)], TO_CHIPS)[0][1]

    (w_in_t,) = place_shards("place_w_in", [w_in[0].T], dev.reshape(1))
    (w_in_t,) = gathered("gather_w_in", [w_in_t])
    w_gate_t, w_up_t, w_out_f, w_down_f = place_shards(
        "place_weights", [w_gate[0].T, w_up[0].T, w_out[0], w_down[0]], dev.reshape(1))
    (w_out_f,) = gathered("gather_w_out", [w_out_f])
    w_gate_t, w_up_t = gathered("gather_w_gate_up", [w_gate_t, w_up_t])
    (w_down_f,) = gathered("gather_w_down", [w_down_f])

    pos_r = _to_residue_order(pos_col)
    h1 = pre_norm(x2d, pre_mix_norm, after=[pos_r, dev_ids, chip_ids])
    u, vs, q, k, v = in_proj(h1, pos_col, w_in_t, rot)
    attn_r, lse, attn = attn_fwd(q, k, v)
    (sgu,) = sgu_fwd(u, vs, sgu_ln_gain, sgu_ln_bias, w_sp, bfull)
    mix, y, x2, h2 = out_proj(attn, sgu, x2d, attn_out_norm, sgu_out_norm, w_out_f, post_mix_norm, pre_ffn_norm)
    gate, up, act = ffn_up(h2, w_gate_t, w_up_t)
    df, dx3, d_post_ffn, sq_err = ffn_down_loss(act, w_down_f, x2, post_ffn_norm, target)

    g_w_down = weight_grad("grad_w_down", act, df)
    (s_down,) = from_sibling("w_down_to_sibling", [g_w_down])
    dgate, dup, dx2, dy, d_pre_ffn, d_post_mix = ffn_bwd(
        df, w_down_f, gate, up, w_gate_t, w_up_t, x2, pre_ffn_norm, dx3, y, post_mix_norm, after=[g_w_down])
    (p_down,) = sum_cores("sum_cores_down", [g_w_down], [s_down], core, after=[dy])
    (c_down,) = from_chips("w_down_to_chips", [p_down])
    g_w_gate, g_w_up = weight_grads("grad_w_gate_up", [dgate, dup], h2, after=[p_down])
    s_gate, s_up = from_sibling("w_gate_up_to_sibling", [g_w_gate, g_w_up])
    g_w_out = weight_grad("grad_w_out", mix, dy, after=[g_w_up])
    (s_out,) = from_sibling("w_out_to_sibling", [g_w_out])
    dsgu, d_attn_out, d_sgu_out, dattn_r = mix_bwd(dy, w_out_f, attn, sgu, attn_out_norm, sgu_out_norm, after=[g_w_out, c_down])
    p_gate, p_up = sum_cores("sum_cores_gate_up", [g_w_gate, g_w_up], [s_gate, s_up], core, after=[dsgu])
    c_gate, c_up = from_chips("w_gate_up_to_chips", [p_gate, p_up])
    du, dvs, d_ln_gain, d_ln_bias, d_w_sp, d_bfull = sgu_bwd(
        u, vs, dsgu, sgu_ln_gain, sgu_ln_bias, w_sp, bfull, after=[p_gate, p_up])

    d_b_sp = d_bfull.reshape(CHUNK, N_GROUPS, HEAD_DIM).sum(axis=-1).T
    small_g = pack_small(dict(sgu_ln_gain=d_ln_gain, sgu_ln_bias=d_ln_bias, sgu_w_spatial=d_w_sp, sgu_b_spatial=d_b_sp,
                              attn_out_norm=d_attn_out, sgu_out_norm=d_sgu_out, post_mix_norm=d_post_mix,
                              pre_ffn_norm=d_pre_ffn, post_ffn_norm=d_post_ffn, loss_sum=sq_err))
    small_g = small_g.reshape(N_DEV, SMALL_ROWS // N_DEV, 128)
    ((_, (o_small,)),) = by_sequencer("small_to_owners", [to_owners(small_g)], TO_ALL)

    dq, dk, dv = attn_bwd(q, k, v, attn_r, lse, dattn_r, pos_r, rot)
    summed_small = sum_owned("sum_small", small_g, o_small, dev_ids, after=[dq, c_gate, c_up])
    (all_small,) = gathered("gather_small_grads", [summed_small])
    (p_out,) = sum_cores("sum_cores_out", [g_w_out], [s_out], core, after=[dq])
    (c_out,) = from_chips("w_out_to_chips", [p_out])
    dproj = [dq, dk, dv, du, dvs]
    g_w_in = weight_grad_of_parts("grad_w_in", dproj, h1, after=[c_gate, c_up])
    (s_in,) = from_sibling("w_in_to_sibling", [g_w_in])

    def same(t):
        return t

    def turned(t):
        return t.T

    big = {}

    def adamw(call, weights, after):
        results = adamw_of_sums(call, [p for _, _, p, _, _, _, _ in weights], [c for _, _, _, c, _, _, _ in weights], chip_ids,
                                [turn(w[0]) for _, w, _, _, _, _, turn in weights], [turn(m[0]) for _, _, _, _, m, _, turn in weights],
                                [turn(vv[0]) for _, _, _, _, _, vv, turn in weights], after)
        for (name, _, _, _, _, _, turn), outs in zip(weights, results):
            big[name] = tuple(turn(t)[None] for t in outs)
        return results[-1][0]

    done_ffn = adamw("adamw_ffn", (("w_down", w_down, p_down, c_down, m_w_down, v_w_down, same),
                                   ("w_gate", w_gate, p_gate, c_gate, m_w_gate, v_w_gate, turned),
                                   ("w_up", w_up, p_up, c_up, m_w_up, v_w_up, turned)), [g_w_in, all_small])
    (p_in,) = sum_cores("sum_cores_in", [g_w_in], [s_in], core, after=[done_ffn, c_out])
    (c_in,) = from_chips("w_in_to_chips", [p_in])
    grad_x, d_pre_mix = in_bwd(dproj, w_in_t, x2d, pre_mix_norm, dx2, after=[p_in])
    ((_, (late_parts,)),) = by_sequencer("pre_mix_to_everyone", [to_everyone(d_pre_mix)], TO_ALL)
    done_out = adamw("adamw_w_out", (("w_out", w_out, p_out, c_out, m_w_out, v_w_out, same),), [d_pre_mix])
    *early, loss_sum = adamw_small(all_small, None, small_w, small_m, small_v, after=[done_out])
    done_in = adamw("adamw_w_in", (("w_in", w_in, p_in, c_in, m_w_in, v_w_in, turned),), [loss_sum])
    *late, _ = adamw_small(None, (late_parts, d_pre_mix, dev.reshape(1)), small_w, small_m, small_v, after=[done_in])
    sg, sd, snm, snv = ({**a, **b} for a, b in zip(early, late))
    loss = loss_sum[0, 0] * np.float32(0.5 / D_MODEL)

    names = ["pre_mix_norm", "w_in", "sgu_ln_gain", "sgu_ln_bias", "sgu_w_spatial", "sgu_b_spatial", "attn_out_norm",
             "sgu_out_norm", "w_out", "post_mix_norm", "pre_ffn_norm", "w_gate", "w_up", "w_down", "post_ffn_norm"]
    outs = [loss, grad_x[None]]
    for i, table in enumerate((sg, sd, snm, snv)):
        for name in names:
            outs.append(big[name][i] if name in big else table[name])
    return tuple(outs)
```

```python
import numpy as np
import jax
import jax.numpy as jnp
from jax import lax
from jax.experimental import pallas as pl
from jax.experimental.pallas import tpu as pltpu
from jax.experimental.pallas import tpu_sc as plsc

F32 = jnp.float32
BF16 = jnp.bfloat16

SEQ = 2048
D_MODEL = 1024
ATTN_W = 512
SGU_W = 512
HEAD_DIM = 64
N_GROUPS = 8
CHUNK = 128
D_FF = 2816
IN_W = 3 * ATTN_W + 2 * SGU_W
DILATIONS = (1, 4, 16)
ROPE_THETA = 500000.0
ROT_DIM = 16
ROT_HALF = 8
RMS_EPS = 1e-6
LN_EPS = 1e-5
Q_SCALE = 0.125
NEG = -1e30

N_DEV = 8
MESH_AXES = ("x", "y", "c")
MESH = pl.DeviceIdType.MESH

ADAM_LR = 0.001
ADAM_B1 = 0.9
ADAM_B2 = 0.999
ADAM_EPS = 1e-08
ADAM_WD = 0.01
ADAM_STEP = 10

VMEM_LIMIT = 60 * 1024 * 1024
ANY = pl.BlockSpec(memory_space=pl.ANY)

SMALL = (("pre_mix_norm", 1024), ("sgu_ln_gain", 512), ("sgu_ln_bias", 512), ("sgu_w_spatial", 8 * 128 * 128),
         ("sgu_b_spatial", 1024), ("attn_out_norm", 512), ("sgu_out_norm", 512), ("post_mix_norm", 1024),
         ("pre_ffn_norm", 1024), ("post_ffn_norm", 1024), ("loss_sum", 1))
SMALL_ROWS = 1152


def _params(sem=("arbitrary",)):
    return pltpu.CompilerParams(dimension_semantics=sem, vmem_limit_bytes=VMEM_LIMIT)


def _dot(a, b):
    return jnp.dot(a, b, preferred_element_type=F32)


def _dot_nt(a, b):
    return lax.dot_general(a, b, (((1,), (1,)), ((), ())), preferred_element_type=F32)


def _dot_tn(a, b):
    return lax.dot_general(a, b, (((0,), (0,)), ((), ())), preferred_element_type=F32)


def _rms(z):
    return lax.rsqrt(jnp.mean(z * z, axis=-1, keepdims=True) + RMS_EPS)


def _rms_bwd(z, gain, d):
    r = _rms(z)
    n = z * r
    dn = d * gain
    dz = r * (dn - n * jnp.mean(dn * n, axis=-1, keepdims=True))
    return dz, jnp.sum(d * n, axis=0, keepdims=True)


def _gelu(z):
    return 0.5 * z * (1.0 + lax.erf(z * np.float32(1.0 / np.sqrt(2.0))))


def _gelu_grad(z):
    cdf = 0.5 * (1.0 + lax.erf(z * np.float32(1.0 / np.sqrt(2.0))))
    return cdf + z * jnp.exp(-0.5 * z * z) * np.float32(1.0 / np.sqrt(2.0 * np.pi))


def _rot_tables(pos_col, invf, ma, mb):
    ang = pos_col.astype(F32) * invf
    s = jnp.sin(ang)
    return jnp.cos(ang), s * ma, s * mb


def _rot(t, c, sa, sb):
    return t * c + pltpu.roll(t, 120, 1) * sa + pltpu.roll(t, 8, 1) * sb


def _rot_t(d, c, sa, sb):
    return d * c + pltpu.roll(d * sa, 8, 1) + pltpu.roll(d * sb, 120, 1)


def _rot_consts():
    lane = np.arange(128) % HEAD_DIM
    inv_freq = (np.float32(ROPE_THETA) ** (-np.arange(0, ROT_DIM, 2, dtype=np.float32) / np.float32(ROT_DIM))).astype(np.float32)
    invf = np.where(lane < ROT_DIM, inv_freq[lane % ROT_HALF], 0.0).astype(np.float32)
    ma = np.where(lane < ROT_HALF, -1.0, 0.0).astype(np.float32)
    mb = np.where((lane >= ROT_HALF) & (lane < ROT_DIM), 1.0, 0.0).astype(np.float32)
    return jnp.asarray(invf[None]), jnp.asarray(ma[None]), jnp.asarray(mb[None])


def _row_spec(tm, w):
    return pl.BlockSpec((tm, w), lambda i: (i, 0))


def _full_spec(shape):
    return pl.BlockSpec(shape, lambda i: (0,) * len(shape))


def _weight_spec(shape):
    return pl.BlockSpec(shape, lambda i: (0,) * len(shape), pipeline_mode=pl.Buffered(1))


FF_CHUNKS = (256, 512, 1024, 1024)
FF_SPANS = [(int(o), n) for o, n in zip(np.cumsum((0,) + FF_CHUNKS[:-1]), FF_CHUNKS)]
FF_WHOLE = [(0, D_FF)]


def _ffn_weight_scratch(n_weights, spans):
    return [pltpu.VMEM((D_FF, D_MODEL), BF16)] * n_weights + [pltpu.SemaphoreType.DMA((n_weights, len(spans)))]


def _with_ffn_weights(w_hbm, w_vmem, sems, spans, run):
    copies = [[pltpu.make_async_copy(h.at[pl.ds(o, n)], v.at[pl.ds(o, n)], sems.at[j, c]) for c, (o, n) in enumerate(spans)]
              for j, (h, v) in enumerate(zip(w_hbm, w_vmem))]
    first = pl.program_id(0) == 0

    @pl.when(first)
    def _():
        for of_weight in copies:
            for cp in of_weight:
                cp.start()
        run(lambda j, c: copies[j][c].wait())

    @pl.when(jnp.logical_not(first))
    def _():
        run(None)


RES = 16


def _residue_scratch(n_arrays, tm, width):
    return [pltpu.VMEM((2, n_arrays, tm // RES, RES, width), F32), pltpu.SemaphoreType.DMA((2, n_arrays, RES))]


def _to_residue_rows(tiles, outs, scratch, sems, tm, n_steps):
    i = pl.program_id(0)
    slot = i % 2
    per = tm // RES

    def copies(step, s):
        return [pltpu.make_async_copy(scratch.at[s, a, :, b, :],
                                      outs[a].at[pl.ds(pl.multiple_of(b * (SEQ // RES) + per * step, per), per), :],
                                      sems.at[s, a, b]) for a in range(len(outs)) for b in range(RES)]

    @pl.when(i >= 2)
    def _():
        for cp in copies(i - 2, slot):
            cp.wait()

    for a, tile in enumerate(tiles):
        scratch[slot, a] = tile.reshape(per, RES, tile.shape[-1])
    for cp in copies(i, slot):
        cp.start()

    @pl.when(i == n_steps - 1)
    def _():
        for cp in copies(i - 1, 1 - slot) + copies(i, slot):
            cp.wait()


def pre_norm(x, g1, after=()):
    tm = 512

    def body(x_ref, g_ref, h_ref):
        xf = x_ref[...]
        h_ref[...] = (xf * _rms(xf) * g_ref[...]).astype(BF16)

    return _call("pre_norm", body, SEQ // tm, [_row_spec(tm, D_MODEL), _full_spec((1, D_MODEL))], [_row_spec(tm, D_MODEL)],
                 [jax.ShapeDtypeStruct((SEQ, D_MODEL), BF16)], (x, g1), after=after)[0]


def in_proj(h1, pos_col, w_in_t, rot):
    tm = 512
    n_steps = SEQ // tm

    def body(h_ref, pos_ref, w_ref, invf_ref, ma_ref, mb_ref, u_ref, vs_ref, q_ref, k_ref, v_ref, scratch, sems):
        proj = _dot_nt(h_ref[...], w_ref[...])
        c, sa, sb = _rot_tables(pos_ref[...], invf_ref[...], ma_ref[...], mb_ref[...])
        slabs = range(ATTN_W // 128)
        q = jnp.concatenate([_rot(proj[:, j * 128:(j + 1) * 128], c, sa, sb) * Q_SCALE for j in slabs], axis=1)
        k = jnp.concatenate([_rot(proj[:, ATTN_W + j * 128:ATTN_W + (j + 1) * 128], c, sa, sb) for j in slabs], axis=1)
        u_ref[...] = proj[:, 3 * ATTN_W:3 * ATTN_W + SGU_W]
        vs_ref[...] = proj[:, 3 * ATTN_W + SGU_W:]
        _to_residue_rows([q, k, proj[:, 2 * ATTN_W:3 * ATTN_W]], [q_ref, k_ref, v_ref], scratch, sems, tm, n_steps)

    act = jax.ShapeDtypeStruct((SEQ, 512), F32)
    return _call(
        "in_proj", body, n_steps,
        [_row_spec(tm, D_MODEL), _row_spec(tm, 1), _weight_spec((IN_W, D_MODEL)),
         _full_spec((1, 128)), _full_spec((1, 128)), _full_spec((1, 128))],
        [_row_spec(tm, 512)] * 2 + [ANY] * 3, [act] * 5,
        (h1, pos_col, w_in_t, *rot), scratch_shapes=_residue_scratch(3, tm, ATTN_W))


def _to_residue_order(t):
    return t.reshape(SEQ // RES, RES, -1).transpose(1, 0, 2).reshape(t.shape)


def _block_rows(d, r, n):
    if d == 16:
        slices = [(128 * r, 128)]
    elif d == 4:
        slices = [(128 * (4 * b + r) + 32 * n, 32) for b in range(4)]
    else:
        slices = [(128 * b + 8 * n, 8) for b in range(RES)]
    return [(s if isinstance(s, int) else pl.multiple_of(s, z), z) for s, z in slices]


def _block_step(d, i):
    if d == 16:
        return i
    if d == 4:
        return 4 * (i & 31) + (i >> 5)
    return 16 * (i & 7) + (i >> 3)


def _attn_masks(d):
    row2 = _block_step(d, lax.broadcasted_iota(jnp.int32, (128, 256), 0))
    col2 = lax.broadcasted_iota(jnp.int32, (128, 256), 1)
    key2 = _block_step(d, col2 & 127)
    mask2 = jnp.logical_or(jnp.logical_and(col2 < 128, key2 >= row2), jnp.logical_and(col2 >= 128, key2 <= row2))
    row1 = _block_step(d, lax.broadcasted_iota(jnp.int32, (128, 128), 0))
    col1 = lax.broadcasted_iota(jnp.int32, (128, 128), 1)
    return col1 < HEAD_DIM, _block_step(d, col1) <= row1, mask2


def _load_rows(ref, slices):
    parts = [ref[pl.ds(s, z), :] for s, z in slices]
    return parts[0] if len(parts) == 1 else jnp.concatenate(parts, axis=0)


def _for_each_group(fn):
    for p, d in enumerate(DILATIONS):
        masks = _attn_masks(d)
        if d == 16:
            def group(i, carry, p=p, masks=masks):
                fn(p, masks, [(_block_rows(16, 8 * i + g, 0), None) for g in range(8)])
                return carry

            lax.fori_loop(0, 2, group, 0)
        elif d == 4:
            fn(p, masks, [(_block_rows(4, r, 0), None) for r in range(4)])

            def group(i, carry, p=p, masks=masks):
                blocks = [6 * i + g for g in range(6)]
                fn(p, masks, [(_block_rows(4, j % 4, 1 + j // 4), _block_rows(4, j % 4, j // 4)) for j in blocks])
                return carry

            lax.fori_loop(0, 2, group, 0)
        else:
            fn(p, masks, [(_block_rows(1, 0, 0), None)])

            def group(i, carry, p=p, masks=masks):
                fn(p, masks, [(_block_rows(1, 0, 5 * i + g + 1), _block_rows(1, 0, 5 * i + g)) for g in range(5)])
                return carry

            lax.fori_loop(0, 3, group, 0)


def attn_fwd(q, k, v):
    def body(q_ref, k_ref, v_ref, o_ref, lse_ref, nat_ref, op_ref, lp_ref, sems):
        def group(p, masks, blocks):
            head0, mask1, mask2 = masks
            heads = (head0, jnp.logical_not(head0))
            keys = [rows if prev is None else prev + rows for rows, prev in blocks]
            mask = [mask1 if prev is None else mask2 for _, prev in blocks]
            qb = [_load_rows(q_ref, rows) for rows, _ in blocks]
            kk = [_load_rows(k_ref, ks).astype(BF16) for ks in keys]
            vv = [_load_rows(v_ref, ks).astype(BF16) for ks in keys]
            chains = [(g, hm) for g in range(len(blocks)) for hm in heads]
            s = [jnp.where(mask[g], _dot_nt(jnp.where(hm, qb[g], 0.0).astype(BF16), kk[g]), NEG) for g, hm in chains]
            m = [jnp.max(t, axis=-1, keepdims=True) for t in s]
            e = [jnp.exp(t - mt) for t, mt in zip(s, m)]
            l = [jnp.sum(t, axis=-1, keepdims=True) for t in e]
            pv = [_dot(t.astype(BF16), vv[g]) for t, (g, _) in zip(e, chains)]
            for g, (rows, _) in enumerate(blocks):
                o_blk = jnp.where(head0, pv[2 * g] / l[2 * g], pv[2 * g + 1] / l[2 * g + 1])
                l_blk = jnp.where(head0, jnp.broadcast_to(m[2 * g] + jnp.log(l[2 * g]), (128, 128)),
                                  jnp.broadcast_to(m[2 * g + 1] + jnp.log(l[2 * g + 1]), (128, 128)))
                at = 0
                for start, size in rows:
                    op_ref[p, pl.ds(start, size), :] = o_blk[at:at + size]
                    lp_ref[p, pl.ds(start, size), :] = l_blk[at:at + size]
                    at += size

        _for_each_group(group)

        def combine(i, carry):
            rows = pl.ds(pl.multiple_of(i * 256, 256), 256)
            ls = [lp_ref[p, rows, :] for p in range(3)]
            m = jnp.maximum(jnp.maximum(ls[0], ls[1]), ls[2])
            lse = m + jnp.log(jnp.exp(ls[0] - m) + jnp.exp(ls[1] - m) + jnp.exp(ls[2] - m))
            o = jnp.zeros((256, 128), F32)
            for p in range(3):
                o = o + jnp.exp(ls[p] - lse) * op_ref[p, rows, :]
            o_ref[rows, :] = o
            lse_ref[rows, :] = lse
            return carry

        lax.fori_loop(0, SEQ // 256, combine, 0)

        lanes = pl.ds(pl.multiple_of(pl.program_id(0) * 128, 128), 128)
        back = [pltpu.make_async_copy(o_ref.at[pl.ds(b * (SEQ // RES), SEQ // RES), :], nat_ref.at[:, b, lanes], sems.at[b])
                for b in range(RES)]
        for cp in back:
            cp.start()
        for cp in back:
            cp.wait()

    slab = pl.BlockSpec((SEQ, 128), lambda i: (0, i))
    out = jax.ShapeDtypeStruct((SEQ, ATTN_W), F32)
    attn_r, lse, attn = _call(
        "attn_fwd", body, ATTN_W // 128, [slab] * 3, [slab] * 2 + [ANY],
        [out, out, jax.ShapeDtypeStruct((SEQ // RES, RES, ATTN_W), F32)], (q, k, v),
        scratch_shapes=[pltpu.VMEM((3, SEQ, 128), F32), pltpu.VMEM((3, SEQ, 128), F32), pltpu.SemaphoreType.DMA((RES,))])
    return attn_r, lse, attn.reshape(SEQ, ATTN_W)


def _causal_weights(w_ref):
    row = lax.broadcasted_iota(jnp.int32, (CHUNK, CHUNK), 0)
    col = lax.broadcasted_iota(jnp.int32, (CHUNK, CHUNK), 1)
    return [jnp.where(col <= row, w_ref[g], 0.0).astype(BF16) for g in range(N_GROUPS)], col <= row


def _sgu_chunk_fwd(u, vs, lg, lb, wc, bfull, head0):
    ug = _gelu(u)
    vg = _gelu(vs)
    xc = vg - jnp.mean(vg, axis=-1, keepdims=True)
    rstd = lax.rsqrt(jnp.mean(xc * xc, axis=-1, keepdims=True) + LN_EPS)
    xhat = xc * rstd
    vn = xhat * lg + lb
    mixed = []
    for gp in range(SGU_W // 128):
        vp = vn[:, gp * 128:(gp + 1) * 128].astype(BF16)
        mixed.append(jnp.where(head0, _dot(wc[2 * gp], vp), _dot(wc[2 * gp + 1], vp)))
    ms = jnp.concatenate(mixed, axis=1) + bfull
    return ug, xhat, rstd, vn, ms


def sgu_fwd(u, vs, lg, lb, w_sp, bfull):
    cpb = 4

    def body(u_ref, vs_ref, lg_ref, lb_ref, w_ref, b_ref, o_ref):
        wc, _ = _causal_weights(w_ref)
        head0 = lax.broadcasted_iota(jnp.int32, (CHUNK, 128), 1) < HEAD_DIM
        for ci in range(cpb):
            rows = pl.ds(ci * CHUNK, CHUNK)
            ug, _, _, _, ms = _sgu_chunk_fwd(u_ref[rows, :], vs_ref[rows, :], lg_ref[...], lb_ref[...], wc, b_ref[...], head0)
            o_ref[rows, :] = ug * ms

    tm = cpb * CHUNK
    return _call(
        "sgu_fwd", body, SEQ // tm,
        [_row_spec(tm, SGU_W), _row_spec(tm, SGU_W), _full_spec((1, SGU_W)), _full_spec((1, SGU_W)),
         _full_spec((N_GROUPS, CHUNK, CHUNK)), _full_spec((CHUNK, SGU_W))],
        [_row_spec(tm, SGU_W)], [jax.ShapeDtypeStruct((SEQ, SGU_W), F32)],
        (u, vs, lg, lb, w_sp, bfull))


def out_proj(attn, sgu, x, ga, gs, w_out, gpm, gpf):
    tm = 512

    def body(a_ref, s_ref, x_ref, ga_ref, gs_ref, w_ref, gpm_ref, gpf_ref, mix_ref, y_ref, x2_ref, h2_ref):
        a = a_ref[...]
        s = s_ref[...]
        an = (a * _rms(a) * ga_ref[...]).astype(BF16)
        sn = (s * _rms(s) * gs_ref[...]).astype(BF16)
        mix_ref[:, :ATTN_W] = an
        mix_ref[:, ATTN_W:] = sn
        y = _dot(an, w_ref[:ATTN_W, :]) + _dot(sn, w_ref[ATTN_W:, :])
        y_ref[...] = y
        x2 = x_ref[...] + y * _rms(y) * gpm_ref[...]
        x2_ref[...] = x2
        h2_ref[...] = (x2 * _rms(x2) * gpf_ref[...]).astype(BF16)

    wide = jax.ShapeDtypeStruct((SEQ, D_MODEL), F32)
    wide16 = jax.ShapeDtypeStruct((SEQ, D_MODEL), BF16)
    return _call(
        "out_proj", body, SEQ // tm,
        [_row_spec(tm, ATTN_W), _row_spec(tm, SGU_W), _row_spec(tm, D_MODEL), _full_spec((1, ATTN_W)),
         _full_spec((1, SGU_W)), _weight_spec((D_MODEL, D_MODEL)), _full_spec((1, D_MODEL)), _full_spec((1, D_MODEL))],
        [_row_spec(tm, D_MODEL)] * 4, [wide16, wide, wide, wide16],
        (attn, sgu, x, ga, gs, w_out, gpm, gpf))


def ffn_up(h2, w_gate_t, w_up_t):
    tm = 256

    def body(h_ref, wg_hbm, wu_hbm, g_ref, u_ref, a_ref, wg_ref, wu_ref, sems):
        def run(wait):
            h = h_ref[...]
            if wait:
                wait(0, 0)
            g = _dot_nt(h, wg_ref[...])
            g_ref[...] = g.astype(BF16)
            if wait:
                wait(1, 0)
            u = _dot_nt(h, wu_ref[...])
            u_ref[...] = u.astype(BF16)
            a_ref[...] = (g * jax.nn.sigmoid(g) * u).astype(BF16)

        _with_ffn_weights([wg_hbm, wu_hbm], [wg_ref, wu_ref], sems, FF_WHOLE, run)

    ff = jax.ShapeDtypeStruct((SEQ, D_FF), BF16)
    return _call(
        "ffn_up", body, SEQ // tm, [_row_spec(tm, D_MODEL), ANY, ANY],
        [_row_spec(tm, D_FF)] * 3, [ff, ff, jax.ShapeDtypeStruct((SEQ, D_FF), BF16)],
        (h2, w_gate_t, w_up_t), scratch_shapes=_ffn_weight_scratch(2, FF_WHOLE))


def ffn_down_loss(act, w_down, x2, gpo, target):
    tm = 512

    def body(a_ref, w_hbm, x2_ref, g_ref, t_ref, df_ref, dx3_ref, dg_ref, loss_ref, w_ref, sems):
        def run(wait):
            f = None
            for c, (o, n) in enumerate(FF_SPANS if wait else FF_WHOLE):
                if wait:
                    wait(0, c)
                part = _dot(a_ref[:, o:o + n], w_ref[o:o + n, :])
                f = part if f is None else f + part
            gain = g_ref[...]
            err = x2_ref[...] + f * _rms(f) * gain - t_ref[...]
            dx3 = err * np.float32(1.0 / D_MODEL)
            dx3_ref[...] = dx3
            df, dg = _rms_bwd(f, gain, dx3)
            df_ref[...] = df.astype(BF16)
            loss = jnp.sum(err * err, axis=(0, 1), keepdims=True)
            if wait:
                dg_ref[...] = dg
                loss_ref[...] = loss
            else:
                dg_ref[...] += dg
                loss_ref[...] += loss

        _with_ffn_weights([w_hbm], [w_ref], sems, FF_SPANS, run)

    return _call(
        "ffn_down_loss", body, SEQ // tm,
        [_row_spec(tm, D_FF), ANY, _row_spec(tm, D_MODEL), _full_spec((1, D_MODEL)), _row_spec(tm, D_MODEL)],
        [_row_spec(tm, D_MODEL), _row_spec(tm, D_MODEL), _full_spec((1, D_MODEL)), _full_spec((1, 1))],
        [jax.ShapeDtypeStruct((SEQ, D_MODEL), BF16), jax.ShapeDtypeStruct((SEQ, D_MODEL), F32),
         jax.ShapeDtypeStruct((1, D_MODEL), F32), jax.ShapeDtypeStruct((1, 1), F32)],
        (act, w_down, x2, gpo, target), scratch_shapes=_ffn_weight_scratch(1, FF_SPANS))


def ffn_bwd(df, w_down, gate, up, w_gate_t, w_up_t, x2, gpf, dx3, y, gpm, after=()):
    tm = 256

    def body(df_ref, wd_hbm, g_ref, u_ref, wg_hbm, wu_hbm, x2_ref, gpf_ref, dx3_ref, y_ref, gpm_ref,
             dg_ref, du_ref, dx2_ref, dy_ref, dgpf_ref, dgpm_ref, wd_ref, wg_ref, wu_ref, sems):
        def run(wait):
            if wait:
                wait(0, 0)
            dact = _dot_nt(df_ref[...], wd_ref[...])
            g = g_ref[...].astype(F32)
            s = jax.nn.sigmoid(g)
            dup = (dact * g * s).astype(BF16)
            dgate = (dact * u_ref[...].astype(F32) * (s * (1.0 + g * (1.0 - s)))).astype(BF16)
            du_ref[...] = dup
            dg_ref[...] = dgate
            if wait:
                wait(1, 0)
                wait(2, 0)
            dh2 = _dot(dgate, wg_ref[...]) + _dot(dup, wu_ref[...])
            dz, dgpf = _rms_bwd(x2_ref[...], gpf_ref[...], dh2)
            dx2 = dx3_ref[...] + dz
            dx2_ref[...] = dx2
            dy, dgpm = _rms_bwd(y_ref[...], gpm_ref[...], dx2)
            dy_ref[...] = dy.astype(BF16)
            if wait:
                dgpf_ref[...] = dgpf
                dgpm_ref[...] = dgpm
            else:
                dgpf_ref[...] += dgpf
                dgpm_ref[...] += dgpm

        _with_ffn_weights([wd_hbm, wg_hbm, wu_hbm], [wd_ref, wg_ref, wu_ref], sems, FF_WHOLE, run)

    vec = jax.ShapeDtypeStruct((1, D_MODEL), F32)
    ff16 = jax.ShapeDtypeStruct((SEQ, D_FF), BF16)
    return _call(
        "ffn_bwd", body, SEQ // tm,
        [_row_spec(tm, D_MODEL), ANY, _row_spec(tm, D_FF), _row_spec(tm, D_FF), ANY, ANY, _row_spec(tm, D_MODEL),
         _full_spec((1, D_MODEL)), _row_spec(tm, D_MODEL), _row_spec(tm, D_MODEL), _full_spec((1, D_MODEL))],
        [_row_spec(tm, D_FF), _row_spec(tm, D_FF), _row_spec(tm, D_MODEL), _row_spec(tm, D_MODEL),
         _full_spec((1, D_MODEL)), _full_spec((1, D_MODEL))],
        [ff16, ff16, jax.ShapeDtypeStruct((SEQ, D_MODEL), F32), jax.ShapeDtypeStruct((SEQ, D_MODEL), BF16), vec, vec],
        (df, w_down, gate, up, w_gate_t, w_up_t, x2, gpf, dx3, y, gpm), scratch_shapes=_ffn_weight_scratch(3, FF_WHOLE), after=after)


def weight_grads(name, lhs, b, after=()):
    m, n, k = lhs[0].shape[1], b.shape[1], len(lhs)
    tr = 512 if m % 512 == 0 else 256

    def body(*refs):
        for a_ref, o_ref in zip(refs[:k], refs[k + 1:]):
            o_ref[...] = _dot_tn(a_ref[...], refs[k][...]).astype(BF16)

    outs = _call(
        name, body, m // tr, [pl.BlockSpec((SEQ, tr), lambda i: (0, i))] * k + [_weight_spec((SEQ, n))],
        [_row_spec(tr, n)] * k, [jax.ShapeDtypeStruct((m, n), BF16)] * k, (*lhs, b), after=after)
    return [out.reshape(N_DEV, m // N_DEV, n) for out in outs]


def weight_grad(name, a, b, after=()):
    return weight_grads(name, [a], b, after)[0]


def weight_grad_of_parts(name, parts, b, after=()):
    p, n, k = parts[0].shape[1], b.shape[1], len(parts)
    tr = 512
    per = p // tr

    def body(*refs):
        tile = pl.program_id(0)
        for j in range(k):
            @pl.when(tile // per == j)
            def _(j=j):
                refs[k + 1][...] = _dot_tn(refs[j][...].astype(BF16), refs[k][...]).astype(BF16)

    def part_spec(j):
        return pl.BlockSpec((SEQ, tr), lambda i: (0, jnp.clip(i - per * j, 0, per - 1)))

    (out,) = _call(
        name, body, k * per, [part_spec(j) for j in range(k)] + [_weight_spec((SEQ, n))],
        [_row_spec(tr, n)], [jax.ShapeDtypeStruct((k * p, n), BF16)], (*parts, b), after=after)
    return out.reshape(N_DEV, k * p // N_DEV, n)


def mix_bwd(dy, w_out, attn, sgu, ga, gs, after=()):
    tm = 512
    n_steps = SEQ // tm

    def body(dy_ref, w_ref, a_ref, s_ref, ga_ref, gs_ref, ds_ref, dga_ref, dgs_ref, da_ref, scratch, sems):
        dy = dy_ref[...]
        da, dga = _rms_bwd(a_ref[...], ga_ref[...], _dot_nt(dy, w_ref[:ATTN_W, :]))
        ds, dgs = _rms_bwd(s_ref[...], gs_ref[...], _dot_nt(dy, w_ref[ATTN_W:, :]))
        ds_ref[...] = ds
        _to_residue_rows([da], [da_ref], scratch, sems, tm, n_steps)

        @pl.when(pl.program_id(0) == 0)
        def _():
            dga_ref[...] = jnp.zeros_like(dga_ref)
            dgs_ref[...] = jnp.zeros_like(dgs_ref)

        dga_ref[...] += dga
        dgs_ref[...] += dgs

    half = jax.ShapeDtypeStruct((SEQ, 512), F32)
    vec = jax.ShapeDtypeStruct((1, 512), F32)
    return _call(
        "mix_bwd", body, n_steps,
        [_row_spec(tm, D_MODEL), _weight_spec((D_MODEL, D_MODEL)), _row_spec(tm, 512), _row_spec(tm, 512),
         _full_spec((1, 512)), _full_spec((1, 512))],
        [_row_spec(tm, 512), _full_spec((1, 512)), _full_spec((1, 512)), ANY],
        [half, vec, vec, half], (dy, w_out, attn, sgu, ga, gs), scratch_shapes=_residue_scratch(1, tm, ATTN_W), after=after)


def sgu_bwd(u, vs, dsgu, lg, lb, w_sp, bfull, after=()):
    cpb = 4

    def body(u_ref, vs_ref, d_ref, lg_ref, lb_ref, w_ref, b_ref, du_ref, dvs_ref, dlg_ref, dlb_ref, dw_ref, db_ref):
        wc, causal = _causal_weights(w_ref)
        head0 = lax.broadcasted_iota(jnp.int32, (CHUNK, 128), 1) < HEAD_DIM
        lg = lg_ref[...]

        @pl.when(pl.program_id(0) == 0)
        def _():
            dlg_ref[...] = jnp.zeros_like(dlg_ref)
            dlb_ref[...] = jnp.zeros_like(dlb_ref)
            dw_ref[...] = jnp.zeros_like(dw_ref)
            db_ref[...] = jnp.zeros_like(db_ref)

        for ci in range(cpb):
            rows = pl.ds(ci * CHUNK, CHUNK)
            u = u_ref[rows, :]
            vs = vs_ref[rows, :]
            d = d_ref[rows, :]
            ug, xhat, rstd, vn, ms = _sgu_chunk_fwd(u, vs, lg, lb_ref[...], wc, b_ref[...], head0)
            du_ref[rows, :] = (d * ms * _gelu_grad(u)).astype(BF16)
            dms = d * ug
            db_ref[...] += dms
            dvn = []
            for gp in range(SGU_W // 128):
                dmp = dms[:, gp * 128:(gp + 1) * 128]
                dm0 = jnp.where(head0, dmp, 0.0).astype(BF16)
                dm1 = jnp.where(head0, 0.0, dmp).astype(BF16)
                vp = vn[:, gp * 128:(gp + 1) * 128].astype(BF16)
                dw_ref[2 * gp] += _dot_nt(dm0, vp)
                dw_ref[2 * gp + 1] += _dot_nt(dm1, vp)
                dvn.append(_dot_tn(wc[2 * gp], dm0) + _dot_tn(wc[2 * gp + 1], dm1))
            dvn = jnp.concatenate(dvn, axis=1)
            dlg_ref[...] += jnp.sum(dvn * xhat, axis=0, keepdims=True)
            dlb_ref[...] += jnp.sum(dvn, axis=0, keepdims=True)
            dxh = dvn * lg
            dvg = rstd * (dxh - jnp.mean(dxh, axis=-1, keepdims=True) - xhat * jnp.mean(dxh * xhat, axis=-1, keepdims=True))
            dvs_ref[rows, :] = (dvg * _gelu_grad(vs)).astype(BF16)

        @pl.when(pl.program_id(0) == pl.num_programs(0) - 1)
        def _():
            for g in range(N_GROUPS):
                dw_ref[g] = jnp.where(causal, dw_ref[g], 0.0)

    tm = cpb * CHUNK
    half16 = jax.ShapeDtypeStruct((SEQ, SGU_W), BF16)
    vec = jax.ShapeDtypeStruct((1, SGU_W), F32)
    return _call(
        "sgu_bwd", body, SEQ // tm,
        [_row_spec(tm, SGU_W)] * 3 + [_full_spec((1, SGU_W)), _full_spec((1, SGU_W)),
                                      _full_spec((N_GROUPS, CHUNK, CHUNK)), _full_spec((CHUNK, SGU_W))],
        [_row_spec(tm, SGU_W), _row_spec(tm, SGU_W), _full_spec((1, SGU_W)), _full_spec((1, SGU_W)),
         _full_spec((N_GROUPS, CHUNK, CHUNK)), _full_spec((CHUNK, SGU_W))],
        [half16, half16, vec, vec, jax.ShapeDtypeStruct((N_GROUPS, CHUNK, CHUNK), F32),
         jax.ShapeDtypeStruct((CHUNK, SGU_W), F32)],
        (u, vs, dsgu, lg, lb, w_sp, bfull), after=after)


def attn_bwd(q, k, v, o, lse, do, pos_col, rot):
    n_steps = ATTN_W // 128

    def body(q_ref, k_ref, v_ref, o_ref, lse_ref, do_ref, pos_ref, invf_ref, ma_ref, mb_ref,
             dq_ref, dk_ref, dv_ref, dqa_ref, dka_ref, dva_ref, dlt_ref, rot_ref, out_ref, sems):
        step = pl.program_id(0)
        slot = step % 2

        def back(at_step, s):
            lanes = pl.ds(pl.multiple_of(at_step * 128, 128), 128)
            return [pltpu.make_async_copy(out_ref.at[s, a, pl.ds(b * (SEQ // RES), SEQ // RES), :], nat.at[:, b, lanes],
                                          sems.at[s, a, b]) for a, nat in enumerate((dq_ref, dk_ref, dv_ref)) for b in range(RES)]

        dqa_ref[...] = jnp.zeros_like(dqa_ref)
        dka_ref[...] = jnp.zeros_like(dka_ref)
        dva_ref[...] = jnp.zeros_like(dva_ref)

        def delta(i, carry):
            rows = pl.ds(pl.multiple_of(i * 256, 256), 256)
            prod = do_ref[rows, :] * o_ref[rows, :]
            h0 = lax.broadcasted_iota(jnp.int32, (256, 128), 1) < HEAD_DIM
            d0 = jnp.sum(jnp.where(h0, prod, 0.0), axis=-1, keepdims=True)
            d1 = jnp.sum(jnp.where(h0, 0.0, prod), axis=-1, keepdims=True)
            dlt_ref[rows, :] = jnp.where(h0, d0, d1)
            return carry

        lax.fori_loop(0, SEQ // 256, delta, 0)

        def add_rows(ref, slices, val):
            at = 0
            for start, size in slices:
                ref[pl.ds(start, size), :] += val[at:at + size]
                at += size

        def group(p, masks, blocks):
            head0, mask1, mask2 = masks
            heads = (head0, jnp.logical_not(head0))
            keys = [rows if prev is None else prev + rows for rows, prev in blocks]
            mask = [mask1 if prev is None else mask2 for _, prev in blocks]
            kk = [_load_rows(k_ref, ks).astype(BF16) for ks in keys]
            vv = [_load_rows(v_ref, ks).astype(BF16) for ks in keys]
            qb = [_load_rows(q_ref, rows) for rows, _ in blocks]
            dob = [_load_rows(do_ref, rows) for rows, _ in blocks]
            lse_b = [_load_rows(lse_ref, rows) for rows, _ in blocks]
            dlt_b = [_load_rows(dlt_ref, rows) for rows, _ in blocks]
            chains = [(g, h) for g in range(len(blocks)) for h in range(2)]
            qm = [jnp.where(heads[h], qb[g], 0.0).astype(BF16) for g, h in chains]
            dom = [jnp.where(heads[h], dob[g], 0.0).astype(BF16) for g, h in chains]
            s = [_dot_nt(qm[c], kk[g]) for c, (g, h) in enumerate(chains)]
            dp = [_dot_nt(dom[c], vv[g]) for c, (g, h) in enumerate(chains)]
            pr = [jnp.where(mask[g], jnp.exp(s[c] - lse_b[g][:, h * HEAD_DIM:h * HEAD_DIM + 1]), 0.0)
                  for c, (g, h) in enumerate(chains)]
            ds = [(pr[c] * (dp[c] - dlt_b[g][:, h * HEAD_DIM:h * HEAD_DIM + 1])).astype(BF16)
                  for c, (g, h) in enumerate(chains)]
            dv = [_dot_tn(pr[c].astype(BF16), dom[c]) for c in range(len(chains))]
            dk = [_dot_tn(ds[c], qm[c]) for c in range(len(chains))]
            dq = [_dot(ds[c], kk[g]) for c, (g, h) in enumerate(chains)]
            for g, (rows, _) in enumerate(blocks):
                add_rows(dqa_ref, rows, jnp.where(head0, dq[2 * g], dq[2 * g + 1]))
                add_rows(dka_ref, keys[g], dk[2 * g] + dk[2 * g + 1])
                add_rows(dva_ref, keys[g], dv[2 * g] + dv[2 * g + 1])

        _for_each_group(group)

        @pl.when(pl.program_id(0) == 0)
        def _():
            def tables(i, carry):
                rows = pl.ds(pl.multiple_of(i * 256, 256), 256)
                c, sa, sb = _rot_tables(pos_ref[rows, :], invf_ref[...], ma_ref[...], mb_ref[...])
                rot_ref[0, rows, :] = c
                rot_ref[1, rows, :] = sa
                rot_ref[2, rows, :] = sb
                return carry

            lax.fori_loop(0, SEQ // 256, tables, 0)

        @pl.when(step >= 2)
        def _():
            for cp in back(step - 2, slot):
                cp.wait()

        def finish(i, carry):
            rows = pl.ds(pl.multiple_of(i * 256, 256), 256)
            c, sa, sb = rot_ref[0, rows, :], rot_ref[1, rows, :], rot_ref[2, rows, :]
            out_ref[slot, 0, rows, :] = _rot_t(dqa_ref[rows, :] * Q_SCALE, c, sa, sb)
            out_ref[slot, 1, rows, :] = _rot_t(dka_ref[rows, :], c, sa, sb)
            out_ref[slot, 2, rows, :] = dva_ref[rows, :]
            return carry

        lax.fori_loop(0, SEQ // 256, finish, 0)
        for cp in back(step, slot):
            cp.start()

        @pl.when(step == n_steps - 1)
        def _():
            for cp in back(step - 1, 1 - slot) + back(step, slot):
                cp.wait()

    slab = pl.BlockSpec((SEQ, 128), lambda i: (0, i))
    out = jax.ShapeDtypeStruct((SEQ // RES, RES, ATTN_W), F32)
    acc = pltpu.VMEM((SEQ, 128), F32)
    outs = _call(
        "attn_bwd", body, n_steps,
        [slab] * 6 + [_full_spec((SEQ, 1)), _full_spec((1, 128)), _full_spec((1, 128)), _full_spec((1, 128))],
        [ANY] * 3, [out, out, out], (q, k, v, o, lse, do, pos_col, *rot),
        scratch_shapes=[acc, acc, acc, acc, pltpu.VMEM((3, SEQ, 128), F32), pltpu.VMEM((2, 3, SEQ, 128), F32),
                        pltpu.SemaphoreType.DMA((2, 3, RES))])
    return [t.reshape(SEQ, ATTN_W) for t in outs]


def in_bwd(dproj_parts, w_in_t, x, g1, dx2, after=()):
    tm = 512
    k = len(dproj_parts)

    def body(*refs):
        w_ref, x_ref, g_ref, dx2_ref, dx_ref, dg_ref = refs[k:]
        dh1 = _dot(refs[0][...].astype(BF16), w_ref[0:512, :])
        for j in range(1, k):
            dh1 = dh1 + _dot(refs[j][...].astype(BF16), w_ref[512 * j:512 * (j + 1), :])
        dz, dg = _rms_bwd(x_ref[...], g_ref[...], dh1)
        dx_ref[...] = dx2_ref[...] + dz

        @pl.when(pl.program_id(0) == 0)
        def _():
            dg_ref[...] = jnp.zeros_like(dg_ref)

        dg_ref[...] += dg

    return _call(
        "in_bwd", body, SEQ // tm,
        [_row_spec(tm, 512)] * k + [_weight_spec((IN_W, D_MODEL)), _row_spec(tm, D_MODEL), _full_spec((1, D_MODEL)),
                                    _row_spec(tm, D_MODEL)],
        [_row_spec(tm, D_MODEL), _full_spec((1, D_MODEL))],
        [jax.ShapeDtypeStruct((SEQ, D_MODEL), F32), jax.ShapeDtypeStruct((1, D_MODEL), F32)],
        (*dproj_parts, w_in_t, x, g1, dx2), after=after)


def _coords():
    return lax.axis_index("x"), lax.axis_index("y"), lax.axis_index("c")


class Exchange:
    def __init__(self, srcs, bufs, new_shapes, n_sems, make):
        self.srcs, self.bufs, self.new_shapes, self.n_sems, self.make = list(srcs), list(bufs), list(new_shapes), n_sems, make


def _call(name, body, n_steps, in_specs, out_specs, out_shape, args, scratch_shapes=(), after=()):
    n_in = len(args)

    def wrapped(*refs):
        body(*refs[:n_in], *refs[n_in + len(after):])

    return list(pl.pallas_call(
        wrapped, name=name, grid=(n_steps,), in_specs=list(in_specs) + [ANY] * len(after), out_specs=list(out_specs),
        out_shape=list(out_shape), scratch_shapes=list(scratch_shapes), compiler_params=_params(),
    )(*args, *after))


GATHER_SEMS = 8


def gather(bufs):
    n = len(bufs)

    def make(src_refs, buf_refs, new_refs, send_sems, recv_sems):
        x, y, c = _coords()
        me, sibling = (x, y, c), (x, y, 1 - c)
        over_x, over_y, across = (1 - x, y), (x, 1 - y), (1 - x, 1 - y)

        def copy(a, k, block, to, half=None):
            r = buf_refs[a].shape[0] // N_DEV
            lo, size = (0, r) if half is None else (half * (r // 2), r // 2)
            rows = buf_refs[a].at[pl.ds((4 * block[0] + 2 * block[1] + block[2]) * r + lo, size), :]
            return pltpu.make_async_remote_copy(
                src_ref=rows, dst_ref=rows, send_sem=send_sems.at[GATHER_SEMS * a + k],
                recv_sem=recv_sems.at[GATHER_SEMS * a + k], device_id=to, device_id_type=MESH)

        every = range(n)
        out = ([copy(a, 0, me, sibling) for a in every] + [copy(a, 1, me, (*over_x, c)) for a in every]
               + [copy(a, 2, me, (*over_y, c)) for a in every])
        near_in = [copy(a, 1, (*over_x, c), me) for a in every] + [copy(a, 2, (*over_y, c), me) for a in every]
        relay = ([copy(a, 3, (*over_x, c), (*over_y, c), half=0) for a in every]
                 + [copy(a, 4, (*over_y, c), (*over_x, c), half=1) for a in every])
        near_on = [copy(a, 5, (*over_x, c), sibling) for a in every] + [copy(a, 6, (*over_y, c), sibling) for a in every]
        relay_in = ([copy(a, 3, (*across, c), me, half=0) for a in every]
                    + [copy(a, 4, (*across, c), me, half=1) for a in every])
        far_on = [copy(a, 7, (*across, c), sibling) for a in every]
        from_core = ([copy(a, 0, sibling, me) for a in every] + [copy(a, 5, (*over_x, 1 - c), me) for a in every]
                     + [copy(a, 6, (*over_y, 1 - c), me) for a in every] + [copy(a, 7, (*across, 1 - c), me) for a in every])
        stages = [([], out), (near_in, relay + near_on), (relay_in, far_on)]
        return stages, out + relay + near_on + far_on, from_core

    return Exchange([], bufs, [], GATHER_SEMS * n, make)


TO_GATHER = (1, lambda x, y, c: [(x, y, 1 - c), (1 - x, y, c), (x, 1 - y, c)])
TO_SIBLING = (2, lambda x, y, c: [(x, y, 1 - c)])
TO_CHIPS = (3, lambda x, y, c: [(1 - x, y, c), (x, 1 - y, c), (1 - x, 1 - y, c)])
TO_ALL = (4, lambda x, y, c: [(x ^ (m >> 2), y ^ ((m >> 1) & 1), c ^ (m & 1)) for m in range(1, N_DEV)])


def by_sequencer(name, exchanges, who):
    collective_id, peers_of = who
    hbm = pltpu.MemorySpace.HBM
    refs = [([jax.new_ref(a, memory_space=hbm) for a in ex.srcs], [jax.new_ref(a, memory_space=hbm) for a in ex.bufs],
             [jax.empty_ref(s, memory_space=hbm) for s in ex.new_shapes]) for ex in exchanges]
    sems = []
    for ex in exchanges:
        sems += [pltpu.SemaphoreType.DMA((ex.n_sems,)), pltpu.SemaphoreType.DMA((ex.n_sems,))]

    @pl.kernel(mesh=plsc.ScalarSubcoreMesh(axis_name="sequencer", num_cores=1), name=name, scratch_types=tuple(sems),
               compiler_params=pltpu.CompilerParams(collective_id=collective_id))
    def launch(*sem_refs):
        peers = peers_of(*_coords())
        barrier = pltpu.get_barrier_semaphore()
        for peer in peers:
            pl.semaphore_signal(barrier, inc=1, device_id=peer, device_id_type=MESH)
        pl.semaphore_wait(barrier, len(peers))

        made = [ex.make(*refs[k], sem_refs[2 * k], sem_refs[2 * k + 1]) for k, ex in enumerate(exchanges)]
        for stage in range(max(len(stages) for stages, _, _ in made)):
            for stages, _, _ in made:
                if stage < len(stages):
                    arrivals, starts = stages[stage]
                    for cp in arrivals:
                        cp.wait_recv()
                    for cp in starts:
                        cp.start()
        for _, sends, arrivals in made:
            for cp in arrivals:
                cp.wait_recv()
            for cp in sends:
                cp.wait_send()

    launch()
    return [([ref[...] for ref in bufs], [ref[...] for ref in news]) for _, bufs, news in refs]


def place_shards(name, shards, dev):
    n = len(shards)

    def body(dev_ref, *refs):
        for a in range(n):
            refs[n + a][...] = refs[a][...].astype(BF16)

    spec = pltpu.PrefetchScalarGridSpec(
        num_scalar_prefetch=1, grid=(1,),
        in_specs=[pl.BlockSpec(s.shape, lambda i, dev_ref: (0, 0)) for s in shards],
        out_specs=[pl.BlockSpec(s.shape, lambda i, dev_ref: (dev_ref[0], 0)) for s in shards])
    return pl.pallas_call(
        body, name=name, grid_spec=spec,
        out_shape=[jax.ShapeDtypeStruct((N_DEV * s.shape[0], s.shape[1]), BF16) for s in shards],
        compiler_params=_params(),
    )(dev, *shards)


def _swap(copies_of):
    def make(src_refs, buf_refs, new_refs, send_sems, recv_sems):
        copies = copies_of(src_refs, new_refs, send_sems, recv_sems)
        return [([], copies)], copies, copies

    return make


def to_sibling(grads):
    def copies_of(src_refs, new_refs, send_sems, recv_sems):
        x, y, c = _coords()
        return [pltpu.make_async_remote_copy(
            src_ref=src_refs[a].at[2 * xy + 1 - c], dst_ref=new_refs[a].at[xy], send_sem=send_sems.at[4 * a + xy],
            recv_sem=recv_sems.at[4 * a + xy], device_id=(x, y, 1 - c), device_id_type=MESH)
            for a in range(len(src_refs)) for xy in range(4)]

    return Exchange(grads, [], [jax.ShapeDtypeStruct((4,) + g.shape[1:], g.dtype) for g in grads], 4 * len(grads),
                    _swap(copies_of))


def to_chips(parts):
    def copies_of(src_refs, new_refs, send_sems, recv_sems):
        x, y, c = _coords()
        chips = [(1 - x, y), (x, 1 - y), (1 - x, 1 - y)]
        return [pltpu.make_async_remote_copy(
            src_ref=src_refs[a].at[2 * px + py], dst_ref=new_refs[a].at[2 * x + y], send_sem=send_sems.at[3 * a + j],
            recv_sem=recv_sems.at[3 * a + j], device_id=(px, py, c), device_id_type=MESH)
            for a in range(len(src_refs)) for j, (px, py) in enumerate(chips)]

    return Exchange(parts, [], [jax.ShapeDtypeStruct(p.shape, p.dtype) for p in parts], 3 * len(parts), _swap(copies_of))


def to_owners(grad):
    def copies_of(src_refs, new_refs, send_sems, recv_sems):
        x, y, c = _coords()
        copies = []
        for m in range(1, N_DEV):
            px, py, pc = x ^ (m >> 2), y ^ ((m >> 1) & 1), c ^ (m & 1)
            copies.append(pltpu.make_async_remote_copy(
                src_ref=src_refs[0].at[4 * px + 2 * py + pc], dst_ref=new_refs[0].at[4 * x + 2 * y + c],
                send_sem=send_sems.at[m - 1], recv_sem=recv_sems.at[m - 1], device_id=(px, py, pc), device_id_type=MESH))
        return copies

    return Exchange([grad], [], [jax.ShapeDtypeStruct(grad.shape, grad.dtype)], N_DEV - 1, _swap(copies_of))


def to_everyone(vec):
    def copies_of(src_refs, new_refs, send_sems, recv_sems):
        x, y, c = _coords()
        copies = []
        for m in range(1, N_DEV):
            px, py, pc = x ^ (m >> 2), y ^ ((m >> 1) & 1), c ^ (m & 1)
            copies.append(pltpu.make_async_remote_copy(
                src_ref=src_refs[0], dst_ref=new_refs[0].at[4 * x + 2 * y + c],
                send_sem=send_sems.at[m - 1], recv_sem=recv_sems.at[m - 1], device_id=(px, py, pc), device_id_type=MESH))
        return copies

    return Exchange([vec], [], [jax.ShapeDtypeStruct((N_DEV,) + vec.shape, vec.dtype)], N_DEV - 1, _swap(copies_of))


def sum_cores(name, grads, others, core, after=()):
    k = len(grads)

    def body(core_ref, *refs):
        for j in range(k):
            out_ref = refs[2 * k + len(after) + j]
            out_ref[...] = (refs[j][:, 0].astype(F32) + refs[k + j][...].astype(F32)).astype(out_ref.dtype)

    mine = [pl.BlockSpec((2, 1) + o.shape[1:], lambda i, core_ref: (i, core_ref[0], 0, 0)) for o in others]
    theirs = [pl.BlockSpec((2,) + o.shape[1:], lambda i, core_ref: (i, 0, 0)) for o in others]
    return pl.pallas_call(
        body, name=name,
        grid_spec=pltpu.PrefetchScalarGridSpec(
            num_scalar_prefetch=1, grid=(2,), in_specs=mine + theirs + [ANY] * len(after), out_specs=theirs),
        out_shape=[jax.ShapeDtypeStruct(o.shape, o.dtype) for o in others],
        compiler_params=_params(),
    )(core, *[g.reshape((4, 2) + g.shape[1:]) for g in grads], *others, *after)


def sum_owned(name, grad, others, dev_ids, after=()):
    _, r, w = grad.shape

    def body(ids_ref, *refs):
        acc = refs[0][0]
        for k in range(1, N_DEV):
            acc = acc + refs[k][0]
        refs[-1][...] = acc

    def pick(k):
        return pl.BlockSpec((1, r, w), lambda i, ids_ref: (ids_ref[k], 0, 0))

    return pl.pallas_call(
        body, name=name,
        grid_spec=pltpu.PrefetchScalarGridSpec(
            num_scalar_prefetch=1, grid=(1,), in_specs=[pick(k) for k in range(N_DEV)] + [ANY] * len(after),
            out_specs=pl.BlockSpec((r, w), lambda i, ids_ref: (ids_ref[0], 0))),
        out_shape=jax.ShapeDtypeStruct((N_DEV * r, w), F32),
        compiler_params=_params(),
    )(dev_ids, grad, *([others] * (N_DEV - 1)), *after)


def _adamw_update(w, g, m, v):
    nm = ADAM_B1 * m + np.float32(1.0 - ADAM_B1) * g
    nv = ADAM_B2 * v + np.float32(1.0 - ADAM_B2) * (g * g)
    m_hat = nm / np.float32(1.0 - ADAM_B1 ** ADAM_STEP)
    v_hat = nv / np.float32(1.0 - ADAM_B2 ** ADAM_STEP)
    return -ADAM_LR * (m_hat / (jnp.sqrt(v_hat) + ADAM_EPS) + ADAM_WD * w), nm, nv


def adamw_of_sums(name, parts, others, chip_ids, ws, ms, vs, after):
    n = len(parts)
    halves = 2

    def body(ids_ref, *refs):
        outs = refs[7 * n + len(after):]
        for j in range(n):
            p_ref, a_ref, b_ref, c_ref, w_ref, m_ref, v_ref = refs[7 * j:7 * j + 7]
            g = ((p_ref[0].astype(F32) + a_ref[0].astype(F32)) + b_ref[0].astype(F32)) + c_ref[0].astype(F32)
            outs[4 * j][...] = g
            outs[4 * j + 1][...], outs[4 * j + 2][...], outs[4 * j + 3][...] = _adamw_update(w_ref[...], g, m_ref[...], v_ref[...])

    in_specs, out_specs, out_shape, operands = [], [], [], []
    for part, other, w, m, v in zip(parts, others, ws, ms, vs):
        _, r, wd = part.shape
        rows = r // halves
        whole = pl.BlockSpec((rows, wd), lambda i, ids_ref: (i, 0))
        in_specs += [pl.BlockSpec((1, rows, wd), lambda i, ids_ref, k=k: (ids_ref[k], i, 0)) for k in range(4)] + [whole] * 3
        out_specs += [whole] * 4
        out_shape += [jax.ShapeDtypeStruct((r, wd), F32)] * 4
        operands += [part, other, other, other, w, m, v]
    outs = pl.pallas_call(
        body, name=name,
        grid_spec=pltpu.PrefetchScalarGridSpec(
            num_scalar_prefetch=1, grid=(halves,), in_specs=in_specs + [ANY] * len(after), out_specs=out_specs),
        out_shape=out_shape,
        compiler_params=_params(),
    )(chip_ids, *operands, *after)
    return [tuple(outs[4 * j:4 * j + 4]) for j in range(n)]


def pack_small(parts):
    names = [name for name, _ in SMALL if name in parts]
    operands = [parts[name] for name in names]
    first_row, at = {}, 0
    for name, size in SMALL:
        first_row[name] = at // 128
        at += size
    sizes = dict(SMALL)

    def body(*refs):
        out_ref = refs[-1]
        out_ref[...] = jnp.zeros_like(out_ref)
        for name, ref in zip(names, refs):
            row = first_row[name]
            if name == "loss_sum":
                lane0 = lax.broadcasted_iota(jnp.int32, (1, 128), 1) == 0
                out_ref[row:row + 1, :] = jnp.where(lane0, ref[...], 0.0)
            else:
                rows = sizes[name] // 128
                out_ref[row:row + rows, :] = ref[...].reshape(rows, 128)

    vmem = pl.BlockSpec(memory_space=pltpu.VMEM)
    return pl.pallas_call(
        body, name="pack_small", in_specs=[vmem] * len(names), out_specs=vmem,
        out_shape=jax.ShapeDtypeStruct((SMALL_ROWS, 128), F32), compiler_params=_params(()),
    )(*operands)


LATE = "pre_mix_norm"


def adamw_small(packed_g, late, ws, ms, vs, after=()):
    names = [LATE] if late else [name for name, _ in SMALL if name not in ("loss_sum", LATE)]
    k = len(names)
    shapes = [ws[name].shape[1:] if ws[name].ndim > 2 else ws[name].shape for name in names]
    first_row, at = {}, 0
    for name, size in SMALL:
        first_row[name] = at // 128
        at += size
    sizes = dict(SMALL)
    n_in = 3 if late else 1

    def body(*refs):
        w_refs, m_refs, v_refs = (refs[n_in + j * k:n_in + (j + 1) * k] for j in range(3))
        outs = refs[n_in + 3 * k + len(after):]
        for i, name in enumerate(names):
            if late:
                late_ref, own_ref, dev_ref = refs[:n_in]
                parts = [jnp.where(dev_ref[0] == j, own_ref[...], late_ref[j]) for j in range(N_DEV)]
                g = parts[0]
                for j in range(1, N_DEV):
                    g = g + parts[j]
            else:
                g = refs[0][first_row[name]:first_row[name] + sizes[name] // 128, :].reshape(shapes[i])
            outs[i][...] = g
            outs[k + i][...], outs[2 * k + i][...], outs[3 * k + i][...] = _adamw_update(
                w_refs[i][...], g, m_refs[i][...], v_refs[i][...])
        if not late:
            outs[4 * k][...] = refs[0][first_row["loss_sum"]:first_row["loss_sum"] + 1, 0:1]

    vmem = pl.BlockSpec(memory_space=pltpu.VMEM)
    operands = [t[name].reshape(shape) for t in (ws, ms, vs) for name, shape in zip(names, shapes)]
    outs = pl.pallas_call(
        body, name="adamw_late" if late else "adamw_small",
        in_specs=([vmem, vmem, pl.BlockSpec(memory_space=pltpu.SMEM)] if late else [vmem]) + [vmem] * (3 * k) + [ANY] * len(after),
        out_specs=[vmem] * (4 * k + (0 if late else 1)),
        out_shape=[jax.ShapeDtypeStruct(shape, F32) for _ in range(4) for shape in shapes]
        + ([] if late else [jax.ShapeDtypeStruct((1, 1), F32)]),
        compiler_params=_params(()),
    )(*(late or (packed_g,)), *operands, *after)
    tables = [{name: outs[j * k + i].reshape(ws[name].shape) for i, name in enumerate(names)} for j in range(4)]
    return (*tables, None if late else outs[4 * k])


def kernel(x, positions, pre_mix_norm, w_in, sgu_ln_gain, sgu_ln_bias, sgu_w_spatial, sgu_b_spatial, attn_out_norm, sgu_out_norm, w_out, post_mix_norm, pre_ffn_norm, w_gate, w_up, w_down, post_ffn_norm, loss_target, m_pre_mix_norm, m_w_in, m_sgu_ln_gain, m_sgu_ln_bias, m_sgu_w_spatial, m_sgu_b_spatial, m_attn_out_norm, m_sgu_out_norm, m_w_out, m_post_mix_norm, m_pre_ffn_norm, m_w_gate, m_w_up, m_w_down, m_post_ffn_norm, v_pre_mix_norm, v_w_in, v_sgu_ln_gain, v_sgu_ln_bias, v_sgu_w_spatial, v_sgu_b_spatial, v_attn_out_norm, v_sgu_out_norm, v_w_out, v_post_mix_norm, v_pre_ffn_norm, v_w_gate, v_w_up, v_w_down, v_post_ffn_norm):
    small_w = dict(pre_mix_norm=pre_mix_norm, sgu_ln_gain=sgu_ln_gain, sgu_ln_bias=sgu_ln_bias, sgu_w_spatial=sgu_w_spatial,
                   sgu_b_spatial=sgu_b_spatial, attn_out_norm=attn_out_norm, sgu_out_norm=sgu_out_norm,
                   post_mix_norm=post_mix_norm, pre_ffn_norm=pre_ffn_norm, post_ffn_norm=post_ffn_norm)
    small_m = dict(pre_mix_norm=m_pre_mix_norm, sgu_ln_gain=m_sgu_ln_gain, sgu_ln_bias=m_sgu_ln_bias, sgu_w_spatial=m_sgu_w_spatial,
                   sgu_b_spatial=m_sgu_b_spatial, attn_out_norm=m_attn_out_norm, sgu_out_norm=m_sgu_out_norm,
                   post_mix_norm=m_post_mix_norm, pre_ffn_norm=m_pre_ffn_norm, post_ffn_norm=m_post_ffn_norm)
    small_v = dict(pre_mix_norm=v_pre_mix_norm, sgu_ln_gain=v_sgu_ln_gain, sgu_ln_bias=v_sgu_ln_bias, sgu_w_spatial=v_sgu_w_spatial,
                   sgu_b_spatial=v_sgu_b_spatial, attn_out_norm=v_attn_out_norm, sgu_out_norm=v_sgu_out_norm,
                   post_mix_norm=v_post_mix_norm, pre_ffn_norm=v_pre_ffn_norm, post_ffn_norm=v_post_ffn_norm)

    x2d = x[0]
    target = loss_target[0]
    pos_col = positions.reshape(SEQ, 1)
    rot = _rot_consts()
    w_sp = sgu_w_spatial[0]
    bfull = jnp.repeat(sgu_b_spatial[0].T, HEAD_DIM, axis=1)

    x_i, y_i, c_i = (lax.axis_index(a).astype(jnp.int32) for a in MESH_AXES)
    dev = 4 * x_i + 2 * y_i + c_i
    core = c_i.reshape(1)
    chip = 2 * x_i + y_i
    chip_ids = jnp.stack([chip, chip ^ 1, chip ^ 2, chip ^ 3])
    dev_ids = jnp.stack([dev ^ m for m in range(N_DEV)])

    def gathered(name, bufs):
        return by_sequencer(name, [gather(bufs)], TO_GATHER)[0][0]

    def from_sibling(name, grads):
        return by_sequencer(name, [to_sibling(grads)], TO_SIBLING)[0][1]

    def from_chips(name, parts):
        return by_sequencer(name, [to_chips(parts)], TO_CHIPS)[0][1]

    (w_in_t,) = place_shards("place_w_in", [w_in[0].T], dev.reshape(1))
    (w_in_t,) = gathered("gather_w_in", [w_in_t])
    w_gate_t, w_up_t, w_out_f, w_down_f = place_shards(
        "place_weights", [w_gate[0].T, w_up[0].T, w_out[0], w_down[0]], dev.reshape(1))
    (w_out_f,) = gathered("gather_w_out", [w_out_f])
    w_gate_t, w_up_t = gathered("gather_w_gate_up", [w_gate_t, w_up_t])
    (w_down_f,) = gathered("gather_w_down", [w_down_f])

    pos_r = _to_residue_order(pos_col)
    h1 = pre_norm(x2d, pre_mix_norm, after=[pos_r, dev_ids, chip_ids])
    u, vs, q, k, v = in_proj(h1, pos_col, w_in_t, rot)
    attn_r, lse, attn = attn_fwd(q, k, v)
    (sgu,) = sgu_fwd(u, vs, sgu_ln_gain, sgu_ln_bias, w_sp, bfull)
    mix, y, x2, h2 = out_proj(attn, sgu, x2d, attn_out_norm, sgu_out_norm, w_out_f, post_mix_norm, pre_ffn_norm)
    gate, up, act = ffn_up(h2, w_gate_t, w_up_t)
    df, dx3, d_post_ffn, sq_err = ffn_down_loss(act, w_down_f, x2, post_ffn_norm, target)

    g_w_down = weight_grad("grad_w_down", act, df)
    (s_down,) = from_sibling("w_down_to_sibling", [g_w_down])
    dgate, dup, dx2, dy, d_pre_ffn, d_post_mix = ffn_bwd(
        df, w_down_f, gate, up, w_gate_t, w_up_t, x2, pre_ffn_norm, dx3, y, post_mix_norm, after=[g_w_down])
    (p_down,) = sum_cores("sum_cores_down", [g_w_down], [s_down], core, after=[dy])
    (c_down,) = from_chips("w_down_to_chips", [p_down])
    g_w_gate, g_w_up = weight_grads("grad_w_gate_up", [dgate, dup], h2, after=[p_down])
    s_gate, s_up = from_sibling("w_gate_up_to_sibling", [g_w_gate, g_w_up])
    g_w_out = weight_grad("grad_w_out", mix, dy, after=[g_w_up])
    (s_out,) = from_sibling("w_out_to_sibling", [g_w_out])
    dsgu, d_attn_out, d_sgu_out, dattn_r = mix_bwd(dy, w_out_f, attn, sgu, attn_out_norm, sgu_out_norm, after=[g_w_out, c_down])
    p_gate, p_up = sum_cores("sum_cores_gate_up", [g_w_gate, g_w_up], [s_gate, s_up], core, after=[dsgu])
    c_gate, c_up = from_chips("w_gate_up_to_chips", [p_gate, p_up])
    du, dvs, d_ln_gain, d_ln_bias, d_w_sp, d_bfull = sgu_bwd(
        u, vs, dsgu, sgu_ln_gain, sgu_ln_bias, w_sp, bfull, after=[p_gate, p_up])

    d_b_sp = d_bfull.reshape(CHUNK, N_GROUPS, HEAD_DIM).sum(axis=-1).T
    small_g = pack_small(dict(sgu_ln_gain=d_ln_gain, sgu_ln_bias=d_ln_bias, sgu_w_spatial=d_w_sp, sgu_b_spatial=d_b_sp,
                              attn_out_norm=d_attn_out, sgu_out_norm=d_sgu_out, post_mix_norm=d_post_mix,
                              pre_ffn_norm=d_pre_ffn, post_ffn_norm=d_post_ffn, loss_sum=sq_err))
    small_g = small_g.reshape(N_DEV, SMALL_ROWS // N_DEV, 128)
    ((_, (o_small,)),) = by_sequencer("small_to_owners", [to_owners(small_g)], TO_ALL)

    dq, dk, dv = attn_bwd(q, k, v, attn_r, lse, dattn_r, pos_r, rot)
    summed_small = sum_owned("sum_small", small_g, o_small, dev_ids, after=[dq, c_gate, c_up])
    (all_small,) = gathered("gather_small_grads", [summed_small])
    (p_out,) = sum_cores("sum_cores_out", [g_w_out], [s_out], core, after=[dq])
    (c_out,) = from_chips("w_out_to_chips", [p_out])
    dproj = [dq, dk, dv, du, dvs]
    g_w_in = weight_grad_of_parts("grad_w_in", dproj, h1, after=[c_gate, c_up])
    (s_in,) = from_sibling("w_in_to_sibling", [g_w_in])

    def same(t):
        return t

    def turned(t):
        return t.T

    big = {}

    def adamw(call, weights, after):
        results = adamw_of_sums(call, [p for _, _, p, _, _, _, _ in weights], [c for _, _, _, c, _, _, _ in weights], chip_ids,
                                [turn(w[0]) for _, w, _, _, _, _, turn in weights], [turn(m[0]) for _, _, _, _, m, _, turn in weights],
                                [turn(vv[0]) for _, _, _, _, _, vv, turn in weights], after)
        for (name, _, _, _, _, _, turn), outs in zip(weights, results):
            big[name] = tuple(turn(t)[None] for t in outs)
        return results[-1][0]

    done_ffn = adamw("adamw_ffn", (("w_down", w_down, p_down, c_down, m_w_down, v_w_down, same),
                                   ("w_gate", w_gate, p_gate, c_gate, m_w_gate, v_w_gate, turned),
                                   ("w_up", w_up, p_up, c_up, m_w_up, v_w_up, turned)), [g_w_in, all_small])
    (p_in,) = sum_cores("sum_cores_in", [g_w_in], [s_in], core, after=[done_ffn, c_out])
    (c_in,) = from_chips("w_in_to_chips", [p_in])
    grad_x, d_pre_mix = in_bwd(dproj, w_in_t, x2d, pre_mix_norm, dx2, after=[p_in])
    ((_, (late_parts,)),) = by_sequencer("pre_mix_to_everyone", [to_everyone(d_pre_mix)], TO_ALL)
    done_out = adamw("adamw_w_out", (("w_out", w_out, p_out, c_out, m_w_out, v_w_out, same),), [d_pre_mix])
    *early, loss_sum = adamw_small(all_small, None, small_w, small_m, small_v, after=[done_out])
    done_in = adamw("adamw_w_in", (("w_in", w_in, p_in, c_in, m_w_in, v_w_in, turned),), [loss_sum])
    *late, _ = adamw_small(None, (late_parts, d_pre_mix, dev.reshape(1)), small_w, small_m, small_v, after=[done_in])
    sg, sd, snm, snv = ({**a, **b} for a, b in zip(early, late))
    loss = loss_sum[0, 0] * np.float32(0.5 / D_MODEL)

    names = ["pre_mix_norm", "w_in", "sgu_ln_gain", "sgu_ln_bias", "sgu_w_spatial", "sgu_b_spatial", "attn_out_norm",
             "sgu_out_norm", "w_out", "post_mix_norm", "pre_ffn_norm", "w_gate", "w_up", "w_down", "post_ffn_norm"]
    outs = [loss, grad_x[None]]
    for i, table in enumerate((sg, sd, snm, snv)):
        for name in names:
            outs.append(big[name][i] if name in big else table[name])
    return tuple(outs)
```

```python
import numpy as np
import jax
import jax.numpy as jnp
from jax import lax
from jax.experimental import pallas as pl
from jax.experimental.pallas import tpu as pltpu
from jax.experimental.pallas import tpu_sc as plsc

F32 = jnp.float32
BF16 = jnp.bfloat16

SEQ = 2048
D_MODEL = 1024
ATTN_W = 512
SGU_W = 512
HEAD_DIM = 64
N_GROUPS = 8
CHUNK = 128
D_FF = 2816
IN_W = 3 * ATTN_W + 2 * SGU_W
DILATIONS = (1, 4, 16)
ROPE_THETA = 500000.0
ROT_DIM = 16
ROT_HALF = 8
RMS_EPS = 1e-6
LN_EPS = 1e-5
Q_SCALE = 0.125
NEG = -1e30

N_DEV = 8
MESH_AXES = ("x", "y", "c")
MESH = pl.DeviceIdType.MESH

ADAM_LR = 0.001
ADAM_B1 = 0.9
ADAM_B2 = 0.999
ADAM_EPS = 1e-08
ADAM_WD = 0.01
ADAM_STEP = 10

VMEM_LIMIT = 60 * 1024 * 1024
ANY = pl.BlockSpec(memory_space=pl.ANY)

SMALL = (("pre_mix_norm", 1024), ("sgu_ln_gain", 512), ("sgu_ln_bias", 512), ("sgu_w_spatial", 8 * 128 * 128),
         ("sgu_b_spatial", 1024), ("attn_out_norm", 512), ("sgu_out_norm", 512), ("post_mix_norm", 1024),
         ("pre_ffn_norm", 1024), ("post_ffn_norm", 1024), ("loss_sum", 1))
SMALL_ROWS = 1152


def _params(sem=("arbitrary",)):
    return pltpu.CompilerParams(dimension_semantics=sem, vmem_limit_bytes=VMEM_LIMIT)


def _dot(a, b):
    return jnp.dot(a, b, preferred_element_type=F32)


def _dot_nt(a, b):
    return lax.dot_general(a, b, (((1,), (1,)), ((), ())), preferred_element_type=F32)


def _dot_tn(a, b):
    return lax.dot_general(a, b, (((0,), (0,)), ((), ())), preferred_element_type=F32)


def _rms(z):
    return lax.rsqrt(jnp.mean(z * z, axis=-1, keepdims=True) + RMS_EPS)


def _rms_bwd(z, gain, d):
    r = _rms(z)
    n = z * r
    dn = d * gain
    dz = r * (dn - n * jnp.mean(dn * n, axis=-1, keepdims=True))
    return dz, jnp.sum(d * n, axis=0, keepdims=True)


def _gelu(z):
    return 0.5 * z * (1.0 + lax.erf(z * np.float32(1.0 / np.sqrt(2.0))))


def _gelu_grad(z):
    cdf = 0.5 * (1.0 + lax.erf(z * np.float32(1.0 / np.sqrt(2.0))))
    return cdf + z * jnp.exp(-0.5 * z * z) * np.float32(1.0 / np.sqrt(2.0 * np.pi))


def _rot_tables(pos_col, invf, ma, mb):
    ang = pos_col.astype(F32) * invf
    s = jnp.sin(ang)
    return jnp.cos(ang), s * ma, s * mb


def _rot(t, c, sa, sb):
    return t * c + pltpu.roll(t, 120, 1) * sa + pltpu.roll(t, 8, 1) * sb


def _rot_t(d, c, sa, sb):
    return d * c + pltpu.roll(d * sa, 8, 1) + pltpu.roll(d * sb, 120, 1)


def _rot_consts():
    lane = np.arange(128) % HEAD_DIM
    inv_freq = (np.float32(ROPE_THETA) ** (-np.arange(0, ROT_DIM, 2, dtype=np.float32) / np.float32(ROT_DIM))).astype(np.float32)
    invf = np.where(lane < ROT_DIM, inv_freq[lane % ROT_HALF], 0.0).astype(np.float32)
    ma = np.where(lane < ROT_HALF, -1.0, 0.0).astype(np.float32)
    mb = np.where((lane >= ROT_HALF) & (lane < ROT_DIM), 1.0, 0.0).astype(np.float32)
    return jnp.asarray(invf[None]), jnp.asarray(ma[None]), jnp.asarray(mb[None])


def _row_spec(tm, w):
    return pl.BlockSpec((tm, w), lambda i: (i, 0))


def _full_spec(shape):
    return pl.BlockSpec(shape, lambda i: (0,) * len(shape))


def _weight_spec(shape):
    return pl.BlockSpec(shape, lambda i: (0,) * len(shape), pipeline_mode=pl.Buffered(1))


FF_CHUNKS = (256, 512, 1024, 1024)
FF_SPANS = [(int(o), n) for o, n in zip(np.cumsum((0,) + FF_CHUNKS[:-1]), FF_CHUNKS)]
FF_WHOLE = [(0, D_FF)]


def _ffn_weight_scratch(n_weights, spans):
    return [pltpu.VMEM((D_FF, D_MODEL), BF16)] * n_weights + [pltpu.SemaphoreType.DMA((n_weights, len(spans)))]


def _with_ffn_weights(w_hbm, w_vmem, sems, spans, run):
    copies = [[pltpu.make_async_copy(h.at[pl.ds(o, n)], v.at[pl.ds(o, n)], sems.at[j, c]) for c, (o, n) in enumerate(spans)]
              for j, (h, v) in enumerate(zip(w_hbm, w_vmem))]
    first = pl.program_id(0) == 0

    @pl.when(first)
    def _():
        for of_weight in copies:
            for cp in of_weight:
                cp.start()
        run(lambda j, c: copies[j][c].wait())

    @pl.when(jnp.logical_not(first))
    def _():
        run(None)


RES = 16


def _residue_scratch(n_arrays, tm, width):
    return [pltpu.VMEM((2, n_arrays, tm // RES, RES, width), F32), pltpu.SemaphoreType.DMA((2, n_arrays, RES))]


def _to_residue_rows(tiles, outs, scratch, sems, tm, n_steps):
    i = pl.program_id(0)
    slot = i % 2
    per = tm // RES

    def copies(step, s):
        return [pltpu.make_async_copy(scratch.at[s, a, :, b, :],
                                      outs[a].at[pl.ds(pl.multiple_of(b * (SEQ // RES) + per * step, per), per), :],
                                      sems.at[s, a, b]) for a in range(len(outs)) for b in range(RES)]

    @pl.when(i >= 2)
    def _():
        for cp in copies(i - 2, slot):
            cp.wait()

    for a, tile in enumerate(tiles):
        scratch[slot, a] = tile.reshape(per, RES, tile.shape[-1])
    for cp in copies(i, slot):
        cp.start()

    @pl.when(i == n_steps - 1)
    def _():
        for cp in copies(i - 1, 1 - slot) + copies(i, slot):
            cp.wait()


def pre_norm(x, g1, after=()):
    tm = 512

    def body(x_ref, g_ref, h_ref):
        xf = x_ref[...]
        h_ref[...] = (xf * _rms(xf) * g_ref[...]).astype(BF16)

    return _call("pre_norm", body, SEQ // tm, [_row_spec(tm, D_MODEL), _full_spec((1, D_MODEL))], [_row_spec(tm, D_MODEL)],
                 [jax.ShapeDtypeStruct((SEQ, D_MODEL), BF16)], (x, g1), after=after)[0]


def in_proj(h1, pos_col, w_in_t, rot):
    tm = 512
    n_steps = SEQ // tm

    def body(h_ref, pos_ref, w_ref, invf_ref, ma_ref, mb_ref, u_ref, vs_ref, q_ref, k_ref, v_ref, scratch, sems):
        proj = _dot_nt(h_ref[...], w_ref[...])
        c, sa, sb = _rot_tables(pos_ref[...], invf_ref[...], ma_ref[...], mb_ref[...])
        slabs = range(ATTN_W // 128)
        q = jnp.concatenate([_rot(proj[:, j * 128:(j + 1) * 128], c, sa, sb) * Q_SCALE for j in slabs], axis=1)
        k = jnp.concatenate([_rot(proj[:, ATTN_W + j * 128:ATTN_W + (j + 1) * 128], c, sa, sb) for j in slabs], axis=1)
        u_ref[...] = proj[:, 3 * ATTN_W:3 * ATTN_W + SGU_W]
        vs_ref[...] = proj[:, 3 * ATTN_W + SGU_W:]
        _to_residue_rows([q, k, proj[:, 2 * ATTN_W:3 * ATTN_W]], [q_ref, k_ref, v_ref], scratch, sems, tm, n_steps)

    act = jax.ShapeDtypeStruct((SEQ, 512), F32)
    return _call(
        "in_proj", body, n_steps,
        [_row_spec(tm, D_MODEL), _row_spec(tm, 1), _weight_spec((IN_W, D_MODEL)),
         _full_spec((1, 128)), _full_spec((1, 128)), _full_spec((1, 128))],
        [_row_spec(tm, 512)] * 2 + [ANY] * 3, [act] * 5,
        (h1, pos_col, w_in_t, *rot), scratch_shapes=_residue_scratch(3, tm, ATTN_W))


def _to_residue_order(t):
    return t.reshape(SEQ // RES, RES, -1).transpose(1, 0, 2).reshape(t.shape)


def _block_rows(d, r, n):
    if d == 16:
        slices = [(128 * r, 128)]
    elif d == 4:
        slices = [(128 * (4 * b + r) + 32 * n, 32) for b in range(4)]
    else:
        slices = [(128 * b + 8 * n, 8) for b in range(RES)]
    return [(s if isinstance(s, int) else pl.multiple_of(s, z), z) for s, z in slices]


def _block_step(d, i):
    if d == 16:
        return i
    if d == 4:
        return 4 * (i & 31) + (i >> 5)
    return 16 * (i & 7) + (i >> 3)


def _attn_masks(d):
    row2 = _block_step(d, lax.broadcasted_iota(jnp.int32, (128, 256), 0))
    col2 = lax.broadcasted_iota(jnp.int32, (128, 256), 1)
    key2 = _block_step(d, col2 & 127)
    mask2 = jnp.logical_or(jnp.logical_and(col2 < 128, key2 >= row2), jnp.logical_and(col2 >= 128, key2 <= row2))
    row1 = _block_step(d, lax.broadcasted_iota(jnp.int32, (128, 128), 0))
    col1 = lax.broadcasted_iota(jnp.int32, (128, 128), 1)
    return col1 < HEAD_DIM, _block_step(d, col1) <= row1, mask2


def _load_rows(ref, slices):
    parts = [ref[pl.ds(s, z), :] for s, z in slices]
    return parts[0] if len(parts) == 1 else jnp.concatenate(parts, axis=0)


def _for_each_group(fn):
    for p, d in enumerate(DILATIONS):
        masks = _attn_masks(d)
        if d == 16:
            def group(i, carry, p=p, masks=masks):
                fn(p, masks, [(_block_rows(16, 8 * i + g, 0), None) for g in range(8)])
                return carry

            lax.fori_loop(0, 2, group, 0)
        elif d == 4:
            fn(p, masks, [(_block_rows(4, r, 0), None) for r in range(4)])

            def group(i, carry, p=p, masks=masks):
                blocks = [6 * i + g for g in range(6)]
                fn(p, masks, [(_block_rows(4, j % 4, 1 + j // 4), _block_rows(4, j % 4, j // 4)) for j in blocks])
                return carry

            lax.fori_loop(0, 2, group, 0)
        else:
            fn(p, masks, [(_block_rows(1, 0, 0), None)])

            def group(i, carry, p=p, masks=masks):
                fn(p, masks, [(_block_rows(1, 0, 5 * i + g + 1), _block_rows(1, 0, 5 * i + g)) for g in range(5)])
                return carry

            lax.fori_loop(0, 3, group, 0)


def attn_fwd(q, k, v):
    def body(q_ref, k_ref, v_ref, o_ref, lse_ref, nat_ref, op_ref, lp_ref, sems):
        def group(p, masks, blocks):
            head0, mask1, mask2 = masks
            heads = (head0, jnp.logical_not(head0))
            keys = [rows if prev is None else prev + rows for rows, prev in blocks]
            mask = [mask1 if prev is None else mask2 for _, prev in blocks]
            qb = [_load_rows(q_ref, rows) for rows, _ in blocks]
            kk = [_load_rows(k_ref, ks).astype(BF16) for ks in keys]
            vv = [_load_rows(v_ref, ks).astype(BF16) for ks in keys]
            chains = [(g, hm) for g in range(len(blocks)) for hm in heads]
            s = [jnp.where(mask[g], _dot_nt(jnp.where(hm, qb[g], 0.0).astype(BF16), kk[g]), NEG) for g, hm in chains]
            m = [jnp.max(t, axis=-1, keepdims=True) for t in s]
            e = [jnp.exp(t - mt) for t, mt in zip(s, m)]
            l = [jnp.sum(t, axis=-1, keepdims=True) for t in e]
            pv = [_dot(t.astype(BF16), vv[g]) for t, (g, _) in zip(e, chains)]
            for g, (rows, _) in enumerate(blocks):
                o_blk = jnp.where(head0, pv[2 * g] / l[2 * g], pv[2 * g + 1] / l[2 * g + 1])
                l_blk = jnp.where(head0, jnp.broadcast_to(m[2 * g] + jnp.log(l[2 * g]), (128, 128)),
                                  jnp.broadcast_to(m[2 * g + 1] + jnp.log(l[2 * g + 1]), (128, 128)))
                at = 0
                for start, size in rows:
                    op_ref[p, pl.ds(start, size), :] = o_blk[at:at + size]
                    lp_ref[p, pl.ds(start, size), :] = l_blk[at:at + size]
                    at += size

        _for_each_group(group)

        def combine(i, carry):
            rows = pl.ds(pl.multiple_of(i * 256, 256), 256)
            ls = [lp_ref[p, rows, :] for p in range(3)]
            m = jnp.maximum(jnp.maximum(ls[0], ls[1]), ls[2])
            lse = m + jnp.log(jnp.exp(ls[0] - m) + jnp.exp(ls[1] - m) + jnp.exp(ls[2] - m))
            o = jnp.zeros((256, 128), F32)
            for p in range(3):
                o = o + jnp.exp(ls[p] - lse) * op_ref[p, rows, :]
            o_ref[rows, :] = o
            lse_ref[rows, :] = lse
            return carry

        lax.fori_loop(0, SEQ // 256, combine, 0)

        lanes = pl.ds(pl.multiple_of(pl.program_id(0) * 128, 128), 128)
        back = [pltpu.make_async_copy(o_ref.at[pl.ds(b * (SEQ // RES), SEQ // RES), :], nat_ref.at[:, b, lanes], sems.at[b])
                for b in range(RES)]
        for cp in back:
            cp.start()
        for cp in back:
            cp.wait()

    slab = pl.BlockSpec((SEQ, 128), lambda i: (0, i))
    out = jax.ShapeDtypeStruct((SEQ, ATTN_W), F32)
    attn_r, lse, attn = _call(
        "attn_fwd", body, ATTN_W // 128, [slab] * 3, [slab] * 2 + [ANY],
        [out, out, jax.ShapeDtypeStruct((SEQ // RES, RES, ATTN_W), F32)], (q, k, v),
        scratch_shapes=[pltpu.VMEM((3, SEQ, 128), F32), pltpu.VMEM((3, SEQ, 128), F32), pltpu.SemaphoreType.DMA((RES,))])
    return attn_r, lse, attn.reshape(SEQ, ATTN_W)


def _causal_weights(w_ref):
    row = lax.broadcasted_iota(jnp.int32, (CHUNK, CHUNK), 0)
    col = lax.broadcasted_iota(jnp.int32, (CHUNK, CHUNK), 1)
    return [jnp.where(col <= row, w_ref[g], 0.0).astype(BF16) for g in range(N_GROUPS)], col <= row


def _sgu_chunk_fwd(u, vs, lg, lb, wc, bfull, head0):
    ug = _gelu(u)
    vg = _gelu(vs)
    xc = vg - jnp.mean(vg, axis=-1, keepdims=True)
    rstd = lax.rsqrt(jnp.mean(xc * xc, axis=-1, keepdims=True) + LN_EPS)
    xhat = xc * rstd
    vn = xhat * lg + lb
    mixed = []
    for gp in range(SGU_W // 128):
        vp = vn[:, gp * 128:(gp + 1) * 128].astype(BF16)
        mixed.append(jnp.where(head0, _dot(wc[2 * gp], vp), _dot(wc[2 * gp + 1], vp)))
    ms = jnp.concatenate(mixed, axis=1) + bfull
    return ug, xhat, rstd, vn, ms


def sgu_fwd(u, vs, lg, lb, w_sp, bfull):
    cpb = 4

    def body(u_ref, vs_ref, lg_ref, lb_ref, w_ref, b_ref, o_ref):
        wc, _ = _causal_weights(w_ref)
        head0 = lax.broadcasted_iota(jnp.int32, (CHUNK, 128), 1) < HEAD_DIM
        for ci in range(cpb):
            rows = pl.ds(ci * CHUNK, CHUNK)
            ug, _, _, _, ms = _sgu_chunk_fwd(u_ref[rows, :], vs_ref[rows, :], lg_ref[...], lb_ref[...], wc, b_ref[...], head0)
            o_ref[rows, :] = ug * ms

    tm = cpb * CHUNK
    return _call(
        "sgu_fwd", body, SEQ // tm,
        [_row_spec(tm, SGU_W), _row_spec(tm, SGU_W), _full_spec((1, SGU_W)), _full_spec((1, SGU_W)),
         _full_spec((N_GROUPS, CHUNK, CHUNK)), _full_spec((CHUNK, SGU_W))],
        [_row_spec(tm, SGU_W)], [jax.ShapeDtypeStruct((SEQ, SGU_W), F32)],
        (u, vs, lg, lb, w_sp, bfull))


def out_proj(attn, sgu, x, ga, gs, w_out, gpm, gpf):
    tm = 512

    def body(a_ref, s_ref, x_ref, ga_ref, gs_ref, w_ref, gpm_ref, gpf_ref, mix_ref, y_ref, x2_ref, h2_ref):
        a = a_ref[...]
        s = s_ref[...]
        an = (a * _rms(a) * ga_ref[...]).astype(BF16)
        sn = (s * _rms(s) * gs_ref[...]).astype(BF16)
        mix_ref[:, :ATTN_W] = an
        mix_ref[:, ATTN_W:] = sn
        y = _dot(an, w_ref[:ATTN_W, :]) + _dot(sn, w_ref[ATTN_W:, :])
        y_ref[...] = y
        x2 = x_ref[...] + y * _rms(y) * gpm_ref[...]
        x2_ref[...] = x2
        h2_ref[...] = (x2 * _rms(x2) * gpf_ref[...]).astype(BF16)

    wide = jax.ShapeDtypeStruct((SEQ, D_MODEL), F32)
    wide16 = jax.ShapeDtypeStruct((SEQ, D_MODEL), BF16)
    return _call(
        "out_proj", body, SEQ // tm,
        [_row_spec(tm, ATTN_W), _row_spec(tm, SGU_W), _row_spec(tm, D_MODEL), _full_spec((1, ATTN_W)),
         _full_spec((1, SGU_W)), _weight_spec((D_MODEL, D_MODEL)), _full_spec((1, D_MODEL)), _full_spec((1, D_MODEL))],
        [_row_spec(tm, D_MODEL)] * 4, [wide16, wide, wide, wide16],
        (attn, sgu, x, ga, gs, w_out, gpm, gpf))


def ffn_up(h2, w_gate_t, w_up_t):
    tm = 256

    def body(h_ref, wg_hbm, wu_hbm, g_ref, u_ref, a_ref, wg_ref, wu_ref, sems):
        def run(wait):
            h = h_ref[...]
            if wait:
                wait(0, 0)
            g = _dot_nt(h, wg_ref[...])
            g_ref[...] = g.astype(BF16)
            if wait:
                wait(1, 0)
            u = _dot_nt(h, wu_ref[...])
            u_ref[...] = u.astype(BF16)
            a_ref[...] = (g * jax.nn.sigmoid(g) * u).astype(BF16)

        _with_ffn_weights([wg_hbm, wu_hbm], [wg_ref, wu_ref], sems, FF_WHOLE, run)

    ff = jax.ShapeDtypeStruct((SEQ, D_FF), BF16)
    return _call(
        "ffn_up", body, SEQ // tm, [_row_spec(tm, D_MODEL), ANY, ANY],
        [_row_spec(tm, D_FF)] * 3, [ff, ff, jax.ShapeDtypeStruct((SEQ, D_FF), BF16)],
        (h2, w_gate_t, w_up_t), scratch_shapes=_ffn_weight_scratch(2, FF_WHOLE))


def ffn_down_loss(act, w_down, x2, gpo, target):
    tm = 512

    def body(a_ref, w_hbm, x2_ref, g_ref, t_ref, df_ref, dx3_ref, dg_ref, loss_ref, w_ref, sems):
        def run(wait):
            f = None
            for c, (o, n) in enumerate(FF_SPANS if wait else FF_WHOLE):
                if wait:
                    wait(0, c)
                part = _dot(a_ref[:, o:o + n], w_ref[o:o + n, :])
                f = part if f is None else f + part
            gain = g_ref[...]
            err = x2_ref[...] + f * _rms(f) * gain - t_ref[...]
            dx3 = err * np.float32(1.0 / D_MODEL)
            dx3_ref[...] = dx3
            df, dg = _rms_bwd(f, gain, dx3)
            df_ref[...] = df.astype(BF16)
            loss = jnp.sum(err * err, axis=(0, 1), keepdims=True)
            if wait:
                dg_ref[...] = dg
                loss_ref[...] = loss
            else:
                dg_ref[...] += dg
                loss_ref[...] += loss

        _with_ffn_weights([w_hbm], [w_ref], sems, FF_SPANS, run)

    return _call(
        "ffn_down_loss", body, SEQ // tm,
        [_row_spec(tm, D_FF), ANY, _row_spec(tm, D_MODEL), _full_spec((1, D_MODEL)), _row_spec(tm, D_MODEL)],
        [_row_spec(tm, D_MODEL), _row_spec(tm, D_MODEL), _full_spec((1, D_MODEL)), _full_spec((1, 1))],
        [jax.ShapeDtypeStruct((SEQ, D_MODEL), BF16), jax.ShapeDtypeStruct((SEQ, D_MODEL), F32),
         jax.ShapeDtypeStruct((1, D_MODEL), F32), jax.ShapeDtypeStruct((1, 1), F32)],
        (act, w_down, x2, gpo, target), scratch_shapes=_ffn_weight_scratch(1, FF_SPANS))


def ffn_bwd(df, w_down, gate, up, w_gate_t, w_up_t, x2, gpf, dx3, y, gpm, after=()):
    tm = 256

    def body(df_ref, wd_hbm, g_ref, u_ref, wg_hbm, wu_hbm, x2_ref, gpf_ref, dx3_ref, y_ref, gpm_ref,
             dg_ref, du_ref, dx2_ref, dy_ref, dgpf_ref, dgpm_ref, wd_ref, wg_ref, wu_ref, sems):
        def run(wait):
            if wait:
                wait(0, 0)
            dact = _dot_nt(df_ref[...], wd_ref[...])
            g = g_ref[...].astype(F32)
            s = jax.nn.sigmoid(g)
            dup = (dact * g * s).astype(BF16)
            dgate = (dact * u_ref[...].astype(F32) * (s * (1.0 + g * (1.0 - s)))).astype(BF16)
            du_ref[...] = dup
            dg_ref[...] = dgate
            if wait:
                wait(1, 0)
                wait(2, 0)
            dh2 = _dot(dgate, wg_ref[...]) + _dot(dup, wu_ref[...])
            dz, dgpf = _rms_bwd(x2_ref[...], gpf_ref[...], dh2)
            dx2 = dx3_ref[...] + dz
            dx2_ref[...] = dx2
            dy, dgpm = _rms_bwd(y_ref[...], gpm_ref[...], dx2)
            dy_ref[...] = dy.astype(BF16)
            if wait:
                dgpf_ref[...] = dgpf
                dgpm_ref[...] = dgpm
            else:
                dgpf_ref[...] += dgpf
                dgpm_ref[...] += dgpm

        _with_ffn_weights([wd_hbm, wg_hbm, wu_hbm], [wd_ref, wg_ref, wu_ref], sems, FF_WHOLE, run)

    vec = jax.ShapeDtypeStruct((1, D_MODEL), F32)
    ff16 = jax.ShapeDtypeStruct((SEQ, D_FF), BF16)
    return _call(
        "ffn_bwd", body, SEQ // tm,
        [_row_spec(tm, D_MODEL), ANY, _row_spec(tm, D_FF), _row_spec(tm, D_FF), ANY, ANY, _row_spec(tm, D_MODEL),
         _full_spec((1, D_MODEL)), _row_spec(tm, D_MODEL), _row_spec(tm, D_MODEL), _full_spec((1, D_MODEL))],
        [_row_spec(tm, D_FF), _row_spec(tm, D_FF), _row_spec(tm, D_MODEL), _row_spec(tm, D_MODEL),
         _full_spec((1, D_MODEL)), _full_spec((1, D_MODEL))],
        [ff16, ff16, jax.ShapeDtypeStruct((SEQ, D_MODEL), F32), jax.ShapeDtypeStruct((SEQ, D_MODEL), BF16), vec, vec],
        (df, w_down, gate, up, w_gate_t, w_up_t, x2, gpf, dx3, y, gpm), scratch_shapes=_ffn_weight_scratch(3, FF_WHOLE), after=after)


def weight_grads(name, lhs, b, after=()):
    m, n, k = lhs[0].shape[1], b.shape[1], len(lhs)
    tr = 512 if m % 512 == 0 else m // 2

    def body(*refs):
        for a_ref, o_ref in zip(refs[:k], refs[k + 1:]):
            o_ref[...] = _dot_tn(a_ref[...], refs[k][...]).astype(BF16)

    outs = _call(
        name, body, m // tr, [pl.BlockSpec((SEQ, tr), lambda i: (0, i))] * k + [_weight_spec((SEQ, n))],
        [_row_spec(tr, n)] * k, [jax.ShapeDtypeStruct((m, n), BF16)] * k, (*lhs, b), after=after)
    return [out.reshape(N_DEV, m // N_DEV, n) for out in outs]


def weight_grad(name, a, b, after=()):
    return weight_grads(name, [a], b, after)[0]


def weight_grad_of_parts(name, parts, b, after=()):
    p, n, k = parts[0].shape[1], b.shape[1], len(parts)
    tr = 512
    per = p // tr

    def body(*refs):
        tile = pl.program_id(0)
        for j in range(k):
            @pl.when(tile // per == j)
            def _(j=j):
                refs[k + 1][...] = _dot_tn(refs[j][...].astype(BF16), refs[k][...]).astype(BF16)

    def part_spec(j):
        return pl.BlockSpec((SEQ, tr), lambda i: (0, jnp.clip(i - per * j, 0, per - 1)))

    (out,) = _call(
        name, body, k * per, [part_spec(j) for j in range(k)] + [_weight_spec((SEQ, n))],
        [_row_spec(tr, n)], [jax.ShapeDtypeStruct((k * p, n), BF16)], (*parts, b), after=after)
    return out.reshape(N_DEV, k * p // N_DEV, n)


def mix_bwd(dy, w_out, attn, sgu, ga, gs, after=()):
    tm = 512
    n_steps = SEQ // tm

    def body(dy_ref, w_ref, a_ref, s_ref, ga_ref, gs_ref, ds_ref, dga_ref, dgs_ref, da_ref, scratch, sems):
        dy = dy_ref[...]
        da, dga = _rms_bwd(a_ref[...], ga_ref[...], _dot_nt(dy, w_ref[:ATTN_W, :]))
        ds, dgs = _rms_bwd(s_ref[...], gs_ref[...], _dot_nt(dy, w_ref[ATTN_W:, :]))
        ds_ref[...] = ds
        _to_residue_rows([da], [da_ref], scratch, sems, tm, n_steps)

        @pl.when(pl.program_id(0) == 0)
        def _():
            dga_ref[...] = jnp.zeros_like(dga_ref)
            dgs_ref[...] = jnp.zeros_like(dgs_ref)

        dga_ref[...] += dga
        dgs_ref[...] += dgs

    half = jax.ShapeDtypeStruct((SEQ, 512), F32)
    vec = jax.ShapeDtypeStruct((1, 512), F32)
    return _call(
        "mix_bwd", body, n_steps,
        [_row_spec(tm, D_MODEL), _weight_spec((D_MODEL, D_MODEL)), _row_spec(tm, 512), _row_spec(tm, 512),
         _full_spec((1, 512)), _full_spec((1, 512))],
        [_row_spec(tm, 512), _full_spec((1, 512)), _full_spec((1, 512)), ANY],
        [half, vec, vec, half], (dy, w_out, attn, sgu, ga, gs), scratch_shapes=_residue_scratch(1, tm, ATTN_W), after=after)


def sgu_bwd(u, vs, dsgu, lg, lb, w_sp, bfull, after=()):
    cpb = 4

    def body(u_ref, vs_ref, d_ref, lg_ref, lb_ref, w_ref, b_ref, du_ref, dvs_ref, dlg_ref, dlb_ref, dw_ref, db_ref):
        wc, causal = _causal_weights(w_ref)
        head0 = lax.broadcasted_iota(jnp.int32, (CHUNK, 128), 1) < HEAD_DIM
        lg = lg_ref[...]

        @pl.when(pl.program_id(0) == 0)
        def _():
            dlg_ref[...] = jnp.zeros_like(dlg_ref)
            dlb_ref[...] = jnp.zeros_like(dlb_ref)
            dw_ref[...] = jnp.zeros_like(dw_ref)
            db_ref[...] = jnp.zeros_like(db_ref)

        for ci in range(cpb):
            rows = pl.ds(ci * CHUNK, CHUNK)
            u = u_ref[rows, :]
            vs = vs_ref[rows, :]
            d = d_ref[rows, :]
            ug, xhat, rstd, vn, ms = _sgu_chunk_fwd(u, vs, lg, lb_ref[...], wc, b_ref[...], head0)
            du_ref[rows, :] = (d * ms * _gelu_grad(u)).astype(BF16)
            dms = d * ug
            db_ref[...] += dms
            dvn = []
            for gp in range(SGU_W // 128):
                dmp = dms[:, gp * 128:(gp + 1) * 128]
                dm0 = jnp.where(head0, dmp, 0.0).astype(BF16)
                dm1 = jnp.where(head0, 0.0, dmp).astype(BF16)
                vp = vn[:, gp * 128:(gp + 1) * 128].astype(BF16)
                dw_ref[2 * gp] += _dot_nt(dm0, vp)
                dw_ref[2 * gp + 1] += _dot_nt(dm1, vp)
                dvn.append(_dot_tn(wc[2 * gp], dm0) + _dot_tn(wc[2 * gp + 1], dm1))
            dvn = jnp.concatenate(dvn, axis=1)
            dlg_ref[...] += jnp.sum(dvn * xhat, axis=0, keepdims=True)
            dlb_ref[...] += jnp.sum(dvn, axis=0, keepdims=True)
            dxh = dvn * lg
            dvg = rstd * (dxh - jnp.mean(dxh, axis=-1, keepdims=True) - xhat * jnp.mean(dxh * xhat, axis=-1, keepdims=True))
            dvs_ref[rows, :] = (dvg * _gelu_grad(vs)).astype(BF16)

        @pl.when(pl.program_id(0) == pl.num_programs(0) - 1)
        def _():
            for g in range(N_GROUPS):
                dw_ref[g] = jnp.where(causal, dw_ref[g], 0.0)

    tm = cpb * CHUNK
    half16 = jax.ShapeDtypeStruct((SEQ, SGU_W), BF16)
    vec = jax.ShapeDtypeStruct((1, SGU_W), F32)
    return _call(
        "sgu_bwd", body, SEQ // tm,
        [_row_spec(tm, SGU_W)] * 3 + [_full_spec((1, SGU_W)), _full_spec((1, SGU_W)),
                                      _full_spec((N_GROUPS, CHUNK, CHUNK)), _full_spec((CHUNK, SGU_W))],
        [_row_spec(tm, SGU_W), _row_spec(tm, SGU_W), _full_spec((1, SGU_W)), _full_spec((1, SGU_W)),
         _full_spec((N_GROUPS, CHUNK, CHUNK)), _full_spec((CHUNK, SGU_W))],
        [half16, half16, vec, vec, jax.ShapeDtypeStruct((N_GROUPS, CHUNK, CHUNK), F32),
         jax.ShapeDtypeStruct((CHUNK, SGU_W), F32)],
        (u, vs, dsgu, lg, lb, w_sp, bfull), after=after)


def attn_bwd(q, k, v, o, lse, do, pos_col, rot):
    n_steps = ATTN_W // 128

    def body(q_ref, k_ref, v_ref, o_ref, lse_ref, do_ref, pos_ref, invf_ref, ma_ref, mb_ref,
             dq_ref, dk_ref, dv_ref, dqa_ref, dka_ref, dva_ref, dlt_ref, rot_ref, out_ref, sems):
        step = pl.program_id(0)
        slot = step % 2

        def back(at_step, s):
            lanes = pl.ds(pl.multiple_of(at_step * 128, 128), 128)
            return [pltpu.make_async_copy(out_ref.at[s, a, pl.ds(b * (SEQ // RES), SEQ // RES), :], nat.at[:, b, lanes],
                                          sems.at[s, a, b]) for a, nat in enumerate((dq_ref, dk_ref, dv_ref)) for b in range(RES)]

        dqa_ref[...] = jnp.zeros_like(dqa_ref)
        dka_ref[...] = jnp.zeros_like(dka_ref)
        dva_ref[...] = jnp.zeros_like(dva_ref)

        def delta(i, carry):
            rows = pl.ds(pl.multiple_of(i * 256, 256), 256)
            prod = do_ref[rows, :] * o_ref[rows, :]
            h0 = lax.broadcasted_iota(jnp.int32, (256, 128), 1) < HEAD_DIM
            d0 = jnp.sum(jnp.where(h0, prod, 0.0), axis=-1, keepdims=True)
            d1 = jnp.sum(jnp.where(h0, 0.0, prod), axis=-1, keepdims=True)
            dlt_ref[rows, :] = jnp.where(h0, d0, d1)
            return carry

        lax.fori_loop(0, SEQ // 256, delta, 0)

        def add_rows(ref, slices, val):
            at = 0
            for start, size in slices:
                ref[pl.ds(start, size), :] += val[at:at + size]
                at += size

        def group(p, masks, blocks):
            head0, mask1, mask2 = masks
            heads = (head0, jnp.logical_not(head0))
            keys = [rows if prev is None else prev + rows for rows, prev in blocks]
            mask = [mask1 if prev is None else mask2 for _, prev in blocks]
            kk = [_load_rows(k_ref, ks).astype(BF16) for ks in keys]
            vv = [_load_rows(v_ref, ks).astype(BF16) for ks in keys]
            qb = [_load_rows(q_ref, rows) for rows, _ in blocks]
            dob = [_load_rows(do_ref, rows) for rows, _ in blocks]
            lse_b = [_load_rows(lse_ref, rows) for rows, _ in blocks]
            dlt_b = [_load_rows(dlt_ref, rows) for rows, _ in blocks]
            chains = [(g, h) for g in range(len(blocks)) for h in range(2)]
            qm = [jnp.where(heads[h], qb[g], 0.0).astype(BF16) for g, h in chains]
            dom = [jnp.where(heads[h], dob[g], 0.0).astype(BF16) for g, h in chains]
            s = [_dot_nt(qm[c], kk[g]) for c, (g, h) in enumerate(chains)]
            dp = [_dot_nt(dom[c], vv[g]) for c, (g, h) in enumerate(chains)]
            pr = [jnp.where(mask[g], jnp.exp(s[c] - lse_b[g][:, h * HEAD_DIM:h * HEAD_DIM + 1]), 0.0)
                  for c, (g, h) in enumerate(chains)]
            ds = [(pr[c] * (dp[c] - dlt_b[g][:, h * HEAD_DIM:h * HEAD_DIM + 1])).astype(BF16)
                  for c, (g, h) in enumerate(chains)]
            dv = [_dot_tn(pr[c].astype(BF16), dom[c]) for c in range(len(chains))]
            dk = [_dot_tn(ds[c], qm[c]) for c in range(len(chains))]
            dq = [_dot(ds[c], kk[g]) for c, (g, h) in enumerate(chains)]
            for g, (rows, _) in enumerate(blocks):
                add_rows(dqa_ref, rows, jnp.where(head0, dq[2 * g], dq[2 * g + 1]))
                add_rows(dka_ref, keys[g], dk[2 * g] + dk[2 * g + 1])
                add_rows(dva_ref, keys[g], dv[2 * g] + dv[2 * g + 1])

        _for_each_group(group)

        @pl.when(pl.program_id(0) == 0)
        def _():
            def tables(i, carry):
                rows = pl.ds(pl.multiple_of(i * 256, 256), 256)
                c, sa, sb = _rot_tables(pos_ref[rows, :], invf_ref[...], ma_ref[...], mb_ref[...])
                rot_ref[0, rows, :] = c
                rot_ref[1, rows, :] = sa
                rot_ref[2, rows, :] = sb
                return carry

            lax.fori_loop(0, SEQ // 256, tables, 0)

        @pl.when(step >= 2)
        def _():
            for cp in back(step - 2, slot):
                cp.wait()

        def finish(i, carry):
            rows = pl.ds(pl.multiple_of(i * 256, 256), 256)
            c, sa, sb = rot_ref[0, rows, :], rot_ref[1, rows, :], rot_ref[2, rows, :]
            out_ref[slot, 0, rows, :] = _rot_t(dqa_ref[rows, :] * Q_SCALE, c, sa, sb)
            out_ref[slot, 1, rows, :] = _rot_t(dka_ref[rows, :], c, sa, sb)
            out_ref[slot, 2, rows, :] = dva_ref[rows, :]
            return carry

        lax.fori_loop(0, SEQ // 256, finish, 0)
        for cp in back(step, slot):
            cp.start()

        @pl.when(step == n_steps - 1)
        def _():
            for cp in back(step - 1, 1 - slot) + back(step, slot):
                cp.wait()

    slab = pl.BlockSpec((SEQ, 128), lambda i: (0, i))
    out = jax.ShapeDtypeStruct((SEQ // RES, RES, ATTN_W), F32)
    acc = pltpu.VMEM((SEQ, 128), F32)
    outs = _call(
        "attn_bwd", body, n_steps,
        [slab] * 6 + [_full_spec((SEQ, 1)), _full_spec((1, 128)), _full_spec((1, 128)), _full_spec((1, 128))],
        [ANY] * 3, [out, out, out], (q, k, v, o, lse, do, pos_col, *rot),
        scratch_shapes=[acc, acc, acc, acc, pltpu.VMEM((3, SEQ, 128), F32), pltpu.VMEM((2, 3, SEQ, 128), F32),
                        pltpu.SemaphoreType.DMA((2, 3, RES))])
    return [t.reshape(SEQ, ATTN_W) for t in outs]


def in_bwd(dproj_parts, w_in_t, x, g1, dx2, after=()):
    tm = 512
    k = len(dproj_parts)

    def body(*refs):
        w_ref, x_ref, g_ref, dx2_ref, dx_ref, dg_ref = refs[k:]
        dh1 = _dot(refs[0][...].astype(BF16), w_ref[0:512, :])
        for j in range(1, k):
            dh1 = dh1 + _dot(refs[j][...].astype(BF16), w_ref[512 * j:512 * (j + 1), :])
        dz, dg = _rms_bwd(x_ref[...], g_ref[...], dh1)
        dx_ref[...] = dx2_ref[...] + dz

        @pl.when(pl.program_id(0) == 0)
        def _():
            dg_ref[...] = jnp.zeros_like(dg_ref)

        dg_ref[...] += dg

    return _call(
        "in_bwd", body, SEQ // tm,
        [_row_spec(tm, 512)] * k + [_weight_spec((IN_W, D_MODEL)), _row_spec(tm, D_MODEL), _full_spec((1, D_MODEL)),
                                    _row_spec(tm, D_MODEL)],
        [_row_spec(tm, D_MODEL), _full_spec((1, D_MODEL))],
        [jax.ShapeDtypeStruct((SEQ, D_MODEL), F32), jax.ShapeDtypeStruct((1, D_MODEL), F32)],
        (*dproj_parts, w_in_t, x, g1, dx2), after=after)


def _coords():
    return lax.axis_index("x"), lax.axis_index("y"), lax.axis_index("c")


class Exchange:
    def __init__(self, srcs, bufs, new_shapes, n_sems, make):
        self.srcs, self.bufs, self.new_shapes, self.n_sems, self.make = list(srcs), list(bufs), list(new_shapes), n_sems, make


def _call(name, body, n_steps, in_specs, out_specs, out_shape, args, scratch_shapes=(), after=()):
    n_in = len(args)

    def wrapped(*refs):
        body(*refs[:n_in], *refs[n_in + len(after):])

    return list(pl.pallas_call(
        wrapped, name=name, grid=(n_steps,), in_specs=list(in_specs) + [ANY] * len(after), out_specs=list(out_specs),
        out_shape=list(out_shape), scratch_shapes=list(scratch_shapes), compiler_params=_params(),
    )(*args, *after))


GATHER_SEMS = 8


def gather(bufs):
    n = len(bufs)

    def make(src_refs, buf_refs, new_refs, send_sems, recv_sems):
        x, y, c = _coords()
        me, sibling = (x, y, c), (x, y, 1 - c)
        over_x, over_y, across = (1 - x, y), (x, 1 - y), (1 - x, 1 - y)

        def copy(a, k, block, to, half=None):
            r = buf_refs[a].shape[0] // N_DEV
            lo, size = (0, r) if half is None else (half * (r // 2), r // 2)
            rows = buf_refs[a].at[pl.ds((4 * block[0] + 2 * block[1] + block[2]) * r + lo, size), :]
            return pltpu.make_async_remote_copy(
                src_ref=rows, dst_ref=rows, send_sem=send_sems.at[GATHER_SEMS * a + k],
                recv_sem=recv_sems.at[GATHER_SEMS * a + k], device_id=to, device_id_type=MESH)

        every = range(n)
        out = ([copy(a, 0, me, sibling) for a in every] + [copy(a, 1, me, (*over_x, c)) for a in every]
               + [copy(a, 2, me, (*over_y, c)) for a in every])
        near_in = [copy(a, 1, (*over_x, c), me) for a in every] + [copy(a, 2, (*over_y, c), me) for a in every]
        relay = ([copy(a, 3, (*over_x, c), (*over_y, c), half=0) for a in every]
                 + [copy(a, 4, (*over_y, c), (*over_x, c), half=1) for a in every])
        near_on = [copy(a, 5, (*over_x, c), sibling) for a in every] + [copy(a, 6, (*over_y, c), sibling) for a in every]
        relay_in = ([copy(a, 3, (*across, c), me, half=0) for a in every]
                    + [copy(a, 4, (*across, c), me, half=1) for a in every])
        far_on = [copy(a, 7, (*across, c), sibling) for a in every]
        from_core = ([copy(a, 0, sibling, me) for a in every] + [copy(a, 5, (*over_x, 1 - c), me) for a in every]
                     + [copy(a, 6, (*over_y, 1 - c), me) for a in every] + [copy(a, 7, (*across, 1 - c), me) for a in every])
        stages = [([], out), (near_in, relay + near_on), (relay_in, far_on)]
        return stages, out + relay + near_on + far_on, from_core

    return Exchange([], bufs, [], GATHER_SEMS * n, make)


TO_GATHER = (1, lambda x, y, c: [(x, y, 1 - c), (1 - x, y, c), (x, 1 - y, c)])
TO_SIBLING = (2, lambda x, y, c: [(x, y, 1 - c)])
TO_CHIPS = (3, lambda x, y, c: [(1 - x, y, c), (x, 1 - y, c), (1 - x, 1 - y, c)])
TO_ALL = (4, lambda x, y, c: [(x ^ (m >> 2), y ^ ((m >> 1) & 1), c ^ (m & 1)) for m in range(1, N_DEV)])


def by_sequencer(name, exchanges, who):
    collective_id, peers_of = who
    hbm = pltpu.MemorySpace.HBM
    refs = [([jax.new_ref(a, memory_space=hbm) for a in ex.srcs], [jax.new_ref(a, memory_space=hbm) for a in ex.bufs],
             [jax.empty_ref(s, memory_space=hbm) for s in ex.new_shapes]) for ex in exchanges]
    sems = []
    for ex in exchanges:
        sems += [pltpu.SemaphoreType.DMA((ex.n_sems,)), pltpu.SemaphoreType.DMA((ex.n_sems,))]

    @pl.kernel(mesh=plsc.ScalarSubcoreMesh(axis_name="sequencer", num_cores=1), name=name, scratch_types=tuple(sems),
               compiler_params=pltpu.CompilerParams(collective_id=collective_id))
    def launch(*sem_refs):
        peers = peers_of(*_coords())
        barrier = pltpu.get_barrier_semaphore()
        for peer in peers:
            pl.semaphore_signal(barrier, inc=1, device_id=peer, device_id_type=MESH)
        pl.semaphore_wait(barrier, len(peers))

        made = [ex.make(*refs[k], sem_refs[2 * k], sem_refs[2 * k + 1]) for k, ex in enumerate(exchanges)]
        for stage in range(max(len(stages) for stages, _, _ in made)):
            for stages, _, _ in made:
                if stage < len(stages):
                    arrivals, starts = stages[stage]
                    for cp in arrivals:
                        cp.wait_recv()
                    for cp in starts:
                        cp.start()
        for _, sends, arrivals in made:
            for cp in arrivals:
                cp.wait_recv()
            for cp in sends:
                cp.wait_send()

    launch()
    return [([ref[...] for ref in bufs], [ref[...] for ref in news]) for _, bufs, news in refs]


def place_shards(name, shards, dev):
    n = len(shards)

    def body(dev_ref, *refs):
        for a in range(n):
            refs[n + a][...] = refs[a][...].astype(BF16)

    spec = pltpu.PrefetchScalarGridSpec(
        num_scalar_prefetch=1, grid=(1,),
        in_specs=[pl.BlockSpec(s.shape, lambda i, dev_ref: (0, 0)) for s in shards],
        out_specs=[pl.BlockSpec(s.shape, lambda i, dev_ref: (dev_ref[0], 0)) for s in shards])
    return pl.pallas_call(
        body, name=name, grid_spec=spec,
        out_shape=[jax.ShapeDtypeStruct((N_DEV * s.shape[0], s.shape[1]), BF16) for s in shards],
        compiler_params=_params(),
    )(dev, *shards)


def _swap(copies_of):
    def make(src_refs, buf_refs, new_refs, send_sems, recv_sems):
        copies = copies_of(src_refs, new_refs, send_sems, recv_sems)
        return [([], copies)], copies, copies

    return make


def to_sibling(grads):
    def copies_of(src_refs, new_refs, send_sems, recv_sems):
        x, y, c = _coords()
        return [pltpu.make_async_remote_copy(
            src_ref=src_refs[a].at[2 * xy + 1 - c], dst_ref=new_refs[a].at[xy], send_sem=send_sems.at[4 * a + xy],
            recv_sem=recv_sems.at[4 * a + xy], device_id=(x, y, 1 - c), device_id_type=MESH)
            for a in range(len(src_refs)) for xy in range(4)]

    return Exchange(grads, [], [jax.ShapeDtypeStruct((4,) + g.shape[1:], g.dtype) for g in grads], 4 * len(grads),
                    _swap(copies_of))


def to_chips(parts):
    def copies_of(src_refs, new_refs, send_sems, recv_sems):
        x, y, c = _coords()
        chips = [(1 - x, y), (x, 1 - y), (1 - x, 1 - y)]
        return [pltpu.make_async_remote_copy(
            src_ref=src_refs[a].at[2 * px + py], dst_ref=new_refs[a].at[2 * x + y], send_sem=send_sems.at[3 * a + j],
            recv_sem=recv_sems.at[3 * a + j], device_id=(px, py, c), device_id_type=MESH)
            for a in range(len(src_refs)) for j, (px, py) in enumerate(chips)]

    return Exchange(parts, [], [jax.ShapeDtypeStruct(p.shape, p.dtype) for p in parts], 3 * len(parts), _swap(copies_of))


def to_owners(grad):
    def copies_of(src_refs, new_refs, send_sems, recv_sems):
        x, y, c = _coords()
        copies = []
        for m in range(1, N_DEV):
            px, py, pc = x ^ (m >> 2), y ^ ((m >> 1) & 1), c ^ (m & 1)
            copies.append(pltpu.make_async_remote_copy(
                src_ref=src_refs[0].at[4 * px + 2 * py + pc], dst_ref=new_refs[0].at[4 * x + 2 * y + c],
                send_sem=send_sems.at[m - 1], recv_sem=recv_sems.at[m - 1], device_id=(px, py, pc), device_id_type=MESH))
        return copies

    return Exchange([grad], [], [jax.ShapeDtypeStruct(grad.shape, grad.dtype)], N_DEV - 1, _swap(copies_of))


def to_everyone(vec):
    def copies_of(src_refs, new_refs, send_sems, recv_sems):
        x, y, c = _coords()
        copies = []
        for m in range(1, N_DEV):
            px, py, pc = x ^ (m >> 2), y ^ ((m >> 1) & 1), c ^ (m & 1)
            copies.append(pltpu.make_async_remote_copy(
                src_ref=src_refs[0], dst_ref=new_refs[0].at[4 * x + 2 * y + c],
                send_sem=send_sems.at[m - 1], recv_sem=recv_sems.at[m - 1], device_id=(px, py, pc), device_id_type=MESH))
        return copies

    return Exchange([vec], [], [jax.ShapeDtypeStruct((N_DEV,) + vec.shape, vec.dtype)], N_DEV - 1, _swap(copies_of))


def sum_cores(name, grads, others, core, after=()):
    k = len(grads)

    def body(core_ref, *refs):
        for j in range(k):
            out_ref = refs[2 * k + len(after) + j]
            out_ref[...] = (refs[j][:, 0].astype(F32) + refs[k + j][...].astype(F32)).astype(out_ref.dtype)

    mine = [pl.BlockSpec((2, 1) + o.shape[1:], lambda i, core_ref: (i, core_ref[0], 0, 0)) for o in others]
    theirs = [pl.BlockSpec((2,) + o.shape[1:], lambda i, core_ref: (i, 0, 0)) for o in others]
    return pl.pallas_call(
        body, name=name,
        grid_spec=pltpu.PrefetchScalarGridSpec(
            num_scalar_prefetch=1, grid=(2,), in_specs=mine + theirs + [ANY] * len(after), out_specs=theirs),
        out_shape=[jax.ShapeDtypeStruct(o.shape, o.dtype) for o in others],
        compiler_params=_params(),
    )(core, *[g.reshape((4, 2) + g.shape[1:]) for g in grads], *others, *after)


def sum_owned(name, grad, others, dev_ids, after=()):
    _, r, w = grad.shape

    def body(ids_ref, *refs):
        acc = refs[0][0]
        for k in range(1, N_DEV):
            acc = acc + refs[k][0]
        refs[-1][...] = acc

    def pick(k):
        return pl.BlockSpec((1, r, w), lambda i, ids_ref: (ids_ref[k], 0, 0))

    return pl.pallas_call(
        body, name=name,
        grid_spec=pltpu.PrefetchScalarGridSpec(
            num_scalar_prefetch=1, grid=(1,), in_specs=[pick(k) for k in range(N_DEV)] + [ANY] * len(after),
            out_specs=pl.BlockSpec((r, w), lambda i, ids_ref: (ids_ref[0], 0))),
        out_shape=jax.ShapeDtypeStruct((N_DEV * r, w), F32),
        compiler_params=_params(),
    )(dev_ids, grad, *([others] * (N_DEV - 1)), *after)


def _adamw_update(w, g, m, v):
    nm = ADAM_B1 * m + np.float32(1.0 - ADAM_B1) * g
    nv = ADAM_B2 * v + np.float32(1.0 - ADAM_B2) * (g * g)
    m_hat = nm / np.float32(1.0 - ADAM_B1 ** ADAM_STEP)
    v_hat = nv / np.float32(1.0 - ADAM_B2 ** ADAM_STEP)
    return -ADAM_LR * (m_hat / (jnp.sqrt(v_hat) + ADAM_EPS) + ADAM_WD * w), nm, nv


def adamw_of_sums(name, parts, others, chip_ids, ws, ms, vs, after):
    n = len(parts)
    halves = 2

    def body(ids_ref, *refs):
        outs = refs[7 * n + len(after):]
        for j in range(n):
            p_ref, a_ref, b_ref, c_ref, w_ref, m_ref, v_ref = refs[7 * j:7 * j + 7]
            g = ((p_ref[0].astype(F32) + a_ref[0].astype(F32)) + b_ref[0].astype(F32)) + c_ref[0].astype(F32)
            outs[4 * j][...] = g
            outs[4 * j + 1][...], outs[4 * j + 2][...], outs[4 * j + 3][...] = _adamw_update(w_ref[...], g, m_ref[...], v_ref[...])

    in_specs, out_specs, out_shape, operands = [], [], [], []
    for part, other, w, m, v in zip(parts, others, ws, ms, vs):
        _, r, wd = part.shape
        rows = r // halves
        whole = pl.BlockSpec((rows, wd), lambda i, ids_ref: (i, 0))
        in_specs += [pl.BlockSpec((1, rows, wd), lambda i, ids_ref, k=k: (ids_ref[k], i, 0)) for k in range(4)] + [whole] * 3
        out_specs += [whole] * 4
        out_shape += [jax.ShapeDtypeStruct((r, wd), F32)] * 4
        operands += [part, other, other, other, w, m, v]
    outs = pl.pallas_call(
        body, name=name,
        grid_spec=pltpu.PrefetchScalarGridSpec(
            num_scalar_prefetch=1, grid=(halves,), in_specs=in_specs + [ANY] * len(after), out_specs=out_specs),
        out_shape=out_shape,
        compiler_params=_params(),
    )(chip_ids, *operands, *after)
    return [tuple(outs[4 * j:4 * j + 4]) for j in range(n)]


def pack_small(parts):
    names = [name for name, _ in SMALL if name in parts]
    operands = [parts[name] for name in names]
    first_row, at = {}, 0
    for name, size in SMALL:
        first_row[name] = at // 128
        at += size
    sizes = dict(SMALL)

    def body(*refs):
        out_ref = refs[-1]
        out_ref[...] = jnp.zeros_like(out_ref)
        for name, ref in zip(names, refs):
            row = first_row[name]
            if name == "loss_sum":
                lane0 = lax.broadcasted_iota(jnp.int32, (1, 128), 1) == 0
                out_ref[row:row + 1, :] = jnp.where(lane0, ref[...], 0.0)
            else:
                rows = sizes[name] // 128
                out_ref[row:row + rows, :] = ref[...].reshape(rows, 128)

    vmem = pl.BlockSpec(memory_space=pltpu.VMEM)
    return pl.pallas_call(
        body, name="pack_small", in_specs=[vmem] * len(names), out_specs=vmem,
        out_shape=jax.ShapeDtypeStruct((SMALL_ROWS, 128), F32), compiler_params=_params(()),
    )(*operands)


LATE = "pre_mix_norm"


def adamw_small(packed_g, late, ws, ms, vs, after=()):
    names = [LATE] if late else [name for name, _ in SMALL if name not in ("loss_sum", LATE)]
    k = len(names)
    shapes = [ws[name].shape[1:] if ws[name].ndim > 2 else ws[name].shape for name in names]
    first_row, at = {}, 0
    for name, size in SMALL:
        first_row[name] = at // 128
        at += size
    sizes = dict(SMALL)
    n_in = 3 if late else 1

    def body(*refs):
        w_refs, m_refs, v_refs = (refs[n_in + j * k:n_in + (j + 1) * k] for j in range(3))
        outs = refs[n_in + 3 * k + len(after):]
        for i, name in enumerate(names):
            if late:
                late_ref, own_ref, dev_ref = refs[:n_in]
                parts = [jnp.where(dev_ref[0] == j, own_ref[...], late_ref[j]) for j in range(N_DEV)]
                g = parts[0]
                for j in range(1, N_DEV):
                    g = g + parts[j]
            else:
                g = refs[0][first_row[name]:first_row[name] + sizes[name] // 128, :].reshape(shapes[i])
            outs[i][...] = g
            outs[k + i][...], outs[2 * k + i][...], outs[3 * k + i][...] = _adamw_update(
                w_refs[i][...], g, m_refs[i][...], v_refs[i][...])
        if not late:
            outs[4 * k][...] = refs[0][first_row["loss_sum"]:first_row["loss_sum"] + 1, 0:1]

    vmem = pl.BlockSpec(memory_space=pltpu.VMEM)
    operands = [t[name].reshape(shape) for t in (ws, ms, vs) for name, shape in zip(names, shapes)]
    outs = pl.pallas_call(
        body, name="adamw_late" if late else "adamw_small",
        in_specs=([vmem, vmem, pl.BlockSpec(memory_space=pltpu.SMEM)] if late else [vmem]) + [vmem] * (3 * k) + [ANY] * len(after),
        out_specs=[vmem] * (4 * k + (0 if late else 1)),
        out_shape=[jax.ShapeDtypeStruct(shape, F32) for _ in range(4) for shape in shapes]
        + ([] if late else [jax.ShapeDtypeStruct((1, 1), F32)]),
        compiler_params=_params(()),
    )(*(late or (packed_g,)), *operands, *after)
    tables = [{name: outs[j * k + i].reshape(ws[name].shape) for i, name in enumerate(names)} for j in range(4)]
    return (*tables, None if late else outs[4 * k])


def kernel(x, positions, pre_mix_norm, w_in, sgu_ln_gain, sgu_ln_bias, sgu_w_spatial, sgu_b_spatial, attn_out_norm, sgu_out_norm, w_out, post_mix_norm, pre_ffn_norm, w_gate, w_up, w_down, post_ffn_norm, loss_target, m_pre_mix_norm, m_w_in, m_sgu_ln_gain, m_sgu_ln_bias, m_sgu_w_spatial, m_sgu_b_spatial, m_attn_out_norm, m_sgu_out_norm, m_w_out, m_post_mix_norm, m_pre_ffn_norm, m_w_gate, m_w_up, m_w_down, m_post_ffn_norm, v_pre_mix_norm, v_w_in, v_sgu_ln_gain, v_sgu_ln_bias, v_sgu_w_spatial, v_sgu_b_spatial, v_attn_out_norm, v_sgu_out_norm, v_w_out, v_post_mix_norm, v_pre_ffn_norm, v_w_gate, v_w_up, v_w_down, v_post_ffn_norm):
    small_w = dict(pre_mix_norm=pre_mix_norm, sgu_ln_gain=sgu_ln_gain, sgu_ln_bias=sgu_ln_bias, sgu_w_spatial=sgu_w_spatial,
                   sgu_b_spatial=sgu_b_spatial, attn_out_norm=attn_out_norm, sgu_out_norm=sgu_out_norm,
                   post_mix_norm=post_mix_norm, pre_ffn_norm=pre_ffn_norm, post_ffn_norm=post_ffn_norm)
    small_m = dict(pre_mix_norm=m_pre_mix_norm, sgu_ln_gain=m_sgu_ln_gain, sgu_ln_bias=m_sgu_ln_bias, sgu_w_spatial=m_sgu_w_spatial,
                   sgu_b_spatial=m_sgu_b_spatial, attn_out_norm=m_attn_out_norm, sgu_out_norm=m_sgu_out_norm,
                   post_mix_norm=m_post_mix_norm, pre_ffn_norm=m_pre_ffn_norm, post_ffn_norm=m_post_ffn_norm)
    small_v = dict(pre_mix_norm=v_pre_mix_norm, sgu_ln_gain=v_sgu_ln_gain, sgu_ln_bias=v_sgu_ln_bias, sgu_w_spatial=v_sgu_w_spatial,
                   sgu_b_spatial=v_sgu_b_spatial, attn_out_norm=v_attn_out_norm, sgu_out_norm=v_sgu_out_norm,
                   post_mix_norm=v_post_mix_norm, pre_ffn_norm=v_pre_ffn_norm, post_ffn_norm=v_post_ffn_norm)

    x2d = x[0]
    target = loss_target[0]
    pos_col = positions.reshape(SEQ, 1)
    rot = _rot_consts()
    w_sp = sgu_w_spatial[0]
    bfull = jnp.repeat(sgu_b_spatial[0].T, HEAD_DIM, axis=1)

    x_i, y_i, c_i = (lax.axis_index(a).astype(jnp.int32) for a in MESH_AXES)
    dev = 4 * x_i + 2 * y_i + c_i
    core = c_i.reshape(1)
    chip = 2 * x_i + y_i
    chip_ids = jnp.stack([chip, chip ^ 1, chip ^ 2, chip ^ 3])
    dev_ids = jnp.stack([dev ^ m for m in range(N_DEV)])

    def gathered(name, bufs):
        return by_sequencer(name, [gather(bufs)], TO_GATHER)[0][0]

    def from_sibling(name, grads):
        return by_sequencer(name, [to_sibling(grads)], TO_SIBLING)[0][1]

    def from_chips(name, parts):
        return by_sequencer(name, [to_chips(parts)], TO_CHIPS)[0][1]

    (w_in_t,) = place_shards("place_w_in", [w_in[0].T], dev.reshape(1))
    (w_in_t,) = gathered("gather_w_in", [w_in_t])
    w_gate_t, w_up_t, w_out_f, w_down_f = place_shards(
        "place_weights", [w_gate[0].T, w_up[0].T, w_out[0], w_down[0]], dev.reshape(1))
    (w_out_f,) = gathered("gather_w_out", [w_out_f])
    w_gate_t, w_up_t = gathered("gather_w_gate_up", [w_gate_t, w_up_t])
    (w_down_f,) = gathered("gather_w_down", [w_down_f])

    pos_r = _to_residue_order(pos_col)
    h1 = pre_norm(x2d, pre_mix_norm, after=[pos_r, dev_ids, chip_ids])
    u, vs, q, k, v = in_proj(h1, pos_col, w_in_t, rot)
    attn_r, lse, attn = attn_fwd(q, k, v)
    (sgu,) = sgu_fwd(u, vs, sgu_ln_gain, sgu_ln_bias, w_sp, bfull)
    mix, y, x2, h2 = out_proj(attn, sgu, x2d, attn_out_norm, sgu_out_norm, w_out_f, post_mix_norm, pre_ffn_norm)
    gate, up, act = ffn_up(h2, w_gate_t, w_up_t)
    df, dx3, d_post_ffn, sq_err = ffn_down_loss(act, w_down_f, x2, post_ffn_norm, target)

    g_w_down = weight_grad("grad_w_down", act, df)
    (s_down,) = from_sibling("w_down_to_sibling", [g_w_down])
    dgate, dup, dx2, dy, d_pre_ffn, d_post_mix = ffn_bwd(
        df, w_down_f, gate, up, w_gate_t, w_up_t, x2, pre_ffn_norm, dx3, y, post_mix_norm, after=[g_w_down])
    (p_down,) = sum_cores("sum_cores_down", [g_w_down], [s_down], core, after=[dy])
    (c_down,) = from_chips("w_down_to_chips", [p_down])
    g_w_gate, g_w_up = weight_grads("grad_w_gate_up", [dgate, dup], h2, after=[p_down])
    s_gate, s_up = from_sibling("w_gate_up_to_sibling", [g_w_gate, g_w_up])
    g_w_out = weight_grad("grad_w_out", mix, dy, after=[g_w_up])
    (s_out,) = from_sibling("w_out_to_sibling", [g_w_out])
    dsgu, d_attn_out, d_sgu_out, dattn_r = mix_bwd(dy, w_out_f, attn, sgu, attn_out_norm, sgu_out_norm, after=[g_w_out, c_down])
    p_gate, p_up = sum_cores("sum_cores_gate_up", [g_w_gate, g_w_up], [s_gate, s_up], core, after=[dsgu])
    c_gate, c_up = from_chips("w_gate_up_to_chips", [p_gate, p_up])
    du, dvs, d_ln_gain, d_ln_bias, d_w_sp, d_bfull = sgu_bwd(
        u, vs, dsgu, sgu_ln_gain, sgu_ln_bias, w_sp, bfull, after=[p_gate, p_up])

    d_b_sp = d_bfull.reshape(CHUNK, N_GROUPS, HEAD_DIM).sum(axis=-1).T
    small_g = pack_small(dict(sgu_ln_gain=d_ln_gain, sgu_ln_bias=d_ln_bias, sgu_w_spatial=d_w_sp, sgu_b_spatial=d_b_sp,
                              attn_out_norm=d_attn_out, sgu_out_norm=d_sgu_out, post_mix_norm=d_post_mix,
                              pre_ffn_norm=d_pre_ffn, post_ffn_norm=d_post_ffn, loss_sum=sq_err))
    small_g = small_g.reshape(N_DEV, SMALL_ROWS // N_DEV, 128)
    ((_, (o_small,)),) = by_sequencer("small_to_owners", [to_owners(small_g)], TO_ALL)

    dq, dk, dv = attn_bwd(q, k, v, attn_r, lse, dattn_r, pos_r, rot)
    summed_small = sum_owned("sum_small", small_g, o_small, dev_ids, after=[dq, c_gate, c_up])
    (all_small,) = gathered("gather_small_grads", [summed_small])
    (p_out,) = sum_cores("sum_cores_out", [g_w_out], [s_out], core, after=[dq])
    (c_out,) = from_chips("w_out_to_chips", [p_out])
    dproj = [dq, dk, dv, du, dvs]
    g_w_in = weight_grad_of_parts("grad_w_in", dproj, h1, after=[c_gate, c_up])
    (s_in,) = from_sibling("w_in_to_sibling", [g_w_in])

    def same(t):
        return t

    def turned(t):
        return t.T

    big = {}

    def adamw(call, weights, after):
        results = adamw_of_sums(call, [p for _, _, p, _, _, _, _ in weights], [c for _, _, _, c, _, _, _ in weights], chip_ids,
                                [turn(w[0]) for _, w, _, _, _, _, turn in weights], [turn(m[0]) for _, _, _, _, m, _, turn in weights],
                                [turn(vv[0]) for _, _, _, _, _, vv, turn in weights], after)
        for (name, _, _, _, _, _, turn), outs in zip(weights, results):
            big[name] = tuple(turn(t)[None] for t in outs)
        return results[-1][0]

    done_ffn = adamw("adamw_ffn", (("w_down", w_down, p_down, c_down, m_w_down, v_w_down, same),
                                   ("w_gate", w_gate, p_gate, c_gate, m_w_gate, v_w_gate, turned),
                                   ("w_up", w_up, p_up, c_up, m_w_up, v_w_up, turned)), [g_w_in, all_small])
    (p_in,) = sum_cores("sum_cores_in", [g_w_in], [s_in], core, after=[done_ffn, c_out])
    (c_in,) = from_chips("w_in_to_chips", [p_in])
    grad_x, d_pre_mix = in_bwd(dproj, w_in_t, x2d, pre_mix_norm, dx2, after=[p_in])
    ((_, (late_parts,)),) = by_sequencer("pre_mix_to_everyone", [to_everyone(d_pre_mix)], TO_ALL)
    done_out = adamw("adamw_w_out", (("w_out", w_out, p_out, c_out, m_w_out, v_w_out, same),), [d_pre_mix])
    *early, loss_sum = adamw_small(all_small, None, small_w, small_m, small_v, after=[done_out])
    done_in = adamw("adamw_w_in", (("w_in", w_in, p_in, c_in, m_w_in, v_w_in, turned),), [loss_sum])
    *late, _ = adamw_small(None, (late_parts, d_pre_mix, dev.reshape(1)), small_w, small_m, small_v, after=[done_in])
    sg, sd, snm, snv = ({**a, **b} for a, b in zip(early, late))
    loss = loss_sum[0, 0] * np.float32(0.5 / D_MODEL)

    names = ["pre_mix_norm", "w_in", "sgu_ln_gain", "sgu_ln_bias", "sgu_w_spatial", "sgu_b_spatial", "attn_out_norm",
             "sgu_out_norm", "w_out", "post_mix_norm", "pre_ffn_norm", "w_gate", "w_up", "w_down", "post_ffn_norm"]
    outs = [loss, grad_x[None]]
    for i, table in enumerate((sg, sd, snm, snv)):
        for name in names:
            outs.append(big[name][i] if name in big else table[name])
    return tuple(outs)
```
